```python
import math
import jax, jax.numpy as jnp
from jax import lax
import numpy as np

D_MODEL = 1024
BATCH = 8
SEQ = 4096
DEPTH = 1

PLE_DIM = 256
D_RG = D_MODEL // 2
RG_BLOCKS = 8
RG_BLOCK = D_RG // RG_BLOCKS
CONV_WIDTH = 4
RG_C = 8.0
D_HG = D_MODEL // 2
HG_HEAD_DIM = 128
HG_HEADS = D_HG // HG_HEAD_DIM
HG_CHUNK = 64
D_MIX = D_RG + D_HG
D_IN = 2 * D_RG + 4 * D_HG
EPS = 1e-6

kernel_name = "hymba_style_rglru_hgrn2_block"


def rms_norm(x, w):
    xf = x.astype(jnp.float32)
    y = xf * lax.rsqrt(jnp.mean(xf * xf, axis=-1, keepdims=True) + EPS)
    return (y * w.astype(jnp.float32)).astype(x.dtype)


def causal_depthwise_conv(x, w, b):
    T = x.shape[1]
    xp = jnp.pad(x, ((0, 0), (CONV_WIDTH - 1, 0), (0, 0)))
    y = b
    for j in range(CONV_WIDTH):
        y = y + xp[:, j:j + T] * w[j]
    return y


def rg_lru(x, wa, ba, wx, bx, lam):
    B, T, _ = x.shape
    xf = x.astype(jnp.float32)
    xb = xf.reshape(B, T, RG_BLOCKS, RG_BLOCK)
    r = jax.nn.sigmoid(jnp.einsum('btgi,gij->btgj', xb, wa.astype(jnp.float32)).reshape(B, T, D_RG) + ba)
    i = jax.nn.sigmoid(jnp.einsum('btgi,gij->btgj', xb, wx.astype(jnp.float32)).reshape(B, T, D_RG) + bx)
    log_a = -RG_C * r * jax.nn.softplus(-lam.astype(jnp.float32))
    a = jnp.exp(log_a)
    mult = jnp.sqrt(-jnp.expm1(2.0 * log_a))
    mult = jnp.where(jnp.arange(T)[None, :, None] == 0, 1.0, mult)
    u = mult * (i * xf)

    def combine(left, right):
        a_l, b_l = left
        a_r, b_r = right
        return a_l * a_r, a_r * b_l + b_r

    _, h = lax.associative_scan(combine, (a, u), axis=1)
    return h.astype(x.dtype)


def gla_chunked(q, k, logf, v):
    B, T, H, K = q.shape
    V = v.shape[-1]
    C = HG_CHUNK
    N = T // C

    def chunks(t):
        return t.reshape(B, N, C, H, t.shape[-1]).transpose(0, 3, 1, 2, 4)

    q, k, logf, v = chunks(q), chunks(k), chunks(logf), chunks(v)
    b = jnp.cumsum(logf, axis=3)
    b_last = b[:, :, :, -1:, :]
    qe = q * jnp.exp(b)
    ke = k * jnp.exp(-b)
    scores = jnp.einsum('bhnck,bhnsk->bhncs', qe, ke)
    causal = jnp.tril(jnp.ones((C, C), dtype=bool))
    scores = jnp.where(causal, scores, 0.0)
    o_intra = jnp.einsum('bhncs,bhnsv->bhncv', scores, v)

    kd = k * jnp.exp(b_last - b)
    dS = jnp.einsum('bhnsk,bhnsv->bhnkv', kd, v)
    decay = jnp.exp(b_last[:, :, :, 0, :])

    def step(S, inp):
        d, ds = inp
        return d[..., None] * S + ds, S

    S0 = jnp.zeros((B, H, K, V), jnp.float32)
    _, S_prev = lax.scan(step, S0, (jnp.moveaxis(decay, 2, 0), jnp.moveaxis(dS, 2, 0)))
    o_inter = jnp.einsum('bhnck,nbhkv->bhncv', qe, S_prev)
    o = o_intra + o_inter
    return o.transpose(0, 2, 3, 1, 4).reshape(B, T, H, V)


def hgrn2_branch(q, fz, iv, g, lb, norm_w):
    B, T, _ = q.shape
    qf, fzf, ivf, gf = (t.astype(jnp.float32) for t in (q, fz, iv, g))
    lb = lb.astype(jnp.float32)
    f = lb + (1.0 - lb) * jax.nn.sigmoid(fzf)
    logf = jnp.log(f)
    k = (1.0 - lb) * jax.nn.sigmoid(-fzf)
    qs = jax.nn.silu(qf) * (HG_HEAD_DIM ** -0.5)
    heads = lambda t: t.reshape(B, T, HG_HEADS, HG_HEAD_DIM)
    o = gla_chunked(heads(qs), heads(k), heads(logf), heads(ivf))
    o = o * lax.rsqrt(jnp.mean(o * o, axis=-1, keepdims=True) + EPS) * norm_w.astype(jnp.float32)
    o = o.reshape(B, T, D_HG) * jax.nn.silu(gf)
    return o.astype(q.dtype)


def _fwd_setup_inputs(seed: int = 0) -> dict:
    key = jax.random.key(seed)
    ks = jax.random.split(key, 20)
    f32 = jnp.float32
    nrm = lambda k, shape, scale: scale * jax.random.normal(k, shape, f32)
    u = jax.random.uniform(ks[10], (DEPTH, D_RG), f32, minval=0.9, maxval=0.999)
    s = u ** (1.0 / RG_C)
    rg_lambda = jnp.log(s) - jnp.log1p(-s)
    return {
        "x": jax.random.normal(ks[0], (BATCH, SEQ, D_MODEL), f32),
        "p": jax.random.normal(ks[1], (DEPTH, BATCH, SEQ, PLE_DIM), f32),
        "norm_mix_w": 1.0 + nrm(ks[2], (DEPTH, D_MODEL), 0.1),
        "w_in": nrm(ks[3], (DEPTH, D_MODEL, D_IN), D_MODEL ** -0.5),
        "conv_w": nrm(ks[4], (DEPTH, CONV_WIDTH, D_RG), CONV_WIDTH ** -0.5),
        "conv_b": nrm(ks[5], (DEPTH, D_RG), 0.01),
        "rg_wa": nrm(ks[6], (DEPTH, RG_BLOCKS, RG_BLOCK, RG_BLOCK), RG_BLOCK ** -0.5),
        "rg_ba": nrm(ks[7], (DEPTH, D_RG), 0.1),
        "rg_wx": nrm(ks[8], (DEPTH, RG_BLOCKS, RG_BLOCK, RG_BLOCK), RG_BLOCK ** -0.5),
        "rg_bx": nrm(ks[9], (DEPTH, D_RG), 0.1),
        "rg_lambda": rg_lambda,
        "hg_lb": nrm(ks[11], (DEPTH + 1, D_HG), 0.1),
        "hg_norm_w": 1.0 + nrm(ks[12], (DEPTH, HG_HEAD_DIM), 0.1),
        "w_out": nrm(ks[13], (DEPTH, D_MIX, D_MODEL), D_MIX ** -0.5),
        "ple_norm_w": 1.0 + nrm(ks[14], (DEPTH, D_MODEL), 0.1),
        "w_ple_gate": nrm(ks[15], (DEPTH, D_MODEL, D_MODEL), D_MODEL ** -0.5),
        "b_ple_gate": nrm(ks[16], (DEPTH, D_MODEL), 0.1),
        "w_ple_proj": nrm(ks[17], (DEPTH, PLE_DIM, D_MODEL), PLE_DIM ** -0.5),
        "final_norm_w": 1.0 + nrm(ks[18], (D_MODEL,), 0.1),
    }


def _fwd_reference(x, p, norm_mix_w, w_in, conv_w, conv_b, rg_wa, rg_ba, rg_wx, rg_bx,
              rg_lambda, hg_lb, hg_norm_w, w_out, ple_norm_w, w_ple_gate, b_ple_gate,
              w_ple_proj, final_norm_w):
    lb_all = jnp.cumsum(jax.nn.softmax(hg_lb.astype(jnp.float32), axis=0), axis=0)
    split_at = [D_RG, 2 * D_RG, 2 * D_RG + D_HG, 2 * D_RG + 2 * D_HG, 2 * D_RG + 3 * D_HG]
    h = x
    for l in range(DEPTH):
        u = rms_norm(h, norm_mix_w[l])
        proj = u @ w_in[l]
        xa, ga, qb, fb, ib, gb = jnp.split(proj, split_at, axis=-1)
        xa = causal_depthwise_conv(xa, conv_w[l], conv_b[l])
        ya = rg_lru(xa, rg_wa[l], rg_ba[l], rg_wx[l], rg_bx[l], rg_lambda[l]) * jax.nn.silu(ga)
        yb = hgrn2_branch(qb, fb, ib, gb, lb_all[l], hg_norm_w[l])
        h = h + jnp.concatenate([ya, yb], axis=-1) @ w_out[l]
        gate = jax.nn.sigmoid(rms_norm(h, ple_norm_w[l]) @ w_ple_gate[l] + b_ple_gate[l])
        h = h + gate * (p[l] @ w_ple_proj[l])
    return rms_norm(h, final_norm_w)


import jax as _jax
import jax.numpy as _jnp

TWIN_FORMAT = 'train_step'
FWD_PARAMS = ['x', 'p', 'norm_mix_w', 'w_in', 'conv_w', 'conv_b', 'rg_wa', 'rg_ba', 'rg_wx', 'rg_bx', 'rg_lambda', 'hg_lb', 'hg_norm_w', 'w_out', 'ple_norm_w', 'w_ple_gate', 'b_ple_gate', 'w_ple_proj', 'final_norm_w']
TWIN_WEIGHTS = ['norm_mix_w', 'w_in', 'conv_w', 'conv_b', 'rg_wa', 'rg_ba', 'rg_wx', 'rg_bx', 'rg_lambda', 'hg_lb', 'hg_norm_w', 'w_out', 'ple_norm_w', 'w_ple_gate', 'b_ple_gate', 'w_ple_proj', 'final_norm_w']
TWIN_DIFF_INPUT = 'x'
TWIN_INPUTS = ['x', 'p', 'norm_mix_w', 'w_in', 'conv_w', 'conv_b', 'rg_wa', 'rg_ba', 'rg_wx', 'rg_bx', 'rg_lambda', 'hg_lb', 'hg_norm_w', 'w_out', 'ple_norm_w', 'w_ple_gate', 'b_ple_gate', 'w_ple_proj', 'final_norm_w', 'loss_target', 'm_norm_mix_w', 'm_w_in', 'm_conv_w', 'm_conv_b', 'm_rg_wa', 'm_rg_ba', 'm_rg_wx', 'm_rg_bx', 'm_rg_lambda', 'm_hg_lb', 'm_hg_norm_w', 'm_w_out', 'm_ple_norm_w', 'm_w_ple_gate', 'm_b_ple_gate', 'm_w_ple_proj', 'm_final_norm_w', 'v_norm_mix_w', 'v_w_in', 'v_conv_w', 'v_conv_b', 'v_rg_wa', 'v_rg_ba', 'v_rg_wx', 'v_rg_bx', 'v_rg_lambda', 'v_hg_lb', 'v_hg_norm_w', 'v_w_out', 'v_ple_norm_w', 'v_w_ple_gate', 'v_b_ple_gate', 'v_w_ple_proj', 'v_final_norm_w']
TWIN_OUTPUTS = ['loss', 'grad_x', 'grad_norm_mix_w', 'grad_w_in', 'grad_conv_w', 'grad_conv_b', 'grad_rg_wa', 'grad_rg_ba', 'grad_rg_wx', 'grad_rg_bx', 'grad_rg_lambda', 'grad_hg_lb', 'grad_hg_norm_w', 'grad_w_out', 'grad_ple_norm_w', 'grad_w_ple_gate', 'grad_b_ple_gate', 'grad_w_ple_proj', 'grad_final_norm_w', 'delta_norm_mix_w', 'delta_w_in', 'delta_conv_w', 'delta_conv_b', 'delta_rg_wa', 'delta_rg_ba', 'delta_rg_wx', 'delta_rg_bx', 'delta_rg_lambda', 'delta_hg_lb', 'delta_hg_norm_w', 'delta_w_out', 'delta_ple_norm_w', 'delta_w_ple_gate', 'delta_b_ple_gate', 'delta_w_ple_proj', 'delta_final_norm_w', 'new_m_norm_mix_w', 'new_m_w_in', 'new_m_conv_w', 'new_m_conv_b', 'new_m_rg_wa', 'new_m_rg_ba', 'new_m_rg_wx', 'new_m_rg_bx', 'new_m_rg_lambda', 'new_m_hg_lb', 'new_m_hg_norm_w', 'new_m_w_out', 'new_m_ple_norm_w', 'new_m_w_ple_gate', 'new_m_b_ple_gate', 'new_m_w_ple_proj', 'new_m_final_norm_w', 'new_v_norm_mix_w', 'new_v_w_in', 'new_v_conv_w', 'new_v_conv_b', 'new_v_rg_wa', 'new_v_rg_ba', 'new_v_rg_wx', 'new_v_rg_bx', 'new_v_rg_lambda', 'new_v_hg_lb', 'new_v_hg_norm_w', 'new_v_w_out', 'new_v_ple_norm_w', 'new_v_w_ple_gate', 'new_v_b_ple_gate', 'new_v_w_ple_proj', 'new_v_final_norm_w']
TWIN_LEAF_KINDS = {'loss': 'loss', 'grad_x': 'grad_x', 'grad_norm_mix_w': 'grad_w', 'grad_w_in': 'grad_w', 'grad_conv_w': 'grad_w', 'grad_conv_b': 'grad_w', 'grad_rg_wa': 'grad_w', 'grad_rg_ba': 'grad_w', 'grad_rg_wx': 'grad_w', 'grad_rg_bx': 'grad_w', 'grad_rg_lambda': 'grad_w', 'grad_hg_lb': 'grad_w', 'grad_hg_norm_w': 'grad_w', 'grad_w_out': 'grad_w', 'grad_ple_norm_w': 'grad_w', 'grad_w_ple_gate': 'grad_w', 'grad_b_ple_gate': 'grad_w', 'grad_w_ple_proj': 'grad_w', 'grad_final_norm_w': 'grad_w', 'delta_norm_mix_w': 'delta_w', 'delta_w_in': 'delta_w', 'delta_conv_w': 'delta_w', 'delta_conv_b': 'delta_w', 'delta_rg_wa': 'delta_w', 'delta_rg_ba': 'delta_w', 'delta_rg_wx': 'delta_w', 'delta_rg_bx': 'delta_w', 'delta_rg_lambda': 'delta_w', 'delta_hg_lb': 'delta_w', 'delta_hg_norm_w': 'delta_w', 'delta_w_out': 'delta_w', 'delta_ple_norm_w': 'delta_w', 'delta_w_ple_gate': 'delta_w', 'delta_b_ple_gate': 'delta_w', 'delta_w_ple_proj': 'delta_w', 'delta_final_norm_w': 'delta_w', 'new_m_norm_mix_w': 'new_m', 'new_m_w_in': 'new_m', 'new_m_conv_w': 'new_m', 'new_m_conv_b': 'new_m', 'new_m_rg_wa': 'new_m', 'new_m_rg_ba': 'new_m', 'new_m_rg_wx': 'new_m', 'new_m_rg_bx': 'new_m', 'new_m_rg_lambda': 'new_m', 'new_m_hg_lb': 'new_m', 'new_m_hg_norm_w': 'new_m', 'new_m_w_out': 'new_m', 'new_m_ple_norm_w': 'new_m', 'new_m_w_ple_gate': 'new_m', 'new_m_b_ple_gate': 'new_m', 'new_m_w_ple_proj': 'new_m', 'new_m_final_norm_w': 'new_m', 'new_v_norm_mix_w': 'new_v', 'new_v_w_in': 'new_v', 'new_v_conv_w': 'new_v', 'new_v_conv_b': 'new_v', 'new_v_rg_wa': 'new_v', 'new_v_rg_ba': 'new_v', 'new_v_rg_wx': 'new_v', 'new_v_rg_bx': 'new_v', 'new_v_rg_lambda': 'new_v', 'new_v_hg_lb': 'new_v', 'new_v_hg_norm_w': 'new_v', 'new_v_w_out': 'new_v', 'new_v_ple_norm_w': 'new_v', 'new_v_w_ple_gate': 'new_v', 'new_v_b_ple_gate': 'new_v', 'new_v_w_ple_proj': 'new_v', 'new_v_final_norm_w': 'new_v'}


def _forward(args):
    return _fwd_reference(*[args[k] for k in FWD_PARAMS])


def _output_shape():
    out = _jax.eval_shape(lambda: _forward(_fwd_setup_inputs(0)))
    return out.shape, out.dtype

N_MICROBATCH = 1
ADAM_LR = 0.001
ADAM_B1 = 0.9
ADAM_B2 = 0.999
ADAM_EPS = 1e-08
ADAM_WD = 0.01
ADAM_STEP = 10
PER_EXAMPLE_BATCH_AXIS = {'x': 0, 'p': 1, 'loss_target': 0}
SHARED_INPUTS = []
_WEIGHT_DTYPES = {'norm_mix_w': _jnp.float32, 'w_in': _jnp.float32, 'conv_w': _jnp.float32, 'conv_b': _jnp.float32, 'rg_wa': _jnp.float32, 'rg_ba': _jnp.float32, 'rg_wx': _jnp.float32, 'rg_bx': _jnp.float32, 'rg_lambda': _jnp.float32, 'hg_lb': _jnp.float32, 'hg_norm_w': _jnp.float32, 'w_out': _jnp.float32, 'ple_norm_w': _jnp.float32, 'w_ple_gate': _jnp.float32, 'b_ple_gate': _jnp.float32, 'w_ple_proj': _jnp.float32, 'final_norm_w': _jnp.float32}
MOMENT_SCALE = {'norm_mix_w': 1.082405e-01, 'w_in': 6.584253e-02, 'conv_w': 7.728485e-02, 'conv_b': 8.391934e-01, 'rg_wa': 3.836730e-02, 'rg_ba': 2.899741e-02, 'rg_wx': 7.053524e-02, 'rg_bx': 2.953495e-02, 'rg_lambda': 3.817583e-02, 'hg_lb': 7.923993e-03, 'hg_norm_w': 1.860223e-01, 'w_out': 8.837292e-02, 'ple_norm_w': 3.396648e-02, 'w_ple_gate': 3.313437e-02, 'b_ple_gate': 3.809055e-01, 'w_ple_proj': 8.657368e-02, 'final_norm_w': 3.224693e+01}


def _to_microbatches(a, axis):
    t = _jnp.moveaxis(a, axis, 0)
    t = t.reshape((N_MICROBATCH, t.shape[0] // N_MICROBATCH) + t.shape[1:])
    return _jnp.moveaxis(t, 1, axis + 1)


def setup_inputs(seed: int = 0) -> dict:
    inp = _fwd_setup_inputs(seed)
    key = _jax.random.fold_in(_jax.random.key(seed), 7919)
    shape, _ = _output_shape()
    out = dict(inp)
    out["loss_target"] = _jax.random.normal(_jax.random.fold_in(key, 0), shape, _jnp.float32)
    for i, name in enumerate(TWIN_WEIGHTS):
        w = inp[name].astype(_jnp.float32)
        if MOMENT_SCALE is None:
            s = _jnp.sqrt(_jnp.mean(_jnp.square(w)) + 1e-30)
        else:
            s = MOMENT_SCALE[name]
        km, kv = _jax.random.split(_jax.random.fold_in(key, i + 1))
        out[name] = w
        out["m_" + name] = s * _jax.random.normal(km, w.shape, _jnp.float32)
        out["v_" + name] = (s * s) * _jax.random.uniform(kv, w.shape, _jnp.float32, 0.5, 1.5)
    if N_MICROBATCH > 1:
        for name, axis in PER_EXAMPLE_BATCH_AXIS.items():
            out[name] = _to_microbatches(out[name], axis)
    return {'x': out['x'], 'p': out['p'], 'norm_mix_w': out['norm_mix_w'], 'w_in': out['w_in'], 'conv_w': out['conv_w'], 'conv_b': out['conv_b'], 'rg_wa': out['rg_wa'], 'rg_ba': out['rg_ba'], 'rg_wx': out['rg_wx'], 'rg_bx': out['rg_bx'], 'rg_lambda': out['rg_lambda'], 'hg_lb': out['hg_lb'], 'hg_norm_w': out['hg_norm_w'], 'w_out': out['w_out'], 'ple_norm_w': out['ple_norm_w'], 'w_ple_gate': out['w_ple_gate'], 'b_ple_gate': out['b_ple_gate'], 'w_ple_proj': out['w_ple_proj'], 'final_norm_w': out['final_norm_w'], 'loss_target': out['loss_target'], 'm_norm_mix_w': out['m_norm_mix_w'], 'm_w_in': out['m_w_in'], 'm_conv_w': out['m_conv_w'], 'm_conv_b': out['m_conv_b'], 'm_rg_wa': out['m_rg_wa'], 'm_rg_ba': out['m_rg_ba'], 'm_rg_wx': out['m_rg_wx'], 'm_rg_bx': out['m_rg_bx'], 'm_rg_lambda': out['m_rg_lambda'], 'm_hg_lb': out['m_hg_lb'], 'm_hg_norm_w': out['m_hg_norm_w'], 'm_w_out': out['m_w_out'], 'm_ple_norm_w': out['m_ple_norm_w'], 'm_w_ple_gate': out['m_w_ple_gate'], 'm_b_ple_gate': out['m_b_ple_gate'], 'm_w_ple_proj': out['m_w_ple_proj'], 'm_final_norm_w': out['m_final_norm_w'], 'v_norm_mix_w': out['v_norm_mix_w'], 'v_w_in': out['v_w_in'], 'v_conv_w': out['v_conv_w'], 'v_conv_b': out['v_conv_b'], 'v_rg_wa': out['v_rg_wa'], 'v_rg_ba': out['v_rg_ba'], 'v_rg_wx': out['v_rg_wx'], 'v_rg_bx': out['v_rg_bx'], 'v_rg_lambda': out['v_rg_lambda'], 'v_hg_lb': out['v_hg_lb'], 'v_hg_norm_w': out['v_hg_norm_w'], 'v_w_out': out['v_w_out'], 'v_ple_norm_w': out['v_ple_norm_w'], 'v_w_ple_gate': out['v_w_ple_gate'], 'v_b_ple_gate': out['v_b_ple_gate'], 'v_w_ple_proj': out['v_w_ple_proj'], 'v_final_norm_w': out['v_final_norm_w']}


def _loss(weights, diff, rest, loss_target):
    with _jax.named_scope("forward"):
        args = {**rest, TWIN_DIFF_INPUT: diff, **{k: w.astype(_WEIGHT_DTYPES[k]) for k, w in weights.items()}}
        y = _forward(args)
    with _jax.named_scope("loss_head"):
        err = _jnp.square(y.astype(_jnp.float32) - loss_target)
        return 0.5 * _jnp.sum(_jnp.mean(err, axis=-1)) if err.ndim else 0.5 * err


def _adamw(w, g, m, v):
    m = ADAM_B1 * m + (1.0 - ADAM_B1) * g
    v = ADAM_B2 * v + (1.0 - ADAM_B2) * _jnp.square(g)
    m_hat = m / (1.0 - ADAM_B1 ** ADAM_STEP)
    v_hat = v / (1.0 - ADAM_B2 ** ADAM_STEP)
    delta = -ADAM_LR * (m_hat / (_jnp.sqrt(v_hat) + ADAM_EPS) + ADAM_WD * w)
    return delta, m, v


def reference(x, p, norm_mix_w, w_in, conv_w, conv_b, rg_wa, rg_ba, rg_wx, rg_bx, rg_lambda, hg_lb, hg_norm_w, w_out, ple_norm_w, w_ple_gate, b_ple_gate, w_ple_proj, final_norm_w, loss_target, m_norm_mix_w, m_w_in, m_conv_w, m_conv_b, m_rg_wa, m_rg_ba, m_rg_wx, m_rg_bx, m_rg_lambda, m_hg_lb, m_hg_norm_w, m_w_out, m_ple_norm_w, m_w_ple_gate, m_b_ple_gate, m_w_ple_proj, m_final_norm_w, v_norm_mix_w, v_w_in, v_conv_w, v_conv_b, v_rg_wa, v_rg_ba, v_rg_wx, v_rg_bx, v_rg_lambda, v_hg_lb, v_hg_norm_w, v_w_out, v_ple_norm_w, v_w_ple_gate, v_b_ple_gate, v_w_ple_proj, v_final_norm_w):
    given = dict(x=x, p=p, norm_mix_w=norm_mix_w, w_in=w_in, conv_w=conv_w, conv_b=conv_b, rg_wa=rg_wa, rg_ba=rg_ba, rg_wx=rg_wx, rg_bx=rg_bx, rg_lambda=rg_lambda, hg_lb=hg_lb, hg_norm_w=hg_norm_w, w_out=w_out, ple_norm_w=ple_norm_w, w_ple_gate=w_ple_gate, b_ple_gate=b_ple_gate, w_ple_proj=w_ple_proj, final_norm_w=final_norm_w, loss_target=loss_target, m_norm_mix_w=m_norm_mix_w, m_w_in=m_w_in, m_conv_w=m_conv_w, m_conv_b=m_conv_b, m_rg_wa=m_rg_wa, m_rg_ba=m_rg_ba, m_rg_wx=m_rg_wx, m_rg_bx=m_rg_bx, m_rg_lambda=m_rg_lambda, m_hg_lb=m_hg_lb, m_hg_norm_w=m_hg_norm_w, m_w_out=m_w_out, m_ple_norm_w=m_ple_norm_w, m_w_ple_gate=m_w_ple_gate, m_b_ple_gate=m_b_ple_gate, m_w_ple_proj=m_w_ple_proj, m_final_norm_w=m_final_norm_w, v_norm_mix_w=v_norm_mix_w, v_w_in=v_w_in, v_conv_w=v_conv_w, v_conv_b=v_conv_b, v_rg_wa=v_rg_wa, v_rg_ba=v_rg_ba, v_rg_wx=v_rg_wx, v_rg_bx=v_rg_bx, v_rg_lambda=v_rg_lambda, v_hg_lb=v_hg_lb, v_hg_norm_w=v_hg_norm_w, v_w_out=v_w_out, v_ple_norm_w=v_ple_norm_w, v_w_ple_gate=v_w_ple_gate, v_b_ple_gate=v_b_ple_gate, v_w_ple_proj=v_w_ple_proj, v_final_norm_w=v_final_norm_w)
    weights = {n: given[n] for n in TWIN_WEIGHTS}
    shared = {n: given[n] for n in SHARED_INPUTS}
    per_example = {n: given[n] for n in ['x', 'p']}
    grad_fn = _jax.value_and_grad(_loss, argnums=(0, 1))

    def one_microbatch(ex, loss_target):
        ex = dict(ex)
        diff = ex.pop(TWIN_DIFF_INPUT)
        return grad_fn(weights, diff, {**shared, **ex}, loss_target)

    if N_MICROBATCH == 1:
        loss, (grad_w, grad_x) = one_microbatch(per_example, given["loss_target"])
    else:
        def body(carry, xs):
            loss_sum, grad_sum = carry
            l_k, (gw_k, gx_k) = one_microbatch(xs[0], xs[1])
            with _jax.named_scope("update"):
                return (loss_sum + l_k, _jax.tree.map(_jnp.add, grad_sum, gw_k)), gx_k

        init = (_jnp.zeros((), _jnp.float32), _jax.tree.map(_jnp.zeros_like, weights))
        (loss, grad_w), grad_x = _jax.lax.scan(body, init, (per_example, given["loss_target"]))
    with _jax.named_scope("update"):
        delta_w, new_m, new_v = {}, {}, {}
        for n in TWIN_WEIGHTS:
            delta_w[n], new_m[n], new_v[n] = _adamw(weights[n], grad_w[n], given["m_" + n], given["v_" + n])
    return (loss, grad_x, *[grad_w[n] for n in TWIN_WEIGHTS], *[delta_w[n] for n in TWIN_WEIGHTS],
            *[new_m[n] for n in TWIN_WEIGHTS], *[new_v[n] for n in TWIN_WEIGHTS])
```

```python
import functools

import jax
import jax.numpy as jnp
from jax import lax
from jax.experimental import pallas as pl
from jax.experimental.pallas import tpu as pltpu

F32 = jnp.float32
BF16 = jnp.bfloat16
MESH = pl.DeviceIdType.MESH
HIGHEST = lax.Precision.HIGHEST

T = 4096
D = 1024
DG = 512
DIN = 3072
PLE = 256
NH = 4
HD = 128
CH = 64
NCHUNK = T // CH
EPS = 1e-6
RG_C = 8.0
NSHARD = 4
SHW = DIN // NSHARD

ADAM_LR = 0.001
ADAM_B1 = 0.9
ADAM_B2 = 0.999
ADAM_EPS = 1e-08
ADAM_WD = 0.01
ADAM_STEP = 10

VMEM_SPEC = pl.BlockSpec(memory_space=pltpu.VMEM)
MIB = 1024 * 1024

SMALL_ROWS = 640
SMALL_PER_DEV = SMALL_ROWS // 8


def _mm(a, b):
    return jnp.dot(a.astype(BF16), b.astype(BF16), preferred_element_type=F32)


def _mm_nt(a, b):
    return lax.dot_general(a.astype(BF16), b.astype(BF16), (((1,), (1,)), ((), ())),
                           preferred_element_type=F32)


def _mm_tn(a, b):
    return lax.dot_general(a.astype(BF16), b.astype(BF16), (((0,), (0,)), ((), ())),
                           preferred_element_type=F32)


def _sig_pair(x):
    e = jnp.exp(-jnp.abs(x))
    big = 1.0 / (1.0 + e)
    small = e * big
    pos = x >= 0
    return jnp.where(pos, big, small), jnp.where(pos, small, big)


def _rowsum(v):
    return jnp.sum(v, axis=0, keepdims=True)


def _shift_down(cur, prev8, d, rows):
    n = cur.shape[0]
    return jnp.where(rows < d, jnp.tile(pltpu.roll(prev8, d, 0), (n // 8, 1)), pltpu.roll(cur, d, 0))


def _shift_up(cur, next8, d, rows):
    n = cur.shape[0]
    return jnp.where(rows >= n - d, jnp.tile(pltpu.roll(next8, 8 - d, 0), (n // 8, 1)),
                     pltpu.roll(cur, n - d, 0))


def _cparams(sem, vmem_mib):
    return pltpu.CompilerParams(dimension_semantics=sem, vmem_limit_bytes=vmem_mib * MIB)


def _mesh_pos():
    x, y, c = lax.axis_index("x"), lax.axis_index("y"), lax.axis_index("c")
    chips = [(1 - x, y), (x, 1 - y), (1 - x, 1 - y)]
    return x, y, c, chips


def _remote(src, dst, ssem, rsem, dev):
    return pltpu.make_async_remote_copy(src_ref=src, dst_ref=dst, send_sem=ssem, recv_sem=rsem,
                                        device_id=dev, device_id_type=MESH)


def _gather_weights(w_in, w_out, w_pg, w_pp, conv_w):
    shapes = [w_in.shape, w_out.shape, w_pg.shape, w_pp.shape]

    def body(win, wout, wpg, wpp, cw, o_in, o_out, o_pg, o_pp, o_cw, ssem, rsem):
        x, y, c, chips = _mesh_pos()
        kme = 2 * x + y
        sibling = (x, y, 1 - c)
        params = [(win, o_in), (wout, o_out), (wpg, o_pg), (wpp, o_pp)]
        for src, dst in params:
            rows = src.shape[0]
            for r0 in range(0, rows, 128):
                dst[kme, r0:r0 + 128, :] = src[r0:r0 + 128, :].astype(BF16)
        o_cw[kme] = cw[...]

        def halves(dst):
            hrows = dst.shape[1] // 2
            mine = pl.ds(pl.multiple_of(c * hrows, 128), hrows)
            other = pl.ds(pl.multiple_of((1 - c) * hrows, 128), hrows)
            return mine, other

        sends = []
        n = 0
        for _, dst in params:
            mine, _o = halves(dst)
            for px, py in chips:
                cp = _remote(dst.at[kme, mine], dst.at[kme, mine], ssem.at[n], rsem.at[n], (px, py, c))
                cp.start()
                sends.append(cp)
                n += 1
        for px, py in chips:
            cp = _remote(o_cw.at[kme], o_cw.at[kme], ssem.at[n], rsem.at[n], (px, py, c))
            cp.start()
            sends.append(cp)
            n += 1
        n = 0
        m = 15
        for _, dst in params:
            mine, _o = halves(dst)
            for px, py in chips:
                kj = 2 * px + py
                _remote(dst.at[kj, mine], dst.at[kj, mine], ssem.at[n], rsem.at[n], (px, py, c)).wait_recv()
                cp = _remote(dst.at[kj, mine], dst.at[kj, mine], ssem.at[m], rsem.at[m], sibling)
                cp.start()
                sends.append(cp)
                n += 1
                m += 1
        for px, py in chips:
            kj = 2 * px + py
            _remote(o_cw.at[kj], o_cw.at[kj], ssem.at[n], rsem.at[n], (px, py, c)).wait_recv()
            n += 1
        m = 15
        for _, dst in params:
            _mn, other = halves(dst)
            for px, py in chips:
                kj = 2 * px + py
                _remote(dst.at[kj, other], dst.at[kj, other], ssem.at[m], rsem.at[m], sibling).wait_recv()
                m += 1
        for cp in sends:
            cp.wait_send()

    out_shape = [jax.ShapeDtypeStruct((NSHARD,) + s, BF16) for s in shapes]
    out_shape.append(jax.ShapeDtypeStruct((NSHARD,) + conv_w.shape, F32))
    return pl.pallas_call(
        body, name="gather_weights", out_shape=out_shape,
        in_specs=[VMEM_SPEC] * 5, out_specs=[VMEM_SPEC] * 5,
        scratch_shapes=[pltpu.SemaphoreType.DMA((27,)), pltpu.SemaphoreType.DMA((27,))],
        compiler_params=pltpu.CompilerParams(vmem_limit_bytes=40 * MIB),
    )(w_in, w_out, w_pg, w_pp, conv_w)


def _adam_rows(w, g, m, v):
    m2 = ADAM_B1 * m + (1.0 - ADAM_B1) * g
    v2 = ADAM_B2 * v + (1.0 - ADAM_B2) * (g * g)
    m_hat = m2 / (1.0 - ADAM_B1 ** ADAM_STEP)
    v_hat = v2 / (1.0 - ADAM_B2 ** ADAM_STEP)
    delta = -ADAM_LR * (m_hat / (jnp.sqrt(v_hat) + ADAM_EPS) + ADAM_WD * w)
    return delta, m2, v2


def _adam_refs(w_ref, g_ref, m_ref, v_ref, d_out, m_out, v_out, step):
    rows = w_ref.shape[0]
    for r0 in range(0, rows, step):
        sl = slice(r0, r0 + step)
        d, m2, v2 = _adam_rows(w_ref[sl, :], g_ref[sl, :], m_ref[sl, :], v_ref[sl, :])
        d_out[sl, :] = d
        m_out[sl, :] = m2
        v_out[sl, :] = v2


def _reduce_scatter_adam(name, g_part, w, m, v, vmem_mib):
    R, C = w.shape
    H = R // 2
    step = 128 if H >= 128 else H

    def body(g_ref, w_ref, m_ref, v_ref, g_out, d_out, m_out, v_out,
             gb, sib_recv, chip_b, ici_recv, ssem, rsem):
        x, y, c, chips = _mesh_pos()
        kme = 2 * x + y
        sibling = (x, y, 1 - c)
        mine0 = pl.multiple_of(c * H, step)
        other0 = pl.multiple_of((1 - c) * H, step)
        other = pl.ds(other0, H)

        first = []
        for k in range(NSHARD):
            for r0 in range(0, H, step):
                gb[k, r0:r0 + step, :] = g_ref[k, pl.ds(other0 + r0, step), :].astype(BF16)
            cp = _remote(gb.at[k], sib_recv.at[k], ssem.at[k], rsem.at[k], sibling)
            cp.start()
            first.append(cp)
        for cp in first:
            cp.wait_recv()

        second = []
        for j, (px, py) in enumerate(chips):
            kj = 2 * px + py
            for r0 in range(0, H, step):
                s = g_ref[kj, pl.ds(mine0 + r0, step), :] + sib_recv[kj, r0:r0 + step, :].astype(F32)
                chip_b[j, r0:r0 + step, :] = s.astype(BF16)
            cp = _remote(chip_b.at[j], ici_recv.at[j], ssem.at[4 + j], rsem.at[4 + j], (px, py, c))
            cp.start()
            second.append(cp)
        for cp in second:
            cp.wait_recv()
        for r0 in range(0, H, step):
            s = g_ref[kme, pl.ds(mine0 + r0, step), :] + sib_recv[kme, r0:r0 + step, :].astype(F32)
            for j in range(3):
                s = s + ici_recv[j, r0:r0 + step, :].astype(F32)
            g_out[pl.ds(mine0 + r0, step), :] = s

        mine = pl.ds(mine0, H)
        last = _remote(g_out.at[mine], g_out.at[mine], ssem.at[7], rsem.at[7], sibling)
        last.start()
        _remote(g_out.at[other], g_out.at[other], ssem.at[7], rsem.at[7], sibling).wait_recv()
        for cp in first + second + [last]:
            cp.wait_send()
        _adam_refs(w_ref, g_out, m_ref, v_ref, d_out, m_out, v_out, step)

    out_shape = [jax.ShapeDtypeStruct((R, C), F32)] * 4
    return pl.pallas_call(
        body, name=name, out_shape=out_shape,
        in_specs=[VMEM_SPEC] * 4, out_specs=[VMEM_SPEC] * 4,
        scratch_shapes=[
            pltpu.VMEM((NSHARD, H, C), BF16),
            pltpu.VMEM((NSHARD, H, C), BF16),
            pltpu.VMEM((3, H, C), BF16),
            pltpu.VMEM((3, H, C), BF16),
            pltpu.SemaphoreType.DMA((8,)),
            pltpu.SemaphoreType.DMA((8,)),
        ],
        compiler_params=pltpu.CompilerParams(vmem_limit_bytes=vmem_mib * MIB),
    )(g_part, w, m, v)


def _small_allreduce_adam(g_part, w, m, v):
    P = SMALL_PER_DEV

    def body(g_ref, w_ref, m_ref, v_ref, g_out, d_out, m_out, v_out, recv, ssem, rsem):
        x, y, c = lax.axis_index("x"), lax.axis_index("y"), lax.axis_index("c")
        me = 4 * x + 2 * y + c

        def peer(mask):
            px = x ^ ((mask >> 2) & 1)
            py = y ^ ((mask >> 1) & 1)
            pc = c ^ (mask & 1)
            return (px, py, pc), 4 * px + 2 * py + pc

        def rows_of(r):
            return pl.ds(pl.multiple_of(r * P, 8), P)

        first = []
        for mask in range(1, 8):
            dev, r = peer(mask)
            cp = _remote(g_ref.at[rows_of(r)], recv.at[mask - 1], ssem.at[mask - 1], rsem.at[mask - 1], dev)
            cp.start()
            first.append(cp)
        s = g_ref[rows_of(me), :]
        for mask in range(1, 8):
            first[mask - 1].wait_recv()
            s = s + recv[mask - 1]
        g_out[rows_of(me), :] = s
        second = []
        for mask in range(1, 8):
            dev, r = peer(mask)
            cp = _remote(g_out.at[rows_of(me)], g_out.at[rows_of(me)], ssem.at[6 + mask], rsem.at[6 + mask], dev)
            cp.start()
            second.append(cp)
        for mask in range(1, 8):
            dev, r = peer(mask)
            _remote(g_out.at[rows_of(r)], g_out.at[rows_of(r)], ssem.at[6 + mask], rsem.at[6 + mask], dev).wait_recv()
        for cp in first + second:
            cp.wait_send()
        _adam_refs(w_ref, g_out, m_ref, v_ref, d_out, m_out, v_out, 128)

    out_shape = [jax.ShapeDtypeStruct((SMALL_ROWS, 128), F32)] * 4
    return pl.pallas_call(
        body, name="small_allreduce_adam", out_shape=out_shape,
        in_specs=[VMEM_SPEC] * 4, out_specs=[VMEM_SPEC] * 4,
        scratch_shapes=[pltpu.VMEM((7, P, 128), F32),
                        pltpu.SemaphoreType.DMA((14,)), pltpu.SemaphoreType.DMA((14,))],
    )(g_part, w, m, v)


def _fwd_inproj(x, nw, w_in_b):
    tm = 512

    def body(x_ref, nw_ref, w_ref, proj_ref, u_ref):
        xv = x_ref[...]
        s = lax.rsqrt(jnp.mean(xv * xv, axis=-1, keepdims=True) + EPS)
        u = (xv * s * nw_ref[...]).astype(BF16)
        u_ref[...] = u
        for k in range(NSHARD):
            proj_ref[:, k * SHW:(k + 1) * SHW] = jnp.dot(u, w_ref[k], preferred_element_type=F32)

    return pl.pallas_call(
        body, name="fwd_inproj", grid=(T // tm,),
        in_specs=[pl.BlockSpec((tm, D), lambda i: (i, 0)),
                  pl.BlockSpec((1, D), lambda i: (0, 0)),
                  pl.BlockSpec((NSHARD, D, SHW), lambda i: (0, 0, 0))],
        out_specs=[pl.BlockSpec((tm, DIN), lambda i: (i, 0)),
                   pl.BlockSpec((tm, D), lambda i: (i, 0))],
        out_shape=[jax.ShapeDtypeStruct((T, DIN), F32), jax.ShapeDtypeStruct((T, D), BF16)],
        compiler_params=_cparams(("parallel",), 48),
    )(x, nw, w_in_b)


def _rg_conv(xa, prev8, cw_ref, cb, rows):
    taps = [_shift_down(xa, prev8, 3, rows), _shift_down(xa, prev8, 2, rows),
            _shift_down(xa, prev8, 1, rows), xa]
    xc = cb
    for j in range(4):
        xc = xc + taps[j] * cw_ref[j:j + 1, :]
    return xc, taps


def _rg_gates(xc, wa, ba, wx, bx, sp, first_row):
    r, _ = _sig_pair(_mm(xc, wa) + ba)
    i, _ = _sig_pair(_mm(xc, wx) + bx)
    log_a = (-RG_C) * r * sp
    a = jnp.exp(log_a)
    a2 = a * a
    one_m_a2 = -jnp.tanh(log_a) * (a2 + 1.0)
    mult = jnp.where(first_row, 1.0, jnp.sqrt(one_m_a2))
    return r, i, a, a2, mult


def _softplus(z):
    return jnp.maximum(z, 0.0) + jnp.log1p(jnp.exp(-jnp.abs(z)))


def _fwd_rglru(proj, conv_w, conv_b, wa_d, ba, wx_d, bx, lam):
    tm = 512
    ng = tm // 8

    def body(xa_ref, ga_ref, cw_ref, cb_ref, wa_ref, ba_ref, wx_ref, bx_ref, lam_ref,
             h_ref, ya_ref, a_s, u_s, tail_s, hc_s):
        i = pl.program_id(0)

        @pl.when(i == 0)
        def _():
            tail_s[...] = jnp.zeros_like(tail_s)
            hc_s[...] = jnp.zeros_like(hc_s)

        rows = lax.broadcasted_iota(jnp.int32, (tm, DG), 0)
        xa = xa_ref[...]
        xc, _ = _rg_conv(xa, tail_s[...], cw_ref, cb_ref[...], rows)
        tail_s[...] = xa[tm - 8:tm, :]
        sp = _softplus(-lam_ref[...])
        first_row = (rows + i * tm) == 0
        r, ig, a, a2, mult = _rg_gates(xc, wa_ref[...], ba_ref[...], wx_ref[...], bx_ref[...], sp, first_row)
        a_s[...] = a
        u_s[...] = mult * (ig * xc)

        row8 = lax.broadcasted_iota(jnp.int32, (8, DG), 0)

        def group(g, hprev):
            sl = pl.ds(pl.multiple_of(g * 8, 8), 8)
            av = a_s[sl, :]
            uv = u_s[sl, :]
            for d in (1, 2, 4):
                keep = row8 >= d
                uv = uv + av * jnp.where(keep, pltpu.roll(uv, d, 0), 0.0)
                av = av * jnp.where(keep, pltpu.roll(av, d, 0), 1.0)
            hv = uv + av * hprev
            h_ref[sl, :] = hv
            return hv[7:8, :]

        hlast = lax.fori_loop(0, ng, group, hc_s[0:1, :])
        hc_s[0:1, :] = hlast
        ga = ga_ref[...]
        sg, _ = _sig_pair(ga)
        ya_ref[...] = (h_ref[...] * (ga * sg)).astype(BF16)

    vec = lambda: pl.BlockSpec((1, DG), lambda i: (0, 0))
    return pl.pallas_call(
        body, name="fwd_rglru", grid=(T // tm,),
        in_specs=[pl.BlockSpec((tm, DG), lambda i: (i, 0)),
                  pl.BlockSpec((tm, DG), lambda i: (i, 1)),
                  pl.BlockSpec((4, DG), lambda i: (0, 0)), vec(),
                  pl.BlockSpec((DG, DG), lambda i: (0, 0)), vec(),
                  pl.BlockSpec((DG, DG), lambda i: (0, 0)), vec(), vec()],
        out_specs=[pl.BlockSpec((tm, DG), lambda i: (i, 0)),
                   pl.BlockSpec((tm, DG), lambda i: (i, 0))],
        out_shape=[jax.ShapeDtypeStruct((T, DG), F32), jax.ShapeDtypeStruct((T, DG), BF16)],
        scratch_shapes=[pltpu.VMEM((tm, DG), F32), pltpu.VMEM((tm, DG), F32),
                        pltpu.VMEM((8, DG), F32), pltpu.VMEM((8, DG), F32)],
        compiler_params=_cparams(("arbitrary",), 48),
    )(proj, proj, conv_w, conv_b, wa_d, ba, wx_d, bx, lam)


def _hg_lower_bound(lb_ref):
    return _sig_pair(lb_ref[0:1, :] - lb_ref[1:2, :])


def _hg_gates(fz, lb, one_m_lb):
    sg, sn = _sig_pair(fz)
    f = lb + one_m_lb * sg
    return sg, sn, f, jnp.log(f), one_m_lb * sn


def _tri(lower):
    r = lax.broadcasted_iota(jnp.int32, (CH, CH), 0)
    c = lax.broadcasted_iota(jnp.int32, (CH, CH), 1)
    return (r >= c) if lower else (r <= c)


def _chunk_terms(qs_c, k_c, logf_c, tril_f):
    b = jnp.dot(tril_f, logf_c, precision=HIGHEST, preferred_element_type=F32)
    bl = _rowsum(logf_c)
    eb = jnp.exp(b)
    enb = jnp.exp(-b)
    ekd = jnp.exp(bl - b)
    return bl, eb, enb, ekd, qs_c * eb, k_c * enb, k_c * ekd


def _fwd_hgrn2(proj, hg_lb, hg_nw):
    tm = 512
    nc = tm // CH

    def body(q_ref, f_ref, v_ref, g_ref, lb_ref, nw_ref, yb_ref, o_ref, sp_ref, st_s):
        i = pl.program_id(0)

        @pl.when(i == 0)
        def _():
            st_s[...] = jnp.zeros_like(st_s)

        lb_all, one_m_lb_all = _hg_lower_bound(lb_ref)
        mask = _tri(True)
        tril_f = mask.astype(F32)
        nw = nw_ref[...]
        for hd in range(NH):
            cols = slice(hd * HD, (hd + 1) * HD)
            lb, one_m_lb = lb_all[:, cols], one_m_lb_all[:, cols]
            _sg, _sn, _f, logf, k = _hg_gates(f_ref[:, cols], lb, one_m_lb)
            q = q_ref[:, cols]
            sq, _ = _sig_pair(q)
            qs = q * sq * (HD ** -0.5)
            v = v_ref[:, cols]
            for ci in range(nc):
                rs = slice(ci * CH, (ci + 1) * CH)
                bl, eb, enb, ekd, qe, ke, kd = _chunk_terms(qs[rs], k[rs], logf[rs], tril_f)
                p = jnp.where(mask, _mm_nt(qe, ke), 0.0)
                st = st_s[hd]
                sp_ref[hd, ci] = st
                o_ref[rs, cols] = _mm(p, v[rs]) + _mm_nt(qe, st)
                st_s[hd] = st * jnp.exp(bl) + _mm_tn(v[rs], kd)
            o = o_ref[:, cols]
            so = lax.rsqrt(jnp.mean(o * o, axis=-1, keepdims=True) + EPS)
            g = g_ref[:, cols]
            sg, _ = _sig_pair(g)
            yb_ref[:, cols] = (o * so * nw * (g * sg)).astype(BF16)

    col = lambda j: pl.BlockSpec((tm, DG), lambda i: (i, j))
    return pl.pallas_call(
        body, name="fwd_hgrn2", grid=(T // tm,),
        in_specs=[col(2), col(3), col(4), col(5),
                  pl.BlockSpec((2, DG), lambda i: (0, 0)),
                  pl.BlockSpec((1, HD), lambda i: (0, 0))],
        out_specs=[pl.BlockSpec((tm, DG), lambda i: (i, 0)),
                   pl.BlockSpec((tm, DG), lambda i: (i, 0)),
                   pl.BlockSpec((NH, nc, HD, HD), lambda i: (0, i, 0, 0))],
        out_shape=[jax.ShapeDtypeStruct((T, DG), BF16), jax.ShapeDtypeStruct((T, DG), F32),
                   jax.ShapeDtypeStruct((NH, NCHUNK, HD, HD), F32)],
        scratch_shapes=[pltpu.VMEM((NH, HD, HD), F32)],
        compiler_params=_cparams(("arbitrary",), 48),
    )(proj, proj, proj, proj, hg_lb, hg_nw)


def _tail_fwd_bwd(x, p, tgt, ya, yb, w_out_b, w_pg_b, w_pp_b, ple_nw, b_pg, fnw):
    tm = 256
    nt = T // tm

    def body(x_ref, p_ref, t_ref, ya_ref, yb_ref, wo_ref, wg_ref, wp_ref, pw_ref, b_ref, fw_ref,
             dh1_ref, dyab_ref, dwo_ref, dwg_ref, dwp_ref, sm_ref):
        i = pl.program_id(0)

        @pl.when(i == 0)
        def _():
            dwo_ref[...] = jnp.zeros_like(dwo_ref)
            dwg_ref[...] = jnp.zeros_like(dwg_ref)
            dwp_ref[...] = jnp.zeros_like(dwp_ref)
            sm_ref[...] = jnp.zeros_like(sm_ref)

        ya = ya_ref[...]
        yb = yb_ref[...]
        pv = p_ref[...].astype(BF16)
        pw = pw_ref[...]
        fw = fw_ref[...]
        h1 = x_ref[...] + _mm(ya, wo_ref[0:DG, :]) + _mm(yb, wo_ref[DG:D, :])
        s2 = lax.rsqrt(jnp.mean(h1 * h1, axis=-1, keepdims=True) + EPS)
        n2h = h1 * s2
        n2 = (n2h * pw).astype(BF16)
        z = _mm(n2, wg_ref[...]) + b_ref[...]
        gate, gate_n = _sig_pair(z)
        pp = jnp.concatenate([_mm(pv, wp_ref[k]) for k in range(NSHARD)], axis=1)
        h2 = h1 + gate * pp
        s3 = lax.rsqrt(jnp.mean(h2 * h2, axis=-1, keepdims=True) + EPS)
        hn = h2 * s3
        err = hn * fw - t_ref[...]
        sm_ref[0:1, :] += _rowsum(err * err)
        dy = err * (1.0 / D)
        sm_ref[1:2, :] += _rowsum(dy * hn)
        g3 = dy * fw
        dh2 = s3 * (g3 - hn * jnp.mean(g3 * hn, axis=-1, keepdims=True))
        dpp = (dh2 * gate).astype(BF16)
        dz = dh2 * pp * gate * gate_n
        sm_ref[2:3, :] += _rowsum(dz)
        dzb = dz.astype(BF16)
        dwg_ref[...] += _mm_tn(n2, dzb)
        dn2 = _mm_nt(dzb, wg_ref[...])
        for k in range(NSHARD):
            dwp_ref[k] += _mm_tn(pv, dpp[:, k * PLE:(k + 1) * PLE])
        sm_ref[3:4, :] += _rowsum(dn2 * n2h)
        g2 = dn2 * pw
        dh1 = dh2 + s2 * (g2 - n2h * jnp.mean(g2 * n2h, axis=-1, keepdims=True))
        dh1_ref[...] = dh1
        dh1b = dh1.astype(BF16)
        dyab_ref[...] = _mm_nt(dh1b, wo_ref[...])
        dwo_ref[0:DG, :] += _mm_tn(ya, dh1b)
        dwo_ref[DG:D, :] += _mm_tn(yb, dh1b)

        @pl.when(i == nt - 1)
        def _():
            total = jnp.sum(sm_ref[0:1, :], axis=-1, keepdims=True) * (0.5 / D)
            sm_ref[4:5, :] = jnp.broadcast_to(total, (1, D))

    row = lambda w: pl.BlockSpec((tm, w), lambda i: (i, 0))
    const2 = lambda s: pl.BlockSpec(s, lambda i: (0, 0))
    const3 = lambda s: pl.BlockSpec(s, lambda i: (0, 0, 0))
    return pl.pallas_call(
        body, name="tail_fwd_bwd", grid=(nt,),
        in_specs=[row(D), row(PLE), row(D), row(DG), row(DG),
                  const2((D, D)), const2((D, D)), const3((NSHARD, PLE, PLE)),
                  const2((1, D)), const2((1, D)), const2((1, D))],
        out_specs=[row(D), row(D), const2((D, D)), const2((D, D)),
                   const3((NSHARD, PLE, PLE)), const2((8, D))],
        out_shape=[jax.ShapeDtypeStruct((T, D), F32), jax.ShapeDtypeStruct((T, D), F32),
                   jax.ShapeDtypeStruct((D, D), F32), jax.ShapeDtypeStruct((D, D), F32),
                   jax.ShapeDtypeStruct((NSHARD, PLE, PLE), F32), jax.ShapeDtypeStruct((8, D), F32)],
        compiler_params=_cparams(("arbitrary",), 56),
    )(x, p, tgt, ya, yb, w_out_b, w_pg_b, w_pp_b, ple_nw, b_pg, fnw)


def _bwd_rglru(proj, h, dyab, conv_w, conv_b, wa_d, ba, wx_d, bx, lam):
    tm = 512
    nt = T // tm
    ng = tm // 8

    def body(xa_ref, ga_ref, xp_ref, h_ref, hp_ref, dya_ref, cw_ref, cb_ref, wa_ref, ba_ref, wx_ref, bx_ref,
             lam_ref, da_ref, dwa_ref, dwx_ref, sm_ref, a_s, g_s, cg_s, nxt_s):
        i = pl.program_id(0)
        tile = nt - 1 - i

        @pl.when(i == 0)
        def _():
            dwa_ref[...] = jnp.zeros_like(dwa_ref)
            dwx_ref[...] = jnp.zeros_like(dwx_ref)
            sm_ref[...] = jnp.zeros_like(sm_ref)
            cg_s[...] = jnp.zeros_like(cg_s)
            nxt_s[...] = jnp.zeros_like(nxt_s)

        rows = lax.broadcasted_iota(jnp.int32, (tm, DG), 0)
        has_prev = tile > 0
        xa = xa_ref[...]
        xprev = jnp.where(has_prev, xp_ref[...], 0.0)
        xc, taps = _rg_conv(xa, xprev, cw_ref, cb_ref[...], rows)
        lam_v = lam_ref[...]
        sp = _softplus(-lam_v)
        first_row = (rows + tile * tm) == 0
        r, ig, a, a2, mult = _rg_gates(xc, wa_ref[...], ba_ref[...], wx_ref[...], bx_ref[...], sp, first_row)
        hv = h_ref[...]
        hprev = jnp.where(has_prev, hp_ref[...], 0.0)
        h_m1 = _shift_down(hv, hprev, 1, rows)
        ga = ga_ref[...]
        sg, sgn = _sig_pair(ga)
        dya = dya_ref[...]
        dga = dya * hv * (sg * (1.0 + ga * sgn))

        a_s[...] = jnp.where(rows == tm - 1, 1.0, pltpu.roll(a, tm - 1, 0))
        g_s[...] = dya * (ga * sg)
        row8 = lax.broadcasted_iota(jnp.int32, (8, DG), 0)

        def group(k, carry):
            g = ng - 1 - k
            sl = pl.ds(pl.multiple_of(g * 8, 8), 8)
            av = a_s[sl, :]
            gv = g_s[sl, :]
            for d in (1, 2, 4):
                keep = row8 < 8 - d
                gv = gv + av * jnp.where(keep, pltpu.roll(gv, 8 - d, 0), 0.0)
                av = av * jnp.where(keep, pltpu.roll(av, 8 - d, 0), 1.0)
            gv = gv + av * carry
            g_s[sl, :] = gv
            return gv[0:1, :]

        gfirst = lax.fori_loop(0, ng, group, cg_s[0:1, :])
        cg_s[0:1, :] = a[0:1, :] * gfirst

        gt = g_s[...]
        da = gt * h_m1
        ixc = ig * xc
        di = gt * mult * xc
        dxc = gt * mult * ig
        dlog_a = da * a + jnp.where(first_row, 0.0, gt * ixc * (-a2 / mult))
        sm_ref[3:4, :] += _rowsum(dlog_a * ((-RG_C) * r))
        dpr_f = dlog_a * ((-RG_C) * sp) * r * (1.0 - r)
        dpi_f = di * ig * (1.0 - ig)
        sm_ref[1:2, :] += _rowsum(dpr_f)
        sm_ref[2:3, :] += _rowsum(dpi_f)
        dpr = dpr_f.astype(BF16)
        dpi = dpi_f.astype(BF16)
        xcb = xc.astype(BF16)
        dwa_ref[...] += _mm_tn(xcb, dpr)
        dwx_ref[...] += _mm_tn(xcb, dpi)
        dxc = dxc + _mm_nt(dpr, wa_ref[...]) + _mm_nt(dpi, wx_ref[...])
        sm_ref[0:1, :] += _rowsum(dxc)
        for j in range(4):
            sm_ref[4 + j:5 + j, :] += _rowsum(dxc * taps[j])
        nxt = nxt_s[...]
        dxa = (dxc * cw_ref[3:4, :] + _shift_up(dxc, nxt, 1, rows) * cw_ref[2:3, :]
               + _shift_up(dxc, nxt, 2, rows) * cw_ref[1:2, :] + _shift_up(dxc, nxt, 3, rows) * cw_ref[0:1, :])
        nxt_s[...] = dxc[0:8, :]
        da_ref[:, 0:DG] = dxa.astype(BF16)
        da_ref[:, DG:D] = dga.astype(BF16)

        @pl.when(i == nt - 1)
        def _():
            _, s_neg = _sig_pair(lam_v)
            sm_ref[3:4, :] = sm_ref[3:4, :] * (-s_neg)

    vec = lambda: pl.BlockSpec((1, DG), lambda i: (0, 0))
    mat = lambda: pl.BlockSpec((DG, DG), lambda i: (0, 0))
    prev8 = lambda: pl.BlockSpec((8, DG), lambda i: (jnp.maximum((nt - 1 - i) * (tm // 8) - 1, 0), 0))
    return pl.pallas_call(
        body, name="bwd_rglru", grid=(nt,),
        in_specs=[pl.BlockSpec((tm, DG), lambda i: (nt - 1 - i, 0)),
                  pl.BlockSpec((tm, DG), lambda i: (nt - 1 - i, 1)),
                  prev8(),
                  pl.BlockSpec((tm, DG), lambda i: (nt - 1 - i, 0)),
                  prev8(),
                  pl.BlockSpec((tm, DG), lambda i: (nt - 1 - i, 0)),
                  pl.BlockSpec((4, DG), lambda i: (0, 0)), vec(), mat(), vec(), mat(), vec(), vec()],
        out_specs=[pl.BlockSpec((tm, D), lambda i: (nt - 1 - i, 0)), mat(), mat(),
                   pl.BlockSpec((8, DG), lambda i: (0, 0))],
        out_shape=[jax.ShapeDtypeStruct((T, D), BF16), jax.ShapeDtypeStruct((DG, DG), F32),
                   jax.ShapeDtypeStruct((DG, DG), F32), jax.ShapeDtypeStruct((8, DG), F32)],
        scratch_shapes=[pltpu.VMEM((tm, DG), F32), pltpu.VMEM((tm, DG), F32),
                        pltpu.VMEM((8, DG), F32), pltpu.VMEM((8, DG), F32)],
        compiler_params=_cparams(("arbitrary",), 56),
    )(proj, proj, proj, h, h, dyab, conv_w, conv_b, wa_d, ba, wx_d, bx, lam)


def _bwd_hgrn2(proj, o, s_prev, dyab, hg_lb, hg_nw):
    tm = 512
    nt = T // tm
    nc = tm // CH

    def body(q_ref, f_ref, v_ref, g_ref, o_ref, sp_ref, dy_ref, lb_ref, nw_ref,
             db_ref, sm_ref, dst_s, dq_s, dk_s, dlf_s):
        i = pl.program_id(0)

        @pl.when(i == 0)
        def _():
            sm_ref[...] = jnp.zeros_like(sm_ref)
            dst_s[...] = jnp.zeros_like(dst_s)

        lb_all, one_m_lb_all = _hg_lower_bound(lb_ref)
        mask = _tri(True)
        tril_f = mask.astype(F32)
        triu_f = _tri(False).astype(F32)
        nw = nw_ref[...]
        for hd in range(NH):
            cols = slice(hd * HD, (hd + 1) * HD)
            lb, one_m_lb = lb_all[:, cols], one_m_lb_all[:, cols]
            sg, sn, f, logf, k = _hg_gates(f_ref[:, cols], lb, one_m_lb)
            q = q_ref[:, cols]
            sq, sqn = _sig_pair(q)
            qs = q * sq * (HD ** -0.5)
            v = v_ref[:, cols]
            g = g_ref[:, cols]
            sgg, sggn = _sig_pair(g)
            silu_g = g * sgg
            o = o_ref[:, cols]
            so = lax.rsqrt(jnp.mean(o * o, axis=-1, keepdims=True) + EPS)
            oh = o * so
            dyb = dy_ref[:, cols]
            dgb = dyb * (oh * nw) * (sgg * (1.0 + g * sggn))
            don = dyb * silu_g
            sm_ref[1:2, 0:HD] += _rowsum(don * oh)
            gw = don * nw
            do = so * (gw - oh * jnp.mean(gw * oh, axis=-1, keepdims=True))
            db_ref[:, 3 * DG + hd * HD:3 * DG + (hd + 1) * HD] = dgb.astype(BF16)

            for cc in range(nc):
                ci = nc - 1 - cc
                rs = slice(ci * CH, (ci + 1) * CH)
                bl, eb, enb, ekd, qe, ke, kd = _chunk_terms(qs[rs], k[rs], logf[rs], tril_f)
                p = jnp.where(mask, _mm_nt(qe, ke), 0.0)
                st_prev = sp_ref[hd, ci]
                dst = dst_s[hd]
                do_c = do[rs].astype(BF16)
                v_c = v[rs].astype(BF16)
                qeb = qe.astype(BF16)
                dp = jnp.where(mask, _mm_nt(do_c, v_c), 0.0).astype(BF16)
                dv = _mm_tn(p, do_c) + _mm_nt(kd, dst)
                dqe = _mm(dp, ke) + _mm(do_c, st_prev)
                dke = _mm_tn(dp, qeb)
                dkd = _mm(v_c, dst)
                decay = jnp.exp(bl)
                ddecay = _rowsum(dst * st_prev)
                dst_s[hd] = dst * decay + _mm_tn(do_c, qeb)
                t_kd = dkd * kd
                db = dqe * qe - dke * ke - t_kd
                dbl = _rowsum(t_kd) + ddecay * decay
                dlf_s[rs, :] = jnp.dot(triu_f, db, precision=HIGHEST, preferred_element_type=F32) + dbl
                dq_s[rs, :] = dqe * eb
                dk_s[rs, :] = dke * enb + dkd * ekd
                db_ref[rs, 2 * DG + hd * HD:2 * DG + (hd + 1) * HD] = dv.astype(BF16)

            t = dlf_s[...] / f - dk_s[...]
            db_ref[:, DG + hd * HD:DG + (hd + 1) * HD] = (one_m_lb * sg * sn * t).astype(BF16)
            sm_ref[0:1, cols] += _rowsum(sn * t)
            db_ref[:, hd * HD:(hd + 1) * HD] = (dq_s[...] * (sq * (1.0 + q * sqn)) * (HD ** -0.5)).astype(BF16)

        @pl.when(i == nt - 1)
        def _():
            dsm = sm_ref[0:1, :] * (lb_all * one_m_lb_all)
            sm_ref[2:3, :] = dsm
            sm_ref[3:4, :] = -dsm

    col = lambda j: pl.BlockSpec((tm, DG), lambda i: (nt - 1 - i, j))
    return pl.pallas_call(
        body, name="bwd_hgrn2", grid=(nt,),
        in_specs=[col(2), col(3), col(4), col(5),
                  pl.BlockSpec((tm, DG), lambda i: (nt - 1 - i, 0)),
                  pl.BlockSpec((NH, nc, HD, HD), lambda i: (0, nt - 1 - i, 0, 0)),
                  pl.BlockSpec((tm, DG), lambda i: (nt - 1 - i, 1)),
                  pl.BlockSpec((2, DG), lambda i: (0, 0)),
                  pl.BlockSpec((1, HD), lambda i: (0, 0))],
        out_specs=[pl.BlockSpec((tm, 4 * DG), lambda i: (nt - 1 - i, 0)),
                   pl.BlockSpec((8, DG), lambda i: (0, 0))],
        out_shape=[jax.ShapeDtypeStruct((T, 4 * DG), BF16), jax.ShapeDtypeStruct((8, DG), F32)],
        scratch_shapes=[pltpu.VMEM((NH, HD, HD), F32), pltpu.VMEM((tm, HD), F32),
                        pltpu.VMEM((tm, HD), F32), pltpu.VMEM((tm, HD), F32)],
        compiler_params=_cparams(("arbitrary",), 56),
    )(proj, proj, proj, proj, o, s_prev, dyab, hg_lb, hg_nw)


def _dproj_pieces(da_ref, db_ref):
    return [
        [(da_ref[:, 0:SHW], 0)],
        [(da_ref[:, SHW:D], 0), (db_ref[:, 0:DG], D - SHW)],
        [(db_ref[:, DG:DG + SHW], 0)],
        [(db_ref[:, DG + SHW:4 * DG], 0)],
    ]


def _bwd_inproj_dx(x, dh1, d_a, d_b, w_in_b, nw):
    tm = 512
    nt = T // tm

    def body(x_ref, dh1_ref, da_ref, db_ref, w_ref, nw_ref, dx_ref, sm_ref):
        i = pl.program_id(0)

        @pl.when(i == 0)
        def _():
            sm_ref[...] = jnp.zeros_like(sm_ref)

        du = None
        for k, pieces in enumerate(_dproj_pieces(da_ref, db_ref)):
            for val, off in pieces:
                t = _mm_nt(val, w_ref[k, :, off:off + val.shape[1]])
                du = t if du is None else du + t
        xv = x_ref[...]
        s = lax.rsqrt(jnp.mean(xv * xv, axis=-1, keepdims=True) + EPS)
        xh = xv * s
        sm_ref[0:1, :] += _rowsum(du * xh)
        g = du * nw_ref[...]
        dx_ref[...] = dh1_ref[...] + s * (g - xh * jnp.mean(g * xh, axis=-1, keepdims=True))

    row = lambda w: pl.BlockSpec((tm, w), lambda i: (i, 0))
    return pl.pallas_call(
        body, name="bwd_inproj_dx", grid=(nt,),
        in_specs=[row(D), row(D), row(D), row(4 * DG),
                  pl.BlockSpec((NSHARD, D, SHW), lambda i: (0, 0, 0)),
                  pl.BlockSpec((1, D), lambda i: (0, 0))],
        out_specs=[row(D), pl.BlockSpec((8, D), lambda i: (0, 0))],
        out_shape=[jax.ShapeDtypeStruct((T, D), F32), jax.ShapeDtypeStruct((8, D), F32)],
        compiler_params=_cparams(("arbitrary",), 56),
    )(x, dh1, d_a, d_b, w_in_b, nw)


def _bwd_inproj_dw(u_b, d_a, d_b):
    tm = 512
    nt = T // tm

    def body(u_ref, da_ref, db_ref, dw_ref):
        i = pl.program_id(0)

        @pl.when(i == 0)
        def _():
            dw_ref[...] = jnp.zeros_like(dw_ref)

        u = u_ref[...]
        for k, pieces in enumerate(_dproj_pieces(da_ref, db_ref)):
            for val, off in pieces:
                dw_ref[k, :, off:off + val.shape[1]] += _mm_tn(u, val)

    row = lambda w: pl.BlockSpec((tm, w), lambda i: (i, 0))
    return pl.pallas_call(
        body, name="bwd_inproj_dw", grid=(nt,),
        in_specs=[row(D), row(D), row(4 * DG)],
        out_specs=pl.BlockSpec((NSHARD, D, SHW), lambda i: (0, 0, 0)),
        out_shape=jax.ShapeDtypeStruct((NSHARD, D, SHW), F32),
        compiler_params=_cparams(("arbitrary",), 56),
    )(u_b, d_a, d_b)


_SMALL = ["norm_mix_w", "conv_w", "conv_b", "rg_wa", "rg_ba", "rg_wx", "rg_bx", "rg_lambda", "hg_lb",
          "hg_norm_w", "ple_norm_w", "b_ple_gate", "final_norm_w"]


def _pack(parts):
    flat = jnp.concatenate([a.reshape(-1) for a in parts])
    flat = jnp.pad(flat, (0, SMALL_ROWS * 128 - flat.shape[0]))
    return flat.reshape(SMALL_ROWS, 128)


def _unpack(packed, shapes):
    flat = packed.reshape(-1)
    out, pos = [], 0
    for s in shapes:
        n = 1
        for d in s:
            n *= d
        out.append(flat[pos:pos + n].reshape(s))
        pos += n
    return out


def _block_diag(w):
    eye = jnp.eye(8, dtype=w.dtype)
    return (w[:, :, None, :] * eye[:, None, :, None]).reshape(DG, DG)


def _diag_blocks(dense):
    eye = jnp.eye(8, dtype=dense.dtype)
    return jnp.sum(dense.reshape(8, 64, 8, 64) * eye[:, None, :, None], axis=2)


def kernel(x, p, norm_mix_w, w_in, conv_w, conv_b, rg_wa, rg_ba, rg_wx, rg_bx, rg_lambda, hg_lb, hg_norm_w, w_out, ple_norm_w, w_ple_gate, b_ple_gate, w_ple_proj, final_norm_w, loss_target, m_norm_mix_w, m_w_in, m_conv_w, m_conv_b, m_rg_wa, m_rg_ba, m_rg_wx, m_rg_bx, m_rg_lambda, m_hg_lb, m_hg_norm_w, m_w_out, m_ple_norm_w, m_w_ple_gate, m_b_ple_gate, m_w_ple_proj, m_final_norm_w, v_norm_mix_w, v_w_in, v_conv_w, v_conv_b, v_rg_wa, v_rg_ba, v_rg_wx, v_rg_bx, v_rg_lambda, v_hg_lb, v_hg_norm_w, v_w_out, v_ple_norm_w, v_w_ple_gate, v_b_ple_gate, v_w_ple_proj, v_final_norm_w):
    given = dict(locals())
    kx, ky = lax.axis_index("x"), lax.axis_index("y")
    kme = 2 * kx + ky
    x2, p2, tgt = x[0], p[0, 0], loss_target[0]
    fnw = final_norm_w.reshape(1, D)

    w_in_b, w_out_b, w_pg_b, w_pp_b, cw_all = _gather_weights(
        w_in[0], w_out[0], w_ple_gate[0], w_ple_proj[0], conv_w[0])
    conv_w_full = cw_all.transpose(1, 0, 2).reshape(4, DG)
    wa_d = _block_diag(rg_wa[0]).astype(BF16)
    wx_d = _block_diag(rg_wx[0]).astype(BF16)

    proj, u_b = _fwd_inproj(x2, norm_mix_w, w_in_b)
    h, ya = _fwd_rglru(proj, conv_w_full, conv_b, wa_d, rg_ba, wx_d, rg_bx, rg_lambda)
    yb, o, s_prev = _fwd_hgrn2(proj, hg_lb, hg_norm_w)
    dh1, dyab, dw_out, dw_pg, dw_pp, sm_tail = _tail_fwd_bwd(
        x2, p2, tgt, ya, yb, w_out_b.reshape(D, D), w_pg_b.reshape(D, D), w_pp_b,
        ple_norm_w, b_ple_gate, fnw)
    d_a, dwa_d, dwx_d, sm_a = _bwd_rglru(proj, h, dyab, conv_w_full, conv_b, wa_d, rg_ba, wx_d, rg_bx, rg_lambda)
    d_b, sm_b = _bwd_hgrn2(proj, o, s_prev, dyab, hg_lb, hg_norm_w)
    grad_x, sm_in = _bwd_inproj_dx(x2, dh1, d_a, d_b, w_in_b, norm_mix_w)
    dw_in = _bwd_inproj_dw(u_b, d_a, d_b)

    loss = lax.psum(sm_tail[4, 0], ("x", "y", "c"))

    big = {}
    big["w_in"] = _reduce_scatter_adam("rs_adam_w_in", dw_in, w_in[0], m_w_in[0], v_w_in[0], 60)
    big["w_out"] = _reduce_scatter_adam("rs_adam_w_out", dw_out.reshape(NSHARD, D // NSHARD, D),
                                        w_out[0], m_w_out[0], v_w_out[0], 40)
    big["w_ple_gate"] = _reduce_scatter_adam("rs_adam_w_pg", dw_pg.reshape(NSHARD, D // NSHARD, D),
                                             w_ple_gate[0], m_w_ple_gate[0], v_w_ple_gate[0], 40)
    big["w_ple_proj"] = _reduce_scatter_adam("rs_adam_w_pp", dw_pp, w_ple_proj[0], m_w_ple_proj[0],
                                             v_w_ple_proj[0], 24)

    small_grads = {
        "norm_mix_w": sm_in[0:1, :],
        "conv_w": sm_a[4:8, :],
        "conv_b": sm_a[0:1, :],
        "rg_wa": _diag_blocks(dwa_d),
        "rg_ba": sm_a[1:2, :],
        "rg_wx": _diag_blocks(dwx_d),
        "rg_bx": sm_a[2:3, :],
        "rg_lambda": sm_a[3:4, :],
        "hg_lb": sm_b[2:4, :],
        "hg_norm_w": sm_b[1:2, 0:HD],
        "ple_norm_w": sm_tail[3:4, :],
        "b_ple_gate": sm_tail[2:3, :],
        "final_norm_w": sm_tail[1:2, :],
    }

    def full_conv(a):
        return lax.dynamic_update_slice(jnp.zeros((4, DG), F32), a, (0, kme * 128))

    def small_in(prefix):
        parts = []
        for n in _SMALL:
            if n == "conv_w":
                parts.append(full_conv(given[prefix + n][0]))
            else:
                parts.append(given[prefix + n])
        return _pack(parts)

    g_s, d_s, m_s, v_s = _small_allreduce_adam(
        _pack([small_grads[n] for n in _SMALL]), small_in(""), small_in("m_"), small_in("v_"))
    shapes = [(4, DG) if n == "conv_w" else given[n].shape for n in _SMALL]
    small = {}
    for packed, kind in ((g_s, "grad"), (d_s, "delta"), (m_s, "new_m"), (v_s, "new_v")):
        for n, a in zip(_SMALL, _unpack(packed, shapes)):
            if n == "conv_w":
                a = lax.dynamic_slice(a, (0, kme * 128), (4, 128))[None]
            small[(kind, n)] = a

    order = ["norm_mix_w", "w_in", "conv_w", "conv_b", "rg_wa", "rg_ba", "rg_wx", "rg_bx", "rg_lambda", "hg_lb",
             "hg_norm_w", "w_out", "ple_norm_w", "w_ple_gate", "b_ple_gate", "w_ple_proj", "final_norm_w"]
    outs = [loss, grad_x[None]]
    for ki, kind in enumerate(("grad", "delta", "new_m", "new_v")):
        for n in order:
            if n in big:
                outs.append(big[n][ki][None])
            else:
                outs.append(small[(kind, n)])
    return tuple(outs)
```

```python
import jax
import jax.numpy as jnp
from jax import lax
from jax.experimental import pallas as pl
from jax.experimental.pallas import tpu as pltpu

F32 = jnp.float32
BF16 = jnp.bfloat16
I32 = jnp.int32
MESH = pl.DeviceIdType.MESH
HIGHEST = lax.Precision.HIGHEST

T = 4096
D = 1024
DG = 512
DIN = 3072
PLE = 256
NH = 4
HD = 128
CH = 64
NCHUNK = T // CH
RGB = 64
EPS = 1e-6
RG_C = 8.0
NSHARD = 4
SHW = DIN // NSHARD
NDEV = 8

ADAM_LR = 0.001
ADAM_B1 = 0.9
ADAM_B2 = 0.999
ADAM_EPS = 1e-08
ADAM_WD = 0.01
ADAM_STEP = 10

VMEM_SPEC = pl.BlockSpec(memory_space=pltpu.VMEM)
MIB = 1024 * 1024

VROWS = 16
ROW_NORM_MIX, ROW_FINAL_NORM, ROW_B_PG, ROW_PLE_NORM = 0, 1, 2, 3
ROW_CB_BA, ROW_BX_LAM, ROW_CW01, ROW_CW23, ROW_HG_LB, ROW_HG_NW, ROW_LOSS = 4, 5, 6, 7, 8, 9, 10


def _mm(a, b):
    return jnp.dot(a.astype(BF16), b.astype(BF16), preferred_element_type=F32)


def _mm_nt(a, b):
    return lax.dot_general(a.astype(BF16), b.astype(BF16), (((1,), (1,)), ((), ())),
                           preferred_element_type=F32)


def _mm_tn(a, b):
    return lax.dot_general(a.astype(BF16), b.astype(BF16), (((0,), (0,)), ((), ())),
                           preferred_element_type=F32)


def _mm_exact(a, b):
    return jnp.dot(a, b, precision=HIGHEST, preferred_element_type=F32)


def _sig_pair(x):
    e = jnp.exp(-jnp.abs(x))
    big = 1.0 / (1.0 + e)
    small = e * big
    pos = x >= 0
    return jnp.where(pos, big, small), jnp.where(pos, small, big)


def _rowsum(v):
    return jnp.sum(v, axis=0, keepdims=True)


def _shift_down(cur, prev8, d, rows):
    n = cur.shape[0]
    return jnp.where(rows < d, jnp.tile(pltpu.roll(prev8, d, 0), (n // 8, 1)), pltpu.roll(cur, d, 0))


def _shift_up(cur, next8, d, rows):
    n = cur.shape[0]
    return jnp.where(rows >= n - d, jnp.tile(pltpu.roll(next8, 8 - d, 0), (n // 8, 1)),
                     pltpu.roll(cur, n - d, 0))


def _cparams(sem, vmem_mib):
    return pltpu.CompilerParams(dimension_semantics=sem, vmem_limit_bytes=vmem_mib * MIB)


def _mesh_pos():
    x, y, c = lax.axis_index("x"), lax.axis_index("y"), lax.axis_index("c")
    chips = [(1 - x, y), (x, 1 - y), (1 - x, 1 - y)]
    return x, y, c, chips


def _remote(src, dst, ssem, rsem, dev):
    return pltpu.make_async_remote_copy(src_ref=src, dst_ref=dst, send_sem=ssem, recv_sem=rsem,
                                        device_id=dev, device_id_type=MESH)


def _gather_weights(w_in, w_out, w_pg, w_pp, conv_w):
    shapes = [w_in.shape, w_out.shape, w_pg.shape, w_pp.shape]

    def body(win, wout, wpg, wpp, cw, o_in, o_out, o_pg, o_pp, o_cw, ssem, rsem):
        x, y, c, chips = _mesh_pos()
        kme = 2 * x + y
        sibling = (x, y, 1 - c)
        params = [(win, o_in), (wout, o_out), (wpg, o_pg), (wpp, o_pp)]
        for src, dst in params:
            rows = src.shape[0]
            for r0 in range(0, rows, 128):
                dst[kme, r0:r0 + 128, :] = src[r0:r0 + 128, :].astype(BF16)
        o_cw[kme] = cw[...]

        def halves(dst):
            hrows = dst.shape[1] // 2
            mine = pl.ds(pl.multiple_of(c * hrows, 128), hrows)
            other = pl.ds(pl.multiple_of((1 - c) * hrows, 128), hrows)
            return mine, other

        sends = []
        n = 0
        for _, dst in params:
            mine, _o = halves(dst)
            for px, py in chips:
                cp = _remote(dst.at[kme, mine], dst.at[kme, mine], ssem.at[n], rsem.at[n], (px, py, c))
                cp.start()
                sends.append(cp)
                n += 1
        for px, py in chips:
            cp = _remote(o_cw.at[kme], o_cw.at[kme], ssem.at[n], rsem.at[n], (px, py, c))
            cp.start()
            sends.append(cp)
            n += 1
        n = 0
        m = 15
        for _, dst in params:
            mine, _o = halves(dst)
            for px, py in chips:
                kj = 2 * px + py
                _remote(dst.at[kj, mine], dst.at[kj, mine], ssem.at[n], rsem.at[n], (px, py, c)).wait_recv()
                cp = _remote(dst.at[kj, mine], dst.at[kj, mine], ssem.at[m], rsem.at[m], sibling)
                cp.start()
                sends.append(cp)
                n += 1
                m += 1
        for px, py in chips:
            kj = 2 * px + py
            _remote(o_cw.at[kj], o_cw.at[kj], ssem.at[n], rsem.at[n], (px, py, c)).wait_recv()
            n += 1
        m = 15
        for _, dst in params:
            _mn, other = halves(dst)
            for px, py in chips:
                kj = 2 * px + py
                _remote(dst.at[kj, other], dst.at[kj, other], ssem.at[m], rsem.at[m], sibling).wait_recv()
                m += 1
        for cp in sends:
            cp.wait_send()

    out_shape = [jax.ShapeDtypeStruct((NSHARD,) + s, BF16) for s in shapes]
    out_shape.append(jax.ShapeDtypeStruct((NSHARD,) + conv_w.shape, F32))
    return pl.pallas_call(
        body, name="gather_weights", out_shape=out_shape,
        in_specs=[VMEM_SPEC] * 5, out_specs=[VMEM_SPEC] * 5,
        scratch_shapes=[pltpu.SemaphoreType.DMA((27,)), pltpu.SemaphoreType.DMA((27,))],
        compiler_params=pltpu.CompilerParams(vmem_limit_bytes=40 * MIB),
    )(w_in, w_out, w_pg, w_pp, conv_w)


def _reduce_scatter(name, g_part, vmem_mib):
    _, R, C = g_part.shape
    H = R // 2
    step = 128

    def body(g_ref, g_out, gb, sib_recv, chip_b, ici_recv, ssem, rsem):
        x, y, c, chips = _mesh_pos()
        kme = 2 * x + y
        sibling = (x, y, 1 - c)
        mine0 = pl.multiple_of(c * H, step)
        other0 = pl.multiple_of((1 - c) * H, step)
        other = pl.ds(other0, H)

        first = []
        for k in range(NSHARD):
            for r0 in range(0, H, step):
                gb[k, r0:r0 + step, :] = g_ref[k, pl.ds(other0 + r0, step), :].astype(BF16)
            cp = _remote(gb.at[k], sib_recv.at[k], ssem.at[k], rsem.at[k], sibling)
            cp.start()
            first.append(cp)
        for cp in first:
            cp.wait_recv()

        second = []
        for j, (px, py) in enumerate(chips):
            kj = 2 * px + py
            for r0 in range(0, H, step):
                s = g_ref[kj, pl.ds(mine0 + r0, step), :] + sib_recv[kj, r0:r0 + step, :].astype(F32)
                chip_b[j, r0:r0 + step, :] = s.astype(BF16)
            cp = _remote(chip_b.at[j], ici_recv.at[j], ssem.at[4 + j], rsem.at[4 + j], (px, py, c))
            cp.start()
            second.append(cp)
        for cp in second:
            cp.wait_recv()
        for r0 in range(0, H, step):
            s = g_ref[kme, pl.ds(mine0 + r0, step), :] + sib_recv[kme, r0:r0 + step, :].astype(F32)
            for j in range(3):
                s = s + ici_recv[j, r0:r0 + step, :].astype(F32)
            g_out[pl.ds(mine0 + r0, step), :] = s

        mine = pl.ds(mine0, H)
        last = _remote(g_out.at[mine], g_out.at[mine], ssem.at[7], rsem.at[7], sibling)
        last.start()
        _remote(g_out.at[other], g_out.at[other], ssem.at[7], rsem.at[7], sibling).wait_recv()
        for cp in first + second + [last]:
            cp.wait_send()

    return pl.pallas_call(
        body, name=name, out_shape=jax.ShapeDtypeStruct((R, C), F32),
        in_specs=[VMEM_SPEC], out_specs=VMEM_SPEC,
        scratch_shapes=[
            pltpu.VMEM((NSHARD, H, C), BF16),
            pltpu.VMEM((NSHARD, H, C), BF16),
            pltpu.VMEM((3, H, C), BF16),
            pltpu.VMEM((3, H, C), BF16),
            pltpu.SemaphoreType.DMA((8,)),
            pltpu.SemaphoreType.DMA((8,)),
        ],
        compiler_params=pltpu.CompilerParams(vmem_limit_bytes=vmem_mib * MIB),
    )(g_part)


def _small_allreduce(sm_in, sm_tail, sm_a, sm_b, rg_c):
    PR = DG // NDEV

    def body(in_ref, tail_ref, a_ref, b_ref, rg_ref, v_out, rg_out, vbuf, vrecv, rgrecv, ssem, rsem):
        x, y, c = lax.axis_index("x"), lax.axis_index("y"), lax.axis_index("c")
        me = 4 * x + 2 * y + c

        def pair(ref, r0, r1):
            return jnp.concatenate([ref[r0:r0 + 1, :], ref[r1:r1 + 1, :]], axis=1)

        rows = {
            ROW_NORM_MIX: in_ref[0:1, :], ROW_FINAL_NORM: tail_ref[1:2, :], ROW_B_PG: tail_ref[2:3, :],
            ROW_PLE_NORM: tail_ref[3:4, :], ROW_CB_BA: pair(a_ref, 0, 1), ROW_BX_LAM: pair(a_ref, 2, 3),
            ROW_CW01: pair(a_ref, 4, 5), ROW_CW23: pair(a_ref, 6, 7), ROW_HG_LB: pair(b_ref, 2, 3),
            ROW_HG_NW: jnp.concatenate([b_ref[1:2, :], jnp.zeros((1, DG), F32)], axis=1),
            ROW_LOSS: tail_ref[4:5, :],
        }
        vbuf[...] = jnp.zeros_like(vbuf)
        for r, row in rows.items():
            for j in range(NDEV):
                vbuf[j, r:r + 1, :] = row[:, j * 128:(j + 1) * 128]

        def peer(mask):
            px = x ^ ((mask >> 2) & 1)
            py = y ^ ((mask >> 1) & 1)
            pc = c ^ (mask & 1)
            return (px, py, pc), 4 * px + 2 * py + pc

        def rg_rows(r):
            return pl.ds(pl.multiple_of(r * PR, PR), PR)

        first = []
        for mask in range(1, NDEV):
            dev, r = peer(mask)
            i = mask - 1
            cp = _remote(vbuf.at[r], vrecv.at[i], ssem.at[i], rsem.at[i], dev)
            cp.start()
            first.append(cp)
            cp = _remote(rg_ref.at[rg_rows(r)], rgrecv.at[i], ssem.at[7 + i], rsem.at[7 + i], dev)
            cp.start()
            first.append(cp)
        sv = vbuf[me]
        sr = rg_ref[rg_rows(me), :]
        for i in range(NDEV - 1):
            first[2 * i].wait_recv()
            first[2 * i + 1].wait_recv()
            sv = sv + vrecv[i]
            sr = sr + rgrecv[i]
        v_out[me] = sv
        rg_out[rg_rows(me), :] = sr
        second = []
        for mask in range(1, NDEV):
            dev, r = peer(mask)
            i = mask - 1
            cp = _remote(v_out.at[me], v_out.at[me], ssem.at[14 + i], rsem.at[14 + i], dev)
            cp.start()
            second.append(cp)
            cp = _remote(rg_out.at[rg_rows(me)], rg_out.at[rg_rows(me)], ssem.at[21 + i], rsem.at[21 + i], dev)
            cp.start()
            second.append(cp)
        for mask in range(1, NDEV):
            dev, r = peer(mask)
            i = mask - 1
            _remote(v_out.at[r], v_out.at[r], ssem.at[14 + i], rsem.at[14 + i], dev).wait_recv()
            _remote(rg_out.at[rg_rows(r)], rg_out.at[rg_rows(r)], ssem.at[21 + i], rsem.at[21 + i], dev).wait_recv()
        for cp in first + second:
            cp.wait_send()

    return pl.pallas_call(
        body, name="small_allreduce",
        out_shape=[jax.ShapeDtypeStruct((NDEV, VROWS, 128), F32), jax.ShapeDtypeStruct((DG, 128), F32)],
        in_specs=[VMEM_SPEC] * 5, out_specs=[VMEM_SPEC] * 2,
        scratch_shapes=[pltpu.VMEM((NDEV, VROWS, 128), F32), pltpu.VMEM((NDEV - 1, VROWS, 128), F32),
                        pltpu.VMEM((NDEV - 1, PR, 128), F32),
                        pltpu.SemaphoreType.DMA((28,)), pltpu.SemaphoreType.DMA((28,))],
    )(sm_in, sm_tail, sm_a, sm_b, rg_c)


def _adam_rows(w, g, m, v):
    m2 = ADAM_B1 * m + (1.0 - ADAM_B1) * g
    v2 = ADAM_B2 * v + (1.0 - ADAM_B2) * (g * g)
    m_hat = m2 / (1.0 - ADAM_B1 ** ADAM_STEP)
    v_hat = v2 / (1.0 - ADAM_B2 ** ADAM_STEP)
    delta = -ADAM_LR * (m_hat / (jnp.sqrt(v_hat) + ADAM_EPS) + ADAM_WD * w)
    return delta, m2, v2


def _adam_big(gs, ws, ms, vs):
    n = len(gs)
    steps = 8

    def body(*refs):
        ins, outs = refs[:4 * n], refs[4 * n:]
        for i in range(n):
            g, w, m, v = (r[...] for r in ins[4 * i:4 * i + 4])
            d, m2, v2 = _adam_rows(w, g, m, v)
            outs[3 * i][...] = d
            outs[3 * i + 1][...] = m2
            outs[3 * i + 2][...] = v2

    in_specs, out_specs, out_shape, args = [], [], [], []
    for g, w, m, v in zip(gs, ws, ms, vs):
        r, c = w.shape
        spec = lambda: pl.BlockSpec((r // steps, c), lambda i: (i, 0))
        in_specs += [spec() for _ in range(4)]
        out_specs += [spec() for _ in range(3)]
        out_shape += [jax.ShapeDtypeStruct((r, c), F32)] * 3
        args += [g, w, m, v]
    outs = pl.pallas_call(
        body, name="adam_big", grid=(steps,), in_specs=in_specs, out_specs=out_specs, out_shape=out_shape,
        compiler_params=_cparams(("parallel",), 32),
    )(*args)
    return [tuple(outs[3 * i:3 * i + 3]) for i in range(n)]


_VEC_PARAMS = [
    ("norm_mix_w", ROW_NORM_MIX, 0, D), ("final_norm_w", ROW_FINAL_NORM, 0, D),
    ("b_ple_gate", ROW_B_PG, 0, D), ("ple_norm_w", ROW_PLE_NORM, 0, D),
    ("conv_b", ROW_CB_BA, 0, DG), ("rg_ba", ROW_CB_BA, DG, DG),
    ("rg_bx", ROW_BX_LAM, 0, DG), ("rg_lambda", ROW_BX_LAM, DG, DG),
    ("hg_norm_w", ROW_HG_NW, 0, HD),
]
_SMALL_ORDER = [n for n, _, _, _ in _VEC_PARAMS] + ["hg_lb", "conv_w", "rg_wa", "rg_wx"]


def _adam_small(vred, rgred, ws, ms, vs):
    names = _SMALL_ORDER
    n = len(names)

    def body(vred_ref, rg_ref, *refs):
        w_refs = dict(zip(names, refs[0:n]))
        m_refs = dict(zip(names, refs[n:2 * n]))
        v_refs = dict(zip(names, refs[2 * n:3 * n]))
        outs = refs[3 * n:]
        o_refs = {nm: outs[4 * i:4 * i + 4] for i, nm in enumerate(names)}
        kme = 2 * lax.axis_index("x") + lax.axis_index("y")

        def update(nm, g, idx):
            d, m2, v2 = _adam_rows(w_refs[nm][idx], g, m_refs[nm][idx], v_refs[nm][idx])
            og, od, om, ov = o_refs[nm]
            og[idx] = g
            od[idx] = d
            om[idx] = m2
            ov[idx] = v2

        def packed(row, lane0, width):
            return jnp.concatenate([vred_ref[j, row:row + 1, :] for j in range(lane0 // 128, (lane0 + width) // 128)],
                                   axis=1)

        everything = (slice(None), slice(None))
        for nm, row, lane0, width in _VEC_PARAMS:
            update(nm, packed(row, lane0, width), everything)
        for r in range(2):
            update("hg_lb", packed(ROW_HG_LB, r * DG, DG), (slice(r, r + 1), slice(None)))
        for j in range(4):
            g = vred_ref[(j % 2) * 4 + kme, ROW_CW01 + j // 2:ROW_CW01 + j // 2 + 1, :]
            update("conv_w", g, (slice(j, j + 1), slice(None)))
        for r0 in range(0, DG, 128):
            rs = (slice(r0, r0 + 128), slice(None))
            both = rg_ref[r0:r0 + 128, :]
            update("rg_wa", both[:, 0:RGB], rs)
            update("rg_wx", pltpu.roll(both, RGB, 1)[:, 0:RGB], rs)

    args = [vred, rgred] + [d[nm] for d in (ws, ms, vs) for nm in names]
    out_shape = []
    for nm in names:
        out_shape += [jax.ShapeDtypeStruct(ws[nm].shape, F32)] * 4
    outs = pl.pallas_call(
        body, name="adam_small", out_shape=out_shape,
        in_specs=[VMEM_SPEC] * len(args), out_specs=[VMEM_SPEC] * len(out_shape),
    )(*args)
    return {nm: tuple(outs[4 * i:4 * i + 4]) for i, nm in enumerate(names)}


def _fwd_inproj(x, nw, w_in_b):
    tm = 512

    def body(x_ref, nw_ref, w_ref, proj_ref, u_ref):
        xv = x_ref[...]
        s = lax.rsqrt(jnp.mean(xv * xv, axis=-1, keepdims=True) + EPS)
        u = (xv * s * nw_ref[...]).astype(BF16)
        u_ref[...] = u
        for k in range(NSHARD):
            proj_ref[:, k * SHW:(k + 1) * SHW] = jnp.dot(u, w_ref[k], preferred_element_type=F32)

    return pl.pallas_call(
        body, name="fwd_inproj", grid=(T // tm,),
        in_specs=[pl.BlockSpec((tm, D), lambda i: (i, 0)),
                  pl.BlockSpec((1, D), lambda i: (0, 0)),
                  pl.BlockSpec((NSHARD, D, SHW), lambda i: (0, 0, 0))],
        out_specs=[pl.BlockSpec((tm, DIN), lambda i: (i, 0)),
                   pl.BlockSpec((tm, D), lambda i: (i, 0))],
        out_shape=[jax.ShapeDtypeStruct((T, DIN), F32), jax.ShapeDtypeStruct((T, D), BF16)],
        compiler_params=_cparams(("parallel",), 48),
    )(x, nw, w_in_b)


def _conv_rows(cw_ref):
    return [jnp.concatenate([cw_ref[k, j:j + 1, :] for k in range(NSHARD)], axis=1) for j in range(4)]


def _rg_conv(xa, prev8, cw, cb, rows):
    taps = [_shift_down(xa, prev8, 3, rows), _shift_down(xa, prev8, 2, rows),
            _shift_down(xa, prev8, 1, rows), xa]
    xc = cb
    for j in range(4):
        xc = xc + taps[j] * cw[j]
    return xc, taps


def _block_mask():
    r = lax.broadcasted_iota(I32, (DG, DG), 0)
    c = lax.broadcasted_iota(I32, (DG, DG), 1)
    return (r >> 6) == (c >> 6)


def _dense_from_blocks(wc):
    j = lax.broadcasted_iota(I32, (RGB, DG), 0)
    c = lax.broadcasted_iota(I32, (RGB, DG), 1)
    spread = _mm_exact(wc, ((c & (RGB - 1)) == j).astype(F32))
    return jnp.where(_block_mask(), spread, 0.0)


def _blocks_from_dense(da, dx):
    c = lax.broadcasted_iota(I32, (DG, 128), 0)
    j = lax.broadcasted_iota(I32, (DG, 128), 1)
    hit = (c & (RGB - 1)) == (j & (RGB - 1))
    mask = _block_mask()
    return (_mm_exact(jnp.where(mask, da, 0.0), (hit & (j < RGB)).astype(F32))
            + _mm_exact(jnp.where(mask, dx, 0.0), (hit & (j >= RGB)).astype(F32)))


def _rg_gates(xc, wa, ba, wx, bx, sp, first_row):
    r, _ = _sig_pair(_mm(xc, wa) + ba)
    i, _ = _sig_pair(_mm(xc, wx) + bx)
    log_a = (-RG_C) * r * sp
    a = jnp.exp(log_a)
    a2 = a * a
    one_m_a2 = -jnp.tanh(log_a) * (a2 + 1.0)
    mult = jnp.where(first_row, 1.0, jnp.sqrt(one_m_a2))
    return r, i, a, a2, mult


def _softplus(z):
    return jnp.maximum(z, 0.0) + jnp.log1p(jnp.exp(-jnp.abs(z)))


def _fwd_rglru(proj, cw3, conv_b, wa_c, ba, wx_c, bx, lam):
    tm = 512
    ng = tm // 8

    def body(xa_ref, ga_ref, cw_ref, cb_ref, wa_ref, ba_ref, wx_ref, bx_ref, lam_ref,
             h_ref, ya_ref, a_s, u_s, tail_s, hc_s, wa_s, wx_s):
        i = pl.program_id(0)

        @pl.when(i == 0)
        def _():
            tail_s[...] = jnp.zeros_like(tail_s)
            hc_s[...] = jnp.zeros_like(hc_s)
            wa_s[...] = _dense_from_blocks(wa_ref[...]).astype(BF16)
            wx_s[...] = _dense_from_blocks(wx_ref[...]).astype(BF16)

        rows = lax.broadcasted_iota(I32, (tm, DG), 0)
        xa = xa_ref[...]
        xc, _ = _rg_conv(xa, tail_s[...], _conv_rows(cw_ref), cb_ref[...], rows)
        tail_s[...] = xa[tm - 8:tm, :]
        sp = _softplus(-lam_ref[...])
        first_row = (rows + i * tm) == 0
        r, ig, a, a2, mult = _rg_gates(xc, wa_s[...], ba_ref[...], wx_s[...], bx_ref[...], sp, first_row)
        a_s[...] = a
        u_s[...] = mult * (ig * xc)

        row8 = lax.broadcasted_iota(I32, (8, DG), 0)

        def group(g, hprev):
            sl = pl.ds(pl.multiple_of(g * 8, 8), 8)
            av = a_s[sl, :]
            uv = u_s[sl, :]
            for d in (1, 2, 4):
                keep = row8 >= d
                uv = uv + av * jnp.where(keep, pltpu.roll(uv, d, 0), 0.0)
                av = av * jnp.where(keep, pltpu.roll(av, d, 0), 1.0)
            hv = uv + av * hprev
            h_ref[sl, :] = hv
            return hv[7:8, :]

        hlast = lax.fori_loop(0, ng, group, hc_s[0:1, :])
        hc_s[0:1, :] = hlast
        ga = ga_ref[...]
        sg, _ = _sig_pair(ga)
        ya_ref[...] = (h_ref[...] * (ga * sg)).astype(BF16)

    vec = lambda: pl.BlockSpec((1, DG), lambda i: (0, 0))
    blocks = lambda: pl.BlockSpec((DG, RGB), lambda i: (0, 0))
    return pl.pallas_call(
        body, name="fwd_rglru", grid=(T // tm,),
        in_specs=[pl.BlockSpec((tm, DG), lambda i: (i, 0)),
                  pl.BlockSpec((tm, DG), lambda i: (i, 1)),
                  pl.BlockSpec((NSHARD, 4, 128), lambda i: (0, 0, 0)), vec(),
                  blocks(), vec(), blocks(), vec(), vec()],
        out_specs=[pl.BlockSpec((tm, DG), lambda i: (i, 0)),
                   pl.BlockSpec((tm, DG), lambda i: (i, 0))],
        out_shape=[jax.ShapeDtypeStruct((T, DG), F32), jax.ShapeDtypeStruct((T, DG), BF16)],
        scratch_shapes=[pltpu.VMEM((tm, DG), F32), pltpu.VMEM((tm, DG), F32),
                        pltpu.VMEM((8, DG), F32), pltpu.VMEM((8, DG), F32),
                        pltpu.VMEM((DG, DG), BF16), pltpu.VMEM((DG, DG), BF16)],
        compiler_params=_cparams(("arbitrary",), 48),
    )(proj, proj, cw3, conv_b, wa_c, ba, wx_c, bx, lam)


def _hg_lower_bound(lb_ref):
    return _sig_pair(lb_ref[0:1, :] - lb_ref[1:2, :])


def _hg_gates(fz, lb, one_m_lb):
    sg, sn = _sig_pair(fz)
    f = lb + one_m_lb * sg
    return sg, sn, f, jnp.log(f), one_m_lb * sn


def _tri(lower):
    r = lax.broadcasted_iota(I32, (CH, CH), 0)
    c = lax.broadcasted_iota(I32, (CH, CH), 1)
    return (r >= c) if lower else (r <= c)


def _chunk_terms(qs_c, k_c, logf_c, tril_f):
    b = _mm_exact(tril_f, logf_c)
    bl = _rowsum(logf_c)
    eb = jnp.exp(b)
    enb = jnp.exp(-b)
    ekd = jnp.exp(bl - b)
    return bl, eb, enb, ekd, qs_c * eb, k_c * enb, k_c * ekd


def _fwd_hgrn2(proj, hg_lb, hg_nw):
    tm = 512
    nc = tm // CH

    def body(q_ref, f_ref, v_ref, g_ref, lb_ref, nw_ref, yb_ref, o_ref, sp_ref, st_s):
        i = pl.program_id(0)

        @pl.when(i == 0)
        def _():
            st_s[...] = jnp.zeros_like(st_s)

        lb_all, one_m_lb_all = _hg_lower_bound(lb_ref)
        mask = _tri(True)
        tril_f = mask.astype(F32)
        nw = nw_ref[...]
        for hd in range(NH):
            cols = slice(hd * HD, (hd + 1) * HD)
            lb, one_m_lb = lb_all[:, cols], one_m_lb_all[:, cols]
            _sg, _sn, _f, logf, k = _hg_gates(f_ref[:, cols], lb, one_m_lb)
            q = q_ref[:, cols]
            sq, _ = _sig_pair(q)
            qs = q * sq * (HD ** -0.5)
            v = v_ref[:, cols]
            for ci in range(nc):
                rs = slice(ci * CH, (ci + 1) * CH)
                bl, eb, enb, ekd, qe, ke, kd = _chunk_terms(qs[rs], k[rs], logf[rs], tril_f)
                p = jnp.where(mask, _mm_nt(qe, ke), 0.0)
                st = st_s[hd]
                sp_ref[hd, ci] = st
                o_ref[rs, cols] = _mm(p, v[rs]) + _mm_nt(qe, st)
                st_s[hd] = st * jnp.exp(bl) + _mm_tn(v[rs], kd)
            o = o_ref[:, cols]
            so = lax.rsqrt(jnp.mean(o * o, axis=-1, keepdims=True) + EPS)
            g = g_ref[:, cols]
            sg, _ = _sig_pair(g)
            yb_ref[:, cols] = (o * so * nw * (g * sg)).astype(BF16)

    col = lambda j: pl.BlockSpec((tm, DG), lambda i: (i, j))
    return pl.pallas_call(
        body, name="fwd_hgrn2", grid=(T // tm,),
        in_specs=[col(2), col(3), col(4), col(5),
                  pl.BlockSpec((2, DG), lambda i: (0, 0)),
                  pl.BlockSpec((1, HD), lambda i: (0, 0))],
        out_specs=[pl.BlockSpec((tm, DG), lambda i: (i, 0)),
                   pl.BlockSpec((tm, DG), lambda i: (i, 0)),
                   pl.BlockSpec((NH, nc, HD, HD), lambda i: (0, i, 0, 0))],
        out_shape=[jax.ShapeDtypeStruct((T, DG), BF16), jax.ShapeDtypeStruct((T, DG), F32),
                   jax.ShapeDtypeStruct((NH, NCHUNK, HD, HD), F32)],
        scratch_shapes=[pltpu.VMEM((NH, HD, HD), F32)],
        compiler_params=_cparams(("arbitrary",), 48),
    )(proj, proj, proj, proj, hg_lb, hg_nw)


def _tail_fwd_bwd(x, p, tgt, ya, yb, w_out_b, w_pg_b, w_pp_b, ple_nw, b_pg, fnw):
    tm = 256
    nt = T // tm

    def body(x_ref, p_ref, t_ref, ya_ref, yb_ref, wo_ref, wg_ref, wp_ref, pw_ref, b_ref, fw_ref,
             dh1_ref, dyab_ref, dwo_ref, dwg_ref, dwp_ref, sm_ref):
        i = pl.program_id(0)

        @pl.when(i == 0)
        def _():
            dwo_ref[...] = jnp.zeros_like(dwo_ref)
            dwg_ref[...] = jnp.zeros_like(dwg_ref)
            dwp_ref[...] = jnp.zeros_like(dwp_ref)
            sm_ref[...] = jnp.zeros_like(sm_ref)

        ya = ya_ref[...]
        yb = yb_ref[...]
        pv = p_ref[...].astype(BF16)
        pw = pw_ref[...]
        fw = fw_ref[...]
        h1 = x_ref[...] + _mm(ya, wo_ref[0:DG, :]) + _mm(yb, wo_ref[DG:D, :])
        s2 = lax.rsqrt(jnp.mean(h1 * h1, axis=-1, keepdims=True) + EPS)
        n2h = h1 * s2
        n2 = (n2h * pw).astype(BF16)
        z = _mm(n2, wg_ref[...]) + b_ref[...]
        gate, gate_n = _sig_pair(z)
        pp = jnp.concatenate([_mm(pv, wp_ref[k]) for k in range(NSHARD)], axis=1)
        h2 = h1 + gate * pp
        s3 = lax.rsqrt(jnp.mean(h2 * h2, axis=-1, keepdims=True) + EPS)
        hn = h2 * s3
        err = hn * fw - t_ref[...]
        sm_ref[0:1, :] += _rowsum(err * err)
        dy = err * (1.0 / D)
        sm_ref[1:2, :] += _rowsum(dy * hn)
        g3 = dy * fw
        dh2 = s3 * (g3 - hn * jnp.mean(g3 * hn, axis=-1, keepdims=True))
        dpp = (dh2 * gate).astype(BF16)
        dz = dh2 * pp * gate * gate_n
        sm_ref[2:3, :] += _rowsum(dz)
        dzb = dz.astype(BF16)
        dwg_ref[...] += _mm_tn(n2, dzb)
        dn2 = _mm_nt(dzb, wg_ref[...])
        for k in range(NSHARD):
            dwp_ref[k] += _mm_tn(pv, dpp[:, k * PLE:(k + 1) * PLE])
        sm_ref[3:4, :] += _rowsum(dn2 * n2h)
        g2 = dn2 * pw
        dh1 = dh2 + s2 * (g2 - n2h * jnp.mean(g2 * n2h, axis=-1, keepdims=True))
        dh1_ref[...] = dh1
        dh1b = dh1.astype(BF16)
        dyab_ref[...] = _mm_nt(dh1b, wo_ref[...])
        dwo_ref[0:DG, :] += _mm_tn(ya, dh1b)
        dwo_ref[DG:D, :] += _mm_tn(yb, dh1b)

        @pl.when(i == nt - 1)
        def _():
            total = jnp.sum(sm_ref[0:1, :], axis=-1, keepdims=True) * (0.5 / D)
            sm_ref[4:5, :] = jnp.broadcast_to(total, (1, D))

    row = lambda w: pl.BlockSpec((tm, w), lambda i: (i, 0))
    const2 = lambda s: pl.BlockSpec(s, lambda i: (0, 0))
    const3 = lambda s: pl.BlockSpec(s, lambda i: (0, 0, 0))
    return pl.pallas_call(
        body, name="tail_fwd_bwd", grid=(nt,),
        in_specs=[row(D), row(PLE), row(D), row(DG), row(DG),
                  const2((D, D)), const2((D, D)), const3((NSHARD, PLE, PLE)),
                  const2((1, D)), const2((1, D)), const2((1, D))],
        out_specs=[row(D), row(D), const2((D, D)), const2((D, D)),
                   const3((NSHARD, PLE, PLE)), const2((8, D))],
        out_shape=[jax.ShapeDtypeStruct((T, D), F32), jax.ShapeDtypeStruct((T, D), F32),
                   jax.ShapeDtypeStruct((D, D), F32), jax.ShapeDtypeStruct((D, D), F32),
                   jax.ShapeDtypeStruct((NSHARD, PLE, PLE), F32), jax.ShapeDtypeStruct((8, D), F32)],
        compiler_params=_cparams(("arbitrary",), 56),
    )(x, p, tgt, ya, yb, w_out_b, w_pg_b, w_pp_b, ple_nw, b_pg, fnw)


def _bwd_rglru(proj, h, dyab, cw3, conv_b, wa_c, ba, wx_c, bx, lam):
    tm = 512
    nt = T // tm
    ng = tm // 8

    def body(xa_ref, ga_ref, xp_ref, h_ref, hp_ref, dya_ref, cw_ref, cb_ref, wa_ref, ba_ref, wx_ref, bx_ref,
             lam_ref, da_ref, rg_ref, sm_ref, a_s, g_s, cg_s, nxt_s, wa_s, wx_s, dwa_s, dwx_s):
        i = pl.program_id(0)
        tile = nt - 1 - i

        @pl.when(i == 0)
        def _():
            dwa_s[...] = jnp.zeros_like(dwa_s)
            dwx_s[...] = jnp.zeros_like(dwx_s)
            sm_ref[...] = jnp.zeros_like(sm_ref)
            cg_s[...] = jnp.zeros_like(cg_s)
            nxt_s[...] = jnp.zeros_like(nxt_s)
            wa_s[...] = _dense_from_blocks(wa_ref[...]).astype(BF16)
            wx_s[...] = _dense_from_blocks(wx_ref[...]).astype(BF16)

        rows = lax.broadcasted_iota(I32, (tm, DG), 0)
        has_prev = tile > 0
        xa = xa_ref[...]
        xprev = jnp.where(has_prev, xp_ref[...], 0.0)
        cw = _conv_rows(cw_ref)
        xc, taps = _rg_conv(xa, xprev, cw, cb_ref[...], rows)
        lam_v = lam_ref[...]
        sp = _softplus(-lam_v)
        first_row = (rows + tile * tm) == 0
        r, ig, a, a2, mult = _rg_gates(xc, wa_s[...], ba_ref[...], wx_s[...], bx_ref[...], sp, first_row)
        hv = h_ref[...]
        hprev = jnp.where(has_prev, hp_ref[...], 0.0)
        h_m1 = _shift_down(hv, hprev, 1, rows)
        ga = ga_ref[...]
        sg, sgn = _sig_pair(ga)
        dya = dya_ref[...]
        dga = dya * hv * (sg * (1.0 + ga * sgn))

        a_s[...] = jnp.where(rows == tm - 1, 1.0, pltpu.roll(a, tm - 1, 0))
        g_s[...] = dya * (ga * sg)
        row8 = lax.broadcasted_iota(I32, (8, DG), 0)

        def group(k, carry):
            g = ng - 1 - k
            sl = pl.ds(pl.multiple_of(g * 8, 8), 8)
            av = a_s[sl, :]
            gv = g_s[sl, :]
            for d in (1, 2, 4):
                keep = row8 < 8 - d
                gv = gv + av * jnp.where(keep, pltpu.roll(gv, 8 - d, 0), 0.0)
                av = av * jnp.where(keep, pltpu.roll(av, 8 - d, 0), 1.0)
            gv = gv + av * carry
            g_s[sl, :] = gv
            return gv[0:1, :]

        gfirst = lax.fori_loop(0, ng, group, cg_s[0:1, :])
        cg_s[0:1, :] = a[0:1, :] * gfirst

        gt = g_s[...]
        da = gt * h_m1
        ixc = ig * xc
        di = gt * mult * xc
        dxc = gt * mult * ig
        dlog_a = da * a + jnp.where(first_row, 0.0, gt * ixc * (-a2 / mult))
        sm_ref[3:4, :] += _rowsum(dlog_a * ((-RG_C) * r))
        dpr_f = dlog_a * ((-RG_C) * sp) * r * (1.0 - r)
        dpi_f = di * ig * (1.0 - ig)
        sm_ref[1:2, :] += _rowsum(dpr_f)
        sm_ref[2:3, :] += _rowsum(dpi_f)
        dpr = dpr_f.astype(BF16)
        dpi = dpi_f.astype(BF16)
        xcb = xc.astype(BF16)
        dwa_s[...] += _mm_tn(xcb, dpr)
        dwx_s[...] += _mm_tn(xcb, dpi)
        dxc = dxc + _mm_nt(dpr, wa_s[...]) + _mm_nt(dpi, wx_s[...])
        sm_ref[0:1, :] += _rowsum(dxc)
        for j in range(4):
            sm_ref[4 + j:5 + j, :] += _rowsum(dxc * taps[j])
        nxt = nxt_s[...]
        dxa = (dxc * cw[3] + _shift_up(dxc, nxt, 1, rows) * cw[2]
               + _shift_up(dxc, nxt, 2, rows) * cw[1] + _shift_up(dxc, nxt, 3, rows) * cw[0])
        nxt_s[...] = dxc[0:8, :]
        da_ref[:, 0:DG] = dxa.astype(BF16)
        da_ref[:, DG:D] = dga.astype(BF16)

        @pl.when(i == nt - 1)
        def _():
            _, s_neg = _sig_pair(lam_v)
            sm_ref[3:4, :] = sm_ref[3:4, :] * (-s_neg)
            rg_ref[...] = _blocks_from_dense(dwa_s[...], dwx_s[...])

    vec = lambda: pl.BlockSpec((1, DG), lambda i: (0, 0))
    blocks = lambda: pl.BlockSpec((DG, RGB), lambda i: (0, 0))
    prev8 = lambda: pl.BlockSpec((8, DG), lambda i: (jnp.maximum((nt - 1 - i) * (tm // 8) - 1, 0), 0))
    return pl.pallas_call(
        body, name="bwd_rglru", grid=(nt,),
        in_specs=[pl.BlockSpec((tm, DG), lambda i: (nt - 1 - i, 0)),
                  pl.BlockSpec((tm, DG), lambda i: (nt - 1 - i, 1)),
                  prev8(),
                  pl.BlockSpec((tm, DG), lambda i: (nt - 1 - i, 0)),
                  prev8(),
                  pl.BlockSpec((tm, DG), lambda i: (nt - 1 - i, 0)),
                  pl.BlockSpec((NSHARD, 4, 128), lambda i: (0, 0, 0)), vec(),
                  blocks(), vec(), blocks(), vec(), vec()],
        out_specs=[pl.BlockSpec((tm, D), lambda i: (nt - 1 - i, 0)),
                   pl.BlockSpec((DG, 128), lambda i: (0, 0)),
                   pl.BlockSpec((8, DG), lambda i: (0, 0))],
        out_shape=[jax.ShapeDtypeStruct((T, D), BF16), jax.ShapeDtypeStruct((DG, 128), F32),
                   jax.ShapeDtypeStruct((8, DG), F32)],
        scratch_shapes=[pltpu.VMEM((tm, DG), F32), pltpu.VMEM((tm, DG), F32),
                        pltpu.VMEM((8, DG), F32), pltpu.VMEM((8, DG), F32),
                        pltpu.VMEM((DG, DG), BF16), pltpu.VMEM((DG, DG), BF16),
                        pltpu.VMEM((DG, DG), F32), pltpu.VMEM((DG, DG), F32)],
        compiler_params=_cparams(("arbitrary",), 56),
    )(proj, proj, proj, h, h, dyab, cw3, conv_b, wa_c, ba, wx_c, bx, lam)


def _bwd_hgrn2(proj, o, s_prev, dyab, hg_lb, hg_nw):
    tm = 512
    nt = T // tm
    nc = tm // CH

    def body(q_ref, f_ref, v_ref, g_ref, o_ref, sp_ref, dy_ref, lb_ref, nw_ref,
             db_ref, sm_ref, dst_s, dq_s, dk_s, dlf_s):
        i = pl.program_id(0)

        @pl.when(i == 0)
        def _():
            sm_ref[...] = jnp.zeros_like(sm_ref)
            dst_s[...] = jnp.zeros_like(dst_s)

        lb_all, one_m_lb_all = _hg_lower_bound(lb_ref)
        mask = _tri(True)
        tril_f = mask.astype(F32)
        triu_f = _tri(False).astype(F32)
        nw = nw_ref[...]
        for hd in range(NH):
            cols = slice(hd * HD, (hd + 1) * HD)
            lb, one_m_lb = lb_all[:, cols], one_m_lb_all[:, cols]
            sg, sn, f, logf, k = _hg_gates(f_ref[:, cols], lb, one_m_lb)
            q = q_ref[:, cols]
            sq, sqn = _sig_pair(q)
            qs = q * sq * (HD ** -0.5)
            v = v_ref[:, cols]
            g = g_ref[:, cols]
            sgg, sggn = _sig_pair(g)
            silu_g = g * sgg
            o = o_ref[:, cols]
            so = lax.rsqrt(jnp.mean(o * o, axis=-1, keepdims=True) + EPS)
            oh = o * so
            dyb = dy_ref[:, cols]
            dgb = dyb * (oh * nw) * (sgg * (1.0 + g * sggn))
            don = dyb * silu_g
            sm_ref[1:2, 0:HD] += _rowsum(don * oh)
            gw = don * nw
            do = so * (gw - oh * jnp.mean(gw * oh, axis=-1, keepdims=True))
            db_ref[:, 3 * DG + hd * HD:3 * DG + (hd + 1) * HD] = dgb.astype(BF16)

            for cc in range(nc):
                ci = nc - 1 - cc
                rs = slice(ci * CH, (ci + 1) * CH)
                bl, eb, enb, ekd, qe, ke, kd = _chunk_terms(qs[rs], k[rs], logf[rs], tril_f)
                p = jnp.where(mask, _mm_nt(qe, ke), 0.0)
                st_prev = sp_ref[hd, ci]
                dst = dst_s[hd]
                do_c = do[rs].astype(BF16)
                v_c = v[rs].astype(BF16)
                qeb = qe.astype(BF16)
                dp = jnp.where(mask, _mm_nt(do_c, v_c), 0.0).astype(BF16)
                dv = _mm_tn(p, do_c) + _mm_nt(kd, dst)
                dqe = _mm(dp, ke) + _mm(do_c, st_prev)
                dke = _mm_tn(dp, qeb)
                dkd = _mm(v_c, dst)
                decay = jnp.exp(bl)
                ddecay = _rowsum(dst * st_prev)
                dst_s[hd] = dst * decay + _mm_tn(do_c, qeb)
                t_kd = dkd * kd
                db = dqe * qe - dke * ke - t_kd
                dbl = _rowsum(t_kd) + ddecay * decay
                dlf_s[rs, :] = _mm_exact(triu_f, db) + dbl
                dq_s[rs, :] = dqe * eb
                dk_s[rs, :] = dke * enb + dkd * ekd
                db_ref[rs, 2 * DG + hd * HD:2 * DG + (hd + 1) * HD] = dv.astype(BF16)

            t = dlf_s[...] / f - dk_s[...]
            db_ref[:, DG + hd * HD:DG + (hd + 1) * HD] = (one_m_lb * sg * sn * t).astype(BF16)
            sm_ref[0:1, cols] += _rowsum(sn * t)
            db_ref[:, hd * HD:(hd + 1) * HD] = (dq_s[...] * (sq * (1.0 + q * sqn)) * (HD ** -0.5)).astype(BF16)

        @pl.when(i == nt - 1)
        def _():
            dsm = sm_ref[0:1, :] * (lb_all * one_m_lb_all)
            sm_ref[2:3, :] = dsm
            sm_ref[3:4, :] = -dsm

    col = lambda j: pl.BlockSpec((tm, DG), lambda i: (nt - 1 - i, j))
    return pl.pallas_call(
        body, name="bwd_hgrn2", grid=(nt,),
        in_specs=[col(2), col(3), col(4), col(5),
                  pl.BlockSpec((tm, DG), lambda i: (nt - 1 - i, 0)),
                  pl.BlockSpec((NH, nc, HD, HD), lambda i: (0, nt - 1 - i, 0, 0)),
                  pl.BlockSpec((tm, DG), lambda i: (nt - 1 - i, 1)),
                  pl.BlockSpec((2, DG), lambda i: (0, 0)),
                  pl.BlockSpec((1, HD), lambda i: (0, 0))],
        out_specs=[pl.BlockSpec((tm, 4 * DG), lambda i: (nt - 1 - i, 0)),
                   pl.BlockSpec((8, DG), lambda i: (0, 0))],
        out_shape=[jax.ShapeDtypeStruct((T, 4 * DG), BF16), jax.ShapeDtypeStruct((8, DG), F32)],
        scratch_shapes=[pltpu.VMEM((NH, HD, HD), F32), pltpu.VMEM((tm, HD), F32),
                        pltpu.VMEM((tm, HD), F32), pltpu.VMEM((tm, HD), F32)],
        compiler_params=_cparams(("arbitrary",), 56),
    )(proj, proj, proj, proj, o, s_prev, dyab, hg_lb, hg_nw)


def _dproj_pieces(da_ref, db_ref):
    return [
        [(da_ref[:, 0:SHW], 0)],
        [(da_ref[:, SHW:D], 0), (db_ref[:, 0:DG], D - SHW)],
        [(db_ref[:, DG:DG + SHW], 0)],
        [(db_ref[:, DG + SHW:4 * DG], 0)],
    ]


def _bwd_inproj_dx(x, dh1, d_a, d_b, w_in_b, nw):
    tm = 512
    nt = T // tm

    def body(x_ref, dh1_ref, da_ref, db_ref, w_ref, nw_ref, dx_ref, sm_ref):
        i = pl.program_id(0)

        @pl.when(i == 0)
        def _():
            sm_ref[...] = jnp.zeros_like(sm_ref)

        du = None
        for k, pieces in enumerate(_dproj_pieces(da_ref, db_ref)):
            for val, off in pieces:
                t = _mm_nt(val, w_ref[k, :, off:off + val.shape[1]])
                du = t if du is None else du + t
        xv = x_ref[...]
        s = lax.rsqrt(jnp.mean(xv * xv, axis=-1, keepdims=True) + EPS)
        xh = xv * s
        sm_ref[0:1, :] += _rowsum(du * xh)
        g = du * nw_ref[...]
        dx_ref[...] = dh1_ref[...] + s * (g - xh * jnp.mean(g * xh, axis=-1, keepdims=True))

    row = lambda w: pl.BlockSpec((tm, w), lambda i: (i, 0))
    return pl.pallas_call(
        body, name="bwd_inproj_dx", grid=(nt,),
        in_specs=[row(D), row(D), row(D), row(4 * DG),
                  pl.BlockSpec((NSHARD, D, SHW), lambda i: (0, 0, 0)),
                  pl.BlockSpec((1, D), lambda i: (0, 0))],
        out_specs=[row(D), pl.BlockSpec((8, D), lambda i: (0, 0))],
        out_shape=[jax.ShapeDtypeStruct((T, D), F32), jax.ShapeDtypeStruct((8, D), F32)],
        compiler_params=_cparams(("arbitrary",), 56),
    )(x, dh1, d_a, d_b, w_in_b, nw)


def _bwd_inproj_dw(u_b, d_a, d_b):
    tm = 512
    nt = T // tm

    def body(u_ref, da_ref, db_ref, dw_ref):
        i = pl.program_id(0)

        @pl.when(i == 0)
        def _():
            dw_ref[...] = jnp.zeros_like(dw_ref)

        u = u_ref[...]
        for k, pieces in enumerate(_dproj_pieces(da_ref, db_ref)):
            for val, off in pieces:
                dw_ref[k, :, off:off + val.shape[1]] += _mm_tn(u, val)

    row = lambda w: pl.BlockSpec((tm, w), lambda i: (i, 0))
    return pl.pallas_call(
        body, name="bwd_inproj_dw", grid=(nt,),
        in_specs=[row(D), row(D), row(4 * DG)],
        out_specs=pl.BlockSpec((NSHARD, D, SHW), lambda i: (0, 0, 0)),
        out_shape=jax.ShapeDtypeStruct((NSHARD, D, SHW), F32),
        compiler_params=_cparams(("arbitrary",), 56),
    )(u_b, d_a, d_b)


_OUT_ORDER = ["norm_mix_w", "w_in", "conv_w", "conv_b", "rg_wa", "rg_ba", "rg_wx", "rg_bx", "rg_lambda", "hg_lb",
              "hg_norm_w", "w_out", "ple_norm_w", "w_ple_gate", "b_ple_gate", "w_ple_proj", "final_norm_w"]
_BIG = ["w_in", "w_out", "w_ple_gate", "w_ple_proj"]


def _small_view(name, a):
    if name in ("rg_wa", "rg_wx"):
        return a.reshape(DG, RGB)
    if name == "conv_w":
        return a.reshape(4, 128)
    if name == "final_norm_w":
        return a.reshape(1, D)
    return a


def kernel(x, p, norm_mix_w, w_in, conv_w, conv_b, rg_wa, rg_ba, rg_wx, rg_bx, rg_lambda, hg_lb, hg_norm_w, w_out, ple_norm_w, w_ple_gate, b_ple_gate, w_ple_proj, final_norm_w, loss_target, m_norm_mix_w, m_w_in, m_conv_w, m_conv_b, m_rg_wa, m_rg_ba, m_rg_wx, m_rg_bx, m_rg_lambda, m_hg_lb, m_hg_norm_w, m_w_out, m_ple_norm_w, m_w_ple_gate, m_b_ple_gate, m_w_ple_proj, m_final_norm_w, v_norm_mix_w, v_w_in, v_conv_w, v_conv_b, v_rg_wa, v_rg_ba, v_rg_wx, v_rg_bx, v_rg_lambda, v_hg_lb, v_hg_norm_w, v_w_out, v_ple_norm_w, v_w_ple_gate, v_b_ple_gate, v_w_ple_proj, v_final_norm_w):
    given = dict(locals())
    x2, p2, tgt = x[0], p[0, 0], loss_target[0]
    wa_c, wx_c = _small_view("rg_wa", rg_wa), _small_view("rg_wx", rg_wx)

    w_in_b, w_out_b, w_pg_b, w_pp_b, cw3 = _gather_weights(
        w_in[0], w_out[0], w_ple_gate[0], w_ple_proj[0], conv_w[0])

    proj, u_b = _fwd_inproj(x2, norm_mix_w, w_in_b)
    h, ya = _fwd_rglru(proj, cw3, conv_b, wa_c, rg_ba, wx_c, rg_bx, rg_lambda)
    yb, o, s_prev = _fwd_hgrn2(proj, hg_lb, hg_norm_w)
    dh1, dyab, dw_out, dw_pg, dw_pp, sm_tail = _tail_fwd_bwd(
        x2, p2, tgt, ya, yb, w_out_b.reshape(D, D), w_pg_b.reshape(D, D), w_pp_b,
        ple_norm_w, b_ple_gate, final_norm_w.reshape(1, D))
    d_a, rg_c, sm_a = _bwd_rglru(proj, h, dyab, cw3, conv_b, wa_c, rg_ba, wx_c, rg_bx, rg_lambda)
    d_b, sm_b = _bwd_hgrn2(proj, o, s_prev, dyab, hg_lb, hg_norm_w)
    grad_x, sm_in = _bwd_inproj_dx(x2, dh1, d_a, d_b, w_in_b, norm_mix_w)
    dw_in = _bwd_inproj_dw(u_b, d_a, d_b)

    g_big = [
        _reduce_scatter("rs_w_in", dw_in, 40),
        _reduce_scatter("rs_w_out", dw_out.reshape(NSHARD, D // NSHARD, D), 24),
        _reduce_scatter("rs_w_pg", dw_pg.reshape(NSHARD, D // NSHARD, D), 24),
        _reduce_scatter("rs_w_pp", dw_pp, 16),
    ]
    upd_big = _adam_big(g_big, [given[n][0] for n in _BIG], [given["m_" + n][0] for n in _BIG],
                        [given["v_" + n][0] for n in _BIG])
    vred, rgred = _small_allreduce(sm_in, sm_tail, sm_a, sm_b, rg_c)
    small = _adam_small(vred, rgred,
                        {n: _small_view(n, given[n]) for n in _SMALL_ORDER},
                        {n: _small_view(n, given["m_" + n]) for n in _SMALL_ORDER},
                        {n: _small_view(n, given["v_" + n]) for n in _SMALL_ORDER})

    loss = vred[0, ROW_LOSS, 0]
    outs = [loss, grad_x[None]]
    for ki in range(4):
        for n in _OUT_ORDER:
            if n in _BIG:
                i = _BIG.index(n)
                a = g_big[i] if ki == 0 else upd_big[i][ki - 1]
                outs.append(a[None])
            else:
                outs.append(small[n][ki].reshape(given[n].shape))
    return tuple(outs)
```

```python
import jax
import jax.numpy as jnp
from jax import lax
from jax.experimental import pallas as pl
from jax.experimental.pallas import tpu as pltpu

F32 = jnp.float32
BF16 = jnp.bfloat16
I32 = jnp.int32
MESH = pl.DeviceIdType.MESH
HIGHEST = lax.Precision.HIGHEST

T = 4096
D = 1024
DG = 512
DIN = 3072
PLE = 256
NH = 4
HD = 128
CH = 64
NCHUNK = T // CH
RGB = 64
EPS = 1e-6
RG_C = 8.0
NSHARD = 4
SHW = DIN // NSHARD
NDEV = 8

ADAM_LR = 0.001
ADAM_B1 = 0.9
ADAM_B2 = 0.999
ADAM_EPS = 1e-08
ADAM_WD = 0.01
ADAM_STEP = 10

VMEM_SPEC = pl.BlockSpec(memory_space=pltpu.VMEM)
MIB = 1024 * 1024

VROWS = 16
ROW_NORM_MIX, ROW_FINAL_NORM, ROW_B_PG, ROW_PLE_NORM = 0, 1, 2, 3
ROW_CB_BA, ROW_BX_LAM, ROW_CW01, ROW_CW23, ROW_HG_LB, ROW_HG_NW, ROW_LOSS = 4, 5, 6, 7, 8, 9, 10


def _mm(a, b):
    return jnp.dot(a.astype(BF16), b.astype(BF16), preferred_element_type=F32)


def _mm_nt(a, b):
    return lax.dot_general(a.astype(BF16), b.astype(BF16), (((1,), (1,)), ((), ())),
                           preferred_element_type=F32)


def _mm_tn(a, b):
    return lax.dot_general(a.astype(BF16), b.astype(BF16), (((0,), (0,)), ((), ())),
                           preferred_element_type=F32)


def _mm_exact(a, b):
    return jnp.dot(a, b, precision=HIGHEST, preferred_element_type=F32)


def _sig_pair(x):
    e = jnp.exp(-jnp.abs(x))
    big = 1.0 / (1.0 + e)
    small = e * big
    pos = x >= 0
    return jnp.where(pos, big, small), jnp.where(pos, small, big)


def _sigmoid(x):
    return 1.0 / (1.0 + jnp.exp(-x))


def _rowsum(v):
    return jnp.sum(v, axis=0, keepdims=True)


def _shift_down(cur, prev8, d, rows):
    rolled = pltpu.roll(cur, d, 0)
    head = jnp.where(rows[0:8] < d, pltpu.roll(prev8, d, 0), rolled[0:8])
    return jnp.concatenate([head, rolled[8:]], axis=0)


def _shift_up(cur, next8, d, rows):
    n = cur.shape[0]
    rolled = pltpu.roll(cur, n - d, 0)
    tail = jnp.where(rows[0:8] >= 8 - d, pltpu.roll(next8, 8 - d, 0), rolled[n - 8:n])
    return jnp.concatenate([rolled[0:n - 8], tail], axis=0)


def _roll_in_groups(v, d):
    n, w = v.shape
    return pltpu.roll(v.reshape(n // 8, 8, w), d, 1).reshape(n, w)


def _cparams(sem, vmem_mib):
    return pltpu.CompilerParams(dimension_semantics=sem, vmem_limit_bytes=vmem_mib * MIB)


def _mesh_pos():
    x, y, c = lax.axis_index("x"), lax.axis_index("y"), lax.axis_index("c")
    chips = [(1 - x, y), (x, 1 - y), (1 - x, 1 - y)]
    return x, y, c, chips


def _remote(src, dst, ssem, rsem, dev):
    return pltpu.make_async_remote_copy(src_ref=src, dst_ref=dst, send_sem=ssem, recv_sem=rsem,
                                        device_id=dev, device_id_type=MESH)


def _gather_weights(w_in, w_out, w_pg, w_pp, conv_w):
    shapes = [w_in.shape, w_out.shape, w_pg.shape, w_pp.shape]

    def body(win, wout, wpg, wpp, cw, o_in, o_out, o_pg, o_pp, o_cw, ssem, rsem):
        x, y, c, chips = _mesh_pos()
        kme = 2 * x + y
        sibling = (x, y, 1 - c)
        params = [(win, o_in), (wout, o_out), (wpg, o_pg), (wpp, o_pp)]
        for src, dst in params:
            rows = src.shape[0]
            for r0 in range(0, rows, 128):
                dst[kme, r0:r0 + 128, :] = src[r0:r0 + 128, :].astype(BF16)
        o_cw[kme] = cw[...]

        def halves(dst):
            hrows = dst.shape[1] // 2
            mine = pl.ds(pl.multiple_of(c * hrows, 128), hrows)
            other = pl.ds(pl.multiple_of((1 - c) * hrows, 128), hrows)
            return mine, other

        sends = []
        n = 0
        for _, dst in params:
            mine, _o = halves(dst)
            for px, py in chips:
                cp = _remote(dst.at[kme, mine], dst.at[kme, mine], ssem.at[n], rsem.at[n], (px, py, c))
                cp.start()
                sends.append(cp)
                n += 1
        for px, py in chips:
            cp = _remote(o_cw.at[kme], o_cw.at[kme], ssem.at[n], rsem.at[n], (px, py, c))
            cp.start()
            sends.append(cp)
            n += 1
        n = 0
        m = 15
        for _, dst in params:
            mine, _o = halves(dst)
            for px, py in chips:
                kj = 2 * px + py
                _remote(dst.at[kj, mine], dst.at[kj, mine], ssem.at[n], rsem.at[n], (px, py, c)).wait_recv()
                cp = _remote(dst.at[kj, mine], dst.at[kj, mine], ssem.at[m], rsem.at[m], sibling)
                cp.start()
                sends.append(cp)
                n += 1
                m += 1
        for px, py in chips:
            kj = 2 * px + py
            _remote(o_cw.at[kj], o_cw.at[kj], ssem.at[n], rsem.at[n], (px, py, c)).wait_recv()
            n += 1
        m = 15
        for _, dst in params:
            _mn, other = halves(dst)
            for px, py in chips:
                kj = 2 * px + py
                _remote(dst.at[kj, other], dst.at[kj, other], ssem.at[m], rsem.at[m], sibling).wait_recv()
                m += 1
        for cp in sends:
            cp.wait_send()

    out_shape = [jax.ShapeDtypeStruct((NSHARD,) + s, BF16) for s in shapes]
    out_shape.append(jax.ShapeDtypeStruct((NSHARD,) + conv_w.shape, F32))
    return pl.pallas_call(
        body, name="gather_weights", out_shape=out_shape,
        in_specs=[VMEM_SPEC] * 5, out_specs=[VMEM_SPEC] * 5,
        scratch_shapes=[pltpu.SemaphoreType.DMA((27,)), pltpu.SemaphoreType.DMA((27,))],
        compiler_params=pltpu.CompilerParams(vmem_limit_bytes=40 * MIB),
    )(w_in, w_out, w_pg, w_pp, conv_w)


def _reduce_scatter(name, g_part, vmem_mib):
    _, R, C = g_part.shape
    H = R // 2
    step = 128

    def body(g_ref, g_out, gb, sib_recv, chip_b, ici_recv, ssem, rsem):
        x, y, c, chips = _mesh_pos()
        kme = 2 * x + y
        sibling = (x, y, 1 - c)
        mine0 = pl.multiple_of(c * H, step)
        other0 = pl.multiple_of((1 - c) * H, step)
        other = pl.ds(other0, H)

        first = []
        for k in range(NSHARD):
            for r0 in range(0, H, step):
                gb[k, r0:r0 + step, :] = g_ref[k, pl.ds(other0 + r0, step), :].astype(BF16)
            cp = _remote(gb.at[k], sib_recv.at[k], ssem.at[k], rsem.at[k], sibling)
            cp.start()
            first.append(cp)
        for cp in first:
            cp.wait_recv()

        second = []
        for j, (px, py) in enumerate(chips):
            kj = 2 * px + py
            for r0 in range(0, H, step):
                s = g_ref[kj, pl.ds(mine0 + r0, step), :] + sib_recv[kj, r0:r0 + step, :].astype(F32)
                chip_b[j, r0:r0 + step, :] = s.astype(BF16)
            cp = _remote(chip_b.at[j], ici_recv.at[j], ssem.at[4 + j], rsem.at[4 + j], (px, py, c))
            cp.start()
            second.append(cp)
        for cp in second:
            cp.wait_recv()
        for r0 in range(0, H, step):
            s = g_ref[kme, pl.ds(mine0 + r0, step), :] + sib_recv[kme, r0:r0 + step, :].astype(F32)
            for j in range(3):
                s = s + ici_recv[j, r0:r0 + step, :].astype(F32)
            g_out[pl.ds(mine0 + r0, step), :] = s

        mine = pl.ds(mine0, H)
        last = _remote(g_out.at[mine], g_out.at[mine], ssem.at[7], rsem.at[7], sibling)
        last.start()
        _remote(g_out.at[other], g_out.at[other], ssem.at[7], rsem.at[7], sibling).wait_recv()
        for cp in first + second + [last]:
            cp.wait_send()

    return pl.pallas_call(
        body, name=name, out_shape=jax.ShapeDtypeStruct((R, C), F32),
        in_specs=[VMEM_SPEC], out_specs=VMEM_SPEC,
        scratch_shapes=[
            pltpu.VMEM((NSHARD, H, C), BF16),
            pltpu.VMEM((NSHARD, H, C), BF16),
            pltpu.VMEM((3, H, C), BF16),
            pltpu.VMEM((3, H, C), BF16),
            pltpu.SemaphoreType.DMA((8,)),
            pltpu.SemaphoreType.DMA((8,)),
        ],
        compiler_params=pltpu.CompilerParams(vmem_limit_bytes=vmem_mib * MIB),
    )(g_part)


def _small_allreduce(sm_in, sm_tail, sm_a, sm_b, rg_c):
    PR = DG // NDEV

    def body(in_ref, tail_ref, a_ref, b_ref, rg_ref, v_out, rg_out, vbuf, vrecv, rgrecv, ssem, rsem):
        x, y, c = lax.axis_index("x"), lax.axis_index("y"), lax.axis_index("c")
        me = 4 * x + 2 * y + c

        def pair(ref, r0, r1):
            return jnp.concatenate([ref[r0:r0 + 1, :], ref[r1:r1 + 1, :]], axis=1)

        rows = {
            ROW_NORM_MIX: in_ref[0:1, :], ROW_FINAL_NORM: tail_ref[1:2, :], ROW_B_PG: tail_ref[2:3, :],
            ROW_PLE_NORM: tail_ref[3:4, :], ROW_CB_BA: pair(a_ref, 0, 1), ROW_BX_LAM: pair(a_ref, 2, 3),
            ROW_CW01: pair(a_ref, 4, 5), ROW_CW23: pair(a_ref, 6, 7), ROW_HG_LB: pair(b_ref, 2, 3),
            ROW_HG_NW: jnp.concatenate([b_ref[1:2, :], jnp.zeros((1, DG), F32)], axis=1),
            ROW_LOSS: tail_ref[4:5, :],
        }
        vbuf[...] = jnp.zeros_like(vbuf)
        for r, row in rows.items():
            for j in range(NDEV):
                vbuf[j, r:r + 1, :] = row[:, j * 128:(j + 1) * 128]

        def peer(mask):
            px = x ^ ((mask >> 2) & 1)
            py = y ^ ((mask >> 1) & 1)
            pc = c ^ (mask & 1)
            return (px, py, pc), 4 * px + 2 * py + pc

        def rg_rows(r):
            return pl.ds(pl.multiple_of(r * PR, PR), PR)

        first = []
        for mask in range(1, NDEV):
            dev, r = peer(mask)
            i = mask - 1
            cp = _remote(vbuf.at[r], vrecv.at[i], ssem.at[i], rsem.at[i], dev)
            cp.start()
            first.append(cp)
            cp = _remote(rg_ref.at[rg_rows(r)], rgrecv.at[i], ssem.at[7 + i], rsem.at[7 + i], dev)
            cp.start()
            first.append(cp)
        sv = vbuf[me]
        sr = rg_ref[rg_rows(me), :]
        for i in range(NDEV - 1):
            first[2 * i].wait_recv()
            first[2 * i + 1].wait_recv()
            sv = sv + vrecv[i]
            sr = sr + rgrecv[i]
        v_out[me] = sv
        rg_out[rg_rows(me), :] = sr
        second = []
        for mask in range(1, NDEV):
            dev, r = peer(mask)
            i = mask - 1
            cp = _remote(v_out.at[me], v_out.at[me], ssem.at[14 + i], rsem.at[14 + i], dev)
            cp.start()
            second.append(cp)
            cp = _remote(rg_out.at[rg_rows(me)], rg_out.at[rg_rows(me)], ssem.at[21 + i], rsem.at[21 + i], dev)
            cp.start()
            second.append(cp)
        for mask in range(1, NDEV):
            dev, r = peer(mask)
            i = mask - 1
            _remote(v_out.at[r], v_out.at[r], ssem.at[14 + i], rsem.at[14 + i], dev).wait_recv()
            _remote(rg_out.at[rg_rows(r)], rg_out.at[rg_rows(r)], ssem.at[21 + i], rsem.at[21 + i], dev).wait_recv()
        for cp in first + second:
            cp.wait_send()

    return pl.pallas_call(
        body, name="small_allreduce",
        out_shape=[jax.ShapeDtypeStruct((NDEV, VROWS, 128), F32), jax.ShapeDtypeStruct((DG, 128), F32)],
        in_specs=[VMEM_SPEC] * 5, out_specs=[VMEM_SPEC] * 2,
        scratch_shapes=[pltpu.VMEM((NDEV, VROWS, 128), F32), pltpu.VMEM((NDEV - 1, VROWS, 128), F32),
                        pltpu.VMEM((NDEV - 1, PR, 128), F32),
                        pltpu.SemaphoreType.DMA((28,)), pltpu.SemaphoreType.DMA((28,))],
    )(sm_in, sm_tail, sm_a, sm_b, rg_c)


def _adam_rows(w, g, m, v):
    m2 = ADAM_B1 * m + (1.0 - ADAM_B1) * g
    v2 = ADAM_B2 * v + (1.0 - ADAM_B2) * (g * g)
    m_hat = m2 / (1.0 - ADAM_B1 ** ADAM_STEP)
    v_hat = v2 / (1.0 - ADAM_B2 ** ADAM_STEP)
    delta = -ADAM_LR * (m_hat / (jnp.sqrt(v_hat) + ADAM_EPS) + ADAM_WD * w)
    return delta, m2, v2


def _adam_big(gs, ws, ms, vs):
    n = len(gs)
    steps = 8

    def body(*refs):
        ins, outs = refs[:4 * n], refs[4 * n:]
        for i in range(n):
            g, w, m, v = (r[...] for r in ins[4 * i:4 * i + 4])
            d, m2, v2 = _adam_rows(w, g, m, v)
            outs[3 * i][...] = d
            outs[3 * i + 1][...] = m2
            outs[3 * i + 2][...] = v2

    in_specs, out_specs, out_shape, args = [], [], [], []
    for g, w, m, v in zip(gs, ws, ms, vs):
        r, c = w.shape
        spec = lambda: pl.BlockSpec((r // steps, c), lambda i: (i, 0))
        in_specs += [spec() for _ in range(4)]
        out_specs += [spec() for _ in range(3)]
        out_shape += [jax.ShapeDtypeStruct((r, c), F32)] * 3
        args += [g, w, m, v]
    outs = pl.pallas_call(
        body, name="adam_big", grid=(steps,), in_specs=in_specs, out_specs=out_specs, out_shape=out_shape,
        compiler_params=_cparams(("parallel",), 32),
    )(*args)
    return [tuple(outs[3 * i:3 * i + 3]) for i in range(n)]


_VEC_PARAMS = [
    ("norm_mix_w", ROW_NORM_MIX, 0, D), ("final_norm_w", ROW_FINAL_NORM, 0, D),
    ("b_ple_gate", ROW_B_PG, 0, D), ("ple_norm_w", ROW_PLE_NORM, 0, D),
    ("conv_b", ROW_CB_BA, 0, DG), ("rg_ba", ROW_CB_BA, DG, DG),
    ("rg_bx", ROW_BX_LAM, 0, DG), ("rg_lambda", ROW_BX_LAM, DG, DG),
    ("hg_norm_w", ROW_HG_NW, 0, HD),
]
_SMALL_ORDER = [n for n, _, _, _ in _VEC_PARAMS] + ["hg_lb", "conv_w", "rg_wa", "rg_wx"]


def _adam_small(vred, rgred, ws, ms, vs):
    names = _SMALL_ORDER
    n = len(names)

    def body(vred_ref, rg_ref, *refs):
        w_refs = dict(zip(names, refs[0:n]))
        m_refs = dict(zip(names, refs[n:2 * n]))
        v_refs = dict(zip(names, refs[2 * n:3 * n]))
        outs = refs[3 * n:]
        o_refs = {nm: outs[4 * i:4 * i + 4] for i, nm in enumerate(names)}
        kme = 2 * lax.axis_index("x") + lax.axis_index("y")

        def update(nm, g, idx):
            d, m2, v2 = _adam_rows(w_refs[nm][idx], g, m_refs[nm][idx], v_refs[nm][idx])
            og, od, om, ov = o_refs[nm]
            og[idx] = g
            od[idx] = d
            om[idx] = m2
            ov[idx] = v2

        def packed(row, lane0, width):
            return jnp.concatenate([vred_ref[j, row:row + 1, :] for j in range(lane0 // 128, (lane0 + width) // 128)],
                                   axis=1)

        everything = (slice(None), slice(None))
        for nm, row, lane0, width in _VEC_PARAMS:
            update(nm, packed(row, lane0, width), everything)
        for r in range(2):
            update("hg_lb", packed(ROW_HG_LB, r * DG, DG), (slice(r, r + 1), slice(None)))
        for j in range(4):
            g = vred_ref[(j % 2) * 4 + kme, ROW_CW01 + j // 2:ROW_CW01 + j // 2 + 1, :]
            update("conv_w", g, (slice(j, j + 1), slice(None)))
        for r0 in range(0, DG, 128):
            rs = (slice(r0, r0 + 128), slice(None))
            both = rg_ref[r0:r0 + 128, :]
            update("rg_wa", both[:, 0:RGB], rs)
            update("rg_wx", pltpu.roll(both, RGB, 1)[:, 0:RGB], rs)

    args = [vred, rgred] + [d[nm] for d in (ws, ms, vs) for nm in names]
    out_shape = []
    for nm in names:
        out_shape += [jax.ShapeDtypeStruct(ws[nm].shape, F32)] * 4
    outs = pl.pallas_call(
        body, name="adam_small", out_shape=out_shape,
        in_specs=[VMEM_SPEC] * len(args), out_specs=[VMEM_SPEC] * len(out_shape),
    )(*args)
    return {nm: tuple(outs[4 * i:4 * i + 4]) for i, nm in enumerate(names)}


def _fwd_inproj(x, nw, w_in_b):
    tm = 512

    def body(x_ref, nw_ref, w_ref, proj_ref, u_ref):
        xv = x_ref[...]
        s = lax.rsqrt(jnp.mean(xv * xv, axis=-1, keepdims=True) + EPS)
        u = (xv * s * nw_ref[...]).astype(BF16)
        u_ref[...] = u
        for k in range(NSHARD):
            proj_ref[:, k * SHW:(k + 1) * SHW] = jnp.dot(u, w_ref[k], preferred_element_type=F32)

    return pl.pallas_call(
        body, name="fwd_inproj", grid=(T // tm,),
        in_specs=[pl.BlockSpec((tm, D), lambda i: (i, 0)),
                  pl.BlockSpec((1, D), lambda i: (0, 0)),
                  pl.BlockSpec((NSHARD, D, SHW), lambda i: (0, 0, 0))],
        out_specs=[pl.BlockSpec((tm, DIN), lambda i: (i, 0)),
                   pl.BlockSpec((tm, D), lambda i: (i, 0))],
        out_shape=[jax.ShapeDtypeStruct((T, DIN), F32), jax.ShapeDtypeStruct((T, D), BF16)],
        compiler_params=_cparams(("parallel",), 48),
    )(x, nw, w_in_b)


def _conv_rows(cw_ref):
    return [jnp.concatenate([cw_ref[k, j:j + 1, :] for k in range(NSHARD)], axis=1) for j in range(4)]


def _rg_conv(xa, prev8, cw, cb, rows):
    taps = [_shift_down(xa, prev8, 3, rows), _shift_down(xa, prev8, 2, rows),
            _shift_down(xa, prev8, 1, rows), xa]
    xc = cb
    for j in range(4):
        xc = xc + taps[j] * cw[j]
    return xc, taps


def _block_mask():
    r = lax.broadcasted_iota(I32, (DG, DG), 0)
    c = lax.broadcasted_iota(I32, (DG, DG), 1)
    return (r >> 6) == (c >> 6)


def _dense_from_blocks(wc):
    j = lax.broadcasted_iota(I32, (RGB, DG), 0)
    c = lax.broadcasted_iota(I32, (RGB, DG), 1)
    spread = _mm_exact(wc, ((c & (RGB - 1)) == j).astype(F32))
    return jnp.where(_block_mask(), spread, 0.0)


def _blocks_from_dense(da, dx):
    c = lax.broadcasted_iota(I32, (DG, 128), 0)
    j = lax.broadcasted_iota(I32, (DG, 128), 1)
    hit = (c & (RGB - 1)) == (j & (RGB - 1))
    mask = _block_mask()
    return (_mm_exact(jnp.where(mask, da, 0.0), (hit & (j < RGB)).astype(F32))
            + _mm_exact(jnp.where(mask, dx, 0.0), (hit & (j >= RGB)).astype(F32)))


def _rg_gates(xc, wa, ba, wx, bx, sp, first_row):
    r = _sigmoid(_mm(xc, wa) + ba)
    i = _sigmoid(_mm(xc, wx) + bx)
    log_a = (-RG_C) * r * sp
    a = jnp.exp(log_a)
    a2 = a * a
    one_m_a2 = -jnp.tanh(log_a) * (a2 + 1.0)
    mult = jnp.where(first_row, 1.0, jnp.sqrt(one_m_a2))
    return r, i, a, a2, mult


def _softplus(z):
    return jnp.maximum(z, 0.0) + jnp.log1p(jnp.exp(-jnp.abs(z)))


def _fwd_rglru(proj, cw3, conv_b, wa_c, ba, wx_c, bx, lam):
    tm = 512
    ng = tm // 8

    def body(xa_ref, ga_ref, cw_ref, cb_ref, wa_ref, ba_ref, wx_ref, bx_ref, lam_ref,
             h_ref, ya_ref, a_s, u_s, tail_s, hc_s, wa_s, wx_s):
        i = pl.program_id(0)

        @pl.when(i == 0)
        def _():
            tail_s[...] = jnp.zeros_like(tail_s)
            hc_s[...] = jnp.zeros_like(hc_s)
            wa_s[...] = _dense_from_blocks(wa_ref[...]).astype(BF16)
            wx_s[...] = _dense_from_blocks(wx_ref[...]).astype(BF16)

        rows = lax.broadcasted_iota(I32, (tm, DG), 0)
        xa = xa_ref[...]
        xc, _ = _rg_conv(xa, tail_s[...], _conv_rows(cw_ref), cb_ref[...], rows)
        tail_s[...] = xa[tm - 8:tm, :]
        sp = _softplus(-lam_ref[...])
        first_row = (rows + i * tm) == 0
        r, ig, a, a2, mult = _rg_gates(xc, wa_s[...], ba_ref[...], wx_s[...], bx_ref[...], sp, first_row)
        av, uv = a, mult * (ig * xc)
        rows8 = rows & 7
        for d in (1, 2, 4):
            keep = rows8 >= d
            uv = uv + av * jnp.where(keep, _roll_in_groups(uv, d), 0.0)
            av = av * jnp.where(keep, _roll_in_groups(av, d), 1.0)
        a_s[...] = av
        u_s[...] = uv
        carry = hc_s[0:1, :]
        for g in range(ng):
            sl = slice(g * 8, (g + 1) * 8)
            ab, ub = a_s[sl, :], u_s[sl, :]
            h_ref[sl, :] = ub + ab * carry
            carry = ub[7:8, :] + ab[7:8, :] * carry
        hc_s[0:1, :] = carry
        ga = ga_ref[...]
        sg = _sigmoid(ga)
        ya_ref[...] = (h_ref[...] * (ga * sg)).astype(BF16)

    vec = lambda: pl.BlockSpec((1, DG), lambda i: (0, 0))
    blocks = lambda: pl.BlockSpec((DG, RGB), lambda i: (0, 0))
    return pl.pallas_call(
        body, name="fwd_rglru", grid=(T // tm,),
        in_specs=[pl.BlockSpec((tm, DG), lambda i: (i, 0)),
                  pl.BlockSpec((tm, DG), lambda i: (i, 1)),
                  pl.BlockSpec((NSHARD, 4, 128), lambda i: (0, 0, 0)), vec(),
                  blocks(), vec(), blocks(), vec(), vec()],
        out_specs=[pl.BlockSpec((tm, DG), lambda i: (i, 0)),
                   pl.BlockSpec((tm, DG), lambda i: (i, 0))],
        out_shape=[jax.ShapeDtypeStruct((T, DG), F32), jax.ShapeDtypeStruct((T, DG), BF16)],
        scratch_shapes=[pltpu.VMEM((tm, DG), F32), pltpu.VMEM((tm, DG), F32),
                        pltpu.VMEM((8, DG), F32), pltpu.VMEM((8, DG), F32),
                        pltpu.VMEM((DG, DG), BF16), pltpu.VMEM((DG, DG), BF16)],
        compiler_params=_cparams(("arbitrary",), 48),
    )(proj, proj, cw3, conv_b, wa_c, ba, wx_c, bx, lam)


def _hg_lower_bound(lb_ref):
    return _sig_pair(lb_ref[0:1, :] - lb_ref[1:2, :])


def _hg_gates(fz, lb, one_m_lb):
    sg, sn = _sig_pair(fz)
    f = lb + one_m_lb * sg
    return sg, sn, f, jnp.log(f), one_m_lb * sn


def _tri(lower):
    r = lax.broadcasted_iota(I32, (CH, CH), 0)
    c = lax.broadcasted_iota(I32, (CH, CH), 1)
    return (r >= c) if lower else (r <= c)


def _chunk_terms(qs_c, k_c, logf_c, tril_f):
    b = _mm_exact(tril_f, logf_c)
    bl = _rowsum(logf_c)
    eb = jnp.exp(b)
    enb = jnp.exp(-b)
    ekd = jnp.exp(bl - b)
    return bl, eb, enb, ekd, qs_c * eb, k_c * enb, k_c * ekd


def _fwd_hgrn2(proj, hg_lb, hg_nw):
    tm = 512
    nc = tm // CH

    def body(q_ref, f_ref, v_ref, g_ref, lb_ref, nw_ref, yb_ref, o_ref, sp_ref, st_s):
        i = pl.program_id(0)

        @pl.when(i == 0)
        def _():
            st_s[...] = jnp.zeros_like(st_s)

        lb_all, one_m_lb_all = _hg_lower_bound(lb_ref)
        mask = _tri(True)
        tril_f = mask.astype(F32)
        nw = nw_ref[...]
        for hd in range(NH):
            cols = slice(hd * HD, (hd + 1) * HD)
            lb, one_m_lb = lb_all[:, cols], one_m_lb_all[:, cols]
            _sg, _sn, _f, logf, k = _hg_gates(f_ref[:, cols], lb, one_m_lb)
            q = q_ref[:, cols]
            sq, _ = _sig_pair(q)
            qs = q * sq * (HD ** -0.5)
            v = v_ref[:, cols]
            for ci in range(nc):
                rs = slice(ci * CH, (ci + 1) * CH)
                bl, eb, enb, ekd, qe, ke, kd = _chunk_terms(qs[rs], k[rs], logf[rs], tril_f)
                p = jnp.where(mask, _mm_nt(qe, ke), 0.0)
                st = st_s[hd]
                sp_ref[hd, ci] = st
                o_ref[rs, cols] = _mm(p, v[rs]) + _mm_nt(qe, st)
                st_s[hd] = st * jnp.exp(bl) + _mm_tn(v[rs], kd)
            o = o_ref[:, cols]
            so = lax.rsqrt(jnp.mean(o * o, axis=-1, keepdims=True) + EPS)
            g = g_ref[:, cols]
            sg = _sigmoid(g)
            yb_ref[:, cols] = (o * so * nw * (g * sg)).astype(BF16)

    col = lambda j: pl.BlockSpec((tm, DG), lambda i: (i, j))
    return pl.pallas_call(
        body, name="fwd_hgrn2", grid=(T // tm,),
        in_specs=[col(2), col(3), col(4), col(5),
                  pl.BlockSpec((2, DG), lambda i: (0, 0)),
                  pl.BlockSpec((1, HD), lambda i: (0, 0))],
        out_specs=[pl.BlockSpec((tm, DG), lambda i: (i, 0)),
                   pl.BlockSpec((tm, DG), lambda i: (i, 0)),
                   pl.BlockSpec((NH, nc, HD, HD), lambda i: (0, i, 0, 0))],
        out_shape=[jax.ShapeDtypeStruct((T, DG), BF16), jax.ShapeDtypeStruct((T, DG), F32),
                   jax.ShapeDtypeStruct((NH, NCHUNK, HD, HD), F32)],
        scratch_shapes=[pltpu.VMEM((NH, HD, HD), F32)],
        compiler_params=_cparams(("arbitrary",), 48),
    )(proj, proj, proj, proj, hg_lb, hg_nw)


def _chunk_cumsum(v, rows64):
    for d in (1, 2, 4, 8, 16, 32):
        v = v + jnp.where(rows64 >= d, pltpu.roll(v, d, 0), 0.0)
    return v


def _chunk_rev_cumsum(v, rows64):
    n = v.shape[0]
    for d in (1, 2, 4, 8, 16, 32):
        v = v + jnp.where(rows64 < CH - d, pltpu.roll(v, n - d, 0), 0.0)
    return v


def _hg_recompute(q_ref, f_ref, lb, one_m_lb, rows64, eb_s, enb_s, ekd_s, qe_s, ke_s, kd_s, dec_s):
    nc = q_ref.shape[0] // CH
    sg, sn, f, logf, k = _hg_gates(f_ref[...], lb, one_m_lb)
    q = q_ref[...]
    sq = _sigmoid(q)
    qs = q * sq * (HD ** -0.5)
    b = _chunk_cumsum(logf, rows64)
    for c in range(nc):
        rs = slice(c * CH, (c + 1) * CH)
        b_c = b[rs]
        bl = b_c[CH - 1:CH, :]
        eb, enb, ekd = jnp.exp(b_c), jnp.exp(-b_c), jnp.exp(bl - b_c)
        if eb_s is not None:
            eb_s[rs, :] = eb
            enb_s[rs, :] = enb
            ekd_s[rs, :] = ekd
        qe_s[rs, :] = (qs[rs] * eb).astype(BF16)
        ke_s[rs, :] = (k[rs] * enb).astype(BF16)
        kd_s[rs, :] = (k[rs] * ekd).astype(BF16)
        dec_s[c:c + 1, :] = jnp.exp(bl)
    return sg, sn, f, k, q, sq, qs


def _fwd_hgrn2(proj, hg_lb, hg_nw):
    tm = 512
    nc = tm // CH

    def body(q_ref, f_ref, v_ref, g_ref, lb_ref, nw_ref, yb_ref, o_ref, sp_ref,
             st_s, qe_s, ke_s, kd_s, vb_s, dec_s, p_s, ds_s):
        i = pl.program_id(0)

        @pl.when(i == 0)
        def _():
            st_s[...] = jnp.zeros_like(st_s)

        lb, one_m_lb = _hg_lower_bound(lb_ref)
        rows64 = lax.broadcasted_iota(I32, (tm, DG), 0) & (CH - 1)
        _hg_recompute(q_ref, f_ref, lb, one_m_lb, rows64, None, None, None, qe_s, ke_s, kd_s, dec_s)
        vb_s[...] = v_ref[...].astype(BF16)
        mask = _tri(True)
        items = [(c, hd, slice(c * CH, (c + 1) * CH), slice(hd * HD, (hd + 1) * HD))
                 for c in range(nc) for hd in range(NH)]
        for c, hd, rs, cols in items:
            p_s[c * NH + hd] = jnp.where(mask, _mm_nt(qe_s[rs, cols], ke_s[rs, cols]), 0.0).astype(BF16)
            ds_s[c * NH + hd] = _mm_tn(vb_s[rs, cols], kd_s[rs, cols])
        for c, hd, rs, cols in items:
            st = st_s[hd]
            sp_ref[hd, c] = st
            st_s[hd] = st * dec_s[c:c + 1, cols] + ds_s[c * NH + hd]
        for c, hd, rs, cols in items:
            o_ref[rs, cols] = _mm(p_s[c * NH + hd], vb_s[rs, cols]) + _mm_nt(qe_s[rs, cols], sp_ref[hd, c])
        nw = nw_ref[...]
        for hd in range(NH):
            cols = slice(hd * HD, (hd + 1) * HD)
            o = o_ref[:, cols]
            so = lax.rsqrt(jnp.mean(o * o, axis=-1, keepdims=True) + EPS)
            g = g_ref[:, cols]
            sg = _sigmoid(g)
            yb_ref[:, cols] = (o * so * nw * (g * sg)).astype(BF16)

    col = lambda j: pl.BlockSpec((tm, DG), lambda i: (i, j))
    return pl.pallas_call(
        body, name="fwd_hgrn2", grid=(T // tm,),
        in_specs=[col(2), col(3), col(4), col(5),
                  pl.BlockSpec((2, DG), lambda i: (0, 0)),
                  pl.BlockSpec((1, HD), lambda i: (0, 0))],
        out_specs=[pl.BlockSpec((tm, DG), lambda i: (i, 0)),
                   pl.BlockSpec((tm, DG), lambda i: (i, 0)),
                   pl.BlockSpec((NH, nc, HD, HD), lambda i: (0, i, 0, 0))],
        out_shape=[jax.ShapeDtypeStruct((T, DG), BF16), jax.ShapeDtypeStruct((T, DG), F32),
                   jax.ShapeDtypeStruct((NH, NCHUNK, HD, HD), F32)],
        scratch_shapes=[pltpu.VMEM((NH, HD, HD), F32),
                        pltpu.VMEM((tm, DG), BF16), pltpu.VMEM((tm, DG), BF16), pltpu.VMEM((tm, DG), BF16),
                        pltpu.VMEM((tm, DG), BF16), pltpu.VMEM((nc, DG), F32),
                        pltpu.VMEM((nc * NH, CH, CH), BF16), pltpu.VMEM((nc * NH, HD, HD), F32)],
        compiler_params=_cparams(("arbitrary",), 48),
    )(proj, proj, proj, proj, hg_lb, hg_nw)


def _tail_fwd_bwd(x, p, tgt, ya, yb, w_out_b, w_pg_b, w_pp_b, ple_nw, b_pg, fnw):
    tm = 256
    nt = T // tm

    def body(x_ref, p_ref, t_ref, ya_ref, yb_ref, wo_ref, wg_ref, wp_ref, pw_ref, b_ref, fw_ref,
             dh1_ref, dyab_ref, dwo_ref, dwg_ref, dwp_ref, sm_ref):
        i = pl.program_id(0)

        @pl.when(i == 0)
        def _():
            dwo_ref[...] = jnp.zeros_like(dwo_ref)
            dwg_ref[...] = jnp.zeros_like(dwg_ref)
            dwp_ref[...] = jnp.zeros_like(dwp_ref)
            sm_ref[...] = jnp.zeros_like(sm_ref)

        ya = ya_ref[...]
        yb = yb_ref[...]
        pv = p_ref[...].astype(BF16)
        pw = pw_ref[...]
        fw = fw_ref[...]
        h1 = x_ref[...] + _mm(ya, wo_ref[0:DG, :]) + _mm(yb, wo_ref[DG:D, :])
        s2 = lax.rsqrt(jnp.mean(h1 * h1, axis=-1, keepdims=True) + EPS)
        n2h = h1 * s2
        n2 = (n2h * pw).astype(BF16)
        z = _mm(n2, wg_ref[...]) + b_ref[...]
        gate = _sigmoid(z)
        pp = jnp.concatenate([_mm(pv, wp_ref[k]) for k in range(NSHARD)], axis=1)
        h2 = h1 + gate * pp
        s3 = lax.rsqrt(jnp.mean(h2 * h2, axis=-1, keepdims=True) + EPS)
        hn = h2 * s3
        err = hn * fw - t_ref[...]
        sm_ref[0:1, :] += _rowsum(err * err)
        dy = err * (1.0 / D)
        sm_ref[1:2, :] += _rowsum(dy * hn)
        g3 = dy * fw
        dh2 = s3 * (g3 - hn * jnp.mean(g3 * hn, axis=-1, keepdims=True))
        dpp = (dh2 * gate).astype(BF16)
        dz = dh2 * pp * gate * (1.0 - gate)
        sm_ref[2:3, :] += _rowsum(dz)
        dzb = dz.astype(BF16)
        dwg_ref[...] += _mm_tn(n2, dzb)
        dn2 = _mm_nt(dzb, wg_ref[...])
        for k in range(NSHARD):
            dwp_ref[k] += _mm_tn(pv, dpp[:, k * PLE:(k + 1) * PLE])
        sm_ref[3:4, :] += _rowsum(dn2 * n2h)
        g2 = dn2 * pw
        dh1 = dh2 + s2 * (g2 - n2h * jnp.mean(g2 * n2h, axis=-1, keepdims=True))
        dh1_ref[...] = dh1
        dh1b = dh1.astype(BF16)
        dyab_ref[...] = _mm_nt(dh1b, wo_ref[...])
        dwo_ref[0:DG, :] += _mm_tn(ya, dh1b)
        dwo_ref[DG:D, :] += _mm_tn(yb, dh1b)

        @pl.when(i == nt - 1)
        def _():
            total = jnp.sum(sm_ref[0:1, :], axis=-1, keepdims=True) * (0.5 / D)
            sm_ref[4:5, :] = jnp.broadcast_to(total, (1, D))

    row = lambda w: pl.BlockSpec((tm, w), lambda i: (i, 0))
    const2 = lambda s: pl.BlockSpec(s, lambda i: (0, 0))
    const3 = lambda s: pl.BlockSpec(s, lambda i: (0, 0, 0))
    return pl.pallas_call(
        body, name="tail_fwd_bwd", grid=(nt,),
        in_specs=[row(D), row(PLE), row(D), row(DG), row(DG),
                  const2((D, D)), const2((D, D)), const3((NSHARD, PLE, PLE)),
                  const2((1, D)), const2((1, D)), const2((1, D))],
        out_specs=[row(D), row(D), const2((D, D)), const2((D, D)),
                   const3((NSHARD, PLE, PLE)), const2((8, D))],
        out_shape=[jax.ShapeDtypeStruct((T, D), F32), jax.ShapeDtypeStruct((T, D), F32),
                   jax.ShapeDtypeStruct((D, D), F32), jax.ShapeDtypeStruct((D, D), F32),
                   jax.ShapeDtypeStruct((NSHARD, PLE, PLE), F32), jax.ShapeDtypeStruct((8, D), F32)],
        compiler_params=_cparams(("arbitrary",), 56),
    )(x, p, tgt, ya, yb, w_out_b, w_pg_b, w_pp_b, ple_nw, b_pg, fnw)


def _bwd_rglru(proj, h, dyab, cw3, conv_b, wa_c, ba, wx_c, bx, lam):
    tm = 512
    nt = T // tm
    ng = tm // 8

    def body(xa_ref, ga_ref, xp_ref, h_ref, hp_ref, dya_ref, cw_ref, cb_ref, wa_ref, ba_ref, wx_ref, bx_ref,
             lam_ref, da_ref, rg_ref, sm_ref, a_s, g_s, cg_s, nxt_s, wa_s, wx_s, dwa_s, dwx_s):
        i = pl.program_id(0)
        tile = nt - 1 - i

        @pl.when(i == 0)
        def _():
            dwa_s[...] = jnp.zeros_like(dwa_s)
            dwx_s[...] = jnp.zeros_like(dwx_s)
            sm_ref[...] = jnp.zeros_like(sm_ref)
            cg_s[...] = jnp.zeros_like(cg_s)
            nxt_s[...] = jnp.zeros_like(nxt_s)
            wa_s[...] = _dense_from_blocks(wa_ref[...]).astype(BF16)
            wx_s[...] = _dense_from_blocks(wx_ref[...]).astype(BF16)

        rows = lax.broadcasted_iota(I32, (tm, DG), 0)
        has_prev = tile > 0
        xa = xa_ref[...]
        xprev = jnp.where(has_prev, xp_ref[...], 0.0)
        cw = _conv_rows(cw_ref)
        xc, taps = _rg_conv(xa, xprev, cw, cb_ref[...], rows)
        lam_v = lam_ref[...]
        sp = _softplus(-lam_v)
        first_row = (rows + tile * tm) == 0
        r, ig, a, a2, mult = _rg_gates(xc, wa_s[...], ba_ref[...], wx_s[...], bx_ref[...], sp, first_row)
        hv = h_ref[...]
        hprev = jnp.where(has_prev, hp_ref[...], 0.0)
        h_m1 = _shift_down(hv, hprev, 1, rows)
        ga = ga_ref[...]
        sg = _sigmoid(ga)
        dya = dya_ref[...]
        dga = dya * hv * (sg * (1.0 + ga * (1.0 - sg)))

        av = jnp.where(rows == tm - 1, 1.0, pltpu.roll(a, tm - 1, 0))
        gv = dya * (ga * sg)
        rows8 = rows & 7
        for d in (1, 2, 4):
            keep = rows8 < 8 - d
            gv = gv + av * jnp.where(keep, _roll_in_groups(gv, 8 - d), 0.0)
            av = av * jnp.where(keep, _roll_in_groups(av, 8 - d), 1.0)
        a_s[...] = av
        g_s[...] = gv
        carry = cg_s[0:1, :]
        for g in range(ng - 1, -1, -1):
            sl = slice(g * 8, (g + 1) * 8)
            ab, gb = a_s[sl, :], g_s[sl, :]
            g_s[sl, :] = gb + ab * carry
            carry = gb[0:1, :] + ab[0:1, :] * carry
        cg_s[0:1, :] = a[0:1, :] * carry

        gt = g_s[...]
        da = gt * h_m1
        ixc = ig * xc
        di = gt * mult * xc
        dxc = gt * mult * ig
        dlog_a = da * a + jnp.where(first_row, 0.0, gt * ixc * (-a2 / mult))
        sm_ref[3:4, :] += _rowsum(dlog_a * ((-RG_C) * r))
        dpr_f = dlog_a * ((-RG_C) * sp) * r * (1.0 - r)
        dpi_f = di * ig * (1.0 - ig)
        sm_ref[1:2, :] += _rowsum(dpr_f)
        sm_ref[2:3, :] += _rowsum(dpi_f)
        dpr = dpr_f.astype(BF16)
        dpi = dpi_f.astype(BF16)
        xcb = xc.astype(BF16)
        dwa_s[...] += _mm_tn(xcb, dpr)
        dwx_s[...] += _mm_tn(xcb, dpi)
        dxc = dxc + _mm_nt(dpr, wa_s[...]) + _mm_nt(dpi, wx_s[...])
        sm_ref[0:1, :] += _rowsum(dxc)
        for j in range(4):
            sm_ref[4 + j:5 + j, :] += _rowsum(dxc * taps[j])
        nxt = nxt_s[...]
        dxa = (dxc * cw[3] + _shift_up(dxc, nxt, 1, rows) * cw[2]
               + _shift_up(dxc, nxt, 2, rows) * cw[1] + _shift_up(dxc, nxt, 3, rows) * cw[0])
        nxt_s[...] = dxc[0:8, :]
        da_ref[:, 0:DG] = dxa.astype(BF16)
        da_ref[:, DG:D] = dga.astype(BF16)

        @pl.when(i == nt - 1)
        def _():
            _, s_neg = _sig_pair(lam_v)
            sm_ref[3:4, :] = sm_ref[3:4, :] * (-s_neg)
            rg_ref[...] = _blocks_from_dense(dwa_s[...], dwx_s[...])

    vec = lambda: pl.BlockSpec((1, DG), lambda i: (0, 0))
    blocks = lambda: pl.BlockSpec((DG, RGB), lambda i: (0, 0))
    prev8 = lambda: pl.BlockSpec((8, DG), lambda i: (jnp.maximum((nt - 1 - i) * (tm // 8) - 1, 0), 0))
    return pl.pallas_call(
        body, name="bwd_rglru", grid=(nt,),
        in_specs=[pl.BlockSpec((tm, DG), lambda i: (nt - 1 - i, 0)),
                  pl.BlockSpec((tm, DG), lambda i: (nt - 1 - i, 1)),
                  prev8(),
                  pl.BlockSpec((tm, DG), lambda i: (nt - 1 - i, 0)),
                  prev8(),
                  pl.BlockSpec((tm, DG), lambda i: (nt - 1 - i, 0)),
                  pl.BlockSpec((NSHARD, 4, 128), lambda i: (0, 0, 0)), vec(),
                  blocks(), vec(), blocks(), vec(), vec()],
        out_specs=[pl.BlockSpec((tm, D), lambda i: (nt - 1 - i, 0)),
                   pl.BlockSpec((DG, 128), lambda i: (0, 0)),
                   pl.BlockSpec((8, DG), lambda i: (0, 0))],
        out_shape=[jax.ShapeDtypeStruct((T, D), BF16), jax.ShapeDtypeStruct((DG, 128), F32),
                   jax.ShapeDtypeStruct((8, DG), F32)],
        scratch_shapes=[pltpu.VMEM((tm, DG), F32), pltpu.VMEM((tm, DG), F32),
                        pltpu.VMEM((8, DG), F32), pltpu.VMEM((8, DG), F32),
                        pltpu.VMEM((DG, DG), BF16), pltpu.VMEM((DG, DG), BF16),
                        pltpu.VMEM((DG, DG), F32), pltpu.VMEM((DG, DG), F32)],
        compiler_params=_cparams(("arbitrary",), 56),
    )(proj, proj, proj, h, h, dyab, cw3, conv_b, wa_c, ba, wx_c, bx, lam)


def _bwd_hgrn2(proj, o, s_prev, dyab, hg_lb, hg_nw):
    tm = 512
    nt = T // tm
    nc = tm // CH

    def body(q_ref, f_ref, v_ref, g_ref, o_ref, sp_ref, dy_ref, lb_ref, nw_ref,
             db_ref, sm_ref, dst_s, dq_s, dk_s, dlf_s):
        i = pl.program_id(0)

        @pl.when(i == 0)
        def _():
            sm_ref[...] = jnp.zeros_like(sm_ref)
            dst_s[...] = jnp.zeros_like(dst_s)

        lb_all, one_m_lb_all = _hg_lower_bound(lb_ref)
        mask = _tri(True)
        tril_f = mask.astype(F32)
        triu_f = _tri(False).astype(F32)
        nw = nw_ref[...]
        for hd in range(NH):
            cols = slice(hd * HD, (hd + 1) * HD)
            lb, one_m_lb = lb_all[:, cols], one_m_lb_all[:, cols]
            sg, sn, f, logf, k = _hg_gates(f_ref[:, cols], lb, one_m_lb)
            q = q_ref[:, cols]
            sq, sqn = _sig_pair(q)
            qs = q * sq * (HD ** -0.5)
            v = v_ref[:, cols]
            g = g_ref[:, cols]
            sgg, sggn = _sig_pair(g)
            silu_g = g * sgg
            o = o_ref[:, cols]
            so = lax.rsqrt(jnp.mean(o * o, axis=-1, keepdims=True) + EPS)
            oh = o * so
            dyb = dy_ref[:, cols]
            dgb = dyb * (oh * nw) * (sgg * (1.0 + g * sggn))
            don = dyb * silu_g
            sm_ref[1:2, 0:HD] += _rowsum(don * oh)
            gw = don * nw
            do = so * (gw - oh * jnp.mean(gw * oh, axis=-1, keepdims=True))
            db_ref[:, 3 * DG + hd * HD:3 * DG + (hd + 1) * HD] = dgb.astype(BF16)

            for cc in range(nc):
                ci = nc - 1 - cc
                rs = slice(ci * CH, (ci + 1) * CH)
                bl, eb, enb, ekd, qe, ke, kd = _chunk_terms(qs[rs], k[rs], logf[rs], tril_f)
                p = jnp.where(mask, _mm_nt(qe, ke), 0.0)
                st_prev = sp_ref[hd, ci]
                dst = dst_s[hd]
                do_c = do[rs].astype(BF16)
                v_c = v[rs].astype(BF16)
                qeb = qe.astype(BF16)
                dp = jnp.where(mask, _mm_nt(do_c, v_c), 0.0).astype(BF16)
                dv = _mm_tn(p, do_c) + _mm_nt(kd, dst)
                dqe = _mm(dp, ke) + _mm(do_c, st_prev)
                dke = _mm_tn(dp, qeb)
                dkd = _mm(v_c, dst)
                decay = jnp.exp(bl)
                ddecay = _rowsum(dst * st_prev)
                dst_s[hd] = dst * decay + _mm_tn(do_c, qeb)
                t_kd = dkd * kd
                db = dqe * qe - dke * ke - t_kd
                dbl = _rowsum(t_kd) + ddecay * decay
                dlf_s[rs, :] = _mm_exact(triu_f, db) + dbl
                dq_s[rs, :] = dqe * eb
                dk_s[rs, :] = dke * enb + dkd * ekd
                db_ref[rs, 2 * DG + hd * HD:2 * DG + (hd + 1) * HD] = dv.astype(BF16)

            t = dlf_s[...] / f - dk_s[...]
            db_ref[:, DG + hd * HD:DG + (hd + 1) * HD] = (one_m_lb * sg * sn * t).astype(BF16)
            sm_ref[0:1, cols] += _rowsum(sn * t)
            db_ref[:, hd * HD:(hd + 1) * HD] = (dq_s[...] * (sq * (1.0 + q * (1.0 - sq))) * (HD ** -0.5)).astype(BF16)

        @pl.when(i == nt - 1)
        def _():
            dsm = sm_ref[0:1, :] * (lb_all * one_m_lb_all)
            sm_ref[2:3, :] = dsm
            sm_ref[3:4, :] = -dsm

    col = lambda j: pl.BlockSpec((tm, DG), lambda i: (nt - 1 - i, j))
    return pl.pallas_call(
        body, name="bwd_hgrn2", grid=(nt,),
        in_specs=[col(2), col(3), col(4), col(5),
                  pl.BlockSpec((tm, DG), lambda i: (nt - 1 - i, 0)),
                  pl.BlockSpec((NH, nc, HD, HD), lambda i: (0, nt - 1 - i, 0, 0)),
                  pl.BlockSpec((tm, DG), lambda i: (nt - 1 - i, 1)),
                  pl.BlockSpec((2, DG), lambda i: (0, 0)),
                  pl.BlockSpec((1, HD), lambda i: (0, 0))],
        out_specs=[pl.BlockSpec((tm, 4 * DG), lambda i: (nt - 1 - i, 0)),
                   pl.BlockSpec((8, DG), lambda i: (0, 0))],
        out_shape=[jax.ShapeDtypeStruct((T, 4 * DG), BF16), jax.ShapeDtypeStruct((8, DG), F32)],
        scratch_shapes=[pltpu.VMEM((NH, HD, HD), F32), pltpu.VMEM((tm, HD), F32),
                        pltpu.VMEM((tm, HD), F32), pltpu.VMEM((tm, HD), F32)],
        compiler_params=_cparams(("arbitrary",), 56),
    )(proj, proj, proj, proj, o, s_prev, dyab, hg_lb, hg_nw)


def _bwd_hgrn2(proj, o, s_prev, dyab, hg_lb, hg_nw):
    tm = 512
    nt = T // tm
    nc = tm // CH

    def body(q_ref, f_ref, v_ref, g_ref, o_ref, sp_ref, dy_ref, lb_ref, nw_ref, db_ref, sm_ref,
             dst_s, eb_s, enb_s, ekd_s, qe_s, ke_s, kd_s, vb_s, do_s, dec_s, ddec_s, p_s, dp_s,
             g_s, dsta_s, dva_s, dqe_s, dke_s, dkd_s, dlf_s):
        i = pl.program_id(0)

        @pl.when(i == 0)
        def _():
            sm_ref[...] = jnp.zeros_like(sm_ref)
            dst_s[...] = jnp.zeros_like(dst_s)

        lb, one_m_lb = _hg_lower_bound(lb_ref)
        rows64 = lax.broadcasted_iota(I32, (tm, DG), 0) & (CH - 1)
        sg, sn, f, k, q, sq, qs = _hg_recompute(
            q_ref, f_ref, lb, one_m_lb, rows64, eb_s, enb_s, ekd_s, qe_s, ke_s, kd_s, dec_s)
        vb_s[...] = v_ref[...].astype(BF16)

        nw = nw_ref[...]
        for hd in range(NH):
            cols = slice(hd * HD, (hd + 1) * HD)
            g = g_ref[:, cols]
            sgg = _sigmoid(g)
            o = o_ref[:, cols]
            so = lax.rsqrt(jnp.mean(o * o, axis=-1, keepdims=True) + EPS)
            oh = o * so
            dyb = dy_ref[:, cols]
            db_ref[:, 3 * DG + hd * HD:3 * DG + (hd + 1) * HD] = (
                dyb * (oh * nw) * (sgg * (1.0 + g * (1.0 - sgg)))).astype(BF16)
            don = dyb * (g * sgg)
            sm_ref[1:2, 0:HD] += _rowsum(don * oh)
            gw = don * nw
            do_s[:, cols] = (so * (gw - oh * jnp.mean(gw * oh, axis=-1, keepdims=True))).astype(BF16)

        mask = _tri(True)
        items = [(c, hd, slice(c * CH, (c + 1) * CH), slice(hd * HD, (hd + 1) * HD))
                 for c in range(nc) for hd in range(NH)]
        for c, hd, rs, cols in items:
            p_s[c * NH + hd] = jnp.where(mask, _mm_nt(qe_s[rs, cols], ke_s[rs, cols]), 0.0).astype(BF16)
            dp_s[c * NH + hd] = jnp.where(mask, _mm_nt(do_s[rs, cols], vb_s[rs, cols]), 0.0).astype(BF16)
        for c, hd, rs, cols in items:
            n = c * NH + hd
            dva_s[rs, cols] = _mm_tn(p_s[n], do_s[rs, cols])
            dqe_s[rs, cols] = _mm(dp_s[n], ke_s[rs, cols])
            dke_s[rs, cols] = _mm_tn(dp_s[n], qe_s[rs, cols])
            g_s[n] = _mm_tn(do_s[rs, cols], qe_s[rs, cols])
        for c, hd, rs, cols in reversed(items):
            n = c * NH + hd
            dst = dst_s[hd]
            dsta_s[n] = dst
            dst_s[hd] = dst * dec_s[c:c + 1, cols] + g_s[n]
        for c, hd, rs, cols in items:
            n = c * NH + hd
            dst = dsta_s[n]
            st_prev = sp_ref[hd, c]
            dv = dva_s[rs, cols] + _mm_nt(kd_s[rs, cols], dst)
            db_ref[rs, 2 * DG + hd * HD:2 * DG + (hd + 1) * HD] = dv.astype(BF16)
            dqe_s[rs, cols] += _mm(do_s[rs, cols], st_prev)
            dkd_s[rs, cols] = _mm(vb_s[rs, cols], dst)
            ddec_s[c:c + 1, cols] = _rowsum(dst * st_prev)

        eb, enb, ekd = eb_s[...], enb_s[...], ekd_s[...]
        dqe, dke, dkd = dqe_s[...], dke_s[...], dkd_s[...]
        t_kd = dkd * (k * ekd)
        rc = _chunk_rev_cumsum(dqe * (qs * eb) - dke * (k * enb) - t_kd, rows64)
        for c in range(nc):
            rs = slice(c * CH, (c + 1) * CH)
            dbl = _rowsum(t_kd[rs]) + ddec_s[c:c + 1, :] * dec_s[c:c + 1, :]
            dlf_s[rs, :] = rc[rs] + dbl
        t = dlf_s[...] / f - (dke * enb + dkd * ekd)
        db_ref[:, DG:2 * DG] = (one_m_lb * sg * sn * t).astype(BF16)
        sm_ref[0:1, :] += _rowsum(sn * t)
        db_ref[:, 0:DG] = (dqe * eb * (sq * (1.0 + q * (1.0 - sq))) * (HD ** -0.5)).astype(BF16)

        @pl.when(i == nt - 1)
        def _():
            dsm = sm_ref[0:1, :] * (lb * one_m_lb)
            sm_ref[2:3, :] = dsm
            sm_ref[3:4, :] = -dsm

    col = lambda j: pl.BlockSpec((tm, DG), lambda i: (nt - 1 - i, j))
    big = lambda dt: pltpu.VMEM((tm, DG), dt)
    return pl.pallas_call(
        body, name="bwd_hgrn2", grid=(nt,),
        in_specs=[col(2), col(3), col(4), col(5),
                  pl.BlockSpec((tm, DG), lambda i: (nt - 1 - i, 0)),
                  pl.BlockSpec((NH, nc, HD, HD), lambda i: (0, nt - 1 - i, 0, 0)),
                  pl.BlockSpec((tm, DG), lambda i: (nt - 1 - i, 1)),
                  pl.BlockSpec((2, DG), lambda i: (0, 0)),
                  pl.BlockSpec((1, HD), lambda i: (0, 0))],
        out_specs=[pl.BlockSpec((tm, 4 * DG), lambda i: (nt - 1 - i, 0)),
                   pl.BlockSpec((8, DG), lambda i: (0, 0))],
        out_shape=[jax.ShapeDtypeStruct((T, 4 * DG), BF16), jax.ShapeDtypeStruct((8, DG), F32)],
        scratch_shapes=[pltpu.VMEM((NH, HD, HD), F32),
                        big(F32), big(F32), big(F32),
                        big(BF16), big(BF16), big(BF16), big(BF16), big(BF16),
                        pltpu.VMEM((nc, DG), F32), pltpu.VMEM((nc, DG), F32),
                        pltpu.VMEM((nc * NH, CH, CH), BF16), pltpu.VMEM((nc * NH, CH, CH), BF16),
                        pltpu.VMEM((nc * NH, HD, HD), F32), pltpu.VMEM((nc * NH, HD, HD), F32),
                        big(F32), big(F32), big(F32), big(F32), big(F32)],
        compiler_params=_cparams(("arbitrary",), 56),
    )(proj, proj, proj, proj, o, s_prev, dyab, hg_lb, hg_nw)


def _dproj_pieces(da_ref, db_ref):
    return [
        [(da_ref[:, 0:SHW], 0)],
        [(da_ref[:, SHW:D], 0), (db_ref[:, 0:DG], D - SHW)],
        [(db_ref[:, DG:DG + SHW], 0)],
        [(db_ref[:, DG + SHW:4 * DG], 0)],
    ]


def _bwd_inproj_dx(x, dh1, d_a, d_b, w_in_b, nw):
    tm = 512
    nt = T // tm

    def body(x_ref, dh1_ref, da_ref, db_ref, w_ref, nw_ref, dx_ref, sm_ref):
        i = pl.program_id(0)

        @pl.when(i == 0)
        def _():
            sm_ref[...] = jnp.zeros_like(sm_ref)

        du = None
        for k, pieces in enumerate(_dproj_pieces(da_ref, db_ref)):
            for val, off in pieces:
                t = _mm_nt(val, w_ref[k, :, off:off + val.shape[1]])
                du = t if du is None else du + t
        xv = x_ref[...]
        s = lax.rsqrt(jnp.mean(xv * xv, axis=-1, keepdims=True) + EPS)
        xh = xv * s
        sm_ref[0:1, :] += _rowsum(du * xh)
        g = du * nw_ref[...]
        dx_ref[...] = dh1_ref[...] + s * (g - xh * jnp.mean(g * xh, axis=-1, keepdims=True))

    row = lambda w: pl.BlockSpec((tm, w), lambda i: (i, 0))
    return pl.pallas_call(
        body, name="bwd_inproj_dx", grid=(nt,),
        in_specs=[row(D), row(D), row(D), row(4 * DG),
                  pl.BlockSpec((NSHARD, D, SHW), lambda i: (0, 0, 0)),
                  pl.BlockSpec((1, D), lambda i: (0, 0))],
        out_specs=[row(D), pl.BlockSpec((8, D), lambda i: (0, 0))],
        out_shape=[jax.ShapeDtypeStruct((T, D), F32), jax.ShapeDtypeStruct((8, D), F32)],
        compiler_params=_cparams(("arbitrary",), 56),
    )(x, dh1, d_a, d_b, w_in_b, nw)


def _bwd_inproj_dw(u_b, d_a, d_b):
    tm = 512
    nt = T // tm

    def body(u_ref, da_ref, db_ref, dw_ref):
        i = pl.program_id(0)

        @pl.when(i == 0)
        def _():
            dw_ref[...] = jnp.zeros_like(dw_ref)

        u = u_ref[...]
        for k, pieces in enumerate(_dproj_pieces(da_ref, db_ref)):
            for val, off in pieces:
                dw_ref[k, :, off:off + val.shape[1]] += _mm_tn(u, val)

    row = lambda w: pl.BlockSpec((tm, w), lambda i: (i, 0))
    return pl.pallas_call(
        body, name="bwd_inproj_dw", grid=(nt,),
        in_specs=[row(D), row(D), row(4 * DG)],
        out_specs=pl.BlockSpec((NSHARD, D, SHW), lambda i: (0, 0, 0)),
        out_shape=jax.ShapeDtypeStruct((NSHARD, D, SHW), F32),
        compiler_params=_cparams(("arbitrary",), 56),
    )(u_b, d_a, d_b)


_OUT_ORDER = ["norm_mix_w", "w_in", "conv_w", "conv_b", "rg_wa", "rg_ba", "rg_wx", "rg_bx", "rg_lambda", "hg_lb",
              "hg_norm_w", "w_out", "ple_norm_w", "w_ple_gate", "b_ple_gate", "w_ple_proj", "final_norm_w"]
_BIG = ["w_in", "w_out", "w_ple_gate", "w_ple_proj"]


def _small_view(name, a):
    if name in ("rg_wa", "rg_wx"):
        return a.reshape(DG, RGB)
    if name == "conv_w":
        return a.reshape(4, 128)
    if name == "final_norm_w":
        return a.reshape(1, D)
    return a


def kernel(x, p, norm_mix_w, w_in, conv_w, conv_b, rg_wa, rg_ba, rg_wx, rg_bx, rg_lambda, hg_lb, hg_norm_w, w_out, ple_norm_w, w_ple_gate, b_ple_gate, w_ple_proj, final_norm_w, loss_target, m_norm_mix_w, m_w_in, m_conv_w, m_conv_b, m_rg_wa, m_rg_ba, m_rg_wx, m_rg_bx, m_rg_lambda, m_hg_lb, m_hg_norm_w, m_w_out, m_ple_norm_w, m_w_ple_gate, m_b_ple_gate, m_w_ple_proj, m_final_norm_w, v_norm_mix_w, v_w_in, v_conv_w, v_conv_b, v_rg_wa, v_rg_ba, v_rg_wx, v_rg_bx, v_rg_lambda, v_hg_lb, v_hg_norm_w, v_w_out, v_ple_norm_w, v_w_ple_gate, v_b_ple_gate, v_w_ple_proj, v_final_norm_w):
    given = dict(locals())
    x2, p2, tgt = x[0], p[0, 0], loss_target[0]
    wa_c, wx_c = _small_view("rg_wa", rg_wa), _small_view("rg_wx", rg_wx)

    w_in_b, w_out_b, w_pg_b, w_pp_b, cw3 = _gather_weights(
        w_in[0], w_out[0], w_ple_gate[0], w_ple_proj[0], conv_w[0])

    proj, u_b = _fwd_inproj(x2, norm_mix_w, w_in_b)
    h, ya = _fwd_rglru(proj, cw3, conv_b, wa_c, rg_ba, wx_c, rg_bx, rg_lambda)
    yb, o, s_prev = _fwd_hgrn2(proj, hg_lb, hg_norm_w)
    dh1, dyab, dw_out, dw_pg, dw_pp, sm_tail = _tail_fwd_bwd(
        x2, p2, tgt, ya, yb, w_out_b.reshape(D, D), w_pg_b.reshape(D, D), w_pp_b,
        ple_norm_w, b_ple_gate, final_norm_w.reshape(1, D))
    d_a, rg_c, sm_a = _bwd_rglru(proj, h, dyab, cw3, conv_b, wa_c, rg_ba, wx_c, rg_bx, rg_lambda)
    d_b, sm_b = _bwd_hgrn2(proj, o, s_prev, dyab, hg_lb, hg_norm_w)
    grad_x, sm_in = _bwd_inproj_dx(x2, dh1, d_a, d_b, w_in_b, norm_mix_w)
    dw_in = _bwd_inproj_dw(u_b, d_a, d_b)

    g_big = [
        _reduce_scatter("rs_w_in", dw_in, 40),
        _reduce_scatter("rs_w_out", dw_out.reshape(NSHARD, D // NSHARD, D), 24),
        _reduce_scatter("rs_w_pg", dw_pg.reshape(NSHARD, D // NSHARD, D), 24),
        _reduce_scatter("rs_w_pp", dw_pp, 16),
    ]
    upd_big = _adam_big(g_big, [given[n][0] for n in _BIG], [given["m_" + n][0] for n in _BIG],
                        [given["v_" + n][0] for n in _BIG])
    vred, rgred = _small_allreduce(sm_in, sm_tail, sm_a, sm_b, rg_c)
    small = _adam_small(vred, rgred,
                        {n: _small_view(n, given[n]) for n in _SMALL_ORDER},
                        {n: _small_view(n, given["m_" + n]) for n in _SMALL_ORDER},
                        {n: _small_view(n, given["v_" + n]) for n in _SMALL_ORDER})

    loss = vred[0, ROW_LOSS, 0]
    outs = [loss, grad_x[None]]
    for ki in range(4):
        for n in _OUT_ORDER:
            if n in _BIG:
                i = _BIG.index(n)
                a = g_big[i] if ki == 0 else upd_big[i][ki - 1]
                outs.append(a[None])
            else:
                outs.append(small[n][ki].reshape(given[n].shape))
    return tuple(outs)
```

```python
import jax
import jax.numpy as jnp
from jax import lax
from jax.experimental import pallas as pl
from jax.experimental.pallas import tpu as pltpu

F32 = jnp.float32
BF16 = jnp.bfloat16
I32 = jnp.int32
MESH = pl.DeviceIdType.MESH
HIGHEST = lax.Precision.HIGHEST

T = 4096
D = 1024
DG = 512
DIN = 3072
PLE = 256
NH = 4
HD = 128
CH = 64
NCHUNK = T // CH
RGB = 64
EPS = 1e-6
RG_C = 8.0
NSHARD = 4
SHW = DIN // NSHARD
NDEV = 8

ADAM_LR = 0.001
ADAM_B1 = 0.9
ADAM_B2 = 0.999
ADAM_EPS = 1e-08
ADAM_WD = 0.01
ADAM_STEP = 10

VMEM_SPEC = pl.BlockSpec(memory_space=pltpu.VMEM)
HBM_SPEC = pl.BlockSpec(memory_space=pltpu.HBM)
SEM_SPEC = pl.BlockSpec(memory_space=pltpu.SEMAPHORE)
ANY_SPEC = pl.BlockSpec(memory_space=pl.ANY)
EFFECT = pltpu.SideEffectType.DATAFLOW_SIDE_EFFECTING
MIB = 1024 * 1024

VROWS = 16
ROW_NORM_MIX, ROW_FINAL_NORM, ROW_B_PG, ROW_PLE_NORM = 0, 1, 2, 3
ROW_CB_BA, ROW_BX_LAM, ROW_CW01, ROW_CW23, ROW_HG_LB, ROW_HG_NW, ROW_LOSS = 4, 5, 6, 7, 8, 9, 10


def _mm(a, b):
    return jnp.dot(a.astype(BF16), b.astype(BF16), preferred_element_type=F32)


def _mm_nt(a, b):
    return lax.dot_general(a.astype(BF16), b.astype(BF16), (((1,), (1,)), ((), ())),
                           preferred_element_type=F32)


def _mm_tn(a, b):
    return lax.dot_general(a.astype(BF16), b.astype(BF16), (((0,), (0,)), ((), ())),
                           preferred_element_type=F32)


def _mm_exact(a, b):
    return jnp.dot(a, b, precision=HIGHEST, preferred_element_type=F32)


def _sig_pair(x):
    e = jnp.exp(-jnp.abs(x))
    big = 1.0 / (1.0 + e)
    small = e * big
    pos = x >= 0
    return jnp.where(pos, big, small), jnp.where(pos, small, big)


def _sigmoid(x):
    return 1.0 / (1.0 + jnp.exp(-x))


def _rowsum(v):
    return jnp.sum(v, axis=0, keepdims=True)


def _shift_down(cur, prev8, d, rows):
    rolled = pltpu.roll(cur, d, 0)
    head = jnp.where(rows[0:8] < d, pltpu.roll(prev8, d, 0), rolled[0:8])
    return jnp.concatenate([head, rolled[8:]], axis=0)


def _shift_up(cur, next8, d, rows):
    n = cur.shape[0]
    rolled = pltpu.roll(cur, n - d, 0)
    tail = jnp.where(rows[0:8] >= 8 - d, pltpu.roll(next8, 8 - d, 0), rolled[n - 8:n])
    return jnp.concatenate([rolled[0:n - 8], tail], axis=0)


def _roll_in_groups(v, d):
    n, w = v.shape
    return pltpu.roll(v.reshape(n // 8, 8, w), d, 1).reshape(n, w)


def _cparams(sem, vmem_mib):
    return pltpu.CompilerParams(dimension_semantics=sem, vmem_limit_bytes=vmem_mib * MIB)


def _mesh_pos():
    x, y, c = lax.axis_index("x"), lax.axis_index("y"), lax.axis_index("c")
    chips = [(1 - x, y), (x, 1 - y), (1 - x, 1 - y)]
    return x, y, c, chips


def _remote(src, dst, ssem, rsem, dev):
    return pltpu.make_async_remote_copy(src_ref=src, dst_ref=dst, send_sem=ssem, recv_sem=rsem,
                                        device_id=dev, device_id_type=MESH)


def _gather_w_in(w_in, w_out, w_pg, w_pp, conv_w):
    shapes = [w_in.shape, w_out.shape, w_pg.shape, w_pp.shape]

    def body(win, wout, wpg, wpp, cw, o_in, o_out, o_pg, o_pp, o_cw, ssem, rsem):
        x, y, c, chips = _mesh_pos()
        kme = 2 * x + y
        sibling = (x, y, 1 - c)
        for src, dst in [(win, o_in), (wout, o_out), (wpg, o_pg), (wpp, o_pp)]:
            for r0 in range(0, src.shape[0], 128):
                dst[kme, r0:r0 + 128, :] = src[r0:r0 + 128, :].astype(BF16)
        o_cw[kme] = cw[...]

        hrows = D // 2
        mine = pl.ds(pl.multiple_of(c * hrows, 128), hrows)
        other = pl.ds(pl.multiple_of((1 - c) * hrows, 128), hrows)
        sends = []
        for j, (px, py) in enumerate(chips):
            cp = _remote(o_in.at[kme, mine], o_in.at[kme, mine], ssem.at[j], rsem.at[j], (px, py, c))
            cp.start()
            sends.append(cp)
            cp = _remote(o_cw.at[kme], o_cw.at[kme], ssem.at[3 + j], rsem.at[3 + j], (px, py, c))
            cp.start()
            sends.append(cp)
        for j, (px, py) in enumerate(chips):
            kj = 2 * px + py
            _remote(o_in.at[kj, mine], o_in.at[kj, mine], ssem.at[j], rsem.at[j], (px, py, c)).wait_recv()
            cp = _remote(o_in.at[kj, mine], o_in.at[kj, mine], ssem.at[6 + j], rsem.at[6 + j], sibling)
            cp.start()
            sends.append(cp)
        for j, (px, py) in enumerate(chips):
            kj = 2 * px + py
            _remote(o_cw.at[kj], o_cw.at[kj], ssem.at[3 + j], rsem.at[3 + j], (px, py, c)).wait_recv()
            _remote(o_in.at[kj, other], o_in.at[kj, other], ssem.at[6 + j], rsem.at[6 + j], sibling).wait_recv()
        for cp in sends:
            cp.wait_send()

    out_shape = [jax.ShapeDtypeStruct((NSHARD,) + s, BF16) for s in shapes]
    out_shape.append(jax.ShapeDtypeStruct((NSHARD,) + conv_w.shape, F32))
    return pl.pallas_call(
        body, name="gather_w_in", out_shape=out_shape,
        in_specs=[VMEM_SPEC] * 5, out_specs=[VMEM_SPEC] * 5,
        scratch_shapes=[pltpu.SemaphoreType.DMA((9,)), pltpu.SemaphoreType.DMA((9,))],
        compiler_params=pltpu.CompilerParams(vmem_limit_bytes=40 * MIB),
    )(w_in, w_out, w_pg, w_pp, conv_w)


def _gather_rest_start(lands):
    n = len(lands)

    def body(*refs):
        land_in = refs[0:n]
        ssem, rsem = refs[n], refs[n + 1]
        token = refs[2 * n + 2]
        x, y, c, chips = _mesh_pos()
        kme = 2 * x + y
        for p, land in enumerate(land_in):
            hrows = land.shape[1] // 2
            mine = pl.ds(pl.multiple_of(c * hrows, 128), hrows)
            for px, py in chips:
                for pc in range(2):
                    _remote(land.at[kme, mine], land.at[kme, mine], ssem.at[p], rsem.at[p], (px, py, pc)).start()
        token[...] = jnp.zeros_like(token)

    out_shape = ([pltpu.SemaphoreType.DMA((n,)), pltpu.SemaphoreType.DMA((n,))]
                 + [pltpu.HBM(a.shape, a.dtype) for a in lands] + [jax.ShapeDtypeStruct((8, 128), F32)])
    outs = pl.pallas_call(
        body, name="gather_rest_start", out_shape=out_shape,
        in_specs=[HBM_SPEC] * n, out_specs=[SEM_SPEC, SEM_SPEC] + [HBM_SPEC] * n + [VMEM_SPEC],
        input_output_aliases={i: 2 + i for i in range(n)},
        compiler_params=pltpu.CompilerParams(has_side_effects=EFFECT),
    )(*[pltpu.with_memory_space_constraint(a, pltpu.HBM) for a in lands])
    return outs[0], outs[1], list(outs[2:2 + n]), outs[2 + n]


def _gather_rest_wait(ssem, rsem, lands, after):
    n = len(lands)

    def body(*refs):
        land_in = refs[0:n]
        ssem_ref, rsem_ref = refs[n], refs[n + 1]
        x, y, c = lax.axis_index("x"), lax.axis_index("y"), lax.axis_index("c")
        for p, land in enumerate(land_in):
            three = land.at[pl.ds(0, 3)]
            cp = _remote(three, three, ssem_ref.at[p], rsem_ref.at[p], (x, y, c))
            cp.wait_send()
            cp.wait_recv()

    outs = pl.pallas_call(
        body, name="gather_rest_wait", out_shape=[pltpu.HBM(a.shape, a.dtype) for a in lands],
        in_specs=[HBM_SPEC] * n + [SEM_SPEC, SEM_SPEC, ANY_SPEC], out_specs=[HBM_SPEC] * n,
        input_output_aliases={i: i for i in range(n)},
        compiler_params=pltpu.CompilerParams(has_side_effects=EFFECT),
    )(*lands, ssem, rsem, after)
    return list(outs)


def _rs_start(name, parts, ks, lands):
    n = len(parts)

    def body(*refs):
        part_in, land_in = refs[0:n], refs[n:2 * n]
        ssem, rsem = refs[2 * n], refs[2 * n + 1]
        token = refs[4 * n + 2]
        x, y, c = lax.axis_index("x"), lax.axis_index("y"), lax.axis_index("c")
        kme = 2 * x + y
        me = 4 * x + 2 * y + c
        for p in range(n):
            hrows = land_in[p].shape[1]
            for i, k in enumerate(ks):
                for pc in range(2):
                    @pl.when(jnp.logical_or(kme != k, c != pc))
                    def _():
                        _remote(part_in[p].at[i, pl.ds(pc * hrows, hrows)], land_in[p].at[me],
                                ssem.at[p], rsem.at[p], (k // 2, k % 2, pc)).start()
        token[...] = jnp.zeros_like(token)

    arrays = list(parts) + list(lands)
    out_shape = ([pltpu.SemaphoreType.DMA((n,)), pltpu.SemaphoreType.DMA((n,))]
                 + [pltpu.HBM(a.shape, a.dtype) for a in arrays] + [jax.ShapeDtypeStruct((8, 128), F32)])
    outs = pl.pallas_call(
        body, name=name, out_shape=out_shape,
        in_specs=[HBM_SPEC] * (2 * n), out_specs=[SEM_SPEC, SEM_SPEC] + [HBM_SPEC] * (2 * n) + [VMEM_SPEC],
        input_output_aliases={i: 2 + i for i in range(2 * n)},
        compiler_params=pltpu.CompilerParams(has_side_effects=EFFECT),
    )(*[pltpu.with_memory_space_constraint(a, pltpu.HBM) for a in arrays])
    return outs[0], outs[1], list(outs[2:2 + n]), list(outs[2 + n:2 + 2 * n]), outs[2 + 2 * n]


def _rs_wait(name, ssem, rsem, parts, ks, lands, after):
    n = len(parts)

    def body(*refs):
        part_in, land_in = refs[0:n], refs[n:2 * n]
        ssem_ref, rsem_ref = refs[2 * n], refs[2 * n + 1]
        x, y, c = lax.axis_index("x"), lax.axis_index("y"), lax.axis_index("c")
        kme = 2 * x + y
        for p in range(n):
            piece = land_in[p].at[0]
            for k in ks:
                for pc in range(2):
                    @pl.when(jnp.logical_or(kme != k, c != pc))
                    def _():
                        _remote(piece, piece, ssem_ref.at[p], rsem_ref.at[p], (x, y, c)).wait_send()
            owner = kme == ks[0]
            for k in ks[1:]:
                owner = jnp.logical_or(owner, kme == k)

            @pl.when(owner)
            def _():
                seven = land_in[p].at[pl.ds(0, NDEV - 1)]
                _remote(seven, seven, ssem_ref.at[p], rsem_ref.at[p], (x, y, c)).wait_recv()

    arrays = list(parts) + list(lands)
    outs = pl.pallas_call(
        body, name=name, out_shape=[pltpu.HBM(a.shape, a.dtype) for a in arrays],
        in_specs=[HBM_SPEC] * (2 * n) + [SEM_SPEC, SEM_SPEC, ANY_SPEC], out_specs=[HBM_SPEC] * (2 * n),
        input_output_aliases={i: i for i in range(2 * n)},
        compiler_params=pltpu.CompilerParams(has_side_effects=EFFECT),
    )(*arrays, ssem, rsem, after)
    return list(outs[0:n]), list(outs[n:2 * n])


def _reduce_exchange(parts, lands):
    shapes = [(2 * l.shape[1], l.shape[2]) for l in lands]
    step = 128

    def body(in01, in23, pout, ppg, ppp, l_in, l_out, l_pg, l_pp, g_in, g_out, g_pg, g_pp, ssem, rsem):
        x, y, c = lax.axis_index("x"), lax.axis_index("y"), lax.axis_index("c")
        kme = 2 * x + y
        me = 4 * x + 2 * y + c
        sibling = (x, y, 1 - c)
        sends = []
        for p, (land, gout) in enumerate([(l_in, g_in), (l_out, g_out), (l_pg, g_pg), (l_pp, g_pp)]):
            hrows = land.shape[1]
            mine0 = pl.multiple_of(c * hrows, step)
            for r0 in range(0, hrows, step):
                rs = pl.ds(mine0 + r0, step)
                if p == 0:
                    own = jnp.where(kme >= 2, in23[kme & 1, rs, :], in01[kme & 1, rs, :])
                else:
                    own = (pout, ppg, ppp)[p - 1][kme, rs, :]
                s = jnp.zeros((step, land.shape[2]), F32)
                for j in range(NDEV):
                    s = s + jnp.where(me == j, own, land[j, r0:r0 + step, :]).astype(F32)
                gout[rs, :] = s
            mine = pl.ds(mine0, hrows)
            cp = _remote(gout.at[mine], gout.at[mine], ssem.at[p], rsem.at[p], sibling)
            cp.start()
            sends.append(cp)
        for p, (land, gout) in enumerate([(l_in, g_in), (l_out, g_out), (l_pg, g_pg), (l_pp, g_pp)]):
            hrows = land.shape[1]
            other = pl.ds(pl.multiple_of((1 - c) * hrows, step), hrows)
            _remote(gout.at[other], gout.at[other], ssem.at[p], rsem.at[p], sibling).wait_recv()
        for cp in sends:
            cp.wait_send()

    return pl.pallas_call(
        body, name="reduce_exchange", out_shape=[jax.ShapeDtypeStruct(s, F32) for s in shapes],
        in_specs=[VMEM_SPEC] * 9, out_specs=[VMEM_SPEC] * 4,
        scratch_shapes=[pltpu.SemaphoreType.DMA((4,)), pltpu.SemaphoreType.DMA((4,))],
        compiler_params=pltpu.CompilerParams(vmem_limit_bytes=48 * MIB),
    )(*parts, *lands)


def _small_allreduce(sm_in, sm_tail, sm_a, sm_b, rg_c):
    PR = DG // NDEV

    def body(in_ref, tail_ref, a_ref, b_ref, rg_ref, v_out, rg_out, vbuf, vrecv, rgrecv, ssem, rsem):
        x, y, c = lax.axis_index("x"), lax.axis_index("y"), lax.axis_index("c")
        me = 4 * x + 2 * y + c

        def pair(ref, r0, r1):
            return jnp.concatenate([ref[r0:r0 + 1, :], ref[r1:r1 + 1, :]], axis=1)

        rows = {
            ROW_NORM_MIX: in_ref[0:1, :], ROW_FINAL_NORM: tail_ref[1:2, :], ROW_B_PG: tail_ref[2:3, :],
            ROW_PLE_NORM: tail_ref[3:4, :], ROW_CB_BA: pair(a_ref, 0, 1), ROW_BX_LAM: pair(a_ref, 2, 3),
            ROW_CW01: pair(a_ref, 4, 5), ROW_CW23: pair(a_ref, 6, 7), ROW_HG_LB: pair(b_ref, 2, 3),
            ROW_HG_NW: jnp.concatenate([b_ref[1:2, :], jnp.zeros((1, DG), F32)], axis=1),
            ROW_LOSS: tail_ref[4:5, :],
        }
        vbuf[...] = jnp.zeros_like(vbuf)
        for r, row in rows.items():
            for j in range(NDEV):
                vbuf[j, r:r + 1, :] = row[:, j * 128:(j + 1) * 128]

        def peer(mask):
            px = x ^ ((mask >> 2) & 1)
            py = y ^ ((mask >> 1) & 1)
            pc = c ^ (mask & 1)
            return (px, py, pc), 4 * px + 2 * py + pc

        def rg_rows(r):
            return pl.ds(pl.multiple_of(r * PR, PR), PR)

        first = []
        for mask in range(1, NDEV):
            dev, r = peer(mask)
            i = mask - 1
            cp = _remote(vbuf.at[r], vrecv.at[i], ssem.at[i], rsem.at[i], dev)
            cp.start()
            first.append(cp)
            cp = _remote(rg_ref.at[rg_rows(r)], rgrecv.at[i], ssem.at[7 + i], rsem.at[7 + i], dev)
            cp.start()
            first.append(cp)
        sv = vbuf[me]
        sr = rg_ref[rg_rows(me), :]
        for i in range(NDEV - 1):
            first[2 * i].wait_recv()
            first[2 * i + 1].wait_recv()
            sv = sv + vrecv[i]
            sr = sr + rgrecv[i]
        v_out[me] = sv
        rg_out[rg_rows(me), :] = sr
        second = []
        for mask in range(1, NDEV):
            dev, r = peer(mask)
            i = mask - 1
            cp = _remote(v_out.at[me], v_out.at[me], ssem.at[14 + i], rsem.at[14 + i], dev)
            cp.start()
            second.append(cp)
            cp = _remote(rg_out.at[rg_rows(me)], rg_out.at[rg_rows(me)], ssem.at[21 + i], rsem.at[21 + i], dev)
            cp.start()
            second.append(cp)
        for mask in range(1, NDEV):
            dev, r = peer(mask)
            i = mask - 1
            _remote(v_out.at[r], v_out.at[r], ssem.at[14 + i], rsem.at[14 + i], dev).wait_recv()
            _remote(rg_out.at[rg_rows(r)], rg_out.at[rg_rows(r)], ssem.at[21 + i], rsem.at[21 + i], dev).wait_recv()
        for cp in first + second:
            cp.wait_send()

    return pl.pallas_call(
        body, name="small_allreduce",
        out_shape=[jax.ShapeDtypeStruct((NDEV, VROWS, 128), F32), jax.ShapeDtypeStruct((DG, 128), F32)],
        in_specs=[VMEM_SPEC] * 5, out_specs=[VMEM_SPEC] * 2,
        scratch_shapes=[pltpu.VMEM((NDEV, VROWS, 128), F32), pltpu.VMEM((NDEV - 1, VROWS, 128), F32),
                        pltpu.VMEM((NDEV - 1, PR, 128), F32),
                        pltpu.SemaphoreType.DMA((28,)), pltpu.SemaphoreType.DMA((28,))],
    )(sm_in, sm_tail, sm_a, sm_b, rg_c)


def _adam_rows(w, g, m, v):
    m2 = ADAM_B1 * m + (1.0 - ADAM_B1) * g
    v2 = ADAM_B2 * v + (1.0 - ADAM_B2) * (g * g)
    m_hat = m2 / (1.0 - ADAM_B1 ** ADAM_STEP)
    v_hat = v2 / (1.0 - ADAM_B2 ** ADAM_STEP)
    delta = -ADAM_LR * (m_hat / (jnp.sqrt(v_hat) + ADAM_EPS) + ADAM_WD * w)
    return delta, m2, v2


def _adam_big(gs, ws, ms, vs):
    n = len(gs)
    steps = 8

    def body(*refs):
        ins, outs = refs[:4 * n], refs[4 * n:]
        for i in range(n):
            g, w, m, v = (r[...] for r in ins[4 * i:4 * i + 4])
            d, m2, v2 = _adam_rows(w, g, m, v)
            outs[3 * i][...] = d
            outs[3 * i + 1][...] = m2
            outs[3 * i + 2][...] = v2

    in_specs, out_specs, out_shape, args = [], [], [], []
    for g, w, m, v in zip(gs, ws, ms, vs):
        r, c = w.shape
        spec = lambda: pl.BlockSpec((r // steps, c), lambda i: (i, 0))
        in_specs += [spec() for _ in range(4)]
        out_specs += [spec() for _ in range(3)]
        out_shape += [jax.ShapeDtypeStruct((r, c), F32)] * 3
        args += [g, w, m, v]
    outs = pl.pallas_call(
        body, name="adam_big", grid=(steps,), in_specs=in_specs, out_specs=out_specs, out_shape=out_shape,
        compiler_params=_cparams(("parallel",), 32),
    )(*args)
    return [tuple(outs[3 * i:3 * i + 3]) for i in range(n)]


_VEC_PARAMS = [
    ("norm_mix_w", ROW_NORM_MIX, 0, D), ("final_norm_w", ROW_FINAL_NORM, 0, D),
    ("b_ple_gate", ROW_B_PG, 0, D), ("ple_norm_w", ROW_PLE_NORM, 0, D),
    ("conv_b", ROW_CB_BA, 0, DG), ("rg_ba", ROW_CB_BA, DG, DG),
    ("rg_bx", ROW_BX_LAM, 0, DG), ("rg_lambda", ROW_BX_LAM, DG, DG),
    ("hg_norm_w", ROW_HG_NW, 0, HD),
]
_SMALL_ORDER = [n for n, _, _, _ in _VEC_PARAMS] + ["hg_lb", "conv_w", "rg_wa", "rg_wx"]


def _adam_small(vred, rgred, ws, ms, vs):
    names = _SMALL_ORDER
    n = len(names)

    def body(vred_ref, rg_ref, *refs):
        w_refs = dict(zip(names, refs[0:n]))
        m_refs = dict(zip(names, refs[n:2 * n]))
        v_refs = dict(zip(names, refs[2 * n:3 * n]))
        outs = refs[3 * n:]
        o_refs = {nm: outs[4 * i:4 * i + 4] for i, nm in enumerate(names)}
        kme = 2 * lax.axis_index("x") + lax.axis_index("y")

        def update(nm, g, idx):
            d, m2, v2 = _adam_rows(w_refs[nm][idx], g, m_refs[nm][idx], v_refs[nm][idx])
            og, od, om, ov = o_refs[nm]
            og[idx] = g
            od[idx] = d
            om[idx] = m2
            ov[idx] = v2

        def packed(row, lane0, width):
            return jnp.concatenate([vred_ref[j, row:row + 1, :] for j in range(lane0 // 128, (lane0 + width) // 128)],
                                   axis=1)

        everything = (slice(None), slice(None))
        for nm, row, lane0, width in _VEC_PARAMS:
            update(nm, packed(row, lane0, width), everything)
        for r in range(2):
            update("hg_lb", packed(ROW_HG_LB, r * DG, DG), (slice(r, r + 1), slice(None)))
        for j in range(4):
            g = vred_ref[(j % 2) * 4 + kme, ROW_CW01 + j // 2:ROW_CW01 + j // 2 + 1, :]
            update("conv_w", g, (slice(j, j + 1), slice(None)))
        for r0 in range(0, DG, 128):
            rs = (slice(r0, r0 + 128), slice(None))
            both = rg_ref[r0:r0 + 128, :]
            update("rg_wa", both[:, 0:RGB], rs)
            update("rg_wx", pltpu.roll(both, RGB, 1)[:, 0:RGB], rs)

    args = [vred, rgred] + [d[nm] for d in (ws, ms, vs) for nm in names]
    out_shape = []
    for nm in names:
        out_shape += [jax.ShapeDtypeStruct(ws[nm].shape, F32)] * 4
    outs = pl.pallas_call(
        body, name="adam_small", out_shape=out_shape,
        in_specs=[VMEM_SPEC] * len(args), out_specs=[VMEM_SPEC] * len(out_shape),
    )(*args)
    return {nm: tuple(outs[4 * i:4 * i + 4]) for i, nm in enumerate(names)}


def _fwd_inproj(x, nw, w_in_b, dep):
    tm = 512

    def body(x_ref, nw_ref, w_ref, dep_ref, proj_ref, u_ref):
        xv = x_ref[...]
        s = lax.rsqrt(jnp.mean(xv * xv, axis=-1, keepdims=True) + EPS)
        u = (xv * s * nw_ref[...]).astype(BF16)
        u_ref[...] = u
        for k in range(NSHARD):
            proj_ref[:, k * SHW:(k + 1) * SHW] = jnp.dot(u, w_ref[k], preferred_element_type=F32)

    return pl.pallas_call(
        body, name="fwd_inproj", grid=(T // tm,),
        in_specs=[pl.BlockSpec((tm, D), lambda i: (i, 0)),
                  pl.BlockSpec((1, D), lambda i: (0, 0)),
                  pl.BlockSpec((NSHARD, D, SHW), lambda i: (0, 0, 0)), ANY_SPEC],
        out_specs=[pl.BlockSpec((tm, DIN), lambda i: (i, 0)),
                   pl.BlockSpec((tm, D), lambda i: (i, 0))],
        out_shape=[jax.ShapeDtypeStruct((T, DIN), F32), jax.ShapeDtypeStruct((T, D), BF16)],
        compiler_params=_cparams(("parallel",), 48),
    )(x, nw, w_in_b, dep)


def _conv_rows(cw_ref):
    return [jnp.concatenate([cw_ref[k, j:j + 1, :] for k in range(NSHARD)], axis=1) for j in range(4)]


def _rg_conv(xa, prev8, cw, cb, rows):
    taps = [_shift_down(xa, prev8, 3, rows), _shift_down(xa, prev8, 2, rows),
            _shift_down(xa, prev8, 1, rows), xa]
    xc = cb
    for j in range(4):
        xc = xc + taps[j] * cw[j]
    return xc, taps


def _block_mask():
    r = lax.broadcasted_iota(I32, (DG, DG), 0)
    c = lax.broadcasted_iota(I32, (DG, DG), 1)
    return (r >> 6) == (c >> 6)


def _dense_from_blocks(wc):
    j = lax.broadcasted_iota(I32, (RGB, DG), 0)
    c = lax.broadcasted_iota(I32, (RGB, DG), 1)
    spread = _mm_exact(wc, ((c & (RGB - 1)) == j).astype(F32))
    return jnp.where(_block_mask(), spread, 0.0)


def _blocks_from_dense(da, dx):
    c = lax.broadcasted_iota(I32, (DG, 128), 0)
    j = lax.broadcasted_iota(I32, (DG, 128), 1)
    hit = (c & (RGB - 1)) == (j & (RGB - 1))
    mask = _block_mask()
    return (_mm_exact(jnp.where(mask, da, 0.0), (hit & (j < RGB)).astype(F32))
            + _mm_exact(jnp.where(mask, dx, 0.0), (hit & (j >= RGB)).astype(F32)))


def _rg_gates(xc, wa, ba, wx, bx, sp, first_row):
    r = _sigmoid(_mm(xc, wa) + ba)
    i = _sigmoid(_mm(xc, wx) + bx)
    log_a = (-RG_C) * r * sp
    a = jnp.exp(log_a)
    a2 = a * a
    one_m_a2 = -jnp.tanh(log_a) * (a2 + 1.0)
    mult = jnp.where(first_row, 1.0, jnp.sqrt(one_m_a2))
    return r, i, a, a2, mult


def _softplus(z):
    return jnp.maximum(z, 0.0) + jnp.log1p(jnp.exp(-jnp.abs(z)))


def _fwd_rglru(proj, cw3, conv_b, wa_c, ba, wx_c, bx, lam):
    tm = 512
    ng = tm // 8

    def body(xa_ref, ga_ref, cw_ref, cb_ref, wa_ref, ba_ref, wx_ref, bx_ref, lam_ref,
             h_ref, ya_ref, a_s, u_s, tail_s, hc_s, wa_s, wx_s):
        i = pl.program_id(0)

        @pl.when(i == 0)
        def _():
            tail_s[...] = jnp.zeros_like(tail_s)
            hc_s[...] = jnp.zeros_like(hc_s)
            wa_s[...] = _dense_from_blocks(wa_ref[...]).astype(BF16)
            wx_s[...] = _dense_from_blocks(wx_ref[...]).astype(BF16)

        rows = lax.broadcasted_iota(I32, (tm, DG), 0)
        xa = xa_ref[...]
        xc, _ = _rg_conv(xa, tail_s[...], _conv_rows(cw_ref), cb_ref[...], rows)
        tail_s[...] = xa[tm - 8:tm, :]
        sp = _softplus(-lam_ref[...])
        first_row = (rows + i * tm) == 0
        r, ig, a, a2, mult = _rg_gates(xc, wa_s[...], ba_ref[...], wx_s[...], bx_ref[...], sp, first_row)
        av, uv = a, mult * (ig * xc)
        rows8 = rows & 7
        for d in (1, 2, 4):
            keep = rows8 >= d
            uv = uv + av * jnp.where(keep, _roll_in_groups(uv, d), 0.0)
            av = av * jnp.where(keep, _roll_in_groups(av, d), 1.0)
        a_s[...] = av
        u_s[...] = uv
        carry = hc_s[0:1, :]
        for g in range(ng):
            sl = slice(g * 8, (g + 1) * 8)
            ab, ub = a_s[sl, :], u_s[sl, :]
            h_ref[sl, :] = ub + ab * carry
            carry = ub[7:8, :] + ab[7:8, :] * carry
        hc_s[0:1, :] = carry
        ga = ga_ref[...]
        sg = _sigmoid(ga)
        ya_ref[...] = (h_ref[...] * (ga * sg)).astype(BF16)

    vec = lambda: pl.BlockSpec((1, DG), lambda i: (0, 0))
    blocks = lambda: pl.BlockSpec((DG, RGB), lambda i: (0, 0))
    return pl.pallas_call(
        body, name="fwd_rglru", grid=(T // tm,),
        in_specs=[pl.BlockSpec((tm, DG), lambda i: (i, 0)),
                  pl.BlockSpec((tm, DG), lambda i: (i, 1)),
                  pl.BlockSpec((NSHARD, 4, 128), lambda i: (0, 0, 0)), vec(),
                  blocks(), vec(), blocks(), vec(), vec()],
        out_specs=[pl.BlockSpec((tm, DG), lambda i: (i, 0)),
                   pl.BlockSpec((tm, DG), lambda i: (i, 0))],
        out_shape=[jax.ShapeDtypeStruct((T, DG), F32), jax.ShapeDtypeStruct((T, DG), BF16)],
        scratch_shapes=[pltpu.VMEM((tm, DG), F32), pltpu.VMEM((tm, DG), F32),
                        pltpu.VMEM((8, DG), F32), pltpu.VMEM((8, DG), F32),
                        pltpu.VMEM((DG, DG), BF16), pltpu.VMEM((DG, DG), BF16)],
        compiler_params=_cparams(("arbitrary",), 48),
    )(proj, proj, cw3, conv_b, wa_c, ba, wx_c, bx, lam)


def _hg_lower_bound(lb_ref):
    return _sig_pair(lb_ref[0:1, :] - lb_ref[1:2, :])


def _hg_gates(fz, lb, one_m_lb):
    sg, sn = _sig_pair(fz)
    f = lb + one_m_lb * sg
    return sg, sn, f, jnp.log(f), one_m_lb * sn


def _tri(lower):
    r = lax.broadcasted_iota(I32, (CH, CH), 0)
    c = lax.broadcasted_iota(I32, (CH, CH), 1)
    return (r >= c) if lower else (r <= c)


def _chunk_cumsum(v, rows64):
    for d in (1, 2, 4, 8, 16, 32):
        v = v + jnp.where(rows64 >= d, pltpu.roll(v, d, 0), 0.0)
    return v


def _chunk_rev_cumsum(v, rows64):
    n = v.shape[0]
    for d in (1, 2, 4, 8, 16, 32):
        v = v + jnp.where(rows64 < CH - d, pltpu.roll(v, n - d, 0), 0.0)
    return v


def _hg_recompute(q_ref, f_ref, lb, one_m_lb, rows64, eb_s, enb_s, ekd_s, qe_s, ke_s, kd_s, dec_s):
    nc = q_ref.shape[0] // CH
    sg, sn, f, logf, k = _hg_gates(f_ref[...], lb, one_m_lb)
    q = q_ref[...]
    sq = _sigmoid(q)
    qs = q * sq * (HD ** -0.5)
    b = _chunk_cumsum(logf, rows64)
    for c in range(nc):
        rs = slice(c * CH, (c + 1) * CH)
        b_c = b[rs]
        bl = b_c[CH - 1:CH, :]
        eb, enb, ekd = jnp.exp(b_c), jnp.exp(-b_c), jnp.exp(bl - b_c)
        if eb_s is not None:
            eb_s[rs, :] = eb
            enb_s[rs, :] = enb
            ekd_s[rs, :] = ekd
        qe_s[rs, :] = (qs[rs] * eb).astype(BF16)
        ke_s[rs, :] = (k[rs] * enb).astype(BF16)
        kd_s[rs, :] = (k[rs] * ekd).astype(BF16)
        dec_s[c:c + 1, :] = jnp.exp(bl)
    return sg, sn, f, k, q, sq, qs


def _fwd_hgrn2(proj, hg_lb, hg_nw):
    tm = 512
    nc = tm // CH

    def body(q_ref, f_ref, v_ref, g_ref, lb_ref, nw_ref, yb_ref, o_ref, sp_ref,
             st_s, qe_s, ke_s, kd_s, vb_s, dec_s, p_s, ds_s):
        i = pl.program_id(0)

        @pl.when(i == 0)
        def _():
            st_s[...] = jnp.zeros_like(st_s)

        lb, one_m_lb = _hg_lower_bound(lb_ref)
        rows64 = lax.broadcasted_iota(I32, (tm, DG), 0) & (CH - 1)
        _hg_recompute(q_ref, f_ref, lb, one_m_lb, rows64, None, None, None, qe_s, ke_s, kd_s, dec_s)
        vb_s[...] = v_ref[...].astype(BF16)
        mask = _tri(True)
        items = [(c, hd, slice(c * CH, (c + 1) * CH), slice(hd * HD, (hd + 1) * HD))
                 for c in range(nc) for hd in range(NH)]
        for c, hd, rs, cols in items:
            p_s[c * NH + hd] = jnp.where(mask, _mm_nt(qe_s[rs, cols], ke_s[rs, cols]), 0.0).astype(BF16)
            ds_s[c * NH + hd] = _mm_tn(vb_s[rs, cols], kd_s[rs, cols])
        for c, hd, rs, cols in items:
            st = st_s[hd]
            sp_ref[hd, c] = st
            st_s[hd] = st * dec_s[c:c + 1, cols] + ds_s[c * NH + hd]
        for c, hd, rs, cols in items:
            o_ref[rs, cols] = _mm(p_s[c * NH + hd], vb_s[rs, cols]) + _mm_nt(qe_s[rs, cols], sp_ref[hd, c])
        nw = nw_ref[...]
        for hd in range(NH):
            cols = slice(hd * HD, (hd + 1) * HD)
            o = o_ref[:, cols]
            so = lax.rsqrt(jnp.mean(o * o, axis=-1, keepdims=True) + EPS)
            g = g_ref[:, cols]
            sg = _sigmoid(g)
            yb_ref[:, cols] = (o * so * nw * (g * sg)).astype(BF16)

    col = lambda j: pl.BlockSpec((tm, DG), lambda i: (i, j))
    return pl.pallas_call(
        body, name="fwd_hgrn2", grid=(T // tm,),
        in_specs=[col(2), col(3), col(4), col(5),
                  pl.BlockSpec((2, DG), lambda i: (0, 0)),
                  pl.BlockSpec((1, HD), lambda i: (0, 0))],
        out_specs=[pl.BlockSpec((tm, DG), lambda i: (i, 0)),
                   pl.BlockSpec((tm, DG), lambda i: (i, 0)),
                   pl.BlockSpec((NH, nc, HD, HD), lambda i: (0, i, 0, 0))],
        out_shape=[jax.ShapeDtypeStruct((T, DG), BF16), jax.ShapeDtypeStruct((T, DG), F32),
                   jax.ShapeDtypeStruct((NH, NCHUNK, HD, HD), F32)],
        scratch_shapes=[pltpu.VMEM((NH, HD, HD), F32),
                        pltpu.VMEM((tm, DG), BF16), pltpu.VMEM((tm, DG), BF16), pltpu.VMEM((tm, DG), BF16),
                        pltpu.VMEM((tm, DG), BF16), pltpu.VMEM((nc, DG), F32),
                        pltpu.VMEM((nc * NH, CH, CH), BF16), pltpu.VMEM((nc * NH, HD, HD), F32)],
        compiler_params=_cparams(("arbitrary",), 48),
    )(proj, proj, proj, proj, hg_lb, hg_nw)


def _tail_fwd_bwd(x, p, tgt, ya, yb, w_out_b, w_pg_b, w_pp_b, ple_nw, b_pg, fnw):
    tm = 256
    nt = T // tm
    QR = D // NSHARD

    def body(x_ref, p_ref, t_ref, ya_ref, yb_ref, wo_ref, wg_ref, wp_ref, pw_ref, b_ref, fw_ref,
             dh1_ref, dyab_ref, dwo_ref, dwg_ref, dwp_ref, sm_ref, dwo_s, dwg_s, dwp_s):
        i = pl.program_id(0)

        @pl.when(i == 0)
        def _():
            dwo_s[...] = jnp.zeros_like(dwo_s)
            dwg_s[...] = jnp.zeros_like(dwg_s)
            dwp_s[...] = jnp.zeros_like(dwp_s)
            sm_ref[...] = jnp.zeros_like(sm_ref)

        ya = ya_ref[...]
        yb = yb_ref[...]
        pv = p_ref[...].astype(BF16)
        pw = pw_ref[...]
        fw = fw_ref[...]
        h1 = x_ref[...] + _mm(ya, wo_ref[0:DG, :]) + _mm(yb, wo_ref[DG:D, :])
        s2 = lax.rsqrt(jnp.mean(h1 * h1, axis=-1, keepdims=True) + EPS)
        n2h = h1 * s2
        n2 = (n2h * pw).astype(BF16)
        z = _mm(n2, wg_ref[...]) + b_ref[...]
        gate = _sigmoid(z)
        pp = jnp.concatenate([_mm(pv, wp_ref[k]) for k in range(NSHARD)], axis=1)
        h2 = h1 + gate * pp
        s3 = lax.rsqrt(jnp.mean(h2 * h2, axis=-1, keepdims=True) + EPS)
        hn = h2 * s3
        err = hn * fw - t_ref[...]
        sm_ref[0:1, :] += _rowsum(err * err)
        dy = err * (1.0 / D)
        sm_ref[1:2, :] += _rowsum(dy * hn)
        g3 = dy * fw
        dh2 = s3 * (g3 - hn * jnp.mean(g3 * hn, axis=-1, keepdims=True))
        dpp = (dh2 * gate).astype(BF16)
        dz = dh2 * pp * gate * (1.0 - gate)
        sm_ref[2:3, :] += _rowsum(dz)
        dzb = dz.astype(BF16)
        dwg_s[...] += _mm_tn(n2, dzb)
        dn2 = _mm_nt(dzb, wg_ref[...])
        for k in range(NSHARD):
            dwp_s[k] += _mm_tn(pv, dpp[:, k * PLE:(k + 1) * PLE])
        sm_ref[3:4, :] += _rowsum(dn2 * n2h)
        g2 = dn2 * pw
        dh1 = dh2 + s2 * (g2 - n2h * jnp.mean(g2 * n2h, axis=-1, keepdims=True))
        dh1_ref[...] = dh1
        dh1b = dh1.astype(BF16)
        dyab_ref[...] = _mm_nt(dh1b, wo_ref[...])
        dwo_s[0:DG, :] += _mm_tn(ya, dh1b)
        dwo_s[DG:D, :] += _mm_tn(yb, dh1b)

        @pl.when(i == nt - 1)
        def _():
            total = jnp.sum(sm_ref[0:1, :], axis=-1, keepdims=True) * (0.5 / D)
            sm_ref[4:5, :] = jnp.broadcast_to(total, (1, D))
            for k in range(NSHARD):
                dwo_ref[k] = dwo_s[k * QR:(k + 1) * QR, :].astype(BF16)
                dwg_ref[k] = dwg_s[k * QR:(k + 1) * QR, :].astype(BF16)
                dwp_ref[k] = dwp_s[k].astype(BF16)

    row = lambda w: pl.BlockSpec((tm, w), lambda i: (i, 0))
    const2 = lambda s: pl.BlockSpec(s, lambda i: (0, 0))
    const3 = lambda s: pl.BlockSpec(s, lambda i: (0, 0, 0))
    return pl.pallas_call(
        body, name="tail_fwd_bwd", grid=(nt,),
        in_specs=[row(D), row(PLE), row(D), row(DG), row(DG),
                  const2((D, D)), const2((D, D)), const3((NSHARD, PLE, PLE)),
                  const2((1, D)), const2((1, D)), const2((1, D))],
        out_specs=[row(D), row(D), const3((NSHARD, QR, D)), const3((NSHARD, QR, D)),
                   const3((NSHARD, PLE, PLE)), const2((8, D))],
        out_shape=[jax.ShapeDtypeStruct((T, D), F32), jax.ShapeDtypeStruct((T, D), F32),
                   jax.ShapeDtypeStruct((NSHARD, QR, D), BF16), jax.ShapeDtypeStruct((NSHARD, QR, D), BF16),
                   jax.ShapeDtypeStruct((NSHARD, PLE, PLE), BF16), jax.ShapeDtypeStruct((8, D), F32)],
        scratch_shapes=[pltpu.VMEM((D, D), F32), pltpu.VMEM((D, D), F32), pltpu.VMEM((NSHARD, PLE, PLE), F32)],
        compiler_params=_cparams(("arbitrary",), 56),
    )(x, p, tgt, ya, yb, w_out_b, w_pg_b, w_pp_b, ple_nw, b_pg, fnw)


def _bwd_rglru(proj, h, dyab, cw3, conv_b, wa_c, ba, wx_c, bx, lam, dep):
    tm = 512
    nt = T // tm
    ng = tm // 8

    def body(xa_ref, ga_ref, xp_ref, h_ref, hp_ref, dya_ref, cw_ref, cb_ref, wa_ref, ba_ref, wx_ref, bx_ref,
             lam_ref, dep_ref, da_ref, rg_ref, sm_ref, a_s, g_s, cg_s, nxt_s, wa_s, wx_s, dwa_s, dwx_s):
        i = pl.program_id(0)
        tile = nt - 1 - i

        @pl.when(i == 0)
        def _():
            dwa_s[...] = jnp.zeros_like(dwa_s)
            dwx_s[...] = jnp.zeros_like(dwx_s)
            sm_ref[...] = jnp.zeros_like(sm_ref)
            cg_s[...] = jnp.zeros_like(cg_s)
            nxt_s[...] = jnp.zeros_like(nxt_s)
            wa_s[...] = _dense_from_blocks(wa_ref[...]).astype(BF16)
            wx_s[...] = _dense_from_blocks(wx_ref[...]).astype(BF16)

        rows = lax.broadcasted_iota(I32, (tm, DG), 0)
        has_prev = tile > 0
        xa = xa_ref[...]
        xprev = jnp.where(has_prev, xp_ref[...], 0.0)
        cw = _conv_rows(cw_ref)
        xc, taps = _rg_conv(xa, xprev, cw, cb_ref[...], rows)
        lam_v = lam_ref[...]
        sp = _softplus(-lam_v)
        first_row = (rows + tile * tm) == 0
        r, ig, a, a2, mult = _rg_gates(xc, wa_s[...], ba_ref[...], wx_s[...], bx_ref[...], sp, first_row)
        hv = h_ref[...]
        hprev = jnp.where(has_prev, hp_ref[...], 0.0)
        h_m1 = _shift_down(hv, hprev, 1, rows)
        ga = ga_ref[...]
        sg = _sigmoid(ga)
        dya = dya_ref[...]
        dga = dya * hv * (sg * (1.0 + ga * (1.0 - sg)))

        av = jnp.where(rows == tm - 1, 1.0, pltpu.roll(a, tm - 1, 0))
        gv = dya * (ga * sg)
        rows8 = rows & 7
        for d in (1, 2, 4):
            keep = rows8 < 8 - d
            gv = gv + av * jnp.where(keep, _roll_in_groups(gv, 8 - d), 0.0)
            av = av * jnp.where(keep, _roll_in_groups(av, 8 - d), 1.0)
        a_s[...] = av
        g_s[...] = gv
        carry = cg_s[0:1, :]
        for g in range(ng - 1, -1, -1):
            sl = slice(g * 8, (g + 1) * 8)
            ab, gb = a_s[sl, :], g_s[sl, :]
            g_s[sl, :] = gb + ab * carry
            carry = gb[0:1, :] + ab[0:1, :] * carry
        cg_s[0:1, :] = a[0:1, :] * carry

        gt = g_s[...]
        da = gt * h_m1
        ixc = ig * xc
        di = gt * mult * xc
        dxc = gt * mult * ig
        dlog_a = da * a + jnp.where(first_row, 0.0, gt * ixc * (-a2 / mult))
        sm_ref[3:4, :] += _rowsum(dlog_a * ((-RG_C) * r))
        dpr_f = dlog_a * ((-RG_C) * sp) * r * (1.0 - r)
        dpi_f = di * ig * (1.0 - ig)
        sm_ref[1:2, :] += _rowsum(dpr_f)
        sm_ref[2:3, :] += _rowsum(dpi_f)
        dpr = dpr_f.astype(BF16)
        dpi = dpi_f.astype(BF16)
        xcb = xc.astype(BF16)
        dwa_s[...] += _mm_tn(xcb, dpr)
        dwx_s[...] += _mm_tn(xcb, dpi)
        dxc = dxc + _mm_nt(dpr, wa_s[...]) + _mm_nt(dpi, wx_s[...])
        sm_ref[0:1, :] += _rowsum(dxc)
        for j in range(4):
            sm_ref[4 + j:5 + j, :] += _rowsum(dxc * taps[j])
        nxt = nxt_s[...]
        dxa = (dxc * cw[3] + _shift_up(dxc, nxt, 1, rows) * cw[2]
               + _shift_up(dxc, nxt, 2, rows) * cw[1] + _shift_up(dxc, nxt, 3, rows) * cw[0])
        nxt_s[...] = dxc[0:8, :]
        da_ref[:, 0:DG] = dxa.astype(BF16)
        da_ref[:, DG:D] = dga.astype(BF16)

        @pl.when(i == nt - 1)
        def _():
            _, s_neg = _sig_pair(lam_v)
            sm_ref[3:4, :] = sm_ref[3:4, :] * (-s_neg)
            rg_ref[...] = _blocks_from_dense(dwa_s[...], dwx_s[...])

    vec = lambda: pl.BlockSpec((1, DG), lambda i: (0, 0))
    blocks = lambda: pl.BlockSpec((DG, RGB), lambda i: (0, 0))
    prev8 = lambda: pl.BlockSpec((8, DG), lambda i: (jnp.maximum((nt - 1 - i) * (tm // 8) - 1, 0), 0))
    return pl.pallas_call(
        body, name="bwd_rglru", grid=(nt,),
        in_specs=[pl.BlockSpec((tm, DG), lambda i: (nt - 1 - i, 0)),
                  pl.BlockSpec((tm, DG), lambda i: (nt - 1 - i, 1)),
                  prev8(),
                  pl.BlockSpec((tm, DG), lambda i: (nt - 1 - i, 0)),
                  prev8(),
                  pl.BlockSpec((tm, DG), lambda i: (nt - 1 - i, 0)),
                  pl.BlockSpec((NSHARD, 4, 128), lambda i: (0, 0, 0)), vec(),
                  blocks(), vec(), blocks(), vec(), vec(), ANY_SPEC],
        out_specs=[pl.BlockSpec((tm, D), lambda i: (nt - 1 - i, 0)),
                   pl.BlockSpec((DG, 128), lambda i: (0, 0)),
                   pl.BlockSpec((8, DG), lambda i: (0, 0))],
        out_shape=[jax.ShapeDtypeStruct((T, D), BF16), jax.ShapeDtypeStruct((DG, 128), F32),
                   jax.ShapeDtypeStruct((8, DG), F32)],
        scratch_shapes=[pltpu.VMEM((tm, DG), F32), pltpu.VMEM((tm, DG), F32),
                        pltpu.VMEM((8, DG), F32), pltpu.VMEM((8, DG), F32),
                        pltpu.VMEM((DG, DG), BF16), pltpu.VMEM((DG, DG), BF16),
                        pltpu.VMEM((DG, DG), F32), pltpu.VMEM((DG, DG), F32)],
        compiler_params=_cparams(("arbitrary",), 56),
    )(proj, proj, proj, h, h, dyab, cw3, conv_b, wa_c, ba, wx_c, bx, lam, dep)


def _bwd_hgrn2(proj, o, s_prev, dyab, hg_lb, hg_nw, dep):
    tm = 512
    nt = T // tm
    nc = tm // CH

    def body(q_ref, f_ref, v_ref, g_ref, o_ref, sp_ref, dy_ref, lb_ref, nw_ref, dep_ref, db_ref, sm_ref,
             dst_s, eb_s, enb_s, ekd_s, qe_s, ke_s, kd_s, vb_s, do_s, dec_s, ddec_s, p_s, dp_s,
             g_s, dsta_s, dva_s, dqe_s, dke_s, dkd_s, dlf_s):
        i = pl.program_id(0)

        @pl.when(i == 0)
        def _():
            sm_ref[...] = jnp.zeros_like(sm_ref)
            dst_s[...] = jnp.zeros_like(dst_s)

        lb, one_m_lb = _hg_lower_bound(lb_ref)
        rows64 = lax.broadcasted_iota(I32, (tm, DG), 0) & (CH - 1)
        sg, sn, f, k, q, sq, qs = _hg_recompute(
            q_ref, f_ref, lb, one_m_lb, rows64, eb_s, enb_s, ekd_s, qe_s, ke_s, kd_s, dec_s)
        vb_s[...] = v_ref[...].astype(BF16)

        nw = nw_ref[...]
        for hd in range(NH):
            cols = slice(hd * HD, (hd + 1) * HD)
            g = g_ref[:, cols]
            sgg = _sigmoid(g)
            o = o_ref[:, cols]
            so = lax.rsqrt(jnp.mean(o * o, axis=-1, keepdims=True) + EPS)
            oh = o * so
            dyb = dy_ref[:, cols]
            db_ref[:, 3 * DG + hd * HD:3 * DG + (hd + 1) * HD] = (
                dyb * (oh * nw) * (sgg * (1.0 + g * (1.0 - sgg)))).astype(BF16)
            don = dyb * (g * sgg)
            sm_ref[1:2, 0:HD] += _rowsum(don * oh)
            gw = don * nw
            do_s[:, cols] = (so * (gw - oh * jnp.mean(gw * oh, axis=-1, keepdims=True))).astype(BF16)

        mask = _tri(True)
        items = [(c, hd, slice(c * CH, (c + 1) * CH), slice(hd * HD, (hd + 1) * HD))
                 for c in range(nc) for hd in range(NH)]
        for c, hd, rs, cols in items:
            p_s[c * NH + hd] = jnp.where(mask, _mm_nt(qe_s[rs, cols], ke_s[rs, cols]), 0.0).astype(BF16)
            dp_s[c * NH + hd] = jnp.where(mask, _mm_nt(do_s[rs, cols], vb_s[rs, cols]), 0.0).astype(BF16)
        for c, hd, rs, cols in items:
            n = c * NH + hd
            dva_s[rs, cols] = _mm_tn(p_s[n], do_s[rs, cols])
            dqe_s[rs, cols] = _mm(dp_s[n], ke_s[rs, cols])
            dke_s[rs, cols] = _mm_tn(dp_s[n], qe_s[rs, cols])
            g_s[n] = _mm_tn(do_s[rs, cols], qe_s[rs, cols])
        for c, hd, rs, cols in reversed(items):
            n = c * NH + hd
            dst = dst_s[hd]
            dsta_s[n] = dst
            dst_s[hd] = dst * dec_s[c:c + 1, cols] + g_s[n]
        for c, hd, rs, cols in items:
            n = c * NH + hd
            dst = dsta_s[n]
            st_prev = sp_ref[hd, c]
            dv = dva_s[rs, cols] + _mm_nt(kd_s[rs, cols], dst)
            db_ref[rs, 2 * DG + hd * HD:2 * DG + (hd + 1) * HD] = dv.astype(BF16)
            dqe_s[rs, cols] += _mm(do_s[rs, cols], st_prev)
            dkd_s[rs, cols] = _mm(vb_s[rs, cols], dst)
            ddec_s[c:c + 1, cols] = _rowsum(dst * st_prev)

        eb, enb, ekd = eb_s[...], enb_s[...], ekd_s[...]
        dqe, dke, dkd = dqe_s[...], dke_s[...], dkd_s[...]
        t_kd = dkd * (k * ekd)
        rc = _chunk_rev_cumsum(dqe * (qs * eb) - dke * (k * enb) - t_kd, rows64)
        for c in range(nc):
            rs = slice(c * CH, (c + 1) * CH)
            dbl = _rowsum(t_kd[rs]) + ddec_s[c:c + 1, :] * dec_s[c:c + 1, :]
            dlf_s[rs, :] = rc[rs] + dbl
        t = dlf_s[...] / f - (dke * enb + dkd * ekd)
        db_ref[:, DG:2 * DG] = (one_m_lb * sg * sn * t).astype(BF16)
        sm_ref[0:1, :] += _rowsum(sn * t)
        db_ref[:, 0:DG] = (dqe * eb * (sq * (1.0 + q * (1.0 - sq))) * (HD ** -0.5)).astype(BF16)

        @pl.when(i == nt - 1)
        def _():
            dsm = sm_ref[0:1, :] * (lb * one_m_lb)
            sm_ref[2:3, :] = dsm
            sm_ref[3:4, :] = -dsm

    col = lambda j: pl.BlockSpec((tm, DG), lambda i: (nt - 1 - i, j))
    big = lambda dt: pltpu.VMEM((tm, DG), dt)
    return pl.pallas_call(
        body, name="bwd_hgrn2", grid=(nt,),
        in_specs=[col(2), col(3), col(4), col(5),
                  pl.BlockSpec((tm, DG), lambda i: (nt - 1 - i, 0)),
                  pl.BlockSpec((NH, nc, HD, HD), lambda i: (0, nt - 1 - i, 0, 0)),
                  pl.BlockSpec((tm, DG), lambda i: (nt - 1 - i, 1)),
                  pl.BlockSpec((2, DG), lambda i: (0, 0)),
                  pl.BlockSpec((1, HD), lambda i: (0, 0)), ANY_SPEC],
        out_specs=[pl.BlockSpec((tm, 4 * DG), lambda i: (nt - 1 - i, 0)),
                   pl.BlockSpec((8, DG), lambda i: (0, 0))],
        out_shape=[jax.ShapeDtypeStruct((T, 4 * DG), BF16), jax.ShapeDtypeStruct((8, DG), F32)],
        scratch_shapes=[pltpu.VMEM((NH, HD, HD), F32),
                        big(F32), big(F32), big(F32),
                        big(BF16), big(BF16), big(BF16), big(BF16), big(BF16),
                        pltpu.VMEM((nc, DG), F32), pltpu.VMEM((nc, DG), F32),
                        pltpu.VMEM((nc * NH, CH, CH), BF16), pltpu.VMEM((nc * NH, CH, CH), BF16),
                        pltpu.VMEM((nc * NH, HD, HD), F32), pltpu.VMEM((nc * NH, HD, HD), F32),
                        big(F32), big(F32), big(F32), big(F32), big(F32)],
        compiler_params=_cparams(("arbitrary",), 56),
    )(proj, proj, proj, proj, o, s_prev, dyab, hg_lb, hg_nw, dep)


def _dproj_pieces(k, da_ref, db_ref):
    if k == 0:
        return [(da_ref[:, 0:SHW], 0)]
    if k == 1:
        return [(da_ref[:, SHW:D], 0), (db_ref[:, 0:DG], D - SHW)]
    if k == 2:
        return [(db_ref[:, DG:DG + SHW], 0)]
    return [(db_ref[:, DG + SHW:4 * DG], 0)]


def _bwd_inproj_dx(x, dh1, d_a, d_b, w_in_b, nw, dep):
    tm = 512
    nt = T // tm

    def body(x_ref, dh1_ref, da_ref, db_ref, w_ref, nw_ref, dep_ref, dx_ref, sm_ref):
        i = pl.program_id(0)

        @pl.when(i == 0)
        def _():
            sm_ref[...] = jnp.zeros_like(sm_ref)

        du = None
        for k in range(NSHARD):
            for val, off in _dproj_pieces(k, da_ref, db_ref):
                t = _mm_nt(val, w_ref[k, :, off:off + val.shape[1]])
                du = t if du is None else du + t
        xv = x_ref[...]
        s = lax.rsqrt(jnp.mean(xv * xv, axis=-1, keepdims=True) + EPS)
        xh = xv * s
        sm_ref[0:1, :] += _rowsum(du * xh)
        g = du * nw_ref[...]
        dx_ref[...] = dh1_ref[...] + s * (g - xh * jnp.mean(g * xh, axis=-1, keepdims=True))

    row = lambda w: pl.BlockSpec((tm, w), lambda i: (i, 0))
    return pl.pallas_call(
        body, name="bwd_inproj_dx", grid=(nt,),
        in_specs=[row(D), row(D), row(D), row(4 * DG),
                  pl.BlockSpec((NSHARD, D, SHW), lambda i: (0, 0, 0)),
                  pl.BlockSpec((1, D), lambda i: (0, 0)), ANY_SPEC],
        out_specs=[row(D), pl.BlockSpec((8, D), lambda i: (0, 0))],
        out_shape=[jax.ShapeDtypeStruct((T, D), F32), jax.ShapeDtypeStruct((8, D), F32)],
        compiler_params=_cparams(("arbitrary",), 56),
    )(x, dh1, d_a, d_b, w_in_b, nw, dep)


def _bwd_inproj_dw(name, ks, u_b, d_a, d_b):
    tm = 512
    nt = T // tm
    need_a = 0 in ks or 1 in ks

    def body(*refs):
        if need_a:
            u_ref, da_ref, db_ref, dw_ref, acc = refs
        else:
            u_ref, db_ref, dw_ref, acc = refs
            da_ref = None
        i = pl.program_id(0)

        @pl.when(i == 0)
        def _():
            acc[...] = jnp.zeros_like(acc)

        u = u_ref[...]
        for j, k in enumerate(ks):
            for val, off in _dproj_pieces(k, da_ref, db_ref):
                acc[j, :, off:off + val.shape[1]] += _mm_tn(u, val)

        @pl.when(i == nt - 1)
        def _():
            for j in range(len(ks)):
                for r0 in range(0, D, 256):
                    dw_ref[j, r0:r0 + 256, :] = acc[j, r0:r0 + 256, :].astype(BF16)

    row = lambda w: pl.BlockSpec((tm, w), lambda i: (i, 0))
    ins = [u_b] + ([d_a] if need_a else []) + [d_b]
    in_specs = [row(D)] + ([row(D)] if need_a else []) + [row(4 * DG)]
    return pl.pallas_call(
        body, name=name, grid=(nt,), in_specs=in_specs,
        out_specs=pl.BlockSpec((len(ks), D, SHW), lambda i: (0, 0, 0)),
        out_shape=jax.ShapeDtypeStruct((len(ks), D, SHW), BF16),
        scratch_shapes=[pltpu.VMEM((len(ks), D, SHW), F32)],
        compiler_params=_cparams(("arbitrary",), 48),
    )(*ins)


_OUT_ORDER = ["norm_mix_w", "w_in", "conv_w", "conv_b", "rg_wa", "rg_ba", "rg_wx", "rg_bx", "rg_lambda", "hg_lb",
              "hg_norm_w", "w_out", "ple_norm_w", "w_ple_gate", "b_ple_gate", "w_ple_proj", "final_norm_w"]
_BIG = ["w_in", "w_out", "w_ple_gate", "w_ple_proj"]


def _small_view(name, a):
    if name in ("rg_wa", "rg_wx"):
        return a.reshape(DG, RGB)
    if name == "conv_w":
        return a.reshape(4, 128)
    if name == "final_norm_w":
        return a.reshape(1, D)
    return a


def _landing(rows, cols):
    return lax.empty((NDEV, rows, cols), BF16)


def kernel(x, p, norm_mix_w, w_in, conv_w, conv_b, rg_wa, rg_ba, rg_wx, rg_bx, rg_lambda, hg_lb, hg_norm_w, w_out, ple_norm_w, w_ple_gate, b_ple_gate, w_ple_proj, final_norm_w, loss_target, m_norm_mix_w, m_w_in, m_conv_w, m_conv_b, m_rg_wa, m_rg_ba, m_rg_wx, m_rg_bx, m_rg_lambda, m_hg_lb, m_hg_norm_w, m_w_out, m_ple_norm_w, m_w_ple_gate, m_b_ple_gate, m_w_ple_proj, m_final_norm_w, v_norm_mix_w, v_w_in, v_conv_w, v_conv_b, v_rg_wa, v_rg_ba, v_rg_wx, v_rg_bx, v_rg_lambda, v_hg_lb, v_hg_norm_w, v_w_out, v_ple_norm_w, v_w_ple_gate, v_b_ple_gate, v_w_ple_proj, v_final_norm_w):
    given = dict(locals())
    x2, p2, tgt = x[0], p[0, 0], loss_target[0]
    wa_c, wx_c = _small_view("rg_wa", rg_wa), _small_view("rg_wx", rg_wx)

    w_in_b, l_out, l_pg, l_pp, cw3 = _gather_w_in(w_in[0], w_out[0], w_ple_gate[0], w_ple_proj[0], conv_w[0])
    g_ssem, g_rsem, g_lands, tok = _gather_rest_start([l_out, l_pg, l_pp])

    proj, u_b = _fwd_inproj(x2, norm_mix_w, w_in_b, tok)
    h, ya = _fwd_rglru(proj, cw3, conv_b, wa_c, rg_ba, wx_c, rg_bx, rg_lambda)
    yb, o, s_prev = _fwd_hgrn2(proj, hg_lb, hg_norm_w)
    w_out_b, w_pg_b, w_pp_b = _gather_rest_wait(g_ssem, g_rsem, g_lands, yb)
    dh1, dyab, dwo_b, dwg_b, dwp_b, sm_tail = _tail_fwd_bwd(
        x2, p2, tgt, ya, yb, w_out_b.reshape(D, D), w_pg_b.reshape(D, D), w_pp_b,
        ple_norm_w, b_ple_gate, final_norm_w.reshape(1, D))

    QH = D // NSHARD // 2
    r1 = _rs_start("rs_start_tail", [dwo_b, dwg_b, dwp_b], (0, 1, 2, 3),
                   [_landing(QH, D), _landing(QH, D), _landing(PLE // 2, PLE)])
    d_b, sm_b = _bwd_hgrn2(proj, o, s_prev, dyab, hg_lb, hg_norm_w, r1[4])
    dw23 = _bwd_inproj_dw("bwd_inproj_dw23", (2, 3), u_b, None, d_b)
    r2 = _rs_start("rs_start_in23", [dw23], (2, 3), [_landing(D // 2, SHW)])
    d_a, rg_c, sm_a = _bwd_rglru(proj, h, dyab, cw3, conv_b, wa_c, rg_ba, wx_c, rg_bx, rg_lambda, r2[4])
    dw01 = _bwd_inproj_dw("bwd_inproj_dw01", (0, 1), u_b, d_a, d_b)
    r3 = _rs_start("rs_start_in01", [dw01], (0, 1), r2[3])
    grad_x, sm_in = _bwd_inproj_dx(x2, dh1, d_a, d_b, w_in_b, norm_mix_w, r3[4])

    vred, rgred = _small_allreduce(sm_in, sm_tail, sm_a, sm_b, rg_c)
    parts1, lands1 = _rs_wait("rs_wait_tail", r1[0], r1[1], r1[2], (0, 1, 2, 3), r1[3], vred)
    parts2, lands2 = _rs_wait("rs_wait_in23", r2[0], r2[1], r2[2], (2, 3), r3[3], vred)
    parts3, lands3 = _rs_wait("rs_wait_in01", r3[0], r3[1], r3[2], (0, 1), lands2, vred)
    g_big = _reduce_exchange([parts3[0], parts2[0]] + parts1, lands3 + lands1)

    upd_big = _adam_big(g_big, [given[n][0] for n in _BIG], [given["m_" + n][0] for n in _BIG],
                        [given["v_" + n][0] for n in _BIG])
    small = _adam_small(vred, rgred,
                        {n: _small_view(n, given[n]) for n in _SMALL_ORDER},
                        {n: _small_view(n, given["m_" + n]) for n in _SMALL_ORDER},
                        {n: _small_view(n, given["v_" + n]) for n in _SMALL_ORDER})

    loss = vred[0, ROW_LOSS, 0]
    outs = [loss, grad_x[None]]
    for ki in range(4):
        for n in _OUT_ORDER:
            if n in _BIG:
                i = _BIG.index(n)
                a = g_big[i] if ki == 0 else upd_big[i][ki - 1]
                outs.append(a[None])
            else:
                outs.append(small[n][ki].reshape(given[n].shape))
    return tuple(outs)
```

```python
import jax
import jax.numpy as jnp
from jax import lax
from jax.experimental import pallas as pl
from jax.experimental.pallas import tpu as pltpu

F32 = jnp.float32
BF16 = jnp.bfloat16
I32 = jnp.int32
MESH = pl.DeviceIdType.MESH
HIGHEST = lax.Precision.HIGHEST

T = 4096
D = 1024
DG = 512
DIN = 3072
PLE = 256
NH = 4
HD = 128
CH = 64
NCHUNK = T // CH
RGB = 64
EPS = 1e-6
RG_C = 8.0
NSHARD = 4
SHW = DIN // NSHARD
NDEV = 8

ADAM_LR = 0.001
ADAM_B1 = 0.9
ADAM_B2 = 0.999
ADAM_EPS = 1e-08
ADAM_WD = 0.01
ADAM_STEP = 10

VMEM_SPEC = pl.BlockSpec(memory_space=pltpu.VMEM)
HBM_SPEC = pl.BlockSpec(memory_space=pltpu.HBM)
SEM_SPEC = pl.BlockSpec(memory_space=pltpu.SEMAPHORE)
ANY_SPEC = pl.BlockSpec(memory_space=pl.ANY)
EFFECT = pltpu.SideEffectType.DATAFLOW_SIDE_EFFECTING
MIB = 1024 * 1024

VROWS = 16
ROW_NORM_MIX, ROW_FINAL_NORM, ROW_B_PG, ROW_PLE_NORM = 0, 1, 2, 3
ROW_CB_BA, ROW_BX_LAM, ROW_CW01, ROW_CW23, ROW_HG_LB, ROW_HG_NW, ROW_LOSS = 4, 5, 6, 7, 8, 9, 10


def _mm(a, b):
    return jnp.dot(a.astype(BF16), b.astype(BF16), preferred_element_type=F32)


def _mm_nt(a, b):
    return lax.dot_general(a.astype(BF16), b.astype(BF16), (((1,), (1,)), ((), ())),
                           preferred_element_type=F32)


def _mm_tn(a, b):
    return lax.dot_general(a.astype(BF16), b.astype(BF16), (((0,), (0,)), ((), ())),
                           preferred_element_type=F32)


def _mm_exact(a, b):
    return jnp.dot(a, b, precision=HIGHEST, preferred_element_type=F32)


def _sig_pair(x):
    e = jnp.exp(-jnp.abs(x))
    big = 1.0 / (1.0 + e)
    small = e * big
    pos = x >= 0
    return jnp.where(pos, big, small), jnp.where(pos, small, big)


def _sigmoid(x):
    return 1.0 / (1.0 + jnp.exp(-x))


def _rowsum(v):
    return jnp.sum(v, axis=0, keepdims=True)


def _shift_down(cur, prev8, d, rows):
    rolled = pltpu.roll(cur, d, 0)
    head = jnp.where(rows[0:8] < d, pltpu.roll(prev8, d, 0), rolled[0:8])
    return jnp.concatenate([head, rolled[8:]], axis=0)


def _shift_up(cur, next8, d, rows):
    n = cur.shape[0]
    rolled = pltpu.roll(cur, n - d, 0)
    tail = jnp.where(rows[0:8] >= 8 - d, pltpu.roll(next8, 8 - d, 0), rolled[n - 8:n])
    return jnp.concatenate([rolled[0:n - 8], tail], axis=0)


def _roll_in_groups(v, d):
    n, w = v.shape
    return pltpu.roll(v.reshape(n // 8, 8, w), d, 1).reshape(n, w)


def _cparams(sem, vmem_mib):
    return pltpu.CompilerParams(dimension_semantics=sem, vmem_limit_bytes=vmem_mib * MIB)


def _mesh_pos():
    x, y, c = lax.axis_index("x"), lax.axis_index("y"), lax.axis_index("c")
    chips = [(1 - x, y), (x, 1 - y), (1 - x, 1 - y)]
    return x, y, c, chips


def _remote(src, dst, ssem, rsem, dev):
    return pltpu.make_async_remote_copy(src_ref=src, dst_ref=dst, send_sem=ssem, recv_sem=rsem,
                                        device_id=dev, device_id_type=MESH)


def _gather_w_in(w_in, w_out, w_pg, w_pp, conv_w):
    shapes = [w_in.shape, w_out.shape, w_pg.shape, w_pp.shape]

    def body(win, wout, wpg, wpp, cw, o_in, o_out, o_pg, o_pp, o_cw, ssem, rsem):
        x, y, c, chips = _mesh_pos()
        kme = 2 * x + y
        sibling = (x, y, 1 - c)
        for src, dst in [(win, o_in), (wout, o_out), (wpg, o_pg), (wpp, o_pp)]:
            for r0 in range(0, src.shape[0], 128):
                dst[kme, r0:r0 + 128, :] = src[r0:r0 + 128, :].astype(BF16)
        o_cw[kme] = cw[...]

        hrows = D // 2
        mine = pl.ds(pl.multiple_of(c * hrows, 128), hrows)
        other = pl.ds(pl.multiple_of((1 - c) * hrows, 128), hrows)
        sends = []
        for j, (px, py) in enumerate(chips):
            cp = _remote(o_in.at[kme, mine], o_in.at[kme, mine], ssem.at[j], rsem.at[j], (px, py, c))
            cp.start()
            sends.append(cp)
            cp = _remote(o_cw.at[kme], o_cw.at[kme], ssem.at[3 + j], rsem.at[3 + j], (px, py, c))
            cp.start()
            sends.append(cp)
        for j, (px, py) in enumerate(chips):
            kj = 2 * px + py
            _remote(o_in.at[kj, mine], o_in.at[kj, mine], ssem.at[j], rsem.at[j], (px, py, c)).wait_recv()
            cp = _remote(o_in.at[kj, mine], o_in.at[kj, mine], ssem.at[6 + j], rsem.at[6 + j], sibling)
            cp.start()
            sends.append(cp)
        for j, (px, py) in enumerate(chips):
            kj = 2 * px + py
            _remote(o_cw.at[kj], o_cw.at[kj], ssem.at[3 + j], rsem.at[3 + j], (px, py, c)).wait_recv()
            _remote(o_in.at[kj, other], o_in.at[kj, other], ssem.at[6 + j], rsem.at[6 + j], sibling).wait_recv()
        for cp in sends:
            cp.wait_send()

    out_shape = [jax.ShapeDtypeStruct((NSHARD,) + s, BF16) for s in shapes]
    out_shape.append(jax.ShapeDtypeStruct((NSHARD,) + conv_w.shape, F32))
    return pl.pallas_call(
        body, name="gather_w_in", out_shape=out_shape,
        in_specs=[VMEM_SPEC] * 5, out_specs=[VMEM_SPEC] * 5,
        scratch_shapes=[pltpu.SemaphoreType.DMA((9,)), pltpu.SemaphoreType.DMA((9,))],
        compiler_params=pltpu.CompilerParams(vmem_limit_bytes=40 * MIB),
    )(w_in, w_out, w_pg, w_pp, conv_w)


def _gather_rest_start(lands):
    n = len(lands)

    def body(*refs):
        land_in = refs[0:n]
        ssem, rsem = refs[n], refs[n + 1]
        token = refs[2 * n + 2]
        x, y, c, chips = _mesh_pos()
        kme = 2 * x + y
        for p, land in enumerate(land_in):
            hrows = land.shape[1] // 2
            mine = pl.ds(pl.multiple_of(c * hrows, 128), hrows)
            for px, py in chips:
                for pc in range(2):
                    _remote(land.at[kme, mine], land.at[kme, mine], ssem.at[p], rsem.at[p], (px, py, pc)).start()
        token[...] = jnp.zeros_like(token)

    out_shape = ([pltpu.SemaphoreType.DMA((n,)), pltpu.SemaphoreType.DMA((n,))]
                 + [pltpu.HBM(a.shape, a.dtype) for a in lands] + [jax.ShapeDtypeStruct((8, 128), F32)])
    outs = pl.pallas_call(
        body, name="gather_rest_start", out_shape=out_shape,
        in_specs=[HBM_SPEC] * n, out_specs=[SEM_SPEC, SEM_SPEC] + [HBM_SPEC] * n + [VMEM_SPEC],
        input_output_aliases={i: 2 + i for i in range(n)},
        compiler_params=pltpu.CompilerParams(has_side_effects=EFFECT),
    )(*[pltpu.with_memory_space_constraint(a, pltpu.HBM) for a in lands])
    return outs[0], outs[1], list(outs[2:2 + n]), outs[2 + n]


def _gather_rest_wait(ssem, rsem, lands, after):
    n = len(lands)

    def body(*refs):
        land_in = refs[0:n]
        ssem_ref, rsem_ref = refs[n], refs[n + 1]
        x, y, c = lax.axis_index("x"), lax.axis_index("y"), lax.axis_index("c")
        for p, land in enumerate(land_in):
            three = land.at[pl.ds(0, 3)]
            cp = _remote(three, three, ssem_ref.at[p], rsem_ref.at[p], (x, y, c))
            cp.wait_send()
            cp.wait_recv()

    outs = pl.pallas_call(
        body, name="gather_rest_wait", out_shape=[pltpu.HBM(a.shape, a.dtype) for a in lands],
        in_specs=[HBM_SPEC] * n + [SEM_SPEC, SEM_SPEC, ANY_SPEC], out_specs=[HBM_SPEC] * n,
        input_output_aliases={i: i for i in range(n)},
        compiler_params=pltpu.CompilerParams(has_side_effects=EFFECT),
    )(*lands, ssem, rsem, after)
    return list(outs)


def _rs_start(name, parts, ks, lands, chip_sums=False):
    n = len(parts)

    def body(*refs):
        part_in, land_in = refs[0:n], refs[n:2 * n]
        ssem, rsem = refs[2 * n], refs[2 * n + 1]
        token = refs[4 * n + 2]
        x, y, c = lax.axis_index("x"), lax.axis_index("y"), lax.axis_index("c")
        kme = 2 * x + y
        me = 4 * x + 2 * y + c
        for p in range(n):
            hrows = land_in[p].shape[1]
            for i, k in enumerate(ks):
                if chip_sums:
                    @pl.when(kme != k)
                    def _():
                        _remote(part_in[p].at[i], land_in[p].at[kme], ssem.at[p], rsem.at[p], (k // 2, k % 2, c)).start()
                    continue
                for pc in range(2):
                    @pl.when(jnp.logical_or(kme != k, c != pc))
                    def _():
                        _remote(part_in[p].at[i, pl.ds(pc * hrows, hrows)], land_in[p].at[me],
                                ssem.at[p], rsem.at[p], (k // 2, k % 2, pc)).start()
        token[...] = jnp.zeros_like(token)

    arrays = list(parts) + list(lands)
    out_shape = ([pltpu.SemaphoreType.DMA((n,)), pltpu.SemaphoreType.DMA((n,))]
                 + [pltpu.HBM(a.shape, a.dtype) for a in arrays] + [jax.ShapeDtypeStruct((8, 128), F32)])
    outs = pl.pallas_call(
        body, name=name, out_shape=out_shape,
        in_specs=[HBM_SPEC] * (2 * n), out_specs=[SEM_SPEC, SEM_SPEC] + [HBM_SPEC] * (2 * n) + [VMEM_SPEC],
        input_output_aliases={i: 2 + i for i in range(2 * n)},
        compiler_params=pltpu.CompilerParams(has_side_effects=EFFECT),
    )(*[pltpu.with_memory_space_constraint(a, pltpu.HBM) for a in arrays])
    return outs[0], outs[1], list(outs[2:2 + n]), list(outs[2 + n:2 + 2 * n]), outs[2 + 2 * n]


def _rs_wait(name, ssem, rsem, parts, ks, lands, after, chip_sums=False):
    n = len(parts)

    def body(*refs):
        part_in, land_in = refs[0:n], refs[n:2 * n]
        ssem_ref, rsem_ref = refs[2 * n], refs[2 * n + 1]
        x, y, c = lax.axis_index("x"), lax.axis_index("y"), lax.axis_index("c")
        kme = 2 * x + y
        for p in range(n):
            piece = land_in[p].at[0]
            for k in ks:
                for pc in range(1 if chip_sums else 2):
                    mine = (kme == k) if chip_sums else jnp.logical_and(kme == k, c == pc)

                    @pl.when(jnp.logical_not(mine))
                    def _():
                        _remote(piece, piece, ssem_ref.at[p], rsem_ref.at[p], (x, y, c)).wait_send()
            owner = kme == ks[0]
            for k in ks[1:]:
                owner = jnp.logical_or(owner, kme == k)

            @pl.when(owner)
            def _():
                others = land_in[p].at[pl.ds(0, land_in[p].shape[0] - 1)]
                _remote(others, others, ssem_ref.at[p], rsem_ref.at[p], (x, y, c)).wait_recv()

    arrays = list(parts) + list(lands)
    outs = pl.pallas_call(
        body, name=name, out_shape=[pltpu.HBM(a.shape, a.dtype) for a in arrays],
        in_specs=[HBM_SPEC] * (2 * n) + [SEM_SPEC, SEM_SPEC, ANY_SPEC], out_specs=[HBM_SPEC] * (2 * n),
        input_output_aliases={i: i for i in range(2 * n)},
        compiler_params=pltpu.CompilerParams(has_side_effects=EFFECT),
    )(*arrays, ssem, rsem, after)
    return list(outs[0:n]), list(outs[n:2 * n])


def _reduce_exchange(parts, lands):
    shapes = [(2 * l.shape[1], l.shape[2]) for l in lands]
    step = 128

    def body(in01, in23, pout, ppg, ppp, l_in, l_out, l_pg, l_pp, g_in, g_out, g_pg, g_pp, ssem, rsem):
        x, y, c = lax.axis_index("x"), lax.axis_index("y"), lax.axis_index("c")
        kme = 2 * x + y
        me = 4 * x + 2 * y + c
        sibling = (x, y, 1 - c)
        sends = []
        for p, (land, gout) in enumerate([(l_in, g_in), (l_out, g_out), (l_pg, g_pg), (l_pp, g_pp)]):
            hrows = land.shape[1]
            mine0 = pl.multiple_of(c * hrows, step)
            for r0 in range(0, hrows, step):
                rs = pl.ds(mine0 + r0, step)
                if p == 0:
                    own = jnp.where(kme >= 2, in23[kme & 1, r0:r0 + step, :], in01[kme & 1, r0:r0 + step, :])
                    slot = kme
                else:
                    own = (pout, ppg, ppp)[p - 1][kme, rs, :]
                    slot = me
                s = jnp.zeros((step, land.shape[2]), F32)
                for j in range(land.shape[0]):
                    s = s + jnp.where(slot == j, own, land[j, r0:r0 + step, :]).astype(F32)
                gout[rs, :] = s
            mine = pl.ds(mine0, hrows)
            cp = _remote(gout.at[mine], gout.at[mine], ssem.at[p], rsem.at[p], sibling)
            cp.start()
            sends.append(cp)
        for p, (land, gout) in enumerate([(l_in, g_in), (l_out, g_out), (l_pg, g_pg), (l_pp, g_pp)]):
            hrows = land.shape[1]
            other = pl.ds(pl.multiple_of((1 - c) * hrows, step), hrows)
            _remote(gout.at[other], gout.at[other], ssem.at[p], rsem.at[p], sibling).wait_recv()
        for cp in sends:
            cp.wait_send()

    return pl.pallas_call(
        body, name="reduce_exchange", out_shape=[jax.ShapeDtypeStruct(s, F32) for s in shapes],
        in_specs=[VMEM_SPEC] * 9, out_specs=[VMEM_SPEC] * 4,
        scratch_shapes=[pltpu.SemaphoreType.DMA((4,)), pltpu.SemaphoreType.DMA((4,))],
        compiler_params=pltpu.CompilerParams(vmem_limit_bytes=48 * MIB),
    )(*parts, *lands)


def _small_allreduce(sm_in, sm_tail, sm_a, sm_b, rg_c):
    PR = DG // NDEV

    def body(in_ref, tail_ref, a_ref, b_ref, rg_ref, v_out, rg_out, vbuf, vrecv, rgrecv, ssem, rsem):
        x, y, c = lax.axis_index("x"), lax.axis_index("y"), lax.axis_index("c")
        me = 4 * x + 2 * y + c

        def pair(ref, r0, r1):
            return jnp.concatenate([ref[r0:r0 + 1, :], ref[r1:r1 + 1, :]], axis=1)

        rows = {
            ROW_NORM_MIX: in_ref[0:1, :], ROW_FINAL_NORM: tail_ref[1:2, :], ROW_B_PG: tail_ref[2:3, :],
            ROW_PLE_NORM: tail_ref[3:4, :], ROW_CB_BA: pair(a_ref, 0, 1), ROW_BX_LAM: pair(a_ref, 2, 3),
            ROW_CW01: pair(a_ref, 4, 5), ROW_CW23: pair(a_ref, 6, 7), ROW_HG_LB: pair(b_ref, 2, 3),
            ROW_HG_NW: jnp.concatenate([b_ref[1:2, :], jnp.zeros((1, DG), F32)], axis=1),
            ROW_LOSS: tail_ref[4:5, :],
        }
        vbuf[...] = jnp.zeros_like(vbuf)
        for r, row in rows.items():
            for j in range(NDEV):
                vbuf[j, r:r + 1, :] = row[:, j * 128:(j + 1) * 128]

        def peer(mask):
            px = x ^ ((mask >> 2) & 1)
            py = y ^ ((mask >> 1) & 1)
            pc = c ^ (mask & 1)
            return (px, py, pc), 4 * px + 2 * py + pc

        def rg_rows(r):
            return pl.ds(pl.multiple_of(r * PR, PR), PR)

        first = []
        for mask in range(1, NDEV):
            dev, r = peer(mask)
            i = mask - 1
            cp = _remote(vbuf.at[r], vrecv.at[i], ssem.at[i], rsem.at[i], dev)
            cp.start()
            first.append(cp)
            cp = _remote(rg_ref.at[rg_rows(r)], rgrecv.at[i], ssem.at[7 + i], rsem.at[7 + i], dev)
            cp.start()
            first.append(cp)
        sv = vbuf[me]
        sr = rg_ref[rg_rows(me), :]
        for i in range(NDEV - 1):
            first[2 * i].wait_recv()
            first[2 * i + 1].wait_recv()
            sv = sv + vrecv[i]
            sr = sr + rgrecv[i]
        v_out[me] = sv
        rg_out[rg_rows(me), :] = sr
        second = []
        for mask in range(1, NDEV):
            dev, r = peer(mask)
            i = mask - 1
            cp = _remote(v_out.at[me], v_out.at[me], ssem.at[14 + i], rsem.at[14 + i], dev)
            cp.start()
            second.append(cp)
            cp = _remote(rg_out.at[rg_rows(me)], rg_out.at[rg_rows(me)], ssem.at[21 + i], rsem.at[21 + i], dev)
            cp.start()
            second.append(cp)
        for mask in range(1, NDEV):
            dev, r = peer(mask)
            i = mask - 1
            _remote(v_out.at[r], v_out.at[r], ssem.at[14 + i], rsem.at[14 + i], dev).wait_recv()
            _remote(rg_out.at[rg_rows(r)], rg_out.at[rg_rows(r)], ssem.at[21 + i], rsem.at[21 + i], dev).wait_recv()
        for cp in first + second:
            cp.wait_send()

    return pl.pallas_call(
        body, name="small_allreduce",
        out_shape=[jax.ShapeDtypeStruct((NDEV, VROWS, 128), F32), jax.ShapeDtypeStruct((DG, 128), F32)],
        in_specs=[VMEM_SPEC] * 5, out_specs=[VMEM_SPEC] * 2,
        scratch_shapes=[pltpu.VMEM((NDEV, VROWS, 128), F32), pltpu.VMEM((NDEV - 1, VROWS, 128), F32),
                        pltpu.VMEM((NDEV - 1, PR, 128), F32),
                        pltpu.SemaphoreType.DMA((28,)), pltpu.SemaphoreType.DMA((28,))],
    )(sm_in, sm_tail, sm_a, sm_b, rg_c)


def _adam_rows(w, g, m, v):
    m2 = ADAM_B1 * m + (1.0 - ADAM_B1) * g
    v2 = ADAM_B2 * v + (1.0 - ADAM_B2) * (g * g)
    m_hat = m2 / (1.0 - ADAM_B1 ** ADAM_STEP)
    v_hat = v2 / (1.0 - ADAM_B2 ** ADAM_STEP)
    delta = -ADAM_LR * (m_hat / (jnp.sqrt(v_hat) + ADAM_EPS) + ADAM_WD * w)
    return delta, m2, v2


def _adam_big(gs, ws, ms, vs):
    n = len(gs)
    steps = 8

    def body(*refs):
        ins, outs = refs[:4 * n], refs[4 * n:]
        for i in range(n):
            g, w, m, v = (r[...] for r in ins[4 * i:4 * i + 4])
            d, m2, v2 = _adam_rows(w, g, m, v)
            outs[3 * i][...] = d
            outs[3 * i + 1][...] = m2
            outs[3 * i + 2][...] = v2

    in_specs, out_specs, out_shape, args = [], [], [], []
    for g, w, m, v in zip(gs, ws, ms, vs):
        r, c = w.shape
        spec = lambda: pl.BlockSpec((r // steps, c), lambda i: (i, 0))
        in_specs += [spec() for _ in range(4)]
        out_specs += [spec() for _ in range(3)]
        out_shape += [jax.ShapeDtypeStruct((r, c), F32)] * 3
        args += [g, w, m, v]
    outs = pl.pallas_call(
        body, name="adam_big", grid=(steps,), in_specs=in_specs, out_specs=out_specs, out_shape=out_shape,
        compiler_params=_cparams(("parallel",), 32),
    )(*args)
    return [tuple(outs[3 * i:3 * i + 3]) for i in range(n)]


_VEC_PARAMS = [
    ("norm_mix_w", ROW_NORM_MIX, 0, D), ("final_norm_w", ROW_FINAL_NORM, 0, D),
    ("b_ple_gate", ROW_B_PG, 0, D), ("ple_norm_w", ROW_PLE_NORM, 0, D),
    ("conv_b", ROW_CB_BA, 0, DG), ("rg_ba", ROW_CB_BA, DG, DG),
    ("rg_bx", ROW_BX_LAM, 0, DG), ("rg_lambda", ROW_BX_LAM, DG, DG),
    ("hg_norm_w", ROW_HG_NW, 0, HD),
]
_SMALL_ORDER = [n for n, _, _, _ in _VEC_PARAMS] + ["hg_lb", "conv_w", "rg_wa", "rg_wx"]


def _adam_small(vred, rgred, ws, ms, vs):
    names = _SMALL_ORDER
    n = len(names)

    def body(vred_ref, rg_ref, *refs):
        w_refs = dict(zip(names, refs[0:n]))
        m_refs = dict(zip(names, refs[n:2 * n]))
        v_refs = dict(zip(names, refs[2 * n:3 * n]))
        outs = refs[3 * n:]
        o_refs = {nm: outs[4 * i:4 * i + 4] for i, nm in enumerate(names)}
        kme = 2 * lax.axis_index("x") + lax.axis_index("y")

        def update(nm, g, idx):
            d, m2, v2 = _adam_rows(w_refs[nm][idx], g, m_refs[nm][idx], v_refs[nm][idx])
            og, od, om, ov = o_refs[nm]
            og[idx] = g
            od[idx] = d
            om[idx] = m2
            ov[idx] = v2

        def packed(row, lane0, width):
            return jnp.concatenate([vred_ref[j, row:row + 1, :] for j in range(lane0 // 128, (lane0 + width) // 128)],
                                   axis=1)

        everything = (slice(None), slice(None))
        for nm, row, lane0, width in _VEC_PARAMS:
            update(nm, packed(row, lane0, width), everything)
        for r in range(2):
            update("hg_lb", packed(ROW_HG_LB, r * DG, DG), (slice(r, r + 1), slice(None)))
        for j in range(4):
            g = vred_ref[(j % 2) * 4 + kme, ROW_CW01 + j // 2:ROW_CW01 + j // 2 + 1, :]
            update("conv_w", g, (slice(j, j + 1), slice(None)))
        for r0 in range(0, DG, 128):
            rs = (slice(r0, r0 + 128), slice(None))
            both = rg_ref[r0:r0 + 128, :]
            update("rg_wa", both[:, 0:RGB], rs)
            update("rg_wx", pltpu.roll(both, RGB, 1)[:, 0:RGB], rs)

    args = [vred, rgred] + [d[nm] for d in (ws, ms, vs) for nm in names]
    out_shape = []
    for nm in names:
        out_shape += [jax.ShapeDtypeStruct(ws[nm].shape, F32)] * 4
    whole = lambda s: pl.BlockSpec(s.shape, lambda i, nd=len(s.shape): (0,) * nd)
    outs = pl.pallas_call(
        body, name="adam_small", out_shape=out_shape, grid=(1,),
        in_specs=[whole(a) for a in args], out_specs=[whole(s) for s in out_shape],
    )(*args)
    return {nm: tuple(outs[4 * i:4 * i + 4]) for i, nm in enumerate(names)}


def _fwd_inproj(x, nw, w_in_b, dep):
    tm = 512

    def body(x_ref, nw_ref, w_ref, dep_ref, proj_ref, u_ref):
        xv = x_ref[...]
        s = lax.rsqrt(jnp.mean(xv * xv, axis=-1, keepdims=True) + EPS)
        u = (xv * s * nw_ref[...]).astype(BF16)
        u_ref[...] = u
        for k in range(NSHARD):
            proj_ref[:, k * SHW:(k + 1) * SHW] = jnp.dot(u, w_ref[k], preferred_element_type=F32)

    return pl.pallas_call(
        body, name="fwd_inproj", grid=(T // tm,),
        in_specs=[pl.BlockSpec((tm, D), lambda i: (i, 0)),
                  pl.BlockSpec((1, D), lambda i: (0, 0)),
                  pl.BlockSpec((NSHARD, D, SHW), lambda i: (0, 0, 0)), ANY_SPEC],
        out_specs=[pl.BlockSpec((tm, DIN), lambda i: (i, 0)),
                   pl.BlockSpec((tm, D), lambda i: (i, 0))],
        out_shape=[jax.ShapeDtypeStruct((T, DIN), F32), jax.ShapeDtypeStruct((T, D), BF16)],
        compiler_params=_cparams(("parallel",), 48),
    )(x, nw, w_in_b, dep)


def _conv_rows(cw_ref):
    return [jnp.concatenate([cw_ref[k, j:j + 1, :] for k in range(NSHARD)], axis=1) for j in range(4)]


def _rg_conv(xa, prev8, cw, cb, rows):
    taps = [_shift_down(xa, prev8, 3, rows), _shift_down(xa, prev8, 2, rows),
            _shift_down(xa, prev8, 1, rows), xa]
    xc = cb
    for j in range(4):
        xc = xc + taps[j] * cw[j]
    return xc, taps


def _block_mask():
    r = lax.broadcasted_iota(I32, (DG, DG), 0)
    c = lax.broadcasted_iota(I32, (DG, DG), 1)
    return (r >> 6) == (c >> 6)


def _dense_from_blocks(wc):
    j = lax.broadcasted_iota(I32, (RGB, DG), 0)
    c = lax.broadcasted_iota(I32, (RGB, DG), 1)
    spread = _mm_exact(wc, ((c & (RGB - 1)) == j).astype(F32))
    return jnp.where(_block_mask(), spread, 0.0)


def _blocks_from_dense(da, dx):
    c = lax.broadcasted_iota(I32, (DG, 128), 0)
    j = lax.broadcasted_iota(I32, (DG, 128), 1)
    hit = (c & (RGB - 1)) == (j & (RGB - 1))
    mask = _block_mask()
    return (_mm_exact(jnp.where(mask, da, 0.0), (hit & (j < RGB)).astype(F32))
            + _mm_exact(jnp.where(mask, dx, 0.0), (hit & (j >= RGB)).astype(F32)))


def _rg_gates(xc, wa, ba, wx, bx, sp, first_row):
    r = _sigmoid(_mm(xc, wa) + ba)
    i = _sigmoid(_mm(xc, wx) + bx)
    log_a = (-RG_C) * r * sp
    a = jnp.exp(log_a)
    a2 = a * a
    one_m_a2 = -jnp.tanh(log_a) * (a2 + 1.0)
    mult = jnp.where(first_row, 1.0, jnp.sqrt(one_m_a2))
    return r, i, a, a2, mult


def _softplus(z):
    return jnp.maximum(z, 0.0) + jnp.log1p(jnp.exp(-jnp.abs(z)))


def _fwd_rglru(proj, cw3, conv_b, wa_c, ba, wx_c, bx, lam):
    tm = 512
    ng = tm // 8

    def body(xa_ref, ga_ref, cw_ref, cb_ref, wa_ref, ba_ref, wx_ref, bx_ref, lam_ref,
             h_ref, ya_ref, a_s, u_s, tail_s, hc_s, wa_s, wx_s):
        i = pl.program_id(0)

        @pl.when(i == 0)
        def _():
            tail_s[...] = jnp.zeros_like(tail_s)
            hc_s[...] = jnp.zeros_like(hc_s)
            wa_s[...] = _dense_from_blocks(wa_ref[...]).astype(BF16)
            wx_s[...] = _dense_from_blocks(wx_ref[...]).astype(BF16)

        rows = lax.broadcasted_iota(I32, (tm, DG), 0)
        xa = xa_ref[...]
        xc, _ = _rg_conv(xa, tail_s[...], _conv_rows(cw_ref), cb_ref[...], rows)
        tail_s[...] = xa[tm - 8:tm, :]
        sp = _softplus(-lam_ref[...])
        first_row = (rows + i * tm) == 0
        r, ig, a, a2, mult = _rg_gates(xc, wa_s[...], ba_ref[...], wx_s[...], bx_ref[...], sp, first_row)
        av, uv = a, mult * (ig * xc)
        rows8 = rows & 7
        for d in (1, 2, 4):
            keep = rows8 >= d
            uv = uv + av * jnp.where(keep, _roll_in_groups(uv, d), 0.0)
            av = av * jnp.where(keep, _roll_in_groups(av, d), 1.0)
        a_s[...] = av
        u_s[...] = uv
        carry = hc_s[0:1, :]
        for g in range(ng):
            sl = slice(g * 8, (g + 1) * 8)
            ab, ub = a_s[sl, :], u_s[sl, :]
            h_ref[sl, :] = ub + ab * carry
            carry = ub[7:8, :] + ab[7:8, :] * carry
        hc_s[0:1, :] = carry
        ga = ga_ref[...]
        sg = _sigmoid(ga)
        ya_ref[...] = (h_ref[...] * (ga * sg)).astype(BF16)

    vec = lambda: pl.BlockSpec((1, DG), lambda i: (0, 0))
    blocks = lambda: pl.BlockSpec((DG, RGB), lambda i: (0, 0))
    return pl.pallas_call(
        body, name="fwd_rglru", grid=(T // tm,),
        in_specs=[pl.BlockSpec((tm, DG), lambda i: (i, 0)),
                  pl.BlockSpec((tm, DG), lambda i: (i, 1)),
                  pl.BlockSpec((NSHARD, 4, 128), lambda i: (0, 0, 0)), vec(),
                  blocks(), vec(), blocks(), vec(), vec()],
        out_specs=[pl.BlockSpec((tm, DG), lambda i: (i, 0)),
                   pl.BlockSpec((tm, DG), lambda i: (i, 0))],
        out_shape=[jax.ShapeDtypeStruct((T, DG), F32), jax.ShapeDtypeStruct((T, DG), BF16)],
        scratch_shapes=[pltpu.VMEM((tm, DG), F32), pltpu.VMEM((tm, DG), F32),
                        pltpu.VMEM((8, DG), F32), pltpu.VMEM((8, DG), F32),
                        pltpu.VMEM((DG, DG), BF16), pltpu.VMEM((DG, DG), BF16)],
        compiler_params=_cparams(("arbitrary",), 48),
    )(proj, proj, cw3, conv_b, wa_c, ba, wx_c, bx, lam)


def _hg_lower_bound(lb_ref):
    return _sig_pair(lb_ref[0:1, :] - lb_ref[1:2, :])


def _hg_gates(fz, lb, one_m_lb):
    sg, sn = _sig_pair(fz)
    f = lb + one_m_lb * sg
    return sg, sn, f, jnp.log(f), one_m_lb * sn


def _tri(lower):
    r = lax.broadcasted_iota(I32, (CH, CH), 0)
    c = lax.broadcasted_iota(I32, (CH, CH), 1)
    return (r >= c) if lower else (r <= c)


def _chunk_cumsum(v, rows64):
    for d in (1, 2, 4, 8, 16, 32):
        v = v + jnp.where(rows64 >= d, pltpu.roll(v, d, 0), 0.0)
    return v


def _chunk_rev_cumsum(v, rows64):
    n = v.shape[0]
    for d in (1, 2, 4, 8, 16, 32):
        v = v + jnp.where(rows64 < CH - d, pltpu.roll(v, n - d, 0), 0.0)
    return v


def _hg_recompute(q_ref, f_ref, lb, one_m_lb, rows64, eb_s, enb_s, ekd_s, qe_s, ke_s, kd_s, dec_s):
    nc = q_ref.shape[0] // CH
    sg, sn, f, logf, k = _hg_gates(f_ref[...], lb, one_m_lb)
    q = q_ref[...]
    sq = _sigmoid(q)
    qs = q * sq * (HD ** -0.5)
    b = _chunk_cumsum(logf, rows64)
    for c in range(nc):
        rs = slice(c * CH, (c + 1) * CH)
        b_c = b[rs]
        bl = b_c[CH - 1:CH, :]
        eb, enb, ekd = jnp.exp(b_c), jnp.exp(-b_c), jnp.exp(bl - b_c)
        if eb_s is not None:
            eb_s[rs, :] = eb
            enb_s[rs, :] = enb
            ekd_s[rs, :] = ekd
        qe_s[rs, :] = (qs[rs] * eb).astype(BF16)
        ke_s[rs, :] = (k[rs] * enb).astype(BF16)
        kd_s[rs, :] = (k[rs] * ekd).astype(BF16)
        dec_s[c:c + 1, :] = jnp.exp(bl)
    return sg, sn, f, k, q, sq, qs


def _fwd_hgrn2(proj, hg_lb, hg_nw):
    tm = 512
    nc = tm // CH

    def body(q_ref, f_ref, v_ref, g_ref, lb_ref, nw_ref, yb_ref, o_ref, sp_ref,
             st_s, qe_s, ke_s, kd_s, vb_s, dec_s, p_s, ds_s):
        i = pl.program_id(0)

        @pl.when(i == 0)
        def _():
            st_s[...] = jnp.zeros_like(st_s)

        lb, one_m_lb = _hg_lower_bound(lb_ref)
        rows64 = lax.broadcasted_iota(I32, (tm, DG), 0) & (CH - 1)
        _hg_recompute(q_ref, f_ref, lb, one_m_lb, rows64, None, None, None, qe_s, ke_s, kd_s, dec_s)
        vb_s[...] = v_ref[...].astype(BF16)
        mask = _tri(True)
        items = [(c, hd, slice(c * CH, (c + 1) * CH), slice(hd * HD, (hd + 1) * HD))
                 for c in range(nc) for hd in range(NH)]
        for c, hd, rs, cols in items:
            p_s[c * NH + hd] = jnp.where(mask, _mm_nt(qe_s[rs, cols], ke_s[rs, cols]), 0.0).astype(BF16)
            ds_s[c * NH + hd] = _mm_tn(vb_s[rs, cols], kd_s[rs, cols])
        for c, hd, rs, cols in items:
            st = st_s[hd]
            sp_ref[hd, c] = st
            st_s[hd] = st * dec_s[c:c + 1, cols] + ds_s[c * NH + hd]
        for c, hd, rs, cols in items:
            o_ref[rs, cols] = _mm(p_s[c * NH + hd], vb_s[rs, cols]) + _mm_nt(qe_s[rs, cols], sp_ref[hd, c])
        nw = nw_ref[...]
        for hd in range(NH):
            cols = slice(hd * HD, (hd + 1) * HD)
            o = o_ref[:, cols]
            so = lax.rsqrt(jnp.mean(o * o, axis=-1, keepdims=True) + EPS)
            g = g_ref[:, cols]
            sg = _sigmoid(g)
            yb_ref[:, cols] = (o * so * nw * (g * sg)).astype(BF16)

    col = lambda j: pl.BlockSpec((tm, DG), lambda i: (i, j))
    return pl.pallas_call(
        body, name="fwd_hgrn2", grid=(T // tm,),
        in_specs=[col(2), col(3), col(4), col(5),
                  pl.BlockSpec((2, DG), lambda i: (0, 0)),
                  pl.BlockSpec((1, HD), lambda i: (0, 0))],
        out_specs=[pl.BlockSpec((tm, DG), lambda i: (i, 0)),
                   pl.BlockSpec((tm, DG), lambda i: (i, 0)),
                   pl.BlockSpec((NH, nc, HD, HD), lambda i: (0, i, 0, 0))],
        out_shape=[jax.ShapeDtypeStruct((T, DG), BF16), jax.ShapeDtypeStruct((T, DG), F32),
                   jax.ShapeDtypeStruct((NH, NCHUNK, HD, HD), F32)],
        scratch_shapes=[pltpu.VMEM((NH, HD, HD), F32),
                        pltpu.VMEM((tm, DG), BF16), pltpu.VMEM((tm, DG), BF16), pltpu.VMEM((tm, DG), BF16),
                        pltpu.VMEM((tm, DG), BF16), pltpu.VMEM((nc, DG), F32),
                        pltpu.VMEM((nc * NH, CH, CH), BF16), pltpu.VMEM((nc * NH, HD, HD), F32)],
        compiler_params=_cparams(("arbitrary",), 48),
    )(proj, proj, proj, proj, hg_lb, hg_nw)


def _tail_fwd_bwd(x, p, tgt, ya, yb, w_out_b, w_pg_b, w_pp_b, ple_nw, b_pg, fnw):
    tm = 256
    nt = T // tm
    QR = D // NSHARD

    def body(x_ref, p_ref, t_ref, ya_ref, yb_ref, wo_ref, wg_ref, wp_ref, pw_ref, b_ref, fw_ref,
             dh1_ref, dyab_ref, dwo_ref, dwg_ref, dwp_ref, sm_ref, dwo_s, dwg_s, dwp_s):
        i = pl.program_id(0)

        @pl.when(i == 0)
        def _():
            dwo_s[...] = jnp.zeros_like(dwo_s)
            dwg_s[...] = jnp.zeros_like(dwg_s)
            dwp_s[...] = jnp.zeros_like(dwp_s)
            sm_ref[...] = jnp.zeros_like(sm_ref)

        ya = ya_ref[...]
        yb = yb_ref[...]
        pv = p_ref[...].astype(BF16)
        pw = pw_ref[...]
        fw = fw_ref[...]
        h1 = x_ref[...] + _mm(ya, wo_ref[0:DG, :]) + _mm(yb, wo_ref[DG:D, :])
        s2 = lax.rsqrt(jnp.mean(h1 * h1, axis=-1, keepdims=True) + EPS)
        n2h = h1 * s2
        n2 = (n2h * pw).astype(BF16)
        z = _mm(n2, wg_ref[...]) + b_ref[...]
        gate = _sigmoid(z)
        pp = jnp.concatenate([_mm(pv, wp_ref[k]) for k in range(NSHARD)], axis=1)
        h2 = h1 + gate * pp
        s3 = lax.rsqrt(jnp.mean(h2 * h2, axis=-1, keepdims=True) + EPS)
        hn = h2 * s3
        err = hn * fw - t_ref[...]
        sm_ref[0:1, :] += _rowsum(err * err)
        dy = err * (1.0 / D)
        sm_ref[1:2, :] += _rowsum(dy * hn)
        g3 = dy * fw
        dh2 = s3 * (g3 - hn * jnp.mean(g3 * hn, axis=-1, keepdims=True))
        dpp = (dh2 * gate).astype(BF16)
        dz = dh2 * pp * gate * (1.0 - gate)
        sm_ref[2:3, :] += _rowsum(dz)
        dzb = dz.astype(BF16)
        dwg_s[...] += _mm_tn(n2, dzb)
        dn2 = _mm_nt(dzb, wg_ref[...])
        for k in range(NSHARD):
            dwp_s[k] += _mm_tn(pv, dpp[:, k * PLE:(k + 1) * PLE])
        sm_ref[3:4, :] += _rowsum(dn2 * n2h)
        g2 = dn2 * pw
        dh1 = dh2 + s2 * (g2 - n2h * jnp.mean(g2 * n2h, axis=-1, keepdims=True))
        dh1_ref[...] = dh1
        dh1b = dh1.astype(BF16)
        dyab_ref[...] = _mm_nt(dh1b, wo_ref[...])
        dwo_s[0:DG, :] += _mm_tn(ya, dh1b)
        dwo_s[DG:D, :] += _mm_tn(yb, dh1b)

        @pl.when(i == nt - 1)
        def _():
            total = jnp.sum(sm_ref[0:1, :], axis=-1, keepdims=True) * (0.5 / D)
            sm_ref[4:5, :] = jnp.broadcast_to(total, (1, D))
            for k in range(NSHARD):
                dwo_ref[k] = dwo_s[k * QR:(k + 1) * QR, :].astype(BF16)
                dwg_ref[k] = dwg_s[k * QR:(k + 1) * QR, :].astype(BF16)
                dwp_ref[k] = dwp_s[k].astype(BF16)

    row = lambda w: pl.BlockSpec((tm, w), lambda i: (i, 0))
    const2 = lambda s: pl.BlockSpec(s, lambda i: (0, 0))
    const3 = lambda s: pl.BlockSpec(s, lambda i: (0, 0, 0))
    return pl.pallas_call(
        body, name="tail_fwd_bwd", grid=(nt,),
        in_specs=[row(D), row(PLE), row(D), row(DG), row(DG),
                  const2((D, D)), const2((D, D)), const3((NSHARD, PLE, PLE)),
                  const2((1, D)), const2((1, D)), const2((1, D))],
        out_specs=[row(D), row(D), const3((NSHARD, QR, D)), const3((NSHARD, QR, D)),
                   const3((NSHARD, PLE, PLE)), const2((8, D))],
        out_shape=[jax.ShapeDtypeStruct((T, D), F32), jax.ShapeDtypeStruct((T, D), F32),
                   jax.ShapeDtypeStruct((NSHARD, QR, D), BF16), jax.ShapeDtypeStruct((NSHARD, QR, D), BF16),
                   jax.ShapeDtypeStruct((NSHARD, PLE, PLE), BF16), jax.ShapeDtypeStruct((8, D), F32)],
        scratch_shapes=[pltpu.VMEM((D, D), F32), pltpu.VMEM((D, D), F32), pltpu.VMEM((NSHARD, PLE, PLE), F32)],
        compiler_params=_cparams(("arbitrary",), 56),
    )(x, p, tgt, ya, yb, w_out_b, w_pg_b, w_pp_b, ple_nw, b_pg, fnw)


def _bwd_rglru(proj, h, dyab, cw3, conv_b, wa_c, ba, wx_c, bx, lam, dep):
    tm = 512
    nt = T // tm
    ng = tm // 8

    def body(xa_ref, ga_ref, xp_ref, h_ref, hp_ref, dya_ref, cw_ref, cb_ref, wa_ref, ba_ref, wx_ref, bx_ref,
             lam_ref, dep_ref, da_ref, rg_ref, sm_ref, a_s, g_s, cg_s, nxt_s, wa_s, wx_s, dwa_s, dwx_s):
        i = pl.program_id(0)
        tile = nt - 1 - i

        @pl.when(i == 0)
        def _():
            dwa_s[...] = jnp.zeros_like(dwa_s)
            dwx_s[...] = jnp.zeros_like(dwx_s)
            sm_ref[...] = jnp.zeros_like(sm_ref)
            cg_s[...] = jnp.zeros_like(cg_s)
            nxt_s[...] = jnp.zeros_like(nxt_s)
            wa_s[...] = _dense_from_blocks(wa_ref[...]).astype(BF16)
            wx_s[...] = _dense_from_blocks(wx_ref[...]).astype(BF16)

        rows = lax.broadcasted_iota(I32, (tm, DG), 0)
        has_prev = tile > 0
        xa = xa_ref[...]
        xprev = jnp.where(has_prev, xp_ref[...], 0.0)
        cw = _conv_rows(cw_ref)
        xc, taps = _rg_conv(xa, xprev, cw, cb_ref[...], rows)
        lam_v = lam_ref[...]
        sp = _softplus(-lam_v)
        first_row = (rows + tile * tm) == 0
        r, ig, a, a2, mult = _rg_gates(xc, wa_s[...], ba_ref[...], wx_s[...], bx_ref[...], sp, first_row)
        hv = h_ref[...]
        hprev = jnp.where(has_prev, hp_ref[...], 0.0)
        h_m1 = _shift_down(hv, hprev, 1, rows)
        ga = ga_ref[...]
        sg = _sigmoid(ga)
        dya = dya_ref[...]
        dga = dya * hv * (sg * (1.0 + ga * (1.0 - sg)))

        av = jnp.where(rows == tm - 1, 1.0, pltpu.roll(a, tm - 1, 0))
        gv = dya * (ga * sg)
        rows8 = rows & 7
        for d in (1, 2, 4):
            keep = rows8 < 8 - d
            gv = gv + av * jnp.where(keep, _roll_in_groups(gv, 8 - d), 0.0)
            av = av * jnp.where(keep, _roll_in_groups(av, 8 - d), 1.0)
        a_s[...] = av
        g_s[...] = gv
        carry = cg_s[0:1, :]
        for g in range(ng - 1, -1, -1):
            sl = slice(g * 8, (g + 1) * 8)
            ab, gb = a_s[sl, :], g_s[sl, :]
            g_s[sl, :] = gb + ab * carry
            carry = gb[0:1, :] + ab[0:1, :] * carry
        cg_s[0:1, :] = a[0:1, :] * carry

        gt = g_s[...]
        da = gt * h_m1
        ixc = ig * xc
        di = gt * mult * xc
        dxc = gt * mult * ig
        dlog_a = da * a + jnp.where(first_row, 0.0, gt * ixc * (-a2 / mult))
        sm_ref[3:4, :] += _rowsum(dlog_a * ((-RG_C) * r))
        dpr_f = dlog_a * ((-RG_C) * sp) * r * (1.0 - r)
        dpi_f = di * ig * (1.0 - ig)
        sm_ref[1:2, :] += _rowsum(dpr_f)
        sm_ref[2:3, :] += _rowsum(dpi_f)
        dpr = dpr_f.astype(BF16)
        dpi = dpi_f.astype(BF16)
        xcb = xc.astype(BF16)
        dwa_s[...] += _mm_tn(xcb, dpr)
        dwx_s[...] += _mm_tn(xcb, dpi)
        dxc = dxc + _mm_nt(dpr, wa_s[...]) + _mm_nt(dpi, wx_s[...])
        sm_ref[0:1, :] += _rowsum(dxc)
        for j in range(4):
            sm_ref[4 + j:5 + j, :] += _rowsum(dxc * taps[j])
        nxt = nxt_s[...]
        dxa = (dxc * cw[3] + _shift_up(dxc, nxt, 1, rows) * cw[2]
               + _shift_up(dxc, nxt, 2, rows) * cw[1] + _shift_up(dxc, nxt, 3, rows) * cw[0])
        nxt_s[...] = dxc[0:8, :]
        da_ref[:, 0:DG] = dxa.astype(BF16)
        da_ref[:, DG:D] = dga.astype(BF16)

        @pl.when(i == nt - 1)
        def _():
            _, s_neg = _sig_pair(lam_v)
            sm_ref[3:4, :] = sm_ref[3:4, :] * (-s_neg)
            rg_ref[...] = _blocks_from_dense(dwa_s[...], dwx_s[...])

    vec = lambda: pl.BlockSpec((1, DG), lambda i: (0, 0))
    blocks = lambda: pl.BlockSpec((DG, RGB), lambda i: (0, 0))
    prev8 = lambda: pl.BlockSpec((8, DG), lambda i: (jnp.maximum((nt - 1 - i) * (tm // 8) - 1, 0), 0))
    return pl.pallas_call(
        body, name="bwd_rglru", grid=(nt,),
        in_specs=[pl.BlockSpec((tm, DG), lambda i: (nt - 1 - i, 0)),
                  pl.BlockSpec((tm, DG), lambda i: (nt - 1 - i, 1)),
                  prev8(),
                  pl.BlockSpec((tm, DG), lambda i: (nt - 1 - i, 0)),
                  prev8(),
                  pl.BlockSpec((tm, DG), lambda i: (nt - 1 - i, 0)),
                  pl.BlockSpec((NSHARD, 4, 128), lambda i: (0, 0, 0)), vec(),
                  blocks(), vec(), blocks(), vec(), vec(), ANY_SPEC],
        out_specs=[pl.BlockSpec((tm, D), lambda i: (nt - 1 - i, 0)),
                   pl.BlockSpec((DG, 128), lambda i: (0, 0)),
                   pl.BlockSpec((8, DG), lambda i: (0, 0))],
        out_shape=[jax.ShapeDtypeStruct((T, D), BF16), jax.ShapeDtypeStruct((DG, 128), F32),
                   jax.ShapeDtypeStruct((8, DG), F32)],
        scratch_shapes=[pltpu.VMEM((tm, DG), F32), pltpu.VMEM((tm, DG), F32),
                        pltpu.VMEM((8, DG), F32), pltpu.VMEM((8, DG), F32),
                        pltpu.VMEM((DG, DG), BF16), pltpu.VMEM((DG, DG), BF16),
                        pltpu.VMEM((DG, DG), F32), pltpu.VMEM((DG, DG), F32)],
        compiler_params=_cparams(("arbitrary",), 56),
    )(proj, proj, proj, h, h, dyab, cw3, conv_b, wa_c, ba, wx_c, bx, lam, dep)


def _bwd_hgrn2(proj, o, s_prev, dyab, hg_lb, hg_nw, dep):
    tm = 512
    nt = T // tm
    nc = tm // CH

    def body(q_ref, f_ref, v_ref, g_ref, o_ref, sp_ref, dy_ref, lb_ref, nw_ref, dep_ref, db_ref, sm_ref,
             dst_s, eb_s, enb_s, ekd_s, qe_s, ke_s, kd_s, vb_s, do_s, dec_s, ddec_s, p_s, dp_s,
             g_s, dsta_s, dva_s, dqe_s, dke_s, dkd_s, dlf_s):
        i = pl.program_id(0)

        @pl.when(i == 0)
        def _():
            sm_ref[...] = jnp.zeros_like(sm_ref)
            dst_s[...] = jnp.zeros_like(dst_s)

        lb, one_m_lb = _hg_lower_bound(lb_ref)
        rows64 = lax.broadcasted_iota(I32, (tm, DG), 0) & (CH - 1)
        sg, sn, f, k, q, sq, qs = _hg_recompute(
            q_ref, f_ref, lb, one_m_lb, rows64, eb_s, enb_s, ekd_s, qe_s, ke_s, kd_s, dec_s)
        vb_s[...] = v_ref[...].astype(BF16)

        nw = nw_ref[...]
        for hd in range(NH):
            cols = slice(hd * HD, (hd + 1) * HD)
            g = g_ref[:, cols]
            sgg = _sigmoid(g)
            o = o_ref[:, cols]
            so = lax.rsqrt(jnp.mean(o * o, axis=-1, keepdims=True) + EPS)
            oh = o * so
            dyb = dy_ref[:, cols]
            db_ref[:, 3 * DG + hd * HD:3 * DG + (hd + 1) * HD] = (
                dyb * (oh * nw) * (sgg * (1.0 + g * (1.0 - sgg)))).astype(BF16)
            don = dyb * (g * sgg)
            sm_ref[1:2, 0:HD] += _rowsum(don * oh)
            gw = don * nw
            do_s[:, cols] = (so * (gw - oh * jnp.mean(gw * oh, axis=-1, keepdims=True))).astype(BF16)

        mask = _tri(True)
        items = [(c, hd, slice(c * CH, (c + 1) * CH), slice(hd * HD, (hd + 1) * HD))
                 for c in range(nc) for hd in range(NH)]
        for c, hd, rs, cols in items:
            p_s[c * NH + hd] = jnp.where(mask, _mm_nt(qe_s[rs, cols], ke_s[rs, cols]), 0.0).astype(BF16)
            dp_s[c * NH + hd] = jnp.where(mask, _mm_nt(do_s[rs, cols], vb_s[rs, cols]), 0.0).astype(BF16)
        for c, hd, rs, cols in items:
            n = c * NH + hd
            dva_s[rs, cols] = _mm_tn(p_s[n], do_s[rs, cols])
            dqe_s[rs, cols] = _mm(dp_s[n], ke_s[rs, cols])
            dke_s[rs, cols] = _mm_tn(dp_s[n], qe_s[rs, cols])
            g_s[n] = _mm_tn(do_s[rs, cols], qe_s[rs, cols])
        for c, hd, rs, cols in reversed(items):
            n = c * NH + hd
            dst = dst_s[hd]
            dsta_s[n] = dst
            dst_s[hd] = dst * dec_s[c:c + 1, cols] + g_s[n]
        for c, hd, rs, cols in items:
            n = c * NH + hd
            dst = dsta_s[n]
            st_prev = sp_ref[hd, c]
            dv = dva_s[rs, cols] + _mm_nt(kd_s[rs, cols], dst)
            db_ref[rs, 2 * DG + hd * HD:2 * DG + (hd + 1) * HD] = dv.astype(BF16)
            dqe_s[rs, cols] += _mm(do_s[rs, cols], st_prev)
            dkd_s[rs, cols] = _mm(vb_s[rs, cols], dst)
            ddec_s[c:c + 1, cols] = _rowsum(dst * st_prev)

        eb, enb, ekd = eb_s[...], enb_s[...], ekd_s[...]
        dqe, dke, dkd = dqe_s[...], dke_s[...], dkd_s[...]
        t_kd = dkd * (k * ekd)
        rc = _chunk_rev_cumsum(dqe * (qs * eb) - dke * (k * enb) - t_kd, rows64)
        for c in range(nc):
            rs = slice(c * CH, (c + 1) * CH)
            dbl = _rowsum(t_kd[rs]) + ddec_s[c:c + 1, :] * dec_s[c:c + 1, :]
            dlf_s[rs, :] = rc[rs] + dbl
        t = dlf_s[...] / f - (dke * enb + dkd * ekd)
        db_ref[:, DG:2 * DG] = (one_m_lb * sg * sn * t).astype(BF16)
        sm_ref[0:1, :] += _rowsum(sn * t)
        db_ref[:, 0:DG] = (dqe * eb * (sq * (1.0 + q * (1.0 - sq))) * (HD ** -0.5)).astype(BF16)

        @pl.when(i == nt - 1)
        def _():
            dsm = sm_ref[0:1, :] * (lb * one_m_lb)
            sm_ref[2:3, :] = dsm
            sm_ref[3:4, :] = -dsm

    col = lambda j: pl.BlockSpec((tm, DG), lambda i: (nt - 1 - i, j))
    big = lambda dt: pltpu.VMEM((tm, DG), dt)
    return pl.pallas_call(
        body, name="bwd_hgrn2", grid=(nt,),
        in_specs=[col(2), col(3), col(4), col(5),
                  pl.BlockSpec((tm, DG), lambda i: (nt - 1 - i, 0)),
                  pl.BlockSpec((NH, nc, HD, HD), lambda i: (0, nt - 1 - i, 0, 0)),
                  pl.BlockSpec((tm, DG), lambda i: (nt - 1 - i, 1)),
                  pl.BlockSpec((2, DG), lambda i: (0, 0)),
                  pl.BlockSpec((1, HD), lambda i: (0, 0)), ANY_SPEC],
        out_specs=[pl.BlockSpec((tm, 4 * DG), lambda i: (nt - 1 - i, 0)),
                   pl.BlockSpec((8, DG), lambda i: (0, 0))],
        out_shape=[jax.ShapeDtypeStruct((T, 4 * DG), BF16), jax.ShapeDtypeStruct((8, DG), F32)],
        scratch_shapes=[pltpu.VMEM((NH, HD, HD), F32),
                        big(F32), big(F32), big(F32),
                        big(BF16), big(BF16), big(BF16), big(BF16), big(BF16),
                        pltpu.VMEM((nc, DG), F32), pltpu.VMEM((nc, DG), F32),
                        pltpu.VMEM((nc * NH, CH, CH), BF16), pltpu.VMEM((nc * NH, CH, CH), BF16),
                        pltpu.VMEM((nc * NH, HD, HD), F32), pltpu.VMEM((nc * NH, HD, HD), F32),
                        big(F32), big(F32), big(F32), big(F32), big(F32)],
        compiler_params=_cparams(("arbitrary",), 56),
    )(proj, proj, proj, proj, o, s_prev, dyab, hg_lb, hg_nw, dep)


def _dproj_pieces(k, da_ref, db_ref):
    if k == 0:
        return [(da_ref[:, 0:SHW], 0)]
    if k == 1:
        return [(da_ref[:, SHW:D], 0), (db_ref[:, 0:DG], D - SHW)]
    if k == 2:
        return [(db_ref[:, DG:DG + SHW], 0)]
    return [(db_ref[:, DG + SHW:4 * DG], 0)]


def _bwd_inproj_dx(x, dh1, d_a, d_b, w_in_b, nw, dep):
    tm = 512
    nt = T // tm

    def body(x_ref, dh1_ref, da_ref, db_ref, w_ref, nw_ref, dep_ref, dx_ref, sm_ref):
        i = pl.program_id(0)

        @pl.when(i == 0)
        def _():
            sm_ref[...] = jnp.zeros_like(sm_ref)

        du = None
        for k in range(NSHARD):
            for val, off in _dproj_pieces(k, da_ref, db_ref):
                t = _mm_nt(val, w_ref[k, :, off:off + val.shape[1]])
                du = t if du is None else du + t
        xv = x_ref[...]
        s = lax.rsqrt(jnp.mean(xv * xv, axis=-1, keepdims=True) + EPS)
        xh = xv * s
        sm_ref[0:1, :] += _rowsum(du * xh)
        g = du * nw_ref[...]
        dx_ref[...] = dh1_ref[...] + s * (g - xh * jnp.mean(g * xh, axis=-1, keepdims=True))

    row = lambda w: pl.BlockSpec((tm, w), lambda i: (i, 0))
    return pl.pallas_call(
        body, name="bwd_inproj_dx", grid=(nt,),
        in_specs=[row(D), row(D), row(D), row(4 * DG),
                  pl.BlockSpec((NSHARD, D, SHW), lambda i: (0, 0, 0)),
                  pl.BlockSpec((1, D), lambda i: (0, 0)), ANY_SPEC],
        out_specs=[row(D), pl.BlockSpec((8, D), lambda i: (0, 0))],
        out_shape=[jax.ShapeDtypeStruct((T, D), F32), jax.ShapeDtypeStruct((8, D), F32)],
        compiler_params=_cparams(("arbitrary",), 56),
    )(x, dh1, d_a, d_b, w_in_b, nw, dep)


def _bwd_inproj_dw(name, ks, u_b, d_a, d_b):
    tm = 512
    nt = T // tm
    nk = len(ks)
    H = D // 2
    step = 128
    need_a = 0 in ks or 1 in ks

    def body(*refs):
        if need_a:
            u_ref, da_ref, db_ref, dw_ref, acc, send_s, recv_s, ssem, rsem = refs
        else:
            u_ref, db_ref, dw_ref, acc, send_s, recv_s, ssem, rsem = refs
            da_ref = None
        i = pl.program_id(0)

        @pl.when(i == 0)
        def _():
            acc[...] = jnp.zeros_like(acc)

        u = u_ref[...]
        for j, k in enumerate(ks):
            for val, off in _dproj_pieces(k, da_ref, db_ref):
                acc[j, :, off:off + val.shape[1]] += _mm_tn(u, val)

        @pl.when(i == nt - 1)
        def _():
            x, y, c = lax.axis_index("x"), lax.axis_index("y"), lax.axis_index("c")
            sibling = (x, y, 1 - c)
            mine0 = pl.multiple_of(c * H, step)
            other0 = pl.multiple_of((1 - c) * H, step)
            copies = []
            for j in range(nk):
                for r0 in range(0, H, step):
                    send_s[j, r0:r0 + step, :] = acc[j, pl.ds(other0 + r0, step), :].astype(BF16)
                cp = _remote(send_s.at[j], recv_s.at[j], ssem.at[j], rsem.at[j], sibling)
                cp.start()
                copies.append(cp)
            for j in range(nk):
                copies[j].wait_recv()
                for r0 in range(0, H, step):
                    s = acc[j, pl.ds(mine0 + r0, step), :] + recv_s[j, r0:r0 + step, :].astype(F32)
                    dw_ref[j, r0:r0 + step, :] = s.astype(BF16)
            for cp in copies:
                cp.wait_send()

    row = lambda w: pl.BlockSpec((tm, w), lambda i: (i, 0))
    ins = [u_b] + ([d_a] if need_a else []) + [d_b]
    in_specs = [row(D)] + ([row(D)] if need_a else []) + [row(4 * DG)]
    return pl.pallas_call(
        body, name=name, grid=(nt,), in_specs=in_specs,
        out_specs=pl.BlockSpec((nk, H, SHW), lambda i: (0, 0, 0)),
        out_shape=jax.ShapeDtypeStruct((nk, H, SHW), BF16),
        scratch_shapes=[pltpu.VMEM((nk, D, SHW), F32), pltpu.VMEM((nk, H, SHW), BF16),
                        pltpu.VMEM((nk, H, SHW), BF16),
                        pltpu.SemaphoreType.DMA((nk,)), pltpu.SemaphoreType.DMA((nk,))],
        compiler_params=_cparams(("arbitrary",), 48),
    )(*ins)


_OUT_ORDER = ["norm_mix_w", "w_in", "conv_w", "conv_b", "rg_wa", "rg_ba", "rg_wx", "rg_bx", "rg_lambda", "hg_lb",
              "hg_norm_w", "w_out", "ple_norm_w", "w_ple_gate", "b_ple_gate", "w_ple_proj", "final_norm_w"]
_BIG = ["w_in", "w_out", "w_ple_gate", "w_ple_proj"]


def _small_view(name, a):
    if name in ("rg_wa", "rg_wx"):
        return a.reshape(DG, RGB)
    if name == "conv_w":
        return a.reshape(4, 128)
    if name == "final_norm_w":
        return a.reshape(1, D)
    return a


def _landing(rows, cols):
    return lax.empty((NDEV, rows, cols), BF16)


def kernel(x, p, norm_mix_w, w_in, conv_w, conv_b, rg_wa, rg_ba, rg_wx, rg_bx, rg_lambda, hg_lb, hg_norm_w, w_out, ple_norm_w, w_ple_gate, b_ple_gate, w_ple_proj, final_norm_w, loss_target, m_norm_mix_w, m_w_in, m_conv_w, m_conv_b, m_rg_wa, m_rg_ba, m_rg_wx, m_rg_bx, m_rg_lambda, m_hg_lb, m_hg_norm_w, m_w_out, m_ple_norm_w, m_w_ple_gate, m_b_ple_gate, m_w_ple_proj, m_final_norm_w, v_norm_mix_w, v_w_in, v_conv_w, v_conv_b, v_rg_wa, v_rg_ba, v_rg_wx, v_rg_bx, v_rg_lambda, v_hg_lb, v_hg_norm_w, v_w_out, v_ple_norm_w, v_w_ple_gate, v_b_ple_gate, v_w_ple_proj, v_final_norm_w):
    given = dict(locals())
    x2, p2, tgt = x[0], p[0, 0], loss_target[0]
    wa_c, wx_c = _small_view("rg_wa", rg_wa), _small_view("rg_wx", rg_wx)

    w_in_b, l_out, l_pg, l_pp, cw3 = _gather_w_in(w_in[0], w_out[0], w_ple_gate[0], w_ple_proj[0], conv_w[0])
    g_ssem, g_rsem, g_lands, tok = _gather_rest_start([l_out, l_pg, l_pp])

    proj, u_b = _fwd_inproj(x2, norm_mix_w, w_in_b, tok)
    h, ya = _fwd_rglru(proj, cw3, conv_b, wa_c, rg_ba, wx_c, rg_bx, rg_lambda)
    yb, o, s_prev = _fwd_hgrn2(proj, hg_lb, hg_norm_w)
    w_out_b, w_pg_b, w_pp_b = _gather_rest_wait(g_ssem, g_rsem, g_lands, yb)
    dh1, dyab, dwo_b, dwg_b, dwp_b, sm_tail = _tail_fwd_bwd(
        x2, p2, tgt, ya, yb, w_out_b.reshape(D, D), w_pg_b.reshape(D, D), w_pp_b,
        ple_norm_w, b_ple_gate, final_norm_w.reshape(1, D))

    QH = D // NSHARD // 2
    r1 = _rs_start("rs_start_tail", [dwo_b, dwg_b, dwp_b], (0, 1, 2, 3),
                   [_landing(QH, D), _landing(QH, D), _landing(PLE // 2, PLE)])
    d_b, sm_b = _bwd_hgrn2(proj, o, s_prev, dyab, hg_lb, hg_norm_w, r1[4])
    dw23 = _bwd_inproj_dw("bwd_inproj_dw23", (2, 3), u_b, None, d_b)
    r2 = _rs_start("rs_start_in23", [dw23], (2, 3), [lax.empty((NSHARD, D // 2, SHW), BF16)], chip_sums=True)
    d_a, rg_c, sm_a = _bwd_rglru(proj, h, dyab, cw3, conv_b, wa_c, rg_ba, wx_c, rg_bx, rg_lambda, r2[4])
    dw01 = _bwd_inproj_dw("bwd_inproj_dw01", (0, 1), u_b, d_a, d_b)
    r3 = _rs_start("rs_start_in01", [dw01], (0, 1), r2[3], chip_sums=True)
    grad_x, sm_in = _bwd_inproj_dx(x2, dh1, d_a, d_b, w_in_b, norm_mix_w, r3[4])

    vred, rgred = _small_allreduce(sm_in, sm_tail, sm_a, sm_b, rg_c)
    parts1, lands1 = _rs_wait("rs_wait_tail", r1[0], r1[1], r1[2], (0, 1, 2, 3), r1[3], vred)
    parts2, lands2 = _rs_wait("rs_wait_in23", r2[0], r2[1], r2[2], (2, 3), r3[3], vred, chip_sums=True)
    parts3, lands3 = _rs_wait("rs_wait_in01", r3[0], r3[1], r3[2], (0, 1), lands2, vred, chip_sums=True)
    g_big = _reduce_exchange([parts3[0], parts2[0]] + parts1, lands3 + lands1)

    upd_big = _adam_big(g_big, [given[n][0] for n in _BIG], [given["m_" + n][0] for n in _BIG],
                        [given["v_" + n][0] for n in _BIG])
    small = _adam_small(vred, rgred,
                        {n: _small_view(n, given[n]) for n in _SMALL_ORDER},
                        {n: _small_view(n, given["m_" + n]) for n in _SMALL_ORDER},
                        {n: _small_view(n, given["v_" + n]) for n in _SMALL_ORDER})

    loss = vred[0, ROW_LOSS, 0]
    outs = [loss, grad_x[None]]
    for ki in range(4):
        for n in _OUT_ORDER:
            if n in _BIG:
                i = _BIG.index(n)
                a = g_big[i] if ki == 0 else upd_big[i][ki - 1]
                outs.append(a[None])
            else:
                outs.append(small[n][ki].reshape(given[n].shape))
    return tuple(outs)
```

```python
import jax
import jax.numpy as jnp
from jax import lax
from jax.experimental import pallas as pl
from jax.experimental.pallas import tpu as pltpu

F32 = jnp.float32
BF16 = jnp.bfloat16
I32 = jnp.int32
MESH = pl.DeviceIdType.MESH
HIGHEST = lax.Precision.HIGHEST

T = 4096
D = 1024
DG = 512
DIN = 3072
PLE = 256
NH = 4
HD = 128
CH = 64
NCHUNK = T // CH
RGB = 64
EPS = 1e-6
RG_C = 8.0
NSHARD = 4
SHW = DIN // NSHARD
NDEV = 8

ADAM_LR = 0.001
ADAM_B1 = 0.9
ADAM_B2 = 0.999
ADAM_EPS = 1e-08
ADAM_WD = 0.01
ADAM_STEP = 10

VMEM_SPEC = pl.BlockSpec(memory_space=pltpu.VMEM)
HBM_SPEC = pl.BlockSpec(memory_space=pltpu.HBM)
SEM_SPEC = pl.BlockSpec(memory_space=pltpu.SEMAPHORE)
ANY_SPEC = pl.BlockSpec(memory_space=pl.ANY)
EFFECT = pltpu.SideEffectType.DATAFLOW_SIDE_EFFECTING
MIB = 1024 * 1024

VROWS = 16
ROW_NORM_MIX, ROW_FINAL_NORM, ROW_B_PG, ROW_PLE_NORM = 0, 1, 2, 3
ROW_CB_BA, ROW_BX_LAM, ROW_CW01, ROW_CW23, ROW_HG_LB, ROW_HG_NW, ROW_LOSS = 4, 5, 6, 7, 8, 9, 10


def _mm(a, b):
    return jnp.dot(a.astype(BF16), b.astype(BF16), preferred_element_type=F32)


def _mm_nt(a, b):
    return lax.dot_general(a.astype(BF16), b.astype(BF16), (((1,), (1,)), ((), ())),
                           preferred_element_type=F32)


def _mm_tn(a, b):
    return lax.dot_general(a.astype(BF16), b.astype(BF16), (((0,), (0,)), ((), ())),
                           preferred_element_type=F32)


def _mm_exact(a, b):
    return jnp.dot(a, b, precision=HIGHEST, preferred_element_type=F32)


def _sig_pair(x):
    e = jnp.exp(-jnp.abs(x))
    big = 1.0 / (1.0 + e)
    small = e * big
    pos = x >= 0
    return jnp.where(pos, big, small), jnp.where(pos, small, big)


def _sigmoid(x):
    return 1.0 / (1.0 + jnp.exp(-x))


def _rowsum(v):
    return jnp.sum(v, axis=0, keepdims=True)


def _shift_down(cur, prev8, d, rows):
    rolled = pltpu.roll(cur, d, 0)
    head = jnp.where(rows[0:8] < d, pltpu.roll(prev8, d, 0), rolled[0:8])
    return jnp.concatenate([head, rolled[8:]], axis=0)


def _shift_up(cur, next8, d, rows):
    n = cur.shape[0]
    rolled = pltpu.roll(cur, n - d, 0)
    tail = jnp.where(rows[0:8] >= 8 - d, pltpu.roll(next8, 8 - d, 0), rolled[n - 8:n])
    return jnp.concatenate([rolled[0:n - 8], tail], axis=0)


def _roll_in_groups(v, d):
    n, w = v.shape
    return pltpu.roll(v.reshape(n // 8, 8, w), d, 1).reshape(n, w)


def _cparams(sem, vmem_mib):
    return pltpu.CompilerParams(dimension_semantics=sem, vmem_limit_bytes=vmem_mib * MIB)


def _mesh_pos():
    x, y, c = lax.axis_index("x"), lax.axis_index("y"), lax.axis_index("c")
    chips = [(1 - x, y), (x, 1 - y), (1 - x, 1 - y)]
    return x, y, c, chips


def _remote(src, dst, ssem, rsem, dev):
    return pltpu.make_async_remote_copy(src_ref=src, dst_ref=dst, send_sem=ssem, recv_sem=rsem,
                                        device_id=dev, device_id_type=MESH)


def _gather_w_in(w_in, w_out, w_pg, w_pp, conv_w):
    shapes = [w_in.shape, w_out.shape, w_pg.shape, w_pp.shape]

    def body(win, wout, wpg, wpp, cw, o_in, o_out, o_pg, o_pp, o_cw, ssem, rsem):
        x, y, c, chips = _mesh_pos()
        kme = 2 * x + y
        sibling = (x, y, 1 - c)
        for src, dst in [(win, o_in), (wout, o_out), (wpg, o_pg), (wpp, o_pp)]:
            for r0 in range(0, src.shape[0], 128):
                dst[kme, r0:r0 + 128, :] = src[r0:r0 + 128, :].astype(BF16)
        o_cw[kme] = cw[...]

        hrows = D // 2
        mine = pl.ds(pl.multiple_of(c * hrows, 128), hrows)
        other = pl.ds(pl.multiple_of((1 - c) * hrows, 128), hrows)
        sends = []
        for j, (px, py) in enumerate(chips):
            cp = _remote(o_in.at[kme, mine], o_in.at[kme, mine], ssem.at[j], rsem.at[j], (px, py, c))
            cp.start()
            sends.append(cp)
            cp = _remote(o_cw.at[kme], o_cw.at[kme], ssem.at[3 + j], rsem.at[3 + j], (px, py, c))
            cp.start()
            sends.append(cp)
        for j, (px, py) in enumerate(chips):
            kj = 2 * px + py
            _remote(o_in.at[kj, mine], o_in.at[kj, mine], ssem.at[j], rsem.at[j], (px, py, c)).wait_recv()
            cp = _remote(o_in.at[kj, mine], o_in.at[kj, mine], ssem.at[6 + j], rsem.at[6 + j], sibling)
            cp.start()
            sends.append(cp)
        for j, (px, py) in enumerate(chips):
            kj = 2 * px + py
            _remote(o_cw.at[kj], o_cw.at[kj], ssem.at[3 + j], rsem.at[3 + j], (px, py, c)).wait_recv()
            _remote(o_in.at[kj, other], o_in.at[kj, other], ssem.at[6 + j], rsem.at[6 + j], sibling).wait_recv()
        for cp in sends:
            cp.wait_send()

    out_shape = [jax.ShapeDtypeStruct((NSHARD,) + s, BF16) for s in shapes]
    out_shape.append(jax.ShapeDtypeStruct((NSHARD,) + conv_w.shape, F32))
    return pl.pallas_call(
        body, name="gather_w_in", out_shape=out_shape,
        in_specs=[VMEM_SPEC] * 5, out_specs=[VMEM_SPEC] * 5,
        scratch_shapes=[pltpu.SemaphoreType.DMA((9,)), pltpu.SemaphoreType.DMA((9,))],
        compiler_params=pltpu.CompilerParams(vmem_limit_bytes=40 * MIB),
    )(w_in, w_out, w_pg, w_pp, conv_w)


def _gather_rest_start(lands):
    n = len(lands)

    def body(*refs):
        land_in = refs[0:n]
        ssem, rsem = refs[n], refs[n + 1]
        token = refs[2 * n + 2]
        x, y, c, chips = _mesh_pos()
        kme = 2 * x + y
        for p, land in enumerate(land_in):
            hrows = land.shape[1] // 2
            mine = pl.ds(pl.multiple_of(c * hrows, 128), hrows)
            for px, py in chips:
                for pc in range(2):
                    _remote(land.at[kme, mine], land.at[kme, mine], ssem.at[p], rsem.at[p], (px, py, pc)).start()
        token[...] = jnp.zeros_like(token)

    out_shape = ([pltpu.SemaphoreType.DMA((n,)), pltpu.SemaphoreType.DMA((n,))]
                 + [pltpu.HBM(a.shape, a.dtype) for a in lands] + [jax.ShapeDtypeStruct((8, 128), F32)])
    outs = pl.pallas_call(
        body, name="gather_rest_start", out_shape=out_shape,
        in_specs=[HBM_SPEC] * n, out_specs=[SEM_SPEC, SEM_SPEC] + [HBM_SPEC] * n + [VMEM_SPEC],
        input_output_aliases={i: 2 + i for i in range(n)},
        compiler_params=pltpu.CompilerParams(has_side_effects=EFFECT),
    )(*[pltpu.with_memory_space_constraint(a, pltpu.HBM) for a in lands])
    return outs[0], outs[1], list(outs[2:2 + n]), outs[2 + n]


def _gather_rest_wait(ssem, rsem, lands, after):
    n = len(lands)

    def body(*refs):
        land_in = refs[0:n]
        ssem_ref, rsem_ref = refs[n], refs[n + 1]
        x, y, c = lax.axis_index("x"), lax.axis_index("y"), lax.axis_index("c")
        for p, land in enumerate(land_in):
            three = land.at[pl.ds(0, 3)]
            cp = _remote(three, three, ssem_ref.at[p], rsem_ref.at[p], (x, y, c))
            cp.wait_send()
            cp.wait_recv()

    outs = pl.pallas_call(
        body, name="gather_rest_wait", out_shape=[pltpu.HBM(a.shape, a.dtype) for a in lands],
        in_specs=[HBM_SPEC] * n + [SEM_SPEC, SEM_SPEC, ANY_SPEC], out_specs=[HBM_SPEC] * n,
        input_output_aliases={i: i for i in range(n)},
        compiler_params=pltpu.CompilerParams(has_side_effects=EFFECT),
    )(*lands, ssem, rsem, after)
    return list(outs)


def _rs_start(name, parts, ks, lands, chip_sums=False):
    n = len(parts)

    def body(*refs):
        part_in, land_in = refs[0:n], refs[n:2 * n]
        ssem, rsem = refs[2 * n], refs[2 * n + 1]
        token = refs[4 * n + 2]
        x, y, c = lax.axis_index("x"), lax.axis_index("y"), lax.axis_index("c")
        kme = 2 * x + y
        me = 4 * x + 2 * y + c
        for p in range(n):
            hrows = land_in[p].shape[1]
            for i, k in enumerate(ks):
                if chip_sums:
                    @pl.when(kme != k)
                    def _():
                        _remote(part_in[p].at[i], land_in[p].at[kme], ssem.at[p], rsem.at[p], (k // 2, k % 2, c)).start()
                    continue
                for pc in range(2):
                    @pl.when(jnp.logical_or(kme != k, c != pc))
                    def _():
                        _remote(part_in[p].at[i, pl.ds(pc * hrows, hrows)], land_in[p].at[me],
                                ssem.at[p], rsem.at[p], (k // 2, k % 2, pc)).start()
        token[...] = jnp.zeros_like(token)

    arrays = list(parts) + list(lands)
    out_shape = ([pltpu.SemaphoreType.DMA((n,)), pltpu.SemaphoreType.DMA((n,))]
                 + [pltpu.HBM(a.shape, a.dtype) for a in arrays] + [jax.ShapeDtypeStruct((8, 128), F32)])
    outs = pl.pallas_call(
        body, name=name, out_shape=out_shape,
        in_specs=[HBM_SPEC] * (2 * n), out_specs=[SEM_SPEC, SEM_SPEC] + [HBM_SPEC] * (2 * n) + [VMEM_SPEC],
        input_output_aliases={i: 2 + i for i in range(2 * n)},
        compiler_params=pltpu.CompilerParams(has_side_effects=EFFECT),
    )(*[pltpu.with_memory_space_constraint(a, pltpu.HBM) for a in arrays])
    return outs[0], outs[1], list(outs[2:2 + n]), list(outs[2 + n:2 + 2 * n]), outs[2 + 2 * n]


def _rs_wait(name, ssem, rsem, parts, ks, lands, after, chip_sums=False):
    n = len(parts)

    def body(*refs):
        part_in, land_in = refs[0:n], refs[n:2 * n]
        ssem_ref, rsem_ref = refs[2 * n], refs[2 * n + 1]
        x, y, c = lax.axis_index("x"), lax.axis_index("y"), lax.axis_index("c")
        kme = 2 * x + y
        for p in range(n):
            piece = land_in[p].at[0]
            for k in ks:
                for pc in range(1 if chip_sums else 2):
                    mine = (kme == k) if chip_sums else jnp.logical_and(kme == k, c == pc)

                    @pl.when(jnp.logical_not(mine))
                    def _():
                        _remote(piece, piece, ssem_ref.at[p], rsem_ref.at[p], (x, y, c)).wait_send()
            owner = kme == ks[0]
            for k in ks[1:]:
                owner = jnp.logical_or(owner, kme == k)

            @pl.when(owner)
            def _():
                others = land_in[p].at[pl.ds(0, land_in[p].shape[0] - 1)]
                _remote(others, others, ssem_ref.at[p], rsem_ref.at[p], (x, y, c)).wait_recv()

    arrays = list(parts) + list(lands)
    outs = pl.pallas_call(
        body, name=name, out_shape=[pltpu.HBM(a.shape, a.dtype) for a in arrays],
        in_specs=[HBM_SPEC] * (2 * n) + [SEM_SPEC, SEM_SPEC, ANY_SPEC], out_specs=[HBM_SPEC] * (2 * n),
        input_output_aliases={i: i for i in range(2 * n)},
        compiler_params=pltpu.CompilerParams(has_side_effects=EFFECT),
    )(*arrays, ssem, rsem, after)
    return list(outs[0:n]), list(outs[n:2 * n])


def _reduce_exchange(parts, lands):
    shapes = [(2 * l.shape[1], l.shape[2]) for l in lands]
    step = 128

    def body(in01, in23, pout, ppg, ppp, l_in, l_out, l_pg, l_pp, g_in, g_out, g_pg, g_pp, ssem, rsem):
        x, y, c = lax.axis_index("x"), lax.axis_index("y"), lax.axis_index("c")
        kme = 2 * x + y
        me = 4 * x + 2 * y + c
        sibling = (x, y, 1 - c)
        sends = []
        for p, (land, gout) in enumerate([(l_in, g_in), (l_out, g_out), (l_pg, g_pg), (l_pp, g_pp)]):
            hrows = land.shape[1]
            mine0 = pl.multiple_of(c * hrows, step)
            for r0 in range(0, hrows, step):
                rs = pl.ds(mine0 + r0, step)
                if p == 0:
                    own = jnp.where(kme >= 2, in23[kme & 1, r0:r0 + step, :], in01[kme & 1, r0:r0 + step, :])
                    slot = kme
                else:
                    own = (pout, ppg, ppp)[p - 1][kme, rs, :]
                    slot = me
                s = jnp.zeros((step, land.shape[2]), F32)
                for j in range(land.shape[0]):
                    s = s + jnp.where(slot == j, own, land[j, r0:r0 + step, :]).astype(F32)
                gout[rs, :] = s
            mine = pl.ds(mine0, hrows)
            cp = _remote(gout.at[mine], gout.at[mine], ssem.at[p], rsem.at[p], sibling)
            cp.start()
            sends.append(cp)
        for p, (land, gout) in enumerate([(l_in, g_in), (l_out, g_out), (l_pg, g_pg), (l_pp, g_pp)]):
            hrows = land.shape[1]
            other = pl.ds(pl.multiple_of((1 - c) * hrows, step), hrows)
            _remote(gout.at[other], gout.at[other], ssem.at[p], rsem.at[p], sibling).wait_recv()
        for cp in sends:
            cp.wait_send()

    return pl.pallas_call(
        body, name="reduce_exchange", out_shape=[jax.ShapeDtypeStruct(s, F32) for s in shapes],
        in_specs=[VMEM_SPEC] * 9, out_specs=[VMEM_SPEC] * 4,
        scratch_shapes=[pltpu.SemaphoreType.DMA((4,)), pltpu.SemaphoreType.DMA((4,))],
        compiler_params=pltpu.CompilerParams(vmem_limit_bytes=48 * MIB),
    )(*parts, *lands)


def _small_allreduce(sm_in, sm_tail, sm_a, sm_b, rg_c):
    PR = DG // NDEV

    def body(in_ref, tail_ref, a_ref, b_ref, rg_ref, v_out, rg_out, vbuf, vrecv, rgrecv, ssem, rsem):
        x, y, c = lax.axis_index("x"), lax.axis_index("y"), lax.axis_index("c")
        me = 4 * x + 2 * y + c

        def pair(ref, r0, r1):
            return jnp.concatenate([ref[r0:r0 + 1, :], ref[r1:r1 + 1, :]], axis=1)

        rows = {
            ROW_NORM_MIX: in_ref[0:1, :], ROW_FINAL_NORM: tail_ref[1:2, :], ROW_B_PG: tail_ref[2:3, :],
            ROW_PLE_NORM: tail_ref[3:4, :], ROW_CB_BA: pair(a_ref, 0, 1), ROW_BX_LAM: pair(a_ref, 2, 3),
            ROW_CW01: pair(a_ref, 4, 5), ROW_CW23: pair(a_ref, 6, 7), ROW_HG_LB: pair(b_ref, 2, 3),
            ROW_HG_NW: jnp.concatenate([b_ref[1:2, :], jnp.zeros((1, DG), F32)], axis=1),
            ROW_LOSS: tail_ref[4:5, :],
        }
        vbuf[...] = jnp.zeros_like(vbuf)
        for r, row in rows.items():
            for j in range(NDEV):
                vbuf[j, r:r + 1, :] = row[:, j * 128:(j + 1) * 128]

        def peer(mask):
            px = x ^ ((mask >> 2) & 1)
            py = y ^ ((mask >> 1) & 1)
            pc = c ^ (mask & 1)
            return (px, py, pc), 4 * px + 2 * py + pc

        def rg_rows(r):
            return pl.ds(pl.multiple_of(r * PR, PR), PR)

        first = []
        for mask in range(1, NDEV):
            dev, r = peer(mask)
            i = mask - 1
            cp = _remote(vbuf.at[r], vrecv.at[i], ssem.at[i], rsem.at[i], dev)
            cp.start()
            first.append(cp)
            cp = _remote(rg_ref.at[rg_rows(r)], rgrecv.at[i], ssem.at[7 + i], rsem.at[7 + i], dev)
            cp.start()
            first.append(cp)
        sv = vbuf[me]
        sr = rg_ref[rg_rows(me), :]
        for i in range(NDEV - 1):
            first[2 * i].wait_recv()
            first[2 * i + 1].wait_recv()
            sv = sv + vrecv[i]
            sr = sr + rgrecv[i]
        v_out[me] = sv
        rg_out[rg_rows(me), :] = sr
        second = []
        for mask in range(1, NDEV):
            dev, r = peer(mask)
            i = mask - 1
            cp = _remote(v_out.at[me], v_out.at[me], ssem.at[14 + i], rsem.at[14 + i], dev)
            cp.start()
            second.append(cp)
            cp = _remote(rg_out.at[rg_rows(me)], rg_out.at[rg_rows(me)], ssem.at[21 + i], rsem.at[21 + i], dev)
            cp.start()
            second.append(cp)
        for mask in range(1, NDEV):
            dev, r = peer(mask)
            i = mask - 1
            _remote(v_out.at[r], v_out.at[r], ssem.at[14 + i], rsem.at[14 + i], dev).wait_recv()
            _remote(rg_out.at[rg_rows(r)], rg_out.at[rg_rows(r)], ssem.at[21 + i], rsem.at[21 + i], dev).wait_recv()
        for cp in first + second:
            cp.wait_send()

    return pl.pallas_call(
        body, name="small_allreduce",
        out_shape=[jax.ShapeDtypeStruct((NDEV, VROWS, 128), F32), jax.ShapeDtypeStruct((DG, 128), F32)],
        in_specs=[VMEM_SPEC] * 5, out_specs=[VMEM_SPEC] * 2,
        scratch_shapes=[pltpu.VMEM((NDEV, VROWS, 128), F32), pltpu.VMEM((NDEV - 1, VROWS, 128), F32),
                        pltpu.VMEM((NDEV - 1, PR, 128), F32),
                        pltpu.SemaphoreType.DMA((28,)), pltpu.SemaphoreType.DMA((28,))],
    )(sm_in, sm_tail, sm_a, sm_b, rg_c)


def _adam_rows(w, g, m, v):
    m2 = ADAM_B1 * m + (1.0 - ADAM_B1) * g
    v2 = ADAM_B2 * v + (1.0 - ADAM_B2) * (g * g)
    m_hat = m2 / (1.0 - ADAM_B1 ** ADAM_STEP)
    v_hat = v2 / (1.0 - ADAM_B2 ** ADAM_STEP)
    delta = -ADAM_LR * (m_hat / (jnp.sqrt(v_hat) + ADAM_EPS) + ADAM_WD * w)
    return delta, m2, v2


def _adam_big(gs, ws, ms, vs):
    n = len(gs)
    steps = 8

    def body(*refs):
        ins, outs = refs[:4 * n], refs[4 * n:]
        for i in range(n):
            g, w, m, v = (r[...] for r in ins[4 * i:4 * i + 4])
            d, m2, v2 = _adam_rows(w, g, m, v)
            outs[3 * i][...] = d
            outs[3 * i + 1][...] = m2
            outs[3 * i + 2][...] = v2

    in_specs, out_specs, out_shape, args = [], [], [], []
    for g, w, m, v in zip(gs, ws, ms, vs):
        r, c = w.shape
        spec = lambda: pl.BlockSpec((r // steps, c), lambda i: (i, 0))
        in_specs += [spec() for _ in range(4)]
        out_specs += [spec() for _ in range(3)]
        out_shape += [jax.ShapeDtypeStruct((r, c), F32)] * 3
        args += [g, w, m, v]
    outs = pl.pallas_call(
        body, name="adam_big", grid=(steps,), in_specs=in_specs, out_specs=out_specs, out_shape=out_shape,
        compiler_params=_cparams(("parallel",), 32),
    )(*args)
    return [tuple(outs[3 * i:3 * i + 3]) for i in range(n)]


_VEC_PARAMS = [
    ("norm_mix_w", ROW_NORM_MIX, 0, D), ("final_norm_w", ROW_FINAL_NORM, 0, D),
    ("b_ple_gate", ROW_B_PG, 0, D), ("ple_norm_w", ROW_PLE_NORM, 0, D),
    ("conv_b", ROW_CB_BA, 0, DG), ("rg_ba", ROW_CB_BA, DG, DG),
    ("rg_bx", ROW_BX_LAM, 0, DG), ("rg_lambda", ROW_BX_LAM, DG, DG),
    ("hg_norm_w", ROW_HG_NW, 0, HD),
]
_SMALL_ORDER = [n for n, _, _, _ in _VEC_PARAMS] + ["hg_lb", "conv_w", "rg_wa", "rg_wx"]


def _adam_small(vred, rgred, ws, ms, vs):
    names = _SMALL_ORDER
    n = len(names)

    def body(vred_ref, rg_ref, *refs):
        w_refs = dict(zip(names, refs[0:n]))
        m_refs = dict(zip(names, refs[n:2 * n]))
        v_refs = dict(zip(names, refs[2 * n:3 * n]))
        outs = refs[3 * n:]
        o_refs = {nm: outs[4 * i:4 * i + 4] for i, nm in enumerate(names)}
        kme = 2 * lax.axis_index("x") + lax.axis_index("y")

        def update(nm, g, idx):
            d, m2, v2 = _adam_rows(w_refs[nm][idx], g, m_refs[nm][idx], v_refs[nm][idx])
            og, od, om, ov = o_refs[nm]
            og[idx] = g
            od[idx] = d
            om[idx] = m2
            ov[idx] = v2

        def packed(row, lane0, width):
            return jnp.concatenate([vred_ref[j, row:row + 1, :] for j in range(lane0 // 128, (lane0 + width) // 128)],
                                   axis=1)

        everything = (slice(None), slice(None))
        for nm, row, lane0, width in _VEC_PARAMS:
            update(nm, packed(row, lane0, width), everything)
        for r in range(2):
            update("hg_lb", packed(ROW_HG_LB, r * DG, DG), (slice(r, r + 1), slice(None)))
        for j in range(4):
            g = vred_ref[(j % 2) * 4 + kme, ROW_CW01 + j // 2:ROW_CW01 + j // 2 + 1, :]
            update("conv_w", g, (slice(j, j + 1), slice(None)))
        for r0 in range(0, DG, 128):
            rs = (slice(r0, r0 + 128), slice(None))
            both = rg_ref[r0:r0 + 128, :]
            update("rg_wa", both[:, 0:RGB], rs)
            update("rg_wx", pltpu.roll(both, RGB, 1)[:, 0:RGB], rs)

    args = [vred, rgred] + [d[nm] for d in (ws, ms, vs) for nm in names]
    out_shape = []
    for nm in names:
        out_shape += [jax.ShapeDtypeStruct(ws[nm].shape, F32)] * 4
    whole = lambda s: pl.BlockSpec(s.shape, lambda i, nd=len(s.shape): (0,) * nd)
    outs = pl.pallas_call(
        body, name="adam_small", out_shape=out_shape, grid=(1,),
        in_specs=[whole(a) for a in args], out_specs=[whole(s) for s in out_shape],
    )(*args)
    return {nm: tuple(outs[4 * i:4 * i + 4]) for i, nm in enumerate(names)}


def _fwd_inproj(x, nw, w_in_b, dep):
    tm = 512

    def body(x_ref, nw_ref, w_ref, dep_ref, pf_ref, pb_ref, u_ref):
        xv = x_ref[...]
        s = lax.rsqrt(jnp.mean(xv * xv, axis=-1, keepdims=True) + EPS)
        u = (xv * s * nw_ref[...]).astype(BF16)
        u_ref[...] = u
        r = [jnp.dot(u, w_ref[k], preferred_element_type=F32) for k in range(NSHARD)]
        h = DG // 2
        pf_ref[:, 0:DG] = r[0][:, 0:DG]
        pb_ref[:, 0:h] = r[0][:, DG:SHW].astype(BF16)
        pb_ref[:, h:DG] = r[1][:, 0:h].astype(BF16)
        pb_ref[:, DG:2 * DG] = r[1][:, h:SHW].astype(BF16)
        pf_ref[:, DG:2 * DG] = r[2][:, 0:DG]
        pb_ref[:, 2 * DG:2 * DG + h] = r[2][:, DG:SHW].astype(BF16)
        pb_ref[:, 2 * DG + h:3 * DG] = r[3][:, 0:h].astype(BF16)
        pb_ref[:, 3 * DG:4 * DG] = r[3][:, h:SHW].astype(BF16)

    return pl.pallas_call(
        body, name="fwd_inproj", grid=(T // tm,),
        in_specs=[pl.BlockSpec((tm, D), lambda i: (i, 0)),
                  pl.BlockSpec((1, D), lambda i: (0, 0)),
                  pl.BlockSpec((NSHARD, D, SHW), lambda i: (0, 0, 0)), ANY_SPEC],
        out_specs=[pl.BlockSpec((tm, 2 * DG), lambda i: (i, 0)),
                   pl.BlockSpec((tm, 4 * DG), lambda i: (i, 0)),
                   pl.BlockSpec((tm, D), lambda i: (i, 0))],
        out_shape=[jax.ShapeDtypeStruct((T, 2 * DG), F32), jax.ShapeDtypeStruct((T, 4 * DG), BF16),
                   jax.ShapeDtypeStruct((T, D), BF16)],
        compiler_params=_cparams(("parallel",), 48),
    )(x, nw, w_in_b, dep)


def _conv_rows(cw_ref):
    return [jnp.concatenate([cw_ref[k, j:j + 1, :] for k in range(NSHARD)], axis=1) for j in range(4)]


def _rg_conv(xa, prev8, cw, cb, rows):
    taps = [_shift_down(xa, prev8, 3, rows), _shift_down(xa, prev8, 2, rows),
            _shift_down(xa, prev8, 1, rows), xa]
    xc = cb
    for j in range(4):
        xc = xc + taps[j] * cw[j]
    return xc, taps


def _block_mask():
    r = lax.broadcasted_iota(I32, (DG, DG), 0)
    c = lax.broadcasted_iota(I32, (DG, DG), 1)
    return (r >> 6) == (c >> 6)


def _dense_from_blocks(wc):
    j = lax.broadcasted_iota(I32, (RGB, DG), 0)
    c = lax.broadcasted_iota(I32, (RGB, DG), 1)
    spread = _mm_exact(wc, ((c & (RGB - 1)) == j).astype(F32))
    return jnp.where(_block_mask(), spread, 0.0)


def _blocks_from_dense(da, dx):
    c = lax.broadcasted_iota(I32, (DG, 128), 0)
    j = lax.broadcasted_iota(I32, (DG, 128), 1)
    hit = (c & (RGB - 1)) == (j & (RGB - 1))
    mask = _block_mask()
    return (_mm_exact(jnp.where(mask, da, 0.0), (hit & (j < RGB)).astype(F32))
            + _mm_exact(jnp.where(mask, dx, 0.0), (hit & (j >= RGB)).astype(F32)))


def _rg_gates(xc, wa, ba, wx, bx, sp, first_row):
    r = _sigmoid(_mm(xc, wa) + ba)
    i = _sigmoid(_mm(xc, wx) + bx)
    log_a = (-RG_C) * r * sp
    a = jnp.exp(log_a)
    a2 = a * a
    one_m_a2 = -jnp.tanh(log_a) * (a2 + 1.0)
    mult = jnp.where(first_row, 1.0, jnp.sqrt(one_m_a2))
    return r, i, a, a2, mult


def _softplus(z):
    return jnp.maximum(z, 0.0) + jnp.log1p(jnp.exp(-jnp.abs(z)))


def _fwd_rglru(pf, pb, cw3, conv_b, wa_c, ba, wx_c, bx, lam):
    tm = 512
    ng = tm // 8

    def body(xa_ref, ga_ref, cw_ref, cb_ref, wa_ref, ba_ref, wx_ref, bx_ref, lam_ref,
             h_ref, ya_ref, a_s, u_s, tail_s, hc_s, wa_s, wx_s):
        i = pl.program_id(0)

        @pl.when(i == 0)
        def _():
            tail_s[...] = jnp.zeros_like(tail_s)
            hc_s[...] = jnp.zeros_like(hc_s)
            wa_s[...] = _dense_from_blocks(wa_ref[...]).astype(BF16)
            wx_s[...] = _dense_from_blocks(wx_ref[...]).astype(BF16)

        rows = lax.broadcasted_iota(I32, (tm, DG), 0)
        xa = xa_ref[...]
        xc, _ = _rg_conv(xa, tail_s[...], _conv_rows(cw_ref), cb_ref[...], rows)
        tail_s[...] = xa[tm - 8:tm, :]
        sp = _softplus(-lam_ref[...])
        first_row = (rows + i * tm) == 0
        r, ig, a, a2, mult = _rg_gates(xc, wa_s[...], ba_ref[...], wx_s[...], bx_ref[...], sp, first_row)
        av, uv = a, mult * (ig * xc)
        rows8 = rows & 7
        for d in (1, 2, 4):
            keep = rows8 >= d
            uv = uv + av * jnp.where(keep, _roll_in_groups(uv, d), 0.0)
            av = av * jnp.where(keep, _roll_in_groups(av, d), 1.0)
        a_s[...] = av
        u_s[...] = uv
        carry = hc_s[0:1, :]
        for g in range(ng):
            sl = slice(g * 8, (g + 1) * 8)
            ab, ub = a_s[sl, :], u_s[sl, :]
            h_ref[sl, :] = ub + ab * carry
            carry = ub[7:8, :] + ab[7:8, :] * carry
        hc_s[0:1, :] = carry
        ga = ga_ref[...].astype(F32)
        sg = _sigmoid(ga)
        ya_ref[...] = (h_ref[...] * (ga * sg)).astype(BF16)

    vec = lambda: pl.BlockSpec((1, DG), lambda i: (0, 0))
    blocks = lambda: pl.BlockSpec((DG, RGB), lambda i: (0, 0))
    return pl.pallas_call(
        body, name="fwd_rglru", grid=(T // tm,),
        in_specs=[pl.BlockSpec((tm, DG), lambda i: (i, 0)),
                  pl.BlockSpec((tm, DG), lambda i: (i, 0)),
                  pl.BlockSpec((NSHARD, 4, 128), lambda i: (0, 0, 0)), vec(),
                  blocks(), vec(), blocks(), vec(), vec()],
        out_specs=[pl.BlockSpec((tm, DG), lambda i: (i, 0)),
                   pl.BlockSpec((tm, DG), lambda i: (i, 0))],
        out_shape=[jax.ShapeDtypeStruct((T, DG), F32), jax.ShapeDtypeStruct((T, DG), BF16)],
        scratch_shapes=[pltpu.VMEM((tm, DG), F32), pltpu.VMEM((tm, DG), F32),
                        pltpu.VMEM((8, DG), F32), pltpu.VMEM((8, DG), F32),
                        pltpu.VMEM((DG, DG), BF16), pltpu.VMEM((DG, DG), BF16)],
        compiler_params=_cparams(("arbitrary",), 48),
    )(pf, pb, cw3, conv_b, wa_c, ba, wx_c, bx, lam)


def _hg_lower_bound(lb_ref):
    return _sig_pair(lb_ref[0:1, :] - lb_ref[1:2, :])


def _hg_gates(fz, lb, one_m_lb):
    sg, sn = _sig_pair(fz)
    f = lb + one_m_lb * sg
    return sg, sn, f, jnp.log(f), one_m_lb * sn


def _tri(lower):
    r = lax.broadcasted_iota(I32, (CH, CH), 0)
    c = lax.broadcasted_iota(I32, (CH, CH), 1)
    return (r >= c) if lower else (r <= c)


def _chunk_cumsum(v, rows64):
    for d in (1, 2, 4, 8, 16, 32):
        v = v + jnp.where(rows64 >= d, pltpu.roll(v, d, 0), 0.0)
    return v


def _chunk_rev_cumsum(v, rows64):
    n = v.shape[0]
    for d in (1, 2, 4, 8, 16, 32):
        v = v + jnp.where(rows64 < CH - d, pltpu.roll(v, n - d, 0), 0.0)
    return v


def _hg_recompute(q_ref, f_ref, lb, one_m_lb, rows64, eb_s, enb_s, ekd_s, qe_s, ke_s, kd_s, dec_s):
    nc = q_ref.shape[0] // CH
    sg, sn, f, logf, k = _hg_gates(f_ref[...], lb, one_m_lb)
    q = q_ref[...].astype(F32)
    sq = _sigmoid(q)
    qs = q * sq * (HD ** -0.5)
    b = _chunk_cumsum(logf, rows64)
    for c in range(nc):
        rs = slice(c * CH, (c + 1) * CH)
        b_c = b[rs]
        bl = b_c[CH - 1:CH, :]
        eb, enb, ekd = jnp.exp(b_c), jnp.exp(-b_c), jnp.exp(bl - b_c)
        if eb_s is not None:
            eb_s[rs, :] = eb
            enb_s[rs, :] = enb
            ekd_s[rs, :] = ekd
        qe_s[rs, :] = (qs[rs] * eb).astype(BF16)
        ke_s[rs, :] = (k[rs] * enb).astype(BF16)
        kd_s[rs, :] = (k[rs] * ekd).astype(BF16)
        dec_s[c:c + 1, :] = jnp.exp(bl)
    return sg, sn, f, k, q, sq, qs


def _fwd_hgrn2(pf, pb, hg_lb, hg_nw):
    tm = 512
    nc = tm // CH

    def body(q_ref, f_ref, v_ref, g_ref, lb_ref, nw_ref, yb_ref, o_ref, sp_ref,
             st_s, qe_s, ke_s, kd_s, vb_s, dec_s, p_s, ds_s):
        i = pl.program_id(0)

        @pl.when(i == 0)
        def _():
            st_s[...] = jnp.zeros_like(st_s)

        lb, one_m_lb = _hg_lower_bound(lb_ref)
        rows64 = lax.broadcasted_iota(I32, (tm, DG), 0) & (CH - 1)
        _hg_recompute(q_ref, f_ref, lb, one_m_lb, rows64, None, None, None, qe_s, ke_s, kd_s, dec_s)
        vb_s[...] = v_ref[...]
        mask = _tri(True)
        items = [(c, hd, slice(c * CH, (c + 1) * CH), slice(hd * HD, (hd + 1) * HD))
                 for c in range(nc) for hd in range(NH)]
        for c, hd, rs, cols in items:
            p_s[c * NH + hd] = jnp.where(mask, _mm_nt(qe_s[rs, cols], ke_s[rs, cols]), 0.0).astype(BF16)
            ds_s[c * NH + hd] = _mm_tn(vb_s[rs, cols], kd_s[rs, cols])
        for c, hd, rs, cols in items:
            st = st_s[hd]
            sp_ref[hd, c] = st
            st_s[hd] = st * dec_s[c:c + 1, cols] + ds_s[c * NH + hd]
        for c, hd, rs, cols in items:
            o_ref[rs, cols] = _mm(p_s[c * NH + hd], vb_s[rs, cols]) + _mm_nt(qe_s[rs, cols], sp_ref[hd, c])
        nw = nw_ref[...]
        for hd in range(NH):
            cols = slice(hd * HD, (hd + 1) * HD)
            o = o_ref[:, cols]
            so = lax.rsqrt(jnp.mean(o * o, axis=-1, keepdims=True) + EPS)
            g = g_ref[:, cols].astype(F32)
            sg = _sigmoid(g)
            yb_ref[:, cols] = (o * so * nw * (g * sg)).astype(BF16)

    col = lambda j: pl.BlockSpec((tm, DG), lambda i: (i, j))
    return pl.pallas_call(
        body, name="fwd_hgrn2", grid=(T // tm,),
        in_specs=[col(1), col(1), col(2), col(3),
                  pl.BlockSpec((2, DG), lambda i: (0, 0)),
                  pl.BlockSpec((1, HD), lambda i: (0, 0))],
        out_specs=[pl.BlockSpec((tm, DG), lambda i: (i, 0)),
                   pl.BlockSpec((tm, DG), lambda i: (i, 0)),
                   pl.BlockSpec((NH, nc, HD, HD), lambda i: (0, i, 0, 0))],
        out_shape=[jax.ShapeDtypeStruct((T, DG), BF16), jax.ShapeDtypeStruct((T, DG), F32),
                   jax.ShapeDtypeStruct((NH, NCHUNK, HD, HD), F32)],
        scratch_shapes=[pltpu.VMEM((NH, HD, HD), F32),
                        pltpu.VMEM((tm, DG), BF16), pltpu.VMEM((tm, DG), BF16), pltpu.VMEM((tm, DG), BF16),
                        pltpu.VMEM((tm, DG), BF16), pltpu.VMEM((nc, DG), F32),
                        pltpu.VMEM((nc * NH, CH, CH), BF16), pltpu.VMEM((nc * NH, HD, HD), F32)],
        compiler_params=_cparams(("arbitrary",), 48),
    )(pb, pf, pb, pb, hg_lb, hg_nw)


def _tail_fwd_bwd(x, p, tgt, ya, yb, w_out_b, w_pg_b, w_pp_b, ple_nw, b_pg, fnw):
    tm = 256
    nt = T // tm
    QR = D // NSHARD

    def body(x_ref, p_ref, t_ref, ya_ref, yb_ref, wo_ref, wg_ref, wp_ref, pw_ref, b_ref, fw_ref,
             dh1_ref, dyab_ref, dwo_ref, dwg_ref, dwp_ref, sm_ref, dwo_s, dwg_s, dwp_s):
        i = pl.program_id(0)

        @pl.when(i == 0)
        def _():
            dwo_s[...] = jnp.zeros_like(dwo_s)
            dwg_s[...] = jnp.zeros_like(dwg_s)
            dwp_s[...] = jnp.zeros_like(dwp_s)
            sm_ref[...] = jnp.zeros_like(sm_ref)

        ya = ya_ref[...]
        yb = yb_ref[...]
        pv = p_ref[...].astype(BF16)
        pw = pw_ref[...]
        fw = fw_ref[...]
        h1 = x_ref[...] + _mm(ya, wo_ref[0:DG, :]) + _mm(yb, wo_ref[DG:D, :])
        s2 = lax.rsqrt(jnp.mean(h1 * h1, axis=-1, keepdims=True) + EPS)
        n2h = h1 * s2
        n2 = (n2h * pw).astype(BF16)
        z = _mm(n2, wg_ref[...]) + b_ref[...]
        gate = _sigmoid(z)
        pp = jnp.concatenate([_mm(pv, wp_ref[k]) for k in range(NSHARD)], axis=1)
        h2 = h1 + gate * pp
        s3 = lax.rsqrt(jnp.mean(h2 * h2, axis=-1, keepdims=True) + EPS)
        hn = h2 * s3
        err = hn * fw - t_ref[...]
        sm_ref[0:1, :] += _rowsum(err * err)
        dy = err * (1.0 / D)
        sm_ref[1:2, :] += _rowsum(dy * hn)
        g3 = dy * fw
        dh2 = s3 * (g3 - hn * jnp.mean(g3 * hn, axis=-1, keepdims=True))
        dpp = (dh2 * gate).astype(BF16)
        dz = dh2 * pp * gate * (1.0 - gate)
        sm_ref[2:3, :] += _rowsum(dz)
        dzb = dz.astype(BF16)
        dwg_s[...] += _mm_tn(n2, dzb)
        dn2 = _mm_nt(dzb, wg_ref[...])
        for k in range(NSHARD):
            dwp_s[k] += _mm_tn(pv, dpp[:, k * PLE:(k + 1) * PLE])
        sm_ref[3:4, :] += _rowsum(dn2 * n2h)
        g2 = dn2 * pw
        dh1 = dh2 + s2 * (g2 - n2h * jnp.mean(g2 * n2h, axis=-1, keepdims=True))
        dh1_ref[...] = dh1
        dh1b = dh1.astype(BF16)
        dyab_ref[...] = _mm_nt(dh1b, wo_ref[...])
        dwo_s[0:DG, :] += _mm_tn(ya, dh1b)
        dwo_s[DG:D, :] += _mm_tn(yb, dh1b)

        @pl.when(i == nt - 1)
        def _():
            total = jnp.sum(sm_ref[0:1, :], axis=-1, keepdims=True) * (0.5 / D)
            sm_ref[4:5, :] = jnp.broadcast_to(total, (1, D))
            for k in range(NSHARD):
                dwo_ref[k] = dwo_s[k * QR:(k + 1) * QR, :].astype(BF16)
                dwg_ref[k] = dwg_s[k * QR:(k + 1) * QR, :].astype(BF16)
                dwp_ref[k] = dwp_s[k].astype(BF16)

    row = lambda w: pl.BlockSpec((tm, w), lambda i: (i, 0))
    const2 = lambda s: pl.BlockSpec(s, lambda i: (0, 0))
    const3 = lambda s: pl.BlockSpec(s, lambda i: (0, 0, 0))
    return pl.pallas_call(
        body, name="tail_fwd_bwd", grid=(nt,),
        in_specs=[row(D), row(PLE), row(D), row(DG), row(DG),
                  const2((D, D)), const2((D, D)), const3((NSHARD, PLE, PLE)),
                  const2((1, D)), const2((1, D)), const2((1, D))],
        out_specs=[row(D), row(D), const3((NSHARD, QR, D)), const3((NSHARD, QR, D)),
                   const3((NSHARD, PLE, PLE)), const2((8, D))],
        out_shape=[jax.ShapeDtypeStruct((T, D), F32), jax.ShapeDtypeStruct((T, D), F32),
                   jax.ShapeDtypeStruct((NSHARD, QR, D), BF16), jax.ShapeDtypeStruct((NSHARD, QR, D), BF16),
                   jax.ShapeDtypeStruct((NSHARD, PLE, PLE), BF16), jax.ShapeDtypeStruct((8, D), F32)],
        scratch_shapes=[pltpu.VMEM((D, D), F32), pltpu.VMEM((D, D), F32), pltpu.VMEM((NSHARD, PLE, PLE), F32)],
        compiler_params=_cparams(("arbitrary",), 56),
    )(x, p, tgt, ya, yb, w_out_b, w_pg_b, w_pp_b, ple_nw, b_pg, fnw)


def _bwd_rglru(pf, pb, h, dyab, cw3, conv_b, wa_c, ba, wx_c, bx, lam, dep):
    tm = 512
    nt = T // tm
    ng = tm // 8

    def body(xa_ref, ga_ref, xp_ref, h_ref, hp_ref, dya_ref, cw_ref, cb_ref, wa_ref, ba_ref, wx_ref, bx_ref,
             lam_ref, dep_ref, da_ref, rg_ref, sm_ref, a_s, g_s, cg_s, nxt_s, wa_s, wx_s, dwa_s, dwx_s):
        i = pl.program_id(0)
        tile = nt - 1 - i

        @pl.when(i == 0)
        def _():
            dwa_s[...] = jnp.zeros_like(dwa_s)
            dwx_s[...] = jnp.zeros_like(dwx_s)
            sm_ref[...] = jnp.zeros_like(sm_ref)
            cg_s[...] = jnp.zeros_like(cg_s)
            nxt_s[...] = jnp.zeros_like(nxt_s)
            wa_s[...] = _dense_from_blocks(wa_ref[...]).astype(BF16)
            wx_s[...] = _dense_from_blocks(wx_ref[...]).astype(BF16)

        rows = lax.broadcasted_iota(I32, (tm, DG), 0)
        has_prev = tile > 0
        xa = xa_ref[...]
        xprev = jnp.where(has_prev, xp_ref[...], 0.0)
        cw = _conv_rows(cw_ref)
        xc, taps = _rg_conv(xa, xprev, cw, cb_ref[...], rows)
        lam_v = lam_ref[...]
        sp = _softplus(-lam_v)
        first_row = (rows + tile * tm) == 0
        r, ig, a, a2, mult = _rg_gates(xc, wa_s[...], ba_ref[...], wx_s[...], bx_ref[...], sp, first_row)
        hv = h_ref[...]
        hprev = jnp.where(has_prev, hp_ref[...], 0.0)
        h_m1 = _shift_down(hv, hprev, 1, rows)
        ga = ga_ref[...].astype(F32)
        sg = _sigmoid(ga)
        dya = dya_ref[...]
        dga = dya * hv * (sg * (1.0 + ga * (1.0 - sg)))

        av = jnp.where(rows == tm - 1, 1.0, pltpu.roll(a, tm - 1, 0))
        gv = dya * (ga * sg)
        rows8 = rows & 7
        for d in (1, 2, 4):
            keep = rows8 < 8 - d
            gv = gv + av * jnp.where(keep, _roll_in_groups(gv, 8 - d), 0.0)
            av = av * jnp.where(keep, _roll_in_groups(av, 8 - d), 1.0)
        a_s[...] = av
        g_s[...] = gv
        carry = cg_s[0:1, :]
        for g in range(ng - 1, -1, -1):
            sl = slice(g * 8, (g + 1) * 8)
            ab, gb = a_s[sl, :], g_s[sl, :]
            g_s[sl, :] = gb + ab * carry
            carry = gb[0:1, :] + ab[0:1, :] * carry
        cg_s[0:1, :] = a[0:1, :] * carry

        gt = g_s[...]
        da = gt * h_m1
        ixc = ig * xc
        di = gt * mult * xc
        dxc = gt * mult * ig
        dlog_a = da * a + jnp.where(first_row, 0.0, gt * ixc * (-a2 / mult))
        sm_ref[3:4, :] += _rowsum(dlog_a * ((-RG_C) * r))
        dpr_f = dlog_a * ((-RG_C) * sp) * r * (1.0 - r)
        dpi_f = di * ig * (1.0 - ig)
        sm_ref[1:2, :] += _rowsum(dpr_f)
        sm_ref[2:3, :] += _rowsum(dpi_f)
        dpr = dpr_f.astype(BF16)
        dpi = dpi_f.astype(BF16)
        xcb = xc.astype(BF16)
        dwa_s[...] += _mm_tn(xcb, dpr)
        dwx_s[...] += _mm_tn(xcb, dpi)
        dxc = dxc + _mm_nt(dpr, wa_s[...]) + _mm_nt(dpi, wx_s[...])
        sm_ref[0:1, :] += _rowsum(dxc)
        for j in range(4):
            sm_ref[4 + j:5 + j, :] += _rowsum(dxc * taps[j])
        nxt = nxt_s[...]
        dxa = (dxc * cw[3] + _shift_up(dxc, nxt, 1, rows) * cw[2]
               + _shift_up(dxc, nxt, 2, rows) * cw[1] + _shift_up(dxc, nxt, 3, rows) * cw[0])
        nxt_s[...] = dxc[0:8, :]
        da_ref[:, 0:DG] = dxa.astype(BF16)
        da_ref[:, DG:D] = dga.astype(BF16)

        @pl.when(i == nt - 1)
        def _():
            _, s_neg = _sig_pair(lam_v)
            sm_ref[3:4, :] = sm_ref[3:4, :] * (-s_neg)
            rg_ref[...] = _blocks_from_dense(dwa_s[...], dwx_s[...])

    vec = lambda: pl.BlockSpec((1, DG), lambda i: (0, 0))
    blocks = lambda: pl.BlockSpec((DG, RGB), lambda i: (0, 0))
    prev8 = lambda: pl.BlockSpec((8, DG), lambda i: (jnp.maximum((nt - 1 - i) * (tm // 8) - 1, 0), 0))
    return pl.pallas_call(
        body, name="bwd_rglru", grid=(nt,),
        in_specs=[pl.BlockSpec((tm, DG), lambda i: (nt - 1 - i, 0)),
                  pl.BlockSpec((tm, DG), lambda i: (nt - 1 - i, 0)),
                  prev8(),
                  pl.BlockSpec((tm, DG), lambda i: (nt - 1 - i, 0)),
                  prev8(),
                  pl.BlockSpec((tm, DG), lambda i: (nt - 1 - i, 0)),
                  pl.BlockSpec((NSHARD, 4, 128), lambda i: (0, 0, 0)), vec(),
                  blocks(), vec(), blocks(), vec(), vec(), ANY_SPEC],
        out_specs=[pl.BlockSpec((tm, D), lambda i: (nt - 1 - i, 0)),
                   pl.BlockSpec((DG, 128), lambda i: (0, 0)),
                   pl.BlockSpec((8, DG), lambda i: (0, 0))],
        out_shape=[jax.ShapeDtypeStruct((T, D), BF16), jax.ShapeDtypeStruct((DG, 128), F32),
                   jax.ShapeDtypeStruct((8, DG), F32)],
        scratch_shapes=[pltpu.VMEM((tm, DG), F32), pltpu.VMEM((tm, DG), F32),
                        pltpu.VMEM((8, DG), F32), pltpu.VMEM((8, DG), F32),
                        pltpu.VMEM((DG, DG), BF16), pltpu.VMEM((DG, DG), BF16),
                        pltpu.VMEM((DG, DG), F32), pltpu.VMEM((DG, DG), F32)],
        compiler_params=_cparams(("arbitrary",), 56),
    )(pf, pb, pf, h, h, dyab, cw3, conv_b, wa_c, ba, wx_c, bx, lam, dep)


def _bwd_hgrn2(pf, pb, o, s_prev, dyab, hg_lb, hg_nw, dep):
    tm = 512
    nt = T // tm
    nc = tm // CH

    def body(q_ref, f_ref, v_ref, g_ref, o_ref, sp_ref, dy_ref, lb_ref, nw_ref, dep_ref, db_ref, sm_ref,
             dst_s, eb_s, enb_s, ekd_s, qe_s, ke_s, kd_s, vb_s, do_s, dec_s, ddec_s, p_s, dp_s,
             g_s, dsta_s, dva_s, dqe_s, dke_s, dkd_s, dlf_s):
        i = pl.program_id(0)

        @pl.when(i == 0)
        def _():
            sm_ref[...] = jnp.zeros_like(sm_ref)
            dst_s[...] = jnp.zeros_like(dst_s)

        lb, one_m_lb = _hg_lower_bound(lb_ref)
        rows64 = lax.broadcasted_iota(I32, (tm, DG), 0) & (CH - 1)
        sg, sn, f, k, q, sq, qs = _hg_recompute(
            q_ref, f_ref, lb, one_m_lb, rows64, eb_s, enb_s, ekd_s, qe_s, ke_s, kd_s, dec_s)
        vb_s[...] = v_ref[...]

        nw = nw_ref[...]
        for hd in range(NH):
            cols = slice(hd * HD, (hd + 1) * HD)
            g = g_ref[:, cols].astype(F32)
            sgg = _sigmoid(g)
            o = o_ref[:, cols]
            so = lax.rsqrt(jnp.mean(o * o, axis=-1, keepdims=True) + EPS)
            oh = o * so
            dyb = dy_ref[:, cols]
            db_ref[:, 3 * DG + hd * HD:3 * DG + (hd + 1) * HD] = (
                dyb * (oh * nw) * (sgg * (1.0 + g * (1.0 - sgg)))).astype(BF16)
            don = dyb * (g * sgg)
            sm_ref[1:2, 0:HD] += _rowsum(don * oh)
            gw = don * nw
            do_s[:, cols] = (so * (gw - oh * jnp.mean(gw * oh, axis=-1, keepdims=True))).astype(BF16)

        mask = _tri(True)
        items = [(c, hd, slice(c * CH, (c + 1) * CH), slice(hd * HD, (hd + 1) * HD))
                 for c in range(nc) for hd in range(NH)]
        for c, hd, rs, cols in items:
            p_s[c * NH + hd] = jnp.where(mask, _mm_nt(qe_s[rs, cols], ke_s[rs, cols]), 0.0).astype(BF16)
            dp_s[c * NH + hd] = jnp.where(mask, _mm_nt(do_s[rs, cols], vb_s[rs, cols]), 0.0).astype(BF16)
        for c, hd, rs, cols in items:
            n = c * NH + hd
            dva_s[rs, cols] = _mm_tn(p_s[n], do_s[rs, cols])
            dqe_s[rs, cols] = _mm(dp_s[n], ke_s[rs, cols])
            dke_s[rs, cols] = _mm_tn(dp_s[n], qe_s[rs, cols])
            g_s[n] = _mm_tn(do_s[rs, cols], qe_s[rs, cols])
        for c, hd, rs, cols in reversed(items):
            n = c * NH + hd
            dst = dst_s[hd]
            dsta_s[n] = dst
            dst_s[hd] = dst * dec_s[c:c + 1, cols] + g_s[n]
        for c, hd, rs, cols in items:
            n = c * NH + hd
            dst = dsta_s[n]
            st_prev = sp_ref[hd, c]
            dv = dva_s[rs, cols] + _mm_nt(kd_s[rs, cols], dst)
            db_ref[rs, 2 * DG + hd * HD:2 * DG + (hd + 1) * HD] = dv.astype(BF16)
            dqe_s[rs, cols] += _mm(do_s[rs, cols], st_prev)
            dkd_s[rs, cols] = _mm(vb_s[rs, cols], dst)
            ddec_s[c:c + 1, cols] = _rowsum(dst * st_prev)

        eb, enb, ekd = eb_s[...], enb_s[...], ekd_s[...]
        dqe, dke, dkd = dqe_s[...], dke_s[...], dkd_s[...]
        t_kd = dkd * (k * ekd)
        rc = _chunk_rev_cumsum(dqe * (qs * eb) - dke * (k * enb) - t_kd, rows64)
        for c in range(nc):
            rs = slice(c * CH, (c + 1) * CH)
            dbl = _rowsum(t_kd[rs]) + ddec_s[c:c + 1, :] * dec_s[c:c + 1, :]
            dlf_s[rs, :] = rc[rs] + dbl
        t = dlf_s[...] / f - (dke * enb + dkd * ekd)
        db_ref[:, DG:2 * DG] = (one_m_lb * sg * sn * t).astype(BF16)
        sm_ref[0:1, :] += _rowsum(sn * t)
        db_ref[:, 0:DG] = (dqe * eb * (sq * (1.0 + q * (1.0 - sq))) * (HD ** -0.5)).astype(BF16)

        @pl.when(i == nt - 1)
        def _():
            dsm = sm_ref[0:1, :] * (lb * one_m_lb)
            sm_ref[2:3, :] = dsm
            sm_ref[3:4, :] = -dsm

    col = lambda j: pl.BlockSpec((tm, DG), lambda i: (nt - 1 - i, j))
    big = lambda dt: pltpu.VMEM((tm, DG), dt)
    return pl.pallas_call(
        body, name="bwd_hgrn2", grid=(nt,),
        in_specs=[col(1), col(1), col(2), col(3),
                  pl.BlockSpec((tm, DG), lambda i: (nt - 1 - i, 0)),
                  pl.BlockSpec((NH, nc, HD, HD), lambda i: (0, nt - 1 - i, 0, 0)),
                  pl.BlockSpec((tm, DG), lambda i: (nt - 1 - i, 1)),
                  pl.BlockSpec((2, DG), lambda i: (0, 0)),
                  pl.BlockSpec((1, HD), lambda i: (0, 0)), ANY_SPEC],
        out_specs=[pl.BlockSpec((tm, 4 * DG), lambda i: (nt - 1 - i, 0)),
                   pl.BlockSpec((8, DG), lambda i: (0, 0))],
        out_shape=[jax.ShapeDtypeStruct((T, 4 * DG), BF16), jax.ShapeDtypeStruct((8, DG), F32)],
        scratch_shapes=[pltpu.VMEM((NH, HD, HD), F32),
                        big(F32), big(F32), big(F32),
                        big(BF16), big(BF16), big(BF16), big(BF16), big(BF16),
                        pltpu.VMEM((nc, DG), F32), pltpu.VMEM((nc, DG), F32),
                        pltpu.VMEM((nc * NH, CH, CH), BF16), pltpu.VMEM((nc * NH, CH, CH), BF16),
                        pltpu.VMEM((nc * NH, HD, HD), F32), pltpu.VMEM((nc * NH, HD, HD), F32),
                        big(F32), big(F32), big(F32), big(F32), big(F32)],
        compiler_params=_cparams(("arbitrary",), 56),
    )(pb, pf, pb, pb, o, s_prev, dyab, hg_lb, hg_nw, dep)


def _dproj_pieces(k, da_ref, db_ref):
    if k == 0:
        return [(da_ref[:, 0:SHW], 0)]
    if k == 1:
        return [(da_ref[:, SHW:D], 0), (db_ref[:, 0:DG], D - SHW)]
    if k == 2:
        return [(db_ref[:, DG:DG + SHW], 0)]
    return [(db_ref[:, DG + SHW:4 * DG], 0)]


def _bwd_inproj_dx(x, dh1, d_a, d_b, w_in_b, nw, dep):
    tm = 512
    nt = T // tm

    def body(x_ref, dh1_ref, da_ref, db_ref, w_ref, nw_ref, dep_ref, dx_ref, sm_ref):
        i = pl.program_id(0)

        @pl.when(i == 0)
        def _():
            sm_ref[...] = jnp.zeros_like(sm_ref)

        du = None
        for k in range(NSHARD):
            for val, off in _dproj_pieces(k, da_ref, db_ref):
                t = _mm_nt(val, w_ref[k, :, off:off + val.shape[1]])
                du = t if du is None else du + t
        xv = x_ref[...]
        s = lax.rsqrt(jnp.mean(xv * xv, axis=-1, keepdims=True) + EPS)
        xh = xv * s
        sm_ref[0:1, :] += _rowsum(du * xh)
        g = du * nw_ref[...]
        dx_ref[...] = dh1_ref[...] + s * (g - xh * jnp.mean(g * xh, axis=-1, keepdims=True))

    row = lambda w: pl.BlockSpec((tm, w), lambda i: (i, 0))
    return pl.pallas_call(
        body, name="bwd_inproj_dx", grid=(nt,),
        in_specs=[row(D), row(D), row(D), row(4 * DG),
                  pl.BlockSpec((NSHARD, D, SHW), lambda i: (0, 0, 0)),
                  pl.BlockSpec((1, D), lambda i: (0, 0)), ANY_SPEC],
        out_specs=[row(D), pl.BlockSpec((8, D), lambda i: (0, 0))],
        out_shape=[jax.ShapeDtypeStruct((T, D), F32), jax.ShapeDtypeStruct((8, D), F32)],
        compiler_params=_cparams(("arbitrary",), 56),
    )(x, dh1, d_a, d_b, w_in_b, nw, dep)


def _bwd_inproj_dw(name, ks, u_b, d_a, d_b):
    tm = 512
    nt = T // tm
    nk = len(ks)
    H = D // 2
    step = 128
    need_a = 0 in ks or 1 in ks

    def body(*refs):
        if need_a:
            u_ref, da_ref, db_ref, dw_ref, acc, send_s, recv_s, ssem, rsem = refs
        else:
            u_ref, db_ref, dw_ref, acc, send_s, recv_s, ssem, rsem = refs
            da_ref = None
        i = pl.program_id(0)

        @pl.when(i == 0)
        def _():
            acc[...] = jnp.zeros_like(acc)

        u = u_ref[...]
        for j, k in enumerate(ks):
            for val, off in _dproj_pieces(k, da_ref, db_ref):
                acc[j, :, off:off + val.shape[1]] += _mm_tn(u, val)

        @pl.when(i == nt - 1)
        def _():
            x, y, c = lax.axis_index("x"), lax.axis_index("y"), lax.axis_index("c")
            sibling = (x, y, 1 - c)
            mine0 = pl.multiple_of(c * H, step)
            other0 = pl.multiple_of((1 - c) * H, step)
            copies = []
            for j in range(nk):
                for r0 in range(0, H, step):
                    send_s[j, r0:r0 + step, :] = acc[j, pl.ds(other0 + r0, step), :].astype(BF16)
                cp = _remote(send_s.at[j], recv_s.at[j], ssem.at[j], rsem.at[j], sibling)
                cp.start()
                copies.append(cp)
            for j in range(nk):
                copies[j].wait_recv()
                for r0 in range(0, H, step):
                    s = acc[j, pl.ds(mine0 + r0, step), :] + recv_s[j, r0:r0 + step, :].astype(F32)
                    dw_ref[j, r0:r0 + step, :] = s.astype(BF16)
            for cp in copies:
                cp.wait_send()

    row = lambda w: pl.BlockSpec((tm, w), lambda i: (i, 0))
    ins = [u_b] + ([d_a] if need_a else []) + [d_b]
    in_specs = [row(D)] + ([row(D)] if need_a else []) + [row(4 * DG)]
    return pl.pallas_call(
        body, name=name, grid=(nt,), in_specs=in_specs,
        out_specs=pl.BlockSpec((nk, H, SHW), lambda i: (0, 0, 0)),
        out_shape=jax.ShapeDtypeStruct((nk, H, SHW), BF16),
        scratch_shapes=[pltpu.VMEM((nk, D, SHW), F32), pltpu.VMEM((nk, H, SHW), BF16),
                        pltpu.VMEM((nk, H, SHW), BF16),
                        pltpu.SemaphoreType.DMA((nk,)), pltpu.SemaphoreType.DMA((nk,))],
        compiler_params=_cparams(("arbitrary",), 48),
    )(*ins)


_OUT_ORDER = ["norm_mix_w", "w_in", "conv_w", "conv_b", "rg_wa", "rg_ba", "rg_wx", "rg_bx", "rg_lambda", "hg_lb",
              "hg_norm_w", "w_out", "ple_norm_w", "w_ple_gate", "b_ple_gate", "w_ple_proj", "final_norm_w"]
_BIG = ["w_in", "w_out", "w_ple_gate", "w_ple_proj"]


def _small_view(name, a):
    if name in ("rg_wa", "rg_wx"):
        return a.reshape(DG, RGB)
    if name == "conv_w":
        return a.reshape(4, 128)
    if name == "final_norm_w":
        return a.reshape(1, D)
    return a


def _landing(rows, cols):
    return lax.empty((NDEV, rows, cols), BF16)


def kernel(x, p, norm_mix_w, w_in, conv_w, conv_b, rg_wa, rg_ba, rg_wx, rg_bx, rg_lambda, hg_lb, hg_norm_w, w_out, ple_norm_w, w_ple_gate, b_ple_gate, w_ple_proj, final_norm_w, loss_target, m_norm_mix_w, m_w_in, m_conv_w, m_conv_b, m_rg_wa, m_rg_ba, m_rg_wx, m_rg_bx, m_rg_lambda, m_hg_lb, m_hg_norm_w, m_w_out, m_ple_norm_w, m_w_ple_gate, m_b_ple_gate, m_w_ple_proj, m_final_norm_w, v_norm_mix_w, v_w_in, v_conv_w, v_conv_b, v_rg_wa, v_rg_ba, v_rg_wx, v_rg_bx, v_rg_lambda, v_hg_lb, v_hg_norm_w, v_w_out, v_ple_norm_w, v_w_ple_gate, v_b_ple_gate, v_w_ple_proj, v_final_norm_w):
    given = dict(locals())
    x2, p2, tgt = x[0], p[0, 0], loss_target[0]
    wa_c, wx_c = _small_view("rg_wa", rg_wa), _small_view("rg_wx", rg_wx)

    w_in_b, l_out, l_pg, l_pp, cw3 = _gather_w_in(w_in[0], w_out[0], w_ple_gate[0], w_ple_proj[0], conv_w[0])
    g_ssem, g_rsem, g_lands, tok = _gather_rest_start([l_out, l_pg, l_pp])

    pf, pb, u_b = _fwd_inproj(x2, norm_mix_w, w_in_b, tok)
    h, ya = _fwd_rglru(pf, pb, cw3, conv_b, wa_c, rg_ba, wx_c, rg_bx, rg_lambda)
    yb, o, s_prev = _fwd_hgrn2(pf, pb, hg_lb, hg_norm_w)
    w_out_b, w_pg_b, w_pp_b = _gather_rest_wait(g_ssem, g_rsem, g_lands, yb)
    dh1, dyab, dwo_b, dwg_b, dwp_b, sm_tail = _tail_fwd_bwd(
        x2, p2, tgt, ya, yb, w_out_b.reshape(D, D), w_pg_b.reshape(D, D), w_pp_b,
        ple_norm_w, b_ple_gate, final_norm_w.reshape(1, D))

    QH = D // NSHARD // 2
    r1 = _rs_start("rs_start_tail", [dwo_b, dwg_b, dwp_b], (0, 1, 2, 3),
                   [_landing(QH, D), _landing(QH, D), _landing(PLE // 2, PLE)])
    d_b, sm_b = _bwd_hgrn2(pf, pb, o, s_prev, dyab, hg_lb, hg_norm_w, r1[4])
    dw23 = _bwd_inproj_dw("bwd_inproj_dw23", (2, 3), u_b, None, d_b)
    r2 = _rs_start("rs_start_in23", [dw23], (2, 3), [lax.empty((NSHARD, D // 2, SHW), BF16)], chip_sums=True)
    d_a, rg_c, sm_a = _bwd_rglru(pf, pb, h, dyab, cw3, conv_b, wa_c, rg_ba, wx_c, rg_bx, rg_lambda, r2[4])
    dw01 = _bwd_inproj_dw("bwd_inproj_dw01", (0, 1), u_b, d_a, d_b)
    r3 = _rs_start("rs_start_in01", [dw01], (0, 1), r2[3], chip_sums=True)
    grad_x, sm_in = _bwd_inproj_dx(x2, dh1, d_a, d_b, w_in_b, norm_mix_w, r3[4])

    vred, rgred = _small_allreduce(sm_in, sm_tail, sm_a, sm_b, rg_c)
    parts1, lands1 = _rs_wait("rs_wait_tail", r1[0], r1[1], r1[2], (0, 1, 2, 3), r1[3], vred)
    parts2, lands2 = _rs_wait("rs_wait_in23", r2[0], r2[1], r2[2], (2, 3), r3[3], vred, chip_sums=True)
    parts3, lands3 = _rs_wait("rs_wait_in01", r3[0], r3[1], r3[2], (0, 1), lands2, vred, chip_sums=True)
    g_big = _reduce_exchange([parts3[0], parts2[0]] + parts1, lands3 + lands1)

    upd_big = _adam_big(g_big, [given[n][0] for n in _BIG], [given["m_" + n][0] for n in _BIG],
                        [given["v_" + n][0] for n in _BIG])
    small = _adam_small(vred, rgred,
                        {n: _small_view(n, given[n]) for n in _SMALL_ORDER},
                        {n: _small_view(n, given["m_" + n]) for n in _SMALL_ORDER},
                        {n: _small_view(n, given["v_" + n]) for n in _SMALL_ORDER})

    loss = vred[0, ROW_LOSS, 0]
    outs = [loss, grad_x[None]]
    for ki in range(4):
        for n in _OUT_ORDER:
            if n in _BIG:
                i = _BIG.index(n)
                a = g_big[i] if ki == 0 else upd_big[i][ki - 1]
                outs.append(a[None])
            else:
                outs.append(small[n][ki].reshape(given[n].shape))
    return tuple(outs)
```

```python
import jax
import jax.numpy as jnp
from jax import lax
from jax.experimental import pallas as pl
from jax.experimental.pallas import tpu as pltpu

F32 = jnp.float32
BF16 = jnp.bfloat16
I32 = jnp.int32
MESH = pl.DeviceIdType.MESH
HIGHEST = lax.Precision.HIGHEST

T = 4096
D = 1024
DG = 512
DIN = 3072
PLE = 256
NH = 4
HD = 128
CH = 64
NCHUNK = T // CH
RGB = 64
EPS = 1e-6
RG_C = 8.0
NSHARD = 4
SHW = DIN // NSHARD
NDEV = 8

ADAM_LR = 0.001
ADAM_B1 = 0.9
ADAM_B2 = 0.999
ADAM_EPS = 1e-08
ADAM_WD = 0.01
ADAM_STEP = 10

VMEM_SPEC = pl.BlockSpec(memory_space=pltpu.VMEM)
HBM_SPEC = pl.BlockSpec(memory_space=pltpu.HBM)
SEM_SPEC = pl.BlockSpec(memory_space=pltpu.SEMAPHORE)
ANY_SPEC = pl.BlockSpec(memory_space=pl.ANY)
EFFECT = pltpu.SideEffectType.DATAFLOW_SIDE_EFFECTING
MIB = 1024 * 1024

VROWS = 16
ROW_NORM_MIX, ROW_FINAL_NORM, ROW_B_PG, ROW_PLE_NORM = 0, 1, 2, 3
ROW_CB_BA, ROW_BX_LAM, ROW_CW01, ROW_CW23, ROW_HG_LB, ROW_HG_NW, ROW_LOSS = 4, 5, 6, 7, 8, 9, 10


def _mm(a, b):
    return jnp.dot(a.astype(BF16), b.astype(BF16), preferred_element_type=F32)


def _mm_nt(a, b):
    return lax.dot_general(a.astype(BF16), b.astype(BF16), (((1,), (1,)), ((), ())),
                           preferred_element_type=F32)


def _mm_tn(a, b):
    return lax.dot_general(a.astype(BF16), b.astype(BF16), (((0,), (0,)), ((), ())),
                           preferred_element_type=F32)


def _mm_exact(a, b):
    return jnp.dot(a, b, precision=HIGHEST, preferred_element_type=F32)


def _sig_pair(x):
    e = jnp.exp(-jnp.abs(x))
    big = 1.0 / (1.0 + e)
    small = e * big
    pos = x >= 0
    return jnp.where(pos, big, small), jnp.where(pos, small, big)


def _sigmoid(x):
    return 1.0 / (1.0 + jnp.exp(-x))


def _rowsum(v):
    return jnp.sum(v, axis=0, keepdims=True)


def _shift_down(cur, prev8, d, rows):
    rolled = pltpu.roll(cur, d, 0)
    head = jnp.where(rows[0:8] < d, pltpu.roll(prev8, d, 0), rolled[0:8])
    return jnp.concatenate([head, rolled[8:]], axis=0)


def _shift_up(cur, next8, d, rows):
    n = cur.shape[0]
    rolled = pltpu.roll(cur, n - d, 0)
    tail = jnp.where(rows[0:8] >= 8 - d, pltpu.roll(next8, 8 - d, 0), rolled[n - 8:n])
    return jnp.concatenate([rolled[0:n - 8], tail], axis=0)


def _roll_in_groups(v, d):
    n, w = v.shape
    return pltpu.roll(v.reshape(n // 8, 8, w), d, 1).reshape(n, w)


def _cparams(sem, vmem_mib):
    return pltpu.CompilerParams(dimension_semantics=sem, vmem_limit_bytes=vmem_mib * MIB)


def _mesh_pos():
    x, y, c = lax.axis_index("x"), lax.axis_index("y"), lax.axis_index("c")
    chips = [(1 - x, y), (x, 1 - y), (1 - x, 1 - y)]
    return x, y, c, chips


def _remote(src, dst, ssem, rsem, dev):
    return pltpu.make_async_remote_copy(src_ref=src, dst_ref=dst, send_sem=ssem, recv_sem=rsem,
                                        device_id=dev, device_id_type=MESH)


def _gather_w_in(w_in, w_out, w_pg, w_pp, conv_w):
    shapes = [w_in.shape, w_out.shape, w_pg.shape, w_pp.shape]

    def body(win, wout, wpg, wpp, cw, o_in, o_out, o_pg, o_pp, o_cw, ssem, rsem):
        x, y, c, chips = _mesh_pos()
        kme = 2 * x + y
        sibling = (x, y, 1 - c)
        for src, dst in [(win, o_in), (wout, o_out), (wpg, o_pg), (wpp, o_pp)]:
            for r0 in range(0, src.shape[0], 128):
                dst[kme, r0:r0 + 128, :] = src[r0:r0 + 128, :].astype(BF16)
        o_cw[kme] = cw[...]

        hrows = D // 2
        mine = pl.ds(pl.multiple_of(c * hrows, 128), hrows)
        other = pl.ds(pl.multiple_of((1 - c) * hrows, 128), hrows)
        sends = []
        for j, (px, py) in enumerate(chips):
            cp = _remote(o_in.at[kme, mine], o_in.at[kme, mine], ssem.at[j], rsem.at[j], (px, py, c))
            cp.start()
            sends.append(cp)
            cp = _remote(o_cw.at[kme], o_cw.at[kme], ssem.at[3 + j], rsem.at[3 + j], (px, py, c))
            cp.start()
            sends.append(cp)
        for j, (px, py) in enumerate(chips):
            kj = 2 * px + py
            _remote(o_in.at[kj, mine], o_in.at[kj, mine], ssem.at[j], rsem.at[j], (px, py, c)).wait_recv()
            cp = _remote(o_in.at[kj, mine], o_in.at[kj, mine], ssem.at[6 + j], rsem.at[6 + j], sibling)
            cp.start()
            sends.append(cp)
        for j, (px, py) in enumerate(chips):
            kj = 2 * px + py
            _remote(o_cw.at[kj], o_cw.at[kj], ssem.at[3 + j], rsem.at[3 + j], (px, py, c)).wait_recv()
            _remote(o_in.at[kj, other], o_in.at[kj, other], ssem.at[6 + j], rsem.at[6 + j], sibling).wait_recv()
        for cp in sends:
            cp.wait_send()

    out_shape = [jax.ShapeDtypeStruct((NSHARD,) + s, BF16) for s in shapes]
    out_shape.append(jax.ShapeDtypeStruct((NSHARD,) + conv_w.shape, F32))
    return pl.pallas_call(
        body, name="gather_w_in", out_shape=out_shape,
        in_specs=[VMEM_SPEC] * 5, out_specs=[VMEM_SPEC] * 5,
        scratch_shapes=[pltpu.SemaphoreType.DMA((9,)), pltpu.SemaphoreType.DMA((9,))],
        compiler_params=pltpu.CompilerParams(vmem_limit_bytes=40 * MIB),
    )(w_in, w_out, w_pg, w_pp, conv_w)


def _gather_rest_start(lands):
    n = len(lands)

    def body(*refs):
        land_in = refs[0:n]
        ssem, rsem = refs[n], refs[n + 1]
        token = refs[2 * n + 2]
        x, y, c, chips = _mesh_pos()
        kme = 2 * x + y
        for p, land in enumerate(land_in):
            hrows = land.shape[1] // 2
            mine = pl.ds(pl.multiple_of(c * hrows, 128), hrows)
            for px, py in chips:
                for pc in range(2):
                    _remote(land.at[kme, mine], land.at[kme, mine], ssem.at[p], rsem.at[p], (px, py, pc)).start()
        token[...] = jnp.zeros_like(token)

    out_shape = ([pltpu.SemaphoreType.DMA((n,)), pltpu.SemaphoreType.DMA((n,))]
                 + [pltpu.HBM(a.shape, a.dtype) for a in lands] + [jax.ShapeDtypeStruct((8, 128), F32)])
    outs = pl.pallas_call(
        body, name="gather_rest_start", out_shape=out_shape,
        in_specs=[HBM_SPEC] * n, out_specs=[SEM_SPEC, SEM_SPEC] + [HBM_SPEC] * n + [VMEM_SPEC],
        input_output_aliases={i: 2 + i for i in range(n)},
        compiler_params=pltpu.CompilerParams(has_side_effects=EFFECT),
    )(*[pltpu.with_memory_space_constraint(a, pltpu.HBM) for a in lands])
    return outs[0], outs[1], list(outs[2:2 + n]), outs[2 + n]


def _gather_rest_wait(ssem, rsem, lands, after):
    n = len(lands)

    def body(*refs):
        land_in = refs[0:n]
        ssem_ref, rsem_ref = refs[n], refs[n + 1]
        x, y, c = lax.axis_index("x"), lax.axis_index("y"), lax.axis_index("c")
        for p, land in enumerate(land_in):
            three = land.at[pl.ds(0, 3)]
            cp = _remote(three, three, ssem_ref.at[p], rsem_ref.at[p], (x, y, c))
            cp.wait_send()
            cp.wait_recv()

    outs = pl.pallas_call(
        body, name="gather_rest_wait", out_shape=[pltpu.HBM(a.shape, a.dtype) for a in lands],
        in_specs=[HBM_SPEC] * n + [SEM_SPEC, SEM_SPEC, ANY_SPEC], out_specs=[HBM_SPEC] * n,
        input_output_aliases={i: i for i in range(n)},
        compiler_params=pltpu.CompilerParams(has_side_effects=EFFECT),
    )(*lands, ssem, rsem, after)
    return list(outs)


def _rs_start(name, parts, ks, lands, chip_sums=False):
    n = len(parts)

    def body(*refs):
        part_in, land_in = refs[0:n], refs[n:2 * n]
        ssem, rsem = refs[2 * n], refs[2 * n + 1]
        token = refs[4 * n + 2]
        x, y, c = lax.axis_index("x"), lax.axis_index("y"), lax.axis_index("c")
        kme = 2 * x + y
        me = 4 * x + 2 * y + c
        for p in range(n):
            hrows = land_in[p].shape[1]
            for i, k in enumerate(ks):
                if chip_sums:
                    @pl.when(kme != k)
                    def _():
                        _remote(part_in[p].at[i], land_in[p].at[kme], ssem.at[p], rsem.at[p], (k // 2, k % 2, c)).start()
                    continue
                for pc in range(2):
                    @pl.when(jnp.logical_or(kme != k, c != pc))
                    def _():
                        _remote(part_in[p].at[i, pl.ds(pc * hrows, hrows)], land_in[p].at[me],
                                ssem.at[p], rsem.at[p], (k // 2, k % 2, pc)).start()
        token[...] = jnp.zeros_like(token)

    arrays = list(parts) + list(lands)
    out_shape = ([pltpu.SemaphoreType.DMA((n,)), pltpu.SemaphoreType.DMA((n,))]
                 + [pltpu.HBM(a.shape, a.dtype) for a in arrays] + [jax.ShapeDtypeStruct((8, 128), F32)])
    outs = pl.pallas_call(
        body, name=name, out_shape=out_shape,
        in_specs=[HBM_SPEC] * (2 * n), out_specs=[SEM_SPEC, SEM_SPEC] + [HBM_SPEC] * (2 * n) + [VMEM_SPEC],
        input_output_aliases={i: 2 + i for i in range(2 * n)},
        compiler_params=pltpu.CompilerParams(has_side_effects=EFFECT),
    )(*[pltpu.with_memory_space_constraint(a, pltpu.HBM) for a in arrays])
    return outs[0], outs[1], list(outs[2:2 + n]), list(outs[2 + n:2 + 2 * n]), outs[2 + 2 * n]


def _rs_wait(name, ssem, rsem, parts, ks, lands, after, chip_sums=False):
    n = len(parts)

    def body(*refs):
        part_in, land_in = refs[0:n], refs[n:2 * n]
        ssem_ref, rsem_ref = refs[2 * n], refs[2 * n + 1]
        x, y, c = lax.axis_index("x"), lax.axis_index("y"), lax.axis_index("c")
        kme = 2 * x + y
        for p in range(n):
            piece = land_in[p].at[0]
            for k in ks:
                for pc in range(1 if chip_sums else 2):
                    mine = (kme == k) if chip_sums else jnp.logical_and(kme == k, c == pc)

                    @pl.when(jnp.logical_not(mine))
                    def _():
                        _remote(piece, piece, ssem_ref.at[p], rsem_ref.at[p], (x, y, c)).wait_send()
            owner = kme == ks[0]
            for k in ks[1:]:
                owner = jnp.logical_or(owner, kme == k)

            @pl.when(owner)
            def _():
                others = land_in[p].at[pl.ds(0, land_in[p].shape[0] - 1)]
                _remote(others, others, ssem_ref.at[p], rsem_ref.at[p], (x, y, c)).wait_recv()

    arrays = list(parts) + list(lands)
    outs = pl.pallas_call(
        body, name=name, out_shape=[pltpu.HBM(a.shape, a.dtype) for a in arrays],
        in_specs=[HBM_SPEC] * (2 * n) + [SEM_SPEC, SEM_SPEC, ANY_SPEC], out_specs=[HBM_SPEC] * (2 * n),
        input_output_aliases={i: i for i in range(2 * n)},
        compiler_params=pltpu.CompilerParams(has_side_effects=EFFECT),
    )(*arrays, ssem, rsem, after)
    return list(outs[0:n]), list(outs[n:2 * n])


def _reduce_exchange(parts, lands):
    shapes = [(2 * l.shape[1], l.shape[2]) for l in lands]
    step = 128

    def body(in01, in23, pout, ppg, ppp, l_in, l_out, l_pg, l_pp, g_in, g_out, g_pg, g_pp, ssem, rsem):
        x, y, c = lax.axis_index("x"), lax.axis_index("y"), lax.axis_index("c")
        kme = 2 * x + y
        me = 4 * x + 2 * y + c
        sibling = (x, y, 1 - c)
        sends = []
        for p, (land, gout) in enumerate([(l_in, g_in), (l_out, g_out), (l_pg, g_pg), (l_pp, g_pp)]):
            hrows = land.shape[1]
            mine0 = pl.multiple_of(c * hrows, step)
            for r0 in range(0, hrows, step):
                rs = pl.ds(mine0 + r0, step)
                if p == 0:
                    own = jnp.where(kme >= 2, in23[kme & 1, r0:r0 + step, :], in01[kme & 1, r0:r0 + step, :])
                    slot = kme
                else:
                    own = (pout, ppg, ppp)[p - 1][kme, rs, :]
                    slot = me
                s = jnp.zeros((step, land.shape[2]), F32)
                for j in range(land.shape[0]):
                    s = s + jnp.where(slot == j, own, land[j, r0:r0 + step, :]).astype(F32)
                gout[rs, :] = s
            mine = pl.ds(mine0, hrows)
            cp = _remote(gout.at[mine], gout.at[mine], ssem.at[p], rsem.at[p], sibling)
            cp.start()
            sends.append(cp)
        for p, (land, gout) in enumerate([(l_in, g_in), (l_out, g_out), (l_pg, g_pg), (l_pp, g_pp)]):
            hrows = land.shape[1]
            other = pl.ds(pl.multiple_of((1 - c) * hrows, step), hrows)
            _remote(gout.at[other], gout.at[other], ssem.at[p], rsem.at[p], sibling).wait_recv()
        for cp in sends:
            cp.wait_send()

    return pl.pallas_call(
        body, name="reduce_exchange", out_shape=[jax.ShapeDtypeStruct(s, F32) for s in shapes],
        in_specs=[VMEM_SPEC] * 9, out_specs=[VMEM_SPEC] * 4,
        scratch_shapes=[pltpu.SemaphoreType.DMA((4,)), pltpu.SemaphoreType.DMA((4,))],
        compiler_params=pltpu.CompilerParams(vmem_limit_bytes=48 * MIB),
    )(*parts, *lands)


def _small_allreduce(sm_in, sm_tail, sm_a, sm_b, rg_c):
    PR = DG // NDEV

    def body(in_ref, tail_ref, a_ref, b_ref, rg_ref, v_out, rg_out, vbuf, vrecv, rgrecv, ssem, rsem):
        x, y, c = lax.axis_index("x"), lax.axis_index("y"), lax.axis_index("c")
        me = 4 * x + 2 * y + c

        def pair(ref, r0, r1):
            return jnp.concatenate([ref[r0:r0 + 1, :], ref[r1:r1 + 1, :]], axis=1)

        rows = {
            ROW_NORM_MIX: in_ref[0:1, :], ROW_FINAL_NORM: tail_ref[1:2, :], ROW_B_PG: tail_ref[2:3, :],
            ROW_PLE_NORM: tail_ref[3:4, :], ROW_CB_BA: pair(a_ref, 0, 1), ROW_BX_LAM: pair(a_ref, 2, 3),
            ROW_CW01: pair(a_ref, 4, 5), ROW_CW23: pair(a_ref, 6, 7), ROW_HG_LB: pair(b_ref, 2, 3),
            ROW_HG_NW: jnp.concatenate([b_ref[1:2, :], jnp.zeros((1, DG), F32)], axis=1),
            ROW_LOSS: tail_ref[4:5, :],
        }
        vbuf[...] = jnp.zeros_like(vbuf)
        for r, row in rows.items():
            for j in range(NDEV):
                vbuf[j, r:r + 1, :] = row[:, j * 128:(j + 1) * 128]

        def peer(mask):
            px = x ^ ((mask >> 2) & 1)
            py = y ^ ((mask >> 1) & 1)
            pc = c ^ (mask & 1)
            return (px, py, pc), 4 * px + 2 * py + pc

        def rg_rows(r):
            return pl.ds(pl.multiple_of(r * PR, PR), PR)

        first = []
        for mask in range(1, NDEV):
            dev, r = peer(mask)
            i = mask - 1
            cp = _remote(vbuf.at[r], vrecv.at[i], ssem.at[i], rsem.at[i], dev)
            cp.start()
            first.append(cp)
            cp = _remote(rg_ref.at[rg_rows(r)], rgrecv.at[i], ssem.at[7 + i], rsem.at[7 + i], dev)
            cp.start()
            first.append(cp)
        sv = vbuf[me]
        sr = rg_ref[rg_rows(me), :]
        for i in range(NDEV - 1):
            first[2 * i].wait_recv()
            first[2 * i + 1].wait_recv()
            sv = sv + vrecv[i]
            sr = sr + rgrecv[i]
        v_out[me] = sv
        rg_out[rg_rows(me), :] = sr
        second = []
        for mask in range(1, NDEV):
            dev, r = peer(mask)
            i = mask - 1
            cp = _remote(v_out.at[me], v_out.at[me], ssem.at[14 + i], rsem.at[14 + i], dev)
            cp.start()
            second.append(cp)
            cp = _remote(rg_out.at[rg_rows(me)], rg_out.at[rg_rows(me)], ssem.at[21 + i], rsem.at[21 + i], dev)
            cp.start()
            second.append(cp)
        for mask in range(1, NDEV):
            dev, r = peer(mask)
            i = mask - 1
            _remote(v_out.at[r], v_out.at[r], ssem.at[14 + i], rsem.at[14 + i], dev).wait_recv()
            _remote(rg_out.at[rg_rows(r)], rg_out.at[rg_rows(r)], ssem.at[21 + i], rsem.at[21 + i], dev).wait_recv()
        for cp in first + second:
            cp.wait_send()

    return pl.pallas_call(
        body, name="small_allreduce",
        out_shape=[jax.ShapeDtypeStruct((NDEV, VROWS, 128), F32), jax.ShapeDtypeStruct((DG, 128), F32)],
        in_specs=[VMEM_SPEC] * 5, out_specs=[VMEM_SPEC] * 2,
        scratch_shapes=[pltpu.VMEM((NDEV, VROWS, 128), F32), pltpu.VMEM((NDEV - 1, VROWS, 128), F32),
                        pltpu.VMEM((NDEV - 1, PR, 128), F32),
                        pltpu.SemaphoreType.DMA((28,)), pltpu.SemaphoreType.DMA((28,))],
    )(sm_in, sm_tail, sm_a, sm_b, rg_c)


def _adam_rows(w, g, m, v):
    m2 = ADAM_B1 * m + (1.0 - ADAM_B1) * g
    v2 = ADAM_B2 * v + (1.0 - ADAM_B2) * (g * g)
    m_hat = m2 / (1.0 - ADAM_B1 ** ADAM_STEP)
    v_hat = v2 / (1.0 - ADAM_B2 ** ADAM_STEP)
    delta = -ADAM_LR * (m_hat / (jnp.sqrt(v_hat) + ADAM_EPS) + ADAM_WD * w)
    return delta, m2, v2


def _adam_big(gs, ws, ms, vs):
    n = len(gs)
    steps = 8

    def body(*refs):
        ins, outs = refs[:4 * n], refs[4 * n:]
        for i in range(n):
            g, w, m, v = (r[...] for r in ins[4 * i:4 * i + 4])
            d, m2, v2 = _adam_rows(w, g, m, v)
            outs[3 * i][...] = d
            outs[3 * i + 1][...] = m2
            outs[3 * i + 2][...] = v2

    in_specs, out_specs, out_shape, args = [], [], [], []
    for g, w, m, v in zip(gs, ws, ms, vs):
        r, c = w.shape
        spec = lambda: pl.BlockSpec((r // steps, c), lambda i: (i, 0))
        in_specs += [spec() for _ in range(4)]
        out_specs += [spec() for _ in range(3)]
        out_shape += [jax.ShapeDtypeStruct((r, c), F32)] * 3
        args += [pltpu.with_memory_space_constraint(a, pltpu.HBM) for a in (g, w, m, v)]
    outs = pl.pallas_call(
        body, name="adam_big", grid=(steps,), in_specs=in_specs, out_specs=out_specs, out_shape=out_shape,
        compiler_params=_cparams(("parallel",), 32),
    )(*args)
    return [tuple(outs[3 * i:3 * i + 3]) for i in range(n)]


_VEC_PARAMS = [
    ("norm_mix_w", ROW_NORM_MIX, 0, D), ("final_norm_w", ROW_FINAL_NORM, 0, D),
    ("b_ple_gate", ROW_B_PG, 0, D), ("ple_norm_w", ROW_PLE_NORM, 0, D),
    ("conv_b", ROW_CB_BA, 0, DG), ("rg_ba", ROW_CB_BA, DG, DG),
    ("rg_bx", ROW_BX_LAM, 0, DG), ("rg_lambda", ROW_BX_LAM, DG, DG),
    ("hg_norm_w", ROW_HG_NW, 0, HD),
]
_SMALL_ORDER = [n for n, _, _, _ in _VEC_PARAMS] + ["hg_lb", "conv_w", "rg_wa", "rg_wx"]


def _adam_small(vred, rgred, ws, ms, vs):
    names = _SMALL_ORDER
    n = len(names)

    def body(vred_ref, rg_ref, *refs):
        w_refs = dict(zip(names, refs[0:n]))
        m_refs = dict(zip(names, refs[n:2 * n]))
        v_refs = dict(zip(names, refs[2 * n:3 * n]))
        outs = refs[3 * n:]
        o_refs = {nm: outs[4 * i:4 * i + 4] for i, nm in enumerate(names)}
        kme = 2 * lax.axis_index("x") + lax.axis_index("y")

        def update(nm, g, idx):
            d, m2, v2 = _adam_rows(w_refs[nm][idx], g, m_refs[nm][idx], v_refs[nm][idx])
            og, od, om, ov = o_refs[nm]
            og[idx] = g
            od[idx] = d
            om[idx] = m2
            ov[idx] = v2

        def packed(row, lane0, width):
            return jnp.concatenate([vred_ref[j, row:row + 1, :] for j in range(lane0 // 128, (lane0 + width) // 128)],
                                   axis=1)

        everything = (slice(None), slice(None))
        for nm, row, lane0, width in _VEC_PARAMS:
            update(nm, packed(row, lane0, width), everything)
        for r in range(2):
            update("hg_lb", packed(ROW_HG_LB, r * DG, DG), (slice(r, r + 1), slice(None)))
        for j in range(4):
            g = vred_ref[(j % 2) * 4 + kme, ROW_CW01 + j // 2:ROW_CW01 + j // 2 + 1, :]
            update("conv_w", g, (slice(j, j + 1), slice(None)))
        for r0 in range(0, DG, 128):
            rs = (slice(r0, r0 + 128), slice(None))
            both = rg_ref[r0:r0 + 128, :]
            update("rg_wa", both[:, 0:RGB], rs)
            update("rg_wx", pltpu.roll(both, RGB, 1)[:, 0:RGB], rs)

    args = [vred, rgred] + [d[nm] for d in (ws, ms, vs) for nm in names]
    out_shape = []
    for nm in names:
        out_shape += [jax.ShapeDtypeStruct(ws[nm].shape, F32)] * 4
    whole = lambda s: pl.BlockSpec(s.shape, lambda i, nd=len(s.shape): (0,) * nd)
    outs = pl.pallas_call(
        body, name="adam_small", out_shape=out_shape, grid=(1,),
        in_specs=[whole(a) for a in args], out_specs=[whole(s) for s in out_shape],
    )(*args)
    return {nm: tuple(outs[4 * i:4 * i + 4]) for i, nm in enumerate(names)}


def _fwd_inproj(x, nw, w_in_b, dep):
    tm = 512

    def body(x_ref, nw_ref, w_ref, dep_ref, pf_ref, pb_ref, u_ref):
        xv = x_ref[...]
        s = lax.rsqrt(jnp.mean(xv * xv, axis=-1, keepdims=True) + EPS)
        u = (xv * s * nw_ref[...]).astype(BF16)
        u_ref[...] = u
        r = [jnp.dot(u, w_ref[k], preferred_element_type=F32) for k in range(NSHARD)]
        h = DG // 2
        pf_ref[:, 0:DG] = r[0][:, 0:DG]
        pb_ref[:, 0:h] = r[0][:, DG:SHW].astype(BF16)
        pb_ref[:, h:DG] = r[1][:, 0:h].astype(BF16)
        pb_ref[:, DG:2 * DG] = r[1][:, h:SHW].astype(BF16)
        pf_ref[:, DG:2 * DG] = r[2][:, 0:DG]
        pb_ref[:, 2 * DG:2 * DG + h] = r[2][:, DG:SHW].astype(BF16)
        pb_ref[:, 2 * DG + h:3 * DG] = r[3][:, 0:h].astype(BF16)
        pb_ref[:, 3 * DG:4 * DG] = r[3][:, h:SHW].astype(BF16)

    return pl.pallas_call(
        body, name="fwd_inproj", grid=(T // tm,),
        in_specs=[pl.BlockSpec((tm, D), lambda i: (i, 0)),
                  pl.BlockSpec((1, D), lambda i: (0, 0)),
                  pl.BlockSpec((NSHARD, D, SHW), lambda i: (0, 0, 0), pipeline_mode=pl.Buffered(1)), ANY_SPEC],
        out_specs=[pl.BlockSpec((tm, 2 * DG), lambda i: (i, 0)),
                   pl.BlockSpec((tm, 4 * DG), lambda i: (i, 0)),
                   pl.BlockSpec((tm, D), lambda i: (i, 0))],
        out_shape=[jax.ShapeDtypeStruct((T, 2 * DG), F32), jax.ShapeDtypeStruct((T, 4 * DG), BF16),
                   jax.ShapeDtypeStruct((T, D), BF16)],
        compiler_params=_cparams(("parallel",), 48),
    )(x, nw, w_in_b, dep)


def _conv_rows(cw_ref):
    return [jnp.concatenate([cw_ref[k, j:j + 1, :] for k in range(NSHARD)], axis=1) for j in range(4)]


def _rg_conv(xa, prev8, cw, cb, rows):
    taps = [_shift_down(xa, prev8, 3, rows), _shift_down(xa, prev8, 2, rows),
            _shift_down(xa, prev8, 1, rows), xa]
    xc = cb
    for j in range(4):
        xc = xc + taps[j] * cw[j]
    return xc, taps


def _block_mask():
    r = lax.broadcasted_iota(I32, (DG, DG), 0)
    c = lax.broadcasted_iota(I32, (DG, DG), 1)
    return (r >> 6) == (c >> 6)


def _dense_from_blocks(wc):
    j = lax.broadcasted_iota(I32, (RGB, DG), 0)
    c = lax.broadcasted_iota(I32, (RGB, DG), 1)
    spread = _mm_exact(wc, ((c & (RGB - 1)) == j).astype(F32))
    return jnp.where(_block_mask(), spread, 0.0)


def _blocks_from_dense(da, dx):
    c = lax.broadcasted_iota(I32, (DG, 128), 0)
    j = lax.broadcasted_iota(I32, (DG, 128), 1)
    hit = (c & (RGB - 1)) == (j & (RGB - 1))
    mask = _block_mask()
    return (_mm_exact(jnp.where(mask, da, 0.0), (hit & (j < RGB)).astype(F32))
            + _mm_exact(jnp.where(mask, dx, 0.0), (hit & (j >= RGB)).astype(F32)))


def _rg_gates(xc, wa, ba, wx, bx, sp, first_row):
    r = _sigmoid(_mm(xc, wa) + ba)
    i = _sigmoid(_mm(xc, wx) + bx)
    log_a = (-RG_C) * r * sp
    a = jnp.exp(log_a)
    a2 = a * a
    one_m_a2 = -jnp.tanh(log_a) * (a2 + 1.0)
    mult = jnp.where(first_row, 1.0, jnp.sqrt(one_m_a2))
    return r, i, a, a2, mult


def _softplus(z):
    return jnp.maximum(z, 0.0) + jnp.log1p(jnp.exp(-jnp.abs(z)))


def _fwd_rglru(pf, pb, cw3, conv_b, wa_c, ba, wx_c, bx, lam):
    tm = 512
    ng = tm // 8

    def body(xa_ref, ga_ref, cw_ref, cb_ref, wa_ref, ba_ref, wx_ref, bx_ref, lam_ref,
             h_ref, ya_ref, a_s, u_s, tail_s, hc_s, wa_s, wx_s):
        i = pl.program_id(0)

        @pl.when(i == 0)
        def _():
            tail_s[...] = jnp.zeros_like(tail_s)
            hc_s[...] = jnp.zeros_like(hc_s)
            wa_s[...] = _dense_from_blocks(wa_ref[...]).astype(BF16)
            wx_s[...] = _dense_from_blocks(wx_ref[...]).astype(BF16)

        rows = lax.broadcasted_iota(I32, (tm, DG), 0)
        xa = xa_ref[...]
        xc, _ = _rg_conv(xa, tail_s[...], _conv_rows(cw_ref), cb_ref[...], rows)
        tail_s[...] = xa[tm - 8:tm, :]
        sp = _softplus(-lam_ref[...])
        first_row = (rows + i * tm) == 0
        r, ig, a, a2, mult = _rg_gates(xc, wa_s[...], ba_ref[...], wx_s[...], bx_ref[...], sp, first_row)
        av, uv = a, mult * (ig * xc)
        rows8 = rows & 7
        for d in (1, 2, 4):
            keep = rows8 >= d
            uv = uv + av * jnp.where(keep, _roll_in_groups(uv, d), 0.0)
            av = av * jnp.where(keep, _roll_in_groups(av, d), 1.0)
        a_s[...] = av
        u_s[...] = uv
        carry = hc_s[0:1, :]
        for g in range(ng):
            sl = slice(g * 8, (g + 1) * 8)
            ab, ub = a_s[sl, :], u_s[sl, :]
            h_ref[sl, :] = ub + ab * carry
            carry = ub[7:8, :] + ab[7:8, :] * carry
        hc_s[0:1, :] = carry
        ga = ga_ref[...].astype(F32)
        sg = _sigmoid(ga)
        ya_ref[...] = (h_ref[...] * (ga * sg)).astype(BF16)

    vec = lambda: pl.BlockSpec((1, DG), lambda i: (0, 0))
    blocks = lambda: pl.BlockSpec((DG, RGB), lambda i: (0, 0))
    return pl.pallas_call(
        body, name="fwd_rglru", grid=(T // tm,),
        in_specs=[pl.BlockSpec((tm, DG), lambda i: (i, 0)),
                  pl.BlockSpec((tm, DG), lambda i: (i, 0)),
                  pl.BlockSpec((NSHARD, 4, 128), lambda i: (0, 0, 0)), vec(),
                  blocks(), vec(), blocks(), vec(), vec()],
        out_specs=[pl.BlockSpec((tm, DG), lambda i: (i, 0)),
                   pl.BlockSpec((tm, DG), lambda i: (i, 0))],
        out_shape=[jax.ShapeDtypeStruct((T, DG), F32), jax.ShapeDtypeStruct((T, DG), BF16)],
        scratch_shapes=[pltpu.VMEM((tm, DG), F32), pltpu.VMEM((tm, DG), F32),
                        pltpu.VMEM((8, DG), F32), pltpu.VMEM((8, DG), F32),
                        pltpu.VMEM((DG, DG), BF16), pltpu.VMEM((DG, DG), BF16)],
        compiler_params=_cparams(("arbitrary",), 48),
    )(pf, pb, cw3, conv_b, wa_c, ba, wx_c, bx, lam)


def _hg_lower_bound(lb_ref):
    return _sig_pair(lb_ref[0:1, :] - lb_ref[1:2, :])


def _hg_gates(fz, lb, one_m_lb):
    sg, sn = _sig_pair(fz)
    f = lb + one_m_lb * sg
    return sg, sn, f, jnp.log(f), one_m_lb * sn


def _tri(lower):
    r = lax.broadcasted_iota(I32, (CH, CH), 0)
    c = lax.broadcasted_iota(I32, (CH, CH), 1)
    return (r >= c) if lower else (r <= c)


def _chunk_cumsum(v, rows64):
    for d in (1, 2, 4, 8, 16, 32):
        v = v + jnp.where(rows64 >= d, pltpu.roll(v, d, 0), 0.0)
    return v


def _chunk_rev_cumsum(v, rows64):
    n = v.shape[0]
    for d in (1, 2, 4, 8, 16, 32):
        v = v + jnp.where(rows64 < CH - d, pltpu.roll(v, n - d, 0), 0.0)
    return v


def _hg_recompute(q_ref, f_ref, lb, one_m_lb, rows64, eb_s, enb_s, ekd_s, qe_s, ke_s, kd_s, dec_s):
    nc = q_ref.shape[0] // CH
    sg, sn, f, logf, k = _hg_gates(f_ref[...], lb, one_m_lb)
    q = q_ref[...].astype(F32)
    sq = _sigmoid(q)
    qs = q * sq * (HD ** -0.5)
    b = _chunk_cumsum(logf, rows64)
    for c in range(nc):
        rs = slice(c * CH, (c + 1) * CH)
        b_c = b[rs]
        bl = b_c[CH - 1:CH, :]
        eb, enb, ekd = jnp.exp(b_c), jnp.exp(-b_c), jnp.exp(bl - b_c)
        if eb_s is not None:
            eb_s[rs, :] = eb
            enb_s[rs, :] = enb
            ekd_s[rs, :] = ekd
        qe_s[rs, :] = (qs[rs] * eb).astype(BF16)
        ke_s[rs, :] = (k[rs] * enb).astype(BF16)
        kd_s[rs, :] = (k[rs] * ekd).astype(BF16)
        dec_s[c:c + 1, :] = jnp.exp(bl)
    return sg, sn, f, k, q, sq, qs


def _fwd_hgrn2(pf, pb, hg_lb, hg_nw):
    tm = 512
    nc = tm // CH

    def body(q_ref, f_ref, v_ref, g_ref, lb_ref, nw_ref, yb_ref, o_ref, sp_ref,
             st_s, qe_s, ke_s, kd_s, vb_s, dec_s, p_s, ds_s):
        i = pl.program_id(0)

        @pl.when(i == 0)
        def _():
            st_s[...] = jnp.zeros_like(st_s)

        lb, one_m_lb = _hg_lower_bound(lb_ref)
        rows64 = lax.broadcasted_iota(I32, (tm, DG), 0) & (CH - 1)
        _hg_recompute(q_ref, f_ref, lb, one_m_lb, rows64, None, None, None, qe_s, ke_s, kd_s, dec_s)
        vb_s[...] = v_ref[...]
        mask = _tri(True)
        items = [(c, hd, slice(c * CH, (c + 1) * CH), slice(hd * HD, (hd + 1) * HD))
                 for c in range(nc) for hd in range(NH)]
        for c, hd, rs, cols in items:
            p_s[c * NH + hd] = jnp.where(mask, _mm_nt(qe_s[rs, cols], ke_s[rs, cols]), 0.0).astype(BF16)
            ds_s[c * NH + hd] = _mm_tn(vb_s[rs, cols], kd_s[rs, cols])
        for c, hd, rs, cols in items:
            st = st_s[hd]
            sp_ref[hd, c] = st
            st_s[hd] = st * dec_s[c:c + 1, cols] + ds_s[c * NH + hd]
        for c, hd, rs, cols in items:
            o_ref[rs, cols] = _mm(p_s[c * NH + hd], vb_s[rs, cols]) + _mm_nt(qe_s[rs, cols], sp_ref[hd, c])
        nw = nw_ref[...]
        for hd in range(NH):
            cols = slice(hd * HD, (hd + 1) * HD)
            o = o_ref[:, cols]
            so = lax.rsqrt(jnp.mean(o * o, axis=-1, keepdims=True) + EPS)
            g = g_ref[:, cols].astype(F32)
            sg = _sigmoid(g)
            yb_ref[:, cols] = (o * so * nw * (g * sg)).astype(BF16)

    col = lambda j: pl.BlockSpec((tm, DG), lambda i: (i, j))
    return pl.pallas_call(
        body, name="fwd_hgrn2", grid=(T // tm,),
        in_specs=[col(1), col(1), col(2), col(3),
                  pl.BlockSpec((2, DG), lambda i: (0, 0)),
                  pl.BlockSpec((1, HD), lambda i: (0, 0))],
        out_specs=[pl.BlockSpec((tm, DG), lambda i: (i, 0)),
                   pl.BlockSpec((tm, DG), lambda i: (i, 0)),
                   pl.BlockSpec((NH, nc, HD, HD), lambda i: (0, i, 0, 0))],
        out_shape=[jax.ShapeDtypeStruct((T, DG), BF16), jax.ShapeDtypeStruct((T, DG), F32),
                   jax.ShapeDtypeStruct((NH, NCHUNK, HD, HD), F32)],
        scratch_shapes=[pltpu.VMEM((NH, HD, HD), F32),
                        pltpu.VMEM((tm, DG), BF16), pltpu.VMEM((tm, DG), BF16), pltpu.VMEM((tm, DG), BF16),
                        pltpu.VMEM((tm, DG), BF16), pltpu.VMEM((nc, DG), F32),
                        pltpu.VMEM((nc * NH, CH, CH), BF16), pltpu.VMEM((nc * NH, HD, HD), F32)],
        compiler_params=_cparams(("arbitrary",), 48),
    )(pb, pf, pb, pb, hg_lb, hg_nw)


def _tail_fwd_bwd(x, p, tgt, ya, yb, w_out_b, w_pg_b, w_pp_b, ple_nw, b_pg, fnw):
    tm = 512
    nt = T // tm
    QR = D // NSHARD

    def body(x_ref, p_ref, t_ref, ya_ref, yb_ref, wo_ref, wg_ref, wp_ref, pw_ref, b_ref, fw_ref,
             dh1_ref, dyab_ref, dwo_ref, dwg_ref, dwp_ref, sm_ref, dwo_s, dwg_s, dwp_s):
        i = pl.program_id(0)

        @pl.when(i == 0)
        def _():
            dwo_s[...] = jnp.zeros_like(dwo_s)
            dwg_s[...] = jnp.zeros_like(dwg_s)
            dwp_s[...] = jnp.zeros_like(dwp_s)
            sm_ref[...] = jnp.zeros_like(sm_ref)

        ya = ya_ref[...]
        yb = yb_ref[...]
        pv = p_ref[...].astype(BF16)
        pw = pw_ref[...]
        fw = fw_ref[...]
        h1 = x_ref[...] + _mm(ya, wo_ref[0:DG, :]) + _mm(yb, wo_ref[DG:D, :])
        s2 = lax.rsqrt(jnp.mean(h1 * h1, axis=-1, keepdims=True) + EPS)
        n2h = h1 * s2
        n2 = (n2h * pw).astype(BF16)
        z = _mm(n2, wg_ref[...]) + b_ref[...]
        gate = _sigmoid(z)
        pp = jnp.concatenate([_mm(pv, wp_ref[k]) for k in range(NSHARD)], axis=1)
        h2 = h1 + gate * pp
        s3 = lax.rsqrt(jnp.mean(h2 * h2, axis=-1, keepdims=True) + EPS)
        hn = h2 * s3
        err = hn * fw - t_ref[...]
        sm_ref[0:1, :] += _rowsum(err * err)
        dy = err * (1.0 / D)
        sm_ref[1:2, :] += _rowsum(dy * hn)
        g3 = dy * fw
        dh2 = s3 * (g3 - hn * jnp.mean(g3 * hn, axis=-1, keepdims=True))
        dpp = (dh2 * gate).astype(BF16)
        dz = dh2 * pp * gate * (1.0 - gate)
        sm_ref[2:3, :] += _rowsum(dz)
        dzb = dz.astype(BF16)
        dwg_s[...] += _mm_tn(n2, dzb)
        dn2 = _mm_nt(dzb, wg_ref[...])
        for k in range(NSHARD):
            dwp_s[k] += _mm_tn(pv, dpp[:, k * PLE:(k + 1) * PLE])
        sm_ref[3:4, :] += _rowsum(dn2 * n2h)
        g2 = dn2 * pw
        dh1 = dh2 + s2 * (g2 - n2h * jnp.mean(g2 * n2h, axis=-1, keepdims=True))
        dh1_ref[...] = dh1
        dh1b = dh1.astype(BF16)
        dyab_ref[...] = _mm_nt(dh1b, wo_ref[...])
        dwo_s[0:DG, :] += _mm_tn(ya, dh1b)
        dwo_s[DG:D, :] += _mm_tn(yb, dh1b)

        @pl.when(i == nt - 1)
        def _():
            total = jnp.sum(sm_ref[0:1, :], axis=-1, keepdims=True) * (0.5 / D)
            sm_ref[4:5, :] = jnp.broadcast_to(total, (1, D))
            for k in range(NSHARD):
                dwo_ref[k] = dwo_s[k * QR:(k + 1) * QR, :].astype(BF16)
                dwg_ref[k] = dwg_s[k * QR:(k + 1) * QR, :].astype(BF16)
                dwp_ref[k] = dwp_s[k].astype(BF16)

    row = lambda w: pl.BlockSpec((tm, w), lambda i: (i, 0))
    const2 = lambda s: pl.BlockSpec(s, lambda i: (0, 0), pipeline_mode=pl.Buffered(1))
    const3 = lambda s: pl.BlockSpec(s, lambda i: (0, 0, 0), pipeline_mode=pl.Buffered(1))
    return pl.pallas_call(
        body, name="tail_fwd_bwd", grid=(nt,),
        in_specs=[row(D), row(PLE), row(D), row(DG), row(DG),
                  const2((D, D)), const2((D, D)), const3((NSHARD, PLE, PLE)),
                  const2((1, D)), const2((1, D)), const2((1, D))],
        out_specs=[row(D), row(D), const3((NSHARD, QR, D)), const3((NSHARD, QR, D)),
                   const3((NSHARD, PLE, PLE)), const2((8, D))],
        out_shape=[jax.ShapeDtypeStruct((T, D), F32), jax.ShapeDtypeStruct((T, D), F32),
                   jax.ShapeDtypeStruct((NSHARD, QR, D), BF16), jax.ShapeDtypeStruct((NSHARD, QR, D), BF16),
                   jax.ShapeDtypeStruct((NSHARD, PLE, PLE), BF16), jax.ShapeDtypeStruct((8, D), F32)],
        scratch_shapes=[pltpu.VMEM((D, D), F32), pltpu.VMEM((D, D), F32), pltpu.VMEM((NSHARD, PLE, PLE), F32)],
        compiler_params=_cparams(("arbitrary",), 62),
    )(x, p, tgt, ya, yb, w_out_b, w_pg_b, w_pp_b, ple_nw, b_pg, fnw)


def _bwd_rglru(pf, pb, h, dyab, cw3, conv_b, wa_c, ba, wx_c, bx, lam, dep):
    tm = 512
    nt = T // tm
    ng = tm // 8

    def body(xa_ref, ga_ref, xp_ref, h_ref, hp_ref, dya_ref, cw_ref, cb_ref, wa_ref, ba_ref, wx_ref, bx_ref,
             lam_ref, dep_ref, da_ref, rg_ref, sm_ref, a_s, g_s, cg_s, nxt_s, wa_s, wx_s, dwa_s, dwx_s):
        i = pl.program_id(0)
        tile = nt - 1 - i

        @pl.when(i == 0)
        def _():
            dwa_s[...] = jnp.zeros_like(dwa_s)
            dwx_s[...] = jnp.zeros_like(dwx_s)
            sm_ref[...] = jnp.zeros_like(sm_ref)
            cg_s[...] = jnp.zeros_like(cg_s)
            nxt_s[...] = jnp.zeros_like(nxt_s)
            wa_s[...] = _dense_from_blocks(wa_ref[...]).astype(BF16)
            wx_s[...] = _dense_from_blocks(wx_ref[...]).astype(BF16)

        rows = lax.broadcasted_iota(I32, (tm, DG), 0)
        has_prev = tile > 0
        xa = xa_ref[...]
        xprev = jnp.where(has_prev, xp_ref[...], 0.0)
        cw = _conv_rows(cw_ref)
        xc, taps = _rg_conv(xa, xprev, cw, cb_ref[...], rows)
        lam_v = lam_ref[...]
        sp = _softplus(-lam_v)
        first_row = (rows + tile * tm) == 0
        r, ig, a, a2, mult = _rg_gates(xc, wa_s[...], ba_ref[...], wx_s[...], bx_ref[...], sp, first_row)
        hv = h_ref[...]
        hprev = jnp.where(has_prev, hp_ref[...], 0.0)
        h_m1 = _shift_down(hv, hprev, 1, rows)
        ga = ga_ref[...].astype(F32)
        sg = _sigmoid(ga)
        dya = dya_ref[...]
        dga = dya * hv * (sg * (1.0 + ga * (1.0 - sg)))

        av = jnp.where(rows == tm - 1, 1.0, pltpu.roll(a, tm - 1, 0))
        gv = dya * (ga * sg)
        rows8 = rows & 7
        for d in (1, 2, 4):
            keep = rows8 < 8 - d
            gv = gv + av * jnp.where(keep, _roll_in_groups(gv, 8 - d), 0.0)
            av = av * jnp.where(keep, _roll_in_groups(av, 8 - d), 1.0)
        a_s[...] = av
        g_s[...] = gv
        carry = cg_s[0:1, :]
        for g in range(ng - 1, -1, -1):
            sl = slice(g * 8, (g + 1) * 8)
            ab, gb = a_s[sl, :], g_s[sl, :]
            g_s[sl, :] = gb + ab * carry
            carry = gb[0:1, :] + ab[0:1, :] * carry
        cg_s[0:1, :] = a[0:1, :] * carry

        gt = g_s[...]
        da = gt * h_m1
        ixc = ig * xc
        di = gt * mult * xc
        dxc = gt * mult * ig
        dlog_a = da * a + jnp.where(first_row, 0.0, gt * ixc * (-a2 / mult))
        sm_ref[3:4, :] += _rowsum(dlog_a * ((-RG_C) * r))
        dpr_f = dlog_a * ((-RG_C) * sp) * r * (1.0 - r)
        dpi_f = di * ig * (1.0 - ig)
        sm_ref[1:2, :] += _rowsum(dpr_f)
        sm_ref[2:3, :] += _rowsum(dpi_f)
        dpr = dpr_f.astype(BF16)
        dpi = dpi_f.astype(BF16)
        xcb = xc.astype(BF16)
        dwa_s[...] += _mm_tn(xcb, dpr)
        dwx_s[...] += _mm_tn(xcb, dpi)
        dxc = dxc + _mm_nt(dpr, wa_s[...]) + _mm_nt(dpi, wx_s[...])
        sm_ref[0:1, :] += _rowsum(dxc)
        for j in range(4):
            sm_ref[4 + j:5 + j, :] += _rowsum(dxc * taps[j])
        nxt = nxt_s[...]
        dxa = (dxc * cw[3] + _shift_up(dxc, nxt, 1, rows) * cw[2]
               + _shift_up(dxc, nxt, 2, rows) * cw[1] + _shift_up(dxc, nxt, 3, rows) * cw[0])
        nxt_s[...] = dxc[0:8, :]
        da_ref[:, 0:DG] = dxa.astype(BF16)
        da_ref[:, DG:D] = dga.astype(BF16)

        @pl.when(i == nt - 1)
        def _():
            _, s_neg = _sig_pair(lam_v)
            sm_ref[3:4, :] = sm_ref[3:4, :] * (-s_neg)
            rg_ref[...] = _blocks_from_dense(dwa_s[...], dwx_s[...])

    vec = lambda: pl.BlockSpec((1, DG), lambda i: (0, 0))
    blocks = lambda: pl.BlockSpec((DG, RGB), lambda i: (0, 0))
    prev8 = lambda: pl.BlockSpec((8, DG), lambda i: (jnp.maximum((nt - 1 - i) * (tm // 8) - 1, 0), 0))
    return pl.pallas_call(
        body, name="bwd_rglru", grid=(nt,),
        in_specs=[pl.BlockSpec((tm, DG), lambda i: (nt - 1 - i, 0)),
                  pl.BlockSpec((tm, DG), lambda i: (nt - 1 - i, 0)),
                  prev8(),
                  pl.BlockSpec((tm, DG), lambda i: (nt - 1 - i, 0)),
                  prev8(),
                  pl.BlockSpec((tm, DG), lambda i: (nt - 1 - i, 0)),
                  pl.BlockSpec((NSHARD, 4, 128), lambda i: (0, 0, 0)), vec(),
                  blocks(), vec(), blocks(), vec(), vec(), ANY_SPEC],
        out_specs=[pl.BlockSpec((tm, D), lambda i: (nt - 1 - i, 0)),
                   pl.BlockSpec((DG, 128), lambda i: (0, 0)),
                   pl.BlockSpec((8, DG), lambda i: (0, 0))],
        out_shape=[jax.ShapeDtypeStruct((T, D), BF16), jax.ShapeDtypeStruct((DG, 128), F32),
                   jax.ShapeDtypeStruct((8, DG), F32)],
        scratch_shapes=[pltpu.VMEM((tm, DG), F32), pltpu.VMEM((tm, DG), F32),
                        pltpu.VMEM((8, DG), F32), pltpu.VMEM((8, DG), F32),
                        pltpu.VMEM((DG, DG), BF16), pltpu.VMEM((DG, DG), BF16),
                        pltpu.VMEM((DG, DG), F32), pltpu.VMEM((DG, DG), F32)],
        compiler_params=_cparams(("arbitrary",), 56),
    )(pf, pb, pf, h, h, dyab, cw3, conv_b, wa_c, ba, wx_c, bx, lam, dep)


def _bwd_hgrn2(pf, pb, o, s_prev, dyab, hg_lb, hg_nw, dep):
    tm = 512
    nt = T // tm
    nc = tm // CH

    def body(q_ref, f_ref, v_ref, g_ref, o_ref, sp_ref, dy_ref, lb_ref, nw_ref, dep_ref, db_ref, sm_ref,
             dst_s, eb_s, enb_s, ekd_s, qe_s, ke_s, kd_s, vb_s, do_s, dec_s, ddec_s, p_s, dp_s,
             g_s, dsta_s, dva_s, dqe_s, dke_s, dkd_s, dlf_s):
        i = pl.program_id(0)

        @pl.when(i == 0)
        def _():
            sm_ref[...] = jnp.zeros_like(sm_ref)
            dst_s[...] = jnp.zeros_like(dst_s)

        lb, one_m_lb = _hg_lower_bound(lb_ref)
        rows64 = lax.broadcasted_iota(I32, (tm, DG), 0) & (CH - 1)
        sg, sn, f, k, q, sq, qs = _hg_recompute(
            q_ref, f_ref, lb, one_m_lb, rows64, eb_s, enb_s, ekd_s, qe_s, ke_s, kd_s, dec_s)
        vb_s[...] = v_ref[...]

        nw = nw_ref[...]
        for hd in range(NH):
            cols = slice(hd * HD, (hd + 1) * HD)
            g = g_ref[:, cols].astype(F32)
            sgg = _sigmoid(g)
            o = o_ref[:, cols]
            so = lax.rsqrt(jnp.mean(o * o, axis=-1, keepdims=True) + EPS)
            oh = o * so
            dyb = dy_ref[:, cols]
            db_ref[:, 3 * DG + hd * HD:3 * DG + (hd + 1) * HD] = (
                dyb * (oh * nw) * (sgg * (1.0 + g * (1.0 - sgg)))).astype(BF16)
            don = dyb * (g * sgg)
            sm_ref[1:2, 0:HD] += _rowsum(don * oh)
            gw = don * nw
            do_s[:, cols] = (so * (gw - oh * jnp.mean(gw * oh, axis=-1, keepdims=True))).astype(BF16)

        mask = _tri(True)
        items = [(c, hd, slice(c * CH, (c + 1) * CH), slice(hd * HD, (hd + 1) * HD))
                 for c in range(nc) for hd in range(NH)]
        for c, hd, rs, cols in items:
            p_s[c * NH + hd] = jnp.where(mask, _mm_nt(qe_s[rs, cols], ke_s[rs, cols]), 0.0).astype(BF16)
            dp_s[c * NH + hd] = jnp.where(mask, _mm_nt(do_s[rs, cols], vb_s[rs, cols]), 0.0).astype(BF16)
        for c, hd, rs, cols in items:
            n = c * NH + hd
            dva_s[rs, cols] = _mm_tn(p_s[n], do_s[rs, cols])
            dqe_s[rs, cols] = _mm(dp_s[n], ke_s[rs, cols])
            dke_s[rs, cols] = _mm_tn(dp_s[n], qe_s[rs, cols])
            g_s[n] = _mm_tn(do_s[rs, cols], qe_s[rs, cols])
        for c, hd, rs, cols in reversed(items):
            n = c * NH + hd
            dst = dst_s[hd]
            dsta_s[n] = dst
            dst_s[hd] = dst * dec_s[c:c + 1, cols] + g_s[n]
        for c, hd, rs, cols in items:
            n = c * NH + hd
            dst = dsta_s[n]
            st_prev = sp_ref[hd, c]
            dv = dva_s[rs, cols] + _mm_nt(kd_s[rs, cols], dst)
            db_ref[rs, 2 * DG + hd * HD:2 * DG + (hd + 1) * HD] = dv.astype(BF16)
            dqe_s[rs, cols] += _mm(do_s[rs, cols], st_prev)
            dkd_s[rs, cols] = _mm(vb_s[rs, cols], dst)
            ddec_s[c:c + 1, cols] = _rowsum(dst * st_prev)

        eb, enb, ekd = eb_s[...], enb_s[...], ekd_s[...]
        dqe, dke, dkd = dqe_s[...], dke_s[...], dkd_s[...]
        t_kd = dkd * (k * ekd)
        rc = _chunk_rev_cumsum(dqe * (qs * eb) - dke * (k * enb) - t_kd, rows64)
        for c in range(nc):
            rs = slice(c * CH, (c + 1) * CH)
            dbl = _rowsum(t_kd[rs]) + ddec_s[c:c + 1, :] * dec_s[c:c + 1, :]
            dlf_s[rs, :] = rc[rs] + dbl
        t = dlf_s[...] / f - (dke * enb + dkd * ekd)
        db_ref[:, DG:2 * DG] = (one_m_lb * sg * sn * t).astype(BF16)
        sm_ref[0:1, :] += _rowsum(sn * t)
        db_ref[:, 0:DG] = (dqe * eb * (sq * (1.0 + q * (1.0 - sq))) * (HD ** -0.5)).astype(BF16)

        @pl.when(i == nt - 1)
        def _():
            dsm = sm_ref[0:1, :] * (lb * one_m_lb)
            sm_ref[2:3, :] = dsm
            sm_ref[3:4, :] = -dsm

    col = lambda j: pl.BlockSpec((tm, DG), lambda i: (nt - 1 - i, j))
    big = lambda dt: pltpu.VMEM((tm, DG), dt)
    return pl.pallas_call(
        body, name="bwd_hgrn2", grid=(nt,),
        in_specs=[col(1), col(1), col(2), col(3),
                  pl.BlockSpec((tm, DG), lambda i: (nt - 1 - i, 0)),
                  pl.BlockSpec((NH, nc, HD, HD), lambda i: (0, nt - 1 - i, 0, 0)),
                  pl.BlockSpec((tm, DG), lambda i: (nt - 1 - i, 1)),
                  pl.BlockSpec((2, DG), lambda i: (0, 0)),
                  pl.BlockSpec((1, HD), lambda i: (0, 0)), ANY_SPEC],
        out_specs=[pl.BlockSpec((tm, 4 * DG), lambda i: (nt - 1 - i, 0)),
                   pl.BlockSpec((8, DG), lambda i: (0, 0))],
        out_shape=[jax.ShapeDtypeStruct((T, 4 * DG), BF16), jax.ShapeDtypeStruct((8, DG), F32)],
        scratch_shapes=[pltpu.VMEM((NH, HD, HD), F32),
                        big(F32), big(F32), big(F32),
                        big(BF16), big(BF16), big(BF16), big(BF16), big(BF16),
                        pltpu.VMEM((nc, DG), F32), pltpu.VMEM((nc, DG), F32),
                        pltpu.VMEM((nc * NH, CH, CH), BF16), pltpu.VMEM((nc * NH, CH, CH), BF16),
                        pltpu.VMEM((nc * NH, HD, HD), F32), pltpu.VMEM((nc * NH, HD, HD), F32),
                        big(F32), big(F32), big(F32), big(F32), big(F32)],
        compiler_params=_cparams(("arbitrary",), 56),
    )(pb, pf, pb, pb, o, s_prev, dyab, hg_lb, hg_nw, dep)


def _dproj_pieces(k, da_ref, db_ref):
    if k == 0:
        return [(da_ref[:, 0:SHW], 0)]
    if k == 1:
        return [(da_ref[:, SHW:D], 0), (db_ref[:, 0:DG], D - SHW)]
    if k == 2:
        return [(db_ref[:, DG:DG + SHW], 0)]
    return [(db_ref[:, DG + SHW:4 * DG], 0)]


def _bwd_inproj_dx(x, dh1, d_a, d_b, w_in_b, nw, dep):
    tm = 512
    nt = T // tm

    def body(x_ref, dh1_ref, da_ref, db_ref, w_ref, nw_ref, dep_ref, dx_ref, sm_ref):
        i = pl.program_id(0)

        @pl.when(i == 0)
        def _():
            sm_ref[...] = jnp.zeros_like(sm_ref)

        du = None
        for k in range(NSHARD):
            for val, off in _dproj_pieces(k, da_ref, db_ref):
                t = _mm_nt(val, w_ref[k, :, off:off + val.shape[1]])
                du = t if du is None else du + t
        xv = x_ref[...]
        s = lax.rsqrt(jnp.mean(xv * xv, axis=-1, keepdims=True) + EPS)
        xh = xv * s
        sm_ref[0:1, :] += _rowsum(du * xh)
        g = du * nw_ref[...]
        dx_ref[...] = dh1_ref[...] + s * (g - xh * jnp.mean(g * xh, axis=-1, keepdims=True))

    row = lambda w: pl.BlockSpec((tm, w), lambda i: (i, 0))
    return pl.pallas_call(
        body, name="bwd_inproj_dx", grid=(nt,),
        in_specs=[row(D), row(D), row(D), row(4 * DG),
                  pl.BlockSpec((NSHARD, D, SHW), lambda i: (0, 0, 0), pipeline_mode=pl.Buffered(1)),
                  pl.BlockSpec((1, D), lambda i: (0, 0)), ANY_SPEC],
        out_specs=[row(D), pl.BlockSpec((8, D), lambda i: (0, 0))],
        out_shape=[jax.ShapeDtypeStruct((T, D), F32), jax.ShapeDtypeStruct((8, D), F32)],
        compiler_params=_cparams(("arbitrary",), 56),
    )(x, dh1, d_a, d_b, w_in_b, nw, dep)


def _bwd_inproj_dw(name, ks, u_b, d_a, d_b):
    tm = 512
    nt = T // tm
    nk = len(ks)
    H = D // 2
    step = 128
    need_a = 0 in ks or 1 in ks

    def body(*refs):
        if need_a:
            u_ref, da_ref, db_ref, dw_ref, acc, send_s, recv_s, ssem, rsem = refs
        else:
            u_ref, db_ref, dw_ref, acc, send_s, recv_s, ssem, rsem = refs
            da_ref = None
        i = pl.program_id(0)

        @pl.when(i == 0)
        def _():
            acc[...] = jnp.zeros_like(acc)

        u = u_ref[...]
        for j, k in enumerate(ks):
            for val, off in _dproj_pieces(k, da_ref, db_ref):
                acc[j, :, off:off + val.shape[1]] += _mm_tn(u, val)

        @pl.when(i == nt - 1)
        def _():
            x, y, c = lax.axis_index("x"), lax.axis_index("y"), lax.axis_index("c")
            sibling = (x, y, 1 - c)
            mine0 = pl.multiple_of(c * H, step)
            other0 = pl.multiple_of((1 - c) * H, step)
            copies = []
            for j in range(nk):
                for r0 in range(0, H, step):
                    send_s[j, r0:r0 + step, :] = acc[j, pl.ds(other0 + r0, step), :].astype(BF16)
                cp = _remote(send_s.at[j], recv_s.at[j], ssem.at[j], rsem.at[j], sibling)
                cp.start()
                copies.append(cp)
            for j in range(nk):
                copies[j].wait_recv()
                for r0 in range(0, H, step):
                    s = acc[j, pl.ds(mine0 + r0, step), :] + recv_s[j, r0:r0 + step, :].astype(F32)
                    dw_ref[j, r0:r0 + step, :] = s.astype(BF16)
            for cp in copies:
                cp.wait_send()

    row = lambda w: pl.BlockSpec((tm, w), lambda i: (i, 0))
    ins = [u_b] + ([d_a] if need_a else []) + [d_b]
    in_specs = [row(D)] + ([row(D)] if need_a else []) + [row(4 * DG)]
    return pl.pallas_call(
        body, name=name, grid=(nt,), in_specs=in_specs,
        out_specs=pl.BlockSpec((nk, H, SHW), lambda i: (0, 0, 0)),
        out_shape=jax.ShapeDtypeStruct((nk, H, SHW), BF16),
        scratch_shapes=[pltpu.VMEM((nk, D, SHW), F32), pltpu.VMEM((nk, H, SHW), BF16),
                        pltpu.VMEM((nk, H, SHW), BF16),
                        pltpu.SemaphoreType.DMA((nk,)), pltpu.SemaphoreType.DMA((nk,))],
        compiler_params=_cparams(("arbitrary",), 48),
    )(*ins)


_OUT_ORDER = ["norm_mix_w", "w_in", "conv_w", "conv_b", "rg_wa", "rg_ba", "rg_wx", "rg_bx", "rg_lambda", "hg_lb",
              "hg_norm_w", "w_out", "ple_norm_w", "w_ple_gate", "b_ple_gate", "w_ple_proj", "final_norm_w"]
_BIG = ["w_in", "w_out", "w_ple_gate", "w_ple_proj"]


def _small_view(name, a):
    if name in ("rg_wa", "rg_wx"):
        return a.reshape(DG, RGB)
    if name == "conv_w":
        return a.reshape(4, 128)
    if name == "final_norm_w":
        return a.reshape(1, D)
    return a


def _landing(rows, cols):
    return lax.empty((NDEV, rows, cols), BF16)


def kernel(x, p, norm_mix_w, w_in, conv_w, conv_b, rg_wa, rg_ba, rg_wx, rg_bx, rg_lambda, hg_lb, hg_norm_w, w_out, ple_norm_w, w_ple_gate, b_ple_gate, w_ple_proj, final_norm_w, loss_target, m_norm_mix_w, m_w_in, m_conv_w, m_conv_b, m_rg_wa, m_rg_ba, m_rg_wx, m_rg_bx, m_rg_lambda, m_hg_lb, m_hg_norm_w, m_w_out, m_ple_norm_w, m_w_ple_gate, m_b_ple_gate, m_w_ple_proj, m_final_norm_w, v_norm_mix_w, v_w_in, v_conv_w, v_conv_b, v_rg_wa, v_rg_ba, v_rg_wx, v_rg_bx, v_rg_lambda, v_hg_lb, v_hg_norm_w, v_w_out, v_ple_norm_w, v_w_ple_gate, v_b_ple_gate, v_w_ple_proj, v_final_norm_w):
    given = dict(locals())
    x2, p2, tgt = x[0], p[0, 0], loss_target[0]
    wa_c, wx_c = _small_view("rg_wa", rg_wa), _small_view("rg_wx", rg_wx)

    w_in_b, l_out, l_pg, l_pp, cw3 = _gather_w_in(w_in[0], w_out[0], w_ple_gate[0], w_ple_proj[0], conv_w[0])
    g_ssem, g_rsem, g_lands, tok = _gather_rest_start([l_out, l_pg, l_pp])

    pf, pb, u_b = _fwd_inproj(x2, norm_mix_w, w_in_b, tok)
    h, ya = _fwd_rglru(pf, pb, cw3, conv_b, wa_c, rg_ba, wx_c, rg_bx, rg_lambda)
    yb, o, s_prev = _fwd_hgrn2(pf, pb, hg_lb, hg_norm_w)
    w_out_b, w_pg_b, w_pp_b = _gather_rest_wait(g_ssem, g_rsem, g_lands, yb)
    dh1, dyab, dwo_b, dwg_b, dwp_b, sm_tail = _tail_fwd_bwd(
        x2, p2, tgt, ya, yb, w_out_b.reshape(D, D), w_pg_b.reshape(D, D), w_pp_b,
        ple_norm_w, b_ple_gate, final_norm_w.reshape(1, D))

    QH = D // NSHARD // 2
    r1 = _rs_start("rs_start_tail", [dwo_b, dwg_b, dwp_b], (0, 1, 2, 3),
                   [_landing(QH, D), _landing(QH, D), _landing(PLE // 2, PLE)])
    d_b, sm_b = _bwd_hgrn2(pf, pb, o, s_prev, dyab, hg_lb, hg_norm_w, r1[4])
    dw23 = _bwd_inproj_dw("bwd_inproj_dw23", (2, 3), u_b, None, d_b)
    r2 = _rs_start("rs_start_in23", [dw23], (2, 3), [lax.empty((NSHARD, D // 2, SHW), BF16)], chip_sums=True)
    d_a, rg_c, sm_a = _bwd_rglru(pf, pb, h, dyab, cw3, conv_b, wa_c, rg_ba, wx_c, rg_bx, rg_lambda, r2[4])
    dw01 = _bwd_inproj_dw("bwd_inproj_dw01", (0, 1), u_b, d_a, d_b)
    r3 = _rs_start("rs_start_in01", [dw01], (0, 1), r2[3], chip_sums=True)
    grad_x, sm_in = _bwd_inproj_dx(x2, dh1, d_a, d_b, w_in_b, norm_mix_w, r3[4])

    vred, rgred = _small_allreduce(sm_in, sm_tail, sm_a, sm_b, rg_c)
    parts1, lands1 = _rs_wait("rs_wait_tail", r1[0], r1[1], r1[2], (0, 1, 2, 3), r1[3], vred)
    parts2, lands2 = _rs_wait("rs_wait_in23", r2[0], r2[1], r2[2], (2, 3), r3[3], vred, chip_sums=True)
    parts3, lands3 = _rs_wait("rs_wait_in01", r3[0], r3[1], r3[2], (0, 1), lands2, vred, chip_sums=True)
    g_big = _reduce_exchange([parts3[0], parts2[0]] + parts1, lands3 + lands1)

    upd_big = _adam_big(g_big, [given[n][0] for n in _BIG], [given["m_" + n][0] for n in _BIG],
                        [given["v_" + n][0] for n in _BIG])
    small = _adam_small(vred, rgred,
                        {n: _small_view(n, given[n]) for n in _SMALL_ORDER},
                        {n: _small_view(n, given["m_" + n]) for n in _SMALL_ORDER},
                        {n: _small_view(n, given["v_" + n]) for n in _SMALL_ORDER})

    loss = vred[0, ROW_LOSS, 0]
    outs = [loss, grad_x[None]]
    for ki in range(4):
        for n in _OUT_ORDER:
            if n in _BIG:
                i = _BIG.index(n)
                a = g_big[i] if ki == 0 else upd_big[i][ki - 1]
                outs.append(a[None])
            else:
                outs.append(small[n][ki].reshape(given[n].shape))
    return tuple(outs)
```

```python
import jax
import jax.numpy as jnp
from jax import lax
from jax.experimental import pallas as pl
from jax.experimental.pallas import tpu as pltpu

F32 = jnp.float32
BF16 = jnp.bfloat16
I32 = jnp.int32
MESH = pl.DeviceIdType.MESH
HIGHEST = lax.Precision.HIGHEST

T = 4096
D = 1024
DG = 512
DIN = 3072
PLE = 256
NH = 4
HD = 128
CH = 64
NCHUNK = T // CH
RGB = 64
EPS = 1e-6
RG_C = 8.0
NSHARD = 4
SHW = DIN // NSHARD
NDEV = 8

ADAM_LR = 0.001
ADAM_B1 = 0.9
ADAM_B2 = 0.999
ADAM_EPS = 1e-08
ADAM_WD = 0.01
ADAM_STEP = 10

VMEM_SPEC = pl.BlockSpec(memory_space=pltpu.VMEM)
HBM_SPEC = pl.BlockSpec(memory_space=pltpu.HBM)
SEM_SPEC = pl.BlockSpec(memory_space=pltpu.SEMAPHORE)
ANY_SPEC = pl.BlockSpec(memory_space=pl.ANY)
EFFECT = pltpu.SideEffectType.DATAFLOW_SIDE_EFFECTING
MIB = 1024 * 1024

VROWS = 16
ROW_NORM_MIX, ROW_FINAL_NORM, ROW_B_PG, ROW_PLE_NORM = 0, 1, 2, 3
ROW_CB_BA, ROW_BX_LAM, ROW_CW01, ROW_CW23, ROW_HG_LB, ROW_HG_NW, ROW_LOSS = 4, 5, 6, 7, 8, 9, 10


def _mm(a, b):
    return jnp.dot(a.astype(BF16), b.astype(BF16), preferred_element_type=F32)


def _mm_nt(a, b):
    return lax.dot_general(a.astype(BF16), b.astype(BF16), (((1,), (1,)), ((), ())),
                           preferred_element_type=F32)


def _mm_tn(a, b):
    return lax.dot_general(a.astype(BF16), b.astype(BF16), (((0,), (0,)), ((), ())),
                           preferred_element_type=F32)


def _mm_exact(a, b):
    return jnp.dot(a, b, precision=HIGHEST, preferred_element_type=F32)


def _sig_pair(x):
    e = jnp.exp(-jnp.abs(x))
    big = 1.0 / (1.0 + e)
    small = e * big
    pos = x >= 0
    return jnp.where(pos, big, small), jnp.where(pos, small, big)


def _sigmoid(x):
    return 1.0 / (1.0 + jnp.exp(-x))


def _rowsum(v):
    return jnp.sum(v, axis=0, keepdims=True)


def _shift_down(cur, prev8, d, rows):
    rolled = pltpu.roll(cur, d, 0)
    head = jnp.where(rows[0:8] < d, pltpu.roll(prev8, d, 0), rolled[0:8])
    return jnp.concatenate([head, rolled[8:]], axis=0)


def _shift_up(cur, next8, d, rows):
    n = cur.shape[0]
    rolled = pltpu.roll(cur, n - d, 0)
    tail = jnp.where(rows[0:8] >= 8 - d, pltpu.roll(next8, 8 - d, 0), rolled[n - 8:n])
    return jnp.concatenate([rolled[0:n - 8], tail], axis=0)


def _roll_in_groups(v, d):
    n, w = v.shape
    return pltpu.roll(v.reshape(n // 8, 8, w), d, 1).reshape(n, w)


def _cparams(sem, vmem_mib):
    return pltpu.CompilerParams(dimension_semantics=sem, vmem_limit_bytes=vmem_mib * MIB)


def _mesh_pos():
    x, y, c = lax.axis_index("x"), lax.axis_index("y"), lax.axis_index("c")
    chips = [(1 - x, y), (x, 1 - y), (1 - x, 1 - y)]
    return x, y, c, chips


def _remote(src, dst, ssem, rsem, dev):
    return pltpu.make_async_remote_copy(src_ref=src, dst_ref=dst, send_sem=ssem, recv_sem=rsem,
                                        device_id=dev, device_id_type=MESH)


def _gather_w_in(w_in, w_out, w_pg, w_pp, conv_w):
    shapes = [w_in.shape, w_out.shape, w_pg.shape, w_pp.shape]

    def body(win, wout, wpg, wpp, cw, o_in, o_out, o_pg, o_pp, o_cw, ssem, rsem):
        x, y, c, chips = _mesh_pos()
        kme = 2 * x + y
        sibling = (x, y, 1 - c)

        def cast(src, dst):
            for r0 in range(0, src.shape[0], 128):
                dst[kme, r0:r0 + 128, :] = src[r0:r0 + 128, :].astype(BF16)

        cast(win, o_in)
        o_cw[kme] = cw[...]

        hrows = D // 2
        mine = pl.ds(pl.multiple_of(c * hrows, 128), hrows)
        other = pl.ds(pl.multiple_of((1 - c) * hrows, 128), hrows)

        def half(k, rows):
            return o_in.at[k, rows]

        na = (x ^ c, y ^ (1 - c), c)
        nb = (x ^ (1 - c), y ^ c, c)
        ka = 2 * na[0] + na[1]
        kb = 2 * nb[0] + nb[1]
        kd = 2 * (1 - x) + (1 - y)
        sends = [_remote(half(kme, mine), half(kme, mine), ssem.at[0], rsem.at[0], na),
                 _remote(half(kme, mine), half(kme, mine), ssem.at[1], rsem.at[1], nb)]
        for j, (px, py) in enumerate(chips):
            sends.append(_remote(o_cw.at[kme], o_cw.at[kme], ssem.at[3 + j], rsem.at[3 + j], (px, py, c)))
        for cp in sends:
            cp.start()
        cast(wout, o_out)
        cast(wpg, o_pg)
        cast(wpp, o_pp)

        def passed_on(k, sem, dev):
            cp = _remote(half(k, mine), half(k, mine), ssem.at[sem], rsem.at[sem], dev)
            cp.start()
            sends.append(cp)

        _remote(half(ka, mine), half(ka, mine), ssem.at[0], rsem.at[0], na).wait_recv()
        passed_on(ka, 2, nb)
        passed_on(ka, 6, sibling)
        _remote(half(kb, mine), half(kb, mine), ssem.at[1], rsem.at[1], nb).wait_recv()
        passed_on(kb, 7, sibling)
        _remote(half(kd, mine), half(kd, mine), ssem.at[2], rsem.at[2], nb).wait_recv()
        passed_on(kd, 8, sibling)
        _remote(half(kb, other), half(kb, other), ssem.at[6], rsem.at[6], sibling).wait_recv()
        _remote(half(ka, other), half(ka, other), ssem.at[7], rsem.at[7], sibling).wait_recv()
        _remote(half(kd, other), half(kd, other), ssem.at[8], rsem.at[8], sibling).wait_recv()
        for j, (px, py) in enumerate(chips):
            kj = 2 * px + py
            _remote(o_cw.at[kj], o_cw.at[kj], ssem.at[3 + j], rsem.at[3 + j], (px, py, c)).wait_recv()
        for cp in sends:
            cp.wait_send()

    out_shape = [jax.ShapeDtypeStruct((NSHARD,) + s, BF16) for s in shapes]
    out_shape.append(jax.ShapeDtypeStruct((NSHARD,) + conv_w.shape, F32))
    return pl.pallas_call(
        body, name="gather_w_in", out_shape=out_shape,
        in_specs=[VMEM_SPEC] * 5, out_specs=[VMEM_SPEC] * 5,
        scratch_shapes=[pltpu.SemaphoreType.DMA((9,)), pltpu.SemaphoreType.DMA((9,))],
        compiler_params=pltpu.CompilerParams(vmem_limit_bytes=40 * MIB),
    )(w_in, w_out, w_pg, w_pp, conv_w)


def _gather_rest_start(lands):
    n = len(lands)

    def body(*refs):
        land_in = refs[0:n]
        ssem, rsem = refs[n], refs[n + 1]
        token = refs[2 * n + 2]
        x, y, c, chips = _mesh_pos()
        kme = 2 * x + y
        for p, land in enumerate(land_in):
            hrows = land.shape[1] // 2
            mine = pl.ds(pl.multiple_of(c * hrows, 128), hrows)
            for px, py in chips:
                for pc in range(2):
                    _remote(land.at[kme, mine], land.at[kme, mine], ssem.at[p], rsem.at[p], (px, py, pc)).start()
        token[...] = jnp.zeros_like(token)

    out_shape = ([pltpu.SemaphoreType.DMA((n,)), pltpu.SemaphoreType.DMA((n,))]
                 + [pltpu.HBM(a.shape, a.dtype) for a in lands] + [jax.ShapeDtypeStruct((8, 128), F32)])
    outs = pl.pallas_call(
        body, name="gather_rest_start", out_shape=out_shape,
        in_specs=[HBM_SPEC] * n, out_specs=[SEM_SPEC, SEM_SPEC] + [HBM_SPEC] * n + [VMEM_SPEC],
        input_output_aliases={i: 2 + i for i in range(n)},
        compiler_params=pltpu.CompilerParams(has_side_effects=EFFECT),
    )(*[pltpu.with_memory_space_constraint(a, pltpu.HBM) for a in lands])
    return outs[0], outs[1], list(outs[2:2 + n]), outs[2 + n]


def _gather_rest_wait(ssem, rsem, lands, after):
    n = len(lands)

    def body(*refs):
        land_in = refs[0:n]
        ssem_ref, rsem_ref = refs[n], refs[n + 1]
        x, y, c = lax.axis_index("x"), lax.axis_index("y"), lax.axis_index("c")
        for p, land in enumerate(land_in):
            three = land.at[pl.ds(0, 3)]
            cp = _remote(three, three, ssem_ref.at[p], rsem_ref.at[p], (x, y, c))
            cp.wait_send()
            cp.wait_recv()

    outs = pl.pallas_call(
        body, name="gather_rest_wait", out_shape=[pltpu.HBM(a.shape, a.dtype) for a in lands],
        in_specs=[HBM_SPEC] * n + [SEM_SPEC, SEM_SPEC, ANY_SPEC], out_specs=[HBM_SPEC] * n,
        input_output_aliases={i: i for i in range(n)},
        compiler_params=pltpu.CompilerParams(has_side_effects=EFFECT),
    )(*lands, ssem, rsem, after)
    return list(outs)


def _rs_start(name, parts, ks, lands, chip_sums=False):
    n = len(parts)

    def body(*refs):
        part_in, land_in = refs[0:n], refs[n:2 * n]
        ssem, rsem = refs[2 * n], refs[2 * n + 1]
        token = refs[4 * n + 2]
        x, y, c = lax.axis_index("x"), lax.axis_index("y"), lax.axis_index("c")
        kme = 2 * x + y
        me = 4 * x + 2 * y + c
        for p in range(n):
            hrows = land_in[p].shape[1]
            for i, k in enumerate(ks):
                if chip_sums:
                    @pl.when(kme != k)
                    def _():
                        _remote(part_in[p].at[i], land_in[p].at[kme], ssem.at[p], rsem.at[p], (k // 2, k % 2, c)).start()
                    continue
                for pc in range(2):
                    @pl.when(jnp.logical_or(kme != k, c != pc))
                    def _():
                        _remote(part_in[p].at[i, pl.ds(pc * hrows, hrows)], land_in[p].at[me],
                                ssem.at[p], rsem.at[p], (k // 2, k % 2, pc)).start()
        token[...] = jnp.zeros_like(token)

    arrays = list(parts) + list(lands)
    out_shape = ([pltpu.SemaphoreType.DMA((n,)), pltpu.SemaphoreType.DMA((n,))]
                 + [pltpu.HBM(a.shape, a.dtype) for a in arrays] + [jax.ShapeDtypeStruct((8, 128), F32)])
    outs = pl.pallas_call(
        body, name=name, out_shape=out_shape,
        in_specs=[HBM_SPEC] * (2 * n), out_specs=[SEM_SPEC, SEM_SPEC] + [HBM_SPEC] * (2 * n) + [VMEM_SPEC],
        input_output_aliases={i: 2 + i for i in range(2 * n)},
        compiler_params=pltpu.CompilerParams(has_side_effects=EFFECT),
    )(*[pltpu.with_memory_space_constraint(a, pltpu.HBM) for a in arrays])
    return outs[0], outs[1], list(outs[2:2 + n]), list(outs[2 + n:2 + 2 * n]), outs[2 + 2 * n]


def _rs_wait(name, ssem, rsem, parts, ks, lands, after, chip_sums=False):
    n = len(parts)

    def body(*refs):
        part_in, land_in = refs[0:n], refs[n:2 * n]
        ssem_ref, rsem_ref = refs[2 * n], refs[2 * n + 1]
        x, y, c = lax.axis_index("x"), lax.axis_index("y"), lax.axis_index("c")
        kme = 2 * x + y
        for p in range(n):
            piece = land_in[p].at[0]
            for k in ks:
                for pc in range(1 if chip_sums else 2):
                    mine = (kme == k) if chip_sums else jnp.logical_and(kme == k, c == pc)

                    @pl.when(jnp.logical_not(mine))
                    def _():
                        _remote(piece, piece, ssem_ref.at[p], rsem_ref.at[p], (x, y, c)).wait_send()
            owner = kme == ks[0]
            for k in ks[1:]:
                owner = jnp.logical_or(owner, kme == k)

            @pl.when(owner)
            def _():
                others = land_in[p].at[pl.ds(0, land_in[p].shape[0] - 1)]
                _remote(others, others, ssem_ref.at[p], rsem_ref.at[p], (x, y, c)).wait_recv()

    arrays = list(parts) + list(lands)
    outs = pl.pallas_call(
        body, name=name, out_shape=[pltpu.HBM(a.shape, a.dtype) for a in arrays],
        in_specs=[HBM_SPEC] * (2 * n) + [SEM_SPEC, SEM_SPEC, ANY_SPEC], out_specs=[HBM_SPEC] * (2 * n),
        input_output_aliases={i: i for i in range(2 * n)},
        compiler_params=pltpu.CompilerParams(has_side_effects=EFFECT),
    )(*arrays, ssem, rsem, after)
    return list(outs[0:n]), list(outs[n:2 * n])


def _reduce_exchange(parts, lands):
    shapes = [(2 * l.shape[1], l.shape[2]) for l in lands]
    step = 128

    def body(in01, in23, pout, ppg, ppp, l_in, l_out, l_pg, l_pp, g_in, g_out, g_pg, g_pp, ssem, rsem):
        x, y, c = lax.axis_index("x"), lax.axis_index("y"), lax.axis_index("c")
        kme = 2 * x + y
        me = 4 * x + 2 * y + c
        sibling = (x, y, 1 - c)
        sends = []
        for p, (land, gout) in enumerate([(l_in, g_in), (l_out, g_out), (l_pg, g_pg), (l_pp, g_pp)]):
            hrows = land.shape[1]
            mine0 = pl.multiple_of(c * hrows, step)
            for r0 in range(0, hrows, step):
                rs = pl.ds(mine0 + r0, step)
                if p == 0:
                    own = jnp.where(kme >= 2, in23[kme & 1, r0:r0 + step, :], in01[kme & 1, r0:r0 + step, :])
                    slot = kme
                else:
                    own = (pout, ppg, ppp)[p - 1][kme, rs, :]
                    slot = me
                s = jnp.zeros((step, land.shape[2]), F32)
                for j in range(land.shape[0]):
                    s = s + jnp.where(slot == j, own, land[j, r0:r0 + step, :]).astype(F32)
                gout[rs, :] = s
            mine = pl.ds(mine0, hrows)
            cp = _remote(gout.at[mine], gout.at[mine], ssem.at[p], rsem.at[p], sibling)
            cp.start()
            sends.append(cp)
        for p, (land, gout) in enumerate([(l_in, g_in), (l_out, g_out), (l_pg, g_pg), (l_pp, g_pp)]):
            hrows = land.shape[1]
            other = pl.ds(pl.multiple_of((1 - c) * hrows, step), hrows)
            _remote(gout.at[other], gout.at[other], ssem.at[p], rsem.at[p], sibling).wait_recv()
        for cp in sends:
            cp.wait_send()

    return pl.pallas_call(
        body, name="reduce_exchange", out_shape=[jax.ShapeDtypeStruct(s, F32) for s in shapes],
        in_specs=[VMEM_SPEC] * 9, out_specs=[VMEM_SPEC] * 4,
        scratch_shapes=[pltpu.SemaphoreType.DMA((4,)), pltpu.SemaphoreType.DMA((4,))],
        compiler_params=pltpu.CompilerParams(vmem_limit_bytes=48 * MIB),
    )(*parts, *lands)


def _small_allreduce(sm_in, sm_tail, sm_a, sm_b, rg_c):
    PR = DG // NDEV

    def body(in_ref, tail_ref, a_ref, b_ref, rg_ref, v_out, rg_out, vbuf, vrecv, rgrecv, ssem, rsem):
        x, y, c = lax.axis_index("x"), lax.axis_index("y"), lax.axis_index("c")
        me = 4 * x + 2 * y + c

        def pair(ref, r0, r1):
            return jnp.concatenate([ref[r0:r0 + 1, :], ref[r1:r1 + 1, :]], axis=1)

        rows = {
            ROW_NORM_MIX: in_ref[0:1, :], ROW_FINAL_NORM: tail_ref[1:2, :], ROW_B_PG: tail_ref[2:3, :],
            ROW_PLE_NORM: tail_ref[3:4, :], ROW_CB_BA: pair(a_ref, 0, 1), ROW_BX_LAM: pair(a_ref, 2, 3),
            ROW_CW01: pair(a_ref, 4, 5), ROW_CW23: pair(a_ref, 6, 7), ROW_HG_LB: pair(b_ref, 2, 3),
            ROW_HG_NW: jnp.concatenate([b_ref[1:2, :], jnp.zeros((1, DG), F32)], axis=1),
            ROW_LOSS: tail_ref[4:5, :],
        }
        vbuf[...] = jnp.zeros_like(vbuf)
        for r, row in rows.items():
            for j in range(NDEV):
                vbuf[j, r:r + 1, :] = row[:, j * 128:(j + 1) * 128]

        def peer(mask):
            px = x ^ ((mask >> 2) & 1)
            py = y ^ ((mask >> 1) & 1)
            pc = c ^ (mask & 1)
            return (px, py, pc), 4 * px + 2 * py + pc

        def rg_rows(r):
            return pl.ds(pl.multiple_of(r * PR, PR), PR)

        first = []
        for mask in range(1, NDEV):
            dev, r = peer(mask)
            i = mask - 1
            cp = _remote(vbuf.at[r], vrecv.at[i], ssem.at[i], rsem.at[i], dev)
            cp.start()
            first.append(cp)
            cp = _remote(rg_ref.at[rg_rows(r)], rgrecv.at[i], ssem.at[7 + i], rsem.at[7 + i], dev)
            cp.start()
            first.append(cp)
        sv = vbuf[me]
        sr = rg_ref[rg_rows(me), :]
        for i in range(NDEV - 1):
            first[2 * i].wait_recv()
            first[2 * i + 1].wait_recv()
            sv = sv + vrecv[i]
            sr = sr + rgrecv[i]
        v_out[me] = sv
        rg_out[rg_rows(me), :] = sr
        second = []
        for mask in range(1, NDEV):
            dev, r = peer(mask)
            i = mask - 1
            cp = _remote(v_out.at[me], v_out.at[me], ssem.at[14 + i], rsem.at[14 + i], dev)
            cp.start()
            second.append(cp)
            cp = _remote(rg_out.at[rg_rows(me)], rg_out.at[rg_rows(me)], ssem.at[21 + i], rsem.at[21 + i], dev)
            cp.start()
            second.append(cp)
        for mask in range(1, NDEV):
            dev, r = peer(mask)
            i = mask - 1
            _remote(v_out.at[r], v_out.at[r], ssem.at[14 + i], rsem.at[14 + i], dev).wait_recv()
            _remote(rg_out.at[rg_rows(r)], rg_out.at[rg_rows(r)], ssem.at[21 + i], rsem.at[21 + i], dev).wait_recv()
        for cp in first + second:
            cp.wait_send()

    return pl.pallas_call(
        body, name="small_allreduce",
        out_shape=[jax.ShapeDtypeStruct((NDEV, VROWS, 128), F32), jax.ShapeDtypeStruct((DG, 128), F32)],
        in_specs=[VMEM_SPEC] * 5, out_specs=[VMEM_SPEC] * 2,
        scratch_shapes=[pltpu.VMEM((NDEV, VROWS, 128), F32), pltpu.VMEM((NDEV - 1, VROWS, 128), F32),
                        pltpu.VMEM((NDEV - 1, PR, 128), F32),
                        pltpu.SemaphoreType.DMA((28,)), pltpu.SemaphoreType.DMA((28,))],
    )(sm_in, sm_tail, sm_a, sm_b, rg_c)


def _adam_rows(w, g, m, v):
    m2 = ADAM_B1 * m + (1.0 - ADAM_B1) * g
    v2 = ADAM_B2 * v + (1.0 - ADAM_B2) * (g * g)
    m_hat = m2 / (1.0 - ADAM_B1 ** ADAM_STEP)
    v_hat = v2 / (1.0 - ADAM_B2 ** ADAM_STEP)
    delta = -ADAM_LR * (m_hat / (jnp.sqrt(v_hat) + ADAM_EPS) + ADAM_WD * w)
    return delta, m2, v2


def _adam_big(gs, ws, ms, vs):
    n = len(gs)
    steps = 8

    def body(*refs):
        ins, outs = refs[:4 * n], refs[4 * n:]
        for i in range(n):
            g, w, m, v = (r[...] for r in ins[4 * i:4 * i + 4])
            d, m2, v2 = _adam_rows(w, g, m, v)
            outs[3 * i][...] = d
            outs[3 * i + 1][...] = m2
            outs[3 * i + 2][...] = v2

    in_specs, out_specs, out_shape, args = [], [], [], []
    for g, w, m, v in zip(gs, ws, ms, vs):
        r, c = w.shape
        spec = lambda: pl.BlockSpec((r // steps, c), lambda i: (i, 0))
        in_specs += [spec() for _ in range(4)]
        out_specs += [spec() for _ in range(3)]
        out_shape += [jax.ShapeDtypeStruct((r, c), F32)] * 3
        args += [g, w, m, v]
    outs = pl.pallas_call(
        body, name="adam_big", grid=(steps,), in_specs=in_specs, out_specs=out_specs, out_shape=out_shape,
        compiler_params=_cparams(("parallel",), 32),
    )(*args)
    return [tuple(outs[3 * i:3 * i + 3]) for i in range(n)]


_VEC_PARAMS = [
    ("norm_mix_w", ROW_NORM_MIX, 0, D), ("final_norm_w", ROW_FINAL_NORM, 0, D),
    ("b_ple_gate", ROW_B_PG, 0, D), ("ple_norm_w", ROW_PLE_NORM, 0, D),
    ("conv_b", ROW_CB_BA, 0, DG), ("rg_ba", ROW_CB_BA, DG, DG),
    ("rg_bx", ROW_BX_LAM, 0, DG), ("rg_lambda", ROW_BX_LAM, DG, DG),
    ("hg_norm_w", ROW_HG_NW, 0, HD),
]
_SMALL_ORDER = [n for n, _, _, _ in _VEC_PARAMS] + ["hg_lb", "conv_w", "rg_wa", "rg_wx"]


def _adam_small(vred, rgred, ws, ms, vs):
    names = _SMALL_ORDER
    n = len(names)

    def body(vred_ref, rg_ref, *refs):
        w_refs = dict(zip(names, refs[0:n]))
        m_refs = dict(zip(names, refs[n:2 * n]))
        v_refs = dict(zip(names, refs[2 * n:3 * n]))
        outs = refs[3 * n:]
        o_refs = {nm: outs[4 * i:4 * i + 4] for i, nm in enumerate(names)}
        kme = 2 * lax.axis_index("x") + lax.axis_index("y")

        def update(nm, g, idx):
            d, m2, v2 = _adam_rows(w_refs[nm][idx], g, m_refs[nm][idx], v_refs[nm][idx])
            og, od, om, ov = o_refs[nm]
            og[idx] = g
            od[idx] = d
            om[idx] = m2
            ov[idx] = v2

        def packed(row, lane0, width):
            return jnp.concatenate([vred_ref[j, row:row + 1, :] for j in range(lane0 // 128, (lane0 + width) // 128)],
                                   axis=1)

        everything = (slice(None), slice(None))
        for nm, row, lane0, width in _VEC_PARAMS:
            update(nm, packed(row, lane0, width), everything)
        for r in range(2):
            update("hg_lb", packed(ROW_HG_LB, r * DG, DG), (slice(r, r + 1), slice(None)))
        for j in range(4):
            g = vred_ref[(j % 2) * 4 + kme, ROW_CW01 + j // 2:ROW_CW01 + j // 2 + 1, :]
            update("conv_w", g, (slice(j, j + 1), slice(None)))
        for r0 in range(0, DG, 128):
            rs = (slice(r0, r0 + 128), slice(None))
            both = rg_ref[r0:r0 + 128, :]
            update("rg_wa", both[:, 0:RGB], rs)
            update("rg_wx", pltpu.roll(both, RGB, 1)[:, 0:RGB], rs)

    args = [vred, rgred] + [d[nm] for d in (ws, ms, vs) for nm in names]
    out_shape = []
    for nm in names:
        out_shape += [jax.ShapeDtypeStruct(ws[nm].shape, F32)] * 4
    whole = lambda s: pl.BlockSpec(s.shape, lambda i, nd=len(s.shape): (0,) * nd)
    outs = pl.pallas_call(
        body, name="adam_small", out_shape=out_shape, grid=(1,),
        in_specs=[whole(a) for a in args], out_specs=[whole(s) for s in out_shape],
    )(*args)
    return {nm: tuple(outs[4 * i:4 * i + 4]) for i, nm in enumerate(names)}


def _fwd_inproj(x, nw, w_in_b, dep):
    tm = 512

    def body(x_ref, nw_ref, w_ref, dep_ref, pf_ref, pb_ref, u_ref):
        xv = x_ref[...]
        s = lax.rsqrt(jnp.mean(xv * xv, axis=-1, keepdims=True) + EPS)
        u = (xv * s * nw_ref[...]).astype(BF16)
        u_ref[...] = u
        r = [jnp.dot(u, w_ref[k], preferred_element_type=F32) for k in range(NSHARD)]
        h = DG // 2
        pf_ref[:, 0:DG] = r[0][:, 0:DG]
        pb_ref[:, 0:h] = r[0][:, DG:SHW].astype(BF16)
        pb_ref[:, h:DG] = r[1][:, 0:h].astype(BF16)
        pb_ref[:, DG:2 * DG] = r[1][:, h:SHW].astype(BF16)
        pf_ref[:, DG:2 * DG] = r[2][:, 0:DG]
        pb_ref[:, 2 * DG:2 * DG + h] = r[2][:, DG:SHW].astype(BF16)
        pb_ref[:, 2 * DG + h:3 * DG] = r[3][:, 0:h].astype(BF16)
        pb_ref[:, 3 * DG:4 * DG] = r[3][:, h:SHW].astype(BF16)

    return pl.pallas_call(
        body, name="fwd_inproj", grid=(T // tm,),
        in_specs=[pl.BlockSpec((tm, D), lambda i: (i, 0)),
                  pl.BlockSpec((1, D), lambda i: (0, 0)),
                  pl.BlockSpec((NSHARD, D, SHW), lambda i: (0, 0, 0), pipeline_mode=pl.Buffered(1)), ANY_SPEC],
        out_specs=[pl.BlockSpec((tm, 2 * DG), lambda i: (i, 0)),
                   pl.BlockSpec((tm, 4 * DG), lambda i: (i, 0)),
                   pl.BlockSpec((tm, D), lambda i: (i, 0))],
        out_shape=[jax.ShapeDtypeStruct((T, 2 * DG), F32), jax.ShapeDtypeStruct((T, 4 * DG), BF16),
                   jax.ShapeDtypeStruct((T, D), BF16)],
        compiler_params=_cparams(("parallel",), 48),
    )(x, nw, w_in_b, dep)


def _conv_rows(cw_ref):
    return [jnp.concatenate([cw_ref[k, j:j + 1, :] for k in range(NSHARD)], axis=1) for j in range(4)]


def _rg_conv(xa, prev8, cw, cb, rows):
    taps = [_shift_down(xa, prev8, 3, rows), _shift_down(xa, prev8, 2, rows),
            _shift_down(xa, prev8, 1, rows), xa]
    xc = cb
    for j in range(4):
        xc = xc + taps[j] * cw[j]
    return xc, taps


def _block_mask():
    r = lax.broadcasted_iota(I32, (DG, DG), 0)
    c = lax.broadcasted_iota(I32, (DG, DG), 1)
    return (r >> 6) == (c >> 6)


def _dense_from_blocks(wc):
    j = lax.broadcasted_iota(I32, (RGB, DG), 0)
    c = lax.broadcasted_iota(I32, (RGB, DG), 1)
    spread = _mm_exact(wc, ((c & (RGB - 1)) == j).astype(F32))
    return jnp.where(_block_mask(), spread, 0.0)


def _blocks_from_dense(da, dx):
    c = lax.broadcasted_iota(I32, (DG, 128), 0)
    j = lax.broadcasted_iota(I32, (DG, 128), 1)
    hit = (c & (RGB - 1)) == (j & (RGB - 1))
    mask = _block_mask()
    return (_mm_exact(jnp.where(mask, da, 0.0), (hit & (j < RGB)).astype(F32))
            + _mm_exact(jnp.where(mask, dx, 0.0), (hit & (j >= RGB)).astype(F32)))


def _rg_gates(xc, wa, ba, wx, bx, sp, first_row):
    r = _sigmoid(_mm(xc, wa) + ba)
    i = _sigmoid(_mm(xc, wx) + bx)
    log_a = (-RG_C) * r * sp
    a = jnp.exp(log_a)
    a2 = a * a
    one_m_a2 = -jnp.tanh(log_a) * (a2 + 1.0)
    mult = jnp.where(first_row, 1.0, jnp.sqrt(one_m_a2))
    return r, i, a, a2, mult


def _softplus(z):
    return jnp.maximum(z, 0.0) + jnp.log1p(jnp.exp(-jnp.abs(z)))


def _fwd_rglru(pf, pb, cw3, conv_b, wa_c, ba, wx_c, bx, lam):
    tm = 512
    ng = tm // 8

    def body(xa_ref, ga_ref, cw_ref, cb_ref, wa_ref, ba_ref, wx_ref, bx_ref, lam_ref,
             h_ref, ya_ref, a_s, u_s, tail_s, hc_s, wa_s, wx_s):
        i = pl.program_id(0)

        @pl.when(i == 0)
        def _():
            tail_s[...] = jnp.zeros_like(tail_s)
            hc_s[...] = jnp.zeros_like(hc_s)
            wa_s[...] = _dense_from_blocks(wa_ref[...]).astype(BF16)
            wx_s[...] = _dense_from_blocks(wx_ref[...]).astype(BF16)

        rows = lax.broadcasted_iota(I32, (tm, DG), 0)
        xa = xa_ref[...]
        xc, _ = _rg_conv(xa, tail_s[...], _conv_rows(cw_ref), cb_ref[...], rows)
        tail_s[...] = xa[tm - 8:tm, :]
        sp = _softplus(-lam_ref[...])
        first_row = (rows + i * tm) == 0
        r, ig, a, a2, mult = _rg_gates(xc, wa_s[...], ba_ref[...], wx_s[...], bx_ref[...], sp, first_row)
        av, uv = a, mult * (ig * xc)
        rows8 = rows & 7
        for d in (1, 2, 4):
            keep = rows8 >= d
            uv = uv + av * jnp.where(keep, _roll_in_groups(uv, d), 0.0)
            av = av * jnp.where(keep, _roll_in_groups(av, d), 1.0)
        a_s[...] = av
        u_s[...] = uv
        carry = hc_s[0:1, :]
        for g in range(ng):
            sl = slice(g * 8, (g + 1) * 8)
            ab, ub = a_s[sl, :], u_s[sl, :]
            h_ref[sl, :] = ub + ab * carry
            carry = ub[7:8, :] + ab[7:8, :] * carry
        hc_s[0:1, :] = carry
        ga = ga_ref[...].astype(F32)
        sg = _sigmoid(ga)
        ya_ref[...] = (h_ref[...] * (ga * sg)).astype(BF16)

    vec = lambda: pl.BlockSpec((1, DG), lambda i: (0, 0))
    blocks = lambda: pl.BlockSpec((DG, RGB), lambda i: (0, 0))
    return pl.pallas_call(
        body, name="fwd_rglru", grid=(T // tm,),
        in_specs=[pl.BlockSpec((tm, DG), lambda i: (i, 0)),
                  pl.BlockSpec((tm, DG), lambda i: (i, 0)),
                  pl.BlockSpec((NSHARD, 4, 128), lambda i: (0, 0, 0)), vec(),
                  blocks(), vec(), blocks(), vec(), vec()],
        out_specs=[pl.BlockSpec((tm, DG), lambda i: (i, 0)),
                   pl.BlockSpec((tm, DG), lambda i: (i, 0))],
        out_shape=[jax.ShapeDtypeStruct((T, DG), F32), jax.ShapeDtypeStruct((T, DG), BF16)],
        scratch_shapes=[pltpu.VMEM((tm, DG), F32), pltpu.VMEM((tm, DG), F32),
                        pltpu.VMEM((8, DG), F32), pltpu.VMEM((8, DG), F32),
                        pltpu.VMEM((DG, DG), BF16), pltpu.VMEM((DG, DG), BF16)],
        compiler_params=_cparams(("arbitrary",), 48),
    )(pf, pb, cw3, conv_b, wa_c, ba, wx_c, bx, lam)


def _hg_lower_bound(lb_ref):
    return _sig_pair(lb_ref[0:1, :] - lb_ref[1:2, :])


def _hg_gates(fz, lb, one_m_lb):
    sg, sn = _sig_pair(fz)
    f = lb + one_m_lb * sg
    return sg, sn, f, jnp.log(f), one_m_lb * sn


def _tri(lower):
    r = lax.broadcasted_iota(I32, (CH, CH), 0)
    c = lax.broadcasted_iota(I32, (CH, CH), 1)
    return (r >= c) if lower else (r <= c)


def _chunk_cumsum(v, rows64):
    for d in (1, 2, 4, 8, 16, 32):
        v = v + jnp.where(rows64 >= d, pltpu.roll(v, d, 0), 0.0)
    return v


def _chunk_rev_cumsum(v, rows64):
    n = v.shape[0]
    for d in (1, 2, 4, 8, 16, 32):
        v = v + jnp.where(rows64 < CH - d, pltpu.roll(v, n - d, 0), 0.0)
    return v


def _hg_recompute(q_ref, f_ref, lb, one_m_lb, rows64, eb_s, enb_s, ekd_s, qe_s, ke_s, kd_s, dec_s):
    nc = q_ref.shape[0] // CH
    sg, sn, f, logf, k = _hg_gates(f_ref[...], lb, one_m_lb)
    q = q_ref[...].astype(F32)
    sq = _sigmoid(q)
    qs = q * sq * (HD ** -0.5)
    b = _chunk_cumsum(logf, rows64)
    for c in range(nc):
        rs = slice(c * CH, (c + 1) * CH)
        b_c = b[rs]
        bl = b_c[CH - 1:CH, :]
        eb, enb, ekd = jnp.exp(b_c), jnp.exp(-b_c), jnp.exp(bl - b_c)
        if eb_s is not None:
            eb_s[rs, :] = eb
            enb_s[rs, :] = enb
            ekd_s[rs, :] = ekd
        qe_s[rs, :] = (qs[rs] * eb).astype(BF16)
        ke_s[rs, :] = (k[rs] * enb).astype(BF16)
        kd_s[rs, :] = (k[rs] * ekd).astype(BF16)
        dec_s[c:c + 1, :] = jnp.exp(bl)
    return sg, sn, f, k, q, sq, qs


def _fwd_hgrn2(pf, pb, hg_lb, hg_nw):
    tm = 512
    nc = tm // CH

    def body(q_ref, f_ref, v_ref, g_ref, lb_ref, nw_ref, yb_ref, o_ref, sp_ref,
             st_s, qe_s, ke_s, kd_s, vb_s, dec_s, p_s, ds_s):
        i = pl.program_id(0)

        @pl.when(i == 0)
        def _():
            st_s[...] = jnp.zeros_like(st_s)

        lb, one_m_lb = _hg_lower_bound(lb_ref)
        rows64 = lax.broadcasted_iota(I32, (tm, DG), 0) & (CH - 1)
        _hg_recompute(q_ref, f_ref, lb, one_m_lb, rows64, None, None, None, qe_s, ke_s, kd_s, dec_s)
        vb_s[...] = v_ref[...]
        mask = _tri(True)
        items = [(c, hd, slice(c * CH, (c + 1) * CH), slice(hd * HD, (hd + 1) * HD))
                 for c in range(nc) for hd in range(NH)]
        for c, hd, rs, cols in items:
            p_s[c * NH + hd] = jnp.where(mask, _mm_nt(qe_s[rs, cols], ke_s[rs, cols]), 0.0).astype(BF16)
            ds_s[c * NH + hd] = _mm_tn(vb_s[rs, cols], kd_s[rs, cols])
        for c, hd, rs, cols in items:
            st = st_s[hd]
            sp_ref[hd, c] = st
            st_s[hd] = st * dec_s[c:c + 1, cols] + ds_s[c * NH + hd]
        for c, hd, rs, cols in items:
            o_ref[rs, cols] = _mm(p_s[c * NH + hd], vb_s[rs, cols]) + _mm_nt(qe_s[rs, cols], sp_ref[hd, c])
        nw = nw_ref[...]
        for hd in range(NH):
            cols = slice(hd * HD, (hd + 1) * HD)
            o = o_ref[:, cols]
            so = lax.rsqrt(jnp.mean(o * o, axis=-1, keepdims=True) + EPS)
            g = g_ref[:, cols].astype(F32)
            sg = _sigmoid(g)
            yb_ref[:, cols] = (o * so * nw * (g * sg)).astype(BF16)

    col = lambda j: pl.BlockSpec((tm, DG), lambda i: (i, j))
    return pl.pallas_call(
        body, name="fwd_hgrn2", grid=(T // tm,),
        in_specs=[col(1), col(1), col(2), col(3),
                  pl.BlockSpec((2, DG), lambda i: (0, 0)),
                  pl.BlockSpec((1, HD), lambda i: (0, 0))],
        out_specs=[pl.BlockSpec((tm, DG), lambda i: (i, 0)),
                   pl.BlockSpec((tm, DG), lambda i: (i, 0)),
                   pl.BlockSpec((NH, nc, HD, HD), lambda i: (0, i, 0, 0))],
        out_shape=[jax.ShapeDtypeStruct((T, DG), BF16), jax.ShapeDtypeStruct((T, DG), F32),
                   jax.ShapeDtypeStruct((NH, NCHUNK, HD, HD), F32)],
        scratch_shapes=[pltpu.VMEM((NH, HD, HD), F32),
                        pltpu.VMEM((tm, DG), BF16), pltpu.VMEM((tm, DG), BF16), pltpu.VMEM((tm, DG), BF16),
                        pltpu.VMEM((tm, DG), BF16), pltpu.VMEM((nc, DG), F32),
                        pltpu.VMEM((nc * NH, CH, CH), BF16), pltpu.VMEM((nc * NH, HD, HD), F32)],
        compiler_params=_cparams(("arbitrary",), 48),
    )(pb, pf, pb, pb, hg_lb, hg_nw)


def _tail_fwd_bwd(x, p, tgt, ya, yb, w_out_b, w_pg_b, w_pp_b, ple_nw, b_pg, fnw):
    tm = 512
    nt = T // tm
    QR = D // NSHARD

    def body(x_ref, p_ref, t_ref, ya_ref, yb_ref, wo_ref, wg_ref, wp_ref, pw_ref, b_ref, fw_ref,
             dh1_ref, dyab_ref, dwo_ref, dwg_ref, dwp_ref, sm_ref, dwo_s, dwg_s, dwp_s):
        i = pl.program_id(0)

        @pl.when(i == 0)
        def _():
            dwo_s[...] = jnp.zeros_like(dwo_s)
            dwg_s[...] = jnp.zeros_like(dwg_s)
            dwp_s[...] = jnp.zeros_like(dwp_s)
            sm_ref[...] = jnp.zeros_like(sm_ref)

        ya = ya_ref[...]
        yb = yb_ref[...]
        pv = p_ref[...].astype(BF16)
        pw = pw_ref[...]
        fw = fw_ref[...]
        h1 = x_ref[...] + _mm(ya, wo_ref[0:DG, :]) + _mm(yb, wo_ref[DG:D, :])
        s2 = lax.rsqrt(jnp.mean(h1 * h1, axis=-1, keepdims=True) + EPS)
        n2h = h1 * s2
        n2 = (n2h * pw).astype(BF16)
        z = _mm(n2, wg_ref[...]) + b_ref[...]
        gate = _sigmoid(z)
        pp = jnp.concatenate([_mm(pv, wp_ref[k]) for k in range(NSHARD)], axis=1)
        h2 = h1 + gate * pp
        s3 = lax.rsqrt(jnp.mean(h2 * h2, axis=-1, keepdims=True) + EPS)
        hn = h2 * s3
        err = hn * fw - t_ref[...]
        sm_ref[0:1, :] += _rowsum(err * err)
        dy = err * (1.0 / D)
        sm_ref[1:2, :] += _rowsum(dy * hn)
        g3 = dy * fw
        dh2 = s3 * (g3 - hn * jnp.mean(g3 * hn, axis=-1, keepdims=True))
        dpp = (dh2 * gate).astype(BF16)
        dz = dh2 * pp * gate * (1.0 - gate)
        sm_ref[2:3, :] += _rowsum(dz)
        dzb = dz.astype(BF16)
        dwg_s[...] += _mm_tn(n2, dzb)
        dn2 = _mm_nt(dzb, wg_ref[...])
        for k in range(NSHARD):
            dwp_s[k] += _mm_tn(pv, dpp[:, k * PLE:(k + 1) * PLE])
        sm_ref[3:4, :] += _rowsum(dn2 * n2h)
        g2 = dn2 * pw
        dh1 = dh2 + s2 * (g2 - n2h * jnp.mean(g2 * n2h, axis=-1, keepdims=True))
        dh1_ref[...] = dh1
        dh1b = dh1.astype(BF16)
        dyab_ref[...] = _mm_nt(dh1b, wo_ref[...])
        dwo_s[0:DG, :] += _mm_tn(ya, dh1b)
        dwo_s[DG:D, :] += _mm_tn(yb, dh1b)

        @pl.when(i == nt - 1)
        def _():
            total = jnp.sum(sm_ref[0:1, :], axis=-1, keepdims=True) * (0.5 / D)
            sm_ref[4:5, :] = jnp.broadcast_to(total, (1, D))
            for k in range(NSHARD):
                dwo_ref[k] = dwo_s[k * QR:(k + 1) * QR, :].astype(BF16)
                dwg_ref[k] = dwg_s[k * QR:(k + 1) * QR, :].astype(BF16)
                dwp_ref[k] = dwp_s[k].astype(BF16)

    row = lambda w: pl.BlockSpec((tm, w), lambda i: (i, 0))
    const2 = lambda s: pl.BlockSpec(s, lambda i: (0, 0), pipeline_mode=pl.Buffered(1))
    const3 = lambda s: pl.BlockSpec(s, lambda i: (0, 0, 0), pipeline_mode=pl.Buffered(1))
    return pl.pallas_call(
        body, name="tail_fwd_bwd", grid=(nt,),
        in_specs=[row(D), row(PLE), row(D), row(DG), row(DG),
                  const2((D, D)), const2((D, D)), const3((NSHARD, PLE, PLE)),
                  const2((1, D)), const2((1, D)), const2((1, D))],
        out_specs=[row(D), row(D), const3((NSHARD, QR, D)), const3((NSHARD, QR, D)),
                   const3((NSHARD, PLE, PLE)), const2((8, D))],
        out_shape=[jax.ShapeDtypeStruct((T, D), F32), jax.ShapeDtypeStruct((T, D), F32),
                   jax.ShapeDtypeStruct((NSHARD, QR, D), BF16), jax.ShapeDtypeStruct((NSHARD, QR, D), BF16),
                   jax.ShapeDtypeStruct((NSHARD, PLE, PLE), BF16), jax.ShapeDtypeStruct((8, D), F32)],
        scratch_shapes=[pltpu.VMEM((D, D), F32), pltpu.VMEM((D, D), F32), pltpu.VMEM((NSHARD, PLE, PLE), F32)],
        compiler_params=_cparams(("arbitrary",), 62),
    )(x, p, tgt, ya, yb, w_out_b, w_pg_b, w_pp_b, ple_nw, b_pg, fnw)


def _bwd_rglru(pf, pb, h, dyab, cw3, conv_b, wa_c, ba, wx_c, bx, lam, dep):
    tm = 512
    nt = T // tm
    ng = tm // 8

    def body(xa_ref, ga_ref, xp_ref, h_ref, hp_ref, dya_ref, cw_ref, cb_ref, wa_ref, ba_ref, wx_ref, bx_ref,
             lam_ref, dep_ref, da_ref, rg_ref, sm_ref, a_s, g_s, cg_s, nxt_s, wa_s, wx_s, dwa_s, dwx_s):
        i = pl.program_id(0)
        tile = nt - 1 - i

        @pl.when(i == 0)
        def _():
            dwa_s[...] = jnp.zeros_like(dwa_s)
            dwx_s[...] = jnp.zeros_like(dwx_s)
            sm_ref[...] = jnp.zeros_like(sm_ref)
            cg_s[...] = jnp.zeros_like(cg_s)
            nxt_s[...] = jnp.zeros_like(nxt_s)
            wa_s[...] = _dense_from_blocks(wa_ref[...]).astype(BF16)
            wx_s[...] = _dense_from_blocks(wx_ref[...]).astype(BF16)

        rows = lax.broadcasted_iota(I32, (tm, DG), 0)
        has_prev = tile > 0
        xa = xa_ref[...]
        xprev = jnp.where(has_prev, xp_ref[...], 0.0)
        cw = _conv_rows(cw_ref)
        xc, taps = _rg_conv(xa, xprev, cw, cb_ref[...], rows)
        lam_v = lam_ref[...]
        sp = _softplus(-lam_v)
        first_row = (rows + tile * tm) == 0
        r, ig, a, a2, mult = _rg_gates(xc, wa_s[...], ba_ref[...], wx_s[...], bx_ref[...], sp, first_row)
        hv = h_ref[...]
        hprev = jnp.where(has_prev, hp_ref[...], 0.0)
        h_m1 = _shift_down(hv, hprev, 1, rows)
        ga = ga_ref[...].astype(F32)
        sg = _sigmoid(ga)
        dya = dya_ref[...]
        dga = dya * hv * (sg * (1.0 + ga * (1.0 - sg)))

        av = jnp.where(rows == tm - 1, 1.0, pltpu.roll(a, tm - 1, 0))
        gv = dya * (ga * sg)
        rows8 = rows & 7
        for d in (1, 2, 4):
            keep = rows8 < 8 - d
            gv = gv + av * jnp.where(keep, _roll_in_groups(gv, 8 - d), 0.0)
            av = av * jnp.where(keep, _roll_in_groups(av, 8 - d), 1.0)
        a_s[...] = av
        g_s[...] = gv
        carry = cg_s[0:1, :]
        for g in range(ng - 1, -1, -1):
            sl = slice(g * 8, (g + 1) * 8)
            ab, gb = a_s[sl, :], g_s[sl, :]
            g_s[sl, :] = gb + ab * carry
            carry = gb[0:1, :] + ab[0:1, :] * carry
        cg_s[0:1, :] = a[0:1, :] * carry

        gt = g_s[...]
        da = gt * h_m1
        ixc = ig * xc
        di = gt * mult * xc
        dxc = gt * mult * ig
        dlog_a = da * a + jnp.where(first_row, 0.0, gt * ixc * (-a2 / mult))
        sm_ref[3:4, :] += _rowsum(dlog_a * ((-RG_C) * r))
        dpr_f = dlog_a * ((-RG_C) * sp) * r * (1.0 - r)
        dpi_f = di * ig * (1.0 - ig)
        sm_ref[1:2, :] += _rowsum(dpr_f)
        sm_ref[2:3, :] += _rowsum(dpi_f)
        dpr = dpr_f.astype(BF16)
        dpi = dpi_f.astype(BF16)
        xcb = xc.astype(BF16)
        dwa_s[...] += _mm_tn(xcb, dpr)
        dwx_s[...] += _mm_tn(xcb, dpi)
        dxc = dxc + _mm_nt(dpr, wa_s[...]) + _mm_nt(dpi, wx_s[...])
        sm_ref[0:1, :] += _rowsum(dxc)
        for j in range(4):
            sm_ref[4 + j:5 + j, :] += _rowsum(dxc * taps[j])
        nxt = nxt_s[...]
        dxa = (dxc * cw[3] + _shift_up(dxc, nxt, 1, rows) * cw[2]
               + _shift_up(dxc, nxt, 2, rows) * cw[1] + _shift_up(dxc, nxt, 3, rows) * cw[0])
        nxt_s[...] = dxc[0:8, :]
        da_ref[:, 0:DG] = dxa.astype(BF16)
        da_ref[:, DG:D] = dga.astype(BF16)

        @pl.when(i == nt - 1)
        def _():
            _, s_neg = _sig_pair(lam_v)
            sm_ref[3:4, :] = sm_ref[3:4, :] * (-s_neg)
            rg_ref[...] = _blocks_from_dense(dwa_s[...], dwx_s[...])

    vec = lambda: pl.BlockSpec((1, DG), lambda i: (0, 0))
    blocks = lambda: pl.BlockSpec((DG, RGB), lambda i: (0, 0))
    prev8 = lambda: pl.BlockSpec((8, DG), lambda i: (jnp.maximum((nt - 1 - i) * (tm // 8) - 1, 0), 0))
    return pl.pallas_call(
        body, name="bwd_rglru", grid=(nt,),
        in_specs=[pl.BlockSpec((tm, DG), lambda i: (nt - 1 - i, 0)),
                  pl.BlockSpec((tm, DG), lambda i: (nt - 1 - i, 0)),
                  prev8(),
                  pl.BlockSpec((tm, DG), lambda i: (nt - 1 - i, 0)),
                  prev8(),
                  pl.BlockSpec((tm, DG), lambda i: (nt - 1 - i, 0)),
                  pl.BlockSpec((NSHARD, 4, 128), lambda i: (0, 0, 0)), vec(),
                  blocks(), vec(), blocks(), vec(), vec(), ANY_SPEC],
        out_specs=[pl.BlockSpec((tm, D), lambda i: (nt - 1 - i, 0)),
                   pl.BlockSpec((DG, 128), lambda i: (0, 0)),
                   pl.BlockSpec((8, DG), lambda i: (0, 0))],
        out_shape=[jax.ShapeDtypeStruct((T, D), BF16), jax.ShapeDtypeStruct((DG, 128), F32),
                   jax.ShapeDtypeStruct((8, DG), F32)],
        scratch_shapes=[pltpu.VMEM((tm, DG), F32), pltpu.VMEM((tm, DG), F32),
                        pltpu.VMEM((8, DG), F32), pltpu.VMEM((8, DG), F32),
                        pltpu.VMEM((DG, DG), BF16), pltpu.VMEM((DG, DG), BF16),
                        pltpu.VMEM((DG, DG), F32), pltpu.VMEM((DG, DG), F32)],
        compiler_params=_cparams(("arbitrary",), 56),
    )(pf, pb, pf, h, h, dyab, cw3, conv_b, wa_c, ba, wx_c, bx, lam, dep)


def _bwd_hgrn2(pf, pb, o, s_prev, dyab, hg_lb, hg_nw, dep):
    tm = 512
    nt = T // tm
    nc = tm // CH

    def body(q_ref, f_ref, v_ref, g_ref, o_ref, sp_ref, dy_ref, lb_ref, nw_ref, dep_ref, db_ref, sm_ref,
             dst_s, eb_s, enb_s, ekd_s, qe_s, ke_s, kd_s, vb_s, do_s, dec_s, ddec_s, p_s, dp_s,
             g_s, dsta_s, dva_s, dqe_s, dke_s, dkd_s, dlf_s):
        i = pl.program_id(0)

        @pl.when(i == 0)
        def _():
            sm_ref[...] = jnp.zeros_like(sm_ref)
            dst_s[...] = jnp.zeros_like(dst_s)

        lb, one_m_lb = _hg_lower_bound(lb_ref)
        rows64 = lax.broadcasted_iota(I32, (tm, DG), 0) & (CH - 1)
        sg, sn, f, k, q, sq, qs = _hg_recompute(
            q_ref, f_ref, lb, one_m_lb, rows64, eb_s, enb_s, ekd_s, qe_s, ke_s, kd_s, dec_s)
        vb_s[...] = v_ref[...]

        nw = nw_ref[...]
        for hd in range(NH):
            cols = slice(hd * HD, (hd + 1) * HD)
            g = g_ref[:, cols].astype(F32)
            sgg = _sigmoid(g)
            o = o_ref[:, cols]
            so = lax.rsqrt(jnp.mean(o * o, axis=-1, keepdims=True) + EPS)
            oh = o * so
            dyb = dy_ref[:, cols]
            db_ref[:, 3 * DG + hd * HD:3 * DG + (hd + 1) * HD] = (
                dyb * (oh * nw) * (sgg * (1.0 + g * (1.0 - sgg)))).astype(BF16)
            don = dyb * (g * sgg)
            sm_ref[1:2, 0:HD] += _rowsum(don * oh)
            gw = don * nw
            do_s[:, cols] = (so * (gw - oh * jnp.mean(gw * oh, axis=-1, keepdims=True))).astype(BF16)

        mask = _tri(True)
        items = [(c, hd, slice(c * CH, (c + 1) * CH), slice(hd * HD, (hd + 1) * HD))
                 for c in range(nc) for hd in range(NH)]
        for c, hd, rs, cols in items:
            p_s[c * NH + hd] = jnp.where(mask, _mm_nt(qe_s[rs, cols], ke_s[rs, cols]), 0.0).astype(BF16)
            dp_s[c * NH + hd] = jnp.where(mask, _mm_nt(do_s[rs, cols], vb_s[rs, cols]), 0.0).astype(BF16)
        for c, hd, rs, cols in items:
            n = c * NH + hd
            dva_s[rs, cols] = _mm_tn(p_s[n], do_s[rs, cols])
            dqe_s[rs, cols] = _mm(dp_s[n], ke_s[rs, cols])
            dke_s[rs, cols] = _mm_tn(dp_s[n], qe_s[rs, cols])
            g_s[n] = _mm_tn(do_s[rs, cols], qe_s[rs, cols])
        for c, hd, rs, cols in reversed(items):
            n = c * NH + hd
            dst = dst_s[hd]
            dsta_s[n] = dst
            dst_s[hd] = dst * dec_s[c:c + 1, cols] + g_s[n]
        for c, hd, rs, cols in items:
            n = c * NH + hd
            dst = dsta_s[n]
            st_prev = sp_ref[hd, c]
            dv = dva_s[rs, cols] + _mm_nt(kd_s[rs, cols], dst)
            db_ref[rs, 2 * DG + hd * HD:2 * DG + (hd + 1) * HD] = dv.astype(BF16)
            dqe_s[rs, cols] += _mm(do_s[rs, cols], st_prev)
            dkd_s[rs, cols] = _mm(vb_s[rs, cols], dst)
            ddec_s[c:c + 1, cols] = _rowsum(dst * st_prev)

        eb, enb, ekd = eb_s[...], enb_s[...], ekd_s[...]
        dqe, dke, dkd = dqe_s[...], dke_s[...], dkd_s[...]
        t_kd = dkd * (k * ekd)
        rc = _chunk_rev_cumsum(dqe * (qs * eb) - dke * (k * enb) - t_kd, rows64)
        for c in range(nc):
            rs = slice(c * CH, (c + 1) * CH)
            dbl = _rowsum(t_kd[rs]) + ddec_s[c:c + 1, :] * dec_s[c:c + 1, :]
            dlf_s[rs, :] = rc[rs] + dbl
        t = dlf_s[...] / f - (dke * enb + dkd * ekd)
        db_ref[:, DG:2 * DG] = (one_m_lb * sg * sn * t).astype(BF16)
        sm_ref[0:1, :] += _rowsum(sn * t)
        db_ref[:, 0:DG] = (dqe * eb * (sq * (1.0 + q * (1.0 - sq))) * (HD ** -0.5)).astype(BF16)

        @pl.when(i == nt - 1)
        def _():
            dsm = sm_ref[0:1, :] * (lb * one_m_lb)
            sm_ref[2:3, :] = dsm
            sm_ref[3:4, :] = -dsm

    col = lambda j: pl.BlockSpec((tm, DG), lambda i: (nt - 1 - i, j))
    big = lambda dt: pltpu.VMEM((tm, DG), dt)
    return pl.pallas_call(
        body, name="bwd_hgrn2", grid=(nt,),
        in_specs=[col(1), col(1), col(2), col(3),
                  pl.BlockSpec((tm, DG), lambda i: (nt - 1 - i, 0)),
                  pl.BlockSpec((NH, nc, HD, HD), lambda i: (0, nt - 1 - i, 0, 0)),
                  pl.BlockSpec((tm, DG), lambda i: (nt - 1 - i, 1)),
                  pl.BlockSpec((2, DG), lambda i: (0, 0)),
                  pl.BlockSpec((1, HD), lambda i: (0, 0)), ANY_SPEC],
        out_specs=[pl.BlockSpec((tm, 4 * DG), lambda i: (nt - 1 - i, 0)),
                   pl.BlockSpec((8, DG), lambda i: (0, 0))],
        out_shape=[jax.ShapeDtypeStruct((T, 4 * DG), BF16), jax.ShapeDtypeStruct((8, DG), F32)],
        scratch_shapes=[pltpu.VMEM((NH, HD, HD), F32),
                        big(F32), big(F32), big(F32),
                        big(BF16), big(BF16), big(BF16), big(BF16), big(BF16),
                        pltpu.VMEM((nc, DG), F32), pltpu.VMEM((nc, DG), F32),
                        pltpu.VMEM((nc * NH, CH, CH), BF16), pltpu.VMEM((nc * NH, CH, CH), BF16),
                        pltpu.VMEM((nc * NH, HD, HD), F32), pltpu.VMEM((nc * NH, HD, HD), F32),
                        big(F32), big(F32), big(F32), big(F32), big(F32)],
        compiler_params=_cparams(("arbitrary",), 56),
    )(pb, pf, pb, pb, o, s_prev, dyab, hg_lb, hg_nw, dep)


def _dproj_pieces(k, da_ref, db_ref):
    if k == 0:
        return [(da_ref[:, 0:SHW], 0)]
    if k == 1:
        return [(da_ref[:, SHW:D], 0), (db_ref[:, 0:DG], D - SHW)]
    if k == 2:
        return [(db_ref[:, DG:DG + SHW], 0)]
    return [(db_ref[:, DG + SHW:4 * DG], 0)]


def _bwd_inproj_dx(x, dh1, d_a, d_b, w_in_b, nw, dep):
    tm = 512
    nt = T // tm

    def body(x_ref, dh1_ref, da_ref, db_ref, w_ref, nw_ref, dep_ref, dx_ref, sm_ref):
        i = pl.program_id(0)

        @pl.when(i == 0)
        def _():
            sm_ref[...] = jnp.zeros_like(sm_ref)

        du = None
        for k in range(NSHARD):
            for val, off in _dproj_pieces(k, da_ref, db_ref):
                t = _mm_nt(val, w_ref[k, :, off:off + val.shape[1]])
                du = t if du is None else du + t
        xv = x_ref[...]
        s = lax.rsqrt(jnp.mean(xv * xv, axis=-1, keepdims=True) + EPS)
        xh = xv * s
        sm_ref[0:1, :] += _rowsum(du * xh)
        g = du * nw_ref[...]
        dx_ref[...] = dh1_ref[...] + s * (g - xh * jnp.mean(g * xh, axis=-1, keepdims=True))

    row = lambda w: pl.BlockSpec((tm, w), lambda i: (i, 0))
    return pl.pallas_call(
        body, name="bwd_inproj_dx", grid=(nt,),
        in_specs=[row(D), row(D), row(D), row(4 * DG),
                  pl.BlockSpec((NSHARD, D, SHW), lambda i: (0, 0, 0), pipeline_mode=pl.Buffered(1)),
                  pl.BlockSpec((1, D), lambda i: (0, 0)), ANY_SPEC],
        out_specs=[row(D), pl.BlockSpec((8, D), lambda i: (0, 0))],
        out_shape=[jax.ShapeDtypeStruct((T, D), F32), jax.ShapeDtypeStruct((8, D), F32)],
        compiler_params=_cparams(("arbitrary",), 56),
    )(x, dh1, d_a, d_b, w_in_b, nw, dep)


def _bwd_inproj_dw(name, ks, u_b, d_a, d_b):
    tm = 1024
    nt = T // tm
    nk = len(ks)
    H = D // 2
    step = 128
    need_a = 0 in ks or 1 in ks

    def body(*refs):
        if need_a:
            u_ref, da_ref, db_ref, dw_ref, acc, send_s, recv_s, ssem, rsem = refs
        else:
            u_ref, db_ref, dw_ref, acc, send_s, recv_s, ssem, rsem = refs
            da_ref = None
        i = pl.program_id(0)

        @pl.when(i == 0)
        def _():
            acc[...] = jnp.zeros_like(acc)

        u = u_ref[...]
        for j, k in enumerate(ks):
            for val, off in _dproj_pieces(k, da_ref, db_ref):
                acc[j, :, off:off + val.shape[1]] += _mm_tn(u, val)

        @pl.when(i == nt - 1)
        def _():
            x, y, c = lax.axis_index("x"), lax.axis_index("y"), lax.axis_index("c")
            sibling = (x, y, 1 - c)
            mine0 = pl.multiple_of(c * H, step)
            other0 = pl.multiple_of((1 - c) * H, step)
            copies = []
            for j in range(nk):
                for r0 in range(0, H, step):
                    send_s[j, r0:r0 + step, :] = acc[j, pl.ds(other0 + r0, step), :].astype(BF16)
                cp = _remote(send_s.at[j], recv_s.at[j], ssem.at[j], rsem.at[j], sibling)
                cp.start()
                copies.append(cp)
            for j in range(nk):
                copies[j].wait_recv()
                for r0 in range(0, H, step):
                    s = acc[j, pl.ds(mine0 + r0, step), :] + recv_s[j, r0:r0 + step, :].astype(F32)
                    dw_ref[j, r0:r0 + step, :] = s.astype(BF16)
            for cp in copies:
                cp.wait_send()

    row = lambda w: pl.BlockSpec((tm, w), lambda i: (i, 0))
    ins = [u_b] + ([d_a] if need_a else []) + [d_b]
    in_specs = [row(D)] + ([row(D)] if need_a else []) + [row(4 * DG)]
    return pl.pallas_call(
        body, name=name, grid=(nt,), in_specs=in_specs,
        out_specs=pl.BlockSpec((nk, H, SHW), lambda i: (0, 0, 0)),
        out_shape=jax.ShapeDtypeStruct((nk, H, SHW), BF16),
        scratch_shapes=[pltpu.VMEM((nk, D, SHW), F32), pltpu.VMEM((nk, H, SHW), BF16),
                        pltpu.VMEM((nk, H, SHW), BF16),
                        pltpu.SemaphoreType.DMA((nk,)), pltpu.SemaphoreType.DMA((nk,))],
        compiler_params=_cparams(("arbitrary",), 48),
    )(*ins)


_OUT_ORDER = ["norm_mix_w", "w_in", "conv_w", "conv_b", "rg_wa", "rg_ba", "rg_wx", "rg_bx", "rg_lambda", "hg_lb",
              "hg_norm_w", "w_out", "ple_norm_w", "w_ple_gate", "b_ple_gate", "w_ple_proj", "final_norm_w"]
_BIG = ["w_in", "w_out", "w_ple_gate", "w_ple_proj"]


def _small_view(name, a):
    if name in ("rg_wa", "rg_wx"):
        return a.reshape(DG, RGB)
    if name == "conv_w":
        return a.reshape(4, 128)
    if name == "final_norm_w":
        return a.reshape(1, D)
    return a


def _landing(rows, cols):
    return lax.empty((NDEV, rows, cols), BF16)


def kernel(x, p, norm_mix_w, w_in, conv_w, conv_b, rg_wa, rg_ba, rg_wx, rg_bx, rg_lambda, hg_lb, hg_norm_w, w_out, ple_norm_w, w_ple_gate, b_ple_gate, w_ple_proj, final_norm_w, loss_target, m_norm_mix_w, m_w_in, m_conv_w, m_conv_b, m_rg_wa, m_rg_ba, m_rg_wx, m_rg_bx, m_rg_lambda, m_hg_lb, m_hg_norm_w, m_w_out, m_ple_norm_w, m_w_ple_gate, m_b_ple_gate, m_w_ple_proj, m_final_norm_w, v_norm_mix_w, v_w_in, v_conv_w, v_conv_b, v_rg_wa, v_rg_ba, v_rg_wx, v_rg_bx, v_rg_lambda, v_hg_lb, v_hg_norm_w, v_w_out, v_ple_norm_w, v_w_ple_gate, v_b_ple_gate, v_w_ple_proj, v_final_norm_w):
    given = dict(locals())
    x2, p2, tgt = x[0], p[0, 0], loss_target[0]
    wa_c, wx_c = _small_view("rg_wa", rg_wa), _small_view("rg_wx", rg_wx)

    w_in_b, l_out, l_pg, l_pp, cw3 = _gather_w_in(w_in[0], w_out[0], w_ple_gate[0], w_ple_proj[0], conv_w[0])
    g_ssem, g_rsem, g_lands, tok = _gather_rest_start([l_out, l_pg, l_pp])

    pf, pb, u_b = _fwd_inproj(x2, norm_mix_w, w_in_b, tok)
    h, ya = _fwd_rglru(pf, pb, cw3, conv_b, wa_c, rg_ba, wx_c, rg_bx, rg_lambda)
    yb, o, s_prev = _fwd_hgrn2(pf, pb, hg_lb, hg_norm_w)
    w_out_b, w_pg_b, w_pp_b = _gather_rest_wait(g_ssem, g_rsem, g_lands, yb)
    dh1, dyab, dwo_b, dwg_b, dwp_b, sm_tail = _tail_fwd_bwd(
        x2, p2, tgt, ya, yb, w_out_b.reshape(D, D), w_pg_b.reshape(D, D), w_pp_b,
        ple_norm_w, b_ple_gate, final_norm_w.reshape(1, D))

    QH = D // NSHARD // 2
    r1 = _rs_start("rs_start_tail", [dwo_b, dwg_b, dwp_b], (0, 1, 2, 3),
                   [_landing(QH, D), _landing(QH, D), _landing(PLE // 2, PLE)])
    d_b, sm_b = _bwd_hgrn2(pf, pb, o, s_prev, dyab, hg_lb, hg_norm_w, r1[4])
    dw23 = _bwd_inproj_dw("bwd_inproj_dw23", (2, 3), u_b, None, d_b)
    r2 = _rs_start("rs_start_in23", [dw23], (2, 3), [lax.empty((NSHARD, D // 2, SHW), BF16)], chip_sums=True)
    d_a, rg_c, sm_a = _bwd_rglru(pf, pb, h, dyab, cw3, conv_b, wa_c, rg_ba, wx_c, rg_bx, rg_lambda, r2[4])
    dw01 = _bwd_inproj_dw("bwd_inproj_dw01", (0, 1), u_b, d_a, d_b)
    r3 = _rs_start("rs_start_in01", [dw01], (0, 1), r2[3], chip_sums=True)
    grad_x, sm_in = _bwd_inproj_dx(x2, dh1, d_a, d_b, w_in_b, norm_mix_w, r3[4])

    vred, rgred = _small_allreduce(sm_in, sm_tail, sm_a, sm_b, rg_c)
    parts1, lands1 = _rs_wait("rs_wait_tail", r1[0], r1[1], r1[2], (0, 1, 2, 3), r1[3], vred)
    parts2, lands2 = _rs_wait("rs_wait_in23", r2[0], r2[1], r2[2], (2, 3), r3[3], vred, chip_sums=True)
    parts3, lands3 = _rs_wait("rs_wait_in01", r3[0], r3[1], r3[2], (0, 1), lands2, vred, chip_sums=True)
    g_big = _reduce_exchange([parts3[0], parts2[0]] + parts1, lands3 + lands1)

    upd_big = _adam_big(g_big, [given[n][0] for n in _BIG], [given["m_" + n][0] for n in _BIG],
                        [given["v_" + n][0] for n in _BIG])
    small = _adam_small(vred, rgred,
                        {n: _small_view(n, given[n]) for n in _SMALL_ORDER},
                        {n: _small_view(n, given["m_" + n]) for n in _SMALL_ORDER},
                        {n: _small_view(n, given["v_" + n]) for n in _SMALL_ORDER})

    loss = vred[0, ROW_LOSS, 0]
    outs = [loss, grad_x[None]]
    for ki in range(4):
        for n in _OUT_ORDER:
            if n in _BIG:
                i = _BIG.index(n)
                a = g_big[i] if ki == 0 else upd_big[i][ki - 1]
                outs.append(a[None])
            else:
                outs.append(small[n][ki].reshape(given[n].shape))
    return tuple(outs)
```

```python
import jax
import jax.numpy as jnp
from jax import lax
from jax.experimental import pallas as pl
from jax.experimental.pallas import tpu as pltpu

F32 = jnp.float32
BF16 = jnp.bfloat16
I32 = jnp.int32
MESH = pl.DeviceIdType.MESH
HIGHEST = lax.Precision.HIGHEST

T = 4096
D = 1024
DG = 512
DIN = 3072
PLE = 256
NH = 4
HD = 128
CH = 64
NCHUNK = T // CH
RGB = 64
EPS = 1e-6
RG_C = 8.0
NSHARD = 4
SHW = DIN // NSHARD
NDEV = 8

ADAM_LR = 0.001
ADAM_B1 = 0.9
ADAM_B2 = 0.999
ADAM_EPS = 1e-08
ADAM_WD = 0.01
ADAM_STEP = 10

VMEM_SPEC = pl.BlockSpec(memory_space=pltpu.VMEM)
HBM_SPEC = pl.BlockSpec(memory_space=pltpu.HBM)
SEM_SPEC = pl.BlockSpec(memory_space=pltpu.SEMAPHORE)
ANY_SPEC = pl.BlockSpec(memory_space=pl.ANY)
EFFECT = pltpu.SideEffectType.DATAFLOW_SIDE_EFFECTING
MIB = 1024 * 1024

VROWS = 16
ROW_NORM_MIX, ROW_FINAL_NORM, ROW_B_PG, ROW_PLE_NORM = 0, 1, 2, 3
ROW_CB_BA, ROW_BX_LAM, ROW_CW01, ROW_CW23, ROW_HG_LB, ROW_HG_NW, ROW_LOSS = 4, 5, 6, 7, 8, 9, 10


def _mm(a, b):
    return jnp.dot(a.astype(BF16), b.astype(BF16), preferred_element_type=F32)


def _mm_nt(a, b):
    return lax.dot_general(a.astype(BF16), b.astype(BF16), (((1,), (1,)), ((), ())),
                           preferred_element_type=F32)


def _mm_tn(a, b):
    return lax.dot_general(a.astype(BF16), b.astype(BF16), (((0,), (0,)), ((), ())),
                           preferred_element_type=F32)


def _mm_exact(a, b):
    return jnp.dot(a, b, precision=HIGHEST, preferred_element_type=F32)


def _sig_pair(x):
    e = jnp.exp(-jnp.abs(x))
    big = 1.0 / (1.0 + e)
    small = e * big
    pos = x >= 0
    return jnp.where(pos, big, small), jnp.where(pos, small, big)


def _sigmoid(x):
    return 1.0 / (1.0 + jnp.exp(-x))


def _rowsum(v):
    return jnp.sum(v, axis=0, keepdims=True)


def _shift_down(cur, prev8, d, rows):
    rolled = pltpu.roll(cur, d, 0)
    head = jnp.where(rows[0:8] < d, pltpu.roll(prev8, d, 0), rolled[0:8])
    return jnp.concatenate([head, rolled[8:]], axis=0)


def _shift_up(cur, next8, d, rows):
    n = cur.shape[0]
    rolled = pltpu.roll(cur, n - d, 0)
    tail = jnp.where(rows[0:8] >= 8 - d, pltpu.roll(next8, 8 - d, 0), rolled[n - 8:n])
    return jnp.concatenate([rolled[0:n - 8], tail], axis=0)


def _roll_in_groups(v, d):
    n, w = v.shape
    return pltpu.roll(v.reshape(n // 8, 8, w), d, 1).reshape(n, w)


def _cparams(sem, vmem_mib):
    return pltpu.CompilerParams(dimension_semantics=sem, vmem_limit_bytes=vmem_mib * MIB)


def _mesh_pos():
    x, y, c = lax.axis_index("x"), lax.axis_index("y"), lax.axis_index("c")
    chips = [(1 - x, y), (x, 1 - y), (1 - x, 1 - y)]
    return x, y, c, chips


def _remote(src, dst, ssem, rsem, dev):
    return pltpu.make_async_remote_copy(src_ref=src, dst_ref=dst, send_sem=ssem, recv_sem=rsem,
                                        device_id=dev, device_id_type=MESH)


def _gather_w_in(w_in, w_out, w_pg, w_pp, conv_w):
    shapes = [w_in.shape, w_out.shape, w_pg.shape, w_pp.shape]

    def body(win, wout, wpg, wpp, cw, o_in, o_out, o_pg, o_pp, o_cw, ssem, rsem):
        x, y, c, chips = _mesh_pos()
        kme = 2 * x + y
        sibling = (x, y, 1 - c)

        def cast(src, dst):
            for r0 in range(0, src.shape[0], 128):
                dst[kme, r0:r0 + 128, :] = src[r0:r0 + 128, :].astype(BF16)

        cast(win, o_in)
        o_cw[kme] = cw[...]

        hrows = D // 2
        mine = pl.ds(pl.multiple_of(c * hrows, 128), hrows)
        other = pl.ds(pl.multiple_of((1 - c) * hrows, 128), hrows)

        def half(k, rows):
            return o_in.at[k, rows]

        na = (x ^ c, y ^ (1 - c), c)
        nb = (x ^ (1 - c), y ^ c, c)
        ka = 2 * na[0] + na[1]
        kb = 2 * nb[0] + nb[1]
        kd = 2 * (1 - x) + (1 - y)
        sends = [_remote(half(kme, mine), half(kme, mine), ssem.at[0], rsem.at[0], na),
                 _remote(half(kme, mine), half(kme, mine), ssem.at[1], rsem.at[1], nb)]
        for j, (px, py) in enumerate(chips):
            sends.append(_remote(o_cw.at[kme], o_cw.at[kme], ssem.at[3 + j], rsem.at[3 + j], (px, py, c)))
        for cp in sends:
            cp.start()
        cast(wout, o_out)
        cast(wpg, o_pg)
        cast(wpp, o_pp)

        def passed_on(k, sem, dev):
            cp = _remote(half(k, mine), half(k, mine), ssem.at[sem], rsem.at[sem], dev)
            cp.start()
            sends.append(cp)

        _remote(half(ka, mine), half(ka, mine), ssem.at[0], rsem.at[0], na).wait_recv()
        passed_on(ka, 2, nb)
        passed_on(ka, 6, sibling)
        _remote(half(kb, mine), half(kb, mine), ssem.at[1], rsem.at[1], nb).wait_recv()
        passed_on(kb, 7, sibling)
        _remote(half(kd, mine), half(kd, mine), ssem.at[2], rsem.at[2], nb).wait_recv()
        passed_on(kd, 8, sibling)
        _remote(half(kb, other), half(kb, other), ssem.at[6], rsem.at[6], sibling).wait_recv()
        _remote(half(ka, other), half(ka, other), ssem.at[7], rsem.at[7], sibling).wait_recv()
        _remote(half(kd, other), half(kd, other), ssem.at[8], rsem.at[8], sibling).wait_recv()
        for j, (px, py) in enumerate(chips):
            kj = 2 * px + py
            _remote(o_cw.at[kj], o_cw.at[kj], ssem.at[3 + j], rsem.at[3 + j], (px, py, c)).wait_recv()
        for cp in sends:
            cp.wait_send()

    out_shape = [jax.ShapeDtypeStruct((NSHARD,) + s, BF16) for s in shapes]
    out_shape.append(jax.ShapeDtypeStruct((NSHARD,) + conv_w.shape, F32))
    return pl.pallas_call(
        body, name="gather_w_in", out_shape=out_shape,
        in_specs=[VMEM_SPEC] * 5, out_specs=[VMEM_SPEC] * 5,
        scratch_shapes=[pltpu.SemaphoreType.DMA((9,)), pltpu.SemaphoreType.DMA((9,))],
        compiler_params=pltpu.CompilerParams(vmem_limit_bytes=40 * MIB),
    )(w_in, w_out, w_pg, w_pp, conv_w)


def _gather_rest_start(lands):
    n = len(lands)

    def body(*refs):
        land_in = refs[0:n]
        ssem, rsem = refs[n], refs[n + 1]
        token = refs[2 * n + 2]
        x, y, c, chips = _mesh_pos()
        kme = 2 * x + y
        for p, land in enumerate(land_in):
            hrows = land.shape[1] // 2
            mine = pl.ds(pl.multiple_of(c * hrows, 128), hrows)
            for px, py in chips:
                for pc in range(2):
                    _remote(land.at[kme, mine], land.at[kme, mine], ssem.at[p], rsem.at[p], (px, py, pc)).start()
        token[...] = jnp.zeros_like(token)

    out_shape = ([pltpu.SemaphoreType.DMA((n,)), pltpu.SemaphoreType.DMA((n,))]
                 + [pltpu.HBM(a.shape, a.dtype) for a in lands] + [jax.ShapeDtypeStruct((8, 128), F32)])
    outs = pl.pallas_call(
        body, name="gather_rest_start", out_shape=out_shape,
        in_specs=[HBM_SPEC] * n, out_specs=[SEM_SPEC, SEM_SPEC] + [HBM_SPEC] * n + [VMEM_SPEC],
        input_output_aliases={i: 2 + i for i in range(n)},
        compiler_params=pltpu.CompilerParams(has_side_effects=EFFECT),
    )(*[pltpu.with_memory_space_constraint(a, pltpu.HBM) for a in lands])
    return outs[0], outs[1], list(outs[2:2 + n]), outs[2 + n]


def _gather_rest_wait(ssem, rsem, lands, after):
    n = len(lands)

    def body(*refs):
        land_in = refs[0:n]
        ssem_ref, rsem_ref = refs[n], refs[n + 1]
        x, y, c = lax.axis_index("x"), lax.axis_index("y"), lax.axis_index("c")
        for p, land in enumerate(land_in):
            three = land.at[pl.ds(0, 3)]
            cp = _remote(three, three, ssem_ref.at[p], rsem_ref.at[p], (x, y, c))
            cp.wait_send()
            cp.wait_recv()

    outs = pl.pallas_call(
        body, name="gather_rest_wait", out_shape=[pltpu.HBM(a.shape, a.dtype) for a in lands],
        in_specs=[HBM_SPEC] * n + [SEM_SPEC, SEM_SPEC, ANY_SPEC], out_specs=[HBM_SPEC] * n,
        input_output_aliases={i: i for i in range(n)},
        compiler_params=pltpu.CompilerParams(has_side_effects=EFFECT),
    )(*lands, ssem, rsem, after)
    return list(outs)


def _rs_start(name, parts, ks, lands, chip_sums=False):
    n = len(parts)

    def body(*refs):
        part_in, land_in = refs[0:n], refs[n:2 * n]
        ssem, rsem = refs[2 * n], refs[2 * n + 1]
        token = refs[4 * n + 2]
        x, y, c = lax.axis_index("x"), lax.axis_index("y"), lax.axis_index("c")
        kme = 2 * x + y
        me = 4 * x + 2 * y + c
        for p in range(n):
            hrows = land_in[p].shape[1]
            for i, k in enumerate(ks):
                if chip_sums:
                    @pl.when(kme != k)
                    def _():
                        _remote(part_in[p].at[i], land_in[p].at[kme], ssem.at[p], rsem.at[p], (k // 2, k % 2, c)).start()
                    continue
                for pc in range(2):
                    @pl.when(jnp.logical_or(kme != k, c != pc))
                    def _():
                        _remote(part_in[p].at[i, pl.ds(pc * hrows, hrows)], land_in[p].at[me],
                                ssem.at[p], rsem.at[p], (k // 2, k % 2, pc)).start()
        token[...] = jnp.zeros_like(token)

    arrays = list(parts) + list(lands)
    out_shape = ([pltpu.SemaphoreType.DMA((n,)), pltpu.SemaphoreType.DMA((n,))]
                 + [pltpu.HBM(a.shape, a.dtype) for a in arrays] + [jax.ShapeDtypeStruct((8, 128), F32)])
    outs = pl.pallas_call(
        body, name=name, out_shape=out_shape,
        in_specs=[HBM_SPEC] * (2 * n), out_specs=[SEM_SPEC, SEM_SPEC] + [HBM_SPEC] * (2 * n) + [VMEM_SPEC],
        input_output_aliases={i: 2 + i for i in range(2 * n)},
        compiler_params=pltpu.CompilerParams(has_side_effects=EFFECT),
    )(*[pltpu.with_memory_space_constraint(a, pltpu.HBM) for a in arrays])
    return outs[0], outs[1], list(outs[2:2 + n]), list(outs[2 + n:2 + 2 * n]), outs[2 + 2 * n]


def _rs_wait(name, ssem, rsem, parts, ks, lands, after, chip_sums=False):
    n = len(parts)

    def body(*refs):
        part_in, land_in = refs[0:n], refs[n:2 * n]
        ssem_ref, rsem_ref = refs[2 * n], refs[2 * n + 1]
        x, y, c = lax.axis_index("x"), lax.axis_index("y"), lax.axis_index("c")
        kme = 2 * x + y
        for p in range(n):
            piece = land_in[p].at[0]
            for k in ks:
                for pc in range(1 if chip_sums else 2):
                    mine = (kme == k) if chip_sums else jnp.logical_and(kme == k, c == pc)

                    @pl.when(jnp.logical_not(mine))
                    def _():
                        _remote(piece, piece, ssem_ref.at[p], rsem_ref.at[p], (x, y, c)).wait_send()
            owner = kme == ks[0]
            for k in ks[1:]:
                owner = jnp.logical_or(owner, kme == k)

            @pl.when(owner)
            def _():
                others = land_in[p].at[pl.ds(0, land_in[p].shape[0] - 1)]
                _remote(others, others, ssem_ref.at[p], rsem_ref.at[p], (x, y, c)).wait_recv()

    arrays = list(parts) + list(lands)
    outs = pl.pallas_call(
        body, name=name, out_shape=[pltpu.HBM(a.shape, a.dtype) for a in arrays],
        in_specs=[HBM_SPEC] * (2 * n) + [SEM_SPEC, SEM_SPEC, ANY_SPEC], out_specs=[HBM_SPEC] * (2 * n),
        input_output_aliases={i: i for i in range(2 * n)},
        compiler_params=pltpu.CompilerParams(has_side_effects=EFFECT),
    )(*arrays, ssem, rsem, after)
    return list(outs[0:n]), list(outs[n:2 * n])


def _reduce_exchange(parts, lands):
    shapes = [(2 * l.shape[1], l.shape[2]) for l in lands]
    step = 128

    def body(in01, in23, pout, ppg, ppp, l_in, l_out, l_pg, l_pp, g_in, g_out, g_pg, g_pp, ssem, rsem):
        x, y, c = lax.axis_index("x"), lax.axis_index("y"), lax.axis_index("c")
        kme = 2 * x + y
        me = 4 * x + 2 * y + c
        sibling = (x, y, 1 - c)
        sends = []
        for p, (land, gout) in enumerate([(l_in, g_in), (l_out, g_out), (l_pg, g_pg), (l_pp, g_pp)]):
            hrows = land.shape[1]
            mine0 = pl.multiple_of(c * hrows, step)
            for r0 in range(0, hrows, step):
                rs = pl.ds(mine0 + r0, step)
                if p == 0:
                    own = jnp.where(kme >= 2, in23[kme & 1, r0:r0 + step, :], in01[kme & 1, r0:r0 + step, :])
                    slot = kme
                else:
                    own = (pout, ppg, ppp)[p - 1][kme, rs, :]
                    slot = me
                s = jnp.zeros((step, land.shape[2]), F32)
                for j in range(land.shape[0]):
                    s = s + jnp.where(slot == j, own, land[j, r0:r0 + step, :]).astype(F32)
                gout[rs, :] = s
            mine = pl.ds(mine0, hrows)
            cp = _remote(gout.at[mine], gout.at[mine], ssem.at[p], rsem.at[p], sibling)
            cp.start()
            sends.append(cp)
        for p, (land, gout) in enumerate([(l_in, g_in), (l_out, g_out), (l_pg, g_pg), (l_pp, g_pp)]):
            hrows = land.shape[1]
            other = pl.ds(pl.multiple_of((1 - c) * hrows, step), hrows)
            _remote(gout.at[other], gout.at[other], ssem.at[p], rsem.at[p], sibling).wait_recv()
        for cp in sends:
            cp.wait_send()

    return pl.pallas_call(
        body, name="reduce_exchange", out_shape=[jax.ShapeDtypeStruct(s, F32) for s in shapes],
        in_specs=[VMEM_SPEC] * 9, out_specs=[VMEM_SPEC] * 4,
        scratch_shapes=[pltpu.SemaphoreType.DMA((4,)), pltpu.SemaphoreType.DMA((4,))],
        compiler_params=pltpu.CompilerParams(vmem_limit_bytes=48 * MIB),
    )(*parts, *lands)


def _small_allreduce(sm_in, sm_tail, sm_a, sm_b, rg_c):
    PR = DG // NDEV

    def body(in_ref, tail_ref, a_ref, b_ref, rg_ref, v_out, rg_out, vbuf, vrecv, rgrecv, ssem, rsem):
        x, y, c = lax.axis_index("x"), lax.axis_index("y"), lax.axis_index("c")
        me = 4 * x + 2 * y + c

        def pair(ref, r0, r1):
            return jnp.concatenate([ref[r0:r0 + 1, :], ref[r1:r1 + 1, :]], axis=1)

        rows = {
            ROW_NORM_MIX: in_ref[0:1, :], ROW_FINAL_NORM: tail_ref[1:2, :], ROW_B_PG: tail_ref[2:3, :],
            ROW_PLE_NORM: tail_ref[3:4, :], ROW_CB_BA: pair(a_ref, 0, 1), ROW_BX_LAM: pair(a_ref, 2, 3),
            ROW_CW01: pair(a_ref, 4, 5), ROW_CW23: pair(a_ref, 6, 7), ROW_HG_LB: pair(b_ref, 2, 3),
            ROW_HG_NW: jnp.concatenate([b_ref[1:2, :], jnp.zeros((1, DG), F32)], axis=1),
            ROW_LOSS: tail_ref[4:5, :],
        }
        vbuf[...] = jnp.zeros_like(vbuf)
        for r, row in rows.items():
            for j in range(NDEV):
                vbuf[j, r:r + 1, :] = row[:, j * 128:(j + 1) * 128]

        def peer(mask):
            px = x ^ ((mask >> 2) & 1)
            py = y ^ ((mask >> 1) & 1)
            pc = c ^ (mask & 1)
            return (px, py, pc), 4 * px + 2 * py + pc

        def rg_rows(r):
            return pl.ds(pl.multiple_of(r * PR, PR), PR)

        first = []
        for mask in range(1, NDEV):
            dev, r = peer(mask)
            i = mask - 1
            cp = _remote(vbuf.at[r], vrecv.at[i], ssem.at[i], rsem.at[i], dev)
            cp.start()
            first.append(cp)
            cp = _remote(rg_ref.at[rg_rows(r)], rgrecv.at[i], ssem.at[7 + i], rsem.at[7 + i], dev)
            cp.start()
            first.append(cp)
        sv = vbuf[me]
        sr = rg_ref[rg_rows(me), :]
        for i in range(NDEV - 1):
            first[2 * i].wait_recv()
            first[2 * i + 1].wait_recv()
            sv = sv + vrecv[i]
            sr = sr + rgrecv[i]
        v_out[me] = sv
        rg_out[rg_rows(me), :] = sr
        second = []
        for mask in range(1, NDEV):
            dev, r = peer(mask)
            i = mask - 1
            cp = _remote(v_out.at[me], v_out.at[me], ssem.at[14 + i], rsem.at[14 + i], dev)
            cp.start()
            second.append(cp)
            cp = _remote(rg_out.at[rg_rows(me)], rg_out.at[rg_rows(me)], ssem.at[21 + i], rsem.at[21 + i], dev)
            cp.start()
            second.append(cp)
        for mask in range(1, NDEV):
            dev, r = peer(mask)
            i = mask - 1
            _remote(v_out.at[r], v_out.at[r], ssem.at[14 + i], rsem.at[14 + i], dev).wait_recv()
            _remote(rg_out.at[rg_rows(r)], rg_out.at[rg_rows(r)], ssem.at[21 + i], rsem.at[21 + i], dev).wait_recv()
        for cp in first + second:
            cp.wait_send()

    return pl.pallas_call(
        body, name="small_allreduce",
        out_shape=[jax.ShapeDtypeStruct((NDEV, VROWS, 128), F32), jax.ShapeDtypeStruct((DG, 128), F32)],
        in_specs=[VMEM_SPEC] * 5, out_specs=[VMEM_SPEC] * 2,
        scratch_shapes=[pltpu.VMEM((NDEV, VROWS, 128), F32), pltpu.VMEM((NDEV - 1, VROWS, 128), F32),
                        pltpu.VMEM((NDEV - 1, PR, 128), F32),
                        pltpu.SemaphoreType.DMA((28,)), pltpu.SemaphoreType.DMA((28,))],
    )(sm_in, sm_tail, sm_a, sm_b, rg_c)


def _adam_rows(w, g, m, v):
    m2 = ADAM_B1 * m + (1.0 - ADAM_B1) * g
    v2 = ADAM_B2 * v + (1.0 - ADAM_B2) * (g * g)
    m_hat = m2 / (1.0 - ADAM_B1 ** ADAM_STEP)
    v_hat = v2 / (1.0 - ADAM_B2 ** ADAM_STEP)
    delta = -ADAM_LR * (m_hat / (jnp.sqrt(v_hat) + ADAM_EPS) + ADAM_WD * w)
    return delta, m2, v2


def _adam_big(gs, ws, ms, vs):
    n = len(gs)
    steps = 8

    def body(*refs):
        ins, outs = refs[:4 * n], refs[4 * n:]
        for i in range(n):
            g, w, m, v = (r[...] for r in ins[4 * i:4 * i + 4])
            d, m2, v2 = _adam_rows(w, g, m, v)
            outs[3 * i][...] = d
            outs[3 * i + 1][...] = m2
            outs[3 * i + 2][...] = v2

    in_specs, out_specs, out_shape, args = [], [], [], []
    for g, w, m, v in zip(gs, ws, ms, vs):
        r, c = w.shape
        spec = lambda: pl.BlockSpec((r // steps, c), lambda i: (i, 0))
        in_specs += [spec() for _ in range(4)]
        out_specs += [spec() for _ in range(3)]
        out_shape += [jax.ShapeDtypeStruct((r, c), F32)] * 3
        args += [g, w, m, v]
    outs = pl.pallas_call(
        body, name="adam_big", grid=(steps,), in_specs=in_specs, out_specs=out_specs, out_shape=out_shape,
        compiler_params=_cparams(("parallel",), 32),
    )(*args)
    return [tuple(outs[3 * i:3 * i + 3]) for i in range(n)]


_VEC_PARAMS = [
    ("norm_mix_w", ROW_NORM_MIX, 0, D), ("final_norm_w", ROW_FINAL_NORM, 0, D),
    ("b_ple_gate", ROW_B_PG, 0, D), ("ple_norm_w", ROW_PLE_NORM, 0, D),
    ("conv_b", ROW_CB_BA, 0, DG), ("rg_ba", ROW_CB_BA, DG, DG),
    ("rg_bx", ROW_BX_LAM, 0, DG), ("rg_lambda", ROW_BX_LAM, DG, DG),
    ("hg_norm_w", ROW_HG_NW, 0, HD),
]
_SMALL_ORDER = [n for n, _, _, _ in _VEC_PARAMS] + ["hg_lb", "conv_w", "rg_wa", "rg_wx"]


def _adam_small(vred, rgred, ws, ms, vs):
    names = _SMALL_ORDER
    n = len(names)

    def body(vred_ref, rg_ref, *refs):
        w_refs = dict(zip(names, refs[0:n]))
        m_refs = dict(zip(names, refs[n:2 * n]))
        v_refs = dict(zip(names, refs[2 * n:3 * n]))
        outs = refs[3 * n:]
        o_refs = {nm: outs[4 * i:4 * i + 4] for i, nm in enumerate(names)}
        kme = 2 * lax.axis_index("x") + lax.axis_index("y")

        def update(nm, g, idx):
            d, m2, v2 = _adam_rows(w_refs[nm][idx], g, m_refs[nm][idx], v_refs[nm][idx])
            og, od, om, ov = o_refs[nm]
            og[idx] = g
            od[idx] = d
            om[idx] = m2
            ov[idx] = v2

        def packed(row, lane0, width):
            return jnp.concatenate([vred_ref[j, row:row + 1, :] for j in range(lane0 // 128, (lane0 + width) // 128)],
                                   axis=1)

        everything = (slice(None), slice(None))
        for nm, row, lane0, width in _VEC_PARAMS:
            update(nm, packed(row, lane0, width), everything)
        for r in range(2):
            update("hg_lb", packed(ROW_HG_LB, r * DG, DG), (slice(r, r + 1), slice(None)))
        for j in range(4):
            g = vred_ref[(j % 2) * 4 + kme, ROW_CW01 + j // 2:ROW_CW01 + j // 2 + 1, :]
            update("conv_w", g, (slice(j, j + 1), slice(None)))
        for r0 in range(0, DG, 128):
            rs = (slice(r0, r0 + 128), slice(None))
            both = rg_ref[r0:r0 + 128, :]
            update("rg_wa", both[:, 0:RGB], rs)
            update("rg_wx", pltpu.roll(both, RGB, 1)[:, 0:RGB], rs)

    args = [vred, rgred] + [d[nm] for d in (ws, ms, vs) for nm in names]
    out_shape = []
    for nm in names:
        out_shape += [jax.ShapeDtypeStruct(ws[nm].shape, F32)] * 4
    whole = lambda s: pl.BlockSpec(s.shape, lambda i, nd=len(s.shape): (0,) * nd)
    outs = pl.pallas_call(
        body, name="adam_small", out_shape=out_shape, grid=(1,),
        in_specs=[whole(a) for a in args], out_specs=[whole(s) for s in out_shape],
    )(*args)
    return {nm: tuple(outs[4 * i:4 * i + 4]) for i, nm in enumerate(names)}


def _fwd_inproj(x, nw, w_in_b, dep):
    tm = 1024

    def body(x_ref, nw_ref, w_ref, dep_ref, pf_ref, pb_ref, u_ref):
        xv = x_ref[...]
        s = lax.rsqrt(jnp.mean(xv * xv, axis=-1, keepdims=True) + EPS)
        u = (xv * s * nw_ref[...]).astype(BF16)
        u_ref[...] = u
        r = [jnp.dot(u, w_ref[k], preferred_element_type=F32) for k in range(NSHARD)]
        h = DG // 2
        pf_ref[:, 0:DG] = r[0][:, 0:DG]
        pb_ref[:, 0:h] = r[0][:, DG:SHW].astype(BF16)
        pb_ref[:, h:DG] = r[1][:, 0:h].astype(BF16)
        pb_ref[:, DG:2 * DG] = r[1][:, h:SHW].astype(BF16)
        pf_ref[:, DG:2 * DG] = r[2][:, 0:DG]
        pb_ref[:, 2 * DG:2 * DG + h] = r[2][:, DG:SHW].astype(BF16)
        pb_ref[:, 2 * DG + h:3 * DG] = r[3][:, 0:h].astype(BF16)
        pb_ref[:, 3 * DG:4 * DG] = r[3][:, h:SHW].astype(BF16)

    return pl.pallas_call(
        body, name="fwd_inproj", grid=(T // tm,),
        in_specs=[pl.BlockSpec((tm, D), lambda i: (i, 0)),
                  pl.BlockSpec((1, D), lambda i: (0, 0)),
                  pl.BlockSpec((NSHARD, D, SHW), lambda i: (0, 0, 0), pipeline_mode=pl.Buffered(1)), ANY_SPEC],
        out_specs=[pl.BlockSpec((tm, 2 * DG), lambda i: (i, 0)),
                   pl.BlockSpec((tm, 4 * DG), lambda i: (i, 0)),
                   pl.BlockSpec((tm, D), lambda i: (i, 0))],
        out_shape=[jax.ShapeDtypeStruct((T, 2 * DG), F32), jax.ShapeDtypeStruct((T, 4 * DG), BF16),
                   jax.ShapeDtypeStruct((T, D), BF16)],
        compiler_params=_cparams(("parallel",), 48),
    )(x, nw, w_in_b, dep)


def _conv_rows(cw_ref):
    return [jnp.concatenate([cw_ref[k, j:j + 1, :] for k in range(NSHARD)], axis=1) for j in range(4)]


def _rg_conv(xa, prev8, cw, cb, rows):
    taps = [_shift_down(xa, prev8, 3, rows), _shift_down(xa, prev8, 2, rows),
            _shift_down(xa, prev8, 1, rows), xa]
    xc = cb
    for j in range(4):
        xc = xc + taps[j] * cw[j]
    return xc, taps


def _block_mask():
    r = lax.broadcasted_iota(I32, (DG, DG), 0)
    c = lax.broadcasted_iota(I32, (DG, DG), 1)
    return (r >> 6) == (c >> 6)


def _dense_from_blocks(wc):
    j = lax.broadcasted_iota(I32, (RGB, DG), 0)
    c = lax.broadcasted_iota(I32, (RGB, DG), 1)
    spread = _mm_exact(wc, ((c & (RGB - 1)) == j).astype(F32))
    return jnp.where(_block_mask(), spread, 0.0)


def _blocks_from_dense(da, dx):
    c = lax.broadcasted_iota(I32, (DG, 128), 0)
    j = lax.broadcasted_iota(I32, (DG, 128), 1)
    hit = (c & (RGB - 1)) == (j & (RGB - 1))
    mask = _block_mask()
    return (_mm_exact(jnp.where(mask, da, 0.0), (hit & (j < RGB)).astype(F32))
            + _mm_exact(jnp.where(mask, dx, 0.0), (hit & (j >= RGB)).astype(F32)))


def _rg_gates(xc, wa, ba, wx, bx, sp, first_row):
    r = _sigmoid(_mm(xc, wa) + ba)
    i = _sigmoid(_mm(xc, wx) + bx)
    log_a = (-RG_C) * r * sp
    a = jnp.exp(log_a)
    a2 = a * a
    one_m_a2 = -jnp.tanh(log_a) * (a2 + 1.0)
    mult = jnp.where(first_row, 1.0, jnp.sqrt(one_m_a2))
    return r, i, a, a2, mult


def _softplus(z):
    return jnp.maximum(z, 0.0) + jnp.log1p(jnp.exp(-jnp.abs(z)))


def _fwd_rglru(pf, pb, cw3, conv_b, wa_c, ba, wx_c, bx, lam):
    tm = 512
    ng = tm // 8

    def body(xa_ref, ga_ref, cw_ref, cb_ref, wa_ref, ba_ref, wx_ref, bx_ref, lam_ref,
             h_ref, ya_ref, a_s, u_s, tail_s, hc_s, wa_s, wx_s):
        i = pl.program_id(0)

        @pl.when(i == 0)
        def _():
            tail_s[...] = jnp.zeros_like(tail_s)
            hc_s[...] = jnp.zeros_like(hc_s)
            wa_s[...] = _dense_from_blocks(wa_ref[...]).astype(BF16)
            wx_s[...] = _dense_from_blocks(wx_ref[...]).astype(BF16)

        rows = lax.broadcasted_iota(I32, (tm, DG), 0)
        xa = xa_ref[...]
        xc, _ = _rg_conv(xa, tail_s[...], _conv_rows(cw_ref), cb_ref[...], rows)
        tail_s[...] = xa[tm - 8:tm, :]
        sp = _softplus(-lam_ref[...])
        rp = _mm(xc, wa_s[...]) + ba_ref[...]
        ip = _mm(xc, wx_s[...]) + bx_ref[...]
        rb = 64
        rows_b = lax.broadcasted_iota(I32, (rb, DG), 0)
        rows8 = rows_b & 7
        carry = hc_s[0:1, :]
        for b0 in range(0, tm, rb):
            sl = slice(b0, b0 + rb)
            r = _sigmoid(rp[sl])
            ig = _sigmoid(ip[sl])
            log_a = (-RG_C) * r * sp
            av = jnp.exp(log_a)
            mult = jnp.sqrt(-jnp.tanh(log_a) * (av * av + 1.0))
            if b0 == 0:
                mult = jnp.where((rows_b + i * tm) == 0, 1.0, mult)
            uv = mult * (ig * xc[sl])
            for d in (1, 2, 4):
                keep = rows8 >= d
                uv = uv + av * jnp.where(keep, _roll_in_groups(uv, d), 0.0)
                av = av * jnp.where(keep, _roll_in_groups(av, d), 1.0)
            ga = ga_ref[sl, :].astype(F32)
            gate = ga * _sigmoid(ga)
            hs = []
            for g in range(rb // 8):
                gs = slice(g * 8, (g + 1) * 8)
                hv = uv[gs] + av[gs] * carry
                carry = hv[7:8, :]
                hs.append(hv)
            hb = jnp.concatenate(hs, axis=0)
            h_ref[sl, :] = hb
            ya_ref[sl, :] = (hb * gate).astype(BF16)
        hc_s[0:1, :] = carry

    vec = lambda: pl.BlockSpec((1, DG), lambda i: (0, 0))
    blocks = lambda: pl.BlockSpec((DG, RGB), lambda i: (0, 0))
    return pl.pallas_call(
        body, name="fwd_rglru", grid=(T // tm,),
        in_specs=[pl.BlockSpec((tm, DG), lambda i: (i, 0)),
                  pl.BlockSpec((tm, DG), lambda i: (i, 0)),
                  pl.BlockSpec((NSHARD, 4, 128), lambda i: (0, 0, 0)), vec(),
                  blocks(), vec(), blocks(), vec(), vec()],
        out_specs=[pl.BlockSpec((tm, DG), lambda i: (i, 0)),
                   pl.BlockSpec((tm, DG), lambda i: (i, 0))],
        out_shape=[jax.ShapeDtypeStruct((T, DG), F32), jax.ShapeDtypeStruct((T, DG), BF16)],
        scratch_shapes=[pltpu.VMEM((tm, DG), F32), pltpu.VMEM((tm, DG), F32),
                        pltpu.VMEM((8, DG), F32), pltpu.VMEM((8, DG), F32),
                        pltpu.VMEM((DG, DG), BF16), pltpu.VMEM((DG, DG), BF16)],
        compiler_params=_cparams(("arbitrary",), 48),
    )(pf, pb, cw3, conv_b, wa_c, ba, wx_c, bx, lam)


def _hg_lower_bound(lb_ref):
    return _sig_pair(lb_ref[0:1, :] - lb_ref[1:2, :])


def _hg_gates(fz, lb, one_m_lb):
    sg, sn = _sig_pair(fz)
    f = lb + one_m_lb * sg
    return sg, sn, f, jnp.log(f), one_m_lb * sn


def _tri(lower):
    r = lax.broadcasted_iota(I32, (CH, CH), 0)
    c = lax.broadcasted_iota(I32, (CH, CH), 1)
    return (r >= c) if lower else (r <= c)


def _chunk_cumsum(v, rows64):
    for d in (1, 2, 4, 8, 16, 32):
        v = v + jnp.where(rows64 >= d, pltpu.roll(v, d, 0), 0.0)
    return v


def _chunk_rev_cumsum(v, rows64):
    n = v.shape[0]
    for d in (1, 2, 4, 8, 16, 32):
        v = v + jnp.where(rows64 < CH - d, pltpu.roll(v, n - d, 0), 0.0)
    return v


def _hg_recompute(q_ref, f_ref, lb, one_m_lb, rows64, eb_s, enb_s, ekd_s, qe_s, ke_s, kd_s, dec_s):
    nc = q_ref.shape[0] // CH
    sg, sn, f, logf, k = _hg_gates(f_ref[...], lb, one_m_lb)
    q = q_ref[...].astype(F32)
    sq = _sigmoid(q)
    qs = q * sq * (HD ** -0.5)
    b = _chunk_cumsum(logf, rows64)
    for c in range(nc):
        rs = slice(c * CH, (c + 1) * CH)
        b_c = b[rs]
        bl = b_c[CH - 1:CH, :]
        eb, enb, ekd = jnp.exp(b_c), jnp.exp(-b_c), jnp.exp(bl - b_c)
        if eb_s is not None:
            eb_s[rs, :] = eb
            enb_s[rs, :] = enb
            ekd_s[rs, :] = ekd
        qe_s[rs, :] = (qs[rs] * eb).astype(BF16)
        ke_s[rs, :] = (k[rs] * enb).astype(BF16)
        kd_s[rs, :] = (k[rs] * ekd).astype(BF16)
        dec_s[c:c + 1, :] = jnp.exp(bl)
    return sg, sn, f, k, q, sq, qs


def _fwd_hgrn2(pf, pb, hg_lb, hg_nw):
    tm = 512
    nc = tm // CH

    def body(q_ref, f_ref, v_ref, g_ref, lb_ref, nw_ref, yb_ref, o_ref, sp_ref,
             st_s, qe_s, ke_s, kd_s, vb_s, dec_s, p_s, ds_s):
        i = pl.program_id(0)

        @pl.when(i == 0)
        def _():
            st_s[...] = jnp.zeros_like(st_s)

        lb, one_m_lb = _hg_lower_bound(lb_ref)
        rows64 = lax.broadcasted_iota(I32, (tm, DG), 0) & (CH - 1)
        _hg_recompute(q_ref, f_ref, lb, one_m_lb, rows64, None, None, None, qe_s, ke_s, kd_s, dec_s)
        vb_s[...] = v_ref[...]
        mask = _tri(True)
        items = [(c, hd, slice(c * CH, (c + 1) * CH), slice(hd * HD, (hd + 1) * HD))
                 for c in range(nc) for hd in range(NH)]
        for c, hd, rs, cols in items:
            p_s[c * NH + hd] = jnp.where(mask, _mm_nt(qe_s[rs, cols], ke_s[rs, cols]), 0.0).astype(BF16)
            ds_s[c * NH + hd] = _mm_tn(vb_s[rs, cols], kd_s[rs, cols])
        for c, hd, rs, cols in items:
            st = st_s[hd]
            sp_ref[hd, c] = st
            st_s[hd] = st * dec_s[c:c + 1, cols] + ds_s[c * NH + hd]
        for c, hd, rs, cols in items:
            o_ref[rs, cols] = _mm(p_s[c * NH + hd], vb_s[rs, cols]) + _mm_nt(qe_s[rs, cols], sp_ref[hd, c])
        nw = nw_ref[...]
        for hd in range(NH):
            cols = slice(hd * HD, (hd + 1) * HD)
            o = o_ref[:, cols]
            so = lax.rsqrt(jnp.mean(o * o, axis=-1, keepdims=True) + EPS)
            g = g_ref[:, cols].astype(F32)
            sg = _sigmoid(g)
            yb_ref[:, cols] = (o * so * nw * (g * sg)).astype(BF16)

    col = lambda j: pl.BlockSpec((tm, DG), lambda i: (i, j))
    return pl.pallas_call(
        body, name="fwd_hgrn2", grid=(T // tm,),
        in_specs=[col(1), col(1), col(2), col(3),
                  pl.BlockSpec((2, DG), lambda i: (0, 0)),
                  pl.BlockSpec((1, HD), lambda i: (0, 0))],
        out_specs=[pl.BlockSpec((tm, DG), lambda i: (i, 0)),
                   pl.BlockSpec((tm, DG), lambda i: (i, 0)),
                   pl.BlockSpec((NH, nc, HD, HD), lambda i: (0, i, 0, 0))],
        out_shape=[jax.ShapeDtypeStruct((T, DG), BF16), jax.ShapeDtypeStruct((T, DG), F32),
                   jax.ShapeDtypeStruct((NH, NCHUNK, HD, HD), F32)],
        scratch_shapes=[pltpu.VMEM((NH, HD, HD), F32),
                        pltpu.VMEM((tm, DG), BF16), pltpu.VMEM((tm, DG), BF16), pltpu.VMEM((tm, DG), BF16),
                        pltpu.VMEM((tm, DG), BF16), pltpu.VMEM((nc, DG), F32),
                        pltpu.VMEM((nc * NH, CH, CH), BF16), pltpu.VMEM((nc * NH, HD, HD), F32)],
        compiler_params=_cparams(("arbitrary",), 48),
    )(pb, pf, pb, pb, hg_lb, hg_nw)


def _tail_fwd_bwd(x, p, tgt, ya, yb, w_out_b, w_pg_b, w_pp_b, ple_nw, b_pg, fnw):
    tm = 512
    nt = T // tm
    QR = D // NSHARD

    def body(x_ref, p_ref, t_ref, ya_ref, yb_ref, wo_ref, wg_ref, wp_ref, pw_ref, b_ref, fw_ref,
             dh1_ref, dyab_ref, dwo_ref, dwg_ref, dwp_ref, sm_ref, dwo_s, dwg_s, dwp_s):
        i = pl.program_id(0)

        @pl.when(i == 0)
        def _():
            dwo_s[...] = jnp.zeros_like(dwo_s)
            dwg_s[...] = jnp.zeros_like(dwg_s)
            dwp_s[...] = jnp.zeros_like(dwp_s)
            sm_ref[...] = jnp.zeros_like(sm_ref)

        ya = ya_ref[...]
        yb = yb_ref[...]
        pv = p_ref[...].astype(BF16)
        pw = pw_ref[...]
        fw = fw_ref[...]
        h1 = x_ref[...] + _mm(ya, wo_ref[0:DG, :]) + _mm(yb, wo_ref[DG:D, :])
        s2 = lax.rsqrt(jnp.mean(h1 * h1, axis=-1, keepdims=True) + EPS)
        n2h = h1 * s2
        n2 = (n2h * pw).astype(BF16)
        z = _mm(n2, wg_ref[...]) + b_ref[...]
        gate = _sigmoid(z)
        pp = jnp.concatenate([_mm(pv, wp_ref[k]) for k in range(NSHARD)], axis=1)
        h2 = h1 + gate * pp
        s3 = lax.rsqrt(jnp.mean(h2 * h2, axis=-1, keepdims=True) + EPS)
        hn = h2 * s3
        err = hn * fw - t_ref[...]
        sm_ref[0:1, :] += _rowsum(err * err)
        dy = err * (1.0 / D)
        sm_ref[1:2, :] += _rowsum(dy * hn)
        g3 = dy * fw
        dh2 = s3 * (g3 - hn * jnp.mean(g3 * hn, axis=-1, keepdims=True))
        dpp = (dh2 * gate).astype(BF16)
        dz = dh2 * pp * gate * (1.0 - gate)
        sm_ref[2:3, :] += _rowsum(dz)
        dzb = dz.astype(BF16)
        dwg_s[...] += _mm_tn(n2, dzb)
        dn2 = _mm_nt(dzb, wg_ref[...])
        for k in range(NSHARD):
            dwp_s[k] += _mm_tn(pv, dpp[:, k * PLE:(k + 1) * PLE])
        sm_ref[3:4, :] += _rowsum(dn2 * n2h)
        g2 = dn2 * pw
        dh1 = dh2 + s2 * (g2 - n2h * jnp.mean(g2 * n2h, axis=-1, keepdims=True))
        dh1_ref[...] = dh1
        dh1b = dh1.astype(BF16)
        dyab_ref[...] = _mm_nt(dh1b, wo_ref[...])
        dwo_s[0:DG, :] += _mm_tn(ya, dh1b)
        dwo_s[DG:D, :] += _mm_tn(yb, dh1b)

        @pl.when(i == nt - 1)
        def _():
            total = jnp.sum(sm_ref[0:1, :], axis=-1, keepdims=True) * (0.5 / D)
            sm_ref[4:5, :] = jnp.broadcast_to(total, (1, D))
            for k in range(NSHARD):
                dwo_ref[k] = dwo_s[k * QR:(k + 1) * QR, :].astype(BF16)
                dwg_ref[k] = dwg_s[k * QR:(k + 1) * QR, :].astype(BF16)
                dwp_ref[k] = dwp_s[k].astype(BF16)

    row = lambda w: pl.BlockSpec((tm, w), lambda i: (i, 0))
    const2 = lambda s: pl.BlockSpec(s, lambda i: (0, 0), pipeline_mode=pl.Buffered(1))
    const3 = lambda s: pl.BlockSpec(s, lambda i: (0, 0, 0), pipeline_mode=pl.Buffered(1))
    return pl.pallas_call(
        body, name="tail_fwd_bwd", grid=(nt,),
        in_specs=[row(D), row(PLE), row(D), row(DG), row(DG),
                  const2((D, D)), const2((D, D)), const3((NSHARD, PLE, PLE)),
                  const2((1, D)), const2((1, D)), const2((1, D))],
        out_specs=[row(D), row(D), const3((NSHARD, QR, D)), const3((NSHARD, QR, D)),
                   const3((NSHARD, PLE, PLE)), const2((8, D))],
        out_shape=[jax.ShapeDtypeStruct((T, D), F32), jax.ShapeDtypeStruct((T, D), F32),
                   jax.ShapeDtypeStruct((NSHARD, QR, D), BF16), jax.ShapeDtypeStruct((NSHARD, QR, D), BF16),
                   jax.ShapeDtypeStruct((NSHARD, PLE, PLE), BF16), jax.ShapeDtypeStruct((8, D), F32)],
        scratch_shapes=[pltpu.VMEM((D, D), F32), pltpu.VMEM((D, D), F32), pltpu.VMEM((NSHARD, PLE, PLE), F32)],
        compiler_params=_cparams(("arbitrary",), 62),
    )(x, p, tgt, ya, yb, w_out_b, w_pg_b, w_pp_b, ple_nw, b_pg, fnw)


def _bwd_rglru(pf, pb, h, dyab, cw3, conv_b, wa_c, ba, wx_c, bx, lam, dep):
    tm = 512
    nt = T // tm
    ng = tm // 8

    def body(xa_ref, ga_ref, xp_ref, h_ref, hp_ref, dya_ref, cw_ref, cb_ref, wa_ref, ba_ref, wx_ref, bx_ref,
             lam_ref, dep_ref, da_ref, rg_ref, sm_ref, a_s, g_s, cg_s, nxt_s, wa_s, wx_s, dwa_s, dwx_s):
        i = pl.program_id(0)
        tile = nt - 1 - i

        @pl.when(i == 0)
        def _():
            dwa_s[...] = jnp.zeros_like(dwa_s)
            dwx_s[...] = jnp.zeros_like(dwx_s)
            sm_ref[...] = jnp.zeros_like(sm_ref)
            cg_s[...] = jnp.zeros_like(cg_s)
            nxt_s[...] = jnp.zeros_like(nxt_s)
            wa_s[...] = _dense_from_blocks(wa_ref[...]).astype(BF16)
            wx_s[...] = _dense_from_blocks(wx_ref[...]).astype(BF16)

        rows = lax.broadcasted_iota(I32, (tm, DG), 0)
        has_prev = tile > 0
        xa = xa_ref[...]
        xprev = jnp.where(has_prev, xp_ref[...], 0.0)
        cw = _conv_rows(cw_ref)
        xc, taps = _rg_conv(xa, xprev, cw, cb_ref[...], rows)
        lam_v = lam_ref[...]
        sp = _softplus(-lam_v)
        first_row = (rows + tile * tm) == 0
        r, ig, a, a2, mult = _rg_gates(xc, wa_s[...], ba_ref[...], wx_s[...], bx_ref[...], sp, first_row)
        hv = h_ref[...]
        hprev = jnp.where(has_prev, hp_ref[...], 0.0)
        h_m1 = _shift_down(hv, hprev, 1, rows)
        ga = ga_ref[...].astype(F32)
        sg = _sigmoid(ga)
        dya = dya_ref[...]
        dga = dya * hv * (sg * (1.0 + ga * (1.0 - sg)))

        av = jnp.where(rows == tm - 1, 1.0, pltpu.roll(a, tm - 1, 0))
        gv = dya * (ga * sg)
        rows8 = rows & 7
        for d in (1, 2, 4):
            keep = rows8 < 8 - d
            gv = gv + av * jnp.where(keep, _roll_in_groups(gv, 8 - d), 0.0)
            av = av * jnp.where(keep, _roll_in_groups(av, 8 - d), 1.0)
        a_s[...] = av
        g_s[...] = gv
        carry = cg_s[0:1, :]
        for g in range(ng - 1, -1, -1):
            sl = slice(g * 8, (g + 1) * 8)
            ab, gb = a_s[sl, :], g_s[sl, :]
            g_s[sl, :] = gb + ab * carry
            carry = gb[0:1, :] + ab[0:1, :] * carry
        cg_s[0:1, :] = a[0:1, :] * carry

        gt = g_s[...]
        da = gt * h_m1
        ixc = ig * xc
        di = gt * mult * xc
        dxc = gt * mult * ig
        dlog_a = da * a + jnp.where(first_row, 0.0, gt * ixc * (-a2 / mult))
        sm_ref[3:4, :] += _rowsum(dlog_a * ((-RG_C) * r))
        dpr_f = dlog_a * ((-RG_C) * sp) * r * (1.0 - r)
        dpi_f = di * ig * (1.0 - ig)
        sm_ref[1:2, :] += _rowsum(dpr_f)
        sm_ref[2:3, :] += _rowsum(dpi_f)
        dpr = dpr_f.astype(BF16)
        dpi = dpi_f.astype(BF16)
        xcb = xc.astype(BF16)
        dwa_s[...] += _mm_tn(xcb, dpr)
        dwx_s[...] += _mm_tn(xcb, dpi)
        dxc = dxc + _mm_nt(dpr, wa_s[...]) + _mm_nt(dpi, wx_s[...])
        sm_ref[0:1, :] += _rowsum(dxc)
        for j in range(4):
            sm_ref[4 + j:5 + j, :] += _rowsum(dxc * taps[j])
        nxt = nxt_s[...]
        dxa = (dxc * cw[3] + _shift_up(dxc, nxt, 1, rows) * cw[2]
               + _shift_up(dxc, nxt, 2, rows) * cw[1] + _shift_up(dxc, nxt, 3, rows) * cw[0])
        nxt_s[...] = dxc[0:8, :]
        da_ref[:, 0:DG] = dxa.astype(BF16)
        da_ref[:, DG:D] = dga.astype(BF16)

        @pl.when(i == nt - 1)
        def _():
            _, s_neg = _sig_pair(lam_v)
            sm_ref[3:4, :] = sm_ref[3:4, :] * (-s_neg)
            rg_ref[...] = _blocks_from_dense(dwa_s[...], dwx_s[...])

    vec = lambda: pl.BlockSpec((1, DG), lambda i: (0, 0))
    blocks = lambda: pl.BlockSpec((DG, RGB), lambda i: (0, 0))
    prev8 = lambda: pl.BlockSpec((8, DG), lambda i: (jnp.maximum((nt - 1 - i) * (tm // 8) - 1, 0), 0))
    return pl.pallas_call(
        body, name="bwd_rglru", grid=(nt,),
        in_specs=[pl.BlockSpec((tm, DG), lambda i: (nt - 1 - i, 0)),
                  pl.BlockSpec((tm, DG), lambda i: (nt - 1 - i, 0)),
                  prev8(),
                  pl.BlockSpec((tm, DG), lambda i: (nt - 1 - i, 0)),
                  prev8(),
                  pl.BlockSpec((tm, DG), lambda i: (nt - 1 - i, 0)),
                  pl.BlockSpec((NSHARD, 4, 128), lambda i: (0, 0, 0)), vec(),
                  blocks(), vec(), blocks(), vec(), vec(), ANY_SPEC],
        out_specs=[pl.BlockSpec((tm, D), lambda i: (nt - 1 - i, 0)),
                   pl.BlockSpec((DG, 128), lambda i: (0, 0)),
                   pl.BlockSpec((8, DG), lambda i: (0, 0))],
        out_shape=[jax.ShapeDtypeStruct((T, D), BF16), jax.ShapeDtypeStruct((DG, 128), F32),
                   jax.ShapeDtypeStruct((8, DG), F32)],
        scratch_shapes=[pltpu.VMEM((tm, DG), F32), pltpu.VMEM((tm, DG), F32),
                        pltpu.VMEM((8, DG), F32), pltpu.VMEM((8, DG), F32),
                        pltpu.VMEM((DG, DG), BF16), pltpu.VMEM((DG, DG), BF16),
                        pltpu.VMEM((DG, DG), F32), pltpu.VMEM((DG, DG), F32)],
        compiler_params=_cparams(("arbitrary",), 56),
    )(pf, pb, pf, h, h, dyab, cw3, conv_b, wa_c, ba, wx_c, bx, lam, dep)


def _bwd_hgrn2(pf, pb, o, s_prev, dyab, hg_lb, hg_nw, dep):
    tm = 512
    nt = T // tm
    nc = tm // CH

    def body(q_ref, f_ref, v_ref, g_ref, o_ref, sp_ref, dy_ref, lb_ref, nw_ref, dep_ref, db_ref, sm_ref,
             dst_s, eb_s, enb_s, ekd_s, qe_s, ke_s, kd_s, vb_s, do_s, dec_s, ddec_s, p_s, dp_s,
             g_s, dsta_s, dva_s, dqe_s, dke_s, dkd_s, dlf_s):
        i = pl.program_id(0)

        @pl.when(i == 0)
        def _():
            sm_ref[...] = jnp.zeros_like(sm_ref)
            dst_s[...] = jnp.zeros_like(dst_s)

        lb, one_m_lb = _hg_lower_bound(lb_ref)
        rows64 = lax.broadcasted_iota(I32, (tm, DG), 0) & (CH - 1)
        sg, sn, f, k, q, sq, qs = _hg_recompute(
            q_ref, f_ref, lb, one_m_lb, rows64, eb_s, enb_s, ekd_s, qe_s, ke_s, kd_s, dec_s)
        vb_s[...] = v_ref[...]

        nw = nw_ref[...]
        for hd in range(NH):
            cols = slice(hd * HD, (hd + 1) * HD)
            g = g_ref[:, cols].astype(F32)
            sgg = _sigmoid(g)
            o = o_ref[:, cols]
            so = lax.rsqrt(jnp.mean(o * o, axis=-1, keepdims=True) + EPS)
            oh = o * so
            dyb = dy_ref[:, cols]
            db_ref[:, 3 * DG + hd * HD:3 * DG + (hd + 1) * HD] = (
                dyb * (oh * nw) * (sgg * (1.0 + g * (1.0 - sgg)))).astype(BF16)
            don = dyb * (g * sgg)
            sm_ref[1:2, 0:HD] += _rowsum(don * oh)
            gw = don * nw
            do_s[:, cols] = (so * (gw - oh * jnp.mean(gw * oh, axis=-1, keepdims=True))).astype(BF16)

        mask = _tri(True)
        items = [(c, hd, slice(c * CH, (c + 1) * CH), slice(hd * HD, (hd + 1) * HD))
                 for c in range(nc) for hd in range(NH)]
        for c, hd, rs, cols in items:
            p_s[c * NH + hd] = jnp.where(mask, _mm_nt(qe_s[rs, cols], ke_s[rs, cols]), 0.0).astype(BF16)
            dp_s[c * NH + hd] = jnp.where(mask, _mm_nt(do_s[rs, cols], vb_s[rs, cols]), 0.0).astype(BF16)
        for c, hd, rs, cols in items:
            n = c * NH + hd
            dva_s[rs, cols] = _mm_tn(p_s[n], do_s[rs, cols])
            dqe_s[rs, cols] = _mm(dp_s[n], ke_s[rs, cols])
            dke_s[rs, cols] = _mm_tn(dp_s[n], qe_s[rs, cols])
            g_s[n] = _mm_tn(do_s[rs, cols], qe_s[rs, cols])
        for c, hd, rs, cols in reversed(items):
            n = c * NH + hd
            dst = dst_s[hd]
            dsta_s[n] = dst
            dst_s[hd] = dst * dec_s[c:c + 1, cols] + g_s[n]
        for c, hd, rs, cols in items:
            n = c * NH + hd
            dst = dsta_s[n]
            st_prev = sp_ref[hd, c]
            dv = dva_s[rs, cols] + _mm_nt(kd_s[rs, cols], dst)
            db_ref[rs, 2 * DG + hd * HD:2 * DG + (hd + 1) * HD] = dv.astype(BF16)
            dqe_s[rs, cols] += _mm(do_s[rs, cols], st_prev)
            dkd_s[rs, cols] = _mm(vb_s[rs, cols], dst)
            ddec_s[c:c + 1, cols] = _rowsum(dst * st_prev)

        eb, enb, ekd = eb_s[...], enb_s[...], ekd_s[...]
        dqe, dke, dkd = dqe_s[...], dke_s[...], dkd_s[...]
        t_kd = dkd * (k * ekd)
        rc = _chunk_rev_cumsum(dqe * (qs * eb) - dke * (k * enb) - t_kd, rows64)
        for c in range(nc):
            rs = slice(c * CH, (c + 1) * CH)
            dbl = _rowsum(t_kd[rs]) + ddec_s[c:c + 1, :] * dec_s[c:c + 1, :]
            dlf_s[rs, :] = rc[rs] + dbl
        t = dlf_s[...] / f - (dke * enb + dkd * ekd)
        db_ref[:, DG:2 * DG] = (one_m_lb * sg * sn * t).astype(BF16)
        sm_ref[0:1, :] += _rowsum(sn * t)
        db_ref[:, 0:DG] = (dqe * eb * (sq * (1.0 + q * (1.0 - sq))) * (HD ** -0.5)).astype(BF16)

        @pl.when(i == nt - 1)
        def _():
            dsm = sm_ref[0:1, :] * (lb * one_m_lb)
            sm_ref[2:3, :] = dsm
            sm_ref[3:4, :] = -dsm

    col = lambda j: pl.BlockSpec((tm, DG), lambda i: (nt - 1 - i, j))
    big = lambda dt: pltpu.VMEM((tm, DG), dt)
    return pl.pallas_call(
        body, name="bwd_hgrn2", grid=(nt,),
        in_specs=[col(1), col(1), col(2), col(3),
                  pl.BlockSpec((tm, DG), lambda i: (nt - 1 - i, 0)),
                  pl.BlockSpec((NH, nc, HD, HD), lambda i: (0, nt - 1 - i, 0, 0)),
                  pl.BlockSpec((tm, DG), lambda i: (nt - 1 - i, 1)),
                  pl.BlockSpec((2, DG), lambda i: (0, 0)),
                  pl.BlockSpec((1, HD), lambda i: (0, 0)), ANY_SPEC],
        out_specs=[pl.BlockSpec((tm, 4 * DG), lambda i: (nt - 1 - i, 0)),
                   pl.BlockSpec((8, DG), lambda i: (0, 0))],
        out_shape=[jax.ShapeDtypeStruct((T, 4 * DG), BF16), jax.ShapeDtypeStruct((8, DG), F32)],
        scratch_shapes=[pltpu.VMEM((NH, HD, HD), F32),
                        big(F32), big(F32), big(F32),
                        big(BF16), big(BF16), big(BF16), big(BF16), big(BF16),
                        pltpu.VMEM((nc, DG), F32), pltpu.VMEM((nc, DG), F32),
                        pltpu.VMEM((nc * NH, CH, CH), BF16), pltpu.VMEM((nc * NH, CH, CH), BF16),
                        pltpu.VMEM((nc * NH, HD, HD), F32), pltpu.VMEM((nc * NH, HD, HD), F32),
                        big(F32), big(F32), big(F32), big(F32), big(F32)],
        compiler_params=_cparams(("arbitrary",), 56),
    )(pb, pf, pb, pb, o, s_prev, dyab, hg_lb, hg_nw, dep)


def _dproj_pieces(k, da_ref, db_ref):
    if k == 0:
        return [(da_ref[:, 0:SHW], 0)]
    if k == 1:
        return [(da_ref[:, SHW:D], 0), (db_ref[:, 0:DG], D - SHW)]
    if k == 2:
        return [(db_ref[:, DG:DG + SHW], 0)]
    return [(db_ref[:, DG + SHW:4 * DG], 0)]


def _bwd_inproj_dx(x, dh1, d_a, d_b, w_in_b, nw, dep):
    tm = 1024
    nt = T // tm

    def body(x_ref, dh1_ref, da_ref, db_ref, w_ref, nw_ref, dep_ref, dx_ref, sm_ref):
        i = pl.program_id(0)

        @pl.when(i == 0)
        def _():
            sm_ref[...] = jnp.zeros_like(sm_ref)

        du = None
        for k in range(NSHARD):
            for val, off in _dproj_pieces(k, da_ref, db_ref):
                t = _mm_nt(val, w_ref[k, :, off:off + val.shape[1]])
                du = t if du is None else du + t
        xv = x_ref[...]
        s = lax.rsqrt(jnp.mean(xv * xv, axis=-1, keepdims=True) + EPS)
        xh = xv * s
        sm_ref[0:1, :] += _rowsum(du * xh)
        g = du * nw_ref[...]
        dx_ref[...] = dh1_ref[...] + s * (g - xh * jnp.mean(g * xh, axis=-1, keepdims=True))

    row = lambda w: pl.BlockSpec((tm, w), lambda i: (i, 0))
    return pl.pallas_call(
        body, name="bwd_inproj_dx", grid=(nt,),
        in_specs=[row(D), row(D), row(D), row(4 * DG),
                  pl.BlockSpec((NSHARD, D, SHW), lambda i: (0, 0, 0), pipeline_mode=pl.Buffered(1)),
                  pl.BlockSpec((1, D), lambda i: (0, 0)), ANY_SPEC],
        out_specs=[row(D), pl.BlockSpec((8, D), lambda i: (0, 0))],
        out_shape=[jax.ShapeDtypeStruct((T, D), F32), jax.ShapeDtypeStruct((8, D), F32)],
        compiler_params=_cparams(("arbitrary",), 56),
    )(x, dh1, d_a, d_b, w_in_b, nw, dep)


def _bwd_inproj_dw(name, ks, u_b, d_a, d_b):
    tm = 1024
    nt = T // tm
    nk = len(ks)
    H = D // 2
    step = 128
    need_a = 0 in ks or 1 in ks

    def body(*refs):
        if need_a:
            u_ref, da_ref, db_ref, dw_ref, acc, send_s, recv_s, ssem, rsem = refs
        else:
            u_ref, db_ref, dw_ref, acc, send_s, recv_s, ssem, rsem = refs
            da_ref = None
        i = pl.program_id(0)

        @pl.when(i == 0)
        def _():
            acc[...] = jnp.zeros_like(acc)

        u = u_ref[...]
        for j, k in enumerate(ks):
            for val, off in _dproj_pieces(k, da_ref, db_ref):
                acc[j, :, off:off + val.shape[1]] += _mm_tn(u, val)

        @pl.when(i == nt - 1)
        def _():
            x, y, c = lax.axis_index("x"), lax.axis_index("y"), lax.axis_index("c")
            sibling = (x, y, 1 - c)
            mine0 = pl.multiple_of(c * H, step)
            other0 = pl.multiple_of((1 - c) * H, step)
            copies = []
            for j in range(nk):
                for r0 in range(0, H, step):
                    send_s[j, r0:r0 + step, :] = acc[j, pl.ds(other0 + r0, step), :].astype(BF16)
                cp = _remote(send_s.at[j], recv_s.at[j], ssem.at[j], rsem.at[j], sibling)
                cp.start()
                copies.append(cp)
            for j in range(nk):
                copies[j].wait_recv()
                for r0 in range(0, H, step):
                    s = acc[j, pl.ds(mine0 + r0, step), :] + recv_s[j, r0:r0 + step, :].astype(F32)
                    dw_ref[j, r0:r0 + step, :] = s.astype(BF16)
            for cp in copies:
                cp.wait_send()

    row = lambda w: pl.BlockSpec((tm, w), lambda i: (i, 0))
    ins = [u_b] + ([d_a] if need_a else []) + [d_b]
    in_specs = [row(D)] + ([row(D)] if need_a else []) + [row(4 * DG)]
    return pl.pallas_call(
        body, name=name, grid=(nt,), in_specs=in_specs,
        out_specs=pl.BlockSpec((nk, H, SHW), lambda i: (0, 0, 0)),
        out_shape=jax.ShapeDtypeStruct((nk, H, SHW), BF16),
        scratch_shapes=[pltpu.VMEM((nk, D, SHW), F32), pltpu.VMEM((nk, H, SHW), BF16),
                        pltpu.VMEM((nk, H, SHW), BF16),
                        pltpu.SemaphoreType.DMA((nk,)), pltpu.SemaphoreType.DMA((nk,))],
        compiler_params=_cparams(("arbitrary",), 48),
    )(*ins)


_OUT_ORDER = ["norm_mix_w", "w_in", "conv_w", "conv_b", "rg_wa", "rg_ba", "rg_wx", "rg_bx", "rg_lambda", "hg_lb",
              "hg_norm_w", "w_out", "ple_norm_w", "w_ple_gate", "b_ple_gate", "w_ple_proj", "final_norm_w"]
_BIG = ["w_in", "w_out", "w_ple_gate", "w_ple_proj"]


def _small_view(name, a):
    if name in ("rg_wa", "rg_wx"):
        return a.reshape(DG, RGB)
    if name == "conv_w":
        return a.reshape(4, 128)
    if name == "final_norm_w":
        return a.reshape(1, D)
    return a


def _landing(rows, cols):
    return lax.empty((NDEV, rows, cols), BF16)


def kernel(x, p, norm_mix_w, w_in, conv_w, conv_b, rg_wa, rg_ba, rg_wx, rg_bx, rg_lambda, hg_lb, hg_norm_w, w_out, ple_norm_w, w_ple_gate, b_ple_gate, w_ple_proj, final_norm_w, loss_target, m_norm_mix_w, m_w_in, m_conv_w, m_conv_b, m_rg_wa, m_rg_ba, m_rg_wx, m_rg_bx, m_rg_lambda, m_hg_lb, m_hg_norm_w, m_w_out, m_ple_norm_w, m_w_ple_gate, m_b_ple_gate, m_w_ple_proj, m_final_norm_w, v_norm_mix_w, v_w_in, v_conv_w, v_conv_b, v_rg_wa, v_rg_ba, v_rg_wx, v_rg_bx, v_rg_lambda, v_hg_lb, v_hg_norm_w, v_w_out, v_ple_norm_w, v_w_ple_gate, v_b_ple_gate, v_w_ple_proj, v_final_norm_w):
    given = dict(locals())
    x2, p2, tgt = x[0], p[0, 0], loss_target[0]
    wa_c, wx_c = _small_view("rg_wa", rg_wa), _small_view("rg_wx", rg_wx)

    w_in_b, l_out, l_pg, l_pp, cw3 = _gather_w_in(w_in[0], w_out[0], w_ple_gate[0], w_ple_proj[0], conv_w[0])
    g_ssem, g_rsem, g_lands, tok = _gather_rest_start([l_out, l_pg, l_pp])

    pf, pb, u_b = _fwd_inproj(x2, norm_mix_w, w_in_b, tok)
    h, ya = _fwd_rglru(pf, pb, cw3, conv_b, wa_c, rg_ba, wx_c, rg_bx, rg_lambda)
    yb, o, s_prev = _fwd_hgrn2(pf, pb, hg_lb, hg_norm_w)
    w_out_b, w_pg_b, w_pp_b = _gather_rest_wait(g_ssem, g_rsem, g_lands, yb)
    dh1, dyab, dwo_b, dwg_b, dwp_b, sm_tail = _tail_fwd_bwd(
        x2, p2, tgt, ya, yb, w_out_b.reshape(D, D), w_pg_b.reshape(D, D), w_pp_b,
        ple_norm_w, b_ple_gate, final_norm_w.reshape(1, D))

    QH = D // NSHARD // 2
    r1 = _rs_start("rs_start_tail", [dwo_b, dwg_b, dwp_b], (0, 1, 2, 3),
                   [_landing(QH, D), _landing(QH, D), _landing(PLE // 2, PLE)])
    d_b, sm_b = _bwd_hgrn2(pf, pb, o, s_prev, dyab, hg_lb, hg_norm_w, r1[4])
    dw23 = _bwd_inproj_dw("bwd_inproj_dw23", (2, 3), u_b, None, d_b)
    r2 = _rs_start("rs_start_in23", [dw23], (2, 3), [lax.empty((NSHARD, D // 2, SHW), BF16)], chip_sums=True)
    d_a, rg_c, sm_a = _bwd_rglru(pf, pb, h, dyab, cw3, conv_b, wa_c, rg_ba, wx_c, rg_bx, rg_lambda, r2[4])
    dw01 = _bwd_inproj_dw("bwd_inproj_dw01", (0, 1), u_b, d_a, d_b)
    r3 = _rs_start("rs_start_in01", [dw01], (0, 1), r2[3], chip_sums=True)
    grad_x, sm_in = _bwd_inproj_dx(x2, dh1, d_a, d_b, w_in_b, norm_mix_w, r3[4])

    vred, rgred = _small_allreduce(sm_in, sm_tail, sm_a, sm_b, rg_c)
    parts1, lands1 = _rs_wait("rs_wait_tail", r1[0], r1[1], r1[2], (0, 1, 2, 3), r1[3], vred)
    parts2, lands2 = _rs_wait("rs_wait_in23", r2[0], r2[1], r2[2], (2, 3), r3[3], vred, chip_sums=True)
    parts3, lands3 = _rs_wait("rs_wait_in01", r3[0], r3[1], r3[2], (0, 1), lands2, vred, chip_sums=True)
    g_big = _reduce_exchange([parts3[0], parts2[0]] + parts1, lands3 + lands1)

    upd_big = _adam_big(g_big, [given[n][0] for n in _BIG], [given["m_" + n][0] for n in _BIG],
                        [given["v_" + n][0] for n in _BIG])
    small = _adam_small(vred, rgred,
                        {n: _small_view(n, given[n]) for n in _SMALL_ORDER},
                        {n: _small_view(n, given["m_" + n]) for n in _SMALL_ORDER},
                        {n: _small_view(n, given["v_" + n]) for n in _SMALL_ORDER})

    loss = vred[0, ROW_LOSS, 0]
    outs = [loss, grad_x[None]]
    for ki in range(4):
        for n in _OUT_ORDER:
            if n in _BIG:
                i = _BIG.index(n)
                a = g_big[i] if ki == 0 else upd_big[i][ki - 1]
                outs.append(a[None])
            else:
                outs.append(small[n][ki].reshape(given[n].shape))
    return tuple(outs)
```

```python
import jax
import jax.numpy as jnp
from jax import lax
from jax.experimental import pallas as pl
from jax.experimental.pallas import tpu as pltpu

F32 = jnp.float32
BF16 = jnp.bfloat16
I32 = jnp.int32
MESH = pl.DeviceIdType.MESH
HIGHEST = lax.Precision.HIGHEST

T = 4096
D = 1024
DG = 512
DIN = 3072
PLE = 256
NH = 4
HD = 128
CH = 64
NCHUNK = T // CH
RGB = 64
EPS = 1e-6
RG_C = 8.0
NSHARD = 4
SHW = DIN // NSHARD
NDEV = 8

ADAM_LR = 0.001
ADAM_B1 = 0.9
ADAM_B2 = 0.999
ADAM_EPS = 1e-08
ADAM_WD = 0.01
ADAM_STEP = 10

VMEM_SPEC = pl.BlockSpec(memory_space=pltpu.VMEM)
HBM_SPEC = pl.BlockSpec(memory_space=pltpu.HBM)
SEM_SPEC = pl.BlockSpec(memory_space=pltpu.SEMAPHORE)
ANY_SPEC = pl.BlockSpec(memory_space=pl.ANY)
EFFECT = pltpu.SideEffectType.DATAFLOW_SIDE_EFFECTING
MIB = 1024 * 1024

VROWS = 16
ROW_NORM_MIX, ROW_FINAL_NORM, ROW_B_PG, ROW_PLE_NORM = 0, 1, 2, 3
ROW_CB_BA, ROW_BX_LAM, ROW_CW01, ROW_CW23, ROW_HG_LB, ROW_HG_NW, ROW_LOSS = 4, 5, 6, 7, 8, 9, 10


def _mm(a, b):
    return jnp.dot(a.astype(BF16), b.astype(BF16), preferred_element_type=F32)


def _mm_nt(a, b):
    return lax.dot_general(a.astype(BF16), b.astype(BF16), (((1,), (1,)), ((), ())),
                           preferred_element_type=F32)


def _mm_tn(a, b):
    return lax.dot_general(a.astype(BF16), b.astype(BF16), (((0,), (0,)), ((), ())),
                           preferred_element_type=F32)


def _mm_exact(a, b):
    return jnp.dot(a, b, precision=HIGHEST, preferred_element_type=F32)


def _sig_pair(x):
    e = jnp.exp(-jnp.abs(x))
    big = 1.0 / (1.0 + e)
    small = e * big
    pos = x >= 0
    return jnp.where(pos, big, small), jnp.where(pos, small, big)


def _sigmoid(x):
    return 1.0 / (1.0 + jnp.exp(-x))


def _rowsum(v):
    return jnp.sum(v, axis=0, keepdims=True)


def _shift_down(cur, prev8, d, rows):
    rolled = pltpu.roll(cur, d, 0)
    head = jnp.where(rows[0:8] < d, pltpu.roll(prev8, d, 0), rolled[0:8])
    return jnp.concatenate([head, rolled[8:]], axis=0)


def _shift_up(cur, next8, d, rows):
    n = cur.shape[0]
    rolled = pltpu.roll(cur, n - d, 0)
    tail = jnp.where(rows[0:8] >= 8 - d, pltpu.roll(next8, 8 - d, 0), rolled[n - 8:n])
    return jnp.concatenate([rolled[0:n - 8], tail], axis=0)


def _roll_in_groups(v, d):
    n, w = v.shape
    return pltpu.roll(v.reshape(n // 8, 8, w), d, 1).reshape(n, w)


def _cparams(sem, vmem_mib):
    return pltpu.CompilerParams(dimension_semantics=sem, vmem_limit_bytes=vmem_mib * MIB)


def _mesh_pos():
    x, y, c = lax.axis_index("x"), lax.axis_index("y"), lax.axis_index("c")
    chips = [(1 - x, y), (x, 1 - y), (1 - x, 1 - y)]
    return x, y, c, chips


def _remote(src, dst, ssem, rsem, dev):
    return pltpu.make_async_remote_copy(src_ref=src, dst_ref=dst, send_sem=ssem, recv_sem=rsem,
                                        device_id=dev, device_id_type=MESH)


def _gather_w_in(w_in, w_out, w_pg, w_pp, conv_w):
    shapes = [w_in.shape, w_out.shape, w_pg.shape, w_pp.shape]

    def body(win, wout, wpg, wpp, cw, o_in, o_out, o_pg, o_pp, o_cw, ssem, rsem):
        x, y, c, chips = _mesh_pos()
        kme = 2 * x + y
        sibling = (x, y, 1 - c)

        def cast(src, dst):
            for r0 in range(0, src.shape[0], 128):
                dst[kme, r0:r0 + 128, :] = src[r0:r0 + 128, :].astype(BF16)

        cast(win, o_in)
        o_cw[kme] = cw[...]

        hrows = D // 2
        mine = pl.ds(pl.multiple_of(c * hrows, 128), hrows)
        other = pl.ds(pl.multiple_of((1 - c) * hrows, 128), hrows)

        def half(k, rows):
            return o_in.at[k, rows]

        na = (x ^ c, y ^ (1 - c), c)
        nb = (x ^ (1 - c), y ^ c, c)
        ka = 2 * na[0] + na[1]
        kb = 2 * nb[0] + nb[1]
        kd = 2 * (1 - x) + (1 - y)
        sends = [_remote(half(kme, mine), half(kme, mine), ssem.at[0], rsem.at[0], na),
                 _remote(half(kme, mine), half(kme, mine), ssem.at[1], rsem.at[1], nb)]
        for j, (px, py) in enumerate(chips):
            sends.append(_remote(o_cw.at[kme], o_cw.at[kme], ssem.at[3 + j], rsem.at[3 + j], (px, py, c)))
        for cp in sends:
            cp.start()
        cast(wout, o_out)
        cast(wpg, o_pg)
        cast(wpp, o_pp)

        def passed_on(k, sem, dev):
            cp = _remote(half(k, mine), half(k, mine), ssem.at[sem], rsem.at[sem], dev)
            cp.start()
            sends.append(cp)

        _remote(half(ka, mine), half(ka, mine), ssem.at[0], rsem.at[0], na).wait_recv()
        passed_on(ka, 2, nb)
        passed_on(ka, 6, sibling)
        _remote(half(kb, mine), half(kb, mine), ssem.at[1], rsem.at[1], nb).wait_recv()
        passed_on(kb, 7, sibling)
        _remote(half(kd, mine), half(kd, mine), ssem.at[2], rsem.at[2], nb).wait_recv()
        passed_on(kd, 8, sibling)
        _remote(half(kb, other), half(kb, other), ssem.at[6], rsem.at[6], sibling).wait_recv()
        _remote(half(ka, other), half(ka, other), ssem.at[7], rsem.at[7], sibling).wait_recv()
        _remote(half(kd, other), half(kd, other), ssem.at[8], rsem.at[8], sibling).wait_recv()
        for j, (px, py) in enumerate(chips):
            kj = 2 * px + py
            _remote(o_cw.at[kj], o_cw.at[kj], ssem.at[3 + j], rsem.at[3 + j], (px, py, c)).wait_recv()
        for cp in sends:
            cp.wait_send()

    out_shape = [jax.ShapeDtypeStruct((NSHARD,) + s, BF16) for s in shapes]
    out_shape.append(jax.ShapeDtypeStruct((NSHARD,) + conv_w.shape, F32))
    return pl.pallas_call(
        body, name="gather_w_in", out_shape=out_shape,
        in_specs=[VMEM_SPEC] * 5, out_specs=[VMEM_SPEC] * 5,
        scratch_shapes=[pltpu.SemaphoreType.DMA((9,)), pltpu.SemaphoreType.DMA((9,))],
        compiler_params=pltpu.CompilerParams(vmem_limit_bytes=40 * MIB),
    )(w_in, w_out, w_pg, w_pp, conv_w)


def _gather_rest_start(lands):
    n = len(lands)

    def body(*refs):
        land_in = refs[0:n]
        ssem, rsem = refs[n], refs[n + 1]
        token = refs[2 * n + 2]
        x, y, c, chips = _mesh_pos()
        kme = 2 * x + y
        for p, land in enumerate(land_in):
            hrows = land.shape[1] // 2
            mine = pl.ds(pl.multiple_of(c * hrows, 128), hrows)
            for px, py in chips:
                for pc in range(2):
                    _remote(land.at[kme, mine], land.at[kme, mine], ssem.at[p], rsem.at[p], (px, py, pc)).start()
        token[...] = jnp.zeros_like(token)

    out_shape = ([pltpu.SemaphoreType.DMA((n,)), pltpu.SemaphoreType.DMA((n,))]
                 + [pltpu.HBM(a.shape, a.dtype) for a in lands] + [jax.ShapeDtypeStruct((8, 128), F32)])
    outs = pl.pallas_call(
        body, name="gather_rest_start", out_shape=out_shape,
        in_specs=[HBM_SPEC] * n, out_specs=[SEM_SPEC, SEM_SPEC] + [HBM_SPEC] * n + [VMEM_SPEC],
        input_output_aliases={i: 2 + i for i in range(n)},
        compiler_params=pltpu.CompilerParams(has_side_effects=EFFECT),
    )(*[pltpu.with_memory_space_constraint(a, pltpu.HBM) for a in lands])
    return outs[0], outs[1], list(outs[2:2 + n]), outs[2 + n]


def _gather_rest_wait(ssem, rsem, lands, after):
    n = len(lands)

    def body(*refs):
        land_in = refs[0:n]
        ssem_ref, rsem_ref = refs[n], refs[n + 1]
        x, y, c = lax.axis_index("x"), lax.axis_index("y"), lax.axis_index("c")
        for p, land in enumerate(land_in):
            three = land.at[pl.ds(0, 3)]
            cp = _remote(three, three, ssem_ref.at[p], rsem_ref.at[p], (x, y, c))
            cp.wait_send()
            cp.wait_recv()

    outs = pl.pallas_call(
        body, name="gather_rest_wait", out_shape=[pltpu.HBM(a.shape, a.dtype) for a in lands],
        in_specs=[HBM_SPEC] * n + [SEM_SPEC, SEM_SPEC, ANY_SPEC], out_specs=[HBM_SPEC] * n,
        input_output_aliases={i: i for i in range(n)},
        compiler_params=pltpu.CompilerParams(has_side_effects=EFFECT),
    )(*lands, ssem, rsem, after)
    return list(outs)


def _rs_start(name, parts, ks, lands, chip_sums=False):
    n = len(parts)

    def body(*refs):
        part_in, land_in = refs[0:n], refs[n:2 * n]
        ssem, rsem = refs[2 * n], refs[2 * n + 1]
        token = refs[4 * n + 2]
        x, y, c = lax.axis_index("x"), lax.axis_index("y"), lax.axis_index("c")
        kme = 2 * x + y
        me = 4 * x + 2 * y + c
        for p in range(n):
            hrows = land_in[p].shape[1]
            for i, k in enumerate(ks):
                if chip_sums:
                    @pl.when(kme != k)
                    def _():
                        _remote(part_in[p].at[i], land_in[p].at[kme], ssem.at[p], rsem.at[p], (k // 2, k % 2, c)).start()
                    continue
                for pc in range(2):
                    @pl.when(jnp.logical_or(kme != k, c != pc))
                    def _():
                        _remote(part_in[p].at[i, pl.ds(pc * hrows, hrows)], land_in[p].at[me],
                                ssem.at[p], rsem.at[p], (k // 2, k % 2, pc)).start()
        token[...] = jnp.zeros_like(token)

    arrays = list(parts) + list(lands)
    out_shape = ([pltpu.SemaphoreType.DMA((n,)), pltpu.SemaphoreType.DMA((n,))]
                 + [pltpu.HBM(a.shape, a.dtype) for a in arrays] + [jax.ShapeDtypeStruct((8, 128), F32)])
    outs = pl.pallas_call(
        body, name=name, out_shape=out_shape,
        in_specs=[HBM_SPEC] * (2 * n), out_specs=[SEM_SPEC, SEM_SPEC] + [HBM_SPEC] * (2 * n) + [VMEM_SPEC],
        input_output_aliases={i: 2 + i for i in range(2 * n)},
        compiler_params=pltpu.CompilerParams(has_side_effects=EFFECT),
    )(*[pltpu.with_memory_space_constraint(a, pltpu.HBM) for a in arrays])
    return outs[0], outs[1], list(outs[2:2 + n]), list(outs[2 + n:2 + 2 * n]), outs[2 + 2 * n]


def _rs_wait(name, ssem, rsem, parts, ks, lands, after, chip_sums=False):
    n = len(parts)

    def body(*refs):
        part_in, land_in = refs[0:n], refs[n:2 * n]
        ssem_ref, rsem_ref = refs[2 * n], refs[2 * n + 1]
        x, y, c = lax.axis_index("x"), lax.axis_index("y"), lax.axis_index("c")
        kme = 2 * x + y
        for p in range(n):
            piece = land_in[p].at[0]
            for k in ks:
                for pc in range(1 if chip_sums else 2):
                    mine = (kme == k) if chip_sums else jnp.logical_and(kme == k, c == pc)

                    @pl.when(jnp.logical_not(mine))
                    def _():
                        _remote(piece, piece, ssem_ref.at[p], rsem_ref.at[p], (x, y, c)).wait_send()
            owner = kme == ks[0]
            for k in ks[1:]:
                owner = jnp.logical_or(owner, kme == k)

            @pl.when(owner)
            def _():
                others = land_in[p].at[pl.ds(0, land_in[p].shape[0] - 1)]
                _remote(others, others, ssem_ref.at[p], rsem_ref.at[p], (x, y, c)).wait_recv()

    arrays = list(parts) + list(lands)
    outs = pl.pallas_call(
        body, name=name, out_shape=[pltpu.HBM(a.shape, a.dtype) for a in arrays],
        in_specs=[HBM_SPEC] * (2 * n) + [SEM_SPEC, SEM_SPEC, ANY_SPEC], out_specs=[HBM_SPEC] * (2 * n),
        input_output_aliases={i: i for i in range(2 * n)},
        compiler_params=pltpu.CompilerParams(has_side_effects=EFFECT),
    )(*arrays, ssem, rsem, after)
    return list(outs[0:n]), list(outs[n:2 * n])


def _reduce_exchange(parts, lands):
    shapes = [(2 * l.shape[1], l.shape[2]) for l in lands]
    step = 128

    def body(in01, in23, pout, ppg, ppp, l_in, l_out, l_pg, l_pp, g_in, g_out, g_pg, g_pp, ssem, rsem):
        x, y, c = lax.axis_index("x"), lax.axis_index("y"), lax.axis_index("c")
        kme = 2 * x + y
        me = 4 * x + 2 * y + c
        sibling = (x, y, 1 - c)
        sends = []
        for p, (land, gout) in enumerate([(l_in, g_in), (l_out, g_out), (l_pg, g_pg), (l_pp, g_pp)]):
            hrows = land.shape[1]
            mine0 = pl.multiple_of(c * hrows, step)
            for r0 in range(0, hrows, step):
                rs = pl.ds(mine0 + r0, step)
                if p == 0:
                    own = jnp.where(kme >= 2, in23[kme & 1, r0:r0 + step, :], in01[kme & 1, r0:r0 + step, :])
                    slot = kme
                else:
                    own = (pout, ppg, ppp)[p - 1][kme, rs, :]
                    slot = me
                s = jnp.zeros((step, land.shape[2]), F32)
                for j in range(land.shape[0]):
                    s = s + jnp.where(slot == j, own, land[j, r0:r0 + step, :]).astype(F32)
                gout[rs, :] = s
            mine = pl.ds(mine0, hrows)
            cp = _remote(gout.at[mine], gout.at[mine], ssem.at[p], rsem.at[p], sibling)
            cp.start()
            sends.append(cp)
        for p, (land, gout) in enumerate([(l_in, g_in), (l_out, g_out), (l_pg, g_pg), (l_pp, g_pp)]):
            hrows = land.shape[1]
            other = pl.ds(pl.multiple_of((1 - c) * hrows, step), hrows)
            _remote(gout.at[other], gout.at[other], ssem.at[p], rsem.at[p], sibling).wait_recv()
        for cp in sends:
            cp.wait_send()

    return pl.pallas_call(
        body, name="reduce_exchange", out_shape=[jax.ShapeDtypeStruct(s, F32) for s in shapes],
        in_specs=[VMEM_SPEC] * 9, out_specs=[VMEM_SPEC] * 4,
        scratch_shapes=[pltpu.SemaphoreType.DMA((4,)), pltpu.SemaphoreType.DMA((4,))],
        compiler_params=pltpu.CompilerParams(vmem_limit_bytes=48 * MIB),
    )(*parts, *lands)


def _small_allreduce(sm_in, sm_tail, sm_a, sm_b, rg_c):
    PR = DG // NDEV

    def body(in_ref, tail_ref, a_ref, b_ref, rg_ref, v_out, rg_out, vbuf, vrecv, rgrecv, ssem, rsem):
        x, y, c = lax.axis_index("x"), lax.axis_index("y"), lax.axis_index("c")
        me = 4 * x + 2 * y + c

        def pair(ref, r0, r1):
            return jnp.concatenate([ref[r0:r0 + 1, :], ref[r1:r1 + 1, :]], axis=1)

        rows = {
            ROW_NORM_MIX: in_ref[0:1, :], ROW_FINAL_NORM: tail_ref[1:2, :], ROW_B_PG: tail_ref[2:3, :],
            ROW_PLE_NORM: tail_ref[3:4, :], ROW_CB_BA: pair(a_ref, 0, 1), ROW_BX_LAM: pair(a_ref, 2, 3),
            ROW_CW01: pair(a_ref, 4, 5), ROW_CW23: pair(a_ref, 6, 7), ROW_HG_LB: pair(b_ref, 2, 3),
            ROW_HG_NW: jnp.concatenate([b_ref[1:2, :], jnp.zeros((1, DG), F32)], axis=1),
            ROW_LOSS: tail_ref[4:5, :],
        }
        vbuf[...] = jnp.zeros_like(vbuf)
        for r, row in rows.items():
            for j in range(NDEV):
                vbuf[j, r:r + 1, :] = row[:, j * 128:(j + 1) * 128]

        def peer(mask):
            px = x ^ ((mask >> 2) & 1)
            py = y ^ ((mask >> 1) & 1)
            pc = c ^ (mask & 1)
            return (px, py, pc), 4 * px + 2 * py + pc

        def rg_rows(r):
            return pl.ds(pl.multiple_of(r * PR, PR), PR)

        first = []
        for mask in range(1, NDEV):
            dev, r = peer(mask)
            i = mask - 1
            cp = _remote(vbuf.at[r], vrecv.at[i], ssem.at[i], rsem.at[i], dev)
            cp.start()
            first.append(cp)
            cp = _remote(rg_ref.at[rg_rows(r)], rgrecv.at[i], ssem.at[7 + i], rsem.at[7 + i], dev)
            cp.start()
            first.append(cp)
        sv = vbuf[me]
        sr = rg_ref[rg_rows(me), :]
        for i in range(NDEV - 1):
            first[2 * i].wait_recv()
            first[2 * i + 1].wait_recv()
            sv = sv + vrecv[i]
            sr = sr + rgrecv[i]
        v_out[me] = sv
        rg_out[rg_rows(me), :] = sr
        second = []
        for mask in range(1, NDEV):
            dev, r = peer(mask)
            i = mask - 1
            cp = _remote(v_out.at[me], v_out.at[me], ssem.at[14 + i], rsem.at[14 + i], dev)
            cp.start()
            second.append(cp)
            cp = _remote(rg_out.at[rg_rows(me)], rg_out.at[rg_rows(me)], ssem.at[21 + i], rsem.at[21 + i], dev)
            cp.start()
            second.append(cp)
        for mask in range(1, NDEV):
            dev, r = peer(mask)
            i = mask - 1
            _remote(v_out.at[r], v_out.at[r], ssem.at[14 + i], rsem.at[14 + i], dev).wait_recv()
            _remote(rg_out.at[rg_rows(r)], rg_out.at[rg_rows(r)], ssem.at[21 + i], rsem.at[21 + i], dev).wait_recv()
        for cp in first + second:
            cp.wait_send()

    return pl.pallas_call(
        body, name="small_allreduce",
        out_shape=[jax.ShapeDtypeStruct((NDEV, VROWS, 128), F32), jax.ShapeDtypeStruct((DG, 128), F32)],
        in_specs=[VMEM_SPEC] * 5, out_specs=[VMEM_SPEC] * 2,
        scratch_shapes=[pltpu.VMEM((NDEV, VROWS, 128), F32), pltpu.VMEM((NDEV - 1, VROWS, 128), F32),
                        pltpu.VMEM((NDEV - 1, PR, 128), F32),
                        pltpu.SemaphoreType.DMA((28,)), pltpu.SemaphoreType.DMA((28,))],
    )(sm_in, sm_tail, sm_a, sm_b, rg_c)


def _final_reduce(parts, lands, sm_in, sm_tail, sm_a, sm_b, rg_c):
    shapes = [(2 * l.shape[1], l.shape[2]) for l in lands]
    step = 128
    PR = DG // NDEV

    def body(in01, in23, pout, ppg, ppp, l_in, l_out, l_pg, l_pp, in_ref, tail_ref, a_ref, b_ref, rg_ref,
             g_in, g_out, g_pg, g_pp, v_out, rg_out, vbuf, vrecv, rgrecv, bsem_s, bsem_r, ssem, rsem):
        x, y, c = lax.axis_index("x"), lax.axis_index("y"), lax.axis_index("c")
        kme = 2 * x + y
        me = 4 * x + 2 * y + c
        sibling = (x, y, 1 - c)

        def pair(ref, r0, r1):
            return jnp.concatenate([ref[r0:r0 + 1, :], ref[r1:r1 + 1, :]], axis=1)

        rows = {
            ROW_NORM_MIX: in_ref[0:1, :], ROW_FINAL_NORM: tail_ref[1:2, :], ROW_B_PG: tail_ref[2:3, :],
            ROW_PLE_NORM: tail_ref[3:4, :], ROW_CB_BA: pair(a_ref, 0, 1), ROW_BX_LAM: pair(a_ref, 2, 3),
            ROW_CW01: pair(a_ref, 4, 5), ROW_CW23: pair(a_ref, 6, 7), ROW_HG_LB: pair(b_ref, 2, 3),
            ROW_HG_NW: jnp.concatenate([b_ref[1:2, :], jnp.zeros((1, DG), F32)], axis=1),
            ROW_LOSS: tail_ref[4:5, :],
        }
        vbuf[...] = jnp.zeros_like(vbuf)
        for r, row in rows.items():
            for j in range(NDEV):
                vbuf[j, r:r + 1, :] = row[:, j * 128:(j + 1) * 128]

        def peer(mask):
            px = x ^ ((mask >> 2) & 1)
            py = y ^ ((mask >> 1) & 1)
            pc = c ^ (mask & 1)
            return (px, py, pc), 4 * px + 2 * py + pc

        def rg_rows(r):
            return pl.ds(pl.multiple_of(r * PR, PR), PR)

        first = []
        for mask in range(1, NDEV):
            dev, r = peer(mask)
            i = mask - 1
            cp = _remote(vbuf.at[r], vrecv.at[i], ssem.at[i], rsem.at[i], dev)
            cp.start()
            first.append(cp)
            cp = _remote(rg_ref.at[rg_rows(r)], rgrecv.at[i], ssem.at[7 + i], rsem.at[7 + i], dev)
            cp.start()
            first.append(cp)

        big = [(l_in, g_in), (l_out, g_out), (l_pg, g_pg), (l_pp, g_pp)]
        swaps = []
        for p, (land, gout) in enumerate(big):
            hrows = land.shape[1]
            mine0 = pl.multiple_of(c * hrows, step)
            for r0 in range(0, hrows, step):
                rs = pl.ds(mine0 + r0, step)
                if p == 0:
                    own = jnp.where(kme >= 2, in23[kme & 1, r0:r0 + step, :], in01[kme & 1, r0:r0 + step, :])
                    slot = kme
                else:
                    own = (pout, ppg, ppp)[p - 1][kme, rs, :]
                    slot = me
                s = jnp.zeros((step, land.shape[2]), F32)
                for j in range(land.shape[0]):
                    s = s + jnp.where(slot == j, own, land[j, r0:r0 + step, :]).astype(F32)
                gout[rs, :] = s
            mine = pl.ds(mine0, hrows)
            cp = _remote(gout.at[mine], gout.at[mine], bsem_s.at[p], bsem_r.at[p], sibling)
            cp.start()
            swaps.append(cp)

        sv = vbuf[me]
        sr = rg_ref[rg_rows(me), :]
        for i in range(NDEV - 1):
            first[2 * i].wait_recv()
            first[2 * i + 1].wait_recv()
            sv = sv + vrecv[i]
            sr = sr + rgrecv[i]
        v_out[me] = sv
        rg_out[rg_rows(me), :] = sr
        second = []
        for mask in range(1, NDEV):
            dev, r = peer(mask)
            i = mask - 1
            cp = _remote(v_out.at[me], v_out.at[me], ssem.at[14 + i], rsem.at[14 + i], dev)
            cp.start()
            second.append(cp)
            cp = _remote(rg_out.at[rg_rows(me)], rg_out.at[rg_rows(me)], ssem.at[21 + i], rsem.at[21 + i], dev)
            cp.start()
            second.append(cp)
        for p, (land, gout) in enumerate(big):
            hrows = land.shape[1]
            other = pl.ds(pl.multiple_of((1 - c) * hrows, step), hrows)
            _remote(gout.at[other], gout.at[other], bsem_s.at[p], bsem_r.at[p], sibling).wait_recv()
        for mask in range(1, NDEV):
            dev, r = peer(mask)
            i = mask - 1
            _remote(v_out.at[r], v_out.at[r], ssem.at[14 + i], rsem.at[14 + i], dev).wait_recv()
            _remote(rg_out.at[rg_rows(r)], rg_out.at[rg_rows(r)], ssem.at[21 + i], rsem.at[21 + i], dev).wait_recv()
        for cp in first + swaps + second:
            cp.wait_send()

    out_shape = [jax.ShapeDtypeStruct(s, F32) for s in shapes]
    out_shape += [jax.ShapeDtypeStruct((NDEV, VROWS, 128), F32), jax.ShapeDtypeStruct((DG, 128), F32)]
    outs = pl.pallas_call(
        body, name="final_reduce", out_shape=out_shape,
        in_specs=[VMEM_SPEC] * 14, out_specs=[VMEM_SPEC] * 6,
        scratch_shapes=[pltpu.VMEM((NDEV, VROWS, 128), F32), pltpu.VMEM((NDEV - 1, VROWS, 128), F32),
                        pltpu.VMEM((NDEV - 1, PR, 128), F32),
                        pltpu.SemaphoreType.DMA((4,)), pltpu.SemaphoreType.DMA((4,)),
                        pltpu.SemaphoreType.DMA((28,)), pltpu.SemaphoreType.DMA((28,))],
        compiler_params=pltpu.CompilerParams(vmem_limit_bytes=48 * MIB),
    )(*parts, *lands, sm_in, sm_tail, sm_a, sm_b, rg_c)
    return list(outs[0:4]), outs[4], outs[5]


def _adam_rows(w, g, m, v):
    m2 = ADAM_B1 * m + (1.0 - ADAM_B1) * g
    v2 = ADAM_B2 * v + (1.0 - ADAM_B2) * (g * g)
    m_hat = m2 / (1.0 - ADAM_B1 ** ADAM_STEP)
    v_hat = v2 / (1.0 - ADAM_B2 ** ADAM_STEP)
    delta = -ADAM_LR * (m_hat / (jnp.sqrt(v_hat) + ADAM_EPS) + ADAM_WD * w)
    return delta, m2, v2


def _adam_big(gs, ws, ms, vs):
    n = len(gs)
    steps = 8

    def body(*refs):
        ins, outs = refs[:4 * n], refs[4 * n:]
        for i in range(n):
            g, w, m, v = (r[...] for r in ins[4 * i:4 * i + 4])
            d, m2, v2 = _adam_rows(w, g, m, v)
            outs[3 * i][...] = d
            outs[3 * i + 1][...] = m2
            outs[3 * i + 2][...] = v2

    in_specs, out_specs, out_shape, args = [], [], [], []
    for g, w, m, v in zip(gs, ws, ms, vs):
        r, c = w.shape
        spec = lambda: pl.BlockSpec((r // steps, c), lambda i: (i, 0))
        in_specs += [spec() for _ in range(4)]
        out_specs += [spec() for _ in range(3)]
        out_shape += [jax.ShapeDtypeStruct((r, c), F32)] * 3
        args += [g, w, m, v]
    outs = pl.pallas_call(
        body, name="adam_big", grid=(steps,), in_specs=in_specs, out_specs=out_specs, out_shape=out_shape,
        compiler_params=_cparams(("parallel",), 32),
    )(*args)
    return [tuple(outs[3 * i:3 * i + 3]) for i in range(n)]


_VEC_PARAMS = [
    ("norm_mix_w", ROW_NORM_MIX, 0, D), ("final_norm_w", ROW_FINAL_NORM, 0, D),
    ("b_ple_gate", ROW_B_PG, 0, D), ("ple_norm_w", ROW_PLE_NORM, 0, D),
    ("conv_b", ROW_CB_BA, 0, DG), ("rg_ba", ROW_CB_BA, DG, DG),
    ("rg_bx", ROW_BX_LAM, 0, DG), ("rg_lambda", ROW_BX_LAM, DG, DG),
    ("hg_norm_w", ROW_HG_NW, 0, HD),
]
_SMALL_ORDER = [n for n, _, _, _ in _VEC_PARAMS] + ["hg_lb", "conv_w", "rg_wa", "rg_wx"]


def _adam_small(vred, rgred, ws, ms, vs):
    names = _SMALL_ORDER
    n = len(names)

    def body(vred_ref, rg_ref, *refs):
        w_refs = dict(zip(names, refs[0:n]))
        m_refs = dict(zip(names, refs[n:2 * n]))
        v_refs = dict(zip(names, refs[2 * n:3 * n]))
        outs = refs[3 * n:]
        o_refs = {nm: outs[4 * i:4 * i + 4] for i, nm in enumerate(names)}
        kme = 2 * lax.axis_index("x") + lax.axis_index("y")

        def update(nm, g, idx):
            d, m2, v2 = _adam_rows(w_refs[nm][idx], g, m_refs[nm][idx], v_refs[nm][idx])
            og, od, om, ov = o_refs[nm]
            og[idx] = g
            od[idx] = d
            om[idx] = m2
            ov[idx] = v2

        def packed(row, lane0, width):
            return jnp.concatenate([vred_ref[j, row:row + 1, :] for j in range(lane0 // 128, (lane0 + width) // 128)],
                                   axis=1)

        everything = (slice(None), slice(None))
        for nm, row, lane0, width in _VEC_PARAMS:
            update(nm, packed(row, lane0, width), everything)
        for r in range(2):
            update("hg_lb", packed(ROW_HG_LB, r * DG, DG), (slice(r, r + 1), slice(None)))
        for j in range(4):
            g = vred_ref[(j % 2) * 4 + kme, ROW_CW01 + j // 2:ROW_CW01 + j // 2 + 1, :]
            update("conv_w", g, (slice(j, j + 1), slice(None)))
        for r0 in range(0, DG, 128):
            rs = (slice(r0, r0 + 128), slice(None))
            both = rg_ref[r0:r0 + 128, :]
            update("rg_wa", both[:, 0:RGB], rs)
            update("rg_wx", pltpu.roll(both, RGB, 1)[:, 0:RGB], rs)

    args = [vred, rgred] + [d[nm] for d in (ws, ms, vs) for nm in names]
    out_shape = []
    for nm in names:
        out_shape += [jax.ShapeDtypeStruct(ws[nm].shape, F32)] * 4
    whole = lambda s: pl.BlockSpec(s.shape, lambda i, nd=len(s.shape): (0,) * nd)
    outs = pl.pallas_call(
        body, name="adam_small", out_shape=out_shape, grid=(1,),
        in_specs=[whole(a) for a in args], out_specs=[whole(s) for s in out_shape],
    )(*args)
    return {nm: tuple(outs[4 * i:4 * i + 4]) for i, nm in enumerate(names)}


def _fwd_inproj(x, nw, w_in_b, dep):
    tm = 512

    def body(x_ref, nw_ref, w_ref, dep_ref, pf_ref, pb_ref, u_ref):
        xv = x_ref[...]
        s = lax.rsqrt(jnp.mean(xv * xv, axis=-1, keepdims=True) + EPS)
        u = (xv * s * nw_ref[...]).astype(BF16)
        u_ref[...] = u
        r = [jnp.dot(u, w_ref[k], preferred_element_type=F32) for k in range(NSHARD)]
        h = DG // 2
        pf_ref[:, 0:DG] = r[0][:, 0:DG]
        pb_ref[:, 0:h] = r[0][:, DG:SHW].astype(BF16)
        pb_ref[:, h:DG] = r[1][:, 0:h].astype(BF16)
        pb_ref[:, DG:2 * DG] = r[1][:, h:SHW].astype(BF16)
        pf_ref[:, DG:2 * DG] = r[2][:, 0:DG]
        pb_ref[:, 2 * DG:2 * DG + h] = r[2][:, DG:SHW].astype(BF16)
        pb_ref[:, 2 * DG + h:3 * DG] = r[3][:, 0:h].astype(BF16)
        pb_ref[:, 3 * DG:4 * DG] = r[3][:, h:SHW].astype(BF16)

    return pl.pallas_call(
        body, name="fwd_inproj", grid=(T // tm,),
        in_specs=[pl.BlockSpec((tm, D), lambda i: (i, 0)),
                  pl.BlockSpec((1, D), lambda i: (0, 0)),
                  pl.BlockSpec((NSHARD, D, SHW), lambda i: (0, 0, 0), pipeline_mode=pl.Buffered(1)), ANY_SPEC],
        out_specs=[pl.BlockSpec((tm, 2 * DG), lambda i: (i, 0)),
                   pl.BlockSpec((tm, 4 * DG), lambda i: (i, 0)),
                   pl.BlockSpec((tm, D), lambda i: (i, 0))],
        out_shape=[jax.ShapeDtypeStruct((T, 2 * DG), F32), jax.ShapeDtypeStruct((T, 4 * DG), BF16),
                   jax.ShapeDtypeStruct((T, D), BF16)],
        compiler_params=_cparams(("parallel",), 48),
    )(x, nw, w_in_b, dep)


def _conv_rows(cw_ref):
    return [jnp.concatenate([cw_ref[k, j:j + 1, :] for k in range(NSHARD)], axis=1) for j in range(4)]


def _rg_conv(xa, prev8, cw, cb, rows):
    taps = [_shift_down(xa, prev8, 3, rows), _shift_down(xa, prev8, 2, rows),
            _shift_down(xa, prev8, 1, rows), xa]
    xc = cb
    for j in range(4):
        xc = xc + taps[j] * cw[j]
    return xc, taps


def _block_mask():
    r = lax.broadcasted_iota(I32, (DG, DG), 0)
    c = lax.broadcasted_iota(I32, (DG, DG), 1)
    return (r >> 6) == (c >> 6)


def _dense_from_blocks(wc):
    j = lax.broadcasted_iota(I32, (RGB, DG), 0)
    c = lax.broadcasted_iota(I32, (RGB, DG), 1)
    spread = _mm_exact(wc, ((c & (RGB - 1)) == j).astype(F32))
    return jnp.where(_block_mask(), spread, 0.0)


def _blocks_from_dense(da, dx):
    c = lax.broadcasted_iota(I32, (DG, 128), 0)
    j = lax.broadcasted_iota(I32, (DG, 128), 1)
    hit = (c & (RGB - 1)) == (j & (RGB - 1))
    mask = _block_mask()
    return (_mm_exact(jnp.where(mask, da, 0.0), (hit & (j < RGB)).astype(F32))
            + _mm_exact(jnp.where(mask, dx, 0.0), (hit & (j >= RGB)).astype(F32)))


def _rg_gates(xc, wa, ba, wx, bx, sp, first_row):
    r = _sigmoid(_mm(xc, wa) + ba)
    i = _sigmoid(_mm(xc, wx) + bx)
    log_a = (-RG_C) * r * sp
    a = jnp.exp(log_a)
    a2 = a * a
    one_m_a2 = -jnp.tanh(log_a) * (a2 + 1.0)
    mult = jnp.where(first_row, 1.0, jnp.sqrt(one_m_a2))
    return r, i, a, a2, mult


def _softplus(z):
    return jnp.maximum(z, 0.0) + jnp.log1p(jnp.exp(-jnp.abs(z)))


def _fwd_rglru(pf, pb, cw3, conv_b, wa_c, ba, wx_c, bx, lam):
    tm = 512
    ng = tm // 8

    def body(xa_ref, ga_ref, cw_ref, cb_ref, wa_ref, ba_ref, wx_ref, bx_ref, lam_ref,
             h_ref, ya_ref, a_s, u_s, tail_s, hc_s, wa_s, wx_s):
        i = pl.program_id(0)

        @pl.when(i == 0)
        def _():
            tail_s[...] = jnp.zeros_like(tail_s)
            hc_s[...] = jnp.zeros_like(hc_s)
            wa_s[...] = _dense_from_blocks(wa_ref[...]).astype(BF16)
            wx_s[...] = _dense_from_blocks(wx_ref[...]).astype(BF16)

        rows = lax.broadcasted_iota(I32, (tm, DG), 0)
        xa = xa_ref[...]
        xc, _ = _rg_conv(xa, tail_s[...], _conv_rows(cw_ref), cb_ref[...], rows)
        tail_s[...] = xa[tm - 8:tm, :]
        sp = _softplus(-lam_ref[...])
        rp = _mm(xc, wa_s[...]) + ba_ref[...]
        ip = _mm(xc, wx_s[...]) + bx_ref[...]
        rb = 64
        rows_b = lax.broadcasted_iota(I32, (rb, DG), 0)
        rows8 = rows_b & 7
        carry = hc_s[0:1, :]
        for b0 in range(0, tm, rb):
            sl = slice(b0, b0 + rb)
            r = _sigmoid(rp[sl])
            ig = _sigmoid(ip[sl])
            log_a = (-RG_C) * r * sp
            av = jnp.exp(log_a)
            mult = jnp.sqrt(-jnp.tanh(log_a) * (av * av + 1.0))
            if b0 == 0:
                mult = jnp.where((rows_b + i * tm) == 0, 1.0, mult)
            uv = mult * (ig * xc[sl])
            for d in (1, 2, 4):
                keep = rows8 >= d
                uv = uv + av * jnp.where(keep, _roll_in_groups(uv, d), 0.0)
                av = av * jnp.where(keep, _roll_in_groups(av, d), 1.0)
            ga = ga_ref[sl, :].astype(F32)
            gate = ga * _sigmoid(ga)
            hs = []
            for g in range(rb // 8):
                gs = slice(g * 8, (g + 1) * 8)
                hv = uv[gs] + av[gs] * carry
                carry = hv[7:8, :]
                hs.append(hv)
            hb = jnp.concatenate(hs, axis=0)
            h_ref[sl, :] = hb
            ya_ref[sl, :] = (hb * gate).astype(BF16)
        hc_s[0:1, :] = carry

    vec = lambda: pl.BlockSpec((1, DG), lambda i: (0, 0))
    blocks = lambda: pl.BlockSpec((DG, RGB), lambda i: (0, 0))
    return pl.pallas_call(
        body, name="fwd_rglru", grid=(T // tm,),
        in_specs=[pl.BlockSpec((tm, DG), lambda i: (i, 0)),
                  pl.BlockSpec((tm, DG), lambda i: (i, 0)),
                  pl.BlockSpec((NSHARD, 4, 128), lambda i: (0, 0, 0)), vec(),
                  blocks(), vec(), blocks(), vec(), vec()],
        out_specs=[pl.BlockSpec((tm, DG), lambda i: (i, 0)),
                   pl.BlockSpec((tm, DG), lambda i: (i, 0))],
        out_shape=[jax.ShapeDtypeStruct((T, DG), F32), jax.ShapeDtypeStruct((T, DG), BF16)],
        scratch_shapes=[pltpu.VMEM((tm, DG), F32), pltpu.VMEM((tm, DG), F32),
                        pltpu.VMEM((8, DG), F32), pltpu.VMEM((8, DG), F32),
                        pltpu.VMEM((DG, DG), BF16), pltpu.VMEM((DG, DG), BF16)],
        compiler_params=_cparams(("arbitrary",), 48),
    )(pf, pb, cw3, conv_b, wa_c, ba, wx_c, bx, lam)


def _hg_lower_bound(lb_ref):
    return _sig_pair(lb_ref[0:1, :] - lb_ref[1:2, :])


def _hg_gates(fz, lb, one_m_lb):
    sg, sn = _sig_pair(fz)
    f = lb + one_m_lb * sg
    return sg, sn, f, jnp.log(f), one_m_lb * sn


def _tri(lower):
    r = lax.broadcasted_iota(I32, (CH, CH), 0)
    c = lax.broadcasted_iota(I32, (CH, CH), 1)
    return (r >= c) if lower else (r <= c)


def _chunk_cumsum(v, rows64):
    for d in (1, 2, 4, 8, 16, 32):
        v = v + jnp.where(rows64 >= d, pltpu.roll(v, d, 0), 0.0)
    return v


def _chunk_rev_cumsum(v, rows64):
    n = v.shape[0]
    for d in (1, 2, 4, 8, 16, 32):
        v = v + jnp.where(rows64 < CH - d, pltpu.roll(v, n - d, 0), 0.0)
    return v


def _hg_recompute(q_ref, f_ref, lb, one_m_lb, rows64, eb_s, enb_s, ekd_s, qe_s, ke_s, kd_s, dec_s):
    nc = q_ref.shape[0] // CH
    sg, sn, f, logf, k = _hg_gates(f_ref[...], lb, one_m_lb)
    q = q_ref[...].astype(F32)
    sq = _sigmoid(q)
    qs = q * sq * (HD ** -0.5)
    b = _chunk_cumsum(logf, rows64)
    for c in range(nc):
        rs = slice(c * CH, (c + 1) * CH)
        b_c = b[rs]
        bl = b_c[CH - 1:CH, :]
        eb, enb, ekd = jnp.exp(b_c), jnp.exp(-b_c), jnp.exp(bl - b_c)
        if eb_s is not None:
            eb_s[rs, :] = eb
            enb_s[rs, :] = enb
            ekd_s[rs, :] = ekd
        qe_s[rs, :] = (qs[rs] * eb).astype(BF16)
        ke_s[rs, :] = (k[rs] * enb).astype(BF16)
        kd_s[rs, :] = (k[rs] * ekd).astype(BF16)
        dec_s[c:c + 1, :] = jnp.exp(bl)
    return sg, sn, f, k, q, sq, qs


def _fwd_hgrn2(pf, pb, hg_lb, hg_nw):
    tm = 512
    nc = tm // CH

    def body(q_ref, f_ref, v_ref, g_ref, lb_ref, nw_ref, yb_ref, o_ref, sp_ref,
             st_s, qe_s, ke_s, kd_s, vb_s, dec_s, p_s, ds_s):
        i = pl.program_id(0)

        @pl.when(i == 0)
        def _():
            st_s[...] = jnp.zeros_like(st_s)

        lb, one_m_lb = _hg_lower_bound(lb_ref)
        rows64 = lax.broadcasted_iota(I32, (tm, DG), 0) & (CH - 1)
        _hg_recompute(q_ref, f_ref, lb, one_m_lb, rows64, None, None, None, qe_s, ke_s, kd_s, dec_s)
        vb_s[...] = v_ref[...]
        mask = _tri(True)
        items = [(c, hd, slice(c * CH, (c + 1) * CH), slice(hd * HD, (hd + 1) * HD))
                 for c in range(nc) for hd in range(NH)]
        for c, hd, rs, cols in items:
            p_s[c * NH + hd] = jnp.where(mask, _mm_nt(qe_s[rs, cols], ke_s[rs, cols]), 0.0).astype(BF16)
            ds_s[c * NH + hd] = _mm_tn(vb_s[rs, cols], kd_s[rs, cols])
        for c, hd, rs, cols in items:
            st = st_s[hd]
            sp_ref[hd, c] = st
            st_s[hd] = st * dec_s[c:c + 1, cols] + ds_s[c * NH + hd]
        for c, hd, rs, cols in items:
            o_ref[rs, cols] = _mm(p_s[c * NH + hd], vb_s[rs, cols]) + _mm_nt(qe_s[rs, cols], sp_ref[hd, c])
        nw = nw_ref[...]
        for hd in range(NH):
            cols = slice(hd * HD, (hd + 1) * HD)
            o = o_ref[:, cols]
            so = lax.rsqrt(jnp.mean(o * o, axis=-1, keepdims=True) + EPS)
            g = g_ref[:, cols].astype(F32)
            sg = _sigmoid(g)
            yb_ref[:, cols] = (o * so * nw * (g * sg)).astype(BF16)

    col = lambda j: pl.BlockSpec((tm, DG), lambda i: (i, j))
    return pl.pallas_call(
        body, name="fwd_hgrn2", grid=(T // tm,),
        in_specs=[col(1), col(1), col(2), col(3),
                  pl.BlockSpec((2, DG), lambda i: (0, 0)),
                  pl.BlockSpec((1, HD), lambda i: (0, 0))],
        out_specs=[pl.BlockSpec((tm, DG), lambda i: (i, 0)),
                   pl.BlockSpec((tm, DG), lambda i: (i, 0)),
                   pl.BlockSpec((NH, nc, HD, HD), lambda i: (0, i, 0, 0))],
        out_shape=[jax.ShapeDtypeStruct((T, DG), BF16), jax.ShapeDtypeStruct((T, DG), F32),
                   jax.ShapeDtypeStruct((NH, NCHUNK, HD, HD), F32)],
        scratch_shapes=[pltpu.VMEM((NH, HD, HD), F32),
                        pltpu.VMEM((tm, DG), BF16), pltpu.VMEM((tm, DG), BF16), pltpu.VMEM((tm, DG), BF16),
                        pltpu.VMEM((tm, DG), BF16), pltpu.VMEM((nc, DG), F32),
                        pltpu.VMEM((nc * NH, CH, CH), BF16), pltpu.VMEM((nc * NH, HD, HD), F32)],
        compiler_params=_cparams(("arbitrary",), 48),
    )(pb, pf, pb, pb, hg_lb, hg_nw)


def _tail_fwd_bwd(x, p, tgt, ya, yb, w_out_b, w_pg_b, w_pp_b, ple_nw, b_pg, fnw):
    tm = 512
    nt = T // tm
    QR = D // NSHARD

    def body(x_ref, p_ref, t_ref, ya_ref, yb_ref, wo_ref, wg_ref, wp_ref, pw_ref, b_ref, fw_ref,
             dh1_ref, dyab_ref, dwo_ref, dwg_ref, dwp_ref, sm_ref, dwo_s, dwg_s, dwp_s):
        i = pl.program_id(0)

        @pl.when(i == 0)
        def _():
            dwo_s[...] = jnp.zeros_like(dwo_s)
            dwg_s[...] = jnp.zeros_like(dwg_s)
            dwp_s[...] = jnp.zeros_like(dwp_s)
            sm_ref[...] = jnp.zeros_like(sm_ref)

        ya = ya_ref[...]
        yb = yb_ref[...]
        pv = p_ref[...].astype(BF16)
        pw = pw_ref[...]
        fw = fw_ref[...]
        h1 = x_ref[...] + _mm(ya, wo_ref[0:DG, :]) + _mm(yb, wo_ref[DG:D, :])
        s2 = lax.rsqrt(jnp.mean(h1 * h1, axis=-1, keepdims=True) + EPS)
        n2h = h1 * s2
        n2 = (n2h * pw).astype(BF16)
        z = _mm(n2, wg_ref[...]) + b_ref[...]
        gate = _sigmoid(z)
        pp = jnp.concatenate([_mm(pv, wp_ref[k]) for k in range(NSHARD)], axis=1)
        h2 = h1 + gate * pp
        s3 = lax.rsqrt(jnp.mean(h2 * h2, axis=-1, keepdims=True) + EPS)
        hn = h2 * s3
        err = hn * fw - t_ref[...]
        sm_ref[0:1, :] += _rowsum(err * err)
        dy = err * (1.0 / D)
        sm_ref[1:2, :] += _rowsum(dy * hn)
        g3 = dy * fw
        dh2 = s3 * (g3 - hn * jnp.mean(g3 * hn, axis=-1, keepdims=True))
        dpp = (dh2 * gate).astype(BF16)
        dz = dh2 * pp * gate * (1.0 - gate)
        sm_ref[2:3, :] += _rowsum(dz)
        dzb = dz.astype(BF16)
        dwg_s[...] += _mm_tn(n2, dzb)
        dn2 = _mm_nt(dzb, wg_ref[...])
        for k in range(NSHARD):
            dwp_s[k] += _mm_tn(pv, dpp[:, k * PLE:(k + 1) * PLE])
        sm_ref[3:4, :] += _rowsum(dn2 * n2h)
        g2 = dn2 * pw
        dh1 = dh2 + s2 * (g2 - n2h * jnp.mean(g2 * n2h, axis=-1, keepdims=True))
        dh1_ref[...] = dh1
        dh1b = dh1.astype(BF16)
        dyab_ref[...] = _mm_nt(dh1b, wo_ref[...])
        dwo_s[0:DG, :] += _mm_tn(ya, dh1b)
        dwo_s[DG:D, :] += _mm_tn(yb, dh1b)

        @pl.when(i == nt - 1)
        def _():
            total = jnp.sum(sm_ref[0:1, :], axis=-1, keepdims=True) * (0.5 / D)
            sm_ref[4:5, :] = jnp.broadcast_to(total, (1, D))
            for k in range(NSHARD):
                dwo_ref[k] = dwo_s[k * QR:(k + 1) * QR, :].astype(BF16)
                dwg_ref[k] = dwg_s[k * QR:(k + 1) * QR, :].astype(BF16)
                dwp_ref[k] = dwp_s[k].astype(BF16)

    row = lambda w: pl.BlockSpec((tm, w), lambda i: (i, 0))
    const2 = lambda s: pl.BlockSpec(s, lambda i: (0, 0), pipeline_mode=pl.Buffered(1))
    const3 = lambda s: pl.BlockSpec(s, lambda i: (0, 0, 0), pipeline_mode=pl.Buffered(1))
    return pl.pallas_call(
        body, name="tail_fwd_bwd", grid=(nt,),
        in_specs=[row(D), row(PLE), row(D), row(DG), row(DG),
                  const2((D, D)), const2((D, D)), const3((NSHARD, PLE, PLE)),
                  const2((1, D)), const2((1, D)), const2((1, D))],
        out_specs=[row(D), row(D), const3((NSHARD, QR, D)), const3((NSHARD, QR, D)),
                   const3((NSHARD, PLE, PLE)), const2((8, D))],
        out_shape=[jax.ShapeDtypeStruct((T, D), F32), jax.ShapeDtypeStruct((T, D), F32),
                   jax.ShapeDtypeStruct((NSHARD, QR, D), BF16), jax.ShapeDtypeStruct((NSHARD, QR, D), BF16),
                   jax.ShapeDtypeStruct((NSHARD, PLE, PLE), BF16), jax.ShapeDtypeStruct((8, D), F32)],
        scratch_shapes=[pltpu.VMEM((D, D), F32), pltpu.VMEM((D, D), F32), pltpu.VMEM((NSHARD, PLE, PLE), F32)],
        compiler_params=_cparams(("arbitrary",), 62),
    )(x, p, tgt, ya, yb, w_out_b, w_pg_b, w_pp_b, ple_nw, b_pg, fnw)


def _bwd_rglru(pf, pb, h, dyab, cw3, conv_b, wa_c, ba, wx_c, bx, lam, dep):
    tm = 512
    nt = T // tm
    ng = tm // 8

    def body(xa_ref, ga_ref, xp_ref, h_ref, hp_ref, dya_ref, cw_ref, cb_ref, wa_ref, ba_ref, wx_ref, bx_ref,
             lam_ref, dep_ref, da_ref, rg_ref, sm_ref, a_s, g_s, cg_s, nxt_s, wa_s, wx_s, dwa_s, dwx_s):
        i = pl.program_id(0)
        tile = nt - 1 - i

        @pl.when(i == 0)
        def _():
            dwa_s[...] = jnp.zeros_like(dwa_s)
            dwx_s[...] = jnp.zeros_like(dwx_s)
            sm_ref[...] = jnp.zeros_like(sm_ref)
            cg_s[...] = jnp.zeros_like(cg_s)
            nxt_s[...] = jnp.zeros_like(nxt_s)
            wa_s[...] = _dense_from_blocks(wa_ref[...]).astype(BF16)
            wx_s[...] = _dense_from_blocks(wx_ref[...]).astype(BF16)

        rows = lax.broadcasted_iota(I32, (tm, DG), 0)
        has_prev = tile > 0
        xa = xa_ref[...]
        xprev = jnp.where(has_prev, xp_ref[...], 0.0)
        cw = _conv_rows(cw_ref)
        xc, taps = _rg_conv(xa, xprev, cw, cb_ref[...], rows)
        lam_v = lam_ref[...]
        sp = _softplus(-lam_v)
        first_row = (rows + tile * tm) == 0
        r, ig, a, a2, mult = _rg_gates(xc, wa_s[...], ba_ref[...], wx_s[...], bx_ref[...], sp, first_row)
        hv = h_ref[...]
        hprev = jnp.where(has_prev, hp_ref[...], 0.0)
        h_m1 = _shift_down(hv, hprev, 1, rows)
        ga = ga_ref[...].astype(F32)
        sg = _sigmoid(ga)
        dya = dya_ref[...]
        dga = dya * hv * (sg * (1.0 + ga * (1.0 - sg)))

        av = jnp.where(rows == tm - 1, 1.0, pltpu.roll(a, tm - 1, 0))
        gv = dya * (ga * sg)
        rows8 = rows & 7
        for d in (1, 2, 4):
            keep = rows8 < 8 - d
            gv = gv + av * jnp.where(keep, _roll_in_groups(gv, 8 - d), 0.0)
            av = av * jnp.where(keep, _roll_in_groups(av, 8 - d), 1.0)
        a_s[...] = av
        g_s[...] = gv
        carry = cg_s[0:1, :]
        for g in range(ng - 1, -1, -1):
            sl = slice(g * 8, (g + 1) * 8)
            ab, gb = a_s[sl, :], g_s[sl, :]
            g_s[sl, :] = gb + ab * carry
            carry = gb[0:1, :] + ab[0:1, :] * carry
        cg_s[0:1, :] = a[0:1, :] * carry

        gt = g_s[...]
        da = gt * h_m1
        ixc = ig * xc
        di = gt * mult * xc
        dxc = gt * mult * ig
        dlog_a = da * a + jnp.where(first_row, 0.0, gt * ixc * (-a2 / mult))
        sm_ref[3:4, :] += _rowsum(dlog_a * ((-RG_C) * r))
        dpr_f = dlog_a * ((-RG_C) * sp) * r * (1.0 - r)
        dpi_f = di * ig * (1.0 - ig)
        sm_ref[1:2, :] += _rowsum(dpr_f)
        sm_ref[2:3, :] += _rowsum(dpi_f)
        dpr = dpr_f.astype(BF16)
        dpi = dpi_f.astype(BF16)
        xcb = xc.astype(BF16)
        dwa_s[...] += _mm_tn(xcb, dpr)
        dwx_s[...] += _mm_tn(xcb, dpi)
        dxc = dxc + _mm_nt(dpr, wa_s[...]) + _mm_nt(dpi, wx_s[...])
        sm_ref[0:1, :] += _rowsum(dxc)
        for j in range(4):
            sm_ref[4 + j:5 + j, :] += _rowsum(dxc * taps[j])
        nxt = nxt_s[...]
        dxa = (dxc * cw[3] + _shift_up(dxc, nxt, 1, rows) * cw[2]
               + _shift_up(dxc, nxt, 2, rows) * cw[1] + _shift_up(dxc, nxt, 3, rows) * cw[0])
        nxt_s[...] = dxc[0:8, :]
        da_ref[:, 0:DG] = dxa.astype(BF16)
        da_ref[:, DG:D] = dga.astype(BF16)

        @pl.when(i == nt - 1)
        def _():
            _, s_neg = _sig_pair(lam_v)
            sm_ref[3:4, :] = sm_ref[3:4, :] * (-s_neg)
            rg_ref[...] = _blocks_from_dense(dwa_s[...], dwx_s[...])

    vec = lambda: pl.BlockSpec((1, DG), lambda i: (0, 0))
    blocks = lambda: pl.BlockSpec((DG, RGB), lambda i: (0, 0))
    prev8 = lambda: pl.BlockSpec((8, DG), lambda i: (jnp.maximum((nt - 1 - i) * (tm // 8) - 1, 0), 0))
    return pl.pallas_call(
        body, name="bwd_rglru", grid=(nt,),
        in_specs=[pl.BlockSpec((tm, DG), lambda i: (nt - 1 - i, 0)),
                  pl.BlockSpec((tm, DG), lambda i: (nt - 1 - i, 0)),
                  prev8(),
                  pl.BlockSpec((tm, DG), lambda i: (nt - 1 - i, 0)),
                  prev8(),
                  pl.BlockSpec((tm, DG), lambda i: (nt - 1 - i, 0)),
                  pl.BlockSpec((NSHARD, 4, 128), lambda i: (0, 0, 0)), vec(),
                  blocks(), vec(), blocks(), vec(), vec(), ANY_SPEC],
        out_specs=[pl.BlockSpec((tm, D), lambda i: (nt - 1 - i, 0)),
                   pl.BlockSpec((DG, 128), lambda i: (0, 0)),
                   pl.BlockSpec((8, DG), lambda i: (0, 0))],
        out_shape=[jax.ShapeDtypeStruct((T, D), BF16), jax.ShapeDtypeStruct((DG, 128), F32),
                   jax.ShapeDtypeStruct((8, DG), F32)],
        scratch_shapes=[pltpu.VMEM((tm, DG), F32), pltpu.VMEM((tm, DG), F32),
                        pltpu.VMEM((8, DG), F32), pltpu.VMEM((8, DG), F32),
                        pltpu.VMEM((DG, DG), BF16), pltpu.VMEM((DG, DG), BF16),
                        pltpu.VMEM((DG, DG), F32), pltpu.VMEM((DG, DG), F32)],
        compiler_params=_cparams(("arbitrary",), 56),
    )(pf, pb, pf, h, h, dyab, cw3, conv_b, wa_c, ba, wx_c, bx, lam, dep)


def _bwd_hgrn2(pf, pb, o, s_prev, dyab, hg_lb, hg_nw, dep):
    tm = 512
    nt = T // tm
    nc = tm // CH

    def body(q_ref, f_ref, v_ref, g_ref, o_ref, sp_ref, dy_ref, lb_ref, nw_ref, dep_ref, db_ref, sm_ref,
             dst_s, eb_s, enb_s, ekd_s, qe_s, ke_s, kd_s, vb_s, do_s, dec_s, ddec_s, p_s, dp_s,
             g_s, dsta_s, dva_s, dqe_s, dke_s, dkd_s, dlf_s):
        i = pl.program_id(0)

        @pl.when(i == 0)
        def _():
            sm_ref[...] = jnp.zeros_like(sm_ref)
            dst_s[...] = jnp.zeros_like(dst_s)

        lb, one_m_lb = _hg_lower_bound(lb_ref)
        rows64 = lax.broadcasted_iota(I32, (tm, DG), 0) & (CH - 1)
        sg, sn, f, k, q, sq, qs = _hg_recompute(
            q_ref, f_ref, lb, one_m_lb, rows64, eb_s, enb_s, ekd_s, qe_s, ke_s, kd_s, dec_s)
        vb_s[...] = v_ref[...]

        nw = nw_ref[...]
        for hd in range(NH):
            cols = slice(hd * HD, (hd + 1) * HD)
            g = g_ref[:, cols].astype(F32)
            sgg = _sigmoid(g)
            o = o_ref[:, cols]
            so = lax.rsqrt(jnp.mean(o * o, axis=-1, keepdims=True) + EPS)
            oh = o * so
            dyb = dy_ref[:, cols]
            db_ref[:, 3 * DG + hd * HD:3 * DG + (hd + 1) * HD] = (
                dyb * (oh * nw) * (sgg * (1.0 + g * (1.0 - sgg)))).astype(BF16)
            don = dyb * (g * sgg)
            sm_ref[1:2, 0:HD] += _rowsum(don * oh)
            gw = don * nw
            do_s[:, cols] = (so * (gw - oh * jnp.mean(gw * oh, axis=-1, keepdims=True))).astype(BF16)

        mask = _tri(True)
        items = [(c, hd, slice(c * CH, (c + 1) * CH), slice(hd * HD, (hd + 1) * HD))
                 for c in range(nc) for hd in range(NH)]
        for c, hd, rs, cols in items:
            p_s[c * NH + hd] = jnp.where(mask, _mm_nt(qe_s[rs, cols], ke_s[rs, cols]), 0.0).astype(BF16)
            dp_s[c * NH + hd] = jnp.where(mask, _mm_nt(do_s[rs, cols], vb_s[rs, cols]), 0.0).astype(BF16)
        for c, hd, rs, cols in items:
            n = c * NH + hd
            dva_s[rs, cols] = _mm_tn(p_s[n], do_s[rs, cols])
            dqe_s[rs, cols] = _mm(dp_s[n], ke_s[rs, cols])
            dke_s[rs, cols] = _mm_tn(dp_s[n], qe_s[rs, cols])
            g_s[n] = _mm_tn(do_s[rs, cols], qe_s[rs, cols])
        for c, hd, rs, cols in reversed(items):
            n = c * NH + hd
            dst = dst_s[hd]
            dsta_s[n] = dst
            dst_s[hd] = dst * dec_s[c:c + 1, cols] + g_s[n]
        for c, hd, rs, cols in items:
            n = c * NH + hd
            dst = dsta_s[n]
            st_prev = sp_ref[hd, c]
            dv = dva_s[rs, cols] + _mm_nt(kd_s[rs, cols], dst)
            db_ref[rs, 2 * DG + hd * HD:2 * DG + (hd + 1) * HD] = dv.astype(BF16)
            dqe_s[rs, cols] += _mm(do_s[rs, cols], st_prev)
            dkd_s[rs, cols] = _mm(vb_s[rs, cols], dst)
            ddec_s[c:c + 1, cols] = _rowsum(dst * st_prev)

        eb, enb, ekd = eb_s[...], enb_s[...], ekd_s[...]
        dqe, dke, dkd = dqe_s[...], dke_s[...], dkd_s[...]
        t_kd = dkd * (k * ekd)
        rc = _chunk_rev_cumsum(dqe * (qs * eb) - dke * (k * enb) - t_kd, rows64)
        for c in range(nc):
            rs = slice(c * CH, (c + 1) * CH)
            dbl = _rowsum(t_kd[rs]) + ddec_s[c:c + 1, :] * dec_s[c:c + 1, :]
            dlf_s[rs, :] = rc[rs] + dbl
        t = dlf_s[...] / f - (dke * enb + dkd * ekd)
        db_ref[:, DG:2 * DG] = (one_m_lb * sg * sn * t).astype(BF16)
        sm_ref[0:1, :] += _rowsum(sn * t)
        db_ref[:, 0:DG] = (dqe * eb * (sq * (1.0 + q * (1.0 - sq))) * (HD ** -0.5)).astype(BF16)

        @pl.when(i == nt - 1)
        def _():
            dsm = sm_ref[0:1, :] * (lb * one_m_lb)
            sm_ref[2:3, :] = dsm
            sm_ref[3:4, :] = -dsm

    col = lambda j: pl.BlockSpec((tm, DG), lambda i: (nt - 1 - i, j))
    big = lambda dt: pltpu.VMEM((tm, DG), dt)
    return pl.pallas_call(
        body, name="bwd_hgrn2", grid=(nt,),
        in_specs=[col(1), col(1), col(2), col(3),
                  pl.BlockSpec((tm, DG), lambda i: (nt - 1 - i, 0)),
                  pl.BlockSpec((NH, nc, HD, HD), lambda i: (0, nt - 1 - i, 0, 0)),
                  pl.BlockSpec((tm, DG), lambda i: (nt - 1 - i, 1)),
                  pl.BlockSpec((2, DG), lambda i: (0, 0)),
                  pl.BlockSpec((1, HD), lambda i: (0, 0)), ANY_SPEC],
        out_specs=[pl.BlockSpec((tm, 4 * DG), lambda i: (nt - 1 - i, 0)),
                   pl.BlockSpec((8, DG), lambda i: (0, 0))],
        out_shape=[jax.ShapeDtypeStruct((T, 4 * DG), BF16), jax.ShapeDtypeStruct((8, DG), F32)],
        scratch_shapes=[pltpu.VMEM((NH, HD, HD), F32),
                        big(F32), big(F32), big(F32),
                        big(BF16), big(BF16), big(BF16), big(BF16), big(BF16),
                        pltpu.VMEM((nc, DG), F32), pltpu.VMEM((nc, DG), F32),
                        pltpu.VMEM((nc * NH, CH, CH), BF16), pltpu.VMEM((nc * NH, CH, CH), BF16),
                        pltpu.VMEM((nc * NH, HD, HD), F32), pltpu.VMEM((nc * NH, HD, HD), F32),
                        big(F32), big(F32), big(F32), big(F32), big(F32)],
        compiler_params=_cparams(("arbitrary",), 56),
    )(pb, pf, pb, pb, o, s_prev, dyab, hg_lb, hg_nw, dep)


def _dproj_pieces(k, da_ref, db_ref):
    if k == 0:
        return [(da_ref[:, 0:SHW], 0)]
    if k == 1:
        return [(da_ref[:, SHW:D], 0), (db_ref[:, 0:DG], D - SHW)]
    if k == 2:
        return [(db_ref[:, DG:DG + SHW], 0)]
    return [(db_ref[:, DG + SHW:4 * DG], 0)]


def _bwd_inproj_dx(x, dh1, d_a, d_b, w_in_b, nw, dep):
    tm = 512
    nt = T // tm

    def body(x_ref, dh1_ref, da_ref, db_ref, w_ref, nw_ref, dep_ref, dx_ref, sm_ref):
        i = pl.program_id(0)

        @pl.when(i == 0)
        def _():
            sm_ref[...] = jnp.zeros_like(sm_ref)

        du = None
        for k in range(NSHARD):
            for val, off in _dproj_pieces(k, da_ref, db_ref):
                t = _mm_nt(val, w_ref[k, :, off:off + val.shape[1]])
                du = t if du is None else du + t
        xv = x_ref[...]
        s = lax.rsqrt(jnp.mean(xv * xv, axis=-1, keepdims=True) + EPS)
        xh = xv * s
        sm_ref[0:1, :] += _rowsum(du * xh)
        g = du * nw_ref[...]
        dx_ref[...] = dh1_ref[...] + s * (g - xh * jnp.mean(g * xh, axis=-1, keepdims=True))

    row = lambda w: pl.BlockSpec((tm, w), lambda i: (i, 0))
    return pl.pallas_call(
        body, name="bwd_inproj_dx", grid=(nt,),
        in_specs=[row(D), row(D), row(D), row(4 * DG),
                  pl.BlockSpec((NSHARD, D, SHW), lambda i: (0, 0, 0), pipeline_mode=pl.Buffered(1)),
                  pl.BlockSpec((1, D), lambda i: (0, 0)), ANY_SPEC],
        out_specs=[row(D), pl.BlockSpec((8, D), lambda i: (0, 0))],
        out_shape=[jax.ShapeDtypeStruct((T, D), F32), jax.ShapeDtypeStruct((8, D), F32)],
        compiler_params=_cparams(("arbitrary",), 56),
    )(x, dh1, d_a, d_b, w_in_b, nw, dep)


def _bwd_inproj_dw(name, ks, u_b, d_a, d_b):
    tm = 1024
    nt = T // tm
    nk = len(ks)
    H = D // 2
    step = 128
    need_a = 0 in ks or 1 in ks

    def body(*refs):
        if need_a:
            u_ref, da_ref, db_ref, dw_ref, acc, send_s, recv_s, ssem, rsem = refs
        else:
            u_ref, db_ref, dw_ref, acc, send_s, recv_s, ssem, rsem = refs
            da_ref = None
        i = pl.program_id(0)

        @pl.when(i == 0)
        def _():
            acc[...] = jnp.zeros_like(acc)

        u = u_ref[...]
        for j, k in enumerate(ks):
            for val, off in _dproj_pieces(k, da_ref, db_ref):
                acc[j, :, off:off + val.shape[1]] += _mm_tn(u, val)

        @pl.when(i == nt - 1)
        def _():
            x, y, c = lax.axis_index("x"), lax.axis_index("y"), lax.axis_index("c")
            sibling = (x, y, 1 - c)
            mine0 = pl.multiple_of(c * H, step)
            other0 = pl.multiple_of((1 - c) * H, step)
            copies = []
            for j in range(nk):
                for r0 in range(0, H, step):
                    send_s[j, r0:r0 + step, :] = acc[j, pl.ds(other0 + r0, step), :].astype(BF16)
                cp = _remote(send_s.at[j], recv_s.at[j], ssem.at[j], rsem.at[j], sibling)
                cp.start()
                copies.append(cp)
            for j in range(nk):
                copies[j].wait_recv()
                for r0 in range(0, H, step):
                    s = acc[j, pl.ds(mine0 + r0, step), :] + recv_s[j, r0:r0 + step, :].astype(F32)
                    dw_ref[j, r0:r0 + step, :] = s.astype(BF16)
            for cp in copies:
                cp.wait_send()

    row = lambda w: pl.BlockSpec((tm, w), lambda i: (i, 0))
    ins = [u_b] + ([d_a] if need_a else []) + [d_b]
    in_specs = [row(D)] + ([row(D)] if need_a else []) + [row(4 * DG)]
    return pl.pallas_call(
        body, name=name, grid=(nt,), in_specs=in_specs,
        out_specs=pl.BlockSpec((nk, H, SHW), lambda i: (0, 0, 0)),
        out_shape=jax.ShapeDtypeStruct((nk, H, SHW), BF16),
        scratch_shapes=[pltpu.VMEM((nk, D, SHW), F32), pltpu.VMEM((nk, H, SHW), BF16),
                        pltpu.VMEM((nk, H, SHW), BF16),
                        pltpu.SemaphoreType.DMA((nk,)), pltpu.SemaphoreType.DMA((nk,))],
        compiler_params=_cparams(("arbitrary",), 48),
    )(*ins)


_OUT_ORDER = ["norm_mix_w", "w_in", "conv_w", "conv_b", "rg_wa", "rg_ba", "rg_wx", "rg_bx", "rg_lambda", "hg_lb",
              "hg_norm_w", "w_out", "ple_norm_w", "w_ple_gate", "b_ple_gate", "w_ple_proj", "final_norm_w"]
_BIG = ["w_in", "w_out", "w_ple_gate", "w_ple_proj"]


def _small_view(name, a):
    if name in ("rg_wa", "rg_wx"):
        return a.reshape(DG, RGB)
    if name == "conv_w":
        return a.reshape(4, 128)
    if name == "final_norm_w":
        return a.reshape(1, D)
    return a


def _landing(rows, cols):
    return lax.empty((NDEV, rows, cols), BF16)


def kernel(x, p, norm_mix_w, w_in, conv_w, conv_b, rg_wa, rg_ba, rg_wx, rg_bx, rg_lambda, hg_lb, hg_norm_w, w_out, ple_norm_w, w_ple_gate, b_ple_gate, w_ple_proj, final_norm_w, loss_target, m_norm_mix_w, m_w_in, m_conv_w, m_conv_b, m_rg_wa, m_rg_ba, m_rg_wx, m_rg_bx, m_rg_lambda, m_hg_lb, m_hg_norm_w, m_w_out, m_ple_norm_w, m_w_ple_gate, m_b_ple_gate, m_w_ple_proj, m_final_norm_w, v_norm_mix_w, v_w_in, v_conv_w, v_conv_b, v_rg_wa, v_rg_ba, v_rg_wx, v_rg_bx, v_rg_lambda, v_hg_lb, v_hg_norm_w, v_w_out, v_ple_norm_w, v_w_ple_gate, v_b_ple_gate, v_w_ple_proj, v_final_norm_w):
    given = dict(locals())
    x2, p2, tgt = x[0], p[0, 0], loss_target[0]
    wa_c, wx_c = _small_view("rg_wa", rg_wa), _small_view("rg_wx", rg_wx)

    w_in_b, l_out, l_pg, l_pp, cw3 = _gather_w_in(w_in[0], w_out[0], w_ple_gate[0], w_ple_proj[0], conv_w[0])
    g_ssem, g_rsem, g_lands, tok = _gather_rest_start([l_out, l_pg, l_pp])

    pf, pb, u_b = _fwd_inproj(x2, norm_mix_w, w_in_b, tok)
    h, ya = _fwd_rglru(pf, pb, cw3, conv_b, wa_c, rg_ba, wx_c, rg_bx, rg_lambda)
    yb, o, s_prev = _fwd_hgrn2(pf, pb, hg_lb, hg_norm_w)
    w_out_b, w_pg_b, w_pp_b = _gather_rest_wait(g_ssem, g_rsem, g_lands, yb)
    dh1, dyab, dwo_b, dwg_b, dwp_b, sm_tail = _tail_fwd_bwd(
        x2, p2, tgt, ya, yb, w_out_b.reshape(D, D), w_pg_b.reshape(D, D), w_pp_b,
        ple_norm_w, b_ple_gate, final_norm_w.reshape(1, D))

    QH = D // NSHARD // 2
    r1 = _rs_start("rs_start_tail", [dwo_b, dwg_b, dwp_b], (0, 1, 2, 3),
                   [_landing(QH, D), _landing(QH, D), _landing(PLE // 2, PLE)])
    d_b, sm_b = _bwd_hgrn2(pf, pb, o, s_prev, dyab, hg_lb, hg_norm_w, r1[4])
    dw23 = _bwd_inproj_dw("bwd_inproj_dw23", (2, 3), u_b, None, d_b)
    r2 = _rs_start("rs_start_in23", [dw23], (2, 3), [lax.empty((NSHARD, D // 2, SHW), BF16)], chip_sums=True)
    d_a, rg_c, sm_a = _bwd_rglru(pf, pb, h, dyab, cw3, conv_b, wa_c, rg_ba, wx_c, rg_bx, rg_lambda, r2[4])
    dw01 = _bwd_inproj_dw("bwd_inproj_dw01", (0, 1), u_b, d_a, d_b)
    r3 = _rs_start("rs_start_in01", [dw01], (0, 1), r2[3], chip_sums=True)
    grad_x, sm_in = _bwd_inproj_dx(x2, dh1, d_a, d_b, w_in_b, norm_mix_w, r3[4])

    parts1, lands1 = _rs_wait("rs_wait_tail", r1[0], r1[1], r1[2], (0, 1, 2, 3), r1[3], sm_in)
    parts2, lands2 = _rs_wait("rs_wait_in23", r2[0], r2[1], r2[2], (2, 3), r3[3], sm_in, chip_sums=True)
    parts3, lands3 = _rs_wait("rs_wait_in01", r3[0], r3[1], r3[2], (0, 1), lands2, sm_in, chip_sums=True)
    g_big, vred, rgred = _final_reduce([parts3[0], parts2[0]] + parts1, lands3 + lands1,
                                       sm_in, sm_tail, sm_a, sm_b, rg_c)

    upd_big = _adam_big(g_big, [given[n][0] for n in _BIG], [given["m_" + n][0] for n in _BIG],
                        [given["v_" + n][0] for n in _BIG])
    small = _adam_small(vred, rgred,
                        {n: _small_view(n, given[n]) for n in _SMALL_ORDER},
                        {n: _small_view(n, given["m_" + n]) for n in _SMALL_ORDER},
                        {n: _small_view(n, given["v_" + n]) for n in _SMALL_ORDER})

    loss = vred[0, ROW_LOSS, 0]
    outs = [loss, grad_x[None]]
    for ki in range(4):
        for n in _OUT_ORDER:
            if n in _BIG:
                i = _BIG.index(n)
                a = g_big[i] if ki == 0 else upd_big[i][ki - 1]
                outs.append(a[None])
            else:
                outs.append(small[n][ki].reshape(given[n].shape))
    return tuple(outs)
```

```python
import jax
import jax.numpy as jnp
from jax import lax
from jax.experimental import pallas as pl
from jax.experimental.pallas import tpu as pltpu

F32 = jnp.float32
BF16 = jnp.bfloat16
I32 = jnp.int32
MESH = pl.DeviceIdType.MESH
HIGHEST = lax.Precision.HIGHEST

T = 4096
D = 1024
DG = 512
DIN = 3072
PLE = 256
NH = 4
HD = 128
CH = 64
NCHUNK = T // CH
RGB = 64
EPS = 1e-6
RG_C = 8.0
NSHARD = 4
SHW = DIN // NSHARD
NDEV = 8

ADAM_LR = 0.001
ADAM_B1 = 0.9
ADAM_B2 = 0.999
ADAM_EPS = 1e-08
ADAM_WD = 0.01
ADAM_STEP = 10

VMEM_SPEC = pl.BlockSpec(memory_space=pltpu.VMEM)
HBM_SPEC = pl.BlockSpec(memory_space=pltpu.HBM)
SEM_SPEC = pl.BlockSpec(memory_space=pltpu.SEMAPHORE)
ANY_SPEC = pl.BlockSpec(memory_space=pl.ANY)
EFFECT = pltpu.SideEffectType.DATAFLOW_SIDE_EFFECTING
MIB = 1024 * 1024

VROWS = 16
ROW_NORM_MIX, ROW_FINAL_NORM, ROW_B_PG, ROW_PLE_NORM = 0, 1, 2, 3
ROW_CB_BA, ROW_BX_LAM, ROW_CW01, ROW_CW23, ROW_HG_LB, ROW_HG_NW, ROW_LOSS = 4, 5, 6, 7, 8, 9, 10


def _mm(a, b):
    return jnp.dot(a.astype(BF16), b.astype(BF16), preferred_element_type=F32)


def _mm_nt(a, b):
    return lax.dot_general(a.astype(BF16), b.astype(BF16), (((1,), (1,)), ((), ())),
                           preferred_element_type=F32)


def _mm_tn(a, b):
    return lax.dot_general(a.astype(BF16), b.astype(BF16), (((0,), (0,)), ((), ())),
                           preferred_element_type=F32)


def _mm_exact(a, b):
    return jnp.dot(a, b, precision=HIGHEST, preferred_element_type=F32)


def _sig_pair(x):
    e = jnp.exp(-jnp.abs(x))
    big = 1.0 / (1.0 + e)
    small = e * big
    pos = x >= 0
    return jnp.where(pos, big, small), jnp.where(pos, small, big)


def _sigmoid(x):
    return 1.0 / (1.0 + jnp.exp(-x))


def _rowsum(v):
    return jnp.sum(v, axis=0, keepdims=True)


def _shift_down(cur, prev8, d, rows):
    rolled = pltpu.roll(cur, d, 0)
    head = jnp.where(rows[0:8] < d, pltpu.roll(prev8, d, 0), rolled[0:8])
    return jnp.concatenate([head, rolled[8:]], axis=0)


def _shift_up(cur, next8, d, rows):
    n = cur.shape[0]
    rolled = pltpu.roll(cur, n - d, 0)
    tail = jnp.where(rows[0:8] >= 8 - d, pltpu.roll(next8, 8 - d, 0), rolled[n - 8:n])
    return jnp.concatenate([rolled[0:n - 8], tail], axis=0)


def _roll_in_groups(v, d):
    n, w = v.shape
    return pltpu.roll(v.reshape(n // 8, 8, w), d, 1).reshape(n, w)


def _cparams(sem, vmem_mib):
    return pltpu.CompilerParams(dimension_semantics=sem, vmem_limit_bytes=vmem_mib * MIB)


def _mesh_pos():
    x, y, c = lax.axis_index("x"), lax.axis_index("y"), lax.axis_index("c")
    chips = [(1 - x, y), (x, 1 - y), (1 - x, 1 - y)]
    return x, y, c, chips


def _remote(src, dst, ssem, rsem, dev):
    return pltpu.make_async_remote_copy(src_ref=src, dst_ref=dst, send_sem=ssem, recv_sem=rsem,
                                        device_id=dev, device_id_type=MESH)


def _gather_w_in(w_in, w_out, w_pg, w_pp, conv_w):
    shapes = [w_in.shape, w_out.shape, w_pg.shape, w_pp.shape]

    def body(win, wout, wpg, wpp, cw, o_in, o_out, o_pg, o_pp, o_cw, ssem, rsem):
        x, y, c, chips = _mesh_pos()
        kme = 2 * x + y
        sibling = (x, y, 1 - c)

        def cast(src, dst):
            for r0 in range(0, src.shape[0], 128):
                dst[kme, r0:r0 + 128, :] = src[r0:r0 + 128, :].astype(BF16)

        cast(win, o_in)
        o_cw[kme] = cw[...]

        hrows = D // 2
        mine = pl.ds(pl.multiple_of(c * hrows, 128), hrows)
        other = pl.ds(pl.multiple_of((1 - c) * hrows, 128), hrows)

        def half(k, rows):
            return o_in.at[k, rows]

        na = (x ^ c, y ^ (1 - c), c)
        nb = (x ^ (1 - c), y ^ c, c)
        ka = 2 * na[0] + na[1]
        kb = 2 * nb[0] + nb[1]
        kd = 2 * (1 - x) + (1 - y)
        sends = [_remote(half(kme, mine), half(kme, mine), ssem.at[0], rsem.at[0], na),
                 _remote(half(kme, mine), half(kme, mine), ssem.at[1], rsem.at[1], nb)]
        for j, (px, py) in enumerate(chips):
            sends.append(_remote(o_cw.at[kme], o_cw.at[kme], ssem.at[3 + j], rsem.at[3 + j], (px, py, c)))
        for cp in sends:
            cp.start()
        cast(wout, o_out)
        cast(wpg, o_pg)
        cast(wpp, o_pp)

        def passed_on(k, sem, dev):
            cp = _remote(half(k, mine), half(k, mine), ssem.at[sem], rsem.at[sem], dev)
            cp.start()
            sends.append(cp)

        _remote(half(ka, mine), half(ka, mine), ssem.at[0], rsem.at[0], na).wait_recv()
        passed_on(ka, 2, nb)
        passed_on(ka, 6, sibling)
        _remote(half(kb, mine), half(kb, mine), ssem.at[1], rsem.at[1], nb).wait_recv()
        passed_on(kb, 7, sibling)
        _remote(half(kd, mine), half(kd, mine), ssem.at[2], rsem.at[2], nb).wait_recv()
        passed_on(kd, 8, sibling)
        _remote(half(kb, other), half(kb, other), ssem.at[6], rsem.at[6], sibling).wait_recv()
        _remote(half(ka, other), half(ka, other), ssem.at[7], rsem.at[7], sibling).wait_recv()
        _remote(half(kd, other), half(kd, other), ssem.at[8], rsem.at[8], sibling).wait_recv()
        for j, (px, py) in enumerate(chips):
            kj = 2 * px + py
            _remote(o_cw.at[kj], o_cw.at[kj], ssem.at[3 + j], rsem.at[3 + j], (px, py, c)).wait_recv()
        for cp in sends:
            cp.wait_send()

    out_shape = [jax.ShapeDtypeStruct((NSHARD,) + s, BF16) for s in shapes]
    out_shape.append(jax.ShapeDtypeStruct((NSHARD,) + conv_w.shape, F32))
    return pl.pallas_call(
        body, name="gather_w_in", out_shape=out_shape,
        in_specs=[VMEM_SPEC] * 5, out_specs=[VMEM_SPEC] * 5,
        scratch_shapes=[pltpu.SemaphoreType.DMA((9,)), pltpu.SemaphoreType.DMA((9,))],
        compiler_params=pltpu.CompilerParams(vmem_limit_bytes=40 * MIB),
    )(w_in, w_out, w_pg, w_pp, conv_w)


def _gather_rest_start(lands):
    n = len(lands)

    def body(*refs):
        land_in = refs[0:n]
        ssem, rsem = refs[n], refs[n + 1]
        token = refs[2 * n + 2]
        x, y, c, chips = _mesh_pos()
        kme = 2 * x + y
        for p, land in enumerate(land_in):
            hrows = land.shape[1] // 2
            mine = pl.ds(pl.multiple_of(c * hrows, 128), hrows)
            for px, py in chips:
                for pc in range(2):
                    _remote(land.at[kme, mine], land.at[kme, mine], ssem.at[p], rsem.at[p], (px, py, pc)).start()
        token[...] = jnp.zeros_like(token)

    out_shape = ([pltpu.SemaphoreType.DMA((n,)), pltpu.SemaphoreType.DMA((n,))]
                 + [pltpu.HBM(a.shape, a.dtype) for a in lands] + [jax.ShapeDtypeStruct((8, 128), F32)])
    outs = pl.pallas_call(
        body, name="gather_rest_start", out_shape=out_shape,
        in_specs=[HBM_SPEC] * n, out_specs=[SEM_SPEC, SEM_SPEC] + [HBM_SPEC] * n + [VMEM_SPEC],
        input_output_aliases={i: 2 + i for i in range(n)},
        compiler_params=pltpu.CompilerParams(has_side_effects=EFFECT),
    )(*[pltpu.with_memory_space_constraint(a, pltpu.HBM) for a in lands])
    return outs[0], outs[1], list(outs[2:2 + n]), outs[2 + n]


def _gather_rest_wait(ssem, rsem, lands, after):
    n = len(lands)

    def body(*refs):
        land_in = refs[0:n]
        ssem_ref, rsem_ref = refs[n], refs[n + 1]
        x, y, c = lax.axis_index("x"), lax.axis_index("y"), lax.axis_index("c")
        for p, land in enumerate(land_in):
            three = land.at[pl.ds(0, 3)]
            cp = _remote(three, three, ssem_ref.at[p], rsem_ref.at[p], (x, y, c))
            cp.wait_send()
            cp.wait_recv()

    outs = pl.pallas_call(
        body, name="gather_rest_wait", out_shape=[pltpu.HBM(a.shape, a.dtype) for a in lands],
        in_specs=[HBM_SPEC] * n + [SEM_SPEC, SEM_SPEC, ANY_SPEC], out_specs=[HBM_SPEC] * n,
        input_output_aliases={i: i for i in range(n)},
        compiler_params=pltpu.CompilerParams(has_side_effects=EFFECT),
    )(*lands, ssem, rsem, after)
    return list(outs)


def _rs_start(name, parts, ks, lands, chip_sums=False):
    n = len(parts)

    def body(*refs):
        part_in, land_in = refs[0:n], refs[n:2 * n]
        ssem, rsem = refs[2 * n], refs[2 * n + 1]
        token = refs[4 * n + 2]
        x, y, c = lax.axis_index("x"), lax.axis_index("y"), lax.axis_index("c")
        kme = 2 * x + y
        me = 4 * x + 2 * y + c
        for p in range(n):
            hrows = land_in[p].shape[1]
            for i, k in enumerate(ks):
                if chip_sums:
                    @pl.when(kme != k)
                    def _():
                        _remote(part_in[p].at[i], land_in[p].at[kme], ssem.at[p], rsem.at[p], (k // 2, k % 2, c)).start()
                    continue
                for pc in range(2):
                    @pl.when(jnp.logical_or(kme != k, c != pc))
                    def _():
                        _remote(part_in[p].at[i, pl.ds(pc * hrows, hrows)], land_in[p].at[me],
                                ssem.at[p], rsem.at[p], (k // 2, k % 2, pc)).start()
        token[...] = jnp.zeros_like(token)

    arrays = list(parts) + list(lands)
    out_shape = ([pltpu.SemaphoreType.DMA((n,)), pltpu.SemaphoreType.DMA((n,))]
                 + [pltpu.HBM(a.shape, a.dtype) for a in arrays] + [jax.ShapeDtypeStruct((8, 128), F32)])
    outs = pl.pallas_call(
        body, name=name, out_shape=out_shape,
        in_specs=[HBM_SPEC] * (2 * n), out_specs=[SEM_SPEC, SEM_SPEC] + [HBM_SPEC] * (2 * n) + [VMEM_SPEC],
        input_output_aliases={i: 2 + i for i in range(2 * n)},
        compiler_params=pltpu.CompilerParams(has_side_effects=EFFECT),
    )(*[pltpu.with_memory_space_constraint(a, pltpu.HBM) for a in arrays])
    return outs[0], outs[1], list(outs[2:2 + n]), list(outs[2 + n:2 + 2 * n]), outs[2 + 2 * n]


def _rs_wait(name, ssem, rsem, parts, ks, lands, after, chip_sums=False):
    n = len(parts)

    def body(*refs):
        part_in, land_in = refs[0:n], refs[n:2 * n]
        ssem_ref, rsem_ref = refs[2 * n], refs[2 * n + 1]
        x, y, c = lax.axis_index("x"), lax.axis_index("y"), lax.axis_index("c")
        kme = 2 * x + y
        for p in range(n):
            piece = land_in[p].at[0]
            for k in ks:
                for pc in range(1 if chip_sums else 2):
                    mine = (kme == k) if chip_sums else jnp.logical_and(kme == k, c == pc)

                    @pl.when(jnp.logical_not(mine))
                    def _():
                        _remote(piece, piece, ssem_ref.at[p], rsem_ref.at[p], (x, y, c)).wait_send()
            owner = kme == ks[0]
            for k in ks[1:]:
                owner = jnp.logical_or(owner, kme == k)

            @pl.when(owner)
            def _():
                others = land_in[p].at[pl.ds(0, land_in[p].shape[0] - 1)]
                _remote(others, others, ssem_ref.at[p], rsem_ref.at[p], (x, y, c)).wait_recv()

    arrays = list(parts) + list(lands)
    outs = pl.pallas_call(
        body, name=name, out_shape=[pltpu.HBM(a.shape, a.dtype) for a in arrays],
        in_specs=[HBM_SPEC] * (2 * n) + [SEM_SPEC, SEM_SPEC, ANY_SPEC], out_specs=[HBM_SPEC] * (2 * n),
        input_output_aliases={i: i for i in range(2 * n)},
        compiler_params=pltpu.CompilerParams(has_side_effects=EFFECT),
    )(*arrays, ssem, rsem, after)
    return list(outs[0:n]), list(outs[n:2 * n])


def _reduce_exchange(parts, lands):
    shapes = [(2 * l.shape[1], l.shape[2]) for l in lands]
    step = 128

    def body(in01, in23, pout, ppg, ppp, l_in, l_out, l_pg, l_pp, g_in, g_out, g_pg, g_pp, ssem, rsem):
        x, y, c = lax.axis_index("x"), lax.axis_index("y"), lax.axis_index("c")
        kme = 2 * x + y
        me = 4 * x + 2 * y + c
        sibling = (x, y, 1 - c)
        sends = []
        for p, (land, gout) in enumerate([(l_in, g_in), (l_out, g_out), (l_pg, g_pg), (l_pp, g_pp)]):
            hrows = land.shape[1]
            mine0 = pl.multiple_of(c * hrows, step)
            for r0 in range(0, hrows, step):
                rs = pl.ds(mine0 + r0, step)
                if p == 0:
                    own = jnp.where(kme >= 2, in23[kme & 1, r0:r0 + step, :], in01[kme & 1, r0:r0 + step, :])
                    slot = kme
                else:
                    own = (pout, ppg, ppp)[p - 1][kme, rs, :]
                    slot = me
                s = jnp.zeros((step, land.shape[2]), F32)
                for j in range(land.shape[0]):
                    s = s + jnp.where(slot == j, own, land[j, r0:r0 + step, :]).astype(F32)
                gout[rs, :] = s
            mine = pl.ds(mine0, hrows)
            cp = _remote(gout.at[mine], gout.at[mine], ssem.at[p], rsem.at[p], sibling)
            cp.start()
            sends.append(cp)
        for p, (land, gout) in enumerate([(l_in, g_in), (l_out, g_out), (l_pg, g_pg), (l_pp, g_pp)]):
            hrows = land.shape[1]
            other = pl.ds(pl.multiple_of((1 - c) * hrows, step), hrows)
            _remote(gout.at[other], gout.at[other], ssem.at[p], rsem.at[p], sibling).wait_recv()
        for cp in sends:
            cp.wait_send()

    return pl.pallas_call(
        body, name="reduce_exchange", out_shape=[jax.ShapeDtypeStruct(s, F32) for s in shapes],
        in_specs=[VMEM_SPEC] * 9, out_specs=[VMEM_SPEC] * 4,
        scratch_shapes=[pltpu.SemaphoreType.DMA((4,)), pltpu.SemaphoreType.DMA((4,))],
        compiler_params=pltpu.CompilerParams(vmem_limit_bytes=48 * MIB),
    )(*parts, *lands)


def _small_allreduce(sm_in, sm_tail, sm_a, sm_b, rg_c):
    PR = DG // NDEV

    def body(in_ref, tail_ref, a_ref, b_ref, rg_ref, v_out, rg_out, vbuf, vrecv, rgrecv, ssem, rsem):
        x, y, c = lax.axis_index("x"), lax.axis_index("y"), lax.axis_index("c")
        me = 4 * x + 2 * y + c

        def pair(ref, r0, r1):
            return jnp.concatenate([ref[r0:r0 + 1, :], ref[r1:r1 + 1, :]], axis=1)

        rows = {
            ROW_NORM_MIX: in_ref[0:1, :], ROW_FINAL_NORM: tail_ref[1:2, :], ROW_B_PG: tail_ref[2:3, :],
            ROW_PLE_NORM: tail_ref[3:4, :], ROW_CB_BA: pair(a_ref, 0, 1), ROW_BX_LAM: pair(a_ref, 2, 3),
            ROW_CW01: pair(a_ref, 4, 5), ROW_CW23: pair(a_ref, 6, 7), ROW_HG_LB: pair(b_ref, 2, 3),
            ROW_HG_NW: jnp.concatenate([b_ref[1:2, :], jnp.zeros((1, DG), F32)], axis=1),
            ROW_LOSS: tail_ref[4:5, :],
        }
        vbuf[...] = jnp.zeros_like(vbuf)
        for r, row in rows.items():
            for j in range(NDEV):
                vbuf[j, r:r + 1, :] = row[:, j * 128:(j + 1) * 128]

        def peer(mask):
            px = x ^ ((mask >> 2) & 1)
            py = y ^ ((mask >> 1) & 1)
            pc = c ^ (mask & 1)
            return (px, py, pc), 4 * px + 2 * py + pc

        def rg_rows(r):
            return pl.ds(pl.multiple_of(r * PR, PR), PR)

        first = []
        for mask in range(1, NDEV):
            dev, r = peer(mask)
            i = mask - 1
            cp = _remote(vbuf.at[r], vrecv.at[i], ssem.at[i], rsem.at[i], dev)
            cp.start()
            first.append(cp)
            cp = _remote(rg_ref.at[rg_rows(r)], rgrecv.at[i], ssem.at[7 + i], rsem.at[7 + i], dev)
            cp.start()
            first.append(cp)
        sv = vbuf[me]
        sr = rg_ref[rg_rows(me), :]
        for i in range(NDEV - 1):
            first[2 * i].wait_recv()
            first[2 * i + 1].wait_recv()
            sv = sv + vrecv[i]
            sr = sr + rgrecv[i]
        v_out[me] = sv
        rg_out[rg_rows(me), :] = sr
        second = []
        for mask in range(1, NDEV):
            dev, r = peer(mask)
            i = mask - 1
            cp = _remote(v_out.at[me], v_out.at[me], ssem.at[14 + i], rsem.at[14 + i], dev)
            cp.start()
            second.append(cp)
            cp = _remote(rg_out.at[rg_rows(me)], rg_out.at[rg_rows(me)], ssem.at[21 + i], rsem.at[21 + i], dev)
            cp.start()
            second.append(cp)
        for mask in range(1, NDEV):
            dev, r = peer(mask)
            i = mask - 1
            _remote(v_out.at[r], v_out.at[r], ssem.at[14 + i], rsem.at[14 + i], dev).wait_recv()
            _remote(rg_out.at[rg_rows(r)], rg_out.at[rg_rows(r)], ssem.at[21 + i], rsem.at[21 + i], dev).wait_recv()
        for cp in first + second:
            cp.wait_send()

    return pl.pallas_call(
        body, name="small_allreduce",
        out_shape=[jax.ShapeDtypeStruct((NDEV, VROWS, 128), F32), jax.ShapeDtypeStruct((DG, 128), F32)],
        in_specs=[VMEM_SPEC] * 5, out_specs=[VMEM_SPEC] * 2,
        scratch_shapes=[pltpu.VMEM((NDEV, VROWS, 128), F32), pltpu.VMEM((NDEV - 1, VROWS, 128), F32),
                        pltpu.VMEM((NDEV - 1, PR, 128), F32),
                        pltpu.SemaphoreType.DMA((28,)), pltpu.SemaphoreType.DMA((28,))],
    )(sm_in, sm_tail, sm_a, sm_b, rg_c)


def _final_reduce(parts, lands, sm_in, sm_tail, sm_a, sm_b, rg_c):
    shapes = [(2 * l.shape[1], l.shape[2]) for l in lands]
    step = 128
    PR = DG // NDEV

    def body(in01, in23, pout, ppg, ppp, l_in, l_out, l_pg, l_pp, in_ref, tail_ref, a_ref, b_ref, rg_ref,
             g_in, g_out, g_pg, g_pp, v_out, rg_out, vbuf, vrecv, rgrecv,
             lb_in, lb_out, lb_pg, lb_pp, ob_in, ob_out, ob_pg, ob_pp, lsem, bsem_s, bsem_r, ssem, rsem):
        x, y, c = lax.axis_index("x"), lax.axis_index("y"), lax.axis_index("c")
        kme = 2 * x + y
        me = 4 * x + 2 * y + c
        sibling = (x, y, 1 - c)

        lands_hbm = [l_in, l_out, l_pg, l_pp]
        land_bufs = [lb_in, lb_out, lb_pg, lb_pp]
        own_bufs = [ob_in, ob_out, ob_pg, ob_pp]
        fetches = []
        for p in range(4):
            cp = pltpu.make_async_copy(lands_hbm[p], land_bufs[p], lsem.at[p])
            cp.start()
            fetches.append(cp)

        @pl.when(kme >= 2)
        def _():
            pltpu.make_async_copy(in23.at[kme & 1], ob_in, lsem.at[4]).start()

        @pl.when(kme < 2)
        def _():
            pltpu.make_async_copy(in01.at[kme & 1], ob_in, lsem.at[4]).start()

        own_fetches = [pltpu.make_async_copy(in01.at[0], ob_in, lsem.at[4])]
        for p, part in enumerate([pout, ppg, ppp]):
            hrows = own_bufs[p + 1].shape[0]
            cp = pltpu.make_async_copy(part.at[kme, pl.ds(pl.multiple_of(c * hrows, step), hrows)],
                                       own_bufs[p + 1], lsem.at[5 + p])
            cp.start()
            own_fetches.append(cp)

        def pair(ref, r0, r1):
            return jnp.concatenate([ref[r0:r0 + 1, :], ref[r1:r1 + 1, :]], axis=1)

        rows = {
            ROW_NORM_MIX: in_ref[0:1, :], ROW_FINAL_NORM: tail_ref[1:2, :], ROW_B_PG: tail_ref[2:3, :],
            ROW_PLE_NORM: tail_ref[3:4, :], ROW_CB_BA: pair(a_ref, 0, 1), ROW_BX_LAM: pair(a_ref, 2, 3),
            ROW_CW01: pair(a_ref, 4, 5), ROW_CW23: pair(a_ref, 6, 7), ROW_HG_LB: pair(b_ref, 2, 3),
            ROW_HG_NW: jnp.concatenate([b_ref[1:2, :], jnp.zeros((1, DG), F32)], axis=1),
            ROW_LOSS: tail_ref[4:5, :],
        }
        vbuf[...] = jnp.zeros_like(vbuf)
        for r, row in rows.items():
            for j in range(NDEV):
                vbuf[j, r:r + 1, :] = row[:, j * 128:(j + 1) * 128]

        def peer(mask):
            px = x ^ ((mask >> 2) & 1)
            py = y ^ ((mask >> 1) & 1)
            pc = c ^ (mask & 1)
            return (px, py, pc), 4 * px + 2 * py + pc

        def rg_rows(r):
            return pl.ds(pl.multiple_of(r * PR, PR), PR)

        first = []
        for mask in range(1, NDEV):
            dev, r = peer(mask)
            i = mask - 1
            cp = _remote(vbuf.at[r], vrecv.at[i], ssem.at[i], rsem.at[i], dev)
            cp.start()
            first.append(cp)
            cp = _remote(rg_ref.at[rg_rows(r)], rgrecv.at[i], ssem.at[7 + i], rsem.at[7 + i], dev)
            cp.start()
            first.append(cp)

        big = [(lb_in, g_in), (lb_out, g_out), (lb_pg, g_pg), (lb_pp, g_pp)]
        swaps = []
        for p, (land, gout) in enumerate(big):
            fetches[p].wait()
            own_fetches[p].wait()
            hrows = land.shape[1]
            mine0 = pl.multiple_of(c * hrows, step)
            slot = kme if p == 0 else me
            for r0 in range(0, hrows, step):
                rs = pl.ds(mine0 + r0, step)
                own = own_bufs[p][r0:r0 + step, :]
                s = jnp.zeros((step, land.shape[2]), F32)
                for j in range(land.shape[0]):
                    s = s + jnp.where(slot == j, own, land[j, r0:r0 + step, :]).astype(F32)
                gout[rs, :] = s
            mine = pl.ds(mine0, hrows)
            cp = _remote(gout.at[mine], gout.at[mine], bsem_s.at[p], bsem_r.at[p], sibling)
            cp.start()
            swaps.append(cp)

        sv = vbuf[me]
        sr = rg_ref[rg_rows(me), :]
        for i in range(NDEV - 1):
            first[2 * i].wait_recv()
            first[2 * i + 1].wait_recv()
            sv = sv + vrecv[i]
            sr = sr + rgrecv[i]
        v_out[me] = sv
        rg_out[rg_rows(me), :] = sr
        second = []
        for mask in range(1, NDEV):
            dev, r = peer(mask)
            i = mask - 1
            cp = _remote(v_out.at[me], v_out.at[me], ssem.at[14 + i], rsem.at[14 + i], dev)
            cp.start()
            second.append(cp)
            cp = _remote(rg_out.at[rg_rows(me)], rg_out.at[rg_rows(me)], ssem.at[21 + i], rsem.at[21 + i], dev)
            cp.start()
            second.append(cp)
        for p, (land, gout) in enumerate(big):
            hrows = land.shape[1]
            other = pl.ds(pl.multiple_of((1 - c) * hrows, step), hrows)
            _remote(gout.at[other], gout.at[other], bsem_s.at[p], bsem_r.at[p], sibling).wait_recv()
        for mask in range(1, NDEV):
            dev, r = peer(mask)
            i = mask - 1
            _remote(v_out.at[r], v_out.at[r], ssem.at[14 + i], rsem.at[14 + i], dev).wait_recv()
            _remote(rg_out.at[rg_rows(r)], rg_out.at[rg_rows(r)], ssem.at[21 + i], rsem.at[21 + i], dev).wait_recv()
        for cp in first + swaps + second:
            cp.wait_send()

    out_shape = [jax.ShapeDtypeStruct(s, F32) for s in shapes]
    out_shape += [jax.ShapeDtypeStruct((NDEV, VROWS, 128), F32), jax.ShapeDtypeStruct((DG, 128), F32)]
    outs = pl.pallas_call(
        body, name="final_reduce", out_shape=out_shape,
        in_specs=[ANY_SPEC] * 9 + [VMEM_SPEC] * 5, out_specs=[VMEM_SPEC] * 6,
        scratch_shapes=[pltpu.VMEM((NDEV, VROWS, 128), F32), pltpu.VMEM((NDEV - 1, VROWS, 128), F32),
                        pltpu.VMEM((NDEV - 1, PR, 128), F32)]
        + [pltpu.VMEM(l.shape, BF16) for l in lands]
        + [pltpu.VMEM(l.shape[1:], BF16) for l in lands]
        + [pltpu.SemaphoreType.DMA((8,)),
                        pltpu.SemaphoreType.DMA((4,)), pltpu.SemaphoreType.DMA((4,)),
                        pltpu.SemaphoreType.DMA((28,)), pltpu.SemaphoreType.DMA((28,))],
        compiler_params=pltpu.CompilerParams(vmem_limit_bytes=48 * MIB),
    )(*parts, *lands, sm_in, sm_tail, sm_a, sm_b, rg_c)
    return list(outs[0:4]), outs[4], outs[5]


def _adam_rows(w, g, m, v):
    m2 = ADAM_B1 * m + (1.0 - ADAM_B1) * g
    v2 = ADAM_B2 * v + (1.0 - ADAM_B2) * (g * g)
    m_hat = m2 / (1.0 - ADAM_B1 ** ADAM_STEP)
    v_hat = v2 / (1.0 - ADAM_B2 ** ADAM_STEP)
    delta = -ADAM_LR * (m_hat / (jnp.sqrt(v_hat) + ADAM_EPS) + ADAM_WD * w)
    return delta, m2, v2


def _adam_big(gs, ws, ms, vs):
    n = len(gs)
    steps = 8

    def body(*refs):
        ins, outs = refs[:4 * n], refs[4 * n:]
        for i in range(n):
            g, w, m, v = (r[...] for r in ins[4 * i:4 * i + 4])
            d, m2, v2 = _adam_rows(w, g, m, v)
            outs[3 * i][...] = d
            outs[3 * i + 1][...] = m2
            outs[3 * i + 2][...] = v2

    in_specs, out_specs, out_shape, args = [], [], [], []
    for g, w, m, v in zip(gs, ws, ms, vs):
        r, c = w.shape
        spec = lambda: pl.BlockSpec((r // steps, c), lambda i: (i, 0))
        in_specs += [spec() for _ in range(4)]
        out_specs += [spec() for _ in range(3)]
        out_shape += [jax.ShapeDtypeStruct((r, c), F32)] * 3
        args += [g, w, m, v]
    outs = pl.pallas_call(
        body, name="adam_big", grid=(steps,), in_specs=in_specs, out_specs=out_specs, out_shape=out_shape,
        compiler_params=_cparams(("parallel",), 32),
    )(*args)
    return [tuple(outs[3 * i:3 * i + 3]) for i in range(n)]


_VEC_PARAMS = [
    ("norm_mix_w", ROW_NORM_MIX, 0, D), ("final_norm_w", ROW_FINAL_NORM, 0, D),
    ("b_ple_gate", ROW_B_PG, 0, D), ("ple_norm_w", ROW_PLE_NORM, 0, D),
    ("conv_b", ROW_CB_BA, 0, DG), ("rg_ba", ROW_CB_BA, DG, DG),
    ("rg_bx", ROW_BX_LAM, 0, DG), ("rg_lambda", ROW_BX_LAM, DG, DG),
    ("hg_norm_w", ROW_HG_NW, 0, HD),
]
_SMALL_ORDER = [n for n, _, _, _ in _VEC_PARAMS] + ["hg_lb", "conv_w", "rg_wa", "rg_wx"]


def _adam_small(vred, rgred, ws, ms, vs):
    names = _SMALL_ORDER
    n = len(names)

    def body(vred_ref, rg_ref, *refs):
        w_refs = dict(zip(names, refs[0:n]))
        m_refs = dict(zip(names, refs[n:2 * n]))
        v_refs = dict(zip(names, refs[2 * n:3 * n]))
        outs = refs[3 * n:]
        o_refs = {nm: outs[4 * i:4 * i + 4] for i, nm in enumerate(names)}
        kme = 2 * lax.axis_index("x") + lax.axis_index("y")

        def update(nm, g, idx):
            d, m2, v2 = _adam_rows(w_refs[nm][idx], g, m_refs[nm][idx], v_refs[nm][idx])
            og, od, om, ov = o_refs[nm]
            og[idx] = g
            od[idx] = d
            om[idx] = m2
            ov[idx] = v2

        def packed(row, lane0, width):
            return jnp.concatenate([vred_ref[j, row:row + 1, :] for j in range(lane0 // 128, (lane0 + width) // 128)],
                                   axis=1)

        everything = (slice(None), slice(None))
        for nm, row, lane0, width in _VEC_PARAMS:
            update(nm, packed(row, lane0, width), everything)
        for r in range(2):
            update("hg_lb", packed(ROW_HG_LB, r * DG, DG), (slice(r, r + 1), slice(None)))
        for j in range(4):
            g = vred_ref[(j % 2) * 4 + kme, ROW_CW01 + j // 2:ROW_CW01 + j // 2 + 1, :]
            update("conv_w", g, (slice(j, j + 1), slice(None)))
        for r0 in range(0, DG, 128):
            rs = (slice(r0, r0 + 128), slice(None))
            both = rg_ref[r0:r0 + 128, :]
            update("rg_wa", both[:, 0:RGB], rs)
            update("rg_wx", pltpu.roll(both, RGB, 1)[:, 0:RGB], rs)

    args = [vred, rgred] + [d[nm] for d in (ws, ms, vs) for nm in names]
    out_shape = []
    for nm in names:
        out_shape += [jax.ShapeDtypeStruct(ws[nm].shape, F32)] * 4
    whole = lambda s: pl.BlockSpec(s.shape, lambda i, nd=len(s.shape): (0,) * nd)
    outs = pl.pallas_call(
        body, name="adam_small", out_shape=out_shape, grid=(1,),
        in_specs=[whole(a) for a in args], out_specs=[whole(s) for s in out_shape],
    )(*args)
    return {nm: tuple(outs[4 * i:4 * i + 4]) for i, nm in enumerate(names)}


def _fwd_inproj(x, nw, w_in_b, dep):
    tm = 512

    def body(x_ref, nw_ref, w_ref, dep_ref, pf_ref, pb_ref, u_ref):
        xv = x_ref[...]
        s = lax.rsqrt(jnp.mean(xv * xv, axis=-1, keepdims=True) + EPS)
        u = (xv * s * nw_ref[...]).astype(BF16)
        u_ref[...] = u
        r = [jnp.dot(u, w_ref[k], preferred_element_type=F32) for k in range(NSHARD)]
        h = DG // 2
        pf_ref[:, 0:DG] = r[0][:, 0:DG]
        pb_ref[:, 0:h] = r[0][:, DG:SHW].astype(BF16)
        pb_ref[:, h:DG] = r[1][:, 0:h].astype(BF16)
        pb_ref[:, DG:2 * DG] = r[1][:, h:SHW].astype(BF16)
        pf_ref[:, DG:2 * DG] = r[2][:, 0:DG]
        pb_ref[:, 2 * DG:2 * DG + h] = r[2][:, DG:SHW].astype(BF16)
        pb_ref[:, 2 * DG + h:3 * DG] = r[3][:, 0:h].astype(BF16)
        pb_ref[:, 3 * DG:4 * DG] = r[3][:, h:SHW].astype(BF16)

    return pl.pallas_call(
        body, name="fwd_inproj", grid=(T // tm,),
        in_specs=[pl.BlockSpec((tm, D), lambda i: (i, 0)),
                  pl.BlockSpec((1, D), lambda i: (0, 0)),
                  pl.BlockSpec((NSHARD, D, SHW), lambda i: (0, 0, 0), pipeline_mode=pl.Buffered(1)), ANY_SPEC],
        out_specs=[pl.BlockSpec((tm, 2 * DG), lambda i: (i, 0)),
                   pl.BlockSpec((tm, 4 * DG), lambda i: (i, 0)),
                   pl.BlockSpec((tm, D), lambda i: (i, 0))],
        out_shape=[jax.ShapeDtypeStruct((T, 2 * DG), F32), jax.ShapeDtypeStruct((T, 4 * DG), BF16),
                   jax.ShapeDtypeStruct((T, D), BF16)],
        compiler_params=_cparams(("parallel",), 48),
    )(x, nw, w_in_b, dep)


def _conv_rows(cw_ref):
    return [jnp.concatenate([cw_ref[k, j:j + 1, :] for k in range(NSHARD)], axis=1) for j in range(4)]


def _rg_conv(xa, prev8, cw, cb, rows):
    taps = [_shift_down(xa, prev8, 3, rows), _shift_down(xa, prev8, 2, rows),
            _shift_down(xa, prev8, 1, rows), xa]
    xc = cb
    for j in range(4):
        xc = xc + taps[j] * cw[j]
    return xc, taps


def _block_mask():
    r = lax.broadcasted_iota(I32, (DG, DG), 0)
    c = lax.broadcasted_iota(I32, (DG, DG), 1)
    return (r >> 6) == (c >> 6)


def _dense_from_blocks(wc):
    j = lax.broadcasted_iota(I32, (RGB, DG), 0)
    c = lax.broadcasted_iota(I32, (RGB, DG), 1)
    spread = _mm_exact(wc, ((c & (RGB - 1)) == j).astype(F32))
    return jnp.where(_block_mask(), spread, 0.0)


def _blocks_from_dense(da, dx):
    c = lax.broadcasted_iota(I32, (DG, 128), 0)
    j = lax.broadcasted_iota(I32, (DG, 128), 1)
    hit = (c & (RGB - 1)) == (j & (RGB - 1))
    mask = _block_mask()
    return (_mm_exact(jnp.where(mask, da, 0.0), (hit & (j < RGB)).astype(F32))
            + _mm_exact(jnp.where(mask, dx, 0.0), (hit & (j >= RGB)).astype(F32)))


def _rg_gates(xc, wa, ba, wx, bx, sp, first_row):
    r = _sigmoid(_mm(xc, wa) + ba)
    i = _sigmoid(_mm(xc, wx) + bx)
    log_a = (-RG_C) * r * sp
    a = jnp.exp(log_a)
    a2 = a * a
    one_m_a2 = -jnp.tanh(log_a) * (a2 + 1.0)
    mult = jnp.where(first_row, 1.0, jnp.sqrt(one_m_a2))
    return r, i, a, a2, mult


def _softplus(z):
    return jnp.maximum(z, 0.0) + jnp.log1p(jnp.exp(-jnp.abs(z)))


def _fwd_rglru(pf, pb, cw3, conv_b, wa_c, ba, wx_c, bx, lam):
    tm = 512
    ng = tm // 8

    def body(xa_ref, ga_ref, cw_ref, cb_ref, wa_ref, ba_ref, wx_ref, bx_ref, lam_ref,
             h_ref, ya_ref, a_s, u_s, tail_s, hc_s, wa_s, wx_s):
        i = pl.program_id(0)

        @pl.when(i == 0)
        def _():
            tail_s[...] = jnp.zeros_like(tail_s)
            hc_s[...] = jnp.zeros_like(hc_s)
            wa_s[...] = _dense_from_blocks(wa_ref[...]).astype(BF16)
            wx_s[...] = _dense_from_blocks(wx_ref[...]).astype(BF16)

        rows = lax.broadcasted_iota(I32, (tm, DG), 0)
        xa = xa_ref[...]
        xc, _ = _rg_conv(xa, tail_s[...], _conv_rows(cw_ref), cb_ref[...], rows)
        tail_s[...] = xa[tm - 8:tm, :]
        sp = _softplus(-lam_ref[...])
        rp = _mm(xc, wa_s[...]) + ba_ref[...]
        ip = _mm(xc, wx_s[...]) + bx_ref[...]
        rb = 64
        rows_b = lax.broadcasted_iota(I32, (rb, DG), 0)
        rows8 = rows_b & 7
        carry = hc_s[0:1, :]
        for b0 in range(0, tm, rb):
            sl = slice(b0, b0 + rb)
            r = _sigmoid(rp[sl])
            ig = _sigmoid(ip[sl])
            log_a = (-RG_C) * r * sp
            av = jnp.exp(log_a)
            mult = jnp.sqrt(-jnp.tanh(log_a) * (av * av + 1.0))
            if b0 == 0:
                mult = jnp.where((rows_b + i * tm) == 0, 1.0, mult)
            uv = mult * (ig * xc[sl])
            for d in (1, 2, 4):
                keep = rows8 >= d
                uv = uv + av * jnp.where(keep, _roll_in_groups(uv, d), 0.0)
                av = av * jnp.where(keep, _roll_in_groups(av, d), 1.0)
            ga = ga_ref[sl, :].astype(F32)
            gate = ga * _sigmoid(ga)
            hs = []
            for g in range(rb // 8):
                gs = slice(g * 8, (g + 1) * 8)
                hv = uv[gs] + av[gs] * carry
                carry = hv[7:8, :]
                hs.append(hv)
            hb = jnp.concatenate(hs, axis=0)
            h_ref[sl, :] = hb
            ya_ref[sl, :] = (hb * gate).astype(BF16)
        hc_s[0:1, :] = carry

    vec = lambda: pl.BlockSpec((1, DG), lambda i: (0, 0))
    blocks = lambda: pl.BlockSpec((DG, RGB), lambda i: (0, 0))
    return pl.pallas_call(
        body, name="fwd_rglru", grid=(T // tm,),
        in_specs=[pl.BlockSpec((tm, DG), lambda i: (i, 0)),
                  pl.BlockSpec((tm, DG), lambda i: (i, 0)),
                  pl.BlockSpec((NSHARD, 4, 128), lambda i: (0, 0, 0)), vec(),
                  blocks(), vec(), blocks(), vec(), vec()],
        out_specs=[pl.BlockSpec((tm, DG), lambda i: (i, 0)),
                   pl.BlockSpec((tm, DG), lambda i: (i, 0))],
        out_shape=[jax.ShapeDtypeStruct((T, DG), F32), jax.ShapeDtypeStruct((T, DG), BF16)],
        scratch_shapes=[pltpu.VMEM((tm, DG), F32), pltpu.VMEM((tm, DG), F32),
                        pltpu.VMEM((8, DG), F32), pltpu.VMEM((8, DG), F32),
                        pltpu.VMEM((DG, DG), BF16), pltpu.VMEM((DG, DG), BF16)],
        compiler_params=_cparams(("arbitrary",), 48),
    )(pf, pb, cw3, conv_b, wa_c, ba, wx_c, bx, lam)


def _hg_lower_bound(lb_ref):
    return _sig_pair(lb_ref[0:1, :] - lb_ref[1:2, :])


def _hg_gates(fz, lb, one_m_lb):
    sg, sn = _sig_pair(fz)
    f = lb + one_m_lb * sg
    return sg, sn, f, jnp.log(f), one_m_lb * sn


def _tri(lower):
    r = lax.broadcasted_iota(I32, (CH, CH), 0)
    c = lax.broadcasted_iota(I32, (CH, CH), 1)
    return (r >= c) if lower else (r <= c)


def _chunk_cumsum(v, rows64):
    for d in (1, 2, 4, 8, 16, 32):
        v = v + jnp.where(rows64 >= d, pltpu.roll(v, d, 0), 0.0)
    return v


def _chunk_rev_cumsum(v, rows64):
    n = v.shape[0]
    for d in (1, 2, 4, 8, 16, 32):
        v = v + jnp.where(rows64 < CH - d, pltpu.roll(v, n - d, 0), 0.0)
    return v


def _hg_recompute(q_ref, f_ref, lb, one_m_lb, rows64, eb_s, enb_s, ekd_s, qe_s, ke_s, kd_s, dec_s):
    nc = q_ref.shape[0] // CH
    sg, sn, f, logf, k = _hg_gates(f_ref[...], lb, one_m_lb)
    q = q_ref[...].astype(F32)
    sq = _sigmoid(q)
    qs = q * sq * (HD ** -0.5)
    b = _chunk_cumsum(logf, rows64)
    for c in range(nc):
        rs = slice(c * CH, (c + 1) * CH)
        b_c = b[rs]
        bl = b_c[CH - 1:CH, :]
        eb, enb, ekd = jnp.exp(b_c), jnp.exp(-b_c), jnp.exp(bl - b_c)
        if eb_s is not None:
            eb_s[rs, :] = eb
            enb_s[rs, :] = enb
            ekd_s[rs, :] = ekd
        qe_s[rs, :] = (qs[rs] * eb).astype(BF16)
        ke_s[rs, :] = (k[rs] * enb).astype(BF16)
        kd_s[rs, :] = (k[rs] * ekd).astype(BF16)
        dec_s[c:c + 1, :] = jnp.exp(bl)
    return sg, sn, f, k, q, sq, qs


def _fwd_hgrn2(pf, pb, hg_lb, hg_nw):
    tm = 512
    nc = tm // CH

    def body(q_ref, f_ref, v_ref, g_ref, lb_ref, nw_ref, yb_ref, o_ref, sp_ref,
             st_s, qe_s, ke_s, kd_s, vb_s, dec_s, p_s, ds_s):
        i = pl.program_id(0)

        @pl.when(i == 0)
        def _():
            st_s[...] = jnp.zeros_like(st_s)

        lb, one_m_lb = _hg_lower_bound(lb_ref)
        rows64 = lax.broadcasted_iota(I32, (tm, DG), 0) & (CH - 1)
        _hg_recompute(q_ref, f_ref, lb, one_m_lb, rows64, None, None, None, qe_s, ke_s, kd_s, dec_s)
        vb_s[...] = v_ref[...]
        mask = _tri(True)
        items = [(c, hd, slice(c * CH, (c + 1) * CH), slice(hd * HD, (hd + 1) * HD))
                 for c in range(nc) for hd in range(NH)]
        for c, hd, rs, cols in items:
            p_s[c * NH + hd] = jnp.where(mask, _mm_nt(qe_s[rs, cols], ke_s[rs, cols]), 0.0).astype(BF16)
            ds_s[c * NH + hd] = _mm_tn(vb_s[rs, cols], kd_s[rs, cols])
        for c, hd, rs, cols in items:
            st = st_s[hd]
            sp_ref[hd, c] = st
            st_s[hd] = st * dec_s[c:c + 1, cols] + ds_s[c * NH + hd]
        for c, hd, rs, cols in items:
            o_ref[rs, cols] = _mm(p_s[c * NH + hd], vb_s[rs, cols]) + _mm_nt(qe_s[rs, cols], sp_ref[hd, c])
        nw = nw_ref[...]
        for hd in range(NH):
            cols = slice(hd * HD, (hd + 1) * HD)
            o = o_ref[:, cols]
            so = lax.rsqrt(jnp.mean(o * o, axis=-1, keepdims=True) + EPS)
            g = g_ref[:, cols].astype(F32)
            sg = _sigmoid(g)
            yb_ref[:, cols] = (o * so * nw * (g * sg)).astype(BF16)

    col = lambda j: pl.BlockSpec((tm, DG), lambda i: (i, j))
    return pl.pallas_call(
        body, name="fwd_hgrn2", grid=(T // tm,),
        in_specs=[col(1), col(1), col(2), col(3),
                  pl.BlockSpec((2, DG), lambda i: (0, 0)),
                  pl.BlockSpec((1, HD), lambda i: (0, 0))],
        out_specs=[pl.BlockSpec((tm, DG), lambda i: (i, 0)),
                   pl.BlockSpec((tm, DG), lambda i: (i, 0)),
                   pl.BlockSpec((NH, nc, HD, HD), lambda i: (0, i, 0, 0))],
        out_shape=[jax.ShapeDtypeStruct((T, DG), BF16), jax.ShapeDtypeStruct((T, DG), F32),
                   jax.ShapeDtypeStruct((NH, NCHUNK, HD, HD), F32)],
        scratch_shapes=[pltpu.VMEM((NH, HD, HD), F32),
                        pltpu.VMEM((tm, DG), BF16), pltpu.VMEM((tm, DG), BF16), pltpu.VMEM((tm, DG), BF16),
                        pltpu.VMEM((tm, DG), BF16), pltpu.VMEM((nc, DG), F32),
                        pltpu.VMEM((nc * NH, CH, CH), BF16), pltpu.VMEM((nc * NH, HD, HD), F32)],
        compiler_params=_cparams(("arbitrary",), 48),
    )(pb, pf, pb, pb, hg_lb, hg_nw)


def _tail_fwd_bwd(x, p, tgt, ya, yb, w_out_b, w_pg_b, w_pp_b, ple_nw, b_pg, fnw):
    tm = 512
    nt = T // tm
    QR = D // NSHARD

    def body(x_ref, p_ref, t_ref, ya_ref, yb_ref, wo_ref, wg_ref, wp_ref, pw_ref, b_ref, fw_ref,
             dh1_ref, dyab_ref, dwo_ref, dwg_ref, dwp_ref, sm_ref, dwo_s, dwg_s, dwp_s):
        i = pl.program_id(0)

        @pl.when(i == 0)
        def _():
            dwo_s[...] = jnp.zeros_like(dwo_s)
            dwg_s[...] = jnp.zeros_like(dwg_s)
            dwp_s[...] = jnp.zeros_like(dwp_s)
            sm_ref[...] = jnp.zeros_like(sm_ref)

        ya = ya_ref[...]
        yb = yb_ref[...]
        pv = p_ref[...].astype(BF16)
        pw = pw_ref[...]
        fw = fw_ref[...]
        h1 = x_ref[...] + _mm(ya, wo_ref[0:DG, :]) + _mm(yb, wo_ref[DG:D, :])
        s2 = lax.rsqrt(jnp.mean(h1 * h1, axis=-1, keepdims=True) + EPS)
        n2h = h1 * s2
        n2 = (n2h * pw).astype(BF16)
        z = _mm(n2, wg_ref[...]) + b_ref[...]
        gate = _sigmoid(z)
        pp = jnp.concatenate([_mm(pv, wp_ref[k]) for k in range(NSHARD)], axis=1)
        h2 = h1 + gate * pp
        s3 = lax.rsqrt(jnp.mean(h2 * h2, axis=-1, keepdims=True) + EPS)
        hn = h2 * s3
        err = hn * fw - t_ref[...]
        sm_ref[0:1, :] += _rowsum(err * err)
        dy = err * (1.0 / D)
        sm_ref[1:2, :] += _rowsum(dy * hn)
        g3 = dy * fw
        dh2 = s3 * (g3 - hn * jnp.mean(g3 * hn, axis=-1, keepdims=True))
        dpp = (dh2 * gate).astype(BF16)
        dz = dh2 * pp * gate * (1.0 - gate)
        sm_ref[2:3, :] += _rowsum(dz)
        dzb = dz.astype(BF16)
        dwg_s[...] += _mm_tn(n2, dzb)
        dn2 = _mm_nt(dzb, wg_ref[...])
        for k in range(NSHARD):
            dwp_s[k] += _mm_tn(pv, dpp[:, k * PLE:(k + 1) * PLE])
        sm_ref[3:4, :] += _rowsum(dn2 * n2h)
        g2 = dn2 * pw
        dh1 = dh2 + s2 * (g2 - n2h * jnp.mean(g2 * n2h, axis=-1, keepdims=True))
        dh1_ref[...] = dh1
        dh1b = dh1.astype(BF16)
        dyab_ref[...] = _mm_nt(dh1b, wo_ref[...])
        dwo_s[0:DG, :] += _mm_tn(ya, dh1b)
        dwo_s[DG:D, :] += _mm_tn(yb, dh1b)

        @pl.when(i == nt - 1)
        def _():
            total = jnp.sum(sm_ref[0:1, :], axis=-1, keepdims=True) * (0.5 / D)
            sm_ref[4:5, :] = jnp.broadcast_to(total, (1, D))
            for k in range(NSHARD):
                dwo_ref[k] = dwo_s[k * QR:(k + 1) * QR, :].astype(BF16)
                dwg_ref[k] = dwg_s[k * QR:(k + 1) * QR, :].astype(BF16)
                dwp_ref[k] = dwp_s[k].astype(BF16)

    row = lambda w: pl.BlockSpec((tm, w), lambda i: (i, 0))
    const2 = lambda s: pl.BlockSpec(s, lambda i: (0, 0), pipeline_mode=pl.Buffered(1))
    const3 = lambda s: pl.BlockSpec(s, lambda i: (0, 0, 0), pipeline_mode=pl.Buffered(1))
    return pl.pallas_call(
        body, name="tail_fwd_bwd", grid=(nt,),
        in_specs=[row(D), row(PLE), row(D), row(DG), row(DG),
                  const2((D, D)), const2((D, D)), const3((NSHARD, PLE, PLE)),
                  const2((1, D)), const2((1, D)), const2((1, D))],
        out_specs=[row(D), row(D), const3((NSHARD, QR, D)), const3((NSHARD, QR, D)),
                   const3((NSHARD, PLE, PLE)), const2((8, D))],
        out_shape=[jax.ShapeDtypeStruct((T, D), F32), jax.ShapeDtypeStruct((T, D), F32),
                   jax.ShapeDtypeStruct((NSHARD, QR, D), BF16), jax.ShapeDtypeStruct((NSHARD, QR, D), BF16),
                   jax.ShapeDtypeStruct((NSHARD, PLE, PLE), BF16), jax.ShapeDtypeStruct((8, D), F32)],
        scratch_shapes=[pltpu.VMEM((D, D), F32), pltpu.VMEM((D, D), F32), pltpu.VMEM((NSHARD, PLE, PLE), F32)],
        compiler_params=_cparams(("arbitrary",), 62),
    )(x, p, tgt, ya, yb, w_out_b, w_pg_b, w_pp_b, ple_nw, b_pg, fnw)


def _bwd_rglru(pf, pb, h, dyab, cw3, conv_b, wa_c, ba, wx_c, bx, lam, dep):
    tm = 512
    nt = T // tm
    ng = tm // 8

    def body(xa_ref, ga_ref, xp_ref, h_ref, hp_ref, dya_ref, cw_ref, cb_ref, wa_ref, ba_ref, wx_ref, bx_ref,
             lam_ref, dep_ref, da_ref, rg_ref, sm_ref, a_s, g_s, cg_s, nxt_s, wa_s, wx_s, dwa_s, dwx_s):
        i = pl.program_id(0)
        tile = nt - 1 - i

        @pl.when(i == 0)
        def _():
            dwa_s[...] = jnp.zeros_like(dwa_s)
            dwx_s[...] = jnp.zeros_like(dwx_s)
            sm_ref[...] = jnp.zeros_like(sm_ref)
            cg_s[...] = jnp.zeros_like(cg_s)
            nxt_s[...] = jnp.zeros_like(nxt_s)
            wa_s[...] = _dense_from_blocks(wa_ref[...]).astype(BF16)
            wx_s[...] = _dense_from_blocks(wx_ref[...]).astype(BF16)

        rows = lax.broadcasted_iota(I32, (tm, DG), 0)
        has_prev = tile > 0
        xa = xa_ref[...]
        xprev = jnp.where(has_prev, xp_ref[...], 0.0)
        cw = _conv_rows(cw_ref)
        xc, taps = _rg_conv(xa, xprev, cw, cb_ref[...], rows)
        lam_v = lam_ref[...]
        sp = _softplus(-lam_v)
        first_row = (rows + tile * tm) == 0
        r, ig, a, a2, mult = _rg_gates(xc, wa_s[...], ba_ref[...], wx_s[...], bx_ref[...], sp, first_row)
        hv = h_ref[...]
        hprev = jnp.where(has_prev, hp_ref[...], 0.0)
        h_m1 = _shift_down(hv, hprev, 1, rows)
        ga = ga_ref[...].astype(F32)
        sg = _sigmoid(ga)
        dya = dya_ref[...]
        dga = dya * hv * (sg * (1.0 + ga * (1.0 - sg)))

        av = jnp.where(rows == tm - 1, 1.0, pltpu.roll(a, tm - 1, 0))
        gv = dya * (ga * sg)
        rows8 = rows & 7
        for d in (1, 2, 4):
            keep = rows8 < 8 - d
            gv = gv + av * jnp.where(keep, _roll_in_groups(gv, 8 - d), 0.0)
            av = av * jnp.where(keep, _roll_in_groups(av, 8 - d), 1.0)
        a_s[...] = av
        g_s[...] = gv
        carry = cg_s[0:1, :]
        for g in range(ng - 1, -1, -1):
            sl = slice(g * 8, (g + 1) * 8)
            ab, gb = a_s[sl, :], g_s[sl, :]
            g_s[sl, :] = gb + ab * carry
            carry = gb[0:1, :] + ab[0:1, :] * carry
        cg_s[0:1, :] = a[0:1, :] * carry

        gt = g_s[...]
        da = gt * h_m1
        ixc = ig * xc
        di = gt * mult * xc
        dxc = gt * mult * ig
        dlog_a = da * a + jnp.where(first_row, 0.0, gt * ixc * (-a2 / mult))
        sm_ref[3:4, :] += _rowsum(dlog_a * ((-RG_C) * r))
        dpr_f = dlog_a * ((-RG_C) * sp) * r * (1.0 - r)
        dpi_f = di * ig * (1.0 - ig)
        sm_ref[1:2, :] += _rowsum(dpr_f)
        sm_ref[2:3, :] += _rowsum(dpi_f)
        dpr = dpr_f.astype(BF16)
        dpi = dpi_f.astype(BF16)
        xcb = xc.astype(BF16)
        dwa_s[...] += _mm_tn(xcb, dpr)
        dwx_s[...] += _mm_tn(xcb, dpi)
        dxc = dxc + _mm_nt(dpr, wa_s[...]) + _mm_nt(dpi, wx_s[...])
        sm_ref[0:1, :] += _rowsum(dxc)
        for j in range(4):
            sm_ref[4 + j:5 + j, :] += _rowsum(dxc * taps[j])
        nxt = nxt_s[...]
        dxa = (dxc * cw[3] + _shift_up(dxc, nxt, 1, rows) * cw[2]
               + _shift_up(dxc, nxt, 2, rows) * cw[1] + _shift_up(dxc, nxt, 3, rows) * cw[0])
        nxt_s[...] = dxc[0:8, :]
        da_ref[:, 0:DG] = dxa.astype(BF16)
        da_ref[:, DG:D] = dga.astype(BF16)

        @pl.when(i == nt - 1)
        def _():
            _, s_neg = _sig_pair(lam_v)
            sm_ref[3:4, :] = sm_ref[3:4, :] * (-s_neg)
            rg_ref[...] = _blocks_from_dense(dwa_s[...], dwx_s[...])

    vec = lambda: pl.BlockSpec((1, DG), lambda i: (0, 0))
    blocks = lambda: pl.BlockSpec((DG, RGB), lambda i: (0, 0))
    prev8 = lambda: pl.BlockSpec((8, DG), lambda i: (jnp.maximum((nt - 1 - i) * (tm // 8) - 1, 0), 0))
    return pl.pallas_call(
        body, name="bwd_rglru", grid=(nt,),
        in_specs=[pl.BlockSpec((tm, DG), lambda i: (nt - 1 - i, 0)),
                  pl.BlockSpec((tm, DG), lambda i: (nt - 1 - i, 0)),
                  prev8(),
                  pl.BlockSpec((tm, DG), lambda i: (nt - 1 - i, 0)),
                  prev8(),
                  pl.BlockSpec((tm, DG), lambda i: (nt - 1 - i, 0)),
                  pl.BlockSpec((NSHARD, 4, 128), lambda i: (0, 0, 0)), vec(),
                  blocks(), vec(), blocks(), vec(), vec(), ANY_SPEC],
        out_specs=[pl.BlockSpec((tm, D), lambda i: (nt - 1 - i, 0)),
                   pl.BlockSpec((DG, 128), lambda i: (0, 0)),
                   pl.BlockSpec((8, DG), lambda i: (0, 0))],
        out_shape=[jax.ShapeDtypeStruct((T, D), BF16), jax.ShapeDtypeStruct((DG, 128), F32),
                   jax.ShapeDtypeStruct((8, DG), F32)],
        scratch_shapes=[pltpu.VMEM((tm, DG), F32), pltpu.VMEM((tm, DG), F32),
                        pltpu.VMEM((8, DG), F32), pltpu.VMEM((8, DG), F32),
                        pltpu.VMEM((DG, DG), BF16), pltpu.VMEM((DG, DG), BF16),
                        pltpu.VMEM((DG, DG), F32), pltpu.VMEM((DG, DG), F32)],
        compiler_params=_cparams(("arbitrary",), 56),
    )(pf, pb, pf, h, h, dyab, cw3, conv_b, wa_c, ba, wx_c, bx, lam, dep)


def _bwd_hgrn2(pf, pb, o, s_prev, dyab, hg_lb, hg_nw, dep):
    tm = 512
    nt = T // tm
    nc = tm // CH

    def body(q_ref, f_ref, v_ref, g_ref, o_ref, sp_ref, dy_ref, lb_ref, nw_ref, dep_ref, db_ref, sm_ref,
             dst_s, eb_s, enb_s, ekd_s, qe_s, ke_s, kd_s, vb_s, do_s, dec_s, ddec_s, p_s, dp_s,
             g_s, dsta_s, dva_s, dqe_s, dke_s, dkd_s, dlf_s):
        i = pl.program_id(0)

        @pl.when(i == 0)
        def _():
            sm_ref[...] = jnp.zeros_like(sm_ref)
            dst_s[...] = jnp.zeros_like(dst_s)

        lb, one_m_lb = _hg_lower_bound(lb_ref)
        rows64 = lax.broadcasted_iota(I32, (tm, DG), 0) & (CH - 1)
        sg, sn, f, k, q, sq, qs = _hg_recompute(
            q_ref, f_ref, lb, one_m_lb, rows64, eb_s, enb_s, ekd_s, qe_s, ke_s, kd_s, dec_s)
        vb_s[...] = v_ref[...]

        nw = nw_ref[...]
        for hd in range(NH):
            cols = slice(hd * HD, (hd + 1) * HD)
            g = g_ref[:, cols].astype(F32)
            sgg = _sigmoid(g)
            o = o_ref[:, cols]
            so = lax.rsqrt(jnp.mean(o * o, axis=-1, keepdims=True) + EPS)
            oh = o * so
            dyb = dy_ref[:, cols]
            db_ref[:, 3 * DG + hd * HD:3 * DG + (hd + 1) * HD] = (
                dyb * (oh * nw) * (sgg * (1.0 + g * (1.0 - sgg)))).astype(BF16)
            don = dyb * (g * sgg)
            sm_ref[1:2, 0:HD] += _rowsum(don * oh)
            gw = don * nw
            do_s[:, cols] = (so * (gw - oh * jnp.mean(gw * oh, axis=-1, keepdims=True))).astype(BF16)

        mask = _tri(True)
        items = [(c, hd, slice(c * CH, (c + 1) * CH), slice(hd * HD, (hd + 1) * HD))
                 for c in range(nc) for hd in range(NH)]
        for c, hd, rs, cols in items:
            p_s[c * NH + hd] = jnp.where(mask, _mm_nt(qe_s[rs, cols], ke_s[rs, cols]), 0.0).astype(BF16)
            dp_s[c * NH + hd] = jnp.where(mask, _mm_nt(do_s[rs, cols], vb_s[rs, cols]), 0.0).astype(BF16)
        for c, hd, rs, cols in items:
            n = c * NH + hd
            dva_s[rs, cols] = _mm_tn(p_s[n], do_s[rs, cols])
            dqe_s[rs, cols] = _mm(dp_s[n], ke_s[rs, cols])
            dke_s[rs, cols] = _mm_tn(dp_s[n], qe_s[rs, cols])
            g_s[n] = _mm_tn(do_s[rs, cols], qe_s[rs, cols])
        for c, hd, rs, cols in reversed(items):
            n = c * NH + hd
            dst = dst_s[hd]
            dsta_s[n] = dst
            dst_s[hd] = dst * dec_s[c:c + 1, cols] + g_s[n]
        for c, hd, rs, cols in items:
            n = c * NH + hd
            dst = dsta_s[n]
            st_prev = sp_ref[hd, c]
            dv = dva_s[rs, cols] + _mm_nt(kd_s[rs, cols], dst)
            db_ref[rs, 2 * DG + hd * HD:2 * DG + (hd + 1) * HD] = dv.astype(BF16)
            dqe_s[rs, cols] += _mm(do_s[rs, cols], st_prev)
            dkd_s[rs, cols] = _mm(vb_s[rs, cols], dst)
            ddec_s[c:c + 1, cols] = _rowsum(dst * st_prev)

        eb, enb, ekd = eb_s[...], enb_s[...], ekd_s[...]
        dqe, dke, dkd = dqe_s[...], dke_s[...], dkd_s[...]
        t_kd = dkd * (k * ekd)
        rc = _chunk_rev_cumsum(dqe * (qs * eb) - dke * (k * enb) - t_kd, rows64)
        for c in range(nc):
            rs = slice(c * CH, (c + 1) * CH)
            dbl = _rowsum(t_kd[rs]) + ddec_s[c:c + 1, :] * dec_s[c:c + 1, :]
            dlf_s[rs, :] = rc[rs] + dbl
        t = dlf_s[...] / f - (dke * enb + dkd * ekd)
        db_ref[:, DG:2 * DG] = (one_m_lb * sg * sn * t).astype(BF16)
        sm_ref[0:1, :] += _rowsum(sn * t)
        db_ref[:, 0:DG] = (dqe * eb * (sq * (1.0 + q * (1.0 - sq))) * (HD ** -0.5)).astype(BF16)

        @pl.when(i == nt - 1)
        def _():
            dsm = sm_ref[0:1, :] * (lb * one_m_lb)
            sm_ref[2:3, :] = dsm
            sm_ref[3:4, :] = -dsm

    col = lambda j: pl.BlockSpec((tm, DG), lambda i: (nt - 1 - i, j))
    big = lambda dt: pltpu.VMEM((tm, DG), dt)
    return pl.pallas_call(
        body, name="bwd_hgrn2", grid=(nt,),
        in_specs=[col(1), col(1), col(2), col(3),
                  pl.BlockSpec((tm, DG), lambda i: (nt - 1 - i, 0)),
                  pl.BlockSpec((NH, nc, HD, HD), lambda i: (0, nt - 1 - i, 0, 0)),
                  pl.BlockSpec((tm, DG), lambda i: (nt - 1 - i, 1)),
                  pl.BlockSpec((2, DG), lambda i: (0, 0)),
                  pl.BlockSpec((1, HD), lambda i: (0, 0)), ANY_SPEC],
        out_specs=[pl.BlockSpec((tm, 4 * DG), lambda i: (nt - 1 - i, 0)),
                   pl.BlockSpec((8, DG), lambda i: (0, 0))],
        out_shape=[jax.ShapeDtypeStruct((T, 4 * DG), BF16), jax.ShapeDtypeStruct((8, DG), F32)],
        scratch_shapes=[pltpu.VMEM((NH, HD, HD), F32),
                        big(F32), big(F32), big(F32),
                        big(BF16), big(BF16), big(BF16), big(BF16), big(BF16),
                        pltpu.VMEM((nc, DG), F32), pltpu.VMEM((nc, DG), F32),
                        pltpu.VMEM((nc * NH, CH, CH), BF16), pltpu.VMEM((nc * NH, CH, CH), BF16),
                        pltpu.VMEM((nc * NH, HD, HD), F32), pltpu.VMEM((nc * NH, HD, HD), F32),
                        big(F32), big(F32), big(F32), big(F32), big(F32)],
        compiler_params=_cparams(("arbitrary",), 56),
    )(pb, pf, pb, pb, o, s_prev, dyab, hg_lb, hg_nw, dep)


def _dproj_pieces(k, da_ref, db_ref):
    if k == 0:
        return [(da_ref[:, 0:SHW], 0)]
    if k == 1:
        return [(da_ref[:, SHW:D], 0), (db_ref[:, 0:DG], D - SHW)]
    if k == 2:
        return [(db_ref[:, DG:DG + SHW], 0)]
    return [(db_ref[:, DG + SHW:4 * DG], 0)]


def _bwd_inproj_dx(x, dh1, d_a, d_b, w_in_b, nw, dep):
    tm = 512
    nt = T // tm

    def body(x_ref, dh1_ref, da_ref, db_ref, w_ref, nw_ref, dep_ref, dx_ref, sm_ref):
        i = pl.program_id(0)

        @pl.when(i == 0)
        def _():
            sm_ref[...] = jnp.zeros_like(sm_ref)

        du = None
        for k in range(NSHARD):
            for val, off in _dproj_pieces(k, da_ref, db_ref):
                t = _mm_nt(val, w_ref[k, :, off:off + val.shape[1]])
                du = t if du is None else du + t
        xv = x_ref[...]
        s = lax.rsqrt(jnp.mean(xv * xv, axis=-1, keepdims=True) + EPS)
        xh = xv * s
        sm_ref[0:1, :] += _rowsum(du * xh)
        g = du * nw_ref[...]
        dx_ref[...] = dh1_ref[...] + s * (g - xh * jnp.mean(g * xh, axis=-1, keepdims=True))

    row = lambda w: pl.BlockSpec((tm, w), lambda i: (i, 0))
    return pl.pallas_call(
        body, name="bwd_inproj_dx", grid=(nt,),
        in_specs=[row(D), row(D), row(D), row(4 * DG),
                  pl.BlockSpec((NSHARD, D, SHW), lambda i: (0, 0, 0), pipeline_mode=pl.Buffered(1)),
                  pl.BlockSpec((1, D), lambda i: (0, 0)), ANY_SPEC],
        out_specs=[row(D), pl.BlockSpec((8, D), lambda i: (0, 0))],
        out_shape=[jax.ShapeDtypeStruct((T, D), F32), jax.ShapeDtypeStruct((8, D), F32)],
        compiler_params=_cparams(("arbitrary",), 56),
    )(x, dh1, d_a, d_b, w_in_b, nw, dep)


def _bwd_inproj_dw(name, ks, u_b, d_a, d_b):
    tm = 1024
    nt = T // tm
    nk = len(ks)
    H = D // 2
    step = 128
    need_a = 0 in ks or 1 in ks

    def body(*refs):
        if need_a:
            u_ref, da_ref, db_ref, dw_ref, acc, send_s, recv_s, ssem, rsem = refs
        else:
            u_ref, db_ref, dw_ref, acc, send_s, recv_s, ssem, rsem = refs
            da_ref = None
        i = pl.program_id(0)

        @pl.when(i == 0)
        def _():
            acc[...] = jnp.zeros_like(acc)

        u = u_ref[...]
        for j, k in enumerate(ks):
            for val, off in _dproj_pieces(k, da_ref, db_ref):
                acc[j, :, off:off + val.shape[1]] += _mm_tn(u, val)

        @pl.when(i == nt - 1)
        def _():
            x, y, c = lax.axis_index("x"), lax.axis_index("y"), lax.axis_index("c")
            sibling = (x, y, 1 - c)
            mine0 = pl.multiple_of(c * H, step)
            other0 = pl.multiple_of((1 - c) * H, step)
            copies = []
            for j in range(nk):
                for r0 in range(0, H, step):
                    send_s[j, r0:r0 + step, :] = acc[j, pl.ds(other0 + r0, step), :].astype(BF16)
                cp = _remote(send_s.at[j], recv_s.at[j], ssem.at[j], rsem.at[j], sibling)
                cp.start()
                copies.append(cp)
            for j in range(nk):
                copies[j].wait_recv()
                for r0 in range(0, H, step):
                    s = acc[j, pl.ds(mine0 + r0, step), :] + recv_s[j, r0:r0 + step, :].astype(F32)
                    dw_ref[j, r0:r0 + step, :] = s.astype(BF16)
            for cp in copies:
                cp.wait_send()

    row = lambda w: pl.BlockSpec((tm, w), lambda i: (i, 0))
    ins = [u_b] + ([d_a] if need_a else []) + [d_b]
    in_specs = [row(D)] + ([row(D)] if need_a else []) + [row(4 * DG)]
    return pl.pallas_call(
        body, name=name, grid=(nt,), in_specs=in_specs,
        out_specs=pl.BlockSpec((nk, H, SHW), lambda i: (0, 0, 0)),
        out_shape=jax.ShapeDtypeStruct((nk, H, SHW), BF16),
        scratch_shapes=[pltpu.VMEM((nk, D, SHW), F32), pltpu.VMEM((nk, H, SHW), BF16),
                        pltpu.VMEM((nk, H, SHW), BF16),
                        pltpu.SemaphoreType.DMA((nk,)), pltpu.SemaphoreType.DMA((nk,))],
        compiler_params=_cparams(("arbitrary",), 48),
    )(*ins)


_OUT_ORDER = ["norm_mix_w", "w_in", "conv_w", "conv_b", "rg_wa", "rg_ba", "rg_wx", "rg_bx", "rg_lambda", "hg_lb",
              "hg_norm_w", "w_out", "ple_norm_w", "w_ple_gate", "b_ple_gate", "w_ple_proj", "final_norm_w"]
_BIG = ["w_in", "w_out", "w_ple_gate", "w_ple_proj"]


def _small_view(name, a):
    if name in ("rg_wa", "rg_wx"):
        return a.reshape(DG, RGB)
    if name == "conv_w":
        return a.reshape(4, 128)
    if name == "final_norm_w":
        return a.reshape(1, D)
    return a


def _landing(rows, cols):
    return lax.empty((NDEV, rows, cols), BF16)


def kernel(x, p, norm_mix_w, w_in, conv_w, conv_b, rg_wa, rg_ba, rg_wx, rg_bx, rg_lambda, hg_lb, hg_norm_w, w_out, ple_norm_w, w_ple_gate, b_ple_gate, w_ple_proj, final_norm_w, loss_target, m_norm_mix_w, m_w_in, m_conv_w, m_conv_b, m_rg_wa, m_rg_ba, m_rg_wx, m_rg_bx, m_rg_lambda, m_hg_lb, m_hg_norm_w, m_w_out, m_ple_norm_w, m_w_ple_gate, m_b_ple_gate, m_w_ple_proj, m_final_norm_w, v_norm_mix_w, v_w_in, v_conv_w, v_conv_b, v_rg_wa, v_rg_ba, v_rg_wx, v_rg_bx, v_rg_lambda, v_hg_lb, v_hg_norm_w, v_w_out, v_ple_norm_w, v_w_ple_gate, v_b_ple_gate, v_w_ple_proj, v_final_norm_w):
    given = dict(locals())
    x2, p2, tgt = x[0], p[0, 0], loss_target[0]
    wa_c, wx_c = _small_view("rg_wa", rg_wa), _small_view("rg_wx", rg_wx)

    w_in_b, l_out, l_pg, l_pp, cw3 = _gather_w_in(w_in[0], w_out[0], w_ple_gate[0], w_ple_proj[0], conv_w[0])
    g_ssem, g_rsem, g_lands, tok = _gather_rest_start([l_out, l_pg, l_pp])

    pf, pb, u_b = _fwd_inproj(x2, norm_mix_w, w_in_b, tok)
    h, ya = _fwd_rglru(pf, pb, cw3, conv_b, wa_c, rg_ba, wx_c, rg_bx, rg_lambda)
    yb, o, s_prev = _fwd_hgrn2(pf, pb, hg_lb, hg_norm_w)
    w_out_b, w_pg_b, w_pp_b = _gather_rest_wait(g_ssem, g_rsem, g_lands, yb)
    dh1, dyab, dwo_b, dwg_b, dwp_b, sm_tail = _tail_fwd_bwd(
        x2, p2, tgt, ya, yb, w_out_b.reshape(D, D), w_pg_b.reshape(D, D), w_pp_b,
        ple_norm_w, b_ple_gate, final_norm_w.reshape(1, D))

    QH = D // NSHARD // 2
    r1 = _rs_start("rs_start_tail", [dwo_b, dwg_b, dwp_b], (0, 1, 2, 3),
                   [_landing(QH, D), _landing(QH, D), _landing(PLE // 2, PLE)])
    d_b, sm_b = _bwd_hgrn2(pf, pb, o, s_prev, dyab, hg_lb, hg_norm_w, r1[4])
    dw23 = _bwd_inproj_dw("bwd_inproj_dw23", (2, 3), u_b, None, d_b)
    r2 = _rs_start("rs_start_in23", [dw23], (2, 3), [lax.empty((NSHARD, D // 2, SHW), BF16)], chip_sums=True)
    d_a, rg_c, sm_a = _bwd_rglru(pf, pb, h, dyab, cw3, conv_b, wa_c, rg_ba, wx_c, rg_bx, rg_lambda, r2[4])
    dw01 = _bwd_inproj_dw("bwd_inproj_dw01", (0, 1), u_b, d_a, d_b)
    r3 = _rs_start("rs_start_in01", [dw01], (0, 1), r2[3], chip_sums=True)
    grad_x, sm_in = _bwd_inproj_dx(x2, dh1, d_a, d_b, w_in_b, norm_mix_w, r3[4])

    parts1, lands1 = _rs_wait("rs_wait_tail", r1[0], r1[1], r1[2], (0, 1, 2, 3), r1[3], sm_in)
    parts2, lands2 = _rs_wait("rs_wait_in23", r2[0], r2[1], r2[2], (2, 3), r3[3], sm_in, chip_sums=True)
    parts3, lands3 = _rs_wait("rs_wait_in01", r3[0], r3[1], r3[2], (0, 1), lands2, sm_in, chip_sums=True)
    g_big, vred, rgred = _final_reduce([parts3[0], parts2[0]] + parts1, lands3 + lands1,
                                       sm_in, sm_tail, sm_a, sm_b, rg_c)

    upd_big = _adam_big(g_big, [given[n][0] for n in _BIG], [given["m_" + n][0] for n in _BIG],
                        [given["v_" + n][0] for n in _BIG])
    small = _adam_small(vred, rgred,
                        {n: _small_view(n, given[n]) for n in _SMALL_ORDER},
                        {n: _small_view(n, given["m_" + n]) for n in _SMALL_ORDER},
                        {n: _small_view(n, given["v_" + n]) for n in _SMALL_ORDER})

    loss = vred[0, ROW_LOSS, 0]
    outs = [loss, grad_x[None]]
    for ki in range(4):
        for n in _OUT_ORDER:
            if n in _BIG:
                i = _BIG.index(n)
                a = g_big[i] if ki == 0 else upd_big[i][ki - 1]
                outs.append(a[None])
            else:
                outs.append(small[n][ki].reshape(given[n].shape))
    return tuple(outs)
```

```python
import jax
import jax.numpy as jnp
from jax import lax
from jax.experimental import pallas as pl
from jax.experimental.pallas import tpu as pltpu

F32 = jnp.float32
BF16 = jnp.bfloat16
I32 = jnp.int32
MESH = pl.DeviceIdType.MESH
HIGHEST = lax.Precision.HIGHEST

T = 4096
D = 1024
DG = 512
DIN = 3072
PLE = 256
NH = 4
HD = 128
CH = 64
NCHUNK = T // CH
RGB = 64
EPS = 1e-6
RG_C = 8.0
NSHARD = 4
SHW = DIN // NSHARD
NDEV = 8

ADAM_LR = 0.001
ADAM_B1 = 0.9
ADAM_B2 = 0.999
ADAM_EPS = 1e-08
ADAM_WD = 0.01
ADAM_STEP = 10

VMEM_SPEC = pl.BlockSpec(memory_space=pltpu.VMEM)
HBM_SPEC = pl.BlockSpec(memory_space=pltpu.HBM)
SEM_SPEC = pl.BlockSpec(memory_space=pltpu.SEMAPHORE)
ANY_SPEC = pl.BlockSpec(memory_space=pl.ANY)
EFFECT = pltpu.SideEffectType.DATAFLOW_SIDE_EFFECTING
MIB = 1024 * 1024

VROWS = 16
ROW_NORM_MIX, ROW_FINAL_NORM, ROW_B_PG, ROW_PLE_NORM = 0, 1, 2, 3
ROW_CB_BA, ROW_BX_LAM, ROW_CW01, ROW_CW23, ROW_HG_LB, ROW_HG_NW, ROW_LOSS = 4, 5, 6, 7, 8, 9, 10


def _mm(a, b):
    return jnp.dot(a.astype(BF16), b.astype(BF16), preferred_element_type=F32)


def _mm_nt(a, b):
    return lax.dot_general(a.astype(BF16), b.astype(BF16), (((1,), (1,)), ((), ())),
                           preferred_element_type=F32)


def _mm_tn(a, b):
    return lax.dot_general(a.astype(BF16), b.astype(BF16), (((0,), (0,)), ((), ())),
                           preferred_element_type=F32)


def _mm_exact(a, b):
    return jnp.dot(a, b, precision=HIGHEST, preferred_element_type=F32)


def _sig_pair(x):
    e = jnp.exp(-jnp.abs(x))
    big = 1.0 / (1.0 + e)
    small = e * big
    pos = x >= 0
    return jnp.where(pos, big, small), jnp.where(pos, small, big)


def _sigmoid(x):
    return 1.0 / (1.0 + jnp.exp(-x))


def _rowsum(v):
    return jnp.sum(v, axis=0, keepdims=True)


def _shift_down(cur, prev8, d, rows):
    rolled = pltpu.roll(cur, d, 0)
    head = jnp.where(rows[0:8] < d, pltpu.roll(prev8, d, 0), rolled[0:8])
    return jnp.concatenate([head, rolled[8:]], axis=0)


def _shift_up(cur, next8, d, rows):
    n = cur.shape[0]
    rolled = pltpu.roll(cur, n - d, 0)
    tail = jnp.where(rows[0:8] >= 8 - d, pltpu.roll(next8, 8 - d, 0), rolled[n - 8:n])
    return jnp.concatenate([rolled[0:n - 8], tail], axis=0)


def _roll_in_groups(v, d):
    n, w = v.shape
    return pltpu.roll(v.reshape(n // 8, 8, w), d, 1).reshape(n, w)


def _cparams(sem, vmem_mib):
    return pltpu.CompilerParams(dimension_semantics=sem, vmem_limit_bytes=vmem_mib * MIB)


def _mesh_pos():
    x, y, c = lax.axis_index("x"), lax.axis_index("y"), lax.axis_index("c")
    chips = [(1 - x, y), (x, 1 - y), (1 - x, 1 - y)]
    return x, y, c, chips


def _remote(src, dst, ssem, rsem, dev):
    return pltpu.make_async_remote_copy(src_ref=src, dst_ref=dst, send_sem=ssem, recv_sem=rsem,
                                        device_id=dev, device_id_type=MESH)


def _gather_w_in(w_in, w_out, w_pg, w_pp, conv_w):
    shapes = [w_in.shape, w_out.shape, w_pg.shape, w_pp.shape]

    def body(win, wout, wpg, wpp, cw, o_in, o_out, o_pg, o_pp, o_cw, ssem, rsem):
        x, y, c, chips = _mesh_pos()
        kme = 2 * x + y
        sibling = (x, y, 1 - c)

        def cast(src, dst):
            for r0 in range(0, src.shape[0], 128):
                dst[kme, r0:r0 + 128, :] = src[r0:r0 + 128, :].astype(BF16)

        cast(win, o_in)
        o_cw[kme] = cw[...]

        hrows = D // 2
        mine = pl.ds(pl.multiple_of(c * hrows, 128), hrows)
        other = pl.ds(pl.multiple_of((1 - c) * hrows, 128), hrows)

        def half(k, rows):
            return o_in.at[k, rows]

        na = (x ^ c, y ^ (1 - c), c)
        nb = (x ^ (1 - c), y ^ c, c)
        ka = 2 * na[0] + na[1]
        kb = 2 * nb[0] + nb[1]
        kd = 2 * (1 - x) + (1 - y)
        sends = [_remote(half(kme, mine), half(kme, mine), ssem.at[0], rsem.at[0], na),
                 _remote(half(kme, mine), half(kme, mine), ssem.at[1], rsem.at[1], nb)]
        for j, (px, py) in enumerate(chips):
            sends.append(_remote(o_cw.at[kme], o_cw.at[kme], ssem.at[3 + j], rsem.at[3 + j], (px, py, c)))
        for cp in sends:
            cp.start()
        cast(wout, o_out)
        cast(wpg, o_pg)
        cast(wpp, o_pp)

        def passed_on(k, sem, dev):
            cp = _remote(half(k, mine), half(k, mine), ssem.at[sem], rsem.at[sem], dev)
            cp.start()
            sends.append(cp)

        _remote(half(ka, mine), half(ka, mine), ssem.at[0], rsem.at[0], na).wait_recv()
        passed_on(ka, 2, nb)
        passed_on(ka, 6, sibling)
        _remote(half(kb, mine), half(kb, mine), ssem.at[1], rsem.at[1], nb).wait_recv()
        passed_on(kb, 7, sibling)
        _remote(half(kd, mine), half(kd, mine), ssem.at[2], rsem.at[2], nb).wait_recv()
        passed_on(kd, 8, sibling)
        _remote(half(kb, other), half(kb, other), ssem.at[6], rsem.at[6], sibling).wait_recv()
        _remote(half(ka, other), half(ka, other), ssem.at[7], rsem.at[7], sibling).wait_recv()
        _remote(half(kd, other), half(kd, other), ssem.at[8], rsem.at[8], sibling).wait_recv()
        for j, (px, py) in enumerate(chips):
            kj = 2 * px + py
            _remote(o_cw.at[kj], o_cw.at[kj], ssem.at[3 + j], rsem.at[3 + j], (px, py, c)).wait_recv()
        for cp in sends:
            cp.wait_send()

    out_shape = [jax.ShapeDtypeStruct((NSHARD,) + s, BF16) for s in shapes]
    out_shape.append(jax.ShapeDtypeStruct((NSHARD,) + conv_w.shape, F32))
    return pl.pallas_call(
        body, name="gather_w_in", out_shape=out_shape,
        in_specs=[VMEM_SPEC] * 5, out_specs=[VMEM_SPEC] * 5,
        scratch_shapes=[pltpu.SemaphoreType.DMA((9,)), pltpu.SemaphoreType.DMA((9,))],
        compiler_params=pltpu.CompilerParams(vmem_limit_bytes=40 * MIB),
    )(w_in, w_out, w_pg, w_pp, conv_w)


def _gather_rest_start(lands):
    n = len(lands)

    def body(*refs):
        land_in = refs[0:n]
        ssem, rsem = refs[n], refs[n + 1]
        token = refs[2 * n + 2]
        x, y, c, chips = _mesh_pos()
        kme = 2 * x + y
        for p, land in enumerate(land_in):
            hrows = land.shape[1] // 2
            mine = pl.ds(pl.multiple_of(c * hrows, 128), hrows)
            for px, py in chips:
                for pc in range(2):
                    _remote(land.at[kme, mine], land.at[kme, mine], ssem.at[p], rsem.at[p], (px, py, pc)).start()
        token[...] = jnp.zeros_like(token)

    out_shape = ([pltpu.SemaphoreType.DMA((n,)), pltpu.SemaphoreType.DMA((n,))]
                 + [pltpu.HBM(a.shape, a.dtype) for a in lands] + [jax.ShapeDtypeStruct((8, 128), F32)])
    outs = pl.pallas_call(
        body, name="gather_rest_start", out_shape=out_shape,
        in_specs=[HBM_SPEC] * n, out_specs=[SEM_SPEC, SEM_SPEC] + [HBM_SPEC] * n + [VMEM_SPEC],
        input_output_aliases={i: 2 + i for i in range(n)},
        compiler_params=pltpu.CompilerParams(has_side_effects=EFFECT),
    )(*[pltpu.with_memory_space_constraint(a, pltpu.HBM) for a in lands])
    return outs[0], outs[1], list(outs[2:2 + n]), outs[2 + n]


def _gather_rest_wait(ssem, rsem, lands, after):
    n = len(lands)

    def body(*refs):
        land_in = refs[0:n]
        ssem_ref, rsem_ref = refs[n], refs[n + 1]
        x, y, c = lax.axis_index("x"), lax.axis_index("y"), lax.axis_index("c")
        for p, land in enumerate(land_in):
            three = land.at[pl.ds(0, 3)]
            cp = _remote(three, three, ssem_ref.at[p], rsem_ref.at[p], (x, y, c))
            cp.wait_send()
            cp.wait_recv()

    outs = pl.pallas_call(
        body, name="gather_rest_wait", out_shape=[pltpu.HBM(a.shape, a.dtype) for a in lands],
        in_specs=[HBM_SPEC] * n + [SEM_SPEC, SEM_SPEC, ANY_SPEC], out_specs=[HBM_SPEC] * n,
        input_output_aliases={i: i for i in range(n)},
        compiler_params=pltpu.CompilerParams(has_side_effects=EFFECT),
    )(*lands, ssem, rsem, after)
    return list(outs)


def _rs_start(name, parts, ks, lands, chip_sums=False):
    n = len(parts)

    def body(*refs):
        part_in, land_in = refs[0:n], refs[n:2 * n]
        ssem, rsem = refs[2 * n], refs[2 * n + 1]
        token = refs[4 * n + 2]
        x, y, c = lax.axis_index("x"), lax.axis_index("y"), lax.axis_index("c")
        kme = 2 * x + y
        me = 4 * x + 2 * y + c
        for p in range(n):
            hrows = land_in[p].shape[1]
            for i, k in enumerate(ks):
                if chip_sums:
                    @pl.when(kme != k)
                    def _():
                        _remote(part_in[p].at[i], land_in[p].at[kme], ssem.at[p], rsem.at[p], (k // 2, k % 2, c)).start()
                    continue
                for pc in range(2):
                    @pl.when(jnp.logical_or(kme != k, c != pc))
                    def _():
                        _remote(part_in[p].at[i, pl.ds(pc * hrows, hrows)], land_in[p].at[me],
                                ssem.at[p], rsem.at[p], (k // 2, k % 2, pc)).start()
        token[...] = jnp.zeros_like(token)

    arrays = list(parts) + list(lands)
    out_shape = ([pltpu.SemaphoreType.DMA((n,)), pltpu.SemaphoreType.DMA((n,))]
                 + [pltpu.HBM(a.shape, a.dtype) for a in arrays] + [jax.ShapeDtypeStruct((8, 128), F32)])
    outs = pl.pallas_call(
        body, name=name, out_shape=out_shape,
        in_specs=[HBM_SPEC] * (2 * n), out_specs=[SEM_SPEC, SEM_SPEC] + [HBM_SPEC] * (2 * n) + [VMEM_SPEC],
        input_output_aliases={i: 2 + i for i in range(2 * n)},
        compiler_params=pltpu.CompilerParams(has_side_effects=EFFECT),
    )(*[pltpu.with_memory_space_constraint(a, pltpu.HBM) for a in arrays])
    return outs[0], outs[1], list(outs[2:2 + n]), list(outs[2 + n:2 + 2 * n]), outs[2 + 2 * n]


def _rs_wait(name, ssem, rsem, parts, ks, lands, after, chip_sums=False):
    n = len(parts)

    def body(*refs):
        part_in, land_in = refs[0:n], refs[n:2 * n]
        ssem_ref, rsem_ref = refs[2 * n], refs[2 * n + 1]
        x, y, c = lax.axis_index("x"), lax.axis_index("y"), lax.axis_index("c")
        kme = 2 * x + y
        for p in range(n):
            piece = land_in[p].at[0]
            for k in ks:
                for pc in range(1 if chip_sums else 2):
                    mine = (kme == k) if chip_sums else jnp.logical_and(kme == k, c == pc)

                    @pl.when(jnp.logical_not(mine))
                    def _():
                        _remote(piece, piece, ssem_ref.at[p], rsem_ref.at[p], (x, y, c)).wait_send()
            owner = kme == ks[0]
            for k in ks[1:]:
                owner = jnp.logical_or(owner, kme == k)

            @pl.when(owner)
            def _():
                others = land_in[p].at[pl.ds(0, land_in[p].shape[0] - 1)]
                _remote(others, others, ssem_ref.at[p], rsem_ref.at[p], (x, y, c)).wait_recv()

    arrays = list(parts) + list(lands)
    outs = pl.pallas_call(
        body, name=name, out_shape=[pltpu.HBM(a.shape, a.dtype) for a in arrays],
        in_specs=[HBM_SPEC] * (2 * n) + [SEM_SPEC, SEM_SPEC, ANY_SPEC], out_specs=[HBM_SPEC] * (2 * n),
        input_output_aliases={i: i for i in range(2 * n)},
        compiler_params=pltpu.CompilerParams(has_side_effects=EFFECT),
    )(*arrays, ssem, rsem, after)
    return list(outs[0:n]), list(outs[n:2 * n])


def _reduce_exchange(parts, lands):
    shapes = [(2 * l.shape[1], l.shape[2]) for l in lands]
    step = 128

    def body(in01, in23, pout, ppg, ppp, l_in, l_out, l_pg, l_pp, g_in, g_out, g_pg, g_pp, ssem, rsem):
        x, y, c = lax.axis_index("x"), lax.axis_index("y"), lax.axis_index("c")
        kme = 2 * x + y
        me = 4 * x + 2 * y + c
        sibling = (x, y, 1 - c)
        sends = []
        for p, (land, gout) in enumerate([(l_in, g_in), (l_out, g_out), (l_pg, g_pg), (l_pp, g_pp)]):
            hrows = land.shape[1]
            mine0 = pl.multiple_of(c * hrows, step)
            for r0 in range(0, hrows, step):
                rs = pl.ds(mine0 + r0, step)
                if p == 0:
                    own = jnp.where(kme >= 2, in23[kme & 1, r0:r0 + step, :], in01[kme & 1, r0:r0 + step, :])
                    slot = kme
                else:
                    own = (pout, ppg, ppp)[p - 1][kme, rs, :]
                    slot = me
                s = jnp.zeros((step, land.shape[2]), F32)
                for j in range(land.shape[0]):
                    s = s + jnp.where(slot == j, own, land[j, r0:r0 + step, :]).astype(F32)
                gout[rs, :] = s
            mine = pl.ds(mine0, hrows)
            cp = _remote(gout.at[mine], gout.at[mine], ssem.at[p], rsem.at[p], sibling)
            cp.start()
            sends.append(cp)
        for p, (land, gout) in enumerate([(l_in, g_in), (l_out, g_out), (l_pg, g_pg), (l_pp, g_pp)]):
            hrows = land.shape[1]
            other = pl.ds(pl.multiple_of((1 - c) * hrows, step), hrows)
            _remote(gout.at[other], gout.at[other], ssem.at[p], rsem.at[p], sibling).wait_recv()
        for cp in sends:
            cp.wait_send()

    return pl.pallas_call(
        body, name="reduce_exchange", out_shape=[jax.ShapeDtypeStruct(s, F32) for s in shapes],
        in_specs=[VMEM_SPEC] * 9, out_specs=[VMEM_SPEC] * 4,
        scratch_shapes=[pltpu.SemaphoreType.DMA((4,)), pltpu.SemaphoreType.DMA((4,))],
        compiler_params=pltpu.CompilerParams(vmem_limit_bytes=48 * MIB),
    )(*parts, *lands)


def _small_allreduce(sm_in, sm_tail, sm_a, sm_b, rg_c):
    PR = DG // NDEV

    def body(in_ref, tail_ref, a_ref, b_ref, rg_ref, v_out, rg_out, vbuf, vrecv, rgrecv, ssem, rsem):
        x, y, c = lax.axis_index("x"), lax.axis_index("y"), lax.axis_index("c")
        me = 4 * x + 2 * y + c

        def pair(ref, r0, r1):
            return jnp.concatenate([ref[r0:r0 + 1, :], ref[r1:r1 + 1, :]], axis=1)

        rows = {
            ROW_NORM_MIX: in_ref[0:1, :], ROW_FINAL_NORM: tail_ref[1:2, :], ROW_B_PG: tail_ref[2:3, :],
            ROW_PLE_NORM: tail_ref[3:4, :], ROW_CB_BA: pair(a_ref, 0, 1), ROW_BX_LAM: pair(a_ref, 2, 3),
            ROW_CW01: pair(a_ref, 4, 5), ROW_CW23: pair(a_ref, 6, 7), ROW_HG_LB: pair(b_ref, 2, 3),
            ROW_HG_NW: jnp.concatenate([b_ref[1:2, :], jnp.zeros((1, DG), F32)], axis=1),
            ROW_LOSS: tail_ref[4:5, :],
        }
        vbuf[...] = jnp.zeros_like(vbuf)
        for r, row in rows.items():
            for j in range(NDEV):
                vbuf[j, r:r + 1, :] = row[:, j * 128:(j + 1) * 128]

        def peer(mask):
            px = x ^ ((mask >> 2) & 1)
            py = y ^ ((mask >> 1) & 1)
            pc = c ^ (mask & 1)
            return (px, py, pc), 4 * px + 2 * py + pc

        def rg_rows(r):
            return pl.ds(pl.multiple_of(r * PR, PR), PR)

        first = []
        for mask in range(1, NDEV):
            dev, r = peer(mask)
            i = mask - 1
            cp = _remote(vbuf.at[r], vrecv.at[i], ssem.at[i], rsem.at[i], dev)
            cp.start()
            first.append(cp)
            cp = _remote(rg_ref.at[rg_rows(r)], rgrecv.at[i], ssem.at[7 + i], rsem.at[7 + i], dev)
            cp.start()
            first.append(cp)
        sv = vbuf[me]
        sr = rg_ref[rg_rows(me), :]
        for i in range(NDEV - 1):
            first[2 * i].wait_recv()
            first[2 * i + 1].wait_recv()
            sv = sv + vrecv[i]
            sr = sr + rgrecv[i]
        v_out[me] = sv
        rg_out[rg_rows(me), :] = sr
        second = []
        for mask in range(1, NDEV):
            dev, r = peer(mask)
            i = mask - 1
            cp = _remote(v_out.at[me], v_out.at[me], ssem.at[14 + i], rsem.at[14 + i], dev)
            cp.start()
            second.append(cp)
            cp = _remote(rg_out.at[rg_rows(me)], rg_out.at[rg_rows(me)], ssem.at[21 + i], rsem.at[21 + i], dev)
            cp.start()
            second.append(cp)
        for mask in range(1, NDEV):
            dev, r = peer(mask)
            i = mask - 1
            _remote(v_out.at[r], v_out.at[r], ssem.at[14 + i], rsem.at[14 + i], dev).wait_recv()
            _remote(rg_out.at[rg_rows(r)], rg_out.at[rg_rows(r)], ssem.at[21 + i], rsem.at[21 + i], dev).wait_recv()
        for cp in first + second:
            cp.wait_send()

    return pl.pallas_call(
        body, name="small_allreduce",
        out_shape=[jax.ShapeDtypeStruct((NDEV, VROWS, 128), F32), jax.ShapeDtypeStruct((DG, 128), F32)],
        in_specs=[VMEM_SPEC] * 5, out_specs=[VMEM_SPEC] * 2,
        scratch_shapes=[pltpu.VMEM((NDEV, VROWS, 128), F32), pltpu.VMEM((NDEV - 1, VROWS, 128), F32),
                        pltpu.VMEM((NDEV - 1, PR, 128), F32),
                        pltpu.SemaphoreType.DMA((28,)), pltpu.SemaphoreType.DMA((28,))],
    )(sm_in, sm_tail, sm_a, sm_b, rg_c)


def _final_reduce(parts, lands, sm_in, sm_tail, sm_a, sm_b, rg_c):
    shapes = [(2 * l.shape[1], l.shape[2]) for l in lands]
    step = 128
    PR = DG // NDEV

    def body(in01, in23, pout, ppg, ppp, l_in, l_out, l_pg, l_pp, in_ref, tail_ref, a_ref, b_ref, rg_ref,
             g_in, g_out, g_pg, g_pp, v_out, rg_out, vbuf, vrecv, rgrecv,
             lb_in, lb_out, lb_pg, lb_pp, ob_in, ob_out, ob_pg, ob_pp, lsem, bsem_s, bsem_r, ssem, rsem):
        x, y, c = lax.axis_index("x"), lax.axis_index("y"), lax.axis_index("c")
        kme = 2 * x + y
        me = 4 * x + 2 * y + c
        sibling = (x, y, 1 - c)

        lands_hbm = [l_in, l_out, l_pg, l_pp]
        land_bufs = [lb_in, lb_out, lb_pg, lb_pp]
        own_bufs = [ob_in, ob_out, ob_pg, ob_pp]
        fetches = []
        for p in range(4):
            cp = pltpu.make_async_copy(lands_hbm[p], land_bufs[p], lsem.at[p])
            cp.start()
            fetches.append(cp)

        @pl.when(kme >= 2)
        def _():
            pltpu.make_async_copy(in23.at[kme & 1], ob_in, lsem.at[4]).start()

        @pl.when(kme < 2)
        def _():
            pltpu.make_async_copy(in01.at[kme & 1], ob_in, lsem.at[4]).start()

        own_fetches = [pltpu.make_async_copy(in01.at[0], ob_in, lsem.at[4])]
        for p, part in enumerate([pout, ppg, ppp]):
            hrows = own_bufs[p + 1].shape[0]
            cp = pltpu.make_async_copy(part.at[kme, pl.ds(pl.multiple_of(c * hrows, step), hrows)],
                                       own_bufs[p + 1], lsem.at[5 + p])
            cp.start()
            own_fetches.append(cp)

        def pair(ref, r0, r1):
            return jnp.concatenate([ref[r0:r0 + 1, :], ref[r1:r1 + 1, :]], axis=1)

        rows = {
            ROW_NORM_MIX: in_ref[0:1, :], ROW_FINAL_NORM: tail_ref[1:2, :], ROW_B_PG: tail_ref[2:3, :],
            ROW_PLE_NORM: tail_ref[3:4, :], ROW_CB_BA: pair(a_ref, 0, 1), ROW_BX_LAM: pair(a_ref, 2, 3),
            ROW_CW01: pair(a_ref, 4, 5), ROW_CW23: pair(a_ref, 6, 7), ROW_HG_LB: pair(b_ref, 2, 3),
            ROW_HG_NW: jnp.concatenate([b_ref[1:2, :], jnp.zeros((1, DG), F32)], axis=1),
            ROW_LOSS: tail_ref[4:5, :],
        }
        vbuf[...] = jnp.zeros_like(vbuf)
        for r, row in rows.items():
            for j in range(NDEV):
                vbuf[j, r:r + 1, :] = row[:, j * 128:(j + 1) * 128]

        def peer(mask):
            px = x ^ ((mask >> 2) & 1)
            py = y ^ ((mask >> 1) & 1)
            pc = c ^ (mask & 1)
            return (px, py, pc), 4 * px + 2 * py + pc

        def rg_rows(r):
            return pl.ds(pl.multiple_of(r * PR, PR), PR)

        first = []
        for mask in range(1, NDEV):
            dev, r = peer(mask)
            i = mask - 1
            cp = _remote(vbuf.at[r], vrecv.at[i], ssem.at[i], rsem.at[i], dev)
            cp.start()
            first.append(cp)
            cp = _remote(rg_ref.at[rg_rows(r)], rgrecv.at[i], ssem.at[7 + i], rsem.at[7 + i], dev)
            cp.start()
            first.append(cp)

        big = [(lb_in, g_in), (lb_out, g_out), (lb_pg, g_pg), (lb_pp, g_pp)]
        swaps = []
        for p, (land, gout) in enumerate(big):
            fetches[p].wait()
            own_fetches[p].wait()
            hrows = land.shape[1]
            mine0 = pl.multiple_of(c * hrows, step)
            slot = kme if p == 0 else me
            for r0 in range(0, hrows, step):
                rs = pl.ds(mine0 + r0, step)
                own = own_bufs[p][r0:r0 + step, :]
                s = jnp.zeros((step, land.shape[2]), F32)
                for j in range(land.shape[0]):
                    s = s + jnp.where(slot == j, own, land[j, r0:r0 + step, :]).astype(F32)
                gout[rs, :] = s
            mine = pl.ds(mine0, hrows)
            cp = _remote(gout.at[mine], gout.at[mine], bsem_s.at[p], bsem_r.at[p], sibling)
            cp.start()
            swaps.append(cp)

        sv = vbuf[me]
        sr = rg_ref[rg_rows(me), :]
        for i in range(NDEV - 1):
            first[2 * i].wait_recv()
            first[2 * i + 1].wait_recv()
            sv = sv + vrecv[i]
            sr = sr + rgrecv[i]
        v_out[me] = sv
        rg_out[rg_rows(me), :] = sr
        second = []
        for mask in range(1, NDEV):
            dev, r = peer(mask)
            i = mask - 1
            cp = _remote(v_out.at[me], v_out.at[me], ssem.at[14 + i], rsem.at[14 + i], dev)
            cp.start()
            second.append(cp)
            cp = _remote(rg_out.at[rg_rows(me)], rg_out.at[rg_rows(me)], ssem.at[21 + i], rsem.at[21 + i], dev)
            cp.start()
            second.append(cp)
        for p, (land, gout) in enumerate(big):
            hrows = land.shape[1]
            other = pl.ds(pl.multiple_of((1 - c) * hrows, step), hrows)
            _remote(gout.at[other], gout.at[other], bsem_s.at[p], bsem_r.at[p], sibling).wait_recv()
        for mask in range(1, NDEV):
            dev, r = peer(mask)
            i = mask - 1
            _remote(v_out.at[r], v_out.at[r], ssem.at[14 + i], rsem.at[14 + i], dev).wait_recv()
            _remote(rg_out.at[rg_rows(r)], rg_out.at[rg_rows(r)], ssem.at[21 + i], rsem.at[21 + i], dev).wait_recv()
        for cp in first + swaps + second:
            cp.wait_send()

    out_shape = [jax.ShapeDtypeStruct(s, F32) for s in shapes]
    out_shape += [jax.ShapeDtypeStruct((NDEV, VROWS, 128), F32), jax.ShapeDtypeStruct((DG, 128), F32)]
    outs = pl.pallas_call(
        body, name="final_reduce", out_shape=out_shape,
        in_specs=[ANY_SPEC] * 9 + [VMEM_SPEC] * 5, out_specs=[VMEM_SPEC] * 6,
        scratch_shapes=[pltpu.VMEM((NDEV, VROWS, 128), F32), pltpu.VMEM((NDEV - 1, VROWS, 128), F32),
                        pltpu.VMEM((NDEV - 1, PR, 128), F32)]
        + [pltpu.VMEM(l.shape, BF16) for l in lands]
        + [pltpu.VMEM(l.shape[1:], BF16) for l in lands]
        + [pltpu.SemaphoreType.DMA((8,)),
                        pltpu.SemaphoreType.DMA((4,)), pltpu.SemaphoreType.DMA((4,)),
                        pltpu.SemaphoreType.DMA((28,)), pltpu.SemaphoreType.DMA((28,))],
        compiler_params=pltpu.CompilerParams(vmem_limit_bytes=48 * MIB),
    )(*parts, *lands, sm_in, sm_tail, sm_a, sm_b, rg_c)
    return list(outs[0:4]), outs[4], outs[5]


def _adam_rows(w, g, m, v):
    m2 = ADAM_B1 * m + (1.0 - ADAM_B1) * g
    v2 = ADAM_B2 * v + (1.0 - ADAM_B2) * (g * g)
    m_hat = m2 / (1.0 - ADAM_B1 ** ADAM_STEP)
    v_hat = v2 / (1.0 - ADAM_B2 ** ADAM_STEP)
    delta = -ADAM_LR * (m_hat / (jnp.sqrt(v_hat) + ADAM_EPS) + ADAM_WD * w)
    return delta, m2, v2


def _adam_big(gs, ws, ms, vs):
    n = len(gs)
    steps = 8

    def body(*refs):
        ins, outs = refs[:4 * n], refs[4 * n:]
        for i in range(n):
            g, w, m, v = (r[...] for r in ins[4 * i:4 * i + 4])
            d, m2, v2 = _adam_rows(w, g, m, v)
            outs[3 * i][...] = d
            outs[3 * i + 1][...] = m2
            outs[3 * i + 2][...] = v2

    in_specs, out_specs, out_shape, args = [], [], [], []
    for g, w, m, v in zip(gs, ws, ms, vs):
        r, c = w.shape
        spec = lambda: pl.BlockSpec((r // steps, c), lambda i: (i, 0))
        in_specs += [spec() for _ in range(4)]
        out_specs += [spec() for _ in range(3)]
        out_shape += [jax.ShapeDtypeStruct((r, c), F32)] * 3
        args += [pltpu.with_memory_space_constraint(a, pltpu.HBM) for a in (g, w, m, v)]
    outs = pl.pallas_call(
        body, name="adam_big", grid=(steps,), in_specs=in_specs, out_specs=out_specs, out_shape=out_shape,
        compiler_params=_cparams(("parallel",), 32),
    )(*args)
    return [tuple(outs[3 * i:3 * i + 3]) for i in range(n)]


_VEC_PARAMS = [
    ("norm_mix_w", ROW_NORM_MIX, 0, D), ("final_norm_w", ROW_FINAL_NORM, 0, D),
    ("b_ple_gate", ROW_B_PG, 0, D), ("ple_norm_w", ROW_PLE_NORM, 0, D),
    ("conv_b", ROW_CB_BA, 0, DG), ("rg_ba", ROW_CB_BA, DG, DG),
    ("rg_bx", ROW_BX_LAM, 0, DG), ("rg_lambda", ROW_BX_LAM, DG, DG),
    ("hg_norm_w", ROW_HG_NW, 0, HD),
]
_SMALL_ORDER = [n for n, _, _, _ in _VEC_PARAMS] + ["hg_lb", "conv_w", "rg_wa", "rg_wx"]


def _adam_small(vred, rgred, ws, ms, vs):
    names = _SMALL_ORDER
    n = len(names)

    def body(vred_ref, rg_ref, *refs):
        w_refs = dict(zip(names, refs[0:n]))
        m_refs = dict(zip(names, refs[n:2 * n]))
        v_refs = dict(zip(names, refs[2 * n:3 * n]))
        outs = refs[3 * n:]
        o_refs = {nm: outs[4 * i:4 * i + 4] for i, nm in enumerate(names)}
        kme = 2 * lax.axis_index("x") + lax.axis_index("y")

        def update(nm, g, idx):
            d, m2, v2 = _adam_rows(w_refs[nm][idx], g, m_refs[nm][idx], v_refs[nm][idx])
            og, od, om, ov = o_refs[nm]
            og[idx] = g
            od[idx] = d
            om[idx] = m2
            ov[idx] = v2

        def packed(row, lane0, width):
            return jnp.concatenate([vred_ref[j, row:row + 1, :] for j in range(lane0 // 128, (lane0 + width) // 128)],
                                   axis=1)

        everything = (slice(None), slice(None))
        for nm, row, lane0, width in _VEC_PARAMS:
            update(nm, packed(row, lane0, width), everything)
        for r in range(2):
            update("hg_lb", packed(ROW_HG_LB, r * DG, DG), (slice(r, r + 1), slice(None)))
        for j in range(4):
            g = vred_ref[(j % 2) * 4 + kme, ROW_CW01 + j // 2:ROW_CW01 + j // 2 + 1, :]
            update("conv_w", g, (slice(j, j + 1), slice(None)))
        for r0 in range(0, DG, 128):
            rs = (slice(r0, r0 + 128), slice(None))
            both = rg_ref[r0:r0 + 128, :]
            update("rg_wa", both[:, 0:RGB], rs)
            update("rg_wx", pltpu.roll(both, RGB, 1)[:, 0:RGB], rs)

    args = [vred, rgred] + [d[nm] for d in (ws, ms, vs) for nm in names]
    out_shape = []
    for nm in names:
        out_shape += [jax.ShapeDtypeStruct(ws[nm].shape, F32)] * 4
    whole = lambda s: pl.BlockSpec(s.shape, lambda i, nd=len(s.shape): (0,) * nd)
    outs = pl.pallas_call(
        body, name="adam_small", out_shape=out_shape, grid=(1,),
        in_specs=[whole(a) for a in args], out_specs=[whole(s) for s in out_shape],
    )(*args)
    return {nm: tuple(outs[4 * i:4 * i + 4]) for i, nm in enumerate(names)}


def _fwd_inproj(x, nw, w_in_b, dep):
    tm = 512

    def body(x_ref, nw_ref, w_ref, dep_ref, pf_ref, pb_ref, u_ref):
        xv = x_ref[...]
        s = lax.rsqrt(jnp.mean(xv * xv, axis=-1, keepdims=True) + EPS)
        u = (xv * s * nw_ref[...]).astype(BF16)
        u_ref[...] = u
        r = [jnp.dot(u, w_ref[k], preferred_element_type=F32) for k in range(NSHARD)]
        h = DG // 2
        pf_ref[:, 0:DG] = r[0][:, 0:DG]
        pb_ref[:, 0:h] = r[0][:, DG:SHW].astype(BF16)
        pb_ref[:, h:DG] = r[1][:, 0:h].astype(BF16)
        pb_ref[:, DG:2 * DG] = r[1][:, h:SHW].astype(BF16)
        pf_ref[:, DG:2 * DG] = r[2][:, 0:DG]
        pb_ref[:, 2 * DG:2 * DG + h] = r[2][:, DG:SHW].astype(BF16)
        pb_ref[:, 2 * DG + h:3 * DG] = r[3][:, 0:h].astype(BF16)
        pb_ref[:, 3 * DG:4 * DG] = r[3][:, h:SHW].astype(BF16)

    return pl.pallas_call(
        body, name="fwd_inproj", grid=(T // tm,),
        in_specs=[pl.BlockSpec((tm, D), lambda i: (i, 0)),
                  pl.BlockSpec((1, D), lambda i: (0, 0)),
                  pl.BlockSpec((NSHARD, D, SHW), lambda i: (0, 0, 0), pipeline_mode=pl.Buffered(1)), ANY_SPEC],
        out_specs=[pl.BlockSpec((tm, 2 * DG), lambda i: (i, 0)),
                   pl.BlockSpec((tm, 4 * DG), lambda i: (i, 0)),
                   pl.BlockSpec((tm, D), lambda i: (i, 0))],
        out_shape=[jax.ShapeDtypeStruct((T, 2 * DG), F32), jax.ShapeDtypeStruct((T, 4 * DG), BF16),
                   jax.ShapeDtypeStruct((T, D), BF16)],
        compiler_params=_cparams(("parallel",), 48),
    )(x, nw, w_in_b, dep)


def _conv_rows(cw_ref):
    return [jnp.concatenate([cw_ref[k, j:j + 1, :] for k in range(NSHARD)], axis=1) for j in range(4)]


def _rg_conv(xa, prev8, cw, cb, rows):
    taps = [_shift_down(xa, prev8, 3, rows), _shift_down(xa, prev8, 2, rows),
            _shift_down(xa, prev8, 1, rows), xa]
    xc = cb
    for j in range(4):
        xc = xc + taps[j] * cw[j]
    return xc, taps


def _block_mask():
    r = lax.broadcasted_iota(I32, (DG, DG), 0)
    c = lax.broadcasted_iota(I32, (DG, DG), 1)
    return (r >> 6) == (c >> 6)


def _dense_from_blocks(wc):
    j = lax.broadcasted_iota(I32, (RGB, DG), 0)
    c = lax.broadcasted_iota(I32, (RGB, DG), 1)
    spread = _mm_exact(wc, ((c & (RGB - 1)) == j).astype(F32))
    return jnp.where(_block_mask(), spread, 0.0)


def _blocks_from_dense(da, dx):
    c = lax.broadcasted_iota(I32, (DG, 128), 0)
    j = lax.broadcasted_iota(I32, (DG, 128), 1)
    hit = (c & (RGB - 1)) == (j & (RGB - 1))
    mask = _block_mask()
    return (_mm_exact(jnp.where(mask, da, 0.0), (hit & (j < RGB)).astype(F32))
            + _mm_exact(jnp.where(mask, dx, 0.0), (hit & (j >= RGB)).astype(F32)))


def _rg_gates(xc, wa, ba, wx, bx, sp, first_row):
    r = _sigmoid(_mm(xc, wa) + ba)
    i = _sigmoid(_mm(xc, wx) + bx)
    log_a = (-RG_C) * r * sp
    a = jnp.exp(log_a)
    a2 = a * a
    one_m_a2 = -jnp.tanh(log_a) * (a2 + 1.0)
    mult = jnp.where(first_row, 1.0, jnp.sqrt(one_m_a2))
    return r, i, a, a2, mult


def _softplus(z):
    return jnp.maximum(z, 0.0) + jnp.log1p(jnp.exp(-jnp.abs(z)))


def _fwd_rglru(pf, pb, cw3, conv_b, wa_c, ba, wx_c, bx, lam):
    tm = 512
    ng = tm // 8

    def body(xa_ref, ga_ref, cw_ref, cb_ref, wa_ref, ba_ref, wx_ref, bx_ref, lam_ref,
             h_ref, ya_ref, a_s, u_s, tail_s, hc_s, wa_s, wx_s):
        i = pl.program_id(0)

        @pl.when(i == 0)
        def _():
            tail_s[...] = jnp.zeros_like(tail_s)
            hc_s[...] = jnp.zeros_like(hc_s)
            wa_s[...] = _dense_from_blocks(wa_ref[...]).astype(BF16)
            wx_s[...] = _dense_from_blocks(wx_ref[...]).astype(BF16)

        rows = lax.broadcasted_iota(I32, (tm, DG), 0)
        xa = xa_ref[...]
        xc, _ = _rg_conv(xa, tail_s[...], _conv_rows(cw_ref), cb_ref[...], rows)
        tail_s[...] = xa[tm - 8:tm, :]
        sp = _softplus(-lam_ref[...])
        rp = _mm(xc, wa_s[...]) + ba_ref[...]
        ip = _mm(xc, wx_s[...]) + bx_ref[...]
        rb = 64
        rows_b = lax.broadcasted_iota(I32, (rb, DG), 0)
        rows8 = rows_b & 7
        carry = hc_s[0:1, :]
        for b0 in range(0, tm, rb):
            sl = slice(b0, b0 + rb)
            r = _sigmoid(rp[sl])
            ig = _sigmoid(ip[sl])
            log_a = (-RG_C) * r * sp
            av = jnp.exp(log_a)
            mult = jnp.sqrt(-jnp.tanh(log_a) * (av * av + 1.0))
            if b0 == 0:
                mult = jnp.where((rows_b + i * tm) == 0, 1.0, mult)
            uv = mult * (ig * xc[sl])
            for d in (1, 2, 4):
                keep = rows8 >= d
                uv = uv + av * jnp.where(keep, _roll_in_groups(uv, d), 0.0)
                av = av * jnp.where(keep, _roll_in_groups(av, d), 1.0)
            ga = ga_ref[sl, :].astype(F32)
            gate = ga * _sigmoid(ga)
            hs = []
            for g in range(rb // 8):
                gs = slice(g * 8, (g + 1) * 8)
                hv = uv[gs] + av[gs] * carry
                carry = hv[7:8, :]
                hs.append(hv)
            hb = jnp.concatenate(hs, axis=0)
            h_ref[sl, :] = hb
            ya_ref[sl, :] = (hb * gate).astype(BF16)
        hc_s[0:1, :] = carry

    vec = lambda: pl.BlockSpec((1, DG), lambda i: (0, 0))
    blocks = lambda: pl.BlockSpec((DG, RGB), lambda i: (0, 0))
    return pl.pallas_call(
        body, name="fwd_rglru", grid=(T // tm,),
        in_specs=[pl.BlockSpec((tm, DG), lambda i: (i, 0)),
                  pl.BlockSpec((tm, DG), lambda i: (i, 0)),
                  pl.BlockSpec((NSHARD, 4, 128), lambda i: (0, 0, 0)), vec(),
                  blocks(), vec(), blocks(), vec(), vec()],
        out_specs=[pl.BlockSpec((tm, DG), lambda i: (i, 0)),
                   pl.BlockSpec((tm, DG), lambda i: (i, 0))],
        out_shape=[jax.ShapeDtypeStruct((T, DG), F32), jax.ShapeDtypeStruct((T, DG), BF16)],
        scratch_shapes=[pltpu.VMEM((tm, DG), F32), pltpu.VMEM((tm, DG), F32),
                        pltpu.VMEM((8, DG), F32), pltpu.VMEM((8, DG), F32),
                        pltpu.VMEM((DG, DG), BF16), pltpu.VMEM((DG, DG), BF16)],
        compiler_params=_cparams(("arbitrary",), 48),
    )(pf, pb, cw3, conv_b, wa_c, ba, wx_c, bx, lam)


def _hg_lower_bound(lb_ref):
    return _sig_pair(lb_ref[0:1, :] - lb_ref[1:2, :])


def _hg_gates(fz, lb, one_m_lb):
    sg, sn = _sig_pair(fz)
    f = lb + one_m_lb * sg
    return sg, sn, f, jnp.log(f), one_m_lb * sn


def _tri(lower):
    r = lax.broadcasted_iota(I32, (CH, CH), 0)
    c = lax.broadcasted_iota(I32, (CH, CH), 1)
    return (r >= c) if lower else (r <= c)


def _chunk_cumsum(v, rows64):
    for d in (1, 2, 4, 8, 16, 32):
        v = v + jnp.where(rows64 >= d, pltpu.roll(v, d, 0), 0.0)
    return v


def _chunk_rev_cumsum(v, rows64):
    n = v.shape[0]
    for d in (1, 2, 4, 8, 16, 32):
        v = v + jnp.where(rows64 < CH - d, pltpu.roll(v, n - d, 0), 0.0)
    return v


def _hg_recompute(q_ref, f_ref, lb, one_m_lb, rows64, eb_s, enb_s, ekd_s, qe_s, ke_s, kd_s, dec_s):
    nc = q_ref.shape[0] // CH
    sg, sn, f, logf, k = _hg_gates(f_ref[...], lb, one_m_lb)
    q = q_ref[...].astype(F32)
    sq = _sigmoid(q)
    qs = q * sq * (HD ** -0.5)
    b = _chunk_cumsum(logf, rows64)
    for c in range(nc):
        rs = slice(c * CH, (c + 1) * CH)
        b_c = b[rs]
        bl = b_c[CH - 1:CH, :]
        eb, enb, ekd = jnp.exp(b_c), jnp.exp(-b_c), jnp.exp(bl - b_c)
        if eb_s is not None:
            eb_s[rs, :] = eb
            enb_s[rs, :] = enb
            ekd_s[rs, :] = ekd
        qe_s[rs, :] = (qs[rs] * eb).astype(BF16)
        ke_s[rs, :] = (k[rs] * enb).astype(BF16)
        kd_s[rs, :] = (k[rs] * ekd).astype(BF16)
        dec_s[c:c + 1, :] = jnp.exp(bl)
    return sg, sn, f, k, q, sq, qs


def _fwd_hgrn2(pf, pb, hg_lb, hg_nw):
    tm = 512
    nc = tm // CH

    def body(q_ref, f_ref, v_ref, g_ref, lb_ref, nw_ref, yb_ref, o_ref, sp_ref,
             st_s, qe_s, ke_s, kd_s, vb_s, dec_s, p_s, ds_s):
        i = pl.program_id(0)

        @pl.when(i == 0)
        def _():
            st_s[...] = jnp.zeros_like(st_s)

        lb, one_m_lb = _hg_lower_bound(lb_ref)
        rows64 = lax.broadcasted_iota(I32, (tm, DG), 0) & (CH - 1)
        _hg_recompute(q_ref, f_ref, lb, one_m_lb, rows64, None, None, None, qe_s, ke_s, kd_s, dec_s)
        vb_s[...] = v_ref[...]
        mask = _tri(True)
        items = [(c, hd, slice(c * CH, (c + 1) * CH), slice(hd * HD, (hd + 1) * HD))
                 for c in range(nc) for hd in range(NH)]
        for c, hd, rs, cols in items:
            p_s[c * NH + hd] = jnp.where(mask, _mm_nt(qe_s[rs, cols], ke_s[rs, cols]), 0.0).astype(BF16)
            ds_s[c * NH + hd] = _mm_tn(vb_s[rs, cols], kd_s[rs, cols])
        for c, hd, rs, cols in items:
            st = st_s[hd]
            sp_ref[hd, c] = st
            st_s[hd] = st * dec_s[c:c + 1, cols] + ds_s[c * NH + hd]
        for c, hd, rs, cols in items:
            o_ref[rs, cols] = _mm(p_s[c * NH + hd], vb_s[rs, cols]) + _mm_nt(qe_s[rs, cols], sp_ref[hd, c])
        nw = nw_ref[...]
        for hd in range(NH):
            cols = slice(hd * HD, (hd + 1) * HD)
            o = o_ref[:, cols]
            so = lax.rsqrt(jnp.mean(o * o, axis=-1, keepdims=True) + EPS)
            g = g_ref[:, cols].astype(F32)
            sg = _sigmoid(g)
            yb_ref[:, cols] = (o * so * nw * (g * sg)).astype(BF16)

    col = lambda j: pl.BlockSpec((tm, DG), lambda i: (i, j))
    return pl.pallas_call(
        body, name="fwd_hgrn2", grid=(T // tm,),
        in_specs=[col(1), col(1), col(2), col(3),
                  pl.BlockSpec((2, DG), lambda i: (0, 0)),
                  pl.BlockSpec((1, HD), lambda i: (0, 0))],
        out_specs=[pl.BlockSpec((tm, DG), lambda i: (i, 0)),
                   pl.BlockSpec((tm, DG), lambda i: (i, 0)),
                   pl.BlockSpec((NH, nc, HD, HD), lambda i: (0, i, 0, 0))],
        out_shape=[jax.ShapeDtypeStruct((T, DG), BF16), jax.ShapeDtypeStruct((T, DG), F32),
                   jax.ShapeDtypeStruct((NH, NCHUNK, HD, HD), F32)],
        scratch_shapes=[pltpu.VMEM((NH, HD, HD), F32),
                        pltpu.VMEM((tm, DG), BF16), pltpu.VMEM((tm, DG), BF16), pltpu.VMEM((tm, DG), BF16),
                        pltpu.VMEM((tm, DG), BF16), pltpu.VMEM((nc, DG), F32),
                        pltpu.VMEM((nc * NH, CH, CH), BF16), pltpu.VMEM((nc * NH, HD, HD), F32)],
        compiler_params=_cparams(("arbitrary",), 48),
    )(pb, pf, pb, pb, hg_lb, hg_nw)


def _tail_fwd_bwd(x, p, tgt, ya, yb, w_out_b, w_pg_b, w_pp_b, ple_nw, b_pg, fnw):
    tm = 512
    nt = T // tm
    QR = D // NSHARD

    def body(x_ref, p_ref, t_ref, ya_ref, yb_ref, wo_ref, wg_ref, wp_ref, pw_ref, b_ref, fw_ref,
             dh1_ref, dyab_ref, dwo_ref, dwg_ref, dwp_ref, sm_ref, dwo_s, dwg_s, dwp_s):
        i = pl.program_id(0)

        @pl.when(i == 0)
        def _():
            dwo_s[...] = jnp.zeros_like(dwo_s)
            dwg_s[...] = jnp.zeros_like(dwg_s)
            dwp_s[...] = jnp.zeros_like(dwp_s)
            sm_ref[...] = jnp.zeros_like(sm_ref)

        ya = ya_ref[...]
        yb = yb_ref[...]
        pv = p_ref[...].astype(BF16)
        pw = pw_ref[...]
        fw = fw_ref[...]
        h1 = x_ref[...] + _mm(ya, wo_ref[0:DG, :]) + _mm(yb, wo_ref[DG:D, :])
        s2 = lax.rsqrt(jnp.mean(h1 * h1, axis=-1, keepdims=True) + EPS)
        n2h = h1 * s2
        n2 = (n2h * pw).astype(BF16)
        z = _mm(n2, wg_ref[...]) + b_ref[...]
        gate = _sigmoid(z)
        pp = jnp.concatenate([_mm(pv, wp_ref[k]) for k in range(NSHARD)], axis=1)
        h2 = h1 + gate * pp
        s3 = lax.rsqrt(jnp.mean(h2 * h2, axis=-1, keepdims=True) + EPS)
        hn = h2 * s3
        err = hn * fw - t_ref[...]
        sm_ref[0:1, :] += _rowsum(err * err)
        dy = err * (1.0 / D)
        sm_ref[1:2, :] += _rowsum(dy * hn)
        g3 = dy * fw
        dh2 = s3 * (g3 - hn * jnp.mean(g3 * hn, axis=-1, keepdims=True))
        dpp = (dh2 * gate).astype(BF16)
        dz = dh2 * pp * gate * (1.0 - gate)
        sm_ref[2:3, :] += _rowsum(dz)
        dzb = dz.astype(BF16)
        dwg_s[...] += _mm_tn(n2, dzb)
        dn2 = _mm_nt(dzb, wg_ref[...])
        for k in range(NSHARD):
            dwp_s[k] += _mm_tn(pv, dpp[:, k * PLE:(k + 1) * PLE])
        sm_ref[3:4, :] += _rowsum(dn2 * n2h)
        g2 = dn2 * pw
        dh1 = dh2 + s2 * (g2 - n2h * jnp.mean(g2 * n2h, axis=-1, keepdims=True))
        dh1_ref[...] = dh1
        dh1b = dh1.astype(BF16)
        dyab_ref[...] = _mm_nt(dh1b, wo_ref[...])
        dwo_s[0:DG, :] += _mm_tn(ya, dh1b)
        dwo_s[DG:D, :] += _mm_tn(yb, dh1b)

        @pl.when(i == nt - 1)
        def _():
            total = jnp.sum(sm_ref[0:1, :], axis=-1, keepdims=True) * (0.5 / D)
            sm_ref[4:5, :] = jnp.broadcast_to(total, (1, D))
            for k in range(NSHARD):
                dwo_ref[k] = dwo_s[k * QR:(k + 1) * QR, :].astype(BF16)
                dwg_ref[k] = dwg_s[k * QR:(k + 1) * QR, :].astype(BF16)
                dwp_ref[k] = dwp_s[k].astype(BF16)

    row = lambda w: pl.BlockSpec((tm, w), lambda i: (i, 0))
    const2 = lambda s: pl.BlockSpec(s, lambda i: (0, 0), pipeline_mode=pl.Buffered(1))
    const3 = lambda s: pl.BlockSpec(s, lambda i: (0, 0, 0), pipeline_mode=pl.Buffered(1))
    return pl.pallas_call(
        body, name="tail_fwd_bwd", grid=(nt,),
        in_specs=[row(D), row(PLE), row(D), row(DG), row(DG),
                  const2((D, D)), const2((D, D)), const3((NSHARD, PLE, PLE)),
                  const2((1, D)), const2((1, D)), const2((1, D))],
        out_specs=[row(D), row(D), const3((NSHARD, QR, D)), const3((NSHARD, QR, D)),
                   const3((NSHARD, PLE, PLE)), const2((8, D))],
        out_shape=[jax.ShapeDtypeStruct((T, D), F32), jax.ShapeDtypeStruct((T, D), F32),
                   jax.ShapeDtypeStruct((NSHARD, QR, D), BF16), jax.ShapeDtypeStruct((NSHARD, QR, D), BF16),
                   jax.ShapeDtypeStruct((NSHARD, PLE, PLE), BF16), jax.ShapeDtypeStruct((8, D), F32)],
        scratch_shapes=[pltpu.VMEM((D, D), F32), pltpu.VMEM((D, D), F32), pltpu.VMEM((NSHARD, PLE, PLE), F32)],
        compiler_params=_cparams(("arbitrary",), 62),
    )(x, p, tgt, pltpu.with_memory_space_constraint(ya, pltpu.HBM), pltpu.with_memory_space_constraint(yb, pltpu.HBM),
      w_out_b, w_pg_b, w_pp_b, ple_nw, b_pg, fnw)


def _bwd_rglru(pf, pb, h, dyab, cw3, conv_b, wa_c, ba, wx_c, bx, lam, dep):
    tm = 512
    nt = T // tm
    ng = tm // 8

    def body(xa_ref, ga_ref, xp_ref, h_ref, hp_ref, dya_ref, cw_ref, cb_ref, wa_ref, ba_ref, wx_ref, bx_ref,
             lam_ref, dep_ref, da_ref, rg_ref, sm_ref, a_s, g_s, cg_s, nxt_s, wa_s, wx_s, dwa_s, dwx_s):
        i = pl.program_id(0)
        tile = nt - 1 - i

        @pl.when(i == 0)
        def _():
            dwa_s[...] = jnp.zeros_like(dwa_s)
            dwx_s[...] = jnp.zeros_like(dwx_s)
            sm_ref[...] = jnp.zeros_like(sm_ref)
            cg_s[...] = jnp.zeros_like(cg_s)
            nxt_s[...] = jnp.zeros_like(nxt_s)
            wa_s[...] = _dense_from_blocks(wa_ref[...]).astype(BF16)
            wx_s[...] = _dense_from_blocks(wx_ref[...]).astype(BF16)

        rows = lax.broadcasted_iota(I32, (tm, DG), 0)
        has_prev = tile > 0
        xa = xa_ref[...]
        xprev = jnp.where(has_prev, xp_ref[...], 0.0)
        cw = _conv_rows(cw_ref)
        xc, taps = _rg_conv(xa, xprev, cw, cb_ref[...], rows)
        lam_v = lam_ref[...]
        sp = _softplus(-lam_v)
        first_row = (rows + tile * tm) == 0
        r, ig, a, a2, mult = _rg_gates(xc, wa_s[...], ba_ref[...], wx_s[...], bx_ref[...], sp, first_row)
        hv = h_ref[...]
        hprev = jnp.where(has_prev, hp_ref[...], 0.0)
        h_m1 = _shift_down(hv, hprev, 1, rows)
        ga = ga_ref[...].astype(F32)
        sg = _sigmoid(ga)
        dya = dya_ref[...]
        dga = dya * hv * (sg * (1.0 + ga * (1.0 - sg)))

        av = jnp.where(rows == tm - 1, 1.0, pltpu.roll(a, tm - 1, 0))
        gv = dya * (ga * sg)
        rows8 = rows & 7
        for d in (1, 2, 4):
            keep = rows8 < 8 - d
            gv = gv + av * jnp.where(keep, _roll_in_groups(gv, 8 - d), 0.0)
            av = av * jnp.where(keep, _roll_in_groups(av, 8 - d), 1.0)
        a_s[...] = av
        g_s[...] = gv
        carry = cg_s[0:1, :]
        for g in range(ng - 1, -1, -1):
            sl = slice(g * 8, (g + 1) * 8)
            ab, gb = a_s[sl, :], g_s[sl, :]
            g_s[sl, :] = gb + ab * carry
            carry = gb[0:1, :] + ab[0:1, :] * carry
        cg_s[0:1, :] = a[0:1, :] * carry

        gt = g_s[...]
        da = gt * h_m1
        ixc = ig * xc
        di = gt * mult * xc
        dxc = gt * mult * ig
        dlog_a = da * a + jnp.where(first_row, 0.0, gt * ixc * (-a2 / mult))
        sm_ref[3:4, :] += _rowsum(dlog_a * ((-RG_C) * r))
        dpr_f = dlog_a * ((-RG_C) * sp) * r * (1.0 - r)
        dpi_f = di * ig * (1.0 - ig)
        sm_ref[1:2, :] += _rowsum(dpr_f)
        sm_ref[2:3, :] += _rowsum(dpi_f)
        dpr = dpr_f.astype(BF16)
        dpi = dpi_f.astype(BF16)
        xcb = xc.astype(BF16)
        dwa_s[...] += _mm_tn(xcb, dpr)
        dwx_s[...] += _mm_tn(xcb, dpi)
        dxc = dxc + _mm_nt(dpr, wa_s[...]) + _mm_nt(dpi, wx_s[...])
        sm_ref[0:1, :] += _rowsum(dxc)
        for j in range(4):
            sm_ref[4 + j:5 + j, :] += _rowsum(dxc * taps[j])
        nxt = nxt_s[...]
        dxa = (dxc * cw[3] + _shift_up(dxc, nxt, 1, rows) * cw[2]
               + _shift_up(dxc, nxt, 2, rows) * cw[1] + _shift_up(dxc, nxt, 3, rows) * cw[0])
        nxt_s[...] = dxc[0:8, :]
        da_ref[:, 0:DG] = dxa.astype(BF16)
        da_ref[:, DG:D] = dga.astype(BF16)

        @pl.when(i == nt - 1)
        def _():
            _, s_neg = _sig_pair(lam_v)
            sm_ref[3:4, :] = sm_ref[3:4, :] * (-s_neg)
            rg_ref[...] = _blocks_from_dense(dwa_s[...], dwx_s[...])

    vec = lambda: pl.BlockSpec((1, DG), lambda i: (0, 0))
    blocks = lambda: pl.BlockSpec((DG, RGB), lambda i: (0, 0))
    prev8 = lambda: pl.BlockSpec((8, DG), lambda i: (jnp.maximum((nt - 1 - i) * (tm // 8) - 1, 0), 0))
    return pl.pallas_call(
        body, name="bwd_rglru", grid=(nt,),
        in_specs=[pl.BlockSpec((tm, DG), lambda i: (nt - 1 - i, 0)),
                  pl.BlockSpec((tm, DG), lambda i: (nt - 1 - i, 0)),
                  prev8(),
                  pl.BlockSpec((tm, DG), lambda i: (nt - 1 - i, 0)),
                  prev8(),
                  pl.BlockSpec((tm, DG), lambda i: (nt - 1 - i, 0)),
                  pl.BlockSpec((NSHARD, 4, 128), lambda i: (0, 0, 0)), vec(),
                  blocks(), vec(), blocks(), vec(), vec(), ANY_SPEC],
        out_specs=[pl.BlockSpec((tm, D), lambda i: (nt - 1 - i, 0)),
                   pl.BlockSpec((DG, 128), lambda i: (0, 0)),
                   pl.BlockSpec((8, DG), lambda i: (0, 0))],
        out_shape=[jax.ShapeDtypeStruct((T, D), BF16), jax.ShapeDtypeStruct((DG, 128), F32),
                   jax.ShapeDtypeStruct((8, DG), F32)],
        scratch_shapes=[pltpu.VMEM((tm, DG), F32), pltpu.VMEM((tm, DG), F32),
                        pltpu.VMEM((8, DG), F32), pltpu.VMEM((8, DG), F32),
                        pltpu.VMEM((DG, DG), BF16), pltpu.VMEM((DG, DG), BF16),
                        pltpu.VMEM((DG, DG), F32), pltpu.VMEM((DG, DG), F32)],
        compiler_params=_cparams(("arbitrary",), 56),
    )(pf, pb, pf, h, h, dyab, cw3, conv_b, wa_c, ba, wx_c, bx, lam, dep)


def _bwd_hgrn2(pf, pb, o, s_prev, dyab, hg_lb, hg_nw, dep):
    tm = 512
    nt = T // tm
    nc = tm // CH

    def body(q_ref, f_ref, v_ref, g_ref, o_ref, sp_ref, dy_ref, lb_ref, nw_ref, dep_ref, db_ref, sm_ref,
             dst_s, eb_s, enb_s, ekd_s, qe_s, ke_s, kd_s, vb_s, do_s, dec_s, ddec_s, p_s, dp_s,
             g_s, dsta_s, dva_s, dqe_s, dke_s, dkd_s, dlf_s):
        i = pl.program_id(0)

        @pl.when(i == 0)
        def _():
            sm_ref[...] = jnp.zeros_like(sm_ref)
            dst_s[...] = jnp.zeros_like(dst_s)

        lb, one_m_lb = _hg_lower_bound(lb_ref)
        rows64 = lax.broadcasted_iota(I32, (tm, DG), 0) & (CH - 1)
        sg, sn, f, k, q, sq, qs = _hg_recompute(
            q_ref, f_ref, lb, one_m_lb, rows64, eb_s, enb_s, ekd_s, qe_s, ke_s, kd_s, dec_s)
        vb_s[...] = v_ref[...]

        nw = nw_ref[...]
        for hd in range(NH):
            cols = slice(hd * HD, (hd + 1) * HD)
            g = g_ref[:, cols].astype(F32)
            sgg = _sigmoid(g)
            o = o_ref[:, cols]
            so = lax.rsqrt(jnp.mean(o * o, axis=-1, keepdims=True) + EPS)
            oh = o * so
            dyb = dy_ref[:, cols]
            db_ref[:, 3 * DG + hd * HD:3 * DG + (hd + 1) * HD] = (
                dyb * (oh * nw) * (sgg * (1.0 + g * (1.0 - sgg)))).astype(BF16)
            don = dyb * (g * sgg)
            sm_ref[1:2, 0:HD] += _rowsum(don * oh)
            gw = don * nw
            do_s[:, cols] = (so * (gw - oh * jnp.mean(gw * oh, axis=-1, keepdims=True))).astype(BF16)

        mask = _tri(True)
        items = [(c, hd, slice(c * CH, (c + 1) * CH), slice(hd * HD, (hd + 1) * HD))
                 for c in range(nc) for hd in range(NH)]
        for c, hd, rs, cols in items:
            p_s[c * NH + hd] = jnp.where(mask, _mm_nt(qe_s[rs, cols], ke_s[rs, cols]), 0.0).astype(BF16)
            dp_s[c * NH + hd] = jnp.where(mask, _mm_nt(do_s[rs, cols], vb_s[rs, cols]), 0.0).astype(BF16)
        for c, hd, rs, cols in items:
            n = c * NH + hd
            dva_s[rs, cols] = _mm_tn(p_s[n], do_s[rs, cols])
            dqe_s[rs, cols] = _mm(dp_s[n], ke_s[rs, cols])
            dke_s[rs, cols] = _mm_tn(dp_s[n], qe_s[rs, cols])
            g_s[n] = _mm_tn(do_s[rs, cols], qe_s[rs, cols])
        for c, hd, rs, cols in reversed(items):
            n = c * NH + hd
            dst = dst_s[hd]
            dsta_s[n] = dst
            dst_s[hd] = dst * dec_s[c:c + 1, cols] + g_s[n]
        for c, hd, rs, cols in items:
            n = c * NH + hd
            dst = dsta_s[n]
            st_prev = sp_ref[hd, c]
            dv = dva_s[rs, cols] + _mm_nt(kd_s[rs, cols], dst)
            db_ref[rs, 2 * DG + hd * HD:2 * DG + (hd + 1) * HD] = dv.astype(BF16)
            dqe_s[rs, cols] += _mm(do_s[rs, cols], st_prev)
            dkd_s[rs, cols] = _mm(vb_s[rs, cols], dst)
            ddec_s[c:c + 1, cols] = _rowsum(dst * st_prev)

        eb, enb, ekd = eb_s[...], enb_s[...], ekd_s[...]
        dqe, dke, dkd = dqe_s[...], dke_s[...], dkd_s[...]
        t_kd = dkd * (k * ekd)
        rc = _chunk_rev_cumsum(dqe * (qs * eb) - dke * (k * enb) - t_kd, rows64)
        for c in range(nc):
            rs = slice(c * CH, (c + 1) * CH)
            dbl = _rowsum(t_kd[rs]) + ddec_s[c:c + 1, :] * dec_s[c:c + 1, :]
            dlf_s[rs, :] = rc[rs] + dbl
        t = dlf_s[...] / f - (dke * enb + dkd * ekd)
        db_ref[:, DG:2 * DG] = (one_m_lb * sg * sn * t).astype(BF16)
        sm_ref[0:1, :] += _rowsum(sn * t)
        db_ref[:, 0:DG] = (dqe * eb * (sq * (1.0 + q * (1.0 - sq))) * (HD ** -0.5)).astype(BF16)

        @pl.when(i == nt - 1)
        def _():
            dsm = sm_ref[0:1, :] * (lb * one_m_lb)
            sm_ref[2:3, :] = dsm
            sm_ref[3:4, :] = -dsm

    col = lambda j: pl.BlockSpec((tm, DG), lambda i: (nt - 1 - i, j))
    big = lambda dt: pltpu.VMEM((tm, DG), dt)
    return pl.pallas_call(
        body, name="bwd_hgrn2", grid=(nt,),
        in_specs=[col(1), col(1), col(2), col(3),
                  pl.BlockSpec((tm, DG), lambda i: (nt - 1 - i, 0)),
                  pl.BlockSpec((NH, nc, HD, HD), lambda i: (0, nt - 1 - i, 0, 0)),
                  pl.BlockSpec((tm, DG), lambda i: (nt - 1 - i, 1)),
                  pl.BlockSpec((2, DG), lambda i: (0, 0)),
                  pl.BlockSpec((1, HD), lambda i: (0, 0)), ANY_SPEC],
        out_specs=[pl.BlockSpec((tm, 4 * DG), lambda i: (nt - 1 - i, 0)),
                   pl.BlockSpec((8, DG), lambda i: (0, 0))],
        out_shape=[jax.ShapeDtypeStruct((T, 4 * DG), BF16), jax.ShapeDtypeStruct((8, DG), F32)],
        scratch_shapes=[pltpu.VMEM((NH, HD, HD), F32),
                        big(F32), big(F32), big(F32),
                        big(BF16), big(BF16), big(BF16), big(BF16), big(BF16),
                        pltpu.VMEM((nc, DG), F32), pltpu.VMEM((nc, DG), F32),
                        pltpu.VMEM((nc * NH, CH, CH), BF16), pltpu.VMEM((nc * NH, CH, CH), BF16),
                        pltpu.VMEM((nc * NH, HD, HD), F32), pltpu.VMEM((nc * NH, HD, HD), F32),
                        big(F32), big(F32), big(F32), big(F32), big(F32)],
        compiler_params=_cparams(("arbitrary",), 56),
    )(pb, pf, pb, pb, o, s_prev, dyab, hg_lb, hg_nw, dep)


def _dproj_pieces(k, da_ref, db_ref):
    if k == 0:
        return [(da_ref[:, 0:SHW], 0)]
    if k == 1:
        return [(da_ref[:, SHW:D], 0), (db_ref[:, 0:DG], D - SHW)]
    if k == 2:
        return [(db_ref[:, DG:DG + SHW], 0)]
    return [(db_ref[:, DG + SHW:4 * DG], 0)]


def _bwd_inproj_dx(x, dh1, d_a, d_b, w_in_b, nw, dep):
    tm = 512
    nt = T // tm

    def body(x_ref, dh1_ref, da_ref, db_ref, w_ref, nw_ref, dep_ref, dx_ref, sm_ref):
        i = pl.program_id(0)

        @pl.when(i == 0)
        def _():
            sm_ref[...] = jnp.zeros_like(sm_ref)

        du = None
        for k in range(NSHARD):
            for val, off in _dproj_pieces(k, da_ref, db_ref):
                t = _mm_nt(val, w_ref[k, :, off:off + val.shape[1]])
                du = t if du is None else du + t
        xv = x_ref[...]
        s = lax.rsqrt(jnp.mean(xv * xv, axis=-1, keepdims=True) + EPS)
        xh = xv * s
        sm_ref[0:1, :] += _rowsum(du * xh)
        g = du * nw_ref[...]
        dx_ref[...] = dh1_ref[...] + s * (g - xh * jnp.mean(g * xh, axis=-1, keepdims=True))

    row = lambda w: pl.BlockSpec((tm, w), lambda i: (i, 0))
    return pl.pallas_call(
        body, name="bwd_inproj_dx", grid=(nt,),
        in_specs=[row(D), row(D), row(D), row(4 * DG),
                  pl.BlockSpec((NSHARD, D, SHW), lambda i: (0, 0, 0), pipeline_mode=pl.Buffered(1)),
                  pl.BlockSpec((1, D), lambda i: (0, 0)), ANY_SPEC],
        out_specs=[row(D), pl.BlockSpec((8, D), lambda i: (0, 0))],
        out_shape=[jax.ShapeDtypeStruct((T, D), F32), jax.ShapeDtypeStruct((8, D), F32)],
        compiler_params=_cparams(("arbitrary",), 56),
    )(x, dh1, d_a, d_b, w_in_b, nw, dep)


def _bwd_inproj_dw(name, ks, u_b, d_a, d_b):
    tm = 1024
    nt = T // tm
    nk = len(ks)
    H = D // 2
    step = 128
    need_a = 0 in ks or 1 in ks

    def body(*refs):
        if need_a:
            u_ref, da_ref, db_ref, dw_ref, acc, send_s, recv_s, ssem, rsem = refs
        else:
            u_ref, db_ref, dw_ref, acc, send_s, recv_s, ssem, rsem = refs
            da_ref = None
        i = pl.program_id(0)

        @pl.when(i == 0)
        def _():
            acc[...] = jnp.zeros_like(acc)

        u = u_ref[...]
        for j, k in enumerate(ks):
            for val, off in _dproj_pieces(k, da_ref, db_ref):
                acc[j, :, off:off + val.shape[1]] += _mm_tn(u, val)

        @pl.when(i == nt - 1)
        def _():
            x, y, c = lax.axis_index("x"), lax.axis_index("y"), lax.axis_index("c")
            sibling = (x, y, 1 - c)
            mine0 = pl.multiple_of(c * H, step)
            other0 = pl.multiple_of((1 - c) * H, step)
            copies = []
            for j in range(nk):
                for r0 in range(0, H, step):
                    send_s[j, r0:r0 + step, :] = acc[j, pl.ds(other0 + r0, step), :].astype(BF16)
                cp = _remote(send_s.at[j], recv_s.at[j], ssem.at[j], rsem.at[j], sibling)
                cp.start()
                copies.append(cp)
            for j in range(nk):
                copies[j].wait_recv()
                for r0 in range(0, H, step):
                    s = acc[j, pl.ds(mine0 + r0, step), :] + recv_s[j, r0:r0 + step, :].astype(F32)
                    dw_ref[j, r0:r0 + step, :] = s.astype(BF16)
            for cp in copies:
                cp.wait_send()

    row = lambda w: pl.BlockSpec((tm, w), lambda i: (i, 0))
    ins = [u_b] + ([d_a] if need_a else []) + [d_b]
    in_specs = [row(D)] + ([row(D)] if need_a else []) + [row(4 * DG)]
    return pl.pallas_call(
        body, name=name, grid=(nt,), in_specs=in_specs,
        out_specs=pl.BlockSpec((nk, H, SHW), lambda i: (0, 0, 0)),
        out_shape=jax.ShapeDtypeStruct((nk, H, SHW), BF16),
        scratch_shapes=[pltpu.VMEM((nk, D, SHW), F32), pltpu.VMEM((nk, H, SHW), BF16),
                        pltpu.VMEM((nk, H, SHW), BF16),
                        pltpu.SemaphoreType.DMA((nk,)), pltpu.SemaphoreType.DMA((nk,))],
        compiler_params=_cparams(("arbitrary",), 48),
    )(*ins)


_OUT_ORDER = ["norm_mix_w", "w_in", "conv_w", "conv_b", "rg_wa", "rg_ba", "rg_wx", "rg_bx", "rg_lambda", "hg_lb",
              "hg_norm_w", "w_out", "ple_norm_w", "w_ple_gate", "b_ple_gate", "w_ple_proj", "final_norm_w"]
_BIG = ["w_in", "w_out", "w_ple_gate", "w_ple_proj"]


def _small_view(name, a):
    if name in ("rg_wa", "rg_wx"):
        return a.reshape(DG, RGB)
    if name == "conv_w":
        return a.reshape(4, 128)
    if name == "final_norm_w":
        return a.reshape(1, D)
    return a


def _landing(rows, cols):
    return lax.empty((NDEV, rows, cols), BF16)


def kernel(x, p, norm_mix_w, w_in, conv_w, conv_b, rg_wa, rg_ba, rg_wx, rg_bx, rg_lambda, hg_lb, hg_norm_w, w_out, ple_norm_w, w_ple_gate, b_ple_gate, w_ple_proj, final_norm_w, loss_target, m_norm_mix_w, m_w_in, m_conv_w, m_conv_b, m_rg_wa, m_rg_ba, m_rg_wx, m_rg_bx, m_rg_lambda, m_hg_lb, m_hg_norm_w, m_w_out, m_ple_norm_w, m_w_ple_gate, m_b_ple_gate, m_w_ple_proj, m_final_norm_w, v_norm_mix_w, v_w_in, v_conv_w, v_conv_b, v_rg_wa, v_rg_ba, v_rg_wx, v_rg_bx, v_rg_lambda, v_hg_lb, v_hg_norm_w, v_w_out, v_ple_norm_w, v_w_ple_gate, v_b_ple_gate, v_w_ple_proj, v_final_norm_w):
    given = dict(locals())
    x2, p2, tgt = x[0], p[0, 0], loss_target[0]
    wa_c, wx_c = _small_view("rg_wa", rg_wa), _small_view("rg_wx", rg_wx)

    w_in_b, l_out, l_pg, l_pp, cw3 = _gather_w_in(w_in[0], w_out[0], w_ple_gate[0], w_ple_proj[0], conv_w[0])
    g_ssem, g_rsem, g_lands, tok = _gather_rest_start([l_out, l_pg, l_pp])

    pf, pb, u_b = _fwd_inproj(x2, norm_mix_w, w_in_b, tok)
    h, ya = _fwd_rglru(pf, pb, cw3, conv_b, wa_c, rg_ba, wx_c, rg_bx, rg_lambda)
    yb, o, s_prev = _fwd_hgrn2(pf, pb, hg_lb, hg_norm_w)
    w_out_b, w_pg_b, w_pp_b = _gather_rest_wait(g_ssem, g_rsem, g_lands, yb)
    dh1, dyab, dwo_b, dwg_b, dwp_b, sm_tail = _tail_fwd_bwd(
        x2, p2, tgt, ya, yb, w_out_b.reshape(D, D), w_pg_b.reshape(D, D), w_pp_b,
        ple_norm_w, b_ple_gate, final_norm_w.reshape(1, D))

    QH = D // NSHARD // 2
    r1 = _rs_start("rs_start_tail", [dwo_b, dwg_b, dwp_b], (0, 1, 2, 3),
                   [_landing(QH, D), _landing(QH, D), _landing(PLE // 2, PLE)])
    d_b, sm_b = _bwd_hgrn2(pf, pb, o, s_prev, dyab, hg_lb, hg_norm_w, r1[4])
    dw23 = _bwd_inproj_dw("bwd_inproj_dw23", (2, 3), u_b, None, d_b)
    r2 = _rs_start("rs_start_in23", [dw23], (2, 3), [lax.empty((NSHARD, D // 2, SHW), BF16)], chip_sums=True)
    d_a, rg_c, sm_a = _bwd_rglru(pf, pb, h, dyab, cw3, conv_b, wa_c, rg_ba, wx_c, rg_bx, rg_lambda, r2[4])
    dw01 = _bwd_inproj_dw("bwd_inproj_dw01", (0, 1), u_b, d_a, d_b)
    r3 = _rs_start("rs_start_in01", [dw01], (0, 1), r2[3], chip_sums=True)
    grad_x, sm_in = _bwd_inproj_dx(x2, dh1, d_a, d_b, w_in_b, norm_mix_w, r3[4])

    parts1, lands1 = _rs_wait("rs_wait_tail", r1[0], r1[1], r1[2], (0, 1, 2, 3), r1[3], sm_in)
    parts2, lands2 = _rs_wait("rs_wait_in23", r2[0], r2[1], r2[2], (2, 3), r3[3], sm_in, chip_sums=True)
    parts3, lands3 = _rs_wait("rs_wait_in01", r3[0], r3[1], r3[2], (0, 1), lands2, sm_in, chip_sums=True)
    g_big, vred, rgred = _final_reduce([parts3[0], parts2[0]] + parts1, lands3 + lands1,
                                       sm_in, sm_tail, sm_a, sm_b, rg_c)

    upd_big = _adam_big(g_big, [given[n][0] for n in _BIG], [given["m_" + n][0] for n in _BIG],
                        [given["v_" + n][0] for n in _BIG])
    small = _adam_small(vred, rgred,
                        {n: _small_view(n, given[n]) for n in _SMALL_ORDER},
                        {n: _small_view(n, given["m_" + n]) for n in _SMALL_ORDER},
                        {n: _small_view(n, given["v_" + n]) for n in _SMALL_ORDER})

    loss = vred[0, ROW_LOSS, 0]
    outs = [loss, grad_x[None]]
    for ki in range(4):
        for n in _OUT_ORDER:
            if n in _BIG:
                i = _BIG.index(n)
                a = g_big[i] if ki == 0 else upd_big[i][ki - 1]
                outs.append(a[None])
            else:
                outs.append(small[n][ki].reshape(given[n].shape))
    return tuple(outs)
```

```python
import jax
import jax.numpy as jnp
from jax import lax
from jax.experimental import pallas as pl
from jax.experimental.pallas import tpu as pltpu

F32 = jnp.float32
BF16 = jnp.bfloat16
I32 = jnp.int32
MESH = pl.DeviceIdType.MESH
HIGHEST = lax.Precision.HIGHEST

T = 4096
D = 1024
DG = 512
DIN = 3072
PLE = 256
NH = 4
HD = 128
CH = 64
NCHUNK = T // CH
RGB = 64
EPS = 1e-6
RG_C = 8.0
NSHARD = 4
SHW = DIN // NSHARD
NDEV = 8

ADAM_LR = 0.001
ADAM_B1 = 0.9
ADAM_B2 = 0.999
ADAM_EPS = 1e-08
ADAM_WD = 0.01
ADAM_STEP = 10

VMEM_SPEC = pl.BlockSpec(memory_space=pltpu.VMEM)
HBM_SPEC = pl.BlockSpec(memory_space=pltpu.HBM)
SEM_SPEC = pl.BlockSpec(memory_space=pltpu.SEMAPHORE)
ANY_SPEC = pl.BlockSpec(memory_space=pl.ANY)
EFFECT = pltpu.SideEffectType.DATAFLOW_SIDE_EFFECTING
MIB = 1024 * 1024

VROWS = 16
ROW_NORM_MIX, ROW_FINAL_NORM, ROW_B_PG, ROW_PLE_NORM = 0, 1, 2, 3
ROW_CB_BA, ROW_BX_LAM, ROW_CW01, ROW_CW23, ROW_HG_LB, ROW_HG_NW, ROW_LOSS = 4, 5, 6, 7, 8, 9, 10


def _mm(a, b):
    return jnp.dot(a.astype(BF16), b.astype(BF16), preferred_element_type=F32)


def _mm_nt(a, b):
    return lax.dot_general(a.astype(BF16), b.astype(BF16), (((1,), (1,)), ((), ())),
                           preferred_element_type=F32)


def _mm_tn(a, b):
    return lax.dot_general(a.astype(BF16), b.astype(BF16), (((0,), (0,)), ((), ())),
                           preferred_element_type=F32)


def _mm_exact(a, b):
    return jnp.dot(a, b, precision=HIGHEST, preferred_element_type=F32)


def _sig_pair(x):
    e = jnp.exp(-jnp.abs(x))
    big = 1.0 / (1.0 + e)
    small = e * big
    pos = x >= 0
    return jnp.where(pos, big, small), jnp.where(pos, small, big)


def _sigmoid(x):
    return 1.0 / (1.0 + jnp.exp(-x))


def _rowsum(v):
    return jnp.sum(v, axis=0, keepdims=True)


def _shift_down(cur, prev8, d, rows):
    rolled = pltpu.roll(cur, d, 0)
    head = jnp.where(rows[0:8] < d, pltpu.roll(prev8, d, 0), rolled[0:8])
    return jnp.concatenate([head, rolled[8:]], axis=0)


def _shift_up(cur, next8, d, rows):
    n = cur.shape[0]
    rolled = pltpu.roll(cur, n - d, 0)
    tail = jnp.where(rows[0:8] >= 8 - d, pltpu.roll(next8, 8 - d, 0), rolled[n - 8:n])
    return jnp.concatenate([rolled[0:n - 8], tail], axis=0)


def _roll_in_groups(v, d):
    n, w = v.shape
    return pltpu.roll(v.reshape(n // 8, 8, w), d, 1).reshape(n, w)


def _cparams(sem, vmem_mib):
    return pltpu.CompilerParams(dimension_semantics=sem, vmem_limit_bytes=vmem_mib * MIB)


def _mesh_pos():
    x, y, c = lax.axis_index("x"), lax.axis_index("y"), lax.axis_index("c")
    chips = [(1 - x, y), (x, 1 - y), (1 - x, 1 - y)]
    return x, y, c, chips


def _remote(src, dst, ssem, rsem, dev):
    return pltpu.make_async_remote_copy(src_ref=src, dst_ref=dst, send_sem=ssem, recv_sem=rsem,
                                        device_id=dev, device_id_type=MESH)


def _gather_w_in(w_in, w_out, w_pg, w_pp, conv_w):
    shapes = [w_in.shape, w_out.shape, w_pg.shape, w_pp.shape]

    def body(win_h, wout_h, wpg_h, wpp_h, cw_h, o_in, o_out, o_pg, o_pp, o_cw,
             win, wout, wpg, wpp, cw, lsem, ssem, rsem):
        x, y, c, chips = _mesh_pos()
        kme = 2 * x + y
        sibling = (x, y, 1 - c)
        fetch = [pltpu.make_async_copy(src, dst, lsem.at[i])
                 for i, (src, dst) in enumerate([(win_h, win), (cw_h, cw), (wout_h, wout), (wpg_h, wpg), (wpp_h, wpp)])]
        for cp in fetch:
            cp.start()

        def cast(src, dst):
            for r0 in range(0, src.shape[0], 128):
                dst[kme, r0:r0 + 128, :] = src[r0:r0 + 128, :].astype(BF16)

        fetch[0].wait()
        cast(win, o_in)
        fetch[1].wait()
        o_cw[kme] = cw[...]

        hrows = D // 2
        mine = pl.ds(pl.multiple_of(c * hrows, 128), hrows)
        other = pl.ds(pl.multiple_of((1 - c) * hrows, 128), hrows)

        def half(k, rows):
            return o_in.at[k, rows]

        na = (x ^ c, y ^ (1 - c), c)
        nb = (x ^ (1 - c), y ^ c, c)
        ka = 2 * na[0] + na[1]
        kb = 2 * nb[0] + nb[1]
        kd = 2 * (1 - x) + (1 - y)
        sends = [_remote(half(kme, mine), half(kme, mine), ssem.at[0], rsem.at[0], na),
                 _remote(half(kme, mine), half(kme, mine), ssem.at[1], rsem.at[1], nb)]
        for j, (px, py) in enumerate(chips):
            sends.append(_remote(o_cw.at[kme], o_cw.at[kme], ssem.at[3 + j], rsem.at[3 + j], (px, py, c)))
        for cp in sends:
            cp.start()
        for i, (src, dst) in enumerate([(wout, o_out), (wpg, o_pg), (wpp, o_pp)]):
            fetch[2 + i].wait()
            cast(src, dst)

        def passed_on(k, sem, dev):
            cp = _remote(half(k, mine), half(k, mine), ssem.at[sem], rsem.at[sem], dev)
            cp.start()
            sends.append(cp)

        _remote(half(ka, mine), half(ka, mine), ssem.at[0], rsem.at[0], na).wait_recv()
        passed_on(ka, 2, nb)
        passed_on(ka, 6, sibling)
        _remote(half(kb, mine), half(kb, mine), ssem.at[1], rsem.at[1], nb).wait_recv()
        passed_on(kb, 7, sibling)
        _remote(half(kd, mine), half(kd, mine), ssem.at[2], rsem.at[2], nb).wait_recv()
        passed_on(kd, 8, sibling)
        _remote(half(kb, other), half(kb, other), ssem.at[6], rsem.at[6], sibling).wait_recv()
        _remote(half(ka, other), half(ka, other), ssem.at[7], rsem.at[7], sibling).wait_recv()
        _remote(half(kd, other), half(kd, other), ssem.at[8], rsem.at[8], sibling).wait_recv()
        for j, (px, py) in enumerate(chips):
            kj = 2 * px + py
            _remote(o_cw.at[kj], o_cw.at[kj], ssem.at[3 + j], rsem.at[3 + j], (px, py, c)).wait_recv()
        for cp in sends:
            cp.wait_send()

    out_shape = [jax.ShapeDtypeStruct((NSHARD,) + s, BF16) for s in shapes]
    out_shape.append(jax.ShapeDtypeStruct((NSHARD,) + conv_w.shape, F32))
    return pl.pallas_call(
        body, name="gather_w_in", out_shape=out_shape,
        in_specs=[ANY_SPEC] * 5, out_specs=[VMEM_SPEC] * 5,
        scratch_shapes=[pltpu.VMEM(a.shape, F32) for a in (w_in, w_out, w_pg, w_pp, conv_w)]
        + [pltpu.SemaphoreType.DMA((5,)), pltpu.SemaphoreType.DMA((9,)), pltpu.SemaphoreType.DMA((9,))],
        compiler_params=pltpu.CompilerParams(vmem_limit_bytes=40 * MIB),
    )(w_in, w_out, w_pg, w_pp, conv_w)


def _gather_rest_start(lands):
    n = len(lands)

    def body(*refs):
        land_in = refs[0:n]
        ssem, rsem = refs[n], refs[n + 1]
        token = refs[2 * n + 2]
        x, y, c, chips = _mesh_pos()
        kme = 2 * x + y
        for p, land in enumerate(land_in):
            hrows = land.shape[1] // 2
            mine = pl.ds(pl.multiple_of(c * hrows, 128), hrows)
            for px, py in chips:
                for pc in range(2):
                    _remote(land.at[kme, mine], land.at[kme, mine], ssem.at[p], rsem.at[p], (px, py, pc)).start()
        token[...] = jnp.zeros_like(token)

    out_shape = ([pltpu.SemaphoreType.DMA((n,)), pltpu.SemaphoreType.DMA((n,))]
                 + [pltpu.HBM(a.shape, a.dtype) for a in lands] + [jax.ShapeDtypeStruct((8, 128), F32)])
    outs = pl.pallas_call(
        body, name="gather_rest_start", out_shape=out_shape,
        in_specs=[HBM_SPEC] * n, out_specs=[SEM_SPEC, SEM_SPEC] + [HBM_SPEC] * n + [VMEM_SPEC],
        input_output_aliases={i: 2 + i for i in range(n)},
        compiler_params=pltpu.CompilerParams(has_side_effects=EFFECT),
    )(*[pltpu.with_memory_space_constraint(a, pltpu.HBM) for a in lands])
    return outs[0], outs[1], list(outs[2:2 + n]), outs[2 + n]


def _gather_rest_wait(ssem, rsem, lands, after):
    n = len(lands)

    def body(*refs):
        land_in = refs[0:n]
        ssem_ref, rsem_ref = refs[n], refs[n + 1]
        x, y, c = lax.axis_index("x"), lax.axis_index("y"), lax.axis_index("c")
        for p, land in enumerate(land_in):
            three = land.at[pl.ds(0, 3)]
            cp = _remote(three, three, ssem_ref.at[p], rsem_ref.at[p], (x, y, c))
            cp.wait_send()
            cp.wait_recv()

    outs = pl.pallas_call(
        body, name="gather_rest_wait", out_shape=[pltpu.HBM(a.shape, a.dtype) for a in lands],
        in_specs=[HBM_SPEC] * n + [SEM_SPEC, SEM_SPEC, ANY_SPEC], out_specs=[HBM_SPEC] * n,
        input_output_aliases={i: i for i in range(n)},
        compiler_params=pltpu.CompilerParams(has_side_effects=EFFECT),
    )(*lands, ssem, rsem, after)
    return list(outs)


def _rs_start(name, parts, ks, lands, chip_sums=False):
    n = len(parts)

    def body(*refs):
        part_in, land_in = refs[0:n], refs[n:2 * n]
        ssem, rsem = refs[2 * n], refs[2 * n + 1]
        token = refs[4 * n + 2]
        x, y, c = lax.axis_index("x"), lax.axis_index("y"), lax.axis_index("c")
        kme = 2 * x + y
        me = 4 * x + 2 * y + c
        for p in range(n):
            hrows = land_in[p].shape[1]
            for i, k in enumerate(ks):
                if chip_sums:
                    @pl.when(kme != k)
                    def _():
                        _remote(part_in[p].at[i], land_in[p].at[kme], ssem.at[p], rsem.at[p], (k // 2, k % 2, c)).start()
                    continue
                for pc in range(2):
                    @pl.when(jnp.logical_or(kme != k, c != pc))
                    def _():
                        _remote(part_in[p].at[i, pl.ds(pc * hrows, hrows)], land_in[p].at[me],
                                ssem.at[p], rsem.at[p], (k // 2, k % 2, pc)).start()
        token[...] = jnp.zeros_like(token)

    arrays = list(parts) + list(lands)
    out_shape = ([pltpu.SemaphoreType.DMA((n,)), pltpu.SemaphoreType.DMA((n,))]
                 + [pltpu.HBM(a.shape, a.dtype) for a in arrays] + [jax.ShapeDtypeStruct((8, 128), F32)])
    outs = pl.pallas_call(
        body, name=name, out_shape=out_shape,
        in_specs=[HBM_SPEC] * (2 * n), out_specs=[SEM_SPEC, SEM_SPEC] + [HBM_SPEC] * (2 * n) + [VMEM_SPEC],
        input_output_aliases={i: 2 + i for i in range(2 * n)},
        compiler_params=pltpu.CompilerParams(has_side_effects=EFFECT),
    )(*[pltpu.with_memory_space_constraint(a, pltpu.HBM) for a in arrays])
    return outs[0], outs[1], list(outs[2:2 + n]), list(outs[2 + n:2 + 2 * n]), outs[2 + 2 * n]


def _rs_wait(name, ssem, rsem, parts, ks, lands, after, chip_sums=False):
    n = len(parts)

    def body(*refs):
        part_in, land_in = refs[0:n], refs[n:2 * n]
        ssem_ref, rsem_ref = refs[2 * n], refs[2 * n + 1]
        x, y, c = lax.axis_index("x"), lax.axis_index("y"), lax.axis_index("c")
        kme = 2 * x + y
        for p in range(n):
            piece = land_in[p].at[0]
            for k in ks:
                for pc in range(1 if chip_sums else 2):
                    mine = (kme == k) if chip_sums else jnp.logical_and(kme == k, c == pc)

                    @pl.when(jnp.logical_not(mine))
                    def _():
                        _remote(piece, piece, ssem_ref.at[p], rsem_ref.at[p], (x, y, c)).wait_send()
            owner = kme == ks[0]
            for k in ks[1:]:
                owner = jnp.logical_or(owner, kme == k)

            @pl.when(owner)
            def _():
                others = land_in[p].at[pl.ds(0, land_in[p].shape[0] - 1)]
                _remote(others, others, ssem_ref.at[p], rsem_ref.at[p], (x, y, c)).wait_recv()

    arrays = list(parts) + list(lands)
    outs = pl.pallas_call(
        body, name=name, out_shape=[pltpu.HBM(a.shape, a.dtype) for a in arrays],
        in_specs=[HBM_SPEC] * (2 * n) + [SEM_SPEC, SEM_SPEC, ANY_SPEC], out_specs=[HBM_SPEC] * (2 * n),
        input_output_aliases={i: i for i in range(2 * n)},
        compiler_params=pltpu.CompilerParams(has_side_effects=EFFECT),
    )(*arrays, ssem, rsem, after)
    return list(outs[0:n]), list(outs[n:2 * n])


def _reduce_exchange(parts, lands):
    shapes = [(2 * l.shape[1], l.shape[2]) for l in lands]
    step = 128

    def body(in01, in23, pout, ppg, ppp, l_in, l_out, l_pg, l_pp, g_in, g_out, g_pg, g_pp, ssem, rsem):
        x, y, c = lax.axis_index("x"), lax.axis_index("y"), lax.axis_index("c")
        kme = 2 * x + y
        me = 4 * x + 2 * y + c
        sibling = (x, y, 1 - c)
        sends = []
        for p, (land, gout) in enumerate([(l_in, g_in), (l_out, g_out), (l_pg, g_pg), (l_pp, g_pp)]):
            hrows = land.shape[1]
            mine0 = pl.multiple_of(c * hrows, step)
            for r0 in range(0, hrows, step):
                rs = pl.ds(mine0 + r0, step)
                if p == 0:
                    own = jnp.where(kme >= 2, in23[kme & 1, r0:r0 + step, :], in01[kme & 1, r0:r0 + step, :])
                    slot = kme
                else:
                    own = (pout, ppg, ppp)[p - 1][kme, rs, :]
                    slot = me
                s = jnp.zeros((step, land.shape[2]), F32)
                for j in range(land.shape[0]):
                    s = s + jnp.where(slot == j, own, land[j, r0:r0 + step, :]).astype(F32)
                gout[rs, :] = s
            mine = pl.ds(mine0, hrows)
            cp = _remote(gout.at[mine], gout.at[mine], ssem.at[p], rsem.at[p], sibling)
            cp.start()
            sends.append(cp)
        for p, (land, gout) in enumerate([(l_in, g_in), (l_out, g_out), (l_pg, g_pg), (l_pp, g_pp)]):
            hrows = land.shape[1]
            other = pl.ds(pl.multiple_of((1 - c) * hrows, step), hrows)
            _remote(gout.at[other], gout.at[other], ssem.at[p], rsem.at[p], sibling).wait_recv()
        for cp in sends:
            cp.wait_send()

    return pl.pallas_call(
        body, name="reduce_exchange", out_shape=[jax.ShapeDtypeStruct(s, F32) for s in shapes],
        in_specs=[VMEM_SPEC] * 9, out_specs=[VMEM_SPEC] * 4,
        scratch_shapes=[pltpu.SemaphoreType.DMA((4,)), pltpu.SemaphoreType.DMA((4,))],
        compiler_params=pltpu.CompilerParams(vmem_limit_bytes=48 * MIB),
    )(*parts, *lands)


def _small_allreduce(sm_in, sm_tail, sm_a, sm_b, rg_c):
    PR = DG // NDEV

    def body(in_ref, tail_ref, a_ref, b_ref, rg_ref, v_out, rg_out, vbuf, vrecv, rgrecv, ssem, rsem):
        x, y, c = lax.axis_index("x"), lax.axis_index("y"), lax.axis_index("c")
        me = 4 * x + 2 * y + c

        def pair(ref, r0, r1):
            return jnp.concatenate([ref[r0:r0 + 1, :], ref[r1:r1 + 1, :]], axis=1)

        rows = {
            ROW_NORM_MIX: in_ref[0:1, :], ROW_FINAL_NORM: tail_ref[1:2, :], ROW_B_PG: tail_ref[2:3, :],
            ROW_PLE_NORM: tail_ref[3:4, :], ROW_CB_BA: pair(a_ref, 0, 1), ROW_BX_LAM: pair(a_ref, 2, 3),
            ROW_CW01: pair(a_ref, 4, 5), ROW_CW23: pair(a_ref, 6, 7), ROW_HG_LB: pair(b_ref, 2, 3),
            ROW_HG_NW: jnp.concatenate([b_ref[1:2, :], jnp.zeros((1, DG), F32)], axis=1),
            ROW_LOSS: tail_ref[4:5, :],
        }
        vbuf[...] = jnp.zeros_like(vbuf)
        for r, row in rows.items():
            for j in range(NDEV):
                vbuf[j, r:r + 1, :] = row[:, j * 128:(j + 1) * 128]

        def peer(mask):
            px = x ^ ((mask >> 2) & 1)
            py = y ^ ((mask >> 1) & 1)
            pc = c ^ (mask & 1)
            return (px, py, pc), 4 * px + 2 * py + pc

        def rg_rows(r):
            return pl.ds(pl.multiple_of(r * PR, PR), PR)

        first = []
        for mask in range(1, NDEV):
            dev, r = peer(mask)
            i = mask - 1
            cp = _remote(vbuf.at[r], vrecv.at[i], ssem.at[i], rsem.at[i], dev)
            cp.start()
            first.append(cp)
            cp = _remote(rg_ref.at[rg_rows(r)], rgrecv.at[i], ssem.at[7 + i], rsem.at[7 + i], dev)
            cp.start()
            first.append(cp)
        sv = vbuf[me]
        sr = rg_ref[rg_rows(me), :]
        for i in range(NDEV - 1):
            first[2 * i].wait_recv()
            first[2 * i + 1].wait_recv()
            sv = sv + vrecv[i]
            sr = sr + rgrecv[i]
        v_out[me] = sv
        rg_out[rg_rows(me), :] = sr
        second = []
        for mask in range(1, NDEV):
            dev, r = peer(mask)
            i = mask - 1
            cp = _remote(v_out.at[me], v_out.at[me], ssem.at[14 + i], rsem.at[14 + i], dev)
            cp.start()
            second.append(cp)
            cp = _remote(rg_out.at[rg_rows(me)], rg_out.at[rg_rows(me)], ssem.at[21 + i], rsem.at[21 + i], dev)
            cp.start()
            second.append(cp)
        for mask in range(1, NDEV):
            dev, r = peer(mask)
            i = mask - 1
            _remote(v_out.at[r], v_out.at[r], ssem.at[14 + i], rsem.at[14 + i], dev).wait_recv()
            _remote(rg_out.at[rg_rows(r)], rg_out.at[rg_rows(r)], ssem.at[21 + i], rsem.at[21 + i], dev).wait_recv()
        for cp in first + second:
            cp.wait_send()

    return pl.pallas_call(
        body, name="small_allreduce",
        out_shape=[jax.ShapeDtypeStruct((NDEV, VROWS, 128), F32), jax.ShapeDtypeStruct((DG, 128), F32)],
        in_specs=[VMEM_SPEC] * 5, out_specs=[VMEM_SPEC] * 2,
        scratch_shapes=[pltpu.VMEM((NDEV, VROWS, 128), F32), pltpu.VMEM((NDEV - 1, VROWS, 128), F32),
                        pltpu.VMEM((NDEV - 1, PR, 128), F32),
                        pltpu.SemaphoreType.DMA((28,)), pltpu.SemaphoreType.DMA((28,))],
    )(sm_in, sm_tail, sm_a, sm_b, rg_c)


def _final_reduce(parts, lands, sm_in, sm_tail, sm_a, sm_b, rg_c):
    shapes = [(2 * l.shape[1], l.shape[2]) for l in lands]
    step = 128
    PR = DG // NDEV

    def body(in01, in23, pout, ppg, ppp, l_in, l_out, l_pg, l_pp, in_h, tail_h, a_h, b_h, rg_h,
             g_in, g_out, g_pg, g_pp, v_out, rg_out, vbuf, vrecv, rgrecv,
             lb_in, lb_out, lb_pg, lb_pp, ob_in, ob_out, ob_pg, ob_pp,
             in_ref, tail_ref, a_ref, b_ref, rg_ref, lsem, ssem_l, bsem_s, bsem_r, ssem, rsem):
        x, y, c = lax.axis_index("x"), lax.axis_index("y"), lax.axis_index("c")
        kme = 2 * x + y
        me = 4 * x + 2 * y + c
        sibling = (x, y, 1 - c)
        small_fetch = [pltpu.make_async_copy(src, dst, ssem_l.at[i]) for i, (src, dst) in enumerate(
            [(in_h, in_ref), (tail_h, tail_ref), (a_h, a_ref), (b_h, b_ref), (rg_h, rg_ref)])]
        for cp in small_fetch:
            cp.start()

        lands_hbm = [l_in, l_out, l_pg, l_pp]
        land_bufs = [lb_in, lb_out, lb_pg, lb_pp]
        own_bufs = [ob_in, ob_out, ob_pg, ob_pp]
        fetches = []
        for p in range(4):
            cp = pltpu.make_async_copy(lands_hbm[p], land_bufs[p], lsem.at[p])
            cp.start()
            fetches.append(cp)

        @pl.when(kme >= 2)
        def _():
            pltpu.make_async_copy(in23.at[kme & 1], ob_in, lsem.at[4]).start()

        @pl.when(kme < 2)
        def _():
            pltpu.make_async_copy(in01.at[kme & 1], ob_in, lsem.at[4]).start()

        own_fetches = [pltpu.make_async_copy(in01.at[0], ob_in, lsem.at[4])]
        for p, part in enumerate([pout, ppg, ppp]):
            hrows = own_bufs[p + 1].shape[0]
            cp = pltpu.make_async_copy(part.at[kme, pl.ds(pl.multiple_of(c * hrows, step), hrows)],
                                       own_bufs[p + 1], lsem.at[5 + p])
            cp.start()
            own_fetches.append(cp)

        def pair(ref, r0, r1):
            return jnp.concatenate([ref[r0:r0 + 1, :], ref[r1:r1 + 1, :]], axis=1)

        for cp in small_fetch:
            cp.wait()
        rows = {
            ROW_NORM_MIX: in_ref[0:1, :], ROW_FINAL_NORM: tail_ref[1:2, :], ROW_B_PG: tail_ref[2:3, :],
            ROW_PLE_NORM: tail_ref[3:4, :], ROW_CB_BA: pair(a_ref, 0, 1), ROW_BX_LAM: pair(a_ref, 2, 3),
            ROW_CW01: pair(a_ref, 4, 5), ROW_CW23: pair(a_ref, 6, 7), ROW_HG_LB: pair(b_ref, 2, 3),
            ROW_HG_NW: jnp.concatenate([b_ref[1:2, :], jnp.zeros((1, DG), F32)], axis=1),
            ROW_LOSS: tail_ref[4:5, :],
        }
        vbuf[...] = jnp.zeros_like(vbuf)
        for r, row in rows.items():
            for j in range(NDEV):
                vbuf[j, r:r + 1, :] = row[:, j * 128:(j + 1) * 128]

        def peer(mask):
            px = x ^ ((mask >> 2) & 1)
            py = y ^ ((mask >> 1) & 1)
            pc = c ^ (mask & 1)
            return (px, py, pc), 4 * px + 2 * py + pc

        def rg_rows(r):
            return pl.ds(pl.multiple_of(r * PR, PR), PR)

        first = []
        for mask in range(1, NDEV):
            dev, r = peer(mask)
            i = mask - 1
            cp = _remote(vbuf.at[r], vrecv.at[i], ssem.at[i], rsem.at[i], dev)
            cp.start()
            first.append(cp)
            cp = _remote(rg_ref.at[rg_rows(r)], rgrecv.at[i], ssem.at[7 + i], rsem.at[7 + i], dev)
            cp.start()
            first.append(cp)

        big = [(lb_in, g_in), (lb_out, g_out), (lb_pg, g_pg), (lb_pp, g_pp)]
        swaps = []
        for p, (land, gout) in enumerate(big):
            fetches[p].wait()
            own_fetches[p].wait()
            hrows = land.shape[1]
            mine0 = pl.multiple_of(c * hrows, step)
            slot = kme if p == 0 else me
            for r0 in range(0, hrows, step):
                rs = pl.ds(mine0 + r0, step)
                own = own_bufs[p][r0:r0 + step, :]
                s = jnp.zeros((step, land.shape[2]), F32)
                for j in range(land.shape[0]):
                    s = s + jnp.where(slot == j, own, land[j, r0:r0 + step, :]).astype(F32)
                gout[rs, :] = s
            mine = pl.ds(mine0, hrows)
            cp = _remote(gout.at[mine], gout.at[mine], bsem_s.at[p], bsem_r.at[p], sibling)
            cp.start()
            swaps.append(cp)

        sv = vbuf[me]
        sr = rg_ref[rg_rows(me), :]
        for i in range(NDEV - 1):
            first[2 * i].wait_recv()
            first[2 * i + 1].wait_recv()
            sv = sv + vrecv[i]
            sr = sr + rgrecv[i]
        v_out[me] = sv
        rg_out[rg_rows(me), :] = sr
        second = []
        for mask in range(1, NDEV):
            dev, r = peer(mask)
            i = mask - 1
            cp = _remote(v_out.at[me], v_out.at[me], ssem.at[14 + i], rsem.at[14 + i], dev)
            cp.start()
            second.append(cp)
            cp = _remote(rg_out.at[rg_rows(me)], rg_out.at[rg_rows(me)], ssem.at[21 + i], rsem.at[21 + i], dev)
            cp.start()
            second.append(cp)
        for p, (land, gout) in enumerate(big):
            hrows = land.shape[1]
            other = pl.ds(pl.multiple_of((1 - c) * hrows, step), hrows)
            _remote(gout.at[other], gout.at[other], bsem_s.at[p], bsem_r.at[p], sibling).wait_recv()
        for mask in range(1, NDEV):
            dev, r = peer(mask)
            i = mask - 1
            _remote(v_out.at[r], v_out.at[r], ssem.at[14 + i], rsem.at[14 + i], dev).wait_recv()
            _remote(rg_out.at[rg_rows(r)], rg_out.at[rg_rows(r)], ssem.at[21 + i], rsem.at[21 + i], dev).wait_recv()
        for cp in first + swaps + second:
            cp.wait_send()

    out_shape = [jax.ShapeDtypeStruct(s, F32) for s in shapes]
    out_shape += [jax.ShapeDtypeStruct((NDEV, VROWS, 128), F32), jax.ShapeDtypeStruct((DG, 128), F32)]
    outs = pl.pallas_call(
        body, name="final_reduce", out_shape=out_shape,
        in_specs=[ANY_SPEC] * 14, out_specs=[VMEM_SPEC] * 6,
        scratch_shapes=[pltpu.VMEM((NDEV, VROWS, 128), F32), pltpu.VMEM((NDEV - 1, VROWS, 128), F32),
                        pltpu.VMEM((NDEV - 1, PR, 128), F32)]
        + [pltpu.VMEM(l.shape, BF16) for l in lands]
        + [pltpu.VMEM(l.shape[1:], BF16) for l in lands]
        + [pltpu.VMEM(a.shape, F32) for a in (sm_in, sm_tail, sm_a, sm_b, rg_c)]
        + [pltpu.SemaphoreType.DMA((8,)), pltpu.SemaphoreType.DMA((5,)),
                        pltpu.SemaphoreType.DMA((4,)), pltpu.SemaphoreType.DMA((4,)),
                        pltpu.SemaphoreType.DMA((28,)), pltpu.SemaphoreType.DMA((28,))],
        compiler_params=pltpu.CompilerParams(vmem_limit_bytes=48 * MIB),
    )(*parts, *lands, sm_in, sm_tail, sm_a, sm_b, rg_c)
    return list(outs[0:4]), outs[4], outs[5]


def _adam_rows(w, g, m, v):
    m2 = ADAM_B1 * m + (1.0 - ADAM_B1) * g
    v2 = ADAM_B2 * v + (1.0 - ADAM_B2) * (g * g)
    m_hat = m2 / (1.0 - ADAM_B1 ** ADAM_STEP)
    v_hat = v2 / (1.0 - ADAM_B2 ** ADAM_STEP)
    delta = -ADAM_LR * (m_hat / (jnp.sqrt(v_hat) + ADAM_EPS) + ADAM_WD * w)
    return delta, m2, v2


def _adam_big(gs, ws, ms, vs):
    n = len(gs)
    steps = 8

    def body(*refs):
        ins, outs = refs[:4 * n], refs[4 * n:]
        for i in range(n):
            g, w, m, v = (r[...] for r in ins[4 * i:4 * i + 4])
            d, m2, v2 = _adam_rows(w, g, m, v)
            outs[3 * i][...] = d
            outs[3 * i + 1][...] = m2
            outs[3 * i + 2][...] = v2

    in_specs, out_specs, out_shape, args = [], [], [], []
    for g, w, m, v in zip(gs, ws, ms, vs):
        r, c = w.shape
        spec = lambda: pl.BlockSpec((r // steps, c), lambda i: (i, 0))
        in_specs += [spec() for _ in range(4)]
        out_specs += [spec() for _ in range(3)]
        out_shape += [jax.ShapeDtypeStruct((r, c), F32)] * 3
        args += [pltpu.with_memory_space_constraint(a, pltpu.HBM) for a in (g, w, m, v)]
    outs = pl.pallas_call(
        body, name="adam_big", grid=(steps,), in_specs=in_specs, out_specs=out_specs, out_shape=out_shape,
        compiler_params=_cparams(("parallel",), 32),
    )(*args)
    return [tuple(outs[3 * i:3 * i + 3]) for i in range(n)]


_VEC_PARAMS = [
    ("norm_mix_w", ROW_NORM_MIX, 0, D), ("final_norm_w", ROW_FINAL_NORM, 0, D),
    ("b_ple_gate", ROW_B_PG, 0, D), ("ple_norm_w", ROW_PLE_NORM, 0, D),
    ("conv_b", ROW_CB_BA, 0, DG), ("rg_ba", ROW_CB_BA, DG, DG),
    ("rg_bx", ROW_BX_LAM, 0, DG), ("rg_lambda", ROW_BX_LAM, DG, DG),
    ("hg_norm_w", ROW_HG_NW, 0, HD),
]
_SMALL_ORDER = [n for n, _, _, _ in _VEC_PARAMS] + ["hg_lb", "conv_w", "rg_wa", "rg_wx"]


def _adam_small(vred, rgred, ws, ms, vs):
    names = _SMALL_ORDER
    n = len(names)

    def body(vred_ref, rg_ref, *refs):
        w_refs = dict(zip(names, refs[0:n]))
        m_refs = dict(zip(names, refs[n:2 * n]))
        v_refs = dict(zip(names, refs[2 * n:3 * n]))
        outs = refs[3 * n:]
        o_refs = {nm: outs[4 * i:4 * i + 4] for i, nm in enumerate(names)}
        kme = 2 * lax.axis_index("x") + lax.axis_index("y")

        def update(nm, g, idx):
            d, m2, v2 = _adam_rows(w_refs[nm][idx], g, m_refs[nm][idx], v_refs[nm][idx])
            og, od, om, ov = o_refs[nm]
            og[idx] = g
            od[idx] = d
            om[idx] = m2
            ov[idx] = v2

        def packed(row, lane0, width):
            return jnp.concatenate([vred_ref[j, row:row + 1, :] for j in range(lane0 // 128, (lane0 + width) // 128)],
                                   axis=1)

        everything = (slice(None), slice(None))
        for nm, row, lane0, width in _VEC_PARAMS:
            update(nm, packed(row, lane0, width), everything)
        for r in range(2):
            update("hg_lb", packed(ROW_HG_LB, r * DG, DG), (slice(r, r + 1), slice(None)))
        for j in range(4):
            g = vred_ref[(j % 2) * 4 + kme, ROW_CW01 + j // 2:ROW_CW01 + j // 2 + 1, :]
            update("conv_w", g, (slice(j, j + 1), slice(None)))
        for r0 in range(0, DG, 128):
            rs = (slice(r0, r0 + 128), slice(None))
            both = rg_ref[r0:r0 + 128, :]
            update("rg_wa", both[:, 0:RGB], rs)
            update("rg_wx", pltpu.roll(both, RGB, 1)[:, 0:RGB], rs)

    args = [vred, rgred] + [d[nm] for d in (ws, ms, vs) for nm in names]
    out_shape = []
    for nm in names:
        out_shape += [jax.ShapeDtypeStruct(ws[nm].shape, F32)] * 4
    whole = lambda s: pl.BlockSpec(s.shape, lambda i, nd=len(s.shape): (0,) * nd)
    outs = pl.pallas_call(
        body, name="adam_small", out_shape=out_shape, grid=(1,),
        in_specs=[whole(a) for a in args], out_specs=[whole(s) for s in out_shape],
    )(*args)
    return {nm: tuple(outs[4 * i:4 * i + 4]) for i, nm in enumerate(names)}


def _fwd_inproj(x, nw, w_in_b, dep):
    tm = 512

    def body(x_ref, nw_ref, w_ref, dep_ref, pf_ref, pb_ref, u_ref):
        xv = x_ref[...]
        s = lax.rsqrt(jnp.mean(xv * xv, axis=-1, keepdims=True) + EPS)
        u = (xv * s * nw_ref[...]).astype(BF16)
        u_ref[...] = u
        r = [jnp.dot(u, w_ref[k], preferred_element_type=F32) for k in range(NSHARD)]
        h = DG // 2
        pf_ref[:, 0:DG] = r[0][:, 0:DG]
        pb_ref[:, 0:h] = r[0][:, DG:SHW].astype(BF16)
        pb_ref[:, h:DG] = r[1][:, 0:h].astype(BF16)
        pb_ref[:, DG:2 * DG] = r[1][:, h:SHW].astype(BF16)
        pf_ref[:, DG:2 * DG] = r[2][:, 0:DG]
        pb_ref[:, 2 * DG:2 * DG + h] = r[2][:, DG:SHW].astype(BF16)
        pb_ref[:, 2 * DG + h:3 * DG] = r[3][:, 0:h].astype(BF16)
        pb_ref[:, 3 * DG:4 * DG] = r[3][:, h:SHW].astype(BF16)

    return pl.pallas_call(
        body, name="fwd_inproj", grid=(T // tm,),
        in_specs=[pl.BlockSpec((tm, D), lambda i: (i, 0)),
                  pl.BlockSpec((1, D), lambda i: (0, 0)),
                  pl.BlockSpec((NSHARD, D, SHW), lambda i: (0, 0, 0), pipeline_mode=pl.Buffered(1)), ANY_SPEC],
        out_specs=[pl.BlockSpec((tm, 2 * DG), lambda i: (i, 0)),
                   pl.BlockSpec((tm, 4 * DG), lambda i: (i, 0)),
                   pl.BlockSpec((tm, D), lambda i: (i, 0))],
        out_shape=[jax.ShapeDtypeStruct((T, 2 * DG), F32), jax.ShapeDtypeStruct((T, 4 * DG), BF16),
                   jax.ShapeDtypeStruct((T, D), BF16)],
        compiler_params=_cparams(("parallel",), 48),
    )(x, nw, w_in_b, dep)


def _conv_rows(cw_ref):
    return [jnp.concatenate([cw_ref[k, j:j + 1, :] for k in range(NSHARD)], axis=1) for j in range(4)]


def _rg_conv(xa, prev8, cw, cb, rows):
    taps = [_shift_down(xa, prev8, 3, rows), _shift_down(xa, prev8, 2, rows),
            _shift_down(xa, prev8, 1, rows), xa]
    xc = cb
    for j in range(4):
        xc = xc + taps[j] * cw[j]
    return xc, taps


def _block_mask():
    r = lax.broadcasted_iota(I32, (DG, DG), 0)
    c = lax.broadcasted_iota(I32, (DG, DG), 1)
    return (r >> 6) == (c >> 6)


def _dense_from_blocks(wc):
    j = lax.broadcasted_iota(I32, (RGB, DG), 0)
    c = lax.broadcasted_iota(I32, (RGB, DG), 1)
    spread = _mm_exact(wc, ((c & (RGB - 1)) == j).astype(F32))
    return jnp.where(_block_mask(), spread, 0.0)


def _blocks_from_dense(da, dx):
    c = lax.broadcasted_iota(I32, (DG, 128), 0)
    j = lax.broadcasted_iota(I32, (DG, 128), 1)
    hit = (c & (RGB - 1)) == (j & (RGB - 1))
    mask = _block_mask()
    return (_mm_exact(jnp.where(mask, da, 0.0), (hit & (j < RGB)).astype(F32))
            + _mm_exact(jnp.where(mask, dx, 0.0), (hit & (j >= RGB)).astype(F32)))


def _rg_gates(xc, wa, ba, wx, bx, sp, first_row):
    r = _sigmoid(_mm(xc, wa) + ba)
    i = _sigmoid(_mm(xc, wx) + bx)
    log_a = (-RG_C) * r * sp
    a = jnp.exp(log_a)
    a2 = a * a
    one_m_a2 = -jnp.tanh(log_a) * (a2 + 1.0)
    mult = jnp.where(first_row, 1.0, jnp.sqrt(one_m_a2))
    return r, i, a, a2, mult


def _softplus(z):
    return jnp.maximum(z, 0.0) + jnp.log1p(jnp.exp(-jnp.abs(z)))


def _fwd_rglru(pf, pb, cw3, conv_b, wa_c, ba, wx_c, bx, lam):
    tm = 512
    ng = tm // 8

    def body(xa_ref, ga_ref, cw_ref, cb_ref, wa_ref, ba_ref, wx_ref, bx_ref, lam_ref,
             h_ref, ya_ref, a_s, u_s, tail_s, hc_s, wa_s, wx_s):
        i = pl.program_id(0)

        @pl.when(i == 0)
        def _():
            tail_s[...] = jnp.zeros_like(tail_s)
            hc_s[...] = jnp.zeros_like(hc_s)
            wa_s[...] = _dense_from_blocks(wa_ref[...]).astype(BF16)
            wx_s[...] = _dense_from_blocks(wx_ref[...]).astype(BF16)

        rows = lax.broadcasted_iota(I32, (tm, DG), 0)
        xa = xa_ref[...]
        xc, _ = _rg_conv(xa, tail_s[...], _conv_rows(cw_ref), cb_ref[...], rows)
        tail_s[...] = xa[tm - 8:tm, :]
        sp = _softplus(-lam_ref[...])
        rp = _mm(xc, wa_s[...]) + ba_ref[...]
        ip = _mm(xc, wx_s[...]) + bx_ref[...]
        rb = 64
        rows_b = lax.broadcasted_iota(I32, (rb, DG), 0)
        rows8 = rows_b & 7
        carry = hc_s[0:1, :]
        for b0 in range(0, tm, rb):
            sl = slice(b0, b0 + rb)
            r = _sigmoid(rp[sl])
            ig = _sigmoid(ip[sl])
            log_a = (-RG_C) * r * sp
            av = jnp.exp(log_a)
            mult = jnp.sqrt(-jnp.tanh(log_a) * (av * av + 1.0))
            if b0 == 0:
                mult = jnp.where((rows_b + i * tm) == 0, 1.0, mult)
            uv = mult * (ig * xc[sl])
            for d in (1, 2, 4):
                keep = rows8 >= d
                uv = uv + av * jnp.where(keep, _roll_in_groups(uv, d), 0.0)
                av = av * jnp.where(keep, _roll_in_groups(av, d), 1.0)
            ga = ga_ref[sl, :].astype(F32)
            gate = ga * _sigmoid(ga)
            hs = []
            for g in range(rb // 8):
                gs = slice(g * 8, (g + 1) * 8)
                hv = uv[gs] + av[gs] * carry
                carry = hv[7:8, :]
                hs.append(hv)
            hb = jnp.concatenate(hs, axis=0)
            h_ref[sl, :] = hb
            ya_ref[sl, :] = (hb * gate).astype(BF16)
        hc_s[0:1, :] = carry

    vec = lambda: pl.BlockSpec((1, DG), lambda i: (0, 0))
    blocks = lambda: pl.BlockSpec((DG, RGB), lambda i: (0, 0))
    return pl.pallas_call(
        body, name="fwd_rglru", grid=(T // tm,),
        in_specs=[pl.BlockSpec((tm, DG), lambda i: (i, 0)),
                  pl.BlockSpec((tm, DG), lambda i: (i, 0)),
                  pl.BlockSpec((NSHARD, 4, 128), lambda i: (0, 0, 0)), vec(),
                  blocks(), vec(), blocks(), vec(), vec()],
        out_specs=[pl.BlockSpec((tm, DG), lambda i: (i, 0)),
                   pl.BlockSpec((tm, DG), lambda i: (i, 0))],
        out_shape=[jax.ShapeDtypeStruct((T, DG), F32), jax.ShapeDtypeStruct((T, DG), BF16)],
        scratch_shapes=[pltpu.VMEM((tm, DG), F32), pltpu.VMEM((tm, DG), F32),
                        pltpu.VMEM((8, DG), F32), pltpu.VMEM((8, DG), F32),
                        pltpu.VMEM((DG, DG), BF16), pltpu.VMEM((DG, DG), BF16)],
        compiler_params=_cparams(("arbitrary",), 48),
    )(pf, pb, cw3, conv_b, wa_c, ba, wx_c, bx, lam)


def _hg_lower_bound(lb_ref):
    return _sig_pair(lb_ref[0:1, :] - lb_ref[1:2, :])


def _hg_gates(fz, lb, one_m_lb):
    sg, sn = _sig_pair(fz)
    f = lb + one_m_lb * sg
    return sg, sn, f, jnp.log(f), one_m_lb * sn


def _tri(lower):
    r = lax.broadcasted_iota(I32, (CH, CH), 0)
    c = lax.broadcasted_iota(I32, (CH, CH), 1)
    return (r >= c) if lower else (r <= c)


def _chunk_cumsum(v, rows64):
    for d in (1, 2, 4, 8, 16, 32):
        v = v + jnp.where(rows64 >= d, pltpu.roll(v, d, 0), 0.0)
    return v


def _chunk_rev_cumsum(v, rows64):
    n = v.shape[0]
    for d in (1, 2, 4, 8, 16, 32):
        v = v + jnp.where(rows64 < CH - d, pltpu.roll(v, n - d, 0), 0.0)
    return v


def _hg_recompute(q_ref, f_ref, lb, one_m_lb, rows64, eb_s, enb_s, ekd_s, qe_s, ke_s, kd_s, dec_s):
    nc = q_ref.shape[0] // CH
    sg, sn, f, logf, k = _hg_gates(f_ref[...], lb, one_m_lb)
    q = q_ref[...].astype(F32)
    sq = _sigmoid(q)
    qs = q * sq * (HD ** -0.5)
    b = _chunk_cumsum(logf, rows64)
    for c in range(nc):
        rs = slice(c * CH, (c + 1) * CH)
        b_c = b[rs]
        bl = b_c[CH - 1:CH, :]
        eb, enb, ekd = jnp.exp(b_c), jnp.exp(-b_c), jnp.exp(bl - b_c)
        if eb_s is not None:
            eb_s[rs, :] = eb
            enb_s[rs, :] = enb
            ekd_s[rs, :] = ekd
        qe_s[rs, :] = (qs[rs] * eb).astype(BF16)
        ke_s[rs, :] = (k[rs] * enb).astype(BF16)
        kd_s[rs, :] = (k[rs] * ekd).astype(BF16)
        dec_s[c:c + 1, :] = jnp.exp(bl)
    return sg, sn, f, k, q, sq, qs


def _fwd_hgrn2(pf, pb, hg_lb, hg_nw):
    tm = 512
    nc = tm // CH

    def body(q_ref, f_ref, v_ref, g_ref, lb_ref, nw_ref, yb_ref, o_ref, sp_ref,
             st_s, qe_s, ke_s, kd_s, vb_s, dec_s, p_s, ds_s):
        i = pl.program_id(0)

        @pl.when(i == 0)
        def _():
            st_s[...] = jnp.zeros_like(st_s)

        lb, one_m_lb = _hg_lower_bound(lb_ref)
        rows64 = lax.broadcasted_iota(I32, (tm, DG), 0) & (CH - 1)
        _hg_recompute(q_ref, f_ref, lb, one_m_lb, rows64, None, None, None, qe_s, ke_s, kd_s, dec_s)
        vb_s[...] = v_ref[...]
        mask = _tri(True)
        items = [(c, hd, slice(c * CH, (c + 1) * CH), slice(hd * HD, (hd + 1) * HD))
                 for c in range(nc) for hd in range(NH)]
        for c, hd, rs, cols in items:
            p_s[c * NH + hd] = jnp.where(mask, _mm_nt(qe_s[rs, cols], ke_s[rs, cols]), 0.0).astype(BF16)
            ds_s[c * NH + hd] = _mm_tn(vb_s[rs, cols], kd_s[rs, cols])
        for c, hd, rs, cols in items:
            st = st_s[hd]
            sp_ref[hd, c] = st
            st_s[hd] = st * dec_s[c:c + 1, cols] + ds_s[c * NH + hd]
        for c, hd, rs, cols in items:
            o_ref[rs, cols] = _mm(p_s[c * NH + hd], vb_s[rs, cols]) + _mm_nt(qe_s[rs, cols], sp_ref[hd, c])
        nw = nw_ref[...]
        for hd in range(NH):
            cols = slice(hd * HD, (hd + 1) * HD)
            o = o_ref[:, cols]
            so = lax.rsqrt(jnp.mean(o * o, axis=-1, keepdims=True) + EPS)
            g = g_ref[:, cols].astype(F32)
            sg = _sigmoid(g)
            yb_ref[:, cols] = (o * so * nw * (g * sg)).astype(BF16)

    col = lambda j: pl.BlockSpec((tm, DG), lambda i: (i, j))
    return pl.pallas_call(
        body, name="fwd_hgrn2", grid=(T // tm,),
        in_specs=[col(1), col(1), col(2), col(3),
                  pl.BlockSpec((2, DG), lambda i: (0, 0)),
                  pl.BlockSpec((1, HD), lambda i: (0, 0))],
        out_specs=[pl.BlockSpec((tm, DG), lambda i: (i, 0)),
                   pl.BlockSpec((tm, DG), lambda i: (i, 0)),
                   pl.BlockSpec((NH, nc, HD, HD), lambda i: (0, i, 0, 0))],
        out_shape=[jax.ShapeDtypeStruct((T, DG), BF16), jax.ShapeDtypeStruct((T, DG), F32),
                   jax.ShapeDtypeStruct((NH, NCHUNK, HD, HD), F32)],
        scratch_shapes=[pltpu.VMEM((NH, HD, HD), F32),
                        pltpu.VMEM((tm, DG), BF16), pltpu.VMEM((tm, DG), BF16), pltpu.VMEM((tm, DG), BF16),
                        pltpu.VMEM((tm, DG), BF16), pltpu.VMEM((nc, DG), F32),
                        pltpu.VMEM((nc * NH, CH, CH), BF16), pltpu.VMEM((nc * NH, HD, HD), F32)],
        compiler_params=_cparams(("arbitrary",), 48),
    )(pb, pf, pb, pb, hg_lb, hg_nw)


def _tail_fwd_bwd(x, p, tgt, ya, yb, w_out_b, w_pg_b, w_pp_b, ple_nw, b_pg, fnw):
    tm = 512
    nt = T // tm
    QR = D // NSHARD

    def body(x_ref, p_ref, t_ref, ya_ref, yb_ref, wo_ref, wg_ref, wp_ref, pw_ref, b_ref, fw_ref,
             dh1_ref, dyab_ref, dwo_ref, dwg_ref, dwp_ref, sm_ref, dwo_s, dwg_s, dwp_s):
        i = pl.program_id(0)

        @pl.when(i == 0)
        def _():
            dwo_s[...] = jnp.zeros_like(dwo_s)
            dwg_s[...] = jnp.zeros_like(dwg_s)
            dwp_s[...] = jnp.zeros_like(dwp_s)
            sm_ref[...] = jnp.zeros_like(sm_ref)

        ya = ya_ref[...]
        yb = yb_ref[...]
        pv = p_ref[...].astype(BF16)
        pw = pw_ref[...]
        fw = fw_ref[...]
        h1 = x_ref[...] + _mm(ya, wo_ref[0:DG, :]) + _mm(yb, wo_ref[DG:D, :])
        s2 = lax.rsqrt(jnp.mean(h1 * h1, axis=-1, keepdims=True) + EPS)
        n2h = h1 * s2
        n2 = (n2h * pw).astype(BF16)
        z = _mm(n2, wg_ref[...]) + b_ref[...]
        gate = _sigmoid(z)
        pp = jnp.concatenate([_mm(pv, wp_ref[k]) for k in range(NSHARD)], axis=1)
        h2 = h1 + gate * pp
        s3 = lax.rsqrt(jnp.mean(h2 * h2, axis=-1, keepdims=True) + EPS)
        hn = h2 * s3
        err = hn * fw - t_ref[...]
        sm_ref[0:1, :] += _rowsum(err * err)
        dy = err * (1.0 / D)
        sm_ref[1:2, :] += _rowsum(dy * hn)
        g3 = dy * fw
        dh2 = s3 * (g3 - hn * jnp.mean(g3 * hn, axis=-1, keepdims=True))
        dpp = (dh2 * gate).astype(BF16)
        dz = dh2 * pp * gate * (1.0 - gate)
        sm_ref[2:3, :] += _rowsum(dz)
        dzb = dz.astype(BF16)
        dwg_s[...] += _mm_tn(n2, dzb)
        dn2 = _mm_nt(dzb, wg_ref[...])
        for k in range(NSHARD):
            dwp_s[k] += _mm_tn(pv, dpp[:, k * PLE:(k + 1) * PLE])
        sm_ref[3:4, :] += _rowsum(dn2 * n2h)
        g2 = dn2 * pw
        dh1 = dh2 + s2 * (g2 - n2h * jnp.mean(g2 * n2h, axis=-1, keepdims=True))
        dh1_ref[...] = dh1
        dh1b = dh1.astype(BF16)
        dyab_ref[...] = _mm_nt(dh1b, wo_ref[...])
        dwo_s[0:DG, :] += _mm_tn(ya, dh1b)
        dwo_s[DG:D, :] += _mm_tn(yb, dh1b)

        @pl.when(i == nt - 1)
        def _():
            total = jnp.sum(sm_ref[0:1, :], axis=-1, keepdims=True) * (0.5 / D)
            sm_ref[4:5, :] = jnp.broadcast_to(total, (1, D))
            for k in range(NSHARD):
                dwo_ref[k] = dwo_s[k * QR:(k + 1) * QR, :].astype(BF16)
                dwg_ref[k] = dwg_s[k * QR:(k + 1) * QR, :].astype(BF16)
                dwp_ref[k] = dwp_s[k].astype(BF16)

    row = lambda w: pl.BlockSpec((tm, w), lambda i: (i, 0))
    const2 = lambda s: pl.BlockSpec(s, lambda i: (0, 0), pipeline_mode=pl.Buffered(1))
    const3 = lambda s: pl.BlockSpec(s, lambda i: (0, 0, 0), pipeline_mode=pl.Buffered(1))
    return pl.pallas_call(
        body, name="tail_fwd_bwd", grid=(nt,),
        in_specs=[row(D), row(PLE), row(D), row(DG), row(DG),
                  const2((D, D)), const2((D, D)), const3((NSHARD, PLE, PLE)),
                  const2((1, D)), const2((1, D)), const2((1, D))],
        out_specs=[row(D), row(D), const3((NSHARD, QR, D)), const3((NSHARD, QR, D)),
                   const3((NSHARD, PLE, PLE)), const2((8, D))],
        out_shape=[jax.ShapeDtypeStruct((T, D), F32), jax.ShapeDtypeStruct((T, D), F32),
                   jax.ShapeDtypeStruct((NSHARD, QR, D), BF16), jax.ShapeDtypeStruct((NSHARD, QR, D), BF16),
                   jax.ShapeDtypeStruct((NSHARD, PLE, PLE), BF16), jax.ShapeDtypeStruct((8, D), F32)],
        scratch_shapes=[pltpu.VMEM((D, D), F32), pltpu.VMEM((D, D), F32), pltpu.VMEM((NSHARD, PLE, PLE), F32)],
        compiler_params=_cparams(("arbitrary",), 62),
    )(x, p, tgt, pltpu.with_memory_space_constraint(ya, pltpu.HBM), pltpu.with_memory_space_constraint(yb, pltpu.HBM),
      w_out_b, w_pg_b, w_pp_b, ple_nw, b_pg, fnw)


def _bwd_rglru(pf, pb, h, dyab, cw3, conv_b, wa_c, ba, wx_c, bx, lam, dep):
    tm = 512
    nt = T // tm
    ng = tm // 8

    def body(xa_ref, ga_ref, xp_ref, h_ref, hp_ref, dya_ref, cw_ref, cb_ref, wa_ref, ba_ref, wx_ref, bx_ref,
             lam_ref, dep_ref, da_ref, rg_ref, sm_ref, a_s, g_s, cg_s, nxt_s, wa_s, wx_s, dwa_s, dwx_s):
        i = pl.program_id(0)
        tile = nt - 1 - i

        @pl.when(i == 0)
        def _():
            dwa_s[...] = jnp.zeros_like(dwa_s)
            dwx_s[...] = jnp.zeros_like(dwx_s)
            sm_ref[...] = jnp.zeros_like(sm_ref)
            cg_s[...] = jnp.zeros_like(cg_s)
            nxt_s[...] = jnp.zeros_like(nxt_s)
            wa_s[...] = _dense_from_blocks(wa_ref[...]).astype(BF16)
            wx_s[...] = _dense_from_blocks(wx_ref[...]).astype(BF16)

        rows = lax.broadcasted_iota(I32, (tm, DG), 0)
        has_prev = tile > 0
        xa = xa_ref[...]
        xprev = jnp.where(has_prev, xp_ref[...], 0.0)
        cw = _conv_rows(cw_ref)
        xc, taps = _rg_conv(xa, xprev, cw, cb_ref[...], rows)
        lam_v = lam_ref[...]
        sp = _softplus(-lam_v)
        first_row = (rows + tile * tm) == 0
        r, ig, a, a2, mult = _rg_gates(xc, wa_s[...], ba_ref[...], wx_s[...], bx_ref[...], sp, first_row)
        hv = h_ref[...]
        hprev = jnp.where(has_prev, hp_ref[...], 0.0)
        h_m1 = _shift_down(hv, hprev, 1, rows)
        ga = ga_ref[...].astype(F32)
        sg = _sigmoid(ga)
        dya = dya_ref[...]
        dga = dya * hv * (sg * (1.0 + ga * (1.0 - sg)))

        av = jnp.where(rows == tm - 1, 1.0, pltpu.roll(a, tm - 1, 0))
        gv = dya * (ga * sg)
        rows8 = rows & 7
        for d in (1, 2, 4):
            keep = rows8 < 8 - d
            gv = gv + av * jnp.where(keep, _roll_in_groups(gv, 8 - d), 0.0)
            av = av * jnp.where(keep, _roll_in_groups(av, 8 - d), 1.0)
        a_s[...] = av
        g_s[...] = gv
        carry = cg_s[0:1, :]
        for g in range(ng - 1, -1, -1):
            sl = slice(g * 8, (g + 1) * 8)
            ab, gb = a_s[sl, :], g_s[sl, :]
            g_s[sl, :] = gb + ab * carry
            carry = gb[0:1, :] + ab[0:1, :] * carry
        cg_s[0:1, :] = a[0:1, :] * carry

        gt = g_s[...]
        da = gt * h_m1
        ixc = ig * xc
        di = gt * mult * xc
        dxc = gt * mult * ig
        dlog_a = da * a + jnp.where(first_row, 0.0, gt * ixc * (-a2 / mult))
        sm_ref[3:4, :] += _rowsum(dlog_a * ((-RG_C) * r))
        dpr_f = dlog_a * ((-RG_C) * sp) * r * (1.0 - r)
        dpi_f = di * ig * (1.0 - ig)
        sm_ref[1:2, :] += _rowsum(dpr_f)
        sm_ref[2:3, :] += _rowsum(dpi_f)
        dpr = dpr_f.astype(BF16)
        dpi = dpi_f.astype(BF16)
        xcb = xc.astype(BF16)
        dwa_s[...] += _mm_tn(xcb, dpr)
        dwx_s[...] += _mm_tn(xcb, dpi)
        dxc = dxc + _mm_nt(dpr, wa_s[...]) + _mm_nt(dpi, wx_s[...])
        sm_ref[0:1, :] += _rowsum(dxc)
        for j in range(4):
            sm_ref[4 + j:5 + j, :] += _rowsum(dxc * taps[j])
        nxt = nxt_s[...]
        dxa = (dxc * cw[3] + _shift_up(dxc, nxt, 1, rows) * cw[2]
               + _shift_up(dxc, nxt, 2, rows) * cw[1] + _shift_up(dxc, nxt, 3, rows) * cw[0])
        nxt_s[...] = dxc[0:8, :]
        da_ref[:, 0:DG] = dxa.astype(BF16)
        da_ref[:, DG:D] = dga.astype(BF16)

        @pl.when(i == nt - 1)
        def _():
            _, s_neg = _sig_pair(lam_v)
            sm_ref[3:4, :] = sm_ref[3:4, :] * (-s_neg)
            rg_ref[...] = _blocks_from_dense(dwa_s[...], dwx_s[...])

    vec = lambda: pl.BlockSpec((1, DG), lambda i: (0, 0))
    blocks = lambda: pl.BlockSpec((DG, RGB), lambda i: (0, 0))
    prev8 = lambda: pl.BlockSpec((8, DG), lambda i: (jnp.maximum((nt - 1 - i) * (tm // 8) - 1, 0), 0))
    return pl.pallas_call(
        body, name="bwd_rglru", grid=(nt,),
        in_specs=[pl.BlockSpec((tm, DG), lambda i: (nt - 1 - i, 0)),
                  pl.BlockSpec((tm, DG), lambda i: (nt - 1 - i, 0)),
                  prev8(),
                  pl.BlockSpec((tm, DG), lambda i: (nt - 1 - i, 0)),
                  prev8(),
                  pl.BlockSpec((tm, DG), lambda i: (nt - 1 - i, 0)),
                  pl.BlockSpec((NSHARD, 4, 128), lambda i: (0, 0, 0)), vec(),
                  blocks(), vec(), blocks(), vec(), vec(), ANY_SPEC],
        out_specs=[pl.BlockSpec((tm, D), lambda i: (nt - 1 - i, 0)),
                   pl.BlockSpec((DG, 128), lambda i: (0, 0)),
                   pl.BlockSpec((8, DG), lambda i: (0, 0))],
        out_shape=[jax.ShapeDtypeStruct((T, D), BF16), jax.ShapeDtypeStruct((DG, 128), F32),
                   jax.ShapeDtypeStruct((8, DG), F32)],
        scratch_shapes=[pltpu.VMEM((tm, DG), F32), pltpu.VMEM((tm, DG), F32),
                        pltpu.VMEM((8, DG), F32), pltpu.VMEM((8, DG), F32),
                        pltpu.VMEM((DG, DG), BF16), pltpu.VMEM((DG, DG), BF16),
                        pltpu.VMEM((DG, DG), F32), pltpu.VMEM((DG, DG), F32)],
        compiler_params=_cparams(("arbitrary",), 56),
    )(pf, pb, pf, h, h, dyab, cw3, conv_b, wa_c, ba, wx_c, bx, lam, dep)


def _bwd_hgrn2(pf, pb, o, s_prev, dyab, hg_lb, hg_nw, dep):
    tm = 512
    nt = T // tm
    nc = tm // CH

    def body(q_ref, f_ref, v_ref, g_ref, o_ref, sp_ref, dy_ref, lb_ref, nw_ref, dep_ref, db_ref, sm_ref,
             dst_s, eb_s, enb_s, ekd_s, qe_s, ke_s, kd_s, vb_s, do_s, dec_s, ddec_s, p_s, dp_s,
             g_s, dsta_s, dva_s, dqe_s, dke_s, dkd_s, dlf_s):
        i = pl.program_id(0)

        @pl.when(i == 0)
        def _():
            sm_ref[...] = jnp.zeros_like(sm_ref)
            dst_s[...] = jnp.zeros_like(dst_s)

        lb, one_m_lb = _hg_lower_bound(lb_ref)
        rows64 = lax.broadcasted_iota(I32, (tm, DG), 0) & (CH - 1)
        sg, sn, f, k, q, sq, qs = _hg_recompute(
            q_ref, f_ref, lb, one_m_lb, rows64, eb_s, enb_s, ekd_s, qe_s, ke_s, kd_s, dec_s)
        vb_s[...] = v_ref[...]

        nw = nw_ref[...]
        for hd in range(NH):
            cols = slice(hd * HD, (hd + 1) * HD)
            g = g_ref[:, cols].astype(F32)
            sgg = _sigmoid(g)
            o = o_ref[:, cols]
            so = lax.rsqrt(jnp.mean(o * o, axis=-1, keepdims=True) + EPS)
            oh = o * so
            dyb = dy_ref[:, cols]
            db_ref[:, 3 * DG + hd * HD:3 * DG + (hd + 1) * HD] = (
                dyb * (oh * nw) * (sgg * (1.0 + g * (1.0 - sgg)))).astype(BF16)
            don = dyb * (g * sgg)
            sm_ref[1:2, 0:HD] += _rowsum(don * oh)
            gw = don * nw
            do_s[:, cols] = (so * (gw - oh * jnp.mean(gw * oh, axis=-1, keepdims=True))).astype(BF16)

        mask = _tri(True)
        items = [(c, hd, slice(c * CH, (c + 1) * CH), slice(hd * HD, (hd + 1) * HD))
                 for c in range(nc) for hd in range(NH)]
        for c, hd, rs, cols in items:
            p_s[c * NH + hd] = jnp.where(mask, _mm_nt(qe_s[rs, cols], ke_s[rs, cols]), 0.0).astype(BF16)
            dp_s[c * NH + hd] = jnp.where(mask, _mm_nt(do_s[rs, cols], vb_s[rs, cols]), 0.0).astype(BF16)
        for c, hd, rs, cols in items:
            n = c * NH + hd
            dva_s[rs, cols] = _mm_tn(p_s[n], do_s[rs, cols])
            dqe_s[rs, cols] = _mm(dp_s[n], ke_s[rs, cols])
            dke_s[rs, cols] = _mm_tn(dp_s[n], qe_s[rs, cols])
            g_s[n] = _mm_tn(do_s[rs, cols], qe_s[rs, cols])
        for c, hd, rs, cols in reversed(items):
            n = c * NH + hd
            dst = dst_s[hd]
            dsta_s[n] = dst
            dst_s[hd] = dst * dec_s[c:c + 1, cols] + g_s[n]
        for c, hd, rs, cols in items:
            n = c * NH + hd
            dst = dsta_s[n]
            st_prev = sp_ref[hd, c]
            dv = dva_s[rs, cols] + _mm_nt(kd_s[rs, cols], dst)
            db_ref[rs, 2 * DG + hd * HD:2 * DG + (hd + 1) * HD] = dv.astype(BF16)
            dqe_s[rs, cols] += _mm(do_s[rs, cols], st_prev)
            dkd_s[rs, cols] = _mm(vb_s[rs, cols], dst)
            ddec_s[c:c + 1, cols] = _rowsum(dst * st_prev)

        eb, enb, ekd = eb_s[...], enb_s[...], ekd_s[...]
        dqe, dke, dkd = dqe_s[...], dke_s[...], dkd_s[...]
        t_kd = dkd * (k * ekd)
        rc = _chunk_rev_cumsum(dqe * (qs * eb) - dke * (k * enb) - t_kd, rows64)
        for c in range(nc):
            rs = slice(c * CH, (c + 1) * CH)
            dbl = _rowsum(t_kd[rs]) + ddec_s[c:c + 1, :] * dec_s[c:c + 1, :]
            dlf_s[rs, :] = rc[rs] + dbl
        t = dlf_s[...] / f - (dke * enb + dkd * ekd)
        db_ref[:, DG:2 * DG] = (one_m_lb * sg * sn * t).astype(BF16)
        sm_ref[0:1, :] += _rowsum(sn * t)
        db_ref[:, 0:DG] = (dqe * eb * (sq * (1.0 + q * (1.0 - sq))) * (HD ** -0.5)).astype(BF16)

        @pl.when(i == nt - 1)
        def _():
            dsm = sm_ref[0:1, :] * (lb * one_m_lb)
            sm_ref[2:3, :] = dsm
            sm_ref[3:4, :] = -dsm

    col = lambda j: pl.BlockSpec((tm, DG), lambda i: (nt - 1 - i, j))
    big = lambda dt: pltpu.VMEM((tm, DG), dt)
    return pl.pallas_call(
        body, name="bwd_hgrn2", grid=(nt,),
        in_specs=[col(1), col(1), col(2), col(3),
                  pl.BlockSpec((tm, DG), lambda i: (nt - 1 - i, 0)),
                  pl.BlockSpec((NH, nc, HD, HD), lambda i: (0, nt - 1 - i, 0, 0)),
                  pl.BlockSpec((tm, DG), lambda i: (nt - 1 - i, 1)),
                  pl.BlockSpec((2, DG), lambda i: (0, 0)),
                  pl.BlockSpec((1, HD), lambda i: (0, 0)), ANY_SPEC],
        out_specs=[pl.BlockSpec((tm, 4 * DG), lambda i: (nt - 1 - i, 0)),
                   pl.BlockSpec((8, DG), lambda i: (0, 0))],
        out_shape=[jax.ShapeDtypeStruct((T, 4 * DG), BF16), jax.ShapeDtypeStruct((8, DG), F32)],
        scratch_shapes=[pltpu.VMEM((NH, HD, HD), F32),
                        big(F32), big(F32), big(F32),
                        big(BF16), big(BF16), big(BF16), big(BF16), big(BF16),
                        pltpu.VMEM((nc, DG), F32), pltpu.VMEM((nc, DG), F32),
                        pltpu.VMEM((nc * NH, CH, CH), BF16), pltpu.VMEM((nc * NH, CH, CH), BF16),
                        pltpu.VMEM((nc * NH, HD, HD), F32), pltpu.VMEM((nc * NH, HD, HD), F32),
                        big(F32), big(F32), big(F32), big(F32), big(F32)],
        compiler_params=_cparams(("arbitrary",), 56),
    )(pb, pf, pb, pb, o, s_prev, dyab, hg_lb, hg_nw, dep)


def _dproj_pieces(k, da_ref, db_ref):
    if k == 0:
        return [(da_ref[:, 0:SHW], 0)]
    if k == 1:
        return [(da_ref[:, SHW:D], 0), (db_ref[:, 0:DG], D - SHW)]
    if k == 2:
        return [(db_ref[:, DG:DG + SHW], 0)]
    return [(db_ref[:, DG + SHW:4 * DG], 0)]


def _bwd_inproj_dx(x, dh1, d_a, d_b, w_in_b, nw, dep):
    tm = 512
    nt = T // tm

    def body(x_ref, dh1_ref, da_ref, db_ref, w_ref, nw_ref, dep_ref, dx_ref, sm_ref):
        i = pl.program_id(0)

        @pl.when(i == 0)
        def _():
            sm_ref[...] = jnp.zeros_like(sm_ref)

        du = None
        for k in range(NSHARD):
            for val, off in _dproj_pieces(k, da_ref, db_ref):
                t = _mm_nt(val, w_ref[k, :, off:off + val.shape[1]])
                du = t if du is None else du + t
        xv = x_ref[...]
        s = lax.rsqrt(jnp.mean(xv * xv, axis=-1, keepdims=True) + EPS)
        xh = xv * s
        sm_ref[0:1, :] += _rowsum(du * xh)
        g = du * nw_ref[...]
        dx_ref[...] = dh1_ref[...] + s * (g - xh * jnp.mean(g * xh, axis=-1, keepdims=True))

    row = lambda w: pl.BlockSpec((tm, w), lambda i: (i, 0))
    return pl.pallas_call(
        body, name="bwd_inproj_dx", grid=(nt,),
        in_specs=[row(D), row(D), row(D), row(4 * DG),
                  pl.BlockSpec((NSHARD, D, SHW), lambda i: (0, 0, 0), pipeline_mode=pl.Buffered(1)),
                  pl.BlockSpec((1, D), lambda i: (0, 0)), ANY_SPEC],
        out_specs=[row(D), pl.BlockSpec((8, D), lambda i: (0, 0))],
        out_shape=[jax.ShapeDtypeStruct((T, D), F32), jax.ShapeDtypeStruct((8, D), F32)],
        compiler_params=_cparams(("arbitrary",), 56),
    )(x, dh1, d_a, d_b, w_in_b, nw, dep)


def _bwd_inproj_dw(name, ks, u_b, d_a, d_b):
    tm = 1024
    nt = T // tm
    nk = len(ks)
    H = D // 2
    step = 128
    need_a = 0 in ks or 1 in ks

    def body(*refs):
        if need_a:
            u_ref, da_ref, db_ref, dw_ref, acc, send_s, recv_s, ssem, rsem = refs
        else:
            u_ref, db_ref, dw_ref, acc, send_s, recv_s, ssem, rsem = refs
            da_ref = None
        i = pl.program_id(0)

        @pl.when(i == 0)
        def _():
            acc[...] = jnp.zeros_like(acc)

        u = u_ref[...]
        for j, k in enumerate(ks):
            for val, off in _dproj_pieces(k, da_ref, db_ref):
                acc[j, :, off:off + val.shape[1]] += _mm_tn(u, val)

        @pl.when(i == nt - 1)
        def _():
            x, y, c = lax.axis_index("x"), lax.axis_index("y"), lax.axis_index("c")
            sibling = (x, y, 1 - c)
            mine0 = pl.multiple_of(c * H, step)
            other0 = pl.multiple_of((1 - c) * H, step)
            copies = []
            for j in range(nk):
                for r0 in range(0, H, step):
                    send_s[j, r0:r0 + step, :] = acc[j, pl.ds(other0 + r0, step), :].astype(BF16)
                cp = _remote(send_s.at[j], recv_s.at[j], ssem.at[j], rsem.at[j], sibling)
                cp.start()
                copies.append(cp)
            for j in range(nk):
                copies[j].wait_recv()
                for r0 in range(0, H, step):
                    s = acc[j, pl.ds(mine0 + r0, step), :] + recv_s[j, r0:r0 + step, :].astype(F32)
                    dw_ref[j, r0:r0 + step, :] = s.astype(BF16)
            for cp in copies:
                cp.wait_send()

    row = lambda w: pl.BlockSpec((tm, w), lambda i: (i, 0))
    ins = [u_b] + ([d_a] if need_a else []) + [d_b]
    in_specs = [row(D)] + ([row(D)] if need_a else []) + [row(4 * DG)]
    return pl.pallas_call(
        body, name=name, grid=(nt,), in_specs=in_specs,
        out_specs=pl.BlockSpec((nk, H, SHW), lambda i: (0, 0, 0)),
        out_shape=jax.ShapeDtypeStruct((nk, H, SHW), BF16),
        scratch_shapes=[pltpu.VMEM((nk, D, SHW), F32), pltpu.VMEM((nk, H, SHW), BF16),
                        pltpu.VMEM((nk, H, SHW), BF16),
                        pltpu.SemaphoreType.DMA((nk,)), pltpu.SemaphoreType.DMA((nk,))],
        compiler_params=_cparams(("arbitrary",), 48),
    )(*ins)


_OUT_ORDER = ["norm_mix_w", "w_in", "conv_w", "conv_b", "rg_wa", "rg_ba", "rg_wx", "rg_bx", "rg_lambda", "hg_lb",
              "hg_norm_w", "w_out", "ple_norm_w", "w_ple_gate", "b_ple_gate", "w_ple_proj", "final_norm_w"]
_BIG = ["w_in", "w_out", "w_ple_gate", "w_ple_proj"]


def _small_view(name, a):
    if name in ("rg_wa", "rg_wx"):
        return a.reshape(DG, RGB)
    if name == "conv_w":
        return a.reshape(4, 128)
    if name == "final_norm_w":
        return a.reshape(1, D)
    return a


def _landing(rows, cols):
    return lax.empty((NDEV, rows, cols), BF16)


def kernel(x, p, norm_mix_w, w_in, conv_w, conv_b, rg_wa, rg_ba, rg_wx, rg_bx, rg_lambda, hg_lb, hg_norm_w, w_out, ple_norm_w, w_ple_gate, b_ple_gate, w_ple_proj, final_norm_w, loss_target, m_norm_mix_w, m_w_in, m_conv_w, m_conv_b, m_rg_wa, m_rg_ba, m_rg_wx, m_rg_bx, m_rg_lambda, m_hg_lb, m_hg_norm_w, m_w_out, m_ple_norm_w, m_w_ple_gate, m_b_ple_gate, m_w_ple_proj, m_final_norm_w, v_norm_mix_w, v_w_in, v_conv_w, v_conv_b, v_rg_wa, v_rg_ba, v_rg_wx, v_rg_bx, v_rg_lambda, v_hg_lb, v_hg_norm_w, v_w_out, v_ple_norm_w, v_w_ple_gate, v_b_ple_gate, v_w_ple_proj, v_final_norm_w):
    given = dict(locals())
    x2, p2, tgt = x[0], p[0, 0], loss_target[0]
    wa_c, wx_c = _small_view("rg_wa", rg_wa), _small_view("rg_wx", rg_wx)

    w_in_b, l_out, l_pg, l_pp, cw3 = _gather_w_in(w_in[0], w_out[0], w_ple_gate[0], w_ple_proj[0], conv_w[0])
    g_ssem, g_rsem, g_lands, tok = _gather_rest_start([l_out, l_pg, l_pp])

    pf, pb, u_b = _fwd_inproj(x2, norm_mix_w, w_in_b, tok)
    h, ya = _fwd_rglru(pf, pb, cw3, conv_b, wa_c, rg_ba, wx_c, rg_bx, rg_lambda)
    yb, o, s_prev = _fwd_hgrn2(pf, pb, hg_lb, hg_norm_w)
    w_out_b, w_pg_b, w_pp_b = _gather_rest_wait(g_ssem, g_rsem, g_lands, yb)
    dh1, dyab, dwo_b, dwg_b, dwp_b, sm_tail = _tail_fwd_bwd(
        x2, p2, tgt, ya, yb, w_out_b.reshape(D, D), w_pg_b.reshape(D, D), w_pp_b,
        ple_norm_w, b_ple_gate, final_norm_w.reshape(1, D))

    QH = D // NSHARD // 2
    r1 = _rs_start("rs_start_tail", [dwo_b, dwg_b, dwp_b], (0, 1, 2, 3),
                   [_landing(QH, D), _landing(QH, D), _landing(PLE // 2, PLE)])
    d_b, sm_b = _bwd_hgrn2(pf, pb, o, s_prev, dyab, hg_lb, hg_norm_w, r1[4])
    dw23 = _bwd_inproj_dw("bwd_inproj_dw23", (2, 3), u_b, None, d_b)
    r2 = _rs_start("rs_start_in23", [dw23], (2, 3), [lax.empty((NSHARD, D // 2, SHW), BF16)], chip_sums=True)
    d_a, rg_c, sm_a = _bwd_rglru(pf, pb, h, dyab, cw3, conv_b, wa_c, rg_ba, wx_c, rg_bx, rg_lambda, r2[4])
    dw01 = _bwd_inproj_dw("bwd_inproj_dw01", (0, 1), u_b, d_a, d_b)
    r3 = _rs_start("rs_start_in01", [dw01], (0, 1), r2[3], chip_sums=True)
    grad_x, sm_in = _bwd_inproj_dx(x2, dh1, d_a, d_b, w_in_b, norm_mix_w, r3[4])

    parts1, lands1 = _rs_wait("rs_wait_tail", r1[0], r1[1], r1[2], (0, 1, 2, 3), r1[3], sm_in)
    parts2, lands2 = _rs_wait("rs_wait_in23", r2[0], r2[1], r2[2], (2, 3), r3[3], sm_in, chip_sums=True)
    parts3, lands3 = _rs_wait("rs_wait_in01", r3[0], r3[1], r3[2], (0, 1), lands2, sm_in, chip_sums=True)
    g_big, vred, rgred = _final_reduce([parts3[0], parts2[0]] + parts1, lands3 + lands1,
                                       sm_in, sm_tail, sm_a, sm_b, rg_c)

    upd_big = _adam_big(g_big, [given[n][0] for n in _BIG], [given["m_" + n][0] for n in _BIG],
                        [given["v_" + n][0] for n in _BIG])
    small = _adam_small(vred, rgred,
                        {n: _small_view(n, given[n]) for n in _SMALL_ORDER},
                        {n: _small_view(n, given["m_" + n]) for n in _SMALL_ORDER},
                        {n: _small_view(n, given["v_" + n]) for n in _SMALL_ORDER})

    loss = vred[0, ROW_LOSS, 0]
    outs = [loss, grad_x[None]]
    for ki in range(4):
        for n in _OUT_ORDER:
            if n in _BIG:
                i = _BIG.index(n)
                a = g_big[i] if ki == 0 else upd_big[i][ki - 1]
                outs.append(a[None])
            else:
                outs.append(small[n][ki].reshape(given[n].shape))
    return tuple(outs)
```

```python
import jax
import jax.numpy as jnp
from jax import lax
from jax.experimental import pallas as pl
from jax.experimental.pallas import tpu as pltpu

F32 = jnp.float32
BF16 = jnp.bfloat16
I32 = jnp.int32
MESH = pl.DeviceIdType.MESH
HIGHEST = lax.Precision.HIGHEST

T = 4096
D = 1024
DG = 512
DIN = 3072
PLE = 256
NH = 4
HD = 128
CH = 64
NCHUNK = T // CH
RGB = 64
EPS = 1e-6
RG_C = 8.0
NSHARD = 4
SHW = DIN // NSHARD
NDEV = 8

ADAM_LR = 0.001
ADAM_B1 = 0.9
ADAM_B2 = 0.999
ADAM_EPS = 1e-08
ADAM_WD = 0.01
ADAM_STEP = 10

VMEM_SPEC = pl.BlockSpec(memory_space=pltpu.VMEM)
HBM_SPEC = pl.BlockSpec(memory_space=pltpu.HBM)
SEM_SPEC = pl.BlockSpec(memory_space=pltpu.SEMAPHORE)
ANY_SPEC = pl.BlockSpec(memory_space=pl.ANY)
EFFECT = pltpu.SideEffectType.DATAFLOW_SIDE_EFFECTING
MIB = 1024 * 1024

VROWS = 16
ROW_NORM_MIX, ROW_FINAL_NORM, ROW_B_PG, ROW_PLE_NORM = 0, 1, 2, 3
ROW_CB_BA, ROW_BX_LAM, ROW_CW01, ROW_CW23, ROW_HG_LB, ROW_HG_NW, ROW_LOSS = 4, 5, 6, 7, 8, 9, 10


def _mm(a, b):
    return jnp.dot(a.astype(BF16), b.astype(BF16), preferred_element_type=F32)


def _mm_nt(a, b):
    return lax.dot_general(a.astype(BF16), b.astype(BF16), (((1,), (1,)), ((), ())),
                           preferred_element_type=F32)


def _mm_tn(a, b):
    return lax.dot_general(a.astype(BF16), b.astype(BF16), (((0,), (0,)), ((), ())),
                           preferred_element_type=F32)


def _mm_exact(a, b):
    return jnp.dot(a, b, precision=HIGHEST, preferred_element_type=F32)


def _sig_pair(x):
    e = jnp.exp(-jnp.abs(x))
    big = 1.0 / (1.0 + e)
    small = e * big
    pos = x >= 0
    return jnp.where(pos, big, small), jnp.where(pos, small, big)


def _sigmoid(x):
    return 1.0 / (1.0 + jnp.exp(-x))


def _rowsum(v):
    return jnp.sum(v, axis=0, keepdims=True)


def _shift_down(cur, prev8, d, rows):
    rolled = pltpu.roll(cur, d, 0)
    head = jnp.where(rows[0:8] < d, pltpu.roll(prev8, d, 0), rolled[0:8])
    return jnp.concatenate([head, rolled[8:]], axis=0)


def _shift_up(cur, next8, d, rows):
    n = cur.shape[0]
    rolled = pltpu.roll(cur, n - d, 0)
    tail = jnp.where(rows[0:8] >= 8 - d, pltpu.roll(next8, 8 - d, 0), rolled[n - 8:n])
    return jnp.concatenate([rolled[0:n - 8], tail], axis=0)


def _roll_in_groups(v, d):
    n, w = v.shape
    return pltpu.roll(v.reshape(n // 8, 8, w), d, 1).reshape(n, w)


def _cparams(sem, vmem_mib):
    return pltpu.CompilerParams(dimension_semantics=sem, vmem_limit_bytes=vmem_mib * MIB)


def _mesh_pos():
    x, y, c = lax.axis_index("x"), lax.axis_index("y"), lax.axis_index("c")
    chips = [(1 - x, y), (x, 1 - y), (1 - x, 1 - y)]
    return x, y, c, chips


def _remote(src, dst, ssem, rsem, dev):
    return pltpu.make_async_remote_copy(src_ref=src, dst_ref=dst, send_sem=ssem, recv_sem=rsem,
                                        device_id=dev, device_id_type=MESH)


def _gather_w_in(w_in, w_out, w_pg, w_pp, conv_w):
    shapes = [w_in.shape, w_out.shape, w_pg.shape, w_pp.shape]

    def body(win_h, wout_h, wpg_h, wpp_h, cw_h, o_in, o_out, o_pg, o_pp, o_cw,
             win, wout, wpg, wpp, cw, lsem, ssem, rsem):
        x, y, c, chips = _mesh_pos()
        kme = 2 * x + y
        sibling = (x, y, 1 - c)
        fetch = [pltpu.make_async_copy(src, dst, lsem.at[i])
                 for i, (src, dst) in enumerate([(win_h, win), (cw_h, cw), (wout_h, wout), (wpg_h, wpg), (wpp_h, wpp)])]
        for cp in fetch:
            cp.start()

        def cast(src, dst):
            for r0 in range(0, src.shape[0], 128):
                dst[kme, r0:r0 + 128, :] = src[r0:r0 + 128, :].astype(BF16)

        fetch[0].wait()
        cast(win, o_in)
        fetch[1].wait()
        o_cw[kme] = cw[...]

        hrows = D // 2
        mine = pl.ds(pl.multiple_of(c * hrows, 128), hrows)
        other = pl.ds(pl.multiple_of((1 - c) * hrows, 128), hrows)

        def half(k, rows):
            return o_in.at[k, rows]

        na = (x ^ c, y ^ (1 - c), c)
        nb = (x ^ (1 - c), y ^ c, c)
        ka = 2 * na[0] + na[1]
        kb = 2 * nb[0] + nb[1]
        kd = 2 * (1 - x) + (1 - y)
        sends = [_remote(half(kme, mine), half(kme, mine), ssem.at[0], rsem.at[0], na),
                 _remote(half(kme, mine), half(kme, mine), ssem.at[1], rsem.at[1], nb)]
        for j, (px, py) in enumerate(chips):
            sends.append(_remote(o_cw.at[kme], o_cw.at[kme], ssem.at[3 + j], rsem.at[3 + j], (px, py, c)))
        for cp in sends:
            cp.start()
        for i, (src, dst) in enumerate([(wout, o_out), (wpg, o_pg), (wpp, o_pp)]):
            fetch[2 + i].wait()
            cast(src, dst)

        def passed_on(k, sem, dev):
            cp = _remote(half(k, mine), half(k, mine), ssem.at[sem], rsem.at[sem], dev)
            cp.start()
            sends.append(cp)

        _remote(half(ka, mine), half(ka, mine), ssem.at[0], rsem.at[0], na).wait_recv()
        passed_on(ka, 2, nb)
        passed_on(ka, 6, sibling)
        _remote(half(kb, mine), half(kb, mine), ssem.at[1], rsem.at[1], nb).wait_recv()
        passed_on(kb, 7, sibling)
        _remote(half(kd, mine), half(kd, mine), ssem.at[2], rsem.at[2], nb).wait_recv()
        passed_on(kd, 8, sibling)
        _remote(half(kb, other), half(kb, other), ssem.at[6], rsem.at[6], sibling).wait_recv()
        _remote(half(ka, other), half(ka, other), ssem.at[7], rsem.at[7], sibling).wait_recv()
        _remote(half(kd, other), half(kd, other), ssem.at[8], rsem.at[8], sibling).wait_recv()
        for j, (px, py) in enumerate(chips):
            kj = 2 * px + py
            _remote(o_cw.at[kj], o_cw.at[kj], ssem.at[3 + j], rsem.at[3 + j], (px, py, c)).wait_recv()
        for cp in sends:
            cp.wait_send()

    out_shape = [jax.ShapeDtypeStruct((NSHARD,) + s, BF16) for s in shapes]
    out_shape.append(jax.ShapeDtypeStruct((NSHARD,) + conv_w.shape, F32))
    return pl.pallas_call(
        body, name="gather_w_in", out_shape=out_shape,
        in_specs=[HBM_SPEC] * 5, out_specs=[VMEM_SPEC] * 5,
        scratch_shapes=[pltpu.VMEM(a.shape, F32) for a in (w_in, w_out, w_pg, w_pp, conv_w)]
        + [pltpu.SemaphoreType.DMA((5,)), pltpu.SemaphoreType.DMA((9,)), pltpu.SemaphoreType.DMA((9,))],
        compiler_params=pltpu.CompilerParams(vmem_limit_bytes=40 * MIB),
    )(*[pltpu.with_memory_space_constraint(a, pltpu.HBM) for a in (w_in, w_out, w_pg, w_pp, conv_w)])


def _gather_rest_start(lands):
    n = len(lands)

    def body(*refs):
        land_in = refs[0:n]
        ssem, rsem = refs[n], refs[n + 1]
        token = refs[2 * n + 2]
        x, y, c, chips = _mesh_pos()
        kme = 2 * x + y
        for p, land in enumerate(land_in):
            hrows = land.shape[1] // 2
            mine = pl.ds(pl.multiple_of(c * hrows, 128), hrows)
            for px, py in chips:
                for pc in range(2):
                    _remote(land.at[kme, mine], land.at[kme, mine], ssem.at[p], rsem.at[p], (px, py, pc)).start()
        token[...] = jnp.zeros_like(token)

    out_shape = ([pltpu.SemaphoreType.DMA((n,)), pltpu.SemaphoreType.DMA((n,))]
                 + [pltpu.HBM(a.shape, a.dtype) for a in lands] + [jax.ShapeDtypeStruct((8, 128), F32)])
    outs = pl.pallas_call(
        body, name="gather_rest_start", out_shape=out_shape,
        in_specs=[HBM_SPEC] * n, out_specs=[SEM_SPEC, SEM_SPEC] + [HBM_SPEC] * n + [VMEM_SPEC],
        input_output_aliases={i: 2 + i for i in range(n)},
        compiler_params=pltpu.CompilerParams(has_side_effects=EFFECT),
    )(*[pltpu.with_memory_space_constraint(a, pltpu.HBM) for a in lands])
    return outs[0], outs[1], list(outs[2:2 + n]), outs[2 + n]


def _gather_rest_wait(ssem, rsem, lands, after):
    n = len(lands)

    def body(*refs):
        land_in = refs[0:n]
        ssem_ref, rsem_ref = refs[n], refs[n + 1]
        x, y, c = lax.axis_index("x"), lax.axis_index("y"), lax.axis_index("c")
        for p, land in enumerate(land_in):
            three = land.at[pl.ds(0, 3)]
            cp = _remote(three, three, ssem_ref.at[p], rsem_ref.at[p], (x, y, c))
            cp.wait_send()
            cp.wait_recv()

    outs = pl.pallas_call(
        body, name="gather_rest_wait", out_shape=[pltpu.HBM(a.shape, a.dtype) for a in lands],
        in_specs=[HBM_SPEC] * n + [SEM_SPEC, SEM_SPEC, ANY_SPEC], out_specs=[HBM_SPEC] * n,
        input_output_aliases={i: i for i in range(n)},
        compiler_params=pltpu.CompilerParams(has_side_effects=EFFECT),
    )(*lands, ssem, rsem, after)
    return list(outs)


def _rs_start(name, parts, ks, lands, chip_sums=False):
    n = len(parts)

    def body(*refs):
        part_in, land_in = refs[0:n], refs[n:2 * n]
        ssem, rsem = refs[2 * n], refs[2 * n + 1]
        token = refs[4 * n + 2]
        x, y, c = lax.axis_index("x"), lax.axis_index("y"), lax.axis_index("c")
        kme = 2 * x + y
        me = 4 * x + 2 * y + c
        for p in range(n):
            hrows = land_in[p].shape[1]
            for i, k in enumerate(ks):
                if chip_sums:
                    @pl.when(kme != k)
                    def _():
                        _remote(part_in[p].at[i], land_in[p].at[kme], ssem.at[p], rsem.at[p], (k // 2, k % 2, c)).start()
                    continue
                for pc in range(2):
                    @pl.when(jnp.logical_or(kme != k, c != pc))
                    def _():
                        _remote(part_in[p].at[i, pl.ds(pc * hrows, hrows)], land_in[p].at[me],
                                ssem.at[p], rsem.at[p], (k // 2, k % 2, pc)).start()
        token[...] = jnp.zeros_like(token)

    arrays = list(parts) + list(lands)
    out_shape = ([pltpu.SemaphoreType.DMA((n,)), pltpu.SemaphoreType.DMA((n,))]
                 + [pltpu.HBM(a.shape, a.dtype) for a in arrays] + [jax.ShapeDtypeStruct((8, 128), F32)])
    outs = pl.pallas_call(
        body, name=name, out_shape=out_shape,
        in_specs=[HBM_SPEC] * (2 * n), out_specs=[SEM_SPEC, SEM_SPEC] + [HBM_SPEC] * (2 * n) + [VMEM_SPEC],
        input_output_aliases={i: 2 + i for i in range(2 * n)},
        compiler_params=pltpu.CompilerParams(has_side_effects=EFFECT),
    )(*[pltpu.with_memory_space_constraint(a, pltpu.HBM) for a in arrays])
    return outs[0], outs[1], list(outs[2:2 + n]), list(outs[2 + n:2 + 2 * n]), outs[2 + 2 * n]


def _rs_wait(name, ssem, rsem, parts, ks, lands, after, chip_sums=False):
    n = len(parts)

    def body(*refs):
        part_in, land_in = refs[0:n], refs[n:2 * n]
        ssem_ref, rsem_ref = refs[2 * n], refs[2 * n + 1]
        x, y, c = lax.axis_index("x"), lax.axis_index("y"), lax.axis_index("c")
        kme = 2 * x + y
        for p in range(n):
            piece = land_in[p].at[0]
            for k in ks:
                for pc in range(1 if chip_sums else 2):
                    mine = (kme == k) if chip_sums else jnp.logical_and(kme == k, c == pc)

                    @pl.when(jnp.logical_not(mine))
                    def _():
                        _remote(piece, piece, ssem_ref.at[p], rsem_ref.at[p], (x, y, c)).wait_send()
            owner = kme == ks[0]
            for k in ks[1:]:
                owner = jnp.logical_or(owner, kme == k)

            @pl.when(owner)
            def _():
                others = land_in[p].at[pl.ds(0, land_in[p].shape[0] - 1)]
                _remote(others, others, ssem_ref.at[p], rsem_ref.at[p], (x, y, c)).wait_recv()

    arrays = list(parts) + list(lands)
    outs = pl.pallas_call(
        body, name=name, out_shape=[pltpu.HBM(a.shape, a.dtype) for a in arrays],
        in_specs=[HBM_SPEC] * (2 * n) + [SEM_SPEC, SEM_SPEC, ANY_SPEC], out_specs=[HBM_SPEC] * (2 * n),
        input_output_aliases={i: i for i in range(2 * n)},
        compiler_params=pltpu.CompilerParams(has_side_effects=EFFECT),
    )(*arrays, ssem, rsem, after)
    return list(outs[0:n]), list(outs[n:2 * n])


def _reduce_exchange(parts, lands):
    shapes = [(2 * l.shape[1], l.shape[2]) for l in lands]
    step = 128

    def body(in01, in23, pout, ppg, ppp, l_in, l_out, l_pg, l_pp, g_in, g_out, g_pg, g_pp, ssem, rsem):
        x, y, c = lax.axis_index("x"), lax.axis_index("y"), lax.axis_index("c")
        kme = 2 * x + y
        me = 4 * x + 2 * y + c
        sibling = (x, y, 1 - c)
        sends = []
        for p, (land, gout) in enumerate([(l_in, g_in), (l_out, g_out), (l_pg, g_pg), (l_pp, g_pp)]):
            hrows = land.shape[1]
            mine0 = pl.multiple_of(c * hrows, step)
            for r0 in range(0, hrows, step):
                rs = pl.ds(mine0 + r0, step)
                if p == 0:
                    own = jnp.where(kme >= 2, in23[kme & 1, r0:r0 + step, :], in01[kme & 1, r0:r0 + step, :])
                    slot = kme
                else:
                    own = (pout, ppg, ppp)[p - 1][kme, rs, :]
                    slot = me
                s = jnp.zeros((step, land.shape[2]), F32)
                for j in range(land.shape[0]):
                    s = s + jnp.where(slot == j, own, land[j, r0:r0 + step, :]).astype(F32)
                gout[rs, :] = s
            mine = pl.ds(mine0, hrows)
            cp = _remote(gout.at[mine], gout.at[mine], ssem.at[p], rsem.at[p], sibling)
            cp.start()
            sends.append(cp)
        for p, (land, gout) in enumerate([(l_in, g_in), (l_out, g_out), (l_pg, g_pg), (l_pp, g_pp)]):
            hrows = land.shape[1]
            other = pl.ds(pl.multiple_of((1 - c) * hrows, step), hrows)
            _remote(gout.at[other], gout.at[other], ssem.at[p], rsem.at[p], sibling).wait_recv()
        for cp in sends:
            cp.wait_send()

    return pl.pallas_call(
        body, name="reduce_exchange", out_shape=[jax.ShapeDtypeStruct(s, F32) for s in shapes],
        in_specs=[VMEM_SPEC] * 9, out_specs=[VMEM_SPEC] * 4,
        scratch_shapes=[pltpu.SemaphoreType.DMA((4,)), pltpu.SemaphoreType.DMA((4,))],
        compiler_params=pltpu.CompilerParams(vmem_limit_bytes=48 * MIB),
    )(*parts, *lands)


def _small_allreduce(sm_in, sm_tail, sm_a, sm_b, rg_c):
    PR = DG // NDEV

    def body(in_ref, tail_ref, a_ref, b_ref, rg_ref, v_out, rg_out, vbuf, vrecv, rgrecv, ssem, rsem):
        x, y, c = lax.axis_index("x"), lax.axis_index("y"), lax.axis_index("c")
        me = 4 * x + 2 * y + c

        def pair(ref, r0, r1):
            return jnp.concatenate([ref[r0:r0 + 1, :], ref[r1:r1 + 1, :]], axis=1)

        rows = {
            ROW_NORM_MIX: in_ref[0:1, :], ROW_FINAL_NORM: tail_ref[1:2, :], ROW_B_PG: tail_ref[2:3, :],
            ROW_PLE_NORM: tail_ref[3:4, :], ROW_CB_BA: pair(a_ref, 0, 1), ROW_BX_LAM: pair(a_ref, 2, 3),
            ROW_CW01: pair(a_ref, 4, 5), ROW_CW23: pair(a_ref, 6, 7), ROW_HG_LB: pair(b_ref, 2, 3),
            ROW_HG_NW: jnp.concatenate([b_ref[1:2, :], jnp.zeros((1, DG), F32)], axis=1),
            ROW_LOSS: tail_ref[4:5, :],
        }
        vbuf[...] = jnp.zeros_like(vbuf)
        for r, row in rows.items():
            for j in range(NDEV):
                vbuf[j, r:r + 1, :] = row[:, j * 128:(j + 1) * 128]

        def peer(mask):
            px = x ^ ((mask >> 2) & 1)
            py = y ^ ((mask >> 1) & 1)
            pc = c ^ (mask & 1)
            return (px, py, pc), 4 * px + 2 * py + pc

        def rg_rows(r):
            return pl.ds(pl.multiple_of(r * PR, PR), PR)

        first = []
        for mask in range(1, NDEV):
            dev, r = peer(mask)
            i = mask - 1
            cp = _remote(vbuf.at[r], vrecv.at[i], ssem.at[i], rsem.at[i], dev)
            cp.start()
            first.append(cp)
            cp = _remote(rg_ref.at[rg_rows(r)], rgrecv.at[i], ssem.at[7 + i], rsem.at[7 + i], dev)
            cp.start()
            first.append(cp)
        sv = vbuf[me]
        sr = rg_ref[rg_rows(me), :]
        for i in range(NDEV - 1):
            first[2 * i].wait_recv()
            first[2 * i + 1].wait_recv()
            sv = sv + vrecv[i]
            sr = sr + rgrecv[i]
        v_out[me] = sv
        rg_out[rg_rows(me), :] = sr
        second = []
        for mask in range(1, NDEV):
            dev, r = peer(mask)
            i = mask - 1
            cp = _remote(v_out.at[me], v_out.at[me], ssem.at[14 + i], rsem.at[14 + i], dev)
            cp.start()
            second.append(cp)
            cp = _remote(rg_out.at[rg_rows(me)], rg_out.at[rg_rows(me)], ssem.at[21 + i], rsem.at[21 + i], dev)
            cp.start()
            second.append(cp)
        for mask in range(1, NDEV):
            dev, r = peer(mask)
            i = mask - 1
            _remote(v_out.at[r], v_out.at[r], ssem.at[14 + i], rsem.at[14 + i], dev).wait_recv()
            _remote(rg_out.at[rg_rows(r)], rg_out.at[rg_rows(r)], ssem.at[21 + i], rsem.at[21 + i], dev).wait_recv()
        for cp in first + second:
            cp.wait_send()

    return pl.pallas_call(
        body, name="small_allreduce",
        out_shape=[jax.ShapeDtypeStruct((NDEV, VROWS, 128), F32), jax.ShapeDtypeStruct((DG, 128), F32)],
        in_specs=[VMEM_SPEC] * 5, out_specs=[VMEM_SPEC] * 2,
        scratch_shapes=[pltpu.VMEM((NDEV, VROWS, 128), F32), pltpu.VMEM((NDEV - 1, VROWS, 128), F32),
                        pltpu.VMEM((NDEV - 1, PR, 128), F32),
                        pltpu.SemaphoreType.DMA((28,)), pltpu.SemaphoreType.DMA((28,))],
    )(sm_in, sm_tail, sm_a, sm_b, rg_c)


def _final_reduce(parts, lands, sm_in, sm_tail, sm_a, sm_b, rg_c):
    shapes = [(2 * l.shape[1], l.shape[2]) for l in lands]
    step = 128
    PR = DG // NDEV

    def body(in01, in23, pout, ppg, ppp, l_in, l_out, l_pg, l_pp, in_h, tail_h, a_h, b_h, rg_h,
             g_in, g_out, g_pg, g_pp, v_out, rg_out, vbuf, vrecv, rgrecv,
             lb_in, lb_out, lb_pg, lb_pp, ob_in, ob_out, ob_pg, ob_pp,
             in_ref, tail_ref, a_ref, b_ref, rg_ref, lsem, ssem_l, bsem_s, bsem_r, ssem, rsem):
        x, y, c = lax.axis_index("x"), lax.axis_index("y"), lax.axis_index("c")
        kme = 2 * x + y
        me = 4 * x + 2 * y + c
        sibling = (x, y, 1 - c)
        small_fetch = [pltpu.make_async_copy(src, dst, ssem_l.at[i]) for i, (src, dst) in enumerate(
            [(in_h, in_ref), (tail_h, tail_ref), (a_h, a_ref), (b_h, b_ref), (rg_h, rg_ref)])]
        for cp in small_fetch:
            cp.start()

        lands_hbm = [l_in, l_out, l_pg, l_pp]
        land_bufs = [lb_in, lb_out, lb_pg, lb_pp]
        own_bufs = [ob_in, ob_out, ob_pg, ob_pp]
        fetches = []
        for p in range(4):
            cp = pltpu.make_async_copy(lands_hbm[p], land_bufs[p], lsem.at[p])
            cp.start()
            fetches.append(cp)

        @pl.when(kme >= 2)
        def _():
            pltpu.make_async_copy(in23.at[kme & 1], ob_in, lsem.at[4]).start()

        @pl.when(kme < 2)
        def _():
            pltpu.make_async_copy(in01.at[kme & 1], ob_in, lsem.at[4]).start()

        own_fetches = [pltpu.make_async_copy(in01.at[0], ob_in, lsem.at[4])]
        for p, part in enumerate([pout, ppg, ppp]):
            hrows = own_bufs[p + 1].shape[0]
            cp = pltpu.make_async_copy(part.at[kme, pl.ds(pl.multiple_of(c * hrows, step), hrows)],
                                       own_bufs[p + 1], lsem.at[5 + p])
            cp.start()
            own_fetches.append(cp)

        def pair(ref, r0, r1):
            return jnp.concatenate([ref[r0:r0 + 1, :], ref[r1:r1 + 1, :]], axis=1)

        for cp in small_fetch:
            cp.wait()
        rows = {
            ROW_NORM_MIX: in_ref[0:1, :], ROW_FINAL_NORM: tail_ref[1:2, :], ROW_B_PG: tail_ref[2:3, :],
            ROW_PLE_NORM: tail_ref[3:4, :], ROW_CB_BA: pair(a_ref, 0, 1), ROW_BX_LAM: pair(a_ref, 2, 3),
            ROW_CW01: pair(a_ref, 4, 5), ROW_CW23: pair(a_ref, 6, 7), ROW_HG_LB: pair(b_ref, 2, 3),
            ROW_HG_NW: jnp.concatenate([b_ref[1:2, :], jnp.zeros((1, DG), F32)], axis=1),
            ROW_LOSS: tail_ref[4:5, :],
        }
        vbuf[...] = jnp.zeros_like(vbuf)
        for r, row in rows.items():
            for j in range(NDEV):
                vbuf[j, r:r + 1, :] = row[:, j * 128:(j + 1) * 128]

        def peer(mask):
            px = x ^ ((mask >> 2) & 1)
            py = y ^ ((mask >> 1) & 1)
            pc = c ^ (mask & 1)
            return (px, py, pc), 4 * px + 2 * py + pc

        def rg_rows(r):
            return pl.ds(pl.multiple_of(r * PR, PR), PR)

        first = []
        for mask in range(1, NDEV):
            dev, r = peer(mask)
            i = mask - 1
            cp = _remote(vbuf.at[r], vrecv.at[i], ssem.at[i], rsem.at[i], dev)
            cp.start()
            first.append(cp)
            cp = _remote(rg_ref.at[rg_rows(r)], rgrecv.at[i], ssem.at[7 + i], rsem.at[7 + i], dev)
            cp.start()
            first.append(cp)

        big = [(lb_in, g_in), (lb_out, g_out), (lb_pg, g_pg), (lb_pp, g_pp)]
        swaps = []
        for p, (land, gout) in enumerate(big):
            fetches[p].wait()
            own_fetches[p].wait()
            hrows = land.shape[1]
            mine0 = pl.multiple_of(c * hrows, step)
            slot = kme if p == 0 else me
            for r0 in range(0, hrows, step):
                rs = pl.ds(mine0 + r0, step)
                own = own_bufs[p][r0:r0 + step, :]
                s = jnp.zeros((step, land.shape[2]), F32)
                for j in range(land.shape[0]):
                    s = s + jnp.where(slot == j, own, land[j, r0:r0 + step, :]).astype(F32)
                gout[rs, :] = s
            mine = pl.ds(mine0, hrows)
            cp = _remote(gout.at[mine], gout.at[mine], bsem_s.at[p], bsem_r.at[p], sibling)
            cp.start()
            swaps.append(cp)

        sv = vbuf[me]
        sr = rg_ref[rg_rows(me), :]
        for i in range(NDEV - 1):
            first[2 * i].wait_recv()
            first[2 * i + 1].wait_recv()
            sv = sv + vrecv[i]
            sr = sr + rgrecv[i]
        v_out[me] = sv
        rg_out[rg_rows(me), :] = sr
        second = []
        for mask in range(1, NDEV):
            dev, r = peer(mask)
            i = mask - 1
            cp = _remote(v_out.at[me], v_out.at[me], ssem.at[14 + i], rsem.at[14 + i], dev)
            cp.start()
            second.append(cp)
            cp = _remote(rg_out.at[rg_rows(me)], rg_out.at[rg_rows(me)], ssem.at[21 + i], rsem.at[21 + i], dev)
            cp.start()
            second.append(cp)
        for p, (land, gout) in enumerate(big):
            hrows = land.shape[1]
            other = pl.ds(pl.multiple_of((1 - c) * hrows, step), hrows)
            _remote(gout.at[other], gout.at[other], bsem_s.at[p], bsem_r.at[p], sibling).wait_recv()
        for mask in range(1, NDEV):
            dev, r = peer(mask)
            i = mask - 1
            _remote(v_out.at[r], v_out.at[r], ssem.at[14 + i], rsem.at[14 + i], dev).wait_recv()
            _remote(rg_out.at[rg_rows(r)], rg_out.at[rg_rows(r)], ssem.at[21 + i], rsem.at[21 + i], dev).wait_recv()
        for cp in first + swaps + second:
            cp.wait_send()

    out_shape = [jax.ShapeDtypeStruct(s, F32) for s in shapes]
    out_shape += [jax.ShapeDtypeStruct((NDEV, VROWS, 128), F32), jax.ShapeDtypeStruct((DG, 128), F32)]
    outs = pl.pallas_call(
        body, name="final_reduce", out_shape=out_shape,
        in_specs=[HBM_SPEC] * 14, out_specs=[VMEM_SPEC] * 6,
        scratch_shapes=[pltpu.VMEM((NDEV, VROWS, 128), F32), pltpu.VMEM((NDEV - 1, VROWS, 128), F32),
                        pltpu.VMEM((NDEV - 1, PR, 128), F32)]
        + [pltpu.VMEM(l.shape, BF16) for l in lands]
        + [pltpu.VMEM(l.shape[1:], BF16) for l in lands]
        + [pltpu.VMEM(a.shape, F32) for a in (sm_in, sm_tail, sm_a, sm_b, rg_c)]
        + [pltpu.SemaphoreType.DMA((8,)), pltpu.SemaphoreType.DMA((5,)),
                        pltpu.SemaphoreType.DMA((4,)), pltpu.SemaphoreType.DMA((4,)),
                        pltpu.SemaphoreType.DMA((28,)), pltpu.SemaphoreType.DMA((28,))],
        compiler_params=pltpu.CompilerParams(vmem_limit_bytes=48 * MIB),
    )(*[pltpu.with_memory_space_constraint(a, pltpu.HBM)
        for a in (*parts, *lands, sm_in, sm_tail, sm_a, sm_b, rg_c)])
    return list(outs[0:4]), outs[4], outs[5]


def _adam_rows(w, g, m, v):
    m2 = ADAM_B1 * m + (1.0 - ADAM_B1) * g
    v2 = ADAM_B2 * v + (1.0 - ADAM_B2) * (g * g)
    m_hat = m2 / (1.0 - ADAM_B1 ** ADAM_STEP)
    v_hat = v2 / (1.0 - ADAM_B2 ** ADAM_STEP)
    delta = -ADAM_LR * (m_hat / (jnp.sqrt(v_hat) + ADAM_EPS) + ADAM_WD * w)
    return delta, m2, v2


def _adam_big(gs, ws, ms, vs):
    n = len(gs)
    steps = 8

    def body(*refs):
        ins, outs = refs[:4 * n], refs[4 * n:]
        for i in range(n):
            g, w, m, v = (r[...] for r in ins[4 * i:4 * i + 4])
            d, m2, v2 = _adam_rows(w, g, m, v)
            outs[3 * i][...] = d
            outs[3 * i + 1][...] = m2
            outs[3 * i + 2][...] = v2

    in_specs, out_specs, out_shape, args = [], [], [], []
    for g, w, m, v in zip(gs, ws, ms, vs):
        r, c = w.shape
        spec = lambda: pl.BlockSpec((r // steps, c), lambda i: (i, 0))
        in_specs += [spec() for _ in range(4)]
        out_specs += [spec() for _ in range(3)]
        out_shape += [jax.ShapeDtypeStruct((r, c), F32)] * 3
        args += [pltpu.with_memory_space_constraint(a, pltpu.HBM) for a in (g, w, m, v)]
    outs = pl.pallas_call(
        body, name="adam_big", grid=(steps,), in_specs=in_specs, out_specs=out_specs, out_shape=out_shape,
        compiler_params=_cparams(("parallel",), 32),
    )(*args)
    return [tuple(outs[3 * i:3 * i + 3]) for i in range(n)]


_VEC_PARAMS = [
    ("norm_mix_w", ROW_NORM_MIX, 0, D), ("final_norm_w", ROW_FINAL_NORM, 0, D),
    ("b_ple_gate", ROW_B_PG, 0, D), ("ple_norm_w", ROW_PLE_NORM, 0, D),
    ("conv_b", ROW_CB_BA, 0, DG), ("rg_ba", ROW_CB_BA, DG, DG),
    ("rg_bx", ROW_BX_LAM, 0, DG), ("rg_lambda", ROW_BX_LAM, DG, DG),
    ("hg_norm_w", ROW_HG_NW, 0, HD),
]
_SMALL_ORDER = [n for n, _, _, _ in _VEC_PARAMS] + ["hg_lb", "conv_w", "rg_wa", "rg_wx"]


def _adam_small(vred, rgred, ws, ms, vs):
    names = _SMALL_ORDER
    n = len(names)

    def body(vred_ref, rg_ref, *refs):
        w_refs = dict(zip(names, refs[0:n]))
        m_refs = dict(zip(names, refs[n:2 * n]))
        v_refs = dict(zip(names, refs[2 * n:3 * n]))
        outs = refs[3 * n:]
        o_refs = {nm: outs[4 * i:4 * i + 4] for i, nm in enumerate(names)}
        kme = 2 * lax.axis_index("x") + lax.axis_index("y")

        def update(nm, g, idx):
            d, m2, v2 = _adam_rows(w_refs[nm][idx], g, m_refs[nm][idx], v_refs[nm][idx])
            og, od, om, ov = o_refs[nm]
            og[idx] = g
            od[idx] = d
            om[idx] = m2
            ov[idx] = v2

        def packed(row, lane0, width):
            return jnp.concatenate([vred_ref[j, row:row + 1, :] for j in range(lane0 // 128, (lane0 + width) // 128)],
                                   axis=1)

        everything = (slice(None), slice(None))
        for nm, row, lane0, width in _VEC_PARAMS:
            update(nm, packed(row, lane0, width), everything)
        for r in range(2):
            update("hg_lb", packed(ROW_HG_LB, r * DG, DG), (slice(r, r + 1), slice(None)))
        for j in range(4):
            g = vred_ref[(j % 2) * 4 + kme, ROW_CW01 + j // 2:ROW_CW01 + j // 2 + 1, :]
            update("conv_w", g, (slice(j, j + 1), slice(None)))
        for r0 in range(0, DG, 128):
            rs = (slice(r0, r0 + 128), slice(None))
            both = rg_ref[r0:r0 + 128, :]
            update("rg_wa", both[:, 0:RGB], rs)
            update("rg_wx", pltpu.roll(both, RGB, 1)[:, 0:RGB], rs)

    args = [vred, rgred] + [d[nm] for d in (ws, ms, vs) for nm in names]
    out_shape = []
    for nm in names:
        out_shape += [jax.ShapeDtypeStruct(ws[nm].shape, F32)] * 4
    whole = lambda s: pl.BlockSpec(s.shape, lambda i, nd=len(s.shape): (0,) * nd)
    outs = pl.pallas_call(
        body, name="adam_small", out_shape=out_shape, grid=(1,),
        in_specs=[whole(a) for a in args], out_specs=[whole(s) for s in out_shape],
    )(*args)
    return {nm: tuple(outs[4 * i:4 * i + 4]) for i, nm in enumerate(names)}


def _fwd_inproj(x, nw, w_in_b, dep):
    tm = 512

    def body(x_ref, nw_ref, w_ref, dep_ref, pf_ref, pb_ref, u_ref):
        xv = x_ref[...]
        s = lax.rsqrt(jnp.mean(xv * xv, axis=-1, keepdims=True) + EPS)
        u = (xv * s * nw_ref[...]).astype(BF16)
        u_ref[...] = u
        r = [jnp.dot(u, w_ref[k], preferred_element_type=F32) for k in range(NSHARD)]
        h = DG // 2
        pf_ref[:, 0:DG] = r[0][:, 0:DG]
        pb_ref[:, 0:h] = r[0][:, DG:SHW].astype(BF16)
        pb_ref[:, h:DG] = r[1][:, 0:h].astype(BF16)
        pb_ref[:, DG:2 * DG] = r[1][:, h:SHW].astype(BF16)
        pf_ref[:, DG:2 * DG] = r[2][:, 0:DG]
        pb_ref[:, 2 * DG:2 * DG + h] = r[2][:, DG:SHW].astype(BF16)
        pb_ref[:, 2 * DG + h:3 * DG] = r[3][:, 0:h].astype(BF16)
        pb_ref[:, 3 * DG:4 * DG] = r[3][:, h:SHW].astype(BF16)

    return pl.pallas_call(
        body, name="fwd_inproj", grid=(T // tm,),
        in_specs=[pl.BlockSpec((tm, D), lambda i: (i, 0)),
                  pl.BlockSpec((1, D), lambda i: (0, 0)),
                  pl.BlockSpec((NSHARD, D, SHW), lambda i: (0, 0, 0), pipeline_mode=pl.Buffered(1)), ANY_SPEC],
        out_specs=[pl.BlockSpec((tm, 2 * DG), lambda i: (i, 0)),
                   pl.BlockSpec((tm, 4 * DG), lambda i: (i, 0)),
                   pl.BlockSpec((tm, D), lambda i: (i, 0))],
        out_shape=[jax.ShapeDtypeStruct((T, 2 * DG), F32), jax.ShapeDtypeStruct((T, 4 * DG), BF16),
                   jax.ShapeDtypeStruct((T, D), BF16)],
        compiler_params=_cparams(("parallel",), 48),
    )(x, nw, w_in_b, dep)


def _conv_rows(cw_ref):
    return [jnp.concatenate([cw_ref[k, j:j + 1, :] for k in range(NSHARD)], axis=1) for j in range(4)]


def _rg_conv(xa, prev8, cw, cb, rows):
    taps = [_shift_down(xa, prev8, 3, rows), _shift_down(xa, prev8, 2, rows),
            _shift_down(xa, prev8, 1, rows), xa]
    xc = cb
    for j in range(4):
        xc = xc + taps[j] * cw[j]
    return xc, taps


def _block_mask():
    r = lax.broadcasted_iota(I32, (DG, DG), 0)
    c = lax.broadcasted_iota(I32, (DG, DG), 1)
    return (r >> 6) == (c >> 6)


def _dense_from_blocks(wc):
    j = lax.broadcasted_iota(I32, (RGB, DG), 0)
    c = lax.broadcasted_iota(I32, (RGB, DG), 1)
    spread = _mm_exact(wc, ((c & (RGB - 1)) == j).astype(F32))
    return jnp.where(_block_mask(), spread, 0.0)


def _blocks_from_dense(da, dx):
    c = lax.broadcasted_iota(I32, (DG, 128), 0)
    j = lax.broadcasted_iota(I32, (DG, 128), 1)
    hit = (c & (RGB - 1)) == (j & (RGB - 1))
    mask = _block_mask()
    return (_mm_exact(jnp.where(mask, da, 0.0), (hit & (j < RGB)).astype(F32))
            + _mm_exact(jnp.where(mask, dx, 0.0), (hit & (j >= RGB)).astype(F32)))


def _rg_gates(xc, wa, ba, wx, bx, sp, first_row):
    r = _sigmoid(_mm(xc, wa) + ba)
    i = _sigmoid(_mm(xc, wx) + bx)
    log_a = (-RG_C) * r * sp
    a = jnp.exp(log_a)
    a2 = a * a
    one_m_a2 = -jnp.tanh(log_a) * (a2 + 1.0)
    mult = jnp.where(first_row, 1.0, jnp.sqrt(one_m_a2))
    return r, i, a, a2, mult


def _softplus(z):
    return jnp.maximum(z, 0.0) + jnp.log1p(jnp.exp(-jnp.abs(z)))


def _fwd_rglru(pf, pb, cw3, conv_b, wa_c, ba, wx_c, bx, lam):
    tm = 512
    ng = tm // 8

    def body(xa_ref, ga_ref, cw_ref, cb_ref, wa_ref, ba_ref, wx_ref, bx_ref, lam_ref,
             h_ref, ya_ref, a_s, u_s, tail_s, hc_s, wa_s, wx_s):
        i = pl.program_id(0)

        @pl.when(i == 0)
        def _():
            tail_s[...] = jnp.zeros_like(tail_s)
            hc_s[...] = jnp.zeros_like(hc_s)
            wa_s[...] = _dense_from_blocks(wa_ref[...]).astype(BF16)
            wx_s[...] = _dense_from_blocks(wx_ref[...]).astype(BF16)

        rows = lax.broadcasted_iota(I32, (tm, DG), 0)
        xa = xa_ref[...]
        xc, _ = _rg_conv(xa, tail_s[...], _conv_rows(cw_ref), cb_ref[...], rows)
        tail_s[...] = xa[tm - 8:tm, :]
        sp = _softplus(-lam_ref[...])
        rp = _mm(xc, wa_s[...]) + ba_ref[...]
        ip = _mm(xc, wx_s[...]) + bx_ref[...]
        rb = 64
        rows_b = lax.broadcasted_iota(I32, (rb, DG), 0)
        rows8 = rows_b & 7
        carry = hc_s[0:1, :]
        for b0 in range(0, tm, rb):
            sl = slice(b0, b0 + rb)
            r = _sigmoid(rp[sl])
            ig = _sigmoid(ip[sl])
            log_a = (-RG_C) * r * sp
            av = jnp.exp(log_a)
            mult = jnp.sqrt(-jnp.tanh(log_a) * (av * av + 1.0))
            if b0 == 0:
                mult = jnp.where((rows_b + i * tm) == 0, 1.0, mult)
            uv = mult * (ig * xc[sl])
            for d in (1, 2, 4):
                keep = rows8 >= d
                uv = uv + av * jnp.where(keep, _roll_in_groups(uv, d), 0.0)
                av = av * jnp.where(keep, _roll_in_groups(av, d), 1.0)
            ga = ga_ref[sl, :].astype(F32)
            gate = ga * _sigmoid(ga)
            hs = []
            for g in range(rb // 8):
                gs = slice(g * 8, (g + 1) * 8)
                hv = uv[gs] + av[gs] * carry
                carry = hv[7:8, :]
                hs.append(hv)
            hb = jnp.concatenate(hs, axis=0)
            h_ref[sl, :] = hb
            ya_ref[sl, :] = (hb * gate).astype(BF16)
        hc_s[0:1, :] = carry

    vec = lambda: pl.BlockSpec((1, DG), lambda i: (0, 0))
    blocks = lambda: pl.BlockSpec((DG, RGB), lambda i: (0, 0))
    return pl.pallas_call(
        body, name="fwd_rglru", grid=(T // tm,),
        in_specs=[pl.BlockSpec((tm, DG), lambda i: (i, 0)),
                  pl.BlockSpec((tm, DG), lambda i: (i, 0)),
                  pl.BlockSpec((NSHARD, 4, 128), lambda i: (0, 0, 0)), vec(),
                  blocks(), vec(), blocks(), vec(), vec()],
        out_specs=[pl.BlockSpec((tm, DG), lambda i: (i, 0)),
                   pl.BlockSpec((tm, DG), lambda i: (i, 0))],
        out_shape=[jax.ShapeDtypeStruct((T, DG), F32), pltpu.HBM((T, DG), BF16)],
        scratch_shapes=[pltpu.VMEM((tm, DG), F32), pltpu.VMEM((tm, DG), F32),
                        pltpu.VMEM((8, DG), F32), pltpu.VMEM((8, DG), F32),
                        pltpu.VMEM((DG, DG), BF16), pltpu.VMEM((DG, DG), BF16)],
        compiler_params=_cparams(("arbitrary",), 48),
    )(pf, pb, cw3, conv_b, wa_c, ba, wx_c, bx, lam)


def _hg_lower_bound(lb_ref):
    return _sig_pair(lb_ref[0:1, :] - lb_ref[1:2, :])


def _hg_gates(fz, lb, one_m_lb):
    sg, sn = _sig_pair(fz)
    f = lb + one_m_lb * sg
    return sg, sn, f, jnp.log(f), one_m_lb * sn


def _tri(lower):
    r = lax.broadcasted_iota(I32, (CH, CH), 0)
    c = lax.broadcasted_iota(I32, (CH, CH), 1)
    return (r >= c) if lower else (r <= c)


def _chunk_cumsum(v, rows64):
    for d in (1, 2, 4, 8, 16, 32):
        v = v + jnp.where(rows64 >= d, pltpu.roll(v, d, 0), 0.0)
    return v


def _chunk_rev_cumsum(v, rows64):
    n = v.shape[0]
    for d in (1, 2, 4, 8, 16, 32):
        v = v + jnp.where(rows64 < CH - d, pltpu.roll(v, n - d, 0), 0.0)
    return v


def _hg_recompute(q_ref, f_ref, lb, one_m_lb, rows64, eb_s, enb_s, ekd_s, qe_s, ke_s, kd_s, dec_s):
    nc = q_ref.shape[0] // CH
    sg, sn, f, logf, k = _hg_gates(f_ref[...], lb, one_m_lb)
    q = q_ref[...].astype(F32)
    sq = _sigmoid(q)
    qs = q * sq * (HD ** -0.5)
    b = _chunk_cumsum(logf, rows64)
    for c in range(nc):
        rs = slice(c * CH, (c + 1) * CH)
        b_c = b[rs]
        bl = b_c[CH - 1:CH, :]
        eb, enb, ekd = jnp.exp(b_c), jnp.exp(-b_c), jnp.exp(bl - b_c)
        if eb_s is not None:
            eb_s[rs, :] = eb
            enb_s[rs, :] = enb
            ekd_s[rs, :] = ekd
        qe_s[rs, :] = (qs[rs] * eb).astype(BF16)
        ke_s[rs, :] = (k[rs] * enb).astype(BF16)
        kd_s[rs, :] = (k[rs] * ekd).astype(BF16)
        dec_s[c:c + 1, :] = jnp.exp(bl)
    return sg, sn, f, k, q, sq, qs


def _fwd_hgrn2(pf, pb, hg_lb, hg_nw):
    tm = 512
    nc = tm // CH

    def body(q_ref, f_ref, v_ref, g_ref, lb_ref, nw_ref, yb_ref, o_ref, sp_ref,
             st_s, qe_s, ke_s, kd_s, vb_s, dec_s, p_s, ds_s):
        i = pl.program_id(0)

        @pl.when(i == 0)
        def _():
            st_s[...] = jnp.zeros_like(st_s)

        lb, one_m_lb = _hg_lower_bound(lb_ref)
        rows64 = lax.broadcasted_iota(I32, (tm, DG), 0) & (CH - 1)
        _hg_recompute(q_ref, f_ref, lb, one_m_lb, rows64, None, None, None, qe_s, ke_s, kd_s, dec_s)
        vb_s[...] = v_ref[...]
        mask = _tri(True)
        items = [(c, hd, slice(c * CH, (c + 1) * CH), slice(hd * HD, (hd + 1) * HD))
                 for c in range(nc) for hd in range(NH)]
        for c, hd, rs, cols in items:
            p_s[c * NH + hd] = jnp.where(mask, _mm_nt(qe_s[rs, cols], ke_s[rs, cols]), 0.0).astype(BF16)
            ds_s[c * NH + hd] = _mm_tn(vb_s[rs, cols], kd_s[rs, cols])
        for c, hd, rs, cols in items:
            st = st_s[hd]
            sp_ref[hd, c] = st
            st_s[hd] = st * dec_s[c:c + 1, cols] + ds_s[c * NH + hd]
        for c, hd, rs, cols in items:
            o_ref[rs, cols] = _mm(p_s[c * NH + hd], vb_s[rs, cols]) + _mm_nt(qe_s[rs, cols], sp_ref[hd, c])
        nw = nw_ref[...]
        for hd in range(NH):
            cols = slice(hd * HD, (hd + 1) * HD)
            o = o_ref[:, cols]
            so = lax.rsqrt(jnp.mean(o * o, axis=-1, keepdims=True) + EPS)
            g = g_ref[:, cols].astype(F32)
            sg = _sigmoid(g)
            yb_ref[:, cols] = (o * so * nw * (g * sg)).astype(BF16)

    col = lambda j: pl.BlockSpec((tm, DG), lambda i: (i, j))
    return pl.pallas_call(
        body, name="fwd_hgrn2", grid=(T // tm,),
        in_specs=[col(1), col(1), col(2), col(3),
                  pl.BlockSpec((2, DG), lambda i: (0, 0)),
                  pl.BlockSpec((1, HD), lambda i: (0, 0))],
        out_specs=[pl.BlockSpec((tm, DG), lambda i: (i, 0)),
                   pl.BlockSpec((tm, DG), lambda i: (i, 0)),
                   pl.BlockSpec((NH, nc, HD, HD), lambda i: (0, i, 0, 0))],
        out_shape=[pltpu.HBM((T, DG), BF16), jax.ShapeDtypeStruct((T, DG), F32),
                   jax.ShapeDtypeStruct((NH, NCHUNK, HD, HD), F32)],
        scratch_shapes=[pltpu.VMEM((NH, HD, HD), F32),
                        pltpu.VMEM((tm, DG), BF16), pltpu.VMEM((tm, DG), BF16), pltpu.VMEM((tm, DG), BF16),
                        pltpu.VMEM((tm, DG), BF16), pltpu.VMEM((nc, DG), F32),
                        pltpu.VMEM((nc * NH, CH, CH), BF16), pltpu.VMEM((nc * NH, HD, HD), F32)],
        compiler_params=_cparams(("arbitrary",), 48),
    )(pb, pf, pb, pb, hg_lb, hg_nw)


def _tail_fwd_bwd(x, p, tgt, ya, yb, w_out_b, w_pg_b, w_pp_b, ple_nw, b_pg, fnw):
    tm = 512
    nt = T // tm
    QR = D // NSHARD

    def body(x_ref, p_ref, t_ref, ya_ref, yb_ref, wo_ref, wg_ref, wp_ref, pw_ref, b_ref, fw_ref,
             dh1_ref, dyab_ref, dwo_ref, dwg_ref, dwp_ref, sm_ref, dwo_s, dwg_s, dwp_s):
        i = pl.program_id(0)

        @pl.when(i == 0)
        def _():
            dwo_s[...] = jnp.zeros_like(dwo_s)
            dwg_s[...] = jnp.zeros_like(dwg_s)
            dwp_s[...] = jnp.zeros_like(dwp_s)
            sm_ref[...] = jnp.zeros_like(sm_ref)

        ya = ya_ref[...]
        yb = yb_ref[...]
        pv = p_ref[...].astype(BF16)
        pw = pw_ref[...]
        fw = fw_ref[...]
        h1 = x_ref[...] + _mm(ya, wo_ref[0:DG, :]) + _mm(yb, wo_ref[DG:D, :])
        s2 = lax.rsqrt(jnp.mean(h1 * h1, axis=-1, keepdims=True) + EPS)
        n2h = h1 * s2
        n2 = (n2h * pw).astype(BF16)
        z = _mm(n2, wg_ref[...]) + b_ref[...]
        gate = _sigmoid(z)
        pp = jnp.concatenate([_mm(pv, wp_ref[k]) for k in range(NSHARD)], axis=1)
        h2 = h1 + gate * pp
        s3 = lax.rsqrt(jnp.mean(h2 * h2, axis=-1, keepdims=True) + EPS)
        hn = h2 * s3
        err = hn * fw - t_ref[...]
        sm_ref[0:1, :] += _rowsum(err * err)
        dy = err * (1.0 / D)
        sm_ref[1:2, :] += _rowsum(dy * hn)
        g3 = dy * fw
        dh2 = s3 * (g3 - hn * jnp.mean(g3 * hn, axis=-1, keepdims=True))
        dpp = (dh2 * gate).astype(BF16)
        dz = dh2 * pp * gate * (1.0 - gate)
        sm_ref[2:3, :] += _rowsum(dz)
        dzb = dz.astype(BF16)
        dwg_s[...] += _mm_tn(n2, dzb)
        dn2 = _mm_nt(dzb, wg_ref[...])
        for k in range(NSHARD):
            dwp_s[k] += _mm_tn(pv, dpp[:, k * PLE:(k + 1) * PLE])
        sm_ref[3:4, :] += _rowsum(dn2 * n2h)
        g2 = dn2 * pw
        dh1 = dh2 + s2 * (g2 - n2h * jnp.mean(g2 * n2h, axis=-1, keepdims=True))
        dh1_ref[...] = dh1
        dh1b = dh1.astype(BF16)
        dyab_ref[...] = _mm_nt(dh1b, wo_ref[...])
        dwo_s[0:DG, :] += _mm_tn(ya, dh1b)
        dwo_s[DG:D, :] += _mm_tn(yb, dh1b)

        @pl.when(i == nt - 1)
        def _():
            total = jnp.sum(sm_ref[0:1, :], axis=-1, keepdims=True) * (0.5 / D)
            sm_ref[4:5, :] = jnp.broadcast_to(total, (1, D))
            for k in range(NSHARD):
                dwo_ref[k] = dwo_s[k * QR:(k + 1) * QR, :].astype(BF16)
                dwg_ref[k] = dwg_s[k * QR:(k + 1) * QR, :].astype(BF16)
                dwp_ref[k] = dwp_s[k].astype(BF16)

    row = lambda w: pl.BlockSpec((tm, w), lambda i: (i, 0))
    const2 = lambda s: pl.BlockSpec(s, lambda i: (0, 0), pipeline_mode=pl.Buffered(1))
    const3 = lambda s: pl.BlockSpec(s, lambda i: (0, 0, 0), pipeline_mode=pl.Buffered(1))
    return pl.pallas_call(
        body, name="tail_fwd_bwd", grid=(nt,),
        in_specs=[row(D), row(PLE), row(D), row(DG), row(DG),
                  const2((D, D)), const2((D, D)), const3((NSHARD, PLE, PLE)),
                  const2((1, D)), const2((1, D)), const2((1, D))],
        out_specs=[row(D), row(D), const3((NSHARD, QR, D)), const3((NSHARD, QR, D)),
                   const3((NSHARD, PLE, PLE)), const2((8, D))],
        out_shape=[jax.ShapeDtypeStruct((T, D), F32), jax.ShapeDtypeStruct((T, D), F32),
                   jax.ShapeDtypeStruct((NSHARD, QR, D), BF16), jax.ShapeDtypeStruct((NSHARD, QR, D), BF16),
                   jax.ShapeDtypeStruct((NSHARD, PLE, PLE), BF16), jax.ShapeDtypeStruct((8, D), F32)],
        scratch_shapes=[pltpu.VMEM((D, D), F32), pltpu.VMEM((D, D), F32), pltpu.VMEM((NSHARD, PLE, PLE), F32)],
        compiler_params=_cparams(("arbitrary",), 62),
    )(x, p, tgt, pltpu.with_memory_space_constraint(ya, pltpu.HBM), pltpu.with_memory_space_constraint(yb, pltpu.HBM),
      w_out_b, w_pg_b, w_pp_b, ple_nw, b_pg, fnw)


def _bwd_rglru(pf, pb, h, dyab, cw3, conv_b, wa_c, ba, wx_c, bx, lam, dep):
    tm = 512
    nt = T // tm
    ng = tm // 8

    def body(xa_ref, ga_ref, xp_ref, h_ref, hp_ref, dya_ref, cw_ref, cb_ref, wa_ref, ba_ref, wx_ref, bx_ref,
             lam_ref, dep_ref, da_ref, rg_ref, sm_ref, a_s, g_s, cg_s, nxt_s, wa_s, wx_s, dwa_s, dwx_s):
        i = pl.program_id(0)
        tile = nt - 1 - i

        @pl.when(i == 0)
        def _():
            dwa_s[...] = jnp.zeros_like(dwa_s)
            dwx_s[...] = jnp.zeros_like(dwx_s)
            sm_ref[...] = jnp.zeros_like(sm_ref)
            cg_s[...] = jnp.zeros_like(cg_s)
            nxt_s[...] = jnp.zeros_like(nxt_s)
            wa_s[...] = _dense_from_blocks(wa_ref[...]).astype(BF16)
            wx_s[...] = _dense_from_blocks(wx_ref[...]).astype(BF16)

        rows = lax.broadcasted_iota(I32, (tm, DG), 0)
        has_prev = tile > 0
        xa = xa_ref[...]
        xprev = jnp.where(has_prev, xp_ref[...], 0.0)
        cw = _conv_rows(cw_ref)
        xc, taps = _rg_conv(xa, xprev, cw, cb_ref[...], rows)
        lam_v = lam_ref[...]
        sp = _softplus(-lam_v)
        first_row = (rows + tile * tm) == 0
        r, ig, a, a2, mult = _rg_gates(xc, wa_s[...], ba_ref[...], wx_s[...], bx_ref[...], sp, first_row)
        hv = h_ref[...]
        hprev = jnp.where(has_prev, hp_ref[...], 0.0)
        h_m1 = _shift_down(hv, hprev, 1, rows)
        ga = ga_ref[...].astype(F32)
        sg = _sigmoid(ga)
        dya = dya_ref[...]
        dga = dya * hv * (sg * (1.0 + ga * (1.0 - sg)))

        av = jnp.where(rows == tm - 1, 1.0, pltpu.roll(a, tm - 1, 0))
        gv = dya * (ga * sg)
        rows8 = rows & 7
        for d in (1, 2, 4):
            keep = rows8 < 8 - d
            gv = gv + av * jnp.where(keep, _roll_in_groups(gv, 8 - d), 0.0)
            av = av * jnp.where(keep, _roll_in_groups(av, 8 - d), 1.0)
        a_s[...] = av
        g_s[...] = gv
        carry = cg_s[0:1, :]
        for g in range(ng - 1, -1, -1):
            sl = slice(g * 8, (g + 1) * 8)
            ab, gb = a_s[sl, :], g_s[sl, :]
            g_s[sl, :] = gb + ab * carry
            carry = gb[0:1, :] + ab[0:1, :] * carry
        cg_s[0:1, :] = a[0:1, :] * carry

        gt = g_s[...]
        da = gt * h_m1
        ixc = ig * xc
        di = gt * mult * xc
        dxc = gt * mult * ig
        dlog_a = da * a + jnp.where(first_row, 0.0, gt * ixc * (-a2 / mult))
        sm_ref[3:4, :] += _rowsum(dlog_a * ((-RG_C) * r))
        dpr_f = dlog_a * ((-RG_C) * sp) * r * (1.0 - r)
        dpi_f = di * ig * (1.0 - ig)
        sm_ref[1:2, :] += _rowsum(dpr_f)
        sm_ref[2:3, :] += _rowsum(dpi_f)
        dpr = dpr_f.astype(BF16)
        dpi = dpi_f.astype(BF16)
        xcb = xc.astype(BF16)
        dwa_s[...] += _mm_tn(xcb, dpr)
        dwx_s[...] += _mm_tn(xcb, dpi)
        dxc = dxc + _mm_nt(dpr, wa_s[...]) + _mm_nt(dpi, wx_s[...])
        sm_ref[0:1, :] += _rowsum(dxc)
        for j in range(4):
            sm_ref[4 + j:5 + j, :] += _rowsum(dxc * taps[j])
        nxt = nxt_s[...]
        dxa = (dxc * cw[3] + _shift_up(dxc, nxt, 1, rows) * cw[2]
               + _shift_up(dxc, nxt, 2, rows) * cw[1] + _shift_up(dxc, nxt, 3, rows) * cw[0])
        nxt_s[...] = dxc[0:8, :]
        da_ref[:, 0:DG] = dxa.astype(BF16)
        da_ref[:, DG:D] = dga.astype(BF16)

        @pl.when(i == nt - 1)
        def _():
            _, s_neg = _sig_pair(lam_v)
            sm_ref[3:4, :] = sm_ref[3:4, :] * (-s_neg)
            rg_ref[...] = _blocks_from_dense(dwa_s[...], dwx_s[...])

    vec = lambda: pl.BlockSpec((1, DG), lambda i: (0, 0))
    blocks = lambda: pl.BlockSpec((DG, RGB), lambda i: (0, 0))
    prev8 = lambda: pl.BlockSpec((8, DG), lambda i: (jnp.maximum((nt - 1 - i) * (tm // 8) - 1, 0), 0))
    return pl.pallas_call(
        body, name="bwd_rglru", grid=(nt,),
        in_specs=[pl.BlockSpec((tm, DG), lambda i: (nt - 1 - i, 0)),
                  pl.BlockSpec((tm, DG), lambda i: (nt - 1 - i, 0)),
                  prev8(),
                  pl.BlockSpec((tm, DG), lambda i: (nt - 1 - i, 0)),
                  prev8(),
                  pl.BlockSpec((tm, DG), lambda i: (nt - 1 - i, 0)),
                  pl.BlockSpec((NSHARD, 4, 128), lambda i: (0, 0, 0)), vec(),
                  blocks(), vec(), blocks(), vec(), vec(), ANY_SPEC],
        out_specs=[pl.BlockSpec((tm, D), lambda i: (nt - 1 - i, 0)),
                   pl.BlockSpec((DG, 128), lambda i: (0, 0)),
                   pl.BlockSpec((8, DG), lambda i: (0, 0))],
        out_shape=[jax.ShapeDtypeStruct((T, D), BF16), jax.ShapeDtypeStruct((DG, 128), F32),
                   jax.ShapeDtypeStruct((8, DG), F32)],
        scratch_shapes=[pltpu.VMEM((tm, DG), F32), pltpu.VMEM((tm, DG), F32),
                        pltpu.VMEM((8, DG), F32), pltpu.VMEM((8, DG), F32),
                        pltpu.VMEM((DG, DG), BF16), pltpu.VMEM((DG, DG), BF16),
                        pltpu.VMEM((DG, DG), F32), pltpu.VMEM((DG, DG), F32)],
        compiler_params=_cparams(("arbitrary",), 56),
    )(pf, pb, pf, h, h, dyab, cw3, conv_b, wa_c, ba, wx_c, bx, lam, dep)


def _bwd_hgrn2(pf, pb, o, s_prev, dyab, hg_lb, hg_nw, dep):
    tm = 512
    nt = T // tm
    nc = tm // CH

    def body(q_ref, f_ref, v_ref, g_ref, o_ref, sp_ref, dy_ref, lb_ref, nw_ref, dep_ref, db_ref, sm_ref,
             dst_s, eb_s, enb_s, ekd_s, qe_s, ke_s, kd_s, vb_s, do_s, dec_s, ddec_s, p_s, dp_s,
             g_s, dsta_s, dva_s, dqe_s, dke_s, dkd_s, dlf_s):
        i = pl.program_id(0)

        @pl.when(i == 0)
        def _():
            sm_ref[...] = jnp.zeros_like(sm_ref)
            dst_s[...] = jnp.zeros_like(dst_s)

        lb, one_m_lb = _hg_lower_bound(lb_ref)
        rows64 = lax.broadcasted_iota(I32, (tm, DG), 0) & (CH - 1)
        sg, sn, f, k, q, sq, qs = _hg_recompute(
            q_ref, f_ref, lb, one_m_lb, rows64, eb_s, enb_s, ekd_s, qe_s, ke_s, kd_s, dec_s)
        vb_s[...] = v_ref[...]

        nw = nw_ref[...]
        for hd in range(NH):
            cols = slice(hd * HD, (hd + 1) * HD)
            g = g_ref[:, cols].astype(F32)
            sgg = _sigmoid(g)
            o = o_ref[:, cols]
            so = lax.rsqrt(jnp.mean(o * o, axis=-1, keepdims=True) + EPS)
            oh = o * so
            dyb = dy_ref[:, cols]
            db_ref[:, 3 * DG + hd * HD:3 * DG + (hd + 1) * HD] = (
                dyb * (oh * nw) * (sgg * (1.0 + g * (1.0 - sgg)))).astype(BF16)
            don = dyb * (g * sgg)
            sm_ref[1:2, 0:HD] += _rowsum(don * oh)
            gw = don * nw
            do_s[:, cols] = (so * (gw - oh * jnp.mean(gw * oh, axis=-1, keepdims=True))).astype(BF16)

        mask = _tri(True)
        items = [(c, hd, slice(c * CH, (c + 1) * CH), slice(hd * HD, (hd + 1) * HD))
                 for c in range(nc) for hd in range(NH)]
        for c, hd, rs, cols in items:
            p_s[c * NH + hd] = jnp.where(mask, _mm_nt(qe_s[rs, cols], ke_s[rs, cols]), 0.0).astype(BF16)
            dp_s[c * NH + hd] = jnp.where(mask, _mm_nt(do_s[rs, cols], vb_s[rs, cols]), 0.0).astype(BF16)
        for c, hd, rs, cols in items:
            n = c * NH + hd
            dva_s[rs, cols] = _mm_tn(p_s[n], do_s[rs, cols])
            dqe_s[rs, cols] = _mm(dp_s[n], ke_s[rs, cols])
            dke_s[rs, cols] = _mm_tn(dp_s[n], qe_s[rs, cols])
            g_s[n] = _mm_tn(do_s[rs, cols], qe_s[rs, cols])
        for c, hd, rs, cols in reversed(items):
            n = c * NH + hd
            dst = dst_s[hd]
            dsta_s[n] = dst
            dst_s[hd] = dst * dec_s[c:c + 1, cols] + g_s[n]
        for c, hd, rs, cols in items:
            n = c * NH + hd
            dst = dsta_s[n]
            st_prev = sp_ref[hd, c]
            dv = dva_s[rs, cols] + _mm_nt(kd_s[rs, cols], dst)
            db_ref[rs, 2 * DG + hd * HD:2 * DG + (hd + 1) * HD] = dv.astype(BF16)
            dqe_s[rs, cols] += _mm(do_s[rs, cols], st_prev)
            dkd_s[rs, cols] = _mm(vb_s[rs, cols], dst)
            ddec_s[c:c + 1, cols] = _rowsum(dst * st_prev)

        eb, enb, ekd = eb_s[...], enb_s[...], ekd_s[...]
        dqe, dke, dkd = dqe_s[...], dke_s[...], dkd_s[...]
        t_kd = dkd * (k * ekd)
        rc = _chunk_rev_cumsum(dqe * (qs * eb) - dke * (k * enb) - t_kd, rows64)
        for c in range(nc):
            rs = slice(c * CH, (c + 1) * CH)
            dbl = _rowsum(t_kd[rs]) + ddec_s[c:c + 1, :] * dec_s[c:c + 1, :]
            dlf_s[rs, :] = rc[rs] + dbl
        t = dlf_s[...] / f - (dke * enb + dkd * ekd)
        db_ref[:, DG:2 * DG] = (one_m_lb * sg * sn * t).astype(BF16)
        sm_ref[0:1, :] += _rowsum(sn * t)
        db_ref[:, 0:DG] = (dqe * eb * (sq * (1.0 + q * (1.0 - sq))) * (HD ** -0.5)).astype(BF16)

        @pl.when(i == nt - 1)
        def _():
            dsm = sm_ref[0:1, :] * (lb * one_m_lb)
            sm_ref[2:3, :] = dsm
            sm_ref[3:4, :] = -dsm

    col = lambda j: pl.BlockSpec((tm, DG), lambda i: (nt - 1 - i, j))
    big = lambda dt: pltpu.VMEM((tm, DG), dt)
    return pl.pallas_call(
        body, name="bwd_hgrn2", grid=(nt,),
        in_specs=[col(1), col(1), col(2), col(3),
                  pl.BlockSpec((tm, DG), lambda i: (nt - 1 - i, 0)),
                  pl.BlockSpec((NH, nc, HD, HD), lambda i: (0, nt - 1 - i, 0, 0)),
                  pl.BlockSpec((tm, DG), lambda i: (nt - 1 - i, 1)),
                  pl.BlockSpec((2, DG), lambda i: (0, 0)),
                  pl.BlockSpec((1, HD), lambda i: (0, 0)), ANY_SPEC],
        out_specs=[pl.BlockSpec((tm, 4 * DG), lambda i: (nt - 1 - i, 0)),
                   pl.BlockSpec((8, DG), lambda i: (0, 0))],
        out_shape=[jax.ShapeDtypeStruct((T, 4 * DG), BF16), jax.ShapeDtypeStruct((8, DG), F32)],
        scratch_shapes=[pltpu.VMEM((NH, HD, HD), F32),
                        big(F32), big(F32), big(F32),
                        big(BF16), big(BF16), big(BF16), big(BF16), big(BF16),
                        pltpu.VMEM((nc, DG), F32), pltpu.VMEM((nc, DG), F32),
                        pltpu.VMEM((nc * NH, CH, CH), BF16), pltpu.VMEM((nc * NH, CH, CH), BF16),
                        pltpu.VMEM((nc * NH, HD, HD), F32), pltpu.VMEM((nc * NH, HD, HD), F32),
                        big(F32), big(F32), big(F32), big(F32), big(F32)],
        compiler_params=_cparams(("arbitrary",), 56),
    )(pb, pf, pb, pb, o, s_prev, dyab, hg_lb, hg_nw, dep)


def _dproj_pieces(k, da_ref, db_ref):
    if k == 0:
        return [(da_ref[:, 0:SHW], 0)]
    if k == 1:
        return [(da_ref[:, SHW:D], 0), (db_ref[:, 0:DG], D - SHW)]
    if k == 2:
        return [(db_ref[:, DG:DG + SHW], 0)]
    return [(db_ref[:, DG + SHW:4 * DG], 0)]


def _bwd_inproj_dx(x, dh1, d_a, d_b, w_in_b, nw, dep):
    tm = 512
    nt = T // tm

    def body(x_ref, dh1_ref, da_ref, db_ref, w_ref, nw_ref, dep_ref, dx_ref, sm_ref):
        i = pl.program_id(0)

        @pl.when(i == 0)
        def _():
            sm_ref[...] = jnp.zeros_like(sm_ref)

        du = None
        for k in range(NSHARD):
            for val, off in _dproj_pieces(k, da_ref, db_ref):
                t = _mm_nt(val, w_ref[k, :, off:off + val.shape[1]])
                du = t if du is None else du + t
        xv = x_ref[...]
        s = lax.rsqrt(jnp.mean(xv * xv, axis=-1, keepdims=True) + EPS)
        xh = xv * s
        sm_ref[0:1, :] += _rowsum(du * xh)
        g = du * nw_ref[...]
        dx_ref[...] = dh1_ref[...] + s * (g - xh * jnp.mean(g * xh, axis=-1, keepdims=True))

    row = lambda w: pl.BlockSpec((tm, w), lambda i: (i, 0))
    return pl.pallas_call(
        body, name="bwd_inproj_dx", grid=(nt,),
        in_specs=[row(D), row(D), row(D), row(4 * DG),
                  pl.BlockSpec((NSHARD, D, SHW), lambda i: (0, 0, 0), pipeline_mode=pl.Buffered(1)),
                  pl.BlockSpec((1, D), lambda i: (0, 0)), ANY_SPEC],
        out_specs=[row(D), pl.BlockSpec((8, D), lambda i: (0, 0))],
        out_shape=[jax.ShapeDtypeStruct((T, D), F32), jax.ShapeDtypeStruct((8, D), F32)],
        compiler_params=_cparams(("arbitrary",), 56),
    )(x, dh1, d_a, d_b, w_in_b, nw, dep)


def _bwd_inproj_dw(name, ks, u_b, d_a, d_b):
    tm = 1024
    nt = T // tm
    nk = len(ks)
    H = D // 2
    step = 128
    need_a = 0 in ks or 1 in ks

    def body(*refs):
        if need_a:
            u_ref, da_ref, db_ref, dw_ref, acc, send_s, recv_s, ssem, rsem = refs
        else:
            u_ref, db_ref, dw_ref, acc, send_s, recv_s, ssem, rsem = refs
            da_ref = None
        i = pl.program_id(0)

        @pl.when(i == 0)
        def _():
            acc[...] = jnp.zeros_like(acc)

        u = u_ref[...]
        for j, k in enumerate(ks):
            for val, off in _dproj_pieces(k, da_ref, db_ref):
                acc[j, :, off:off + val.shape[1]] += _mm_tn(u, val)

        @pl.when(i == nt - 1)
        def _():
            x, y, c = lax.axis_index("x"), lax.axis_index("y"), lax.axis_index("c")
            sibling = (x, y, 1 - c)
            mine0 = pl.multiple_of(c * H, step)
            other0 = pl.multiple_of((1 - c) * H, step)
            copies = []
            for j in range(nk):
                for r0 in range(0, H, step):
                    send_s[j, r0:r0 + step, :] = acc[j, pl.ds(other0 + r0, step), :].astype(BF16)
                cp = _remote(send_s.at[j], recv_s.at[j], ssem.at[j], rsem.at[j], sibling)
                cp.start()
                copies.append(cp)
            for j in range(nk):
                copies[j].wait_recv()
                for r0 in range(0, H, step):
                    s = acc[j, pl.ds(mine0 + r0, step), :] + recv_s[j, r0:r0 + step, :].astype(F32)
                    dw_ref[j, r0:r0 + step, :] = s.astype(BF16)
            for cp in copies:
                cp.wait_send()

    row = lambda w: pl.BlockSpec((tm, w), lambda i: (i, 0))
    ins = [u_b] + ([d_a] if need_a else []) + [d_b]
    in_specs = [row(D)] + ([row(D)] if need_a else []) + [row(4 * DG)]
    return pl.pallas_call(
        body, name=name, grid=(nt,), in_specs=in_specs,
        out_specs=pl.BlockSpec((nk, H, SHW), lambda i: (0, 0, 0)),
        out_shape=jax.ShapeDtypeStruct((nk, H, SHW), BF16),
        scratch_shapes=[pltpu.VMEM((nk, D, SHW), F32), pltpu.VMEM((nk, H, SHW), BF16),
                        pltpu.VMEM((nk, H, SHW), BF16),
                        pltpu.SemaphoreType.DMA((nk,)), pltpu.SemaphoreType.DMA((nk,))],
        compiler_params=_cparams(("arbitrary",), 48),
    )(*ins)


_OUT_ORDER = ["norm_mix_w", "w_in", "conv_w", "conv_b", "rg_wa", "rg_ba", "rg_wx", "rg_bx", "rg_lambda", "hg_lb",
              "hg_norm_w", "w_out", "ple_norm_w", "w_ple_gate", "b_ple_gate", "w_ple_proj", "final_norm_w"]
_BIG = ["w_in", "w_out", "w_ple_gate", "w_ple_proj"]


def _small_view(name, a):
    if name in ("rg_wa", "rg_wx"):
        return a.reshape(DG, RGB)
    if name == "conv_w":
        return a.reshape(4, 128)
    if name == "final_norm_w":
        return a.reshape(1, D)
    return a


def _landing(rows, cols):
    return lax.empty((NDEV, rows, cols), BF16)


def kernel(x, p, norm_mix_w, w_in, conv_w, conv_b, rg_wa, rg_ba, rg_wx, rg_bx, rg_lambda, hg_lb, hg_norm_w, w_out, ple_norm_w, w_ple_gate, b_ple_gate, w_ple_proj, final_norm_w, loss_target, m_norm_mix_w, m_w_in, m_conv_w, m_conv_b, m_rg_wa, m_rg_ba, m_rg_wx, m_rg_bx, m_rg_lambda, m_hg_lb, m_hg_norm_w, m_w_out, m_ple_norm_w, m_w_ple_gate, m_b_ple_gate, m_w_ple_proj, m_final_norm_w, v_norm_mix_w, v_w_in, v_conv_w, v_conv_b, v_rg_wa, v_rg_ba, v_rg_wx, v_rg_bx, v_rg_lambda, v_hg_lb, v_hg_norm_w, v_w_out, v_ple_norm_w, v_w_ple_gate, v_b_ple_gate, v_w_ple_proj, v_final_norm_w):
    given = dict(locals())
    x2, p2, tgt = x[0], p[0, 0], loss_target[0]
    wa_c, wx_c = _small_view("rg_wa", rg_wa), _small_view("rg_wx", rg_wx)

    w_in_b, l_out, l_pg, l_pp, cw3 = _gather_w_in(w_in[0], w_out[0], w_ple_gate[0], w_ple_proj[0], conv_w[0])
    g_ssem, g_rsem, g_lands, tok = _gather_rest_start([l_out, l_pg, l_pp])

    pf, pb, u_b = _fwd_inproj(x2, norm_mix_w, w_in_b, tok)
    h, ya = _fwd_rglru(pf, pb, cw3, conv_b, wa_c, rg_ba, wx_c, rg_bx, rg_lambda)
    yb, o, s_prev = _fwd_hgrn2(pf, pb, hg_lb, hg_norm_w)
    w_out_b, w_pg_b, w_pp_b = _gather_rest_wait(g_ssem, g_rsem, g_lands, yb)
    dh1, dyab, dwo_b, dwg_b, dwp_b, sm_tail = _tail_fwd_bwd(
        x2, p2, tgt, ya, yb, w_out_b.reshape(D, D), w_pg_b.reshape(D, D), w_pp_b,
        ple_norm_w, b_ple_gate, final_norm_w.reshape(1, D))

    QH = D // NSHARD // 2
    r1 = _rs_start("rs_start_tail", [dwo_b, dwg_b, dwp_b], (0, 1, 2, 3),
                   [_landing(QH, D), _landing(QH, D), _landing(PLE // 2, PLE)])
    d_b, sm_b = _bwd_hgrn2(pf, pb, o, s_prev, dyab, hg_lb, hg_norm_w, r1[4])
    dw23 = _bwd_inproj_dw("bwd_inproj_dw23", (2, 3), u_b, None, d_b)
    r2 = _rs_start("rs_start_in23", [dw23], (2, 3), [lax.empty((NSHARD, D // 2, SHW), BF16)], chip_sums=True)
    d_a, rg_c, sm_a = _bwd_rglru(pf, pb, h, dyab, cw3, conv_b, wa_c, rg_ba, wx_c, rg_bx, rg_lambda, r2[4])
    dw01 = _bwd_inproj_dw("bwd_inproj_dw01", (0, 1), u_b, d_a, d_b)
    r3 = _rs_start("rs_start_in01", [dw01], (0, 1), r2[3], chip_sums=True)
    grad_x, sm_in = _bwd_inproj_dx(x2, dh1, d_a, d_b, w_in_b, norm_mix_w, r3[4])

    parts1, lands1 = _rs_wait("rs_wait_tail", r1[0], r1[1], r1[2], (0, 1, 2, 3), r1[3], sm_in)
    parts2, lands2 = _rs_wait("rs_wait_in23", r2[0], r2[1], r2[2], (2, 3), r3[3], sm_in, chip_sums=True)
    parts3, lands3 = _rs_wait("rs_wait_in01", r3[0], r3[1], r3[2], (0, 1), lands2, sm_in, chip_sums=True)
    g_big, vred, rgred = _final_reduce([parts3[0], parts2[0]] + parts1, lands3 + lands1,
                                       sm_in, sm_tail, sm_a, sm_b, rg_c)

    upd_big = _adam_big(g_big, [given[n][0] for n in _BIG], [given["m_" + n][0] for n in _BIG],
                        [given["v_" + n][0] for n in _BIG])
    small = _adam_small(vred, rgred,
                        {n: _small_view(n, given[n]) for n in _SMALL_ORDER},
                        {n: _small_view(n, given["m_" + n]) for n in _SMALL_ORDER},
                        {n: _small_view(n, given["v_" + n]) for n in _SMALL_ORDER})

    loss = vred[0, ROW_LOSS, 0]
    outs = [loss, grad_x[None]]
    for ki in range(4):
        for n in _OUT_ORDER:
            if n in _BIG:
                i = _BIG.index(n)
                a = g_big[i] if ki == 0 else upd_big[i][ki - 1]
                outs.append(a[None])
            else:
                outs.append(small[n][ki].reshape(given[n].shape))
    return tuple(outs)
```

```python
import jax
import jax.numpy as jnp
from jax import lax
from jax.experimental import pallas as pl
from jax.experimental.pallas import tpu as pltpu

F32 = jnp.float32
BF16 = jnp.bfloat16
I32 = jnp.int32
MESH = pl.DeviceIdType.MESH
HIGHEST = lax.Precision.HIGHEST

T = 4096
D = 1024
DG = 512
DIN = 3072
PLE = 256
NH = 4
HD = 128
CH = 64
NCHUNK = T // CH
RGB = 64
EPS = 1e-6
RG_C = 8.0
NSHARD = 4
SHW = DIN // NSHARD
NDEV = 8

ADAM_LR = 0.001
ADAM_B1 = 0.9
ADAM_B2 = 0.999
ADAM_EPS = 1e-08
ADAM_WD = 0.01
ADAM_STEP = 10

VMEM_SPEC = pl.BlockSpec(memory_space=pltpu.VMEM)
HBM_SPEC = pl.BlockSpec(memory_space=pltpu.HBM)
SEM_SPEC = pl.BlockSpec(memory_space=pltpu.SEMAPHORE)
ANY_SPEC = pl.BlockSpec(memory_space=pl.ANY)
EFFECT = pltpu.SideEffectType.DATAFLOW_SIDE_EFFECTING
MIB = 1024 * 1024

VROWS = 16
ROW_NORM_MIX, ROW_FINAL_NORM, ROW_B_PG, ROW_PLE_NORM = 0, 1, 2, 3
ROW_CB_BA, ROW_BX_LAM, ROW_CW01, ROW_CW23, ROW_HG_LB, ROW_HG_NW, ROW_LOSS = 4, 5, 6, 7, 8, 9, 10


def _mm(a, b):
    return jnp.dot(a.astype(BF16), b.astype(BF16), preferred_element_type=F32)


def _mm_nt(a, b):
    return lax.dot_general(a.astype(BF16), b.astype(BF16), (((1,), (1,)), ((), ())),
                           preferred_element_type=F32)


def _mm_tn(a, b):
    return lax.dot_general(a.astype(BF16), b.astype(BF16), (((0,), (0,)), ((), ())),
                           preferred_element_type=F32)


def _mm_exact(a, b):
    return jnp.dot(a, b, precision=HIGHEST, preferred_element_type=F32)


def _sig_pair(x):
    e = jnp.exp(-jnp.abs(x))
    big = 1.0 / (1.0 + e)
    small = e * big
    pos = x >= 0
    return jnp.where(pos, big, small), jnp.where(pos, small, big)


def _sigmoid(x):
    return 1.0 / (1.0 + jnp.exp(-x))


def _rowsum(v):
    return jnp.sum(v, axis=0, keepdims=True)


def _shift_down(cur, prev8, d, rows):
    rolled = pltpu.roll(cur, d, 0)
    head = jnp.where(rows[0:8] < d, pltpu.roll(prev8, d, 0), rolled[0:8])
    return jnp.concatenate([head, rolled[8:]], axis=0)


def _shift_up(cur, next8, d, rows):
    n = cur.shape[0]
    rolled = pltpu.roll(cur, n - d, 0)
    tail = jnp.where(rows[0:8] >= 8 - d, pltpu.roll(next8, 8 - d, 0), rolled[n - 8:n])
    return jnp.concatenate([rolled[0:n - 8], tail], axis=0)


def _roll_in_groups(v, d):
    n, w = v.shape
    return pltpu.roll(v.reshape(n // 8, 8, w), d, 1).reshape(n, w)


def _cparams(sem, vmem_mib):
    return pltpu.CompilerParams(dimension_semantics=sem, vmem_limit_bytes=vmem_mib * MIB)


def _mesh_pos():
    x, y, c = lax.axis_index("x"), lax.axis_index("y"), lax.axis_index("c")
    chips = [(1 - x, y), (x, 1 - y), (1 - x, 1 - y)]
    return x, y, c, chips


def _remote(src, dst, ssem, rsem, dev):
    return pltpu.make_async_remote_copy(src_ref=src, dst_ref=dst, send_sem=ssem, recv_sem=rsem,
                                        device_id=dev, device_id_type=MESH)


def _gather_w_in(w_in, w_out, w_pg, w_pp, conv_w):
    shapes = [w_in.shape, w_out.shape, w_pg.shape, w_pp.shape]

    def body(win_h, wout_h, wpg_h, wpp_h, cw_h, o_in, o_out, o_pg, o_pp, o_cw,
             win, wout, wpg, wpp, cw, lsem, ssem, rsem):
        x, y, c, chips = _mesh_pos()
        kme = 2 * x + y
        sibling = (x, y, 1 - c)
        fetch = [pltpu.make_async_copy(src, dst, lsem.at[i])
                 for i, (src, dst) in enumerate([(win_h, win), (cw_h, cw), (wout_h, wout), (wpg_h, wpg), (wpp_h, wpp)])]
        for cp in fetch:
            cp.start()

        def cast(src, dst):
            for r0 in range(0, src.shape[0], 128):
                dst[kme, r0:r0 + 128, :] = src[r0:r0 + 128, :].astype(BF16)

        fetch[0].wait()
        cast(win, o_in)
        fetch[1].wait()
        o_cw[kme] = cw[...]

        hrows = D // 2
        mine = pl.ds(pl.multiple_of(c * hrows, 128), hrows)
        other = pl.ds(pl.multiple_of((1 - c) * hrows, 128), hrows)

        def half(k, rows):
            return o_in.at[k, rows]

        na = (x ^ c, y ^ (1 - c), c)
        nb = (x ^ (1 - c), y ^ c, c)
        ka = 2 * na[0] + na[1]
        kb = 2 * nb[0] + nb[1]
        kd = 2 * (1 - x) + (1 - y)
        sends = [_remote(half(kme, mine), half(kme, mine), ssem.at[0], rsem.at[0], na),
                 _remote(half(kme, mine), half(kme, mine), ssem.at[1], rsem.at[1], nb)]
        for j, (px, py) in enumerate(chips):
            sends.append(_remote(o_cw.at[kme], o_cw.at[kme], ssem.at[3 + j], rsem.at[3 + j], (px, py, c)))
        for cp in sends:
            cp.start()
        for i, (src, dst) in enumerate([(wout, o_out), (wpg, o_pg), (wpp, o_pp)]):
            fetch[2 + i].wait()
            cast(src, dst)

        def passed_on(k, sem, dev):
            cp = _remote(half(k, mine), half(k, mine), ssem.at[sem], rsem.at[sem], dev)
            cp.start()
            sends.append(cp)

        _remote(half(ka, mine), half(ka, mine), ssem.at[0], rsem.at[0], na).wait_recv()
        passed_on(ka, 2, nb)
        passed_on(ka, 6, sibling)
        _remote(half(kb, mine), half(kb, mine), ssem.at[1], rsem.at[1], nb).wait_recv()
        passed_on(kb, 7, sibling)
        _remote(half(kd, mine), half(kd, mine), ssem.at[2], rsem.at[2], nb).wait_recv()
        passed_on(kd, 8, sibling)
        _remote(half(kb, other), half(kb, other), ssem.at[6], rsem.at[6], sibling).wait_recv()
        _remote(half(ka, other), half(ka, other), ssem.at[7], rsem.at[7], sibling).wait_recv()
        _remote(half(kd, other), half(kd, other), ssem.at[8], rsem.at[8], sibling).wait_recv()
        for j, (px, py) in enumerate(chips):
            kj = 2 * px + py
            _remote(o_cw.at[kj], o_cw.at[kj], ssem.at[3 + j], rsem.at[3 + j], (px, py, c)).wait_recv()
        for cp in sends:
            cp.wait_send()

    out_shape = [jax.ShapeDtypeStruct((NSHARD,) + s, BF16) for s in shapes]
    out_shape.append(jax.ShapeDtypeStruct((NSHARD,) + conv_w.shape, F32))
    return pl.pallas_call(
        body, name="gather_w_in", out_shape=out_shape,
        in_specs=[HBM_SPEC] * 5, out_specs=[VMEM_SPEC] * 5,
        scratch_shapes=[pltpu.VMEM(a.shape, F32) for a in (w_in, w_out, w_pg, w_pp, conv_w)]
        + [pltpu.SemaphoreType.DMA((5,)), pltpu.SemaphoreType.DMA((9,)), pltpu.SemaphoreType.DMA((9,))],
        compiler_params=pltpu.CompilerParams(vmem_limit_bytes=40 * MIB),
    )(*[pltpu.with_memory_space_constraint(a, pltpu.HBM) for a in (w_in, w_out, w_pg, w_pp, conv_w)])


def _gather_rest_start(lands):
    n = len(lands)

    def body(*refs):
        land_in = refs[0:n]
        ssem, rsem = refs[n], refs[n + 1]
        token = refs[2 * n + 2]
        x, y, c, chips = _mesh_pos()
        kme = 2 * x + y
        for p, land in enumerate(land_in):
            hrows = land.shape[1] // 2
            mine = pl.ds(pl.multiple_of(c * hrows, 128), hrows)
            for px, py in chips:
                for pc in range(2):
                    _remote(land.at[kme, mine], land.at[kme, mine], ssem.at[p], rsem.at[p], (px, py, pc)).start()
        token[...] = jnp.zeros_like(token)

    out_shape = ([pltpu.SemaphoreType.DMA((n,)), pltpu.SemaphoreType.DMA((n,))]
                 + [pltpu.HBM(a.shape, a.dtype) for a in lands] + [jax.ShapeDtypeStruct((8, 128), F32)])
    outs = pl.pallas_call(
        body, name="gather_rest_start", out_shape=out_shape,
        in_specs=[HBM_SPEC] * n, out_specs=[SEM_SPEC, SEM_SPEC] + [HBM_SPEC] * n + [VMEM_SPEC],
        input_output_aliases={i: 2 + i for i in range(n)},
        compiler_params=pltpu.CompilerParams(has_side_effects=EFFECT),
    )(*[pltpu.with_memory_space_constraint(a, pltpu.HBM) for a in lands])
    return outs[0], outs[1], list(outs[2:2 + n]), outs[2 + n]


def _gather_rest_wait(ssem, rsem, lands, after):
    n = len(lands)

    def body(*refs):
        land_in = refs[0:n]
        ssem_ref, rsem_ref = refs[n], refs[n + 1]
        x, y, c = lax.axis_index("x"), lax.axis_index("y"), lax.axis_index("c")
        for p, land in enumerate(land_in):
            three = land.at[pl.ds(0, 3)]
            cp = _remote(three, three, ssem_ref.at[p], rsem_ref.at[p], (x, y, c))
            cp.wait_send()
            cp.wait_recv()

    outs = pl.pallas_call(
        body, name="gather_rest_wait", out_shape=[pltpu.HBM(a.shape, a.dtype) for a in lands],
        in_specs=[HBM_SPEC] * n + [SEM_SPEC, SEM_SPEC, ANY_SPEC], out_specs=[HBM_SPEC] * n,
        input_output_aliases={i: i for i in range(n)},
        compiler_params=pltpu.CompilerParams(has_side_effects=EFFECT),
    )(*lands, ssem, rsem, after)
    return list(outs)


def _rs_start(name, parts, ks, lands, chip_sums=False):
    n = len(parts)

    def body(*refs):
        part_in, land_in = refs[0:n], refs[n:2 * n]
        ssem, rsem = refs[2 * n], refs[2 * n + 1]
        token = refs[4 * n + 2]
        x, y, c = lax.axis_index("x"), lax.axis_index("y"), lax.axis_index("c")
        kme = 2 * x + y
        me = 4 * x + 2 * y + c
        for p in range(n):
            hrows = land_in[p].shape[1]
            for i, k in enumerate(ks):
                if chip_sums:
                    @pl.when(kme != k)
                    def _():
                        _remote(part_in[p].at[i], land_in[p].at[kme], ssem.at[p], rsem.at[p], (k // 2, k % 2, c)).start()
                    continue
                for pc in range(2):
                    @pl.when(jnp.logical_or(kme != k, c != pc))
                    def _():
                        _remote(part_in[p].at[i, pl.ds(pc * hrows, hrows)], land_in[p].at[me],
                                ssem.at[p], rsem.at[p], (k // 2, k % 2, pc)).start()
        token[...] = jnp.zeros_like(token)

    arrays = list(parts) + list(lands)
    out_shape = ([pltpu.SemaphoreType.DMA((n,)), pltpu.SemaphoreType.DMA((n,))]
                 + [pltpu.HBM(a.shape, a.dtype) for a in arrays] + [jax.ShapeDtypeStruct((8, 128), F32)])
    outs = pl.pallas_call(
        body, name=name, out_shape=out_shape,
        in_specs=[HBM_SPEC] * (2 * n), out_specs=[SEM_SPEC, SEM_SPEC] + [HBM_SPEC] * (2 * n) + [VMEM_SPEC],
        input_output_aliases={i: 2 + i for i in range(2 * n)},
        compiler_params=pltpu.CompilerParams(has_side_effects=EFFECT),
    )(*[pltpu.with_memory_space_constraint(a, pltpu.HBM) for a in arrays])
    return outs[0], outs[1], list(outs[2:2 + n]), list(outs[2 + n:2 + 2 * n]), outs[2 + 2 * n]


def _rs_wait(name, ssem, rsem, parts, ks, lands, after, chip_sums=False):
    n = len(parts)

    def body(*refs):
        part_in, land_in = refs[0:n], refs[n:2 * n]
        ssem_ref, rsem_ref = refs[2 * n], refs[2 * n + 1]
        x, y, c = lax.axis_index("x"), lax.axis_index("y"), lax.axis_index("c")
        kme = 2 * x + y
        for p in range(n):
            piece = land_in[p].at[0]
            for k in ks:
                for pc in range(1 if chip_sums else 2):
                    mine = (kme == k) if chip_sums else jnp.logical_and(kme == k, c == pc)

                    @pl.when(jnp.logical_not(mine))
                    def _():
                        _remote(piece, piece, ssem_ref.at[p], rsem_ref.at[p], (x, y, c)).wait_send()
            owner = kme == ks[0]
            for k in ks[1:]:
                owner = jnp.logical_or(owner, kme == k)

            @pl.when(owner)
            def _():
                others = land_in[p].at[pl.ds(0, land_in[p].shape[0] - 1)]
                _remote(others, others, ssem_ref.at[p], rsem_ref.at[p], (x, y, c)).wait_recv()

    arrays = list(parts) + list(lands)
    outs = pl.pallas_call(
        body, name=name, out_shape=[pltpu.HBM(a.shape, a.dtype) for a in arrays],
        in_specs=[HBM_SPEC] * (2 * n) + [SEM_SPEC, SEM_SPEC, ANY_SPEC], out_specs=[HBM_SPEC] * (2 * n),
        input_output_aliases={i: i for i in range(2 * n)},
        compiler_params=pltpu.CompilerParams(has_side_effects=EFFECT),
    )(*arrays, ssem, rsem, after)
    return list(outs[0:n]), list(outs[n:2 * n])


def _reduce_exchange(parts, lands):
    shapes = [(2 * l.shape[1], l.shape[2]) for l in lands]
    step = 128

    def body(in01, in23, pout, ppg, ppp, l_in, l_out, l_pg, l_pp, g_in, g_out, g_pg, g_pp, ssem, rsem):
        x, y, c = lax.axis_index("x"), lax.axis_index("y"), lax.axis_index("c")
        kme = 2 * x + y
        me = 4 * x + 2 * y + c
        sibling = (x, y, 1 - c)
        sends = []
        for p, (land, gout) in enumerate([(l_in, g_in), (l_out, g_out), (l_pg, g_pg), (l_pp, g_pp)]):
            hrows = land.shape[1]
            mine0 = pl.multiple_of(c * hrows, step)
            for r0 in range(0, hrows, step):
                rs = pl.ds(mine0 + r0, step)
                if p == 0:
                    own = jnp.where(kme >= 2, in23[kme & 1, r0:r0 + step, :], in01[kme & 1, r0:r0 + step, :])
                    slot = kme
                else:
                    own = (pout, ppg, ppp)[p - 1][kme, rs, :]
                    slot = me
                s = jnp.zeros((step, land.shape[2]), F32)
                for j in range(land.shape[0]):
                    s = s + jnp.where(slot == j, own, land[j, r0:r0 + step, :]).astype(F32)
                gout[rs, :] = s
            mine = pl.ds(mine0, hrows)
            cp = _remote(gout.at[mine], gout.at[mine], ssem.at[p], rsem.at[p], sibling)
            cp.start()
            sends.append(cp)
        for p, (land, gout) in enumerate([(l_in, g_in), (l_out, g_out), (l_pg, g_pg), (l_pp, g_pp)]):
            hrows = land.shape[1]
            other = pl.ds(pl.multiple_of((1 - c) * hrows, step), hrows)
            _remote(gout.at[other], gout.at[other], ssem.at[p], rsem.at[p], sibling).wait_recv()
        for cp in sends:
            cp.wait_send()

    return pl.pallas_call(
        body, name="reduce_exchange", out_shape=[jax.ShapeDtypeStruct(s, F32) for s in shapes],
        in_specs=[VMEM_SPEC] * 9, out_specs=[VMEM_SPEC] * 4,
        scratch_shapes=[pltpu.SemaphoreType.DMA((4,)), pltpu.SemaphoreType.DMA((4,))],
        compiler_params=pltpu.CompilerParams(vmem_limit_bytes=48 * MIB),
    )(*parts, *lands)


def _small_allreduce(sm_in, sm_tail, sm_a, sm_b, rg_c):
    PR = DG // NDEV

    def body(in_ref, tail_ref, a_ref, b_ref, rg_ref, v_out, rg_out, vbuf, vrecv, rgrecv, ssem, rsem):
        x, y, c = lax.axis_index("x"), lax.axis_index("y"), lax.axis_index("c")
        me = 4 * x + 2 * y + c

        def pair(ref, r0, r1):
            return jnp.concatenate([ref[r0:r0 + 1, :], ref[r1:r1 + 1, :]], axis=1)

        rows = {
            ROW_NORM_MIX: in_ref[0:1, :], ROW_FINAL_NORM: tail_ref[1:2, :], ROW_B_PG: tail_ref[2:3, :],
            ROW_PLE_NORM: tail_ref[3:4, :], ROW_CB_BA: pair(a_ref, 0, 1), ROW_BX_LAM: pair(a_ref, 2, 3),
            ROW_CW01: pair(a_ref, 4, 5), ROW_CW23: pair(a_ref, 6, 7), ROW_HG_LB: pair(b_ref, 2, 3),
            ROW_HG_NW: jnp.concatenate([b_ref[1:2, :], jnp.zeros((1, DG), F32)], axis=1),
            ROW_LOSS: tail_ref[4:5, :],
        }
        vbuf[...] = jnp.zeros_like(vbuf)
        for r, row in rows.items():
            for j in range(NDEV):
                vbuf[j, r:r + 1, :] = row[:, j * 128:(j + 1) * 128]

        def peer(mask):
            px = x ^ ((mask >> 2) & 1)
            py = y ^ ((mask >> 1) & 1)
            pc = c ^ (mask & 1)
            return (px, py, pc), 4 * px + 2 * py + pc

        def rg_rows(r):
            return pl.ds(pl.multiple_of(r * PR, PR), PR)

        first = []
        for mask in range(1, NDEV):
            dev, r = peer(mask)
            i = mask - 1
            cp = _remote(vbuf.at[r], vrecv.at[i], ssem.at[i], rsem.at[i], dev)
            cp.start()
            first.append(cp)
            cp = _remote(rg_ref.at[rg_rows(r)], rgrecv.at[i], ssem.at[7 + i], rsem.at[7 + i], dev)
            cp.start()
            first.append(cp)
        sv = vbuf[me]
        sr = rg_ref[rg_rows(me), :]
        for i in range(NDEV - 1):
            first[2 * i].wait_recv()
            first[2 * i + 1].wait_recv()
            sv = sv + vrecv[i]
            sr = sr + rgrecv[i]
        v_out[me] = sv
        rg_out[rg_rows(me), :] = sr
        second = []
        for mask in range(1, NDEV):
            dev, r = peer(mask)
            i = mask - 1
            cp = _remote(v_out.at[me], v_out.at[me], ssem.at[14 + i], rsem.at[14 + i], dev)
            cp.start()
            second.append(cp)
            cp = _remote(rg_out.at[rg_rows(me)], rg_out.at[rg_rows(me)], ssem.at[21 + i], rsem.at[21 + i], dev)
            cp.start()
            second.append(cp)
        for mask in range(1, NDEV):
            dev, r = peer(mask)
            i = mask - 1
            _remote(v_out.at[r], v_out.at[r], ssem.at[14 + i], rsem.at[14 + i], dev).wait_recv()
            _remote(rg_out.at[rg_rows(r)], rg_out.at[rg_rows(r)], ssem.at[21 + i], rsem.at[21 + i], dev).wait_recv()
        for cp in first + second:
            cp.wait_send()

    return pl.pallas_call(
        body, name="small_allreduce",
        out_shape=[jax.ShapeDtypeStruct((NDEV, VROWS, 128), F32), jax.ShapeDtypeStruct((DG, 128), F32)],
        in_specs=[VMEM_SPEC] * 5, out_specs=[VMEM_SPEC] * 2,
        scratch_shapes=[pltpu.VMEM((NDEV, VROWS, 128), F32), pltpu.VMEM((NDEV - 1, VROWS, 128), F32),
                        pltpu.VMEM((NDEV - 1, PR, 128), F32),
                        pltpu.SemaphoreType.DMA((28,)), pltpu.SemaphoreType.DMA((28,))],
    )(sm_in, sm_tail, sm_a, sm_b, rg_c)


def _final_reduce(parts, lands, sm_in, sm_tail, sm_a, sm_b, rg_c):
    shapes = [(2 * l.shape[1], l.shape[2]) for l in lands]
    step = 128
    PR = DG // NDEV

    def body(in01, in23, pout, ppg, ppp, l_in, l_out, l_pg, l_pp, in_h, tail_h, a_h, b_h, rg_h,
             g_in, g_out, g_pg, g_pp, v_out, rg_out, vbuf, vrecv, rgrecv,
             lb_in, lb_out, lb_pg, lb_pp, ob_in, ob_out, ob_pg, ob_pp,
             in_ref, tail_ref, a_ref, b_ref, rg_ref, lsem, ssem_l, bsem_s, bsem_r, ssem, rsem):
        x, y, c = lax.axis_index("x"), lax.axis_index("y"), lax.axis_index("c")
        kme = 2 * x + y
        me = 4 * x + 2 * y + c
        sibling = (x, y, 1 - c)
        small_fetch = [pltpu.make_async_copy(src, dst, ssem_l.at[i]) for i, (src, dst) in enumerate(
            [(in_h, in_ref), (tail_h, tail_ref), (a_h, a_ref), (b_h, b_ref), (rg_h, rg_ref)])]
        for cp in small_fetch:
            cp.start()

        lands_hbm = [l_in, l_out, l_pg, l_pp]
        land_bufs = [lb_in, lb_out, lb_pg, lb_pp]
        own_bufs = [ob_in, ob_out, ob_pg, ob_pp]
        fetches = []
        for p in range(4):
            cp = pltpu.make_async_copy(lands_hbm[p], land_bufs[p], lsem.at[p])
            cp.start()
            fetches.append(cp)

        @pl.when(kme >= 2)
        def _():
            pltpu.make_async_copy(in23.at[kme & 1], ob_in, lsem.at[4]).start()

        @pl.when(kme < 2)
        def _():
            pltpu.make_async_copy(in01.at[kme & 1], ob_in, lsem.at[4]).start()

        own_fetches = [pltpu.make_async_copy(in01.at[0], ob_in, lsem.at[4])]
        for p, part in enumerate([pout, ppg, ppp]):
            hrows = own_bufs[p + 1].shape[0]
            cp = pltpu.make_async_copy(part.at[kme, pl.ds(pl.multiple_of(c * hrows, step), hrows)],
                                       own_bufs[p + 1], lsem.at[5 + p])
            cp.start()
            own_fetches.append(cp)

        def pair(ref, r0, r1):
            return jnp.concatenate([ref[r0:r0 + 1, :], ref[r1:r1 + 1, :]], axis=1)

        for cp in small_fetch:
            cp.wait()
        rows = {
            ROW_NORM_MIX: in_ref[0:1, :], ROW_FINAL_NORM: tail_ref[1:2, :], ROW_B_PG: tail_ref[2:3, :],
            ROW_PLE_NORM: tail_ref[3:4, :], ROW_CB_BA: pair(a_ref, 0, 1), ROW_BX_LAM: pair(a_ref, 2, 3),
            ROW_CW01: pair(a_ref, 4, 5), ROW_CW23: pair(a_ref, 6, 7), ROW_HG_LB: pair(b_ref, 2, 3),
            ROW_HG_NW: jnp.concatenate([b_ref[1:2, :], jnp.zeros((1, DG), F32)], axis=1),
            ROW_LOSS: tail_ref[4:5, :],
        }
        vbuf[...] = jnp.zeros_like(vbuf)
        for r, row in rows.items():
            for j in range(NDEV):
                vbuf[j, r:r + 1, :] = row[:, j * 128:(j + 1) * 128]

        def peer(mask):
            px = x ^ ((mask >> 2) & 1)
            py = y ^ ((mask >> 1) & 1)
            pc = c ^ (mask & 1)
            return (px, py, pc), 4 * px + 2 * py + pc

        def rg_rows(r):
            return pl.ds(pl.multiple_of(r * PR, PR), PR)

        first = []
        for mask in range(1, NDEV):
            dev, r = peer(mask)
            i = mask - 1
            cp = _remote(vbuf.at[r], vrecv.at[i], ssem.at[i], rsem.at[i], dev)
            cp.start()
            first.append(cp)
            cp = _remote(rg_ref.at[rg_rows(r)], rgrecv.at[i], ssem.at[7 + i], rsem.at[7 + i], dev)
            cp.start()
            first.append(cp)

        big = [(lb_in, g_in), (lb_out, g_out), (lb_pg, g_pg), (lb_pp, g_pp)]
        swaps = []
        for p, (land, gout) in enumerate(big):
            fetches[p].wait()
            own_fetches[p].wait()
            hrows = land.shape[1]
            mine0 = pl.multiple_of(c * hrows, step)
            slot = kme if p == 0 else me
            for r0 in range(0, hrows, step):
                rs = pl.ds(mine0 + r0, step)
                own = own_bufs[p][r0:r0 + step, :]
                s = jnp.zeros((step, land.shape[2]), F32)
                for j in range(land.shape[0]):
                    s = s + jnp.where(slot == j, own, land[j, r0:r0 + step, :]).astype(F32)
                gout[rs, :] = s
            mine = pl.ds(mine0, hrows)
            cp = _remote(gout.at[mine], gout.at[mine], bsem_s.at[p], bsem_r.at[p], sibling)
            cp.start()
            swaps.append(cp)

        sv = vbuf[me]
        sr = rg_ref[rg_rows(me), :]
        for i in range(NDEV - 1):
            first[2 * i].wait_recv()
            first[2 * i + 1].wait_recv()
            sv = sv + vrecv[i]
            sr = sr + rgrecv[i]
        v_out[me] = sv
        rg_out[rg_rows(me), :] = sr
        second = []
        for mask in range(1, NDEV):
            dev, r = peer(mask)
            i = mask - 1
            cp = _remote(v_out.at[me], v_out.at[me], ssem.at[14 + i], rsem.at[14 + i], dev)
            cp.start()
            second.append(cp)
            cp = _remote(rg_out.at[rg_rows(me)], rg_out.at[rg_rows(me)], ssem.at[21 + i], rsem.at[21 + i], dev)
            cp.start()
            second.append(cp)
        for p, (land, gout) in enumerate(big):
            hrows = land.shape[1]
            other = pl.ds(pl.multiple_of((1 - c) * hrows, step), hrows)
            _remote(gout.at[other], gout.at[other], bsem_s.at[p], bsem_r.at[p], sibling).wait_recv()
        for mask in range(1, NDEV):
            dev, r = peer(mask)
            i = mask - 1
            _remote(v_out.at[r], v_out.at[r], ssem.at[14 + i], rsem.at[14 + i], dev).wait_recv()
            _remote(rg_out.at[rg_rows(r)], rg_out.at[rg_rows(r)], ssem.at[21 + i], rsem.at[21 + i], dev).wait_recv()
        for cp in first + swaps + second:
            cp.wait_send()

    out_shape = [jax.ShapeDtypeStruct(s, F32) for s in shapes]
    out_shape += [jax.ShapeDtypeStruct((NDEV, VROWS, 128), F32), jax.ShapeDtypeStruct((DG, 128), F32)]
    outs = pl.pallas_call(
        body, name="final_reduce", out_shape=out_shape,
        in_specs=[HBM_SPEC] * 14, out_specs=[VMEM_SPEC] * 6,
        scratch_shapes=[pltpu.VMEM((NDEV, VROWS, 128), F32), pltpu.VMEM((NDEV - 1, VROWS, 128), F32),
                        pltpu.VMEM((NDEV - 1, PR, 128), F32)]
        + [pltpu.VMEM(l.shape, BF16) for l in lands]
        + [pltpu.VMEM(l.shape[1:], BF16) for l in lands]
        + [pltpu.VMEM(a.shape, F32) for a in (sm_in, sm_tail, sm_a, sm_b, rg_c)]
        + [pltpu.SemaphoreType.DMA((8,)), pltpu.SemaphoreType.DMA((5,)),
                        pltpu.SemaphoreType.DMA((4,)), pltpu.SemaphoreType.DMA((4,)),
                        pltpu.SemaphoreType.DMA((28,)), pltpu.SemaphoreType.DMA((28,))],
        compiler_params=pltpu.CompilerParams(vmem_limit_bytes=48 * MIB),
    )(*[pltpu.with_memory_space_constraint(a, pltpu.HBM)
        for a in (*parts, *lands, sm_in, sm_tail, sm_a, sm_b, rg_c)])
    return list(outs[0:4]), outs[4], outs[5]


def _adam_rows(w, g, m, v):
    m2 = ADAM_B1 * m + (1.0 - ADAM_B1) * g
    v2 = ADAM_B2 * v + (1.0 - ADAM_B2) * (g * g)
    m_hat = m2 / (1.0 - ADAM_B1 ** ADAM_STEP)
    v_hat = v2 / (1.0 - ADAM_B2 ** ADAM_STEP)
    delta = -ADAM_LR * (m_hat / (jnp.sqrt(v_hat) + ADAM_EPS) + ADAM_WD * w)
    return delta, m2, v2


def _adam_big(gs, ws, ms, vs):
    n = len(gs)
    steps = 8

    def body(*refs):
        ins, outs = refs[:4 * n], refs[4 * n:]
        for i in range(n):
            g, w, m, v = (r[...] for r in ins[4 * i:4 * i + 4])
            d, m2, v2 = _adam_rows(w, g, m, v)
            outs[3 * i][...] = d
            outs[3 * i + 1][...] = m2
            outs[3 * i + 2][...] = v2

    in_specs, out_specs, out_shape, args = [], [], [], []
    for g, w, m, v in zip(gs, ws, ms, vs):
        r, c = w.shape
        spec = lambda: pl.BlockSpec((r // steps, c), lambda i: (i, 0))
        in_specs += [spec() for _ in range(4)]
        out_specs += [spec() for _ in range(3)]
        out_shape += [jax.ShapeDtypeStruct((r, c), F32)] * 3
        args += [pltpu.with_memory_space_constraint(a, pltpu.HBM) for a in (g, w, m, v)]
    outs = pl.pallas_call(
        body, name="adam_big", grid=(steps,), in_specs=in_specs, out_specs=out_specs, out_shape=out_shape,
        compiler_params=_cparams(("parallel",), 32),
    )(*args)
    return [tuple(outs[3 * i:3 * i + 3]) for i in range(n)]


_VEC_PARAMS = [
    ("norm_mix_w", ROW_NORM_MIX, 0, D), ("final_norm_w", ROW_FINAL_NORM, 0, D),
    ("b_ple_gate", ROW_B_PG, 0, D), ("ple_norm_w", ROW_PLE_NORM, 0, D),
    ("conv_b", ROW_CB_BA, 0, DG), ("rg_ba", ROW_CB_BA, DG, DG),
    ("rg_bx", ROW_BX_LAM, 0, DG), ("rg_lambda", ROW_BX_LAM, DG, DG),
    ("hg_norm_w", ROW_HG_NW, 0, HD),
]
_SMALL_ORDER = [n for n, _, _, _ in _VEC_PARAMS] + ["hg_lb", "conv_w", "rg_wa", "rg_wx"]


def _adam_small(vred, rgred, ws, ms, vs):
    names = _SMALL_ORDER
    n = len(names)

    def body(vred_ref, rg_ref, *refs):
        w_refs = dict(zip(names, refs[0:n]))
        m_refs = dict(zip(names, refs[n:2 * n]))
        v_refs = dict(zip(names, refs[2 * n:3 * n]))
        outs = refs[3 * n:]
        o_refs = {nm: outs[4 * i:4 * i + 4] for i, nm in enumerate(names)}
        kme = 2 * lax.axis_index("x") + lax.axis_index("y")

        def update(nm, g, idx):
            d, m2, v2 = _adam_rows(w_refs[nm][idx], g, m_refs[nm][idx], v_refs[nm][idx])
            og, od, om, ov = o_refs[nm]
            og[idx] = g
            od[idx] = d
            om[idx] = m2
            ov[idx] = v2

        def packed(row, lane0, width):
            return jnp.concatenate([vred_ref[j, row:row + 1, :] for j in range(lane0 // 128, (lane0 + width) // 128)],
                                   axis=1)

        everything = (slice(None), slice(None))
        for nm, row, lane0, width in _VEC_PARAMS:
            update(nm, packed(row, lane0, width), everything)
        for r in range(2):
            update("hg_lb", packed(ROW_HG_LB, r * DG, DG), (slice(r, r + 1), slice(None)))
        for j in range(4):
            g = vred_ref[(j % 2) * 4 + kme, ROW_CW01 + j // 2:ROW_CW01 + j // 2 + 1, :]
            update("conv_w", g, (slice(j, j + 1), slice(None)))
        for r0 in range(0, DG, 128):
            rs = (slice(r0, r0 + 128), slice(None))
            both = rg_ref[r0:r0 + 128, :]
            update("rg_wa", both[:, 0:RGB], rs)
            update("rg_wx", pltpu.roll(both, RGB, 1)[:, 0:RGB], rs)

    args = [vred, rgred] + [d[nm] for d in (ws, ms, vs) for nm in names]
    out_shape = []
    for nm in names:
        out_shape += [jax.ShapeDtypeStruct(ws[nm].shape, F32)] * 4
    whole = lambda s: pl.BlockSpec(s.shape, lambda i, nd=len(s.shape): (0,) * nd)
    outs = pl.pallas_call(
        body, name="adam_small", out_shape=out_shape, grid=(1,),
        in_specs=[whole(a) for a in args], out_specs=[whole(s) for s in out_shape],
    )(*args)
    return {nm: tuple(outs[4 * i:4 * i + 4]) for i, nm in enumerate(names)}


def _fwd_inproj(x, nw, w_in_b, dep):
    tm = 512

    def body(x_ref, nw_ref, w_ref, dep_ref, pf_ref, pb_ref, u_ref):
        xv = x_ref[...]
        s = lax.rsqrt(jnp.mean(xv * xv, axis=-1, keepdims=True) + EPS)
        u = (xv * s * nw_ref[...]).astype(BF16)
        u_ref[...] = u
        r = [jnp.dot(u, w_ref[k], preferred_element_type=F32) for k in range(NSHARD)]
        h = DG // 2
        pf_ref[:, 0:DG] = r[0][:, 0:DG]
        pb_ref[:, 0:h] = r[0][:, DG:SHW].astype(BF16)
        pb_ref[:, h:DG] = r[1][:, 0:h].astype(BF16)
        pb_ref[:, DG:2 * DG] = r[1][:, h:SHW].astype(BF16)
        pf_ref[:, DG:2 * DG] = r[2][:, 0:DG]
        pb_ref[:, 2 * DG:2 * DG + h] = r[2][:, DG:SHW].astype(BF16)
        pb_ref[:, 2 * DG + h:3 * DG] = r[3][:, 0:h].astype(BF16)
        pb_ref[:, 3 * DG:4 * DG] = r[3][:, h:SHW].astype(BF16)

    return pl.pallas_call(
        body, name="fwd_inproj", grid=(T // tm,),
        in_specs=[pl.BlockSpec((tm, D), lambda i: (i, 0)),
                  pl.BlockSpec((1, D), lambda i: (0, 0)),
                  pl.BlockSpec((NSHARD, D, SHW), lambda i: (0, 0, 0), pipeline_mode=pl.Buffered(1)), ANY_SPEC],
        out_specs=[pl.BlockSpec((tm, 2 * DG), lambda i: (i, 0)),
                   pl.BlockSpec((tm, 4 * DG), lambda i: (i, 0)),
                   pl.BlockSpec((tm, D), lambda i: (i, 0))],
        out_shape=[jax.ShapeDtypeStruct((T, 2 * DG), F32), jax.ShapeDtypeStruct((T, 4 * DG), BF16),
                   jax.ShapeDtypeStruct((T, D), BF16)],
        compiler_params=_cparams(("parallel",), 48),
    )(x, nw, w_in_b, dep)


def _conv_rows(cw_ref):
    return [jnp.concatenate([cw_ref[k, j:j + 1, :] for k in range(NSHARD)], axis=1) for j in range(4)]


def _rg_conv(xa, prev8, cw, cb, rows):
    taps = [_shift_down(xa, prev8, 3, rows), _shift_down(xa, prev8, 2, rows),
            _shift_down(xa, prev8, 1, rows), xa]
    xc = cb
    for j in range(4):
        xc = xc + taps[j] * cw[j]
    return xc, taps


def _block_mask():
    r = lax.broadcasted_iota(I32, (DG, DG), 0)
    c = lax.broadcasted_iota(I32, (DG, DG), 1)
    return (r >> 6) == (c >> 6)


def _dense_from_blocks(wc):
    j = lax.broadcasted_iota(I32, (RGB, DG), 0)
    c = lax.broadcasted_iota(I32, (RGB, DG), 1)
    spread = _mm_exact(wc, ((c & (RGB - 1)) == j).astype(F32))
    return jnp.where(_block_mask(), spread, 0.0)


def _blocks_from_dense(da, dx):
    c = lax.broadcasted_iota(I32, (DG, 128), 0)
    j = lax.broadcasted_iota(I32, (DG, 128), 1)
    hit = (c & (RGB - 1)) == (j & (RGB - 1))
    mask = _block_mask()
    return (_mm_exact(jnp.where(mask, da, 0.0), (hit & (j < RGB)).astype(F32))
            + _mm_exact(jnp.where(mask, dx, 0.0), (hit & (j >= RGB)).astype(F32)))


def _rg_gates(xc, wa, ba, wx, bx, sp, first_row):
    r = _sigmoid(_mm(xc, wa) + ba)
    i = _sigmoid(_mm(xc, wx) + bx)
    log_a = (-RG_C) * r * sp
    a = jnp.exp(log_a)
    a2 = a * a
    one_m_a2 = -jnp.tanh(log_a) * (a2 + 1.0)
    mult = jnp.where(first_row, 1.0, jnp.sqrt(one_m_a2))
    return r, i, a, a2, mult


def _softplus(z):
    return jnp.maximum(z, 0.0) + jnp.log1p(jnp.exp(-jnp.abs(z)))


def _fwd_rglru(pf, pb, cw3, conv_b, wa_c, ba, wx_c, bx, lam):
    tm = 512
    ng = tm // 8

    def body(xa_ref, ga_ref, cw_ref, cb_ref, wa_ref, ba_ref, wx_ref, bx_ref, lam_ref,
             h_ref, ya_ref, a_s, u_s, tail_s, hc_s, wa_s, wx_s):
        i = pl.program_id(0)

        @pl.when(i == 0)
        def _():
            tail_s[...] = jnp.zeros_like(tail_s)
            hc_s[...] = jnp.zeros_like(hc_s)
            wa_s[...] = _dense_from_blocks(wa_ref[...]).astype(BF16)
            wx_s[...] = _dense_from_blocks(wx_ref[...]).astype(BF16)

        rows = lax.broadcasted_iota(I32, (tm, DG), 0)
        xa = xa_ref[...]
        xc, _ = _rg_conv(xa, tail_s[...], _conv_rows(cw_ref), cb_ref[...], rows)
        tail_s[...] = xa[tm - 8:tm, :]
        sp = _softplus(-lam_ref[...])
        rp = _mm(xc, wa_s[...]) + ba_ref[...]
        ip = _mm(xc, wx_s[...]) + bx_ref[...]
        rb = 64
        rows_b = lax.broadcasted_iota(I32, (rb, DG), 0)
        rows8 = rows_b & 7
        carry = hc_s[0:1, :]
        for b0 in range(0, tm, rb):
            sl = slice(b0, b0 + rb)
            r = _sigmoid(rp[sl])
            ig = _sigmoid(ip[sl])
            log_a = (-RG_C) * r * sp
            av = jnp.exp(log_a)
            mult = jnp.sqrt(-jnp.tanh(log_a) * (av * av + 1.0))
            if b0 == 0:
                mult = jnp.where((rows_b + i * tm) == 0, 1.0, mult)
            uv = mult * (ig * xc[sl])
            for d in (1, 2, 4):
                keep = rows8 >= d
                uv = uv + av * jnp.where(keep, _roll_in_groups(uv, d), 0.0)
                av = av * jnp.where(keep, _roll_in_groups(av, d), 1.0)
            ga = ga_ref[sl, :].astype(F32)
            gate = ga * _sigmoid(ga)
            hs = []
            for g in range(rb // 8):
                gs = slice(g * 8, (g + 1) * 8)
                hv = uv[gs] + av[gs] * carry
                carry = hv[7:8, :]
                hs.append(hv)
            hb = jnp.concatenate(hs, axis=0)
            h_ref[sl, :] = hb
            ya_ref[sl, :] = (hb * gate).astype(BF16)
        hc_s[0:1, :] = carry

    vec = lambda: pl.BlockSpec((1, DG), lambda i: (0, 0))
    blocks = lambda: pl.BlockSpec((DG, RGB), lambda i: (0, 0))
    return pl.pallas_call(
        body, name="fwd_rglru", grid=(T // tm,),
        in_specs=[pl.BlockSpec((tm, DG), lambda i: (i, 0)),
                  pl.BlockSpec((tm, DG), lambda i: (i, 0)),
                  pl.BlockSpec((NSHARD, 4, 128), lambda i: (0, 0, 0)), vec(),
                  blocks(), vec(), blocks(), vec(), vec()],
        out_specs=[pl.BlockSpec((tm, DG), lambda i: (i, 0)),
                   pl.BlockSpec((tm, DG), lambda i: (i, 0))],
        out_shape=[jax.ShapeDtypeStruct((T, DG), F32), pltpu.HBM((T, DG), BF16)],
        scratch_shapes=[pltpu.VMEM((tm, DG), F32), pltpu.VMEM((tm, DG), F32),
                        pltpu.VMEM((8, DG), F32), pltpu.VMEM((8, DG), F32),
                        pltpu.VMEM((DG, DG), BF16), pltpu.VMEM((DG, DG), BF16)],
        compiler_params=_cparams(("arbitrary",), 48),
    )(pf, pb, cw3, conv_b, wa_c, ba, wx_c, bx, lam)


def _hg_lower_bound(lb_ref):
    return _sig_pair(lb_ref[0:1, :] - lb_ref[1:2, :])


def _hg_gates(fz, lb, one_m_lb):
    sg, sn = _sig_pair(fz)
    f = lb + one_m_lb * sg
    return sg, sn, f, jnp.log(f), one_m_lb * sn


def _tri(lower):
    r = lax.broadcasted_iota(I32, (CH, CH), 0)
    c = lax.broadcasted_iota(I32, (CH, CH), 1)
    return (r >= c) if lower else (r <= c)


def _chunk_cumsum(v, rows64):
    for d in (1, 2, 4, 8, 16, 32):
        v = v + jnp.where(rows64 >= d, pltpu.roll(v, d, 0), 0.0)
    return v


def _chunk_rev_cumsum(v, rows64):
    n = v.shape[0]
    for d in (1, 2, 4, 8, 16, 32):
        v = v + jnp.where(rows64 < CH - d, pltpu.roll(v, n - d, 0), 0.0)
    return v


def _hg_recompute(q_ref, f_ref, lb, one_m_lb, rows64, eb_s, enb_s, ekd_s, qe_s, ke_s, kd_s, dec_s):
    nc = q_ref.shape[0] // CH
    sg, sn, f, logf, k = _hg_gates(f_ref[...], lb, one_m_lb)
    q = q_ref[...].astype(F32)
    sq = _sigmoid(q)
    qs = q * sq * (HD ** -0.5)
    b = _chunk_cumsum(logf, rows64)
    for c in range(nc):
        rs = slice(c * CH, (c + 1) * CH)
        b_c = b[rs]
        bl = b_c[CH - 1:CH, :]
        eb, enb, ekd = jnp.exp(b_c), jnp.exp(-b_c), jnp.exp(bl - b_c)
        if eb_s is not None:
            eb_s[rs, :] = eb
            enb_s[rs, :] = enb
            ekd_s[rs, :] = ekd
        qe_s[rs, :] = (qs[rs] * eb).astype(BF16)
        ke_s[rs, :] = (k[rs] * enb).astype(BF16)
        kd_s[rs, :] = (k[rs] * ekd).astype(BF16)
        dec_s[c:c + 1, :] = jnp.exp(bl)
    return sg, sn, f, k, q, sq, qs


def _fwd_hgrn2(pf, pb, hg_lb, hg_nw):
    tm = 512
    nc = tm // CH

    def body(q_ref, f_ref, v_ref, g_ref, lb_ref, nw_ref, yb_ref, o_ref, sp_ref,
             st_s, qe_s, ke_s, kd_s, vb_s, dec_s, p_s, ds_s):
        i = pl.program_id(0)

        @pl.when(i == 0)
        def _():
            st_s[...] = jnp.zeros_like(st_s)

        lb, one_m_lb = _hg_lower_bound(lb_ref)
        rows64 = lax.broadcasted_iota(I32, (tm, DG), 0) & (CH - 1)
        _hg_recompute(q_ref, f_ref, lb, one_m_lb, rows64, None, None, None, qe_s, ke_s, kd_s, dec_s)
        vb_s[...] = v_ref[...]
        mask = _tri(True)
        items = [(c, hd, slice(c * CH, (c + 1) * CH), slice(hd * HD, (hd + 1) * HD))
                 for c in range(nc) for hd in range(NH)]
        for c, hd, rs, cols in items:
            p_s[c * NH + hd] = jnp.where(mask, _mm_nt(qe_s[rs, cols], ke_s[rs, cols]), 0.0).astype(BF16)
            ds_s[c * NH + hd] = _mm_tn(vb_s[rs, cols], kd_s[rs, cols])
        for c, hd, rs, cols in items:
            st = st_s[hd]
            sp_ref[hd, c] = st
            st_s[hd] = st * dec_s[c:c + 1, cols] + ds_s[c * NH + hd]
        for c, hd, rs, cols in items:
            o_ref[rs, cols] = _mm(p_s[c * NH + hd], vb_s[rs, cols]) + _mm_nt(qe_s[rs, cols], sp_ref[hd, c])
        nw = nw_ref[...]
        for hd in range(NH):
            cols = slice(hd * HD, (hd + 1) * HD)
            o = o_ref[:, cols]
            so = lax.rsqrt(jnp.mean(o * o, axis=-1, keepdims=True) + EPS)
            g = g_ref[:, cols].astype(F32)
            sg = _sigmoid(g)
            yb_ref[:, cols] = (o * so * nw * (g * sg)).astype(BF16)

    col = lambda j: pl.BlockSpec((tm, DG), lambda i: (i, j))
    return pl.pallas_call(
        body, name="fwd_hgrn2", grid=(T // tm,),
        in_specs=[col(1), col(1), col(2), col(3),
                  pl.BlockSpec((2, DG), lambda i: (0, 0)),
                  pl.BlockSpec((1, HD), lambda i: (0, 0))],
        out_specs=[pl.BlockSpec((tm, DG), lambda i: (i, 0)),
                   pl.BlockSpec((tm, DG), lambda i: (i, 0)),
                   pl.BlockSpec((NH, nc, HD, HD), lambda i: (0, i, 0, 0))],
        out_shape=[pltpu.HBM((T, DG), BF16), jax.ShapeDtypeStruct((T, DG), F32),
                   jax.ShapeDtypeStruct((NH, NCHUNK, HD, HD), F32)],
        scratch_shapes=[pltpu.VMEM((NH, HD, HD), F32),
                        pltpu.VMEM((tm, DG), BF16), pltpu.VMEM((tm, DG), BF16), pltpu.VMEM((tm, DG), BF16),
                        pltpu.VMEM((tm, DG), BF16), pltpu.VMEM((nc, DG), F32),
                        pltpu.VMEM((nc * NH, CH, CH), BF16), pltpu.VMEM((nc * NH, HD, HD), F32)],
        compiler_params=_cparams(("arbitrary",), 48),
    )(pb, pf, pb, pb, hg_lb, hg_nw)


def _tail_fwd_bwd(x, p, tgt, ya, yb, w_out_b, w_pg_b, w_pp_b, ple_nw, b_pg, fnw):
    tm = 512
    nt = T // tm
    QR = D // NSHARD

    def body(x_ref, p_ref, t_ref, ya_ref, yb_ref, wo_ref, wg_ref, wp_ref, pw_ref, b_ref, fw_ref,
             dh1_ref, dyab_ref, dwo_ref, dwg_ref, dwp_ref, sm_ref, dwo_s, dwg_s, dwp_s):
        i = pl.program_id(0)

        @pl.when(i == 0)
        def _():
            dwo_s[...] = jnp.zeros_like(dwo_s)
            dwg_s[...] = jnp.zeros_like(dwg_s)
            dwp_s[...] = jnp.zeros_like(dwp_s)
            sm_ref[...] = jnp.zeros_like(sm_ref)

        ya = ya_ref[...]
        yb = yb_ref[...]
        pv = p_ref[...].astype(BF16)
        pw = pw_ref[...]
        fw = fw_ref[...]
        h1 = x_ref[...] + _mm(ya, wo_ref[0:DG, :]) + _mm(yb, wo_ref[DG:D, :])
        s2 = lax.rsqrt(jnp.mean(h1 * h1, axis=-1, keepdims=True) + EPS)
        n2h = h1 * s2
        n2 = (n2h * pw).astype(BF16)
        z = _mm(n2, wg_ref[...]) + b_ref[...]
        gate = _sigmoid(z)
        pp = jnp.concatenate([_mm(pv, wp_ref[k]) for k in range(NSHARD)], axis=1)
        h2 = h1 + gate * pp
        s3 = lax.rsqrt(jnp.mean(h2 * h2, axis=-1, keepdims=True) + EPS)
        hn = h2 * s3
        err = hn * fw - t_ref[...]
        sm_ref[0:1, :] += _rowsum(err * err)
        dy = err * (1.0 / D)
        sm_ref[1:2, :] += _rowsum(dy * hn)
        g3 = dy * fw
        dh2 = s3 * (g3 - hn * jnp.mean(g3 * hn, axis=-1, keepdims=True))
        dpp = (dh2 * gate).astype(BF16)
        dz = dh2 * pp * gate * (1.0 - gate)
        sm_ref[2:3, :] += _rowsum(dz)
        dzb = dz.astype(BF16)
        dwg_s[...] += _mm_tn(n2, dzb)
        dn2 = _mm_nt(dzb, wg_ref[...])
        for k in range(NSHARD):
            dwp_s[k] += _mm_tn(pv, dpp[:, k * PLE:(k + 1) * PLE])
        sm_ref[3:4, :] += _rowsum(dn2 * n2h)
        g2 = dn2 * pw
        dh1 = dh2 + s2 * (g2 - n2h * jnp.mean(g2 * n2h, axis=-1, keepdims=True))
        dh1_ref[...] = dh1
        dh1b = dh1.astype(BF16)
        dyab_ref[...] = _mm_nt(dh1b, wo_ref[...])
        dwo_s[0:DG, :] += _mm_tn(ya, dh1b)
        dwo_s[DG:D, :] += _mm_tn(yb, dh1b)

        @pl.when(i == nt - 1)
        def _():
            total = jnp.sum(sm_ref[0:1, :], axis=-1, keepdims=True) * (0.5 / D)
            sm_ref[4:5, :] = jnp.broadcast_to(total, (1, D))
            for k in range(NSHARD):
                dwo_ref[k] = dwo_s[k * QR:(k + 1) * QR, :].astype(BF16)
                dwg_ref[k] = dwg_s[k * QR:(k + 1) * QR, :].astype(BF16)
                dwp_ref[k] = dwp_s[k].astype(BF16)

    row = lambda w: pl.BlockSpec((tm, w), lambda i: (i, 0))
    const2 = lambda s: pl.BlockSpec(s, lambda i: (0, 0), pipeline_mode=pl.Buffered(1))
    const3 = lambda s: pl.BlockSpec(s, lambda i: (0, 0, 0), pipeline_mode=pl.Buffered(1))
    return pl.pallas_call(
        body, name="tail_fwd_bwd", grid=(nt,),
        in_specs=[row(D), row(PLE), row(D), row(DG), row(DG),
                  const2((D, D)), const2((D, D)), const3((NSHARD, PLE, PLE)),
                  const2((1, D)), const2((1, D)), const2((1, D))],
        out_specs=[row(D), row(D), const3((NSHARD, QR, D)), const3((NSHARD, QR, D)),
                   const3((NSHARD, PLE, PLE)), const2((8, D))],
        out_shape=[jax.ShapeDtypeStruct((T, D), F32), jax.ShapeDtypeStruct((T, D), F32),
                   jax.ShapeDtypeStruct((NSHARD, QR, D), BF16), jax.ShapeDtypeStruct((NSHARD, QR, D), BF16),
                   jax.ShapeDtypeStruct((NSHARD, PLE, PLE), BF16), jax.ShapeDtypeStruct((8, D), F32)],
        scratch_shapes=[pltpu.VMEM((D, D), F32), pltpu.VMEM((D, D), F32), pltpu.VMEM((NSHARD, PLE, PLE), F32)],
        compiler_params=_cparams(("arbitrary",), 62),
    )(x, p, tgt, pltpu.with_memory_space_constraint(ya, pltpu.HBM), pltpu.with_memory_space_constraint(yb, pltpu.HBM),
      w_out_b, w_pg_b, w_pp_b, ple_nw, b_pg, fnw)


def _bwd_rglru(pf, pb, h, dyab, cw3, conv_b, wa_c, ba, wx_c, bx, lam, dep):
    tm = 512
    nt = T // tm
    ng = tm // 8

    def body(xa_ref, ga_ref, xp_ref, h_ref, hp_ref, dya_ref, cw_ref, cb_ref, wa_ref, ba_ref, wx_ref, bx_ref,
             lam_ref, dep_ref, da_ref, rg_ref, sm_ref, a_s, g_s, cg_s, nxt_s, wa_s, wx_s, dwa_s, dwx_s):
        i = pl.program_id(0)
        tile = nt - 1 - i

        @pl.when(i == 0)
        def _():
            dwa_s[...] = jnp.zeros_like(dwa_s)
            dwx_s[...] = jnp.zeros_like(dwx_s)
            sm_ref[...] = jnp.zeros_like(sm_ref)
            cg_s[...] = jnp.zeros_like(cg_s)
            nxt_s[...] = jnp.zeros_like(nxt_s)
            wa_s[...] = _dense_from_blocks(wa_ref[...]).astype(BF16)
            wx_s[...] = _dense_from_blocks(wx_ref[...]).astype(BF16)

        rows = lax.broadcasted_iota(I32, (tm, DG), 0)
        has_prev = tile > 0
        xa = xa_ref[...]
        xprev = jnp.where(has_prev, xp_ref[...], 0.0)
        cw = _conv_rows(cw_ref)
        xc, taps = _rg_conv(xa, xprev, cw, cb_ref[...], rows)
        lam_v = lam_ref[...]
        sp = _softplus(-lam_v)
        first_row = (rows + tile * tm) == 0
        r, ig, a, a2, mult = _rg_gates(xc, wa_s[...], ba_ref[...], wx_s[...], bx_ref[...], sp, first_row)
        hv = h_ref[...]
        hprev = jnp.where(has_prev, hp_ref[...], 0.0)
        h_m1 = _shift_down(hv, hprev, 1, rows)
        ga = ga_ref[...].astype(F32)
        sg = _sigmoid(ga)
        dya = dya_ref[...]
        dga = dya * hv * (sg * (1.0 + ga * (1.0 - sg)))

        av = jnp.where(rows == tm - 1, 1.0, pltpu.roll(a, tm - 1, 0))
        gv = dya * (ga * sg)
        rows8 = rows & 7
        for d in (1, 2, 4):
            keep = rows8 < 8 - d
            gv = gv + av * jnp.where(keep, _roll_in_groups(gv, 8 - d), 0.0)
            av = av * jnp.where(keep, _roll_in_groups(av, 8 - d), 1.0)
        a_s[...] = av
        g_s[...] = gv
        carry = cg_s[0:1, :]
        for g in range(ng - 1, -1, -1):
            sl = slice(g * 8, (g + 1) * 8)
            ab, gb = a_s[sl, :], g_s[sl, :]
            g_s[sl, :] = gb + ab * carry
            carry = gb[0:1, :] + ab[0:1, :] * carry
        cg_s[0:1, :] = a[0:1, :] * carry

        gt = g_s[...]
        da = gt * h_m1
        ixc = ig * xc
        di = gt * mult * xc
        dxc = gt * mult * ig
        dlog_a = da * a + jnp.where(first_row, 0.0, gt * ixc * (-a2 / mult))
        sm_ref[3:4, :] += _rowsum(dlog_a * ((-RG_C) * r))
        dpr_f = dlog_a * ((-RG_C) * sp) * r * (1.0 - r)
        dpi_f = di * ig * (1.0 - ig)
        sm_ref[1:2, :] += _rowsum(dpr_f)
        sm_ref[2:3, :] += _rowsum(dpi_f)
        dpr = dpr_f.astype(BF16)
        dpi = dpi_f.astype(BF16)
        xcb = xc.astype(BF16)
        dwa_s[...] += _mm_tn(xcb, dpr)
        dwx_s[...] += _mm_tn(xcb, dpi)
        dxc = dxc + _mm_nt(dpr, wa_s[...]) + _mm_nt(dpi, wx_s[...])
        sm_ref[0:1, :] += _rowsum(dxc)
        for j in range(4):
            sm_ref[4 + j:5 + j, :] += _rowsum(dxc * taps[j])
        nxt = nxt_s[...]
        dxa = (dxc * cw[3] + _shift_up(dxc, nxt, 1, rows) * cw[2]
               + _shift_up(dxc, nxt, 2, rows) * cw[1] + _shift_up(dxc, nxt, 3, rows) * cw[0])
        nxt_s[...] = dxc[0:8, :]
        da_ref[:, 0:DG] = dxa.astype(BF16)
        da_ref[:, DG:D] = dga.astype(BF16)

        @pl.when(i == nt - 1)
        def _():
            _, s_neg = _sig_pair(lam_v)
            sm_ref[3:4, :] = sm_ref[3:4, :] * (-s_neg)
            rg_ref[...] = _blocks_from_dense(dwa_s[...], dwx_s[...])

    vec = lambda: pl.BlockSpec((1, DG), lambda i: (0, 0))
    blocks = lambda: pl.BlockSpec((DG, RGB), lambda i: (0, 0))
    prev8 = lambda: pl.BlockSpec((8, DG), lambda i: (jnp.maximum((nt - 1 - i) * (tm // 8) - 1, 0), 0))
    return pl.pallas_call(
        body, name="bwd_rglru", grid=(nt,),
        in_specs=[pl.BlockSpec((tm, DG), lambda i: (nt - 1 - i, 0)),
                  pl.BlockSpec((tm, DG), lambda i: (nt - 1 - i, 0)),
                  prev8(),
                  pl.BlockSpec((tm, DG), lambda i: (nt - 1 - i, 0)),
                  prev8(),
                  pl.BlockSpec((tm, DG), lambda i: (nt - 1 - i, 0)),
                  pl.BlockSpec((NSHARD, 4, 128), lambda i: (0, 0, 0)), vec(),
                  blocks(), vec(), blocks(), vec(), vec(), ANY_SPEC],
        out_specs=[pl.BlockSpec((tm, D), lambda i: (nt - 1 - i, 0)),
                   pl.BlockSpec((DG, 128), lambda i: (0, 0)),
                   pl.BlockSpec((8, DG), lambda i: (0, 0))],
        out_shape=[jax.ShapeDtypeStruct((T, D), BF16), jax.ShapeDtypeStruct((DG, 128), F32),
                   jax.ShapeDtypeStruct((8, DG), F32)],
        scratch_shapes=[pltpu.VMEM((tm, DG), F32), pltpu.VMEM((tm, DG), F32),
                        pltpu.VMEM((8, DG), F32), pltpu.VMEM((8, DG), F32),
                        pltpu.VMEM((DG, DG), BF16), pltpu.VMEM((DG, DG), BF16),
                        pltpu.VMEM((DG, DG), F32), pltpu.VMEM((DG, DG), F32)],
        compiler_params=_cparams(("arbitrary",), 56),
    )(pf, pb, pf, h, h, dyab, cw3, conv_b, wa_c, ba, wx_c, bx, lam, dep)


def _bwd_hgrn2(pf, pb, o, s_prev, dyab, hg_lb, hg_nw, dep):
    tm = 512
    nt = T // tm
    nc = tm // CH

    def body(q_ref, f_ref, v_ref, g_ref, o_ref, sp_ref, dy_ref, lb_ref, nw_ref, dep_ref, db_ref, sm_ref,
             dst_s, eb_s, enb_s, ekd_s, qe_s, ke_s, kd_s, vb_s, do_s, dec_s, ddec_s, p_s, dp_s,
             g_s, dsta_s, dva_s, dqe_s, dke_s, dkd_s, dlf_s):
        i = pl.program_id(0)

        @pl.when(i == 0)
        def _():
            sm_ref[...] = jnp.zeros_like(sm_ref)
            dst_s[...] = jnp.zeros_like(dst_s)

        lb, one_m_lb = _hg_lower_bound(lb_ref)
        rows64 = lax.broadcasted_iota(I32, (tm, DG), 0) & (CH - 1)
        sg, sn, f, k, q, sq, qs = _hg_recompute(
            q_ref, f_ref, lb, one_m_lb, rows64, eb_s, enb_s, ekd_s, qe_s, ke_s, kd_s, dec_s)
        vb_s[...] = v_ref[...]

        nw = nw_ref[...]
        for hd in range(NH):
            cols = slice(hd * HD, (hd + 1) * HD)
            g = g_ref[:, cols].astype(F32)
            sgg = _sigmoid(g)
            o = o_ref[:, cols]
            so = lax.rsqrt(jnp.mean(o * o, axis=-1, keepdims=True) + EPS)
            oh = o * so
            dyb = dy_ref[:, cols]
            db_ref[:, 3 * DG + hd * HD:3 * DG + (hd + 1) * HD] = (
                dyb * (oh * nw) * (sgg * (1.0 + g * (1.0 - sgg)))).astype(BF16)
            don = dyb * (g * sgg)
            sm_ref[1:2, 0:HD] += _rowsum(don * oh)
            gw = don * nw
            do_s[:, cols] = (so * (gw - oh * jnp.mean(gw * oh, axis=-1, keepdims=True))).astype(BF16)

        mask = _tri(True)
        items = [(c, hd, slice(c * CH, (c + 1) * CH), slice(hd * HD, (hd + 1) * HD))
                 for c in range(nc) for hd in range(NH)]
        for c, hd, rs, cols in items:
            p_s[c * NH + hd] = jnp.where(mask, _mm_nt(qe_s[rs, cols], ke_s[rs, cols]), 0.0).astype(BF16)
            dp_s[c * NH + hd] = jnp.where(mask, _mm_nt(do_s[rs, cols], vb_s[rs, cols]), 0.0).astype(BF16)
        for c, hd, rs, cols in items:
            n = c * NH + hd
            dva_s[rs, cols] = _mm_tn(p_s[n], do_s[rs, cols])
            dqe_s[rs, cols] = _mm(dp_s[n], ke_s[rs, cols])
            dke_s[rs, cols] = _mm_tn(dp_s[n], qe_s[rs, cols])
            g_s[n] = _mm_tn(do_s[rs, cols], qe_s[rs, cols])
        for c, hd, rs, cols in reversed(items):
            n = c * NH + hd
            dst = dst_s[hd]
            dsta_s[n] = dst
            dst_s[hd] = dst * dec_s[c:c + 1, cols] + g_s[n]
        for c, hd, rs, cols in items:
            n = c * NH + hd
            dst = dsta_s[n]
            st_prev = sp_ref[hd, c]
            dv = dva_s[rs, cols] + _mm_nt(kd_s[rs, cols], dst)
            db_ref[rs, 2 * DG + hd * HD:2 * DG + (hd + 1) * HD] = dv.astype(BF16)
            dqe_s[rs, cols] += _mm(do_s[rs, cols], st_prev)
            dkd_s[rs, cols] = _mm(vb_s[rs, cols], dst)
            ddec_s[c:c + 1, cols] = _rowsum(dst * st_prev)

        eb, enb, ekd = eb_s[...], enb_s[...], ekd_s[...]
        dqe, dke, dkd = dqe_s[...], dke_s[...], dkd_s[...]
        t_kd = dkd * (k * ekd)
        rc = _chunk_rev_cumsum(dqe * (qs * eb) - dke * (k * enb) - t_kd, rows64)
        for c in range(nc):
            rs = slice(c * CH, (c + 1) * CH)
            dbl = _rowsum(t_kd[rs]) + ddec_s[c:c + 1, :] * dec_s[c:c + 1, :]
            dlf_s[rs, :] = rc[rs] + dbl
        t = dlf_s[...] / f - (dke * enb + dkd * ekd)
        db_ref[:, DG:2 * DG] = (one_m_lb * sg * sn * t).astype(BF16)
        sm_ref[0:1, :] += _rowsum(sn * t)
        db_ref[:, 0:DG] = (dqe * eb * (sq * (1.0 + q * (1.0 - sq))) * (HD ** -0.5)).astype(BF16)

        @pl.when(i == nt - 1)
        def _():
            dsm = sm_ref[0:1, :] * (lb * one_m_lb)
            sm_ref[2:3, :] = dsm
            sm_ref[3:4, :] = -dsm

    col = lambda j: pl.BlockSpec((tm, DG), lambda i: (nt - 1 - i, j))
    big = lambda dt: pltpu.VMEM((tm, DG), dt)
    return pl.pallas_call(
        body, name="bwd_hgrn2", grid=(nt,),
        in_specs=[col(1), col(1), col(2), col(3),
                  pl.BlockSpec((tm, DG), lambda i: (nt - 1 - i, 0)),
                  pl.BlockSpec((NH, nc, HD, HD), lambda i: (0, nt - 1 - i, 0, 0)),
                  pl.BlockSpec((tm, DG), lambda i: (nt - 1 - i, 1)),
                  pl.BlockSpec((2, DG), lambda i: (0, 0)),
                  pl.BlockSpec((1, HD), lambda i: (0, 0)), ANY_SPEC],
        out_specs=[pl.BlockSpec((tm, 4 * DG), lambda i: (nt - 1 - i, 0)),
                   pl.BlockSpec((8, DG), lambda i: (0, 0))],
        out_shape=[jax.ShapeDtypeStruct((T, 4 * DG), BF16), jax.ShapeDtypeStruct((8, DG), F32)],
        scratch_shapes=[pltpu.VMEM((NH, HD, HD), F32),
                        big(F32), big(F32), big(F32),
                        big(BF16), big(BF16), big(BF16), big(BF16), big(BF16),
                        pltpu.VMEM((nc, DG), F32), pltpu.VMEM((nc, DG), F32),
                        pltpu.VMEM((nc * NH, CH, CH), BF16), pltpu.VMEM((nc * NH, CH, CH), BF16),
                        pltpu.VMEM((nc * NH, HD, HD), F32), pltpu.VMEM((nc * NH, HD, HD), F32),
                        big(F32), big(F32), big(F32), big(F32), big(F32)],
        compiler_params=_cparams(("arbitrary",), 56),
    )(pb, pf, pb, pb, o, s_prev, dyab, hg_lb, hg_nw, dep)


def _dproj_pieces(k, da_ref, db_ref):
    if k == 0:
        return [(da_ref[:, 0:SHW], 0)]
    if k == 1:
        return [(da_ref[:, SHW:D], 0), (db_ref[:, 0:DG], D - SHW)]
    if k == 2:
        return [(db_ref[:, DG:DG + SHW], 0)]
    return [(db_ref[:, DG + SHW:4 * DG], 0)]


def _bwd_inproj_dx(x, dh1, d_a, d_b, w_in_b, nw, dep):
    tm = 512
    nt = T // tm

    def body(x_ref, dh1_ref, da_ref, db_ref, w_ref, nw_ref, dep_ref, dx_ref, sm_ref):
        i = pl.program_id(0)

        @pl.when(i == 0)
        def _():
            sm_ref[...] = jnp.zeros_like(sm_ref)

        du = None
        for k in range(NSHARD):
            for val, off in _dproj_pieces(k, da_ref, db_ref):
                t = _mm_nt(val, w_ref[k, :, off:off + val.shape[1]])
                du = t if du is None else du + t
        xv = x_ref[...]
        s = lax.rsqrt(jnp.mean(xv * xv, axis=-1, keepdims=True) + EPS)
        xh = xv * s
        sm_ref[0:1, :] += _rowsum(du * xh)
        g = du * nw_ref[...]
        dx_ref[...] = dh1_ref[...] + s * (g - xh * jnp.mean(g * xh, axis=-1, keepdims=True))

    row = lambda w: pl.BlockSpec((tm, w), lambda i: (i, 0))
    return pl.pallas_call(
        body, name="bwd_inproj_dx", grid=(nt,),
        in_specs=[row(D), row(D), row(D), row(4 * DG),
                  pl.BlockSpec((NSHARD, D, SHW), lambda i: (0, 0, 0), pipeline_mode=pl.Buffered(1)),
                  pl.BlockSpec((1, D), lambda i: (0, 0)), ANY_SPEC],
        out_specs=[row(D), pl.BlockSpec((8, D), lambda i: (0, 0))],
        out_shape=[jax.ShapeDtypeStruct((T, D), F32), jax.ShapeDtypeStruct((8, D), F32)],
        compiler_params=_cparams(("arbitrary",), 56),
    )(x, dh1, d_a, d_b, w_in_b, nw, dep)


def _bwd_inproj_dw(name, ks, u_b, d_a, d_b):
    tm = 1024
    nt = T // tm
    nk = len(ks)
    H = D // 2
    step = 128
    need_a = 0 in ks or 1 in ks

    def body(*refs):
        if need_a:
            u_ref, da_ref, db_ref, dw_ref, acc, send_s, recv_s, ssem, rsem = refs
        else:
            u_ref, db_ref, dw_ref, acc, send_s, recv_s, ssem, rsem = refs
            da_ref = None
        i = pl.program_id(0)

        @pl.when(i == 0)
        def _():
            acc[...] = jnp.zeros_like(acc)

        u = u_ref[...]
        for j, k in enumerate(ks):
            for val, off in _dproj_pieces(k, da_ref, db_ref):
                acc[j, :, off:off + val.shape[1]] += _mm_tn(u, val)

        @pl.when(i == nt - 1)
        def _():
            x, y, c = lax.axis_index("x"), lax.axis_index("y"), lax.axis_index("c")
            sibling = (x, y, 1 - c)
            mine0 = pl.multiple_of(c * H, step)
            other0 = pl.multiple_of((1 - c) * H, step)
            copies = []
            for j in range(nk):
                for r0 in range(0, H, step):
                    send_s[j, r0:r0 + step, :] = acc[j, pl.ds(other0 + r0, step), :].astype(BF16)
                cp = _remote(send_s.at[j], recv_s.at[j], ssem.at[j], rsem.at[j], sibling)
                cp.start()
                copies.append(cp)
            for j in range(nk):
                copies[j].wait_recv()
                for r0 in range(0, H, step):
                    s = acc[j, pl.ds(mine0 + r0, step), :] + recv_s[j, r0:r0 + step, :].astype(F32)
                    dw_ref[j, r0:r0 + step, :] = s.astype(BF16)
            for cp in copies:
                cp.wait_send()

    row = lambda w: pl.BlockSpec((tm, w), lambda i: (i, 0))
    ins = [u_b] + ([d_a] if need_a else []) + [d_b]
    in_specs = [row(D)] + ([row(D)] if need_a else []) + [row(4 * DG)]
    return pl.pallas_call(
        body, name=name, grid=(nt,), in_specs=in_specs,
        out_specs=pl.BlockSpec((nk, H, SHW), lambda i: (0, 0, 0)),
        out_shape=jax.ShapeDtypeStruct((nk, H, SHW), BF16),
        scratch_shapes=[pltpu.VMEM((nk, D, SHW), F32), pltpu.VMEM((nk, H, SHW), BF16),
                        pltpu.VMEM((nk, H, SHW), BF16),
                        pltpu.SemaphoreType.DMA((nk,)), pltpu.SemaphoreType.DMA((nk,))],
        compiler_params=_cparams(("arbitrary",), 48),
    )(*ins)


_OUT_ORDER = ["norm_mix_w", "w_in", "conv_w", "conv_b", "rg_wa", "rg_ba", "rg_wx", "rg_bx", "rg_lambda", "hg_lb",
              "hg_norm_w", "w_out", "ple_norm_w", "w_ple_gate", "b_ple_gate", "w_ple_proj", "final_norm_w"]
_BIG = ["w_in", "w_out", "w_ple_gate", "w_ple_proj"]


def _small_view(name, a):
    if name in ("rg_wa", "rg_wx"):
        return a.reshape(DG, RGB)
    if name == "conv_w":
        return a.reshape(4, 128)
    if name == "final_norm_w":
        return a.reshape(1, D)
    return a


def _landing(rows, cols):
    return lax.empty((NDEV, rows, cols), BF16)


def kernel(x, p, norm_mix_w, w_in, conv_w, conv_b, rg_wa, rg_ba, rg_wx, rg_bx, rg_lambda, hg_lb, hg_norm_w, w_out, ple_norm_w, w_ple_gate, b_ple_gate, w_ple_proj, final_norm_w, loss_target, m_norm_mix_w, m_w_in, m_conv_w, m_conv_b, m_rg_wa, m_rg_ba, m_rg_wx, m_rg_bx, m_rg_lambda, m_hg_lb, m_hg_norm_w, m_w_out, m_ple_norm_w, m_w_ple_gate, m_b_ple_gate, m_w_ple_proj, m_final_norm_w, v_norm_mix_w, v_w_in, v_conv_w, v_conv_b, v_rg_wa, v_rg_ba, v_rg_wx, v_rg_bx, v_rg_lambda, v_hg_lb, v_hg_norm_w, v_w_out, v_ple_norm_w, v_w_ple_gate, v_b_ple_gate, v_w_ple_proj, v_final_norm_w):
    given = dict(locals())
    x2, p2, tgt = x[0], p[0, 0], loss_target[0]
    hbm = lambda a: pltpu.with_memory_space_constraint(a, pltpu.HBM)
    norm_mix_w, conv_b, rg_ba, rg_bx, rg_lambda, hg_lb, hg_norm_w, ple_norm_w, b_ple_gate = (
        hbm(a) for a in (norm_mix_w, conv_b, rg_ba, rg_bx, rg_lambda, hg_lb, hg_norm_w, ple_norm_w, b_ple_gate))
    wa_c, wx_c = hbm(_small_view("rg_wa", rg_wa)), hbm(_small_view("rg_wx", rg_wx))

    w_in_b, l_out, l_pg, l_pp, cw3 = _gather_w_in(w_in[0], w_out[0], w_ple_gate[0], w_ple_proj[0], conv_w[0])
    g_ssem, g_rsem, g_lands, tok = _gather_rest_start([l_out, l_pg, l_pp])

    pf, pb, u_b = _fwd_inproj(x2, norm_mix_w, w_in_b, tok)
    h, ya = _fwd_rglru(pf, pb, cw3, conv_b, wa_c, rg_ba, wx_c, rg_bx, rg_lambda)
    yb, o, s_prev = _fwd_hgrn2(pf, pb, hg_lb, hg_norm_w)
    w_out_b, w_pg_b, w_pp_b = _gather_rest_wait(g_ssem, g_rsem, g_lands, yb)
    dh1, dyab, dwo_b, dwg_b, dwp_b, sm_tail = _tail_fwd_bwd(
        x2, p2, tgt, ya, yb, w_out_b.reshape(D, D), w_pg_b.reshape(D, D), w_pp_b,
        ple_norm_w, b_ple_gate, final_norm_w.reshape(1, D))

    QH = D // NSHARD // 2
    r1 = _rs_start("rs_start_tail", [dwo_b, dwg_b, dwp_b], (0, 1, 2, 3),
                   [_landing(QH, D), _landing(QH, D), _landing(PLE // 2, PLE)])
    d_b, sm_b = _bwd_hgrn2(pf, pb, o, s_prev, dyab, hg_lb, hg_norm_w, r1[4])
    dw23 = _bwd_inproj_dw("bwd_inproj_dw23", (2, 3), u_b, None, d_b)
    r2 = _rs_start("rs_start_in23", [dw23], (2, 3), [lax.empty((NSHARD, D // 2, SHW), BF16)], chip_sums=True)
    d_a, rg_c, sm_a = _bwd_rglru(pf, pb, h, dyab, cw3, conv_b, wa_c, rg_ba, wx_c, rg_bx, rg_lambda, r2[4])
    dw01 = _bwd_inproj_dw("bwd_inproj_dw01", (0, 1), u_b, d_a, d_b)
    r3 = _rs_start("rs_start_in01", [dw01], (0, 1), r2[3], chip_sums=True)
    grad_x, sm_in = _bwd_inproj_dx(x2, dh1, d_a, d_b, w_in_b, norm_mix_w, r3[4])

    parts1, lands1 = _rs_wait("rs_wait_tail", r1[0], r1[1], r1[2], (0, 1, 2, 3), r1[3], sm_in)
    parts2, lands2 = _rs_wait("rs_wait_in23", r2[0], r2[1], r2[2], (2, 3), r3[3], sm_in, chip_sums=True)
    parts3, lands3 = _rs_wait("rs_wait_in01", r3[0], r3[1], r3[2], (0, 1), lands2, sm_in, chip_sums=True)
    g_big, vred, rgred = _final_reduce([parts3[0], parts2[0]] + parts1, lands3 + lands1,
                                       sm_in, sm_tail, sm_a, sm_b, rg_c)

    upd_big = _adam_big(g_big, [given[n][0] for n in _BIG], [given["m_" + n][0] for n in _BIG],
                        [given["v_" + n][0] for n in _BIG])
    small = _adam_small(vred, rgred,
                        {n: _small_view(n, given[n]) for n in _SMALL_ORDER},
                        {n: _small_view(n, given["m_" + n]) for n in _SMALL_ORDER},
                        {n: _small_view(n, given["v_" + n]) for n in _SMALL_ORDER})

    loss = vred[0, ROW_LOSS, 0]
    outs = [loss, grad_x[None]]
    for ki in range(4):
        for n in _OUT_ORDER:
            if n in _BIG:
                i = _BIG.index(n)
                a = g_big[i] if ki == 0 else upd_big[i][ki - 1]
                outs.append(a[None])
            else:
                outs.append(small[n][ki].reshape(given[n].shape))
    return tuple(outs)
```

```python
import jax
import jax.numpy as jnp
from jax import lax
from jax.experimental import pallas as pl
from jax.experimental.pallas import tpu as pltpu

F32 = jnp.float32
BF16 = jnp.bfloat16
I32 = jnp.int32
MESH = pl.DeviceIdType.MESH
HIGHEST = lax.Precision.HIGHEST

T = 4096
D = 1024
DG = 512
DIN = 3072
PLE = 256
NH = 4
HD = 128
CH = 64
NCHUNK = T // CH
RGB = 64
EPS = 1e-6
RG_C = 8.0
NSHARD = 4
SHW = DIN // NSHARD
NDEV = 8

ADAM_LR = 0.001
ADAM_B1 = 0.9
ADAM_B2 = 0.999
ADAM_EPS = 1e-08
ADAM_WD = 0.01
ADAM_STEP = 10

VMEM_SPEC = pl.BlockSpec(memory_space=pltpu.VMEM)
HBM_SPEC = pl.BlockSpec(memory_space=pltpu.HBM)
SEM_SPEC = pl.BlockSpec(memory_space=pltpu.SEMAPHORE)
ANY_SPEC = pl.BlockSpec(memory_space=pl.ANY)
EFFECT = pltpu.SideEffectType.DATAFLOW_SIDE_EFFECTING
MIB = 1024 * 1024

VROWS = 16
ROW_NORM_MIX, ROW_FINAL_NORM, ROW_B_PG, ROW_PLE_NORM = 0, 1, 2, 3
ROW_CB_BA, ROW_BX_LAM, ROW_CW01, ROW_CW23, ROW_HG_LB, ROW_HG_NW, ROW_LOSS = 4, 5, 6, 7, 8, 9, 10


def _mm(a, b):
    return jnp.dot(a.astype(BF16), b.astype(BF16), preferred_element_type=F32)


def _mm_nt(a, b):
    return lax.dot_general(a.astype(BF16), b.astype(BF16), (((1,), (1,)), ((), ())),
                           preferred_element_type=F32)


def _mm_tn(a, b):
    return lax.dot_general(a.astype(BF16), b.astype(BF16), (((0,), (0,)), ((), ())),
                           preferred_element_type=F32)


def _mm_exact(a, b):
    return jnp.dot(a, b, precision=HIGHEST, preferred_element_type=F32)


def _sig_pair(x):
    e = jnp.exp(-jnp.abs(x))
    big = 1.0 / (1.0 + e)
    small = e * big
    pos = x >= 0
    return jnp.where(pos, big, small), jnp.where(pos, small, big)


def _sigmoid(x):
    return 1.0 / (1.0 + jnp.exp(-x))


def _rowsum(v):
    return jnp.sum(v, axis=0, keepdims=True)


def _shift_down(cur, prev8, d, rows):
    rolled = pltpu.roll(cur, d, 0)
    head = jnp.where(rows[0:8] < d, pltpu.roll(prev8, d, 0), rolled[0:8])
    return jnp.concatenate([head, rolled[8:]], axis=0)


def _shift_up(cur, next8, d, rows):
    n = cur.shape[0]
    rolled = pltpu.roll(cur, n - d, 0)
    tail = jnp.where(rows[0:8] >= 8 - d, pltpu.roll(next8, 8 - d, 0), rolled[n - 8:n])
    return jnp.concatenate([rolled[0:n - 8], tail], axis=0)


def _roll_in_groups(v, d):
    n, w = v.shape
    return pltpu.roll(v.reshape(n // 8, 8, w), d, 1).reshape(n, w)


def _cparams(sem, vmem_mib):
    return pltpu.CompilerParams(dimension_semantics=sem, vmem_limit_bytes=vmem_mib * MIB)


def _mesh_pos():
    x, y, c = lax.axis_index("x"), lax.axis_index("y"), lax.axis_index("c")
    chips = [(1 - x, y), (x, 1 - y), (1 - x, 1 - y)]
    return x, y, c, chips


def _remote(src, dst, ssem, rsem, dev):
    return pltpu.make_async_remote_copy(src_ref=src, dst_ref=dst, send_sem=ssem, recv_sem=rsem,
                                        device_id=dev, device_id_type=MESH)


def _gather_w_in(w_in, w_out, w_pg, w_pp, conv_w):
    shapes = [w_in.shape, w_out.shape, w_pg.shape, w_pp.shape]

    def body(win_h, wout_h, wpg_h, wpp_h, cw_h, o_in, o_out, o_pg, o_pp, o_cw,
             win, wout, wpg, wpp, cw, lsem, ssem, rsem):
        x, y, c, chips = _mesh_pos()
        kme = 2 * x + y
        sibling = (x, y, 1 - c)
        fetch = [pltpu.make_async_copy(src, dst, lsem.at[i])
                 for i, (src, dst) in enumerate([(win_h, win), (cw_h, cw), (wout_h, wout), (wpg_h, wpg), (wpp_h, wpp)])]
        for cp in fetch:
            cp.start()

        def cast(src, dst):
            for r0 in range(0, src.shape[0], 128):
                dst[kme, r0:r0 + 128, :] = src[r0:r0 + 128, :].astype(BF16)

        fetch[0].wait()
        cast(win, o_in)
        fetch[1].wait()
        o_cw[kme] = cw[...]

        hrows = D // 2
        mine = pl.ds(pl.multiple_of(c * hrows, 128), hrows)
        other = pl.ds(pl.multiple_of((1 - c) * hrows, 128), hrows)

        def half(k, rows):
            return o_in.at[k, rows]

        na = (x ^ c, y ^ (1 - c), c)
        nb = (x ^ (1 - c), y ^ c, c)
        ka = 2 * na[0] + na[1]
        kb = 2 * nb[0] + nb[1]
        kd = 2 * (1 - x) + (1 - y)
        sends = [_remote(half(kme, mine), half(kme, mine), ssem.at[0], rsem.at[0], na),
                 _remote(half(kme, mine), half(kme, mine), ssem.at[1], rsem.at[1], nb)]
        for j, (px, py) in enumerate(chips):
            sends.append(_remote(o_cw.at[kme], o_cw.at[kme], ssem.at[3 + j], rsem.at[3 + j], (px, py, c)))
        for cp in sends:
            cp.start()
        for i, (src, dst) in enumerate([(wout, o_out), (wpg, o_pg), (wpp, o_pp)]):
            fetch[2 + i].wait()
            cast(src, dst)

        def passed_on(k, sem, dev):
            cp = _remote(half(k, mine), half(k, mine), ssem.at[sem], rsem.at[sem], dev)
            cp.start()
            sends.append(cp)

        _remote(half(ka, mine), half(ka, mine), ssem.at[0], rsem.at[0], na).wait_recv()
        passed_on(ka, 2, nb)
        passed_on(ka, 6, sibling)
        _remote(half(kb, mine), half(kb, mine), ssem.at[1], rsem.at[1], nb).wait_recv()
        passed_on(kb, 7, sibling)
        _remote(half(kd, mine), half(kd, mine), ssem.at[2], rsem.at[2], nb).wait_recv()
        passed_on(kd, 8, sibling)
        _remote(half(kb, other), half(kb, other), ssem.at[6], rsem.at[6], sibling).wait_recv()
        _remote(half(ka, other), half(ka, other), ssem.at[7], rsem.at[7], sibling).wait_recv()
        _remote(half(kd, other), half(kd, other), ssem.at[8], rsem.at[8], sibling).wait_recv()
        for j, (px, py) in enumerate(chips):
            kj = 2 * px + py
            _remote(o_cw.at[kj], o_cw.at[kj], ssem.at[3 + j], rsem.at[3 + j], (px, py, c)).wait_recv()
        for cp in sends:
            cp.wait_send()

    out_shape = [jax.ShapeDtypeStruct((NSHARD,) + s, BF16) for s in shapes]
    out_shape.append(jax.ShapeDtypeStruct((NSHARD,) + conv_w.shape, F32))
    return pl.pallas_call(
        body, name="gather_w_in", out_shape=out_shape,
        in_specs=[HBM_SPEC] * 5, out_specs=[VMEM_SPEC] * 5,
        scratch_shapes=[pltpu.VMEM(a.shape, F32) for a in (w_in, w_out, w_pg, w_pp, conv_w)]
        + [pltpu.SemaphoreType.DMA((5,)), pltpu.SemaphoreType.DMA((9,)), pltpu.SemaphoreType.DMA((9,))],
        compiler_params=pltpu.CompilerParams(vmem_limit_bytes=40 * MIB),
    )(*[pltpu.with_memory_space_constraint(a, pltpu.HBM) for a in (w_in, w_out, w_pg, w_pp, conv_w)])


def _gather_inproj(x, nw, w_in, w_out, w_pg, w_pp, conv_w):
    shapes = [w_in.shape, w_out.shape, w_pg.shape, w_pp.shape]
    tm = 512
    nt = T // tm

    def body(x_h, nw_h, win_h, wout_h, wpg_h, wpp_h, cw_h,
             o_in, o_out, o_pg, o_pp, o_cw, pf_h, pb_h, u_h,
             win, wout, wpg, wpp, cw, nw_s, xbuf, ubuf, rfbuf, rbbuf,
             lsem, xsem, usem, fsem, bsem, ssem, rsem):
        x, y, c, chips = _mesh_pos()
        kme = 2 * x + y
        sibling = (x, y, 1 - c)
        fetch = [pltpu.make_async_copy(src, dst, lsem.at[i]) for i, (src, dst) in enumerate(
            [(win_h, win), (cw_h, cw), (nw_h, nw_s), (wout_h, wout), (wpg_h, wpg), (wpp_h, wpp)])]
        for cp in fetch:
            cp.start()

        def cast(src, dst):
            for r0 in range(0, src.shape[0], 128):
                dst[kme, r0:r0 + 128, :] = src[r0:r0 + 128, :].astype(BF16)

        fetch[0].wait()
        cast(win, o_in)
        fetch[1].wait()
        o_cw[kme] = cw[...]

        hrows = D // 2
        mine = pl.ds(pl.multiple_of(c * hrows, 128), hrows)
        other = pl.ds(pl.multiple_of((1 - c) * hrows, 128), hrows)

        def half(k, rows):
            return o_in.at[k, rows]

        na = (x ^ c, y ^ (1 - c), c)
        nb = (x ^ (1 - c), y ^ c, c)
        ka = 2 * na[0] + na[1]
        kb = 2 * nb[0] + nb[1]
        kd = 2 * (1 - x) + (1 - y)
        sends = [_remote(half(kme, mine), half(kme, mine), ssem.at[0], rsem.at[0], na),
                 _remote(half(kme, mine), half(kme, mine), ssem.at[1], rsem.at[1], nb)]
        for j, (px, py) in enumerate(chips):
            sends.append(_remote(o_cw.at[kme], o_cw.at[kme], ssem.at[3 + j], rsem.at[3 + j], (px, py, c)))
        for cp in sends:
            cp.start()
        fetch[2].wait()

        def rows_of(t):
            return pl.ds(pl.multiple_of(t * tm, tm), tm)

        def x_copy(t, slot):
            return pltpu.make_async_copy(x_h.at[rows_of(t)], xbuf.at[slot], xsem.at[slot])

        def u_load(t, slot):
            return pltpu.make_async_copy(u_h.at[rows_of(t)], ubuf.at[slot], xsem.at[slot])

        def u_store(t, slot):
            return pltpu.make_async_copy(ubuf.at[slot], u_h.at[rows_of(t)], usem.at[slot])

        def f_store(t, slot, kh):
            return pltpu.make_async_copy(rfbuf.at[slot], pf_h.at[rows_of(t), pl.ds(pl.multiple_of(kh * DG, DG), DG)],
                                         fsem.at[slot])

        def b_store_even(t, slot, kh):
            return pltpu.make_async_copy(rbbuf.at[slot, :, 0:256],
                                         pb_h.at[rows_of(t), pl.ds(pl.multiple_of(kh * 2 * DG, 256), 256)],
                                         bsem.at[slot])

        def b_store_odd(t, slot, kh):
            return pltpu.make_async_copy(rbbuf.at[slot],
                                         pb_h.at[rows_of(t), pl.ds(pl.multiple_of(256 + kh * 2 * DG, 256), SHW)],
                                         bsem.at[slot])

        def run_pass(k, first, even):
            kh = k >> 1
            load = x_copy if first else u_load
            load(0, 0).start()

            def tile(t, carry):
                slot = t & 1

                @pl.when(t + 1 < nt)
                def _():
                    load(t + 1, 1 - slot).start()

                load(t, slot).wait()

                @pl.when(t >= 2)
                def _():
                    if first:
                        u_store(t - 2, slot).wait()
                    if even:
                        f_store(t - 2, slot, kh).wait()
                        b_store_even(t - 2, slot, kh).wait()
                    else:
                        b_store_odd(t - 2, slot, kh).wait()

                if first:
                    xv = xbuf[slot]
                    s = lax.rsqrt(jnp.mean(xv * xv, axis=-1, keepdims=True) + EPS)
                    u = (xv * s * nw_s[...]).astype(BF16)
                    ubuf[slot] = u
                    u_store(t, slot).start()
                else:
                    u = ubuf[slot]
                r = jnp.dot(u, o_in[k], preferred_element_type=F32)
                if even:
                    rfbuf[slot] = r[:, 0:DG]
                    rbbuf[slot, :, 0:256] = r[:, DG:SHW].astype(BF16)
                    f_store(t, slot, kh).start()
                    b_store_even(t, slot, kh).start()
                else:
                    rbbuf[slot] = r.astype(BF16)
                    b_store_odd(t, slot, kh).start()
                return carry

            lax.fori_loop(0, nt, tile, 0)
            for t in (nt - 2, nt - 1):
                slot = t & 1
                if first:
                    u_store(t, slot).wait()
                if even:
                    f_store(t, slot, kh).wait()
                    b_store_even(t, slot, kh).wait()
                else:
                    b_store_odd(t, slot, kh).wait()

        def project(k, first=False):
            @pl.when((k & 1) == 0)
            def _():
                run_pass(k, first, True)

            @pl.when((k & 1) == 1)
            def _():
                run_pass(k, first, False)

        def passed_on(k, sem, dev):
            cp = _remote(half(k, mine), half(k, mine), ssem.at[sem], rsem.at[sem], dev)
            cp.start()
            sends.append(cp)

        project(kme, first=True)
        for i, (src, dst) in enumerate([(wout, o_out), (wpg, o_pg), (wpp, o_pp)]):
            fetch[3 + i].wait()
            cast(src, dst)
        _remote(half(ka, mine), half(ka, mine), ssem.at[0], rsem.at[0], na).wait_recv()
        passed_on(ka, 2, nb)
        passed_on(ka, 6, sibling)
        _remote(half(kb, mine), half(kb, mine), ssem.at[1], rsem.at[1], nb).wait_recv()
        passed_on(kb, 7, sibling)
        _remote(half(ka, other), half(ka, other), ssem.at[7], rsem.at[7], sibling).wait_recv()
        project(ka)
        _remote(half(kd, mine), half(kd, mine), ssem.at[2], rsem.at[2], nb).wait_recv()
        passed_on(kd, 8, sibling)
        _remote(half(kb, other), half(kb, other), ssem.at[6], rsem.at[6], sibling).wait_recv()
        project(kb)
        _remote(half(kd, other), half(kd, other), ssem.at[8], rsem.at[8], sibling).wait_recv()
        project(kd)
        for j, (px, py) in enumerate(chips):
            kj = 2 * px + py
            _remote(o_cw.at[kj], o_cw.at[kj], ssem.at[3 + j], rsem.at[3 + j], (px, py, c)).wait_recv()
        for cp in sends:
            cp.wait_send()

    out_shape = [jax.ShapeDtypeStruct((NSHARD,) + s, BF16) for s in shapes]
    out_shape.append(jax.ShapeDtypeStruct((NSHARD,) + conv_w.shape, F32))
    out_shape += [pltpu.HBM((T, 2 * DG), F32), pltpu.HBM((T, 4 * DG), BF16), pltpu.HBM((T, D), BF16)]
    hbm_args = [pltpu.with_memory_space_constraint(a, pltpu.HBM) for a in (x, nw, w_in, w_out, w_pg, w_pp, conv_w)]
    return pl.pallas_call(
        body, name="gather_inproj", out_shape=out_shape,
        in_specs=[HBM_SPEC] * 7, out_specs=[VMEM_SPEC] * 5 + [HBM_SPEC] * 3,
        scratch_shapes=[pltpu.VMEM(a.shape, F32) for a in (w_in, w_out, w_pg, w_pp, conv_w, nw)]
        + [pltpu.VMEM((2, tm, D), F32), pltpu.VMEM((2, tm, D), BF16),
           pltpu.VMEM((2, tm, DG), F32), pltpu.VMEM((2, tm, SHW), BF16),
           pltpu.SemaphoreType.DMA((6,)), pltpu.SemaphoreType.DMA((2,)), pltpu.SemaphoreType.DMA((2,)),
           pltpu.SemaphoreType.DMA((2,)), pltpu.SemaphoreType.DMA((2,)),
           pltpu.SemaphoreType.DMA((9,)), pltpu.SemaphoreType.DMA((9,))],
        compiler_params=pltpu.CompilerParams(vmem_limit_bytes=48 * MIB),
    )(*hbm_args)


def _gather_rest_start(lands):
    n = len(lands)

    def body(*refs):
        land_in = refs[0:n]
        ssem, rsem = refs[n], refs[n + 1]
        token = refs[2 * n + 2]
        x, y, c, chips = _mesh_pos()
        kme = 2 * x + y
        for p, land in enumerate(land_in):
            hrows = land.shape[1] // 2
            mine = pl.ds(pl.multiple_of(c * hrows, 128), hrows)
            for px, py in chips:
                for pc in range(2):
                    _remote(land.at[kme, mine], land.at[kme, mine], ssem.at[p], rsem.at[p], (px, py, pc)).start()
        token[...] = jnp.zeros_like(token)

    out_shape = ([pltpu.SemaphoreType.DMA((n,)), pltpu.SemaphoreType.DMA((n,))]
                 + [pltpu.HBM(a.shape, a.dtype) for a in lands] + [jax.ShapeDtypeStruct((8, 128), F32)])
    outs = pl.pallas_call(
        body, name="gather_rest_start", out_shape=out_shape,
        in_specs=[HBM_SPEC] * n, out_specs=[SEM_SPEC, SEM_SPEC] + [HBM_SPEC] * n + [VMEM_SPEC],
        input_output_aliases={i: 2 + i for i in range(n)},
        compiler_params=pltpu.CompilerParams(has_side_effects=EFFECT),
    )(*[pltpu.with_memory_space_constraint(a, pltpu.HBM) for a in lands])
    return outs[0], outs[1], list(outs[2:2 + n]), outs[2 + n]


def _gather_rest_wait(ssem, rsem, lands, after):
    n = len(lands)

    def body(*refs):
        land_in = refs[0:n]
        ssem_ref, rsem_ref = refs[n], refs[n + 1]
        x, y, c = lax.axis_index("x"), lax.axis_index("y"), lax.axis_index("c")
        for p, land in enumerate(land_in):
            three = land.at[pl.ds(0, 3)]
            cp = _remote(three, three, ssem_ref.at[p], rsem_ref.at[p], (x, y, c))
            cp.wait_send()
            cp.wait_recv()

    outs = pl.pallas_call(
        body, name="gather_rest_wait", out_shape=[pltpu.HBM(a.shape, a.dtype) for a in lands],
        in_specs=[HBM_SPEC] * n + [SEM_SPEC, SEM_SPEC, ANY_SPEC], out_specs=[HBM_SPEC] * n,
        input_output_aliases={i: i for i in range(n)},
        compiler_params=pltpu.CompilerParams(has_side_effects=EFFECT),
    )(*lands, ssem, rsem, after)
    return list(outs)


def _rs_start(name, parts, ks, lands, chip_sums=False):
    n = len(parts)

    def body(*refs):
        part_in, land_in = refs[0:n], refs[n:2 * n]
        ssem, rsem = refs[2 * n], refs[2 * n + 1]
        token = refs[4 * n + 2]
        x, y, c = lax.axis_index("x"), lax.axis_index("y"), lax.axis_index("c")
        kme = 2 * x + y
        me = 4 * x + 2 * y + c
        for p in range(n):
            hrows = land_in[p].shape[1]
            for i, k in enumerate(ks):
                if chip_sums:
                    @pl.when(kme != k)
                    def _():
                        _remote(part_in[p].at[i], land_in[p].at[kme], ssem.at[p], rsem.at[p], (k // 2, k % 2, c)).start()
                    continue
                for pc in range(2):
                    @pl.when(jnp.logical_or(kme != k, c != pc))
                    def _():
                        _remote(part_in[p].at[i, pl.ds(pc * hrows, hrows)], land_in[p].at[me],
                                ssem.at[p], rsem.at[p], (k // 2, k % 2, pc)).start()
        token[...] = jnp.zeros_like(token)

    arrays = list(parts) + list(lands)
    out_shape = ([pltpu.SemaphoreType.DMA((n,)), pltpu.SemaphoreType.DMA((n,))]
                 + [pltpu.HBM(a.shape, a.dtype) for a in arrays] + [jax.ShapeDtypeStruct((8, 128), F32)])
    outs = pl.pallas_call(
        body, name=name, out_shape=out_shape,
        in_specs=[HBM_SPEC] * (2 * n), out_specs=[SEM_SPEC, SEM_SPEC] + [HBM_SPEC] * (2 * n) + [VMEM_SPEC],
        input_output_aliases={i: 2 + i for i in range(2 * n)},
        compiler_params=pltpu.CompilerParams(has_side_effects=EFFECT),
    )(*[pltpu.with_memory_space_constraint(a, pltpu.HBM) for a in arrays])
    return outs[0], outs[1], list(outs[2:2 + n]), list(outs[2 + n:2 + 2 * n]), outs[2 + 2 * n]


def _rs_wait(name, ssem, rsem, parts, ks, lands, after, chip_sums=False):
    n = len(parts)

    def body(*refs):
        part_in, land_in = refs[0:n], refs[n:2 * n]
        ssem_ref, rsem_ref = refs[2 * n], refs[2 * n + 1]
        x, y, c = lax.axis_index("x"), lax.axis_index("y"), lax.axis_index("c")
        kme = 2 * x + y
        for p in range(n):
            piece = land_in[p].at[0]
            for k in ks:
                for pc in range(1 if chip_sums else 2):
                    mine = (kme == k) if chip_sums else jnp.logical_and(kme == k, c == pc)

                    @pl.when(jnp.logical_not(mine))
                    def _():
                        _remote(piece, piece, ssem_ref.at[p], rsem_ref.at[p], (x, y, c)).wait_send()
            owner = kme == ks[0]
            for k in ks[1:]:
                owner = jnp.logical_or(owner, kme == k)

            @pl.when(owner)
            def _():
                others = land_in[p].at[pl.ds(0, land_in[p].shape[0] - 1)]
                _remote(others, others, ssem_ref.at[p], rsem_ref.at[p], (x, y, c)).wait_recv()

    arrays = list(parts) + list(lands)
    outs = pl.pallas_call(
        body, name=name, out_shape=[pltpu.HBM(a.shape, a.dtype) for a in arrays],
        in_specs=[HBM_SPEC] * (2 * n) + [SEM_SPEC, SEM_SPEC, ANY_SPEC], out_specs=[HBM_SPEC] * (2 * n),
        input_output_aliases={i: i for i in range(2 * n)},
        compiler_params=pltpu.CompilerParams(has_side_effects=EFFECT),
    )(*arrays, ssem, rsem, after)
    return list(outs[0:n]), list(outs[n:2 * n])


def _reduce_exchange(parts, lands):
    shapes = [(2 * l.shape[1], l.shape[2]) for l in lands]
    step = 128

    def body(in01, in23, pout, ppg, ppp, l_in, l_out, l_pg, l_pp, g_in, g_out, g_pg, g_pp, ssem, rsem):
        x, y, c = lax.axis_index("x"), lax.axis_index("y"), lax.axis_index("c")
        kme = 2 * x + y
        me = 4 * x + 2 * y + c
        sibling = (x, y, 1 - c)
        sends = []
        for p, (land, gout) in enumerate([(l_in, g_in), (l_out, g_out), (l_pg, g_pg), (l_pp, g_pp)]):
            hrows = land.shape[1]
            mine0 = pl.multiple_of(c * hrows, step)
            for r0 in range(0, hrows, step):
                rs = pl.ds(mine0 + r0, step)
                if p == 0:
                    own = jnp.where(kme >= 2, in23[kme & 1, r0:r0 + step, :], in01[kme & 1, r0:r0 + step, :])
                    slot = kme
                else:
                    own = (pout, ppg, ppp)[p - 1][kme, rs, :]
                    slot = me
                s = jnp.zeros((step, land.shape[2]), F32)
                for j in range(land.shape[0]):
                    s = s + jnp.where(slot == j, own, land[j, r0:r0 + step, :]).astype(F32)
                gout[rs, :] = s
            mine = pl.ds(mine0, hrows)
            cp = _remote(gout.at[mine], gout.at[mine], ssem.at[p], rsem.at[p], sibling)
            cp.start()
            sends.append(cp)
        for p, (land, gout) in enumerate([(l_in, g_in), (l_out, g_out), (l_pg, g_pg), (l_pp, g_pp)]):
            hrows = land.shape[1]
            other = pl.ds(pl.multiple_of((1 - c) * hrows, step), hrows)
            _remote(gout.at[other], gout.at[other], ssem.at[p], rsem.at[p], sibling).wait_recv()
        for cp in sends:
            cp.wait_send()

    return pl.pallas_call(
        body, name="reduce_exchange", out_shape=[jax.ShapeDtypeStruct(s, F32) for s in shapes],
        in_specs=[VMEM_SPEC] * 9, out_specs=[VMEM_SPEC] * 4,
        scratch_shapes=[pltpu.SemaphoreType.DMA((4,)), pltpu.SemaphoreType.DMA((4,))],
        compiler_params=pltpu.CompilerParams(vmem_limit_bytes=48 * MIB),
    )(*parts, *lands)


def _small_allreduce(sm_in, sm_tail, sm_a, sm_b, rg_c):
    PR = DG // NDEV

    def body(in_ref, tail_ref, a_ref, b_ref, rg_ref, v_out, rg_out, vbuf, vrecv, rgrecv, ssem, rsem):
        x, y, c = lax.axis_index("x"), lax.axis_index("y"), lax.axis_index("c")
        me = 4 * x + 2 * y + c

        def pair(ref, r0, r1):
            return jnp.concatenate([ref[r0:r0 + 1, :], ref[r1:r1 + 1, :]], axis=1)

        rows = {
            ROW_NORM_MIX: in_ref[0:1, :], ROW_FINAL_NORM: tail_ref[1:2, :], ROW_B_PG: tail_ref[2:3, :],
            ROW_PLE_NORM: tail_ref[3:4, :], ROW_CB_BA: pair(a_ref, 0, 1), ROW_BX_LAM: pair(a_ref, 2, 3),
            ROW_CW01: pair(a_ref, 4, 5), ROW_CW23: pair(a_ref, 6, 7), ROW_HG_LB: pair(b_ref, 2, 3),
            ROW_HG_NW: jnp.concatenate([b_ref[1:2, :], jnp.zeros((1, DG), F32)], axis=1),
            ROW_LOSS: tail_ref[4:5, :],
        }
        vbuf[...] = jnp.zeros_like(vbuf)
        for r, row in rows.items():
            for j in range(NDEV):
                vbuf[j, r:r + 1, :] = row[:, j * 128:(j + 1) * 128]

        def peer(mask):
            px = x ^ ((mask >> 2) & 1)
            py = y ^ ((mask >> 1) & 1)
            pc = c ^ (mask & 1)
            return (px, py, pc), 4 * px + 2 * py + pc

        def rg_rows(r):
            return pl.ds(pl.multiple_of(r * PR, PR), PR)

        first = []
        for mask in range(1, NDEV):
            dev, r = peer(mask)
            i = mask - 1
            cp = _remote(vbuf.at[r], vrecv.at[i], ssem.at[i], rsem.at[i], dev)
            cp.start()
            first.append(cp)
            cp = _remote(rg_ref.at[rg_rows(r)], rgrecv.at[i], ssem.at[7 + i], rsem.at[7 + i], dev)
            cp.start()
            first.append(cp)
        sv = vbuf[me]
        sr = rg_ref[rg_rows(me), :]
        for i in range(NDEV - 1):
            first[2 * i].wait_recv()
            first[2 * i + 1].wait_recv()
            sv = sv + vrecv[i]
            sr = sr + rgrecv[i]
        v_out[me] = sv
        rg_out[rg_rows(me), :] = sr
        second = []
        for mask in range(1, NDEV):
            dev, r = peer(mask)
            i = mask - 1
            cp = _remote(v_out.at[me], v_out.at[me], ssem.at[14 + i], rsem.at[14 + i], dev)
            cp.start()
            second.append(cp)
            cp = _remote(rg_out.at[rg_rows(me)], rg_out.at[rg_rows(me)], ssem.at[21 + i], rsem.at[21 + i], dev)
            cp.start()
            second.append(cp)
        for mask in range(1, NDEV):
            dev, r = peer(mask)
            i = mask - 1
            _remote(v_out.at[r], v_out.at[r], ssem.at[14 + i], rsem.at[14 + i], dev).wait_recv()
            _remote(rg_out.at[rg_rows(r)], rg_out.at[rg_rows(r)], ssem.at[21 + i], rsem.at[21 + i], dev).wait_recv()
        for cp in first + second:
            cp.wait_send()

    return pl.pallas_call(
        body, name="small_allreduce",
        out_shape=[jax.ShapeDtypeStruct((NDEV, VROWS, 128), F32), jax.ShapeDtypeStruct((DG, 128), F32)],
        in_specs=[VMEM_SPEC] * 5, out_specs=[VMEM_SPEC] * 2,
        scratch_shapes=[pltpu.VMEM((NDEV, VROWS, 128), F32), pltpu.VMEM((NDEV - 1, VROWS, 128), F32),
                        pltpu.VMEM((NDEV - 1, PR, 128), F32),
                        pltpu.SemaphoreType.DMA((28,)), pltpu.SemaphoreType.DMA((28,))],
    )(sm_in, sm_tail, sm_a, sm_b, rg_c)


def _final_reduce(parts, lands, sm_in, sm_tail, sm_a, sm_b, rg_c):
    shapes = [(2 * l.shape[1], l.shape[2]) for l in lands]
    step = 128
    PR = DG // NDEV

    def body(in01, in23, pout, ppg, ppp, l_in, l_out, l_pg, l_pp, in_h, tail_h, a_h, b_h, rg_h,
             g_in, g_out, g_pg, g_pp, v_out, rg_out, vbuf, vrecv, rgrecv,
             lb_in, lb_out, lb_pg, lb_pp, ob_in, ob_out, ob_pg, ob_pp,
             in_ref, tail_ref, a_ref, b_ref, rg_ref, lsem, ssem_l, bsem_s, bsem_r, ssem, rsem):
        x, y, c = lax.axis_index("x"), lax.axis_index("y"), lax.axis_index("c")
        kme = 2 * x + y
        me = 4 * x + 2 * y + c
        sibling = (x, y, 1 - c)
        small_fetch = [pltpu.make_async_copy(src, dst, ssem_l.at[i]) for i, (src, dst) in enumerate(
            [(in_h, in_ref), (tail_h, tail_ref), (a_h, a_ref), (b_h, b_ref), (rg_h, rg_ref)])]
        for cp in small_fetch:
            cp.start()

        lands_hbm = [l_in, l_out, l_pg, l_pp]
        land_bufs = [lb_in, lb_out, lb_pg, lb_pp]
        own_bufs = [ob_in, ob_out, ob_pg, ob_pp]
        fetches = []
        for p in range(4):
            cp = pltpu.make_async_copy(lands_hbm[p], land_bufs[p], lsem.at[p])
            cp.start()
            fetches.append(cp)

        @pl.when(kme >= 2)
        def _():
            pltpu.make_async_copy(in23.at[kme & 1], ob_in, lsem.at[4]).start()

        @pl.when(kme < 2)
        def _():
            pltpu.make_async_copy(in01.at[kme & 1], ob_in, lsem.at[4]).start()

        own_fetches = [pltpu.make_async_copy(in01.at[0], ob_in, lsem.at[4])]
        for p, part in enumerate([pout, ppg, ppp]):
            hrows = own_bufs[p + 1].shape[0]
            cp = pltpu.make_async_copy(part.at[kme, pl.ds(pl.multiple_of(c * hrows, step), hrows)],
                                       own_bufs[p + 1], lsem.at[5 + p])
            cp.start()
            own_fetches.append(cp)

        def pair(ref, r0, r1):
            return jnp.concatenate([ref[r0:r0 + 1, :], ref[r1:r1 + 1, :]], axis=1)

        for cp in small_fetch:
            cp.wait()
        rows = {
            ROW_NORM_MIX: in_ref[0:1, :], ROW_FINAL_NORM: tail_ref[1:2, :], ROW_B_PG: tail_ref[2:3, :],
            ROW_PLE_NORM: tail_ref[3:4, :], ROW_CB_BA: pair(a_ref, 0, 1), ROW_BX_LAM: pair(a_ref, 2, 3),
            ROW_CW01: pair(a_ref, 4, 5), ROW_CW23: pair(a_ref, 6, 7), ROW_HG_LB: pair(b_ref, 2, 3),
            ROW_HG_NW: jnp.concatenate([b_ref[1:2, :], jnp.zeros((1, DG), F32)], axis=1),
            ROW_LOSS: tail_ref[4:5, :],
        }
        vbuf[...] = jnp.zeros_like(vbuf)
        for r, row in rows.items():
            for j in range(NDEV):
                vbuf[j, r:r + 1, :] = row[:, j * 128:(j + 1) * 128]

        def peer(mask):
            px = x ^ ((mask >> 2) & 1)
            py = y ^ ((mask >> 1) & 1)
            pc = c ^ (mask & 1)
            return (px, py, pc), 4 * px + 2 * py + pc

        def rg_rows(r):
            return pl.ds(pl.multiple_of(r * PR, PR), PR)

        first = []
        for mask in range(1, NDEV):
            dev, r = peer(mask)
            i = mask - 1
            cp = _remote(vbuf.at[r], vrecv.at[i], ssem.at[i], rsem.at[i], dev)
            cp.start()
            first.append(cp)
            cp = _remote(rg_ref.at[rg_rows(r)], rgrecv.at[i], ssem.at[7 + i], rsem.at[7 + i], dev)
            cp.start()
            first.append(cp)

        big = [(lb_in, g_in), (lb_out, g_out), (lb_pg, g_pg), (lb_pp, g_pp)]
        swaps = []
        for p, (land, gout) in enumerate(big):
            fetches[p].wait()
            own_fetches[p].wait()
            hrows = land.shape[1]
            mine0 = pl.multiple_of(c * hrows, step)
            slot = kme if p == 0 else me
            for r0 in range(0, hrows, step):
                rs = pl.ds(mine0 + r0, step)
                own = own_bufs[p][r0:r0 + step, :]
                s = jnp.zeros((step, land.shape[2]), F32)
                for j in range(land.shape[0]):
                    s = s + jnp.where(slot == j, own, land[j, r0:r0 + step, :]).astype(F32)
                gout[rs, :] = s
            mine = pl.ds(mine0, hrows)
            cp = _remote(gout.at[mine], gout.at[mine], bsem_s.at[p], bsem_r.at[p], sibling)
            cp.start()
            swaps.append(cp)

        sv = vbuf[me]
        sr = rg_ref[rg_rows(me), :]
        for i in range(NDEV - 1):
            first[2 * i].wait_recv()
            first[2 * i + 1].wait_recv()
            sv = sv + vrecv[i]
            sr = sr + rgrecv[i]
        v_out[me] = sv
        rg_out[rg_rows(me), :] = sr
        second = []
        for mask in range(1, NDEV):
            dev, r = peer(mask)
            i = mask - 1
            cp = _remote(v_out.at[me], v_out.at[me], ssem.at[14 + i], rsem.at[14 + i], dev)
            cp.start()
            second.append(cp)
            cp = _remote(rg_out.at[rg_rows(me)], rg_out.at[rg_rows(me)], ssem.at[21 + i], rsem.at[21 + i], dev)
            cp.start()
            second.append(cp)
        for p, (land, gout) in enumerate(big):
            hrows = land.shape[1]
            other = pl.ds(pl.multiple_of((1 - c) * hrows, step), hrows)
            _remote(gout.at[other], gout.at[other], bsem_s.at[p], bsem_r.at[p], sibling).wait_recv()
        for mask in range(1, NDEV):
            dev, r = peer(mask)
            i = mask - 1
            _remote(v_out.at[r], v_out.at[r], ssem.at[14 + i], rsem.at[14 + i], dev).wait_recv()
            _remote(rg_out.at[rg_rows(r)], rg_out.at[rg_rows(r)], ssem.at[21 + i], rsem.at[21 + i], dev).wait_recv()
        for cp in first + swaps + second:
            cp.wait_send()

    out_shape = [jax.ShapeDtypeStruct(s, F32) for s in shapes]
    out_shape += [jax.ShapeDtypeStruct((NDEV, VROWS, 128), F32), jax.ShapeDtypeStruct((DG, 128), F32)]
    outs = pl.pallas_call(
        body, name="final_reduce", out_shape=out_shape,
        in_specs=[HBM_SPEC] * 14, out_specs=[VMEM_SPEC] * 6,
        scratch_shapes=[pltpu.VMEM((NDEV, VROWS, 128), F32), pltpu.VMEM((NDEV - 1, VROWS, 128), F32),
                        pltpu.VMEM((NDEV - 1, PR, 128), F32)]
        + [pltpu.VMEM(l.shape, BF16) for l in lands]
        + [pltpu.VMEM(l.shape[1:], BF16) for l in lands]
        + [pltpu.VMEM(a.shape, F32) for a in (sm_in, sm_tail, sm_a, sm_b, rg_c)]
        + [pltpu.SemaphoreType.DMA((8,)), pltpu.SemaphoreType.DMA((5,)),
                        pltpu.SemaphoreType.DMA((4,)), pltpu.SemaphoreType.DMA((4,)),
                        pltpu.SemaphoreType.DMA((28,)), pltpu.SemaphoreType.DMA((28,))],
        compiler_params=pltpu.CompilerParams(vmem_limit_bytes=48 * MIB),
    )(*[pltpu.with_memory_space_constraint(a, pltpu.HBM)
        for a in (*parts, *lands, sm_in, sm_tail, sm_a, sm_b, rg_c)])
    return list(outs[0:4]), outs[4], outs[5]


def _adam_rows(w, g, m, v):
    m2 = ADAM_B1 * m + (1.0 - ADAM_B1) * g
    v2 = ADAM_B2 * v + (1.0 - ADAM_B2) * (g * g)
    m_hat = m2 / (1.0 - ADAM_B1 ** ADAM_STEP)
    v_hat = v2 / (1.0 - ADAM_B2 ** ADAM_STEP)
    delta = -ADAM_LR * (m_hat / (jnp.sqrt(v_hat) + ADAM_EPS) + ADAM_WD * w)
    return delta, m2, v2


def _adam_big(gs, ws, ms, vs):
    n = len(gs)
    steps = 8

    def body(*refs):
        ins, outs = refs[:4 * n], refs[4 * n:]
        for i in range(n):
            g, w, m, v = (r[...] for r in ins[4 * i:4 * i + 4])
            d, m2, v2 = _adam_rows(w, g, m, v)
            outs[3 * i][...] = d
            outs[3 * i + 1][...] = m2
            outs[3 * i + 2][...] = v2

    in_specs, out_specs, out_shape, args = [], [], [], []
    for g, w, m, v in zip(gs, ws, ms, vs):
        r, c = w.shape
        spec = lambda: pl.BlockSpec((r // steps, c), lambda i: (i, 0))
        in_specs += [spec() for _ in range(4)]
        out_specs += [spec() for _ in range(3)]
        out_shape += [jax.ShapeDtypeStruct((r, c), F32)] * 3
        args += [pltpu.with_memory_space_constraint(a, pltpu.HBM) for a in (g, w, m, v)]
    outs = pl.pallas_call(
        body, name="adam_big", grid=(steps,), in_specs=in_specs, out_specs=out_specs, out_shape=out_shape,
        compiler_params=_cparams(("parallel",), 32),
    )(*args)
    return [tuple(outs[3 * i:3 * i + 3]) for i in range(n)]


_VEC_PARAMS = [
    ("norm_mix_w", ROW_NORM_MIX, 0, D), ("final_norm_w", ROW_FINAL_NORM, 0, D),
    ("b_ple_gate", ROW_B_PG, 0, D), ("ple_norm_w", ROW_PLE_NORM, 0, D),
    ("conv_b", ROW_CB_BA, 0, DG), ("rg_ba", ROW_CB_BA, DG, DG),
    ("rg_bx", ROW_BX_LAM, 0, DG), ("rg_lambda", ROW_BX_LAM, DG, DG),
    ("hg_norm_w", ROW_HG_NW, 0, HD),
]
_SMALL_ORDER = [n for n, _, _, _ in _VEC_PARAMS] + ["hg_lb", "conv_w", "rg_wa", "rg_wx"]


def _adam_small(vred, rgred, ws, ms, vs):
    names = _SMALL_ORDER
    n = len(names)

    def body(vred_ref, rg_ref, *refs):
        w_refs = dict(zip(names, refs[0:n]))
        m_refs = dict(zip(names, refs[n:2 * n]))
        v_refs = dict(zip(names, refs[2 * n:3 * n]))
        outs = refs[3 * n:]
        o_refs = {nm: outs[4 * i:4 * i + 4] for i, nm in enumerate(names)}
        kme = 2 * lax.axis_index("x") + lax.axis_index("y")

        def update(nm, g, idx):
            d, m2, v2 = _adam_rows(w_refs[nm][idx], g, m_refs[nm][idx], v_refs[nm][idx])
            og, od, om, ov = o_refs[nm]
            og[idx] = g
            od[idx] = d
            om[idx] = m2
            ov[idx] = v2

        def packed(row, lane0, width):
            return jnp.concatenate([vred_ref[j, row:row + 1, :] for j in range(lane0 // 128, (lane0 + width) // 128)],
                                   axis=1)

        everything = (slice(None), slice(None))
        for nm, row, lane0, width in _VEC_PARAMS:
            update(nm, packed(row, lane0, width), everything)
        for r in range(2):
            update("hg_lb", packed(ROW_HG_LB, r * DG, DG), (slice(r, r + 1), slice(None)))
        for j in range(4):
            g = vred_ref[(j % 2) * 4 + kme, ROW_CW01 + j // 2:ROW_CW01 + j // 2 + 1, :]
            update("conv_w", g, (slice(j, j + 1), slice(None)))
        for r0 in range(0, DG, 128):
            rs = (slice(r0, r0 + 128), slice(None))
            both = rg_ref[r0:r0 + 128, :]
            update("rg_wa", both[:, 0:RGB], rs)
            update("rg_wx", pltpu.roll(both, RGB, 1)[:, 0:RGB], rs)

    args = [vred, rgred] + [d[nm] for d in (ws, ms, vs) for nm in names]
    out_shape = []
    for nm in names:
        out_shape += [jax.ShapeDtypeStruct(ws[nm].shape, F32)] * 4
    whole = lambda s: pl.BlockSpec(s.shape, lambda i, nd=len(s.shape): (0,) * nd)
    outs = pl.pallas_call(
        body, name="adam_small", out_shape=out_shape, grid=(1,),
        in_specs=[whole(a) for a in args], out_specs=[whole(s) for s in out_shape],
    )(*args)
    return {nm: tuple(outs[4 * i:4 * i + 4]) for i, nm in enumerate(names)}


def _fwd_inproj(x, nw, w_in_b, dep):
    tm = 512

    def body(x_ref, nw_ref, w_ref, dep_ref, pf_ref, pb_ref, u_ref):
        xv = x_ref[...]
        s = lax.rsqrt(jnp.mean(xv * xv, axis=-1, keepdims=True) + EPS)
        u = (xv * s * nw_ref[...]).astype(BF16)
        u_ref[...] = u
        r = [jnp.dot(u, w_ref[k], preferred_element_type=F32) for k in range(NSHARD)]
        h = DG // 2
        pf_ref[:, 0:DG] = r[0][:, 0:DG]
        pb_ref[:, 0:h] = r[0][:, DG:SHW].astype(BF16)
        pb_ref[:, h:DG] = r[1][:, 0:h].astype(BF16)
        pb_ref[:, DG:2 * DG] = r[1][:, h:SHW].astype(BF16)
        pf_ref[:, DG:2 * DG] = r[2][:, 0:DG]
        pb_ref[:, 2 * DG:2 * DG + h] = r[2][:, DG:SHW].astype(BF16)
        pb_ref[:, 2 * DG + h:3 * DG] = r[3][:, 0:h].astype(BF16)
        pb_ref[:, 3 * DG:4 * DG] = r[3][:, h:SHW].astype(BF16)

    return pl.pallas_call(
        body, name="fwd_inproj", grid=(T // tm,),
        in_specs=[pl.BlockSpec((tm, D), lambda i: (i, 0)),
                  pl.BlockSpec((1, D), lambda i: (0, 0)),
                  pl.BlockSpec((NSHARD, D, SHW), lambda i: (0, 0, 0), pipeline_mode=pl.Buffered(1)), ANY_SPEC],
        out_specs=[pl.BlockSpec((tm, 2 * DG), lambda i: (i, 0)),
                   pl.BlockSpec((tm, 4 * DG), lambda i: (i, 0)),
                   pl.BlockSpec((tm, D), lambda i: (i, 0))],
        out_shape=[jax.ShapeDtypeStruct((T, 2 * DG), F32), jax.ShapeDtypeStruct((T, 4 * DG), BF16),
                   jax.ShapeDtypeStruct((T, D), BF16)],
        compiler_params=_cparams(("parallel",), 48),
    )(x, nw, w_in_b, dep)


def _conv_rows(cw_ref):
    return [jnp.concatenate([cw_ref[k, j:j + 1, :] for k in range(NSHARD)], axis=1) for j in range(4)]


def _rg_conv(xa, prev8, cw, cb, rows):
    taps = [_shift_down(xa, prev8, 3, rows), _shift_down(xa, prev8, 2, rows),
            _shift_down(xa, prev8, 1, rows), xa]
    xc = cb
    for j in range(4):
        xc = xc + taps[j] * cw[j]
    return xc, taps


def _block_mask():
    r = lax.broadcasted_iota(I32, (DG, DG), 0)
    c = lax.broadcasted_iota(I32, (DG, DG), 1)
    return (r >> 6) == (c >> 6)


def _dense_from_blocks(wc):
    j = lax.broadcasted_iota(I32, (RGB, DG), 0)
    c = lax.broadcasted_iota(I32, (RGB, DG), 1)
    spread = _mm_exact(wc, ((c & (RGB - 1)) == j).astype(F32))
    return jnp.where(_block_mask(), spread, 0.0)


def _blocks_from_dense(da, dx):
    c = lax.broadcasted_iota(I32, (DG, 128), 0)
    j = lax.broadcasted_iota(I32, (DG, 128), 1)
    hit = (c & (RGB - 1)) == (j & (RGB - 1))
    mask = _block_mask()
    return (_mm_exact(jnp.where(mask, da, 0.0), (hit & (j < RGB)).astype(F32))
            + _mm_exact(jnp.where(mask, dx, 0.0), (hit & (j >= RGB)).astype(F32)))


def _rg_gates(xc, wa, ba, wx, bx, sp, first_row):
    r = _sigmoid(_mm(xc, wa) + ba)
    i = _sigmoid(_mm(xc, wx) + bx)
    log_a = (-RG_C) * r * sp
    a = jnp.exp(log_a)
    a2 = a * a
    one_m_a2 = -jnp.tanh(log_a) * (a2 + 1.0)
    mult = jnp.where(first_row, 1.0, jnp.sqrt(one_m_a2))
    return r, i, a, a2, mult


def _softplus(z):
    return jnp.maximum(z, 0.0) + jnp.log1p(jnp.exp(-jnp.abs(z)))


def _fwd_rglru(pf, pb, cw3, conv_b, wa_c, ba, wx_c, bx, lam, dep):
    tm = 512
    ng = tm // 8

    def body(xa_ref, ga_ref, cw_ref, cb_ref, wa_ref, ba_ref, wx_ref, bx_ref, lam_ref, dep_ref,
             h_ref, ya_ref, a_s, u_s, tail_s, hc_s, wa_s, wx_s):
        i = pl.program_id(0)

        @pl.when(i == 0)
        def _():
            tail_s[...] = jnp.zeros_like(tail_s)
            hc_s[...] = jnp.zeros_like(hc_s)
            wa_s[...] = _dense_from_blocks(wa_ref[...]).astype(BF16)
            wx_s[...] = _dense_from_blocks(wx_ref[...]).astype(BF16)

        rows = lax.broadcasted_iota(I32, (tm, DG), 0)
        xa = xa_ref[...]
        xc, _ = _rg_conv(xa, tail_s[...], _conv_rows(cw_ref), cb_ref[...], rows)
        tail_s[...] = xa[tm - 8:tm, :]
        sp = _softplus(-lam_ref[...])
        rp = _mm(xc, wa_s[...]) + ba_ref[...]
        ip = _mm(xc, wx_s[...]) + bx_ref[...]
        rb = 64
        rows_b = lax.broadcasted_iota(I32, (rb, DG), 0)
        rows8 = rows_b & 7
        carry = hc_s[0:1, :]
        for b0 in range(0, tm, rb):
            sl = slice(b0, b0 + rb)
            r = _sigmoid(rp[sl])
            ig = _sigmoid(ip[sl])
            log_a = (-RG_C) * r * sp
            av = jnp.exp(log_a)
            mult = jnp.sqrt(-jnp.tanh(log_a) * (av * av + 1.0))
            if b0 == 0:
                mult = jnp.where((rows_b + i * tm) == 0, 1.0, mult)
            uv = mult * (ig * xc[sl])
            for d in (1, 2, 4):
                keep = rows8 >= d
                uv = uv + av * jnp.where(keep, _roll_in_groups(uv, d), 0.0)
                av = av * jnp.where(keep, _roll_in_groups(av, d), 1.0)
            ga = ga_ref[sl, :].astype(F32)
            gate = ga * _sigmoid(ga)
            hs = []
            for g in range(rb // 8):
                gs = slice(g * 8, (g + 1) * 8)
                hv = uv[gs] + av[gs] * carry
                carry = hv[7:8, :]
                hs.append(hv)
            hb = jnp.concatenate(hs, axis=0)
            h_ref[sl, :] = hb
            ya_ref[sl, :] = (hb * gate).astype(BF16)
        hc_s[0:1, :] = carry

    vec = lambda: pl.BlockSpec((1, DG), lambda i: (0, 0))
    blocks = lambda: pl.BlockSpec((DG, RGB), lambda i: (0, 0))
    return pl.pallas_call(
        body, name="fwd_rglru", grid=(T // tm,),
        in_specs=[pl.BlockSpec((tm, DG), lambda i: (i, 0)),
                  pl.BlockSpec((tm, DG), lambda i: (i, 0)),
                  pl.BlockSpec((NSHARD, 4, 128), lambda i: (0, 0, 0)), vec(),
                  blocks(), vec(), blocks(), vec(), vec(), ANY_SPEC],
        out_specs=[pl.BlockSpec((tm, DG), lambda i: (i, 0)),
                   pl.BlockSpec((tm, DG), lambda i: (i, 0))],
        out_shape=[jax.ShapeDtypeStruct((T, DG), F32), pltpu.HBM((T, DG), BF16)],
        scratch_shapes=[pltpu.VMEM((tm, DG), F32), pltpu.VMEM((tm, DG), F32),
                        pltpu.VMEM((8, DG), F32), pltpu.VMEM((8, DG), F32),
                        pltpu.VMEM((DG, DG), BF16), pltpu.VMEM((DG, DG), BF16)],
        compiler_params=_cparams(("arbitrary",), 48),
    )(pf, pb, cw3, conv_b, wa_c, ba, wx_c, bx, lam, dep)


def _hg_lower_bound(lb_ref):
    return _sig_pair(lb_ref[0:1, :] - lb_ref[1:2, :])


def _hg_gates(fz, lb, one_m_lb):
    sg, sn = _sig_pair(fz)
    f = lb + one_m_lb * sg
    return sg, sn, f, jnp.log(f), one_m_lb * sn


def _tri(lower):
    r = lax.broadcasted_iota(I32, (CH, CH), 0)
    c = lax.broadcasted_iota(I32, (CH, CH), 1)
    return (r >= c) if lower else (r <= c)


def _chunk_cumsum(v, rows64):
    for d in (1, 2, 4, 8, 16, 32):
        v = v + jnp.where(rows64 >= d, pltpu.roll(v, d, 0), 0.0)
    return v


def _chunk_rev_cumsum(v, rows64):
    n = v.shape[0]
    for d in (1, 2, 4, 8, 16, 32):
        v = v + jnp.where(rows64 < CH - d, pltpu.roll(v, n - d, 0), 0.0)
    return v


def _hg_recompute(q_ref, f_ref, lb, one_m_lb, rows64, eb_s, enb_s, ekd_s, qe_s, ke_s, kd_s, dec_s):
    nc = q_ref.shape[0] // CH
    sg, sn, f, logf, k = _hg_gates(f_ref[...], lb, one_m_lb)
    q = q_ref[...].astype(F32)
    sq = _sigmoid(q)
    qs = q * sq * (HD ** -0.5)
    b = _chunk_cumsum(logf, rows64)
    for c in range(nc):
        rs = slice(c * CH, (c + 1) * CH)
        b_c = b[rs]
        bl = b_c[CH - 1:CH, :]
        eb, enb, ekd = jnp.exp(b_c), jnp.exp(-b_c), jnp.exp(bl - b_c)
        if eb_s is not None:
            eb_s[rs, :] = eb
            enb_s[rs, :] = enb
            ekd_s[rs, :] = ekd
        qe_s[rs, :] = (qs[rs] * eb).astype(BF16)
        ke_s[rs, :] = (k[rs] * enb).astype(BF16)
        kd_s[rs, :] = (k[rs] * ekd).astype(BF16)
        dec_s[c:c + 1, :] = jnp.exp(bl)
    return sg, sn, f, k, q, sq, qs


def _fwd_hgrn2(pf, pb, hg_lb, hg_nw):
    tm = 512
    nc = tm // CH

    def body(q_ref, f_ref, v_ref, g_ref, lb_ref, nw_ref, yb_ref, o_ref, sp_ref,
             st_s, qe_s, ke_s, kd_s, vb_s, dec_s, p_s, ds_s):
        i = pl.program_id(0)

        @pl.when(i == 0)
        def _():
            st_s[...] = jnp.zeros_like(st_s)

        lb, one_m_lb = _hg_lower_bound(lb_ref)
        rows64 = lax.broadcasted_iota(I32, (tm, DG), 0) & (CH - 1)
        _hg_recompute(q_ref, f_ref, lb, one_m_lb, rows64, None, None, None, qe_s, ke_s, kd_s, dec_s)
        vb_s[...] = v_ref[...]
        mask = _tri(True)
        items = [(c, hd, slice(c * CH, (c + 1) * CH), slice(hd * HD, (hd + 1) * HD))
                 for c in range(nc) for hd in range(NH)]
        for c, hd, rs, cols in items:
            p_s[c * NH + hd] = jnp.where(mask, _mm_nt(qe_s[rs, cols], ke_s[rs, cols]), 0.0).astype(BF16)
            ds_s[c * NH + hd] = _mm_tn(vb_s[rs, cols], kd_s[rs, cols])
        for c, hd, rs, cols in items:
            st = st_s[hd]
            sp_ref[hd, c] = st
            st_s[hd] = st * dec_s[c:c + 1, cols] + ds_s[c * NH + hd]
        for c, hd, rs, cols in items:
            o_ref[rs, cols] = _mm(p_s[c * NH + hd], vb_s[rs, cols]) + _mm_nt(qe_s[rs, cols], sp_ref[hd, c])
        nw = nw_ref[...]
        for hd in range(NH):
            cols = slice(hd * HD, (hd + 1) * HD)
            o = o_ref[:, cols]
            so = lax.rsqrt(jnp.mean(o * o, axis=-1, keepdims=True) + EPS)
            g = g_ref[:, cols].astype(F32)
            sg = _sigmoid(g)
            yb_ref[:, cols] = (o * so * nw * (g * sg)).astype(BF16)

    col = lambda j: pl.BlockSpec((tm, DG), lambda i: (i, j))
    return pl.pallas_call(
        body, name="fwd_hgrn2", grid=(T // tm,),
        in_specs=[col(1), col(1), col(2), col(3),
                  pl.BlockSpec((2, DG), lambda i: (0, 0)),
                  pl.BlockSpec((1, HD), lambda i: (0, 0))],
        out_specs=[pl.BlockSpec((tm, DG), lambda i: (i, 0)),
                   pl.BlockSpec((tm, DG), lambda i: (i, 0)),
                   pl.BlockSpec((NH, nc, HD, HD), lambda i: (0, i, 0, 0))],
        out_shape=[pltpu.HBM((T, DG), BF16), jax.ShapeDtypeStruct((T, DG), F32),
                   jax.ShapeDtypeStruct((NH, NCHUNK, HD, HD), F32)],
        scratch_shapes=[pltpu.VMEM((NH, HD, HD), F32),
                        pltpu.VMEM((tm, DG), BF16), pltpu.VMEM((tm, DG), BF16), pltpu.VMEM((tm, DG), BF16),
                        pltpu.VMEM((tm, DG), BF16), pltpu.VMEM((nc, DG), F32),
                        pltpu.VMEM((nc * NH, CH, CH), BF16), pltpu.VMEM((nc * NH, HD, HD), F32)],
        compiler_params=_cparams(("arbitrary",), 48),
    )(pb, pf, pb, pb, hg_lb, hg_nw)


def _tail_fwd_bwd(x, p, tgt, ya, yb, w_out_b, w_pg_b, w_pp_b, ple_nw, b_pg, fnw):
    tm = 512
    nt = T // tm
    QR = D // NSHARD

    def body(x_ref, p_ref, t_ref, ya_ref, yb_ref, wo_ref, wg_ref, wp_ref, pw_ref, b_ref, fw_ref,
             dh1_ref, dyab_ref, dwo_ref, dwg_ref, dwp_ref, sm_ref, dwo_s, dwg_s, dwp_s):
        i = pl.program_id(0)

        @pl.when(i == 0)
        def _():
            dwo_s[...] = jnp.zeros_like(dwo_s)
            dwg_s[...] = jnp.zeros_like(dwg_s)
            dwp_s[...] = jnp.zeros_like(dwp_s)
            sm_ref[...] = jnp.zeros_like(sm_ref)

        ya = ya_ref[...]
        yb = yb_ref[...]
        pv = p_ref[...].astype(BF16)
        pw = pw_ref[...]
        fw = fw_ref[...]
        h1 = x_ref[...] + _mm(ya, wo_ref[0:DG, :]) + _mm(yb, wo_ref[DG:D, :])
        s2 = lax.rsqrt(jnp.mean(h1 * h1, axis=-1, keepdims=True) + EPS)
        n2h = h1 * s2
        n2 = (n2h * pw).astype(BF16)
        z = _mm(n2, wg_ref[...]) + b_ref[...]
        gate = _sigmoid(z)
        pp = jnp.concatenate([_mm(pv, wp_ref[k]) for k in range(NSHARD)], axis=1)
        h2 = h1 + gate * pp
        s3 = lax.rsqrt(jnp.mean(h2 * h2, axis=-1, keepdims=True) + EPS)
        hn = h2 * s3
        err = hn * fw - t_ref[...]
        sm_ref[0:1, :] += _rowsum(err * err)
        dy = err * (1.0 / D)
        sm_ref[1:2, :] += _rowsum(dy * hn)
        g3 = dy * fw
        dh2 = s3 * (g3 - hn * jnp.mean(g3 * hn, axis=-1, keepdims=True))
        dpp = (dh2 * gate).astype(BF16)
        dz = dh2 * pp * gate * (1.0 - gate)
        sm_ref[2:3, :] += _rowsum(dz)
        dzb = dz.astype(BF16)
        dwg_s[...] += _mm_tn(n2, dzb)
        dn2 = _mm_nt(dzb, wg_ref[...])
        for k in range(NSHARD):
            dwp_s[k] += _mm_tn(pv, dpp[:, k * PLE:(k + 1) * PLE])
        sm_ref[3:4, :] += _rowsum(dn2 * n2h)
        g2 = dn2 * pw
        dh1 = dh2 + s2 * (g2 - n2h * jnp.mean(g2 * n2h, axis=-1, keepdims=True))
        dh1_ref[...] = dh1
        dh1b = dh1.astype(BF16)
        dyab_ref[...] = _mm_nt(dh1b, wo_ref[...])
        dwo_s[0:DG, :] += _mm_tn(ya, dh1b)
        dwo_s[DG:D, :] += _mm_tn(yb, dh1b)

        @pl.when(i == nt - 1)
        def _():
            total = jnp.sum(sm_ref[0:1, :], axis=-1, keepdims=True) * (0.5 / D)
            sm_ref[4:5, :] = jnp.broadcast_to(total, (1, D))
            for k in range(NSHARD):
                dwo_ref[k] = dwo_s[k * QR:(k + 1) * QR, :].astype(BF16)
                dwg_ref[k] = dwg_s[k * QR:(k + 1) * QR, :].astype(BF16)
                dwp_ref[k] = dwp_s[k].astype(BF16)

    row = lambda w: pl.BlockSpec((tm, w), lambda i: (i, 0))
    const2 = lambda s: pl.BlockSpec(s, lambda i: (0, 0), pipeline_mode=pl.Buffered(1))
    const3 = lambda s: pl.BlockSpec(s, lambda i: (0, 0, 0), pipeline_mode=pl.Buffered(1))
    return pl.pallas_call(
        body, name="tail_fwd_bwd", grid=(nt,),
        in_specs=[row(D), row(PLE), row(D), row(DG), row(DG),
                  const2((D, D)), const2((D, D)), const3((NSHARD, PLE, PLE)),
                  const2((1, D)), const2((1, D)), const2((1, D))],
        out_specs=[row(D), row(D), const3((NSHARD, QR, D)), const3((NSHARD, QR, D)),
                   const3((NSHARD, PLE, PLE)), const2((8, D))],
        out_shape=[jax.ShapeDtypeStruct((T, D), F32), jax.ShapeDtypeStruct((T, D), F32),
                   jax.ShapeDtypeStruct((NSHARD, QR, D), BF16), jax.ShapeDtypeStruct((NSHARD, QR, D), BF16),
                   jax.ShapeDtypeStruct((NSHARD, PLE, PLE), BF16), jax.ShapeDtypeStruct((8, D), F32)],
        scratch_shapes=[pltpu.VMEM((D, D), F32), pltpu.VMEM((D, D), F32), pltpu.VMEM((NSHARD, PLE, PLE), F32)],
        compiler_params=_cparams(("arbitrary",), 62),
    )(x, p, tgt, pltpu.with_memory_space_constraint(ya, pltpu.HBM), pltpu.with_memory_space_constraint(yb, pltpu.HBM),
      w_out_b, w_pg_b, w_pp_b, ple_nw, b_pg, fnw)


def _bwd_rglru(pf, pb, h, dyab, cw3, conv_b, wa_c, ba, wx_c, bx, lam, dep):
    tm = 512
    nt = T // tm
    ng = tm // 8

    def body(xa_ref, ga_ref, xp_ref, h_ref, hp_ref, dya_ref, cw_ref, cb_ref, wa_ref, ba_ref, wx_ref, bx_ref,
             lam_ref, dep_ref, da_ref, rg_ref, sm_ref, a_s, g_s, cg_s, nxt_s, wa_s, wx_s, dwa_s, dwx_s):
        i = pl.program_id(0)
        tile = nt - 1 - i

        @pl.when(i == 0)
        def _():
            dwa_s[...] = jnp.zeros_like(dwa_s)
            dwx_s[...] = jnp.zeros_like(dwx_s)
            sm_ref[...] = jnp.zeros_like(sm_ref)
            cg_s[...] = jnp.zeros_like(cg_s)
            nxt_s[...] = jnp.zeros_like(nxt_s)
            wa_s[...] = _dense_from_blocks(wa_ref[...]).astype(BF16)
            wx_s[...] = _dense_from_blocks(wx_ref[...]).astype(BF16)

        rows = lax.broadcasted_iota(I32, (tm, DG), 0)
        has_prev = tile > 0
        xa = xa_ref[...]
        xprev = jnp.where(has_prev, xp_ref[...], 0.0)
        cw = _conv_rows(cw_ref)
        xc, taps = _rg_conv(xa, xprev, cw, cb_ref[...], rows)
        lam_v = lam_ref[...]
        sp = _softplus(-lam_v)
        first_row = (rows + tile * tm) == 0
        r, ig, a, a2, mult = _rg_gates(xc, wa_s[...], ba_ref[...], wx_s[...], bx_ref[...], sp, first_row)
        hv = h_ref[...]
        hprev = jnp.where(has_prev, hp_ref[...], 0.0)
        h_m1 = _shift_down(hv, hprev, 1, rows)
        ga = ga_ref[...].astype(F32)
        sg = _sigmoid(ga)
        dya = dya_ref[...]
        dga = dya * hv * (sg * (1.0 + ga * (1.0 - sg)))

        av = jnp.where(rows == tm - 1, 1.0, pltpu.roll(a, tm - 1, 0))
        gv = dya * (ga * sg)
        rows8 = rows & 7
        for d in (1, 2, 4):
            keep = rows8 < 8 - d
            gv = gv + av * jnp.where(keep, _roll_in_groups(gv, 8 - d), 0.0)
            av = av * jnp.where(keep, _roll_in_groups(av, 8 - d), 1.0)
        a_s[...] = av
        g_s[...] = gv
        carry = cg_s[0:1, :]
        for g in range(ng - 1, -1, -1):
            sl = slice(g * 8, (g + 1) * 8)
            ab, gb = a_s[sl, :], g_s[sl, :]
            g_s[sl, :] = gb + ab * carry
            carry = gb[0:1, :] + ab[0:1, :] * carry
        cg_s[0:1, :] = a[0:1, :] * carry

        gt = g_s[...]
        da = gt * h_m1
        ixc = ig * xc
        di = gt * mult * xc
        dxc = gt * mult * ig
        dlog_a = da * a + jnp.where(first_row, 0.0, gt * ixc * (-a2 / mult))
        sm_ref[3:4, :] += _rowsum(dlog_a * ((-RG_C) * r))
        dpr_f = dlog_a * ((-RG_C) * sp) * r * (1.0 - r)
        dpi_f = di * ig * (1.0 - ig)
        sm_ref[1:2, :] += _rowsum(dpr_f)
        sm_ref[2:3, :] += _rowsum(dpi_f)
        dpr = dpr_f.astype(BF16)
        dpi = dpi_f.astype(BF16)
        xcb = xc.astype(BF16)
        dwa_s[...] += _mm_tn(xcb, dpr)
        dwx_s[...] += _mm_tn(xcb, dpi)
        dxc = dxc + _mm_nt(dpr, wa_s[...]) + _mm_nt(dpi, wx_s[...])
        sm_ref[0:1, :] += _rowsum(dxc)
        for j in range(4):
            sm_ref[4 + j:5 + j, :] += _rowsum(dxc * taps[j])
        nxt = nxt_s[...]
        dxa = (dxc * cw[3] + _shift_up(dxc, nxt, 1, rows) * cw[2]
               + _shift_up(dxc, nxt, 2, rows) * cw[1] + _shift_up(dxc, nxt, 3, rows) * cw[0])
        nxt_s[...] = dxc[0:8, :]
        da_ref[:, 0:DG] = dxa.astype(BF16)
        da_ref[:, DG:D] = dga.astype(BF16)

        @pl.when(i == nt - 1)
        def _():
            _, s_neg = _sig_pair(lam_v)
            sm_ref[3:4, :] = sm_ref[3:4, :] * (-s_neg)
            rg_ref[...] = _blocks_from_dense(dwa_s[...], dwx_s[...])

    vec = lambda: pl.BlockSpec((1, DG), lambda i: (0, 0))
    blocks = lambda: pl.BlockSpec((DG, RGB), lambda i: (0, 0))
    prev8 = lambda: pl.BlockSpec((8, DG), lambda i: (jnp.maximum((nt - 1 - i) * (tm // 8) - 1, 0), 0))
    return pl.pallas_call(
        body, name="bwd_rglru", grid=(nt,),
        in_specs=[pl.BlockSpec((tm, DG), lambda i: (nt - 1 - i, 0)),
                  pl.BlockSpec((tm, DG), lambda i: (nt - 1 - i, 0)),
                  prev8(),
                  pl.BlockSpec((tm, DG), lambda i: (nt - 1 - i, 0)),
                  prev8(),
                  pl.BlockSpec((tm, DG), lambda i: (nt - 1 - i, 0)),
                  pl.BlockSpec((NSHARD, 4, 128), lambda i: (0, 0, 0)), vec(),
                  blocks(), vec(), blocks(), vec(), vec(), ANY_SPEC],
        out_specs=[pl.BlockSpec((tm, D), lambda i: (nt - 1 - i, 0)),
                   pl.BlockSpec((DG, 128), lambda i: (0, 0)),
                   pl.BlockSpec((8, DG), lambda i: (0, 0))],
        out_shape=[jax.ShapeDtypeStruct((T, D), BF16), jax.ShapeDtypeStruct((DG, 128), F32),
                   jax.ShapeDtypeStruct((8, DG), F32)],
        scratch_shapes=[pltpu.VMEM((tm, DG), F32), pltpu.VMEM((tm, DG), F32),
                        pltpu.VMEM((8, DG), F32), pltpu.VMEM((8, DG), F32),
                        pltpu.VMEM((DG, DG), BF16), pltpu.VMEM((DG, DG), BF16),
                        pltpu.VMEM((DG, DG), F32), pltpu.VMEM((DG, DG), F32)],
        compiler_params=_cparams(("arbitrary",), 56),
    )(pf, pb, pf, h, h, dyab, cw3, conv_b, wa_c, ba, wx_c, bx, lam, dep)


def _bwd_hgrn2(pf, pb, o, s_prev, dyab, hg_lb, hg_nw, dep):
    tm = 512
    nt = T // tm
    nc = tm // CH

    def body(q_ref, f_ref, v_ref, g_ref, o_ref, sp_ref, dy_ref, lb_ref, nw_ref, dep_ref, db_ref, sm_ref,
             dst_s, eb_s, enb_s, ekd_s, qe_s, ke_s, kd_s, vb_s, do_s, dec_s, ddec_s, p_s, dp_s,
             g_s, dsta_s, dva_s, dqe_s, dke_s, dkd_s, dlf_s):
        i = pl.program_id(0)

        @pl.when(i == 0)
        def _():
            sm_ref[...] = jnp.zeros_like(sm_ref)
            dst_s[...] = jnp.zeros_like(dst_s)

        lb, one_m_lb = _hg_lower_bound(lb_ref)
        rows64 = lax.broadcasted_iota(I32, (tm, DG), 0) & (CH - 1)
        sg, sn, f, k, q, sq, qs = _hg_recompute(
            q_ref, f_ref, lb, one_m_lb, rows64, eb_s, enb_s, ekd_s, qe_s, ke_s, kd_s, dec_s)
        vb_s[...] = v_ref[...]

        nw = nw_ref[...]
        for hd in range(NH):
            cols = slice(hd * HD, (hd + 1) * HD)
            g = g_ref[:, cols].astype(F32)
            sgg = _sigmoid(g)
            o = o_ref[:, cols]
            so = lax.rsqrt(jnp.mean(o * o, axis=-1, keepdims=True) + EPS)
            oh = o * so
            dyb = dy_ref[:, cols]
            db_ref[:, 3 * DG + hd * HD:3 * DG + (hd + 1) * HD] = (
                dyb * (oh * nw) * (sgg * (1.0 + g * (1.0 - sgg)))).astype(BF16)
            don = dyb * (g * sgg)
            sm_ref[1:2, 0:HD] += _rowsum(don * oh)
            gw = don * nw
            do_s[:, cols] = (so * (gw - oh * jnp.mean(gw * oh, axis=-1, keepdims=True))).astype(BF16)

        mask = _tri(True)
        items = [(c, hd, slice(c * CH, (c + 1) * CH), slice(hd * HD, (hd + 1) * HD))
                 for c in range(nc) for hd in range(NH)]
        for c, hd, rs, cols in items:
            p_s[c * NH + hd] = jnp.where(mask, _mm_nt(qe_s[rs, cols], ke_s[rs, cols]), 0.0).astype(BF16)
            dp_s[c * NH + hd] = jnp.where(mask, _mm_nt(do_s[rs, cols], vb_s[rs, cols]), 0.0).astype(BF16)
        for c, hd, rs, cols in items:
            n = c * NH + hd
            dva_s[rs, cols] = _mm_tn(p_s[n], do_s[rs, cols])
            dqe_s[rs, cols] = _mm(dp_s[n], ke_s[rs, cols])
            dke_s[rs, cols] = _mm_tn(dp_s[n], qe_s[rs, cols])
            g_s[n] = _mm_tn(do_s[rs, cols], qe_s[rs, cols])
        for c, hd, rs, cols in reversed(items):
            n = c * NH + hd
            dst = dst_s[hd]
            dsta_s[n] = dst
            dst_s[hd] = dst * dec_s[c:c + 1, cols] + g_s[n]
        for c, hd, rs, cols in items:
            n = c * NH + hd
            dst = dsta_s[n]
            st_prev = sp_ref[hd, c]
            dv = dva_s[rs, cols] + _mm_nt(kd_s[rs, cols], dst)
            db_ref[rs, 2 * DG + hd * HD:2 * DG + (hd + 1) * HD] = dv.astype(BF16)
            dqe_s[rs, cols] += _mm(do_s[rs, cols], st_prev)
            dkd_s[rs, cols] = _mm(vb_s[rs, cols], dst)
            ddec_s[c:c + 1, cols] = _rowsum(dst * st_prev)

        eb, enb, ekd = eb_s[...], enb_s[...], ekd_s[...]
        dqe, dke, dkd = dqe_s[...], dke_s[...], dkd_s[...]
        t_kd = dkd * (k * ekd)
        rc = _chunk_rev_cumsum(dqe * (qs * eb) - dke * (k * enb) - t_kd, rows64)
        for c in range(nc):
            rs = slice(c * CH, (c + 1) * CH)
            dbl = _rowsum(t_kd[rs]) + ddec_s[c:c + 1, :] * dec_s[c:c + 1, :]
            dlf_s[rs, :] = rc[rs] + dbl
        t = dlf_s[...] / f - (dke * enb + dkd * ekd)
        db_ref[:, DG:2 * DG] = (one_m_lb * sg * sn * t).astype(BF16)
        sm_ref[0:1, :] += _rowsum(sn * t)
        db_ref[:, 0:DG] = (dqe * eb * (sq * (1.0 + q * (1.0 - sq))) * (HD ** -0.5)).astype(BF16)

        @pl.when(i == nt - 1)
        def _():
            dsm = sm_ref[0:1, :] * (lb * one_m_lb)
            sm_ref[2:3, :] = dsm
            sm_ref[3:4, :] = -dsm

    col = lambda j: pl.BlockSpec((tm, DG), lambda i: (nt - 1 - i, j))
    big = lambda dt: pltpu.VMEM((tm, DG), dt)
    return pl.pallas_call(
        body, name="bwd_hgrn2", grid=(nt,),
        in_specs=[col(1), col(1), col(2), col(3),
                  pl.BlockSpec((tm, DG), lambda i: (nt - 1 - i, 0)),
                  pl.BlockSpec((NH, nc, HD, HD), lambda i: (0, nt - 1 - i, 0, 0)),
                  pl.BlockSpec((tm, DG), lambda i: (nt - 1 - i, 1)),
                  pl.BlockSpec((2, DG), lambda i: (0, 0)),
                  pl.BlockSpec((1, HD), lambda i: (0, 0)), ANY_SPEC],
        out_specs=[pl.BlockSpec((tm, 4 * DG), lambda i: (nt - 1 - i, 0)),
                   pl.BlockSpec((8, DG), lambda i: (0, 0))],
        out_shape=[jax.ShapeDtypeStruct((T, 4 * DG), BF16), jax.ShapeDtypeStruct((8, DG), F32)],
        scratch_shapes=[pltpu.VMEM((NH, HD, HD), F32),
                        big(F32), big(F32), big(F32),
                        big(BF16), big(BF16), big(BF16), big(BF16), big(BF16),
                        pltpu.VMEM((nc, DG), F32), pltpu.VMEM((nc, DG), F32),
                        pltpu.VMEM((nc * NH, CH, CH), BF16), pltpu.VMEM((nc * NH, CH, CH), BF16),
                        pltpu.VMEM((nc * NH, HD, HD), F32), pltpu.VMEM((nc * NH, HD, HD), F32),
                        big(F32), big(F32), big(F32), big(F32), big(F32)],
        compiler_params=_cparams(("arbitrary",), 56),
    )(pb, pf, pb, pb, o, s_prev, dyab, hg_lb, hg_nw, dep)


def _dproj_pieces(k, da_ref, db_ref):
    if k == 0:
        return [(da_ref[:, 0:SHW], 0)]
    if k == 1:
        return [(da_ref[:, SHW:D], 0), (db_ref[:, 0:DG], D - SHW)]
    if k == 2:
        return [(db_ref[:, DG:DG + SHW], 0)]
    return [(db_ref[:, DG + SHW:4 * DG], 0)]


def _bwd_inproj_dx(x, dh1, d_a, d_b, w_in_b, nw, dep):
    tm = 512
    nt = T // tm

    def body(x_ref, dh1_ref, da_ref, db_ref, w_ref, nw_ref, dep_ref, dx_ref, sm_ref):
        i = pl.program_id(0)

        @pl.when(i == 0)
        def _():
            sm_ref[...] = jnp.zeros_like(sm_ref)

        du = None
        for k in range(NSHARD):
            for val, off in _dproj_pieces(k, da_ref, db_ref):
                t = _mm_nt(val, w_ref[k, :, off:off + val.shape[1]])
                du = t if du is None else du + t
        xv = x_ref[...]
        s = lax.rsqrt(jnp.mean(xv * xv, axis=-1, keepdims=True) + EPS)
        xh = xv * s
        sm_ref[0:1, :] += _rowsum(du * xh)
        g = du * nw_ref[...]
        dx_ref[...] = dh1_ref[...] + s * (g - xh * jnp.mean(g * xh, axis=-1, keepdims=True))

    row = lambda w: pl.BlockSpec((tm, w), lambda i: (i, 0))
    return pl.pallas_call(
        body, name="bwd_inproj_dx", grid=(nt,),
        in_specs=[row(D), row(D), row(D), row(4 * DG),
                  pl.BlockSpec((NSHARD, D, SHW), lambda i: (0, 0, 0), pipeline_mode=pl.Buffered(1)),
                  pl.BlockSpec((1, D), lambda i: (0, 0)), ANY_SPEC],
        out_specs=[row(D), pl.BlockSpec((8, D), lambda i: (0, 0))],
        out_shape=[jax.ShapeDtypeStruct((T, D), F32), jax.ShapeDtypeStruct((8, D), F32)],
        compiler_params=_cparams(("arbitrary",), 56),
    )(x, dh1, d_a, d_b, w_in_b, nw, dep)


def _bwd_inproj_dw(name, ks, u_b, d_a, d_b):
    tm = 1024
    nt = T // tm
    nk = len(ks)
    H = D // 2
    step = 128
    need_a = 0 in ks or 1 in ks

    def body(*refs):
        if need_a:
            u_ref, da_ref, db_ref, dw_ref, acc, send_s, recv_s, ssem, rsem = refs
        else:
            u_ref, db_ref, dw_ref, acc, send_s, recv_s, ssem, rsem = refs
            da_ref = None
        i = pl.program_id(0)

        @pl.when(i == 0)
        def _():
            acc[...] = jnp.zeros_like(acc)

        u = u_ref[...]
        for j, k in enumerate(ks):
            for val, off in _dproj_pieces(k, da_ref, db_ref):
                acc[j, :, off:off + val.shape[1]] += _mm_tn(u, val)

        @pl.when(i == nt - 1)
        def _():
            x, y, c = lax.axis_index("x"), lax.axis_index("y"), lax.axis_index("c")
            sibling = (x, y, 1 - c)
            mine0 = pl.multiple_of(c * H, step)
            other0 = pl.multiple_of((1 - c) * H, step)
            copies = []
            for j in range(nk):
                for r0 in range(0, H, step):
                    send_s[j, r0:r0 + step, :] = acc[j, pl.ds(other0 + r0, step), :].astype(BF16)
                cp = _remote(send_s.at[j], recv_s.at[j], ssem.at[j], rsem.at[j], sibling)
                cp.start()
                copies.append(cp)
            for j in range(nk):
                copies[j].wait_recv()
                for r0 in range(0, H, step):
                    s = acc[j, pl.ds(mine0 + r0, step), :] + recv_s[j, r0:r0 + step, :].astype(F32)
                    dw_ref[j, r0:r0 + step, :] = s.astype(BF16)
            for cp in copies:
                cp.wait_send()

    row = lambda w: pl.BlockSpec((tm, w), lambda i: (i, 0))
    ins = [u_b] + ([d_a] if need_a else []) + [d_b]
    in_specs = [row(D)] + ([row(D)] if need_a else []) + [row(4 * DG)]
    return pl.pallas_call(
        body, name=name, grid=(nt,), in_specs=in_specs,
        out_specs=pl.BlockSpec((nk, H, SHW), lambda i: (0, 0, 0)),
        out_shape=jax.ShapeDtypeStruct((nk, H, SHW), BF16),
        scratch_shapes=[pltpu.VMEM((nk, D, SHW), F32), pltpu.VMEM((nk, H, SHW), BF16),
                        pltpu.VMEM((nk, H, SHW), BF16),
                        pltpu.SemaphoreType.DMA((nk,)), pltpu.SemaphoreType.DMA((nk,))],
        compiler_params=_cparams(("arbitrary",), 48),
    )(*ins)


_OUT_ORDER = ["norm_mix_w", "w_in", "conv_w", "conv_b", "rg_wa", "rg_ba", "rg_wx", "rg_bx", "rg_lambda", "hg_lb",
              "hg_norm_w", "w_out", "ple_norm_w", "w_ple_gate", "b_ple_gate", "w_ple_proj", "final_norm_w"]
_BIG = ["w_in", "w_out", "w_ple_gate", "w_ple_proj"]


def _small_view(name, a):
    if name in ("rg_wa", "rg_wx"):
        return a.reshape(DG, RGB)
    if name == "conv_w":
        return a.reshape(4, 128)
    if name == "final_norm_w":
        return a.reshape(1, D)
    return a


def _landing(rows, cols):
    return lax.empty((NDEV, rows, cols), BF16)


def kernel(x, p, norm_mix_w, w_in, conv_w, conv_b, rg_wa, rg_ba, rg_wx, rg_bx, rg_lambda, hg_lb, hg_norm_w, w_out, ple_norm_w, w_ple_gate, b_ple_gate, w_ple_proj, final_norm_w, loss_target, m_norm_mix_w, m_w_in, m_conv_w, m_conv_b, m_rg_wa, m_rg_ba, m_rg_wx, m_rg_bx, m_rg_lambda, m_hg_lb, m_hg_norm_w, m_w_out, m_ple_norm_w, m_w_ple_gate, m_b_ple_gate, m_w_ple_proj, m_final_norm_w, v_norm_mix_w, v_w_in, v_conv_w, v_conv_b, v_rg_wa, v_rg_ba, v_rg_wx, v_rg_bx, v_rg_lambda, v_hg_lb, v_hg_norm_w, v_w_out, v_ple_norm_w, v_w_ple_gate, v_b_ple_gate, v_w_ple_proj, v_final_norm_w):
    given = dict(locals())
    x2, p2, tgt = x[0], p[0, 0], loss_target[0]
    hbm = lambda a: pltpu.with_memory_space_constraint(a, pltpu.HBM)
    norm_mix_w, conv_b, rg_ba, rg_bx, rg_lambda, hg_lb, hg_norm_w, ple_norm_w, b_ple_gate = (
        hbm(a) for a in (norm_mix_w, conv_b, rg_ba, rg_bx, rg_lambda, hg_lb, hg_norm_w, ple_norm_w, b_ple_gate))
    wa_c, wx_c = hbm(_small_view("rg_wa", rg_wa)), hbm(_small_view("rg_wx", rg_wx))

    w_in_b, l_out, l_pg, l_pp, cw3, pf, pb, u_b = _gather_inproj(
        x2, norm_mix_w, w_in[0], w_out[0], w_ple_gate[0], w_ple_proj[0], conv_w[0])
    g_ssem, g_rsem, g_lands, tok = _gather_rest_start([l_out, l_pg, l_pp])

    h, ya = _fwd_rglru(pf, pb, cw3, conv_b, wa_c, rg_ba, wx_c, rg_bx, rg_lambda, tok)
    yb, o, s_prev = _fwd_hgrn2(pf, pb, hg_lb, hg_norm_w)
    w_out_b, w_pg_b, w_pp_b = _gather_rest_wait(g_ssem, g_rsem, g_lands, yb)
    dh1, dyab, dwo_b, dwg_b, dwp_b, sm_tail = _tail_fwd_bwd(
        x2, p2, tgt, ya, yb, w_out_b.reshape(D, D), w_pg_b.reshape(D, D), w_pp_b,
        ple_norm_w, b_ple_gate, final_norm_w.reshape(1, D))

    QH = D // NSHARD // 2
    r1 = _rs_start("rs_start_tail", [dwo_b, dwg_b, dwp_b], (0, 1, 2, 3),
                   [_landing(QH, D), _landing(QH, D), _landing(PLE // 2, PLE)])
    d_b, sm_b = _bwd_hgrn2(pf, pb, o, s_prev, dyab, hg_lb, hg_norm_w, r1[4])
    dw23 = _bwd_inproj_dw("bwd_inproj_dw23", (2, 3), u_b, None, d_b)
    r2 = _rs_start("rs_start_in23", [dw23], (2, 3), [lax.empty((NSHARD, D // 2, SHW), BF16)], chip_sums=True)
    d_a, rg_c, sm_a = _bwd_rglru(pf, pb, h, dyab, cw3, conv_b, wa_c, rg_ba, wx_c, rg_bx, rg_lambda, r2[4])
    dw01 = _bwd_inproj_dw("bwd_inproj_dw01", (0, 1), u_b, d_a, d_b)
    r3 = _rs_start("rs_start_in01", [dw01], (0, 1), r2[3], chip_sums=True)
    grad_x, sm_in = _bwd_inproj_dx(x2, dh1, d_a, d_b, w_in_b, norm_mix_w, r3[4])

    parts1, lands1 = _rs_wait("rs_wait_tail", r1[0], r1[1], r1[2], (0, 1, 2, 3), r1[3], sm_in)
    parts2, lands2 = _rs_wait("rs_wait_in23", r2[0], r2[1], r2[2], (2, 3), r3[3], sm_in, chip_sums=True)
    parts3, lands3 = _rs_wait("rs_wait_in01", r3[0], r3[1], r3[2], (0, 1), lands2, sm_in, chip_sums=True)
    g_big, vred, rgred = _final_reduce([parts3[0], parts2[0]] + parts1, lands3 + lands1,
                                       sm_in, sm_tail, sm_a, sm_b, rg_c)

    upd_big = _adam_big(g_big, [given[n][0] for n in _BIG], [given["m_" + n][0] for n in _BIG],
                        [given["v_" + n][0] for n in _BIG])
    small = _adam_small(vred, rgred,
                        {n: _small_view(n, given[n]) for n in _SMALL_ORDER},
                        {n: _small_view(n, given["m_" + n]) for n in _SMALL_ORDER},
                        {n: _small_view(n, given["v_" + n]) for n in _SMALL_ORDER})

    loss = vred[0, ROW_LOSS, 0]
    outs = [loss, grad_x[None]]
    for ki in range(4):
        for n in _OUT_ORDER:
            if n in _BIG:
                i = _BIG.index(n)
                a = g_big[i] if ki == 0 else upd_big[i][ki - 1]
                outs.append(a[None])
            else:
                outs.append(small[n][ki].reshape(given[n].shape))
    return tuple(outs)
```

```python
import jax
import jax.numpy as jnp
from jax import lax
from jax.experimental import pallas as pl
from jax.experimental.pallas import tpu as pltpu

F32 = jnp.float32
BF16 = jnp.bfloat16
I32 = jnp.int32
MESH = pl.DeviceIdType.MESH
HIGHEST = lax.Precision.HIGHEST

T = 4096
D = 1024
DG = 512
DIN = 3072
PLE = 256
NH = 4
HD = 128
CH = 64
NCHUNK = T // CH
RGB = 64
EPS = 1e-6
RG_C = 8.0
NSHARD = 4
SHW = DIN // NSHARD
NDEV = 8

ADAM_LR = 0.001
ADAM_B1 = 0.9
ADAM_B2 = 0.999
ADAM_EPS = 1e-08
ADAM_WD = 0.01
ADAM_STEP = 10

VMEM_SPEC = pl.BlockSpec(memory_space=pltpu.VMEM)
HBM_SPEC = pl.BlockSpec(memory_space=pltpu.HBM)
SEM_SPEC = pl.BlockSpec(memory_space=pltpu.SEMAPHORE)
ANY_SPEC = pl.BlockSpec(memory_space=pl.ANY)
EFFECT = pltpu.SideEffectType.DATAFLOW_SIDE_EFFECTING
MIB = 1024 * 1024

VROWS = 16
ROW_NORM_MIX, ROW_FINAL_NORM, ROW_B_PG, ROW_PLE_NORM = 0, 1, 2, 3
ROW_CB_BA, ROW_BX_LAM, ROW_CW01, ROW_CW23, ROW_HG_LB, ROW_HG_NW, ROW_LOSS = 4, 5, 6, 7, 8, 9, 10


def _mm(a, b):
    return jnp.dot(a.astype(BF16), b.astype(BF16), preferred_element_type=F32)


def _mm_nt(a, b):
    return lax.dot_general(a.astype(BF16), b.astype(BF16), (((1,), (1,)), ((), ())),
                           preferred_element_type=F32)


def _mm_tn(a, b):
    return lax.dot_general(a.astype(BF16), b.astype(BF16), (((0,), (0,)), ((), ())),
                           preferred_element_type=F32)


def _mm_exact(a, b):
    return jnp.dot(a, b, precision=HIGHEST, preferred_element_type=F32)


def _sig_pair(x):
    e = jnp.exp(-jnp.abs(x))
    big = 1.0 / (1.0 + e)
    small = e * big
    pos = x >= 0
    return jnp.where(pos, big, small), jnp.where(pos, small, big)


def _sigmoid(x):
    return 1.0 / (1.0 + jnp.exp(-x))


def _rowsum(v):
    return jnp.sum(v, axis=0, keepdims=True)


def _shift_down(cur, prev8, d, rows):
    rolled = pltpu.roll(cur, d, 0)
    head = jnp.where(rows[0:8] < d, pltpu.roll(prev8, d, 0), rolled[0:8])
    return jnp.concatenate([head, rolled[8:]], axis=0)


def _shift_up(cur, next8, d, rows):
    n = cur.shape[0]
    rolled = pltpu.roll(cur, n - d, 0)
    tail = jnp.where(rows[0:8] >= 8 - d, pltpu.roll(next8, 8 - d, 0), rolled[n - 8:n])
    return jnp.concatenate([rolled[0:n - 8], tail], axis=0)


def _roll_in_groups(v, d):
    n, w = v.shape
    return pltpu.roll(v.reshape(n // 8, 8, w), d, 1).reshape(n, w)


def _cparams(sem, vmem_mib):
    return pltpu.CompilerParams(dimension_semantics=sem, vmem_limit_bytes=vmem_mib * MIB)


def _mesh_pos():
    x, y, c = lax.axis_index("x"), lax.axis_index("y"), lax.axis_index("c")
    chips = [(1 - x, y), (x, 1 - y), (1 - x, 1 - y)]
    return x, y, c, chips


def _remote(src, dst, ssem, rsem, dev):
    return pltpu.make_async_remote_copy(src_ref=src, dst_ref=dst, send_sem=ssem, recv_sem=rsem,
                                        device_id=dev, device_id_type=MESH)


def _gather_w_in(w_in, w_out, w_pg, w_pp, conv_w):
    shapes = [w_in.shape, w_out.shape, w_pg.shape, w_pp.shape]

    def body(win_h, wout_h, wpg_h, wpp_h, cw_h, o_in, o_out, o_pg, o_pp, o_cw,
             win, wout, wpg, wpp, cw, lsem, ssem, rsem):
        x, y, c, chips = _mesh_pos()
        kme = 2 * x + y
        sibling = (x, y, 1 - c)
        fetch = [pltpu.make_async_copy(src, dst, lsem.at[i])
                 for i, (src, dst) in enumerate([(win_h, win), (cw_h, cw), (wout_h, wout), (wpg_h, wpg), (wpp_h, wpp)])]
        for cp in fetch:
            cp.start()

        def cast(src, dst):
            for r0 in range(0, src.shape[0], 128):
                dst[kme, r0:r0 + 128, :] = src[r0:r0 + 128, :].astype(BF16)

        fetch[0].wait()
        cast(win, o_in)
        fetch[1].wait()
        o_cw[kme] = cw[...]

        hrows = D // 2
        mine = pl.ds(pl.multiple_of(c * hrows, 128), hrows)
        other = pl.ds(pl.multiple_of((1 - c) * hrows, 128), hrows)

        def half(k, rows):
            return o_in.at[k, rows]

        na = (x ^ c, y ^ (1 - c), c)
        nb = (x ^ (1 - c), y ^ c, c)
        ka = 2 * na[0] + na[1]
        kb = 2 * nb[0] + nb[1]
        kd = 2 * (1 - x) + (1 - y)
        sends = [_remote(half(kme, mine), half(kme, mine), ssem.at[0], rsem.at[0], na),
                 _remote(half(kme, mine), half(kme, mine), ssem.at[1], rsem.at[1], nb)]
        for j, (px, py) in enumerate(chips):
            sends.append(_remote(o_cw.at[kme], o_cw.at[kme], ssem.at[3 + j], rsem.at[3 + j], (px, py, c)))
        for cp in sends:
            cp.start()
        for i, (src, dst) in enumerate([(wout, o_out), (wpg, o_pg), (wpp, o_pp)]):
            fetch[2 + i].wait()
            cast(src, dst)

        def passed_on(k, sem, dev):
            cp = _remote(half(k, mine), half(k, mine), ssem.at[sem], rsem.at[sem], dev)
            cp.start()
            sends.append(cp)

        _remote(half(ka, mine), half(ka, mine), ssem.at[0], rsem.at[0], na).wait_recv()
        passed_on(ka, 2, nb)
        passed_on(ka, 6, sibling)
        _remote(half(kb, mine), half(kb, mine), ssem.at[1], rsem.at[1], nb).wait_recv()
        passed_on(kb, 7, sibling)
        _remote(half(kd, mine), half(kd, mine), ssem.at[2], rsem.at[2], nb).wait_recv()
        passed_on(kd, 8, sibling)
        _remote(half(kb, other), half(kb, other), ssem.at[6], rsem.at[6], sibling).wait_recv()
        _remote(half(ka, other), half(ka, other), ssem.at[7], rsem.at[7], sibling).wait_recv()
        _remote(half(kd, other), half(kd, other), ssem.at[8], rsem.at[8], sibling).wait_recv()
        for j, (px, py) in enumerate(chips):
            kj = 2 * px + py
            _remote(o_cw.at[kj], o_cw.at[kj], ssem.at[3 + j], rsem.at[3 + j], (px, py, c)).wait_recv()
        for cp in sends:
            cp.wait_send()

    out_shape = [jax.ShapeDtypeStruct((NSHARD,) + s, BF16) for s in shapes]
    out_shape.append(jax.ShapeDtypeStruct((NSHARD,) + conv_w.shape, F32))
    return pl.pallas_call(
        body, name="gather_w_in", out_shape=out_shape,
        in_specs=[HBM_SPEC] * 5, out_specs=[VMEM_SPEC] * 5,
        scratch_shapes=[pltpu.VMEM(a.shape, F32) for a in (w_in, w_out, w_pg, w_pp, conv_w)]
        + [pltpu.SemaphoreType.DMA((5,)), pltpu.SemaphoreType.DMA((9,)), pltpu.SemaphoreType.DMA((9,))],
        compiler_params=pltpu.CompilerParams(vmem_limit_bytes=40 * MIB),
    )(*[pltpu.with_memory_space_constraint(a, pltpu.HBM) for a in (w_in, w_out, w_pg, w_pp, conv_w)])


def _gather_inproj(x, nw, w_in, w_out, w_pg, w_pp, conv_w):
    shapes = [w_in.shape, w_out.shape, w_pg.shape, w_pp.shape]
    tm = 512
    nt = T // tm

    def body(x_h, nw_h, win_h, wout_h, wpg_h, wpp_h, cw_h,
             o_in, o_out, o_pg, o_pp, o_cw, pf_h, pb_h, u_h,
             win, wout, wpg, wpp, cw, nw_s, xbuf, ubuf, rfbuf, rbbuf,
             lsem, xsem, usem, fsem, bsem, ssem, rsem):
        x, y, c, chips = _mesh_pos()
        kme = 2 * x + y
        sibling = (x, y, 1 - c)
        fetch = [pltpu.make_async_copy(src, dst, lsem.at[i]) for i, (src, dst) in enumerate(
            [(win_h, win), (cw_h, cw), (nw_h, nw_s), (wout_h, wout), (wpg_h, wpg), (wpp_h, wpp)])]
        for cp in fetch:
            cp.start()

        def cast(src, dst):
            for r0 in range(0, src.shape[0], 128):
                dst[kme, r0:r0 + 128, :] = src[r0:r0 + 128, :].astype(BF16)

        fetch[0].wait()
        cast(win, o_in)
        fetch[1].wait()
        o_cw[kme] = cw[...]

        hrows = D // 2
        mine = pl.ds(pl.multiple_of(c * hrows, 128), hrows)
        other = pl.ds(pl.multiple_of((1 - c) * hrows, 128), hrows)

        def half(k, rows):
            return o_in.at[k, rows]

        na = (x ^ c, y ^ (1 - c), c)
        nb = (x ^ (1 - c), y ^ c, c)
        ka = 2 * na[0] + na[1]
        kb = 2 * nb[0] + nb[1]
        kd = 2 * (1 - x) + (1 - y)
        sends = [_remote(half(kme, mine), half(kme, mine), ssem.at[0], rsem.at[0], na),
                 _remote(half(kme, mine), half(kme, mine), ssem.at[1], rsem.at[1], nb)]
        for j, (px, py) in enumerate(chips):
            sends.append(_remote(o_cw.at[kme], o_cw.at[kme], ssem.at[3 + j], rsem.at[3 + j], (px, py, c)))
        for cp in sends:
            cp.start()
        fetch[2].wait()

        def rows_of(t):
            return pl.ds(pl.multiple_of(t * tm, tm), tm)

        def x_copy(t, slot):
            return pltpu.make_async_copy(x_h.at[rows_of(t)], xbuf.at[slot], xsem.at[slot])

        u_out = pltpu.make_async_copy(ubuf, u_h, usem.at[0])

        def f_store(t, slot, kh):
            return pltpu.make_async_copy(rfbuf.at[slot], pf_h.at[rows_of(t), pl.ds(pl.multiple_of(kh * DG, DG), DG)],
                                         fsem.at[slot])

        def b_store_even(t, slot, kh):
            return pltpu.make_async_copy(rbbuf.at[slot, :, 0:256],
                                         pb_h.at[rows_of(t), pl.ds(pl.multiple_of(kh * 2 * DG, 256), 256)],
                                         bsem.at[slot])

        def b_store_odd(t, slot, kh):
            return pltpu.make_async_copy(rbbuf.at[slot],
                                         pb_h.at[rows_of(t), pl.ds(pl.multiple_of(256 + kh * 2 * DG, 256), SHW)],
                                         bsem.at[slot])

        def run_pass(k, first, even):
            kh = k >> 1
            if first:
                x_copy(0, 0).start()

            def tile(t, carry):
                slot = t & 1
                if first:
                    @pl.when(t + 1 < nt)
                    def _():
                        x_copy(t + 1, 1 - slot).start()

                    x_copy(t, slot).wait()

                @pl.when(t >= 2)
                def _():
                    if even:
                        f_store(t - 2, slot, kh).wait()
                        b_store_even(t - 2, slot, kh).wait()
                    else:
                        b_store_odd(t - 2, slot, kh).wait()

                if first:
                    xv = xbuf[slot]
                    s = lax.rsqrt(jnp.mean(xv * xv, axis=-1, keepdims=True) + EPS)
                    u = (xv * s * nw_s[...]).astype(BF16)
                    ubuf[rows_of(t), :] = u
                else:
                    u = ubuf[rows_of(t), :]
                r = jnp.dot(u, o_in[k], preferred_element_type=F32)
                if even:
                    rfbuf[slot] = r[:, 0:DG]
                    rbbuf[slot, :, 0:256] = r[:, DG:SHW].astype(BF16)
                    f_store(t, slot, kh).start()
                    b_store_even(t, slot, kh).start()
                else:
                    rbbuf[slot] = r.astype(BF16)
                    b_store_odd(t, slot, kh).start()
                return carry

            lax.fori_loop(0, nt, tile, 0)
            for t in (nt - 2, nt - 1):
                slot = t & 1
                if even:
                    f_store(t, slot, kh).wait()
                    b_store_even(t, slot, kh).wait()
                else:
                    b_store_odd(t, slot, kh).wait()

        def project(k, first=False):
            @pl.when((k & 1) == 0)
            def _():
                run_pass(k, first, True)

            @pl.when((k & 1) == 1)
            def _():
                run_pass(k, first, False)

        def passed_on(k, sem, dev):
            cp = _remote(half(k, mine), half(k, mine), ssem.at[sem], rsem.at[sem], dev)
            cp.start()
            sends.append(cp)

        project(kme, first=True)
        u_out.start()
        for i, (src, dst) in enumerate([(wout, o_out), (wpg, o_pg), (wpp, o_pp)]):
            fetch[3 + i].wait()
            cast(src, dst)
        _remote(half(ka, mine), half(ka, mine), ssem.at[0], rsem.at[0], na).wait_recv()
        passed_on(ka, 2, nb)
        passed_on(ka, 6, sibling)
        _remote(half(kb, mine), half(kb, mine), ssem.at[1], rsem.at[1], nb).wait_recv()
        passed_on(kb, 7, sibling)
        _remote(half(ka, other), half(ka, other), ssem.at[7], rsem.at[7], sibling).wait_recv()
        project(ka)
        _remote(half(kd, mine), half(kd, mine), ssem.at[2], rsem.at[2], nb).wait_recv()
        passed_on(kd, 8, sibling)
        _remote(half(kb, other), half(kb, other), ssem.at[6], rsem.at[6], sibling).wait_recv()
        project(kb)
        _remote(half(kd, other), half(kd, other), ssem.at[8], rsem.at[8], sibling).wait_recv()
        project(kd)
        u_out.wait()
        for j, (px, py) in enumerate(chips):
            kj = 2 * px + py
            _remote(o_cw.at[kj], o_cw.at[kj], ssem.at[3 + j], rsem.at[3 + j], (px, py, c)).wait_recv()
        for cp in sends:
            cp.wait_send()

    out_shape = [jax.ShapeDtypeStruct((NSHARD,) + s, BF16) for s in shapes]
    out_shape.append(jax.ShapeDtypeStruct((NSHARD,) + conv_w.shape, F32))
    out_shape += [pltpu.HBM((T, 2 * DG), F32), pltpu.HBM((T, 4 * DG), BF16), pltpu.HBM((T, D), BF16)]
    hbm_args = [pltpu.with_memory_space_constraint(a, pltpu.HBM) for a in (x, nw, w_in, w_out, w_pg, w_pp, conv_w)]
    return pl.pallas_call(
        body, name="gather_inproj", out_shape=out_shape,
        in_specs=[HBM_SPEC] * 7, out_specs=[VMEM_SPEC] * 5 + [HBM_SPEC] * 3,
        scratch_shapes=[pltpu.VMEM(a.shape, F32) for a in (w_in, w_out, w_pg, w_pp, conv_w, nw)]
        + [pltpu.VMEM((2, tm, D), F32), pltpu.VMEM((T, D), BF16),
           pltpu.VMEM((2, tm, DG), F32), pltpu.VMEM((2, tm, SHW), BF16),
           pltpu.SemaphoreType.DMA((6,)), pltpu.SemaphoreType.DMA((2,)), pltpu.SemaphoreType.DMA((2,)),
           pltpu.SemaphoreType.DMA((2,)), pltpu.SemaphoreType.DMA((2,)),
           pltpu.SemaphoreType.DMA((9,)), pltpu.SemaphoreType.DMA((9,))],
        compiler_params=pltpu.CompilerParams(vmem_limit_bytes=48 * MIB),
    )(*hbm_args)


def _gather_rest_start(lands):
    n = len(lands)

    def body(*refs):
        land_in = refs[0:n]
        ssem, rsem = refs[n], refs[n + 1]
        token = refs[2 * n + 2]
        x, y, c, chips = _mesh_pos()
        kme = 2 * x + y
        for p, land in enumerate(land_in):
            hrows = land.shape[1] // 2
            mine = pl.ds(pl.multiple_of(c * hrows, 128), hrows)
            for px, py in chips:
                for pc in range(2):
                    _remote(land.at[kme, mine], land.at[kme, mine], ssem.at[p], rsem.at[p], (px, py, pc)).start()
        token[...] = jnp.zeros_like(token)

    out_shape = ([pltpu.SemaphoreType.DMA((n,)), pltpu.SemaphoreType.DMA((n,))]
                 + [pltpu.HBM(a.shape, a.dtype) for a in lands] + [jax.ShapeDtypeStruct((8, 128), F32)])
    outs = pl.pallas_call(
        body, name="gather_rest_start", out_shape=out_shape,
        in_specs=[HBM_SPEC] * n, out_specs=[SEM_SPEC, SEM_SPEC] + [HBM_SPEC] * n + [VMEM_SPEC],
        input_output_aliases={i: 2 + i for i in range(n)},
        compiler_params=pltpu.CompilerParams(has_side_effects=EFFECT),
    )(*[pltpu.with_memory_space_constraint(a, pltpu.HBM) for a in lands])
    return outs[0], outs[1], list(outs[2:2 + n]), outs[2 + n]


def _gather_rest_wait(ssem, rsem, lands, after):
    n = len(lands)

    def body(*refs):
        land_in = refs[0:n]
        ssem_ref, rsem_ref = refs[n], refs[n + 1]
        x, y, c = lax.axis_index("x"), lax.axis_index("y"), lax.axis_index("c")
        for p, land in enumerate(land_in):
            three = land.at[pl.ds(0, 3)]
            cp = _remote(three, three, ssem_ref.at[p], rsem_ref.at[p], (x, y, c))
            cp.wait_send()
            cp.wait_recv()

    outs = pl.pallas_call(
        body, name="gather_rest_wait", out_shape=[pltpu.HBM(a.shape, a.dtype) for a in lands],
        in_specs=[HBM_SPEC] * n + [SEM_SPEC, SEM_SPEC, ANY_SPEC], out_specs=[HBM_SPEC] * n,
        input_output_aliases={i: i for i in range(n)},
        compiler_params=pltpu.CompilerParams(has_side_effects=EFFECT),
    )(*lands, ssem, rsem, after)
    return list(outs)


def _rs_start(name, parts, ks, lands, chip_sums=False):
    n = len(parts)

    def body(*refs):
        part_in, land_in = refs[0:n], refs[n:2 * n]
        ssem, rsem = refs[2 * n], refs[2 * n + 1]
        token = refs[4 * n + 2]
        x, y, c = lax.axis_index("x"), lax.axis_index("y"), lax.axis_index("c")
        kme = 2 * x + y
        me = 4 * x + 2 * y + c
        for p in range(n):
            hrows = land_in[p].shape[1]
            for i, k in enumerate(ks):
                if chip_sums:
                    @pl.when(kme != k)
                    def _():
                        _remote(part_in[p].at[i], land_in[p].at[kme], ssem.at[p], rsem.at[p], (k // 2, k % 2, c)).start()
                    continue
                for pc in range(2):
                    @pl.when(jnp.logical_or(kme != k, c != pc))
                    def _():
                        _remote(part_in[p].at[i, pl.ds(pc * hrows, hrows)], land_in[p].at[me],
                                ssem.at[p], rsem.at[p], (k // 2, k % 2, pc)).start()
        token[...] = jnp.zeros_like(token)

    arrays = list(parts) + list(lands)
    out_shape = ([pltpu.SemaphoreType.DMA((n,)), pltpu.SemaphoreType.DMA((n,))]
                 + [pltpu.HBM(a.shape, a.dtype) for a in arrays] + [jax.ShapeDtypeStruct((8, 128), F32)])
    outs = pl.pallas_call(
        body, name=name, out_shape=out_shape,
        in_specs=[HBM_SPEC] * (2 * n), out_specs=[SEM_SPEC, SEM_SPEC] + [HBM_SPEC] * (2 * n) + [VMEM_SPEC],
        input_output_aliases={i: 2 + i for i in range(2 * n)},
        compiler_params=pltpu.CompilerParams(has_side_effects=EFFECT),
    )(*[pltpu.with_memory_space_constraint(a, pltpu.HBM) for a in arrays])
    return outs[0], outs[1], list(outs[2:2 + n]), list(outs[2 + n:2 + 2 * n]), outs[2 + 2 * n]


def _rs_wait(name, ssem, rsem, parts, ks, lands, after, chip_sums=False):
    n = len(parts)

    def body(*refs):
        part_in, land_in = refs[0:n], refs[n:2 * n]
        ssem_ref, rsem_ref = refs[2 * n], refs[2 * n + 1]
        x, y, c = lax.axis_index("x"), lax.axis_index("y"), lax.axis_index("c")
        kme = 2 * x + y
        for p in range(n):
            piece = land_in[p].at[0]
            for k in ks:
                for pc in range(1 if chip_sums else 2):
                    mine = (kme == k) if chip_sums else jnp.logical_and(kme == k, c == pc)

                    @pl.when(jnp.logical_not(mine))
                    def _():
                        _remote(piece, piece, ssem_ref.at[p], rsem_ref.at[p], (x, y, c)).wait_send()
            owner = kme == ks[0]
            for k in ks[1:]:
                owner = jnp.logical_or(owner, kme == k)

            @pl.when(owner)
            def _():
                others = land_in[p].at[pl.ds(0, land_in[p].shape[0] - 1)]
                _remote(others, others, ssem_ref.at[p], rsem_ref.at[p], (x, y, c)).wait_recv()

    arrays = list(parts) + list(lands)
    outs = pl.pallas_call(
        body, name=name, out_shape=[pltpu.HBM(a.shape, a.dtype) for a in arrays],
        in_specs=[HBM_SPEC] * (2 * n) + [SEM_SPEC, SEM_SPEC, ANY_SPEC], out_specs=[HBM_SPEC] * (2 * n),
        input_output_aliases={i: i for i in range(2 * n)},
        compiler_params=pltpu.CompilerParams(has_side_effects=EFFECT),
    )(*arrays, ssem, rsem, after)
    return list(outs[0:n]), list(outs[n:2 * n])


def _reduce_exchange(parts, lands):
    shapes = [(2 * l.shape[1], l.shape[2]) for l in lands]
    step = 128

    def body(in01, in23, pout, ppg, ppp, l_in, l_out, l_pg, l_pp, g_in, g_out, g_pg, g_pp, ssem, rsem):
        x, y, c = lax.axis_index("x"), lax.axis_index("y"), lax.axis_index("c")
        kme = 2 * x + y
        me = 4 * x + 2 * y + c
        sibling = (x, y, 1 - c)
        sends = []
        for p, (land, gout) in enumerate([(l_in, g_in), (l_out, g_out), (l_pg, g_pg), (l_pp, g_pp)]):
            hrows = land.shape[1]
            mine0 = pl.multiple_of(c * hrows, step)
            for r0 in range(0, hrows, step):
                rs = pl.ds(mine0 + r0, step)
                if p == 0:
                    own = jnp.where(kme >= 2, in23[kme & 1, r0:r0 + step, :], in01[kme & 1, r0:r0 + step, :])
                    slot = kme
                else:
                    own = (pout, ppg, ppp)[p - 1][kme, rs, :]
                    slot = me
                s = jnp.zeros((step, land.shape[2]), F32)
                for j in range(land.shape[0]):
                    s = s + jnp.where(slot == j, own, land[j, r0:r0 + step, :]).astype(F32)
                gout[rs, :] = s
            mine = pl.ds(mine0, hrows)
            cp = _remote(gout.at[mine], gout.at[mine], ssem.at[p], rsem.at[p], sibling)
            cp.start()
            sends.append(cp)
        for p, (land, gout) in enumerate([(l_in, g_in), (l_out, g_out), (l_pg, g_pg), (l_pp, g_pp)]):
            hrows = land.shape[1]
            other = pl.ds(pl.multiple_of((1 - c) * hrows, step), hrows)
            _remote(gout.at[other], gout.at[other], ssem.at[p], rsem.at[p], sibling).wait_recv()
        for cp in sends:
            cp.wait_send()

    return pl.pallas_call(
        body, name="reduce_exchange", out_shape=[jax.ShapeDtypeStruct(s, F32) for s in shapes],
        in_specs=[VMEM_SPEC] * 9, out_specs=[VMEM_SPEC] * 4,
        scratch_shapes=[pltpu.SemaphoreType.DMA((4,)), pltpu.SemaphoreType.DMA((4,))],
        compiler_params=pltpu.CompilerParams(vmem_limit_bytes=48 * MIB),
    )(*parts, *lands)


def _small_allreduce(sm_in, sm_tail, sm_a, sm_b, rg_c):
    PR = DG // NDEV

    def body(in_ref, tail_ref, a_ref, b_ref, rg_ref, v_out, rg_out, vbuf, vrecv, rgrecv, ssem, rsem):
        x, y, c = lax.axis_index("x"), lax.axis_index("y"), lax.axis_index("c")
        me = 4 * x + 2 * y + c

        def pair(ref, r0, r1):
            return jnp.concatenate([ref[r0:r0 + 1, :], ref[r1:r1 + 1, :]], axis=1)

        rows = {
            ROW_NORM_MIX: in_ref[0:1, :], ROW_FINAL_NORM: tail_ref[1:2, :], ROW_B_PG: tail_ref[2:3, :],
            ROW_PLE_NORM: tail_ref[3:4, :], ROW_CB_BA: pair(a_ref, 0, 1), ROW_BX_LAM: pair(a_ref, 2, 3),
            ROW_CW01: pair(a_ref, 4, 5), ROW_CW23: pair(a_ref, 6, 7), ROW_HG_LB: pair(b_ref, 2, 3),
            ROW_HG_NW: jnp.concatenate([b_ref[1:2, :], jnp.zeros((1, DG), F32)], axis=1),
            ROW_LOSS: tail_ref[4:5, :],
        }
        vbuf[...] = jnp.zeros_like(vbuf)
        for r, row in rows.items():
            for j in range(NDEV):
                vbuf[j, r:r + 1, :] = row[:, j * 128:(j + 1) * 128]

        def peer(mask):
            px = x ^ ((mask >> 2) & 1)
            py = y ^ ((mask >> 1) & 1)
            pc = c ^ (mask & 1)
            return (px, py, pc), 4 * px + 2 * py + pc

        def rg_rows(r):
            return pl.ds(pl.multiple_of(r * PR, PR), PR)

        first = []
        for mask in range(1, NDEV):
            dev, r = peer(mask)
            i = mask - 1
            cp = _remote(vbuf.at[r], vrecv.at[i], ssem.at[i], rsem.at[i], dev)
            cp.start()
            first.append(cp)
            cp = _remote(rg_ref.at[rg_rows(r)], rgrecv.at[i], ssem.at[7 + i], rsem.at[7 + i], dev)
            cp.start()
            first.append(cp)
        sv = vbuf[me]
        sr = rg_ref[rg_rows(me), :]
        for i in range(NDEV - 1):
            first[2 * i].wait_recv()
            first[2 * i + 1].wait_recv()
            sv = sv + vrecv[i]
            sr = sr + rgrecv[i]
        v_out[me] = sv
        rg_out[rg_rows(me), :] = sr
        second = []
        for mask in range(1, NDEV):
            dev, r = peer(mask)
            i = mask - 1
            cp = _remote(v_out.at[me], v_out.at[me], ssem.at[14 + i], rsem.at[14 + i], dev)
            cp.start()
            second.append(cp)
            cp = _remote(rg_out.at[rg_rows(me)], rg_out.at[rg_rows(me)], ssem.at[21 + i], rsem.at[21 + i], dev)
            cp.start()
            second.append(cp)
        for mask in range(1, NDEV):
            dev, r = peer(mask)
            i = mask - 1
            _remote(v_out.at[r], v_out.at[r], ssem.at[14 + i], rsem.at[14 + i], dev).wait_recv()
            _remote(rg_out.at[rg_rows(r)], rg_out.at[rg_rows(r)], ssem.at[21 + i], rsem.at[21 + i], dev).wait_recv()
        for cp in first + second:
            cp.wait_send()

    return pl.pallas_call(
        body, name="small_allreduce",
        out_shape=[jax.ShapeDtypeStruct((NDEV, VROWS, 128), F32), jax.ShapeDtypeStruct((DG, 128), F32)],
        in_specs=[VMEM_SPEC] * 5, out_specs=[VMEM_SPEC] * 2,
        scratch_shapes=[pltpu.VMEM((NDEV, VROWS, 128), F32), pltpu.VMEM((NDEV - 1, VROWS, 128), F32),
                        pltpu.VMEM((NDEV - 1, PR, 128), F32),
                        pltpu.SemaphoreType.DMA((28,)), pltpu.SemaphoreType.DMA((28,))],
    )(sm_in, sm_tail, sm_a, sm_b, rg_c)


def _final_reduce(parts, lands, sm_in, sm_tail, sm_a, sm_b, rg_c):
    shapes = [(2 * l.shape[1], l.shape[2]) for l in lands]
    step = 128
    PR = DG // NDEV

    def body(in01, in23, pout, ppg, ppp, l_in, l_out, l_pg, l_pp, in_h, tail_h, a_h, b_h, rg_h,
             g_in, g_out, g_pg, g_pp, v_out, rg_out, vbuf, vrecv, rgrecv,
             lb_in, lb_out, lb_pg, lb_pp, ob_in, ob_out, ob_pg, ob_pp,
             in_ref, tail_ref, a_ref, b_ref, rg_ref, lsem, ssem_l, bsem_s, bsem_r, ssem, rsem):
        x, y, c = lax.axis_index("x"), lax.axis_index("y"), lax.axis_index("c")
        kme = 2 * x + y
        me = 4 * x + 2 * y + c
        sibling = (x, y, 1 - c)
        small_fetch = [pltpu.make_async_copy(src, dst, ssem_l.at[i]) for i, (src, dst) in enumerate(
            [(in_h, in_ref), (tail_h, tail_ref), (a_h, a_ref), (b_h, b_ref), (rg_h, rg_ref)])]
        for cp in small_fetch:
            cp.start()

        lands_hbm = [l_in, l_out, l_pg, l_pp]
        land_bufs = [lb_in, lb_out, lb_pg, lb_pp]
        own_bufs = [ob_in, ob_out, ob_pg, ob_pp]
        fetches = []
        for p in range(4):
            cp = pltpu.make_async_copy(lands_hbm[p], land_bufs[p], lsem.at[p])
            cp.start()
            fetches.append(cp)

        @pl.when(kme >= 2)
        def _():
            pltpu.make_async_copy(in23.at[kme & 1], ob_in, lsem.at[4]).start()

        @pl.when(kme < 2)
        def _():
            pltpu.make_async_copy(in01.at[kme & 1], ob_in, lsem.at[4]).start()

        own_fetches = [pltpu.make_async_copy(in01.at[0], ob_in, lsem.at[4])]
        for p, part in enumerate([pout, ppg, ppp]):
            hrows = own_bufs[p + 1].shape[0]
            cp = pltpu.make_async_copy(part.at[kme, pl.ds(pl.multiple_of(c * hrows, step), hrows)],
                                       own_bufs[p + 1], lsem.at[5 + p])
            cp.start()
            own_fetches.append(cp)

        def pair(ref, r0, r1):
            return jnp.concatenate([ref[r0:r0 + 1, :], ref[r1:r1 + 1, :]], axis=1)

        for cp in small_fetch:
            cp.wait()
        rows = {
            ROW_NORM_MIX: in_ref[0:1, :], ROW_FINAL_NORM: tail_ref[1:2, :], ROW_B_PG: tail_ref[2:3, :],
            ROW_PLE_NORM: tail_ref[3:4, :], ROW_CB_BA: pair(a_ref, 0, 1), ROW_BX_LAM: pair(a_ref, 2, 3),
            ROW_CW01: pair(a_ref, 4, 5), ROW_CW23: pair(a_ref, 6, 7), ROW_HG_LB: pair(b_ref, 2, 3),
            ROW_HG_NW: jnp.concatenate([b_ref[1:2, :], jnp.zeros((1, DG), F32)], axis=1),
            ROW_LOSS: tail_ref[4:5, :],
        }
        vbuf[...] = jnp.zeros_like(vbuf)
        for r, row in rows.items():
            for j in range(NDEV):
                vbuf[j, r:r + 1, :] = row[:, j * 128:(j + 1) * 128]

        def peer(mask):
            px = x ^ ((mask >> 2) & 1)
            py = y ^ ((mask >> 1) & 1)
            pc = c ^ (mask & 1)
            return (px, py, pc), 4 * px + 2 * py + pc

        def rg_rows(r):
            return pl.ds(pl.multiple_of(r * PR, PR), PR)

        first = []
        for mask in range(1, NDEV):
            dev, r = peer(mask)
            i = mask - 1
            cp = _remote(vbuf.at[r], vrecv.at[i], ssem.at[i], rsem.at[i], dev)
            cp.start()
            first.append(cp)
            cp = _remote(rg_ref.at[rg_rows(r)], rgrecv.at[i], ssem.at[7 + i], rsem.at[7 + i], dev)
            cp.start()
            first.append(cp)

        big = [(lb_in, g_in), (lb_out, g_out), (lb_pg, g_pg), (lb_pp, g_pp)]
        swaps = []
        for p, (land, gout) in enumerate(big):
            fetches[p].wait()
            own_fetches[p].wait()
            hrows = land.shape[1]
            mine0 = pl.multiple_of(c * hrows, step)
            slot = kme if p == 0 else me
            for r0 in range(0, hrows, step):
                rs = pl.ds(mine0 + r0, step)
                own = own_bufs[p][r0:r0 + step, :]
                s = jnp.zeros((step, land.shape[2]), F32)
                for j in range(land.shape[0]):
                    s = s + jnp.where(slot == j, own, land[j, r0:r0 + step, :]).astype(F32)
                gout[rs, :] = s
            mine = pl.ds(mine0, hrows)
            cp = _remote(gout.at[mine], gout.at[mine], bsem_s.at[p], bsem_r.at[p], sibling)
            cp.start()
            swaps.append(cp)

        sv = vbuf[me]
        sr = rg_ref[rg_rows(me), :]
        for i in range(NDEV - 1):
            first[2 * i].wait_recv()
            first[2 * i + 1].wait_recv()
            sv = sv + vrecv[i]
            sr = sr + rgrecv[i]
        v_out[me] = sv
        rg_out[rg_rows(me), :] = sr
        second = []
        for mask in range(1, NDEV):
            dev, r = peer(mask)
            i = mask - 1
            cp = _remote(v_out.at[me], v_out.at[me], ssem.at[14 + i], rsem.at[14 + i], dev)
            cp.start()
            second.append(cp)
            cp = _remote(rg_out.at[rg_rows(me)], rg_out.at[rg_rows(me)], ssem.at[21 + i], rsem.at[21 + i], dev)
            cp.start()
            second.append(cp)
        for p, (land, gout) in enumerate(big):
            hrows = land.shape[1]
            other = pl.ds(pl.multiple_of((1 - c) * hrows, step), hrows)
            _remote(gout.at[other], gout.at[other], bsem_s.at[p], bsem_r.at[p], sibling).wait_recv()
        for mask in range(1, NDEV):
            dev, r = peer(mask)
            i = mask - 1
            _remote(v_out.at[r], v_out.at[r], ssem.at[14 + i], rsem.at[14 + i], dev).wait_recv()
            _remote(rg_out.at[rg_rows(r)], rg_out.at[rg_rows(r)], ssem.at[21 + i], rsem.at[21 + i], dev).wait_recv()
        for cp in first + swaps + second:
            cp.wait_send()

    out_shape = [jax.ShapeDtypeStruct(s, F32) for s in shapes]
    out_shape += [jax.ShapeDtypeStruct((NDEV, VROWS, 128), F32), jax.ShapeDtypeStruct((DG, 128), F32)]
    outs = pl.pallas_call(
        body, name="final_reduce", out_shape=out_shape,
        in_specs=[HBM_SPEC] * 14, out_specs=[VMEM_SPEC] * 6,
        scratch_shapes=[pltpu.VMEM((NDEV, VROWS, 128), F32), pltpu.VMEM((NDEV - 1, VROWS, 128), F32),
                        pltpu.VMEM((NDEV - 1, PR, 128), F32)]
        + [pltpu.VMEM(l.shape, BF16) for l in lands]
        + [pltpu.VMEM(l.shape[1:], BF16) for l in lands]
        + [pltpu.VMEM(a.shape, F32) for a in (sm_in, sm_tail, sm_a, sm_b, rg_c)]
        + [pltpu.SemaphoreType.DMA((8,)), pltpu.SemaphoreType.DMA((5,)),
                        pltpu.SemaphoreType.DMA((4,)), pltpu.SemaphoreType.DMA((4,)),
                        pltpu.SemaphoreType.DMA((28,)), pltpu.SemaphoreType.DMA((28,))],
        compiler_params=pltpu.CompilerParams(vmem_limit_bytes=48 * MIB),
    )(*[pltpu.with_memory_space_constraint(a, pltpu.HBM)
        for a in (*parts, *lands, sm_in, sm_tail, sm_a, sm_b, rg_c)])
    return list(outs[0:4]), outs[4], outs[5]


def _adam_rows(w, g, m, v):
    m2 = ADAM_B1 * m + (1.0 - ADAM_B1) * g
    v2 = ADAM_B2 * v + (1.0 - ADAM_B2) * (g * g)
    m_hat = m2 / (1.0 - ADAM_B1 ** ADAM_STEP)
    v_hat = v2 / (1.0 - ADAM_B2 ** ADAM_STEP)
    delta = -ADAM_LR * (m_hat / (jnp.sqrt(v_hat) + ADAM_EPS) + ADAM_WD * w)
    return delta, m2, v2


def _adam_big(gs, ws, ms, vs):
    n = len(gs)
    steps = 8

    def body(*refs):
        ins, outs = refs[:4 * n], refs[4 * n:]
        for i in range(n):
            g, w, m, v = (r[...] for r in ins[4 * i:4 * i + 4])
            d, m2, v2 = _adam_rows(w, g, m, v)
            outs[3 * i][...] = d
            outs[3 * i + 1][...] = m2
            outs[3 * i + 2][...] = v2

    in_specs, out_specs, out_shape, args = [], [], [], []
    for g, w, m, v in zip(gs, ws, ms, vs):
        r, c = w.shape
        spec = lambda: pl.BlockSpec((r // steps, c), lambda i: (i, 0))
        in_specs += [spec() for _ in range(4)]
        out_specs += [spec() for _ in range(3)]
        out_shape += [jax.ShapeDtypeStruct((r, c), F32)] * 3
        args += [pltpu.with_memory_space_constraint(a, pltpu.HBM) for a in (g, w, m, v)]
    outs = pl.pallas_call(
        body, name="adam_big", grid=(steps,), in_specs=in_specs, out_specs=out_specs, out_shape=out_shape,
        compiler_params=_cparams(("parallel",), 32),
    )(*args)
    return [tuple(outs[3 * i:3 * i + 3]) for i in range(n)]


_VEC_PARAMS = [
    ("norm_mix_w", ROW_NORM_MIX, 0, D), ("final_norm_w", ROW_FINAL_NORM, 0, D),
    ("b_ple_gate", ROW_B_PG, 0, D), ("ple_norm_w", ROW_PLE_NORM, 0, D),
    ("conv_b", ROW_CB_BA, 0, DG), ("rg_ba", ROW_CB_BA, DG, DG),
    ("rg_bx", ROW_BX_LAM, 0, DG), ("rg_lambda", ROW_BX_LAM, DG, DG),
    ("hg_norm_w", ROW_HG_NW, 0, HD),
]
_SMALL_ORDER = [n for n, _, _, _ in _VEC_PARAMS] + ["hg_lb", "conv_w", "rg_wa", "rg_wx"]


def _adam_small(vred, rgred, ws, ms, vs):
    names = _SMALL_ORDER
    n = len(names)

    def body(vred_ref, rg_ref, *refs):
        w_refs = dict(zip(names, refs[0:n]))
        m_refs = dict(zip(names, refs[n:2 * n]))
        v_refs = dict(zip(names, refs[2 * n:3 * n]))
        outs = refs[3 * n:]
        o_refs = {nm: outs[4 * i:4 * i + 4] for i, nm in enumerate(names)}
        kme = 2 * lax.axis_index("x") + lax.axis_index("y")

        def update(nm, g, idx):
            d, m2, v2 = _adam_rows(w_refs[nm][idx], g, m_refs[nm][idx], v_refs[nm][idx])
            og, od, om, ov = o_refs[nm]
            og[idx] = g
            od[idx] = d
            om[idx] = m2
            ov[idx] = v2

        def packed(row, lane0, width):
            return jnp.concatenate([vred_ref[j, row:row + 1, :] for j in range(lane0 // 128, (lane0 + width) // 128)],
                                   axis=1)

        everything = (slice(None), slice(None))
        for nm, row, lane0, width in _VEC_PARAMS:
            update(nm, packed(row, lane0, width), everything)
        for r in range(2):
            update("hg_lb", packed(ROW_HG_LB, r * DG, DG), (slice(r, r + 1), slice(None)))
        for j in range(4):
            g = vred_ref[(j % 2) * 4 + kme, ROW_CW01 + j // 2:ROW_CW01 + j // 2 + 1, :]
            update("conv_w", g, (slice(j, j + 1), slice(None)))
        for r0 in range(0, DG, 128):
            rs = (slice(r0, r0 + 128), slice(None))
            both = rg_ref[r0:r0 + 128, :]
            update("rg_wa", both[:, 0:RGB], rs)
            update("rg_wx", pltpu.roll(both, RGB, 1)[:, 0:RGB], rs)

    args = [vred, rgred] + [d[nm] for d in (ws, ms, vs) for nm in names]
    out_shape = []
    for nm in names:
        out_shape += [jax.ShapeDtypeStruct(ws[nm].shape, F32)] * 4
    whole = lambda s: pl.BlockSpec(s.shape, lambda i, nd=len(s.shape): (0,) * nd)
    outs = pl.pallas_call(
        body, name="adam_small", out_shape=out_shape, grid=(1,),
        in_specs=[whole(a) for a in args], out_specs=[whole(s) for s in out_shape],
    )(*args)
    return {nm: tuple(outs[4 * i:4 * i + 4]) for i, nm in enumerate(names)}


def _fwd_inproj(x, nw, w_in_b, dep):
    tm = 512

    def body(x_ref, nw_ref, w_ref, dep_ref, pf_ref, pb_ref, u_ref):
        xv = x_ref[...]
        s = lax.rsqrt(jnp.mean(xv * xv, axis=-1, keepdims=True) + EPS)
        u = (xv * s * nw_ref[...]).astype(BF16)
        u_ref[...] = u
        r = [jnp.dot(u, w_ref[k], preferred_element_type=F32) for k in range(NSHARD)]
        h = DG // 2
        pf_ref[:, 0:DG] = r[0][:, 0:DG]
        pb_ref[:, 0:h] = r[0][:, DG:SHW].astype(BF16)
        pb_ref[:, h:DG] = r[1][:, 0:h].astype(BF16)
        pb_ref[:, DG:2 * DG] = r[1][:, h:SHW].astype(BF16)
        pf_ref[:, DG:2 * DG] = r[2][:, 0:DG]
        pb_ref[:, 2 * DG:2 * DG + h] = r[2][:, DG:SHW].astype(BF16)
        pb_ref[:, 2 * DG + h:3 * DG] = r[3][:, 0:h].astype(BF16)
        pb_ref[:, 3 * DG:4 * DG] = r[3][:, h:SHW].astype(BF16)

    return pl.pallas_call(
        body, name="fwd_inproj", grid=(T // tm,),
        in_specs=[pl.BlockSpec((tm, D), lambda i: (i, 0)),
                  pl.BlockSpec((1, D), lambda i: (0, 0)),
                  pl.BlockSpec((NSHARD, D, SHW), lambda i: (0, 0, 0), pipeline_mode=pl.Buffered(1)), ANY_SPEC],
        out_specs=[pl.BlockSpec((tm, 2 * DG), lambda i: (i, 0)),
                   pl.BlockSpec((tm, 4 * DG), lambda i: (i, 0)),
                   pl.BlockSpec((tm, D), lambda i: (i, 0))],
        out_shape=[jax.ShapeDtypeStruct((T, 2 * DG), F32), jax.ShapeDtypeStruct((T, 4 * DG), BF16),
                   jax.ShapeDtypeStruct((T, D), BF16)],
        compiler_params=_cparams(("parallel",), 48),
    )(x, nw, w_in_b, dep)


def _conv_rows(cw_ref):
    return [jnp.concatenate([cw_ref[k, j:j + 1, :] for k in range(NSHARD)], axis=1) for j in range(4)]


def _rg_conv(xa, prev8, cw, cb, rows):
    taps = [_shift_down(xa, prev8, 3, rows), _shift_down(xa, prev8, 2, rows),
            _shift_down(xa, prev8, 1, rows), xa]
    xc = cb
    for j in range(4):
        xc = xc + taps[j] * cw[j]
    return xc, taps


def _block_mask():
    r = lax.broadcasted_iota(I32, (DG, DG), 0)
    c = lax.broadcasted_iota(I32, (DG, DG), 1)
    return (r >> 6) == (c >> 6)


def _dense_from_blocks(wc):
    j = lax.broadcasted_iota(I32, (RGB, DG), 0)
    c = lax.broadcasted_iota(I32, (RGB, DG), 1)
    spread = _mm_exact(wc, ((c & (RGB - 1)) == j).astype(F32))
    return jnp.where(_block_mask(), spread, 0.0)


def _blocks_from_dense(da, dx):
    c = lax.broadcasted_iota(I32, (DG, 128), 0)
    j = lax.broadcasted_iota(I32, (DG, 128), 1)
    hit = (c & (RGB - 1)) == (j & (RGB - 1))
    mask = _block_mask()
    return (_mm_exact(jnp.where(mask, da, 0.0), (hit & (j < RGB)).astype(F32))
            + _mm_exact(jnp.where(mask, dx, 0.0), (hit & (j >= RGB)).astype(F32)))


def _rg_gates(xc, wa, ba, wx, bx, sp, first_row):
    r = _sigmoid(_mm(xc, wa) + ba)
    i = _sigmoid(_mm(xc, wx) + bx)
    log_a = (-RG_C) * r * sp
    a = jnp.exp(log_a)
    a2 = a * a
    one_m_a2 = -jnp.tanh(log_a) * (a2 + 1.0)
    mult = jnp.where(first_row, 1.0, jnp.sqrt(one_m_a2))
    return r, i, a, a2, mult


def _softplus(z):
    return jnp.maximum(z, 0.0) + jnp.log1p(jnp.exp(-jnp.abs(z)))


def _fwd_rglru(pf, pb, cw3, conv_b, wa_c, ba, wx_c, bx, lam, dep):
    tm = 512
    ng = tm // 8

    def body(xa_ref, ga_ref, cw_ref, cb_ref, wa_ref, ba_ref, wx_ref, bx_ref, lam_ref, dep_ref,
             h_ref, ya_ref, a_s, u_s, tail_s, hc_s, wa_s, wx_s):
        i = pl.program_id(0)

        @pl.when(i == 0)
        def _():
            tail_s[...] = jnp.zeros_like(tail_s)
            hc_s[...] = jnp.zeros_like(hc_s)
            wa_s[...] = _dense_from_blocks(wa_ref[...]).astype(BF16)
            wx_s[...] = _dense_from_blocks(wx_ref[...]).astype(BF16)

        rows = lax.broadcasted_iota(I32, (tm, DG), 0)
        xa = xa_ref[...]
        xc, _ = _rg_conv(xa, tail_s[...], _conv_rows(cw_ref), cb_ref[...], rows)
        tail_s[...] = xa[tm - 8:tm, :]
        sp = _softplus(-lam_ref[...])
        rp = _mm(xc, wa_s[...]) + ba_ref[...]
        ip = _mm(xc, wx_s[...]) + bx_ref[...]
        rb = 64
        rows_b = lax.broadcasted_iota(I32, (rb, DG), 0)
        rows8 = rows_b & 7
        carry = hc_s[0:1, :]
        for b0 in range(0, tm, rb):
            sl = slice(b0, b0 + rb)
            r = _sigmoid(rp[sl])
            ig = _sigmoid(ip[sl])
            log_a = (-RG_C) * r * sp
            av = jnp.exp(log_a)
            mult = jnp.sqrt(-jnp.tanh(log_a) * (av * av + 1.0))
            if b0 == 0:
                mult = jnp.where((rows_b + i * tm) == 0, 1.0, mult)
            uv = mult * (ig * xc[sl])
            for d in (1, 2, 4):
                keep = rows8 >= d
                uv = uv + av * jnp.where(keep, _roll_in_groups(uv, d), 0.0)
                av = av * jnp.where(keep, _roll_in_groups(av, d), 1.0)
            ga = ga_ref[sl, :].astype(F32)
            gate = ga * _sigmoid(ga)
            hs = []
            for g in range(rb // 8):
                gs = slice(g * 8, (g + 1) * 8)
                hv = uv[gs] + av[gs] * carry
                carry = hv[7:8, :]
                hs.append(hv)
            hb = jnp.concatenate(hs, axis=0)
            h_ref[sl, :] = hb
            ya_ref[sl, :] = (hb * gate).astype(BF16)
        hc_s[0:1, :] = carry

    vec = lambda: pl.BlockSpec((1, DG), lambda i: (0, 0))
    blocks = lambda: pl.BlockSpec((DG, RGB), lambda i: (0, 0))
    return pl.pallas_call(
        body, name="fwd_rglru", grid=(T // tm,),
        in_specs=[pl.BlockSpec((tm, DG), lambda i: (i, 0)),
                  pl.BlockSpec((tm, DG), lambda i: (i, 0)),
                  pl.BlockSpec((NSHARD, 4, 128), lambda i: (0, 0, 0)), vec(),
                  blocks(), vec(), blocks(), vec(), vec(), ANY_SPEC],
        out_specs=[pl.BlockSpec((tm, DG), lambda i: (i, 0)),
                   pl.BlockSpec((tm, DG), lambda i: (i, 0))],
        out_shape=[jax.ShapeDtypeStruct((T, DG), F32), pltpu.HBM((T, DG), BF16)],
        scratch_shapes=[pltpu.VMEM((tm, DG), F32), pltpu.VMEM((tm, DG), F32),
                        pltpu.VMEM((8, DG), F32), pltpu.VMEM((8, DG), F32),
                        pltpu.VMEM((DG, DG), BF16), pltpu.VMEM((DG, DG), BF16)],
        compiler_params=_cparams(("arbitrary",), 48),
    )(pf, pb, cw3, conv_b, wa_c, ba, wx_c, bx, lam, dep)


def _hg_lower_bound(lb_ref):
    return _sig_pair(lb_ref[0:1, :] - lb_ref[1:2, :])


def _hg_gates(fz, lb, one_m_lb):
    sg, sn = _sig_pair(fz)
    f = lb + one_m_lb * sg
    return sg, sn, f, jnp.log(f), one_m_lb * sn


def _tri(lower):
    r = lax.broadcasted_iota(I32, (CH, CH), 0)
    c = lax.broadcasted_iota(I32, (CH, CH), 1)
    return (r >= c) if lower else (r <= c)


def _chunk_cumsum(v, rows64):
    for d in (1, 2, 4, 8, 16, 32):
        v = v + jnp.where(rows64 >= d, pltpu.roll(v, d, 0), 0.0)
    return v


def _chunk_rev_cumsum(v, rows64):
    n = v.shape[0]
    for d in (1, 2, 4, 8, 16, 32):
        v = v + jnp.where(rows64 < CH - d, pltpu.roll(v, n - d, 0), 0.0)
    return v


def _hg_recompute(q_ref, f_ref, lb, one_m_lb, rows64, eb_s, enb_s, ekd_s, qe_s, ke_s, kd_s, dec_s):
    nc = q_ref.shape[0] // CH
    sg, sn, f, logf, k = _hg_gates(f_ref[...], lb, one_m_lb)
    q = q_ref[...].astype(F32)
    sq = _sigmoid(q)
    qs = q * sq * (HD ** -0.5)
    b = _chunk_cumsum(logf, rows64)
    for c in range(nc):
        rs = slice(c * CH, (c + 1) * CH)
        b_c = b[rs]
        bl = b_c[CH - 1:CH, :]
        eb, enb, ekd = jnp.exp(b_c), jnp.exp(-b_c), jnp.exp(bl - b_c)
        if eb_s is not None:
            eb_s[rs, :] = eb
            enb_s[rs, :] = enb
            ekd_s[rs, :] = ekd
        qe_s[rs, :] = (qs[rs] * eb).astype(BF16)
        ke_s[rs, :] = (k[rs] * enb).astype(BF16)
        kd_s[rs, :] = (k[rs] * ekd).astype(BF16)
        dec_s[c:c + 1, :] = jnp.exp(bl)
    return sg, sn, f, k, q, sq, qs


def _fwd_hgrn2(pf, pb, hg_lb, hg_nw):
    tm = 512
    nc = tm // CH

    def body(q_ref, f_ref, v_ref, g_ref, lb_ref, nw_ref, yb_ref, o_ref, sp_ref,
             st_s, qe_s, ke_s, kd_s, vb_s, dec_s, p_s, ds_s):
        i = pl.program_id(0)

        @pl.when(i == 0)
        def _():
            st_s[...] = jnp.zeros_like(st_s)

        lb, one_m_lb = _hg_lower_bound(lb_ref)
        rows64 = lax.broadcasted_iota(I32, (tm, DG), 0) & (CH - 1)
        _hg_recompute(q_ref, f_ref, lb, one_m_lb, rows64, None, None, None, qe_s, ke_s, kd_s, dec_s)
        vb_s[...] = v_ref[...]
        mask = _tri(True)
        items = [(c, hd, slice(c * CH, (c + 1) * CH), slice(hd * HD, (hd + 1) * HD))
                 for c in range(nc) for hd in range(NH)]
        for c, hd, rs, cols in items:
            p_s[c * NH + hd] = jnp.where(mask, _mm_nt(qe_s[rs, cols], ke_s[rs, cols]), 0.0).astype(BF16)
            ds_s[c * NH + hd] = _mm_tn(vb_s[rs, cols], kd_s[rs, cols])
        for c, hd, rs, cols in items:
            st = st_s[hd]
            sp_ref[hd, c] = st
            st_s[hd] = st * dec_s[c:c + 1, cols] + ds_s[c * NH + hd]
        for c, hd, rs, cols in items:
            o_ref[rs, cols] = _mm(p_s[c * NH + hd], vb_s[rs, cols]) + _mm_nt(qe_s[rs, cols], sp_ref[hd, c])
        nw = nw_ref[...]
        for hd in range(NH):
            cols = slice(hd * HD, (hd + 1) * HD)
            o = o_ref[:, cols]
            so = lax.rsqrt(jnp.mean(o * o, axis=-1, keepdims=True) + EPS)
            g = g_ref[:, cols].astype(F32)
            sg = _sigmoid(g)
            yb_ref[:, cols] = (o * so * nw * (g * sg)).astype(BF16)

    col = lambda j: pl.BlockSpec((tm, DG), lambda i: (i, j))
    return pl.pallas_call(
        body, name="fwd_hgrn2", grid=(T // tm,),
        in_specs=[col(1), col(1), col(2), col(3),
                  pl.BlockSpec((2, DG), lambda i: (0, 0)),
                  pl.BlockSpec((1, HD), lambda i: (0, 0))],
        out_specs=[pl.BlockSpec((tm, DG), lambda i: (i, 0)),
                   pl.BlockSpec((tm, DG), lambda i: (i, 0)),
                   pl.BlockSpec((NH, nc, HD, HD), lambda i: (0, i, 0, 0))],
        out_shape=[pltpu.HBM((T, DG), BF16), jax.ShapeDtypeStruct((T, DG), F32),
                   jax.ShapeDtypeStruct((NH, NCHUNK, HD, HD), F32)],
        scratch_shapes=[pltpu.VMEM((NH, HD, HD), F32),
                        pltpu.VMEM((tm, DG), BF16), pltpu.VMEM((tm, DG), BF16), pltpu.VMEM((tm, DG), BF16),
                        pltpu.VMEM((tm, DG), BF16), pltpu.VMEM((nc, DG), F32),
                        pltpu.VMEM((nc * NH, CH, CH), BF16), pltpu.VMEM((nc * NH, HD, HD), F32)],
        compiler_params=_cparams(("arbitrary",), 48),
    )(pb, pf, pb, pb, hg_lb, hg_nw)


def _tail_fwd_bwd(x, p, tgt, ya, yb, w_out_b, w_pg_b, w_pp_b, ple_nw, b_pg, fnw):
    tm = 512
    nt = T // tm
    QR = D // NSHARD

    def body(x_ref, p_ref, t_ref, ya_ref, yb_ref, wo_ref, wg_ref, wp_ref, pw_ref, b_ref, fw_ref,
             dh1_ref, dyab_ref, dwo_ref, dwg_ref, dwp_ref, sm_ref, dwo_s, dwg_s, dwp_s):
        i = pl.program_id(0)

        @pl.when(i == 0)
        def _():
            dwo_s[...] = jnp.zeros_like(dwo_s)
            dwg_s[...] = jnp.zeros_like(dwg_s)
            dwp_s[...] = jnp.zeros_like(dwp_s)
            sm_ref[...] = jnp.zeros_like(sm_ref)

        ya = ya_ref[...]
        yb = yb_ref[...]
        pv = p_ref[...].astype(BF16)
        pw = pw_ref[...]
        fw = fw_ref[...]
        h1 = x_ref[...] + _mm(ya, wo_ref[0:DG, :]) + _mm(yb, wo_ref[DG:D, :])
        s2 = lax.rsqrt(jnp.mean(h1 * h1, axis=-1, keepdims=True) + EPS)
        n2h = h1 * s2
        n2 = (n2h * pw).astype(BF16)
        z = _mm(n2, wg_ref[...]) + b_ref[...]
        gate = _sigmoid(z)
        pp = jnp.concatenate([_mm(pv, wp_ref[k]) for k in range(NSHARD)], axis=1)
        h2 = h1 + gate * pp
        s3 = lax.rsqrt(jnp.mean(h2 * h2, axis=-1, keepdims=True) + EPS)
        hn = h2 * s3
        err = hn * fw - t_ref[...]
        sm_ref[0:1, :] += _rowsum(err * err)
        dy = err * (1.0 / D)
        sm_ref[1:2, :] += _rowsum(dy * hn)
        g3 = dy * fw
        dh2 = s3 * (g3 - hn * jnp.mean(g3 * hn, axis=-1, keepdims=True))
        dpp = (dh2 * gate).astype(BF16)
        dz = dh2 * pp * gate * (1.0 - gate)
        sm_ref[2:3, :] += _rowsum(dz)
        dzb = dz.astype(BF16)
        dwg_s[...] += _mm_tn(n2, dzb)
        dn2 = _mm_nt(dzb, wg_ref[...])
        for k in range(NSHARD):
            dwp_s[k] += _mm_tn(pv, dpp[:, k * PLE:(k + 1) * PLE])
        sm_ref[3:4, :] += _rowsum(dn2 * n2h)
        g2 = dn2 * pw
        dh1 = dh2 + s2 * (g2 - n2h * jnp.mean(g2 * n2h, axis=-1, keepdims=True))
        dh1_ref[...] = dh1
        dh1b = dh1.astype(BF16)
        dyab_ref[...] = _mm_nt(dh1b, wo_ref[...])
        dwo_s[0:DG, :] += _mm_tn(ya, dh1b)
        dwo_s[DG:D, :] += _mm_tn(yb, dh1b)

        @pl.when(i == nt - 1)
        def _():
            total = jnp.sum(sm_ref[0:1, :], axis=-1, keepdims=True) * (0.5 / D)
            sm_ref[4:5, :] = jnp.broadcast_to(total, (1, D))
            for k in range(NSHARD):
                dwo_ref[k] = dwo_s[k * QR:(k + 1) * QR, :].astype(BF16)
                dwg_ref[k] = dwg_s[k * QR:(k + 1) * QR, :].astype(BF16)
                dwp_ref[k] = dwp_s[k].astype(BF16)

    row = lambda w: pl.BlockSpec((tm, w), lambda i: (i, 0))
    const2 = lambda s: pl.BlockSpec(s, lambda i: (0, 0), pipeline_mode=pl.Buffered(1))
    const3 = lambda s: pl.BlockSpec(s, lambda i: (0, 0, 0), pipeline_mode=pl.Buffered(1))
    return pl.pallas_call(
        body, name="tail_fwd_bwd", grid=(nt,),
        in_specs=[row(D), row(PLE), row(D), row(DG), row(DG),
                  const2((D, D)), const2((D, D)), const3((NSHARD, PLE, PLE)),
                  const2((1, D)), const2((1, D)), const2((1, D))],
        out_specs=[row(D), row(D), const3((NSHARD, QR, D)), const3((NSHARD, QR, D)),
                   const3((NSHARD, PLE, PLE)), const2((8, D))],
        out_shape=[jax.ShapeDtypeStruct((T, D), F32), jax.ShapeDtypeStruct((T, D), F32),
                   jax.ShapeDtypeStruct((NSHARD, QR, D), BF16), jax.ShapeDtypeStruct((NSHARD, QR, D), BF16),
                   jax.ShapeDtypeStruct((NSHARD, PLE, PLE), BF16), jax.ShapeDtypeStruct((8, D), F32)],
        scratch_shapes=[pltpu.VMEM((D, D), F32), pltpu.VMEM((D, D), F32), pltpu.VMEM((NSHARD, PLE, PLE), F32)],
        compiler_params=_cparams(("arbitrary",), 62),
    )(x, p, tgt, pltpu.with_memory_space_constraint(ya, pltpu.HBM), pltpu.with_memory_space_constraint(yb, pltpu.HBM),
      w_out_b, w_pg_b, w_pp_b, ple_nw, b_pg, fnw)


def _bwd_rglru(pf, pb, h, dyab, cw3, conv_b, wa_c, ba, wx_c, bx, lam, dep):
    tm = 512
    nt = T // tm
    ng = tm // 8

    def body(xa_ref, ga_ref, xp_ref, h_ref, hp_ref, dya_ref, cw_ref, cb_ref, wa_ref, ba_ref, wx_ref, bx_ref,
             lam_ref, dep_ref, da_ref, rg_ref, sm_ref, a_s, g_s, cg_s, nxt_s, wa_s, wx_s, dwa_s, dwx_s):
        i = pl.program_id(0)
        tile = nt - 1 - i

        @pl.when(i == 0)
        def _():
            dwa_s[...] = jnp.zeros_like(dwa_s)
            dwx_s[...] = jnp.zeros_like(dwx_s)
            sm_ref[...] = jnp.zeros_like(sm_ref)
            cg_s[...] = jnp.zeros_like(cg_s)
            nxt_s[...] = jnp.zeros_like(nxt_s)
            wa_s[...] = _dense_from_blocks(wa_ref[...]).astype(BF16)
            wx_s[...] = _dense_from_blocks(wx_ref[...]).astype(BF16)

        rows = lax.broadcasted_iota(I32, (tm, DG), 0)
        has_prev = tile > 0
        xa = xa_ref[...]
        xprev = jnp.where(has_prev, xp_ref[...], 0.0)
        cw = _conv_rows(cw_ref)
        xc, taps = _rg_conv(xa, xprev, cw, cb_ref[...], rows)
        lam_v = lam_ref[...]
        sp = _softplus(-lam_v)
        first_row = (rows + tile * tm) == 0
        r, ig, a, a2, mult = _rg_gates(xc, wa_s[...], ba_ref[...], wx_s[...], bx_ref[...], sp, first_row)
        hv = h_ref[...]
        hprev = jnp.where(has_prev, hp_ref[...], 0.0)
        h_m1 = _shift_down(hv, hprev, 1, rows)
        ga = ga_ref[...].astype(F32)
        sg = _sigmoid(ga)
        dya = dya_ref[...]
        dga = dya * hv * (sg * (1.0 + ga * (1.0 - sg)))

        av = jnp.where(rows == tm - 1, 1.0, pltpu.roll(a, tm - 1, 0))
        gv = dya * (ga * sg)
        rows8 = rows & 7
        for d in (1, 2, 4):
            keep = rows8 < 8 - d
            gv = gv + av * jnp.where(keep, _roll_in_groups(gv, 8 - d), 0.0)
            av = av * jnp.where(keep, _roll_in_groups(av, 8 - d), 1.0)
        a_s[...] = av
        g_s[...] = gv
        carry = cg_s[0:1, :]
        for g in range(ng - 1, -1, -1):
            sl = slice(g * 8, (g + 1) * 8)
            ab, gb = a_s[sl, :], g_s[sl, :]
            g_s[sl, :] = gb + ab * carry
            carry = gb[0:1, :] + ab[0:1, :] * carry
        cg_s[0:1, :] = a[0:1, :] * carry

        gt = g_s[...]
        da = gt * h_m1
        ixc = ig * xc
        di = gt * mult * xc
        dxc = gt * mult * ig
        dlog_a = da * a + jnp.where(first_row, 0.0, gt * ixc * (-a2 / mult))
        sm_ref[3:4, :] += _rowsum(dlog_a * ((-RG_C) * r))
        dpr_f = dlog_a * ((-RG_C) * sp) * r * (1.0 - r)
        dpi_f = di * ig * (1.0 - ig)
        sm_ref[1:2, :] += _rowsum(dpr_f)
        sm_ref[2:3, :] += _rowsum(dpi_f)
        dpr = dpr_f.astype(BF16)
        dpi = dpi_f.astype(BF16)
        xcb = xc.astype(BF16)
        dwa_s[...] += _mm_tn(xcb, dpr)
        dwx_s[...] += _mm_tn(xcb, dpi)
        dxc = dxc + _mm_nt(dpr, wa_s[...]) + _mm_nt(dpi, wx_s[...])
        sm_ref[0:1, :] += _rowsum(dxc)
        for j in range(4):
            sm_ref[4 + j:5 + j, :] += _rowsum(dxc * taps[j])
        nxt = nxt_s[...]
        dxa = (dxc * cw[3] + _shift_up(dxc, nxt, 1, rows) * cw[2]
               + _shift_up(dxc, nxt, 2, rows) * cw[1] + _shift_up(dxc, nxt, 3, rows) * cw[0])
        nxt_s[...] = dxc[0:8, :]
        da_ref[:, 0:DG] = dxa.astype(BF16)
        da_ref[:, DG:D] = dga.astype(BF16)

        @pl.when(i == nt - 1)
        def _():
            _, s_neg = _sig_pair(lam_v)
            sm_ref[3:4, :] = sm_ref[3:4, :] * (-s_neg)
            rg_ref[...] = _blocks_from_dense(dwa_s[...], dwx_s[...])

    vec = lambda: pl.BlockSpec((1, DG), lambda i: (0, 0))
    blocks = lambda: pl.BlockSpec((DG, RGB), lambda i: (0, 0))
    prev8 = lambda: pl.BlockSpec((8, DG), lambda i: (jnp.maximum((nt - 1 - i) * (tm // 8) - 1, 0), 0))
    return pl.pallas_call(
        body, name="bwd_rglru", grid=(nt,),
        in_specs=[pl.BlockSpec((tm, DG), lambda i: (nt - 1 - i, 0)),
                  pl.BlockSpec((tm, DG), lambda i: (nt - 1 - i, 0)),
                  prev8(),
                  pl.BlockSpec((tm, DG), lambda i: (nt - 1 - i, 0)),
                  prev8(),
                  pl.BlockSpec((tm, DG), lambda i: (nt - 1 - i, 0)),
                  pl.BlockSpec((NSHARD, 4, 128), lambda i: (0, 0, 0)), vec(),
                  blocks(), vec(), blocks(), vec(), vec(), ANY_SPEC],
        out_specs=[pl.BlockSpec((tm, D), lambda i: (nt - 1 - i, 0)),
                   pl.BlockSpec((DG, 128), lambda i: (0, 0)),
                   pl.BlockSpec((8, DG), lambda i: (0, 0))],
        out_shape=[jax.ShapeDtypeStruct((T, D), BF16), jax.ShapeDtypeStruct((DG, 128), F32),
                   jax.ShapeDtypeStruct((8, DG), F32)],
        scratch_shapes=[pltpu.VMEM((tm, DG), F32), pltpu.VMEM((tm, DG), F32),
                        pltpu.VMEM((8, DG), F32), pltpu.VMEM((8, DG), F32),
                        pltpu.VMEM((DG, DG), BF16), pltpu.VMEM((DG, DG), BF16),
                        pltpu.VMEM((DG, DG), F32), pltpu.VMEM((DG, DG), F32)],
        compiler_params=_cparams(("arbitrary",), 56),
    )(pf, pb, pf, h, h, dyab, cw3, conv_b, wa_c, ba, wx_c, bx, lam, dep)


def _bwd_hgrn2(pf, pb, o, s_prev, dyab, hg_lb, hg_nw, dep):
    tm = 512
    nt = T // tm
    nc = tm // CH

    def body(q_ref, f_ref, v_ref, g_ref, o_ref, sp_ref, dy_ref, lb_ref, nw_ref, dep_ref, db_ref, sm_ref,
             dst_s, eb_s, enb_s, ekd_s, qe_s, ke_s, kd_s, vb_s, do_s, dec_s, ddec_s, p_s, dp_s,
             g_s, dsta_s, dva_s, dqe_s, dke_s, dkd_s, dlf_s):
        i = pl.program_id(0)

        @pl.when(i == 0)
        def _():
            sm_ref[...] = jnp.zeros_like(sm_ref)
            dst_s[...] = jnp.zeros_like(dst_s)

        lb, one_m_lb = _hg_lower_bound(lb_ref)
        rows64 = lax.broadcasted_iota(I32, (tm, DG), 0) & (CH - 1)
        sg, sn, f, k, q, sq, qs = _hg_recompute(
            q_ref, f_ref, lb, one_m_lb, rows64, eb_s, enb_s, ekd_s, qe_s, ke_s, kd_s, dec_s)
        vb_s[...] = v_ref[...]

        nw = nw_ref[...]
        for hd in range(NH):
            cols = slice(hd * HD, (hd + 1) * HD)
            g = g_ref[:, cols].astype(F32)
            sgg = _sigmoid(g)
            o = o_ref[:, cols]
            so = lax.rsqrt(jnp.mean(o * o, axis=-1, keepdims=True) + EPS)
            oh = o * so
            dyb = dy_ref[:, cols]
            db_ref[:, 3 * DG + hd * HD:3 * DG + (hd + 1) * HD] = (
                dyb * (oh * nw) * (sgg * (1.0 + g * (1.0 - sgg)))).astype(BF16)
            don = dyb * (g * sgg)
            sm_ref[1:2, 0:HD] += _rowsum(don * oh)
            gw = don * nw
            do_s[:, cols] = (so * (gw - oh * jnp.mean(gw * oh, axis=-1, keepdims=True))).astype(BF16)

        mask = _tri(True)
        items = [(c, hd, slice(c * CH, (c + 1) * CH), slice(hd * HD, (hd + 1) * HD))
                 for c in range(nc) for hd in range(NH)]
        for c, hd, rs, cols in items:
            p_s[c * NH + hd] = jnp.where(mask, _mm_nt(qe_s[rs, cols], ke_s[rs, cols]), 0.0).astype(BF16)
            dp_s[c * NH + hd] = jnp.where(mask, _mm_nt(do_s[rs, cols], vb_s[rs, cols]), 0.0).astype(BF16)
        for c, hd, rs, cols in items:
            n = c * NH + hd
            dva_s[rs, cols] = _mm_tn(p_s[n], do_s[rs, cols])
            dqe_s[rs, cols] = _mm(dp_s[n], ke_s[rs, cols])
            dke_s[rs, cols] = _mm_tn(dp_s[n], qe_s[rs, cols])
            g_s[n] = _mm_tn(do_s[rs, cols], qe_s[rs, cols])
        for c, hd, rs, cols in reversed(items):
            n = c * NH + hd
            dst = dst_s[hd]
            dsta_s[n] = dst
            dst_s[hd] = dst * dec_s[c:c + 1, cols] + g_s[n]
        for c, hd, rs, cols in items:
            n = c * NH + hd
            dst = dsta_s[n]
            st_prev = sp_ref[hd, c]
            dv = dva_s[rs, cols] + _mm_nt(kd_s[rs, cols], dst)
            db_ref[rs, 2 * DG + hd * HD:2 * DG + (hd + 1) * HD] = dv.astype(BF16)
            dqe_s[rs, cols] += _mm(do_s[rs, cols], st_prev)
            dkd_s[rs, cols] = _mm(vb_s[rs, cols], dst)
            ddec_s[c:c + 1, cols] = _rowsum(dst * st_prev)

        eb, enb, ekd = eb_s[...], enb_s[...], ekd_s[...]
        dqe, dke, dkd = dqe_s[...], dke_s[...], dkd_s[...]
        t_kd = dkd * (k * ekd)
        rc = _chunk_rev_cumsum(dqe * (qs * eb) - dke * (k * enb) - t_kd, rows64)
        for c in range(nc):
            rs = slice(c * CH, (c + 1) * CH)
            dbl = _rowsum(t_kd[rs]) + ddec_s[c:c + 1, :] * dec_s[c:c + 1, :]
            dlf_s[rs, :] = rc[rs] + dbl
        t = dlf_s[...] / f - (dke * enb + dkd * ekd)
        db_ref[:, DG:2 * DG] = (one_m_lb * sg * sn * t).astype(BF16)
        sm_ref[0:1, :] += _rowsum(sn * t)
        db_ref[:, 0:DG] = (dqe * eb * (sq * (1.0 + q * (1.0 - sq))) * (HD ** -0.5)).astype(BF16)

        @pl.when(i == nt - 1)
        def _():
            dsm = sm_ref[0:1, :] * (lb * one_m_lb)
            sm_ref[2:3, :] = dsm
            sm_ref[3:4, :] = -dsm

    col = lambda j: pl.BlockSpec((tm, DG), lambda i: (nt - 1 - i, j))
    big = lambda dt: pltpu.VMEM((tm, DG), dt)
    return pl.pallas_call(
        body, name="bwd_hgrn2", grid=(nt,),
        in_specs=[col(1), col(1), col(2), col(3),
                  pl.BlockSpec((tm, DG), lambda i: (nt - 1 - i, 0)),
                  pl.BlockSpec((NH, nc, HD, HD), lambda i: (0, nt - 1 - i, 0, 0)),
                  pl.BlockSpec((tm, DG), lambda i: (nt - 1 - i, 1)),
                  pl.BlockSpec((2, DG), lambda i: (0, 0)),
                  pl.BlockSpec((1, HD), lambda i: (0, 0)), ANY_SPEC],
        out_specs=[pl.BlockSpec((tm, 4 * DG), lambda i: (nt - 1 - i, 0)),
                   pl.BlockSpec((8, DG), lambda i: (0, 0))],
        out_shape=[jax.ShapeDtypeStruct((T, 4 * DG), BF16), jax.ShapeDtypeStruct((8, DG), F32)],
        scratch_shapes=[pltpu.VMEM((NH, HD, HD), F32),
                        big(F32), big(F32), big(F32),
                        big(BF16), big(BF16), big(BF16), big(BF16), big(BF16),
                        pltpu.VMEM((nc, DG), F32), pltpu.VMEM((nc, DG), F32),
                        pltpu.VMEM((nc * NH, CH, CH), BF16), pltpu.VMEM((nc * NH, CH, CH), BF16),
                        pltpu.VMEM((nc * NH, HD, HD), F32), pltpu.VMEM((nc * NH, HD, HD), F32),
                        big(F32), big(F32), big(F32), big(F32), big(F32)],
        compiler_params=_cparams(("arbitrary",), 56),
    )(pb, pf, pb, pb, o, s_prev, dyab, hg_lb, hg_nw, dep)


def _dproj_pieces(k, da_ref, db_ref):
    if k == 0:
        return [(da_ref[:, 0:SHW], 0)]
    if k == 1:
        return [(da_ref[:, SHW:D], 0), (db_ref[:, 0:DG], D - SHW)]
    if k == 2:
        return [(db_ref[:, DG:DG + SHW], 0)]
    return [(db_ref[:, DG + SHW:4 * DG], 0)]


def _bwd_inproj_dx(x, dh1, d_a, d_b, w_in_b, nw, dep):
    tm = 512
    nt = T // tm

    def body(x_ref, dh1_ref, da_ref, db_ref, w_ref, nw_ref, dep_ref, dx_ref, sm_ref):
        i = pl.program_id(0)

        @pl.when(i == 0)
        def _():
            sm_ref[...] = jnp.zeros_like(sm_ref)

        du = None
        for k in range(NSHARD):
            for val, off in _dproj_pieces(k, da_ref, db_ref):
                t = _mm_nt(val, w_ref[k, :, off:off + val.shape[1]])
                du = t if du is None else du + t
        xv = x_ref[...]
        s = lax.rsqrt(jnp.mean(xv * xv, axis=-1, keepdims=True) + EPS)
        xh = xv * s
        sm_ref[0:1, :] += _rowsum(du * xh)
        g = du * nw_ref[...]
        dx_ref[...] = dh1_ref[...] + s * (g - xh * jnp.mean(g * xh, axis=-1, keepdims=True))

    row = lambda w: pl.BlockSpec((tm, w), lambda i: (i, 0))
    return pl.pallas_call(
        body, name="bwd_inproj_dx", grid=(nt,),
        in_specs=[row(D), row(D), row(D), row(4 * DG),
                  pl.BlockSpec((NSHARD, D, SHW), lambda i: (0, 0, 0), pipeline_mode=pl.Buffered(1)),
                  pl.BlockSpec((1, D), lambda i: (0, 0)), ANY_SPEC],
        out_specs=[row(D), pl.BlockSpec((8, D), lambda i: (0, 0))],
        out_shape=[jax.ShapeDtypeStruct((T, D), F32), jax.ShapeDtypeStruct((8, D), F32)],
        compiler_params=_cparams(("arbitrary",), 56),
    )(x, dh1, d_a, d_b, w_in_b, nw, dep)


def _bwd_inproj_dw(name, ks, u_b, d_a, d_b):
    tm = 1024
    nt = T // tm
    nk = len(ks)
    H = D // 2
    step = 128
    need_a = 0 in ks or 1 in ks

    def body(*refs):
        if need_a:
            u_ref, da_ref, db_ref, dw_ref, acc, send_s, recv_s, ssem, rsem = refs
        else:
            u_ref, db_ref, dw_ref, acc, send_s, recv_s, ssem, rsem = refs
            da_ref = None
        i = pl.program_id(0)

        @pl.when(i == 0)
        def _():
            acc[...] = jnp.zeros_like(acc)

        u = u_ref[...]
        for j, k in enumerate(ks):
            for val, off in _dproj_pieces(k, da_ref, db_ref):
                acc[j, :, off:off + val.shape[1]] += _mm_tn(u, val)

        @pl.when(i == nt - 1)
        def _():
            x, y, c = lax.axis_index("x"), lax.axis_index("y"), lax.axis_index("c")
            sibling = (x, y, 1 - c)
            mine0 = pl.multiple_of(c * H, step)
            other0 = pl.multiple_of((1 - c) * H, step)
            copies = []
            for j in range(nk):
                for r0 in range(0, H, step):
                    send_s[j, r0:r0 + step, :] = acc[j, pl.ds(other0 + r0, step), :].astype(BF16)
                cp = _remote(send_s.at[j], recv_s.at[j], ssem.at[j], rsem.at[j], sibling)
                cp.start()
                copies.append(cp)
            for j in range(nk):
                copies[j].wait_recv()
                for r0 in range(0, H, step):
                    s = acc[j, pl.ds(mine0 + r0, step), :] + recv_s[j, r0:r0 + step, :].astype(F32)
                    dw_ref[j, r0:r0 + step, :] = s.astype(BF16)
            for cp in copies:
                cp.wait_send()

    row = lambda w: pl.BlockSpec((tm, w), lambda i: (i, 0))
    ins = [u_b] + ([d_a] if need_a else []) + [d_b]
    in_specs = [row(D)] + ([row(D)] if need_a else []) + [row(4 * DG)]
    return pl.pallas_call(
        body, name=name, grid=(nt,), in_specs=in_specs,
        out_specs=pl.BlockSpec((nk, H, SHW), lambda i: (0, 0, 0)),
        out_shape=jax.ShapeDtypeStruct((nk, H, SHW), BF16),
        scratch_shapes=[pltpu.VMEM((nk, D, SHW), F32), pltpu.VMEM((nk, H, SHW), BF16),
                        pltpu.VMEM((nk, H, SHW), BF16),
                        pltpu.SemaphoreType.DMA((nk,)), pltpu.SemaphoreType.DMA((nk,))],
        compiler_params=_cparams(("arbitrary",), 48),
    )(*ins)


_OUT_ORDER = ["norm_mix_w", "w_in", "conv_w", "conv_b", "rg_wa", "rg_ba", "rg_wx", "rg_bx", "rg_lambda", "hg_lb",
              "hg_norm_w", "w_out", "ple_norm_w", "w_ple_gate", "b_ple_gate", "w_ple_proj", "final_norm_w"]
_BIG = ["w_in", "w_out", "w_ple_gate", "w_ple_proj"]


def _small_view(name, a):
    if name in ("rg_wa", "rg_wx"):
        return a.reshape(DG, RGB)
    if name == "conv_w":
        return a.reshape(4, 128)
    if name == "final_norm_w":
        return a.reshape(1, D)
    return a


def _landing(rows, cols):
    return lax.empty((NDEV, rows, cols), BF16)


def kernel(x, p, norm_mix_w, w_in, conv_w, conv_b, rg_wa, rg_ba, rg_wx, rg_bx, rg_lambda, hg_lb, hg_norm_w, w_out, ple_norm_w, w_ple_gate, b_ple_gate, w_ple_proj, final_norm_w, loss_target, m_norm_mix_w, m_w_in, m_conv_w, m_conv_b, m_rg_wa, m_rg_ba, m_rg_wx, m_rg_bx, m_rg_lambda, m_hg_lb, m_hg_norm_w, m_w_out, m_ple_norm_w, m_w_ple_gate, m_b_ple_gate, m_w_ple_proj, m_final_norm_w, v_norm_mix_w, v_w_in, v_conv_w, v_conv_b, v_rg_wa, v_rg_ba, v_rg_wx, v_rg_bx, v_rg_lambda, v_hg_lb, v_hg_norm_w, v_w_out, v_ple_norm_w, v_w_ple_gate, v_b_ple_gate, v_w_ple_proj, v_final_norm_w):
    given = dict(locals())
    x2, p2, tgt = x[0], p[0, 0], loss_target[0]
    hbm = lambda a: pltpu.with_memory_space_constraint(a, pltpu.HBM)
    norm_mix_w, conv_b, rg_ba, rg_bx, rg_lambda, hg_lb, hg_norm_w, ple_norm_w, b_ple_gate = (
        hbm(a) for a in (norm_mix_w, conv_b, rg_ba, rg_bx, rg_lambda, hg_lb, hg_norm_w, ple_norm_w, b_ple_gate))
    wa_c, wx_c = hbm(_small_view("rg_wa", rg_wa)), hbm(_small_view("rg_wx", rg_wx))

    w_in_b, l_out, l_pg, l_pp, cw3, pf, pb, u_b = _gather_inproj(
        x2, norm_mix_w, w_in[0], w_out[0], w_ple_gate[0], w_ple_proj[0], conv_w[0])
    g_ssem, g_rsem, g_lands, tok = _gather_rest_start([l_out, l_pg, l_pp])

    h, ya = _fwd_rglru(pf, pb, cw3, conv_b, wa_c, rg_ba, wx_c, rg_bx, rg_lambda, tok)
    yb, o, s_prev = _fwd_hgrn2(pf, pb, hg_lb, hg_norm_w)
    w_out_b, w_pg_b, w_pp_b = _gather_rest_wait(g_ssem, g_rsem, g_lands, yb)
    dh1, dyab, dwo_b, dwg_b, dwp_b, sm_tail = _tail_fwd_bwd(
        x2, p2, tgt, ya, yb, w_out_b.reshape(D, D), w_pg_b.reshape(D, D), w_pp_b,
        ple_norm_w, b_ple_gate, final_norm_w.reshape(1, D))

    QH = D // NSHARD // 2
    r1 = _rs_start("rs_start_tail", [dwo_b, dwg_b, dwp_b], (0, 1, 2, 3),
                   [_landing(QH, D), _landing(QH, D), _landing(PLE // 2, PLE)])
    d_b, sm_b = _bwd_hgrn2(pf, pb, o, s_prev, dyab, hg_lb, hg_norm_w, r1[4])
    dw23 = _bwd_inproj_dw("bwd_inproj_dw23", (2, 3), u_b, None, d_b)
    r2 = _rs_start("rs_start_in23", [dw23], (2, 3), [lax.empty((NSHARD, D // 2, SHW), BF16)], chip_sums=True)
    d_a, rg_c, sm_a = _bwd_rglru(pf, pb, h, dyab, cw3, conv_b, wa_c, rg_ba, wx_c, rg_bx, rg_lambda, r2[4])
    dw01 = _bwd_inproj_dw("bwd_inproj_dw01", (0, 1), u_b, d_a, d_b)
    r3 = _rs_start("rs_start_in01", [dw01], (0, 1), r2[3], chip_sums=True)
    grad_x, sm_in = _bwd_inproj_dx(x2, dh1, d_a, d_b, w_in_b, norm_mix_w, r3[4])

    parts1, lands1 = _rs_wait("rs_wait_tail", r1[0], r1[1], r1[2], (0, 1, 2, 3), r1[3], sm_in)
    parts2, lands2 = _rs_wait("rs_wait_in23", r2[0], r2[1], r2[2], (2, 3), r3[3], sm_in, chip_sums=True)
    parts3, lands3 = _rs_wait("rs_wait_in01", r3[0], r3[1], r3[2], (0, 1), lands2, sm_in, chip_sums=True)
    g_big, vred, rgred = _final_reduce([parts3[0], parts2[0]] + parts1, lands3 + lands1,
                                       sm_in, sm_tail, sm_a, sm_b, rg_c)

    upd_big = _adam_big(g_big, [given[n][0] for n in _BIG], [given["m_" + n][0] for n in _BIG],
                        [given["v_" + n][0] for n in _BIG])
    small = _adam_small(vred, rgred,
                        {n: _small_view(n, given[n]) for n in _SMALL_ORDER},
                        {n: _small_view(n, given["m_" + n]) for n in _SMALL_ORDER},
                        {n: _small_view(n, given["v_" + n]) for n in _SMALL_ORDER})

    loss = vred[0, ROW_LOSS, 0]
    outs = [loss, grad_x[None]]
    for ki in range(4):
        for n in _OUT_ORDER:
            if n in _BIG:
                i = _BIG.index(n)
                a = g_big[i] if ki == 0 else upd_big[i][ki - 1]
                outs.append(a[None])
            else:
                outs.append(small[n][ki].reshape(given[n].shape))
    return tuple(outs)
```

```python
import jax
import jax.numpy as jnp
from jax import lax
from jax.experimental import pallas as pl
from jax.experimental.pallas import tpu as pltpu

F32 = jnp.float32
BF16 = jnp.bfloat16
I32 = jnp.int32
MESH = pl.DeviceIdType.MESH
HIGHEST = lax.Precision.HIGHEST

T = 4096
D = 1024
DG = 512
DIN = 3072
PLE = 256
NH = 4
HD = 128
CH = 64
NCHUNK = T // CH
RGB = 64
EPS = 1e-6
RG_C = 8.0
NSHARD = 4
SHW = DIN // NSHARD
NDEV = 8

ADAM_LR = 0.001
ADAM_B1 = 0.9
ADAM_B2 = 0.999
ADAM_EPS = 1e-08
ADAM_WD = 0.01
ADAM_STEP = 10

VMEM_SPEC = pl.BlockSpec(memory_space=pltpu.VMEM)
HBM_SPEC = pl.BlockSpec(memory_space=pltpu.HBM)
SEM_SPEC = pl.BlockSpec(memory_space=pltpu.SEMAPHORE)
ANY_SPEC = pl.BlockSpec(memory_space=pl.ANY)
EFFECT = pltpu.SideEffectType.DATAFLOW_SIDE_EFFECTING
MIB = 1024 * 1024

VROWS = 16
ROW_NORM_MIX, ROW_FINAL_NORM, ROW_B_PG, ROW_PLE_NORM = 0, 1, 2, 3
ROW_CB_BA, ROW_BX_LAM, ROW_CW01, ROW_CW23, ROW_HG_LB, ROW_HG_NW, ROW_LOSS = 4, 5, 6, 7, 8, 9, 10


def _mm(a, b):
    return jnp.dot(a.astype(BF16), b.astype(BF16), preferred_element_type=F32)


def _mm_nt(a, b):
    return lax.dot_general(a.astype(BF16), b.astype(BF16), (((1,), (1,)), ((), ())),
                           preferred_element_type=F32)


def _mm_tn(a, b):
    return lax.dot_general(a.astype(BF16), b.astype(BF16), (((0,), (0,)), ((), ())),
                           preferred_element_type=F32)


def _mm_exact(a, b):
    return jnp.dot(a, b, precision=HIGHEST, preferred_element_type=F32)


def _sig_pair(x):
    e = jnp.exp(-jnp.abs(x))
    big = 1.0 / (1.0 + e)
    small = e * big
    pos = x >= 0
    return jnp.where(pos, big, small), jnp.where(pos, small, big)


def _sigmoid(x):
    return 1.0 / (1.0 + jnp.exp(-x))


def _rowsum(v):
    return jnp.sum(v, axis=0, keepdims=True)


def _shift_down(cur, prev8, d, rows):
    rolled = pltpu.roll(cur, d, 0)
    head = jnp.where(rows[0:8] < d, pltpu.roll(prev8, d, 0), rolled[0:8])
    return jnp.concatenate([head, rolled[8:]], axis=0)


def _shift_up(cur, next8, d, rows):
    n = cur.shape[0]
    rolled = pltpu.roll(cur, n - d, 0)
    tail = jnp.where(rows[0:8] >= 8 - d, pltpu.roll(next8, 8 - d, 0), rolled[n - 8:n])
    return jnp.concatenate([rolled[0:n - 8], tail], axis=0)


def _roll_in_groups(v, d):
    n, w = v.shape
    return pltpu.roll(v.reshape(n // 8, 8, w), d, 1).reshape(n, w)


def _cparams(sem, vmem_mib):
    return pltpu.CompilerParams(dimension_semantics=sem, vmem_limit_bytes=vmem_mib * MIB)


def _mesh_pos():
    x, y, c = lax.axis_index("x"), lax.axis_index("y"), lax.axis_index("c")
    chips = [(1 - x, y), (x, 1 - y), (1 - x, 1 - y)]
    return x, y, c, chips


def _remote(src, dst, ssem, rsem, dev):
    return pltpu.make_async_remote_copy(src_ref=src, dst_ref=dst, send_sem=ssem, recv_sem=rsem,
                                        device_id=dev, device_id_type=MESH)


def _gather_w_in(w_in, w_out, w_pg, w_pp, conv_w):
    shapes = [w_in.shape, w_out.shape, w_pg.shape, w_pp.shape]

    def body(win_h, wout_h, wpg_h, wpp_h, cw_h, o_in, o_out, o_pg, o_pp, o_cw,
             win, wout, wpg, wpp, cw, lsem, ssem, rsem):
        x, y, c, chips = _mesh_pos()
        kme = 2 * x + y
        sibling = (x, y, 1 - c)
        fetch = [pltpu.make_async_copy(src, dst, lsem.at[i])
                 for i, (src, dst) in enumerate([(win_h, win), (cw_h, cw), (wout_h, wout), (wpg_h, wpg), (wpp_h, wpp)])]
        for cp in fetch:
            cp.start()

        def cast(src, dst):
            for r0 in range(0, src.shape[0], 128):
                dst[kme, r0:r0 + 128, :] = src[r0:r0 + 128, :].astype(BF16)

        fetch[0].wait()
        cast(win, o_in)
        fetch[1].wait()
        o_cw[kme] = cw[...]

        hrows = D // 2
        mine = pl.ds(pl.multiple_of(c * hrows, 128), hrows)
        other = pl.ds(pl.multiple_of((1 - c) * hrows, 128), hrows)

        def half(k, rows):
            return o_in.at[k, rows]

        na = (x ^ c, y ^ (1 - c), c)
        nb = (x ^ (1 - c), y ^ c, c)
        ka = 2 * na[0] + na[1]
        kb = 2 * nb[0] + nb[1]
        kd = 2 * (1 - x) + (1 - y)
        sends = [_remote(half(kme, mine), half(kme, mine), ssem.at[0], rsem.at[0], na),
                 _remote(half(kme, mine), half(kme, mine), ssem.at[1], rsem.at[1], nb)]
        for j, (px, py) in enumerate(chips):
            sends.append(_remote(o_cw.at[kme], o_cw.at[kme], ssem.at[3 + j], rsem.at[3 + j], (px, py, c)))
        for cp in sends:
            cp.start()
        for i, (src, dst) in enumerate([(wout, o_out), (wpg, o_pg), (wpp, o_pp)]):
            fetch[2 + i].wait()
            cast(src, dst)

        def passed_on(k, sem, dev):
            cp = _remote(half(k, mine), half(k, mine), ssem.at[sem], rsem.at[sem], dev)
            cp.start()
            sends.append(cp)

        _remote(half(ka, mine), half(ka, mine), ssem.at[0], rsem.at[0], na).wait_recv()
        passed_on(ka, 2, nb)
        passed_on(ka, 6, sibling)
        _remote(half(kb, mine), half(kb, mine), ssem.at[1], rsem.at[1], nb).wait_recv()
        passed_on(kb, 7, sibling)
        _remote(half(kd, mine), half(kd, mine), ssem.at[2], rsem.at[2], nb).wait_recv()
        passed_on(kd, 8, sibling)
        _remote(half(kb, other), half(kb, other), ssem.at[6], rsem.at[6], sibling).wait_recv()
        _remote(half(ka, other), half(ka, other), ssem.at[7], rsem.at[7], sibling).wait_recv()
        _remote(half(kd, other), half(kd, other), ssem.at[8], rsem.at[8], sibling).wait_recv()
        for j, (px, py) in enumerate(chips):
            kj = 2 * px + py
            _remote(o_cw.at[kj], o_cw.at[kj], ssem.at[3 + j], rsem.at[3 + j], (px, py, c)).wait_recv()
        for cp in sends:
            cp.wait_send()

    out_shape = [jax.ShapeDtypeStruct((NSHARD,) + s, BF16) for s in shapes]
    out_shape.append(jax.ShapeDtypeStruct((NSHARD,) + conv_w.shape, F32))
    return pl.pallas_call(
        body, name="gather_w_in", out_shape=out_shape,
        in_specs=[HBM_SPEC] * 5, out_specs=[VMEM_SPEC] * 5,
        scratch_shapes=[pltpu.VMEM(a.shape, F32) for a in (w_in, w_out, w_pg, w_pp, conv_w)]
        + [pltpu.SemaphoreType.DMA((5,)), pltpu.SemaphoreType.DMA((9,)), pltpu.SemaphoreType.DMA((9,))],
        compiler_params=pltpu.CompilerParams(vmem_limit_bytes=40 * MIB),
    )(*[pltpu.with_memory_space_constraint(a, pltpu.HBM) for a in (w_in, w_out, w_pg, w_pp, conv_w)])


def _gather_inproj(x, nw, w_in, w_out, w_pg, w_pp, conv_w):
    shapes = [w_in.shape, w_out.shape, w_pg.shape, w_pp.shape]
    tm = 512
    nt = T // tm

    def body(x_h, nw_h, win_h, wout_h, wpg_h, wpp_h, cw_h,
             o_in_h, o_out_h, o_pg_h, o_pp_h, o_cw_h, pf_h, pb_h, u_h,
             win, wout, wpg, wpp, cw, nw_s, xbuf, ubuf, rfbuf, rbbuf, o_in, o_out, o_pg, o_pp, o_cw,
             lsem, xsem, usem, fsem, bsem, osem, ssem, rsem):
        x, y, c, chips = _mesh_pos()
        kme = 2 * x + y
        sibling = (x, y, 1 - c)
        fetch = [pltpu.make_async_copy(src, dst, lsem.at[i]) for i, (src, dst) in enumerate(
            [(win_h, win), (cw_h, cw), (nw_h, nw_s), (wout_h, wout), (wpg_h, wpg), (wpp_h, wpp)])]
        for cp in fetch:
            cp.start()

        def cast(src, dst):
            for r0 in range(0, src.shape[0], 128):
                dst[r0:r0 + 128, :] = src[r0:r0 + 128, :].astype(BF16)

        outs = []

        def write_out(src, dst, i):
            cp = pltpu.make_async_copy(src, dst, osem.at[i])
            cp.start()
            outs.append(cp)

        fetch[0].wait()
        cast(win, o_in.at[kme])
        write_out(o_in.at[kme], o_in_h.at[kme], 0)
        fetch[1].wait()
        o_cw[kme] = cw[...]

        hrows = D // 2
        mine = pl.ds(pl.multiple_of(c * hrows, 128), hrows)
        other = pl.ds(pl.multiple_of((1 - c) * hrows, 128), hrows)

        def half(k, rows):
            return o_in.at[k, rows]

        na = (x ^ c, y ^ (1 - c), c)
        nb = (x ^ (1 - c), y ^ c, c)
        ka = 2 * na[0] + na[1]
        kb = 2 * nb[0] + nb[1]
        kd = 2 * (1 - x) + (1 - y)
        sends = [_remote(half(kme, mine), half(kme, mine), ssem.at[0], rsem.at[0], na),
                 _remote(half(kme, mine), half(kme, mine), ssem.at[1], rsem.at[1], nb)]
        for j, (px, py) in enumerate(chips):
            sends.append(_remote(o_cw.at[kme], o_cw.at[kme], ssem.at[3 + j], rsem.at[3 + j], (px, py, c)))
        for cp in sends:
            cp.start()
        fetch[2].wait()

        def rows_of(t):
            return pl.ds(pl.multiple_of(t * tm, tm), tm)

        def x_copy(t, slot):
            return pltpu.make_async_copy(x_h.at[rows_of(t)], xbuf.at[slot], xsem.at[slot])

        u_out = pltpu.make_async_copy(ubuf, u_h, usem.at[0])

        def f_store(t, slot, kh):
            return pltpu.make_async_copy(rfbuf.at[slot], pf_h.at[rows_of(t), pl.ds(pl.multiple_of(kh * DG, DG), DG)],
                                         fsem.at[slot])

        def b_store_even(t, slot, kh):
            return pltpu.make_async_copy(rbbuf.at[slot, :, 0:256],
                                         pb_h.at[rows_of(t), pl.ds(pl.multiple_of(kh * 2 * DG, 256), 256)],
                                         bsem.at[slot])

        def b_store_odd(t, slot, kh):
            return pltpu.make_async_copy(rbbuf.at[slot],
                                         pb_h.at[rows_of(t), pl.ds(pl.multiple_of(256 + kh * 2 * DG, 256), SHW)],
                                         bsem.at[slot])

        def run_pass(k, first, even):
            kh = k >> 1
            if first:
                x_copy(0, 0).start()

            def tile(t, carry):
                slot = t & 1
                if first:
                    @pl.when(t + 1 < nt)
                    def _():
                        x_copy(t + 1, 1 - slot).start()

                    x_copy(t, slot).wait()

                @pl.when(t >= 2)
                def _():
                    if even:
                        f_store(t - 2, slot, kh).wait()
                        b_store_even(t - 2, slot, kh).wait()
                    else:
                        b_store_odd(t - 2, slot, kh).wait()

                if first:
                    xv = xbuf[slot]
                    s = lax.rsqrt(jnp.mean(xv * xv, axis=-1, keepdims=True) + EPS)
                    u = (xv * s * nw_s[...]).astype(BF16)
                    ubuf[rows_of(t), :] = u
                else:
                    u = ubuf[rows_of(t), :]
                r = jnp.dot(u, o_in[k], preferred_element_type=F32)
                if even:
                    rfbuf[slot] = r[:, 0:DG]
                    rbbuf[slot, :, 0:256] = r[:, DG:SHW].astype(BF16)
                    f_store(t, slot, kh).start()
                    b_store_even(t, slot, kh).start()
                else:
                    rbbuf[slot] = r.astype(BF16)
                    b_store_odd(t, slot, kh).start()
                return carry

            lax.fori_loop(0, nt, tile, 0)
            for t in (nt - 2, nt - 1):
                slot = t & 1
                if even:
                    f_store(t, slot, kh).wait()
                    b_store_even(t, slot, kh).wait()
                else:
                    b_store_odd(t, slot, kh).wait()

        def project(k, first=False):
            @pl.when((k & 1) == 0)
            def _():
                run_pass(k, first, True)

            @pl.when((k & 1) == 1)
            def _():
                run_pass(k, first, False)

        def passed_on(k, sem, dev):
            cp = _remote(half(k, mine), half(k, mine), ssem.at[sem], rsem.at[sem], dev)
            cp.start()
            sends.append(cp)

        project(kme, first=True)
        u_out.start()
        for i, (src, dst, dst_h) in enumerate([(wout, o_out, o_out_h), (wpg, o_pg, o_pg_h), (wpp, o_pp, o_pp_h)]):
            fetch[3 + i].wait()
            cast(src, dst)
            write_out(dst, dst_h.at[kme], 4 + i)
        _remote(half(ka, mine), half(ka, mine), ssem.at[0], rsem.at[0], na).wait_recv()
        passed_on(ka, 2, nb)
        passed_on(ka, 6, sibling)
        _remote(half(kb, mine), half(kb, mine), ssem.at[1], rsem.at[1], nb).wait_recv()
        passed_on(kb, 7, sibling)
        _remote(half(ka, other), half(ka, other), ssem.at[7], rsem.at[7], sibling).wait_recv()
        write_out(o_in.at[ka], o_in_h.at[ka], 1)
        project(ka)
        _remote(half(kd, mine), half(kd, mine), ssem.at[2], rsem.at[2], nb).wait_recv()
        passed_on(kd, 8, sibling)
        _remote(half(kb, other), half(kb, other), ssem.at[6], rsem.at[6], sibling).wait_recv()
        write_out(o_in.at[kb], o_in_h.at[kb], 2)
        project(kb)
        _remote(half(kd, other), half(kd, other), ssem.at[8], rsem.at[8], sibling).wait_recv()
        write_out(o_in.at[kd], o_in_h.at[kd], 3)
        project(kd)
        u_out.wait()
        for j, (px, py) in enumerate(chips):
            kj = 2 * px + py
            _remote(o_cw.at[kj], o_cw.at[kj], ssem.at[3 + j], rsem.at[3 + j], (px, py, c)).wait_recv()
        write_out(o_cw, o_cw_h, 7)
        for cp in outs:
            cp.wait()
        for cp in sends:
            cp.wait_send()

    out_shape = [pltpu.HBM((NSHARD,) + s, BF16) for s in shapes]
    out_shape.append(pltpu.HBM((NSHARD,) + conv_w.shape, F32))
    out_shape += [pltpu.HBM((T, 2 * DG), F32), pltpu.HBM((T, 4 * DG), BF16), pltpu.HBM((T, D), BF16)]
    hbm_args = [pltpu.with_memory_space_constraint(a, pltpu.HBM) for a in (x, nw, w_in, w_out, w_pg, w_pp, conv_w)]
    return pl.pallas_call(
        body, name="gather_inproj", out_shape=out_shape,
        in_specs=[HBM_SPEC] * 7, out_specs=[HBM_SPEC] * 8,
        scratch_shapes=[pltpu.VMEM(a.shape, F32) for a in (w_in, w_out, w_pg, w_pp, conv_w, nw)]
        + [pltpu.VMEM((2, tm, D), F32), pltpu.VMEM((T, D), BF16),
           pltpu.VMEM((2, tm, DG), F32), pltpu.VMEM((2, tm, SHW), BF16),
           pltpu.VMEM((NSHARD,) + w_in.shape, BF16), pltpu.VMEM(w_out.shape, BF16), pltpu.VMEM(w_pg.shape, BF16),
           pltpu.VMEM(w_pp.shape, BF16), pltpu.VMEM((NSHARD,) + conv_w.shape, F32),
           pltpu.SemaphoreType.DMA((6,)), pltpu.SemaphoreType.DMA((2,)), pltpu.SemaphoreType.DMA((2,)),
           pltpu.SemaphoreType.DMA((2,)), pltpu.SemaphoreType.DMA((2,)), pltpu.SemaphoreType.DMA((8,)),
           pltpu.SemaphoreType.DMA((9,)), pltpu.SemaphoreType.DMA((9,))],
        compiler_params=pltpu.CompilerParams(vmem_limit_bytes=48 * MIB),
    )(*hbm_args)


def _gather_rest_start(lands):
    n = len(lands)

    def body(*refs):
        land_in = refs[0:n]
        ssem, rsem = refs[n], refs[n + 1]
        token = refs[2 * n + 2]
        x, y, c, chips = _mesh_pos()
        kme = 2 * x + y
        for p, land in enumerate(land_in):
            hrows = land.shape[1] // 2
            mine = pl.ds(pl.multiple_of(c * hrows, 128), hrows)
            for px, py in chips:
                for pc in range(2):
                    _remote(land.at[kme, mine], land.at[kme, mine], ssem.at[p], rsem.at[p], (px, py, pc)).start()
        token[...] = jnp.zeros_like(token)

    out_shape = ([pltpu.SemaphoreType.DMA((n,)), pltpu.SemaphoreType.DMA((n,))]
                 + [pltpu.HBM(a.shape, a.dtype) for a in lands] + [jax.ShapeDtypeStruct((8, 128), F32)])
    outs = pl.pallas_call(
        body, name="gather_rest_start", out_shape=out_shape,
        in_specs=[HBM_SPEC] * n, out_specs=[SEM_SPEC, SEM_SPEC] + [HBM_SPEC] * n + [VMEM_SPEC],
        input_output_aliases={i: 2 + i for i in range(n)},
        compiler_params=pltpu.CompilerParams(has_side_effects=EFFECT),
    )(*[pltpu.with_memory_space_constraint(a, pltpu.HBM) for a in lands])
    return outs[0], outs[1], list(outs[2:2 + n]), outs[2 + n]


def _gather_rest_wait(ssem, rsem, lands, after):
    n = len(lands)

    def body(*refs):
        land_in = refs[0:n]
        ssem_ref, rsem_ref = refs[n], refs[n + 1]
        x, y, c = lax.axis_index("x"), lax.axis_index("y"), lax.axis_index("c")
        for p, land in enumerate(land_in):
            three = land.at[pl.ds(0, 3)]
            cp = _remote(three, three, ssem_ref.at[p], rsem_ref.at[p], (x, y, c))
            cp.wait_send()
            cp.wait_recv()

    outs = pl.pallas_call(
        body, name="gather_rest_wait", out_shape=[pltpu.HBM(a.shape, a.dtype) for a in lands],
        in_specs=[HBM_SPEC] * n + [SEM_SPEC, SEM_SPEC, ANY_SPEC], out_specs=[HBM_SPEC] * n,
        input_output_aliases={i: i for i in range(n)},
        compiler_params=pltpu.CompilerParams(has_side_effects=EFFECT),
    )(*lands, ssem, rsem, after)
    return list(outs)


def _rs_start(name, parts, ks, lands, chip_sums=False):
    n = len(parts)

    def body(*refs):
        part_in, land_in = refs[0:n], refs[n:2 * n]
        ssem, rsem = refs[2 * n], refs[2 * n + 1]
        token = refs[4 * n + 2]
        x, y, c = lax.axis_index("x"), lax.axis_index("y"), lax.axis_index("c")
        kme = 2 * x + y
        me = 4 * x + 2 * y + c
        for p in range(n):
            hrows = land_in[p].shape[1]
            for i, k in enumerate(ks):
                if chip_sums:
                    @pl.when(kme != k)
                    def _():
                        _remote(part_in[p].at[i], land_in[p].at[kme], ssem.at[p], rsem.at[p], (k // 2, k % 2, c)).start()
                    continue
                for pc in range(2):
                    @pl.when(jnp.logical_or(kme != k, c != pc))
                    def _():
                        _remote(part_in[p].at[i, pl.ds(pc * hrows, hrows)], land_in[p].at[me],
                                ssem.at[p], rsem.at[p], (k // 2, k % 2, pc)).start()
        token[...] = jnp.zeros_like(token)

    arrays = list(parts) + list(lands)
    out_shape = ([pltpu.SemaphoreType.DMA((n,)), pltpu.SemaphoreType.DMA((n,))]
                 + [pltpu.HBM(a.shape, a.dtype) for a in arrays] + [jax.ShapeDtypeStruct((8, 128), F32)])
    outs = pl.pallas_call(
        body, name=name, out_shape=out_shape,
        in_specs=[HBM_SPEC] * (2 * n), out_specs=[SEM_SPEC, SEM_SPEC] + [HBM_SPEC] * (2 * n) + [VMEM_SPEC],
        input_output_aliases={i: 2 + i for i in range(2 * n)},
        compiler_params=pltpu.CompilerParams(has_side_effects=EFFECT),
    )(*[pltpu.with_memory_space_constraint(a, pltpu.HBM) for a in arrays])
    return outs[0], outs[1], list(outs[2:2 + n]), list(outs[2 + n:2 + 2 * n]), outs[2 + 2 * n]


def _rs_wait(name, ssem, rsem, parts, ks, lands, after, chip_sums=False):
    n = len(parts)

    def body(*refs):
        part_in, land_in = refs[0:n], refs[n:2 * n]
        ssem_ref, rsem_ref = refs[2 * n], refs[2 * n + 1]
        x, y, c = lax.axis_index("x"), lax.axis_index("y"), lax.axis_index("c")
        kme = 2 * x + y
        for p in range(n):
            piece = land_in[p].at[0]
            for k in ks:
                for pc in range(1 if chip_sums else 2):
                    mine = (kme == k) if chip_sums else jnp.logical_and(kme == k, c == pc)

                    @pl.when(jnp.logical_not(mine))
                    def _():
                        _remote(piece, piece, ssem_ref.at[p], rsem_ref.at[p], (x, y, c)).wait_send()
            owner = kme == ks[0]
            for k in ks[1:]:
                owner = jnp.logical_or(owner, kme == k)

            @pl.when(owner)
            def _():
                others = land_in[p].at[pl.ds(0, land_in[p].shape[0] - 1)]
                _remote(others, others, ssem_ref.at[p], rsem_ref.at[p], (x, y, c)).wait_recv()

    arrays = list(parts) + list(lands)
    outs = pl.pallas_call(
        body, name=name, out_shape=[pltpu.HBM(a.shape, a.dtype) for a in arrays],
        in_specs=[HBM_SPEC] * (2 * n) + [SEM_SPEC, SEM_SPEC, ANY_SPEC], out_specs=[HBM_SPEC] * (2 * n),
        input_output_aliases={i: i for i in range(2 * n)},
        compiler_params=pltpu.CompilerParams(has_side_effects=EFFECT),
    )(*arrays, ssem, rsem, after)
    return list(outs[0:n]), list(outs[n:2 * n])


def _reduce_exchange(parts, lands):
    shapes = [(2 * l.shape[1], l.shape[2]) for l in lands]
    step = 128

    def body(in01, in23, pout, ppg, ppp, l_in, l_out, l_pg, l_pp, g_in, g_out, g_pg, g_pp, ssem, rsem):
        x, y, c = lax.axis_index("x"), lax.axis_index("y"), lax.axis_index("c")
        kme = 2 * x + y
        me = 4 * x + 2 * y + c
        sibling = (x, y, 1 - c)
        sends = []
        for p, (land, gout) in enumerate([(l_in, g_in), (l_out, g_out), (l_pg, g_pg), (l_pp, g_pp)]):
            hrows = land.shape[1]
            mine0 = pl.multiple_of(c * hrows, step)
            for r0 in range(0, hrows, step):
                rs = pl.ds(mine0 + r0, step)
                if p == 0:
                    own = jnp.where(kme >= 2, in23[kme & 1, r0:r0 + step, :], in01[kme & 1, r0:r0 + step, :])
                    slot = kme
                else:
                    own = (pout, ppg, ppp)[p - 1][kme, rs, :]
                    slot = me
                s = jnp.zeros((step, land.shape[2]), F32)
                for j in range(land.shape[0]):
                    s = s + jnp.where(slot == j, own, land[j, r0:r0 + step, :]).astype(F32)
                gout[rs, :] = s
            mine = pl.ds(mine0, hrows)
            cp = _remote(gout.at[mine], gout.at[mine], ssem.at[p], rsem.at[p], sibling)
            cp.start()
            sends.append(cp)
        for p, (land, gout) in enumerate([(l_in, g_in), (l_out, g_out), (l_pg, g_pg), (l_pp, g_pp)]):
            hrows = land.shape[1]
            other = pl.ds(pl.multiple_of((1 - c) * hrows, step), hrows)
            _remote(gout.at[other], gout.at[other], ssem.at[p], rsem.at[p], sibling).wait_recv()
        for cp in sends:
            cp.wait_send()

    return pl.pallas_call(
        body, name="reduce_exchange", out_shape=[jax.ShapeDtypeStruct(s, F32) for s in shapes],
        in_specs=[VMEM_SPEC] * 9, out_specs=[VMEM_SPEC] * 4,
        scratch_shapes=[pltpu.SemaphoreType.DMA((4,)), pltpu.SemaphoreType.DMA((4,))],
        compiler_params=pltpu.CompilerParams(vmem_limit_bytes=48 * MIB),
    )(*parts, *lands)


def _small_allreduce(sm_in, sm_tail, sm_a, sm_b, rg_c):
    PR = DG // NDEV

    def body(in_ref, tail_ref, a_ref, b_ref, rg_ref, v_out, rg_out, vbuf, vrecv, rgrecv, ssem, rsem):
        x, y, c = lax.axis_index("x"), lax.axis_index("y"), lax.axis_index("c")
        me = 4 * x + 2 * y + c

        def pair(ref, r0, r1):
            return jnp.concatenate([ref[r0:r0 + 1, :], ref[r1:r1 + 1, :]], axis=1)

        rows = {
            ROW_NORM_MIX: in_ref[0:1, :], ROW_FINAL_NORM: tail_ref[1:2, :], ROW_B_PG: tail_ref[2:3, :],
            ROW_PLE_NORM: tail_ref[3:4, :], ROW_CB_BA: pair(a_ref, 0, 1), ROW_BX_LAM: pair(a_ref, 2, 3),
            ROW_CW01: pair(a_ref, 4, 5), ROW_CW23: pair(a_ref, 6, 7), ROW_HG_LB: pair(b_ref, 2, 3),
            ROW_HG_NW: jnp.concatenate([b_ref[1:2, :], jnp.zeros((1, DG), F32)], axis=1),
            ROW_LOSS: tail_ref[4:5, :],
        }
        vbuf[...] = jnp.zeros_like(vbuf)
        for r, row in rows.items():
            for j in range(NDEV):
                vbuf[j, r:r + 1, :] = row[:, j * 128:(j + 1) * 128]

        def peer(mask):
            px = x ^ ((mask >> 2) & 1)
            py = y ^ ((mask >> 1) & 1)
            pc = c ^ (mask & 1)
            return (px, py, pc), 4 * px + 2 * py + pc

        def rg_rows(r):
            return pl.ds(pl.multiple_of(r * PR, PR), PR)

        first = []
        for mask in range(1, NDEV):
            dev, r = peer(mask)
            i = mask - 1
            cp = _remote(vbuf.at[r], vrecv.at[i], ssem.at[i], rsem.at[i], dev)
            cp.start()
            first.append(cp)
            cp = _remote(rg_ref.at[rg_rows(r)], rgrecv.at[i], ssem.at[7 + i], rsem.at[7 + i], dev)
            cp.start()
            first.append(cp)
        sv = vbuf[me]
        sr = rg_ref[rg_rows(me), :]
        for i in range(NDEV - 1):
            first[2 * i].wait_recv()
            first[2 * i + 1].wait_recv()
            sv = sv + vrecv[i]
            sr = sr + rgrecv[i]
        v_out[me] = sv
        rg_out[rg_rows(me), :] = sr
        second = []
        for mask in range(1, NDEV):
            dev, r = peer(mask)
            i = mask - 1
            cp = _remote(v_out.at[me], v_out.at[me], ssem.at[14 + i], rsem.at[14 + i], dev)
            cp.start()
            second.append(cp)
            cp = _remote(rg_out.at[rg_rows(me)], rg_out.at[rg_rows(me)], ssem.at[21 + i], rsem.at[21 + i], dev)
            cp.start()
            second.append(cp)
        for mask in range(1, NDEV):
            dev, r = peer(mask)
            i = mask - 1
            _remote(v_out.at[r], v_out.at[r], ssem.at[14 + i], rsem.at[14 + i], dev).wait_recv()
            _remote(rg_out.at[rg_rows(r)], rg_out.at[rg_rows(r)], ssem.at[21 + i], rsem.at[21 + i], dev).wait_recv()
        for cp in first + second:
            cp.wait_send()

    return pl.pallas_call(
        body, name="small_allreduce",
        out_shape=[jax.ShapeDtypeStruct((NDEV, VROWS, 128), F32), jax.ShapeDtypeStruct((DG, 128), F32)],
        in_specs=[VMEM_SPEC] * 5, out_specs=[VMEM_SPEC] * 2,
        scratch_shapes=[pltpu.VMEM((NDEV, VROWS, 128), F32), pltpu.VMEM((NDEV - 1, VROWS, 128), F32),
                        pltpu.VMEM((NDEV - 1, PR, 128), F32),
                        pltpu.SemaphoreType.DMA((28,)), pltpu.SemaphoreType.DMA((28,))],
    )(sm_in, sm_tail, sm_a, sm_b, rg_c)


def _final_reduce(parts, lands, sm_in, sm_tail, sm_a, sm_b, rg_c):
    shapes = [(2 * l.shape[1], l.shape[2]) for l in lands]
    step = 128
    PR = DG // NDEV

    def body(in01, in23, pout, ppg, ppp, l_in, l_out, l_pg, l_pp, in_h, tail_h, a_h, b_h, rg_h,
             g_in, g_out, g_pg, g_pp, v_out, rg_out, vbuf, vrecv, rgrecv,
             lb_in, lb_out, lb_pg, lb_pp, ob_in, ob_out, ob_pg, ob_pp,
             in_ref, tail_ref, a_ref, b_ref, rg_ref, lsem, ssem_l, bsem_s, bsem_r, ssem, rsem):
        x, y, c = lax.axis_index("x"), lax.axis_index("y"), lax.axis_index("c")
        kme = 2 * x + y
        me = 4 * x + 2 * y + c
        sibling = (x, y, 1 - c)
        small_fetch = [pltpu.make_async_copy(src, dst, ssem_l.at[i]) for i, (src, dst) in enumerate(
            [(in_h, in_ref), (tail_h, tail_ref), (a_h, a_ref), (b_h, b_ref), (rg_h, rg_ref)])]
        for cp in small_fetch:
            cp.start()

        lands_hbm = [l_in, l_out, l_pg, l_pp]
        land_bufs = [lb_in, lb_out, lb_pg, lb_pp]
        own_bufs = [ob_in, ob_out, ob_pg, ob_pp]
        fetches = []
        for p in range(4):
            cp = pltpu.make_async_copy(lands_hbm[p], land_bufs[p], lsem.at[p])
            cp.start()
            fetches.append(cp)

        @pl.when(kme >= 2)
        def _():
            pltpu.make_async_copy(in23.at[kme & 1], ob_in, lsem.at[4]).start()

        @pl.when(kme < 2)
        def _():
            pltpu.make_async_copy(in01.at[kme & 1], ob_in, lsem.at[4]).start()

        own_fetches = [pltpu.make_async_copy(in01.at[0], ob_in, lsem.at[4])]
        for p, part in enumerate([pout, ppg, ppp]):
            hrows = own_bufs[p + 1].shape[0]
            cp = pltpu.make_async_copy(part.at[kme, pl.ds(pl.multiple_of(c * hrows, step), hrows)],
                                       own_bufs[p + 1], lsem.at[5 + p])
            cp.start()
            own_fetches.append(cp)

        def pair(ref, r0, r1):
            return jnp.concatenate([ref[r0:r0 + 1, :], ref[r1:r1 + 1, :]], axis=1)

        for cp in small_fetch:
            cp.wait()
        rows = {
            ROW_NORM_MIX: in_ref[0:1, :], ROW_FINAL_NORM: tail_ref[1:2, :], ROW_B_PG: tail_ref[2:3, :],
            ROW_PLE_NORM: tail_ref[3:4, :], ROW_CB_BA: pair(a_ref, 0, 1), ROW_BX_LAM: pair(a_ref, 2, 3),
            ROW_CW01: pair(a_ref, 4, 5), ROW_CW23: pair(a_ref, 6, 7), ROW_HG_LB: pair(b_ref, 2, 3),
            ROW_HG_NW: jnp.concatenate([b_ref[1:2, :], jnp.zeros((1, DG), F32)], axis=1),
            ROW_LOSS: tail_ref[4:5, :],
        }
        vbuf[...] = jnp.zeros_like(vbuf)
        for r, row in rows.items():
            for j in range(NDEV):
                vbuf[j, r:r + 1, :] = row[:, j * 128:(j + 1) * 128]

        def peer(mask):
            px = x ^ ((mask >> 2) & 1)
            py = y ^ ((mask >> 1) & 1)
            pc = c ^ (mask & 1)
            return (px, py, pc), 4 * px + 2 * py + pc

        def rg_rows(r):
            return pl.ds(pl.multiple_of(r * PR, PR), PR)

        first = []
        for mask in range(1, NDEV):
            dev, r = peer(mask)
            i = mask - 1
            cp = _remote(vbuf.at[r], vrecv.at[i], ssem.at[i], rsem.at[i], dev)
            cp.start()
            first.append(cp)
            cp = _remote(rg_ref.at[rg_rows(r)], rgrecv.at[i], ssem.at[7 + i], rsem.at[7 + i], dev)
            cp.start()
            first.append(cp)

        big = [(lb_in, g_in), (lb_out, g_out), (lb_pg, g_pg), (lb_pp, g_pp)]
        swaps = []
        for p, (land, gout) in enumerate(big):
            fetches[p].wait()
            own_fetches[p].wait()
            hrows = land.shape[1]
            mine0 = pl.multiple_of(c * hrows, step)
            slot = kme if p == 0 else me
            for r0 in range(0, hrows, step):
                rs = pl.ds(mine0 + r0, step)
                own = own_bufs[p][r0:r0 + step, :]
                s = jnp.zeros((step, land.shape[2]), F32)
                for j in range(land.shape[0]):
                    s = s + jnp.where(slot == j, own, land[j, r0:r0 + step, :]).astype(F32)
                gout[rs, :] = s
            mine = pl.ds(mine0, hrows)
            cp = _remote(gout.at[mine], gout.at[mine], bsem_s.at[p], bsem_r.at[p], sibling)
            cp.start()
            swaps.append(cp)

        sv = vbuf[me]
        sr = rg_ref[rg_rows(me), :]
        for i in range(NDEV - 1):
            first[2 * i].wait_recv()
            first[2 * i + 1].wait_recv()
            sv = sv + vrecv[i]
            sr = sr + rgrecv[i]
        v_out[me] = sv
        rg_out[rg_rows(me), :] = sr
        second = []
        for mask in range(1, NDEV):
            dev, r = peer(mask)
            i = mask - 1
            cp = _remote(v_out.at[me], v_out.at[me], ssem.at[14 + i], rsem.at[14 + i], dev)
            cp.start()
            second.append(cp)
            cp = _remote(rg_out.at[rg_rows(me)], rg_out.at[rg_rows(me)], ssem.at[21 + i], rsem.at[21 + i], dev)
            cp.start()
            second.append(cp)
        for p, (land, gout) in enumerate(big):
            hrows = land.shape[1]
            other = pl.ds(pl.multiple_of((1 - c) * hrows, step), hrows)
            _remote(gout.at[other], gout.at[other], bsem_s.at[p], bsem_r.at[p], sibling).wait_recv()
        for mask in range(1, NDEV):
            dev, r = peer(mask)
            i = mask - 1
            _remote(v_out.at[r], v_out.at[r], ssem.at[14 + i], rsem.at[14 + i], dev).wait_recv()
            _remote(rg_out.at[rg_rows(r)], rg_out.at[rg_rows(r)], ssem.at[21 + i], rsem.at[21 + i], dev).wait_recv()
        for cp in first + swaps + second:
            cp.wait_send()

    out_shape = [jax.ShapeDtypeStruct(s, F32) for s in shapes]
    out_shape += [jax.ShapeDtypeStruct((NDEV, VROWS, 128), F32), jax.ShapeDtypeStruct((DG, 128), F32)]
    outs = pl.pallas_call(
        body, name="final_reduce", out_shape=out_shape,
        in_specs=[HBM_SPEC] * 14, out_specs=[VMEM_SPEC] * 6,
        scratch_shapes=[pltpu.VMEM((NDEV, VROWS, 128), F32), pltpu.VMEM((NDEV - 1, VROWS, 128), F32),
                        pltpu.VMEM((NDEV - 1, PR, 128), F32)]
        + [pltpu.VMEM(l.shape, BF16) for l in lands]
        + [pltpu.VMEM(l.shape[1:], BF16) for l in lands]
        + [pltpu.VMEM(a.shape, F32) for a in (sm_in, sm_tail, sm_a, sm_b, rg_c)]
        + [pltpu.SemaphoreType.DMA((8,)), pltpu.SemaphoreType.DMA((5,)),
                        pltpu.SemaphoreType.DMA((4,)), pltpu.SemaphoreType.DMA((4,)),
                        pltpu.SemaphoreType.DMA((28,)), pltpu.SemaphoreType.DMA((28,))],
        compiler_params=pltpu.CompilerParams(vmem_limit_bytes=48 * MIB),
    )(*[pltpu.with_memory_space_constraint(a, pltpu.HBM)
        for a in (*parts, *lands, sm_in, sm_tail, sm_a, sm_b, rg_c)])
    return list(outs[0:4]), outs[4], outs[5]


def _adam_rows(w, g, m, v):
    m2 = ADAM_B1 * m + (1.0 - ADAM_B1) * g
    v2 = ADAM_B2 * v + (1.0 - ADAM_B2) * (g * g)
    m_hat = m2 / (1.0 - ADAM_B1 ** ADAM_STEP)
    v_hat = v2 / (1.0 - ADAM_B2 ** ADAM_STEP)
    delta = -ADAM_LR * (m_hat / (jnp.sqrt(v_hat) + ADAM_EPS) + ADAM_WD * w)
    return delta, m2, v2


def _adam_big(gs, ws, ms, vs):
    n = len(gs)
    steps = 8

    def body(*refs):
        ins, outs = refs[:4 * n], refs[4 * n:]
        for i in range(n):
            g, w, m, v = (r[...] for r in ins[4 * i:4 * i + 4])
            d, m2, v2 = _adam_rows(w, g, m, v)
            outs[3 * i][...] = d
            outs[3 * i + 1][...] = m2
            outs[3 * i + 2][...] = v2

    in_specs, out_specs, out_shape, args = [], [], [], []
    for g, w, m, v in zip(gs, ws, ms, vs):
        r, c = w.shape
        spec = lambda: pl.BlockSpec((r // steps, c), lambda i: (i, 0))
        in_specs += [spec() for _ in range(4)]
        out_specs += [spec() for _ in range(3)]
        out_shape += [jax.ShapeDtypeStruct((r, c), F32)] * 3
        args += [pltpu.with_memory_space_constraint(a, pltpu.HBM) for a in (g, w, m, v)]
    outs = pl.pallas_call(
        body, name="adam_big", grid=(steps,), in_specs=in_specs, out_specs=out_specs, out_shape=out_shape,
        compiler_params=_cparams(("parallel",), 32),
    )(*args)
    return [tuple(outs[3 * i:3 * i + 3]) for i in range(n)]


_VEC_PARAMS = [
    ("norm_mix_w", ROW_NORM_MIX, 0, D), ("final_norm_w", ROW_FINAL_NORM, 0, D),
    ("b_ple_gate", ROW_B_PG, 0, D), ("ple_norm_w", ROW_PLE_NORM, 0, D),
    ("conv_b", ROW_CB_BA, 0, DG), ("rg_ba", ROW_CB_BA, DG, DG),
    ("rg_bx", ROW_BX_LAM, 0, DG), ("rg_lambda", ROW_BX_LAM, DG, DG),
    ("hg_norm_w", ROW_HG_NW, 0, HD),
]
_SMALL_ORDER = [n for n, _, _, _ in _VEC_PARAMS] + ["hg_lb", "conv_w", "rg_wa", "rg_wx"]


def _adam_small(vred, rgred, ws, ms, vs):
    names = _SMALL_ORDER
    n = len(names)

    def body(vred_ref, rg_ref, *refs):
        w_refs = dict(zip(names, refs[0:n]))
        m_refs = dict(zip(names, refs[n:2 * n]))
        v_refs = dict(zip(names, refs[2 * n:3 * n]))
        outs = refs[3 * n:]
        o_refs = {nm: outs[4 * i:4 * i + 4] for i, nm in enumerate(names)}
        kme = 2 * lax.axis_index("x") + lax.axis_index("y")

        def update(nm, g, idx):
            d, m2, v2 = _adam_rows(w_refs[nm][idx], g, m_refs[nm][idx], v_refs[nm][idx])
            og, od, om, ov = o_refs[nm]
            og[idx] = g
            od[idx] = d
            om[idx] = m2
            ov[idx] = v2

        def packed(row, lane0, width):
            return jnp.concatenate([vred_ref[j, row:row + 1, :] for j in range(lane0 // 128, (lane0 + width) // 128)],
                                   axis=1)

        everything = (slice(None), slice(None))
        for nm, row, lane0, width in _VEC_PARAMS:
            update(nm, packed(row, lane0, width), everything)
        for r in range(2):
            update("hg_lb", packed(ROW_HG_LB, r * DG, DG), (slice(r, r + 1), slice(None)))
        for j in range(4):
            g = vred_ref[(j % 2) * 4 + kme, ROW_CW01 + j // 2:ROW_CW01 + j // 2 + 1, :]
            update("conv_w", g, (slice(j, j + 1), slice(None)))
        for r0 in range(0, DG, 128):
            rs = (slice(r0, r0 + 128), slice(None))
            both = rg_ref[r0:r0 + 128, :]
            update("rg_wa", both[:, 0:RGB], rs)
            update("rg_wx", pltpu.roll(both, RGB, 1)[:, 0:RGB], rs)

    args = [vred, rgred] + [d[nm] for d in (ws, ms, vs) for nm in names]
    out_shape = []
    for nm in names:
        out_shape += [jax.ShapeDtypeStruct(ws[nm].shape, F32)] * 4
    whole = lambda s: pl.BlockSpec(s.shape, lambda i, nd=len(s.shape): (0,) * nd)
    outs = pl.pallas_call(
        body, name="adam_small", out_shape=out_shape, grid=(1,),
        in_specs=[whole(a) for a in args], out_specs=[whole(s) for s in out_shape],
    )(*args)
    return {nm: tuple(outs[4 * i:4 * i + 4]) for i, nm in enumerate(names)}


def _fwd_inproj(x, nw, w_in_b, dep):
    tm = 512

    def body(x_ref, nw_ref, w_ref, dep_ref, pf_ref, pb_ref, u_ref):
        xv = x_ref[...]
        s = lax.rsqrt(jnp.mean(xv * xv, axis=-1, keepdims=True) + EPS)
        u = (xv * s * nw_ref[...]).astype(BF16)
        u_ref[...] = u
        r = [jnp.dot(u, w_ref[k], preferred_element_type=F32) for k in range(NSHARD)]
        h = DG // 2
        pf_ref[:, 0:DG] = r[0][:, 0:DG]
        pb_ref[:, 0:h] = r[0][:, DG:SHW].astype(BF16)
        pb_ref[:, h:DG] = r[1][:, 0:h].astype(BF16)
        pb_ref[:, DG:2 * DG] = r[1][:, h:SHW].astype(BF16)
        pf_ref[:, DG:2 * DG] = r[2][:, 0:DG]
        pb_ref[:, 2 * DG:2 * DG + h] = r[2][:, DG:SHW].astype(BF16)
        pb_ref[:, 2 * DG + h:3 * DG] = r[3][:, 0:h].astype(BF16)
        pb_ref[:, 3 * DG:4 * DG] = r[3][:, h:SHW].astype(BF16)

    return pl.pallas_call(
        body, name="fwd_inproj", grid=(T // tm,),
        in_specs=[pl.BlockSpec((tm, D), lambda i: (i, 0)),
                  pl.BlockSpec((1, D), lambda i: (0, 0)),
                  pl.BlockSpec((NSHARD, D, SHW), lambda i: (0, 0, 0), pipeline_mode=pl.Buffered(1)), ANY_SPEC],
        out_specs=[pl.BlockSpec((tm, 2 * DG), lambda i: (i, 0)),
                   pl.BlockSpec((tm, 4 * DG), lambda i: (i, 0)),
                   pl.BlockSpec((tm, D), lambda i: (i, 0))],
        out_shape=[jax.ShapeDtypeStruct((T, 2 * DG), F32), jax.ShapeDtypeStruct((T, 4 * DG), BF16),
                   jax.ShapeDtypeStruct((T, D), BF16)],
        compiler_params=_cparams(("parallel",), 48),
    )(x, nw, w_in_b, dep)


def _conv_rows(cw_ref):
    return [jnp.concatenate([cw_ref[k, j:j + 1, :] for k in range(NSHARD)], axis=1) for j in range(4)]


def _rg_conv(xa, prev8, cw, cb, rows):
    taps = [_shift_down(xa, prev8, 3, rows), _shift_down(xa, prev8, 2, rows),
            _shift_down(xa, prev8, 1, rows), xa]
    xc = cb
    for j in range(4):
        xc = xc + taps[j] * cw[j]
    return xc, taps


def _block_mask():
    r = lax.broadcasted_iota(I32, (DG, DG), 0)
    c = lax.broadcasted_iota(I32, (DG, DG), 1)
    return (r >> 6) == (c >> 6)


def _dense_from_blocks(wc):
    j = lax.broadcasted_iota(I32, (RGB, DG), 0)
    c = lax.broadcasted_iota(I32, (RGB, DG), 1)
    spread = _mm_exact(wc, ((c & (RGB - 1)) == j).astype(F32))
    return jnp.where(_block_mask(), spread, 0.0)


def _blocks_from_dense(da, dx):
    c = lax.broadcasted_iota(I32, (DG, 128), 0)
    j = lax.broadcasted_iota(I32, (DG, 128), 1)
    hit = (c & (RGB - 1)) == (j & (RGB - 1))
    mask = _block_mask()
    return (_mm_exact(jnp.where(mask, da, 0.0), (hit & (j < RGB)).astype(F32))
            + _mm_exact(jnp.where(mask, dx, 0.0), (hit & (j >= RGB)).astype(F32)))


def _rg_gates(xc, wa, ba, wx, bx, sp, first_row):
    r = _sigmoid(_mm(xc, wa) + ba)
    i = _sigmoid(_mm(xc, wx) + bx)
    log_a = (-RG_C) * r * sp
    a = jnp.exp(log_a)
    a2 = a * a
    one_m_a2 = -jnp.tanh(log_a) * (a2 + 1.0)
    mult = jnp.where(first_row, 1.0, jnp.sqrt(one_m_a2))
    return r, i, a, a2, mult


def _softplus(z):
    return jnp.maximum(z, 0.0) + jnp.log1p(jnp.exp(-jnp.abs(z)))


def _fwd_rglru(pf, pb, cw3, conv_b, wa_c, ba, wx_c, bx, lam, dep):
    tm = 512
    ng = tm // 8

    def body(xa_ref, ga_ref, cw_ref, cb_ref, wa_ref, ba_ref, wx_ref, bx_ref, lam_ref, dep_ref,
             h_ref, ya_ref, a_s, u_s, tail_s, hc_s, wa_s, wx_s):
        i = pl.program_id(0)

        @pl.when(i == 0)
        def _():
            tail_s[...] = jnp.zeros_like(tail_s)
            hc_s[...] = jnp.zeros_like(hc_s)
            wa_s[...] = _dense_from_blocks(wa_ref[...]).astype(BF16)
            wx_s[...] = _dense_from_blocks(wx_ref[...]).astype(BF16)

        rows = lax.broadcasted_iota(I32, (tm, DG), 0)
        xa = xa_ref[...]
        xc, _ = _rg_conv(xa, tail_s[...], _conv_rows(cw_ref), cb_ref[...], rows)
        tail_s[...] = xa[tm - 8:tm, :]
        sp = _softplus(-lam_ref[...])
        rp = _mm(xc, wa_s[...]) + ba_ref[...]
        ip = _mm(xc, wx_s[...]) + bx_ref[...]
        rb = 64
        rows_b = lax.broadcasted_iota(I32, (rb, DG), 0)
        rows8 = rows_b & 7
        carry = hc_s[0:1, :]
        for b0 in range(0, tm, rb):
            sl = slice(b0, b0 + rb)
            r = _sigmoid(rp[sl])
            ig = _sigmoid(ip[sl])
            log_a = (-RG_C) * r * sp
            av = jnp.exp(log_a)
            mult = jnp.sqrt(-jnp.tanh(log_a) * (av * av + 1.0))
            if b0 == 0:
                mult = jnp.where((rows_b + i * tm) == 0, 1.0, mult)
            uv = mult * (ig * xc[sl])
            for d in (1, 2, 4):
                keep = rows8 >= d
                uv = uv + av * jnp.where(keep, _roll_in_groups(uv, d), 0.0)
                av = av * jnp.where(keep, _roll_in_groups(av, d), 1.0)
            ga = ga_ref[sl, :].astype(F32)
            gate = ga * _sigmoid(ga)
            hs = []
            for g in range(rb // 8):
                gs = slice(g * 8, (g + 1) * 8)
                hv = uv[gs] + av[gs] * carry
                carry = hv[7:8, :]
                hs.append(hv)
            hb = jnp.concatenate(hs, axis=0)
            h_ref[sl, :] = hb
            ya_ref[sl, :] = (hb * gate).astype(BF16)
        hc_s[0:1, :] = carry

    vec = lambda: pl.BlockSpec((1, DG), lambda i: (0, 0))
    blocks = lambda: pl.BlockSpec((DG, RGB), lambda i: (0, 0))
    return pl.pallas_call(
        body, name="fwd_rglru", grid=(T // tm,),
        in_specs=[pl.BlockSpec((tm, DG), lambda i: (i, 0)),
                  pl.BlockSpec((tm, DG), lambda i: (i, 0)),
                  pl.BlockSpec((NSHARD, 4, 128), lambda i: (0, 0, 0)), vec(),
                  blocks(), vec(), blocks(), vec(), vec(), ANY_SPEC],
        out_specs=[pl.BlockSpec((tm, DG), lambda i: (i, 0)),
                   pl.BlockSpec((tm, DG), lambda i: (i, 0))],
        out_shape=[jax.ShapeDtypeStruct((T, DG), F32), pltpu.HBM((T, DG), BF16)],
        scratch_shapes=[pltpu.VMEM((tm, DG), F32), pltpu.VMEM((tm, DG), F32),
                        pltpu.VMEM((8, DG), F32), pltpu.VMEM((8, DG), F32),
                        pltpu.VMEM((DG, DG), BF16), pltpu.VMEM((DG, DG), BF16)],
        compiler_params=_cparams(("arbitrary",), 48),
    )(pf, pb, cw3, conv_b, wa_c, ba, wx_c, bx, lam, dep)


def _hg_lower_bound(lb_ref):
    return _sig_pair(lb_ref[0:1, :] - lb_ref[1:2, :])


def _hg_gates(fz, lb, one_m_lb):
    sg, sn = _sig_pair(fz)
    f = lb + one_m_lb * sg
    return sg, sn, f, jnp.log(f), one_m_lb * sn


def _tri(lower):
    r = lax.broadcasted_iota(I32, (CH, CH), 0)
    c = lax.broadcasted_iota(I32, (CH, CH), 1)
    return (r >= c) if lower else (r <= c)


def _chunk_cumsum(v, rows64):
    for d in (1, 2, 4, 8, 16, 32):
        v = v + jnp.where(rows64 >= d, pltpu.roll(v, d, 0), 0.0)
    return v


def _chunk_rev_cumsum(v, rows64):
    n = v.shape[0]
    for d in (1, 2, 4, 8, 16, 32):
        v = v + jnp.where(rows64 < CH - d, pltpu.roll(v, n - d, 0), 0.0)
    return v


def _hg_recompute(q_ref, f_ref, lb, one_m_lb, rows64, eb_s, enb_s, ekd_s, qe_s, ke_s, kd_s, dec_s):
    nc = q_ref.shape[0] // CH
    sg, sn, f, logf, k = _hg_gates(f_ref[...], lb, one_m_lb)
    q = q_ref[...].astype(F32)
    sq = _sigmoid(q)
    qs = q * sq * (HD ** -0.5)
    b = _chunk_cumsum(logf, rows64)
    for c in range(nc):
        rs = slice(c * CH, (c + 1) * CH)
        b_c = b[rs]
        bl = b_c[CH - 1:CH, :]
        eb, enb, ekd = jnp.exp(b_c), jnp.exp(-b_c), jnp.exp(bl - b_c)
        if eb_s is not None:
            eb_s[rs, :] = eb
            enb_s[rs, :] = enb
            ekd_s[rs, :] = ekd
        qe_s[rs, :] = (qs[rs] * eb).astype(BF16)
        ke_s[rs, :] = (k[rs] * enb).astype(BF16)
        kd_s[rs, :] = (k[rs] * ekd).astype(BF16)
        dec_s[c:c + 1, :] = jnp.exp(bl)
    return sg, sn, f, k, q, sq, qs


def _fwd_hgrn2(pf, pb, hg_lb, hg_nw):
    tm = 512
    nc = tm // CH

    def body(q_ref, f_ref, v_ref, g_ref, lb_ref, nw_ref, yb_ref, o_ref, sp_ref,
             st_s, qe_s, ke_s, kd_s, vb_s, dec_s, p_s, ds_s):
        i = pl.program_id(0)

        @pl.when(i == 0)
        def _():
            st_s[...] = jnp.zeros_like(st_s)

        lb, one_m_lb = _hg_lower_bound(lb_ref)
        rows64 = lax.broadcasted_iota(I32, (tm, DG), 0) & (CH - 1)
        _hg_recompute(q_ref, f_ref, lb, one_m_lb, rows64, None, None, None, qe_s, ke_s, kd_s, dec_s)
        vb_s[...] = v_ref[...]
        mask = _tri(True)
        items = [(c, hd, slice(c * CH, (c + 1) * CH), slice(hd * HD, (hd + 1) * HD))
                 for c in range(nc) for hd in range(NH)]
        for c, hd, rs, cols in items:
            p_s[c * NH + hd] = jnp.where(mask, _mm_nt(qe_s[rs, cols], ke_s[rs, cols]), 0.0).astype(BF16)
            ds_s[c * NH + hd] = _mm_tn(vb_s[rs, cols], kd_s[rs, cols])
        for c, hd, rs, cols in items:
            st = st_s[hd]
            sp_ref[hd, c] = st
            st_s[hd] = st * dec_s[c:c + 1, cols] + ds_s[c * NH + hd]
        for c, hd, rs, cols in items:
            o_ref[rs, cols] = _mm(p_s[c * NH + hd], vb_s[rs, cols]) + _mm_nt(qe_s[rs, cols], sp_ref[hd, c])
        nw = nw_ref[...]
        for hd in range(NH):
            cols = slice(hd * HD, (hd + 1) * HD)
            o = o_ref[:, cols]
            so = lax.rsqrt(jnp.mean(o * o, axis=-1, keepdims=True) + EPS)
            g = g_ref[:, cols].astype(F32)
            sg = _sigmoid(g)
            yb_ref[:, cols] = (o * so * nw * (g * sg)).astype(BF16)

    col = lambda j: pl.BlockSpec((tm, DG), lambda i: (i, j))
    return pl.pallas_call(
        body, name="fwd_hgrn2", grid=(T // tm,),
        in_specs=[col(1), col(1), col(2), col(3),
                  pl.BlockSpec((2, DG), lambda i: (0, 0)),
                  pl.BlockSpec((1, HD), lambda i: (0, 0))],
        out_specs=[pl.BlockSpec((tm, DG), lambda i: (i, 0)),
                   pl.BlockSpec((tm, DG), lambda i: (i, 0)),
                   pl.BlockSpec((NH, nc, HD, HD), lambda i: (0, i, 0, 0))],
        out_shape=[pltpu.HBM((T, DG), BF16), jax.ShapeDtypeStruct((T, DG), F32),
                   jax.ShapeDtypeStruct((NH, NCHUNK, HD, HD), F32)],
        scratch_shapes=[pltpu.VMEM((NH, HD, HD), F32),
                        pltpu.VMEM((tm, DG), BF16), pltpu.VMEM((tm, DG), BF16), pltpu.VMEM((tm, DG), BF16),
                        pltpu.VMEM((tm, DG), BF16), pltpu.VMEM((nc, DG), F32),
                        pltpu.VMEM((nc * NH, CH, CH), BF16), pltpu.VMEM((nc * NH, HD, HD), F32)],
        compiler_params=_cparams(("arbitrary",), 48),
    )(pb, pf, pb, pb, hg_lb, hg_nw)


def _tail_fwd_bwd(x, p, tgt, ya, yb, w_out_b, w_pg_b, w_pp_b, ple_nw, b_pg, fnw):
    tm = 512
    nt = T // tm
    QR = D // NSHARD

    def body(x_ref, p_ref, t_ref, ya_ref, yb_ref, wo_ref, wg_ref, wp_ref, pw_ref, b_ref, fw_ref,
             dh1_ref, dyab_ref, dwo_ref, dwg_ref, dwp_ref, sm_ref, dwo_s, dwg_s, dwp_s):
        i = pl.program_id(0)

        @pl.when(i == 0)
        def _():
            dwo_s[...] = jnp.zeros_like(dwo_s)
            dwg_s[...] = jnp.zeros_like(dwg_s)
            dwp_s[...] = jnp.zeros_like(dwp_s)
            sm_ref[...] = jnp.zeros_like(sm_ref)

        ya = ya_ref[...]
        yb = yb_ref[...]
        pv = p_ref[...].astype(BF16)
        pw = pw_ref[...]
        fw = fw_ref[...]
        h1 = x_ref[...] + _mm(ya, wo_ref[0:DG, :]) + _mm(yb, wo_ref[DG:D, :])
        s2 = lax.rsqrt(jnp.mean(h1 * h1, axis=-1, keepdims=True) + EPS)
        n2h = h1 * s2
        n2 = (n2h * pw).astype(BF16)
        z = _mm(n2, wg_ref[...]) + b_ref[...]
        gate = _sigmoid(z)
        pp = jnp.concatenate([_mm(pv, wp_ref[k]) for k in range(NSHARD)], axis=1)
        h2 = h1 + gate * pp
        s3 = lax.rsqrt(jnp.mean(h2 * h2, axis=-1, keepdims=True) + EPS)
        hn = h2 * s3
        err = hn * fw - t_ref[...]
        sm_ref[0:1, :] += _rowsum(err * err)
        dy = err * (1.0 / D)
        sm_ref[1:2, :] += _rowsum(dy * hn)
        g3 = dy * fw
        dh2 = s3 * (g3 - hn * jnp.mean(g3 * hn, axis=-1, keepdims=True))
        dpp = (dh2 * gate).astype(BF16)
        dz = dh2 * pp * gate * (1.0 - gate)
        sm_ref[2:3, :] += _rowsum(dz)
        dzb = dz.astype(BF16)
        dwg_s[...] += _mm_tn(n2, dzb)
        dn2 = _mm_nt(dzb, wg_ref[...])
        for k in range(NSHARD):
            dwp_s[k] += _mm_tn(pv, dpp[:, k * PLE:(k + 1) * PLE])
        sm_ref[3:4, :] += _rowsum(dn2 * n2h)
        g2 = dn2 * pw
        dh1 = dh2 + s2 * (g2 - n2h * jnp.mean(g2 * n2h, axis=-1, keepdims=True))
        dh1_ref[...] = dh1
        dh1b = dh1.astype(BF16)
        dyab_ref[...] = _mm_nt(dh1b, wo_ref[...])
        dwo_s[0:DG, :] += _mm_tn(ya, dh1b)
        dwo_s[DG:D, :] += _mm_tn(yb, dh1b)

        @pl.when(i == nt - 1)
        def _():
            total = jnp.sum(sm_ref[0:1, :], axis=-1, keepdims=True) * (0.5 / D)
            sm_ref[4:5, :] = jnp.broadcast_to(total, (1, D))
            for k in range(NSHARD):
                dwo_ref[k] = dwo_s[k * QR:(k + 1) * QR, :].astype(BF16)
                dwg_ref[k] = dwg_s[k * QR:(k + 1) * QR, :].astype(BF16)
                dwp_ref[k] = dwp_s[k].astype(BF16)

    row = lambda w: pl.BlockSpec((tm, w), lambda i: (i, 0))
    const2 = lambda s: pl.BlockSpec(s, lambda i: (0, 0), pipeline_mode=pl.Buffered(1))
    const3 = lambda s: pl.BlockSpec(s, lambda i: (0, 0, 0), pipeline_mode=pl.Buffered(1))
    return pl.pallas_call(
        body, name="tail_fwd_bwd", grid=(nt,),
        in_specs=[row(D), row(PLE), row(D), row(DG), row(DG),
                  const2((D, D)), const2((D, D)), const3((NSHARD, PLE, PLE)),
                  const2((1, D)), const2((1, D)), const2((1, D))],
        out_specs=[row(D), row(D), const3((NSHARD, QR, D)), const3((NSHARD, QR, D)),
                   const3((NSHARD, PLE, PLE)), const2((8, D))],
        out_shape=[jax.ShapeDtypeStruct((T, D), F32), jax.ShapeDtypeStruct((T, D), F32),
                   jax.ShapeDtypeStruct((NSHARD, QR, D), BF16), jax.ShapeDtypeStruct((NSHARD, QR, D), BF16),
                   jax.ShapeDtypeStruct((NSHARD, PLE, PLE), BF16), jax.ShapeDtypeStruct((8, D), F32)],
        scratch_shapes=[pltpu.VMEM((D, D), F32), pltpu.VMEM((D, D), F32), pltpu.VMEM((NSHARD, PLE, PLE), F32)],
        compiler_params=_cparams(("arbitrary",), 62),
    )(x, p, tgt, pltpu.with_memory_space_constraint(ya, pltpu.HBM), pltpu.with_memory_space_constraint(yb, pltpu.HBM),
      w_out_b, w_pg_b, w_pp_b, ple_nw, b_pg, fnw)


def _bwd_rglru(pf, pb, h, dyab, cw3, conv_b, wa_c, ba, wx_c, bx, lam, dep):
    tm = 512
    nt = T // tm
    ng = tm // 8

    def body(xa_ref, ga_ref, xp_ref, h_ref, hp_ref, dya_ref, cw_ref, cb_ref, wa_ref, ba_ref, wx_ref, bx_ref,
             lam_ref, dep_ref, da_ref, rg_ref, sm_ref, a_s, g_s, cg_s, nxt_s, wa_s, wx_s, dwa_s, dwx_s):
        i = pl.program_id(0)
        tile = nt - 1 - i

        @pl.when(i == 0)
        def _():
            dwa_s[...] = jnp.zeros_like(dwa_s)
            dwx_s[...] = jnp.zeros_like(dwx_s)
            sm_ref[...] = jnp.zeros_like(sm_ref)
            cg_s[...] = jnp.zeros_like(cg_s)
            nxt_s[...] = jnp.zeros_like(nxt_s)
            wa_s[...] = _dense_from_blocks(wa_ref[...]).astype(BF16)
            wx_s[...] = _dense_from_blocks(wx_ref[...]).astype(BF16)

        rows = lax.broadcasted_iota(I32, (tm, DG), 0)
        has_prev = tile > 0
        xa = xa_ref[...]
        xprev = jnp.where(has_prev, xp_ref[...], 0.0)
        cw = _conv_rows(cw_ref)
        xc, taps = _rg_conv(xa, xprev, cw, cb_ref[...], rows)
        lam_v = lam_ref[...]
        sp = _softplus(-lam_v)
        first_row = (rows + tile * tm) == 0
        r, ig, a, a2, mult = _rg_gates(xc, wa_s[...], ba_ref[...], wx_s[...], bx_ref[...], sp, first_row)
        hv = h_ref[...]
        hprev = jnp.where(has_prev, hp_ref[...], 0.0)
        h_m1 = _shift_down(hv, hprev, 1, rows)
        ga = ga_ref[...].astype(F32)
        sg = _sigmoid(ga)
        dya = dya_ref[...]
        dga = dya * hv * (sg * (1.0 + ga * (1.0 - sg)))

        av = jnp.where(rows == tm - 1, 1.0, pltpu.roll(a, tm - 1, 0))
        gv = dya * (ga * sg)
        rows8 = rows & 7
        for d in (1, 2, 4):
            keep = rows8 < 8 - d
            gv = gv + av * jnp.where(keep, _roll_in_groups(gv, 8 - d), 0.0)
            av = av * jnp.where(keep, _roll_in_groups(av, 8 - d), 1.0)
        a_s[...] = av
        g_s[...] = gv
        carry = cg_s[0:1, :]
        for g in range(ng - 1, -1, -1):
            sl = slice(g * 8, (g + 1) * 8)
            ab, gb = a_s[sl, :], g_s[sl, :]
            g_s[sl, :] = gb + ab * carry
            carry = gb[0:1, :] + ab[0:1, :] * carry
        cg_s[0:1, :] = a[0:1, :] * carry

        gt = g_s[...]
        da = gt * h_m1
        ixc = ig * xc
        di = gt * mult * xc
        dxc = gt * mult * ig
        dlog_a = da * a + jnp.where(first_row, 0.0, gt * ixc * (-a2 / mult))
        sm_ref[3:4, :] += _rowsum(dlog_a * ((-RG_C) * r))
        dpr_f = dlog_a * ((-RG_C) * sp) * r * (1.0 - r)
        dpi_f = di * ig * (1.0 - ig)
        sm_ref[1:2, :] += _rowsum(dpr_f)
        sm_ref[2:3, :] += _rowsum(dpi_f)
        dpr = dpr_f.astype(BF16)
        dpi = dpi_f.astype(BF16)
        xcb = xc.astype(BF16)
        dwa_s[...] += _mm_tn(xcb, dpr)
        dwx_s[...] += _mm_tn(xcb, dpi)
        dxc = dxc + _mm_nt(dpr, wa_s[...]) + _mm_nt(dpi, wx_s[...])
        sm_ref[0:1, :] += _rowsum(dxc)
        for j in range(4):
            sm_ref[4 + j:5 + j, :] += _rowsum(dxc * taps[j])
        nxt = nxt_s[...]
        dxa = (dxc * cw[3] + _shift_up(dxc, nxt, 1, rows) * cw[2]
               + _shift_up(dxc, nxt, 2, rows) * cw[1] + _shift_up(dxc, nxt, 3, rows) * cw[0])
        nxt_s[...] = dxc[0:8, :]
        da_ref[:, 0:DG] = dxa.astype(BF16)
        da_ref[:, DG:D] = dga.astype(BF16)

        @pl.when(i == nt - 1)
        def _():
            _, s_neg = _sig_pair(lam_v)
            sm_ref[3:4, :] = sm_ref[3:4, :] * (-s_neg)
            rg_ref[...] = _blocks_from_dense(dwa_s[...], dwx_s[...])

    vec = lambda: pl.BlockSpec((1, DG), lambda i: (0, 0))
    blocks = lambda: pl.BlockSpec((DG, RGB), lambda i: (0, 0))
    prev8 = lambda: pl.BlockSpec((8, DG), lambda i: (jnp.maximum((nt - 1 - i) * (tm // 8) - 1, 0), 0))
    return pl.pallas_call(
        body, name="bwd_rglru", grid=(nt,),
        in_specs=[pl.BlockSpec((tm, DG), lambda i: (nt - 1 - i, 0)),
                  pl.BlockSpec((tm, DG), lambda i: (nt - 1 - i, 0)),
                  prev8(),
                  pl.BlockSpec((tm, DG), lambda i: (nt - 1 - i, 0)),
                  prev8(),
                  pl.BlockSpec((tm, DG), lambda i: (nt - 1 - i, 0)),
                  pl.BlockSpec((NSHARD, 4, 128), lambda i: (0, 0, 0)), vec(),
                  blocks(), vec(), blocks(), vec(), vec(), ANY_SPEC],
        out_specs=[pl.BlockSpec((tm, D), lambda i: (nt - 1 - i, 0)),
                   pl.BlockSpec((DG, 128), lambda i: (0, 0)),
                   pl.BlockSpec((8, DG), lambda i: (0, 0))],
        out_shape=[jax.ShapeDtypeStruct((T, D), BF16), jax.ShapeDtypeStruct((DG, 128), F32),
                   jax.ShapeDtypeStruct((8, DG), F32)],
        scratch_shapes=[pltpu.VMEM((tm, DG), F32), pltpu.VMEM((tm, DG), F32),
                        pltpu.VMEM((8, DG), F32), pltpu.VMEM((8, DG), F32),
                        pltpu.VMEM((DG, DG), BF16), pltpu.VMEM((DG, DG), BF16),
                        pltpu.VMEM((DG, DG), F32), pltpu.VMEM((DG, DG), F32)],
        compiler_params=_cparams(("arbitrary",), 56),
    )(pf, pb, pf, h, h, dyab, cw3, conv_b, wa_c, ba, wx_c, bx, lam, dep)


def _bwd_hgrn2(pf, pb, o, s_prev, dyab, hg_lb, hg_nw, dep):
    tm = 512
    nt = T // tm
    nc = tm // CH

    def body(q_ref, f_ref, v_ref, g_ref, o_ref, sp_ref, dy_ref, lb_ref, nw_ref, dep_ref, db_ref, sm_ref,
             dst_s, eb_s, enb_s, ekd_s, qe_s, ke_s, kd_s, vb_s, do_s, dec_s, ddec_s, p_s, dp_s,
             g_s, dsta_s, dva_s, dqe_s, dke_s, dkd_s, dlf_s):
        i = pl.program_id(0)

        @pl.when(i == 0)
        def _():
            sm_ref[...] = jnp.zeros_like(sm_ref)
            dst_s[...] = jnp.zeros_like(dst_s)

        lb, one_m_lb = _hg_lower_bound(lb_ref)
        rows64 = lax.broadcasted_iota(I32, (tm, DG), 0) & (CH - 1)
        sg, sn, f, k, q, sq, qs = _hg_recompute(
            q_ref, f_ref, lb, one_m_lb, rows64, eb_s, enb_s, ekd_s, qe_s, ke_s, kd_s, dec_s)
        vb_s[...] = v_ref[...]

        nw = nw_ref[...]
        for hd in range(NH):
            cols = slice(hd * HD, (hd + 1) * HD)
            g = g_ref[:, cols].astype(F32)
            sgg = _sigmoid(g)
            o = o_ref[:, cols]
            so = lax.rsqrt(jnp.mean(o * o, axis=-1, keepdims=True) + EPS)
            oh = o * so
            dyb = dy_ref[:, cols]
            db_ref[:, 3 * DG + hd * HD:3 * DG + (hd + 1) * HD] = (
                dyb * (oh * nw) * (sgg * (1.0 + g * (1.0 - sgg)))).astype(BF16)
            don = dyb * (g * sgg)
            sm_ref[1:2, 0:HD] += _rowsum(don * oh)
            gw = don * nw
            do_s[:, cols] = (so * (gw - oh * jnp.mean(gw * oh, axis=-1, keepdims=True))).astype(BF16)

        mask = _tri(True)
        items = [(c, hd, slice(c * CH, (c + 1) * CH), slice(hd * HD, (hd + 1) * HD))
                 for c in range(nc) for hd in range(NH)]
        for c, hd, rs, cols in items:
            p_s[c * NH + hd] = jnp.where(mask, _mm_nt(qe_s[rs, cols], ke_s[rs, cols]), 0.0).astype(BF16)
            dp_s[c * NH + hd] = jnp.where(mask, _mm_nt(do_s[rs, cols], vb_s[rs, cols]), 0.0).astype(BF16)
        for c, hd, rs, cols in items:
            n = c * NH + hd
            dva_s[rs, cols] = _mm_tn(p_s[n], do_s[rs, cols])
            dqe_s[rs, cols] = _mm(dp_s[n], ke_s[rs, cols])
            dke_s[rs, cols] = _mm_tn(dp_s[n], qe_s[rs, cols])
            g_s[n] = _mm_tn(do_s[rs, cols], qe_s[rs, cols])
        for c, hd, rs, cols in reversed(items):
            n = c * NH + hd
            dst = dst_s[hd]
            dsta_s[n] = dst
            dst_s[hd] = dst * dec_s[c:c + 1, cols] + g_s[n]
        for c, hd, rs, cols in items:
            n = c * NH + hd
            dst = dsta_s[n]
            st_prev = sp_ref[hd, c]
            dv = dva_s[rs, cols] + _mm_nt(kd_s[rs, cols], dst)
            db_ref[rs, 2 * DG + hd * HD:2 * DG + (hd + 1) * HD] = dv.astype(BF16)
            dqe_s[rs, cols] += _mm(do_s[rs, cols], st_prev)
            dkd_s[rs, cols] = _mm(vb_s[rs, cols], dst)
            ddec_s[c:c + 1, cols] = _rowsum(dst * st_prev)

        eb, enb, ekd = eb_s[...], enb_s[...], ekd_s[...]
        dqe, dke, dkd = dqe_s[...], dke_s[...], dkd_s[...]
        t_kd = dkd * (k * ekd)
        rc = _chunk_rev_cumsum(dqe * (qs * eb) - dke * (k * enb) - t_kd, rows64)
        for c in range(nc):
            rs = slice(c * CH, (c + 1) * CH)
            dbl = _rowsum(t_kd[rs]) + ddec_s[c:c + 1, :] * dec_s[c:c + 1, :]
            dlf_s[rs, :] = rc[rs] + dbl
        t = dlf_s[...] / f - (dke * enb + dkd * ekd)
        db_ref[:, DG:2 * DG] = (one_m_lb * sg * sn * t).astype(BF16)
        sm_ref[0:1, :] += _rowsum(sn * t)
        db_ref[:, 0:DG] = (dqe * eb * (sq * (1.0 + q * (1.0 - sq))) * (HD ** -0.5)).astype(BF16)

        @pl.when(i == nt - 1)
        def _():
            dsm = sm_ref[0:1, :] * (lb * one_m_lb)
            sm_ref[2:3, :] = dsm
            sm_ref[3:4, :] = -dsm

    col = lambda j: pl.BlockSpec((tm, DG), lambda i: (nt - 1 - i, j))
    big = lambda dt: pltpu.VMEM((tm, DG), dt)
    return pl.pallas_call(
        body, name="bwd_hgrn2", grid=(nt,),
        in_specs=[col(1), col(1), col(2), col(3),
                  pl.BlockSpec((tm, DG), lambda i: (nt - 1 - i, 0)),
                  pl.BlockSpec((NH, nc, HD, HD), lambda i: (0, nt - 1 - i, 0, 0)),
                  pl.BlockSpec((tm, DG), lambda i: (nt - 1 - i, 1)),
                  pl.BlockSpec((2, DG), lambda i: (0, 0)),
                  pl.BlockSpec((1, HD), lambda i: (0, 0)), ANY_SPEC],
        out_specs=[pl.BlockSpec((tm, 4 * DG), lambda i: (nt - 1 - i, 0)),
                   pl.BlockSpec((8, DG), lambda i: (0, 0))],
        out_shape=[jax.ShapeDtypeStruct((T, 4 * DG), BF16), jax.ShapeDtypeStruct((8, DG), F32)],
        scratch_shapes=[pltpu.VMEM((NH, HD, HD), F32),
                        big(F32), big(F32), big(F32),
                        big(BF16), big(BF16), big(BF16), big(BF16), big(BF16),
                        pltpu.VMEM((nc, DG), F32), pltpu.VMEM((nc, DG), F32),
                        pltpu.VMEM((nc * NH, CH, CH), BF16), pltpu.VMEM((nc * NH, CH, CH), BF16),
                        pltpu.VMEM((nc * NH, HD, HD), F32), pltpu.VMEM((nc * NH, HD, HD), F32),
                        big(F32), big(F32), big(F32), big(F32), big(F32)],
        compiler_params=_cparams(("arbitrary",), 56),
    )(pb, pf, pb, pb, o, s_prev, dyab, hg_lb, hg_nw, dep)


def _dproj_pieces(k, da_ref, db_ref):
    if k == 0:
        return [(da_ref[:, 0:SHW], 0)]
    if k == 1:
        return [(da_ref[:, SHW:D], 0), (db_ref[:, 0:DG], D - SHW)]
    if k == 2:
        return [(db_ref[:, DG:DG + SHW], 0)]
    return [(db_ref[:, DG + SHW:4 * DG], 0)]


def _bwd_inproj_dx(x, dh1, d_a, d_b, w_in_b, nw, dep):
    tm = 512
    nt = T // tm

    def body(x_ref, dh1_ref, da_ref, db_ref, w_ref, nw_ref, dep_ref, dx_ref, sm_ref):
        i = pl.program_id(0)

        @pl.when(i == 0)
        def _():
            sm_ref[...] = jnp.zeros_like(sm_ref)

        du = None
        for k in range(NSHARD):
            for val, off in _dproj_pieces(k, da_ref, db_ref):
                t = _mm_nt(val, w_ref[k, :, off:off + val.shape[1]])
                du = t if du is None else du + t
        xv = x_ref[...]
        s = lax.rsqrt(jnp.mean(xv * xv, axis=-1, keepdims=True) + EPS)
        xh = xv * s
        sm_ref[0:1, :] += _rowsum(du * xh)
        g = du * nw_ref[...]
        dx_ref[...] = dh1_ref[...] + s * (g - xh * jnp.mean(g * xh, axis=-1, keepdims=True))

    row = lambda w: pl.BlockSpec((tm, w), lambda i: (i, 0))
    return pl.pallas_call(
        body, name="bwd_inproj_dx", grid=(nt,),
        in_specs=[row(D), row(D), row(D), row(4 * DG),
                  pl.BlockSpec((NSHARD, D, SHW), lambda i: (0, 0, 0), pipeline_mode=pl.Buffered(1)),
                  pl.BlockSpec((1, D), lambda i: (0, 0)), ANY_SPEC],
        out_specs=[row(D), pl.BlockSpec((8, D), lambda i: (0, 0))],
        out_shape=[jax.ShapeDtypeStruct((T, D), F32), jax.ShapeDtypeStruct((8, D), F32)],
        compiler_params=_cparams(("arbitrary",), 56),
    )(x, dh1, d_a, d_b, w_in_b, nw, dep)


def _bwd_inproj_dw(name, ks, u_b, d_a, d_b):
    tm = 1024
    nt = T // tm
    nk = len(ks)
    H = D // 2
    step = 128
    need_a = 0 in ks or 1 in ks

    def body(*refs):
        if need_a:
            u_ref, da_ref, db_ref, dw_ref, acc, send_s, recv_s, ssem, rsem = refs
        else:
            u_ref, db_ref, dw_ref, acc, send_s, recv_s, ssem, rsem = refs
            da_ref = None
        i = pl.program_id(0)

        @pl.when(i == 0)
        def _():
            acc[...] = jnp.zeros_like(acc)

        u = u_ref[...]
        for j, k in enumerate(ks):
            for val, off in _dproj_pieces(k, da_ref, db_ref):
                acc[j, :, off:off + val.shape[1]] += _mm_tn(u, val)

        @pl.when(i == nt - 1)
        def _():
            x, y, c = lax.axis_index("x"), lax.axis_index("y"), lax.axis_index("c")
            sibling = (x, y, 1 - c)
            mine0 = pl.multiple_of(c * H, step)
            other0 = pl.multiple_of((1 - c) * H, step)
            copies = []
            for j in range(nk):
                for r0 in range(0, H, step):
                    send_s[j, r0:r0 + step, :] = acc[j, pl.ds(other0 + r0, step), :].astype(BF16)
                cp = _remote(send_s.at[j], recv_s.at[j], ssem.at[j], rsem.at[j], sibling)
                cp.start()
                copies.append(cp)
            for j in range(nk):
                copies[j].wait_recv()
                for r0 in range(0, H, step):
                    s = acc[j, pl.ds(mine0 + r0, step), :] + recv_s[j, r0:r0 + step, :].astype(F32)
                    dw_ref[j, r0:r0 + step, :] = s.astype(BF16)
            for cp in copies:
                cp.wait_send()

    row = lambda w: pl.BlockSpec((tm, w), lambda i: (i, 0))
    ins = [u_b] + ([d_a] if need_a else []) + [d_b]
    in_specs = [row(D)] + ([row(D)] if need_a else []) + [row(4 * DG)]
    return pl.pallas_call(
        body, name=name, grid=(nt,), in_specs=in_specs,
        out_specs=pl.BlockSpec((nk, H, SHW), lambda i: (0, 0, 0)),
        out_shape=jax.ShapeDtypeStruct((nk, H, SHW), BF16),
        scratch_shapes=[pltpu.VMEM((nk, D, SHW), F32), pltpu.VMEM((nk, H, SHW), BF16),
                        pltpu.VMEM((nk, H, SHW), BF16),
                        pltpu.SemaphoreType.DMA((nk,)), pltpu.SemaphoreType.DMA((nk,))],
        compiler_params=_cparams(("arbitrary",), 48),
    )(*ins)


_OUT_ORDER = ["norm_mix_w", "w_in", "conv_w", "conv_b", "rg_wa", "rg_ba", "rg_wx", "rg_bx", "rg_lambda", "hg_lb",
              "hg_norm_w", "w_out", "ple_norm_w", "w_ple_gate", "b_ple_gate", "w_ple_proj", "final_norm_w"]
_BIG = ["w_in", "w_out", "w_ple_gate", "w_ple_proj"]


def _small_view(name, a):
    if name in ("rg_wa", "rg_wx"):
        return a.reshape(DG, RGB)
    if name == "conv_w":
        return a.reshape(4, 128)
    if name == "final_norm_w":
        return a.reshape(1, D)
    return a


def _landing(rows, cols):
    return lax.empty((NDEV, rows, cols), BF16)


def kernel(x, p, norm_mix_w, w_in, conv_w, conv_b, rg_wa, rg_ba, rg_wx, rg_bx, rg_lambda, hg_lb, hg_norm_w, w_out, ple_norm_w, w_ple_gate, b_ple_gate, w_ple_proj, final_norm_w, loss_target, m_norm_mix_w, m_w_in, m_conv_w, m_conv_b, m_rg_wa, m_rg_ba, m_rg_wx, m_rg_bx, m_rg_lambda, m_hg_lb, m_hg_norm_w, m_w_out, m_ple_norm_w, m_w_ple_gate, m_b_ple_gate, m_w_ple_proj, m_final_norm_w, v_norm_mix_w, v_w_in, v_conv_w, v_conv_b, v_rg_wa, v_rg_ba, v_rg_wx, v_rg_bx, v_rg_lambda, v_hg_lb, v_hg_norm_w, v_w_out, v_ple_norm_w, v_w_ple_gate, v_b_ple_gate, v_w_ple_proj, v_final_norm_w):
    given = dict(locals())
    x2, p2, tgt = x[0], p[0, 0], loss_target[0]
    hbm = lambda a: pltpu.with_memory_space_constraint(a, pltpu.HBM)
    norm_mix_w, conv_b, rg_ba, rg_bx, rg_lambda, hg_lb, hg_norm_w, ple_norm_w, b_ple_gate = (
        hbm(a) for a in (norm_mix_w, conv_b, rg_ba, rg_bx, rg_lambda, hg_lb, hg_norm_w, ple_norm_w, b_ple_gate))
    wa_c, wx_c = hbm(_small_view("rg_wa", rg_wa)), hbm(_small_view("rg_wx", rg_wx))

    w_in_b, l_out, l_pg, l_pp, cw3, pf, pb, u_b = _gather_inproj(
        x2, norm_mix_w, w_in[0], w_out[0], w_ple_gate[0], w_ple_proj[0], conv_w[0])
    g_ssem, g_rsem, g_lands, tok = _gather_rest_start([l_out, l_pg, l_pp])

    h, ya = _fwd_rglru(pf, pb, cw3, conv_b, wa_c, rg_ba, wx_c, rg_bx, rg_lambda, tok)
    yb, o, s_prev = _fwd_hgrn2(pf, pb, hg_lb, hg_norm_w)
    w_out_b, w_pg_b, w_pp_b = _gather_rest_wait(g_ssem, g_rsem, g_lands, yb)
    dh1, dyab, dwo_b, dwg_b, dwp_b, sm_tail = _tail_fwd_bwd(
        x2, p2, tgt, ya, yb, w_out_b.reshape(D, D), w_pg_b.reshape(D, D), w_pp_b,
        ple_norm_w, b_ple_gate, final_norm_w.reshape(1, D))

    QH = D // NSHARD // 2
    r1 = _rs_start("rs_start_tail", [dwo_b, dwg_b, dwp_b], (0, 1, 2, 3),
                   [_landing(QH, D), _landing(QH, D), _landing(PLE // 2, PLE)])
    d_b, sm_b = _bwd_hgrn2(pf, pb, o, s_prev, dyab, hg_lb, hg_norm_w, r1[4])
    dw23 = _bwd_inproj_dw("bwd_inproj_dw23", (2, 3), u_b, None, d_b)
    r2 = _rs_start("rs_start_in23", [dw23], (2, 3), [lax.empty((NSHARD, D // 2, SHW), BF16)], chip_sums=True)
    d_a, rg_c, sm_a = _bwd_rglru(pf, pb, h, dyab, cw3, conv_b, wa_c, rg_ba, wx_c, rg_bx, rg_lambda, r2[4])
    dw01 = _bwd_inproj_dw("bwd_inproj_dw01", (0, 1), u_b, d_a, d_b)
    r3 = _rs_start("rs_start_in01", [dw01], (0, 1), r2[3], chip_sums=True)
    grad_x, sm_in = _bwd_inproj_dx(x2, dh1, d_a, d_b, w_in_b, norm_mix_w, r3[4])

    parts1, lands1 = _rs_wait("rs_wait_tail", r1[0], r1[1], r1[2], (0, 1, 2, 3), r1[3], sm_in)
    parts2, lands2 = _rs_wait("rs_wait_in23", r2[0], r2[1], r2[2], (2, 3), r3[3], sm_in, chip_sums=True)
    parts3, lands3 = _rs_wait("rs_wait_in01", r3[0], r3[1], r3[2], (0, 1), lands2, sm_in, chip_sums=True)
    g_big, vred, rgred = _final_reduce([parts3[0], parts2[0]] + parts1, lands3 + lands1,
                                       sm_in, sm_tail, sm_a, sm_b, rg_c)

    upd_big = _adam_big(g_big, [given[n][0] for n in _BIG], [given["m_" + n][0] for n in _BIG],
                        [given["v_" + n][0] for n in _BIG])
    small = _adam_small(vred, rgred,
                        {n: _small_view(n, given[n]) for n in _SMALL_ORDER},
                        {n: _small_view(n, given["m_" + n]) for n in _SMALL_ORDER},
                        {n: _small_view(n, given["v_" + n]) for n in _SMALL_ORDER})

    loss = vred[0, ROW_LOSS, 0]
    outs = [loss, grad_x[None]]
    for ki in range(4):
        for n in _OUT_ORDER:
            if n in _BIG:
                i = _BIG.index(n)
                a = g_big[i] if ki == 0 else upd_big[i][ki - 1]
                outs.append(a[None])
            else:
                outs.append(small[n][ki].reshape(given[n].shape))
    return tuple(outs)
```

```python
import jax
import jax.numpy as jnp
from jax import lax
from jax.experimental import pallas as pl
from jax.experimental.pallas import tpu as pltpu

F32 = jnp.float32
BF16 = jnp.bfloat16
I32 = jnp.int32
MESH = pl.DeviceIdType.MESH
HIGHEST = lax.Precision.HIGHEST

T = 4096
D = 1024
DG = 512
DIN = 3072
PLE = 256
NH = 4
HD = 128
CH = 64
NCHUNK = T // CH
RGB = 64
EPS = 1e-6
RG_C = 8.0
NSHARD = 4
SHW = DIN // NSHARD
NDEV = 8

ADAM_LR = 0.001
ADAM_B1 = 0.9
ADAM_B2 = 0.999
ADAM_EPS = 1e-08
ADAM_WD = 0.01
ADAM_STEP = 10

VMEM_SPEC = pl.BlockSpec(memory_space=pltpu.VMEM)
HBM_SPEC = pl.BlockSpec(memory_space=pltpu.HBM)
SEM_SPEC = pl.BlockSpec(memory_space=pltpu.SEMAPHORE)
ANY_SPEC = pl.BlockSpec(memory_space=pl.ANY)
EFFECT = pltpu.SideEffectType.DATAFLOW_SIDE_EFFECTING
MIB = 1024 * 1024

VROWS = 16
ROW_NORM_MIX, ROW_FINAL_NORM, ROW_B_PG, ROW_PLE_NORM = 0, 1, 2, 3
ROW_CB_BA, ROW_BX_LAM, ROW_CW01, ROW_CW23, ROW_HG_LB, ROW_HG_NW, ROW_LOSS = 4, 5, 6, 7, 8, 9, 10


def _mm(a, b):
    return jnp.dot(a.astype(BF16), b.astype(BF16), preferred_element_type=F32)


def _mm_nt(a, b):
    return lax.dot_general(a.astype(BF16), b.astype(BF16), (((1,), (1,)), ((), ())),
                           preferred_element_type=F32)


def _mm_tn(a, b):
    return lax.dot_general(a.astype(BF16), b.astype(BF16), (((0,), (0,)), ((), ())),
                           preferred_element_type=F32)


def _mm_exact(a, b):
    bb = b.astype(BF16)
    hi = a.astype(BF16)
    r1 = a - hi.astype(F32)
    mid = r1.astype(BF16)
    lo = (r1 - mid.astype(F32)).astype(BF16)
    return sum(jnp.dot(p, bb, preferred_element_type=F32) for p in (hi, mid, lo))


def _sig_pair(x):
    e = jnp.exp(-jnp.abs(x))
    big = 1.0 / (1.0 + e)
    small = e * big
    pos = x >= 0
    return jnp.where(pos, big, small), jnp.where(pos, small, big)


def _sigmoid(x):
    return 1.0 / (1.0 + jnp.exp(-x))


def _rowsum(v):
    return jnp.sum(v, axis=0, keepdims=True)


def _shift_down(cur, prev8, d, rows):
    rolled = pltpu.roll(cur, d, 0)
    head = jnp.where(rows[0:8] < d, pltpu.roll(prev8, d, 0), rolled[0:8])
    return jnp.concatenate([head, rolled[8:]], axis=0)


def _shift_up(cur, next8, d, rows):
    n = cur.shape[0]
    rolled = pltpu.roll(cur, n - d, 0)
    tail = jnp.where(rows[0:8] >= 8 - d, pltpu.roll(next8, 8 - d, 0), rolled[n - 8:n])
    return jnp.concatenate([rolled[0:n - 8], tail], axis=0)


def _roll_in_groups(v, d):
    n, w = v.shape
    return pltpu.roll(v.reshape(n // 8, 8, w), d, 1).reshape(n, w)


def _cparams(sem, vmem_mib):
    return pltpu.CompilerParams(dimension_semantics=sem, vmem_limit_bytes=vmem_mib * MIB)


def _mesh_pos():
    x, y, c = lax.axis_index("x"), lax.axis_index("y"), lax.axis_index("c")
    chips = [(1 - x, y), (x, 1 - y), (1 - x, 1 - y)]
    return x, y, c, chips


def _remote(src, dst, ssem, rsem, dev):
    return pltpu.make_async_remote_copy(src_ref=src, dst_ref=dst, send_sem=ssem, recv_sem=rsem,
                                        device_id=dev, device_id_type=MESH)


def _gather_w_in(w_in, w_out, w_pg, w_pp, conv_w):
    shapes = [w_in.shape, w_out.shape, w_pg.shape, w_pp.shape]

    def body(win_h, wout_h, wpg_h, wpp_h, cw_h, o_in, o_out, o_pg, o_pp, o_cw,
             win, wout, wpg, wpp, cw, lsem, ssem, rsem):
        x, y, c, chips = _mesh_pos()
        kme = 2 * x + y
        sibling = (x, y, 1 - c)
        fetch = [pltpu.make_async_copy(src, dst, lsem.at[i])
                 for i, (src, dst) in enumerate([(win_h, win), (cw_h, cw), (wout_h, wout), (wpg_h, wpg), (wpp_h, wpp)])]
        for cp in fetch:
            cp.start()

        def cast(src, dst):
            for r0 in range(0, src.shape[0], 128):
                dst[kme, r0:r0 + 128, :] = src[r0:r0 + 128, :].astype(BF16)

        fetch[0].wait()
        cast(win, o_in)
        fetch[1].wait()
        o_cw[kme] = cw[...]

        hrows = D // 2
        mine = pl.ds(pl.multiple_of(c * hrows, 128), hrows)
        other = pl.ds(pl.multiple_of((1 - c) * hrows, 128), hrows)

        def half(k, rows):
            return o_in.at[k, rows]

        na = (x ^ c, y ^ (1 - c), c)
        nb = (x ^ (1 - c), y ^ c, c)
        ka = 2 * na[0] + na[1]
        kb = 2 * nb[0] + nb[1]
        kd = 2 * (1 - x) + (1 - y)
        sends = [_remote(half(kme, mine), half(kme, mine), ssem.at[0], rsem.at[0], na),
                 _remote(half(kme, mine), half(kme, mine), ssem.at[1], rsem.at[1], nb)]
        for j, (px, py) in enumerate(chips):
            sends.append(_remote(o_cw.at[kme], o_cw.at[kme], ssem.at[3 + j], rsem.at[3 + j], (px, py, c)))
        for cp in sends:
            cp.start()
        for i, (src, dst) in enumerate([(wout, o_out), (wpg, o_pg), (wpp, o_pp)]):
            fetch[2 + i].wait()
            cast(src, dst)

        def passed_on(k, sem, dev):
            cp = _remote(half(k, mine), half(k, mine), ssem.at[sem], rsem.at[sem], dev)
            cp.start()
            sends.append(cp)

        _remote(half(ka, mine), half(ka, mine), ssem.at[0], rsem.at[0], na).wait_recv()
        passed_on(ka, 2, nb)
        passed_on(ka, 6, sibling)
        _remote(half(kb, mine), half(kb, mine), ssem.at[1], rsem.at[1], nb).wait_recv()
        passed_on(kb, 7, sibling)
        _remote(half(kd, mine), half(kd, mine), ssem.at[2], rsem.at[2], nb).wait_recv()
        passed_on(kd, 8, sibling)
        _remote(half(kb, other), half(kb, other), ssem.at[6], rsem.at[6], sibling).wait_recv()
        _remote(half(ka, other), half(ka, other), ssem.at[7], rsem.at[7], sibling).wait_recv()
        _remote(half(kd, other), half(kd, other), ssem.at[8], rsem.at[8], sibling).wait_recv()
        for j, (px, py) in enumerate(chips):
            kj = 2 * px + py
            _remote(o_cw.at[kj], o_cw.at[kj], ssem.at[3 + j], rsem.at[3 + j], (px, py, c)).wait_recv()
        for cp in sends:
            cp.wait_send()

    out_shape = [jax.ShapeDtypeStruct((NSHARD,) + s, BF16) for s in shapes]
    out_shape.append(jax.ShapeDtypeStruct((NSHARD,) + conv_w.shape, F32))
    return pl.pallas_call(
        body, name="gather_w_in", out_shape=out_shape,
        in_specs=[HBM_SPEC] * 5, out_specs=[VMEM_SPEC] * 5,
        scratch_shapes=[pltpu.VMEM(a.shape, F32) for a in (w_in, w_out, w_pg, w_pp, conv_w)]
        + [pltpu.SemaphoreType.DMA((5,)), pltpu.SemaphoreType.DMA((9,)), pltpu.SemaphoreType.DMA((9,))],
        compiler_params=pltpu.CompilerParams(vmem_limit_bytes=40 * MIB),
    )(*[pltpu.with_memory_space_constraint(a, pltpu.HBM) for a in (w_in, w_out, w_pg, w_pp, conv_w)])


def _gather_inproj(x, nw, w_in, w_out, w_pg, w_pp, conv_w):
    shapes = [w_in.shape, w_out.shape, w_pg.shape, w_pp.shape]
    tm = 512
    nt = T // tm

    def body(x_h, nw_h, win_h, wout_h, wpg_h, wpp_h, cw_h,
             o_in_h, o_out_h, o_pg_h, o_pp_h, o_cw_h, pf_h, pb_h, u_h,
             win, wout, wpg, wpp, cw, nw_s, xbuf, ubuf, rfbuf, rbbuf, o_in, o_out, o_pg, o_pp, o_cw,
             lsem, xsem, usem, fsem, bsem, osem, ssem, rsem):
        x, y, c, chips = _mesh_pos()
        kme = 2 * x + y
        sibling = (x, y, 1 - c)
        fetch = [pltpu.make_async_copy(src, dst, lsem.at[i]) for i, (src, dst) in enumerate(
            [(win_h, win), (cw_h, cw), (nw_h, nw_s), (wout_h, wout), (wpg_h, wpg), (wpp_h, wpp)])]
        for cp in fetch:
            cp.start()

        def cast(src, dst):
            for r0 in range(0, src.shape[0], 128):
                dst[r0:r0 + 128, :] = src[r0:r0 + 128, :].astype(BF16)

        outs = []

        def write_out(src, dst, i):
            cp = pltpu.make_async_copy(src, dst, osem.at[i])
            cp.start()
            outs.append(cp)

        fetch[0].wait()
        cast(win, o_in.at[kme])
        write_out(o_in.at[kme], o_in_h.at[kme], 0)
        fetch[1].wait()
        o_cw[kme] = cw[...]

        hrows = D // 2
        mine = pl.ds(pl.multiple_of(c * hrows, 128), hrows)
        other = pl.ds(pl.multiple_of((1 - c) * hrows, 128), hrows)

        def half(k, rows):
            return o_in.at[k, rows]

        na = (x ^ c, y ^ (1 - c), c)
        nb = (x ^ (1 - c), y ^ c, c)
        ka = 2 * na[0] + na[1]
        kb = 2 * nb[0] + nb[1]
        kd = 2 * (1 - x) + (1 - y)
        sends = [_remote(half(kme, mine), half(kme, mine), ssem.at[0], rsem.at[0], na),
                 _remote(half(kme, mine), half(kme, mine), ssem.at[1], rsem.at[1], nb)]
        for j, (px, py) in enumerate(chips):
            sends.append(_remote(o_cw.at[kme], o_cw.at[kme], ssem.at[3 + j], rsem.at[3 + j], (px, py, c)))
        for cp in sends:
            cp.start()
        fetch[2].wait()

        def rows_of(t):
            return pl.ds(pl.multiple_of(t * tm, tm), tm)

        def x_copy(t, slot):
            return pltpu.make_async_copy(x_h.at[rows_of(t)], xbuf.at[slot], xsem.at[slot])

        u_out = pltpu.make_async_copy(ubuf, u_h, usem.at[0])

        def f_store(t, slot, kh):
            return pltpu.make_async_copy(rfbuf.at[slot], pf_h.at[rows_of(t), pl.ds(pl.multiple_of(kh * DG, DG), DG)],
                                         fsem.at[slot])

        def b_store_even(t, slot, kh):
            return pltpu.make_async_copy(rbbuf.at[slot, :, 0:256],
                                         pb_h.at[rows_of(t), pl.ds(pl.multiple_of(kh * 2 * DG, 256), 256)],
                                         bsem.at[slot])

        def b_store_odd(t, slot, kh):
            return pltpu.make_async_copy(rbbuf.at[slot],
                                         pb_h.at[rows_of(t), pl.ds(pl.multiple_of(256 + kh * 2 * DG, 256), SHW)],
                                         bsem.at[slot])

        def run_pass(k, first, even):
            kh = k >> 1
            if first:
                x_copy(0, 0).start()

            def tile(t, carry):
                slot = t & 1
                if first:
                    @pl.when(t + 1 < nt)
                    def _():
                        x_copy(t + 1, 1 - slot).start()

                    x_copy(t, slot).wait()

                @pl.when(t >= 2)
                def _():
                    if even:
                        f_store(t - 2, slot, kh).wait()
                        b_store_even(t - 2, slot, kh).wait()
                    else:
                        b_store_odd(t - 2, slot, kh).wait()

                if first:
                    xv = xbuf[slot]
                    s = lax.rsqrt(jnp.mean(xv * xv, axis=-1, keepdims=True) + EPS)
                    u = (xv * s * nw_s[...]).astype(BF16)
                    ubuf[rows_of(t), :] = u
                else:
                    u = ubuf[rows_of(t), :]
                r = jnp.dot(u, o_in[k], preferred_element_type=F32)
                if even:
                    rfbuf[slot] = r[:, 0:DG]
                    rbbuf[slot, :, 0:256] = r[:, DG:SHW].astype(BF16)
                    f_store(t, slot, kh).start()
                    b_store_even(t, slot, kh).start()
                else:
                    rbbuf[slot] = r.astype(BF16)
                    b_store_odd(t, slot, kh).start()
                return carry

            lax.fori_loop(0, nt, tile, 0)
            for t in (nt - 2, nt - 1):
                slot = t & 1
                if even:
                    f_store(t, slot, kh).wait()
                    b_store_even(t, slot, kh).wait()
                else:
                    b_store_odd(t, slot, kh).wait()

        def project(k, first=False):
            @pl.when((k & 1) == 0)
            def _():
                run_pass(k, first, True)

            @pl.when((k & 1) == 1)
            def _():
                run_pass(k, first, False)

        def passed_on(k, sem, dev):
            cp = _remote(half(k, mine), half(k, mine), ssem.at[sem], rsem.at[sem], dev)
            cp.start()
            sends.append(cp)

        project(kme, first=True)
        u_out.start()
        for i, (src, dst, dst_h) in enumerate([(wout, o_out, o_out_h), (wpg, o_pg, o_pg_h), (wpp, o_pp, o_pp_h)]):
            fetch[3 + i].wait()
            cast(src, dst)
            write_out(dst, dst_h.at[kme], 4 + i)
        _remote(half(ka, mine), half(ka, mine), ssem.at[0], rsem.at[0], na).wait_recv()
        passed_on(ka, 2, nb)
        passed_on(ka, 6, sibling)
        _remote(half(kb, mine), half(kb, mine), ssem.at[1], rsem.at[1], nb).wait_recv()
        passed_on(kb, 7, sibling)
        _remote(half(ka, other), half(ka, other), ssem.at[7], rsem.at[7], sibling).wait_recv()
        write_out(o_in.at[ka], o_in_h.at[ka], 1)
        project(ka)
        _remote(half(kd, mine), half(kd, mine), ssem.at[2], rsem.at[2], nb).wait_recv()
        passed_on(kd, 8, sibling)
        _remote(half(kb, other), half(kb, other), ssem.at[6], rsem.at[6], sibling).wait_recv()
        write_out(o_in.at[kb], o_in_h.at[kb], 2)
        project(kb)
        _remote(half(kd, other), half(kd, other), ssem.at[8], rsem.at[8], sibling).wait_recv()
        write_out(o_in.at[kd], o_in_h.at[kd], 3)
        project(kd)
        u_out.wait()
        for j, (px, py) in enumerate(chips):
            kj = 2 * px + py
            _remote(o_cw.at[kj], o_cw.at[kj], ssem.at[3 + j], rsem.at[3 + j], (px, py, c)).wait_recv()
        write_out(o_cw, o_cw_h, 7)
        for cp in outs:
            cp.wait()
        for cp in sends:
            cp.wait_send()

    out_shape = [pltpu.HBM((NSHARD,) + s, BF16) for s in shapes]
    out_shape.append(pltpu.HBM((NSHARD,) + conv_w.shape, F32))
    out_shape += [pltpu.HBM((T, 2 * DG), F32), pltpu.HBM((T, 4 * DG), BF16), pltpu.HBM((T, D), BF16)]
    hbm_args = [pltpu.with_memory_space_constraint(a, pltpu.HBM) for a in (x, nw, w_in, w_out, w_pg, w_pp, conv_w)]
    return pl.pallas_call(
        body, name="gather_inproj", out_shape=out_shape,
        in_specs=[HBM_SPEC] * 7, out_specs=[HBM_SPEC] * 8,
        scratch_shapes=[pltpu.VMEM(a.shape, F32) for a in (w_in, w_out, w_pg, w_pp, conv_w, nw)]
        + [pltpu.VMEM((2, tm, D), F32), pltpu.VMEM((T, D), BF16),
           pltpu.VMEM((2, tm, DG), F32), pltpu.VMEM((2, tm, SHW), BF16),
           pltpu.VMEM((NSHARD,) + w_in.shape, BF16), pltpu.VMEM(w_out.shape, BF16), pltpu.VMEM(w_pg.shape, BF16),
           pltpu.VMEM(w_pp.shape, BF16), pltpu.VMEM((NSHARD,) + conv_w.shape, F32),
           pltpu.SemaphoreType.DMA((6,)), pltpu.SemaphoreType.DMA((2,)), pltpu.SemaphoreType.DMA((2,)),
           pltpu.SemaphoreType.DMA((2,)), pltpu.SemaphoreType.DMA((2,)), pltpu.SemaphoreType.DMA((8,)),
           pltpu.SemaphoreType.DMA((9,)), pltpu.SemaphoreType.DMA((9,))],
        compiler_params=pltpu.CompilerParams(vmem_limit_bytes=48 * MIB),
    )(*hbm_args)


def _gather_rest_start(lands):
    n = len(lands)

    def body(*refs):
        land_in = refs[0:n]
        ssem, rsem = refs[n], refs[n + 1]
        token = refs[2 * n + 2]
        x, y, c, chips = _mesh_pos()
        kme = 2 * x + y
        for p, land in enumerate(land_in):
            hrows = land.shape[1] // 2
            mine = pl.ds(pl.multiple_of(c * hrows, 128), hrows)
            for px, py in chips:
                for pc in range(2):
                    _remote(land.at[kme, mine], land.at[kme, mine], ssem.at[p], rsem.at[p], (px, py, pc)).start()
        token[...] = jnp.zeros_like(token)

    out_shape = ([pltpu.SemaphoreType.DMA((n,)), pltpu.SemaphoreType.DMA((n,))]
                 + [pltpu.HBM(a.shape, a.dtype) for a in lands] + [jax.ShapeDtypeStruct((8, 128), F32)])
    outs = pl.pallas_call(
        body, name="gather_rest_start", out_shape=out_shape,
        in_specs=[HBM_SPEC] * n, out_specs=[SEM_SPEC, SEM_SPEC] + [HBM_SPEC] * n + [VMEM_SPEC],
        input_output_aliases={i: 2 + i for i in range(n)},
        compiler_params=pltpu.CompilerParams(has_side_effects=EFFECT),
    )(*[pltpu.with_memory_space_constraint(a, pltpu.HBM) for a in lands])
    return outs[0], outs[1], list(outs[2:2 + n]), outs[2 + n]


def _gather_rest_wait(ssem, rsem, lands, after):
    n = len(lands)

    def body(*refs):
        land_in = refs[0:n]
        ssem_ref, rsem_ref = refs[n], refs[n + 1]
        x, y, c = lax.axis_index("x"), lax.axis_index("y"), lax.axis_index("c")
        for p, land in enumerate(land_in):
            three = land.at[pl.ds(0, 3)]
            cp = _remote(three, three, ssem_ref.at[p], rsem_ref.at[p], (x, y, c))
            cp.wait_send()
            cp.wait_recv()

    outs = pl.pallas_call(
        body, name="gather_rest_wait", out_shape=[pltpu.HBM(a.shape, a.dtype) for a in lands],
        in_specs=[HBM_SPEC] * n + [SEM_SPEC, SEM_SPEC, ANY_SPEC], out_specs=[HBM_SPEC] * n,
        input_output_aliases={i: i for i in range(n)},
        compiler_params=pltpu.CompilerParams(has_side_effects=EFFECT),
    )(*lands, ssem, rsem, after)
    return list(outs)


def _rs_start(name, parts, ks, lands, chip_sums=False):
    n = len(parts)

    def body(*refs):
        part_in, land_in = refs[0:n], refs[n:2 * n]
        ssem, rsem = refs[2 * n], refs[2 * n + 1]
        token = refs[4 * n + 2]
        x, y, c = lax.axis_index("x"), lax.axis_index("y"), lax.axis_index("c")
        kme = 2 * x + y
        me = 4 * x + 2 * y + c
        for p in range(n):
            hrows = land_in[p].shape[1]
            for i, k in enumerate(ks):
                if chip_sums:
                    @pl.when(kme != k)
                    def _():
                        _remote(part_in[p].at[i], land_in[p].at[kme], ssem.at[p], rsem.at[p], (k // 2, k % 2, c)).start()
                    continue
                for pc in range(2):
                    @pl.when(jnp.logical_or(kme != k, c != pc))
                    def _():
                        _remote(part_in[p].at[i, pl.ds(pc * hrows, hrows)], land_in[p].at[me],
                                ssem.at[p], rsem.at[p], (k // 2, k % 2, pc)).start()
        token[...] = jnp.zeros_like(token)

    arrays = list(parts) + list(lands)
    out_shape = ([pltpu.SemaphoreType.DMA((n,)), pltpu.SemaphoreType.DMA((n,))]
                 + [pltpu.HBM(a.shape, a.dtype) for a in arrays] + [jax.ShapeDtypeStruct((8, 128), F32)])
    outs = pl.pallas_call(
        body, name=name, out_shape=out_shape,
        in_specs=[HBM_SPEC] * (2 * n), out_specs=[SEM_SPEC, SEM_SPEC] + [HBM_SPEC] * (2 * n) + [VMEM_SPEC],
        input_output_aliases={i: 2 + i for i in range(2 * n)},
        compiler_params=pltpu.CompilerParams(has_side_effects=EFFECT),
    )(*[pltpu.with_memory_space_constraint(a, pltpu.HBM) for a in arrays])
    return outs[0], outs[1], list(outs[2:2 + n]), list(outs[2 + n:2 + 2 * n]), outs[2 + 2 * n]


def _rs_wait(name, ssem, rsem, parts, ks, lands, after, chip_sums=False):
    n = len(parts)

    def body(*refs):
        part_in, land_in = refs[0:n], refs[n:2 * n]
        ssem_ref, rsem_ref = refs[2 * n], refs[2 * n + 1]
        x, y, c = lax.axis_index("x"), lax.axis_index("y"), lax.axis_index("c")
        kme = 2 * x + y
        for p in range(n):
            piece = land_in[p].at[0]
            for k in ks:
                for pc in range(1 if chip_sums else 2):
                    mine = (kme == k) if chip_sums else jnp.logical_and(kme == k, c == pc)

                    @pl.when(jnp.logical_not(mine))
                    def _():
                        _remote(piece, piece, ssem_ref.at[p], rsem_ref.at[p], (x, y, c)).wait_send()
            owner = kme == ks[0]
            for k in ks[1:]:
                owner = jnp.logical_or(owner, kme == k)

            @pl.when(owner)
            def _():
                others = land_in[p].at[pl.ds(0, land_in[p].shape[0] - 1)]
                _remote(others, others, ssem_ref.at[p], rsem_ref.at[p], (x, y, c)).wait_recv()

    arrays = list(parts) + list(lands)
    outs = pl.pallas_call(
        body, name=name, out_shape=[pltpu.HBM(a.shape, a.dtype) for a in arrays],
        in_specs=[HBM_SPEC] * (2 * n) + [SEM_SPEC, SEM_SPEC, ANY_SPEC], out_specs=[HBM_SPEC] * (2 * n),
        input_output_aliases={i: i for i in range(2 * n)},
        compiler_params=pltpu.CompilerParams(has_side_effects=EFFECT),
    )(*arrays, ssem, rsem, after)
    return list(outs[0:n]), list(outs[n:2 * n])


def _reduce_exchange(parts, lands):
    shapes = [(2 * l.shape[1], l.shape[2]) for l in lands]
    step = 128

    def body(in01, in23, pout, ppg, ppp, l_in, l_out, l_pg, l_pp, g_in, g_out, g_pg, g_pp, ssem, rsem):
        x, y, c = lax.axis_index("x"), lax.axis_index("y"), lax.axis_index("c")
        kme = 2 * x + y
        me = 4 * x + 2 * y + c
        sibling = (x, y, 1 - c)
        sends = []
        for p, (land, gout) in enumerate([(l_in, g_in), (l_out, g_out), (l_pg, g_pg), (l_pp, g_pp)]):
            hrows = land.shape[1]
            mine0 = pl.multiple_of(c * hrows, step)
            for r0 in range(0, hrows, step):
                rs = pl.ds(mine0 + r0, step)
                if p == 0:
                    own = jnp.where(kme >= 2, in23[kme & 1, r0:r0 + step, :], in01[kme & 1, r0:r0 + step, :])
                    slot = kme
                else:
                    own = (pout, ppg, ppp)[p - 1][kme, rs, :]
                    slot = me
                s = jnp.zeros((step, land.shape[2]), F32)
                for j in range(land.shape[0]):
                    s = s + jnp.where(slot == j, own, land[j, r0:r0 + step, :]).astype(F32)
                gout[rs, :] = s
            mine = pl.ds(mine0, hrows)
            cp = _remote(gout.at[mine], gout.at[mine], ssem.at[p], rsem.at[p], sibling)
            cp.start()
            sends.append(cp)
        for p, (land, gout) in enumerate([(l_in, g_in), (l_out, g_out), (l_pg, g_pg), (l_pp, g_pp)]):
            hrows = land.shape[1]
            other = pl.ds(pl.multiple_of((1 - c) * hrows, step), hrows)
            _remote(gout.at[other], gout.at[other], ssem.at[p], rsem.at[p], sibling).wait_recv()
        for cp in sends:
            cp.wait_send()

    return pl.pallas_call(
        body, name="reduce_exchange", out_shape=[jax.ShapeDtypeStruct(s, F32) for s in shapes],
        in_specs=[VMEM_SPEC] * 9, out_specs=[VMEM_SPEC] * 4,
        scratch_shapes=[pltpu.SemaphoreType.DMA((4,)), pltpu.SemaphoreType.DMA((4,))],
        compiler_params=pltpu.CompilerParams(vmem_limit_bytes=48 * MIB),
    )(*parts, *lands)


def _small_allreduce(sm_in, sm_tail, sm_a, sm_b, rg_c):
    PR = DG // NDEV

    def body(in_ref, tail_ref, a_ref, b_ref, rg_ref, v_out, rg_out, vbuf, vrecv, rgrecv, ssem, rsem):
        x, y, c = lax.axis_index("x"), lax.axis_index("y"), lax.axis_index("c")
        me = 4 * x + 2 * y + c

        def pair(ref, r0, r1):
            return jnp.concatenate([ref[r0:r0 + 1, :], ref[r1:r1 + 1, :]], axis=1)

        rows = {
            ROW_NORM_MIX: in_ref[0:1, :], ROW_FINAL_NORM: tail_ref[1:2, :], ROW_B_PG: tail_ref[2:3, :],
            ROW_PLE_NORM: tail_ref[3:4, :], ROW_CB_BA: pair(a_ref, 0, 1), ROW_BX_LAM: pair(a_ref, 2, 3),
            ROW_CW01: pair(a_ref, 4, 5), ROW_CW23: pair(a_ref, 6, 7), ROW_HG_LB: pair(b_ref, 2, 3),
            ROW_HG_NW: jnp.concatenate([b_ref[1:2, :], jnp.zeros((1, DG), F32)], axis=1),
            ROW_LOSS: tail_ref[4:5, :],
        }
        vbuf[...] = jnp.zeros_like(vbuf)
        for r, row in rows.items():
            for j in range(NDEV):
                vbuf[j, r:r + 1, :] = row[:, j * 128:(j + 1) * 128]

        def peer(mask):
            px = x ^ ((mask >> 2) & 1)
            py = y ^ ((mask >> 1) & 1)
            pc = c ^ (mask & 1)
            return (px, py, pc), 4 * px + 2 * py + pc

        def rg_rows(r):
            return pl.ds(pl.multiple_of(r * PR, PR), PR)

        first = []
        for mask in range(1, NDEV):
            dev, r = peer(mask)
            i = mask - 1
            cp = _remote(vbuf.at[r], vrecv.at[i], ssem.at[i], rsem.at[i], dev)
            cp.start()
            first.append(cp)
            cp = _remote(rg_ref.at[rg_rows(r)], rgrecv.at[i], ssem.at[7 + i], rsem.at[7 + i], dev)
            cp.start()
            first.append(cp)
        sv = vbuf[me]
        sr = rg_ref[rg_rows(me), :]
        for i in range(NDEV - 1):
            first[2 * i].wait_recv()
            first[2 * i + 1].wait_recv()
            sv = sv + vrecv[i]
            sr = sr + rgrecv[i]
        v_out[me] = sv
        rg_out[rg_rows(me), :] = sr
        second = []
        for mask in range(1, NDEV):
            dev, r = peer(mask)
            i = mask - 1
            cp = _remote(v_out.at[me], v_out.at[me], ssem.at[14 + i], rsem.at[14 + i], dev)
            cp.start()
            second.append(cp)
            cp = _remote(rg_out.at[rg_rows(me)], rg_out.at[rg_rows(me)], ssem.at[21 + i], rsem.at[21 + i], dev)
            cp.start()
            second.append(cp)
        for mask in range(1, NDEV):
            dev, r = peer(mask)
            i = mask - 1
            _remote(v_out.at[r], v_out.at[r], ssem.at[14 + i], rsem.at[14 + i], dev).wait_recv()
            _remote(rg_out.at[rg_rows(r)], rg_out.at[rg_rows(r)], ssem.at[21 + i], rsem.at[21 + i], dev).wait_recv()
        for cp in first + second:
            cp.wait_send()

    return pl.pallas_call(
        body, name="small_allreduce",
        out_shape=[jax.ShapeDtypeStruct((NDEV, VROWS, 128), F32), jax.ShapeDtypeStruct((DG, 128), F32)],
        in_specs=[VMEM_SPEC] * 5, out_specs=[VMEM_SPEC] * 2,
        scratch_shapes=[pltpu.VMEM((NDEV, VROWS, 128), F32), pltpu.VMEM((NDEV - 1, VROWS, 128), F32),
                        pltpu.VMEM((NDEV - 1, PR, 128), F32),
                        pltpu.SemaphoreType.DMA((28,)), pltpu.SemaphoreType.DMA((28,))],
    )(sm_in, sm_tail, sm_a, sm_b, rg_c)


def _final_reduce(parts, lands, sm_in, sm_tail, sm_a, sm_b, rg_c):
    shapes = [(2 * l.shape[1], l.shape[2]) for l in lands]
    step = 128
    PR = DG // NDEV

    def body(in01, in23, pout, ppg, ppp, l_in, l_out, l_pg, l_pp, in_h, tail_h, a_h, b_h, rg_h,
             g_in, g_out, g_pg, g_pp, v_out, rg_out, vbuf, vrecv, rgrecv,
             lb_in, lb_out, lb_pg, lb_pp, ob_in, ob_out, ob_pg, ob_pp,
             in_ref, tail_ref, a_ref, b_ref, rg_ref, lsem, ssem_l, bsem_s, bsem_r, ssem, rsem):
        x, y, c = lax.axis_index("x"), lax.axis_index("y"), lax.axis_index("c")
        kme = 2 * x + y
        me = 4 * x + 2 * y + c
        sibling = (x, y, 1 - c)
        small_fetch = [pltpu.make_async_copy(src, dst, ssem_l.at[i]) for i, (src, dst) in enumerate(
            [(in_h, in_ref), (tail_h, tail_ref), (a_h, a_ref), (b_h, b_ref), (rg_h, rg_ref)])]
        for cp in small_fetch:
            cp.start()

        lands_hbm = [l_in, l_out, l_pg, l_pp]
        land_bufs = [lb_in, lb_out, lb_pg, lb_pp]
        own_bufs = [ob_in, ob_out, ob_pg, ob_pp]
        fetches = []
        for p in range(4):
            cp = pltpu.make_async_copy(lands_hbm[p], land_bufs[p], lsem.at[p])
            cp.start()
            fetches.append(cp)

        @pl.when(kme >= 2)
        def _():
            pltpu.make_async_copy(in23.at[kme & 1], ob_in, lsem.at[4]).start()

        @pl.when(kme < 2)
        def _():
            pltpu.make_async_copy(in01.at[kme & 1], ob_in, lsem.at[4]).start()

        own_fetches = [pltpu.make_async_copy(in01.at[0], ob_in, lsem.at[4])]
        for p, part in enumerate([pout, ppg, ppp]):
            hrows = own_bufs[p + 1].shape[0]
            cp = pltpu.make_async_copy(part.at[kme, pl.ds(pl.multiple_of(c * hrows, step), hrows)],
                                       own_bufs[p + 1], lsem.at[5 + p])
            cp.start()
            own_fetches.append(cp)

        def pair(ref, r0, r1):
            return jnp.concatenate([ref[r0:r0 + 1, :], ref[r1:r1 + 1, :]], axis=1)

        for cp in small_fetch:
            cp.wait()
        rows = {
            ROW_NORM_MIX: in_ref[0:1, :], ROW_FINAL_NORM: tail_ref[1:2, :], ROW_B_PG: tail_ref[2:3, :],
            ROW_PLE_NORM: tail_ref[3:4, :], ROW_CB_BA: pair(a_ref, 0, 1), ROW_BX_LAM: pair(a_ref, 2, 3),
            ROW_CW01: pair(a_ref, 4, 5), ROW_CW23: pair(a_ref, 6, 7), ROW_HG_LB: pair(b_ref, 2, 3),
            ROW_HG_NW: jnp.concatenate([b_ref[1:2, :], jnp.zeros((1, DG), F32)], axis=1),
            ROW_LOSS: tail_ref[4:5, :],
        }
        vbuf[...] = jnp.zeros_like(vbuf)
        for r, row in rows.items():
            for j in range(NDEV):
                vbuf[j, r:r + 1, :] = row[:, j * 128:(j + 1) * 128]

        def peer(mask):
            px = x ^ ((mask >> 2) & 1)
            py = y ^ ((mask >> 1) & 1)
            pc = c ^ (mask & 1)
            return (px, py, pc), 4 * px + 2 * py + pc

        def rg_rows(r):
            return pl.ds(pl.multiple_of(r * PR, PR), PR)

        first = []
        for mask in range(1, NDEV):
            dev, r = peer(mask)
            i = mask - 1
            cp = _remote(vbuf.at[r], vrecv.at[i], ssem.at[i], rsem.at[i], dev)
            cp.start()
            first.append(cp)
            cp = _remote(rg_ref.at[rg_rows(r)], rgrecv.at[i], ssem.at[7 + i], rsem.at[7 + i], dev)
            cp.start()
            first.append(cp)

        big = [(lb_in, g_in), (lb_out, g_out), (lb_pg, g_pg), (lb_pp, g_pp)]
        swaps = []
        for p, (land, gout) in enumerate(big):
            fetches[p].wait()
            own_fetches[p].wait()
            hrows = land.shape[1]
            mine0 = pl.multiple_of(c * hrows, step)
            slot = kme if p == 0 else me
            for r0 in range(0, hrows, step):
                rs = pl.ds(mine0 + r0, step)
                own = own_bufs[p][r0:r0 + step, :]
                s = jnp.zeros((step, land.shape[2]), F32)
                for j in range(land.shape[0]):
                    s = s + jnp.where(slot == j, own, land[j, r0:r0 + step, :]).astype(F32)
                gout[rs, :] = s
            mine = pl.ds(mine0, hrows)
            cp = _remote(gout.at[mine], gout.at[mine], bsem_s.at[p], bsem_r.at[p], sibling)
            cp.start()
            swaps.append(cp)

        sv = vbuf[me]
        sr = rg_ref[rg_rows(me), :]
        for i in range(NDEV - 1):
            first[2 * i].wait_recv()
            first[2 * i + 1].wait_recv()
            sv = sv + vrecv[i]
            sr = sr + rgrecv[i]
        v_out[me] = sv
        rg_out[rg_rows(me), :] = sr
        second = []
        for mask in range(1, NDEV):
            dev, r = peer(mask)
            i = mask - 1
            cp = _remote(v_out.at[me], v_out.at[me], ssem.at[14 + i], rsem.at[14 + i], dev)
            cp.start()
            second.append(cp)
            cp = _remote(rg_out.at[rg_rows(me)], rg_out.at[rg_rows(me)], ssem.at[21 + i], rsem.at[21 + i], dev)
            cp.start()
            second.append(cp)
        for p, (land, gout) in enumerate(big):
            hrows = land.shape[1]
            other = pl.ds(pl.multiple_of((1 - c) * hrows, step), hrows)
            _remote(gout.at[other], gout.at[other], bsem_s.at[p], bsem_r.at[p], sibling).wait_recv()
        for mask in range(1, NDEV):
            dev, r = peer(mask)
            i = mask - 1
            _remote(v_out.at[r], v_out.at[r], ssem.at[14 + i], rsem.at[14 + i], dev).wait_recv()
            _remote(rg_out.at[rg_rows(r)], rg_out.at[rg_rows(r)], ssem.at[21 + i], rsem.at[21 + i], dev).wait_recv()
        for cp in first + swaps + second:
            cp.wait_send()

    out_shape = [jax.ShapeDtypeStruct(s, F32) for s in shapes]
    out_shape += [jax.ShapeDtypeStruct((NDEV, VROWS, 128), F32), jax.ShapeDtypeStruct((DG, 128), F32)]
    outs = pl.pallas_call(
        body, name="final_reduce", out_shape=out_shape,
        in_specs=[HBM_SPEC] * 14, out_specs=[VMEM_SPEC] * 6,
        scratch_shapes=[pltpu.VMEM((NDEV, VROWS, 128), F32), pltpu.VMEM((NDEV - 1, VROWS, 128), F32),
                        pltpu.VMEM((NDEV - 1, PR, 128), F32)]
        + [pltpu.VMEM(l.shape, BF16) for l in lands]
        + [pltpu.VMEM(l.shape[1:], BF16) for l in lands]
        + [pltpu.VMEM(a.shape, F32) for a in (sm_in, sm_tail, sm_a, sm_b, rg_c)]
        + [pltpu.SemaphoreType.DMA((8,)), pltpu.SemaphoreType.DMA((5,)),
                        pltpu.SemaphoreType.DMA((4,)), pltpu.SemaphoreType.DMA((4,)),
                        pltpu.SemaphoreType.DMA((28,)), pltpu.SemaphoreType.DMA((28,))],
        compiler_params=pltpu.CompilerParams(vmem_limit_bytes=48 * MIB),
    )(*[pltpu.with_memory_space_constraint(a, pltpu.HBM)
        for a in (*parts, *lands, sm_in, sm_tail, sm_a, sm_b, rg_c)])
    return list(outs[0:4]), outs[4], outs[5]


def _adam_rows(w, g, m, v):
    m2 = ADAM_B1 * m + (1.0 - ADAM_B1) * g
    v2 = ADAM_B2 * v + (1.0 - ADAM_B2) * (g * g)
    m_hat = m2 / (1.0 - ADAM_B1 ** ADAM_STEP)
    v_hat = v2 / (1.0 - ADAM_B2 ** ADAM_STEP)
    delta = -ADAM_LR * (m_hat / (jnp.sqrt(v_hat) + ADAM_EPS) + ADAM_WD * w)
    return delta, m2, v2


def _adam_big(gs, ws, ms, vs):
    n = len(gs)
    steps = 8

    def body(*refs):
        ins, outs = refs[:4 * n], refs[4 * n:]
        for i in range(n):
            g, w, m, v = (r[...] for r in ins[4 * i:4 * i + 4])
            d, m2, v2 = _adam_rows(w, g, m, v)
            outs[3 * i][...] = d
            outs[3 * i + 1][...] = m2
            outs[3 * i + 2][...] = v2

    in_specs, out_specs, out_shape, args = [], [], [], []
    for g, w, m, v in zip(gs, ws, ms, vs):
        r, c = w.shape
        spec = lambda: pl.BlockSpec((r // steps, c), lambda i: (i, 0))
        in_specs += [spec() for _ in range(4)]
        out_specs += [spec() for _ in range(3)]
        out_shape += [jax.ShapeDtypeStruct((r, c), F32)] * 3
        args += [pltpu.with_memory_space_constraint(a, pltpu.HBM) for a in (g, w, m, v)]
    outs = pl.pallas_call(
        body, name="adam_big", grid=(steps,), in_specs=in_specs, out_specs=out_specs, out_shape=out_shape,
        compiler_params=_cparams(("parallel",), 32),
    )(*args)
    return [tuple(outs[3 * i:3 * i + 3]) for i in range(n)]


_VEC_PARAMS = [
    ("norm_mix_w", ROW_NORM_MIX, 0, D), ("final_norm_w", ROW_FINAL_NORM, 0, D),
    ("b_ple_gate", ROW_B_PG, 0, D), ("ple_norm_w", ROW_PLE_NORM, 0, D),
    ("conv_b", ROW_CB_BA, 0, DG), ("rg_ba", ROW_CB_BA, DG, DG),
    ("rg_bx", ROW_BX_LAM, 0, DG), ("rg_lambda", ROW_BX_LAM, DG, DG),
    ("hg_norm_w", ROW_HG_NW, 0, HD),
]
_SMALL_ORDER = [n for n, _, _, _ in _VEC_PARAMS] + ["hg_lb", "conv_w", "rg_wa", "rg_wx"]


def _adam_small(vred, rgred, ws, ms, vs):
    names = _SMALL_ORDER
    n = len(names)

    def body(vred_ref, rg_ref, *refs):
        w_refs = dict(zip(names, refs[0:n]))
        m_refs = dict(zip(names, refs[n:2 * n]))
        v_refs = dict(zip(names, refs[2 * n:3 * n]))
        outs = refs[3 * n:]
        o_refs = {nm: outs[4 * i:4 * i + 4] for i, nm in enumerate(names)}
        kme = 2 * lax.axis_index("x") + lax.axis_index("y")

        def update(nm, g, idx):
            d, m2, v2 = _adam_rows(w_refs[nm][idx], g, m_refs[nm][idx], v_refs[nm][idx])
            og, od, om, ov = o_refs[nm]
            og[idx] = g
            od[idx] = d
            om[idx] = m2
            ov[idx] = v2

        def packed(row, lane0, width):
            return jnp.concatenate([vred_ref[j, row:row + 1, :] for j in range(lane0 // 128, (lane0 + width) // 128)],
                                   axis=1)

        everything = (slice(None), slice(None))
        for nm, row, lane0, width in _VEC_PARAMS:
            update(nm, packed(row, lane0, width), everything)
        for r in range(2):
            update("hg_lb", packed(ROW_HG_LB, r * DG, DG), (slice(r, r + 1), slice(None)))
        for j in range(4):
            g = vred_ref[(j % 2) * 4 + kme, ROW_CW01 + j // 2:ROW_CW01 + j // 2 + 1, :]
            update("conv_w", g, (slice(j, j + 1), slice(None)))
        for r0 in range(0, DG, 128):
            rs = (slice(r0, r0 + 128), slice(None))
            both = rg_ref[r0:r0 + 128, :]
            update("rg_wa", both[:, 0:RGB], rs)
            update("rg_wx", pltpu.roll(both, RGB, 1)[:, 0:RGB], rs)

    args = [vred, rgred] + [d[nm] for d in (ws, ms, vs) for nm in names]
    out_shape = []
    for nm in names:
        out_shape += [jax.ShapeDtypeStruct(ws[nm].shape, F32)] * 4
    whole = lambda s: pl.BlockSpec(s.shape, lambda i, nd=len(s.shape): (0,) * nd)
    outs = pl.pallas_call(
        body, name="adam_small", out_shape=out_shape, grid=(1,),
        in_specs=[whole(a) for a in args], out_specs=[whole(s) for s in out_shape],
    )(*args)
    return {nm: tuple(outs[4 * i:4 * i + 4]) for i, nm in enumerate(names)}


def _fwd_inproj(x, nw, w_in_b, dep):
    tm = 512

    def body(x_ref, nw_ref, w_ref, dep_ref, pf_ref, pb_ref, u_ref):
        xv = x_ref[...]
        s = lax.rsqrt(jnp.mean(xv * xv, axis=-1, keepdims=True) + EPS)
        u = (xv * s * nw_ref[...]).astype(BF16)
        u_ref[...] = u
        r = [jnp.dot(u, w_ref[k], preferred_element_type=F32) for k in range(NSHARD)]
        h = DG // 2
        pf_ref[:, 0:DG] = r[0][:, 0:DG]
        pb_ref[:, 0:h] = r[0][:, DG:SHW].astype(BF16)
        pb_ref[:, h:DG] = r[1][:, 0:h].astype(BF16)
        pb_ref[:, DG:2 * DG] = r[1][:, h:SHW].astype(BF16)
        pf_ref[:, DG:2 * DG] = r[2][:, 0:DG]
        pb_ref[:, 2 * DG:2 * DG + h] = r[2][:, DG:SHW].astype(BF16)
        pb_ref[:, 2 * DG + h:3 * DG] = r[3][:, 0:h].astype(BF16)
        pb_ref[:, 3 * DG:4 * DG] = r[3][:, h:SHW].astype(BF16)

    return pl.pallas_call(
        body, name="fwd_inproj", grid=(T // tm,),
        in_specs=[pl.BlockSpec((tm, D), lambda i: (i, 0)),
                  pl.BlockSpec((1, D), lambda i: (0, 0)),
                  pl.BlockSpec((NSHARD, D, SHW), lambda i: (0, 0, 0), pipeline_mode=pl.Buffered(1)), ANY_SPEC],
        out_specs=[pl.BlockSpec((tm, 2 * DG), lambda i: (i, 0)),
                   pl.BlockSpec((tm, 4 * DG), lambda i: (i, 0)),
                   pl.BlockSpec((tm, D), lambda i: (i, 0))],
        out_shape=[jax.ShapeDtypeStruct((T, 2 * DG), F32), jax.ShapeDtypeStruct((T, 4 * DG), BF16),
                   jax.ShapeDtypeStruct((T, D), BF16)],
        compiler_params=_cparams(("parallel",), 48),
    )(x, nw, w_in_b, dep)


def _conv_rows(cw_ref):
    return [jnp.concatenate([cw_ref[k, j:j + 1, :] for k in range(NSHARD)], axis=1) for j in range(4)]


def _rg_conv(xa, prev8, cw, cb, rows):
    taps = [_shift_down(xa, prev8, 3, rows), _shift_down(xa, prev8, 2, rows),
            _shift_down(xa, prev8, 1, rows), xa]
    xc = cb
    for j in range(4):
        xc = xc + taps[j] * cw[j]
    return xc, taps


def _block_mask():
    r = lax.broadcasted_iota(I32, (DG, DG), 0)
    c = lax.broadcasted_iota(I32, (DG, DG), 1)
    return (r >> 6) == (c >> 6)


def _dense_from_blocks(wc):
    j = lax.broadcasted_iota(I32, (RGB, DG), 0)
    c = lax.broadcasted_iota(I32, (RGB, DG), 1)
    spread = _mm_exact(wc, ((c & (RGB - 1)) == j).astype(F32))
    return jnp.where(_block_mask(), spread, 0.0)


def _blocks_from_dense(da, dx):
    c = lax.broadcasted_iota(I32, (DG, 128), 0)
    j = lax.broadcasted_iota(I32, (DG, 128), 1)
    hit = (c & (RGB - 1)) == (j & (RGB - 1))
    mask = _block_mask()
    return (_mm_exact(jnp.where(mask, da, 0.0), (hit & (j < RGB)).astype(F32))
            + _mm_exact(jnp.where(mask, dx, 0.0), (hit & (j >= RGB)).astype(F32)))


def _rg_gates(xc, wa, ba, wx, bx, sp, first_row):
    r = _sigmoid(_mm(xc, wa) + ba)
    i = _sigmoid(_mm(xc, wx) + bx)
    log_a = (-RG_C) * r * sp
    a = jnp.exp(log_a)
    a2 = a * a
    one_m_a2 = -jnp.tanh(log_a) * (a2 + 1.0)
    mult = jnp.where(first_row, 1.0, jnp.sqrt(one_m_a2))
    return r, i, a, a2, mult


def _softplus(z):
    return jnp.maximum(z, 0.0) + jnp.log1p(jnp.exp(-jnp.abs(z)))


def _fwd_rglru(pf, pb, cw3, conv_b, wa_c, ba, wx_c, bx, lam, dep):
    tm = 512
    ng = tm // 8

    def body(xa_ref, ga_ref, cw_ref, cb_ref, wa_ref, ba_ref, wx_ref, bx_ref, lam_ref, dep_ref,
             h_ref, ya_ref, a_s, u_s, tail_s, hc_s, wa_s, wx_s):
        i = pl.program_id(0)

        @pl.when(i == 0)
        def _():
            tail_s[...] = jnp.zeros_like(tail_s)
            hc_s[...] = jnp.zeros_like(hc_s)
            wa_s[...] = _dense_from_blocks(wa_ref[...]).astype(BF16)
            wx_s[...] = _dense_from_blocks(wx_ref[...]).astype(BF16)

        rows = lax.broadcasted_iota(I32, (tm, DG), 0)
        xa = xa_ref[...]
        xc, _ = _rg_conv(xa, tail_s[...], _conv_rows(cw_ref), cb_ref[...], rows)
        tail_s[...] = xa[tm - 8:tm, :]
        sp = _softplus(-lam_ref[...])
        rp = _mm(xc, wa_s[...]) + ba_ref[...]
        ip = _mm(xc, wx_s[...]) + bx_ref[...]
        rb = 64
        rows_b = lax.broadcasted_iota(I32, (rb, DG), 0)
        rows8 = rows_b & 7
        carry = hc_s[0:1, :]
        for b0 in range(0, tm, rb):
            sl = slice(b0, b0 + rb)
            r = _sigmoid(rp[sl])
            ig = _sigmoid(ip[sl])
            log_a = (-RG_C) * r * sp
            av = jnp.exp(log_a)
            mult = jnp.sqrt(-jnp.tanh(log_a) * (av * av + 1.0))
            if b0 == 0:
                mult = jnp.where((rows_b + i * tm) == 0, 1.0, mult)
            uv = mult * (ig * xc[sl])
            for d in (1, 2, 4):
                keep = rows8 >= d
                uv = uv + av * jnp.where(keep, _roll_in_groups(uv, d), 0.0)
                av = av * jnp.where(keep, _roll_in_groups(av, d), 1.0)
            ga = ga_ref[sl, :].astype(F32)
            gate = ga * _sigmoid(ga)
            hs = []
            for g in range(rb // 8):
                gs = slice(g * 8, (g + 1) * 8)
                hv = uv[gs] + av[gs] * carry
                carry = hv[7:8, :]
                hs.append(hv)
            hb = jnp.concatenate(hs, axis=0)
            h_ref[sl, :] = hb
            ya_ref[sl, :] = (hb * gate).astype(BF16)
        hc_s[0:1, :] = carry

    vec = lambda: pl.BlockSpec((1, DG), lambda i: (0, 0))
    blocks = lambda: pl.BlockSpec((DG, RGB), lambda i: (0, 0))
    return pl.pallas_call(
        body, name="fwd_rglru", grid=(T // tm,),
        in_specs=[pl.BlockSpec((tm, DG), lambda i: (i, 0)),
                  pl.BlockSpec((tm, DG), lambda i: (i, 0)),
                  pl.BlockSpec((NSHARD, 4, 128), lambda i: (0, 0, 0)), vec(),
                  blocks(), vec(), blocks(), vec(), vec(), ANY_SPEC],
        out_specs=[pl.BlockSpec((tm, DG), lambda i: (i, 0)),
                   pl.BlockSpec((tm, DG), lambda i: (i, 0))],
        out_shape=[jax.ShapeDtypeStruct((T, DG), F32), pltpu.HBM((T, DG), BF16)],
        scratch_shapes=[pltpu.VMEM((tm, DG), F32), pltpu.VMEM((tm, DG), F32),
                        pltpu.VMEM((8, DG), F32), pltpu.VMEM((8, DG), F32),
                        pltpu.VMEM((DG, DG), BF16), pltpu.VMEM((DG, DG), BF16)],
        compiler_params=_cparams(("arbitrary",), 48),
    )(pf, pb, cw3, conv_b, wa_c, ba, wx_c, bx, lam, dep)


def _hg_lower_bound(lb_ref):
    return _sig_pair(lb_ref[0:1, :] - lb_ref[1:2, :])


def _hg_gates(fz, lb, one_m_lb):
    sg, sn = _sig_pair(fz)
    f = lb + one_m_lb * sg
    return sg, sn, f, jnp.log(f), one_m_lb * sn


def _tri(lower):
    r = lax.broadcasted_iota(I32, (CH, CH), 0)
    c = lax.broadcasted_iota(I32, (CH, CH), 1)
    return (r >= c) if lower else (r <= c)


def _split3(v):
    hi = v.astype(BF16)
    r1 = v - hi.astype(F32)
    mid = r1.astype(BF16)
    lo = (r1 - mid.astype(F32)).astype(BF16)
    return hi, mid, lo


def _chunk_cumsum(v, rows64, reverse=False):
    del rows64
    tri = _tri(not reverse).astype(BF16)
    out = []
    for c in range(v.shape[0] // CH):
        pieces = _split3(v[c * CH:(c + 1) * CH])
        out.append(sum(jnp.dot(tri, p, preferred_element_type=F32) for p in pieces))
    return jnp.concatenate(out, axis=0)


def _chunk_rev_cumsum(v, rows64):
    return _chunk_cumsum(v, rows64, reverse=True)


def _hg_recompute(q_ref, f_ref, lb, one_m_lb, rows64, eb_s, enb_s, ekd_s, qe_s, ke_s, kd_s, dec_s):
    nc = q_ref.shape[0] // CH
    sg, sn, f, logf, k = _hg_gates(f_ref[...], lb, one_m_lb)
    q = q_ref[...].astype(F32)
    sq = _sigmoid(q)
    qs = q * sq * (HD ** -0.5)
    b = _chunk_cumsum(logf, rows64)
    for c in range(nc):
        rs = slice(c * CH, (c + 1) * CH)
        b_c = b[rs]
        bl = b_c[CH - 1:CH, :]
        eb, enb, ekd = jnp.exp(b_c), jnp.exp(-b_c), jnp.exp(bl - b_c)
        if eb_s is not None:
            eb_s[rs, :] = eb
            enb_s[rs, :] = enb
            ekd_s[rs, :] = ekd
        qe_s[rs, :] = (qs[rs] * eb).astype(BF16)
        ke_s[rs, :] = (k[rs] * enb).astype(BF16)
        kd_s[rs, :] = (k[rs] * ekd).astype(BF16)
        dec_s[c:c + 1, :] = jnp.exp(bl)
    return sg, sn, f, k, q, sq, qs


def _fwd_hgrn2(pf, pb, hg_lb, hg_nw):
    tm = 512
    nc = tm // CH

    def body(q_ref, f_ref, v_ref, g_ref, lb_ref, nw_ref, yb_ref, o_ref, sp_ref,
             st_s, qe_s, ke_s, kd_s, vb_s, dec_s, p_s, ds_s):
        i = pl.program_id(0)

        @pl.when(i == 0)
        def _():
            st_s[...] = jnp.zeros_like(st_s)

        lb, one_m_lb = _hg_lower_bound(lb_ref)
        rows64 = lax.broadcasted_iota(I32, (tm, DG), 0) & (CH - 1)
        _hg_recompute(q_ref, f_ref, lb, one_m_lb, rows64, None, None, None, qe_s, ke_s, kd_s, dec_s)
        vb_s[...] = v_ref[...]
        mask = _tri(True)
        items = [(c, hd, slice(c * CH, (c + 1) * CH), slice(hd * HD, (hd + 1) * HD))
                 for c in range(nc) for hd in range(NH)]
        for c, hd, rs, cols in items:
            p_s[c * NH + hd] = jnp.where(mask, _mm_nt(qe_s[rs, cols], ke_s[rs, cols]), 0.0).astype(BF16)
            ds_s[c * NH + hd] = _mm_tn(vb_s[rs, cols], kd_s[rs, cols])
        for c, hd, rs, cols in items:
            st = st_s[hd]
            sp_ref[hd, c] = st
            st_s[hd] = st * dec_s[c:c + 1, cols] + ds_s[c * NH + hd]
        for c, hd, rs, cols in items:
            o_ref[rs, cols] = _mm(p_s[c * NH + hd], vb_s[rs, cols]) + _mm_nt(qe_s[rs, cols], sp_ref[hd, c])
        nw = nw_ref[...]
        for hd in range(NH):
            cols = slice(hd * HD, (hd + 1) * HD)
            o = o_ref[:, cols]
            so = lax.rsqrt(jnp.mean(o * o, axis=-1, keepdims=True) + EPS)
            g = g_ref[:, cols].astype(F32)
            sg = _sigmoid(g)
            yb_ref[:, cols] = (o * so * nw * (g * sg)).astype(BF16)

    col = lambda j: pl.BlockSpec((tm, DG), lambda i: (i, j))
    return pl.pallas_call(
        body, name="fwd_hgrn2", grid=(T // tm,),
        in_specs=[col(1), col(1), col(2), col(3),
                  pl.BlockSpec((2, DG), lambda i: (0, 0)),
                  pl.BlockSpec((1, HD), lambda i: (0, 0))],
        out_specs=[pl.BlockSpec((tm, DG), lambda i: (i, 0)),
                   pl.BlockSpec((tm, DG), lambda i: (i, 0)),
                   pl.BlockSpec((NH, nc, HD, HD), lambda i: (0, i, 0, 0))],
        out_shape=[pltpu.HBM((T, DG), BF16), jax.ShapeDtypeStruct((T, DG), F32),
                   jax.ShapeDtypeStruct((NH, NCHUNK, HD, HD), F32)],
        scratch_shapes=[pltpu.VMEM((NH, HD, HD), F32),
                        pltpu.VMEM((tm, DG), BF16), pltpu.VMEM((tm, DG), BF16), pltpu.VMEM((tm, DG), BF16),
                        pltpu.VMEM((tm, DG), BF16), pltpu.VMEM((nc, DG), F32),
                        pltpu.VMEM((nc * NH, CH, CH), BF16), pltpu.VMEM((nc * NH, HD, HD), F32)],
        compiler_params=_cparams(("arbitrary",), 48),
    )(pb, pf, pb, pb, hg_lb, hg_nw)


def _tail_fwd_bwd(x, p, tgt, ya, yb, w_out_b, w_pg_b, w_pp_b, ple_nw, b_pg, fnw):
    tm = 512
    nt = T // tm
    QR = D // NSHARD

    def body(x_ref, p_ref, t_ref, ya_ref, yb_ref, wo_ref, wg_ref, wp_ref, pw_ref, b_ref, fw_ref,
             dh1_ref, dyab_ref, dwo_ref, dwg_ref, dwp_ref, sm_ref, dwo_s, dwg_s, dwp_s):
        i = pl.program_id(0)

        @pl.when(i == 0)
        def _():
            dwo_s[...] = jnp.zeros_like(dwo_s)
            dwg_s[...] = jnp.zeros_like(dwg_s)
            dwp_s[...] = jnp.zeros_like(dwp_s)
            sm_ref[...] = jnp.zeros_like(sm_ref)

        ya = ya_ref[...]
        yb = yb_ref[...]
        pv = p_ref[...].astype(BF16)
        pw = pw_ref[...]
        fw = fw_ref[...]
        h1 = x_ref[...] + _mm(ya, wo_ref[0:DG, :]) + _mm(yb, wo_ref[DG:D, :])
        s2 = lax.rsqrt(jnp.mean(h1 * h1, axis=-1, keepdims=True) + EPS)
        n2h = h1 * s2
        n2 = (n2h * pw).astype(BF16)
        z = _mm(n2, wg_ref[...]) + b_ref[...]
        gate = _sigmoid(z)
        pp = jnp.concatenate([_mm(pv, wp_ref[k]) for k in range(NSHARD)], axis=1)
        h2 = h1 + gate * pp
        s3 = lax.rsqrt(jnp.mean(h2 * h2, axis=-1, keepdims=True) + EPS)
        hn = h2 * s3
        err = hn * fw - t_ref[...]
        sm_ref[0:1, :] += _rowsum(err * err)
        dy = err * (1.0 / D)
        sm_ref[1:2, :] += _rowsum(dy * hn)
        g3 = dy * fw
        dh2 = s3 * (g3 - hn * jnp.mean(g3 * hn, axis=-1, keepdims=True))
        dpp = (dh2 * gate).astype(BF16)
        dz = dh2 * pp * gate * (1.0 - gate)
        sm_ref[2:3, :] += _rowsum(dz)
        dzb = dz.astype(BF16)
        dwg_s[...] += _mm_tn(n2, dzb)
        dn2 = _mm_nt(dzb, wg_ref[...])
        for k in range(NSHARD):
            dwp_s[k] += _mm_tn(pv, dpp[:, k * PLE:(k + 1) * PLE])
        sm_ref[3:4, :] += _rowsum(dn2 * n2h)
        g2 = dn2 * pw
        dh1 = dh2 + s2 * (g2 - n2h * jnp.mean(g2 * n2h, axis=-1, keepdims=True))
        dh1_ref[...] = dh1
        dh1b = dh1.astype(BF16)
        dyab_ref[...] = _mm_nt(dh1b, wo_ref[...])
        dwo_s[0:DG, :] += _mm_tn(ya, dh1b)
        dwo_s[DG:D, :] += _mm_tn(yb, dh1b)

        @pl.when(i == nt - 1)
        def _():
            total = jnp.sum(sm_ref[0:1, :], axis=-1, keepdims=True) * (0.5 / D)
            sm_ref[4:5, :] = jnp.broadcast_to(total, (1, D))
            for k in range(NSHARD):
                dwo_ref[k] = dwo_s[k * QR:(k + 1) * QR, :].astype(BF16)
                dwg_ref[k] = dwg_s[k * QR:(k + 1) * QR, :].astype(BF16)
                dwp_ref[k] = dwp_s[k].astype(BF16)

    row = lambda w: pl.BlockSpec((tm, w), lambda i: (i, 0))
    const2 = lambda s: pl.BlockSpec(s, lambda i: (0, 0), pipeline_mode=pl.Buffered(1))
    const3 = lambda s: pl.BlockSpec(s, lambda i: (0, 0, 0), pipeline_mode=pl.Buffered(1))
    return pl.pallas_call(
        body, name="tail_fwd_bwd", grid=(nt,),
        in_specs=[row(D), row(PLE), row(D), row(DG), row(DG),
                  const2((D, D)), const2((D, D)), const3((NSHARD, PLE, PLE)),
                  const2((1, D)), const2((1, D)), const2((1, D))],
        out_specs=[row(D), row(D), const3((NSHARD, QR, D)), const3((NSHARD, QR, D)),
                   const3((NSHARD, PLE, PLE)), const2((8, D))],
        out_shape=[jax.ShapeDtypeStruct((T, D), F32), jax.ShapeDtypeStruct((T, D), F32),
                   jax.ShapeDtypeStruct((NSHARD, QR, D), BF16), jax.ShapeDtypeStruct((NSHARD, QR, D), BF16),
                   jax.ShapeDtypeStruct((NSHARD, PLE, PLE), BF16), jax.ShapeDtypeStruct((8, D), F32)],
        scratch_shapes=[pltpu.VMEM((D, D), F32), pltpu.VMEM((D, D), F32), pltpu.VMEM((NSHARD, PLE, PLE), F32)],
        compiler_params=_cparams(("arbitrary",), 62),
    )(x, p, tgt, pltpu.with_memory_space_constraint(ya, pltpu.HBM), pltpu.with_memory_space_constraint(yb, pltpu.HBM),
      w_out_b, w_pg_b, w_pp_b, ple_nw, b_pg, fnw)


def _bwd_rglru(pf, pb, h, dyab, cw3, conv_b, wa_c, ba, wx_c, bx, lam, dep):
    tm = 512
    nt = T // tm
    ng = tm // 8

    def body(xa_ref, ga_ref, xp_ref, h_ref, hp_ref, dya_ref, cw_ref, cb_ref, wa_ref, ba_ref, wx_ref, bx_ref,
             lam_ref, dep_ref, da_ref, rg_ref, sm_ref, a_s, g_s, cg_s, nxt_s, wa_s, wx_s, dwa_s, dwx_s):
        i = pl.program_id(0)
        tile = nt - 1 - i

        @pl.when(i == 0)
        def _():
            dwa_s[...] = jnp.zeros_like(dwa_s)
            dwx_s[...] = jnp.zeros_like(dwx_s)
            sm_ref[...] = jnp.zeros_like(sm_ref)
            cg_s[...] = jnp.zeros_like(cg_s)
            nxt_s[...] = jnp.zeros_like(nxt_s)
            wa_s[...] = _dense_from_blocks(wa_ref[...]).astype(BF16)
            wx_s[...] = _dense_from_blocks(wx_ref[...]).astype(BF16)

        rows = lax.broadcasted_iota(I32, (tm, DG), 0)
        has_prev = tile > 0
        xa = xa_ref[...]
        xprev = jnp.where(has_prev, xp_ref[...], 0.0)
        cw = _conv_rows(cw_ref)
        xc, taps = _rg_conv(xa, xprev, cw, cb_ref[...], rows)
        lam_v = lam_ref[...]
        sp = _softplus(-lam_v)
        first_row = (rows + tile * tm) == 0
        r, ig, a, a2, mult = _rg_gates(xc, wa_s[...], ba_ref[...], wx_s[...], bx_ref[...], sp, first_row)
        hv = h_ref[...]
        hprev = jnp.where(has_prev, hp_ref[...], 0.0)
        h_m1 = _shift_down(hv, hprev, 1, rows)
        ga = ga_ref[...].astype(F32)
        sg = _sigmoid(ga)
        dya = dya_ref[...]
        dga = dya * hv * (sg * (1.0 + ga * (1.0 - sg)))

        av = jnp.where(rows == tm - 1, 1.0, pltpu.roll(a, tm - 1, 0))
        gv = dya * (ga * sg)
        rows8 = rows & 7
        for d in (1, 2, 4):
            keep = rows8 < 8 - d
            gv = gv + av * jnp.where(keep, _roll_in_groups(gv, 8 - d), 0.0)
            av = av * jnp.where(keep, _roll_in_groups(av, 8 - d), 1.0)
        a_s[...] = av
        g_s[...] = gv
        carry = cg_s[0:1, :]
        for g in range(ng - 1, -1, -1):
            sl = slice(g * 8, (g + 1) * 8)
            ab, gb = a_s[sl, :], g_s[sl, :]
            g_s[sl, :] = gb + ab * carry
            carry = gb[0:1, :] + ab[0:1, :] * carry
        cg_s[0:1, :] = a[0:1, :] * carry

        gt = g_s[...]
        da = gt * h_m1
        ixc = ig * xc
        di = gt * mult * xc
        dxc = gt * mult * ig
        dlog_a = da * a + jnp.where(first_row, 0.0, gt * ixc * (-a2 / mult))
        sm_ref[3:4, :] += _rowsum(dlog_a * ((-RG_C) * r))
        dpr_f = dlog_a * ((-RG_C) * sp) * r * (1.0 - r)
        dpi_f = di * ig * (1.0 - ig)
        sm_ref[1:2, :] += _rowsum(dpr_f)
        sm_ref[2:3, :] += _rowsum(dpi_f)
        dpr = dpr_f.astype(BF16)
        dpi = dpi_f.astype(BF16)
        xcb = xc.astype(BF16)
        dwa_s[...] += _mm_tn(xcb, dpr)
        dwx_s[...] += _mm_tn(xcb, dpi)
        dxc = dxc + _mm_nt(dpr, wa_s[...]) + _mm_nt(dpi, wx_s[...])
        sm_ref[0:1, :] += _rowsum(dxc)
        for j in range(4):
            sm_ref[4 + j:5 + j, :] += _rowsum(dxc * taps[j])
        nxt = nxt_s[...]
        dxa = (dxc * cw[3] + _shift_up(dxc, nxt, 1, rows) * cw[2]
               + _shift_up(dxc, nxt, 2, rows) * cw[1] + _shift_up(dxc, nxt, 3, rows) * cw[0])
        nxt_s[...] = dxc[0:8, :]
        da_ref[:, 0:DG] = dxa.astype(BF16)
        da_ref[:, DG:D] = dga.astype(BF16)

        @pl.when(i == nt - 1)
        def _():
            _, s_neg = _sig_pair(lam_v)
            sm_ref[3:4, :] = sm_ref[3:4, :] * (-s_neg)
            rg_ref[...] = _blocks_from_dense(dwa_s[...], dwx_s[...])

    vec = lambda: pl.BlockSpec((1, DG), lambda i: (0, 0))
    blocks = lambda: pl.BlockSpec((DG, RGB), lambda i: (0, 0))
    prev8 = lambda: pl.BlockSpec((8, DG), lambda i: (jnp.maximum((nt - 1 - i) * (tm // 8) - 1, 0), 0))
    return pl.pallas_call(
        body, name="bwd_rglru", grid=(nt,),
        in_specs=[pl.BlockSpec((tm, DG), lambda i: (nt - 1 - i, 0)),
                  pl.BlockSpec((tm, DG), lambda i: (nt - 1 - i, 0)),
                  prev8(),
                  pl.BlockSpec((tm, DG), lambda i: (nt - 1 - i, 0)),
                  prev8(),
                  pl.BlockSpec((tm, DG), lambda i: (nt - 1 - i, 0)),
                  pl.BlockSpec((NSHARD, 4, 128), lambda i: (0, 0, 0)), vec(),
                  blocks(), vec(), blocks(), vec(), vec(), ANY_SPEC],
        out_specs=[pl.BlockSpec((tm, D), lambda i: (nt - 1 - i, 0)),
                   pl.BlockSpec((DG, 128), lambda i: (0, 0)),
                   pl.BlockSpec((8, DG), lambda i: (0, 0))],
        out_shape=[jax.ShapeDtypeStruct((T, D), BF16), jax.ShapeDtypeStruct((DG, 128), F32),
                   jax.ShapeDtypeStruct((8, DG), F32)],
        scratch_shapes=[pltpu.VMEM((tm, DG), F32), pltpu.VMEM((tm, DG), F32),
                        pltpu.VMEM((8, DG), F32), pltpu.VMEM((8, DG), F32),
                        pltpu.VMEM((DG, DG), BF16), pltpu.VMEM((DG, DG), BF16),
                        pltpu.VMEM((DG, DG), F32), pltpu.VMEM((DG, DG), F32)],
        compiler_params=_cparams(("arbitrary",), 56),
    )(pf, pb, pf, h, h, dyab, cw3, conv_b, wa_c, ba, wx_c, bx, lam, dep)


def _bwd_hgrn2(pf, pb, o, s_prev, dyab, hg_lb, hg_nw, dep):
    tm = 512
    nt = T // tm
    nc = tm // CH

    def body(q_ref, f_ref, v_ref, g_ref, o_ref, sp_ref, dy_ref, lb_ref, nw_ref, dep_ref, db_ref, sm_ref,
             dst_s, eb_s, enb_s, ekd_s, qe_s, ke_s, kd_s, vb_s, do_s, dec_s, ddec_s, p_s, dp_s,
             g_s, dsta_s, dva_s, dqe_s, dke_s, dkd_s, dlf_s):
        i = pl.program_id(0)

        @pl.when(i == 0)
        def _():
            sm_ref[...] = jnp.zeros_like(sm_ref)
            dst_s[...] = jnp.zeros_like(dst_s)

        lb, one_m_lb = _hg_lower_bound(lb_ref)
        rows64 = lax.broadcasted_iota(I32, (tm, DG), 0) & (CH - 1)
        sg, sn, f, k, q, sq, qs = _hg_recompute(
            q_ref, f_ref, lb, one_m_lb, rows64, eb_s, enb_s, ekd_s, qe_s, ke_s, kd_s, dec_s)
        vb_s[...] = v_ref[...]

        nw = nw_ref[...]
        for hd in range(NH):
            cols = slice(hd * HD, (hd + 1) * HD)
            g = g_ref[:, cols].astype(F32)
            sgg = _sigmoid(g)
            o = o_ref[:, cols]
            so = lax.rsqrt(jnp.mean(o * o, axis=-1, keepdims=True) + EPS)
            oh = o * so
            dyb = dy_ref[:, cols]
            db_ref[:, 3 * DG + hd * HD:3 * DG + (hd + 1) * HD] = (
                dyb * (oh * nw) * (sgg * (1.0 + g * (1.0 - sgg)))).astype(BF16)
            don = dyb * (g * sgg)
            sm_ref[1:2, 0:HD] += _rowsum(don * oh)
            gw = don * nw
            do_s[:, cols] = (so * (gw - oh * jnp.mean(gw * oh, axis=-1, keepdims=True))).astype(BF16)

        mask = _tri(True)
        items = [(c, hd, slice(c * CH, (c + 1) * CH), slice(hd * HD, (hd + 1) * HD))
                 for c in range(nc) for hd in range(NH)]
        for c, hd, rs, cols in items:
            p_s[c * NH + hd] = jnp.where(mask, _mm_nt(qe_s[rs, cols], ke_s[rs, cols]), 0.0).astype(BF16)
            dp_s[c * NH + hd] = jnp.where(mask, _mm_nt(do_s[rs, cols], vb_s[rs, cols]), 0.0).astype(BF16)
        for c, hd, rs, cols in items:
            n = c * NH + hd
            dva_s[rs, cols] = _mm_tn(p_s[n], do_s[rs, cols])
            dqe_s[rs, cols] = _mm(dp_s[n], ke_s[rs, cols])
            dke_s[rs, cols] = _mm_tn(dp_s[n], qe_s[rs, cols])
            g_s[n] = _mm_tn(do_s[rs, cols], qe_s[rs, cols])
        for c, hd, rs, cols in reversed(items):
            n = c * NH + hd
            dst = dst_s[hd]
            dsta_s[n] = dst
            dst_s[hd] = dst * dec_s[c:c + 1, cols] + g_s[n]
        for c, hd, rs, cols in items:
            n = c * NH + hd
            dst = dsta_s[n]
            st_prev = sp_ref[hd, c]
            dv = dva_s[rs, cols] + _mm_nt(kd_s[rs, cols], dst)
            db_ref[rs, 2 * DG + hd * HD:2 * DG + (hd + 1) * HD] = dv.astype(BF16)
            dqe_s[rs, cols] += _mm(do_s[rs, cols], st_prev)
            dkd_s[rs, cols] = _mm(vb_s[rs, cols], dst)
            ddec_s[c:c + 1, cols] = _rowsum(dst * st_prev)

        eb, enb, ekd = eb_s[...], enb_s[...], ekd_s[...]
        dqe, dke, dkd = dqe_s[...], dke_s[...], dkd_s[...]
        t_kd = dkd * (k * ekd)
        rc = _chunk_rev_cumsum(dqe * (qs * eb) - dke * (k * enb) - t_kd, rows64)
        for c in range(nc):
            rs = slice(c * CH, (c + 1) * CH)
            dbl = _rowsum(t_kd[rs]) + ddec_s[c:c + 1, :] * dec_s[c:c + 1, :]
            dlf_s[rs, :] = rc[rs] + dbl
        t = dlf_s[...] / f - (dke * enb + dkd * ekd)
        db_ref[:, DG:2 * DG] = (one_m_lb * sg * sn * t).astype(BF16)
        sm_ref[0:1, :] += _rowsum(sn * t)
        db_ref[:, 0:DG] = (dqe * eb * (sq * (1.0 + q * (1.0 - sq))) * (HD ** -0.5)).astype(BF16)

        @pl.when(i == nt - 1)
        def _():
            dsm = sm_ref[0:1, :] * (lb * one_m_lb)
            sm_ref[2:3, :] = dsm
            sm_ref[3:4, :] = -dsm

    col = lambda j: pl.BlockSpec((tm, DG), lambda i: (nt - 1 - i, j))
    big = lambda dt: pltpu.VMEM((tm, DG), dt)
    return pl.pallas_call(
        body, name="bwd_hgrn2", grid=(nt,),
        in_specs=[col(1), col(1), col(2), col(3),
                  pl.BlockSpec((tm, DG), lambda i: (nt - 1 - i, 0)),
                  pl.BlockSpec((NH, nc, HD, HD), lambda i: (0, nt - 1 - i, 0, 0)),
                  pl.BlockSpec((tm, DG), lambda i: (nt - 1 - i, 1)),
                  pl.BlockSpec((2, DG), lambda i: (0, 0)),
                  pl.BlockSpec((1, HD), lambda i: (0, 0)), ANY_SPEC],
        out_specs=[pl.BlockSpec((tm, 4 * DG), lambda i: (nt - 1 - i, 0)),
                   pl.BlockSpec((8, DG), lambda i: (0, 0))],
        out_shape=[jax.ShapeDtypeStruct((T, 4 * DG), BF16), jax.ShapeDtypeStruct((8, DG), F32)],
        scratch_shapes=[pltpu.VMEM((NH, HD, HD), F32),
                        big(F32), big(F32), big(F32),
                        big(BF16), big(BF16), big(BF16), big(BF16), big(BF16),
                        pltpu.VMEM((nc, DG), F32), pltpu.VMEM((nc, DG), F32),
                        pltpu.VMEM((nc * NH, CH, CH), BF16), pltpu.VMEM((nc * NH, CH, CH), BF16),
                        pltpu.VMEM((nc * NH, HD, HD), F32), pltpu.VMEM((nc * NH, HD, HD), F32),
                        big(F32), big(F32), big(F32), big(F32), big(F32)],
        compiler_params=_cparams(("arbitrary",), 56),
    )(pb, pf, pb, pb, o, s_prev, dyab, hg_lb, hg_nw, dep)


def _dproj_pieces(k, da_ref, db_ref):
    if k == 0:
        return [(da_ref[:, 0:SHW], 0)]
    if k == 1:
        return [(da_ref[:, SHW:D], 0), (db_ref[:, 0:DG], D - SHW)]
    if k == 2:
        return [(db_ref[:, DG:DG + SHW], 0)]
    return [(db_ref[:, DG + SHW:4 * DG], 0)]


def _bwd_inproj_dx(x, dh1, d_a, d_b, w_in_b, nw, dep):
    tm = 512
    nt = T // tm

    def body(x_ref, dh1_ref, da_ref, db_ref, w_ref, nw_ref, dep_ref, dx_ref, sm_ref):
        i = pl.program_id(0)

        @pl.when(i == 0)
        def _():
            sm_ref[...] = jnp.zeros_like(sm_ref)

        du = None
        for k in range(NSHARD):
            for val, off in _dproj_pieces(k, da_ref, db_ref):
                t = _mm_nt(val, w_ref[k, :, off:off + val.shape[1]])
                du = t if du is None else du + t
        xv = x_ref[...]
        s = lax.rsqrt(jnp.mean(xv * xv, axis=-1, keepdims=True) + EPS)
        xh = xv * s
        sm_ref[0:1, :] += _rowsum(du * xh)
        g = du * nw_ref[...]
        dx_ref[...] = dh1_ref[...] + s * (g - xh * jnp.mean(g * xh, axis=-1, keepdims=True))

    row = lambda w: pl.BlockSpec((tm, w), lambda i: (i, 0))
    return pl.pallas_call(
        body, name="bwd_inproj_dx", grid=(nt,),
        in_specs=[row(D), row(D), row(D), row(4 * DG),
                  pl.BlockSpec((NSHARD, D, SHW), lambda i: (0, 0, 0), pipeline_mode=pl.Buffered(1)),
                  pl.BlockSpec((1, D), lambda i: (0, 0)), ANY_SPEC],
        out_specs=[row(D), pl.BlockSpec((8, D), lambda i: (0, 0))],
        out_shape=[jax.ShapeDtypeStruct((T, D), F32), jax.ShapeDtypeStruct((8, D), F32)],
        compiler_params=_cparams(("arbitrary",), 56),
    )(x, dh1, d_a, d_b, w_in_b, nw, dep)


def _bwd_inproj_dw(name, ks, u_b, d_a, d_b):
    tm = 1024
    nt = T // tm
    nk = len(ks)
    H = D // 2
    step = 128
    need_a = 0 in ks or 1 in ks

    def body(*refs):
        if need_a:
            u_ref, da_ref, db_ref, dw_ref, acc, send_s, recv_s, ssem, rsem = refs
        else:
            u_ref, db_ref, dw_ref, acc, send_s, recv_s, ssem, rsem = refs
            da_ref = None
        i = pl.program_id(0)

        @pl.when(i == 0)
        def _():
            acc[...] = jnp.zeros_like(acc)

        u = u_ref[...]
        for j, k in enumerate(ks):
            for val, off in _dproj_pieces(k, da_ref, db_ref):
                acc[j, :, off:off + val.shape[1]] += _mm_tn(u, val)

        @pl.when(i == nt - 1)
        def _():
            x, y, c = lax.axis_index("x"), lax.axis_index("y"), lax.axis_index("c")
            sibling = (x, y, 1 - c)
            mine0 = pl.multiple_of(c * H, step)
            other0 = pl.multiple_of((1 - c) * H, step)
            copies = []
            for j in range(nk):
                for r0 in range(0, H, step):
                    send_s[j, r0:r0 + step, :] = acc[j, pl.ds(other0 + r0, step), :].astype(BF16)
                cp = _remote(send_s.at[j], recv_s.at[j], ssem.at[j], rsem.at[j], sibling)
                cp.start()
                copies.append(cp)
            for j in range(nk):
                copies[j].wait_recv()
                for r0 in range(0, H, step):
                    s = acc[j, pl.ds(mine0 + r0, step), :] + recv_s[j, r0:r0 + step, :].astype(F32)
                    dw_ref[j, r0:r0 + step, :] = s.astype(BF16)
            for cp in copies:
                cp.wait_send()

    row = lambda w: pl.BlockSpec((tm, w), lambda i: (i, 0))
    ins = [u_b] + ([d_a] if need_a else []) + [d_b]
    in_specs = [row(D)] + ([row(D)] if need_a else []) + [row(4 * DG)]
    return pl.pallas_call(
        body, name=name, grid=(nt,), in_specs=in_specs,
        out_specs=pl.BlockSpec((nk, H, SHW), lambda i: (0, 0, 0)),
        out_shape=jax.ShapeDtypeStruct((nk, H, SHW), BF16),
        scratch_shapes=[pltpu.VMEM((nk, D, SHW), F32), pltpu.VMEM((nk, H, SHW), BF16),
                        pltpu.VMEM((nk, H, SHW), BF16),
                        pltpu.SemaphoreType.DMA((nk,)), pltpu.SemaphoreType.DMA((nk,))],
        compiler_params=_cparams(("arbitrary",), 48),
    )(*ins)


_OUT_ORDER = ["norm_mix_w", "w_in", "conv_w", "conv_b", "rg_wa", "rg_ba", "rg_wx", "rg_bx", "rg_lambda", "hg_lb",
              "hg_norm_w", "w_out", "ple_norm_w", "w_ple_gate", "b_ple_gate", "w_ple_proj", "final_norm_w"]
_BIG = ["w_in", "w_out", "w_ple_gate", "w_ple_proj"]


def _small_view(name, a):
    if name in ("rg_wa", "rg_wx"):
        return a.reshape(DG, RGB)
    if name == "conv_w":
        return a.reshape(4, 128)
    if name == "final_norm_w":
        return a.reshape(1, D)
    return a


def _landing(rows, cols):
    return lax.empty((NDEV, rows, cols), BF16)


def kernel(x, p, norm_mix_w, w_in, conv_w, conv_b, rg_wa, rg_ba, rg_wx, rg_bx, rg_lambda, hg_lb, hg_norm_w, w_out, ple_norm_w, w_ple_gate, b_ple_gate, w_ple_proj, final_norm_w, loss_target, m_norm_mix_w, m_w_in, m_conv_w, m_conv_b, m_rg_wa, m_rg_ba, m_rg_wx, m_rg_bx, m_rg_lambda, m_hg_lb, m_hg_norm_w, m_w_out, m_ple_norm_w, m_w_ple_gate, m_b_ple_gate, m_w_ple_proj, m_final_norm_w, v_norm_mix_w, v_w_in, v_conv_w, v_conv_b, v_rg_wa, v_rg_ba, v_rg_wx, v_rg_bx, v_rg_lambda, v_hg_lb, v_hg_norm_w, v_w_out, v_ple_norm_w, v_w_ple_gate, v_b_ple_gate, v_w_ple_proj, v_final_norm_w):
    given = dict(locals())
    x2, p2, tgt = x[0], p[0, 0], loss_target[0]
    hbm = lambda a: pltpu.with_memory_space_constraint(a, pltpu.HBM)
    norm_mix_w, conv_b, rg_ba, rg_bx, rg_lambda, hg_lb, hg_norm_w, ple_norm_w, b_ple_gate = (
        hbm(a) for a in (norm_mix_w, conv_b, rg_ba, rg_bx, rg_lambda, hg_lb, hg_norm_w, ple_norm_w, b_ple_gate))
    wa_c, wx_c = hbm(_small_view("rg_wa", rg_wa)), hbm(_small_view("rg_wx", rg_wx))

    w_in_b, l_out, l_pg, l_pp, cw3, pf, pb, u_b = _gather_inproj(
        x2, norm_mix_w, w_in[0], w_out[0], w_ple_gate[0], w_ple_proj[0], conv_w[0])
    g_ssem, g_rsem, g_lands, tok = _gather_rest_start([l_out, l_pg, l_pp])

    h, ya = _fwd_rglru(pf, pb, cw3, conv_b, wa_c, rg_ba, wx_c, rg_bx, rg_lambda, tok)
    yb, o, s_prev = _fwd_hgrn2(pf, pb, hg_lb, hg_norm_w)
    w_out_b, w_pg_b, w_pp_b = _gather_rest_wait(g_ssem, g_rsem, g_lands, yb)
    dh1, dyab, dwo_b, dwg_b, dwp_b, sm_tail = _tail_fwd_bwd(
        x2, p2, tgt, ya, yb, w_out_b.reshape(D, D), w_pg_b.reshape(D, D), w_pp_b,
        ple_norm_w, b_ple_gate, final_norm_w.reshape(1, D))

    QH = D // NSHARD // 2
    r1 = _rs_start("rs_start_tail", [dwo_b, dwg_b, dwp_b], (0, 1, 2, 3),
                   [_landing(QH, D), _landing(QH, D), _landing(PLE // 2, PLE)])
    d_b, sm_b = _bwd_hgrn2(pf, pb, o, s_prev, dyab, hg_lb, hg_norm_w, r1[4])
    dw23 = _bwd_inproj_dw("bwd_inproj_dw23", (2, 3), u_b, None, d_b)
    r2 = _rs_start("rs_start_in23", [dw23], (2, 3), [lax.empty((NSHARD, D // 2, SHW), BF16)], chip_sums=True)
    d_a, rg_c, sm_a = _bwd_rglru(pf, pb, h, dyab, cw3, conv_b, wa_c, rg_ba, wx_c, rg_bx, rg_lambda, r2[4])
    dw01 = _bwd_inproj_dw("bwd_inproj_dw01", (0, 1), u_b, d_a, d_b)
    r3 = _rs_start("rs_start_in01", [dw01], (0, 1), r2[3], chip_sums=True)
    grad_x, sm_in = _bwd_inproj_dx(x2, dh1, d_a, d_b, w_in_b, norm_mix_w, r3[4])

    parts1, lands1 = _rs_wait("rs_wait_tail", r1[0], r1[1], r1[2], (0, 1, 2, 3), r1[3], sm_in)
    parts2, lands2 = _rs_wait("rs_wait_in23", r2[0], r2[1], r2[2], (2, 3), r3[3], sm_in, chip_sums=True)
    parts3, lands3 = _rs_wait("rs_wait_in01", r3[0], r3[1], r3[2], (0, 1), lands2, sm_in, chip_sums=True)
    g_big, vred, rgred = _final_reduce([parts3[0], parts2[0]] + parts1, lands3 + lands1,
                                       sm_in, sm_tail, sm_a, sm_b, rg_c)

    upd_big = _adam_big(g_big, [given[n][0] for n in _BIG], [given["m_" + n][0] for n in _BIG],
                        [given["v_" + n][0] for n in _BIG])
    small = _adam_small(vred, rgred,
                        {n: _small_view(n, given[n]) for n in _SMALL_ORDER},
                        {n: _small_view(n, given["m_" + n]) for n in _SMALL_ORDER},
                        {n: _small_view(n, given["v_" + n]) for n in _SMALL_ORDER})

    loss = vred[0, ROW_LOSS, 0]
    outs = [loss, grad_x[None]]
    for ki in range(4):
        for n in _OUT_ORDER:
            if n in _BIG:
                i = _BIG.index(n)
                a = g_big[i] if ki == 0 else upd_big[i][ki - 1]
                outs.append(a[None])
            else:
                outs.append(small[n][ki].reshape(given[n].shape))
    return tuple(outs)
```

```python
import jax
import jax.numpy as jnp
from jax import lax
from jax.experimental import pallas as pl
from jax.experimental.pallas import tpu as pltpu

F32 = jnp.float32
BF16 = jnp.bfloat16
I32 = jnp.int32
MESH = pl.DeviceIdType.MESH
HIGHEST = lax.Precision.HIGHEST

T = 4096
D = 1024
DG = 512
DIN = 3072
PLE = 256
NH = 4
HD = 128
CH = 64
NCHUNK = T // CH
RGB = 64
EPS = 1e-6
RG_C = 8.0
NSHARD = 4
SHW = DIN // NSHARD
NDEV = 8

ADAM_LR = 0.001
ADAM_B1 = 0.9
ADAM_B2 = 0.999
ADAM_EPS = 1e-08
ADAM_WD = 0.01
ADAM_STEP = 10

VMEM_SPEC = pl.BlockSpec(memory_space=pltpu.VMEM)
HBM_SPEC = pl.BlockSpec(memory_space=pltpu.HBM)
SEM_SPEC = pl.BlockSpec(memory_space=pltpu.SEMAPHORE)
ANY_SPEC = pl.BlockSpec(memory_space=pl.ANY)
EFFECT = pltpu.SideEffectType.DATAFLOW_SIDE_EFFECTING
MIB = 1024 * 1024

VROWS = 16
ROW_NORM_MIX, ROW_FINAL_NORM, ROW_B_PG, ROW_PLE_NORM = 0, 1, 2, 3
ROW_CB_BA, ROW_BX_LAM, ROW_CW01, ROW_CW23, ROW_HG_LB, ROW_HG_NW, ROW_LOSS = 4, 5, 6, 7, 8, 9, 10


def _mm(a, b):
    return jnp.dot(a.astype(BF16), b.astype(BF16), preferred_element_type=F32)


def _mm_nt(a, b):
    return lax.dot_general(a.astype(BF16), b.astype(BF16), (((1,), (1,)), ((), ())),
                           preferred_element_type=F32)


def _mm_tn(a, b):
    return lax.dot_general(a.astype(BF16), b.astype(BF16), (((0,), (0,)), ((), ())),
                           preferred_element_type=F32)


def _mm_exact(a, b):
    bb = b.astype(BF16)
    hi = a.astype(BF16)
    r1 = a - hi.astype(F32)
    mid = r1.astype(BF16)
    lo = (r1 - mid.astype(F32)).astype(BF16)
    return sum(jnp.dot(p, bb, preferred_element_type=F32) for p in (hi, mid, lo))


def _sig_pair(x):
    e = jnp.exp(-jnp.abs(x))
    big = 1.0 / (1.0 + e)
    small = e * big
    pos = x >= 0
    return jnp.where(pos, big, small), jnp.where(pos, small, big)


def _sigmoid(x):
    return 1.0 / (1.0 + jnp.exp(-x))


def _rowsum(v):
    return jnp.sum(v, axis=0, keepdims=True)


def _shift_down(cur, prev8, d, rows):
    rolled = pltpu.roll(cur, d, 0)
    head = jnp.where(rows[0:8] < d, pltpu.roll(prev8, d, 0), rolled[0:8])
    return jnp.concatenate([head, rolled[8:]], axis=0)


def _shift_up(cur, next8, d, rows):
    n = cur.shape[0]
    rolled = pltpu.roll(cur, n - d, 0)
    tail = jnp.where(rows[0:8] >= 8 - d, pltpu.roll(next8, 8 - d, 0), rolled[n - 8:n])
    return jnp.concatenate([rolled[0:n - 8], tail], axis=0)


def _roll_in_groups(v, d):
    n, w = v.shape
    return pltpu.roll(v.reshape(n // 8, 8, w), d, 1).reshape(n, w)


def _cparams(sem, vmem_mib):
    return pltpu.CompilerParams(dimension_semantics=sem, vmem_limit_bytes=vmem_mib * MIB)


def _mesh_pos():
    x, y, c = lax.axis_index("x"), lax.axis_index("y"), lax.axis_index("c")
    chips = [(1 - x, y), (x, 1 - y), (1 - x, 1 - y)]
    return x, y, c, chips


def _remote(src, dst, ssem, rsem, dev):
    return pltpu.make_async_remote_copy(src_ref=src, dst_ref=dst, send_sem=ssem, recv_sem=rsem,
                                        device_id=dev, device_id_type=MESH)


def _gather_w_in(w_in, w_out, w_pg, w_pp, conv_w):
    shapes = [w_in.shape, w_out.shape, w_pg.shape, w_pp.shape]

    def body(win_h, wout_h, wpg_h, wpp_h, cw_h, o_in, o_out, o_pg, o_pp, o_cw,
             win, wout, wpg, wpp, cw, lsem, ssem, rsem):
        x, y, c, chips = _mesh_pos()
        kme = 2 * x + y
        sibling = (x, y, 1 - c)
        fetch = [pltpu.make_async_copy(src, dst, lsem.at[i])
                 for i, (src, dst) in enumerate([(win_h, win), (cw_h, cw), (wout_h, wout), (wpg_h, wpg), (wpp_h, wpp)])]
        for cp in fetch:
            cp.start()

        def cast(src, dst):
            for r0 in range(0, src.shape[0], 128):
                dst[kme, r0:r0 + 128, :] = src[r0:r0 + 128, :].astype(BF16)

        fetch[0].wait()
        cast(win, o_in)
        fetch[1].wait()
        o_cw[kme] = cw[...]

        hrows = D // 2
        mine = pl.ds(pl.multiple_of(c * hrows, 128), hrows)
        other = pl.ds(pl.multiple_of((1 - c) * hrows, 128), hrows)

        def half(k, rows):
            return o_in.at[k, rows]

        na = (x ^ c, y ^ (1 - c), c)
        nb = (x ^ (1 - c), y ^ c, c)
        ka = 2 * na[0] + na[1]
        kb = 2 * nb[0] + nb[1]
        kd = 2 * (1 - x) + (1 - y)
        sends = [_remote(half(kme, mine), half(kme, mine), ssem.at[0], rsem.at[0], na),
                 _remote(half(kme, mine), half(kme, mine), ssem.at[1], rsem.at[1], nb)]
        for j, (px, py) in enumerate(chips):
            sends.append(_remote(o_cw.at[kme], o_cw.at[kme], ssem.at[3 + j], rsem.at[3 + j], (px, py, c)))
        for cp in sends:
            cp.start()
        for i, (src, dst) in enumerate([(wout, o_out), (wpg, o_pg), (wpp, o_pp)]):
            fetch[2 + i].wait()
            cast(src, dst)

        def passed_on(k, sem, dev):
            cp = _remote(half(k, mine), half(k, mine), ssem.at[sem], rsem.at[sem], dev)
            cp.start()
            sends.append(cp)

        _remote(half(ka, mine), half(ka, mine), ssem.at[0], rsem.at[0], na).wait_recv()
        passed_on(ka, 2, nb)
        passed_on(ka, 6, sibling)
        _remote(half(kb, mine), half(kb, mine), ssem.at[1], rsem.at[1], nb).wait_recv()
        passed_on(kb, 7, sibling)
        _remote(half(kd, mine), half(kd, mine), ssem.at[2], rsem.at[2], nb).wait_recv()
        passed_on(kd, 8, sibling)
        _remote(half(kb, other), half(kb, other), ssem.at[6], rsem.at[6], sibling).wait_recv()
        _remote(half(ka, other), half(ka, other), ssem.at[7], rsem.at[7], sibling).wait_recv()
        _remote(half(kd, other), half(kd, other), ssem.at[8], rsem.at[8], sibling).wait_recv()
        for j, (px, py) in enumerate(chips):
            kj = 2 * px + py
            _remote(o_cw.at[kj], o_cw.at[kj], ssem.at[3 + j], rsem.at[3 + j], (px, py, c)).wait_recv()
        for cp in sends:
            cp.wait_send()

    out_shape = [jax.ShapeDtypeStruct((NSHARD,) + s, BF16) for s in shapes]
    out_shape.append(jax.ShapeDtypeStruct((NSHARD,) + conv_w.shape, F32))
    return pl.pallas_call(
        body, name="gather_w_in", out_shape=out_shape,
        in_specs=[HBM_SPEC] * 5, out_specs=[VMEM_SPEC] * 5,
        scratch_shapes=[pltpu.VMEM(a.shape, F32) for a in (w_in, w_out, w_pg, w_pp, conv_w)]
        + [pltpu.SemaphoreType.DMA((5,)), pltpu.SemaphoreType.DMA((9,)), pltpu.SemaphoreType.DMA((9,))],
        compiler_params=pltpu.CompilerParams(vmem_limit_bytes=40 * MIB),
    )(*[pltpu.with_memory_space_constraint(a, pltpu.HBM) for a in (w_in, w_out, w_pg, w_pp, conv_w)])


def _gather_inproj(x, nw, w_in, w_out, w_pg, w_pp, conv_w):
    shapes = [w_in.shape, w_out.shape, w_pg.shape, w_pp.shape]
    tm = 512
    nt = T // tm

    def body(x_h, nw_h, win_h, wout_h, wpg_h, wpp_h, cw_h,
             o_in_h, o_out_h, o_pg_h, o_pp_h, o_cw_h, pf_h, pb_h, u_h,
             win, wout, wpg, wpp, cw, nw_s, xbuf, ubuf, rfbuf, rbbuf, o_in, o_out, o_pg, o_pp, o_cw,
             lsem, xsem, usem, fsem, bsem, osem, ssem, rsem):
        x, y, c, chips = _mesh_pos()
        kme = 2 * x + y
        sibling = (x, y, 1 - c)
        fetch = [pltpu.make_async_copy(src, dst, lsem.at[i]) for i, (src, dst) in enumerate(
            [(win_h, win), (cw_h, cw), (nw_h, nw_s), (wout_h, wout), (wpg_h, wpg), (wpp_h, wpp)])]
        for cp in fetch:
            cp.start()

        def cast(src, dst):
            for r0 in range(0, src.shape[0], 128):
                dst[r0:r0 + 128, :] = src[r0:r0 + 128, :].astype(BF16)

        outs = []

        def write_out(src, dst, i):
            cp = pltpu.make_async_copy(src, dst, osem.at[i])
            cp.start()
            outs.append(cp)

        fetch[0].wait()
        cast(win, o_in.at[kme])
        write_out(o_in.at[kme], o_in_h.at[kme], 0)
        fetch[1].wait()
        o_cw[kme] = cw[...]

        hrows = D // 2
        mine = pl.ds(pl.multiple_of(c * hrows, 128), hrows)
        other = pl.ds(pl.multiple_of((1 - c) * hrows, 128), hrows)

        def half(k, rows):
            return o_in.at[k, rows]

        na = (x ^ c, y ^ (1 - c), c)
        nb = (x ^ (1 - c), y ^ c, c)
        ka = 2 * na[0] + na[1]
        kb = 2 * nb[0] + nb[1]
        kd = 2 * (1 - x) + (1 - y)
        sends = [_remote(half(kme, mine), half(kme, mine), ssem.at[0], rsem.at[0], na),
                 _remote(half(kme, mine), half(kme, mine), ssem.at[1], rsem.at[1], nb)]
        for j, (px, py) in enumerate(chips):
            sends.append(_remote(o_cw.at[kme], o_cw.at[kme], ssem.at[3 + j], rsem.at[3 + j], (px, py, c)))
        for cp in sends:
            cp.start()
        fetch[2].wait()

        def rows_of(t):
            return pl.ds(pl.multiple_of(t * tm, tm), tm)

        def x_copy(t, slot):
            return pltpu.make_async_copy(x_h.at[rows_of(t)], xbuf.at[slot], xsem.at[slot])

        u_out = pltpu.make_async_copy(ubuf, u_h, usem.at[0])

        def f_store(t, slot, kh):
            return pltpu.make_async_copy(rfbuf.at[slot], pf_h.at[rows_of(t), pl.ds(pl.multiple_of(kh * DG, DG), DG)],
                                         fsem.at[slot])

        def b_store_even(t, slot, kh):
            return pltpu.make_async_copy(rbbuf.at[slot, :, 0:256],
                                         pb_h.at[rows_of(t), pl.ds(pl.multiple_of(kh * 2 * DG, 256), 256)],
                                         bsem.at[slot])

        def b_store_odd(t, slot, kh):
            return pltpu.make_async_copy(rbbuf.at[slot],
                                         pb_h.at[rows_of(t), pl.ds(pl.multiple_of(256 + kh * 2 * DG, 256), SHW)],
                                         bsem.at[slot])

        def run_pass(k, first, even):
            kh = k >> 1
            if first:
                x_copy(0, 0).start()

            def tile(t, carry):
                slot = t & 1
                if first:
                    @pl.when(t + 1 < nt)
                    def _():
                        x_copy(t + 1, 1 - slot).start()

                    x_copy(t, slot).wait()

                @pl.when(t >= 2)
                def _():
                    if even:
                        f_store(t - 2, slot, kh).wait()
                        b_store_even(t - 2, slot, kh).wait()
                    else:
                        b_store_odd(t - 2, slot, kh).wait()

                if first:
                    xv = xbuf[slot]
                    s = lax.rsqrt(jnp.mean(xv * xv, axis=-1, keepdims=True) + EPS)
                    u = (xv * s * nw_s[...]).astype(BF16)
                    ubuf[rows_of(t), :] = u
                else:
                    u = ubuf[rows_of(t), :]
                r = jnp.dot(u, o_in[k], preferred_element_type=F32)
                if even:
                    rfbuf[slot] = r[:, 0:DG]
                    rbbuf[slot, :, 0:256] = r[:, DG:SHW].astype(BF16)
                    f_store(t, slot, kh).start()
                    b_store_even(t, slot, kh).start()
                else:
                    rbbuf[slot] = r.astype(BF16)
                    b_store_odd(t, slot, kh).start()
                return carry

            lax.fori_loop(0, nt, tile, 0)
            for t in (nt - 2, nt - 1):
                slot = t & 1
                if even:
                    f_store(t, slot, kh).wait()
                    b_store_even(t, slot, kh).wait()
                else:
                    b_store_odd(t, slot, kh).wait()

        def project(k, first=False):
            @pl.when((k & 1) == 0)
            def _():
                run_pass(k, first, True)

            @pl.when((k & 1) == 1)
            def _():
                run_pass(k, first, False)

        def passed_on(k, sem, dev):
            cp = _remote(half(k, mine), half(k, mine), ssem.at[sem], rsem.at[sem], dev)
            cp.start()
            sends.append(cp)

        project(kme, first=True)
        u_out.start()
        for i, (src, dst, dst_h) in enumerate([(wout, o_out, o_out_h), (wpg, o_pg, o_pg_h), (wpp, o_pp, o_pp_h)]):
            fetch[3 + i].wait()
            cast(src, dst)
            write_out(dst, dst_h.at[kme], 4 + i)
        _remote(half(ka, mine), half(ka, mine), ssem.at[0], rsem.at[0], na).wait_recv()
        passed_on(ka, 2, nb)
        passed_on(ka, 6, sibling)
        _remote(half(kb, mine), half(kb, mine), ssem.at[1], rsem.at[1], nb).wait_recv()
        passed_on(kb, 7, sibling)
        _remote(half(ka, other), half(ka, other), ssem.at[7], rsem.at[7], sibling).wait_recv()
        write_out(o_in.at[ka], o_in_h.at[ka], 1)
        project(ka)
        _remote(half(kd, mine), half(kd, mine), ssem.at[2], rsem.at[2], nb).wait_recv()
        passed_on(kd, 8, sibling)
        _remote(half(kb, other), half(kb, other), ssem.at[6], rsem.at[6], sibling).wait_recv()
        write_out(o_in.at[kb], o_in_h.at[kb], 2)
        project(kb)
        _remote(half(kd, other), half(kd, other), ssem.at[8], rsem.at[8], sibling).wait_recv()
        write_out(o_in.at[kd], o_in_h.at[kd], 3)
        project(kd)
        u_out.wait()
        for j, (px, py) in enumerate(chips):
            kj = 2 * px + py
            _remote(o_cw.at[kj], o_cw.at[kj], ssem.at[3 + j], rsem.at[3 + j], (px, py, c)).wait_recv()
        write_out(o_cw, o_cw_h, 7)
        for cp in outs:
            cp.wait()
        for cp in sends:
            cp.wait_send()

    out_shape = [pltpu.HBM((NSHARD,) + s, BF16) for s in shapes]
    out_shape.append(pltpu.HBM((NSHARD,) + conv_w.shape, F32))
    out_shape += [pltpu.HBM((T, 2 * DG), F32), pltpu.HBM((T, 4 * DG), BF16), pltpu.HBM((T, D), BF16)]
    hbm_args = [pltpu.with_memory_space_constraint(a, pltpu.HBM) for a in (x, nw, w_in, w_out, w_pg, w_pp, conv_w)]
    return pl.pallas_call(
        body, name="gather_inproj", out_shape=out_shape,
        in_specs=[HBM_SPEC] * 7, out_specs=[HBM_SPEC] * 8,
        scratch_shapes=[pltpu.VMEM(a.shape, F32) for a in (w_in, w_out, w_pg, w_pp, conv_w, nw)]
        + [pltpu.VMEM((2, tm, D), F32), pltpu.VMEM((T, D), BF16),
           pltpu.VMEM((2, tm, DG), F32), pltpu.VMEM((2, tm, SHW), BF16),
           pltpu.VMEM((NSHARD,) + w_in.shape, BF16), pltpu.VMEM(w_out.shape, BF16), pltpu.VMEM(w_pg.shape, BF16),
           pltpu.VMEM(w_pp.shape, BF16), pltpu.VMEM((NSHARD,) + conv_w.shape, F32),
           pltpu.SemaphoreType.DMA((6,)), pltpu.SemaphoreType.DMA((2,)), pltpu.SemaphoreType.DMA((2,)),
           pltpu.SemaphoreType.DMA((2,)), pltpu.SemaphoreType.DMA((2,)), pltpu.SemaphoreType.DMA((8,)),
           pltpu.SemaphoreType.DMA((9,)), pltpu.SemaphoreType.DMA((9,))],
        compiler_params=pltpu.CompilerParams(vmem_limit_bytes=48 * MIB),
    )(*hbm_args)


def _gather_rest_start(lands):
    n = len(lands)

    def body(*refs):
        land_in = refs[0:n]
        ssem, rsem = refs[n], refs[n + 1]
        token = refs[2 * n + 2]
        x, y, c, chips = _mesh_pos()
        kme = 2 * x + y
        for p, land in enumerate(land_in):
            hrows = land.shape[1] // 2
            mine = pl.ds(pl.multiple_of(c * hrows, 128), hrows)
            for px, py in chips:
                for pc in range(2):
                    _remote(land.at[kme, mine], land.at[kme, mine], ssem.at[p], rsem.at[p], (px, py, pc)).start()
        token[...] = jnp.zeros_like(token)

    out_shape = ([pltpu.SemaphoreType.DMA((n,)), pltpu.SemaphoreType.DMA((n,))]
                 + [pltpu.HBM(a.shape, a.dtype) for a in lands] + [jax.ShapeDtypeStruct((8, 128), F32)])
    outs = pl.pallas_call(
        body, name="gather_rest_start", out_shape=out_shape,
        in_specs=[HBM_SPEC] * n, out_specs=[SEM_SPEC, SEM_SPEC] + [HBM_SPEC] * n + [VMEM_SPEC],
        input_output_aliases={i: 2 + i for i in range(n)},
        compiler_params=pltpu.CompilerParams(has_side_effects=EFFECT),
    )(*[pltpu.with_memory_space_constraint(a, pltpu.HBM) for a in lands])
    return outs[0], outs[1], list(outs[2:2 + n]), outs[2 + n]


def _gather_rest_wait(ssem, rsem, lands, after):
    n = len(lands)

    def body(*refs):
        land_in = refs[0:n]
        ssem_ref, rsem_ref = refs[n], refs[n + 1]
        x, y, c = lax.axis_index("x"), lax.axis_index("y"), lax.axis_index("c")
        for p, land in enumerate(land_in):
            three = land.at[pl.ds(0, 3)]
            cp = _remote(three, three, ssem_ref.at[p], rsem_ref.at[p], (x, y, c))
            cp.wait_send()
            cp.wait_recv()

    outs = pl.pallas_call(
        body, name="gather_rest_wait", out_shape=[pltpu.HBM(a.shape, a.dtype) for a in lands],
        in_specs=[HBM_SPEC] * n + [SEM_SPEC, SEM_SPEC, ANY_SPEC], out_specs=[HBM_SPEC] * n,
        input_output_aliases={i: i for i in range(n)},
        compiler_params=pltpu.CompilerParams(has_side_effects=EFFECT),
    )(*lands, ssem, rsem, after)
    return list(outs)


def _rs_start(name, parts, ks, lands, chip_sums=False):
    n = len(parts)

    def body(*refs):
        part_in, land_in = refs[0:n], refs[n:2 * n]
        ssem, rsem = refs[2 * n], refs[2 * n + 1]
        token = refs[4 * n + 2]
        x, y, c = lax.axis_index("x"), lax.axis_index("y"), lax.axis_index("c")
        kme = 2 * x + y
        me = 4 * x + 2 * y + c
        for p in range(n):
            hrows = land_in[p].shape[1]
            for i, k in enumerate(ks):
                if chip_sums:
                    @pl.when(kme != k)
                    def _():
                        _remote(part_in[p].at[i], land_in[p].at[kme], ssem.at[p], rsem.at[p], (k // 2, k % 2, c)).start()
                    continue
                for pc in range(2):
                    @pl.when(jnp.logical_or(kme != k, c != pc))
                    def _():
                        _remote(part_in[p].at[i, pl.ds(pc * hrows, hrows)], land_in[p].at[me],
                                ssem.at[p], rsem.at[p], (k // 2, k % 2, pc)).start()
        token[...] = jnp.zeros_like(token)

    arrays = list(parts) + list(lands)
    out_shape = ([pltpu.SemaphoreType.DMA((n,)), pltpu.SemaphoreType.DMA((n,))]
                 + [pltpu.HBM(a.shape, a.dtype) for a in arrays] + [jax.ShapeDtypeStruct((8, 128), F32)])
    outs = pl.pallas_call(
        body, name=name, out_shape=out_shape,
        in_specs=[HBM_SPEC] * (2 * n), out_specs=[SEM_SPEC, SEM_SPEC] + [HBM_SPEC] * (2 * n) + [VMEM_SPEC],
        input_output_aliases={i: 2 + i for i in range(2 * n)},
        compiler_params=pltpu.CompilerParams(has_side_effects=EFFECT),
    )(*[pltpu.with_memory_space_constraint(a, pltpu.HBM) for a in arrays])
    return outs[0], outs[1], list(outs[2:2 + n]), list(outs[2 + n:2 + 2 * n]), outs[2 + 2 * n]


def _rs_wait(name, ssem, rsem, parts, ks, lands, after, chip_sums=False):
    n = len(parts)

    def body(*refs):
        part_in, land_in = refs[0:n], refs[n:2 * n]
        ssem_ref, rsem_ref = refs[2 * n], refs[2 * n + 1]
        x, y, c = lax.axis_index("x"), lax.axis_index("y"), lax.axis_index("c")
        kme = 2 * x + y
        for p in range(n):
            piece = land_in[p].at[0]
            for k in ks:
                for pc in range(1 if chip_sums else 2):
                    mine = (kme == k) if chip_sums else jnp.logical_and(kme == k, c == pc)

                    @pl.when(jnp.logical_not(mine))
                    def _():
                        _remote(piece, piece, ssem_ref.at[p], rsem_ref.at[p], (x, y, c)).wait_send()
            owner = kme == ks[0]
            for k in ks[1:]:
                owner = jnp.logical_or(owner, kme == k)

            @pl.when(owner)
            def _():
                others = land_in[p].at[pl.ds(0, land_in[p].shape[0] - 1)]
                _remote(others, others, ssem_ref.at[p], rsem_ref.at[p], (x, y, c)).wait_recv()

    arrays = list(parts) + list(lands)
    outs = pl.pallas_call(
        body, name=name, out_shape=[pltpu.HBM(a.shape, a.dtype) for a in arrays],
        in_specs=[HBM_SPEC] * (2 * n) + [SEM_SPEC, SEM_SPEC, ANY_SPEC], out_specs=[HBM_SPEC] * (2 * n),
        input_output_aliases={i: i for i in range(2 * n)},
        compiler_params=pltpu.CompilerParams(has_side_effects=EFFECT),
    )(*arrays, ssem, rsem, after)
    return list(outs[0:n]), list(outs[n:2 * n])


def _reduce_exchange(parts, lands):
    shapes = [(2 * l.shape[1], l.shape[2]) for l in lands]
    step = 128

    def body(in01, in23, pout, ppg, ppp, l_in, l_out, l_pg, l_pp, g_in, g_out, g_pg, g_pp, ssem, rsem):
        x, y, c = lax.axis_index("x"), lax.axis_index("y"), lax.axis_index("c")
        kme = 2 * x + y
        me = 4 * x + 2 * y + c
        sibling = (x, y, 1 - c)
        sends = []
        for p, (land, gout) in enumerate([(l_in, g_in), (l_out, g_out), (l_pg, g_pg), (l_pp, g_pp)]):
            hrows = land.shape[1]
            mine0 = pl.multiple_of(c * hrows, step)
            for r0 in range(0, hrows, step):
                rs = pl.ds(mine0 + r0, step)
                if p == 0:
                    own = jnp.where(kme >= 2, in23[kme & 1, r0:r0 + step, :], in01[kme & 1, r0:r0 + step, :])
                    slot = kme
                else:
                    own = (pout, ppg, ppp)[p - 1][kme, rs, :]
                    slot = me
                s = jnp.zeros((step, land.shape[2]), F32)
                for j in range(land.shape[0]):
                    s = s + jnp.where(slot == j, own, land[j, r0:r0 + step, :]).astype(F32)
                gout[rs, :] = s
            mine = pl.ds(mine0, hrows)
            cp = _remote(gout.at[mine], gout.at[mine], ssem.at[p], rsem.at[p], sibling)
            cp.start()
            sends.append(cp)
        for p, (land, gout) in enumerate([(l_in, g_in), (l_out, g_out), (l_pg, g_pg), (l_pp, g_pp)]):
            hrows = land.shape[1]
            other = pl.ds(pl.multiple_of((1 - c) * hrows, step), hrows)
            _remote(gout.at[other], gout.at[other], ssem.at[p], rsem.at[p], sibling).wait_recv()
        for cp in sends:
            cp.wait_send()

    return pl.pallas_call(
        body, name="reduce_exchange", out_shape=[jax.ShapeDtypeStruct(s, F32) for s in shapes],
        in_specs=[VMEM_SPEC] * 9, out_specs=[VMEM_SPEC] * 4,
        scratch_shapes=[pltpu.SemaphoreType.DMA((4,)), pltpu.SemaphoreType.DMA((4,))],
        compiler_params=pltpu.CompilerParams(vmem_limit_bytes=48 * MIB),
    )(*parts, *lands)


def _small_allreduce(sm_in, sm_tail, sm_a, sm_b, rg_c):
    PR = DG // NDEV

    def body(in_ref, tail_ref, a_ref, b_ref, rg_ref, v_out, rg_out, vbuf, vrecv, rgrecv, ssem, rsem):
        x, y, c = lax.axis_index("x"), lax.axis_index("y"), lax.axis_index("c")
        me = 4 * x + 2 * y + c

        def pair(ref, r0, r1):
            return jnp.concatenate([ref[r0:r0 + 1, :], ref[r1:r1 + 1, :]], axis=1)

        rows = {
            ROW_NORM_MIX: in_ref[0:1, :], ROW_FINAL_NORM: tail_ref[1:2, :], ROW_B_PG: tail_ref[2:3, :],
            ROW_PLE_NORM: tail_ref[3:4, :], ROW_CB_BA: pair(a_ref, 0, 1), ROW_BX_LAM: pair(a_ref, 2, 3),
            ROW_CW01: pair(a_ref, 4, 5), ROW_CW23: pair(a_ref, 6, 7), ROW_HG_LB: pair(b_ref, 2, 3),
            ROW_HG_NW: jnp.concatenate([b_ref[1:2, :], jnp.zeros((1, DG), F32)], axis=1),
            ROW_LOSS: tail_ref[4:5, :],
        }
        vbuf[...] = jnp.zeros_like(vbuf)
        for r, row in rows.items():
            for j in range(NDEV):
                vbuf[j, r:r + 1, :] = row[:, j * 128:(j + 1) * 128]

        def peer(mask):
            px = x ^ ((mask >> 2) & 1)
            py = y ^ ((mask >> 1) & 1)
            pc = c ^ (mask & 1)
            return (px, py, pc), 4 * px + 2 * py + pc

        def rg_rows(r):
            return pl.ds(pl.multiple_of(r * PR, PR), PR)

        first = []
        for mask in range(1, NDEV):
            dev, r = peer(mask)
            i = mask - 1
            cp = _remote(vbuf.at[r], vrecv.at[i], ssem.at[i], rsem.at[i], dev)
            cp.start()
            first.append(cp)
            cp = _remote(rg_ref.at[rg_rows(r)], rgrecv.at[i], ssem.at[7 + i], rsem.at[7 + i], dev)
            cp.start()
            first.append(cp)
        sv = vbuf[me]
        sr = rg_ref[rg_rows(me), :]
        for i in range(NDEV - 1):
            first[2 * i].wait_recv()
            first[2 * i + 1].wait_recv()
            sv = sv + vrecv[i]
            sr = sr + rgrecv[i]
        v_out[me] = sv
        rg_out[rg_rows(me), :] = sr
        second = []
        for mask in range(1, NDEV):
            dev, r = peer(mask)
            i = mask - 1
            cp = _remote(v_out.at[me], v_out.at[me], ssem.at[14 + i], rsem.at[14 + i], dev)
            cp.start()
            second.append(cp)
            cp = _remote(rg_out.at[rg_rows(me)], rg_out.at[rg_rows(me)], ssem.at[21 + i], rsem.at[21 + i], dev)
            cp.start()
            second.append(cp)
        for mask in range(1, NDEV):
            dev, r = peer(mask)
            i = mask - 1
            _remote(v_out.at[r], v_out.at[r], ssem.at[14 + i], rsem.at[14 + i], dev).wait_recv()
            _remote(rg_out.at[rg_rows(r)], rg_out.at[rg_rows(r)], ssem.at[21 + i], rsem.at[21 + i], dev).wait_recv()
        for cp in first + second:
            cp.wait_send()

    return pl.pallas_call(
        body, name="small_allreduce",
        out_shape=[jax.ShapeDtypeStruct((NDEV, VROWS, 128), F32), jax.ShapeDtypeStruct((DG, 128), F32)],
        in_specs=[VMEM_SPEC] * 5, out_specs=[VMEM_SPEC] * 2,
        scratch_shapes=[pltpu.VMEM((NDEV, VROWS, 128), F32), pltpu.VMEM((NDEV - 1, VROWS, 128), F32),
                        pltpu.VMEM((NDEV - 1, PR, 128), F32),
                        pltpu.SemaphoreType.DMA((28,)), pltpu.SemaphoreType.DMA((28,))],
    )(sm_in, sm_tail, sm_a, sm_b, rg_c)


def _final_reduce(parts, lands, sm_in, sm_tail, sm_a, sm_b, rg_c):
    shapes = [(2 * l.shape[1], l.shape[2]) for l in lands]
    step = 128
    PR = DG // NDEV

    def body(pin, pout, ppg, ppp, l_in, l_out, l_pg, l_pp, in_h, tail_h, a_h, b_h, rg_h,
             g_in, g_out, g_pg, g_pp, v_out, rg_out, vbuf, vrecv, rgrecv,
             lb_in, lb_out, lb_pg, lb_pp, ob_in, ob_out, ob_pg, ob_pp,
             in_ref, tail_ref, a_ref, b_ref, rg_ref, lsem, ssem_l, bsem_s, bsem_r, ssem, rsem):
        x, y, c = lax.axis_index("x"), lax.axis_index("y"), lax.axis_index("c")
        kme = 2 * x + y
        me = 4 * x + 2 * y + c
        sibling = (x, y, 1 - c)
        small_fetch = [pltpu.make_async_copy(src, dst, ssem_l.at[i]) for i, (src, dst) in enumerate(
            [(in_h, in_ref), (tail_h, tail_ref), (a_h, a_ref), (b_h, b_ref), (rg_h, rg_ref)])]
        for cp in small_fetch:
            cp.start()

        lands_hbm = [l_in, l_out, l_pg, l_pp]
        land_bufs = [lb_in, lb_out, lb_pg, lb_pp]
        own_bufs = [ob_in, ob_out, ob_pg, ob_pp]
        fetches = []
        for p in range(4):
            cp = pltpu.make_async_copy(lands_hbm[p], land_bufs[p], lsem.at[p])
            cp.start()
            fetches.append(cp)

        own_fetches = [pltpu.make_async_copy(pin.at[kme], ob_in, lsem.at[4])]
        own_fetches[0].start()
        for p, part in enumerate([pout, ppg, ppp]):
            hrows = own_bufs[p + 1].shape[0]
            cp = pltpu.make_async_copy(part.at[kme, pl.ds(pl.multiple_of(c * hrows, step), hrows)],
                                       own_bufs[p + 1], lsem.at[5 + p])
            cp.start()
            own_fetches.append(cp)

        def pair(ref, r0, r1):
            return jnp.concatenate([ref[r0:r0 + 1, :], ref[r1:r1 + 1, :]], axis=1)

        for cp in small_fetch:
            cp.wait()
        rows = {
            ROW_NORM_MIX: in_ref[0:1, :], ROW_FINAL_NORM: tail_ref[1:2, :], ROW_B_PG: tail_ref[2:3, :],
            ROW_PLE_NORM: tail_ref[3:4, :], ROW_CB_BA: pair(a_ref, 0, 1), ROW_BX_LAM: pair(a_ref, 2, 3),
            ROW_CW01: pair(a_ref, 4, 5), ROW_CW23: pair(a_ref, 6, 7), ROW_HG_LB: pair(b_ref, 2, 3),
            ROW_HG_NW: jnp.concatenate([b_ref[1:2, :], jnp.zeros((1, DG), F32)], axis=1),
            ROW_LOSS: tail_ref[4:5, :],
        }
        vbuf[...] = jnp.zeros_like(vbuf)
        for r, row in rows.items():
            for j in range(NDEV):
                vbuf[j, r:r + 1, :] = row[:, j * 128:(j + 1) * 128]

        def peer(mask):
            px = x ^ ((mask >> 2) & 1)
            py = y ^ ((mask >> 1) & 1)
            pc = c ^ (mask & 1)
            return (px, py, pc), 4 * px + 2 * py + pc

        def rg_rows(r):
            return pl.ds(pl.multiple_of(r * PR, PR), PR)

        first = []
        for mask in range(1, NDEV):
            dev, r = peer(mask)
            i = mask - 1
            cp = _remote(vbuf.at[r], vrecv.at[i], ssem.at[i], rsem.at[i], dev)
            cp.start()
            first.append(cp)
            cp = _remote(rg_ref.at[rg_rows(r)], rgrecv.at[i], ssem.at[7 + i], rsem.at[7 + i], dev)
            cp.start()
            first.append(cp)

        big = [(lb_in, g_in), (lb_out, g_out), (lb_pg, g_pg), (lb_pp, g_pp)]
        swaps = []
        for p, (land, gout) in enumerate(big):
            fetches[p].wait()
            own_fetches[p].wait()
            hrows = land.shape[1]
            mine0 = pl.multiple_of(c * hrows, step)
            slot = kme if p == 0 else me
            for r0 in range(0, hrows, step):
                rs = pl.ds(mine0 + r0, step)
                own = own_bufs[p][r0:r0 + step, :]
                s = jnp.zeros((step, land.shape[2]), F32)
                for j in range(land.shape[0]):
                    s = s + jnp.where(slot == j, own, land[j, r0:r0 + step, :]).astype(F32)
                gout[rs, :] = s
            mine = pl.ds(mine0, hrows)
            cp = _remote(gout.at[mine], gout.at[mine], bsem_s.at[p], bsem_r.at[p], sibling)
            cp.start()
            swaps.append(cp)

        sv = vbuf[me]
        sr = rg_ref[rg_rows(me), :]
        for i in range(NDEV - 1):
            first[2 * i].wait_recv()
            first[2 * i + 1].wait_recv()
            sv = sv + vrecv[i]
            sr = sr + rgrecv[i]
        v_out[me] = sv
        rg_out[rg_rows(me), :] = sr
        second = []
        for mask in range(1, NDEV):
            dev, r = peer(mask)
            i = mask - 1
            cp = _remote(v_out.at[me], v_out.at[me], ssem.at[14 + i], rsem.at[14 + i], dev)
            cp.start()
            second.append(cp)
            cp = _remote(rg_out.at[rg_rows(me)], rg_out.at[rg_rows(me)], ssem.at[21 + i], rsem.at[21 + i], dev)
            cp.start()
            second.append(cp)
        for p, (land, gout) in enumerate(big):
            hrows = land.shape[1]
            other = pl.ds(pl.multiple_of((1 - c) * hrows, step), hrows)
            _remote(gout.at[other], gout.at[other], bsem_s.at[p], bsem_r.at[p], sibling).wait_recv()
        for mask in range(1, NDEV):
            dev, r = peer(mask)
            i = mask - 1
            _remote(v_out.at[r], v_out.at[r], ssem.at[14 + i], rsem.at[14 + i], dev).wait_recv()
            _remote(rg_out.at[rg_rows(r)], rg_out.at[rg_rows(r)], ssem.at[21 + i], rsem.at[21 + i], dev).wait_recv()
        for cp in first + swaps + second:
            cp.wait_send()

    out_shape = [jax.ShapeDtypeStruct(s, F32) for s in shapes]
    out_shape += [jax.ShapeDtypeStruct((NDEV, VROWS, 128), F32), jax.ShapeDtypeStruct((DG, 128), F32)]
    outs = pl.pallas_call(
        body, name="final_reduce", out_shape=out_shape,
        in_specs=[HBM_SPEC] * 13, out_specs=[VMEM_SPEC] * 6,
        scratch_shapes=[pltpu.VMEM((NDEV, VROWS, 128), F32), pltpu.VMEM((NDEV - 1, VROWS, 128), F32),
                        pltpu.VMEM((NDEV - 1, PR, 128), F32)]
        + [pltpu.VMEM(l.shape, BF16) for l in lands]
        + [pltpu.VMEM(l.shape[1:], BF16) for l in lands]
        + [pltpu.VMEM(a.shape, F32) for a in (sm_in, sm_tail, sm_a, sm_b, rg_c)]
        + [pltpu.SemaphoreType.DMA((8,)), pltpu.SemaphoreType.DMA((5,)),
                        pltpu.SemaphoreType.DMA((4,)), pltpu.SemaphoreType.DMA((4,)),
                        pltpu.SemaphoreType.DMA((28,)), pltpu.SemaphoreType.DMA((28,))],
        compiler_params=pltpu.CompilerParams(vmem_limit_bytes=48 * MIB),
    )(*[pltpu.with_memory_space_constraint(a, pltpu.HBM)
        for a in (*parts, *lands, sm_in, sm_tail, sm_a, sm_b, rg_c)])
    return list(outs[0:4]), outs[4], outs[5]


def _adam_rows(w, g, m, v):
    m2 = ADAM_B1 * m + (1.0 - ADAM_B1) * g
    v2 = ADAM_B2 * v + (1.0 - ADAM_B2) * (g * g)
    m_hat = m2 / (1.0 - ADAM_B1 ** ADAM_STEP)
    v_hat = v2 / (1.0 - ADAM_B2 ** ADAM_STEP)
    delta = -ADAM_LR * (m_hat / (jnp.sqrt(v_hat) + ADAM_EPS) + ADAM_WD * w)
    return delta, m2, v2


def _adam_big(gs, ws, ms, vs):
    n = len(gs)
    steps = 8

    def body(*refs):
        ins, outs = refs[:4 * n], refs[4 * n:]
        for i in range(n):
            g, w, m, v = (r[...] for r in ins[4 * i:4 * i + 4])
            d, m2, v2 = _adam_rows(w, g, m, v)
            outs[3 * i][...] = d
            outs[3 * i + 1][...] = m2
            outs[3 * i + 2][...] = v2

    in_specs, out_specs, out_shape, args = [], [], [], []
    for g, w, m, v in zip(gs, ws, ms, vs):
        r, c = w.shape
        spec = lambda: pl.BlockSpec((r // steps, c), lambda i: (i, 0))
        in_specs += [spec() for _ in range(4)]
        out_specs += [spec() for _ in range(3)]
        out_shape += [jax.ShapeDtypeStruct((r, c), F32)] * 3
        args += [pltpu.with_memory_space_constraint(a, pltpu.HBM) for a in (g, w, m, v)]
    outs = pl.pallas_call(
        body, name="adam_big", grid=(steps,), in_specs=in_specs, out_specs=out_specs, out_shape=out_shape,
        compiler_params=_cparams(("parallel",), 32),
    )(*args)
    return [tuple(outs[3 * i:3 * i + 3]) for i in range(n)]


_VEC_PARAMS = [
    ("norm_mix_w", ROW_NORM_MIX, 0, D), ("final_norm_w", ROW_FINAL_NORM, 0, D),
    ("b_ple_gate", ROW_B_PG, 0, D), ("ple_norm_w", ROW_PLE_NORM, 0, D),
    ("conv_b", ROW_CB_BA, 0, DG), ("rg_ba", ROW_CB_BA, DG, DG),
    ("rg_bx", ROW_BX_LAM, 0, DG), ("rg_lambda", ROW_BX_LAM, DG, DG),
    ("hg_norm_w", ROW_HG_NW, 0, HD),
]
_SMALL_ORDER = [n for n, _, _, _ in _VEC_PARAMS] + ["hg_lb", "conv_w", "rg_wa", "rg_wx"]


def _adam_small(vred, rgred, ws, ms, vs):
    names = _SMALL_ORDER
    n = len(names)

    def body(vred_ref, rg_ref, *refs):
        w_refs = dict(zip(names, refs[0:n]))
        m_refs = dict(zip(names, refs[n:2 * n]))
        v_refs = dict(zip(names, refs[2 * n:3 * n]))
        outs = refs[3 * n:]
        o_refs = {nm: outs[4 * i:4 * i + 4] for i, nm in enumerate(names)}
        kme = 2 * lax.axis_index("x") + lax.axis_index("y")

        def update(nm, g, idx):
            d, m2, v2 = _adam_rows(w_refs[nm][idx], g, m_refs[nm][idx], v_refs[nm][idx])
            og, od, om, ov = o_refs[nm]
            og[idx] = g
            od[idx] = d
            om[idx] = m2
            ov[idx] = v2

        def packed(row, lane0, width):
            return jnp.concatenate([vred_ref[j, row:row + 1, :] for j in range(lane0 // 128, (lane0 + width) // 128)],
                                   axis=1)

        everything = (slice(None), slice(None))
        for nm, row, lane0, width in _VEC_PARAMS:
            update(nm, packed(row, lane0, width), everything)
        for r in range(2):
            update("hg_lb", packed(ROW_HG_LB, r * DG, DG), (slice(r, r + 1), slice(None)))
        for j in range(4):
            g = vred_ref[(j % 2) * 4 + kme, ROW_CW01 + j // 2:ROW_CW01 + j // 2 + 1, :]
            update("conv_w", g, (slice(j, j + 1), slice(None)))
        for r0 in range(0, DG, 128):
            rs = (slice(r0, r0 + 128), slice(None))
            both = rg_ref[r0:r0 + 128, :]
            update("rg_wa", both[:, 0:RGB], rs)
            update("rg_wx", pltpu.roll(both, RGB, 1)[:, 0:RGB], rs)

    args = [vred, rgred] + [d[nm] for d in (ws, ms, vs) for nm in names]
    out_shape = []
    for nm in names:
        out_shape += [jax.ShapeDtypeStruct(ws[nm].shape, F32)] * 4
    whole = lambda s: pl.BlockSpec(s.shape, lambda i, nd=len(s.shape): (0,) * nd)
    outs = pl.pallas_call(
        body, name="adam_small", out_shape=out_shape, grid=(1,),
        in_specs=[whole(a) for a in args], out_specs=[whole(s) for s in out_shape],
    )(*args)
    return {nm: tuple(outs[4 * i:4 * i + 4]) for i, nm in enumerate(names)}


def _fwd_inproj(x, nw, w_in_b, dep):
    tm = 512

    def body(x_ref, nw_ref, w_ref, dep_ref, pf_ref, pb_ref, u_ref):
        xv = x_ref[...]
        s = lax.rsqrt(jnp.mean(xv * xv, axis=-1, keepdims=True) + EPS)
        u = (xv * s * nw_ref[...]).astype(BF16)
        u_ref[...] = u
        r = [jnp.dot(u, w_ref[k], preferred_element_type=F32) for k in range(NSHARD)]
        h = DG // 2
        pf_ref[:, 0:DG] = r[0][:, 0:DG]
        pb_ref[:, 0:h] = r[0][:, DG:SHW].astype(BF16)
        pb_ref[:, h:DG] = r[1][:, 0:h].astype(BF16)
        pb_ref[:, DG:2 * DG] = r[1][:, h:SHW].astype(BF16)
        pf_ref[:, DG:2 * DG] = r[2][:, 0:DG]
        pb_ref[:, 2 * DG:2 * DG + h] = r[2][:, DG:SHW].astype(BF16)
        pb_ref[:, 2 * DG + h:3 * DG] = r[3][:, 0:h].astype(BF16)
        pb_ref[:, 3 * DG:4 * DG] = r[3][:, h:SHW].astype(BF16)

    return pl.pallas_call(
        body, name="fwd_inproj", grid=(T // tm,),
        in_specs=[pl.BlockSpec((tm, D), lambda i: (i, 0)),
                  pl.BlockSpec((1, D), lambda i: (0, 0)),
                  pl.BlockSpec((NSHARD, D, SHW), lambda i: (0, 0, 0), pipeline_mode=pl.Buffered(1)), ANY_SPEC],
        out_specs=[pl.BlockSpec((tm, 2 * DG), lambda i: (i, 0)),
                   pl.BlockSpec((tm, 4 * DG), lambda i: (i, 0)),
                   pl.BlockSpec((tm, D), lambda i: (i, 0))],
        out_shape=[jax.ShapeDtypeStruct((T, 2 * DG), F32), jax.ShapeDtypeStruct((T, 4 * DG), BF16),
                   jax.ShapeDtypeStruct((T, D), BF16)],
        compiler_params=_cparams(("parallel",), 48),
    )(x, nw, w_in_b, dep)


def _conv_rows(cw_ref):
    return [jnp.concatenate([cw_ref[k, j:j + 1, :] for k in range(NSHARD)], axis=1) for j in range(4)]


def _rg_conv(xa, prev8, cw, cb, rows):
    taps = [_shift_down(xa, prev8, 3, rows), _shift_down(xa, prev8, 2, rows),
            _shift_down(xa, prev8, 1, rows), xa]
    xc = cb
    for j in range(4):
        xc = xc + taps[j] * cw[j]
    return xc, taps


def _block_mask():
    r = lax.broadcasted_iota(I32, (DG, DG), 0)
    c = lax.broadcasted_iota(I32, (DG, DG), 1)
    return (r >> 6) == (c >> 6)


def _dense_from_blocks(wc):
    j = lax.broadcasted_iota(I32, (RGB, DG), 0)
    c = lax.broadcasted_iota(I32, (RGB, DG), 1)
    spread = _mm_exact(wc, ((c & (RGB - 1)) == j).astype(F32))
    return jnp.where(_block_mask(), spread, 0.0)


def _blocks_from_dense(da, dx):
    c = lax.broadcasted_iota(I32, (DG, 128), 0)
    j = lax.broadcasted_iota(I32, (DG, 128), 1)
    hit = (c & (RGB - 1)) == (j & (RGB - 1))
    mask = _block_mask()
    return (_mm_exact(jnp.where(mask, da, 0.0), (hit & (j < RGB)).astype(F32))
            + _mm_exact(jnp.where(mask, dx, 0.0), (hit & (j >= RGB)).astype(F32)))


def _rg_gates(xc, wa, ba, wx, bx, sp, first_row):
    r = _sigmoid(_mm(xc, wa) + ba)
    i = _sigmoid(_mm(xc, wx) + bx)
    log_a = (-RG_C) * r * sp
    a = jnp.exp(log_a)
    a2 = a * a
    one_m_a2 = -jnp.tanh(log_a) * (a2 + 1.0)
    mult = jnp.where(first_row, 1.0, jnp.sqrt(one_m_a2))
    return r, i, a, a2, mult


def _softplus(z):
    return jnp.maximum(z, 0.0) + jnp.log1p(jnp.exp(-jnp.abs(z)))


def _fwd_rglru(pf, pb, cw3, conv_b, wa_c, ba, wx_c, bx, lam, dep):
    tm = 512
    ng = tm // 8

    def body(xa_ref, ga_ref, cw_ref, cb_ref, wa_ref, ba_ref, wx_ref, bx_ref, lam_ref, dep_ref,
             h_ref, ya_ref, a_s, u_s, tail_s, hc_s, wa_s, wx_s):
        i = pl.program_id(0)

        @pl.when(i == 0)
        def _():
            tail_s[...] = jnp.zeros_like(tail_s)
            hc_s[...] = jnp.zeros_like(hc_s)
            wa_s[...] = _dense_from_blocks(wa_ref[...]).astype(BF16)
            wx_s[...] = _dense_from_blocks(wx_ref[...]).astype(BF16)

        rows = lax.broadcasted_iota(I32, (tm, DG), 0)
        xa = xa_ref[...]
        xc, _ = _rg_conv(xa, tail_s[...], _conv_rows(cw_ref), cb_ref[...], rows)
        tail_s[...] = xa[tm - 8:tm, :]
        sp = _softplus(-lam_ref[...])
        rp = _mm(xc, wa_s[...]) + ba_ref[...]
        ip = _mm(xc, wx_s[...]) + bx_ref[...]
        rb = 64
        rows_b = lax.broadcasted_iota(I32, (rb, DG), 0)
        rows8 = rows_b & 7
        carry = hc_s[0:1, :]
        for b0 in range(0, tm, rb):
            sl = slice(b0, b0 + rb)
            r = _sigmoid(rp[sl])
            ig = _sigmoid(ip[sl])
            log_a = (-RG_C) * r * sp
            av = jnp.exp(log_a)
            mult = jnp.sqrt(-jnp.tanh(log_a) * (av * av + 1.0))
            if b0 == 0:
                mult = jnp.where((rows_b + i * tm) == 0, 1.0, mult)
            uv = mult * (ig * xc[sl])
            for d in (1, 2, 4):
                keep = rows8 >= d
                uv = uv + av * jnp.where(keep, _roll_in_groups(uv, d), 0.0)
                av = av * jnp.where(keep, _roll_in_groups(av, d), 1.0)
            ga = ga_ref[sl, :].astype(F32)
            gate = ga * _sigmoid(ga)
            hs = []
            for g in range(rb // 8):
                gs = slice(g * 8, (g + 1) * 8)
                hv = uv[gs] + av[gs] * carry
                carry = hv[7:8, :]
                hs.append(hv)
            hb = jnp.concatenate(hs, axis=0)
            h_ref[sl, :] = hb
            ya_ref[sl, :] = (hb * gate).astype(BF16)
        hc_s[0:1, :] = carry

    vec = lambda: pl.BlockSpec((1, DG), lambda i: (0, 0))
    blocks = lambda: pl.BlockSpec((DG, RGB), lambda i: (0, 0))
    return pl.pallas_call(
        body, name="fwd_rglru", grid=(T // tm,),
        in_specs=[pl.BlockSpec((tm, DG), lambda i: (i, 0)),
                  pl.BlockSpec((tm, DG), lambda i: (i, 0)),
                  pl.BlockSpec((NSHARD, 4, 128), lambda i: (0, 0, 0)), vec(),
                  blocks(), vec(), blocks(), vec(), vec(), ANY_SPEC],
        out_specs=[pl.BlockSpec((tm, DG), lambda i: (i, 0)),
                   pl.BlockSpec((tm, DG), lambda i: (i, 0))],
        out_shape=[jax.ShapeDtypeStruct((T, DG), F32), pltpu.HBM((T, DG), BF16)],
        scratch_shapes=[pltpu.VMEM((tm, DG), F32), pltpu.VMEM((tm, DG), F32),
                        pltpu.VMEM((8, DG), F32), pltpu.VMEM((8, DG), F32),
                        pltpu.VMEM((DG, DG), BF16), pltpu.VMEM((DG, DG), BF16)],
        compiler_params=_cparams(("arbitrary",), 48),
    )(pf, pb, cw3, conv_b, wa_c, ba, wx_c, bx, lam, dep)


def _hg_lower_bound(lb_ref):
    return _sig_pair(lb_ref[0:1, :] - lb_ref[1:2, :])


def _hg_gates(fz, lb, one_m_lb):
    sg, sn = _sig_pair(fz)
    f = lb + one_m_lb * sg
    return sg, sn, f, jnp.log(f), one_m_lb * sn


def _tri(lower):
    r = lax.broadcasted_iota(I32, (CH, CH), 0)
    c = lax.broadcasted_iota(I32, (CH, CH), 1)
    return (r >= c) if lower else (r <= c)


def _split3(v):
    hi = v.astype(BF16)
    r1 = v - hi.astype(F32)
    mid = r1.astype(BF16)
    lo = (r1 - mid.astype(F32)).astype(BF16)
    return hi, mid, lo


def _chunk_cumsum(v, rows64, reverse=False):
    del rows64
    tri = _tri(not reverse).astype(BF16)
    out = []
    for c in range(v.shape[0] // CH):
        pieces = _split3(v[c * CH:(c + 1) * CH])
        out.append(sum(jnp.dot(tri, p, preferred_element_type=F32) for p in pieces))
    return jnp.concatenate(out, axis=0)


def _chunk_rev_cumsum(v, rows64):
    return _chunk_cumsum(v, rows64, reverse=True)


def _hg_recompute(q_ref, f_ref, lb, one_m_lb, rows64, eb_s, enb_s, ekd_s, qe_s, ke_s, kd_s, dec_s):
    nc = q_ref.shape[0] // CH
    sg, sn, f, logf, k = _hg_gates(f_ref[...], lb, one_m_lb)
    q = q_ref[...].astype(F32)
    sq = _sigmoid(q)
    qs = q * sq * (HD ** -0.5)
    b = _chunk_cumsum(logf, rows64)
    for c in range(nc):
        rs = slice(c * CH, (c + 1) * CH)
        b_c = b[rs]
        bl = b_c[CH - 1:CH, :]
        eb, enb, ekd = jnp.exp(b_c), jnp.exp(-b_c), jnp.exp(bl - b_c)
        if eb_s is not None:
            eb_s[rs, :] = eb
            enb_s[rs, :] = enb
            ekd_s[rs, :] = ekd
        qe_s[rs, :] = (qs[rs] * eb).astype(BF16)
        ke_s[rs, :] = (k[rs] * enb).astype(BF16)
        kd_s[rs, :] = (k[rs] * ekd).astype(BF16)
        dec_s[c:c + 1, :] = jnp.exp(bl)
    return sg, sn, f, k, q, sq, qs


def _fwd_hgrn2(pf, pb, hg_lb, hg_nw):
    tm = 512
    nc = tm // CH

    def body(q_ref, f_ref, v_ref, g_ref, lb_ref, nw_ref, yb_ref, o_ref, sp_ref,
             st_s, qe_s, ke_s, kd_s, vb_s, dec_s, p_s, ds_s):
        i = pl.program_id(0)

        @pl.when(i == 0)
        def _():
            st_s[...] = jnp.zeros_like(st_s)

        lb, one_m_lb = _hg_lower_bound(lb_ref)
        rows64 = lax.broadcasted_iota(I32, (tm, DG), 0) & (CH - 1)
        _hg_recompute(q_ref, f_ref, lb, one_m_lb, rows64, None, None, None, qe_s, ke_s, kd_s, dec_s)
        vb_s[...] = v_ref[...]
        mask = _tri(True)
        items = [(c, hd, slice(c * CH, (c + 1) * CH), slice(hd * HD, (hd + 1) * HD))
                 for c in range(nc) for hd in range(NH)]
        for c, hd, rs, cols in items:
            p_s[c * NH + hd] = jnp.where(mask, _mm_nt(qe_s[rs, cols], ke_s[rs, cols]), 0.0).astype(BF16)
            ds_s[c * NH + hd] = _mm_tn(vb_s[rs, cols], kd_s[rs, cols])
        for c, hd, rs, cols in items:
            st = st_s[hd]
            sp_ref[hd, c] = st
            st_s[hd] = st * dec_s[c:c + 1, cols] + ds_s[c * NH + hd]
        for c, hd, rs, cols in items:
            o_ref[rs, cols] = _mm(p_s[c * NH + hd], vb_s[rs, cols]) + _mm_nt(qe_s[rs, cols], sp_ref[hd, c])
        nw = nw_ref[...]
        for hd in range(NH):
            cols = slice(hd * HD, (hd + 1) * HD)
            o = o_ref[:, cols]
            so = lax.rsqrt(jnp.mean(o * o, axis=-1, keepdims=True) + EPS)
            g = g_ref[:, cols].astype(F32)
            sg = _sigmoid(g)
            yb_ref[:, cols] = (o * so * nw * (g * sg)).astype(BF16)

    col = lambda j: pl.BlockSpec((tm, DG), lambda i: (i, j))
    return pl.pallas_call(
        body, name="fwd_hgrn2", grid=(T // tm,),
        in_specs=[col(1), col(1), col(2), col(3),
                  pl.BlockSpec((2, DG), lambda i: (0, 0)),
                  pl.BlockSpec((1, HD), lambda i: (0, 0))],
        out_specs=[pl.BlockSpec((tm, DG), lambda i: (i, 0)),
                   pl.BlockSpec((tm, DG), lambda i: (i, 0)),
                   pl.BlockSpec((NH, nc, HD, HD), lambda i: (0, i, 0, 0))],
        out_shape=[pltpu.HBM((T, DG), BF16), jax.ShapeDtypeStruct((T, DG), F32),
                   jax.ShapeDtypeStruct((NH, NCHUNK, HD, HD), F32)],
        scratch_shapes=[pltpu.VMEM((NH, HD, HD), F32),
                        pltpu.VMEM((tm, DG), BF16), pltpu.VMEM((tm, DG), BF16), pltpu.VMEM((tm, DG), BF16),
                        pltpu.VMEM((tm, DG), BF16), pltpu.VMEM((nc, DG), F32),
                        pltpu.VMEM((nc * NH, CH, CH), BF16), pltpu.VMEM((nc * NH, HD, HD), F32)],
        compiler_params=_cparams(("arbitrary",), 48),
    )(pb, pf, pb, pb, hg_lb, hg_nw)


def _tail_fwd_bwd(x, p, tgt, ya, yb, w_out_b, w_pg_b, w_pp_b, ple_nw, b_pg, fnw):
    tm = 512
    nt = T // tm
    QR = D // NSHARD

    def body(x_ref, p_ref, t_ref, ya_ref, yb_ref, wo_ref, wg_ref, wp_ref, pw_ref, b_ref, fw_ref,
             dh1_ref, dyab_ref, dwo_ref, dwg_ref, dwp_ref, sm_ref, dwo_s, dwg_s, dwp_s):
        i = pl.program_id(0)

        @pl.when(i == 0)
        def _():
            dwo_s[...] = jnp.zeros_like(dwo_s)
            dwg_s[...] = jnp.zeros_like(dwg_s)
            dwp_s[...] = jnp.zeros_like(dwp_s)
            sm_ref[...] = jnp.zeros_like(sm_ref)

        ya = ya_ref[...]
        yb = yb_ref[...]
        pv = p_ref[...].astype(BF16)
        pw = pw_ref[...]
        fw = fw_ref[...]
        h1 = x_ref[...] + _mm(ya, wo_ref[0:DG, :]) + _mm(yb, wo_ref[DG:D, :])
        s2 = lax.rsqrt(jnp.mean(h1 * h1, axis=-1, keepdims=True) + EPS)
        n2h = h1 * s2
        n2 = (n2h * pw).astype(BF16)
        z = _mm(n2, wg_ref[...]) + b_ref[...]
        gate = _sigmoid(z)
        pp = jnp.concatenate([_mm(pv, wp_ref[k]) for k in range(NSHARD)], axis=1)
        h2 = h1 + gate * pp
        s3 = lax.rsqrt(jnp.mean(h2 * h2, axis=-1, keepdims=True) + EPS)
        hn = h2 * s3
        err = hn * fw - t_ref[...]
        sm_ref[0:1, :] += _rowsum(err * err)
        dy = err * (1.0 / D)
        sm_ref[1:2, :] += _rowsum(dy * hn)
        g3 = dy * fw
        dh2 = s3 * (g3 - hn * jnp.mean(g3 * hn, axis=-1, keepdims=True))
        dpp = (dh2 * gate).astype(BF16)
        dz = dh2 * pp * gate * (1.0 - gate)
        sm_ref[2:3, :] += _rowsum(dz)
        dzb = dz.astype(BF16)
        dwg_s[...] += _mm_tn(n2, dzb)
        dn2 = _mm_nt(dzb, wg_ref[...])
        for k in range(NSHARD):
            dwp_s[k] += _mm_tn(pv, dpp[:, k * PLE:(k + 1) * PLE])
        sm_ref[3:4, :] += _rowsum(dn2 * n2h)
        g2 = dn2 * pw
        dh1 = dh2 + s2 * (g2 - n2h * jnp.mean(g2 * n2h, axis=-1, keepdims=True))
        dh1_ref[...] = dh1
        dh1b = dh1.astype(BF16)
        dyab_ref[...] = _mm_nt(dh1b, wo_ref[...])
        dwo_s[0:DG, :] += _mm_tn(ya, dh1b)
        dwo_s[DG:D, :] += _mm_tn(yb, dh1b)

        @pl.when(i == nt - 1)
        def _():
            total = jnp.sum(sm_ref[0:1, :], axis=-1, keepdims=True) * (0.5 / D)
            sm_ref[4:5, :] = jnp.broadcast_to(total, (1, D))
            for k in range(NSHARD):
                dwo_ref[k] = dwo_s[k * QR:(k + 1) * QR, :].astype(BF16)
                dwg_ref[k] = dwg_s[k * QR:(k + 1) * QR, :].astype(BF16)
                dwp_ref[k] = dwp_s[k].astype(BF16)

    row = lambda w: pl.BlockSpec((tm, w), lambda i: (i, 0))
    const2 = lambda s: pl.BlockSpec(s, lambda i: (0, 0), pipeline_mode=pl.Buffered(1))
    const3 = lambda s: pl.BlockSpec(s, lambda i: (0, 0, 0), pipeline_mode=pl.Buffered(1))
    return pl.pallas_call(
        body, name="tail_fwd_bwd", grid=(nt,),
        in_specs=[row(D), row(PLE), row(D), row(DG), row(DG),
                  const2((D, D)), const2((D, D)), const3((NSHARD, PLE, PLE)),
                  const2((1, D)), const2((1, D)), const2((1, D))],
        out_specs=[row(D), row(D), const3((NSHARD, QR, D)), const3((NSHARD, QR, D)),
                   const3((NSHARD, PLE, PLE)), const2((8, D))],
        out_shape=[jax.ShapeDtypeStruct((T, D), F32), jax.ShapeDtypeStruct((T, D), F32),
                   jax.ShapeDtypeStruct((NSHARD, QR, D), BF16), jax.ShapeDtypeStruct((NSHARD, QR, D), BF16),
                   jax.ShapeDtypeStruct((NSHARD, PLE, PLE), BF16), jax.ShapeDtypeStruct((8, D), F32)],
        scratch_shapes=[pltpu.VMEM((D, D), F32), pltpu.VMEM((D, D), F32), pltpu.VMEM((NSHARD, PLE, PLE), F32)],
        compiler_params=_cparams(("arbitrary",), 62),
    )(x, p, tgt, pltpu.with_memory_space_constraint(ya, pltpu.HBM), pltpu.with_memory_space_constraint(yb, pltpu.HBM),
      w_out_b, w_pg_b, w_pp_b, ple_nw, b_pg, fnw)


def _bwd_rglru(pf, pb, h, dyab, cw3, conv_b, wa_c, ba, wx_c, bx, lam, dep):
    tm = 512
    nt = T // tm
    ng = tm // 8

    def body(xa_ref, ga_ref, xp_ref, h_ref, hp_ref, dya_ref, cw_ref, cb_ref, wa_ref, ba_ref, wx_ref, bx_ref,
             lam_ref, dep_ref, da_ref, rg_ref, sm_ref, a_s, g_s, cg_s, nxt_s, wa_s, wx_s, dwa_s, dwx_s):
        i = pl.program_id(0)
        tile = nt - 1 - i

        @pl.when(i == 0)
        def _():
            dwa_s[...] = jnp.zeros_like(dwa_s)
            dwx_s[...] = jnp.zeros_like(dwx_s)
            sm_ref[...] = jnp.zeros_like(sm_ref)
            cg_s[...] = jnp.zeros_like(cg_s)
            nxt_s[...] = jnp.zeros_like(nxt_s)
            wa_s[...] = _dense_from_blocks(wa_ref[...]).astype(BF16)
            wx_s[...] = _dense_from_blocks(wx_ref[...]).astype(BF16)

        rows = lax.broadcasted_iota(I32, (tm, DG), 0)
        has_prev = tile > 0
        xa = xa_ref[...]
        xprev = jnp.where(has_prev, xp_ref[...], 0.0)
        cw = _conv_rows(cw_ref)
        xc, taps = _rg_conv(xa, xprev, cw, cb_ref[...], rows)
        lam_v = lam_ref[...]
        sp = _softplus(-lam_v)
        first_row = (rows + tile * tm) == 0
        r, ig, a, a2, mult = _rg_gates(xc, wa_s[...], ba_ref[...], wx_s[...], bx_ref[...], sp, first_row)
        hv = h_ref[...]
        hprev = jnp.where(has_prev, hp_ref[...], 0.0)
        h_m1 = _shift_down(hv, hprev, 1, rows)
        ga = ga_ref[...].astype(F32)
        sg = _sigmoid(ga)
        dya = dya_ref[...]
        dga = dya * hv * (sg * (1.0 + ga * (1.0 - sg)))

        av = jnp.where(rows == tm - 1, 1.0, pltpu.roll(a, tm - 1, 0))
        gv = dya * (ga * sg)
        rows8 = rows & 7
        for d in (1, 2, 4):
            keep = rows8 < 8 - d
            gv = gv + av * jnp.where(keep, _roll_in_groups(gv, 8 - d), 0.0)
            av = av * jnp.where(keep, _roll_in_groups(av, 8 - d), 1.0)
        a_s[...] = av
        g_s[...] = gv
        carry = cg_s[0:1, :]
        for g in range(ng - 1, -1, -1):
            sl = slice(g * 8, (g + 1) * 8)
            ab, gb = a_s[sl, :], g_s[sl, :]
            g_s[sl, :] = gb + ab * carry
            carry = gb[0:1, :] + ab[0:1, :] * carry
        cg_s[0:1, :] = a[0:1, :] * carry

        gt = g_s[...]
        da = gt * h_m1
        ixc = ig * xc
        di = gt * mult * xc
        dxc = gt * mult * ig
        dlog_a = da * a + jnp.where(first_row, 0.0, gt * ixc * (-a2 / mult))
        sm_ref[3:4, :] += _rowsum(dlog_a * ((-RG_C) * r))
        dpr_f = dlog_a * ((-RG_C) * sp) * r * (1.0 - r)
        dpi_f = di * ig * (1.0 - ig)
        sm_ref[1:2, :] += _rowsum(dpr_f)
        sm_ref[2:3, :] += _rowsum(dpi_f)
        dpr = dpr_f.astype(BF16)
        dpi = dpi_f.astype(BF16)
        xcb = xc.astype(BF16)
        dwa_s[...] += _mm_tn(xcb, dpr)
        dwx_s[...] += _mm_tn(xcb, dpi)
        dxc = dxc + _mm_nt(dpr, wa_s[...]) + _mm_nt(dpi, wx_s[...])
        sm_ref[0:1, :] += _rowsum(dxc)
        for j in range(4):
            sm_ref[4 + j:5 + j, :] += _rowsum(dxc * taps[j])
        nxt = nxt_s[...]
        dxa = (dxc * cw[3] + _shift_up(dxc, nxt, 1, rows) * cw[2]
               + _shift_up(dxc, nxt, 2, rows) * cw[1] + _shift_up(dxc, nxt, 3, rows) * cw[0])
        nxt_s[...] = dxc[0:8, :]
        da_ref[:, 0:DG] = dxa.astype(BF16)
        da_ref[:, DG:D] = dga.astype(BF16)

        @pl.when(i == nt - 1)
        def _():
            _, s_neg = _sig_pair(lam_v)
            sm_ref[3:4, :] = sm_ref[3:4, :] * (-s_neg)
            rg_ref[...] = _blocks_from_dense(dwa_s[...], dwx_s[...])

    vec = lambda: pl.BlockSpec((1, DG), lambda i: (0, 0))
    blocks = lambda: pl.BlockSpec((DG, RGB), lambda i: (0, 0))
    prev8 = lambda: pl.BlockSpec((8, DG), lambda i: (jnp.maximum((nt - 1 - i) * (tm // 8) - 1, 0), 0))
    return pl.pallas_call(
        body, name="bwd_rglru", grid=(nt,),
        in_specs=[pl.BlockSpec((tm, DG), lambda i: (nt - 1 - i, 0)),
                  pl.BlockSpec((tm, DG), lambda i: (nt - 1 - i, 0)),
                  prev8(),
                  pl.BlockSpec((tm, DG), lambda i: (nt - 1 - i, 0)),
                  prev8(),
                  pl.BlockSpec((tm, DG), lambda i: (nt - 1 - i, 0)),
                  pl.BlockSpec((NSHARD, 4, 128), lambda i: (0, 0, 0)), vec(),
                  blocks(), vec(), blocks(), vec(), vec(), ANY_SPEC],
        out_specs=[pl.BlockSpec((tm, D), lambda i: (nt - 1 - i, 0)),
                   pl.BlockSpec((DG, 128), lambda i: (0, 0)),
                   pl.BlockSpec((8, DG), lambda i: (0, 0))],
        out_shape=[jax.ShapeDtypeStruct((T, D), BF16), jax.ShapeDtypeStruct((DG, 128), F32),
                   jax.ShapeDtypeStruct((8, DG), F32)],
        scratch_shapes=[pltpu.VMEM((tm, DG), F32), pltpu.VMEM((tm, DG), F32),
                        pltpu.VMEM((8, DG), F32), pltpu.VMEM((8, DG), F32),
                        pltpu.VMEM((DG, DG), BF16), pltpu.VMEM((DG, DG), BF16),
                        pltpu.VMEM((DG, DG), F32), pltpu.VMEM((DG, DG), F32)],
        compiler_params=_cparams(("arbitrary",), 56),
    )(pf, pb, pf, h, h, dyab, cw3, conv_b, wa_c, ba, wx_c, bx, lam, dep)


def _bwd_hgrn2(pf, pb, o, s_prev, dyab, hg_lb, hg_nw, dep):
    tm = 512
    nt = T // tm
    nc = tm // CH

    def body(q_ref, f_ref, v_ref, g_ref, o_ref, sp_ref, dy_ref, lb_ref, nw_ref, dep_ref, db_ref, sm_ref,
             dst_s, eb_s, enb_s, ekd_s, qe_s, ke_s, kd_s, vb_s, do_s, dec_s, ddec_s, p_s, dp_s,
             g_s, dsta_s, dva_s, dqe_s, dke_s, dkd_s, dlf_s):
        i = pl.program_id(0)

        @pl.when(i == 0)
        def _():
            sm_ref[...] = jnp.zeros_like(sm_ref)
            dst_s[...] = jnp.zeros_like(dst_s)

        lb, one_m_lb = _hg_lower_bound(lb_ref)
        rows64 = lax.broadcasted_iota(I32, (tm, DG), 0) & (CH - 1)
        sg, sn, f, k, q, sq, qs = _hg_recompute(
            q_ref, f_ref, lb, one_m_lb, rows64, eb_s, enb_s, ekd_s, qe_s, ke_s, kd_s, dec_s)
        vb_s[...] = v_ref[...]

        nw = nw_ref[...]
        for hd in range(NH):
            cols = slice(hd * HD, (hd + 1) * HD)
            g = g_ref[:, cols].astype(F32)
            sgg = _sigmoid(g)
            o = o_ref[:, cols]
            so = lax.rsqrt(jnp.mean(o * o, axis=-1, keepdims=True) + EPS)
            oh = o * so
            dyb = dy_ref[:, cols]
            db_ref[:, 3 * DG + hd * HD:3 * DG + (hd + 1) * HD] = (
                dyb * (oh * nw) * (sgg * (1.0 + g * (1.0 - sgg)))).astype(BF16)
            don = dyb * (g * sgg)
            sm_ref[1:2, 0:HD] += _rowsum(don * oh)
            gw = don * nw
            do_s[:, cols] = (so * (gw - oh * jnp.mean(gw * oh, axis=-1, keepdims=True))).astype(BF16)

        mask = _tri(True)
        items = [(c, hd, slice(c * CH, (c + 1) * CH), slice(hd * HD, (hd + 1) * HD))
                 for c in range(nc) for hd in range(NH)]
        for c, hd, rs, cols in items:
            p_s[c * NH + hd] = jnp.where(mask, _mm_nt(qe_s[rs, cols], ke_s[rs, cols]), 0.0).astype(BF16)
            dp_s[c * NH + hd] = jnp.where(mask, _mm_nt(do_s[rs, cols], vb_s[rs, cols]), 0.0).astype(BF16)
        for c, hd, rs, cols in items:
            n = c * NH + hd
            dva_s[rs, cols] = _mm_tn(p_s[n], do_s[rs, cols])
            dqe_s[rs, cols] = _mm(dp_s[n], ke_s[rs, cols])
            dke_s[rs, cols] = _mm_tn(dp_s[n], qe_s[rs, cols])
            g_s[n] = _mm_tn(do_s[rs, cols], qe_s[rs, cols])
        for c, hd, rs, cols in reversed(items):
            n = c * NH + hd
            dst = dst_s[hd]
            dsta_s[n] = dst
            dst_s[hd] = dst * dec_s[c:c + 1, cols] + g_s[n]
        for c, hd, rs, cols in items:
            n = c * NH + hd
            dst = dsta_s[n]
            st_prev = sp_ref[hd, c]
            dv = dva_s[rs, cols] + _mm_nt(kd_s[rs, cols], dst)
            db_ref[rs, 2 * DG + hd * HD:2 * DG + (hd + 1) * HD] = dv.astype(BF16)
            dqe_s[rs, cols] += _mm(do_s[rs, cols], st_prev)
            dkd_s[rs, cols] = _mm(vb_s[rs, cols], dst)
            ddec_s[c:c + 1, cols] = _rowsum(dst * st_prev)

        eb, enb, ekd = eb_s[...], enb_s[...], ekd_s[...]
        dqe, dke, dkd = dqe_s[...], dke_s[...], dkd_s[...]
        t_kd = dkd * (k * ekd)
        rc = _chunk_rev_cumsum(dqe * (qs * eb) - dke * (k * enb) - t_kd, rows64)
        for c in range(nc):
            rs = slice(c * CH, (c + 1) * CH)
            dbl = _rowsum(t_kd[rs]) + ddec_s[c:c + 1, :] * dec_s[c:c + 1, :]
            dlf_s[rs, :] = rc[rs] + dbl
        t = dlf_s[...] / f - (dke * enb + dkd * ekd)
        db_ref[:, DG:2 * DG] = (one_m_lb * sg * sn * t).astype(BF16)
        sm_ref[0:1, :] += _rowsum(sn * t)
        db_ref[:, 0:DG] = (dqe * eb * (sq * (1.0 + q * (1.0 - sq))) * (HD ** -0.5)).astype(BF16)

        @pl.when(i == nt - 1)
        def _():
            dsm = sm_ref[0:1, :] * (lb * one_m_lb)
            sm_ref[2:3, :] = dsm
            sm_ref[3:4, :] = -dsm

    col = lambda j: pl.BlockSpec((tm, DG), lambda i: (nt - 1 - i, j))
    big = lambda dt: pltpu.VMEM((tm, DG), dt)
    return pl.pallas_call(
        body, name="bwd_hgrn2", grid=(nt,),
        in_specs=[col(1), col(1), col(2), col(3),
                  pl.BlockSpec((tm, DG), lambda i: (nt - 1 - i, 0)),
                  pl.BlockSpec((NH, nc, HD, HD), lambda i: (0, nt - 1 - i, 0, 0)),
                  pl.BlockSpec((tm, DG), lambda i: (nt - 1 - i, 1)),
                  pl.BlockSpec((2, DG), lambda i: (0, 0)),
                  pl.BlockSpec((1, HD), lambda i: (0, 0)), ANY_SPEC],
        out_specs=[pl.BlockSpec((tm, 4 * DG), lambda i: (nt - 1 - i, 0)),
                   pl.BlockSpec((8, DG), lambda i: (0, 0))],
        out_shape=[jax.ShapeDtypeStruct((T, 4 * DG), BF16), jax.ShapeDtypeStruct((8, DG), F32)],
        scratch_shapes=[pltpu.VMEM((NH, HD, HD), F32),
                        big(F32), big(F32), big(F32),
                        big(BF16), big(BF16), big(BF16), big(BF16), big(BF16),
                        pltpu.VMEM((nc, DG), F32), pltpu.VMEM((nc, DG), F32),
                        pltpu.VMEM((nc * NH, CH, CH), BF16), pltpu.VMEM((nc * NH, CH, CH), BF16),
                        pltpu.VMEM((nc * NH, HD, HD), F32), pltpu.VMEM((nc * NH, HD, HD), F32),
                        big(F32), big(F32), big(F32), big(F32), big(F32)],
        compiler_params=_cparams(("arbitrary",), 56),
    )(pb, pf, pb, pb, o, s_prev, dyab, hg_lb, hg_nw, dep)


def _dproj_pieces(k, da_ref, db_ref):
    if k == 0:
        return [(da_ref[:, 0:SHW], 0)]
    if k == 1:
        return [(da_ref[:, SHW:D], 0), (db_ref[:, 0:DG], D - SHW)]
    if k == 2:
        return [(db_ref[:, DG:DG + SHW], 0)]
    return [(db_ref[:, DG + SHW:4 * DG], 0)]


def _bwd_inproj_dx(x, dh1, d_a, d_b, w_in_b, nw, dep):
    tm = 512
    nt = T // tm

    def body(x_ref, dh1_ref, da_ref, db_ref, w_ref, nw_ref, dep_ref, dx_ref, sm_ref):
        i = pl.program_id(0)

        @pl.when(i == 0)
        def _():
            sm_ref[...] = jnp.zeros_like(sm_ref)

        du = None
        for k in range(NSHARD):
            for val, off in _dproj_pieces(k, da_ref, db_ref):
                t = _mm_nt(val, w_ref[k, :, off:off + val.shape[1]])
                du = t if du is None else du + t
        xv = x_ref[...]
        s = lax.rsqrt(jnp.mean(xv * xv, axis=-1, keepdims=True) + EPS)
        xh = xv * s
        sm_ref[0:1, :] += _rowsum(du * xh)
        g = du * nw_ref[...]
        dx_ref[...] = dh1_ref[...] + s * (g - xh * jnp.mean(g * xh, axis=-1, keepdims=True))

    row = lambda w: pl.BlockSpec((tm, w), lambda i: (i, 0))
    return pl.pallas_call(
        body, name="bwd_inproj_dx", grid=(nt,),
        in_specs=[row(D), row(D), row(D), row(4 * DG),
                  pl.BlockSpec((NSHARD, D, SHW), lambda i: (0, 0, 0), pipeline_mode=pl.Buffered(1)),
                  pl.BlockSpec((1, D), lambda i: (0, 0)), ANY_SPEC],
        out_specs=[row(D), pl.BlockSpec((8, D), lambda i: (0, 0))],
        out_shape=[jax.ShapeDtypeStruct((T, D), F32), jax.ShapeDtypeStruct((8, D), F32)],
        compiler_params=_cparams(("arbitrary",), 56),
    )(x, dh1, d_a, d_b, w_in_b, nw, dep)


def _bwd_inproj_dw23(u_b, d_b):
    tm = 1024
    nt = T // tm

    def body(u_ref, db_ref, dw_ref, acc):
        i = pl.program_id(0)

        @pl.when(i == 0)
        def _():
            acc[...] = jnp.zeros_like(acc)

        u = u_ref[...]
        for j, k in enumerate((2, 3)):
            for val, off in _dproj_pieces(k, None, db_ref):
                acc[j, :, off:off + val.shape[1]] += _mm_tn(u, val)

        @pl.when(i == nt - 1)
        def _():
            for j in range(2):
                for r0 in range(0, D, 256):
                    dw_ref[j, r0:r0 + 256, :] = acc[j, r0:r0 + 256, :].astype(BF16)

    row = lambda w: pl.BlockSpec((tm, w), lambda i: (i, 0))
    return pl.pallas_call(
        body, name="bwd_inproj_dw23", grid=(nt,), in_specs=[row(D), row(4 * DG)],
        out_specs=pl.BlockSpec((2, D, SHW), lambda i: (0, 0, 0)),
        out_shape=jax.ShapeDtypeStruct((2, D, SHW), BF16),
        scratch_shapes=[pltpu.VMEM((2, D, SHW), F32)],
        compiler_params=_cparams(("arbitrary",), 48),
    )(u_b, d_b)


def _bwd_inproj_dw01(u_b, d_a, d_b, dw23):
    tm = 512
    nt = T // tm
    H = D // 2
    step = 128

    def body(u_ref, da_ref, db_ref, p23_ref, dw_ref, acc, send_s, recv_s, ssem, rsem):
        i = pl.program_id(0)

        @pl.when(i == 0)
        def _():
            acc[...] = jnp.zeros_like(acc)

        u = u_ref[...]
        for j, k in enumerate((0, 1)):
            for val, off in _dproj_pieces(k, da_ref, db_ref):
                acc[j, :, off:off + val.shape[1]] += _mm_tn(u, val)

        @pl.when(i == nt - 1)
        def _():
            x, y, c = lax.axis_index("x"), lax.axis_index("y"), lax.axis_index("c")
            sibling = (x, y, 1 - c)
            mine0 = pl.multiple_of(c * H, step)
            other0 = pl.multiple_of((1 - c) * H, step)
            copies = []
            for k in range(NSHARD):
                for r0 in range(0, H, step):
                    rs = pl.ds(other0 + r0, step)
                    send_s[k, r0:r0 + step, :] = acc[k, rs, :].astype(BF16) if k < 2 else p23_ref[k - 2, rs, :]
                cp = _remote(send_s.at[k], recv_s.at[k], ssem.at[k], rsem.at[k], sibling)
                cp.start()
                copies.append(cp)
            for k in range(NSHARD):
                copies[k].wait_recv()
                for r0 in range(0, H, step):
                    rs = pl.ds(mine0 + r0, step)
                    own = acc[k, rs, :] if k < 2 else p23_ref[k - 2, rs, :].astype(F32)
                    dw_ref[k, r0:r0 + step, :] = (own + recv_s[k, r0:r0 + step, :].astype(F32)).astype(BF16)
            for cp in copies:
                cp.wait_send()

    row = lambda w: pl.BlockSpec((tm, w), lambda i: (i, 0))
    return pl.pallas_call(
        body, name="bwd_inproj_dw01", grid=(nt,),
        in_specs=[row(D), row(D), row(4 * DG),
                  pl.BlockSpec((2, D, SHW), lambda i: (0, 0, 0), pipeline_mode=pl.Buffered(1))],
        out_specs=pl.BlockSpec((NSHARD, H, SHW), lambda i: (0, 0, 0)),
        out_shape=jax.ShapeDtypeStruct((NSHARD, H, SHW), BF16),
        scratch_shapes=[pltpu.VMEM((2, D, SHW), F32), pltpu.VMEM((NSHARD, H, SHW), BF16),
                        pltpu.VMEM((NSHARD, H, SHW), BF16),
                        pltpu.SemaphoreType.DMA((NSHARD,)), pltpu.SemaphoreType.DMA((NSHARD,))],
        compiler_params=_cparams(("arbitrary",), 56),
    )(u_b, d_a, d_b, dw23)


_OUT_ORDER = ["norm_mix_w", "w_in", "conv_w", "conv_b", "rg_wa", "rg_ba", "rg_wx", "rg_bx", "rg_lambda", "hg_lb",
              "hg_norm_w", "w_out", "ple_norm_w", "w_ple_gate", "b_ple_gate", "w_ple_proj", "final_norm_w"]
_BIG = ["w_in", "w_out", "w_ple_gate", "w_ple_proj"]


def _small_view(name, a):
    if name in ("rg_wa", "rg_wx"):
        return a.reshape(DG, RGB)
    if name == "conv_w":
        return a.reshape(4, 128)
    if name == "final_norm_w":
        return a.reshape(1, D)
    return a


def _landing(rows, cols):
    return lax.empty((NDEV, rows, cols), BF16)


def kernel(x, p, norm_mix_w, w_in, conv_w, conv_b, rg_wa, rg_ba, rg_wx, rg_bx, rg_lambda, hg_lb, hg_norm_w, w_out, ple_norm_w, w_ple_gate, b_ple_gate, w_ple_proj, final_norm_w, loss_target, m_norm_mix_w, m_w_in, m_conv_w, m_conv_b, m_rg_wa, m_rg_ba, m_rg_wx, m_rg_bx, m_rg_lambda, m_hg_lb, m_hg_norm_w, m_w_out, m_ple_norm_w, m_w_ple_gate, m_b_ple_gate, m_w_ple_proj, m_final_norm_w, v_norm_mix_w, v_w_in, v_conv_w, v_conv_b, v_rg_wa, v_rg_ba, v_rg_wx, v_rg_bx, v_rg_lambda, v_hg_lb, v_hg_norm_w, v_w_out, v_ple_norm_w, v_w_ple_gate, v_b_ple_gate, v_w_ple_proj, v_final_norm_w):
    given = dict(locals())
    x2, p2, tgt = x[0], p[0, 0], loss_target[0]
    hbm = lambda a: pltpu.with_memory_space_constraint(a, pltpu.HBM)
    norm_mix_w, conv_b, rg_ba, rg_bx, rg_lambda, hg_lb, hg_norm_w, ple_norm_w, b_ple_gate = (
        hbm(a) for a in (norm_mix_w, conv_b, rg_ba, rg_bx, rg_lambda, hg_lb, hg_norm_w, ple_norm_w, b_ple_gate))
    wa_c, wx_c = hbm(_small_view("rg_wa", rg_wa)), hbm(_small_view("rg_wx", rg_wx))

    w_in_b, l_out, l_pg, l_pp, cw3, pf, pb, u_b = _gather_inproj(
        x2, norm_mix_w, w_in[0], w_out[0], w_ple_gate[0], w_ple_proj[0], conv_w[0])
    g_ssem, g_rsem, g_lands, tok = _gather_rest_start([l_out, l_pg, l_pp])

    h, ya = _fwd_rglru(pf, pb, cw3, conv_b, wa_c, rg_ba, wx_c, rg_bx, rg_lambda, tok)
    yb, o, s_prev = _fwd_hgrn2(pf, pb, hg_lb, hg_norm_w)
    w_out_b, w_pg_b, w_pp_b = _gather_rest_wait(g_ssem, g_rsem, g_lands, yb)
    dh1, dyab, dwo_b, dwg_b, dwp_b, sm_tail = _tail_fwd_bwd(
        x2, p2, tgt, ya, yb, w_out_b.reshape(D, D), w_pg_b.reshape(D, D), w_pp_b,
        ple_norm_w, b_ple_gate, final_norm_w.reshape(1, D))

    QH = D // NSHARD // 2
    r1 = _rs_start("rs_start_tail", [dwo_b, dwg_b, dwp_b], (0, 1, 2, 3),
                   [_landing(QH, D), _landing(QH, D), _landing(PLE // 2, PLE)])
    d_b, sm_b = _bwd_hgrn2(pf, pb, o, s_prev, dyab, hg_lb, hg_norm_w, r1[4])
    dw23 = _bwd_inproj_dw23(u_b, d_b)
    d_a, rg_c, sm_a = _bwd_rglru(pf, pb, h, dyab, cw3, conv_b, wa_c, rg_ba, wx_c, rg_bx, rg_lambda, dw23)
    dw_in = _bwd_inproj_dw01(u_b, d_a, d_b, dw23)
    r2 = _rs_start("rs_start_in", [dw_in], (0, 1, 2, 3), [lax.empty((NSHARD, D // 2, SHW), BF16)], chip_sums=True)
    grad_x, sm_in = _bwd_inproj_dx(x2, dh1, d_a, d_b, w_in_b, norm_mix_w, r2[4])

    parts1, lands1 = _rs_wait("rs_wait_tail", r1[0], r1[1], r1[2], (0, 1, 2, 3), r1[3], sm_in)
    parts2, lands2 = _rs_wait("rs_wait_in", r2[0], r2[1], r2[2], (0, 1, 2, 3), r2[3], sm_in, chip_sums=True)
    g_big, vred, rgred = _final_reduce(parts2 + parts1, lands2 + lands1, sm_in, sm_tail, sm_a, sm_b, rg_c)

    upd_big = _adam_big(g_big, [given[n][0] for n in _BIG], [given["m_" + n][0] for n in _BIG],
                        [given["v_" + n][0] for n in _BIG])
    small = _adam_small(vred, rgred,
                        {n: _small_view(n, given[n]) for n in _SMALL_ORDER},
                        {n: _small_view(n, given["m_" + n]) for n in _SMALL_ORDER},
                        {n: _small_view(n, given["v_" + n]) for n in _SMALL_ORDER})

    loss = vred[0, ROW_LOSS, 0]
    outs = [loss, grad_x[None]]
    for ki in range(4):
        for n in _OUT_ORDER:
            if n in _BIG:
                i = _BIG.index(n)
                a = g_big[i] if ki == 0 else upd_big[i][ki - 1]
                outs.append(a[None])
            else:
                outs.append(small[n][ki].reshape(given[n].shape))
    return tuple(outs)
```

```python
import jax
import jax.numpy as jnp
from jax import lax
from jax.experimental import pallas as pl
from jax.experimental.pallas import tpu as pltpu

F32 = jnp.float32
BF16 = jnp.bfloat16
I32 = jnp.int32
MESH = pl.DeviceIdType.MESH
HIGHEST = lax.Precision.HIGHEST

T = 4096
D = 1024
DG = 512
DIN = 3072
PLE = 256
NH = 4
HD = 128
CH = 64
NCHUNK = T // CH
RGB = 64
EPS = 1e-6
RG_C = 8.0
NSHARD = 4
SHW = DIN // NSHARD
NDEV = 8

ADAM_LR = 0.001
ADAM_B1 = 0.9
ADAM_B2 = 0.999
ADAM_EPS = 1e-08
ADAM_WD = 0.01
ADAM_STEP = 10

VMEM_SPEC = pl.BlockSpec(memory_space=pltpu.VMEM)
HBM_SPEC = pl.BlockSpec(memory_space=pltpu.HBM)
SEM_SPEC = pl.BlockSpec(memory_space=pltpu.SEMAPHORE)
ANY_SPEC = pl.BlockSpec(memory_space=pl.ANY)
EFFECT = pltpu.SideEffectType.DATAFLOW_SIDE_EFFECTING
MIB = 1024 * 1024

VROWS = 16
ROW_NORM_MIX, ROW_FINAL_NORM, ROW_B_PG, ROW_PLE_NORM = 0, 1, 2, 3
ROW_CB_BA, ROW_BX_LAM, ROW_CW01, ROW_CW23, ROW_HG_LB, ROW_HG_NW, ROW_LOSS = 4, 5, 6, 7, 8, 9, 10


def _mm(a, b):
    return jnp.dot(a.astype(BF16), b.astype(BF16), preferred_element_type=F32)


def _mm_nt(a, b):
    return lax.dot_general(a.astype(BF16), b.astype(BF16), (((1,), (1,)), ((), ())),
                           preferred_element_type=F32)


def _mm_tn(a, b):
    return lax.dot_general(a.astype(BF16), b.astype(BF16), (((0,), (0,)), ((), ())),
                           preferred_element_type=F32)


def _mm_exact(a, b):
    bb = b.astype(BF16)
    hi = a.astype(BF16)
    r1 = a - hi.astype(F32)
    mid = r1.astype(BF16)
    lo = (r1 - mid.astype(F32)).astype(BF16)
    return sum(jnp.dot(p, bb, preferred_element_type=F32) for p in (hi, mid, lo))


def _sig_pair(x):
    e = jnp.exp(-jnp.abs(x))
    big = 1.0 / (1.0 + e)
    small = e * big
    pos = x >= 0
    return jnp.where(pos, big, small), jnp.where(pos, small, big)


def _sigmoid(x):
    return 1.0 / (1.0 + jnp.exp(-x))


def _rowsum(v):
    return jnp.sum(v, axis=0, keepdims=True)


def _shift_down(cur, prev8, d, rows):
    rolled = pltpu.roll(cur, d, 0)
    head = jnp.where(rows[0:8] < d, pltpu.roll(prev8, d, 0), rolled[0:8])
    return jnp.concatenate([head, rolled[8:]], axis=0)


def _shift_up(cur, next8, d, rows):
    n = cur.shape[0]
    rolled = pltpu.roll(cur, n - d, 0)
    tail = jnp.where(rows[0:8] >= 8 - d, pltpu.roll(next8, 8 - d, 0), rolled[n - 8:n])
    return jnp.concatenate([rolled[0:n - 8], tail], axis=0)


def _roll_in_groups(v, d):
    n, w = v.shape
    return pltpu.roll(v.reshape(n // 8, 8, w), d, 1).reshape(n, w)


def _cparams(sem, vmem_mib):
    return pltpu.CompilerParams(dimension_semantics=sem, vmem_limit_bytes=vmem_mib * MIB)


def _mesh_pos():
    x, y, c = lax.axis_index("x"), lax.axis_index("y"), lax.axis_index("c")
    chips = [(1 - x, y), (x, 1 - y), (1 - x, 1 - y)]
    return x, y, c, chips


def _remote(src, dst, ssem, rsem, dev):
    return pltpu.make_async_remote_copy(src_ref=src, dst_ref=dst, send_sem=ssem, recv_sem=rsem,
                                        device_id=dev, device_id_type=MESH)


def _gather_w_in(w_in, w_out, w_pg, w_pp, conv_w):
    shapes = [w_in.shape, w_out.shape, w_pg.shape, w_pp.shape]

    def body(win_h, wout_h, wpg_h, wpp_h, cw_h, o_in, o_out, o_pg, o_pp, o_cw,
             win, wout, wpg, wpp, cw, lsem, ssem, rsem):
        x, y, c, chips = _mesh_pos()
        kme = 2 * x + y
        sibling = (x, y, 1 - c)
        fetch = [pltpu.make_async_copy(src, dst, lsem.at[i])
                 for i, (src, dst) in enumerate([(win_h, win), (cw_h, cw), (wout_h, wout), (wpg_h, wpg), (wpp_h, wpp)])]
        for cp in fetch:
            cp.start()

        def cast(src, dst):
            for r0 in range(0, src.shape[0], 128):
                dst[kme, r0:r0 + 128, :] = src[r0:r0 + 128, :].astype(BF16)

        fetch[0].wait()
        cast(win, o_in)
        fetch[1].wait()
        o_cw[kme] = cw[...]

        hrows = D // 2
        mine = pl.ds(pl.multiple_of(c * hrows, 128), hrows)
        other = pl.ds(pl.multiple_of((1 - c) * hrows, 128), hrows)

        def half(k, rows):
            return o_in.at[k, rows]

        na = (x ^ c, y ^ (1 - c), c)
        nb = (x ^ (1 - c), y ^ c, c)
        ka = 2 * na[0] + na[1]
        kb = 2 * nb[0] + nb[1]
        kd = 2 * (1 - x) + (1 - y)
        sends = [_remote(half(kme, mine), half(kme, mine), ssem.at[0], rsem.at[0], na),
                 _remote(half(kme, mine), half(kme, mine), ssem.at[1], rsem.at[1], nb)]
        for j, (px, py) in enumerate(chips):
            sends.append(_remote(o_cw.at[kme], o_cw.at[kme], ssem.at[3 + j], rsem.at[3 + j], (px, py, c)))
        for cp in sends:
            cp.start()
        for i, (src, dst) in enumerate([(wout, o_out), (wpg, o_pg), (wpp, o_pp)]):
            fetch[2 + i].wait()
            cast(src, dst)

        def passed_on(k, sem, dev):
            cp = _remote(half(k, mine), half(k, mine), ssem.at[sem], rsem.at[sem], dev)
            cp.start()
            sends.append(cp)

        _remote(half(ka, mine), half(ka, mine), ssem.at[0], rsem.at[0], na).wait_recv()
        passed_on(ka, 2, nb)
        passed_on(ka, 6, sibling)
        _remote(half(kb, mine), half(kb, mine), ssem.at[1], rsem.at[1], nb).wait_recv()
        passed_on(kb, 7, sibling)
        _remote(half(kd, mine), half(kd, mine), ssem.at[2], rsem.at[2], nb).wait_recv()
        passed_on(kd, 8, sibling)
        _remote(half(kb, other), half(kb, other), ssem.at[6], rsem.at[6], sibling).wait_recv()
        _remote(half(ka, other), half(ka, other), ssem.at[7], rsem.at[7], sibling).wait_recv()
        _remote(half(kd, other), half(kd, other), ssem.at[8], rsem.at[8], sibling).wait_recv()
        for j, (px, py) in enumerate(chips):
            kj = 2 * px + py
            _remote(o_cw.at[kj], o_cw.at[kj], ssem.at[3 + j], rsem.at[3 + j], (px, py, c)).wait_recv()
        for cp in sends:
            cp.wait_send()

    out_shape = [jax.ShapeDtypeStruct((NSHARD,) + s, BF16) for s in shapes]
    out_shape.append(jax.ShapeDtypeStruct((NSHARD,) + conv_w.shape, F32))
    return pl.pallas_call(
        body, name="gather_w_in", out_shape=out_shape,
        in_specs=[HBM_SPEC] * 5, out_specs=[VMEM_SPEC] * 5,
        scratch_shapes=[pltpu.VMEM(a.shape, F32) for a in (w_in, w_out, w_pg, w_pp, conv_w)]
        + [pltpu.SemaphoreType.DMA((5,)), pltpu.SemaphoreType.DMA((9,)), pltpu.SemaphoreType.DMA((9,))],
        compiler_params=pltpu.CompilerParams(vmem_limit_bytes=40 * MIB),
    )(*[pltpu.with_memory_space_constraint(a, pltpu.HBM) for a in (w_in, w_out, w_pg, w_pp, conv_w)])


def _gather_inproj(x, nw, w_in, w_out, w_pg, w_pp, conv_w):
    shapes = [w_in.shape, w_out.shape, w_pg.shape, w_pp.shape]
    tm = 512
    nt = T // tm

    def body(x_h, nw_h, win_h, wout_h, wpg_h, wpp_h, cw_h,
             o_in_h, o_out_h, o_pg_h, o_pp_h, o_cw_h, pf_h, pb_h, u_h,
             win, wout, wpg, wpp, cw, nw_s, xbuf, ubuf, rfbuf, rbbuf, o_in, o_out, o_pg, o_pp, o_cw,
             lsem, xsem, usem, fsem, bsem, osem, ssem, rsem):
        x, y, c, chips = _mesh_pos()
        kme = 2 * x + y
        sibling = (x, y, 1 - c)
        fetch = [pltpu.make_async_copy(src, dst, lsem.at[i]) for i, (src, dst) in enumerate(
            [(win_h, win), (cw_h, cw), (nw_h, nw_s), (wout_h, wout), (wpg_h, wpg), (wpp_h, wpp)])]
        for cp in fetch:
            cp.start()

        def cast(src, dst):
            for r0 in range(0, src.shape[0], 128):
                dst[r0:r0 + 128, :] = src[r0:r0 + 128, :].astype(BF16)

        outs = []

        def write_out(src, dst, i):
            cp = pltpu.make_async_copy(src, dst, osem.at[i])
            cp.start()
            outs.append(cp)

        fetch[0].wait()
        cast(win, o_in.at[kme])
        write_out(o_in.at[kme], o_in_h.at[kme], 0)
        fetch[1].wait()
        o_cw[kme] = cw[...]

        hrows = D // 2
        mine = pl.ds(pl.multiple_of(c * hrows, 128), hrows)
        other = pl.ds(pl.multiple_of((1 - c) * hrows, 128), hrows)

        def half(k, rows):
            return o_in.at[k, rows]

        na = (x ^ c, y ^ (1 - c), c)
        nb = (x ^ (1 - c), y ^ c, c)
        ka = 2 * na[0] + na[1]
        kb = 2 * nb[0] + nb[1]
        kd = 2 * (1 - x) + (1 - y)
        sends = [_remote(half(kme, mine), half(kme, mine), ssem.at[0], rsem.at[0], na),
                 _remote(half(kme, mine), half(kme, mine), ssem.at[1], rsem.at[1], nb)]
        for j, (px, py) in enumerate(chips):
            sends.append(_remote(o_cw.at[kme], o_cw.at[kme], ssem.at[3 + j], rsem.at[3 + j], (px, py, c)))
        for cp in sends:
            cp.start()
        fetch[2].wait()

        def rows_of(t):
            return pl.ds(pl.multiple_of(t * tm, tm), tm)

        def x_copy(t, slot):
            return pltpu.make_async_copy(x_h.at[rows_of(t)], xbuf.at[slot], xsem.at[slot])

        u_out = pltpu.make_async_copy(ubuf, u_h, usem.at[0])

        def f_store(t, slot, kh):
            return pltpu.make_async_copy(rfbuf.at[slot], pf_h.at[rows_of(t), pl.ds(pl.multiple_of(kh * DG, DG), DG)],
                                         fsem.at[slot])

        def b_store_even(t, slot, kh):
            return pltpu.make_async_copy(rbbuf.at[slot, :, 0:256],
                                         pb_h.at[rows_of(t), pl.ds(pl.multiple_of(kh * 2 * DG, 256), 256)],
                                         bsem.at[slot])

        def b_store_odd(t, slot, kh):
            return pltpu.make_async_copy(rbbuf.at[slot],
                                         pb_h.at[rows_of(t), pl.ds(pl.multiple_of(256 + kh * 2 * DG, 256), SHW)],
                                         bsem.at[slot])

        def run_pass(k, first, even):
            kh = k >> 1
            if first:
                x_copy(0, 0).start()

            def tile(t, carry):
                slot = t & 1
                if first:
                    @pl.when(t + 1 < nt)
                    def _():
                        x_copy(t + 1, 1 - slot).start()

                    x_copy(t, slot).wait()

                @pl.when(t >= 2)
                def _():
                    if even:
                        f_store(t - 2, slot, kh).wait()
                        b_store_even(t - 2, slot, kh).wait()
                    else:
                        b_store_odd(t - 2, slot, kh).wait()

                if first:
                    xv = xbuf[slot]
                    s = lax.rsqrt(jnp.mean(xv * xv, axis=-1, keepdims=True) + EPS)
                    u = (xv * s * nw_s[...]).astype(BF16)
                    ubuf[rows_of(t), :] = u
                else:
                    u = ubuf[rows_of(t), :]
                r = jnp.dot(u, o_in[k], preferred_element_type=F32)
                if even:
                    rfbuf[slot] = r[:, 0:DG]
                    rbbuf[slot, :, 0:256] = r[:, DG:SHW].astype(BF16)
                    f_store(t, slot, kh).start()
                    b_store_even(t, slot, kh).start()
                else:
                    rbbuf[slot] = r.astype(BF16)
                    b_store_odd(t, slot, kh).start()
                return carry

            lax.fori_loop(0, nt, tile, 0)
            for t in (nt - 2, nt - 1):
                slot = t & 1
                if even:
                    f_store(t, slot, kh).wait()
                    b_store_even(t, slot, kh).wait()
                else:
                    b_store_odd(t, slot, kh).wait()

        def project(k, first=False):
            @pl.when((k & 1) == 0)
            def _():
                run_pass(k, first, True)

            @pl.when((k & 1) == 1)
            def _():
                run_pass(k, first, False)

        def passed_on(k, sem, dev):
            cp = _remote(half(k, mine), half(k, mine), ssem.at[sem], rsem.at[sem], dev)
            cp.start()
            sends.append(cp)

        project(kme, first=True)
        u_out.start()
        for i, (src, dst, dst_h) in enumerate([(wout, o_out, o_out_h), (wpg, o_pg, o_pg_h), (wpp, o_pp, o_pp_h)]):
            fetch[3 + i].wait()
            cast(src, dst)
            write_out(dst, dst_h.at[kme], 4 + i)
        _remote(half(ka, mine), half(ka, mine), ssem.at[0], rsem.at[0], na).wait_recv()
        passed_on(ka, 2, nb)
        passed_on(ka, 6, sibling)
        _remote(half(kb, mine), half(kb, mine), ssem.at[1], rsem.at[1], nb).wait_recv()
        passed_on(kb, 7, sibling)
        _remote(half(ka, other), half(ka, other), ssem.at[7], rsem.at[7], sibling).wait_recv()
        write_out(o_in.at[ka], o_in_h.at[ka], 1)
        project(ka)
        _remote(half(kd, mine), half(kd, mine), ssem.at[2], rsem.at[2], nb).wait_recv()
        passed_on(kd, 8, sibling)
        _remote(half(kb, other), half(kb, other), ssem.at[6], rsem.at[6], sibling).wait_recv()
        write_out(o_in.at[kb], o_in_h.at[kb], 2)
        project(kb)
        _remote(half(kd, other), half(kd, other), ssem.at[8], rsem.at[8], sibling).wait_recv()
        write_out(o_in.at[kd], o_in_h.at[kd], 3)
        project(kd)
        u_out.wait()
        for j, (px, py) in enumerate(chips):
            kj = 2 * px + py
            _remote(o_cw.at[kj], o_cw.at[kj], ssem.at[3 + j], rsem.at[3 + j], (px, py, c)).wait_recv()
        write_out(o_cw, o_cw_h, 7)
        for cp in outs:
            cp.wait()
        for cp in sends:
            cp.wait_send()

    out_shape = [pltpu.HBM((NSHARD,) + s, BF16) for s in shapes]
    out_shape.append(pltpu.HBM((NSHARD,) + conv_w.shape, F32))
    out_shape += [pltpu.HBM((T, 2 * DG), F32), pltpu.HBM((T, 4 * DG), BF16), pltpu.HBM((T, D), BF16)]
    hbm_args = [pltpu.with_memory_space_constraint(a, pltpu.HBM) for a in (x, nw, w_in, w_out, w_pg, w_pp, conv_w)]
    return pl.pallas_call(
        body, name="gather_inproj", out_shape=out_shape,
        in_specs=[HBM_SPEC] * 7, out_specs=[HBM_SPEC] * 8,
        scratch_shapes=[pltpu.VMEM(a.shape, F32) for a in (w_in, w_out, w_pg, w_pp, conv_w, nw)]
        + [pltpu.VMEM((2, tm, D), F32), pltpu.VMEM((T, D), BF16),
           pltpu.VMEM((2, tm, DG), F32), pltpu.VMEM((2, tm, SHW), BF16),
           pltpu.VMEM((NSHARD,) + w_in.shape, BF16), pltpu.VMEM(w_out.shape, BF16), pltpu.VMEM(w_pg.shape, BF16),
           pltpu.VMEM(w_pp.shape, BF16), pltpu.VMEM((NSHARD,) + conv_w.shape, F32),
           pltpu.SemaphoreType.DMA((6,)), pltpu.SemaphoreType.DMA((2,)), pltpu.SemaphoreType.DMA((2,)),
           pltpu.SemaphoreType.DMA((2,)), pltpu.SemaphoreType.DMA((2,)), pltpu.SemaphoreType.DMA((8,)),
           pltpu.SemaphoreType.DMA((9,)), pltpu.SemaphoreType.DMA((9,))],
        compiler_params=pltpu.CompilerParams(vmem_limit_bytes=48 * MIB),
    )(*hbm_args)


def _gather_rest_start(lands):
    n = len(lands)

    def body(*refs):
        land_in = refs[0:n]
        ssem, rsem = refs[n], refs[n + 1]
        token = refs[2 * n + 2]
        x, y, c, chips = _mesh_pos()
        kme = 2 * x + y
        for p, land in enumerate(land_in):
            hrows = land.shape[1] // 2
            mine = pl.ds(pl.multiple_of(c * hrows, 128), hrows)
            for px, py in chips:
                for pc in range(2):
                    _remote(land.at[kme, mine], land.at[kme, mine], ssem.at[p], rsem.at[p], (px, py, pc)).start()
        token[...] = jnp.zeros_like(token)

    out_shape = ([pltpu.SemaphoreType.DMA((n,)), pltpu.SemaphoreType.DMA((n,))]
                 + [pltpu.HBM(a.shape, a.dtype) for a in lands] + [jax.ShapeDtypeStruct((8, 128), F32)])
    outs = pl.pallas_call(
        body, name="gather_rest_start", out_shape=out_shape,
        in_specs=[HBM_SPEC] * n, out_specs=[SEM_SPEC, SEM_SPEC] + [HBM_SPEC] * n + [VMEM_SPEC],
        input_output_aliases={i: 2 + i for i in range(n)},
        compiler_params=pltpu.CompilerParams(has_side_effects=EFFECT),
    )(*[pltpu.with_memory_space_constraint(a, pltpu.HBM) for a in lands])
    return outs[0], outs[1], list(outs[2:2 + n]), outs[2 + n]


def _gather_rest_wait(ssem, rsem, lands, after):
    n = len(lands)

    def body(*refs):
        land_in = refs[0:n]
        ssem_ref, rsem_ref = refs[n], refs[n + 1]
        x, y, c = lax.axis_index("x"), lax.axis_index("y"), lax.axis_index("c")
        for p, land in enumerate(land_in):
            three = land.at[pl.ds(0, 3)]
            cp = _remote(three, three, ssem_ref.at[p], rsem_ref.at[p], (x, y, c))
            cp.wait_send()
            cp.wait_recv()

    outs = pl.pallas_call(
        body, name="gather_rest_wait", out_shape=[pltpu.HBM(a.shape, a.dtype) for a in lands],
        in_specs=[HBM_SPEC] * n + [SEM_SPEC, SEM_SPEC, ANY_SPEC], out_specs=[HBM_SPEC] * n,
        input_output_aliases={i: i for i in range(n)},
        compiler_params=pltpu.CompilerParams(has_side_effects=EFFECT),
    )(*lands, ssem, rsem, after)
    return list(outs)


def _rs_start(name, parts, ks, lands, chip_sums=False):
    n = len(parts)

    def body(*refs):
        part_in, land_in = refs[0:n], refs[n:2 * n]
        ssem, rsem = refs[2 * n], refs[2 * n + 1]
        token = refs[4 * n + 2]
        x, y, c = lax.axis_index("x"), lax.axis_index("y"), lax.axis_index("c")
        kme = 2 * x + y
        me = 4 * x + 2 * y + c
        for p in range(n):
            hrows = land_in[p].shape[1]
            for i, k in enumerate(ks):
                if chip_sums:
                    @pl.when(kme != k)
                    def _():
                        _remote(part_in[p].at[i], land_in[p].at[kme], ssem.at[p], rsem.at[p], (k // 2, k % 2, c)).start()
                    continue
                for pc in range(2):
                    @pl.when(jnp.logical_or(kme != k, c != pc))
                    def _():
                        _remote(part_in[p].at[i, pl.ds(pc * hrows, hrows)], land_in[p].at[me],
                                ssem.at[p], rsem.at[p], (k // 2, k % 2, pc)).start()
        token[...] = jnp.zeros_like(token)

    arrays = list(parts) + list(lands)
    out_shape = ([pltpu.SemaphoreType.DMA((n,)), pltpu.SemaphoreType.DMA((n,))]
                 + [pltpu.HBM(a.shape, a.dtype) for a in arrays] + [jax.ShapeDtypeStruct((8, 128), F32)])
    outs = pl.pallas_call(
        body, name=name, out_shape=out_shape,
        in_specs=[HBM_SPEC] * (2 * n), out_specs=[SEM_SPEC, SEM_SPEC] + [HBM_SPEC] * (2 * n) + [VMEM_SPEC],
        input_output_aliases={i: 2 + i for i in range(2 * n)},
        compiler_params=pltpu.CompilerParams(has_side_effects=EFFECT),
    )(*[pltpu.with_memory_space_constraint(a, pltpu.HBM) for a in arrays])
    return outs[0], outs[1], list(outs[2:2 + n]), list(outs[2 + n:2 + 2 * n]), outs[2 + 2 * n]


def _rs_wait(name, ssem, rsem, parts, ks, lands, after, chip_sums=False):
    n = len(parts)

    def body(*refs):
        part_in, land_in = refs[0:n], refs[n:2 * n]
        ssem_ref, rsem_ref = refs[2 * n], refs[2 * n + 1]
        x, y, c = lax.axis_index("x"), lax.axis_index("y"), lax.axis_index("c")
        kme = 2 * x + y
        for p in range(n):
            piece = land_in[p].at[0]
            for k in ks:
                for pc in range(1 if chip_sums else 2):
                    mine = (kme == k) if chip_sums else jnp.logical_and(kme == k, c == pc)

                    @pl.when(jnp.logical_not(mine))
                    def _():
                        _remote(piece, piece, ssem_ref.at[p], rsem_ref.at[p], (x, y, c)).wait_send()
            owner = kme == ks[0]
            for k in ks[1:]:
                owner = jnp.logical_or(owner, kme == k)

            @pl.when(owner)
            def _():
                others = land_in[p].at[pl.ds(0, land_in[p].shape[0] - 1)]
                _remote(others, others, ssem_ref.at[p], rsem_ref.at[p], (x, y, c)).wait_recv()

    arrays = list(parts) + list(lands)
    outs = pl.pallas_call(
        body, name=name, out_shape=[pltpu.HBM(a.shape, a.dtype) for a in arrays],
        in_specs=[HBM_SPEC] * (2 * n) + [SEM_SPEC, SEM_SPEC, ANY_SPEC], out_specs=[HBM_SPEC] * (2 * n),
        input_output_aliases={i: i for i in range(2 * n)},
        compiler_params=pltpu.CompilerParams(has_side_effects=EFFECT),
    )(*arrays, ssem, rsem, after)
    return list(outs[0:n]), list(outs[n:2 * n])


def _reduce_exchange(parts, lands):
    shapes = [(2 * l.shape[1], l.shape[2]) for l in lands]
    step = 128

    def body(in01, in23, pout, ppg, ppp, l_in, l_out, l_pg, l_pp, g_in, g_out, g_pg, g_pp, ssem, rsem):
        x, y, c = lax.axis_index("x"), lax.axis_index("y"), lax.axis_index("c")
        kme = 2 * x + y
        me = 4 * x + 2 * y + c
        sibling = (x, y, 1 - c)
        sends = []
        for p, (land, gout) in enumerate([(l_in, g_in), (l_out, g_out), (l_pg, g_pg), (l_pp, g_pp)]):
            hrows = land.shape[1]
            mine0 = pl.multiple_of(c * hrows, step)
            for r0 in range(0, hrows, step):
                rs = pl.ds(mine0 + r0, step)
                if p == 0:
                    own = jnp.where(kme >= 2, in23[kme & 1, r0:r0 + step, :], in01[kme & 1, r0:r0 + step, :])
                    slot = kme
                else:
                    own = (pout, ppg, ppp)[p - 1][kme, rs, :]
                    slot = me
                s = jnp.zeros((step, land.shape[2]), F32)
                for j in range(land.shape[0]):
                    s = s + jnp.where(slot == j, own, land[j, r0:r0 + step, :]).astype(F32)
                gout[rs, :] = s
            mine = pl.ds(mine0, hrows)
            cp = _remote(gout.at[mine], gout.at[mine], ssem.at[p], rsem.at[p], sibling)
            cp.start()
            sends.append(cp)
        for p, (land, gout) in enumerate([(l_in, g_in), (l_out, g_out), (l_pg, g_pg), (l_pp, g_pp)]):
            hrows = land.shape[1]
            other = pl.ds(pl.multiple_of((1 - c) * hrows, step), hrows)
            _remote(gout.at[other], gout.at[other], ssem.at[p], rsem.at[p], sibling).wait_recv()
        for cp in sends:
            cp.wait_send()

    return pl.pallas_call(
        body, name="reduce_exchange", out_shape=[jax.ShapeDtypeStruct(s, F32) for s in shapes],
        in_specs=[VMEM_SPEC] * 9, out_specs=[VMEM_SPEC] * 4,
        scratch_shapes=[pltpu.SemaphoreType.DMA((4,)), pltpu.SemaphoreType.DMA((4,))],
        compiler_params=pltpu.CompilerParams(vmem_limit_bytes=48 * MIB),
    )(*parts, *lands)


def _small_allreduce(sm_in, sm_tail, sm_a, sm_b, rg_c):
    PR = DG // NDEV

    def body(in_ref, tail_ref, a_ref, b_ref, rg_ref, v_out, rg_out, vbuf, vrecv, rgrecv, ssem, rsem):
        x, y, c = lax.axis_index("x"), lax.axis_index("y"), lax.axis_index("c")
        me = 4 * x + 2 * y + c

        def pair(ref, r0, r1):
            return jnp.concatenate([ref[r0:r0 + 1, :], ref[r1:r1 + 1, :]], axis=1)

        rows = {
            ROW_NORM_MIX: in_ref[0:1, :], ROW_FINAL_NORM: tail_ref[1:2, :], ROW_B_PG: tail_ref[2:3, :],
            ROW_PLE_NORM: tail_ref[3:4, :], ROW_CB_BA: pair(a_ref, 0, 1), ROW_BX_LAM: pair(a_ref, 2, 3),
            ROW_CW01: pair(a_ref, 4, 5), ROW_CW23: pair(a_ref, 6, 7), ROW_HG_LB: pair(b_ref, 2, 3),
            ROW_HG_NW: jnp.concatenate([b_ref[1:2, :], jnp.zeros((1, DG), F32)], axis=1),
            ROW_LOSS: tail_ref[4:5, :],
        }
        vbuf[...] = jnp.zeros_like(vbuf)
        for r, row in rows.items():
            for j in range(NDEV):
                vbuf[j, r:r + 1, :] = row[:, j * 128:(j + 1) * 128]

        def peer(mask):
            px = x ^ ((mask >> 2) & 1)
            py = y ^ ((mask >> 1) & 1)
            pc = c ^ (mask & 1)
            return (px, py, pc), 4 * px + 2 * py + pc

        def rg_rows(r):
            return pl.ds(pl.multiple_of(r * PR, PR), PR)

        first = []
        for mask in range(1, NDEV):
            dev, r = peer(mask)
            i = mask - 1
            cp = _remote(vbuf.at[r], vrecv.at[i], ssem.at[i], rsem.at[i], dev)
            cp.start()
            first.append(cp)
            cp = _remote(rg_ref.at[rg_rows(r)], rgrecv.at[i], ssem.at[7 + i], rsem.at[7 + i], dev)
            cp.start()
            first.append(cp)
        sv = vbuf[me]
        sr = rg_ref[rg_rows(me), :]
        for i in range(NDEV - 1):
            first[2 * i].wait_recv()
            first[2 * i + 1].wait_recv()
            sv = sv + vrecv[i]
            sr = sr + rgrecv[i]
        v_out[me] = sv
        rg_out[rg_rows(me), :] = sr
        second = []
        for mask in range(1, NDEV):
            dev, r = peer(mask)
            i = mask - 1
            cp = _remote(v_out.at[me], v_out.at[me], ssem.at[14 + i], rsem.at[14 + i], dev)
            cp.start()
            second.append(cp)
            cp = _remote(rg_out.at[rg_rows(me)], rg_out.at[rg_rows(me)], ssem.at[21 + i], rsem.at[21 + i], dev)
            cp.start()
            second.append(cp)
        for mask in range(1, NDEV):
            dev, r = peer(mask)
            i = mask - 1
            _remote(v_out.at[r], v_out.at[r], ssem.at[14 + i], rsem.at[14 + i], dev).wait_recv()
            _remote(rg_out.at[rg_rows(r)], rg_out.at[rg_rows(r)], ssem.at[21 + i], rsem.at[21 + i], dev).wait_recv()
        for cp in first + second:
            cp.wait_send()

    return pl.pallas_call(
        body, name="small_allreduce",
        out_shape=[jax.ShapeDtypeStruct((NDEV, VROWS, 128), F32), jax.ShapeDtypeStruct((DG, 128), F32)],
        in_specs=[VMEM_SPEC] * 5, out_specs=[VMEM_SPEC] * 2,
        scratch_shapes=[pltpu.VMEM((NDEV, VROWS, 128), F32), pltpu.VMEM((NDEV - 1, VROWS, 128), F32),
                        pltpu.VMEM((NDEV - 1, PR, 128), F32),
                        pltpu.SemaphoreType.DMA((28,)), pltpu.SemaphoreType.DMA((28,))],
    )(sm_in, sm_tail, sm_a, sm_b, rg_c)


def _final_reduce(parts, lands, sm_in, sm_tail, sm_a, sm_b, rg_c):
    shapes = [(2 * l.shape[1], l.shape[2]) for l in lands]
    step = 128
    PR = DG // NDEV

    def body(pin, pout, ppg, ppp, l_in, l_out, l_pg, l_pp, in_h, tail_h, a_h, b_h, rg_h,
             g_in, g_out, g_pg, g_pp, v_out, rg_out, vbuf, vrecv, rgrecv,
             lb_in, lb_out, lb_pg, lb_pp, ob_in, ob_out, ob_pg, ob_pp,
             in_ref, tail_ref, a_ref, b_ref, rg_ref, lsem, ssem_l, bsem_s, bsem_r, ssem, rsem):
        x, y, c = lax.axis_index("x"), lax.axis_index("y"), lax.axis_index("c")
        kme = 2 * x + y
        me = 4 * x + 2 * y + c
        sibling = (x, y, 1 - c)
        small_fetch = [pltpu.make_async_copy(src, dst, ssem_l.at[i]) for i, (src, dst) in enumerate(
            [(in_h, in_ref), (tail_h, tail_ref), (a_h, a_ref), (b_h, b_ref), (rg_h, rg_ref)])]
        for cp in small_fetch:
            cp.start()

        lands_hbm = [l_in, l_out, l_pg, l_pp]
        land_bufs = [lb_in, lb_out, lb_pg, lb_pp]
        own_bufs = [ob_in, ob_out, ob_pg, ob_pp]
        fetches = []
        for p in range(4):
            cp = pltpu.make_async_copy(lands_hbm[p], land_bufs[p], lsem.at[p])
            cp.start()
            fetches.append(cp)

        own_fetches = [pltpu.make_async_copy(pin.at[kme], ob_in, lsem.at[4])]
        own_fetches[0].start()
        for p, part in enumerate([pout, ppg, ppp]):
            hrows = own_bufs[p + 1].shape[0]
            cp = pltpu.make_async_copy(part.at[kme, pl.ds(pl.multiple_of(c * hrows, step), hrows)],
                                       own_bufs[p + 1], lsem.at[5 + p])
            cp.start()
            own_fetches.append(cp)

        def pair(ref, r0, r1):
            return jnp.concatenate([ref[r0:r0 + 1, :], ref[r1:r1 + 1, :]], axis=1)

        for cp in small_fetch:
            cp.wait()
        rows = {
            ROW_NORM_MIX: in_ref[0:1, :], ROW_FINAL_NORM: tail_ref[1:2, :], ROW_B_PG: tail_ref[2:3, :],
            ROW_PLE_NORM: tail_ref[3:4, :], ROW_CB_BA: pair(a_ref, 0, 1), ROW_BX_LAM: pair(a_ref, 2, 3),
            ROW_CW01: pair(a_ref, 4, 5), ROW_CW23: pair(a_ref, 6, 7), ROW_HG_LB: pair(b_ref, 2, 3),
            ROW_HG_NW: jnp.concatenate([b_ref[1:2, :], jnp.zeros((1, DG), F32)], axis=1),
            ROW_LOSS: tail_ref[4:5, :],
        }
        vbuf[...] = jnp.zeros_like(vbuf)
        for r, row in rows.items():
            for j in range(NDEV):
                vbuf[j, r:r + 1, :] = row[:, j * 128:(j + 1) * 128]

        def peer(mask):
            px = x ^ ((mask >> 2) & 1)
            py = y ^ ((mask >> 1) & 1)
            pc = c ^ (mask & 1)
            return (px, py, pc), 4 * px + 2 * py + pc

        def rg_rows(r):
            return pl.ds(pl.multiple_of(r * PR, PR), PR)

        first = []
        for mask in range(1, NDEV):
            dev, r = peer(mask)
            i = mask - 1
            cp = _remote(vbuf.at[r], vrecv.at[i], ssem.at[i], rsem.at[i], dev)
            cp.start()
            first.append(cp)
            cp = _remote(rg_ref.at[rg_rows(r)], rgrecv.at[i], ssem.at[7 + i], rsem.at[7 + i], dev)
            cp.start()
            first.append(cp)

        big = [(lb_in, g_in), (lb_out, g_out), (lb_pg, g_pg), (lb_pp, g_pp)]
        swaps = []
        for p, (land, gout) in enumerate(big):
            fetches[p].wait()
            own_fetches[p].wait()
            hrows = land.shape[1]
            mine0 = pl.multiple_of(c * hrows, step)
            slot = kme if p == 0 else me
            for r0 in range(0, hrows, step):
                rs = pl.ds(mine0 + r0, step)
                own = own_bufs[p][r0:r0 + step, :]
                s = jnp.zeros((step, land.shape[2]), F32)
                for j in range(land.shape[0]):
                    s = s + jnp.where(slot == j, own, land[j, r0:r0 + step, :]).astype(F32)
                gout[rs, :] = s
            mine = pl.ds(mine0, hrows)
            cp = _remote(gout.at[mine], gout.at[mine], bsem_s.at[p], bsem_r.at[p], sibling)
            cp.start()
            swaps.append(cp)

        sv = vbuf[me]
        sr = rg_ref[rg_rows(me), :]
        for i in range(NDEV - 1):
            first[2 * i].wait_recv()
            first[2 * i + 1].wait_recv()
            sv = sv + vrecv[i]
            sr = sr + rgrecv[i]
        v_out[me] = sv
        rg_out[rg_rows(me), :] = sr
        second = []
        for mask in range(1, NDEV):
            dev, r = peer(mask)
            i = mask - 1
            cp = _remote(v_out.at[me], v_out.at[me], ssem.at[14 + i], rsem.at[14 + i], dev)
            cp.start()
            second.append(cp)
            cp = _remote(rg_out.at[rg_rows(me)], rg_out.at[rg_rows(me)], ssem.at[21 + i], rsem.at[21 + i], dev)
            cp.start()
            second.append(cp)
        for p, (land, gout) in enumerate(big):
            hrows = land.shape[1]
            other = pl.ds(pl.multiple_of((1 - c) * hrows, step), hrows)
            _remote(gout.at[other], gout.at[other], bsem_s.at[p], bsem_r.at[p], sibling).wait_recv()
        for mask in range(1, NDEV):
            dev, r = peer(mask)
            i = mask - 1
            _remote(v_out.at[r], v_out.at[r], ssem.at[14 + i], rsem.at[14 + i], dev).wait_recv()
            _remote(rg_out.at[rg_rows(r)], rg_out.at[rg_rows(r)], ssem.at[21 + i], rsem.at[21 + i], dev).wait_recv()
        for cp in first + swaps + second:
            cp.wait_send()

    out_shape = [jax.ShapeDtypeStruct(s, F32) for s in shapes]
    out_shape += [jax.ShapeDtypeStruct((NDEV, VROWS, 128), F32), jax.ShapeDtypeStruct((DG, 128), F32)]
    outs = pl.pallas_call(
        body, name="final_reduce", out_shape=out_shape,
        in_specs=[HBM_SPEC] * 13, out_specs=[VMEM_SPEC] * 6,
        scratch_shapes=[pltpu.VMEM((NDEV, VROWS, 128), F32), pltpu.VMEM((NDEV - 1, VROWS, 128), F32),
                        pltpu.VMEM((NDEV - 1, PR, 128), F32)]
        + [pltpu.VMEM(l.shape, BF16) for l in lands]
        + [pltpu.VMEM(l.shape[1:], BF16) for l in lands]
        + [pltpu.VMEM(a.shape, F32) for a in (sm_in, sm_tail, sm_a, sm_b, rg_c)]
        + [pltpu.SemaphoreType.DMA((8,)), pltpu.SemaphoreType.DMA((5,)),
                        pltpu.SemaphoreType.DMA((4,)), pltpu.SemaphoreType.DMA((4,)),
                        pltpu.SemaphoreType.DMA((28,)), pltpu.SemaphoreType.DMA((28,))],
        compiler_params=pltpu.CompilerParams(vmem_limit_bytes=48 * MIB),
    )(*[pltpu.with_memory_space_constraint(a, pltpu.HBM)
        for a in (*parts, *lands, sm_in, sm_tail, sm_a, sm_b, rg_c)])
    return list(outs[0:4]), outs[4], outs[5]


def _adam_rows(w, g, m, v):
    m2 = ADAM_B1 * m + (1.0 - ADAM_B1) * g
    v2 = ADAM_B2 * v + (1.0 - ADAM_B2) * (g * g)
    m_hat = m2 / (1.0 - ADAM_B1 ** ADAM_STEP)
    v_hat = v2 / (1.0 - ADAM_B2 ** ADAM_STEP)
    delta = -ADAM_LR * (m_hat / (jnp.sqrt(v_hat) + ADAM_EPS) + ADAM_WD * w)
    return delta, m2, v2


def _adam_big(gs, ws, ms, vs):
    n = len(gs)
    steps = 8

    def body(*refs):
        ins, outs = refs[:4 * n], refs[4 * n:]
        for i in range(n):
            g, w, m, v = (r[...] for r in ins[4 * i:4 * i + 4])
            d, m2, v2 = _adam_rows(w, g, m, v)
            outs[3 * i][...] = d
            outs[3 * i + 1][...] = m2
            outs[3 * i + 2][...] = v2

    in_specs, out_specs, out_shape, args = [], [], [], []
    for g, w, m, v in zip(gs, ws, ms, vs):
        r, c = w.shape
        spec = lambda: pl.BlockSpec((r // steps, c), lambda i: (i, 0))
        in_specs += [spec() for _ in range(4)]
        out_specs += [spec() for _ in range(3)]
        out_shape += [jax.ShapeDtypeStruct((r, c), F32)] * 3
        args += [pltpu.with_memory_space_constraint(a, pltpu.HBM) for a in (g, w, m, v)]
    outs = pl.pallas_call(
        body, name="adam_big", grid=(steps,), in_specs=in_specs, out_specs=out_specs, out_shape=out_shape,
        compiler_params=_cparams(("parallel",), 32),
    )(*args)
    return [tuple(outs[3 * i:3 * i + 3]) for i in range(n)]


_VEC_PARAMS = [
    ("norm_mix_w", ROW_NORM_MIX, 0, D), ("final_norm_w", ROW_FINAL_NORM, 0, D),
    ("b_ple_gate", ROW_B_PG, 0, D), ("ple_norm_w", ROW_PLE_NORM, 0, D),
    ("conv_b", ROW_CB_BA, 0, DG), ("rg_ba", ROW_CB_BA, DG, DG),
    ("rg_bx", ROW_BX_LAM, 0, DG), ("rg_lambda", ROW_BX_LAM, DG, DG),
    ("hg_norm_w", ROW_HG_NW, 0, HD),
]
_SMALL_ORDER = [n for n, _, _, _ in _VEC_PARAMS] + ["hg_lb", "conv_w", "rg_wa", "rg_wx"]


def _adam_small(vred, rgred, ws, ms, vs):
    names = _SMALL_ORDER
    n = len(names)

    def body(vred_ref, rg_ref, *refs):
        w_refs = dict(zip(names, refs[0:n]))
        m_refs = dict(zip(names, refs[n:2 * n]))
        v_refs = dict(zip(names, refs[2 * n:3 * n]))
        outs = refs[3 * n:]
        o_refs = {nm: outs[4 * i:4 * i + 4] for i, nm in enumerate(names)}
        kme = 2 * lax.axis_index("x") + lax.axis_index("y")

        def update(nm, g, idx):
            d, m2, v2 = _adam_rows(w_refs[nm][idx], g, m_refs[nm][idx], v_refs[nm][idx])
            og, od, om, ov = o_refs[nm]
            og[idx] = g
            od[idx] = d
            om[idx] = m2
            ov[idx] = v2

        def packed(row, lane0, width):
            return jnp.concatenate([vred_ref[j, row:row + 1, :] for j in range(lane0 // 128, (lane0 + width) // 128)],
                                   axis=1)

        everything = (slice(None), slice(None))
        for nm, row, lane0, width in _VEC_PARAMS:
            update(nm, packed(row, lane0, width), everything)
        for r in range(2):
            update("hg_lb", packed(ROW_HG_LB, r * DG, DG), (slice(r, r + 1), slice(None)))
        for j in range(4):
            g = vred_ref[(j % 2) * 4 + kme, ROW_CW01 + j // 2:ROW_CW01 + j // 2 + 1, :]
            update("conv_w", g, (slice(j, j + 1), slice(None)))
        for r0 in range(0, DG, 128):
            rs = (slice(r0, r0 + 128), slice(None))
            both = rg_ref[r0:r0 + 128, :]
            update("rg_wa", both[:, 0:RGB], rs)
            update("rg_wx", pltpu.roll(both, RGB, 1)[:, 0:RGB], rs)

    args = [vred, rgred] + [d[nm] for d in (ws, ms, vs) for nm in names]
    out_shape = []
    for nm in names:
        out_shape += [jax.ShapeDtypeStruct(ws[nm].shape, F32)] * 4
    whole = lambda s: pl.BlockSpec(s.shape, lambda i, nd=len(s.shape): (0,) * nd)
    outs = pl.pallas_call(
        body, name="adam_small", out_shape=out_shape, grid=(1,),
        in_specs=[whole(a) for a in args], out_specs=[whole(s) for s in out_shape],
    )(*args)
    return {nm: tuple(outs[4 * i:4 * i + 4]) for i, nm in enumerate(names)}


def _fwd_inproj(x, nw, w_in_b, dep):
    tm = 512

    def body(x_ref, nw_ref, w_ref, dep_ref, pf_ref, pb_ref, u_ref):
        xv = x_ref[...]
        s = lax.rsqrt(jnp.mean(xv * xv, axis=-1, keepdims=True) + EPS)
        u = (xv * s * nw_ref[...]).astype(BF16)
        u_ref[...] = u
        r = [jnp.dot(u, w_ref[k], preferred_element_type=F32) for k in range(NSHARD)]
        h = DG // 2
        pf_ref[:, 0:DG] = r[0][:, 0:DG]
        pb_ref[:, 0:h] = r[0][:, DG:SHW].astype(BF16)
        pb_ref[:, h:DG] = r[1][:, 0:h].astype(BF16)
        pb_ref[:, DG:2 * DG] = r[1][:, h:SHW].astype(BF16)
        pf_ref[:, DG:2 * DG] = r[2][:, 0:DG]
        pb_ref[:, 2 * DG:2 * DG + h] = r[2][:, DG:SHW].astype(BF16)
        pb_ref[:, 2 * DG + h:3 * DG] = r[3][:, 0:h].astype(BF16)
        pb_ref[:, 3 * DG:4 * DG] = r[3][:, h:SHW].astype(BF16)

    return pl.pallas_call(
        body, name="fwd_inproj", grid=(T // tm,),
        in_specs=[pl.BlockSpec((tm, D), lambda i: (i, 0)),
                  pl.BlockSpec((1, D), lambda i: (0, 0)),
                  pl.BlockSpec((NSHARD, D, SHW), lambda i: (0, 0, 0), pipeline_mode=pl.Buffered(1)), ANY_SPEC],
        out_specs=[pl.BlockSpec((tm, 2 * DG), lambda i: (i, 0)),
                   pl.BlockSpec((tm, 4 * DG), lambda i: (i, 0)),
                   pl.BlockSpec((tm, D), lambda i: (i, 0))],
        out_shape=[jax.ShapeDtypeStruct((T, 2 * DG), F32), jax.ShapeDtypeStruct((T, 4 * DG), BF16),
                   jax.ShapeDtypeStruct((T, D), BF16)],
        compiler_params=_cparams(("parallel",), 48),
    )(x, nw, w_in_b, dep)


def _conv_rows(cw_ref):
    return [jnp.concatenate([cw_ref[k, j:j + 1, :] for k in range(NSHARD)], axis=1) for j in range(4)]


def _rg_conv(xa, prev8, cw, cb, rows):
    taps = [_shift_down(xa, prev8, 3, rows), _shift_down(xa, prev8, 2, rows),
            _shift_down(xa, prev8, 1, rows), xa]
    xc = cb
    for j in range(4):
        xc = xc + taps[j] * cw[j]
    return xc, taps


def _block_mask():
    r = lax.broadcasted_iota(I32, (DG, DG), 0)
    c = lax.broadcasted_iota(I32, (DG, DG), 1)
    return (r >> 6) == (c >> 6)


def _dense_from_blocks(wc):
    j = lax.broadcasted_iota(I32, (RGB, DG), 0)
    c = lax.broadcasted_iota(I32, (RGB, DG), 1)
    spread = _mm_exact(wc, ((c & (RGB - 1)) == j).astype(F32))
    return jnp.where(_block_mask(), spread, 0.0)


def _blocks_from_dense(da, dx):
    c = lax.broadcasted_iota(I32, (DG, 128), 0)
    j = lax.broadcasted_iota(I32, (DG, 128), 1)
    hit = (c & (RGB - 1)) == (j & (RGB - 1))
    mask = _block_mask()
    return (_mm_exact(jnp.where(mask, da, 0.0), (hit & (j < RGB)).astype(F32))
            + _mm_exact(jnp.where(mask, dx, 0.0), (hit & (j >= RGB)).astype(F32)))


def _rg_gates(xc, wa, ba, wx, bx, sp, first_row):
    r = _sigmoid(_mm(xc, wa) + ba)
    i = _sigmoid(_mm(xc, wx) + bx)
    log_a = (-RG_C) * r * sp
    a = jnp.exp(log_a)
    a2 = a * a
    one_m_a2 = -jnp.tanh(log_a) * (a2 + 1.0)
    mult = jnp.where(first_row, 1.0, jnp.sqrt(one_m_a2))
    return r, i, a, a2, mult


def _softplus(z):
    return jnp.maximum(z, 0.0) + jnp.log1p(jnp.exp(-jnp.abs(z)))


def _fwd_rglru(pf, pb, cw3, conv_b, wa_c, ba, wx_c, bx, lam, dep):
    tm = 512
    ng = tm // 8

    def body(xa_ref, ga_ref, cw_ref, cb_ref, wa_ref, ba_ref, wx_ref, bx_ref, lam_ref, dep_ref,
             h_ref, ya_ref, a_s, u_s, tail_s, hc_s, wa_s, wx_s):
        i = pl.program_id(0)

        @pl.when(i == 0)
        def _():
            tail_s[...] = jnp.zeros_like(tail_s)
            hc_s[...] = jnp.zeros_like(hc_s)
            wa_s[...] = _dense_from_blocks(wa_ref[...]).astype(BF16)
            wx_s[...] = _dense_from_blocks(wx_ref[...]).astype(BF16)

        rows = lax.broadcasted_iota(I32, (tm, DG), 0)
        xa = xa_ref[...]
        xc, _ = _rg_conv(xa, tail_s[...], _conv_rows(cw_ref), cb_ref[...], rows)
        tail_s[...] = xa[tm - 8:tm, :]
        sp = _softplus(-lam_ref[...])
        rp = _mm(xc, wa_s[...]) + ba_ref[...]
        ip = _mm(xc, wx_s[...]) + bx_ref[...]
        rb = 64
        rows_b = lax.broadcasted_iota(I32, (rb, DG), 0)
        rows8 = rows_b & 7
        carry = hc_s[0:1, :]
        for b0 in range(0, tm, rb):
            sl = slice(b0, b0 + rb)
            r = _sigmoid(rp[sl])
            ig = _sigmoid(ip[sl])
            log_a = (-RG_C) * r * sp
            av = jnp.exp(log_a)
            mult = jnp.sqrt(-jnp.tanh(log_a) * (av * av + 1.0))
            if b0 == 0:
                mult = jnp.where((rows_b + i * tm) == 0, 1.0, mult)
            uv = mult * (ig * xc[sl])
            for d in (1, 2, 4):
                keep = rows8 >= d
                uv = uv + av * jnp.where(keep, _roll_in_groups(uv, d), 0.0)
                av = av * jnp.where(keep, _roll_in_groups(av, d), 1.0)
            ga = ga_ref[sl, :].astype(F32)
            gate = ga * _sigmoid(ga)
            hs = []
            for g in range(rb // 8):
                gs = slice(g * 8, (g + 1) * 8)
                hv = uv[gs] + av[gs] * carry
                carry = hv[7:8, :]
                hs.append(hv)
            hb = jnp.concatenate(hs, axis=0)
            h_ref[sl, :] = hb
            ya_ref[sl, :] = (hb * gate).astype(BF16)
        hc_s[0:1, :] = carry

    vec = lambda: pl.BlockSpec((1, DG), lambda i: (0, 0))
    blocks = lambda: pl.BlockSpec((DG, RGB), lambda i: (0, 0))
    return pl.pallas_call(
        body, name="fwd_rglru", grid=(T // tm,),
        in_specs=[pl.BlockSpec((tm, DG), lambda i: (i, 0)),
                  pl.BlockSpec((tm, DG), lambda i: (i, 0)),
                  pl.BlockSpec((NSHARD, 4, 128), lambda i: (0, 0, 0)), vec(),
                  blocks(), vec(), blocks(), vec(), vec(), ANY_SPEC],
        out_specs=[pl.BlockSpec((tm, DG), lambda i: (i, 0)),
                   pl.BlockSpec((tm, DG), lambda i: (i, 0))],
        out_shape=[jax.ShapeDtypeStruct((T, DG), F32), pltpu.HBM((T, DG), BF16)],
        scratch_shapes=[pltpu.VMEM((tm, DG), F32), pltpu.VMEM((tm, DG), F32),
                        pltpu.VMEM((8, DG), F32), pltpu.VMEM((8, DG), F32),
                        pltpu.VMEM((DG, DG), BF16), pltpu.VMEM((DG, DG), BF16)],
        compiler_params=_cparams(("arbitrary",), 48),
    )(pf, pb, cw3, conv_b, wa_c, ba, wx_c, bx, lam, dep)


def _hg_lower_bound(lb_ref):
    return _sig_pair(lb_ref[0:1, :] - lb_ref[1:2, :])


def _hg_gates(fz, lb, one_m_lb):
    sg, sn = _sig_pair(fz)
    f = lb + one_m_lb * sg
    return sg, sn, f, jnp.log(f), one_m_lb * sn


def _tri(lower):
    r = lax.broadcasted_iota(I32, (CH, CH), 0)
    c = lax.broadcasted_iota(I32, (CH, CH), 1)
    return (r >= c) if lower else (r <= c)


def _split3(v):
    hi = v.astype(BF16)
    r1 = v - hi.astype(F32)
    mid = r1.astype(BF16)
    lo = (r1 - mid.astype(F32)).astype(BF16)
    return hi, mid, lo


def _chunk_cumsum(v, rows64, reverse=False):
    del rows64
    tri = _tri(not reverse).astype(BF16)
    out = []
    for c in range(v.shape[0] // CH):
        pieces = _split3(v[c * CH:(c + 1) * CH])
        out.append(sum(jnp.dot(tri, p, preferred_element_type=F32) for p in pieces))
    return jnp.concatenate(out, axis=0)


def _chunk_rev_cumsum(v, rows64):
    return _chunk_cumsum(v, rows64, reverse=True)


def _hg_recompute(q_ref, f_ref, lb, one_m_lb, rows64, eb_s, enb_s, ekd_s, qe_s, ke_s, kd_s, dec_s):
    nc = q_ref.shape[0] // CH
    sg, sn, f, logf, k = _hg_gates(f_ref[...], lb, one_m_lb)
    q = q_ref[...].astype(F32)
    sq = _sigmoid(q)
    qs = q * sq * (HD ** -0.5)
    b = _chunk_cumsum(logf, rows64)
    for c in range(nc):
        rs = slice(c * CH, (c + 1) * CH)
        b_c = b[rs]
        bl = b_c[CH - 1:CH, :]
        eb, enb, ekd = jnp.exp(b_c), jnp.exp(-b_c), jnp.exp(bl - b_c)
        if eb_s is not None:
            eb_s[rs, :] = eb
            enb_s[rs, :] = enb
            ekd_s[rs, :] = ekd
        qe_s[rs, :] = (qs[rs] * eb).astype(BF16)
        ke_s[rs, :] = (k[rs] * enb).astype(BF16)
        kd_s[rs, :] = (k[rs] * ekd).astype(BF16)
        dec_s[c:c + 1, :] = jnp.exp(bl)
    return sg, sn, f, k, q, sq, qs


def _fwd_hgrn2(pf, pb, hg_lb, hg_nw):
    tm = 512
    nc = tm // CH

    def body(q_ref, f_ref, v_ref, g_ref, lb_ref, nw_ref, yb_ref, o_ref, sp_ref,
             st_s, qe_s, ke_s, kd_s, vb_s, dec_s, p_s, ds_s):
        i = pl.program_id(0)

        @pl.when(i == 0)
        def _():
            st_s[...] = jnp.zeros_like(st_s)

        lb, one_m_lb = _hg_lower_bound(lb_ref)
        rows64 = lax.broadcasted_iota(I32, (tm, DG), 0) & (CH - 1)
        _hg_recompute(q_ref, f_ref, lb, one_m_lb, rows64, None, None, None, qe_s, ke_s, kd_s, dec_s)
        vb_s[...] = v_ref[...]
        mask = _tri(True)
        items = [(c, hd, slice(c * CH, (c + 1) * CH), slice(hd * HD, (hd + 1) * HD))
                 for c in range(nc) for hd in range(NH)]
        for c, hd, rs, cols in items:
            p_s[c * NH + hd] = jnp.where(mask, _mm_nt(qe_s[rs, cols], ke_s[rs, cols]), 0.0).astype(BF16)
            ds_s[c * NH + hd] = _mm_tn(vb_s[rs, cols], kd_s[rs, cols])
        for c, hd, rs, cols in items:
            st = st_s[hd]
            sp_ref[hd, c] = st
            st_s[hd] = st * dec_s[c:c + 1, cols] + ds_s[c * NH + hd]
        for c, hd, rs, cols in items:
            o_ref[rs, cols] = _mm(p_s[c * NH + hd], vb_s[rs, cols]) + _mm_nt(qe_s[rs, cols], sp_ref[hd, c])
        nw = nw_ref[...]
        for hd in range(NH):
            cols = slice(hd * HD, (hd + 1) * HD)
            o = o_ref[:, cols]
            so = lax.rsqrt(jnp.mean(o * o, axis=-1, keepdims=True) + EPS)
            g = g_ref[:, cols].astype(F32)
            sg = _sigmoid(g)
            yb_ref[:, cols] = (o * so * nw * (g * sg)).astype(BF16)

    col = lambda j: pl.BlockSpec((tm, DG), lambda i: (i, j))
    return pl.pallas_call(
        body, name="fwd_hgrn2", grid=(T // tm,),
        in_specs=[col(1), col(1), col(2), col(3),
                  pl.BlockSpec((2, DG), lambda i: (0, 0)),
                  pl.BlockSpec((1, HD), lambda i: (0, 0))],
        out_specs=[pl.BlockSpec((tm, DG), lambda i: (i, 0)),
                   pl.BlockSpec((tm, DG), lambda i: (i, 0)),
                   pl.BlockSpec((NH, nc, HD, HD), lambda i: (0, i, 0, 0))],
        out_shape=[pltpu.HBM((T, DG), BF16), jax.ShapeDtypeStruct((T, DG), F32),
                   jax.ShapeDtypeStruct((NH, NCHUNK, HD, HD), F32)],
        scratch_shapes=[pltpu.VMEM((NH, HD, HD), F32),
                        pltpu.VMEM((tm, DG), BF16), pltpu.VMEM((tm, DG), BF16), pltpu.VMEM((tm, DG), BF16),
                        pltpu.VMEM((tm, DG), BF16), pltpu.VMEM((nc, DG), F32),
                        pltpu.VMEM((nc * NH, CH, CH), BF16), pltpu.VMEM((nc * NH, HD, HD), F32)],
        compiler_params=_cparams(("arbitrary",), 48),
    )(pb, pf, pb, pb, hg_lb, hg_nw)


def _tail_fwd_bwd(x, p, tgt, ya, yb, w_out_b, w_pg_b, w_pp_b, ple_nw, b_pg, fnw):
    tm = 512
    nt = T // tm
    QR = D // NSHARD

    def body(x_ref, p_ref, t_ref, ya_ref, yb_ref, wo_ref, wg_ref, wp_ref, pw_ref, b_ref, fw_ref,
             dh1_ref, dyab_ref, dwo_ref, dwg_ref, dwp_ref, sm_ref, dwo_s, dwg_s, dwp_s):
        i = pl.program_id(0)

        @pl.when(i == 0)
        def _():
            dwo_s[...] = jnp.zeros_like(dwo_s)
            dwg_s[...] = jnp.zeros_like(dwg_s)
            dwp_s[...] = jnp.zeros_like(dwp_s)
            sm_ref[...] = jnp.zeros_like(sm_ref)

        ya = ya_ref[...]
        yb = yb_ref[...]
        pv = p_ref[...].astype(BF16)
        pw = pw_ref[...]
        fw = fw_ref[...]
        h1 = x_ref[...] + _mm(ya, wo_ref[0:DG, :]) + _mm(yb, wo_ref[DG:D, :])
        s2 = lax.rsqrt(jnp.mean(h1 * h1, axis=-1, keepdims=True) + EPS)
        n2h = h1 * s2
        n2 = (n2h * pw).astype(BF16)
        z = _mm(n2, wg_ref[...]) + b_ref[...]
        gate = _sigmoid(z)
        pp = jnp.concatenate([_mm(pv, wp_ref[k]) for k in range(NSHARD)], axis=1)
        h2 = h1 + gate * pp
        s3 = lax.rsqrt(jnp.mean(h2 * h2, axis=-1, keepdims=True) + EPS)
        hn = h2 * s3
        err = hn * fw - t_ref[...]
        sm_ref[0:1, :] += _rowsum(err * err)
        dy = err * (1.0 / D)
        sm_ref[1:2, :] += _rowsum(dy * hn)
        g3 = dy * fw
        dh2 = s3 * (g3 - hn * jnp.mean(g3 * hn, axis=-1, keepdims=True))
        dpp = (dh2 * gate).astype(BF16)
        dz = dh2 * pp * gate * (1.0 - gate)
        sm_ref[2:3, :] += _rowsum(dz)
        dzb = dz.astype(BF16)
        dwg_s[...] += _mm_tn(n2, dzb)
        dn2 = _mm_nt(dzb, wg_ref[...])
        for k in range(NSHARD):
            dwp_s[k] += _mm_tn(pv, dpp[:, k * PLE:(k + 1) * PLE])
        sm_ref[3:4, :] += _rowsum(dn2 * n2h)
        g2 = dn2 * pw
        dh1 = dh2 + s2 * (g2 - n2h * jnp.mean(g2 * n2h, axis=-1, keepdims=True))
        dh1_ref[...] = dh1
        dh1b = dh1.astype(BF16)
        dyab_ref[...] = _mm_nt(dh1b, wo_ref[...])
        dwo_s[0:DG, :] += _mm_tn(ya, dh1b)
        dwo_s[DG:D, :] += _mm_tn(yb, dh1b)

        @pl.when(i == nt - 1)
        def _():
            total = jnp.sum(sm_ref[0:1, :], axis=-1, keepdims=True) * (0.5 / D)
            sm_ref[4:5, :] = jnp.broadcast_to(total, (1, D))
            for k in range(NSHARD):
                dwo_ref[k] = dwo_s[k * QR:(k + 1) * QR, :].astype(BF16)
                dwg_ref[k] = dwg_s[k * QR:(k + 1) * QR, :].astype(BF16)
                dwp_ref[k] = dwp_s[k].astype(BF16)

    row = lambda w: pl.BlockSpec((tm, w), lambda i: (i, 0))
    const2 = lambda s: pl.BlockSpec(s, lambda i: (0, 0), pipeline_mode=pl.Buffered(1))
    const3 = lambda s: pl.BlockSpec(s, lambda i: (0, 0, 0), pipeline_mode=pl.Buffered(1))
    return pl.pallas_call(
        body, name="tail_fwd_bwd", grid=(nt,),
        in_specs=[row(D), row(PLE), row(D), row(DG), row(DG),
                  const2((D, D)), const2((D, D)), const3((NSHARD, PLE, PLE)),
                  const2((1, D)), const2((1, D)), const2((1, D))],
        out_specs=[row(D), row(D), const3((NSHARD, QR, D)), const3((NSHARD, QR, D)),
                   const3((NSHARD, PLE, PLE)), const2((8, D))],
        out_shape=[jax.ShapeDtypeStruct((T, D), F32), jax.ShapeDtypeStruct((T, D), F32),
                   jax.ShapeDtypeStruct((NSHARD, QR, D), BF16), jax.ShapeDtypeStruct((NSHARD, QR, D), BF16),
                   jax.ShapeDtypeStruct((NSHARD, PLE, PLE), BF16), jax.ShapeDtypeStruct((8, D), F32)],
        scratch_shapes=[pltpu.VMEM((D, D), F32), pltpu.VMEM((D, D), F32), pltpu.VMEM((NSHARD, PLE, PLE), F32)],
        compiler_params=_cparams(("arbitrary",), 62),
    )(x, p, tgt, pltpu.with_memory_space_constraint(ya, pltpu.HBM), pltpu.with_memory_space_constraint(yb, pltpu.HBM),
      w_out_b, w_pg_b, w_pp_b, ple_nw, b_pg, fnw)


def _bwd_rglru(pf, pb, h, dyab, cw3, conv_b, wa_c, ba, wx_c, bx, lam, dep):
    tm = 512
    nt = T // tm
    ng = tm // 8

    def body(xa_ref, ga_ref, xp_ref, h_ref, hp_ref, dya_ref, cw_ref, cb_ref, wa_ref, ba_ref, wx_ref, bx_ref,
             lam_ref, dep_ref, da_ref, rg_ref, sm_ref, a_s, g_s, cg_s, nxt_s, wa_s, wx_s, dwa_s, dwx_s):
        i = pl.program_id(0)
        tile = nt - 1 - i

        @pl.when(i == 0)
        def _():
            dwa_s[...] = jnp.zeros_like(dwa_s)
            dwx_s[...] = jnp.zeros_like(dwx_s)
            sm_ref[...] = jnp.zeros_like(sm_ref)
            cg_s[...] = jnp.zeros_like(cg_s)
            nxt_s[...] = jnp.zeros_like(nxt_s)
            wa_s[...] = _dense_from_blocks(wa_ref[...]).astype(BF16)
            wx_s[...] = _dense_from_blocks(wx_ref[...]).astype(BF16)

        rows = lax.broadcasted_iota(I32, (tm, DG), 0)
        has_prev = tile > 0
        xa = xa_ref[...]
        xprev = jnp.where(has_prev, xp_ref[...], 0.0)
        cw = _conv_rows(cw_ref)
        xc, taps = _rg_conv(xa, xprev, cw, cb_ref[...], rows)
        lam_v = lam_ref[...]
        sp = _softplus(-lam_v)
        first_row = (rows + tile * tm) == 0
        r, ig, a, a2, mult = _rg_gates(xc, wa_s[...], ba_ref[...], wx_s[...], bx_ref[...], sp, first_row)
        hv = h_ref[...]
        hprev = jnp.where(has_prev, hp_ref[...], 0.0)
        h_m1 = _shift_down(hv, hprev, 1, rows)
        ga = ga_ref[...].astype(F32)
        sg = _sigmoid(ga)
        dya = dya_ref[...]
        dga = dya * hv * (sg * (1.0 + ga * (1.0 - sg)))

        av = jnp.where(rows == tm - 1, 1.0, pltpu.roll(a, tm - 1, 0))
        gv = dya * (ga * sg)
        rows8 = rows & 7
        for d in (1, 2, 4):
            keep = rows8 < 8 - d
            gv = gv + av * jnp.where(keep, _roll_in_groups(gv, 8 - d), 0.0)
            av = av * jnp.where(keep, _roll_in_groups(av, 8 - d), 1.0)
        a_s[...] = av
        g_s[...] = gv
        carry = cg_s[0:1, :]
        for g in range(ng - 1, -1, -1):
            sl = slice(g * 8, (g + 1) * 8)
            ab, gb = a_s[sl, :], g_s[sl, :]
            g_s[sl, :] = gb + ab * carry
            carry = gb[0:1, :] + ab[0:1, :] * carry
        cg_s[0:1, :] = a[0:1, :] * carry

        gt = g_s[...]
        da = gt * h_m1
        ixc = ig * xc
        di = gt * mult * xc
        dxc = gt * mult * ig
        dlog_a = da * a + jnp.where(first_row, 0.0, gt * ixc * (-a2 / mult))
        sm_ref[3:4, :] += _rowsum(dlog_a * ((-RG_C) * r))
        dpr_f = dlog_a * ((-RG_C) * sp) * r * (1.0 - r)
        dpi_f = di * ig * (1.0 - ig)
        sm_ref[1:2, :] += _rowsum(dpr_f)
        sm_ref[2:3, :] += _rowsum(dpi_f)
        dpr = dpr_f.astype(BF16)
        dpi = dpi_f.astype(BF16)
        xcb = xc.astype(BF16)
        dwa_s[...] += _mm_tn(xcb, dpr)
        dwx_s[...] += _mm_tn(xcb, dpi)
        dxc = dxc + _mm_nt(dpr, wa_s[...]) + _mm_nt(dpi, wx_s[...])
        sm_ref[0:1, :] += _rowsum(dxc)
        for j in range(4):
            sm_ref[4 + j:5 + j, :] += _rowsum(dxc * taps[j])
        nxt = nxt_s[...]
        dxa = (dxc * cw[3] + _shift_up(dxc, nxt, 1, rows) * cw[2]
               + _shift_up(dxc, nxt, 2, rows) * cw[1] + _shift_up(dxc, nxt, 3, rows) * cw[0])
        nxt_s[...] = dxc[0:8, :]
        da_ref[:, 0:DG] = dxa.astype(BF16)
        da_ref[:, DG:D] = dga.astype(BF16)

        @pl.when(i == nt - 1)
        def _():
            _, s_neg = _sig_pair(lam_v)
            sm_ref[3:4, :] = sm_ref[3:4, :] * (-s_neg)
            rg_ref[...] = _blocks_from_dense(dwa_s[...], dwx_s[...])

    vec = lambda: pl.BlockSpec((1, DG), lambda i: (0, 0))
    blocks = lambda: pl.BlockSpec((DG, RGB), lambda i: (0, 0))
    prev8 = lambda: pl.BlockSpec((8, DG), lambda i: (jnp.maximum((nt - 1 - i) * (tm // 8) - 1, 0), 0))
    return pl.pallas_call(
        body, name="bwd_rglru", grid=(nt,),
        in_specs=[pl.BlockSpec((tm, DG), lambda i: (nt - 1 - i, 0)),
                  pl.BlockSpec((tm, DG), lambda i: (nt - 1 - i, 0)),
                  prev8(),
                  pl.BlockSpec((tm, DG), lambda i: (nt - 1 - i, 0)),
                  prev8(),
                  pl.BlockSpec((tm, DG), lambda i: (nt - 1 - i, 0)),
                  pl.BlockSpec((NSHARD, 4, 128), lambda i: (0, 0, 0)), vec(),
                  blocks(), vec(), blocks(), vec(), vec(), ANY_SPEC],
        out_specs=[pl.BlockSpec((tm, D), lambda i: (nt - 1 - i, 0)),
                   pl.BlockSpec((DG, 128), lambda i: (0, 0)),
                   pl.BlockSpec((8, DG), lambda i: (0, 0))],
        out_shape=[jax.ShapeDtypeStruct((T, D), BF16), jax.ShapeDtypeStruct((DG, 128), F32),
                   jax.ShapeDtypeStruct((8, DG), F32)],
        scratch_shapes=[pltpu.VMEM((tm, DG), F32), pltpu.VMEM((tm, DG), F32),
                        pltpu.VMEM((8, DG), F32), pltpu.VMEM((8, DG), F32),
                        pltpu.VMEM((DG, DG), BF16), pltpu.VMEM((DG, DG), BF16),
                        pltpu.VMEM((DG, DG), F32), pltpu.VMEM((DG, DG), F32)],
        compiler_params=_cparams(("arbitrary",), 56),
    )(pf, pb, pf, h, h, dyab, cw3, conv_b, wa_c, ba, wx_c, bx, lam, dep)


def _bwd_hgrn2(pf, pb, o, s_prev, dyab, hg_lb, hg_nw, dep):
    tm = 512
    nt = T // tm
    nc = tm // CH

    def body(q_ref, f_ref, v_ref, g_ref, o_ref, sp_ref, dy_ref, lb_ref, nw_ref, dep_ref, db_ref, sm_ref,
             dst_s, eb_s, enb_s, ekd_s, qe_s, ke_s, kd_s, vb_s, do_s, dec_s, ddec_s, p_s, dp_s,
             g_s, dsta_s, dva_s, dqe_s, dke_s, dkd_s, dlf_s):
        i = pl.program_id(0)

        @pl.when(i == 0)
        def _():
            sm_ref[...] = jnp.zeros_like(sm_ref)
            dst_s[...] = jnp.zeros_like(dst_s)

        lb, one_m_lb = _hg_lower_bound(lb_ref)
        rows64 = lax.broadcasted_iota(I32, (tm, DG), 0) & (CH - 1)
        sg, sn, f, k, q, sq, qs = _hg_recompute(
            q_ref, f_ref, lb, one_m_lb, rows64, eb_s, enb_s, ekd_s, qe_s, ke_s, kd_s, dec_s)
        vb_s[...] = v_ref[...]

        nw = nw_ref[...]
        for hd in range(NH):
            cols = slice(hd * HD, (hd + 1) * HD)
            g = g_ref[:, cols].astype(F32)
            sgg = _sigmoid(g)
            o = o_ref[:, cols]
            so = lax.rsqrt(jnp.mean(o * o, axis=-1, keepdims=True) + EPS)
            oh = o * so
            dyb = dy_ref[:, cols]
            db_ref[:, 3 * DG + hd * HD:3 * DG + (hd + 1) * HD] = (
                dyb * (oh * nw) * (sgg * (1.0 + g * (1.0 - sgg)))).astype(BF16)
            don = dyb * (g * sgg)
            sm_ref[1:2, 0:HD] += _rowsum(don * oh)
            gw = don * nw
            do_s[:, cols] = (so * (gw - oh * jnp.mean(gw * oh, axis=-1, keepdims=True))).astype(BF16)

        mask = _tri(True)
        items = [(c, hd, slice(c * CH, (c + 1) * CH), slice(hd * HD, (hd + 1) * HD))
                 for c in range(nc) for hd in range(NH)]
        for c, hd, rs, cols in items:
            p_s[c * NH + hd] = jnp.where(mask, _mm_nt(qe_s[rs, cols], ke_s[rs, cols]), 0.0).astype(BF16)
            dp_s[c * NH + hd] = jnp.where(mask, _mm_nt(do_s[rs, cols], vb_s[rs, cols]), 0.0).astype(BF16)
        for c, hd, rs, cols in items:
            n = c * NH + hd
            dva_s[rs, cols] = _mm_tn(p_s[n], do_s[rs, cols])
            dqe_s[rs, cols] = _mm(dp_s[n], ke_s[rs, cols])
            dke_s[rs, cols] = _mm_tn(dp_s[n], qe_s[rs, cols])
            g_s[n] = _mm_tn(do_s[rs, cols], qe_s[rs, cols])
        for c, hd, rs, cols in reversed(items):
            n = c * NH + hd
            dst = dst_s[hd]
            dsta_s[n] = dst
            dst_s[hd] = dst * dec_s[c:c + 1, cols] + g_s[n]
        for c, hd, rs, cols in items:
            n = c * NH + hd
            dst = dsta_s[n]
            st_prev = sp_ref[hd, c]
            dv = dva_s[rs, cols] + _mm_nt(kd_s[rs, cols], dst)
            db_ref[rs, 2 * DG + hd * HD:2 * DG + (hd + 1) * HD] = dv.astype(BF16)
            dqe_s[rs, cols] += _mm(do_s[rs, cols], st_prev)
            dkd_s[rs, cols] = _mm(vb_s[rs, cols], dst)
            ddec_s[c:c + 1, cols] = _rowsum(dst * st_prev)

        eb, enb, ekd = eb_s[...], enb_s[...], ekd_s[...]
        dqe, dke, dkd = dqe_s[...], dke_s[...], dkd_s[...]
        t_kd = dkd * (k * ekd)
        rc = _chunk_rev_cumsum(dqe * (qs * eb) - dke * (k * enb) - t_kd, rows64)
        for c in range(nc):
            rs = slice(c * CH, (c + 1) * CH)
            dbl = _rowsum(t_kd[rs]) + ddec_s[c:c + 1, :] * dec_s[c:c + 1, :]
            dlf_s[rs, :] = rc[rs] + dbl
        t = dlf_s[...] / f - (dke * enb + dkd * ekd)
        db_ref[:, DG:2 * DG] = (one_m_lb * sg * sn * t).astype(BF16)
        sm_ref[0:1, :] += _rowsum(sn * t)
        db_ref[:, 0:DG] = (dqe * eb * (sq * (1.0 + q * (1.0 - sq))) * (HD ** -0.5)).astype(BF16)

        @pl.when(i == nt - 1)
        def _():
            dsm = sm_ref[0:1, :] * (lb * one_m_lb)
            sm_ref[2:3, :] = dsm
            sm_ref[3:4, :] = -dsm

    col = lambda j: pl.BlockSpec((tm, DG), lambda i: (nt - 1 - i, j))
    big = lambda dt: pltpu.VMEM((tm, DG), dt)
    return pl.pallas_call(
        body, name="bwd_hgrn2", grid=(nt,),
        in_specs=[col(1), col(1), col(2), col(3),
                  pl.BlockSpec((tm, DG), lambda i: (nt - 1 - i, 0)),
                  pl.BlockSpec((NH, nc, HD, HD), lambda i: (0, nt - 1 - i, 0, 0)),
                  pl.BlockSpec((tm, DG), lambda i: (nt - 1 - i, 1)),
                  pl.BlockSpec((2, DG), lambda i: (0, 0)),
                  pl.BlockSpec((1, HD), lambda i: (0, 0)), ANY_SPEC],
        out_specs=[pl.BlockSpec((tm, 4 * DG), lambda i: (nt - 1 - i, 0)),
                   pl.BlockSpec((8, DG), lambda i: (0, 0))],
        out_shape=[jax.ShapeDtypeStruct((T, 4 * DG), BF16), jax.ShapeDtypeStruct((8, DG), F32)],
        scratch_shapes=[pltpu.VMEM((NH, HD, HD), F32),
                        big(F32), big(F32), big(F32),
                        big(BF16), big(BF16), big(BF16), big(BF16), big(BF16),
                        pltpu.VMEM((nc, DG), F32), pltpu.VMEM((nc, DG), F32),
                        pltpu.VMEM((nc * NH, CH, CH), BF16), pltpu.VMEM((nc * NH, CH, CH), BF16),
                        pltpu.VMEM((nc * NH, HD, HD), F32), pltpu.VMEM((nc * NH, HD, HD), F32),
                        big(F32), big(F32), big(F32), big(F32), big(F32)],
        compiler_params=_cparams(("arbitrary",), 56),
    )(pb, pf, pb, pb, o, s_prev, dyab, hg_lb, hg_nw, dep)


def _dproj_pieces(k, da_ref, db_ref):
    if k == 0:
        return [(da_ref[:, 0:SHW], 0)]
    if k == 1:
        return [(da_ref[:, SHW:D], 0), (db_ref[:, 0:DG], D - SHW)]
    if k == 2:
        return [(db_ref[:, DG:DG + SHW], 0)]
    return [(db_ref[:, DG + SHW:4 * DG], 0)]


def _bwd_inproj_dx(x, dh1, d_a, d_b, w_in_b, nw, dep):
    tm = 512
    nt = T // tm

    def body(x_ref, dh1_ref, da_ref, db_ref, w_ref, nw_ref, dep_ref, dx_ref, sm_ref):
        i = pl.program_id(0)

        @pl.when(i == 0)
        def _():
            sm_ref[...] = jnp.zeros_like(sm_ref)

        du = None
        for k in range(NSHARD):
            for val, off in _dproj_pieces(k, da_ref, db_ref):
                t = _mm_nt(val, w_ref[k, :, off:off + val.shape[1]])
                du = t if du is None else du + t
        xv = x_ref[...]
        s = lax.rsqrt(jnp.mean(xv * xv, axis=-1, keepdims=True) + EPS)
        xh = xv * s
        sm_ref[0:1, :] += _rowsum(du * xh)
        g = du * nw_ref[...]
        dx_ref[...] = dh1_ref[...] + s * (g - xh * jnp.mean(g * xh, axis=-1, keepdims=True))

    row = lambda w: pl.BlockSpec((tm, w), lambda i: (i, 0))
    return pl.pallas_call(
        body, name="bwd_inproj_dx", grid=(nt,),
        in_specs=[row(D), row(D), row(D), row(4 * DG),
                  pl.BlockSpec((NSHARD, D, SHW), lambda i: (0, 0, 0), pipeline_mode=pl.Buffered(1)),
                  pl.BlockSpec((1, D), lambda i: (0, 0)), ANY_SPEC],
        out_specs=[row(D), pl.BlockSpec((8, D), lambda i: (0, 0))],
        out_shape=[jax.ShapeDtypeStruct((T, D), F32), jax.ShapeDtypeStruct((8, D), F32)],
        compiler_params=_cparams(("arbitrary",), 56),
    )(x, dh1, d_a, d_b, w_in_b, nw, dep)


def _bwd_inproj_dw23(u_b, d_b):
    tm = 1024
    nt = T // tm

    def body(u_ref, db_ref, dw_ref, acc):
        i = pl.program_id(0)

        @pl.when(i == 0)
        def _():
            acc[...] = jnp.zeros_like(acc)

        u = u_ref[...]
        for j, k in enumerate((2, 3)):
            for val, off in _dproj_pieces(k, None, db_ref):
                acc[j, :, off:off + val.shape[1]] += _mm_tn(u, val)

        @pl.when(i == nt - 1)
        def _():
            for j in range(2):
                for r0 in range(0, D, 256):
                    dw_ref[j, r0:r0 + 256, :] = acc[j, r0:r0 + 256, :].astype(BF16)

    row = lambda w: pl.BlockSpec((tm, w), lambda i: (i, 0))
    return pl.pallas_call(
        body, name="bwd_inproj_dw23", grid=(nt,), in_specs=[row(D), row(4 * DG)],
        out_specs=pl.BlockSpec((2, D, SHW), lambda i: (0, 0, 0)),
        out_shape=jax.ShapeDtypeStruct((2, D, SHW), BF16),
        scratch_shapes=[pltpu.VMEM((2, D, SHW), F32)],
        compiler_params=_cparams(("arbitrary",), 48),
    )(u_b, d_b)


def _bwd_inproj_dw01(u_b, d_a, d_b, dw23):
    tm = 1024
    nt = T // tm
    H = D // 2
    step = 128

    def body(u_ref, da_ref, db_ref, p23_ref, dw_ref, acc, send_s, recv_s, ssem, rsem):
        i = pl.program_id(0)

        @pl.when(i == 0)
        def _():
            acc[...] = jnp.zeros_like(acc)

        u = u_ref[...]
        for j, k in enumerate((0, 1)):
            for val, off in _dproj_pieces(k, da_ref, db_ref):
                acc[j, :, off:off + val.shape[1]] += _mm_tn(u, val)

        @pl.when(i == nt - 1)
        def _():
            x, y, c = lax.axis_index("x"), lax.axis_index("y"), lax.axis_index("c")
            sibling = (x, y, 1 - c)
            mine0 = pl.multiple_of(c * H, step)
            other0 = pl.multiple_of((1 - c) * H, step)
            copies = []
            for k in range(NSHARD):
                for r0 in range(0, H, step):
                    rs = pl.ds(other0 + r0, step)
                    send_s[k, r0:r0 + step, :] = acc[k, rs, :].astype(BF16) if k < 2 else p23_ref[k - 2, rs, :]
                cp = _remote(send_s.at[k], recv_s.at[k], ssem.at[k], rsem.at[k], sibling)
                cp.start()
                copies.append(cp)
            for k in range(NSHARD):
                copies[k].wait_recv()
                for r0 in range(0, H, step):
                    rs = pl.ds(mine0 + r0, step)
                    own = acc[k, rs, :] if k < 2 else p23_ref[k - 2, rs, :].astype(F32)
                    dw_ref[k, r0:r0 + step, :] = (own + recv_s[k, r0:r0 + step, :].astype(F32)).astype(BF16)
            for cp in copies:
                cp.wait_send()

    row = lambda w: pl.BlockSpec((tm, w), lambda i: (i, 0))
    return pl.pallas_call(
        body, name="bwd_inproj_dw01", grid=(nt,),
        in_specs=[row(D), row(D), row(4 * DG),
                  pl.BlockSpec((2, D, SHW), lambda i: (0, 0, 0), pipeline_mode=pl.Buffered(1))],
        out_specs=pl.BlockSpec((NSHARD, H, SHW), lambda i: (0, 0, 0)),
        out_shape=jax.ShapeDtypeStruct((NSHARD, H, SHW), BF16),
        scratch_shapes=[pltpu.VMEM((2, D, SHW), F32), pltpu.VMEM((NSHARD, H, SHW), BF16),
                        pltpu.VMEM((NSHARD, H, SHW), BF16),
                        pltpu.SemaphoreType.DMA((NSHARD,)), pltpu.SemaphoreType.DMA((NSHARD,))],
        compiler_params=_cparams(("arbitrary",), 56),
    )(u_b, d_a, d_b, dw23)


_OUT_ORDER = ["norm_mix_w", "w_in", "conv_w", "conv_b", "rg_wa", "rg_ba", "rg_wx", "rg_bx", "rg_lambda", "hg_lb",
              "hg_norm_w", "w_out", "ple_norm_w", "w_ple_gate", "b_ple_gate", "w_ple_proj", "final_norm_w"]
_BIG = ["w_in", "w_out", "w_ple_gate", "w_ple_proj"]


def _small_view(name, a):
    if name in ("rg_wa", "rg_wx"):
        return a.reshape(DG, RGB)
    if name == "conv_w":
        return a.reshape(4, 128)
    if name == "final_norm_w":
        return a.reshape(1, D)
    return a


def _landing(rows, cols):
    return lax.empty((NDEV, rows, cols), BF16)


def kernel(x, p, norm_mix_w, w_in, conv_w, conv_b, rg_wa, rg_ba, rg_wx, rg_bx, rg_lambda, hg_lb, hg_norm_w, w_out, ple_norm_w, w_ple_gate, b_ple_gate, w_ple_proj, final_norm_w, loss_target, m_norm_mix_w, m_w_in, m_conv_w, m_conv_b, m_rg_wa, m_rg_ba, m_rg_wx, m_rg_bx, m_rg_lambda, m_hg_lb, m_hg_norm_w, m_w_out, m_ple_norm_w, m_w_ple_gate, m_b_ple_gate, m_w_ple_proj, m_final_norm_w, v_norm_mix_w, v_w_in, v_conv_w, v_conv_b, v_rg_wa, v_rg_ba, v_rg_wx, v_rg_bx, v_rg_lambda, v_hg_lb, v_hg_norm_w, v_w_out, v_ple_norm_w, v_w_ple_gate, v_b_ple_gate, v_w_ple_proj, v_final_norm_w):
    given = dict(locals())
    x2, p2, tgt = x[0], p[0, 0], loss_target[0]
    hbm = lambda a: pltpu.with_memory_space_constraint(a, pltpu.HBM)
    norm_mix_w, conv_b, rg_ba, rg_bx, rg_lambda, hg_lb, hg_norm_w, ple_norm_w, b_ple_gate = (
        hbm(a) for a in (norm_mix_w, conv_b, rg_ba, rg_bx, rg_lambda, hg_lb, hg_norm_w, ple_norm_w, b_ple_gate))
    wa_c, wx_c = hbm(_small_view("rg_wa", rg_wa)), hbm(_small_view("rg_wx", rg_wx))

    w_in_b, l_out, l_pg, l_pp, cw3, pf, pb, u_b = _gather_inproj(
        x2, norm_mix_w, w_in[0], w_out[0], w_ple_gate[0], w_ple_proj[0], conv_w[0])
    g_ssem, g_rsem, g_lands, tok = _gather_rest_start([l_out, l_pg, l_pp])

    h, ya = _fwd_rglru(pf, pb, cw3, conv_b, wa_c, rg_ba, wx_c, rg_bx, rg_lambda, tok)
    yb, o, s_prev = _fwd_hgrn2(pf, pb, hg_lb, hg_norm_w)
    w_out_b, w_pg_b, w_pp_b = _gather_rest_wait(g_ssem, g_rsem, g_lands, yb)
    dh1, dyab, dwo_b, dwg_b, dwp_b, sm_tail = _tail_fwd_bwd(
        x2, p2, tgt, ya, yb, w_out_b.reshape(D, D), w_pg_b.reshape(D, D), w_pp_b,
        ple_norm_w, b_ple_gate, final_norm_w.reshape(1, D))

    QH = D // NSHARD // 2
    r1 = _rs_start("rs_start_tail", [dwo_b, dwg_b, dwp_b], (0, 1, 2, 3),
                   [_landing(QH, D), _landing(QH, D), _landing(PLE // 2, PLE)])
    d_b, sm_b = _bwd_hgrn2(pf, pb, o, s_prev, dyab, hg_lb, hg_norm_w, r1[4])
    dw23 = _bwd_inproj_dw23(u_b, d_b)
    d_a, rg_c, sm_a = _bwd_rglru(pf, pb, h, dyab, cw3, conv_b, wa_c, rg_ba, wx_c, rg_bx, rg_lambda, dw23)
    dw_in = _bwd_inproj_dw01(u_b, d_a, d_b, dw23)
    r2 = _rs_start("rs_start_in", [dw_in], (0, 1, 2, 3), [lax.empty((NSHARD, D // 2, SHW), BF16)], chip_sums=True)
    grad_x, sm_in = _bwd_inproj_dx(x2, dh1, d_a, d_b, w_in_b, norm_mix_w, r2[4])

    parts1, lands1 = _rs_wait("rs_wait_tail", r1[0], r1[1], r1[2], (0, 1, 2, 3), r1[3], sm_in)
    parts2, lands2 = _rs_wait("rs_wait_in", r2[0], r2[1], r2[2], (0, 1, 2, 3), r2[3], sm_in, chip_sums=True)
    g_big, vred, rgred = _final_reduce(parts2 + parts1, lands2 + lands1, sm_in, sm_tail, sm_a, sm_b, rg_c)

    upd_big = _adam_big(g_big, [given[n][0] for n in _BIG], [given["m_" + n][0] for n in _BIG],
                        [given["v_" + n][0] for n in _BIG])
    small = _adam_small(vred, rgred,
                        {n: _small_view(n, given[n]) for n in _SMALL_ORDER},
                        {n: _small_view(n, given["m_" + n]) for n in _SMALL_ORDER},
                        {n: _small_view(n, given["v_" + n]) for n in _SMALL_ORDER})

    loss = vred[0, ROW_LOSS, 0]
    outs = [loss, grad_x[None]]
    for ki in range(4):
        for n in _OUT_ORDER:
            if n in _BIG:
                i = _BIG.index(n)
                a = g_big[i] if ki == 0 else upd_big[i][ki - 1]
                outs.append(a[None])
            else:
                outs.append(small[n][ki].reshape(given[n].shape))
    return tuple(outs)
```

```python
import jax
import jax.numpy as jnp
from jax import lax
from jax.experimental import pallas as pl
from jax.experimental.pallas import tpu as pltpu

F32 = jnp.float32
BF16 = jnp.bfloat16
I32 = jnp.int32
MESH = pl.DeviceIdType.MESH
HIGHEST = lax.Precision.HIGHEST

T = 4096
D = 1024
DG = 512
DIN = 3072
PLE = 256
NH = 4
HD = 128
CH = 64
NCHUNK = T // CH
RGB = 64
EPS = 1e-6
RG_C = 8.0
NSHARD = 4
SHW = DIN // NSHARD
NDEV = 8

ADAM_LR = 0.001
ADAM_B1 = 0.9
ADAM_B2 = 0.999
ADAM_EPS = 1e-08
ADAM_WD = 0.01
ADAM_STEP = 10

VMEM_SPEC = pl.BlockSpec(memory_space=pltpu.VMEM)
HBM_SPEC = pl.BlockSpec(memory_space=pltpu.HBM)
SEM_SPEC = pl.BlockSpec(memory_space=pltpu.SEMAPHORE)
ANY_SPEC = pl.BlockSpec(memory_space=pl.ANY)
EFFECT = pltpu.SideEffectType.DATAFLOW_SIDE_EFFECTING
MIB = 1024 * 1024

VROWS = 16
ROW_NORM_MIX, ROW_FINAL_NORM, ROW_B_PG, ROW_PLE_NORM = 0, 1, 2, 3
ROW_CB_BA, ROW_BX_LAM, ROW_CW01, ROW_CW23, ROW_HG_LB, ROW_HG_NW, ROW_LOSS = 4, 5, 6, 7, 8, 9, 10


def _mm(a, b):
    return jnp.dot(a.astype(BF16), b.astype(BF16), preferred_element_type=F32)


def _mm_nt(a, b):
    return lax.dot_general(a.astype(BF16), b.astype(BF16), (((1,), (1,)), ((), ())),
                           preferred_element_type=F32)


def _mm_tn(a, b):
    return lax.dot_general(a.astype(BF16), b.astype(BF16), (((0,), (0,)), ((), ())),
                           preferred_element_type=F32)


def _mm_exact(a, b):
    bb = b.astype(BF16)
    hi = a.astype(BF16)
    r1 = a - hi.astype(F32)
    mid = r1.astype(BF16)
    lo = (r1 - mid.astype(F32)).astype(BF16)
    return sum(jnp.dot(p, bb, preferred_element_type=F32) for p in (hi, mid, lo))


def _sig_pair(x):
    e = jnp.exp(-jnp.abs(x))
    big = 1.0 / (1.0 + e)
    small = e * big
    pos = x >= 0
    return jnp.where(pos, big, small), jnp.where(pos, small, big)


def _sigmoid(x):
    return 1.0 / (1.0 + jnp.exp(-x))


def _rowsum(v):
    return jnp.sum(v, axis=0, keepdims=True)


def _shift_down(cur, prev8, d, rows):
    rolled = pltpu.roll(cur, d, 0)
    head = jnp.where(rows[0:8] < d, pltpu.roll(prev8, d, 0), rolled[0:8])
    return jnp.concatenate([head, rolled[8:]], axis=0)


def _shift_up(cur, next8, d, rows):
    n = cur.shape[0]
    rolled = pltpu.roll(cur, n - d, 0)
    tail = jnp.where(rows[0:8] >= 8 - d, pltpu.roll(next8, 8 - d, 0), rolled[n - 8:n])
    return jnp.concatenate([rolled[0:n - 8], tail], axis=0)


def _roll_in_groups(v, d):
    n, w = v.shape
    return pltpu.roll(v.reshape(n // 8, 8, w), d, 1).reshape(n, w)


def _cparams(sem, vmem_mib):
    return pltpu.CompilerParams(dimension_semantics=sem, vmem_limit_bytes=vmem_mib * MIB)


def _mesh_pos():
    x, y, c = lax.axis_index("x"), lax.axis_index("y"), lax.axis_index("c")
    chips = [(1 - x, y), (x, 1 - y), (1 - x, 1 - y)]
    return x, y, c, chips


def _remote(src, dst, ssem, rsem, dev):
    return pltpu.make_async_remote_copy(src_ref=src, dst_ref=dst, send_sem=ssem, recv_sem=rsem,
                                        device_id=dev, device_id_type=MESH)


def _gather_w_in(w_in, w_out, w_pg, w_pp, conv_w):
    shapes = [w_in.shape, w_out.shape, w_pg.shape, w_pp.shape]

    def body(win_h, wout_h, wpg_h, wpp_h, cw_h, o_in, o_out, o_pg, o_pp, o_cw,
             win, wout, wpg, wpp, cw, lsem, ssem, rsem):
        x, y, c, chips = _mesh_pos()
        kme = 2 * x + y
        sibling = (x, y, 1 - c)
        fetch = [pltpu.make_async_copy(src, dst, lsem.at[i])
                 for i, (src, dst) in enumerate([(win_h, win), (cw_h, cw), (wout_h, wout), (wpg_h, wpg), (wpp_h, wpp)])]
        for cp in fetch:
            cp.start()

        def cast(src, dst):
            for r0 in range(0, src.shape[0], 128):
                dst[kme, r0:r0 + 128, :] = src[r0:r0 + 128, :].astype(BF16)

        fetch[0].wait()
        cast(win, o_in)
        fetch[1].wait()
        o_cw[kme] = cw[...]

        hrows = D // 2
        mine = pl.ds(pl.multiple_of(c * hrows, 128), hrows)
        other = pl.ds(pl.multiple_of((1 - c) * hrows, 128), hrows)

        def half(k, rows):
            return o_in.at[k, rows]

        na = (x ^ c, y ^ (1 - c), c)
        nb = (x ^ (1 - c), y ^ c, c)
        ka = 2 * na[0] + na[1]
        kb = 2 * nb[0] + nb[1]
        kd = 2 * (1 - x) + (1 - y)
        sends = [_remote(half(kme, mine), half(kme, mine), ssem.at[0], rsem.at[0], na),
                 _remote(half(kme, mine), half(kme, mine), ssem.at[1], rsem.at[1], nb)]
        for j, (px, py) in enumerate(chips):
            sends.append(_remote(o_cw.at[kme], o_cw.at[kme], ssem.at[3 + j], rsem.at[3 + j], (px, py, c)))
        for cp in sends:
            cp.start()
        for i, (src, dst) in enumerate([(wout, o_out), (wpg, o_pg), (wpp, o_pp)]):
            fetch[2 + i].wait()
            cast(src, dst)

        def passed_on(k, sem, dev):
            cp = _remote(half(k, mine), half(k, mine), ssem.at[sem], rsem.at[sem], dev)
            cp.start()
            sends.append(cp)

        _remote(half(ka, mine), half(ka, mine), ssem.at[0], rsem.at[0], na).wait_recv()
        passed_on(ka, 2, nb)
        passed_on(ka, 6, sibling)
        _remote(half(kb, mine), half(kb, mine), ssem.at[1], rsem.at[1], nb).wait_recv()
        passed_on(kb, 7, sibling)
        _remote(half(kd, mine), half(kd, mine), ssem.at[2], rsem.at[2], nb).wait_recv()
        passed_on(kd, 8, sibling)
        _remote(half(kb, other), half(kb, other), ssem.at[6], rsem.at[6], sibling).wait_recv()
        _remote(half(ka, other), half(ka, other), ssem.at[7], rsem.at[7], sibling).wait_recv()
        _remote(half(kd, other), half(kd, other), ssem.at[8], rsem.at[8], sibling).wait_recv()
        for j, (px, py) in enumerate(chips):
            kj = 2 * px + py
            _remote(o_cw.at[kj], o_cw.at[kj], ssem.at[3 + j], rsem.at[3 + j], (px, py, c)).wait_recv()
        for cp in sends:
            cp.wait_send()

    out_shape = [jax.ShapeDtypeStruct((NSHARD,) + s, BF16) for s in shapes]
    out_shape.append(jax.ShapeDtypeStruct((NSHARD,) + conv_w.shape, F32))
    return pl.pallas_call(
        body, name="gather_w_in", out_shape=out_shape,
        in_specs=[HBM_SPEC] * 5, out_specs=[VMEM_SPEC] * 5,
        scratch_shapes=[pltpu.VMEM(a.shape, F32) for a in (w_in, w_out, w_pg, w_pp, conv_w)]
        + [pltpu.SemaphoreType.DMA((5,)), pltpu.SemaphoreType.DMA((9,)), pltpu.SemaphoreType.DMA((9,))],
        compiler_params=pltpu.CompilerParams(vmem_limit_bytes=40 * MIB),
    )(*[pltpu.with_memory_space_constraint(a, pltpu.HBM) for a in (w_in, w_out, w_pg, w_pp, conv_w)])


def _gather_inproj(x, nw, w_in, w_out, w_pg, w_pp, conv_w):
    shapes = [w_in.shape, w_out.shape, w_pg.shape, w_pp.shape]
    tm = 512
    nt = T // tm

    def body(x_h, nw_h, win_h, wout_h, wpg_h, wpp_h, cw_h,
             o_in_h, o_out_h, o_pg_h, o_pp_h, o_cw_h, pf_h, pb_h, u_h,
             win, wout, wpg, wpp, cw, nw_s, xbuf, ubuf, rfbuf, rbbuf, o_in, o_out, o_pg, o_pp, o_cw,
             lsem, xsem, usem, fsem, bsem, osem, ssem, rsem):
        x, y, c, chips = _mesh_pos()
        kme = 2 * x + y
        sibling = (x, y, 1 - c)
        fetch = [pltpu.make_async_copy(src, dst, lsem.at[i]) for i, (src, dst) in enumerate(
            [(win_h, win), (cw_h, cw), (nw_h, nw_s), (wout_h, wout), (wpg_h, wpg), (wpp_h, wpp)])]
        for cp in fetch:
            cp.start()

        def cast(src, dst):
            for r0 in range(0, src.shape[0], 128):
                dst[r0:r0 + 128, :] = src[r0:r0 + 128, :].astype(BF16)

        outs = []

        def write_out(src, dst, i):
            cp = pltpu.make_async_copy(src, dst, osem.at[i])
            cp.start()
            outs.append(cp)

        fetch[0].wait()
        cast(win, o_in.at[kme])
        write_out(o_in.at[kme], o_in_h.at[kme], 0)
        fetch[1].wait()
        o_cw[kme] = cw[...]

        hrows = D // 2
        mine = pl.ds(pl.multiple_of(c * hrows, 128), hrows)
        other = pl.ds(pl.multiple_of((1 - c) * hrows, 128), hrows)

        def half(k, rows):
            return o_in.at[k, rows]

        na = (x ^ c, y ^ (1 - c), c)
        nb = (x ^ (1 - c), y ^ c, c)
        ka = 2 * na[0] + na[1]
        kb = 2 * nb[0] + nb[1]
        kd = 2 * (1 - x) + (1 - y)
        sends = [_remote(half(kme, mine), half(kme, mine), ssem.at[0], rsem.at[0], na),
                 _remote(half(kme, mine), half(kme, mine), ssem.at[1], rsem.at[1], nb)]
        for j, (px, py) in enumerate(chips):
            sends.append(_remote(o_cw.at[kme], o_cw.at[kme], ssem.at[3 + j], rsem.at[3 + j], (px, py, c)))
        for cp in sends:
            cp.start()
        fetch[2].wait()

        def rows_of(t):
            return pl.ds(pl.multiple_of(t * tm, tm), tm)

        def x_copy(t, slot):
            return pltpu.make_async_copy(x_h.at[rows_of(t)], xbuf.at[slot], xsem.at[slot])

        u_out = pltpu.make_async_copy(ubuf, u_h, usem.at[0])

        def f_store(t, slot, kh):
            return pltpu.make_async_copy(rfbuf.at[slot], pf_h.at[rows_of(t), pl.ds(pl.multiple_of(kh * DG, DG), DG)],
                                         fsem.at[slot])

        def b_store_even(t, slot, kh):
            return pltpu.make_async_copy(rbbuf.at[slot, :, 0:256],
                                         pb_h.at[rows_of(t), pl.ds(pl.multiple_of(kh * 2 * DG, 256), 256)],
                                         bsem.at[slot])

        def b_store_odd(t, slot, kh):
            return pltpu.make_async_copy(rbbuf.at[slot],
                                         pb_h.at[rows_of(t), pl.ds(pl.multiple_of(256 + kh * 2 * DG, 256), SHW)],
                                         bsem.at[slot])

        def run_pass(k, first, even):
            kh = k >> 1
            if first:
                x_copy(0, 0).start()

            def tile(t, carry):
                slot = t & 1
                if first:
                    @pl.when(t + 1 < nt)
                    def _():
                        x_copy(t + 1, 1 - slot).start()

                    x_copy(t, slot).wait()

                @pl.when(t >= 2)
                def _():
                    if even:
                        f_store(t - 2, slot, kh).wait()
                        b_store_even(t - 2, slot, kh).wait()
                    else:
                        b_store_odd(t - 2, slot, kh).wait()

                if first:
                    xv = xbuf[slot]
                    s = lax.rsqrt(jnp.mean(xv * xv, axis=-1, keepdims=True) + EPS)
                    u = (xv * s * nw_s[...]).astype(BF16)
                    ubuf[rows_of(t), :] = u
                else:
                    u = ubuf[rows_of(t), :]
                r = jnp.dot(u, o_in[k], preferred_element_type=F32)
                if even:
                    rfbuf[slot] = r[:, 0:DG]
                    rbbuf[slot, :, 0:256] = r[:, DG:SHW].astype(BF16)
                    f_store(t, slot, kh).start()
                    b_store_even(t, slot, kh).start()
                else:
                    rbbuf[slot] = r.astype(BF16)
                    b_store_odd(t, slot, kh).start()
                return carry

            lax.fori_loop(0, nt, tile, 0)
            for t in (nt - 2, nt - 1):
                slot = t & 1
                if even:
                    f_store(t, slot, kh).wait()
                    b_store_even(t, slot, kh).wait()
                else:
                    b_store_odd(t, slot, kh).wait()

        def project(k, first=False):
            @pl.when((k & 1) == 0)
            def _():
                run_pass(k, first, True)

            @pl.when((k & 1) == 1)
            def _():
                run_pass(k, first, False)

        def passed_on(k, sem, dev):
            cp = _remote(half(k, mine), half(k, mine), ssem.at[sem], rsem.at[sem], dev)
            cp.start()
            sends.append(cp)

        project(kme, first=True)
        u_out.start()
        for i, (src, dst, dst_h) in enumerate([(wout, o_out, o_out_h), (wpg, o_pg, o_pg_h), (wpp, o_pp, o_pp_h)]):
            fetch[3 + i].wait()
            cast(src, dst)
            write_out(dst, dst_h.at[kme], 4 + i)
        _remote(half(ka, mine), half(ka, mine), ssem.at[0], rsem.at[0], na).wait_recv()
        passed_on(ka, 2, nb)
        passed_on(ka, 6, sibling)
        _remote(half(kb, mine), half(kb, mine), ssem.at[1], rsem.at[1], nb).wait_recv()
        passed_on(kb, 7, sibling)
        _remote(half(ka, other), half(ka, other), ssem.at[7], rsem.at[7], sibling).wait_recv()
        write_out(o_in.at[ka], o_in_h.at[ka], 1)
        project(ka)
        _remote(half(kd, mine), half(kd, mine), ssem.at[2], rsem.at[2], nb).wait_recv()
        passed_on(kd, 8, sibling)
        _remote(half(kb, other), half(kb, other), ssem.at[6], rsem.at[6], sibling).wait_recv()
        write_out(o_in.at[kb], o_in_h.at[kb], 2)
        project(kb)
        _remote(half(kd, other), half(kd, other), ssem.at[8], rsem.at[8], sibling).wait_recv()
        write_out(o_in.at[kd], o_in_h.at[kd], 3)
        project(kd)
        u_out.wait()
        for j, (px, py) in enumerate(chips):
            kj = 2 * px + py
            _remote(o_cw.at[kj], o_cw.at[kj], ssem.at[3 + j], rsem.at[3 + j], (px, py, c)).wait_recv()
        write_out(o_cw, o_cw_h, 7)
        for cp in outs:
            cp.wait()
        for cp in sends:
            cp.wait_send()

    out_shape = [pltpu.HBM((NSHARD,) + s, BF16) for s in shapes]
    out_shape.append(pltpu.HBM((NSHARD,) + conv_w.shape, F32))
    out_shape += [pltpu.HBM((T, 2 * DG), F32), pltpu.HBM((T, 4 * DG), BF16), pltpu.HBM((T, D), BF16)]
    hbm_args = [pltpu.with_memory_space_constraint(a, pltpu.HBM) for a in (x, nw, w_in, w_out, w_pg, w_pp, conv_w)]
    return pl.pallas_call(
        body, name="gather_inproj", out_shape=out_shape,
        in_specs=[HBM_SPEC] * 7, out_specs=[HBM_SPEC] * 8,
        scratch_shapes=[pltpu.VMEM(a.shape, F32) for a in (w_in, w_out, w_pg, w_pp, conv_w, nw)]
        + [pltpu.VMEM((2, tm, D), F32), pltpu.VMEM((T, D), BF16),
           pltpu.VMEM((2, tm, DG), F32), pltpu.VMEM((2, tm, SHW), BF16),
           pltpu.VMEM((NSHARD,) + w_in.shape, BF16), pltpu.VMEM(w_out.shape, BF16), pltpu.VMEM(w_pg.shape, BF16),
           pltpu.VMEM(w_pp.shape, BF16), pltpu.VMEM((NSHARD,) + conv_w.shape, F32),
           pltpu.SemaphoreType.DMA((6,)), pltpu.SemaphoreType.DMA((2,)), pltpu.SemaphoreType.DMA((2,)),
           pltpu.SemaphoreType.DMA((2,)), pltpu.SemaphoreType.DMA((2,)), pltpu.SemaphoreType.DMA((8,)),
           pltpu.SemaphoreType.DMA((9,)), pltpu.SemaphoreType.DMA((9,))],
        compiler_params=pltpu.CompilerParams(vmem_limit_bytes=48 * MIB),
    )(*hbm_args)


def _gather_rest_start(lands):
    n = len(lands)

    def body(*refs):
        land_in = refs[0:n]
        ssem, rsem = refs[n], refs[n + 1]
        token = refs[2 * n + 2]
        x, y, c, chips = _mesh_pos()
        kme = 2 * x + y
        for p, land in enumerate(land_in):
            hrows = land.shape[1] // 2
            mine = pl.ds(pl.multiple_of(c * hrows, 128), hrows)
            for px, py in chips:
                for pc in range(2):
                    _remote(land.at[kme, mine], land.at[kme, mine], ssem.at[p], rsem.at[p], (px, py, pc)).start()
        token[...] = jnp.zeros_like(token)

    out_shape = ([pltpu.SemaphoreType.DMA((n,)), pltpu.SemaphoreType.DMA((n,))]
                 + [pltpu.HBM(a.shape, a.dtype) for a in lands] + [jax.ShapeDtypeStruct((8, 128), F32)])
    outs = pl.pallas_call(
        body, name="gather_rest_start", out_shape=out_shape,
        in_specs=[HBM_SPEC] * n, out_specs=[SEM_SPEC, SEM_SPEC] + [HBM_SPEC] * n + [VMEM_SPEC],
        input_output_aliases={i: 2 + i for i in range(n)},
        compiler_params=pltpu.CompilerParams(has_side_effects=EFFECT),
    )(*[pltpu.with_memory_space_constraint(a, pltpu.HBM) for a in lands])
    return outs[0], outs[1], list(outs[2:2 + n]), outs[2 + n]


def _gather_rest_wait(ssem, rsem, lands, after):
    n = len(lands)

    def body(*refs):
        land_in = refs[0:n]
        ssem_ref, rsem_ref = refs[n], refs[n + 1]
        x, y, c = lax.axis_index("x"), lax.axis_index("y"), lax.axis_index("c")
        for p, land in enumerate(land_in):
            three = land.at[pl.ds(0, 3)]
            cp = _remote(three, three, ssem_ref.at[p], rsem_ref.at[p], (x, y, c))
            cp.wait_send()
            cp.wait_recv()

    outs = pl.pallas_call(
        body, name="gather_rest_wait", out_shape=[pltpu.HBM(a.shape, a.dtype) for a in lands],
        in_specs=[HBM_SPEC] * n + [SEM_SPEC, SEM_SPEC, ANY_SPEC], out_specs=[HBM_SPEC] * n,
        input_output_aliases={i: i for i in range(n)},
        compiler_params=pltpu.CompilerParams(has_side_effects=EFFECT),
    )(*lands, ssem, rsem, after)
    return list(outs)


def _rs_start(name, parts, ks, lands, chip_sums=False):
    n = len(parts)

    def body(*refs):
        part_in, land_in = refs[0:n], refs[n:2 * n]
        ssem, rsem = refs[2 * n], refs[2 * n + 1]
        token = refs[4 * n + 2]
        x, y, c = lax.axis_index("x"), lax.axis_index("y"), lax.axis_index("c")
        kme = 2 * x + y
        me = 4 * x + 2 * y + c
        for p in range(n):
            hrows = land_in[p].shape[1]
            for i, k in enumerate(ks):
                if chip_sums:
                    @pl.when(kme != k)
                    def _():
                        _remote(part_in[p].at[i], land_in[p].at[kme], ssem.at[p], rsem.at[p], (k // 2, k % 2, c)).start()
                    continue
                for pc in range(2):
                    @pl.when(jnp.logical_or(kme != k, c != pc))
                    def _():
                        _remote(part_in[p].at[i, pl.ds(pc * hrows, hrows)], land_in[p].at[me],
                                ssem.at[p], rsem.at[p], (k // 2, k % 2, pc)).start()
        token[...] = jnp.zeros_like(token)

    arrays = list(parts) + list(lands)
    out_shape = ([pltpu.SemaphoreType.DMA((n,)), pltpu.SemaphoreType.DMA((n,))]
                 + [pltpu.HBM(a.shape, a.dtype) for a in arrays] + [jax.ShapeDtypeStruct((8, 128), F32)])
    outs = pl.pallas_call(
        body, name=name, out_shape=out_shape,
        in_specs=[HBM_SPEC] * (2 * n), out_specs=[SEM_SPEC, SEM_SPEC] + [HBM_SPEC] * (2 * n) + [VMEM_SPEC],
        input_output_aliases={i: 2 + i for i in range(2 * n)},
        compiler_params=pltpu.CompilerParams(has_side_effects=EFFECT),
    )(*[pltpu.with_memory_space_constraint(a, pltpu.HBM) for a in arrays])
    return outs[0], outs[1], list(outs[2:2 + n]), list(outs[2 + n:2 + 2 * n]), outs[2 + 2 * n]


def _rs_wait(name, ssem, rsem, parts, ks, lands, after, chip_sums=False):
    n = len(parts)

    def body(*refs):
        part_in, land_in = refs[0:n], refs[n:2 * n]
        ssem_ref, rsem_ref = refs[2 * n], refs[2 * n + 1]
        x, y, c = lax.axis_index("x"), lax.axis_index("y"), lax.axis_index("c")
        kme = 2 * x + y
        for p in range(n):
            piece = land_in[p].at[0]
            for k in ks:
                for pc in range(1 if chip_sums else 2):
                    mine = (kme == k) if chip_sums else jnp.logical_and(kme == k, c == pc)

                    @pl.when(jnp.logical_not(mine))
                    def _():
                        _remote(piece, piece, ssem_ref.at[p], rsem_ref.at[p], (x, y, c)).wait_send()
            owner = kme == ks[0]
            for k in ks[1:]:
                owner = jnp.logical_or(owner, kme == k)

            @pl.when(owner)
            def _():
                others = land_in[p].at[pl.ds(0, land_in[p].shape[0] - 1)]
                _remote(others, others, ssem_ref.at[p], rsem_ref.at[p], (x, y, c)).wait_recv()

    arrays = list(parts) + list(lands)
    outs = pl.pallas_call(
        body, name=name, out_shape=[pltpu.HBM(a.shape, a.dtype) for a in arrays],
        in_specs=[HBM_SPEC] * (2 * n) + [SEM_SPEC, SEM_SPEC, ANY_SPEC], out_specs=[HBM_SPEC] * (2 * n),
        input_output_aliases={i: i for i in range(2 * n)},
        compiler_params=pltpu.CompilerParams(has_side_effects=EFFECT),
    )(*arrays, ssem, rsem, after)
    return list(outs[0:n]), list(outs[n:2 * n])


def _reduce_exchange(parts, lands):
    shapes = [(2 * l.shape[1], l.shape[2]) for l in lands]
    step = 128

    def body(in01, in23, pout, ppg, ppp, l_in, l_out, l_pg, l_pp, g_in, g_out, g_pg, g_pp, ssem, rsem):
        x, y, c = lax.axis_index("x"), lax.axis_index("y"), lax.axis_index("c")
        kme = 2 * x + y
        me = 4 * x + 2 * y + c
        sibling = (x, y, 1 - c)
        sends = []
        for p, (land, gout) in enumerate([(l_in, g_in), (l_out, g_out), (l_pg, g_pg), (l_pp, g_pp)]):
            hrows = land.shape[1]
            mine0 = pl.multiple_of(c * hrows, step)
            for r0 in range(0, hrows, step):
                rs = pl.ds(mine0 + r0, step)
                if p == 0:
                    own = jnp.where(kme >= 2, in23[kme & 1, r0:r0 + step, :], in01[kme & 1, r0:r0 + step, :])
                    slot = kme
                else:
                    own = (pout, ppg, ppp)[p - 1][kme, rs, :]
                    slot = me
                s = jnp.zeros((step, land.shape[2]), F32)
                for j in range(land.shape[0]):
                    s = s + jnp.where(slot == j, own, land[j, r0:r0 + step, :]).astype(F32)
                gout[rs, :] = s
            mine = pl.ds(mine0, hrows)
            cp = _remote(gout.at[mine], gout.at[mine], ssem.at[p], rsem.at[p], sibling)
            cp.start()
            sends.append(cp)
        for p, (land, gout) in enumerate([(l_in, g_in), (l_out, g_out), (l_pg, g_pg), (l_pp, g_pp)]):
            hrows = land.shape[1]
            other = pl.ds(pl.multiple_of((1 - c) * hrows, step), hrows)
            _remote(gout.at[other], gout.at[other], ssem.at[p], rsem.at[p], sibling).wait_recv()
        for cp in sends:
            cp.wait_send()

    return pl.pallas_call(
        body, name="reduce_exchange", out_shape=[jax.ShapeDtypeStruct(s, F32) for s in shapes],
        in_specs=[VMEM_SPEC] * 9, out_specs=[VMEM_SPEC] * 4,
        scratch_shapes=[pltpu.SemaphoreType.DMA((4,)), pltpu.SemaphoreType.DMA((4,))],
        compiler_params=pltpu.CompilerParams(vmem_limit_bytes=48 * MIB),
    )(*parts, *lands)


def _small_allreduce(sm_in, sm_tail, sm_a, sm_b, rg_c):
    PR = DG // NDEV

    def body(in_ref, tail_ref, a_ref, b_ref, rg_ref, v_out, rg_out, vbuf, vrecv, rgrecv, ssem, rsem):
        x, y, c = lax.axis_index("x"), lax.axis_index("y"), lax.axis_index("c")
        me = 4 * x + 2 * y + c

        def pair(ref, r0, r1):
            return jnp.concatenate([ref[r0:r0 + 1, :], ref[r1:r1 + 1, :]], axis=1)

        rows = {
            ROW_NORM_MIX: in_ref[0:1, :], ROW_FINAL_NORM: tail_ref[1:2, :], ROW_B_PG: tail_ref[2:3, :],
            ROW_PLE_NORM: tail_ref[3:4, :], ROW_CB_BA: pair(a_ref, 0, 1), ROW_BX_LAM: pair(a_ref, 2, 3),
            ROW_CW01: pair(a_ref, 4, 5), ROW_CW23: pair(a_ref, 6, 7), ROW_HG_LB: pair(b_ref, 2, 3),
            ROW_HG_NW: jnp.concatenate([b_ref[1:2, :], jnp.zeros((1, DG), F32)], axis=1),
            ROW_LOSS: tail_ref[4:5, :],
        }
        vbuf[...] = jnp.zeros_like(vbuf)
        for r, row in rows.items():
            for j in range(NDEV):
                vbuf[j, r:r + 1, :] = row[:, j * 128:(j + 1) * 128]

        def peer(mask):
            px = x ^ ((mask >> 2) & 1)
            py = y ^ ((mask >> 1) & 1)
            pc = c ^ (mask & 1)
            return (px, py, pc), 4 * px + 2 * py + pc

        def rg_rows(r):
            return pl.ds(pl.multiple_of(r * PR, PR), PR)

        first = []
        for mask in range(1, NDEV):
            dev, r = peer(mask)
            i = mask - 1
            cp = _remote(vbuf.at[r], vrecv.at[i], ssem.at[i], rsem.at[i], dev)
            cp.start()
            first.append(cp)
            cp = _remote(rg_ref.at[rg_rows(r)], rgrecv.at[i], ssem.at[7 + i], rsem.at[7 + i], dev)
            cp.start()
            first.append(cp)
        sv = vbuf[me]
        sr = rg_ref[rg_rows(me), :]
        for i in range(NDEV - 1):
            first[2 * i].wait_recv()
            first[2 * i + 1].wait_recv()
            sv = sv + vrecv[i]
            sr = sr + rgrecv[i]
        v_out[me] = sv
        rg_out[rg_rows(me), :] = sr
        second = []
        for mask in range(1, NDEV):
            dev, r = peer(mask)
            i = mask - 1
            cp = _remote(v_out.at[me], v_out.at[me], ssem.at[14 + i], rsem.at[14 + i], dev)
            cp.start()
            second.append(cp)
            cp = _remote(rg_out.at[rg_rows(me)], rg_out.at[rg_rows(me)], ssem.at[21 + i], rsem.at[21 + i], dev)
            cp.start()
            second.append(cp)
        for mask in range(1, NDEV):
            dev, r = peer(mask)
            i = mask - 1
            _remote(v_out.at[r], v_out.at[r], ssem.at[14 + i], rsem.at[14 + i], dev).wait_recv()
            _remote(rg_out.at[rg_rows(r)], rg_out.at[rg_rows(r)], ssem.at[21 + i], rsem.at[21 + i], dev).wait_recv()
        for cp in first + second:
            cp.wait_send()

    return pl.pallas_call(
        body, name="small_allreduce",
        out_shape=[jax.ShapeDtypeStruct((NDEV, VROWS, 128), F32), jax.ShapeDtypeStruct((DG, 128), F32)],
        in_specs=[VMEM_SPEC] * 5, out_specs=[VMEM_SPEC] * 2,
        scratch_shapes=[pltpu.VMEM((NDEV, VROWS, 128), F32), pltpu.VMEM((NDEV - 1, VROWS, 128), F32),
                        pltpu.VMEM((NDEV - 1, PR, 128), F32),
                        pltpu.SemaphoreType.DMA((28,)), pltpu.SemaphoreType.DMA((28,))],
    )(sm_in, sm_tail, sm_a, sm_b, rg_c)


def _final_reduce(parts, lands, sm_in, sm_tail, sm_a, sm_b, rg_c):
    shapes = [(2 * l.shape[1], l.shape[2]) for l in lands]
    step = 128
    PR = DG // NDEV

    def body(pin, pout, ppg, ppp, l_in, l_out, l_pg, l_pp, in_h, tail_h, a_h, b_h, rg_h,
             g_in, g_out, g_pg, g_pp, v_out, rg_out, vbuf, vrecv, rgrecv,
             lb_in, lb_out, lb_pg, lb_pp, ob_in, ob_out, ob_pg, ob_pp,
             in_ref, tail_ref, a_ref, b_ref, rg_ref, lsem, ssem_l, bsem_s, bsem_r, ssem, rsem):
        x, y, c = lax.axis_index("x"), lax.axis_index("y"), lax.axis_index("c")
        kme = 2 * x + y
        me = 4 * x + 2 * y + c
        sibling = (x, y, 1 - c)
        small_fetch = [pltpu.make_async_copy(src, dst, ssem_l.at[i]) for i, (src, dst) in enumerate(
            [(in_h, in_ref), (tail_h, tail_ref), (a_h, a_ref), (b_h, b_ref), (rg_h, rg_ref)])]
        for cp in small_fetch:
            cp.start()

        lands_hbm = [l_in, l_out, l_pg, l_pp]
        land_bufs = [lb_in, lb_out, lb_pg, lb_pp]
        own_bufs = [ob_in, ob_out, ob_pg, ob_pp]
        fetches = []
        for p in range(4):
            cp = pltpu.make_async_copy(lands_hbm[p], land_bufs[p], lsem.at[p])
            cp.start()
            fetches.append(cp)

        own_fetches = [pltpu.make_async_copy(pin.at[kme], ob_in, lsem.at[4])]
        own_fetches[0].start()
        for p, part in enumerate([pout, ppg, ppp]):
            hrows = own_bufs[p + 1].shape[0]
            cp = pltpu.make_async_copy(part.at[kme, pl.ds(pl.multiple_of(c * hrows, step), hrows)],
                                       own_bufs[p + 1], lsem.at[5 + p])
            cp.start()
            own_fetches.append(cp)

        def pair(ref, r0, r1):
            return jnp.concatenate([ref[r0:r0 + 1, :], ref[r1:r1 + 1, :]], axis=1)

        for cp in small_fetch:
            cp.wait()
        rows = {
            ROW_NORM_MIX: in_ref[0:1, :], ROW_FINAL_NORM: tail_ref[1:2, :], ROW_B_PG: tail_ref[2:3, :],
            ROW_PLE_NORM: tail_ref[3:4, :], ROW_CB_BA: pair(a_ref, 0, 1), ROW_BX_LAM: pair(a_ref, 2, 3),
            ROW_CW01: pair(a_ref, 4, 5), ROW_CW23: pair(a_ref, 6, 7), ROW_HG_LB: pair(b_ref, 2, 3),
            ROW_HG_NW: jnp.concatenate([b_ref[1:2, :], jnp.zeros((1, DG), F32)], axis=1),
            ROW_LOSS: tail_ref[4:5, :],
        }
        vbuf[...] = jnp.zeros_like(vbuf)
        for r, row in rows.items():
            for j in range(NDEV):
                vbuf[j, r:r + 1, :] = row[:, j * 128:(j + 1) * 128]

        def peer(mask):
            px = x ^ ((mask >> 2) & 1)
            py = y ^ ((mask >> 1) & 1)
            pc = c ^ (mask & 1)
            return (px, py, pc), 4 * px + 2 * py + pc

        def rg_rows(r):
            return pl.ds(pl.multiple_of(r * PR, PR), PR)

        first = []
        for mask in range(1, NDEV):
            dev, r = peer(mask)
            i = mask - 1
            cp = _remote(vbuf.at[r], vrecv.at[i], ssem.at[i], rsem.at[i], dev)
            cp.start()
            first.append(cp)
            cp = _remote(rg_ref.at[rg_rows(r)], rgrecv.at[i], ssem.at[7 + i], rsem.at[7 + i], dev)
            cp.start()
            first.append(cp)

        big = [(lb_in, g_in), (lb_out, g_out), (lb_pg, g_pg), (lb_pp, g_pp)]
        swaps = []
        for p, (land, gout) in enumerate(big):
            fetches[p].wait()
            own_fetches[p].wait()
            hrows = land.shape[1]
            mine0 = pl.multiple_of(c * hrows, step)
            slot = kme if p == 0 else me
            for r0 in range(0, hrows, step):
                rs = pl.ds(mine0 + r0, step)
                own = own_bufs[p][r0:r0 + step, :]
                s = jnp.zeros((step, land.shape[2]), F32)
                for j in range(land.shape[0]):
                    s = s + jnp.where(slot == j, own, land[j, r0:r0 + step, :]).astype(F32)
                gout[rs, :] = s
            mine = pl.ds(mine0, hrows)
            cp = _remote(gout.at[mine], gout.at[mine], bsem_s.at[p], bsem_r.at[p], sibling)
            cp.start()
            swaps.append(cp)

        sv = vbuf[me]
        sr = rg_ref[rg_rows(me), :]
        for i in range(NDEV - 1):
            first[2 * i].wait_recv()
            first[2 * i + 1].wait_recv()
            sv = sv + vrecv[i]
            sr = sr + rgrecv[i]
        v_out[me] = sv
        rg_out[rg_rows(me), :] = sr
        second = []
        for mask in range(1, NDEV):
            dev, r = peer(mask)
            i = mask - 1
            cp = _remote(v_out.at[me], v_out.at[me], ssem.at[14 + i], rsem.at[14 + i], dev)
            cp.start()
            second.append(cp)
            cp = _remote(rg_out.at[rg_rows(me)], rg_out.at[rg_rows(me)], ssem.at[21 + i], rsem.at[21 + i], dev)
            cp.start()
            second.append(cp)
        for p, (land, gout) in enumerate(big):
            hrows = land.shape[1]
            other = pl.ds(pl.multiple_of((1 - c) * hrows, step), hrows)
            _remote(gout.at[other], gout.at[other], bsem_s.at[p], bsem_r.at[p], sibling).wait_recv()
        for mask in range(1, NDEV):
            dev, r = peer(mask)
            i = mask - 1
            _remote(v_out.at[r], v_out.at[r], ssem.at[14 + i], rsem.at[14 + i], dev).wait_recv()
            _remote(rg_out.at[rg_rows(r)], rg_out.at[rg_rows(r)], ssem.at[21 + i], rsem.at[21 + i], dev).wait_recv()
        for cp in first + swaps + second:
            cp.wait_send()

    out_shape = [jax.ShapeDtypeStruct(s, F32) for s in shapes]
    out_shape += [jax.ShapeDtypeStruct((NDEV, VROWS, 128), F32), jax.ShapeDtypeStruct((DG, 128), F32)]
    outs = pl.pallas_call(
        body, name="final_reduce", out_shape=out_shape,
        in_specs=[HBM_SPEC] * 13, out_specs=[VMEM_SPEC] * 6,
        scratch_shapes=[pltpu.VMEM((NDEV, VROWS, 128), F32), pltpu.VMEM((NDEV - 1, VROWS, 128), F32),
                        pltpu.VMEM((NDEV - 1, PR, 128), F32)]
        + [pltpu.VMEM(l.shape, BF16) for l in lands]
        + [pltpu.VMEM(l.shape[1:], BF16) for l in lands]
        + [pltpu.VMEM(a.shape, F32) for a in (sm_in, sm_tail, sm_a, sm_b, rg_c)]
        + [pltpu.SemaphoreType.DMA((8,)), pltpu.SemaphoreType.DMA((5,)),
                        pltpu.SemaphoreType.DMA((4,)), pltpu.SemaphoreType.DMA((4,)),
                        pltpu.SemaphoreType.DMA((28,)), pltpu.SemaphoreType.DMA((28,))],
        compiler_params=pltpu.CompilerParams(vmem_limit_bytes=48 * MIB),
    )(*[pltpu.with_memory_space_constraint(a, pltpu.HBM)
        for a in (*parts, *lands, sm_in, sm_tail, sm_a, sm_b, rg_c)])
    return list(outs[0:4]), outs[4], outs[5]


def _adam_rows(w, g, m, v):
    m2 = ADAM_B1 * m + (1.0 - ADAM_B1) * g
    v2 = ADAM_B2 * v + (1.0 - ADAM_B2) * (g * g)
    m_hat = m2 / (1.0 - ADAM_B1 ** ADAM_STEP)
    v_hat = v2 / (1.0 - ADAM_B2 ** ADAM_STEP)
    delta = -ADAM_LR * (m_hat / (jnp.sqrt(v_hat) + ADAM_EPS) + ADAM_WD * w)
    return delta, m2, v2


def _adam_big(gs, ws, ms, vs):
    n = len(gs)
    steps = 8

    def body(*refs):
        ins, outs = refs[:4 * n], refs[4 * n:]
        for i in range(n):
            g, w, m, v = (r[...] for r in ins[4 * i:4 * i + 4])
            d, m2, v2 = _adam_rows(w, g, m, v)
            outs[3 * i][...] = d
            outs[3 * i + 1][...] = m2
            outs[3 * i + 2][...] = v2

    in_specs, out_specs, out_shape, args = [], [], [], []
    for g, w, m, v in zip(gs, ws, ms, vs):
        r, c = w.shape
        spec = lambda: pl.BlockSpec((r // steps, c), lambda i: (i, 0))
        in_specs += [spec() for _ in range(4)]
        out_specs += [spec() for _ in range(3)]
        out_shape += [jax.ShapeDtypeStruct((r, c), F32)] * 3
        args += [pltpu.with_memory_space_constraint(a, pltpu.HBM) for a in (g, w, m, v)]
    outs = pl.pallas_call(
        body, name="adam_big", grid=(steps,), in_specs=in_specs, out_specs=out_specs, out_shape=out_shape,
        compiler_params=_cparams(("parallel",), 32),
    )(*args)
    return [tuple(outs[3 * i:3 * i + 3]) for i in range(n)]


_VEC_PARAMS = [
    ("norm_mix_w", ROW_NORM_MIX, 0, D), ("final_norm_w", ROW_FINAL_NORM, 0, D),
    ("b_ple_gate", ROW_B_PG, 0, D), ("ple_norm_w", ROW_PLE_NORM, 0, D),
    ("conv_b", ROW_CB_BA, 0, DG), ("rg_ba", ROW_CB_BA, DG, DG),
    ("rg_bx", ROW_BX_LAM, 0, DG), ("rg_lambda", ROW_BX_LAM, DG, DG),
    ("hg_norm_w", ROW_HG_NW, 0, HD),
]
_SMALL_ORDER = [n for n, _, _, _ in _VEC_PARAMS] + ["hg_lb", "conv_w", "rg_wa", "rg_wx"]


def _adam_small(vred, rgred, ws, ms, vs):
    names = _SMALL_ORDER
    n = len(names)

    def body(vred_ref, rg_ref, *refs):
        w_refs = dict(zip(names, refs[0:n]))
        m_refs = dict(zip(names, refs[n:2 * n]))
        v_refs = dict(zip(names, refs[2 * n:3 * n]))
        outs = refs[3 * n:]
        o_refs = {nm: outs[4 * i:4 * i + 4] for i, nm in enumerate(names)}
        kme = 2 * lax.axis_index("x") + lax.axis_index("y")

        def update(nm, g, idx):
            d, m2, v2 = _adam_rows(w_refs[nm][idx], g, m_refs[nm][idx], v_refs[nm][idx])
            og, od, om, ov = o_refs[nm]
            og[idx] = g
            od[idx] = d
            om[idx] = m2
            ov[idx] = v2

        def packed(row, lane0, width):
            return jnp.concatenate([vred_ref[j, row:row + 1, :] for j in range(lane0 // 128, (lane0 + width) // 128)],
                                   axis=1)

        everything = (slice(None), slice(None))
        for nm, row, lane0, width in _VEC_PARAMS:
            update(nm, packed(row, lane0, width), everything)
        for r in range(2):
            update("hg_lb", packed(ROW_HG_LB, r * DG, DG), (slice(r, r + 1), slice(None)))
        for j in range(4):
            g = vred_ref[(j % 2) * 4 + kme, ROW_CW01 + j // 2:ROW_CW01 + j // 2 + 1, :]
            update("conv_w", g, (slice(j, j + 1), slice(None)))
        for r0 in range(0, DG, 128):
            rs = (slice(r0, r0 + 128), slice(None))
            both = rg_ref[r0:r0 + 128, :]
            update("rg_wa", both[:, 0:RGB], rs)
            update("rg_wx", pltpu.roll(both, RGB, 1)[:, 0:RGB], rs)

    args = [vred, rgred] + [d[nm] for d in (ws, ms, vs) for nm in names]
    out_shape = []
    for nm in names:
        out_shape += [jax.ShapeDtypeStruct(ws[nm].shape, F32)] * 4
    whole = lambda s: pl.BlockSpec(s.shape, lambda i, nd=len(s.shape): (0,) * nd)
    outs = pl.pallas_call(
        body, name="adam_small", out_shape=out_shape, grid=(1,),
        in_specs=[whole(a) for a in args], out_specs=[whole(s) for s in out_shape],
    )(*args)
    return {nm: tuple(outs[4 * i:4 * i + 4]) for i, nm in enumerate(names)}


def _fwd_inproj(x, nw, w_in_b, dep):
    tm = 512

    def body(x_ref, nw_ref, w_ref, dep_ref, pf_ref, pb_ref, u_ref):
        xv = x_ref[...]
        s = lax.rsqrt(jnp.mean(xv * xv, axis=-1, keepdims=True) + EPS)
        u = (xv * s * nw_ref[...]).astype(BF16)
        u_ref[...] = u
        r = [jnp.dot(u, w_ref[k], preferred_element_type=F32) for k in range(NSHARD)]
        h = DG // 2
        pf_ref[:, 0:DG] = r[0][:, 0:DG]
        pb_ref[:, 0:h] = r[0][:, DG:SHW].astype(BF16)
        pb_ref[:, h:DG] = r[1][:, 0:h].astype(BF16)
        pb_ref[:, DG:2 * DG] = r[1][:, h:SHW].astype(BF16)
        pf_ref[:, DG:2 * DG] = r[2][:, 0:DG]
        pb_ref[:, 2 * DG:2 * DG + h] = r[2][:, DG:SHW].astype(BF16)
        pb_ref[:, 2 * DG + h:3 * DG] = r[3][:, 0:h].astype(BF16)
        pb_ref[:, 3 * DG:4 * DG] = r[3][:, h:SHW].astype(BF16)

    return pl.pallas_call(
        body, name="fwd_inproj", grid=(T // tm,),
        in_specs=[pl.BlockSpec((tm, D), lambda i: (i, 0)),
                  pl.BlockSpec((1, D), lambda i: (0, 0)),
                  pl.BlockSpec((NSHARD, D, SHW), lambda i: (0, 0, 0), pipeline_mode=pl.Buffered(1)), ANY_SPEC],
        out_specs=[pl.BlockSpec((tm, 2 * DG), lambda i: (i, 0)),
                   pl.BlockSpec((tm, 4 * DG), lambda i: (i, 0)),
                   pl.BlockSpec((tm, D), lambda i: (i, 0))],
        out_shape=[jax.ShapeDtypeStruct((T, 2 * DG), F32), jax.ShapeDtypeStruct((T, 4 * DG), BF16),
                   jax.ShapeDtypeStruct((T, D), BF16)],
        compiler_params=_cparams(("parallel",), 48),
    )(x, nw, w_in_b, dep)


def _conv_rows(cw_ref):
    return [jnp.concatenate([cw_ref[k, j:j + 1, :] for k in range(NSHARD)], axis=1) for j in range(4)]


def _rg_conv(xa, prev8, cw, cb, rows):
    taps = [_shift_down(xa, prev8, 3, rows), _shift_down(xa, prev8, 2, rows),
            _shift_down(xa, prev8, 1, rows), xa]
    xc = cb
    for j in range(4):
        xc = xc + taps[j] * cw[j]
    return xc, taps


def _block_mask():
    r = lax.broadcasted_iota(I32, (DG, DG), 0)
    c = lax.broadcasted_iota(I32, (DG, DG), 1)
    return (r >> 6) == (c >> 6)


def _dense_from_blocks(wc):
    j = lax.broadcasted_iota(I32, (RGB, DG), 0)
    c = lax.broadcasted_iota(I32, (RGB, DG), 1)
    spread = _mm_exact(wc, ((c & (RGB - 1)) == j).astype(F32))
    return jnp.where(_block_mask(), spread, 0.0)


def _blocks_from_dense(da, dx):
    c = lax.broadcasted_iota(I32, (DG, 128), 0)
    j = lax.broadcasted_iota(I32, (DG, 128), 1)
    hit = (c & (RGB - 1)) == (j & (RGB - 1))
    mask = _block_mask()
    return (_mm_exact(jnp.where(mask, da, 0.0), (hit & (j < RGB)).astype(F32))
            + _mm_exact(jnp.where(mask, dx, 0.0), (hit & (j >= RGB)).astype(F32)))


def _rg_gates(xc, wa, ba, wx, bx, sp, first_row):
    r = _sigmoid(_mm(xc, wa) + ba)
    i = _sigmoid(_mm(xc, wx) + bx)
    log_a = (-RG_C) * r * sp
    a = jnp.exp(log_a)
    a2 = a * a
    one_m_a2 = -jnp.tanh(log_a) * (a2 + 1.0)
    mult = jnp.where(first_row, 1.0, jnp.sqrt(one_m_a2))
    return r, i, a, a2, mult


def _softplus(z):
    return jnp.maximum(z, 0.0) + jnp.log1p(jnp.exp(-jnp.abs(z)))


def _fwd_rglru(pf, pb, cw3, conv_b, wa_c, ba, wx_c, bx, lam, dep):
    tm = 512
    ng = tm // 8

    def body(xa_ref, ga_ref, cw_ref, cb_ref, wa_ref, ba_ref, wx_ref, bx_ref, lam_ref, dep_ref,
             h_ref, ya_ref, a_s, u_s, tail_s, hc_s, wa_s, wx_s):
        i = pl.program_id(0)

        @pl.when(i == 0)
        def _():
            tail_s[...] = jnp.zeros_like(tail_s)
            hc_s[...] = jnp.zeros_like(hc_s)
            wa_s[...] = _dense_from_blocks(wa_ref[...]).astype(BF16)
            wx_s[...] = _dense_from_blocks(wx_ref[...]).astype(BF16)

        rows = lax.broadcasted_iota(I32, (tm, DG), 0)
        xa = xa_ref[...]
        xc, _ = _rg_conv(xa, tail_s[...], _conv_rows(cw_ref), cb_ref[...], rows)
        tail_s[...] = xa[tm - 8:tm, :]
        sp = _softplus(-lam_ref[...])
        rp = _mm(xc, wa_s[...]) + ba_ref[...]
        ip = _mm(xc, wx_s[...]) + bx_ref[...]
        rb = 64
        rows_b = lax.broadcasted_iota(I32, (rb, DG), 0)
        rows8 = rows_b & 7
        carry = hc_s[0:1, :]
        for b0 in range(0, tm, rb):
            sl = slice(b0, b0 + rb)
            r = _sigmoid(rp[sl])
            ig = _sigmoid(ip[sl])
            log_a = (-RG_C) * r * sp
            av = jnp.exp(log_a)
            mult = jnp.sqrt(-jnp.tanh(log_a) * (av * av + 1.0))
            if b0 == 0:
                mult = jnp.where((rows_b + i * tm) == 0, 1.0, mult)
            uv = mult * (ig * xc[sl])
            for d in (1, 2, 4):
                keep = rows8 >= d
                uv = uv + av * jnp.where(keep, _roll_in_groups(uv, d), 0.0)
                av = av * jnp.where(keep, _roll_in_groups(av, d), 1.0)
            ga = ga_ref[sl, :].astype(F32)
            gate = ga * _sigmoid(ga)
            hs = []
            for g in range(rb // 8):
                gs = slice(g * 8, (g + 1) * 8)
                hv = uv[gs] + av[gs] * carry
                carry = hv[7:8, :]
                hs.append(hv)
            hb = jnp.concatenate(hs, axis=0)
            h_ref[sl, :] = hb
            ya_ref[sl, :] = (hb * gate).astype(BF16)
        hc_s[0:1, :] = carry

    vec = lambda: pl.BlockSpec((1, DG), lambda i: (0, 0))
    blocks = lambda: pl.BlockSpec((DG, RGB), lambda i: (0, 0))
    return pl.pallas_call(
        body, name="fwd_rglru", grid=(T // tm,),
        in_specs=[pl.BlockSpec((tm, DG), lambda i: (i, 0)),
                  pl.BlockSpec((tm, DG), lambda i: (i, 0)),
                  pl.BlockSpec((NSHARD, 4, 128), lambda i: (0, 0, 0)), vec(),
                  blocks(), vec(), blocks(), vec(), vec(), ANY_SPEC],
        out_specs=[pl.BlockSpec((tm, DG), lambda i: (i, 0)),
                   pl.BlockSpec((tm, DG), lambda i: (i, 0))],
        out_shape=[jax.ShapeDtypeStruct((T, DG), F32), pltpu.HBM((T, DG), BF16)],
        scratch_shapes=[pltpu.VMEM((tm, DG), F32), pltpu.VMEM((tm, DG), F32),
                        pltpu.VMEM((8, DG), F32), pltpu.VMEM((8, DG), F32),
                        pltpu.VMEM((DG, DG), BF16), pltpu.VMEM((DG, DG), BF16)],
        compiler_params=_cparams(("arbitrary",), 48),
    )(pf, pb, cw3, conv_b, wa_c, ba, wx_c, bx, lam, dep)


def _hg_lower_bound(lb_ref):
    return _sig_pair(lb_ref[0:1, :] - lb_ref[1:2, :])


def _hg_gates(fz, lb, one_m_lb):
    sg, sn = _sig_pair(fz)
    f = lb + one_m_lb * sg
    return sg, sn, f, jnp.log(f), one_m_lb * sn


def _tri(lower):
    r = lax.broadcasted_iota(I32, (CH, CH), 0)
    c = lax.broadcasted_iota(I32, (CH, CH), 1)
    return (r >= c) if lower else (r <= c)


def _split3(v):
    hi = v.astype(BF16)
    r1 = v - hi.astype(F32)
    mid = r1.astype(BF16)
    lo = (r1 - mid.astype(F32)).astype(BF16)
    return hi, mid, lo


def _chunk_cumsum(v, rows64, reverse=False):
    del rows64
    tri = _tri(not reverse).astype(BF16)
    out = []
    for c in range(v.shape[0] // CH):
        pieces = _split3(v[c * CH:(c + 1) * CH])
        out.append(sum(jnp.dot(tri, p, preferred_element_type=F32) for p in pieces))
    return jnp.concatenate(out, axis=0)


def _chunk_rev_cumsum(v, rows64):
    return _chunk_cumsum(v, rows64, reverse=True)


def _hg_recompute(q_ref, f_ref, lb, one_m_lb, rows64, eb_s, enb_s, ekd_s, qe_s, ke_s, kd_s, dec_s):
    nc = q_ref.shape[0] // CH
    sg, sn, f, logf, k = _hg_gates(f_ref[...], lb, one_m_lb)
    q = q_ref[...].astype(F32)
    sq = _sigmoid(q)
    qs = q * sq * (HD ** -0.5)
    b = _chunk_cumsum(logf, rows64)
    for c in range(nc):
        rs = slice(c * CH, (c + 1) * CH)
        b_c = b[rs]
        bl = b_c[CH - 1:CH, :]
        eb, enb, ekd = jnp.exp(b_c), jnp.exp(-b_c), jnp.exp(bl - b_c)
        if eb_s is not None:
            eb_s[rs, :] = eb
            enb_s[rs, :] = enb
            ekd_s[rs, :] = ekd
        qe_s[rs, :] = (qs[rs] * eb).astype(BF16)
        ke_s[rs, :] = (k[rs] * enb).astype(BF16)
        kd_s[rs, :] = (k[rs] * ekd).astype(BF16)
        dec_s[c:c + 1, :] = jnp.exp(bl)
    return sg, sn, f, k, q, sq, qs


def _fwd_hgrn2(pf, pb, hg_lb, hg_nw):
    tm = 512
    nc = tm // CH

    def body(q_ref, f_ref, v_ref, g_ref, lb_ref, nw_ref, yb_ref, o_ref, sp_ref,
             st_s, qe_s, ke_s, kd_s, vb_s, dec_s, p_s, ds_s):
        i = pl.program_id(0)

        @pl.when(i == 0)
        def _():
            st_s[...] = jnp.zeros_like(st_s)

        lb, one_m_lb = _hg_lower_bound(lb_ref)
        rows64 = lax.broadcasted_iota(I32, (tm, DG), 0) & (CH - 1)
        _hg_recompute(q_ref, f_ref, lb, one_m_lb, rows64, None, None, None, qe_s, ke_s, kd_s, dec_s)
        vb_s[...] = v_ref[...]
        mask = _tri(True)
        items = [(c, hd, slice(c * CH, (c + 1) * CH), slice(hd * HD, (hd + 1) * HD))
                 for c in range(nc) for hd in range(NH)]
        for c, hd, rs, cols in items:
            p_s[c * NH + hd] = jnp.where(mask, _mm_nt(qe_s[rs, cols], ke_s[rs, cols]), 0.0).astype(BF16)
            ds_s[c * NH + hd] = _mm_tn(vb_s[rs, cols], kd_s[rs, cols])
        for c, hd, rs, cols in items:
            st = st_s[hd]
            sp_ref[hd, c] = st
            st_s[hd] = st * dec_s[c:c + 1, cols] + ds_s[c * NH + hd]
        for c, hd, rs, cols in items:
            o_ref[rs, cols] = _mm(p_s[c * NH + hd], vb_s[rs, cols]) + _mm_nt(qe_s[rs, cols], sp_ref[hd, c])
        nw = nw_ref[...]
        for hd in range(NH):
            cols = slice(hd * HD, (hd + 1) * HD)
            o = o_ref[:, cols]
            so = lax.rsqrt(jnp.mean(o * o, axis=-1, keepdims=True) + EPS)
            g = g_ref[:, cols].astype(F32)
            sg = _sigmoid(g)
            yb_ref[:, cols] = (o * so * nw * (g * sg)).astype(BF16)

    col = lambda j: pl.BlockSpec((tm, DG), lambda i: (i, j))
    return pl.pallas_call(
        body, name="fwd_hgrn2", grid=(T // tm,),
        in_specs=[col(1), col(1), col(2), col(3),
                  pl.BlockSpec((2, DG), lambda i: (0, 0)),
                  pl.BlockSpec((1, HD), lambda i: (0, 0))],
        out_specs=[pl.BlockSpec((tm, DG), lambda i: (i, 0)),
                   pl.BlockSpec((tm, DG), lambda i: (i, 0)),
                   pl.BlockSpec((NH, nc, HD, HD), lambda i: (0, i, 0, 0))],
        out_shape=[pltpu.HBM((T, DG), BF16), jax.ShapeDtypeStruct((T, DG), F32),
                   jax.ShapeDtypeStruct((NH, NCHUNK, HD, HD), F32)],
        scratch_shapes=[pltpu.VMEM((NH, HD, HD), F32),
                        pltpu.VMEM((tm, DG), BF16), pltpu.VMEM((tm, DG), BF16), pltpu.VMEM((tm, DG), BF16),
                        pltpu.VMEM((tm, DG), BF16), pltpu.VMEM((nc, DG), F32),
                        pltpu.VMEM((nc * NH, CH, CH), BF16), pltpu.VMEM((nc * NH, HD, HD), F32)],
        compiler_params=_cparams(("arbitrary",), 48),
    )(pb, pf, pb, pb, hg_lb, hg_nw)


def _tail_fwd_bwd(x, p, tgt, ya, yb, w_out_b, w_pg_b, w_pp_b, ple_nw, b_pg, fnw):
    tm = 512
    nt = T // tm
    QR = D // NSHARD

    def body(x_ref, p_ref, t_ref, ya_ref, yb_ref, wo_ref, wg_ref, wp_ref, pw_ref, b_ref, fw_ref,
             dh1_ref, dyab_ref, dwo_ref, dwg_ref, dwp_ref, sm_ref, dwo_s, dwg_s, dwp_s):
        i = pl.program_id(0)

        @pl.when(i == 0)
        def _():
            dwo_s[...] = jnp.zeros_like(dwo_s)
            dwg_s[...] = jnp.zeros_like(dwg_s)
            dwp_s[...] = jnp.zeros_like(dwp_s)
            sm_ref[...] = jnp.zeros_like(sm_ref)

        ya = ya_ref[...]
        yb = yb_ref[...]
        pv = p_ref[...].astype(BF16)
        pw = pw_ref[...]
        fw = fw_ref[...]
        h1 = x_ref[...] + _mm(ya, wo_ref[0:DG, :]) + _mm(yb, wo_ref[DG:D, :])
        s2 = lax.rsqrt(jnp.mean(h1 * h1, axis=-1, keepdims=True) + EPS)
        n2h = h1 * s2
        n2 = (n2h * pw).astype(BF16)
        z = _mm(n2, wg_ref[...]) + b_ref[...]
        gate = _sigmoid(z)
        pp = jnp.concatenate([_mm(pv, wp_ref[k]) for k in range(NSHARD)], axis=1)
        h2 = h1 + gate * pp
        s3 = lax.rsqrt(jnp.mean(h2 * h2, axis=-1, keepdims=True) + EPS)
        hn = h2 * s3
        err = hn * fw - t_ref[...]
        sm_ref[0:1, :] += _rowsum(err * err)
        dy = err * (1.0 / D)
        sm_ref[1:2, :] += _rowsum(dy * hn)
        g3 = dy * fw
        dh2 = s3 * (g3 - hn * jnp.mean(g3 * hn, axis=-1, keepdims=True))
        dpp = (dh2 * gate).astype(BF16)
        dz = dh2 * pp * gate * (1.0 - gate)
        sm_ref[2:3, :] += _rowsum(dz)
        dzb = dz.astype(BF16)
        dwg_s[...] += _mm_tn(n2, dzb)
        dn2 = _mm_nt(dzb, wg_ref[...])
        for k in range(NSHARD):
            dwp_s[k] += _mm_tn(pv, dpp[:, k * PLE:(k + 1) * PLE])
        sm_ref[3:4, :] += _rowsum(dn2 * n2h)
        g2 = dn2 * pw
        dh1 = dh2 + s2 * (g2 - n2h * jnp.mean(g2 * n2h, axis=-1, keepdims=True))
        dh1_ref[...] = dh1
        dh1b = dh1.astype(BF16)
        dyab_ref[...] = _mm_nt(dh1b, wo_ref[...])
        dwo_s[0:DG, :] += _mm_tn(ya, dh1b)
        dwo_s[DG:D, :] += _mm_tn(yb, dh1b)

        @pl.when(i == nt - 1)
        def _():
            total = jnp.sum(sm_ref[0:1, :], axis=-1, keepdims=True) * (0.5 / D)
            sm_ref[4:5, :] = jnp.broadcast_to(total, (1, D))
            for k in range(NSHARD):
                dwo_ref[k] = dwo_s[k * QR:(k + 1) * QR, :].astype(BF16)
                dwg_ref[k] = dwg_s[k * QR:(k + 1) * QR, :].astype(BF16)
                dwp_ref[k] = dwp_s[k].astype(BF16)

    row = lambda w: pl.BlockSpec((tm, w), lambda i: (i, 0))
    const2 = lambda s: pl.BlockSpec(s, lambda i: (0, 0), pipeline_mode=pl.Buffered(1))
    const3 = lambda s: pl.BlockSpec(s, lambda i: (0, 0, 0), pipeline_mode=pl.Buffered(1))
    return pl.pallas_call(
        body, name="tail_fwd_bwd", grid=(nt,),
        in_specs=[row(D), row(PLE), row(D), row(DG), row(DG),
                  const2((D, D)), const2((D, D)), const3((NSHARD, PLE, PLE)),
                  const2((1, D)), const2((1, D)), const2((1, D))],
        out_specs=[row(D), row(D), const3((NSHARD, QR, D)), const3((NSHARD, QR, D)),
                   const3((NSHARD, PLE, PLE)), const2((8, D))],
        out_shape=[jax.ShapeDtypeStruct((T, D), F32), jax.ShapeDtypeStruct((T, D), F32),
                   jax.ShapeDtypeStruct((NSHARD, QR, D), BF16), jax.ShapeDtypeStruct((NSHARD, QR, D), BF16),
                   jax.ShapeDtypeStruct((NSHARD, PLE, PLE), BF16), jax.ShapeDtypeStruct((8, D), F32)],
        scratch_shapes=[pltpu.VMEM((D, D), F32), pltpu.VMEM((D, D), F32), pltpu.VMEM((NSHARD, PLE, PLE), F32)],
        compiler_params=_cparams(("arbitrary",), 62),
    )(x, p, tgt, pltpu.with_memory_space_constraint(ya, pltpu.HBM), pltpu.with_memory_space_constraint(yb, pltpu.HBM),
      w_out_b, w_pg_b, w_pp_b, ple_nw, b_pg, fnw)


def _bwd_rglru(pf, pb, h, dyab, cw3, conv_b, wa_c, ba, wx_c, bx, lam, dep):
    tm = 512
    nt = T // tm
    ng = tm // 8

    def body(xa_ref, ga_ref, xp_ref, h_ref, hp_ref, dya_ref, cw_ref, cb_ref, wa_ref, ba_ref, wx_ref, bx_ref,
             lam_ref, dep_ref, da_ref, rg_ref, sm_ref, a_s, g_s, cg_s, nxt_s, wa_s, wx_s, dwa_s, dwx_s):
        i = pl.program_id(0)
        tile = nt - 1 - i

        @pl.when(i == 0)
        def _():
            dwa_s[...] = jnp.zeros_like(dwa_s)
            dwx_s[...] = jnp.zeros_like(dwx_s)
            sm_ref[...] = jnp.zeros_like(sm_ref)
            cg_s[...] = jnp.zeros_like(cg_s)
            nxt_s[...] = jnp.zeros_like(nxt_s)
            wa_s[...] = _dense_from_blocks(wa_ref[...]).astype(BF16)
            wx_s[...] = _dense_from_blocks(wx_ref[...]).astype(BF16)

        rows = lax.broadcasted_iota(I32, (tm, DG), 0)
        has_prev = tile > 0
        xa = xa_ref[...]
        xprev = jnp.where(has_prev, xp_ref[...], 0.0)
        cw = _conv_rows(cw_ref)
        xc, taps = _rg_conv(xa, xprev, cw, cb_ref[...], rows)
        lam_v = lam_ref[...]
        sp = _softplus(-lam_v)
        first_row = (rows + tile * tm) == 0
        r, ig, a, a2, mult = _rg_gates(xc, wa_s[...], ba_ref[...], wx_s[...], bx_ref[...], sp, first_row)
        hv = h_ref[...]
        hprev = jnp.where(has_prev, hp_ref[...], 0.0)
        h_m1 = _shift_down(hv, hprev, 1, rows)
        ga = ga_ref[...].astype(F32)
        sg = _sigmoid(ga)
        dya = dya_ref[...]
        dga = dya * hv * (sg * (1.0 + ga * (1.0 - sg)))

        av = jnp.where(rows == tm - 1, 1.0, pltpu.roll(a, tm - 1, 0))
        gv = dya * (ga * sg)
        rows8 = rows & 7
        for d in (1, 2, 4):
            keep = rows8 < 8 - d
            gv = gv + av * jnp.where(keep, _roll_in_groups(gv, 8 - d), 0.0)
            av = av * jnp.where(keep, _roll_in_groups(av, 8 - d), 1.0)
        a_s[...] = av
        g_s[...] = gv
        carry = cg_s[0:1, :]
        for g in range(ng - 1, -1, -1):
            sl = slice(g * 8, (g + 1) * 8)
            ab, gb = a_s[sl, :], g_s[sl, :]
            g_s[sl, :] = gb + ab * carry
            carry = gb[0:1, :] + ab[0:1, :] * carry
        cg_s[0:1, :] = a[0:1, :] * carry

        gt = g_s[...]
        da = gt * h_m1
        ixc = ig * xc
        di = gt * mult * xc
        dxc = gt * mult * ig
        dlog_a = da * a + jnp.where(first_row, 0.0, gt * ixc * (-a2 / mult))
        sm_ref[3:4, :] += _rowsum(dlog_a * ((-RG_C) * r))
        dpr_f = dlog_a * ((-RG_C) * sp) * r * (1.0 - r)
        dpi_f = di * ig * (1.0 - ig)
        sm_ref[1:2, :] += _rowsum(dpr_f)
        sm_ref[2:3, :] += _rowsum(dpi_f)
        dpr = dpr_f.astype(BF16)
        dpi = dpi_f.astype(BF16)
        xcb = xc.astype(BF16)
        dwa_s[...] += _mm_tn(xcb, dpr)
        dwx_s[...] += _mm_tn(xcb, dpi)
        dxc = dxc + _mm_nt(dpr, wa_s[...]) + _mm_nt(dpi, wx_s[...])
        sm_ref[0:1, :] += _rowsum(dxc)
        for j in range(4):
            sm_ref[4 + j:5 + j, :] += _rowsum(dxc * taps[j])
        nxt = nxt_s[...]
        dxa = (dxc * cw[3] + _shift_up(dxc, nxt, 1, rows) * cw[2]
               + _shift_up(dxc, nxt, 2, rows) * cw[1] + _shift_up(dxc, nxt, 3, rows) * cw[0])
        nxt_s[...] = dxc[0:8, :]
        da_ref[:, 0:DG] = dxa.astype(BF16)
        da_ref[:, DG:D] = dga.astype(BF16)

        @pl.when(i == nt - 1)
        def _():
            _, s_neg = _sig_pair(lam_v)
            sm_ref[3:4, :] = sm_ref[3:4, :] * (-s_neg)
            rg_ref[...] = _blocks_from_dense(dwa_s[...], dwx_s[...])

    vec = lambda: pl.BlockSpec((1, DG), lambda i: (0, 0))
    blocks = lambda: pl.BlockSpec((DG, RGB), lambda i: (0, 0))
    prev8 = lambda: pl.BlockSpec((8, DG), lambda i: (jnp.maximum((nt - 1 - i) * (tm // 8) - 1, 0), 0))
    return pl.pallas_call(
        body, name="bwd_rglru", grid=(nt,),
        in_specs=[pl.BlockSpec((tm, DG), lambda i: (nt - 1 - i, 0)),
                  pl.BlockSpec((tm, DG), lambda i: (nt - 1 - i, 0)),
                  prev8(),
                  pl.BlockSpec((tm, DG), lambda i: (nt - 1 - i, 0)),
                  prev8(),
                  pl.BlockSpec((tm, DG), lambda i: (nt - 1 - i, 0)),
                  pl.BlockSpec((NSHARD, 4, 128), lambda i: (0, 0, 0)), vec(),
                  blocks(), vec(), blocks(), vec(), vec(), ANY_SPEC],
        out_specs=[pl.BlockSpec((tm, D), lambda i: (nt - 1 - i, 0)),
                   pl.BlockSpec((DG, 128), lambda i: (0, 0)),
                   pl.BlockSpec((8, DG), lambda i: (0, 0))],
        out_shape=[jax.ShapeDtypeStruct((T, D), BF16), jax.ShapeDtypeStruct((DG, 128), F32),
                   jax.ShapeDtypeStruct((8, DG), F32)],
        scratch_shapes=[pltpu.VMEM((tm, DG), F32), pltpu.VMEM((tm, DG), F32),
                        pltpu.VMEM((8, DG), F32), pltpu.VMEM((8, DG), F32),
                        pltpu.VMEM((DG, DG), BF16), pltpu.VMEM((DG, DG), BF16),
                        pltpu.VMEM((DG, DG), F32), pltpu.VMEM((DG, DG), F32)],
        compiler_params=_cparams(("arbitrary",), 56),
    )(pf, pb, pf, h, h, dyab, cw3, conv_b, wa_c, ba, wx_c, bx, lam, dep)


def _bwd_hgrn2(pf, pb, o, s_prev, dyab, hg_lb, hg_nw, dep):
    tm = 512
    nt = T // tm
    nc = tm // CH

    def body(q_ref, f_ref, v_ref, g_ref, o_ref, sp_ref, dy_ref, lb_ref, nw_ref, dep_ref, db_ref, sm_ref,
             dst_s, eb_s, enb_s, ekd_s, qe_s, ke_s, kd_s, vb_s, do_s, dec_s, ddec_s, p_s, dp_s,
             g_s, dsta_s, dva_s, dqe_s, dke_s, dkd_s, dlf_s):
        i = pl.program_id(0)

        @pl.when(i == 0)
        def _():
            sm_ref[...] = jnp.zeros_like(sm_ref)
            dst_s[...] = jnp.zeros_like(dst_s)

        lb, one_m_lb = _hg_lower_bound(lb_ref)
        rows64 = lax.broadcasted_iota(I32, (tm, DG), 0) & (CH - 1)
        sg, sn, f, k, q, sq, qs = _hg_recompute(
            q_ref, f_ref, lb, one_m_lb, rows64, eb_s, enb_s, ekd_s, qe_s, ke_s, kd_s, dec_s)
        vb_s[...] = v_ref[...]

        nw = nw_ref[...]
        for hd in range(NH):
            cols = slice(hd * HD, (hd + 1) * HD)
            g = g_ref[:, cols].astype(F32)
            sgg = _sigmoid(g)
            o = o_ref[:, cols]
            so = lax.rsqrt(jnp.mean(o * o, axis=-1, keepdims=True) + EPS)
            oh = o * so
            dyb = dy_ref[:, cols]
            db_ref[:, 3 * DG + hd * HD:3 * DG + (hd + 1) * HD] = (
                dyb * (oh * nw) * (sgg * (1.0 + g * (1.0 - sgg)))).astype(BF16)
            don = dyb * (g * sgg)
            sm_ref[1:2, 0:HD] += _rowsum(don * oh)
            gw = don * nw
            do_s[:, cols] = (so * (gw - oh * jnp.mean(gw * oh, axis=-1, keepdims=True))).astype(BF16)

        mask = _tri(True)
        items = [(c, hd, slice(c * CH, (c + 1) * CH), slice(hd * HD, (hd + 1) * HD))
                 for c in range(nc) for hd in range(NH)]
        for c, hd, rs, cols in items:
            p_s[c * NH + hd] = jnp.where(mask, _mm_nt(qe_s[rs, cols], ke_s[rs, cols]), 0.0).astype(BF16)
            dp_s[c * NH + hd] = jnp.where(mask, _mm_nt(do_s[rs, cols], vb_s[rs, cols]), 0.0).astype(BF16)
        for c, hd, rs, cols in items:
            n = c * NH + hd
            dva_s[rs, cols] = _mm_tn(p_s[n], do_s[rs, cols])
            dqe_s[rs, cols] = _mm(dp_s[n], ke_s[rs, cols])
            dke_s[rs, cols] = _mm_tn(dp_s[n], qe_s[rs, cols])
            g_s[n] = _mm_tn(do_s[rs, cols], qe_s[rs, cols])
        for c, hd, rs, cols in reversed(items):
            n = c * NH + hd
            dst = dst_s[hd]
            dsta_s[n] = dst
            dst_s[hd] = dst * dec_s[c:c + 1, cols] + g_s[n]
        for c, hd, rs, cols in items:
            n = c * NH + hd
            dst = dsta_s[n]
            st_prev = sp_ref[hd, c]
            dv = dva_s[rs, cols] + _mm_nt(kd_s[rs, cols], dst)
            db_ref[rs, 2 * DG + hd * HD:2 * DG + (hd + 1) * HD] = dv.astype(BF16)
            dqe_s[rs, cols] += _mm(do_s[rs, cols], st_prev)
            dkd_s[rs, cols] = _mm(vb_s[rs, cols], dst)
            ddec_s[c:c + 1, cols] = _rowsum(dst * st_prev)

        eb, enb, ekd = eb_s[...], enb_s[...], ekd_s[...]
        dqe, dke, dkd = dqe_s[...], dke_s[...], dkd_s[...]
        t_kd = dkd * (k * ekd)
        rc = _chunk_rev_cumsum(dqe * (qs * eb) - dke * (k * enb) - t_kd, rows64)
        for c in range(nc):
            rs = slice(c * CH, (c + 1) * CH)
            dbl = _rowsum(t_kd[rs]) + ddec_s[c:c + 1, :] * dec_s[c:c + 1, :]
            dlf_s[rs, :] = rc[rs] + dbl
        t = dlf_s[...] / f - (dke * enb + dkd * ekd)
        db_ref[:, DG:2 * DG] = (one_m_lb * sg * sn * t).astype(BF16)
        sm_ref[0:1, :] += _rowsum(sn * t)
        db_ref[:, 0:DG] = (dqe * eb * (sq * (1.0 + q * (1.0 - sq))) * (HD ** -0.5)).astype(BF16)

        @pl.when(i == nt - 1)
        def _():
            dsm = sm_ref[0:1, :] * (lb * one_m_lb)
            sm_ref[2:3, :] = dsm
            sm_ref[3:4, :] = -dsm

    col = lambda j: pl.BlockSpec((tm, DG), lambda i: (nt - 1 - i, j))
    big = lambda dt: pltpu.VMEM((tm, DG), dt)
    return pl.pallas_call(
        body, name="bwd_hgrn2", grid=(nt,),
        in_specs=[col(1), col(1), col(2), col(3),
                  pl.BlockSpec((tm, DG), lambda i: (nt - 1 - i, 0)),
                  pl.BlockSpec((NH, nc, HD, HD), lambda i: (0, nt - 1 - i, 0, 0)),
                  pl.BlockSpec((tm, DG), lambda i: (nt - 1 - i, 1)),
                  pl.BlockSpec((2, DG), lambda i: (0, 0)),
                  pl.BlockSpec((1, HD), lambda i: (0, 0)), ANY_SPEC],
        out_specs=[pl.BlockSpec((tm, 4 * DG), lambda i: (nt - 1 - i, 0)),
                   pl.BlockSpec((8, DG), lambda i: (0, 0))],
        out_shape=[jax.ShapeDtypeStruct((T, 4 * DG), BF16), jax.ShapeDtypeStruct((8, DG), F32)],
        scratch_shapes=[pltpu.VMEM((NH, HD, HD), F32),
                        big(F32), big(F32), big(F32),
                        big(BF16), big(BF16), big(BF16), big(BF16), big(BF16),
                        pltpu.VMEM((nc, DG), F32), pltpu.VMEM((nc, DG), F32),
                        pltpu.VMEM((nc * NH, CH, CH), BF16), pltpu.VMEM((nc * NH, CH, CH), BF16),
                        pltpu.VMEM((nc * NH, HD, HD), F32), pltpu.VMEM((nc * NH, HD, HD), F32),
                        big(F32), big(F32), big(F32), big(F32), big(F32)],
        compiler_params=_cparams(("arbitrary",), 56),
    )(pb, pf, pb, pb, o, s_prev, dyab, hg_lb, hg_nw, dep)


def _dproj_pieces(k, da_ref, db_ref):
    if k == 0:
        return [(da_ref[:, 0:SHW], 0)]
    if k == 1:
        return [(da_ref[:, SHW:D], 0), (db_ref[:, 0:DG], D - SHW)]
    if k == 2:
        return [(db_ref[:, DG:DG + SHW], 0)]
    return [(db_ref[:, DG + SHW:4 * DG], 0)]


def _bwd_inproj_dx(x, dh1, d_a, d_b, w_in_b, nw, dep):
    tm = 512
    nt = T // tm

    def body(x_ref, dh1_ref, da_ref, db_ref, w_ref, nw_ref, dep_ref, dx_ref, sm_ref):
        i = pl.program_id(0)

        @pl.when(i == 0)
        def _():
            sm_ref[...] = jnp.zeros_like(sm_ref)

        du = None
        for k in range(NSHARD):
            for val, off in _dproj_pieces(k, da_ref, db_ref):
                t = _mm_nt(val, w_ref[k, :, off:off + val.shape[1]])
                du = t if du is None else du + t
        xv = x_ref[...]
        s = lax.rsqrt(jnp.mean(xv * xv, axis=-1, keepdims=True) + EPS)
        xh = xv * s
        sm_ref[0:1, :] += _rowsum(du * xh)
        g = du * nw_ref[...]
        dx_ref[...] = dh1_ref[...] + s * (g - xh * jnp.mean(g * xh, axis=-1, keepdims=True))

    row = lambda w: pl.BlockSpec((tm, w), lambda i: (i, 0))
    return pl.pallas_call(
        body, name="bwd_inproj_dx", grid=(nt,),
        in_specs=[row(D), row(D), row(D), row(4 * DG),
                  pl.BlockSpec((NSHARD, D, SHW), lambda i: (0, 0, 0), pipeline_mode=pl.Buffered(1)),
                  pl.BlockSpec((1, D), lambda i: (0, 0)), ANY_SPEC],
        out_specs=[row(D), pl.BlockSpec((8, D), lambda i: (0, 0))],
        out_shape=[jax.ShapeDtypeStruct((T, D), F32), jax.ShapeDtypeStruct((8, D), F32)],
        compiler_params=_cparams(("arbitrary",), 56),
    )(x, dh1, d_a, d_b, w_in_b, nw, dep)


def _bwd_inproj_dw23(u_b, d_b):
    tm = 1024
    nt = T // tm

    def body(u_ref, db_ref, dw_ref, acc):
        i = pl.program_id(0)

        @pl.when(i == 0)
        def _():
            acc[...] = jnp.zeros_like(acc)

        u = u_ref[...]
        for j, k in enumerate((2, 3)):
            for val, off in _dproj_pieces(k, None, db_ref):
                acc[j, :, off:off + val.shape[1]] += _mm_tn(u, val)

        @pl.when(i == nt - 1)
        def _():
            for j in range(2):
                for r0 in range(0, D, 256):
                    dw_ref[j, r0:r0 + 256, :] = acc[j, r0:r0 + 256, :].astype(BF16)

    row = lambda w: pl.BlockSpec((tm, w), lambda i: (i, 0))
    return pl.pallas_call(
        body, name="bwd_inproj_dw23", grid=(nt,), in_specs=[row(D), row(4 * DG)],
        out_specs=pl.BlockSpec((2, D, SHW), lambda i: (0, 0, 0)),
        out_shape=jax.ShapeDtypeStruct((2, D, SHW), BF16),
        scratch_shapes=[pltpu.VMEM((2, D, SHW), F32)],
        compiler_params=_cparams(("arbitrary",), 48),
    )(u_b, d_b)


def _bwd_inproj_dw01(u_b, d_a, d_b, dw23):
    tm = 1024
    nt = T // tm
    H = D // 2
    step = 128

    def body(u_ref, da_ref, db_ref, p23_ref, dw_ref, acc, send_s, recv_s, ssem, rsem):
        i = pl.program_id(0)
        x, y, c = lax.axis_index("x"), lax.axis_index("y"), lax.axis_index("c")
        sibling = (x, y, 1 - c)
        mine0 = pl.multiple_of(c * H, step)
        other0 = pl.multiple_of((1 - c) * H, step)

        def swap(k):
            return _remote(send_s.at[k], recv_s.at[k], ssem.at[k], rsem.at[k], sibling)

        @pl.when(i == 0)
        def _():
            acc[...] = jnp.zeros_like(acc)
            for k in (2, 3):
                for r0 in range(0, H, step):
                    send_s[k, r0:r0 + step, :] = p23_ref[k - 2, pl.ds(other0 + r0, step), :]
                swap(k).start()

        u = u_ref[...]
        for j, k in enumerate((0, 1)):
            for val, off in _dproj_pieces(k, da_ref, db_ref):
                acc[j, :, off:off + val.shape[1]] += _mm_tn(u, val)

        @pl.when(i == nt - 1)
        def _():
            copies = [swap(k) for k in range(NSHARD)]
            for k in (0, 1):
                for r0 in range(0, H, step):
                    send_s[k, r0:r0 + step, :] = acc[k, pl.ds(other0 + r0, step), :].astype(BF16)
                copies[k].start()
            for k in (2, 3, 0, 1):
                copies[k].wait_recv()
                for r0 in range(0, H, step):
                    rs = pl.ds(mine0 + r0, step)
                    own = acc[k, rs, :] if k < 2 else p23_ref[k - 2, rs, :].astype(F32)
                    dw_ref[k, r0:r0 + step, :] = (own + recv_s[k, r0:r0 + step, :].astype(F32)).astype(BF16)
            for cp in copies:
                cp.wait_send()

    row = lambda w: pl.BlockSpec((tm, w), lambda i: (i, 0))
    return pl.pallas_call(
        body, name="bwd_inproj_dw01", grid=(nt,),
        in_specs=[row(D), row(D), row(4 * DG),
                  pl.BlockSpec((2, D, SHW), lambda i: (0, 0, 0), pipeline_mode=pl.Buffered(1))],
        out_specs=pl.BlockSpec((NSHARD, H, SHW), lambda i: (0, 0, 0)),
        out_shape=jax.ShapeDtypeStruct((NSHARD, H, SHW), BF16),
        scratch_shapes=[pltpu.VMEM((2, D, SHW), F32), pltpu.VMEM((NSHARD, H, SHW), BF16),
                        pltpu.VMEM((NSHARD, H, SHW), BF16),
                        pltpu.SemaphoreType.DMA((NSHARD,)), pltpu.SemaphoreType.DMA((NSHARD,))],
        compiler_params=_cparams(("arbitrary",), 56),
    )(u_b, d_a, d_b, dw23)


_OUT_ORDER = ["norm_mix_w", "w_in", "conv_w", "conv_b", "rg_wa", "rg_ba", "rg_wx", "rg_bx", "rg_lambda", "hg_lb",
              "hg_norm_w", "w_out", "ple_norm_w", "w_ple_gate", "b_ple_gate", "w_ple_proj", "final_norm_w"]
_BIG = ["w_in", "w_out", "w_ple_gate", "w_ple_proj"]


def _small_view(name, a):
    if name in ("rg_wa", "rg_wx"):
        return a.reshape(DG, RGB)
    if name == "conv_w":
        return a.reshape(4, 128)
    if name == "final_norm_w":
        return a.reshape(1, D)
    return a


def _landing(rows, cols):
    return lax.empty((NDEV, rows, cols), BF16)


def kernel(x, p, norm_mix_w, w_in, conv_w, conv_b, rg_wa, rg_ba, rg_wx, rg_bx, rg_lambda, hg_lb, hg_norm_w, w_out, ple_norm_w, w_ple_gate, b_ple_gate, w_ple_proj, final_norm_w, loss_target, m_norm_mix_w, m_w_in, m_conv_w, m_conv_b, m_rg_wa, m_rg_ba, m_rg_wx, m_rg_bx, m_rg_lambda, m_hg_lb, m_hg_norm_w, m_w_out, m_ple_norm_w, m_w_ple_gate, m_b_ple_gate, m_w_ple_proj, m_final_norm_w, v_norm_mix_w, v_w_in, v_conv_w, v_conv_b, v_rg_wa, v_rg_ba, v_rg_wx, v_rg_bx, v_rg_lambda, v_hg_lb, v_hg_norm_w, v_w_out, v_ple_norm_w, v_w_ple_gate, v_b_ple_gate, v_w_ple_proj, v_final_norm_w):
    given = dict(locals())
    x2, p2, tgt = x[0], p[0, 0], loss_target[0]
    hbm = lambda a: pltpu.with_memory_space_constraint(a, pltpu.HBM)
    norm_mix_w, conv_b, rg_ba, rg_bx, rg_lambda, hg_lb, hg_norm_w, ple_norm_w, b_ple_gate = (
        hbm(a) for a in (norm_mix_w, conv_b, rg_ba, rg_bx, rg_lambda, hg_lb, hg_norm_w, ple_norm_w, b_ple_gate))
    wa_c, wx_c = hbm(_small_view("rg_wa", rg_wa)), hbm(_small_view("rg_wx", rg_wx))

    w_in_b, l_out, l_pg, l_pp, cw3, pf, pb, u_b = _gather_inproj(
        x2, norm_mix_w, w_in[0], w_out[0], w_ple_gate[0], w_ple_proj[0], conv_w[0])
    g_ssem, g_rsem, g_lands, tok = _gather_rest_start([l_out, l_pg, l_pp])

    h, ya = _fwd_rglru(pf, pb, cw3, conv_b, wa_c, rg_ba, wx_c, rg_bx, rg_lambda, tok)
    yb, o, s_prev = _fwd_hgrn2(pf, pb, hg_lb, hg_norm_w)
    w_out_b, w_pg_b, w_pp_b = _gather_rest_wait(g_ssem, g_rsem, g_lands, yb)
    dh1, dyab, dwo_b, dwg_b, dwp_b, sm_tail = _tail_fwd_bwd(
        x2, p2, tgt, ya, yb, w_out_b.reshape(D, D), w_pg_b.reshape(D, D), w_pp_b,
        ple_norm_w, b_ple_gate, final_norm_w.reshape(1, D))

    QH = D // NSHARD // 2
    r1 = _rs_start("rs_start_tail", [dwo_b, dwg_b, dwp_b], (0, 1, 2, 3),
                   [_landing(QH, D), _landing(QH, D), _landing(PLE // 2, PLE)])
    d_b, sm_b = _bwd_hgrn2(pf, pb, o, s_prev, dyab, hg_lb, hg_norm_w, r1[4])
    dw23 = _bwd_inproj_dw23(u_b, d_b)
    d_a, rg_c, sm_a = _bwd_rglru(pf, pb, h, dyab, cw3, conv_b, wa_c, rg_ba, wx_c, rg_bx, rg_lambda, dw23)
    dw_in = _bwd_inproj_dw01(u_b, d_a, d_b, dw23)
    r2 = _rs_start("rs_start_in", [dw_in], (0, 1, 2, 3), [lax.empty((NSHARD, D // 2, SHW), BF16)], chip_sums=True)
    grad_x, sm_in = _bwd_inproj_dx(x2, dh1, d_a, d_b, w_in_b, norm_mix_w, r2[4])

    parts1, lands1 = _rs_wait("rs_wait_tail", r1[0], r1[1], r1[2], (0, 1, 2, 3), r1[3], sm_in)
    parts2, lands2 = _rs_wait("rs_wait_in", r2[0], r2[1], r2[2], (0, 1, 2, 3), r2[3], sm_in, chip_sums=True)
    g_big, vred, rgred = _final_reduce(parts2 + parts1, lands2 + lands1, sm_in, sm_tail, sm_a, sm_b, rg_c)

    upd_big = _adam_big(g_big, [given[n][0] for n in _BIG], [given["m_" + n][0] for n in _BIG],
                        [given["v_" + n][0] for n in _BIG])
    small = _adam_small(vred, rgred,
                        {n: _small_view(n, given[n]) for n in _SMALL_ORDER},
                        {n: _small_view(n, given["m_" + n]) for n in _SMALL_ORDER},
                        {n: _small_view(n, given["v_" + n]) for n in _SMALL_ORDER})

    loss = vred[0, ROW_LOSS, 0]
    outs = [loss, grad_x[None]]
    for ki in range(4):
        for n in _OUT_ORDER:
            if n in _BIG:
                i = _BIG.index(n)
                a = g_big[i] if ki == 0 else upd_big[i][ki - 1]
                outs.append(a[None])
            else:
                outs.append(small[n][ki].reshape(given[n].shape))
    return tuple(outs)
```

```python
import jax
import jax.numpy as jnp
from jax import lax
from jax.experimental import pallas as pl
from jax.experimental.pallas import tpu as pltpu

F32 = jnp.float32
BF16 = jnp.bfloat16
I32 = jnp.int32
MESH = pl.DeviceIdType.MESH

T = 4096
D = 1024
DG = 512
DIN = 3072
PLE = 256
NH = 4
HD = 128
CH = 64
NCHUNK = T // CH
RGB = 64
EPS = 1e-6
RG_C = 8.0
NSHARD = 4
SHW = DIN // NSHARD
NDEV = 8

ADAM_LR = 0.001
ADAM_B1 = 0.9
ADAM_B2 = 0.999
ADAM_EPS = 1e-08
ADAM_WD = 0.01
ADAM_STEP = 10

VMEM_SPEC = pl.BlockSpec(memory_space=pltpu.VMEM)
HBM_SPEC = pl.BlockSpec(memory_space=pltpu.HBM)
SEM_SPEC = pl.BlockSpec(memory_space=pltpu.SEMAPHORE)
ANY_SPEC = pl.BlockSpec(memory_space=pl.ANY)
EFFECT = pltpu.SideEffectType.DATAFLOW_SIDE_EFFECTING
MIB = 1024 * 1024

VROWS = 16
ROW_NORM_MIX, ROW_FINAL_NORM, ROW_B_PG, ROW_PLE_NORM = 0, 1, 2, 3
ROW_CB_BA, ROW_BX_LAM, ROW_CW01, ROW_CW23, ROW_HG_LB, ROW_HG_NW, ROW_LOSS = 4, 5, 6, 7, 8, 9, 10


def _mm(a, b):
    return jnp.dot(a.astype(BF16), b.astype(BF16), preferred_element_type=F32)


def _mm_nt(a, b):
    return lax.dot_general(a.astype(BF16), b.astype(BF16), (((1,), (1,)), ((), ())),
                           preferred_element_type=F32)


def _mm_tn(a, b):
    return lax.dot_general(a.astype(BF16), b.astype(BF16), (((0,), (0,)), ((), ())),
                           preferred_element_type=F32)


def _mm_exact(a, b):
    bb = b.astype(BF16)
    hi = a.astype(BF16)
    r1 = a - hi.astype(F32)
    mid = r1.astype(BF16)
    lo = (r1 - mid.astype(F32)).astype(BF16)
    return sum(jnp.dot(p, bb, preferred_element_type=F32) for p in (hi, mid, lo))


def _sig_pair(x):
    e = jnp.exp(-jnp.abs(x))
    big = 1.0 / (1.0 + e)
    small = e * big
    pos = x >= 0
    return jnp.where(pos, big, small), jnp.where(pos, small, big)


def _sigmoid(x):
    return 1.0 / (1.0 + jnp.exp(-x))


def _rowsum(v):
    return jnp.sum(v, axis=0, keepdims=True)


def _shift_down(cur, prev8, d, rows):
    rolled = pltpu.roll(cur, d, 0)
    head = jnp.where(rows[0:8] < d, pltpu.roll(prev8, d, 0), rolled[0:8])
    return jnp.concatenate([head, rolled[8:]], axis=0)


def _shift_up(cur, next8, d, rows):
    n = cur.shape[0]
    rolled = pltpu.roll(cur, n - d, 0)
    tail = jnp.where(rows[0:8] >= 8 - d, pltpu.roll(next8, 8 - d, 0), rolled[n - 8:n])
    return jnp.concatenate([rolled[0:n - 8], tail], axis=0)


def _roll_in_groups(v, d):
    n, w = v.shape
    return pltpu.roll(v.reshape(n // 8, 8, w), d, 1).reshape(n, w)


def _cparams(sem, vmem_mib):
    return pltpu.CompilerParams(dimension_semantics=sem, vmem_limit_bytes=vmem_mib * MIB)


def _mesh_pos():
    x, y, c = lax.axis_index("x"), lax.axis_index("y"), lax.axis_index("c")
    chips = [(1 - x, y), (x, 1 - y), (1 - x, 1 - y)]
    return x, y, c, chips


def _remote(src, dst, ssem, rsem, dev):
    return pltpu.make_async_remote_copy(src_ref=src, dst_ref=dst, send_sem=ssem, recv_sem=rsem,
                                        device_id=dev, device_id_type=MESH)


def _gather_inproj(x, nw, w_in, w_out, w_pg, w_pp, conv_w):
    shapes = [w_in.shape, w_out.shape, w_pg.shape, w_pp.shape]
    tm = 512
    nt = T // tm

    def body(x_h, nw_h, win_h, wout_h, wpg_h, wpp_h, cw_h,
             o_in_h, o_out_h, o_pg_h, o_pp_h, o_cw_h, pf_h, pb_h, u_h,
             win, wout, wpg, wpp, cw, nw_s, xbuf, ubuf, rfbuf, rbbuf, o_in, o_out, o_pg, o_pp, o_cw,
             lsem, xsem, usem, fsem, bsem, osem, ssem, rsem):
        x, y, c, chips = _mesh_pos()
        kme = 2 * x + y
        sibling = (x, y, 1 - c)
        fetch = [pltpu.make_async_copy(src, dst, lsem.at[i]) for i, (src, dst) in enumerate(
            [(win_h, win), (cw_h, cw), (nw_h, nw_s), (wout_h, wout), (wpg_h, wpg), (wpp_h, wpp)])]
        for cp in fetch:
            cp.start()

        def cast(src, dst):
            for r0 in range(0, src.shape[0], 128):
                dst[r0:r0 + 128, :] = src[r0:r0 + 128, :].astype(BF16)

        outs = []

        def write_out(src, dst, i):
            cp = pltpu.make_async_copy(src, dst, osem.at[i])
            cp.start()
            outs.append(cp)

        fetch[0].wait()
        cast(win, o_in.at[kme])
        write_out(o_in.at[kme], o_in_h.at[kme], 0)
        fetch[1].wait()
        o_cw[kme] = cw[...]

        hrows = D // 2
        mine = pl.ds(pl.multiple_of(c * hrows, 128), hrows)
        other = pl.ds(pl.multiple_of((1 - c) * hrows, 128), hrows)

        def half(k, rows):
            return o_in.at[k, rows]

        na = (x ^ c, y ^ (1 - c), c)
        nb = (x ^ (1 - c), y ^ c, c)
        ka = 2 * na[0] + na[1]
        kb = 2 * nb[0] + nb[1]
        kd = 2 * (1 - x) + (1 - y)
        sends = [_remote(half(kme, mine), half(kme, mine), ssem.at[0], rsem.at[0], na),
                 _remote(half(kme, mine), half(kme, mine), ssem.at[1], rsem.at[1], nb)]
        for j, (px, py) in enumerate(chips):
            sends.append(_remote(o_cw.at[kme], o_cw.at[kme], ssem.at[3 + j], rsem.at[3 + j], (px, py, c)))
        for cp in sends:
            cp.start()
        fetch[2].wait()

        def rows_of(t):
            return pl.ds(pl.multiple_of(t * tm, tm), tm)

        def x_copy(t, slot):
            return pltpu.make_async_copy(x_h.at[rows_of(t)], xbuf.at[slot], xsem.at[slot])

        u_out = pltpu.make_async_copy(ubuf, u_h, usem.at[0])

        def f_store(t, slot, kh):
            return pltpu.make_async_copy(rfbuf.at[slot], pf_h.at[rows_of(t), pl.ds(pl.multiple_of(kh * DG, DG), DG)],
                                         fsem.at[slot])

        def b_store_even(t, slot, kh):
            return pltpu.make_async_copy(rbbuf.at[slot, :, 0:256],
                                         pb_h.at[rows_of(t), pl.ds(pl.multiple_of(kh * 2 * DG, 256), 256)],
                                         bsem.at[slot])

        def b_store_odd(t, slot, kh):
            return pltpu.make_async_copy(rbbuf.at[slot],
                                         pb_h.at[rows_of(t), pl.ds(pl.multiple_of(256 + kh * 2 * DG, 256), SHW)],
                                         bsem.at[slot])

        def run_pass(k, first, even):
            kh = k >> 1
            if first:
                x_copy(0, 0).start()

            def tile(t, carry):
                slot = t & 1
                if first:
                    @pl.when(t + 1 < nt)
                    def _():
                        x_copy(t + 1, 1 - slot).start()

                    x_copy(t, slot).wait()

                @pl.when(t >= 2)
                def _():
                    if even:
                        f_store(t - 2, slot, kh).wait()
                        b_store_even(t - 2, slot, kh).wait()
                    else:
                        b_store_odd(t - 2, slot, kh).wait()

                if first:
                    xv = xbuf[slot]
                    s = lax.rsqrt(jnp.mean(xv * xv, axis=-1, keepdims=True) + EPS)
                    u = (xv * s * nw_s[...]).astype(BF16)
                    ubuf[rows_of(t), :] = u
                else:
                    u = ubuf[rows_of(t), :]
                r = jnp.dot(u, o_in[k], preferred_element_type=F32)
                if even:
                    rfbuf[slot] = r[:, 0:DG]
                    rbbuf[slot, :, 0:256] = r[:, DG:SHW].astype(BF16)
                    f_store(t, slot, kh).start()
                    b_store_even(t, slot, kh).start()
                else:
                    rbbuf[slot] = r.astype(BF16)
                    b_store_odd(t, slot, kh).start()
                return carry

            lax.fori_loop(0, nt, tile, 0)
            for t in (nt - 2, nt - 1):
                slot = t & 1
                if even:
                    f_store(t, slot, kh).wait()
                    b_store_even(t, slot, kh).wait()
                else:
                    b_store_odd(t, slot, kh).wait()

        def project(k, first=False):
            @pl.when((k & 1) == 0)
            def _():
                run_pass(k, first, True)

            @pl.when((k & 1) == 1)
            def _():
                run_pass(k, first, False)

        def passed_on(k, sem, dev):
            cp = _remote(half(k, mine), half(k, mine), ssem.at[sem], rsem.at[sem], dev)
            cp.start()
            sends.append(cp)

        project(kme, first=True)
        u_out.start()
        for i, (src, dst, dst_h) in enumerate([(wout, o_out, o_out_h), (wpg, o_pg, o_pg_h), (wpp, o_pp, o_pp_h)]):
            fetch[3 + i].wait()
            cast(src, dst)
            write_out(dst, dst_h.at[kme], 4 + i)
        _remote(half(ka, mine), half(ka, mine), ssem.at[0], rsem.at[0], na).wait_recv()
        passed_on(ka, 2, nb)
        passed_on(ka, 6, sibling)
        _remote(half(kb, mine), half(kb, mine), ssem.at[1], rsem.at[1], nb).wait_recv()
        passed_on(kb, 7, sibling)
        _remote(half(ka, other), half(ka, other), ssem.at[7], rsem.at[7], sibling).wait_recv()
        write_out(o_in.at[ka], o_in_h.at[ka], 1)
        project(ka)
        _remote(half(kd, mine), half(kd, mine), ssem.at[2], rsem.at[2], nb).wait_recv()
        passed_on(kd, 8, sibling)
        _remote(half(kb, other), half(kb, other), ssem.at[6], rsem.at[6], sibling).wait_recv()
        write_out(o_in.at[kb], o_in_h.at[kb], 2)
        project(kb)
        _remote(half(kd, other), half(kd, other), ssem.at[8], rsem.at[8], sibling).wait_recv()
        write_out(o_in.at[kd], o_in_h.at[kd], 3)
        project(kd)
        u_out.wait()
        for j, (px, py) in enumerate(chips):
            kj = 2 * px + py
            _remote(o_cw.at[kj], o_cw.at[kj], ssem.at[3 + j], rsem.at[3 + j], (px, py, c)).wait_recv()
        write_out(o_cw, o_cw_h, 7)
        for cp in outs:
            cp.wait()
        for cp in sends:
            cp.wait_send()

    out_shape = [pltpu.HBM((NSHARD,) + s, BF16) for s in shapes]
    out_shape.append(pltpu.HBM((NSHARD,) + conv_w.shape, F32))
    out_shape += [pltpu.HBM((T, 2 * DG), F32), pltpu.HBM((T, 4 * DG), BF16), pltpu.HBM((T, D), BF16)]
    hbm_args = [pltpu.with_memory_space_constraint(a, pltpu.HBM) for a in (x, nw, w_in, w_out, w_pg, w_pp, conv_w)]
    return pl.pallas_call(
        body, name="gather_inproj", out_shape=out_shape,
        in_specs=[HBM_SPEC] * 7, out_specs=[HBM_SPEC] * 8,
        scratch_shapes=[pltpu.VMEM(a.shape, F32) for a in (w_in, w_out, w_pg, w_pp, conv_w, nw)]
        + [pltpu.VMEM((2, tm, D), F32), pltpu.VMEM((T, D), BF16),
           pltpu.VMEM((2, tm, DG), F32), pltpu.VMEM((2, tm, SHW), BF16),
           pltpu.VMEM((NSHARD,) + w_in.shape, BF16), pltpu.VMEM(w_out.shape, BF16), pltpu.VMEM(w_pg.shape, BF16),
           pltpu.VMEM(w_pp.shape, BF16), pltpu.VMEM((NSHARD,) + conv_w.shape, F32),
           pltpu.SemaphoreType.DMA((6,)), pltpu.SemaphoreType.DMA((2,)), pltpu.SemaphoreType.DMA((2,)),
           pltpu.SemaphoreType.DMA((2,)), pltpu.SemaphoreType.DMA((2,)), pltpu.SemaphoreType.DMA((8,)),
           pltpu.SemaphoreType.DMA((9,)), pltpu.SemaphoreType.DMA((9,))],
        compiler_params=pltpu.CompilerParams(vmem_limit_bytes=48 * MIB),
    )(*hbm_args)


def _gather_rest_start(lands):
    n = len(lands)

    def body(*refs):
        land_in = refs[0:n]
        ssem, rsem = refs[n], refs[n + 1]
        token = refs[2 * n + 2]
        x, y, c, chips = _mesh_pos()
        kme = 2 * x + y
        for p, land in enumerate(land_in):
            hrows = land.shape[1] // 2
            mine = pl.ds(pl.multiple_of(c * hrows, 128), hrows)
            for px, py in chips:
                for pc in range(2):
                    _remote(land.at[kme, mine], land.at[kme, mine], ssem.at[p], rsem.at[p], (px, py, pc)).start()
        token[...] = jnp.zeros_like(token)

    out_shape = ([pltpu.SemaphoreType.DMA((n,)), pltpu.SemaphoreType.DMA((n,))]
                 + [pltpu.HBM(a.shape, a.dtype) for a in lands] + [jax.ShapeDtypeStruct((8, 128), F32)])
    outs = pl.pallas_call(
        body, name="gather_rest_start", out_shape=out_shape,
        in_specs=[HBM_SPEC] * n, out_specs=[SEM_SPEC, SEM_SPEC] + [HBM_SPEC] * n + [VMEM_SPEC],
        input_output_aliases={i: 2 + i for i in range(n)},
        compiler_params=pltpu.CompilerParams(has_side_effects=EFFECT),
    )(*[pltpu.with_memory_space_constraint(a, pltpu.HBM) for a in lands])
    return outs[0], outs[1], list(outs[2:2 + n]), outs[2 + n]


def _gather_rest_wait(ssem, rsem, lands, after):
    n = len(lands)

    def body(*refs):
        land_in = refs[0:n]
        ssem_ref, rsem_ref = refs[n], refs[n + 1]
        x, y, c = lax.axis_index("x"), lax.axis_index("y"), lax.axis_index("c")
        for p, land in enumerate(land_in):
            three = land.at[pl.ds(0, 3)]
            cp = _remote(three, three, ssem_ref.at[p], rsem_ref.at[p], (x, y, c))
            cp.wait_send()
            cp.wait_recv()

    outs = pl.pallas_call(
        body, name="gather_rest_wait", out_shape=[pltpu.HBM(a.shape, a.dtype) for a in lands],
        in_specs=[HBM_SPEC] * n + [SEM_SPEC, SEM_SPEC, ANY_SPEC], out_specs=[HBM_SPEC] * n,
        input_output_aliases={i: i for i in range(n)},
        compiler_params=pltpu.CompilerParams(has_side_effects=EFFECT),
    )(*lands, ssem, rsem, after)
    return list(outs)


def _rs_start(name, parts, ks, lands, chip_sums=False):
    n = len(parts)

    def body(*refs):
        part_in, land_in = refs[0:n], refs[n:2 * n]
        ssem, rsem = refs[2 * n], refs[2 * n + 1]
        token = refs[4 * n + 2]
        x, y, c = lax.axis_index("x"), lax.axis_index("y"), lax.axis_index("c")
        kme = 2 * x + y
        me = 4 * x + 2 * y + c
        for p in range(n):
            hrows = land_in[p].shape[1]
            for i, k in enumerate(ks):
                if chip_sums:
                    @pl.when(kme != k)
                    def _():
                        _remote(part_in[p].at[i], land_in[p].at[kme], ssem.at[p], rsem.at[p], (k // 2, k % 2, c)).start()
                    continue
                for pc in range(2):
                    @pl.when(jnp.logical_or(kme != k, c != pc))
                    def _():
                        _remote(part_in[p].at[i, pl.ds(pc * hrows, hrows)], land_in[p].at[me],
                                ssem.at[p], rsem.at[p], (k // 2, k % 2, pc)).start()
        token[...] = jnp.zeros_like(token)

    arrays = list(parts) + list(lands)
    out_shape = ([pltpu.SemaphoreType.DMA((n,)), pltpu.SemaphoreType.DMA((n,))]
                 + [pltpu.HBM(a.shape, a.dtype) for a in arrays] + [jax.ShapeDtypeStruct((8, 128), F32)])
    outs = pl.pallas_call(
        body, name=name, out_shape=out_shape,
        in_specs=[HBM_SPEC] * (2 * n), out_specs=[SEM_SPEC, SEM_SPEC] + [HBM_SPEC] * (2 * n) + [VMEM_SPEC],
        input_output_aliases={i: 2 + i for i in range(2 * n)},
        compiler_params=pltpu.CompilerParams(has_side_effects=EFFECT),
    )(*[pltpu.with_memory_space_constraint(a, pltpu.HBM) for a in arrays])
    return outs[0], outs[1], list(outs[2:2 + n]), list(outs[2 + n:2 + 2 * n]), outs[2 + 2 * n]


def _rs_wait(name, ssem, rsem, parts, ks, lands, after, chip_sums=False):
    n = len(parts)

    def body(*refs):
        part_in, land_in = refs[0:n], refs[n:2 * n]
        ssem_ref, rsem_ref = refs[2 * n], refs[2 * n + 1]
        x, y, c = lax.axis_index("x"), lax.axis_index("y"), lax.axis_index("c")
        kme = 2 * x + y
        for p in range(n):
            piece = land_in[p].at[0]
            for k in ks:
                for pc in range(1 if chip_sums else 2):
                    mine = (kme == k) if chip_sums else jnp.logical_and(kme == k, c == pc)

                    @pl.when(jnp.logical_not(mine))
                    def _():
                        _remote(piece, piece, ssem_ref.at[p], rsem_ref.at[p], (x, y, c)).wait_send()
            owner = kme == ks[0]
            for k in ks[1:]:
                owner = jnp.logical_or(owner, kme == k)

            @pl.when(owner)
            def _():
                others = land_in[p].at[pl.ds(0, land_in[p].shape[0] - 1)]
                _remote(others, others, ssem_ref.at[p], rsem_ref.at[p], (x, y, c)).wait_recv()

    arrays = list(parts) + list(lands)
    outs = pl.pallas_call(
        body, name=name, out_shape=[pltpu.HBM(a.shape, a.dtype) for a in arrays],
        in_specs=[HBM_SPEC] * (2 * n) + [SEM_SPEC, SEM_SPEC, ANY_SPEC], out_specs=[HBM_SPEC] * (2 * n),
        input_output_aliases={i: i for i in range(2 * n)},
        compiler_params=pltpu.CompilerParams(has_side_effects=EFFECT),
    )(*arrays, ssem, rsem, after)
    return list(outs[0:n]), list(outs[n:2 * n])


def _final_reduce(parts, lands, sm_in, sm_tail, sm_a, sm_b, rg_c):
    shapes = [(2 * l.shape[1], l.shape[2]) for l in lands]
    step = 128
    PR = DG // NDEV

    def body(pin, pout, ppg, ppp, l_in, l_out, l_pg, l_pp, in_h, tail_h, a_h, b_h, rg_h,
             g_in_h, g_out_h, g_pg_h, g_pp_h, v_out, rg_out, vbuf, vrecv, rgrecv,
             lb_in, lb_out, lb_pg, lb_pp, ob_in, ob_out, ob_pg, ob_pp,
             in_ref, tail_ref, a_ref, b_ref, rg_ref, g_in, g_out, g_pg, g_pp,
             lsem, ssem_l, osem, bsem_s, bsem_r, ssem, rsem):
        x, y, c = lax.axis_index("x"), lax.axis_index("y"), lax.axis_index("c")
        kme = 2 * x + y
        me = 4 * x + 2 * y + c
        sibling = (x, y, 1 - c)
        small_fetch = [pltpu.make_async_copy(src, dst, ssem_l.at[i]) for i, (src, dst) in enumerate(
            [(in_h, in_ref), (tail_h, tail_ref), (a_h, a_ref), (b_h, b_ref), (rg_h, rg_ref)])]
        for cp in small_fetch:
            cp.start()

        lands_hbm = [l_in, l_out, l_pg, l_pp]
        land_bufs = [lb_in, lb_out, lb_pg, lb_pp]
        own_bufs = [ob_in, ob_out, ob_pg, ob_pp]
        fetches = []
        for p in range(4):
            cp = pltpu.make_async_copy(lands_hbm[p], land_bufs[p], lsem.at[p])
            cp.start()
            fetches.append(cp)

        own_fetches = [pltpu.make_async_copy(pin.at[kme], ob_in, lsem.at[4])]
        own_fetches[0].start()
        for p, part in enumerate([pout, ppg, ppp]):
            hrows = own_bufs[p + 1].shape[0]
            cp = pltpu.make_async_copy(part.at[kme, pl.ds(pl.multiple_of(c * hrows, step), hrows)],
                                       own_bufs[p + 1], lsem.at[5 + p])
            cp.start()
            own_fetches.append(cp)

        def pair(ref, r0, r1):
            return jnp.concatenate([ref[r0:r0 + 1, :], ref[r1:r1 + 1, :]], axis=1)

        for cp in small_fetch:
            cp.wait()
        rows = {
            ROW_NORM_MIX: in_ref[0:1, :], ROW_FINAL_NORM: tail_ref[1:2, :], ROW_B_PG: tail_ref[2:3, :],
            ROW_PLE_NORM: tail_ref[3:4, :], ROW_CB_BA: pair(a_ref, 0, 1), ROW_BX_LAM: pair(a_ref, 2, 3),
            ROW_CW01: pair(a_ref, 4, 5), ROW_CW23: pair(a_ref, 6, 7), ROW_HG_LB: pair(b_ref, 2, 3),
            ROW_HG_NW: jnp.concatenate([b_ref[1:2, :], jnp.zeros((1, DG), F32)], axis=1),
            ROW_LOSS: tail_ref[4:5, :],
        }
        vbuf[...] = jnp.zeros_like(vbuf)
        for r, row in rows.items():
            for j in range(NDEV):
                vbuf[j, r:r + 1, :] = row[:, j * 128:(j + 1) * 128]

        def peer(mask):
            px = x ^ ((mask >> 2) & 1)
            py = y ^ ((mask >> 1) & 1)
            pc = c ^ (mask & 1)
            return (px, py, pc), 4 * px + 2 * py + pc

        def rg_rows(r):
            return pl.ds(pl.multiple_of(r * PR, PR), PR)

        first = []
        for mask in range(1, NDEV):
            dev, r = peer(mask)
            i = mask - 1
            cp = _remote(vbuf.at[r], vrecv.at[i], ssem.at[i], rsem.at[i], dev)
            cp.start()
            first.append(cp)
            cp = _remote(rg_ref.at[rg_rows(r)], rgrecv.at[i], ssem.at[7 + i], rsem.at[7 + i], dev)
            cp.start()
            first.append(cp)

        big = [(lb_in, g_in), (lb_out, g_out), (lb_pg, g_pg), (lb_pp, g_pp)]
        g_hbm = [g_in_h, g_out_h, g_pg_h, g_pp_h]
        swaps, writes = [], []
        for p, (land, gout) in enumerate(big):
            fetches[p].wait()
            own_fetches[p].wait()
            hrows = land.shape[1]
            mine0 = pl.multiple_of(c * hrows, step)
            slot = kme if p == 0 else me
            for r0 in range(0, hrows, step):
                rs = pl.ds(mine0 + r0, step)
                own = own_bufs[p][r0:r0 + step, :]
                s = jnp.zeros((step, land.shape[2]), F32)
                for j in range(land.shape[0]):
                    s = s + jnp.where(slot == j, own, land[j, r0:r0 + step, :]).astype(F32)
                gout[rs, :] = s
            mine = pl.ds(mine0, hrows)
            cp = _remote(gout.at[mine], gout.at[mine], bsem_s.at[p], bsem_r.at[p], sibling)
            cp.start()
            swaps.append(cp)
            cp = pltpu.make_async_copy(gout.at[mine], g_hbm[p].at[mine], osem.at[p])
            cp.start()
            writes.append(cp)

        sv = vbuf[me]
        sr = rg_ref[rg_rows(me), :]
        for i in range(NDEV - 1):
            first[2 * i].wait_recv()
            first[2 * i + 1].wait_recv()
            sv = sv + vrecv[i]
            sr = sr + rgrecv[i]
        v_out[me] = sv
        rg_out[rg_rows(me), :] = sr
        second = []
        for mask in range(1, NDEV):
            dev, r = peer(mask)
            i = mask - 1
            cp = _remote(v_out.at[me], v_out.at[me], ssem.at[14 + i], rsem.at[14 + i], dev)
            cp.start()
            second.append(cp)
            cp = _remote(rg_out.at[rg_rows(me)], rg_out.at[rg_rows(me)], ssem.at[21 + i], rsem.at[21 + i], dev)
            cp.start()
            second.append(cp)
        for p, (land, gout) in enumerate(big):
            hrows = land.shape[1]
            other = pl.ds(pl.multiple_of((1 - c) * hrows, step), hrows)
            _remote(gout.at[other], gout.at[other], bsem_s.at[p], bsem_r.at[p], sibling).wait_recv()
            cp = pltpu.make_async_copy(gout.at[other], g_hbm[p].at[other], osem.at[4 + p])
            cp.start()
            writes.append(cp)
        for mask in range(1, NDEV):
            dev, r = peer(mask)
            i = mask - 1
            _remote(v_out.at[r], v_out.at[r], ssem.at[14 + i], rsem.at[14 + i], dev).wait_recv()
            _remote(rg_out.at[rg_rows(r)], rg_out.at[rg_rows(r)], ssem.at[21 + i], rsem.at[21 + i], dev).wait_recv()
        for cp in writes:
            cp.wait()
        for cp in first + swaps + second:
            cp.wait_send()

    out_shape = [pltpu.HBM(s, F32) for s in shapes]
    out_shape += [jax.ShapeDtypeStruct((NDEV, VROWS, 128), F32), jax.ShapeDtypeStruct((DG, 128), F32)]
    outs = pl.pallas_call(
        body, name="final_reduce", out_shape=out_shape,
        in_specs=[HBM_SPEC] * 13, out_specs=[HBM_SPEC] * 4 + [VMEM_SPEC] * 2,
        scratch_shapes=[pltpu.VMEM((NDEV, VROWS, 128), F32), pltpu.VMEM((NDEV - 1, VROWS, 128), F32),
                        pltpu.VMEM((NDEV - 1, PR, 128), F32)]
        + [pltpu.VMEM(l.shape, BF16) for l in lands]
        + [pltpu.VMEM(l.shape[1:], BF16) for l in lands]
        + [pltpu.VMEM(a.shape, F32) for a in (sm_in, sm_tail, sm_a, sm_b, rg_c)]
        + [pltpu.VMEM(s, F32) for s in shapes]
        + [pltpu.SemaphoreType.DMA((8,)), pltpu.SemaphoreType.DMA((5,)), pltpu.SemaphoreType.DMA((8,)),
                        pltpu.SemaphoreType.DMA((4,)), pltpu.SemaphoreType.DMA((4,)),
                        pltpu.SemaphoreType.DMA((28,)), pltpu.SemaphoreType.DMA((28,))],
        compiler_params=pltpu.CompilerParams(vmem_limit_bytes=48 * MIB),
    )(*[pltpu.with_memory_space_constraint(a, pltpu.HBM)
        for a in (*parts, *lands, sm_in, sm_tail, sm_a, sm_b, rg_c)])
    return list(outs[0:4]), outs[4], outs[5]


def _adam_rows(w, g, m, v):
    m2 = ADAM_B1 * m + (1.0 - ADAM_B1) * g
    v2 = ADAM_B2 * v + (1.0 - ADAM_B2) * (g * g)
    m_hat = m2 / (1.0 - ADAM_B1 ** ADAM_STEP)
    v_hat = v2 / (1.0 - ADAM_B2 ** ADAM_STEP)
    delta = -ADAM_LR * (m_hat / (jnp.sqrt(v_hat) + ADAM_EPS) + ADAM_WD * w)
    return delta, m2, v2


def _adam_big(gs, ws, ms, vs):
    n = len(gs)
    steps = 8

    def body(*refs):
        ins, outs = refs[:4 * n], refs[4 * n:]
        for i in range(n):
            g, w, m, v = (r[...] for r in ins[4 * i:4 * i + 4])
            d, m2, v2 = _adam_rows(w, g, m, v)
            outs[3 * i][...] = d
            outs[3 * i + 1][...] = m2
            outs[3 * i + 2][...] = v2

    in_specs, out_specs, out_shape, args = [], [], [], []
    for g, w, m, v in zip(gs, ws, ms, vs):
        r, c = w.shape
        spec = lambda: pl.BlockSpec((r // steps, c), lambda i: (i, 0))
        in_specs += [spec() for _ in range(4)]
        out_specs += [spec() for _ in range(3)]
        out_shape += [jax.ShapeDtypeStruct((r, c), F32)] * 3
        args += [pltpu.with_memory_space_constraint(a, pltpu.HBM) for a in (g, w, m, v)]
    outs = pl.pallas_call(
        body, name="adam_big", grid=(steps,), in_specs=in_specs, out_specs=out_specs, out_shape=out_shape,
        compiler_params=_cparams(("parallel",), 32),
    )(*args)
    return [tuple(outs[3 * i:3 * i + 3]) for i in range(n)]


_VEC_PARAMS = [
    ("norm_mix_w", ROW_NORM_MIX, 0, D), ("final_norm_w", ROW_FINAL_NORM, 0, D),
    ("b_ple_gate", ROW_B_PG, 0, D), ("ple_norm_w", ROW_PLE_NORM, 0, D),
    ("conv_b", ROW_CB_BA, 0, DG), ("rg_ba", ROW_CB_BA, DG, DG),
    ("rg_bx", ROW_BX_LAM, 0, DG), ("rg_lambda", ROW_BX_LAM, DG, DG),
    ("hg_norm_w", ROW_HG_NW, 0, HD),
]
_SMALL_ORDER = [n for n, _, _, _ in _VEC_PARAMS] + ["hg_lb", "conv_w", "rg_wa", "rg_wx"]


def _adam_small(vred, rgred, ws, ms, vs):
    names = _SMALL_ORDER
    n = len(names)

    def body(vred_ref, rg_ref, *refs):
        w_refs = dict(zip(names, refs[0:n]))
        m_refs = dict(zip(names, refs[n:2 * n]))
        v_refs = dict(zip(names, refs[2 * n:3 * n]))
        outs = refs[3 * n:]
        o_refs = {nm: outs[4 * i:4 * i + 4] for i, nm in enumerate(names)}
        kme = 2 * lax.axis_index("x") + lax.axis_index("y")

        def update(nm, g, idx):
            d, m2, v2 = _adam_rows(w_refs[nm][idx], g, m_refs[nm][idx], v_refs[nm][idx])
            og, od, om, ov = o_refs[nm]
            og[idx] = g
            od[idx] = d
            om[idx] = m2
            ov[idx] = v2

        def packed(row, lane0, width):
            return jnp.concatenate([vred_ref[j, row:row + 1, :] for j in range(lane0 // 128, (lane0 + width) // 128)],
                                   axis=1)

        everything = (slice(None), slice(None))
        for nm, row, lane0, width in _VEC_PARAMS:
            update(nm, packed(row, lane0, width), everything)
        for r in range(2):
            update("hg_lb", packed(ROW_HG_LB, r * DG, DG), (slice(r, r + 1), slice(None)))
        for j in range(4):
            g = vred_ref[(j % 2) * 4 + kme, ROW_CW01 + j // 2:ROW_CW01 + j // 2 + 1, :]
            update("conv_w", g, (slice(j, j + 1), slice(None)))
        for r0 in range(0, DG, 128):
            rs = (slice(r0, r0 + 128), slice(None))
            both = rg_ref[r0:r0 + 128, :]
            update("rg_wa", both[:, 0:RGB], rs)
            update("rg_wx", pltpu.roll(both, RGB, 1)[:, 0:RGB], rs)

    args = [vred, rgred] + [d[nm] for d in (ws, ms, vs) for nm in names]
    out_shape = []
    for nm in names:
        out_shape += [jax.ShapeDtypeStruct(ws[nm].shape, F32)] * 4
    whole = lambda s: pl.BlockSpec(s.shape, lambda i, nd=len(s.shape): (0,) * nd)
    outs = pl.pallas_call(
        body, name="adam_small", out_shape=out_shape, grid=(1,),
        in_specs=[whole(a) for a in args], out_specs=[whole(s) for s in out_shape],
    )(*args)
    return {nm: tuple(outs[4 * i:4 * i + 4]) for i, nm in enumerate(names)}


def _conv_rows(cw_ref):
    return [jnp.concatenate([cw_ref[k, j:j + 1, :] for k in range(NSHARD)], axis=1) for j in range(4)]


def _rg_conv(xa, prev8, cw, cb, rows):
    taps = [_shift_down(xa, prev8, 3, rows), _shift_down(xa, prev8, 2, rows),
            _shift_down(xa, prev8, 1, rows), xa]
    xc = cb
    for j in range(4):
        xc = xc + taps[j] * cw[j]
    return xc, taps


def _block_mask():
    r = lax.broadcasted_iota(I32, (DG, DG), 0)
    c = lax.broadcasted_iota(I32, (DG, DG), 1)
    return (r >> 6) == (c >> 6)


def _dense_from_blocks(wc):
    j = lax.broadcasted_iota(I32, (RGB, DG), 0)
    c = lax.broadcasted_iota(I32, (RGB, DG), 1)
    spread = _mm_exact(wc, ((c & (RGB - 1)) == j).astype(F32))
    return jnp.where(_block_mask(), spread, 0.0)


def _blocks_from_dense(da, dx):
    c = lax.broadcasted_iota(I32, (DG, 128), 0)
    j = lax.broadcasted_iota(I32, (DG, 128), 1)
    hit = (c & (RGB - 1)) == (j & (RGB - 1))
    mask = _block_mask()
    return (_mm_exact(jnp.where(mask, da, 0.0), (hit & (j < RGB)).astype(F32))
            + _mm_exact(jnp.where(mask, dx, 0.0), (hit & (j >= RGB)).astype(F32)))


def _rg_gates(xc, wa, ba, wx, bx, sp, first_row):
    r = _sigmoid(_mm(xc, wa) + ba)
    i = _sigmoid(_mm(xc, wx) + bx)
    log_a = (-RG_C) * r * sp
    a = jnp.exp(log_a)
    a2 = a * a
    one_m_a2 = -jnp.tanh(log_a) * (a2 + 1.0)
    mult = jnp.where(first_row, 1.0, jnp.sqrt(one_m_a2))
    return r, i, a, a2, mult


def _softplus(z):
    return jnp.maximum(z, 0.0) + jnp.log1p(jnp.exp(-jnp.abs(z)))


def _fwd_rglru(pf, pb, cw3, conv_b, wa_c, ba, wx_c, bx, lam, dep):
    tm = 512
    ng = tm // 8

    def body(xa_ref, ga_ref, cw_ref, cb_ref, wa_ref, ba_ref, wx_ref, bx_ref, lam_ref, dep_ref,
             h_ref, ya_ref, a_s, u_s, tail_s, hc_s, wa_s, wx_s):
        i = pl.program_id(0)

        @pl.when(i == 0)
        def _():
            tail_s[...] = jnp.zeros_like(tail_s)
            hc_s[...] = jnp.zeros_like(hc_s)
            wa_s[...] = _dense_from_blocks(wa_ref[...]).astype(BF16)
            wx_s[...] = _dense_from_blocks(wx_ref[...]).astype(BF16)

        rows = lax.broadcasted_iota(I32, (tm, DG), 0)
        xa = xa_ref[...]
        xc, _ = _rg_conv(xa, tail_s[...], _conv_rows(cw_ref), cb_ref[...], rows)
        tail_s[...] = xa[tm - 8:tm, :]
        sp = _softplus(-lam_ref[...])
        rp = _mm(xc, wa_s[...]) + ba_ref[...]
        ip = _mm(xc, wx_s[...]) + bx_ref[...]
        rb = 64
        rows_b = lax.broadcasted_iota(I32, (rb, DG), 0)
        rows8 = rows_b & 7
        carry = hc_s[0:1, :]
        for b0 in range(0, tm, rb):
            sl = slice(b0, b0 + rb)
            r = _sigmoid(rp[sl])
            ig = _sigmoid(ip[sl])
            log_a = (-RG_C) * r * sp
            av = jnp.exp(log_a)
            mult = jnp.sqrt(-jnp.tanh(log_a) * (av * av + 1.0))
            if b0 == 0:
                mult = jnp.where((rows_b + i * tm) == 0, 1.0, mult)
            uv = mult * (ig * xc[sl])
            for d in (1, 2, 4):
                keep = rows8 >= d
                uv = uv + av * jnp.where(keep, _roll_in_groups(uv, d), 0.0)
                av = av * jnp.where(keep, _roll_in_groups(av, d), 1.0)
            ga = ga_ref[sl, :].astype(F32)
            gate = ga * _sigmoid(ga)
            hs = []
            for g in range(rb // 8):
                gs = slice(g * 8, (g + 1) * 8)
                hv = uv[gs] + av[gs] * carry
                carry = hv[7:8, :]
                hs.append(hv)
            hb = jnp.concatenate(hs, axis=0)
            h_ref[sl, :] = hb
            ya_ref[sl, :] = (hb * gate).astype(BF16)
        hc_s[0:1, :] = carry

    vec = lambda: pl.BlockSpec((1, DG), lambda i: (0, 0))
    blocks = lambda: pl.BlockSpec((DG, RGB), lambda i: (0, 0))
    return pl.pallas_call(
        body, name="fwd_rglru", grid=(T // tm,),
        in_specs=[pl.BlockSpec((tm, DG), lambda i: (i, 0)),
                  pl.BlockSpec((tm, DG), lambda i: (i, 0)),
                  pl.BlockSpec((NSHARD, 4, 128), lambda i: (0, 0, 0)), vec(),
                  blocks(), vec(), blocks(), vec(), vec(), ANY_SPEC],
        out_specs=[pl.BlockSpec((tm, DG), lambda i: (i, 0)),
                   pl.BlockSpec((tm, DG), lambda i: (i, 0))],
        out_shape=[jax.ShapeDtypeStruct((T, DG), F32), pltpu.HBM((T, DG), BF16)],
        scratch_shapes=[pltpu.VMEM((tm, DG), F32), pltpu.VMEM((tm, DG), F32),
                        pltpu.VMEM((8, DG), F32), pltpu.VMEM((8, DG), F32),
                        pltpu.VMEM((DG, DG), BF16), pltpu.VMEM((DG, DG), BF16)],
        compiler_params=_cparams(("arbitrary",), 48),
    )(pf, pb, cw3, conv_b, wa_c, ba, wx_c, bx, lam, dep)


def _hg_lower_bound(lb_ref):
    return _sig_pair(lb_ref[0:1, :] - lb_ref[1:2, :])


def _hg_gates(fz, lb, one_m_lb):
    sg, sn = _sig_pair(fz)
    f = lb + one_m_lb * sg
    return sg, sn, f, jnp.log(f), one_m_lb * sn


def _tri(lower):
    r = lax.broadcasted_iota(I32, (CH, CH), 0)
    c = lax.broadcasted_iota(I32, (CH, CH), 1)
    return (r >= c) if lower else (r <= c)


def _split3(v):
    hi = v.astype(BF16)
    r1 = v - hi.astype(F32)
    mid = r1.astype(BF16)
    lo = (r1 - mid.astype(F32)).astype(BF16)
    return hi, mid, lo


def _chunk_cumsum(v, rows64, reverse=False):
    del rows64
    tri = _tri(not reverse).astype(BF16)
    out = []
    for c in range(v.shape[0] // CH):
        pieces = _split3(v[c * CH:(c + 1) * CH])
        out.append(sum(jnp.dot(tri, p, preferred_element_type=F32) for p in pieces))
    return jnp.concatenate(out, axis=0)


def _chunk_rev_cumsum(v, rows64):
    return _chunk_cumsum(v, rows64, reverse=True)


def _hg_recompute(q_ref, f_ref, lb, one_m_lb, rows64, eb_s, enb_s, ekd_s, qe_s, ke_s, kd_s, dec_s):
    nc = q_ref.shape[0] // CH
    sg, sn, f, logf, k = _hg_gates(f_ref[...], lb, one_m_lb)
    q = q_ref[...].astype(F32)
    sq = _sigmoid(q)
    qs = q * sq * (HD ** -0.5)
    b = _chunk_cumsum(logf, rows64)
    for c in range(nc):
        rs = slice(c * CH, (c + 1) * CH)
        b_c = b[rs]
        bl = b_c[CH - 1:CH, :]
        eb, enb, ekd = jnp.exp(b_c), jnp.exp(-b_c), jnp.exp(bl - b_c)
        if eb_s is not None:
            eb_s[rs, :] = eb
            enb_s[rs, :] = enb
            ekd_s[rs, :] = ekd
        qe_s[rs, :] = (qs[rs] * eb).astype(BF16)
        ke_s[rs, :] = (k[rs] * enb).astype(BF16)
        kd_s[rs, :] = (k[rs] * ekd).astype(BF16)
        dec_s[c:c + 1, :] = jnp.exp(bl)
    return sg, sn, f, k, q, sq, qs


def _fwd_hgrn2(pf, pb, hg_lb, hg_nw):
    tm = 512
    nc = tm // CH

    def body(q_ref, f_ref, v_ref, g_ref, lb_ref, nw_ref, yb_ref, o_ref, sp_ref,
             st_s, qe_s, ke_s, kd_s, vb_s, dec_s, p_s, ds_s):
        i = pl.program_id(0)

        @pl.when(i == 0)
        def _():
            st_s[...] = jnp.zeros_like(st_s)

        lb, one_m_lb = _hg_lower_bound(lb_ref)
        rows64 = lax.broadcasted_iota(I32, (tm, DG), 0) & (CH - 1)
        _hg_recompute(q_ref, f_ref, lb, one_m_lb, rows64, None, None, None, qe_s, ke_s, kd_s, dec_s)
        vb_s[...] = v_ref[...]
        mask = _tri(True)
        items = [(c, hd, slice(c * CH, (c + 1) * CH), slice(hd * HD, (hd + 1) * HD))
                 for c in range(nc) for hd in range(NH)]
        for c, hd, rs, cols in items:
            p_s[c * NH + hd] = jnp.where(mask, _mm_nt(qe_s[rs, cols], ke_s[rs, cols]), 0.0).astype(BF16)
            ds_s[c * NH + hd] = _mm_tn(vb_s[rs, cols], kd_s[rs, cols])
        for c, hd, rs, cols in items:
            st = st_s[hd]
            sp_ref[hd, c] = st
            st_s[hd] = st * dec_s[c:c + 1, cols] + ds_s[c * NH + hd]
        for c, hd, rs, cols in items:
            o_ref[rs, cols] = _mm(p_s[c * NH + hd], vb_s[rs, cols]) + _mm_nt(qe_s[rs, cols], sp_ref[hd, c])
        nw = nw_ref[...]
        for hd in range(NH):
            cols = slice(hd * HD, (hd + 1) * HD)
            o = o_ref[:, cols]
            so = lax.rsqrt(jnp.mean(o * o, axis=-1, keepdims=True) + EPS)
            g = g_ref[:, cols].astype(F32)
            sg = _sigmoid(g)
            yb_ref[:, cols] = (o * so * nw * (g * sg)).astype(BF16)

    col = lambda j: pl.BlockSpec((tm, DG), lambda i: (i, j))
    return pl.pallas_call(
        body, name="fwd_hgrn2", grid=(T // tm,),
        in_specs=[col(1), col(1), col(2), col(3),
                  pl.BlockSpec((2, DG), lambda i: (0, 0)),
                  pl.BlockSpec((1, HD), lambda i: (0, 0))],
        out_specs=[pl.BlockSpec((tm, DG), lambda i: (i, 0)),
                   pl.BlockSpec((tm, DG), lambda i: (i, 0)),
                   pl.BlockSpec((NH, nc, HD, HD), lambda i: (0, i, 0, 0))],
        out_shape=[pltpu.HBM((T, DG), BF16), jax.ShapeDtypeStruct((T, DG), F32),
                   jax.ShapeDtypeStruct((NH, NCHUNK, HD, HD), F32)],
        scratch_shapes=[pltpu.VMEM((NH, HD, HD), F32),
                        pltpu.VMEM((tm, DG), BF16), pltpu.VMEM((tm, DG), BF16), pltpu.VMEM((tm, DG), BF16),
                        pltpu.VMEM((tm, DG), BF16), pltpu.VMEM((nc, DG), F32),
                        pltpu.VMEM((nc * NH, CH, CH), BF16), pltpu.VMEM((nc * NH, HD, HD), F32)],
        compiler_params=_cparams(("arbitrary",), 48),
    )(pb, pf, pb, pb, hg_lb, hg_nw)


def _tail_fwd_bwd(x, p, tgt, ya, yb, w_out_b, w_pg_b, w_pp_b, ple_nw, b_pg, fnw):
    tm = 512
    nt = T // tm
    QR = D // NSHARD

    def body(x_ref, p_ref, t_ref, ya_ref, yb_ref, wo_ref, wg_ref, wp_ref, pw_ref, b_ref, fw_ref,
             dh1_ref, dyab_ref, dwo_ref, dwg_ref, dwp_ref, sm_ref, dwo_s, dwg_s, dwp_s):
        i = pl.program_id(0)

        @pl.when(i == 0)
        def _():
            dwo_s[...] = jnp.zeros_like(dwo_s)
            dwg_s[...] = jnp.zeros_like(dwg_s)
            dwp_s[...] = jnp.zeros_like(dwp_s)
            sm_ref[...] = jnp.zeros_like(sm_ref)

        ya = ya_ref[...]
        yb = yb_ref[...]
        pv = p_ref[...].astype(BF16)
        pw = pw_ref[...]
        fw = fw_ref[...]
        h1 = x_ref[...] + _mm(ya, wo_ref[0:DG, :]) + _mm(yb, wo_ref[DG:D, :])
        s2 = lax.rsqrt(jnp.mean(h1 * h1, axis=-1, keepdims=True) + EPS)
        n2h = h1 * s2
        n2 = (n2h * pw).astype(BF16)
        z = _mm(n2, wg_ref[...]) + b_ref[...]
        gate = _sigmoid(z)
        pp = jnp.concatenate([_mm(pv, wp_ref[k]) for k in range(NSHARD)], axis=1)
        h2 = h1 + gate * pp
        s3 = lax.rsqrt(jnp.mean(h2 * h2, axis=-1, keepdims=True) + EPS)
        hn = h2 * s3
        err = hn * fw - t_ref[...]
        sm_ref[0:1, :] += _rowsum(err * err)
        dy = err * (1.0 / D)
        sm_ref[1:2, :] += _rowsum(dy * hn)
        g3 = dy * fw
        dh2 = s3 * (g3 - hn * jnp.mean(g3 * hn, axis=-1, keepdims=True))
        dpp = (dh2 * gate).astype(BF16)
        dz = dh2 * pp * gate * (1.0 - gate)
        sm_ref[2:3, :] += _rowsum(dz)
        dzb = dz.astype(BF16)
        dwg_s[...] += _mm_tn(n2, dzb)
        dn2 = _mm_nt(dzb, wg_ref[...])
        for k in range(NSHARD):
            dwp_s[k] += _mm_tn(pv, dpp[:, k * PLE:(k + 1) * PLE])
        sm_ref[3:4, :] += _rowsum(dn2 * n2h)
        g2 = dn2 * pw
        dh1 = dh2 + s2 * (g2 - n2h * jnp.mean(g2 * n2h, axis=-1, keepdims=True))
        dh1_ref[...] = dh1
        dh1b = dh1.astype(BF16)
        dyab_ref[...] = _mm_nt(dh1b, wo_ref[...])
        dwo_s[0:DG, :] += _mm_tn(ya, dh1b)
        dwo_s[DG:D, :] += _mm_tn(yb, dh1b)

        @pl.when(i == nt - 1)
        def _():
            total = jnp.sum(sm_ref[0:1, :], axis=-1, keepdims=True) * (0.5 / D)
            sm_ref[4:5, :] = jnp.broadcast_to(total, (1, D))
            for k in range(NSHARD):
                dwo_ref[k] = dwo_s[k * QR:(k + 1) * QR, :].astype(BF16)
                dwg_ref[k] = dwg_s[k * QR:(k + 1) * QR, :].astype(BF16)
                dwp_ref[k] = dwp_s[k].astype(BF16)

    row = lambda w: pl.BlockSpec((tm, w), lambda i: (i, 0))
    const2 = lambda s: pl.BlockSpec(s, lambda i: (0, 0), pipeline_mode=pl.Buffered(1))
    const3 = lambda s: pl.BlockSpec(s, lambda i: (0, 0, 0), pipeline_mode=pl.Buffered(1))
    return pl.pallas_call(
        body, name="tail_fwd_bwd", grid=(nt,),
        in_specs=[row(D), row(PLE), row(D), row(DG), row(DG),
                  const2((D, D)), const2((D, D)), const3((NSHARD, PLE, PLE)),
                  const2((1, D)), const2((1, D)), const2((1, D))],
        out_specs=[row(D), row(D), const3((NSHARD, QR, D)), const3((NSHARD, QR, D)),
                   const3((NSHARD, PLE, PLE)), const2((8, D))],
        out_shape=[jax.ShapeDtypeStruct((T, D), F32), jax.ShapeDtypeStruct((T, D), F32),
                   jax.ShapeDtypeStruct((NSHARD, QR, D), BF16), jax.ShapeDtypeStruct((NSHARD, QR, D), BF16),
                   jax.ShapeDtypeStruct((NSHARD, PLE, PLE), BF16), jax.ShapeDtypeStruct((8, D), F32)],
        scratch_shapes=[pltpu.VMEM((D, D), F32), pltpu.VMEM((D, D), F32), pltpu.VMEM((NSHARD, PLE, PLE), F32)],
        compiler_params=_cparams(("arbitrary",), 62),
    )(x, p, tgt, pltpu.with_memory_space_constraint(ya, pltpu.HBM), pltpu.with_memory_space_constraint(yb, pltpu.HBM),
      w_out_b, w_pg_b, w_pp_b, ple_nw, b_pg, fnw)


def _bwd_rglru(pf, pb, h, dyab, cw3, conv_b, wa_c, ba, wx_c, bx, lam, dep):
    tm = 512
    nt = T // tm
    ng = tm // 8

    def body(xa_ref, ga_ref, xp_ref, h_ref, hp_ref, dya_ref, cw_ref, cb_ref, wa_ref, ba_ref, wx_ref, bx_ref,
             lam_ref, dep_ref, da_ref, rg_ref, sm_ref, a_s, g_s, cg_s, nxt_s, wa_s, wx_s, dwa_s, dwx_s):
        i = pl.program_id(0)
        tile = nt - 1 - i

        @pl.when(i == 0)
        def _():
            dwa_s[...] = jnp.zeros_like(dwa_s)
            dwx_s[...] = jnp.zeros_like(dwx_s)
            sm_ref[...] = jnp.zeros_like(sm_ref)
            cg_s[...] = jnp.zeros_like(cg_s)
            nxt_s[...] = jnp.zeros_like(nxt_s)
            wa_s[...] = _dense_from_blocks(wa_ref[...]).astype(BF16)
            wx_s[...] = _dense_from_blocks(wx_ref[...]).astype(BF16)

        rows = lax.broadcasted_iota(I32, (tm, DG), 0)
        has_prev = tile > 0
        xa = xa_ref[...]
        xprev = jnp.where(has_prev, xp_ref[...], 0.0)
        cw = _conv_rows(cw_ref)
        xc, taps = _rg_conv(xa, xprev, cw, cb_ref[...], rows)
        lam_v = lam_ref[...]
        sp = _softplus(-lam_v)
        first_row = (rows + tile * tm) == 0
        r, ig, a, a2, mult = _rg_gates(xc, wa_s[...], ba_ref[...], wx_s[...], bx_ref[...], sp, first_row)
        hv = h_ref[...]
        hprev = jnp.where(has_prev, hp_ref[...], 0.0)
        h_m1 = _shift_down(hv, hprev, 1, rows)
        ga = ga_ref[...].astype(F32)
        sg = _sigmoid(ga)
        dya = dya_ref[...]
        dga = dya * hv * (sg * (1.0 + ga * (1.0 - sg)))

        av = jnp.where(rows == tm - 1, 1.0, pltpu.roll(a, tm - 1, 0))
        gv = dya * (ga * sg)
        rows8 = rows & 7
        for d in (1, 2, 4):
            keep = rows8 < 8 - d
            gv = gv + av * jnp.where(keep, _roll_in_groups(gv, 8 - d), 0.0)
            av = av * jnp.where(keep, _roll_in_groups(av, 8 - d), 1.0)
        a_s[...] = av
        g_s[...] = gv
        carry = cg_s[0:1, :]
        for g in range(ng - 1, -1, -1):
            sl = slice(g * 8, (g + 1) * 8)
            ab, gb = a_s[sl, :], g_s[sl, :]
            g_s[sl, :] = gb + ab * carry
            carry = gb[0:1, :] + ab[0:1, :] * carry
        cg_s[0:1, :] = a[0:1, :] * carry

        gt = g_s[...]
        da = gt * h_m1
        ixc = ig * xc
        di = gt * mult * xc
        dxc = gt * mult * ig
        dlog_a = da * a + jnp.where(first_row, 0.0, gt * ixc * (-a2 / mult))
        sm_ref[3:4, :] += _rowsum(dlog_a * ((-RG_C) * r))
        dpr_f = dlog_a * ((-RG_C) * sp) * r * (1.0 - r)
        dpi_f = di * ig * (1.0 - ig)
        sm_ref[1:2, :] += _rowsum(dpr_f)
        sm_ref[2:3, :] += _rowsum(dpi_f)
        dpr = dpr_f.astype(BF16)
        dpi = dpi_f.astype(BF16)
        xcb = xc.astype(BF16)
        dwa_s[...] += _mm_tn(xcb, dpr)
        dwx_s[...] += _mm_tn(xcb, dpi)
        dxc = dxc + _mm_nt(dpr, wa_s[...]) + _mm_nt(dpi, wx_s[...])
        sm_ref[0:1, :] += _rowsum(dxc)
        for j in range(4):
            sm_ref[4 + j:5 + j, :] += _rowsum(dxc * taps[j])
        nxt = nxt_s[...]
        dxa = (dxc * cw[3] + _shift_up(dxc, nxt, 1, rows) * cw[2]
               + _shift_up(dxc, nxt, 2, rows) * cw[1] + _shift_up(dxc, nxt, 3, rows) * cw[0])
        nxt_s[...] = dxc[0:8, :]
        da_ref[:, 0:DG] = dxa.astype(BF16)
        da_ref[:, DG:D] = dga.astype(BF16)

        @pl.when(i == nt - 1)
        def _():
            _, s_neg = _sig_pair(lam_v)
            sm_ref[3:4, :] = sm_ref[3:4, :] * (-s_neg)
            rg_ref[...] = _blocks_from_dense(dwa_s[...], dwx_s[...])

    vec = lambda: pl.BlockSpec((1, DG), lambda i: (0, 0))
    blocks = lambda: pl.BlockSpec((DG, RGB), lambda i: (0, 0))
    prev8 = lambda: pl.BlockSpec((8, DG), lambda i: (jnp.maximum((nt - 1 - i) * (tm // 8) - 1, 0), 0))
    return pl.pallas_call(
        body, name="bwd_rglru", grid=(nt,),
        in_specs=[pl.BlockSpec((tm, DG), lambda i: (nt - 1 - i, 0)),
                  pl.BlockSpec((tm, DG), lambda i: (nt - 1 - i, 0)),
                  prev8(),
                  pl.BlockSpec((tm, DG), lambda i: (nt - 1 - i, 0)),
                  prev8(),
                  pl.BlockSpec((tm, DG), lambda i: (nt - 1 - i, 0)),
                  pl.BlockSpec((NSHARD, 4, 128), lambda i: (0, 0, 0)), vec(),
                  blocks(), vec(), blocks(), vec(), vec(), ANY_SPEC],
        out_specs=[pl.BlockSpec((tm, D), lambda i: (nt - 1 - i, 0)),
                   pl.BlockSpec((DG, 128), lambda i: (0, 0)),
                   pl.BlockSpec((8, DG), lambda i: (0, 0))],
        out_shape=[jax.ShapeDtypeStruct((T, D), BF16), jax.ShapeDtypeStruct((DG, 128), F32),
                   jax.ShapeDtypeStruct((8, DG), F32)],
        scratch_shapes=[pltpu.VMEM((tm, DG), F32), pltpu.VMEM((tm, DG), F32),
                        pltpu.VMEM((8, DG), F32), pltpu.VMEM((8, DG), F32),
                        pltpu.VMEM((DG, DG), BF16), pltpu.VMEM((DG, DG), BF16),
                        pltpu.VMEM((DG, DG), F32), pltpu.VMEM((DG, DG), F32)],
        compiler_params=_cparams(("arbitrary",), 56),
    )(pf, pb, pf, h, h, dyab, cw3, conv_b, wa_c, ba, wx_c, bx, lam, dep)


def _bwd_hgrn2(pf, pb, o, s_prev, dyab, hg_lb, hg_nw, dep):
    tm = 512
    nt = T // tm
    nc = tm // CH

    def body(q_ref, f_ref, v_ref, g_ref, o_ref, sp_ref, dy_ref, lb_ref, nw_ref, dep_ref, db_ref, sm_ref,
             dst_s, eb_s, enb_s, ekd_s, qe_s, ke_s, kd_s, vb_s, do_s, dec_s, ddec_s, p_s, dp_s,
             g_s, dsta_s, dva_s, dqe_s, dke_s, dkd_s, dlf_s):
        i = pl.program_id(0)

        @pl.when(i == 0)
        def _():
            sm_ref[...] = jnp.zeros_like(sm_ref)
            dst_s[...] = jnp.zeros_like(dst_s)

        lb, one_m_lb = _hg_lower_bound(lb_ref)
        rows64 = lax.broadcasted_iota(I32, (tm, DG), 0) & (CH - 1)
        sg, sn, f, k, q, sq, qs = _hg_recompute(
            q_ref, f_ref, lb, one_m_lb, rows64, eb_s, enb_s, ekd_s, qe_s, ke_s, kd_s, dec_s)
        vb_s[...] = v_ref[...]

        nw = nw_ref[...]
        for hd in range(NH):
            cols = slice(hd * HD, (hd + 1) * HD)
            g = g_ref[:, cols].astype(F32)
            sgg = _sigmoid(g)
            o = o_ref[:, cols]
            so = lax.rsqrt(jnp.mean(o * o, axis=-1, keepdims=True) + EPS)
            oh = o * so
            dyb = dy_ref[:, cols]
            db_ref[:, 3 * DG + hd * HD:3 * DG + (hd + 1) * HD] = (
                dyb * (oh * nw) * (sgg * (1.0 + g * (1.0 - sgg)))).astype(BF16)
            don = dyb * (g * sgg)
            sm_ref[1:2, 0:HD] += _rowsum(don * oh)
            gw = don * nw
            do_s[:, cols] = (so * (gw - oh * jnp.mean(gw * oh, axis=-1, keepdims=True))).astype(BF16)

        mask = _tri(True)
        items = [(c, hd, slice(c * CH, (c + 1) * CH), slice(hd * HD, (hd + 1) * HD))
                 for c in range(nc) for hd in range(NH)]
        for c, hd, rs, cols in items:
            p_s[c * NH + hd] = jnp.where(mask, _mm_nt(qe_s[rs, cols], ke_s[rs, cols]), 0.0).astype(BF16)
            dp_s[c * NH + hd] = jnp.where(mask, _mm_nt(do_s[rs, cols], vb_s[rs, cols]), 0.0).astype(BF16)
        for c, hd, rs, cols in items:
            n = c * NH + hd
            dva_s[rs, cols] = _mm_tn(p_s[n], do_s[rs, cols])
            dqe_s[rs, cols] = _mm(dp_s[n], ke_s[rs, cols])
            dke_s[rs, cols] = _mm_tn(dp_s[n], qe_s[rs, cols])
            g_s[n] = _mm_tn(do_s[rs, cols], qe_s[rs, cols])
        for c, hd, rs, cols in reversed(items):
            n = c * NH + hd
            dst = dst_s[hd]
            dsta_s[n] = dst
            dst_s[hd] = dst * dec_s[c:c + 1, cols] + g_s[n]
        for c, hd, rs, cols in items:
            n = c * NH + hd
            dst = dsta_s[n]
            st_prev = sp_ref[hd, c]
            dv = dva_s[rs, cols] + _mm_nt(kd_s[rs, cols], dst)
            db_ref[rs, 2 * DG + hd * HD:2 * DG + (hd + 1) * HD] = dv.astype(BF16)
            dqe_s[rs, cols] += _mm(do_s[rs, cols], st_prev)
            dkd_s[rs, cols] = _mm(vb_s[rs, cols], dst)
            ddec_s[c:c + 1, cols] = _rowsum(dst * st_prev)

        eb, enb, ekd = eb_s[...], enb_s[...], ekd_s[...]
        dqe, dke, dkd = dqe_s[...], dke_s[...], dkd_s[...]
        t_kd = dkd * (k * ekd)
        rc = _chunk_rev_cumsum(dqe * (qs * eb) - dke * (k * enb) - t_kd, rows64)
        for c in range(nc):
            rs = slice(c * CH, (c + 1) * CH)
            dbl = _rowsum(t_kd[rs]) + ddec_s[c:c + 1, :] * dec_s[c:c + 1, :]
            dlf_s[rs, :] = rc[rs] + dbl
        t = dlf_s[...] / f - (dke * enb + dkd * ekd)
        db_ref[:, DG:2 * DG] = (one_m_lb * sg * sn * t).astype(BF16)
        sm_ref[0:1, :] += _rowsum(sn * t)
        db_ref[:, 0:DG] = (dqe * eb * (sq * (1.0 + q * (1.0 - sq))) * (HD ** -0.5)).astype(BF16)

        @pl.when(i == nt - 1)
        def _():
            dsm = sm_ref[0:1, :] * (lb * one_m_lb)
            sm_ref[2:3, :] = dsm
            sm_ref[3:4, :] = -dsm

    col = lambda j: pl.BlockSpec((tm, DG), lambda i: (nt - 1 - i, j))
    big = lambda dt: pltpu.VMEM((tm, DG), dt)
    return pl.pallas_call(
        body, name="bwd_hgrn2", grid=(nt,),
        in_specs=[col(1), col(1), col(2), col(3),
                  pl.BlockSpec((tm, DG), lambda i: (nt - 1 - i, 0)),
                  pl.BlockSpec((NH, nc, HD, HD), lambda i: (0, nt - 1 - i, 0, 0)),
                  pl.BlockSpec((tm, DG), lambda i: (nt - 1 - i, 1)),
                  pl.BlockSpec((2, DG), lambda i: (0, 0)),
                  pl.BlockSpec((1, HD), lambda i: (0, 0)), ANY_SPEC],
        out_specs=[pl.BlockSpec((tm, 4 * DG), lambda i: (nt - 1 - i, 0)),
                   pl.BlockSpec((8, DG), lambda i: (0, 0))],
        out_shape=[jax.ShapeDtypeStruct((T, 4 * DG), BF16), jax.ShapeDtypeStruct((8, DG), F32)],
        scratch_shapes=[pltpu.VMEM((NH, HD, HD), F32),
                        big(F32), big(F32), big(F32),
                        big(BF16), big(BF16), big(BF16), big(BF16), big(BF16),
                        pltpu.VMEM((nc, DG), F32), pltpu.VMEM((nc, DG), F32),
                        pltpu.VMEM((nc * NH, CH, CH), BF16), pltpu.VMEM((nc * NH, CH, CH), BF16),
                        pltpu.VMEM((nc * NH, HD, HD), F32), pltpu.VMEM((nc * NH, HD, HD), F32),
                        big(F32), big(F32), big(F32), big(F32), big(F32)],
        compiler_params=_cparams(("arbitrary",), 56),
    )(pb, pf, pb, pb, o, s_prev, dyab, hg_lb, hg_nw, dep)


def _dproj_pieces(k, da_ref, db_ref):
    if k == 0:
        return [(da_ref[:, 0:SHW], 0)]
    if k == 1:
        return [(da_ref[:, SHW:D], 0), (db_ref[:, 0:DG], D - SHW)]
    if k == 2:
        return [(db_ref[:, DG:DG + SHW], 0)]
    return [(db_ref[:, DG + SHW:4 * DG], 0)]


def _bwd_inproj_dx(x, dh1, d_a, d_b, w_in_b, nw, dep):
    tm = 512
    nt = T // tm

    def body(x_ref, dh1_ref, da_ref, db_ref, w_ref, nw_ref, dep_ref, dx_ref, sm_ref):
        i = pl.program_id(0)

        @pl.when(i == 0)
        def _():
            sm_ref[...] = jnp.zeros_like(sm_ref)

        du = None
        for k in range(NSHARD):
            for val, off in _dproj_pieces(k, da_ref, db_ref):
                t = _mm_nt(val, w_ref[k, :, off:off + val.shape[1]])
                du = t if du is None else du + t
        xv = x_ref[...]
        s = lax.rsqrt(jnp.mean(xv * xv, axis=-1, keepdims=True) + EPS)
        xh = xv * s
        sm_ref[0:1, :] += _rowsum(du * xh)
        g = du * nw_ref[...]
        dx_ref[...] = dh1_ref[...] + s * (g - xh * jnp.mean(g * xh, axis=-1, keepdims=True))

    row = lambda w: pl.BlockSpec((tm, w), lambda i: (i, 0))
    return pl.pallas_call(
        body, name="bwd_inproj_dx", grid=(nt,),
        in_specs=[row(D), row(D), row(D), row(4 * DG),
                  pl.BlockSpec((NSHARD, D, SHW), lambda i: (0, 0, 0), pipeline_mode=pl.Buffered(1)),
                  pl.BlockSpec((1, D), lambda i: (0, 0)), ANY_SPEC],
        out_specs=[row(D), pl.BlockSpec((8, D), lambda i: (0, 0))],
        out_shape=[jax.ShapeDtypeStruct((T, D), F32), jax.ShapeDtypeStruct((8, D), F32)],
        compiler_params=_cparams(("arbitrary",), 56),
    )(x, dh1, d_a, d_b, w_in_b, nw, dep)


def _bwd_inproj_dw23(u_b, d_b):
    tm = 1024
    nt = T // tm

    def body(u_ref, db_ref, dw_ref, acc):
        i = pl.program_id(0)

        @pl.when(i == 0)
        def _():
            acc[...] = jnp.zeros_like(acc)

        u = u_ref[...]
        for j, k in enumerate((2, 3)):
            for val, off in _dproj_pieces(k, None, db_ref):
                acc[j, :, off:off + val.shape[1]] += _mm_tn(u, val)

        @pl.when(i == nt - 1)
        def _():
            for j in range(2):
                for r0 in range(0, D, 256):
                    dw_ref[j, r0:r0 + 256, :] = acc[j, r0:r0 + 256, :].astype(BF16)

    row = lambda w: pl.BlockSpec((tm, w), lambda i: (i, 0))
    return pl.pallas_call(
        body, name="bwd_inproj_dw23", grid=(nt,), in_specs=[row(D), row(4 * DG)],
        out_specs=pl.BlockSpec((2, D, SHW), lambda i: (0, 0, 0)),
        out_shape=pltpu.HBM((2, D, SHW), BF16),
        scratch_shapes=[pltpu.VMEM((2, D, SHW), F32)],
        compiler_params=_cparams(("arbitrary",), 48),
    )(u_b, d_b)


def _bwd_inproj_dw01(u_b, d_a, d_b, dw23):
    tm = 1024
    nt = T // tm
    H = D // 2
    step = 128

    def body(u_ref, da_ref, db_ref, p23_ref, dw_ref, acc, send_s, recv_s, ssem, rsem):
        i = pl.program_id(0)
        x, y, c = lax.axis_index("x"), lax.axis_index("y"), lax.axis_index("c")
        sibling = (x, y, 1 - c)
        mine0 = pl.multiple_of(c * H, step)
        other0 = pl.multiple_of((1 - c) * H, step)

        def swap(k):
            return _remote(send_s.at[k], recv_s.at[k], ssem.at[k], rsem.at[k], sibling)

        @pl.when(i == 0)
        def _():
            acc[...] = jnp.zeros_like(acc)
            for k in (2, 3):
                for r0 in range(0, H, step):
                    send_s[k, r0:r0 + step, :] = p23_ref[k - 2, pl.ds(other0 + r0, step), :]
                swap(k).start()

        u = u_ref[...]
        for j, k in enumerate((0, 1)):
            for val, off in _dproj_pieces(k, da_ref, db_ref):
                acc[j, :, off:off + val.shape[1]] += _mm_tn(u, val)

        @pl.when(i == nt - 1)
        def _():
            copies = [swap(k) for k in range(NSHARD)]
            for k in (0, 1):
                for r0 in range(0, H, step):
                    send_s[k, r0:r0 + step, :] = acc[k, pl.ds(other0 + r0, step), :].astype(BF16)
                copies[k].start()
            for k in (2, 3, 0, 1):
                copies[k].wait_recv()
                for r0 in range(0, H, step):
                    rs = pl.ds(mine0 + r0, step)
                    own = acc[k, rs, :] if k < 2 else p23_ref[k - 2, rs, :].astype(F32)
                    dw_ref[k, r0:r0 + step, :] = (own + recv_s[k, r0:r0 + step, :].astype(F32)).astype(BF16)
            for cp in copies:
                cp.wait_send()

    row = lambda w: pl.BlockSpec((tm, w), lambda i: (i, 0))
    return pl.pallas_call(
        body, name="bwd_inproj_dw01", grid=(nt,),
        in_specs=[row(D), row(D), row(4 * DG),
                  pl.BlockSpec((2, D, SHW), lambda i: (0, 0, 0), pipeline_mode=pl.Buffered(1))],
        out_specs=pl.BlockSpec((NSHARD, H, SHW), lambda i: (0, 0, 0)),
        out_shape=jax.ShapeDtypeStruct((NSHARD, H, SHW), BF16),
        scratch_shapes=[pltpu.VMEM((2, D, SHW), F32), pltpu.VMEM((NSHARD, H, SHW), BF16),
                        pltpu.VMEM((NSHARD, H, SHW), BF16),
                        pltpu.SemaphoreType.DMA((NSHARD,)), pltpu.SemaphoreType.DMA((NSHARD,))],
        compiler_params=_cparams(("arbitrary",), 56),
    )(u_b, d_a, d_b, dw23)


_OUT_ORDER = ["norm_mix_w", "w_in", "conv_w", "conv_b", "rg_wa", "rg_ba", "rg_wx", "rg_bx", "rg_lambda", "hg_lb",
              "hg_norm_w", "w_out", "ple_norm_w", "w_ple_gate", "b_ple_gate", "w_ple_proj", "final_norm_w"]
_BIG = ["w_in", "w_out", "w_ple_gate", "w_ple_proj"]


def _small_view(name, a):
    if name in ("rg_wa", "rg_wx"):
        return a.reshape(DG, RGB)
    if name == "conv_w":
        return a.reshape(4, 128)
    if name == "final_norm_w":
        return a.reshape(1, D)
    return a


def _landing(rows, cols):
    return lax.empty((NDEV, rows, cols), BF16)


def kernel(x, p, norm_mix_w, w_in, conv_w, conv_b, rg_wa, rg_ba, rg_wx, rg_bx, rg_lambda, hg_lb, hg_norm_w, w_out, ple_norm_w, w_ple_gate, b_ple_gate, w_ple_proj, final_norm_w, loss_target, m_norm_mix_w, m_w_in, m_conv_w, m_conv_b, m_rg_wa, m_rg_ba, m_rg_wx, m_rg_bx, m_rg_lambda, m_hg_lb, m_hg_norm_w, m_w_out, m_ple_norm_w, m_w_ple_gate, m_b_ple_gate, m_w_ple_proj, m_final_norm_w, v_norm_mix_w, v_w_in, v_conv_w, v_conv_b, v_rg_wa, v_rg_ba, v_rg_wx, v_rg_bx, v_rg_lambda, v_hg_lb, v_hg_norm_w, v_w_out, v_ple_norm_w, v_w_ple_gate, v_b_ple_gate, v_w_ple_proj, v_final_norm_w):
    given = dict(locals())
    x2, p2, tgt = x[0], p[0, 0], loss_target[0]
    hbm = lambda a: pltpu.with_memory_space_constraint(a, pltpu.HBM)
    norm_mix_w, conv_b, rg_ba, rg_bx, rg_lambda, hg_lb, hg_norm_w, ple_norm_w, b_ple_gate = (
        hbm(a) for a in (norm_mix_w, conv_b, rg_ba, rg_bx, rg_lambda, hg_lb, hg_norm_w, ple_norm_w, b_ple_gate))
    wa_c, wx_c = hbm(_small_view("rg_wa", rg_wa)), hbm(_small_view("rg_wx", rg_wx))

    w_in_b, l_out, l_pg, l_pp, cw3, pf, pb, u_b = _gather_inproj(
        x2, norm_mix_w, w_in[0], w_out[0], w_ple_gate[0], w_ple_proj[0], conv_w[0])
    g_ssem, g_rsem, g_lands, tok = _gather_rest_start([l_out, l_pg, l_pp])

    h, ya = _fwd_rglru(pf, pb, cw3, conv_b, wa_c, rg_ba, wx_c, rg_bx, rg_lambda, tok)
    yb, o, s_prev = _fwd_hgrn2(pf, pb, hg_lb, hg_norm_w)
    w_out_b, w_pg_b, w_pp_b = _gather_rest_wait(g_ssem, g_rsem, g_lands, yb)
    dh1, dyab, dwo_b, dwg_b, dwp_b, sm_tail = _tail_fwd_bwd(
        x2, p2, tgt, ya, yb, w_out_b.reshape(D, D), w_pg_b.reshape(D, D), w_pp_b,
        ple_norm_w, b_ple_gate, final_norm_w.reshape(1, D))

    QH = D // NSHARD // 2
    r1 = _rs_start("rs_start_tail", [dwo_b, dwg_b, dwp_b], (0, 1, 2, 3),
                   [_landing(QH, D), _landing(QH, D), _landing(PLE // 2, PLE)])
    d_b, sm_b = _bwd_hgrn2(pf, pb, o, s_prev, dyab, hg_lb, hg_norm_w, r1[4])
    dw23 = _bwd_inproj_dw23(u_b, d_b)
    d_a, rg_c, sm_a = _bwd_rglru(pf, pb, h, dyab, cw3, conv_b, wa_c, rg_ba, wx_c, rg_bx, rg_lambda, dw23)
    dw_in = _bwd_inproj_dw01(u_b, d_a, d_b, dw23)
    r2 = _rs_start("rs_start_in", [dw_in], (0, 1, 2, 3), [lax.empty((NSHARD, D // 2, SHW), BF16)], chip_sums=True)
    grad_x, sm_in = _bwd_inproj_dx(x2, dh1, d_a, d_b, w_in_b, norm_mix_w, r2[4])

    parts1, lands1 = _rs_wait("rs_wait_tail", r1[0], r1[1], r1[2], (0, 1, 2, 3), r1[3], sm_in)
    parts2, lands2 = _rs_wait("rs_wait_in", r2[0], r2[1], r2[2], (0, 1, 2, 3), r2[3], sm_in, chip_sums=True)
    g_big, vred, rgred = _final_reduce(parts2 + parts1, lands2 + lands1, sm_in, sm_tail, sm_a, sm_b, rg_c)

    upd_big = _adam_big(g_big, [given[n][0] for n in _BIG], [given["m_" + n][0] for n in _BIG],
                        [given["v_" + n][0] for n in _BIG])
    small = _adam_small(vred, rgred,
                        {n: _small_view(n, given[n]) for n in _SMALL_ORDER},
                        {n: _small_view(n, given["m_" + n]) for n in _SMALL_ORDER},
                        {n: _small_view(n, given["v_" + n]) for n in _SMALL_ORDER})

    loss = vred[0, ROW_LOSS, 0]
    outs = [loss, grad_x[None]]
    for ki in range(4):
        for n in _OUT_ORDER:
            if n in _BIG:
                i = _BIG.index(n)
                a = g_big[i] if ki == 0 else upd_big[i][ki - 1]
                outs.append(a[None])
            else:
                outs.append(small[n][ki].reshape(given[n].shape))
    return tuple(outs)
```

```python
import jax
import jax.numpy as jnp
from jax import lax
from jax.experimental import pallas as pl
from jax.experimental.pallas import tpu as pltpu

F32 = jnp.float32
BF16 = jnp.bfloat16
I32 = jnp.int32
MESH = pl.DeviceIdType.MESH

T = 4096
D = 1024
DG = 512
DIN = 3072
PLE = 256
NH = 4
HD = 128
CH = 64
NCHUNK = T // CH
RGB = 64
EPS = 1e-6
RG_C = 8.0
NSHARD = 4
SHW = DIN // NSHARD
NDEV = 8

ADAM_LR = 0.001
ADAM_B1 = 0.9
ADAM_B2 = 0.999
ADAM_EPS = 1e-08
ADAM_WD = 0.01
ADAM_STEP = 10

VMEM_SPEC = pl.BlockSpec(memory_space=pltpu.VMEM)
HBM_SPEC = pl.BlockSpec(memory_space=pltpu.HBM)
SEM_SPEC = pl.BlockSpec(memory_space=pltpu.SEMAPHORE)
ANY_SPEC = pl.BlockSpec(memory_space=pl.ANY)
EFFECT = pltpu.SideEffectType.DATAFLOW_SIDE_EFFECTING
MIB = 1024 * 1024

VROWS = 16
ROW_NORM_MIX, ROW_FINAL_NORM, ROW_B_PG, ROW_PLE_NORM = 0, 1, 2, 3
ROW_CB_BA, ROW_BX_LAM, ROW_CW01, ROW_CW23, ROW_HG_LB, ROW_HG_NW, ROW_LOSS = 4, 5, 6, 7, 8, 9, 10


def _mm(a, b):
    return jnp.dot(a.astype(BF16), b.astype(BF16), preferred_element_type=F32)


def _mm_nt(a, b):
    return lax.dot_general(a.astype(BF16), b.astype(BF16), (((1,), (1,)), ((), ())),
                           preferred_element_type=F32)


def _mm_tn(a, b):
    return lax.dot_general(a.astype(BF16), b.astype(BF16), (((0,), (0,)), ((), ())),
                           preferred_element_type=F32)


def _mm_exact(a, b):
    bb = b.astype(BF16)
    hi = a.astype(BF16)
    r1 = a - hi.astype(F32)
    mid = r1.astype(BF16)
    lo = (r1 - mid.astype(F32)).astype(BF16)
    return sum(jnp.dot(p, bb, preferred_element_type=F32) for p in (hi, mid, lo))


def _sig_pair(x):
    e = jnp.exp(-jnp.abs(x))
    big = 1.0 / (1.0 + e)
    small = e * big
    pos = x >= 0
    return jnp.where(pos, big, small), jnp.where(pos, small, big)


def _sigmoid(x):
    return 1.0 / (1.0 + jnp.exp(-x))


def _rowsum(v):
    return jnp.sum(v, axis=0, keepdims=True)


def _shift_down(cur, prev8, d, rows):
    rolled = pltpu.roll(cur, d, 0)
    head = jnp.where(rows[0:8] < d, pltpu.roll(prev8, d, 0), rolled[0:8])
    return jnp.concatenate([head, rolled[8:]], axis=0)


def _shift_up(cur, next8, d, rows):
    n = cur.shape[0]
    rolled = pltpu.roll(cur, n - d, 0)
    tail = jnp.where(rows[0:8] >= 8 - d, pltpu.roll(next8, 8 - d, 0), rolled[n - 8:n])
    return jnp.concatenate([rolled[0:n - 8], tail], axis=0)


def _roll_in_groups(v, d):
    n, w = v.shape
    return pltpu.roll(v.reshape(n // 8, 8, w), d, 1).reshape(n, w)


def _cparams(sem, vmem_mib):
    return pltpu.CompilerParams(dimension_semantics=sem, vmem_limit_bytes=vmem_mib * MIB)


def _mesh_pos():
    x, y, c = lax.axis_index("x"), lax.axis_index("y"), lax.axis_index("c")
    chips = [(1 - x, y), (x, 1 - y), (1 - x, 1 - y)]
    return x, y, c, chips


def _remote(src, dst, ssem, rsem, dev):
    return pltpu.make_async_remote_copy(src_ref=src, dst_ref=dst, send_sem=ssem, recv_sem=rsem,
                                        device_id=dev, device_id_type=MESH)


def _gather_inproj(x, nw, w_in, w_out, w_pg, w_pp, conv_w):
    shapes = [w_in.shape, w_out.shape, w_pg.shape, w_pp.shape]
    tm = 512
    nt = T // tm

    def body(x_h, nw_h, win_h, wout_h, wpg_h, wpp_h, cw_h,
             o_in_h, o_out_h, o_pg_h, o_pp_h, o_cw_h, pf_h, pb_h, u_h,
             win, wout, wpg, wpp, cw, nw_s, xbuf, ubuf, rfbuf, rbbuf, o_in, o_out, o_pg, o_pp, o_cw,
             lsem, xsem, usem, fsem, bsem, osem, ssem, rsem):
        x, y, c, chips = _mesh_pos()
        kme = 2 * x + y
        sibling = (x, y, 1 - c)
        fetch = [pltpu.make_async_copy(src, dst, lsem.at[i]) for i, (src, dst) in enumerate(
            [(win_h, win), (cw_h, cw), (nw_h, nw_s), (wout_h, wout), (wpg_h, wpg), (wpp_h, wpp)])]
        for cp in fetch:
            cp.start()

        def cast(src, dst):
            for r0 in range(0, src.shape[0], 128):
                dst[r0:r0 + 128, :] = src[r0:r0 + 128, :].astype(BF16)

        outs = []

        def write_out(src, dst, i):
            cp = pltpu.make_async_copy(src, dst, osem.at[i])
            cp.start()
            outs.append(cp)

        fetch[0].wait()
        cast(win, o_in.at[kme])
        write_out(o_in.at[kme], o_in_h.at[kme], 0)
        fetch[1].wait()
        o_cw[kme] = cw[...]

        hrows = D // 2
        mine = pl.ds(pl.multiple_of(c * hrows, 128), hrows)
        other = pl.ds(pl.multiple_of((1 - c) * hrows, 128), hrows)

        def half(k, rows):
            return o_in.at[k, rows]

        na = (x ^ c, y ^ (1 - c), c)
        nb = (x ^ (1 - c), y ^ c, c)
        ka = 2 * na[0] + na[1]
        kb = 2 * nb[0] + nb[1]
        kd = 2 * (1 - x) + (1 - y)
        sends = [_remote(half(kme, mine), half(kme, mine), ssem.at[0], rsem.at[0], na),
                 _remote(half(kme, mine), half(kme, mine), ssem.at[1], rsem.at[1], nb)]
        for j, (px, py) in enumerate(chips):
            sends.append(_remote(o_cw.at[kme], o_cw.at[kme], ssem.at[3 + j], rsem.at[3 + j], (px, py, c)))
        for cp in sends:
            cp.start()
        fetch[2].wait()

        def rows_of(t):
            return pl.ds(pl.multiple_of(t * tm, tm), tm)

        def x_copy(t, slot):
            return pltpu.make_async_copy(x_h.at[rows_of(t)], xbuf.at[slot], xsem.at[slot])

        u_out = pltpu.make_async_copy(ubuf, u_h, usem.at[0])

        def f_store(t, slot, kh):
            return pltpu.make_async_copy(rfbuf.at[slot], pf_h.at[rows_of(t), pl.ds(pl.multiple_of(kh * DG, DG), DG)],
                                         fsem.at[slot])

        def b_store_even(t, slot, kh):
            return pltpu.make_async_copy(rbbuf.at[slot, :, 0:256],
                                         pb_h.at[rows_of(t), pl.ds(pl.multiple_of(kh * 2 * DG, 256), 256)],
                                         bsem.at[slot])

        def b_store_odd(t, slot, kh):
            return pltpu.make_async_copy(rbbuf.at[slot],
                                         pb_h.at[rows_of(t), pl.ds(pl.multiple_of(256 + kh * 2 * DG, 256), SHW)],
                                         bsem.at[slot])

        def run_pass(k, first, even):
            kh = k >> 1
            if first:
                x_copy(0, 0).start()

            def tile(t, carry):
                slot = t & 1
                if first:
                    @pl.when(t + 1 < nt)
                    def _():
                        x_copy(t + 1, 1 - slot).start()

                    x_copy(t, slot).wait()

                @pl.when(t >= 2)
                def _():
                    if even:
                        f_store(t - 2, slot, kh).wait()
                        b_store_even(t - 2, slot, kh).wait()
                    else:
                        b_store_odd(t - 2, slot, kh).wait()

                if first:
                    xv = xbuf[slot]
                    s = lax.rsqrt(jnp.mean(xv * xv, axis=-1, keepdims=True) + EPS)
                    u = (xv * s * nw_s[...]).astype(BF16)
                    ubuf[rows_of(t), :] = u
                else:
                    u = ubuf[rows_of(t), :]
                r = jnp.dot(u, o_in[k], preferred_element_type=F32)
                if even:
                    rfbuf[slot] = r[:, 0:DG]
                    rbbuf[slot, :, 0:256] = r[:, DG:SHW].astype(BF16)
                    f_store(t, slot, kh).start()
                    b_store_even(t, slot, kh).start()
                else:
                    rbbuf[slot] = r.astype(BF16)
                    b_store_odd(t, slot, kh).start()
                return carry

            lax.fori_loop(0, nt, tile, 0)
            for t in (nt - 2, nt - 1):
                slot = t & 1
                if even:
                    f_store(t, slot, kh).wait()
                    b_store_even(t, slot, kh).wait()
                else:
                    b_store_odd(t, slot, kh).wait()

        def project(k, first=False):
            @pl.when((k & 1) == 0)
            def _():
                run_pass(k, first, True)

            @pl.when((k & 1) == 1)
            def _():
                run_pass(k, first, False)

        def passed_on(k, sem, dev):
            cp = _remote(half(k, mine), half(k, mine), ssem.at[sem], rsem.at[sem], dev)
            cp.start()
            sends.append(cp)

        project(kme, first=True)
        u_out.start()
        for i, (src, dst, dst_h) in enumerate([(wout, o_out, o_out_h), (wpg, o_pg, o_pg_h), (wpp, o_pp, o_pp_h)]):
            fetch[3 + i].wait()
            cast(src, dst)
            write_out(dst, dst_h.at[kme], 4 + i)
        _remote(half(ka, mine), half(ka, mine), ssem.at[0], rsem.at[0], na).wait_recv()
        passed_on(ka, 2, nb)
        passed_on(ka, 6, sibling)
        _remote(half(kb, mine), half(kb, mine), ssem.at[1], rsem.at[1], nb).wait_recv()
        passed_on(kb, 7, sibling)
        _remote(half(ka, other), half(ka, other), ssem.at[7], rsem.at[7], sibling).wait_recv()
        write_out(o_in.at[ka], o_in_h.at[ka], 1)
        project(ka)
        _remote(half(kd, mine), half(kd, mine), ssem.at[2], rsem.at[2], nb).wait_recv()
        passed_on(kd, 8, sibling)
        _remote(half(kb, other), half(kb, other), ssem.at[6], rsem.at[6], sibling).wait_recv()
        write_out(o_in.at[kb], o_in_h.at[kb], 2)
        project(kb)
        _remote(half(kd, other), half(kd, other), ssem.at[8], rsem.at[8], sibling).wait_recv()
        write_out(o_in.at[kd], o_in_h.at[kd], 3)
        project(kd)
        u_out.wait()
        for j, (px, py) in enumerate(chips):
            kj = 2 * px + py
            _remote(o_cw.at[kj], o_cw.at[kj], ssem.at[3 + j], rsem.at[3 + j], (px, py, c)).wait_recv()
        write_out(o_cw, o_cw_h, 7)
        for cp in outs:
            cp.wait()
        for cp in sends:
            cp.wait_send()

    out_shape = [pltpu.HBM((NSHARD,) + s, BF16) for s in shapes]
    out_shape.append(pltpu.HBM((NSHARD,) + conv_w.shape, F32))
    out_shape += [pltpu.HBM((T, 2 * DG), F32), pltpu.HBM((T, 4 * DG), BF16), pltpu.HBM((T, D), BF16)]
    hbm_args = [pltpu.with_memory_space_constraint(a, pltpu.HBM) for a in (x, nw, w_in, w_out, w_pg, w_pp, conv_w)]
    return pl.pallas_call(
        body, name="gather_inproj", out_shape=out_shape,
        in_specs=[HBM_SPEC] * 7, out_specs=[HBM_SPEC] * 8,
        scratch_shapes=[pltpu.VMEM(a.shape, F32) for a in (w_in, w_out, w_pg, w_pp, conv_w, nw)]
        + [pltpu.VMEM((2, tm, D), F32), pltpu.VMEM((T, D), BF16),
           pltpu.VMEM((2, tm, DG), F32), pltpu.VMEM((2, tm, SHW), BF16),
           pltpu.VMEM((NSHARD,) + w_in.shape, BF16), pltpu.VMEM(w_out.shape, BF16), pltpu.VMEM(w_pg.shape, BF16),
           pltpu.VMEM(w_pp.shape, BF16), pltpu.VMEM((NSHARD,) + conv_w.shape, F32),
           pltpu.SemaphoreType.DMA((6,)), pltpu.SemaphoreType.DMA((2,)), pltpu.SemaphoreType.DMA((2,)),
           pltpu.SemaphoreType.DMA((2,)), pltpu.SemaphoreType.DMA((2,)), pltpu.SemaphoreType.DMA((8,)),
           pltpu.SemaphoreType.DMA((9,)), pltpu.SemaphoreType.DMA((9,))],
        compiler_params=pltpu.CompilerParams(vmem_limit_bytes=48 * MIB),
    )(*hbm_args)


def _gather_rest_start(lands):
    n = len(lands)

    def body(*refs):
        land_in = refs[0:n]
        ssem, rsem = refs[n], refs[n + 1]
        token = refs[2 * n + 2]
        x, y, c, chips = _mesh_pos()
        kme = 2 * x + y
        for p, land in enumerate(land_in):
            hrows = land.shape[1] // 2
            mine = pl.ds(pl.multiple_of(c * hrows, 128), hrows)
            for px, py in chips:
                for pc in range(2):
                    _remote(land.at[kme, mine], land.at[kme, mine], ssem.at[p], rsem.at[p], (px, py, pc)).start()
        token[...] = jnp.zeros_like(token)

    out_shape = ([pltpu.SemaphoreType.DMA((n,)), pltpu.SemaphoreType.DMA((n,))]
                 + [pltpu.HBM(a.shape, a.dtype) for a in lands] + [jax.ShapeDtypeStruct((8, 128), F32)])
    outs = pl.pallas_call(
        body, name="gather_rest_start", out_shape=out_shape,
        in_specs=[HBM_SPEC] * n, out_specs=[SEM_SPEC, SEM_SPEC] + [HBM_SPEC] * n + [VMEM_SPEC],
        input_output_aliases={i: 2 + i for i in range(n)},
        compiler_params=pltpu.CompilerParams(has_side_effects=EFFECT),
    )(*[pltpu.with_memory_space_constraint(a, pltpu.HBM) for a in lands])
    return outs[0], outs[1], list(outs[2:2 + n]), outs[2 + n]


def _gather_rest_wait(ssem, rsem, lands, after):
    n = len(lands)

    def body(*refs):
        land_in = refs[0:n]
        ssem_ref, rsem_ref = refs[n], refs[n + 1]
        x, y, c = lax.axis_index("x"), lax.axis_index("y"), lax.axis_index("c")
        for p, land in enumerate(land_in):
            three = land.at[pl.ds(0, 3)]
            cp = _remote(three, three, ssem_ref.at[p], rsem_ref.at[p], (x, y, c))
            cp.wait_send()
            cp.wait_recv()

    outs = pl.pallas_call(
        body, name="gather_rest_wait", out_shape=[pltpu.HBM(a.shape, a.dtype) for a in lands],
        in_specs=[HBM_SPEC] * n + [SEM_SPEC, SEM_SPEC, ANY_SPEC], out_specs=[HBM_SPEC] * n,
        input_output_aliases={i: i for i in range(n)},
        compiler_params=pltpu.CompilerParams(has_side_effects=EFFECT),
    )(*lands, ssem, rsem, after)
    return list(outs)


def _rs_start(name, parts, ks, lands, chip_sums=False):
    n = len(parts)

    def body(*refs):
        part_in, land_in = refs[0:n], refs[n:2 * n]
        ssem, rsem = refs[2 * n], refs[2 * n + 1]
        token = refs[4 * n + 2]
        x, y, c = lax.axis_index("x"), lax.axis_index("y"), lax.axis_index("c")
        kme = 2 * x + y
        me = 4 * x + 2 * y + c
        for p in range(n):
            hrows = land_in[p].shape[1]
            for i, k in enumerate(ks):
                if chip_sums:
                    @pl.when(kme != k)
                    def _():
                        _remote(part_in[p].at[i], land_in[p].at[kme], ssem.at[p], rsem.at[p], (k // 2, k % 2, c)).start()
                    continue
                for pc in range(2):
                    @pl.when(jnp.logical_or(kme != k, c != pc))
                    def _():
                        _remote(part_in[p].at[i, pl.ds(pc * hrows, hrows)], land_in[p].at[me],
                                ssem.at[p], rsem.at[p], (k // 2, k % 2, pc)).start()
        token[...] = jnp.zeros_like(token)

    arrays = list(parts) + list(lands)
    out_shape = ([pltpu.SemaphoreType.DMA((n,)), pltpu.SemaphoreType.DMA((n,))]
                 + [pltpu.HBM(a.shape, a.dtype) for a in arrays] + [jax.ShapeDtypeStruct((8, 128), F32)])
    outs = pl.pallas_call(
        body, name=name, out_shape=out_shape,
        in_specs=[HBM_SPEC] * (2 * n), out_specs=[SEM_SPEC, SEM_SPEC] + [HBM_SPEC] * (2 * n) + [VMEM_SPEC],
        input_output_aliases={i: 2 + i for i in range(2 * n)},
        compiler_params=pltpu.CompilerParams(has_side_effects=EFFECT),
    )(*[pltpu.with_memory_space_constraint(a, pltpu.HBM) for a in arrays])
    return outs[0], outs[1], list(outs[2:2 + n]), list(outs[2 + n:2 + 2 * n]), outs[2 + 2 * n]


def _rs_wait(name, ssem, rsem, parts, ks, lands, after, chip_sums=False):
    n = len(parts)

    def body(*refs):
        part_in, land_in = refs[0:n], refs[n:2 * n]
        ssem_ref, rsem_ref = refs[2 * n], refs[2 * n + 1]
        x, y, c = lax.axis_index("x"), lax.axis_index("y"), lax.axis_index("c")
        kme = 2 * x + y
        for p in range(n):
            piece = land_in[p].at[0]
            for k in ks:
                for pc in range(1 if chip_sums else 2):
                    mine = (kme == k) if chip_sums else jnp.logical_and(kme == k, c == pc)

                    @pl.when(jnp.logical_not(mine))
                    def _():
                        _remote(piece, piece, ssem_ref.at[p], rsem_ref.at[p], (x, y, c)).wait_send()
            owner = kme == ks[0]
            for k in ks[1:]:
                owner = jnp.logical_or(owner, kme == k)

            @pl.when(owner)
            def _():
                others = land_in[p].at[pl.ds(0, land_in[p].shape[0] - 1)]
                _remote(others, others, ssem_ref.at[p], rsem_ref.at[p], (x, y, c)).wait_recv()

    arrays = list(parts) + list(lands)
    outs = pl.pallas_call(
        body, name=name, out_shape=[pltpu.HBM(a.shape, a.dtype) for a in arrays],
        in_specs=[HBM_SPEC] * (2 * n) + [SEM_SPEC, SEM_SPEC, ANY_SPEC], out_specs=[HBM_SPEC] * (2 * n),
        input_output_aliases={i: i for i in range(2 * n)},
        compiler_params=pltpu.CompilerParams(has_side_effects=EFFECT),
    )(*arrays, ssem, rsem, after)
    return list(outs[0:n]), list(outs[n:2 * n])


def _final_reduce(parts, lands, sm_in, sm_tail, sm_a, sm_b, rg_c):
    shapes = [(2 * l.shape[1], l.shape[2]) for l in lands]
    step = 128
    PR = DG // NDEV

    def body(pin, pout, ppg, ppp, l_in, l_out, l_pg, l_pp, in_h, tail_h, a_h, b_h, rg_h,
             g_in_h, g_out_h, g_pg_h, g_pp_h, v_out, rg_out, vbuf, vrecv, rgrecv,
             lb_in, lb_out, lb_pg, lb_pp, ob_in, ob_out, ob_pg, ob_pp,
             in_ref, tail_ref, a_ref, b_ref, rg_ref, g_in, g_out, g_pg, g_pp,
             lsem, ssem_l, osem, bsem_s, bsem_r, ssem, rsem):
        x, y, c = lax.axis_index("x"), lax.axis_index("y"), lax.axis_index("c")
        kme = 2 * x + y
        me = 4 * x + 2 * y + c
        sibling = (x, y, 1 - c)
        small_fetch = [pltpu.make_async_copy(src, dst, ssem_l.at[i]) for i, (src, dst) in enumerate(
            [(in_h, in_ref), (tail_h, tail_ref), (a_h, a_ref), (b_h, b_ref), (rg_h, rg_ref)])]
        for cp in small_fetch:
            cp.start()

        lands_hbm = [l_in, l_out, l_pg, l_pp]
        land_bufs = [lb_in, lb_out, lb_pg, lb_pp]
        own_bufs = [ob_in, ob_out, ob_pg, ob_pp]
        fetches = []
        for p in range(4):
            cp = pltpu.make_async_copy(lands_hbm[p], land_bufs[p], lsem.at[p])
            cp.start()
            fetches.append(cp)

        own_fetches = [pltpu.make_async_copy(pin.at[kme], ob_in, lsem.at[4])]
        own_fetches[0].start()
        for p, part in enumerate([pout, ppg, ppp]):
            hrows = own_bufs[p + 1].shape[0]
            cp = pltpu.make_async_copy(part.at[kme, pl.ds(pl.multiple_of(c * hrows, step), hrows)],
                                       own_bufs[p + 1], lsem.at[5 + p])
            cp.start()
            own_fetches.append(cp)

        def pair(ref, r0, r1):
            return jnp.concatenate([ref[r0:r0 + 1, :], ref[r1:r1 + 1, :]], axis=1)

        for cp in small_fetch:
            cp.wait()
        rows = {
            ROW_NORM_MIX: in_ref[0:1, :], ROW_FINAL_NORM: tail_ref[1:2, :], ROW_B_PG: tail_ref[2:3, :],
            ROW_PLE_NORM: tail_ref[3:4, :], ROW_CB_BA: pair(a_ref, 0, 1), ROW_BX_LAM: pair(a_ref, 2, 3),
            ROW_CW01: pair(a_ref, 4, 5), ROW_CW23: pair(a_ref, 6, 7), ROW_HG_LB: pair(b_ref, 2, 3),
            ROW_HG_NW: jnp.concatenate([b_ref[1:2, :], jnp.zeros((1, DG), F32)], axis=1),
            ROW_LOSS: tail_ref[4:5, :],
        }
        vbuf[...] = jnp.zeros_like(vbuf)
        for r, row in rows.items():
            for j in range(NDEV):
                vbuf[j, r:r + 1, :] = row[:, j * 128:(j + 1) * 128]

        def peer(mask):
            px = x ^ ((mask >> 2) & 1)
            py = y ^ ((mask >> 1) & 1)
            pc = c ^ (mask & 1)
            return (px, py, pc), 4 * px + 2 * py + pc

        def rg_rows(r):
            return pl.ds(pl.multiple_of(r * PR, PR), PR)

        first = []
        for mask in range(1, NDEV):
            dev, r = peer(mask)
            i = mask - 1
            cp = _remote(vbuf.at[r], vrecv.at[i], ssem.at[i], rsem.at[i], dev)
            cp.start()
            first.append(cp)
            cp = _remote(rg_ref.at[rg_rows(r)], rgrecv.at[i], ssem.at[7 + i], rsem.at[7 + i], dev)
            cp.start()
            first.append(cp)

        big = [(lb_in, g_in), (lb_out, g_out), (lb_pg, g_pg), (lb_pp, g_pp)]
        g_hbm = [g_in_h, g_out_h, g_pg_h, g_pp_h]
        swaps, writes = [], []
        for p, (land, gout) in enumerate(big):
            fetches[p].wait()
            own_fetches[p].wait()
            hrows = land.shape[1]
            mine0 = pl.multiple_of(c * hrows, step)
            slot = kme if p == 0 else me
            for r0 in range(0, hrows, step):
                rs = pl.ds(mine0 + r0, step)
                own = own_bufs[p][r0:r0 + step, :]
                s = jnp.zeros((step, land.shape[2]), F32)
                for j in range(land.shape[0]):
                    s = s + jnp.where(slot == j, own, land[j, r0:r0 + step, :]).astype(F32)
                gout[rs, :] = s
            mine = pl.ds(mine0, hrows)
            cp = _remote(gout.at[mine], gout.at[mine], bsem_s.at[p], bsem_r.at[p], sibling)
            cp.start()
            swaps.append(cp)
            cp = pltpu.make_async_copy(gout.at[mine], g_hbm[p].at[mine], osem.at[p])
            cp.start()
            writes.append(cp)

        sv = vbuf[me]
        sr = rg_ref[rg_rows(me), :]
        for i in range(NDEV - 1):
            first[2 * i].wait_recv()
            first[2 * i + 1].wait_recv()
            sv = sv + vrecv[i]
            sr = sr + rgrecv[i]
        v_out[me] = sv
        rg_out[rg_rows(me), :] = sr
        second = []
        for mask in range(1, NDEV):
            dev, r = peer(mask)
            i = mask - 1
            cp = _remote(v_out.at[me], v_out.at[me], ssem.at[14 + i], rsem.at[14 + i], dev)
            cp.start()
            second.append(cp)
            cp = _remote(rg_out.at[rg_rows(me)], rg_out.at[rg_rows(me)], ssem.at[21 + i], rsem.at[21 + i], dev)
            cp.start()
            second.append(cp)
        for p, (land, gout) in enumerate(big):
            hrows = land.shape[1]
            other = pl.ds(pl.multiple_of((1 - c) * hrows, step), hrows)
            _remote(gout.at[other], gout.at[other], bsem_s.at[p], bsem_r.at[p], sibling).wait_recv()
            cp = pltpu.make_async_copy(gout.at[other], g_hbm[p].at[other], osem.at[4 + p])
            cp.start()
            writes.append(cp)
        for mask in range(1, NDEV):
            dev, r = peer(mask)
            i = mask - 1
            _remote(v_out.at[r], v_out.at[r], ssem.at[14 + i], rsem.at[14 + i], dev).wait_recv()
            _remote(rg_out.at[rg_rows(r)], rg_out.at[rg_rows(r)], ssem.at[21 + i], rsem.at[21 + i], dev).wait_recv()
        for cp in writes:
            cp.wait()
        for cp in first + swaps + second:
            cp.wait_send()

    out_shape = [pltpu.HBM(s, F32) for s in shapes]
    out_shape += [jax.ShapeDtypeStruct((NDEV, VROWS, 128), F32), jax.ShapeDtypeStruct((DG, 128), F32)]
    outs = pl.pallas_call(
        body, name="final_reduce", out_shape=out_shape,
        in_specs=[HBM_SPEC] * 13, out_specs=[HBM_SPEC] * 4 + [VMEM_SPEC] * 2,
        scratch_shapes=[pltpu.VMEM((NDEV, VROWS, 128), F32), pltpu.VMEM((NDEV - 1, VROWS, 128), F32),
                        pltpu.VMEM((NDEV - 1, PR, 128), F32)]
        + [pltpu.VMEM(l.shape, BF16) for l in lands]
        + [pltpu.VMEM(l.shape[1:], BF16) for l in lands]
        + [pltpu.VMEM(a.shape, F32) for a in (sm_in, sm_tail, sm_a, sm_b, rg_c)]
        + [pltpu.VMEM(s, F32) for s in shapes]
        + [pltpu.SemaphoreType.DMA((8,)), pltpu.SemaphoreType.DMA((5,)), pltpu.SemaphoreType.DMA((8,)),
                        pltpu.SemaphoreType.DMA((4,)), pltpu.SemaphoreType.DMA((4,)),
                        pltpu.SemaphoreType.DMA((28,)), pltpu.SemaphoreType.DMA((28,))],
        compiler_params=pltpu.CompilerParams(vmem_limit_bytes=48 * MIB),
    )(*[pltpu.with_memory_space_constraint(a, pltpu.HBM)
        for a in (*parts, *lands, sm_in, sm_tail, sm_a, sm_b, rg_c)])
    return list(outs[0:4]), outs[4], outs[5]


def _adam_rows(w, g, m, v):
    m2 = ADAM_B1 * m + (1.0 - ADAM_B1) * g
    v2 = ADAM_B2 * v + (1.0 - ADAM_B2) * (g * g)
    m_hat = m2 / (1.0 - ADAM_B1 ** ADAM_STEP)
    v_hat = v2 / (1.0 - ADAM_B2 ** ADAM_STEP)
    delta = -ADAM_LR * (m_hat / (jnp.sqrt(v_hat) + ADAM_EPS) + ADAM_WD * w)
    return delta, m2, v2


def _adam_big(gs, ws, ms, vs):
    n = len(gs)
    steps = 4

    def body(*refs):
        ins, outs = refs[:4 * n], refs[4 * n:]
        for i in range(n):
            g, w, m, v = (r[...] for r in ins[4 * i:4 * i + 4])
            d, m2, v2 = _adam_rows(w, g, m, v)
            outs[3 * i][...] = d
            outs[3 * i + 1][...] = m2
            outs[3 * i + 2][...] = v2

    in_specs, out_specs, out_shape, args = [], [], [], []
    for g, w, m, v in zip(gs, ws, ms, vs):
        r, c = w.shape
        spec = lambda: pl.BlockSpec((r // steps, c), lambda i: (i, 0))
        in_specs += [spec() for _ in range(4)]
        out_specs += [spec() for _ in range(3)]
        out_shape += [jax.ShapeDtypeStruct((r, c), F32)] * 3
        args += [pltpu.with_memory_space_constraint(a, pltpu.HBM) for a in (g, w, m, v)]
    outs = pl.pallas_call(
        body, name="adam_big", grid=(steps,), in_specs=in_specs, out_specs=out_specs, out_shape=out_shape,
        compiler_params=_cparams(("parallel",), 32),
    )(*args)
    return [tuple(outs[3 * i:3 * i + 3]) for i in range(n)]


_VEC_PARAMS = [
    ("norm_mix_w", ROW_NORM_MIX, 0, D), ("final_norm_w", ROW_FINAL_NORM, 0, D),
    ("b_ple_gate", ROW_B_PG, 0, D), ("ple_norm_w", ROW_PLE_NORM, 0, D),
    ("conv_b", ROW_CB_BA, 0, DG), ("rg_ba", ROW_CB_BA, DG, DG),
    ("rg_bx", ROW_BX_LAM, 0, DG), ("rg_lambda", ROW_BX_LAM, DG, DG),
    ("hg_norm_w", ROW_HG_NW, 0, HD),
]
_SMALL_ORDER = [n for n, _, _, _ in _VEC_PARAMS] + ["hg_lb", "conv_w", "rg_wa", "rg_wx"]


def _adam_small(vred, rgred, ws, ms, vs):
    names = _SMALL_ORDER
    n = len(names)

    def body(vred_ref, rg_ref, *refs):
        w_refs = dict(zip(names, refs[0:n]))
        m_refs = dict(zip(names, refs[n:2 * n]))
        v_refs = dict(zip(names, refs[2 * n:3 * n]))
        outs = refs[3 * n:]
        o_refs = {nm: outs[4 * i:4 * i + 4] for i, nm in enumerate(names)}
        kme = 2 * lax.axis_index("x") + lax.axis_index("y")

        def update(nm, g, idx):
            d, m2, v2 = _adam_rows(w_refs[nm][idx], g, m_refs[nm][idx], v_refs[nm][idx])
            og, od, om, ov = o_refs[nm]
            og[idx] = g
            od[idx] = d
            om[idx] = m2
            ov[idx] = v2

        def packed(row, lane0, width):
            return jnp.concatenate([vred_ref[j, row:row + 1, :] for j in range(lane0 // 128, (lane0 + width) // 128)],
                                   axis=1)

        everything = (slice(None), slice(None))
        for nm, row, lane0, width in _VEC_PARAMS:
            update(nm, packed(row, lane0, width), everything)
        for r in range(2):
            update("hg_lb", packed(ROW_HG_LB, r * DG, DG), (slice(r, r + 1), slice(None)))
        for j in range(4):
            g = vred_ref[(j % 2) * 4 + kme, ROW_CW01 + j // 2:ROW_CW01 + j // 2 + 1, :]
            update("conv_w", g, (slice(j, j + 1), slice(None)))
        for r0 in range(0, DG, 128):
            rs = (slice(r0, r0 + 128), slice(None))
            both = rg_ref[r0:r0 + 128, :]
            update("rg_wa", both[:, 0:RGB], rs)
            update("rg_wx", pltpu.roll(both, RGB, 1)[:, 0:RGB], rs)

    args = [vred, rgred] + [d[nm] for d in (ws, ms, vs) for nm in names]
    out_shape = []
    for nm in names:
        out_shape += [jax.ShapeDtypeStruct(ws[nm].shape, F32)] * 4
    whole = lambda s: pl.BlockSpec(s.shape, lambda i, nd=len(s.shape): (0,) * nd)
    outs = pl.pallas_call(
        body, name="adam_small", out_shape=out_shape, grid=(1,),
        in_specs=[whole(a) for a in args], out_specs=[whole(s) for s in out_shape],
    )(*args)
    return {nm: tuple(outs[4 * i:4 * i + 4]) for i, nm in enumerate(names)}


def _conv_rows(cw_ref):
    return [jnp.concatenate([cw_ref[k, j:j + 1, :] for k in range(NSHARD)], axis=1) for j in range(4)]


def _rg_conv(xa, prev8, cw, cb, rows):
    taps = [_shift_down(xa, prev8, 3, rows), _shift_down(xa, prev8, 2, rows),
            _shift_down(xa, prev8, 1, rows), xa]
    xc = cb
    for j in range(4):
        xc = xc + taps[j] * cw[j]
    return xc, taps


def _block_mask():
    r = lax.broadcasted_iota(I32, (DG, DG), 0)
    c = lax.broadcasted_iota(I32, (DG, DG), 1)
    return (r >> 6) == (c >> 6)


def _dense_from_blocks(wc):
    j = lax.broadcasted_iota(I32, (RGB, DG), 0)
    c = lax.broadcasted_iota(I32, (RGB, DG), 1)
    spread = _mm_exact(wc, ((c & (RGB - 1)) == j).astype(F32))
    return jnp.where(_block_mask(), spread, 0.0)


def _blocks_from_dense(da, dx):
    c = lax.broadcasted_iota(I32, (DG, 128), 0)
    j = lax.broadcasted_iota(I32, (DG, 128), 1)
    hit = (c & (RGB - 1)) == (j & (RGB - 1))
    mask = _block_mask()
    return (_mm_exact(jnp.where(mask, da, 0.0), (hit & (j < RGB)).astype(F32))
            + _mm_exact(jnp.where(mask, dx, 0.0), (hit & (j >= RGB)).astype(F32)))


def _rg_gates(xc, wa, ba, wx, bx, sp, first_row):
    r = _sigmoid(_mm(xc, wa) + ba)
    i = _sigmoid(_mm(xc, wx) + bx)
    log_a = (-RG_C) * r * sp
    a = jnp.exp(log_a)
    a2 = a * a
    one_m_a2 = -jnp.tanh(log_a) * (a2 + 1.0)
    mult = jnp.where(first_row, 1.0, jnp.sqrt(one_m_a2))
    return r, i, a, a2, mult


def _softplus(z):
    return jnp.maximum(z, 0.0) + jnp.log1p(jnp.exp(-jnp.abs(z)))


def _fwd_rglru(pf, pb, cw3, conv_b, wa_c, ba, wx_c, bx, lam, dep):
    tm = 512
    ng = tm // 8

    def body(xa_ref, ga_ref, cw_ref, cb_ref, wa_ref, ba_ref, wx_ref, bx_ref, lam_ref, dep_ref,
             h_ref, ya_ref, a_s, u_s, tail_s, hc_s, wa_s, wx_s):
        i = pl.program_id(0)

        @pl.when(i == 0)
        def _():
            tail_s[...] = jnp.zeros_like(tail_s)
            hc_s[...] = jnp.zeros_like(hc_s)
            wa_s[...] = _dense_from_blocks(wa_ref[...]).astype(BF16)
            wx_s[...] = _dense_from_blocks(wx_ref[...]).astype(BF16)

        rows = lax.broadcasted_iota(I32, (tm, DG), 0)
        xa = xa_ref[...]
        xc, _ = _rg_conv(xa, tail_s[...], _conv_rows(cw_ref), cb_ref[...], rows)
        tail_s[...] = xa[tm - 8:tm, :]
        sp = _softplus(-lam_ref[...])
        rp = _mm(xc, wa_s[...]) + ba_ref[...]
        ip = _mm(xc, wx_s[...]) + bx_ref[...]
        rb = 64
        rows_b = lax.broadcasted_iota(I32, (rb, DG), 0)
        rows8 = rows_b & 7
        carry = hc_s[0:1, :]
        for b0 in range(0, tm, rb):
            sl = slice(b0, b0 + rb)
            r = _sigmoid(rp[sl])
            ig = _sigmoid(ip[sl])
            log_a = (-RG_C) * r * sp
            av = jnp.exp(log_a)
            mult = jnp.sqrt(-jnp.tanh(log_a) * (av * av + 1.0))
            if b0 == 0:
                mult = jnp.where((rows_b + i * tm) == 0, 1.0, mult)
            uv = mult * (ig * xc[sl])
            for d in (1, 2, 4):
                keep = rows8 >= d
                uv = uv + av * jnp.where(keep, _roll_in_groups(uv, d), 0.0)
                av = av * jnp.where(keep, _roll_in_groups(av, d), 1.0)
            ga = ga_ref[sl, :].astype(F32)
            gate = ga * _sigmoid(ga)
            hs = []
            for g in range(rb // 8):
                gs = slice(g * 8, (g + 1) * 8)
                hv = uv[gs] + av[gs] * carry
                carry = hv[7:8, :]
                hs.append(hv)
            hb = jnp.concatenate(hs, axis=0)
            h_ref[sl, :] = hb
            ya_ref[sl, :] = (hb * gate).astype(BF16)
        hc_s[0:1, :] = carry

    vec = lambda: pl.BlockSpec((1, DG), lambda i: (0, 0))
    blocks = lambda: pl.BlockSpec((DG, RGB), lambda i: (0, 0))
    return pl.pallas_call(
        body, name="fwd_rglru", grid=(T // tm,),
        in_specs=[pl.BlockSpec((tm, DG), lambda i: (i, 0)),
                  pl.BlockSpec((tm, DG), lambda i: (i, 0)),
                  pl.BlockSpec((NSHARD, 4, 128), lambda i: (0, 0, 0)), vec(),
                  blocks(), vec(), blocks(), vec(), vec(), ANY_SPEC],
        out_specs=[pl.BlockSpec((tm, DG), lambda i: (i, 0)),
                   pl.BlockSpec((tm, DG), lambda i: (i, 0))],
        out_shape=[jax.ShapeDtypeStruct((T, DG), F32), pltpu.HBM((T, DG), BF16)],
        scratch_shapes=[pltpu.VMEM((tm, DG), F32), pltpu.VMEM((tm, DG), F32),
                        pltpu.VMEM((8, DG), F32), pltpu.VMEM((8, DG), F32),
                        pltpu.VMEM((DG, DG), BF16), pltpu.VMEM((DG, DG), BF16)],
        compiler_params=_cparams(("arbitrary",), 48),
    )(pf, pb, cw3, conv_b, wa_c, ba, wx_c, bx, lam, dep)


def _hg_lower_bound(lb_ref):
    return _sig_pair(lb_ref[0:1, :] - lb_ref[1:2, :])


def _hg_gates(fz, lb, one_m_lb):
    sg, sn = _sig_pair(fz)
    f = lb + one_m_lb * sg
    return sg, sn, f, jnp.log(f), one_m_lb * sn


def _tri(lower):
    r = lax.broadcasted_iota(I32, (CH, CH), 0)
    c = lax.broadcasted_iota(I32, (CH, CH), 1)
    return (r >= c) if lower else (r <= c)


def _split3(v):
    hi = v.astype(BF16)
    r1 = v - hi.astype(F32)
    mid = r1.astype(BF16)
    lo = (r1 - mid.astype(F32)).astype(BF16)
    return hi, mid, lo


def _chunk_cumsum(v, rows64, reverse=False):
    del rows64
    tri = _tri(not reverse).astype(BF16)
    out = []
    for c in range(v.shape[0] // CH):
        pieces = _split3(v[c * CH:(c + 1) * CH])
        out.append(sum(jnp.dot(tri, p, preferred_element_type=F32) for p in pieces))
    return jnp.concatenate(out, axis=0)


def _chunk_rev_cumsum(v, rows64):
    return _chunk_cumsum(v, rows64, reverse=True)


def _hg_recompute(q_ref, f_ref, lb, one_m_lb, rows64, eb_s, enb_s, ekd_s, qe_s, ke_s, kd_s, dec_s):
    nc = q_ref.shape[0] // CH
    sg, sn, f, logf, k = _hg_gates(f_ref[...], lb, one_m_lb)
    q = q_ref[...].astype(F32)
    sq = _sigmoid(q)
    qs = q * sq * (HD ** -0.5)
    b = _chunk_cumsum(logf, rows64)
    for c in range(nc):
        rs = slice(c * CH, (c + 1) * CH)
        b_c = b[rs]
        bl = b_c[CH - 1:CH, :]
        eb, enb, ekd = jnp.exp(b_c), jnp.exp(-b_c), jnp.exp(bl - b_c)
        if eb_s is not None:
            eb_s[rs, :] = eb
            enb_s[rs, :] = enb
            ekd_s[rs, :] = ekd
        qe_s[rs, :] = (qs[rs] * eb).astype(BF16)
        ke_s[rs, :] = (k[rs] * enb).astype(BF16)
        kd_s[rs, :] = (k[rs] * ekd).astype(BF16)
        dec_s[c:c + 1, :] = jnp.exp(bl)
    return sg, sn, f, k, q, sq, qs


def _fwd_hgrn2(pf, pb, hg_lb, hg_nw):
    tm = 512
    nc = tm // CH

    def body(q_ref, f_ref, v_ref, g_ref, lb_ref, nw_ref, yb_ref, o_ref, sp_ref,
             st_s, qe_s, ke_s, kd_s, vb_s, dec_s, p_s, ds_s):
        i = pl.program_id(0)

        @pl.when(i == 0)
        def _():
            st_s[...] = jnp.zeros_like(st_s)

        lb, one_m_lb = _hg_lower_bound(lb_ref)
        rows64 = lax.broadcasted_iota(I32, (tm, DG), 0) & (CH - 1)
        _hg_recompute(q_ref, f_ref, lb, one_m_lb, rows64, None, None, None, qe_s, ke_s, kd_s, dec_s)
        vb_s[...] = v_ref[...]
        mask = _tri(True)
        items = [(c, hd, slice(c * CH, (c + 1) * CH), slice(hd * HD, (hd + 1) * HD))
                 for c in range(nc) for hd in range(NH)]
        for c, hd, rs, cols in items:
            p_s[c * NH + hd] = jnp.where(mask, _mm_nt(qe_s[rs, cols], ke_s[rs, cols]), 0.0).astype(BF16)
            ds_s[c * NH + hd] = _mm_tn(vb_s[rs, cols], kd_s[rs, cols])
        for c, hd, rs, cols in items:
            st = st_s[hd]
            sp_ref[hd, c] = st
            st_s[hd] = st * dec_s[c:c + 1, cols] + ds_s[c * NH + hd]
        for c, hd, rs, cols in items:
            o_ref[rs, cols] = _mm(p_s[c * NH + hd], vb_s[rs, cols]) + _mm_nt(qe_s[rs, cols], sp_ref[hd, c])
        nw = nw_ref[...]
        for hd in range(NH):
            cols = slice(hd * HD, (hd + 1) * HD)
            o = o_ref[:, cols]
            so = lax.rsqrt(jnp.mean(o * o, axis=-1, keepdims=True) + EPS)
            g = g_ref[:, cols].astype(F32)
            sg = _sigmoid(g)
            yb_ref[:, cols] = (o * so * nw * (g * sg)).astype(BF16)

    col = lambda j: pl.BlockSpec((tm, DG), lambda i: (i, j))
    return pl.pallas_call(
        body, name="fwd_hgrn2", grid=(T // tm,),
        in_specs=[col(1), col(1), col(2), col(3),
                  pl.BlockSpec((2, DG), lambda i: (0, 0)),
                  pl.BlockSpec((1, HD), lambda i: (0, 0))],
        out_specs=[pl.BlockSpec((tm, DG), lambda i: (i, 0)),
                   pl.BlockSpec((tm, DG), lambda i: (i, 0)),
                   pl.BlockSpec((NH, nc, HD, HD), lambda i: (0, i, 0, 0))],
        out_shape=[pltpu.HBM((T, DG), BF16), jax.ShapeDtypeStruct((T, DG), F32),
                   jax.ShapeDtypeStruct((NH, NCHUNK, HD, HD), F32)],
        scratch_shapes=[pltpu.VMEM((NH, HD, HD), F32),
                        pltpu.VMEM((tm, DG), BF16), pltpu.VMEM((tm, DG), BF16), pltpu.VMEM((tm, DG), BF16),
                        pltpu.VMEM((tm, DG), BF16), pltpu.VMEM((nc, DG), F32),
                        pltpu.VMEM((nc * NH, CH, CH), BF16), pltpu.VMEM((nc * NH, HD, HD), F32)],
        compiler_params=_cparams(("arbitrary",), 48),
    )(pb, pf, pb, pb, hg_lb, hg_nw)


def _tail_fwd_bwd(x, p, tgt, ya, yb, w_out_b, w_pg_b, w_pp_b, ple_nw, b_pg, fnw):
    tm = 512
    nt = T // tm
    QR = D // NSHARD

    def body(x_ref, p_ref, t_ref, ya_ref, yb_ref, wo_ref, wg_ref, wp_ref, pw_ref, b_ref, fw_ref,
             dh1_ref, dyab_ref, dwo_ref, dwg_ref, dwp_ref, sm_ref, dwo_s, dwg_s, dwp_s):
        i = pl.program_id(0)

        @pl.when(i == 0)
        def _():
            dwo_s[...] = jnp.zeros_like(dwo_s)
            dwg_s[...] = jnp.zeros_like(dwg_s)
            dwp_s[...] = jnp.zeros_like(dwp_s)
            sm_ref[...] = jnp.zeros_like(sm_ref)

        ya = ya_ref[...]
        yb = yb_ref[...]
        pv = p_ref[...].astype(BF16)
        pw = pw_ref[...]
        fw = fw_ref[...]
        h1 = x_ref[...] + _mm(ya, wo_ref[0:DG, :]) + _mm(yb, wo_ref[DG:D, :])
        s2 = lax.rsqrt(jnp.mean(h1 * h1, axis=-1, keepdims=True) + EPS)
        n2h = h1 * s2
        n2 = (n2h * pw).astype(BF16)
        z = _mm(n2, wg_ref[...]) + b_ref[...]
        gate = _sigmoid(z)
        pp = jnp.concatenate([_mm(pv, wp_ref[k]) for k in range(NSHARD)], axis=1)
        h2 = h1 + gate * pp
        s3 = lax.rsqrt(jnp.mean(h2 * h2, axis=-1, keepdims=True) + EPS)
        hn = h2 * s3
        err = hn * fw - t_ref[...]
        sm_ref[0:1, :] += _rowsum(err * err)
        dy = err * (1.0 / D)
        sm_ref[1:2, :] += _rowsum(dy * hn)
        g3 = dy * fw
        dh2 = s3 * (g3 - hn * jnp.mean(g3 * hn, axis=-1, keepdims=True))
        dpp = (dh2 * gate).astype(BF16)
        dz = dh2 * pp * gate * (1.0 - gate)
        sm_ref[2:3, :] += _rowsum(dz)
        dzb = dz.astype(BF16)
        dwg_s[...] += _mm_tn(n2, dzb)
        dn2 = _mm_nt(dzb, wg_ref[...])
        for k in range(NSHARD):
            dwp_s[k] += _mm_tn(pv, dpp[:, k * PLE:(k + 1) * PLE])
        sm_ref[3:4, :] += _rowsum(dn2 * n2h)
        g2 = dn2 * pw
        dh1 = dh2 + s2 * (g2 - n2h * jnp.mean(g2 * n2h, axis=-1, keepdims=True))
        dh1_ref[...] = dh1
        dh1b = dh1.astype(BF16)
        dyab_ref[...] = _mm_nt(dh1b, wo_ref[...])
        dwo_s[0:DG, :] += _mm_tn(ya, dh1b)
        dwo_s[DG:D, :] += _mm_tn(yb, dh1b)

        @pl.when(i == nt - 1)
        def _():
            total = jnp.sum(sm_ref[0:1, :], axis=-1, keepdims=True) * (0.5 / D)
            sm_ref[4:5, :] = jnp.broadcast_to(total, (1, D))
            for k in range(NSHARD):
                dwo_ref[k] = dwo_s[k * QR:(k + 1) * QR, :].astype(BF16)
                dwg_ref[k] = dwg_s[k * QR:(k + 1) * QR, :].astype(BF16)
                dwp_ref[k] = dwp_s[k].astype(BF16)

    row = lambda w: pl.BlockSpec((tm, w), lambda i: (i, 0))
    const2 = lambda s: pl.BlockSpec(s, lambda i: (0, 0), pipeline_mode=pl.Buffered(1))
    const3 = lambda s: pl.BlockSpec(s, lambda i: (0, 0, 0), pipeline_mode=pl.Buffered(1))
    return pl.pallas_call(
        body, name="tail_fwd_bwd", grid=(nt,),
        in_specs=[row(D), row(PLE), row(D), row(DG), row(DG),
                  const2((D, D)), const2((D, D)), const3((NSHARD, PLE, PLE)),
                  const2((1, D)), const2((1, D)), const2((1, D))],
        out_specs=[row(D), row(D), const3((NSHARD, QR, D)), const3((NSHARD, QR, D)),
                   const3((NSHARD, PLE, PLE)), const2((8, D))],
        out_shape=[jax.ShapeDtypeStruct((T, D), F32), jax.ShapeDtypeStruct((T, D), F32),
                   jax.ShapeDtypeStruct((NSHARD, QR, D), BF16), jax.ShapeDtypeStruct((NSHARD, QR, D), BF16),
                   jax.ShapeDtypeStruct((NSHARD, PLE, PLE), BF16), jax.ShapeDtypeStruct((8, D), F32)],
        scratch_shapes=[pltpu.VMEM((D, D), F32), pltpu.VMEM((D, D), F32), pltpu.VMEM((NSHARD, PLE, PLE), F32)],
        compiler_params=_cparams(("arbitrary",), 62),
    )(x, p, tgt, pltpu.with_memory_space_constraint(ya, pltpu.HBM), pltpu.with_memory_space_constraint(yb, pltpu.HBM),
      w_out_b, w_pg_b, w_pp_b, ple_nw, b_pg, fnw)


def _bwd_rglru(pf, pb, h, dyab, cw3, conv_b, wa_c, ba, wx_c, bx, lam, dep):
    tm = 512
    nt = T // tm
    ng = tm // 8

    def body(xa_ref, ga_ref, xp_ref, h_ref, hp_ref, dya_ref, cw_ref, cb_ref, wa_ref, ba_ref, wx_ref, bx_ref,
             lam_ref, dep_ref, da_ref, rg_ref, sm_ref, a_s, g_s, cg_s, nxt_s, wa_s, wx_s, dwa_s, dwx_s):
        i = pl.program_id(0)
        tile = nt - 1 - i

        @pl.when(i == 0)
        def _():
            dwa_s[...] = jnp.zeros_like(dwa_s)
            dwx_s[...] = jnp.zeros_like(dwx_s)
            sm_ref[...] = jnp.zeros_like(sm_ref)
            cg_s[...] = jnp.zeros_like(cg_s)
            nxt_s[...] = jnp.zeros_like(nxt_s)
            wa_s[...] = _dense_from_blocks(wa_ref[...]).astype(BF16)
            wx_s[...] = _dense_from_blocks(wx_ref[...]).astype(BF16)

        rows = lax.broadcasted_iota(I32, (tm, DG), 0)
        has_prev = tile > 0
        xa = xa_ref[...]
        xprev = jnp.where(has_prev, xp_ref[...], 0.0)
        cw = _conv_rows(cw_ref)
        xc, taps = _rg_conv(xa, xprev, cw, cb_ref[...], rows)
        lam_v = lam_ref[...]
        sp = _softplus(-lam_v)
        first_row = (rows + tile * tm) == 0
        r, ig, a, a2, mult = _rg_gates(xc, wa_s[...], ba_ref[...], wx_s[...], bx_ref[...], sp, first_row)
        hv = h_ref[...]
        hprev = jnp.where(has_prev, hp_ref[...], 0.0)
        h_m1 = _shift_down(hv, hprev, 1, rows)
        ga = ga_ref[...].astype(F32)
        sg = _sigmoid(ga)
        dya = dya_ref[...]
        dga = dya * hv * (sg * (1.0 + ga * (1.0 - sg)))

        av = jnp.where(rows == tm - 1, 1.0, pltpu.roll(a, tm - 1, 0))
        gv = dya * (ga * sg)
        rows8 = rows & 7
        for d in (1, 2, 4):
            keep = rows8 < 8 - d
            gv = gv + av * jnp.where(keep, _roll_in_groups(gv, 8 - d), 0.0)
            av = av * jnp.where(keep, _roll_in_groups(av, 8 - d), 1.0)
        a_s[...] = av
        g_s[...] = gv
        carry = cg_s[0:1, :]
        for g in range(ng - 1, -1, -1):
            sl = slice(g * 8, (g + 1) * 8)
            ab, gb = a_s[sl, :], g_s[sl, :]
            g_s[sl, :] = gb + ab * carry
            carry = gb[0:1, :] + ab[0:1, :] * carry
        cg_s[0:1, :] = a[0:1, :] * carry

        gt = g_s[...]
        da = gt * h_m1
        ixc = ig * xc
        di = gt * mult * xc
        dxc = gt * mult * ig
        dlog_a = da * a + jnp.where(first_row, 0.0, gt * ixc * (-a2 / mult))
        sm_ref[3:4, :] += _rowsum(dlog_a * ((-RG_C) * r))
        dpr_f = dlog_a * ((-RG_C) * sp) * r * (1.0 - r)
        dpi_f = di * ig * (1.0 - ig)
        sm_ref[1:2, :] += _rowsum(dpr_f)
        sm_ref[2:3, :] += _rowsum(dpi_f)
        dpr = dpr_f.astype(BF16)
        dpi = dpi_f.astype(BF16)
        xcb = xc.astype(BF16)
        dwa_s[...] += _mm_tn(xcb, dpr)
        dwx_s[...] += _mm_tn(xcb, dpi)
        dxc = dxc + _mm_nt(dpr, wa_s[...]) + _mm_nt(dpi, wx_s[...])
        sm_ref[0:1, :] += _rowsum(dxc)
        for j in range(4):
            sm_ref[4 + j:5 + j, :] += _rowsum(dxc * taps[j])
        nxt = nxt_s[...]
        dxa = (dxc * cw[3] + _shift_up(dxc, nxt, 1, rows) * cw[2]
               + _shift_up(dxc, nxt, 2, rows) * cw[1] + _shift_up(dxc, nxt, 3, rows) * cw[0])
        nxt_s[...] = dxc[0:8, :]
        da_ref[:, 0:DG] = dxa.astype(BF16)
        da_ref[:, DG:D] = dga.astype(BF16)

        @pl.when(i == nt - 1)
        def _():
            _, s_neg = _sig_pair(lam_v)
            sm_ref[3:4, :] = sm_ref[3:4, :] * (-s_neg)
            rg_ref[...] = _blocks_from_dense(dwa_s[...], dwx_s[...])

    vec = lambda: pl.BlockSpec((1, DG), lambda i: (0, 0))
    blocks = lambda: pl.BlockSpec((DG, RGB), lambda i: (0, 0))
    prev8 = lambda: pl.BlockSpec((8, DG), lambda i: (jnp.maximum((nt - 1 - i) * (tm // 8) - 1, 0), 0))
    return pl.pallas_call(
        body, name="bwd_rglru", grid=(nt,),
        in_specs=[pl.BlockSpec((tm, DG), lambda i: (nt - 1 - i, 0)),
                  pl.BlockSpec((tm, DG), lambda i: (nt - 1 - i, 0)),
                  prev8(),
                  pl.BlockSpec((tm, DG), lambda i: (nt - 1 - i, 0)),
                  prev8(),
                  pl.BlockSpec((tm, DG), lambda i: (nt - 1 - i, 0)),
                  pl.BlockSpec((NSHARD, 4, 128), lambda i: (0, 0, 0)), vec(),
                  blocks(), vec(), blocks(), vec(), vec(), ANY_SPEC],
        out_specs=[pl.BlockSpec((tm, D), lambda i: (nt - 1 - i, 0)),
                   pl.BlockSpec((DG, 128), lambda i: (0, 0)),
                   pl.BlockSpec((8, DG), lambda i: (0, 0))],
        out_shape=[jax.ShapeDtypeStruct((T, D), BF16), jax.ShapeDtypeStruct((DG, 128), F32),
                   jax.ShapeDtypeStruct((8, DG), F32)],
        scratch_shapes=[pltpu.VMEM((tm, DG), F32), pltpu.VMEM((tm, DG), F32),
                        pltpu.VMEM((8, DG), F32), pltpu.VMEM((8, DG), F32),
                        pltpu.VMEM((DG, DG), BF16), pltpu.VMEM((DG, DG), BF16),
                        pltpu.VMEM((DG, DG), F32), pltpu.VMEM((DG, DG), F32)],
        compiler_params=_cparams(("arbitrary",), 56),
    )(pf, pb, pf, h, h, dyab, cw3, conv_b, wa_c, ba, wx_c, bx, lam, dep)


def _bwd_hgrn2(pf, pb, o, s_prev, dyab, hg_lb, hg_nw, dep):
    tm = 512
    nt = T // tm
    nc = tm // CH

    def body(q_ref, f_ref, v_ref, g_ref, o_ref, sp_ref, dy_ref, lb_ref, nw_ref, dep_ref, db_ref, sm_ref,
             dst_s, eb_s, enb_s, ekd_s, qe_s, ke_s, kd_s, vb_s, do_s, dec_s, ddec_s, p_s, dp_s,
             g_s, dsta_s, dva_s, dqe_s, dke_s, dkd_s, dlf_s):
        i = pl.program_id(0)

        @pl.when(i == 0)
        def _():
            sm_ref[...] = jnp.zeros_like(sm_ref)
            dst_s[...] = jnp.zeros_like(dst_s)

        lb, one_m_lb = _hg_lower_bound(lb_ref)
        rows64 = lax.broadcasted_iota(I32, (tm, DG), 0) & (CH - 1)
        sg, sn, f, k, q, sq, qs = _hg_recompute(
            q_ref, f_ref, lb, one_m_lb, rows64, eb_s, enb_s, ekd_s, qe_s, ke_s, kd_s, dec_s)
        vb_s[...] = v_ref[...]

        nw = nw_ref[...]
        for hd in range(NH):
            cols = slice(hd * HD, (hd + 1) * HD)
            g = g_ref[:, cols].astype(F32)
            sgg = _sigmoid(g)
            o = o_ref[:, cols]
            so = lax.rsqrt(jnp.mean(o * o, axis=-1, keepdims=True) + EPS)
            oh = o * so
            dyb = dy_ref[:, cols]
            db_ref[:, 3 * DG + hd * HD:3 * DG + (hd + 1) * HD] = (
                dyb * (oh * nw) * (sgg * (1.0 + g * (1.0 - sgg)))).astype(BF16)
            don = dyb * (g * sgg)
            sm_ref[1:2, 0:HD] += _rowsum(don * oh)
            gw = don * nw
            do_s[:, cols] = (so * (gw - oh * jnp.mean(gw * oh, axis=-1, keepdims=True))).astype(BF16)

        mask = _tri(True)
        items = [(c, hd, slice(c * CH, (c + 1) * CH), slice(hd * HD, (hd + 1) * HD))
                 for c in range(nc) for hd in range(NH)]
        for c, hd, rs, cols in items:
            p_s[c * NH + hd] = jnp.where(mask, _mm_nt(qe_s[rs, cols], ke_s[rs, cols]), 0.0).astype(BF16)
            dp_s[c * NH + hd] = jnp.where(mask, _mm_nt(do_s[rs, cols], vb_s[rs, cols]), 0.0).astype(BF16)
        for c, hd, rs, cols in items:
            n = c * NH + hd
            dva_s[rs, cols] = _mm_tn(p_s[n], do_s[rs, cols])
            dqe_s[rs, cols] = _mm(dp_s[n], ke_s[rs, cols])
            dke_s[rs, cols] = _mm_tn(dp_s[n], qe_s[rs, cols])
            g_s[n] = _mm_tn(do_s[rs, cols], qe_s[rs, cols])
        for c, hd, rs, cols in reversed(items):
            n = c * NH + hd
            dst = dst_s[hd]
            dsta_s[n] = dst
            dst_s[hd] = dst * dec_s[c:c + 1, cols] + g_s[n]
        for c, hd, rs, cols in items:
            n = c * NH + hd
            dst = dsta_s[n]
            st_prev = sp_ref[hd, c]
            dv = dva_s[rs, cols] + _mm_nt(kd_s[rs, cols], dst)
            db_ref[rs, 2 * DG + hd * HD:2 * DG + (hd + 1) * HD] = dv.astype(BF16)
            dqe_s[rs, cols] += _mm(do_s[rs, cols], st_prev)
            dkd_s[rs, cols] = _mm(vb_s[rs, cols], dst)
            ddec_s[c:c + 1, cols] = _rowsum(dst * st_prev)

        eb, enb, ekd = eb_s[...], enb_s[...], ekd_s[...]
        dqe, dke, dkd = dqe_s[...], dke_s[...], dkd_s[...]
        t_kd = dkd * (k * ekd)
        rc = _chunk_rev_cumsum(dqe * (qs * eb) - dke * (k * enb) - t_kd, rows64)
        for c in range(nc):
            rs = slice(c * CH, (c + 1) * CH)
            dbl = _rowsum(t_kd[rs]) + ddec_s[c:c + 1, :] * dec_s[c:c + 1, :]
            dlf_s[rs, :] = rc[rs] + dbl
        t = dlf_s[...] / f - (dke * enb + dkd * ekd)
        db_ref[:, DG:2 * DG] = (one_m_lb * sg * sn * t).astype(BF16)
        sm_ref[0:1, :] += _rowsum(sn * t)
        db_ref[:, 0:DG] = (dqe * eb * (sq * (1.0 + q * (1.0 - sq))) * (HD ** -0.5)).astype(BF16)

        @pl.when(i == nt - 1)
        def _():
            dsm = sm_ref[0:1, :] * (lb * one_m_lb)
            sm_ref[2:3, :] = dsm
            sm_ref[3:4, :] = -dsm

    col = lambda j: pl.BlockSpec((tm, DG), lambda i: (nt - 1 - i, j))
    big = lambda dt: pltpu.VMEM((tm, DG), dt)
    return pl.pallas_call(
        body, name="bwd_hgrn2", grid=(nt,),
        in_specs=[col(1), col(1), col(2), col(3),
                  pl.BlockSpec((tm, DG), lambda i: (nt - 1 - i, 0)),
                  pl.BlockSpec((NH, nc, HD, HD), lambda i: (0, nt - 1 - i, 0, 0)),
                  pl.BlockSpec((tm, DG), lambda i: (nt - 1 - i, 1)),
                  pl.BlockSpec((2, DG), lambda i: (0, 0)),
                  pl.BlockSpec((1, HD), lambda i: (0, 0)), ANY_SPEC],
        out_specs=[pl.BlockSpec((tm, 4 * DG), lambda i: (nt - 1 - i, 0)),
                   pl.BlockSpec((8, DG), lambda i: (0, 0))],
        out_shape=[jax.ShapeDtypeStruct((T, 4 * DG), BF16), jax.ShapeDtypeStruct((8, DG), F32)],
        scratch_shapes=[pltpu.VMEM((NH, HD, HD), F32),
                        big(F32), big(F32), big(F32),
                        big(BF16), big(BF16), big(BF16), big(BF16), big(BF16),
                        pltpu.VMEM((nc, DG), F32), pltpu.VMEM((nc, DG), F32),
                        pltpu.VMEM((nc * NH, CH, CH), BF16), pltpu.VMEM((nc * NH, CH, CH), BF16),
                        pltpu.VMEM((nc * NH, HD, HD), F32), pltpu.VMEM((nc * NH, HD, HD), F32),
                        big(F32), big(F32), big(F32), big(F32), big(F32)],
        compiler_params=_cparams(("arbitrary",), 56),
    )(pb, pf, pb, pb, o, s_prev, dyab, hg_lb, hg_nw, dep)


def _dproj_pieces(k, da_ref, db_ref):
    if k == 0:
        return [(da_ref[:, 0:SHW], 0)]
    if k == 1:
        return [(da_ref[:, SHW:D], 0), (db_ref[:, 0:DG], D - SHW)]
    if k == 2:
        return [(db_ref[:, DG:DG + SHW], 0)]
    return [(db_ref[:, DG + SHW:4 * DG], 0)]


def _bwd_inproj_dx(x, dh1, d_a, d_b, w_in_b, nw, dep):
    tm = 512
    nt = T // tm

    def body(x_ref, dh1_ref, da_ref, db_ref, w_ref, nw_ref, dep_ref, dx_ref, sm_ref):
        i = pl.program_id(0)

        @pl.when(i == 0)
        def _():
            sm_ref[...] = jnp.zeros_like(sm_ref)

        du = None
        for k in range(NSHARD):
            for val, off in _dproj_pieces(k, da_ref, db_ref):
                t = _mm_nt(val, w_ref[k, :, off:off + val.shape[1]])
                du = t if du is None else du + t
        xv = x_ref[...]
        s = lax.rsqrt(jnp.mean(xv * xv, axis=-1, keepdims=True) + EPS)
        xh = xv * s
        sm_ref[0:1, :] += _rowsum(du * xh)
        g = du * nw_ref[...]
        dx_ref[...] = dh1_ref[...] + s * (g - xh * jnp.mean(g * xh, axis=-1, keepdims=True))

    row = lambda w: pl.BlockSpec((tm, w), lambda i: (i, 0))
    return pl.pallas_call(
        body, name="bwd_inproj_dx", grid=(nt,),
        in_specs=[row(D), row(D), row(D), row(4 * DG),
                  pl.BlockSpec((NSHARD, D, SHW), lambda i: (0, 0, 0), pipeline_mode=pl.Buffered(1)),
                  pl.BlockSpec((1, D), lambda i: (0, 0)), ANY_SPEC],
        out_specs=[row(D), pl.BlockSpec((8, D), lambda i: (0, 0))],
        out_shape=[jax.ShapeDtypeStruct((T, D), F32), jax.ShapeDtypeStruct((8, D), F32)],
        compiler_params=_cparams(("arbitrary",), 56),
    )(x, dh1, d_a, d_b, w_in_b, nw, dep)


def _bwd_inproj_dw23(u_b, d_b):
    tm = 1024
    nt = T // tm

    def body(u_ref, db_ref, dw_ref, acc):
        i = pl.program_id(0)

        @pl.when(i == 0)
        def _():
            acc[...] = jnp.zeros_like(acc)

        u = u_ref[...]
        for j, k in enumerate((2, 3)):
            for val, off in _dproj_pieces(k, None, db_ref):
                acc[j, :, off:off + val.shape[1]] += _mm_tn(u, val)

        @pl.when(i == nt - 1)
        def _():
            for j in range(2):
                for r0 in range(0, D, 256):
                    dw_ref[j, r0:r0 + 256, :] = acc[j, r0:r0 + 256, :].astype(BF16)

    row = lambda w: pl.BlockSpec((tm, w), lambda i: (i, 0))
    return pl.pallas_call(
        body, name="bwd_inproj_dw23", grid=(nt,), in_specs=[row(D), row(4 * DG)],
        out_specs=pl.BlockSpec((2, D, SHW), lambda i: (0, 0, 0)),
        out_shape=pltpu.HBM((2, D, SHW), BF16),
        scratch_shapes=[pltpu.VMEM((2, D, SHW), F32)],
        compiler_params=_cparams(("arbitrary",), 48),
    )(u_b, d_b)


def _bwd_inproj_dw01(u_b, d_a, d_b, dw23):
    tm = 1024
    nt = T // tm
    H = D // 2
    step = 128

    def body(u_ref, da_ref, db_ref, p23_ref, dw_ref, acc, send_s, recv_s, ssem, rsem):
        i = pl.program_id(0)
        x, y, c = lax.axis_index("x"), lax.axis_index("y"), lax.axis_index("c")
        sibling = (x, y, 1 - c)
        mine0 = pl.multiple_of(c * H, step)
        other0 = pl.multiple_of((1 - c) * H, step)

        def swap(k):
            return _remote(send_s.at[k], recv_s.at[k], ssem.at[k], rsem.at[k], sibling)

        @pl.when(i == 0)
        def _():
            acc[...] = jnp.zeros_like(acc)
            for k in (2, 3):
                for r0 in range(0, H, step):
                    send_s[k, r0:r0 + step, :] = p23_ref[k - 2, pl.ds(other0 + r0, step), :]
                swap(k).start()

        u = u_ref[...]
        for j, k in enumerate((0, 1)):
            for val, off in _dproj_pieces(k, da_ref, db_ref):
                acc[j, :, off:off + val.shape[1]] += _mm_tn(u, val)

        @pl.when(i == nt - 1)
        def _():
            copies = [swap(k) for k in range(NSHARD)]
            for k in (0, 1):
                for r0 in range(0, H, step):
                    send_s[k, r0:r0 + step, :] = acc[k, pl.ds(other0 + r0, step), :].astype(BF16)
                copies[k].start()
            for k in (2, 3, 0, 1):
                copies[k].wait_recv()
                for r0 in range(0, H, step):
                    rs = pl.ds(mine0 + r0, step)
                    own = acc[k, rs, :] if k < 2 else p23_ref[k - 2, rs, :].astype(F32)
                    dw_ref[k, r0:r0 + step, :] = (own + recv_s[k, r0:r0 + step, :].astype(F32)).astype(BF16)
            for cp in copies:
                cp.wait_send()

    row = lambda w: pl.BlockSpec((tm, w), lambda i: (i, 0))
    return pl.pallas_call(
        body, name="bwd_inproj_dw01", grid=(nt,),
        in_specs=[row(D), row(D), row(4 * DG),
                  pl.BlockSpec((2, D, SHW), lambda i: (0, 0, 0), pipeline_mode=pl.Buffered(1))],
        out_specs=pl.BlockSpec((NSHARD, H, SHW), lambda i: (0, 0, 0)),
        out_shape=jax.ShapeDtypeStruct((NSHARD, H, SHW), BF16),
        scratch_shapes=[pltpu.VMEM((2, D, SHW), F32), pltpu.VMEM((NSHARD, H, SHW), BF16),
                        pltpu.VMEM((NSHARD, H, SHW), BF16),
                        pltpu.SemaphoreType.DMA((NSHARD,)), pltpu.SemaphoreType.DMA((NSHARD,))],
        compiler_params=_cparams(("arbitrary",), 56),
    )(u_b, d_a, d_b, dw23)


_OUT_ORDER = ["norm_mix_w", "w_in", "conv_w", "conv_b", "rg_wa", "rg_ba", "rg_wx", "rg_bx", "rg_lambda", "hg_lb",
              "hg_norm_w", "w_out", "ple_norm_w", "w_ple_gate", "b_ple_gate", "w_ple_proj", "final_norm_w"]
_BIG = ["w_in", "w_out", "w_ple_gate", "w_ple_proj"]


def _small_view(name, a):
    if name in ("rg_wa", "rg_wx"):
        return a.reshape(DG, RGB)
    if name == "conv_w":
        return a.reshape(4, 128)
    if name == "final_norm_w":
        return a.reshape(1, D)
    return a


def _landing(rows, cols):
    return lax.empty((NDEV, rows, cols), BF16)


def kernel(x, p, norm_mix_w, w_in, conv_w, conv_b, rg_wa, rg_ba, rg_wx, rg_bx, rg_lambda, hg_lb, hg_norm_w, w_out, ple_norm_w, w_ple_gate, b_ple_gate, w_ple_proj, final_norm_w, loss_target, m_norm_mix_w, m_w_in, m_conv_w, m_conv_b, m_rg_wa, m_rg_ba, m_rg_wx, m_rg_bx, m_rg_lambda, m_hg_lb, m_hg_norm_w, m_w_out, m_ple_norm_w, m_w_ple_gate, m_b_ple_gate, m_w_ple_proj, m_final_norm_w, v_norm_mix_w, v_w_in, v_conv_w, v_conv_b, v_rg_wa, v_rg_ba, v_rg_wx, v_rg_bx, v_rg_lambda, v_hg_lb, v_hg_norm_w, v_w_out, v_ple_norm_w, v_w_ple_gate, v_b_ple_gate, v_w_ple_proj, v_final_norm_w):
    given = dict(locals())
    x2, p2, tgt = x[0], p[0, 0], loss_target[0]
    hbm = lambda a: pltpu.with_memory_space_constraint(a, pltpu.HBM)
    norm_mix_w, conv_b, rg_ba, rg_bx, rg_lambda, hg_lb, hg_norm_w, ple_norm_w, b_ple_gate = (
        hbm(a) for a in (norm_mix_w, conv_b, rg_ba, rg_bx, rg_lambda, hg_lb, hg_norm_w, ple_norm_w, b_ple_gate))
    wa_c, wx_c = hbm(_small_view("rg_wa", rg_wa)), hbm(_small_view("rg_wx", rg_wx))

    w_in_b, l_out, l_pg, l_pp, cw3, pf, pb, u_b = _gather_inproj(
        x2, norm_mix_w, w_in[0], w_out[0], w_ple_gate[0], w_ple_proj[0], conv_w[0])
    g_ssem, g_rsem, g_lands, tok = _gather_rest_start([l_out, l_pg, l_pp])

    h, ya = _fwd_rglru(pf, pb, cw3, conv_b, wa_c, rg_ba, wx_c, rg_bx, rg_lambda, tok)
    yb, o, s_prev = _fwd_hgrn2(pf, pb, hg_lb, hg_norm_w)
    w_out_b, w_pg_b, w_pp_b = _gather_rest_wait(g_ssem, g_rsem, g_lands, yb)
    dh1, dyab, dwo_b, dwg_b, dwp_b, sm_tail = _tail_fwd_bwd(
        x2, p2, tgt, ya, yb, w_out_b.reshape(D, D), w_pg_b.reshape(D, D), w_pp_b,
        ple_norm_w, b_ple_gate, final_norm_w.reshape(1, D))

    QH = D // NSHARD // 2
    r1 = _rs_start("rs_start_tail", [dwo_b, dwg_b, dwp_b], (0, 1, 2, 3),
                   [_landing(QH, D), _landing(QH, D), _landing(PLE // 2, PLE)])
    d_b, sm_b = _bwd_hgrn2(pf, pb, o, s_prev, dyab, hg_lb, hg_norm_w, r1[4])
    dw23 = _bwd_inproj_dw23(u_b, d_b)
    d_a, rg_c, sm_a = _bwd_rglru(pf, pb, h, dyab, cw3, conv_b, wa_c, rg_ba, wx_c, rg_bx, rg_lambda, dw23)
    dw_in = _bwd_inproj_dw01(u_b, d_a, d_b, dw23)
    r2 = _rs_start("rs_start_in", [dw_in], (0, 1, 2, 3), [lax.empty((NSHARD, D // 2, SHW), BF16)], chip_sums=True)
    grad_x, sm_in = _bwd_inproj_dx(x2, dh1, d_a, d_b, w_in_b, norm_mix_w, r2[4])

    parts1, lands1 = _rs_wait("rs_wait_tail", r1[0], r1[1], r1[2], (0, 1, 2, 3), r1[3], sm_in)
    parts2, lands2 = _rs_wait("rs_wait_in", r2[0], r2[1], r2[2], (0, 1, 2, 3), r2[3], sm_in, chip_sums=True)
    g_big, vred, rgred = _final_reduce(parts2 + parts1, lands2 + lands1, sm_in, sm_tail, sm_a, sm_b, rg_c)

    upd_big = _adam_big(g_big, [given[n][0] for n in _BIG], [given["m_" + n][0] for n in _BIG],
                        [given["v_" + n][0] for n in _BIG])
    small = _adam_small(vred, rgred,
                        {n: _small_view(n, given[n]) for n in _SMALL_ORDER},
                        {n: _small_view(n, given["m_" + n]) for n in _SMALL_ORDER},
                        {n: _small_view(n, given["v_" + n]) for n in _SMALL_ORDER})

    loss = vred[0, ROW_LOSS, 0]
    outs = [loss, grad_x[None]]
    for ki in range(4):
        for n in _OUT_ORDER:
            if n in _BIG:
                i = _BIG.index(n)
                a = g_big[i] if ki == 0 else upd_big[i][ki - 1]
                outs.append(a[None])
            else:
                outs.append(small[n][ki].reshape(given[n].shape))
    return tuple(outs)
```

```python
import jax
import jax.numpy as jnp
from jax import lax
from jax.experimental import pallas as pl
from jax.experimental.pallas import tpu as pltpu

F32 = jnp.float32
BF16 = jnp.bfloat16
I32 = jnp.int32
MESH = pl.DeviceIdType.MESH

T = 4096
D = 1024
DG = 512
DIN = 3072
PLE = 256
NH = 4
HD = 128
CH = 64
NCHUNK = T // CH
RGB = 64
EPS = 1e-6
RG_C = 8.0
NSHARD = 4
SHW = DIN // NSHARD
NDEV = 8

ADAM_LR = 0.001
ADAM_B1 = 0.9
ADAM_B2 = 0.999
ADAM_EPS = 1e-08
ADAM_WD = 0.01
ADAM_STEP = 10

VMEM_SPEC = pl.BlockSpec(memory_space=pltpu.VMEM)
HBM_SPEC = pl.BlockSpec(memory_space=pltpu.HBM)
SEM_SPEC = pl.BlockSpec(memory_space=pltpu.SEMAPHORE)
ANY_SPEC = pl.BlockSpec(memory_space=pl.ANY)
EFFECT = pltpu.SideEffectType.DATAFLOW_SIDE_EFFECTING
MIB = 1024 * 1024

VROWS = 16
ROW_NORM_MIX, ROW_FINAL_NORM, ROW_B_PG, ROW_PLE_NORM = 0, 1, 2, 3
ROW_CB_BA, ROW_BX_LAM, ROW_CW01, ROW_CW23, ROW_HG_LB, ROW_HG_NW, ROW_LOSS = 4, 5, 6, 7, 8, 9, 10


def _mm(a, b):
    return jnp.dot(a.astype(BF16), b.astype(BF16), preferred_element_type=F32)


def _mm_nt(a, b):
    return lax.dot_general(a.astype(BF16), b.astype(BF16), (((1,), (1,)), ((), ())),
                           preferred_element_type=F32)


def _mm_tn(a, b):
    return lax.dot_general(a.astype(BF16), b.astype(BF16), (((0,), (0,)), ((), ())),
                           preferred_element_type=F32)


def _mm_exact(a, b):
    bb = b.astype(BF16)
    hi = a.astype(BF16)
    r1 = a - hi.astype(F32)
    mid = r1.astype(BF16)
    lo = (r1 - mid.astype(F32)).astype(BF16)
    return sum(jnp.dot(p, bb, preferred_element_type=F32) for p in (hi, mid, lo))


def _sig_pair(x):
    e = jnp.exp(-jnp.abs(x))
    big = 1.0 / (1.0 + e)
    small = e * big
    pos = x >= 0
    return jnp.where(pos, big, small), jnp.where(pos, small, big)


def _sigmoid(x):
    return 1.0 / (1.0 + jnp.exp(-x))


def _rowsum(v):
    return jnp.sum(v, axis=0, keepdims=True)


def _shift_down(cur, prev8, d, rows):
    rolled = pltpu.roll(cur, d, 0)
    head = jnp.where(rows[0:8] < d, pltpu.roll(prev8, d, 0), rolled[0:8])
    return jnp.concatenate([head, rolled[8:]], axis=0)


def _shift_up(cur, next8, d, rows):
    n = cur.shape[0]
    rolled = pltpu.roll(cur, n - d, 0)
    tail = jnp.where(rows[0:8] >= 8 - d, pltpu.roll(next8, 8 - d, 0), rolled[n - 8:n])
    return jnp.concatenate([rolled[0:n - 8], tail], axis=0)


def _roll_in_groups(v, d):
    n, w = v.shape
    return pltpu.roll(v.reshape(n // 8, 8, w), d, 1).reshape(n, w)


def _cparams(sem, vmem_mib):
    return pltpu.CompilerParams(dimension_semantics=sem, vmem_limit_bytes=vmem_mib * MIB)


def _mesh_pos():
    x, y, c = lax.axis_index("x"), lax.axis_index("y"), lax.axis_index("c")
    chips = [(1 - x, y), (x, 1 - y), (1 - x, 1 - y)]
    return x, y, c, chips


def _remote(src, dst, ssem, rsem, dev):
    return pltpu.make_async_remote_copy(src_ref=src, dst_ref=dst, send_sem=ssem, recv_sem=rsem,
                                        device_id=dev, device_id_type=MESH)


def _gather_inproj(x, nw, w_in, w_out, w_pg, w_pp, conv_w):
    shapes = [w_in.shape, w_out.shape, w_pg.shape, w_pp.shape]
    tm = 512
    nt = T // tm

    def body(x_h, nw_h, win_h, wout_h, wpg_h, wpp_h, cw_h,
             o_in_h, o_out_h, o_pg_h, o_pp_h, o_cw_h, pf_h, pb_h, u_h,
             win, wout, wpg, wpp, cw, nw_s, xbuf, ubuf, rfbuf, rbbuf, o_in, o_out, o_pg, o_pp, o_cw,
             lsem, xsem, usem, fsem, bsem, osem, ssem, rsem):
        x, y, c, chips = _mesh_pos()
        kme = 2 * x + y
        sibling = (x, y, 1 - c)
        fetch = [pltpu.make_async_copy(src, dst, lsem.at[i]) for i, (src, dst) in enumerate(
            [(win_h, win), (cw_h, cw), (nw_h, nw_s), (wout_h, wout), (wpg_h, wpg), (wpp_h, wpp)])]
        for cp in fetch:
            cp.start()

        def cast(src, dst):
            for r0 in range(0, src.shape[0], 128):
                dst[r0:r0 + 128, :] = src[r0:r0 + 128, :].astype(BF16)

        outs = []

        def write_out(src, dst, i):
            cp = pltpu.make_async_copy(src, dst, osem.at[i])
            cp.start()
            outs.append(cp)

        hrows = D // 2
        mine = pl.ds(pl.multiple_of(c * hrows, 128), hrows)
        other = pl.ds(pl.multiple_of((1 - c) * hrows, 128), hrows)

        def cast_half(rows):
            for r0 in range(0, hrows, 128):
                rs = pl.ds(pl.multiple_of(rows.start + r0, 128), 128)
                o_in[kme, rs, :] = win[rs, :].astype(BF16)

        fetch[0].wait()
        cast_half(mine)
        fetch[1].wait()
        o_cw[kme] = cw[...]

        def half(k, rows):
            return o_in.at[k, rows]

        na = (x ^ c, y ^ (1 - c), c)
        nb = (x ^ (1 - c), y ^ c, c)
        ka = 2 * na[0] + na[1]
        kb = 2 * nb[0] + nb[1]
        kd = 2 * (1 - x) + (1 - y)
        sends = [_remote(half(kme, mine), half(kme, mine), ssem.at[0], rsem.at[0], na),
                 _remote(half(kme, mine), half(kme, mine), ssem.at[1], rsem.at[1], nb)]
        for j, (px, py) in enumerate(chips):
            sends.append(_remote(o_cw.at[kme], o_cw.at[kme], ssem.at[3 + j], rsem.at[3 + j], (px, py, c)))
        for cp in sends:
            cp.start()
        cast_half(other)
        write_out(o_in.at[kme], o_in_h.at[kme], 0)
        fetch[2].wait()

        def rows_of(t):
            return pl.ds(pl.multiple_of(t * tm, tm), tm)

        def x_copy(t, slot):
            return pltpu.make_async_copy(x_h.at[rows_of(t)], xbuf.at[slot], xsem.at[slot])

        u_out = pltpu.make_async_copy(ubuf, u_h, usem.at[0])

        def f_store(t, slot, kh):
            return pltpu.make_async_copy(rfbuf.at[slot], pf_h.at[rows_of(t), pl.ds(pl.multiple_of(kh * DG, DG), DG)],
                                         fsem.at[slot])

        def b_store_even(t, slot, kh):
            return pltpu.make_async_copy(rbbuf.at[slot, :, 0:256],
                                         pb_h.at[rows_of(t), pl.ds(pl.multiple_of(kh * 2 * DG, 256), 256)],
                                         bsem.at[slot])

        def b_store_odd(t, slot, kh):
            return pltpu.make_async_copy(rbbuf.at[slot],
                                         pb_h.at[rows_of(t), pl.ds(pl.multiple_of(256 + kh * 2 * DG, 256), SHW)],
                                         bsem.at[slot])

        def run_pass(k, first, even):
            kh = k >> 1
            if first:
                x_copy(0, 0).start()

            def tile(t, carry):
                slot = t & 1
                if first:
                    @pl.when(t + 1 < nt)
                    def _():
                        x_copy(t + 1, 1 - slot).start()

                    x_copy(t, slot).wait()

                @pl.when(t >= 2)
                def _():
                    if even:
                        f_store(t - 2, slot, kh).wait()
                        b_store_even(t - 2, slot, kh).wait()
                    else:
                        b_store_odd(t - 2, slot, kh).wait()

                if first:
                    xv = xbuf[slot]
                    s = lax.rsqrt(jnp.mean(xv * xv, axis=-1, keepdims=True) + EPS)
                    u = (xv * s * nw_s[...]).astype(BF16)
                    ubuf[rows_of(t), :] = u
                else:
                    u = ubuf[rows_of(t), :]
                r = jnp.dot(u, o_in[k], preferred_element_type=F32)
                if even:
                    rfbuf[slot] = r[:, 0:DG]
                    rbbuf[slot, :, 0:256] = r[:, DG:SHW].astype(BF16)
                    f_store(t, slot, kh).start()
                    b_store_even(t, slot, kh).start()
                else:
                    rbbuf[slot] = r.astype(BF16)
                    b_store_odd(t, slot, kh).start()
                return carry

            lax.fori_loop(0, nt, tile, 0)
            for t in (nt - 2, nt - 1):
                slot = t & 1
                if even:
                    f_store(t, slot, kh).wait()
                    b_store_even(t, slot, kh).wait()
                else:
                    b_store_odd(t, slot, kh).wait()

        def project(k, first=False):
            @pl.when((k & 1) == 0)
            def _():
                run_pass(k, first, True)

            @pl.when((k & 1) == 1)
            def _():
                run_pass(k, first, False)

        def passed_on(k, sem, dev):
            cp = _remote(half(k, mine), half(k, mine), ssem.at[sem], rsem.at[sem], dev)
            cp.start()
            sends.append(cp)

        project(kme, first=True)
        u_out.start()
        for i, (src, dst, dst_h) in enumerate([(wout, o_out, o_out_h), (wpg, o_pg, o_pg_h), (wpp, o_pp, o_pp_h)]):
            fetch[3 + i].wait()
            cast(src, dst)
            write_out(dst, dst_h.at[kme], 4 + i)
        _remote(half(ka, mine), half(ka, mine), ssem.at[0], rsem.at[0], na).wait_recv()
        passed_on(ka, 2, nb)
        passed_on(ka, 6, sibling)
        _remote(half(kb, mine), half(kb, mine), ssem.at[1], rsem.at[1], nb).wait_recv()
        passed_on(kb, 7, sibling)
        _remote(half(ka, other), half(ka, other), ssem.at[7], rsem.at[7], sibling).wait_recv()
        write_out(o_in.at[ka], o_in_h.at[ka], 1)
        project(ka)
        _remote(half(kd, mine), half(kd, mine), ssem.at[2], rsem.at[2], nb).wait_recv()
        passed_on(kd, 8, sibling)
        _remote(half(kb, other), half(kb, other), ssem.at[6], rsem.at[6], sibling).wait_recv()
        write_out(o_in.at[kb], o_in_h.at[kb], 2)
        project(kb)
        _remote(half(kd, other), half(kd, other), ssem.at[8], rsem.at[8], sibling).wait_recv()
        write_out(o_in.at[kd], o_in_h.at[kd], 3)
        project(kd)
        u_out.wait()
        for j, (px, py) in enumerate(chips):
            kj = 2 * px + py
            _remote(o_cw.at[kj], o_cw.at[kj], ssem.at[3 + j], rsem.at[3 + j], (px, py, c)).wait_recv()
        write_out(o_cw, o_cw_h, 7)
        for cp in outs:
            cp.wait()
        for cp in sends:
            cp.wait_send()

    out_shape = [pltpu.HBM((NSHARD,) + s, BF16) for s in shapes]
    out_shape.append(pltpu.HBM((NSHARD,) + conv_w.shape, F32))
    out_shape += [pltpu.HBM((T, 2 * DG), F32), pltpu.HBM((T, 4 * DG), BF16), pltpu.HBM((T, D), BF16)]
    hbm_args = [pltpu.with_memory_space_constraint(a, pltpu.HBM) for a in (x, nw, w_in, w_out, w_pg, w_pp, conv_w)]
    return pl.pallas_call(
        body, name="gather_inproj", out_shape=out_shape,
        in_specs=[HBM_SPEC] * 7, out_specs=[HBM_SPEC] * 8,
        scratch_shapes=[pltpu.VMEM(a.shape, F32) for a in (w_in, w_out, w_pg, w_pp, conv_w, nw)]
        + [pltpu.VMEM((2, tm, D), F32), pltpu.VMEM((T, D), BF16),
           pltpu.VMEM((2, tm, DG), F32), pltpu.VMEM((2, tm, SHW), BF16),
           pltpu.VMEM((NSHARD,) + w_in.shape, BF16), pltpu.VMEM(w_out.shape, BF16), pltpu.VMEM(w_pg.shape, BF16),
           pltpu.VMEM(w_pp.shape, BF16), pltpu.VMEM((NSHARD,) + conv_w.shape, F32),
           pltpu.SemaphoreType.DMA((6,)), pltpu.SemaphoreType.DMA((2,)), pltpu.SemaphoreType.DMA((2,)),
           pltpu.SemaphoreType.DMA((2,)), pltpu.SemaphoreType.DMA((2,)), pltpu.SemaphoreType.DMA((8,)),
           pltpu.SemaphoreType.DMA((9,)), pltpu.SemaphoreType.DMA((9,))],
        compiler_params=pltpu.CompilerParams(vmem_limit_bytes=48 * MIB),
    )(*hbm_args)


def _gather_rest_start(lands):
    n = len(lands)

    def body(*refs):
        land_in = refs[0:n]
        ssem, rsem = refs[n], refs[n + 1]
        token = refs[2 * n + 2]
        x, y, c, chips = _mesh_pos()
        kme = 2 * x + y
        for p, land in enumerate(land_in):
            hrows = land.shape[1] // 2
            mine = pl.ds(pl.multiple_of(c * hrows, 128), hrows)
            for px, py in chips:
                for pc in range(2):
                    _remote(land.at[kme, mine], land.at[kme, mine], ssem.at[p], rsem.at[p], (px, py, pc)).start()
        token[...] = jnp.zeros_like(token)

    out_shape = ([pltpu.SemaphoreType.DMA((n,)), pltpu.SemaphoreType.DMA((n,))]
                 + [pltpu.HBM(a.shape, a.dtype) for a in lands] + [jax.ShapeDtypeStruct((8, 128), F32)])
    outs = pl.pallas_call(
        body, name="gather_rest_start", out_shape=out_shape,
        in_specs=[HBM_SPEC] * n, out_specs=[SEM_SPEC, SEM_SPEC] + [HBM_SPEC] * n + [VMEM_SPEC],
        input_output_aliases={i: 2 + i for i in range(n)},
        compiler_params=pltpu.CompilerParams(has_side_effects=EFFECT),
    )(*[pltpu.with_memory_space_constraint(a, pltpu.HBM) for a in lands])
    return outs[0], outs[1], list(outs[2:2 + n]), outs[2 + n]


def _gather_rest_wait(ssem, rsem, lands, after):
    n = len(lands)

    def body(*refs):
        land_in = refs[0:n]
        ssem_ref, rsem_ref = refs[n], refs[n + 1]
        x, y, c = lax.axis_index("x"), lax.axis_index("y"), lax.axis_index("c")
        for p, land in enumerate(land_in):
            three = land.at[pl.ds(0, 3)]
            cp = _remote(three, three, ssem_ref.at[p], rsem_ref.at[p], (x, y, c))
            cp.wait_send()
            cp.wait_recv()

    outs = pl.pallas_call(
        body, name="gather_rest_wait", out_shape=[pltpu.HBM(a.shape, a.dtype) for a in lands],
        in_specs=[HBM_SPEC] * n + [SEM_SPEC, SEM_SPEC, ANY_SPEC], out_specs=[HBM_SPEC] * n,
        input_output_aliases={i: i for i in range(n)},
        compiler_params=pltpu.CompilerParams(has_side_effects=EFFECT),
    )(*lands, ssem, rsem, after)
    return list(outs)


def _rs_start(name, parts, ks, lands, chip_sums=False):
    n = len(parts)

    def body(*refs):
        part_in, land_in = refs[0:n], refs[n:2 * n]
        ssem, rsem = refs[2 * n], refs[2 * n + 1]
        token = refs[4 * n + 2]
        x, y, c = lax.axis_index("x"), lax.axis_index("y"), lax.axis_index("c")
        kme = 2 * x + y
        me = 4 * x + 2 * y + c
        for p in range(n):
            hrows = land_in[p].shape[1]
            for i, k in enumerate(ks):
                if chip_sums:
                    @pl.when(kme != k)
                    def _():
                        _remote(part_in[p].at[i], land_in[p].at[kme], ssem.at[p], rsem.at[p], (k // 2, k % 2, c)).start()
                    continue
                for pc in range(2):
                    @pl.when(jnp.logical_or(kme != k, c != pc))
                    def _():
                        _remote(part_in[p].at[i, pl.ds(pc * hrows, hrows)], land_in[p].at[me],
                                ssem.at[p], rsem.at[p], (k // 2, k % 2, pc)).start()
        token[...] = jnp.zeros_like(token)

    arrays = list(parts) + list(lands)
    out_shape = ([pltpu.SemaphoreType.DMA((n,)), pltpu.SemaphoreType.DMA((n,))]
                 + [pltpu.HBM(a.shape, a.dtype) for a in arrays] + [jax.ShapeDtypeStruct((8, 128), F32)])
    outs = pl.pallas_call(
        body, name=name, out_shape=out_shape,
        in_specs=[HBM_SPEC] * (2 * n), out_specs=[SEM_SPEC, SEM_SPEC] + [HBM_SPEC] * (2 * n) + [VMEM_SPEC],
        input_output_aliases={i: 2 + i for i in range(2 * n)},
        compiler_params=pltpu.CompilerParams(has_side_effects=EFFECT),
    )(*[pltpu.with_memory_space_constraint(a, pltpu.HBM) for a in arrays])
    return outs[0], outs[1], list(outs[2:2 + n]), list(outs[2 + n:2 + 2 * n]), outs[2 + 2 * n]


def _rs_wait(name, ssem, rsem, parts, ks, lands, after, chip_sums=False):
    n = len(parts)

    def body(*refs):
        part_in, land_in = refs[0:n], refs[n:2 * n]
        ssem_ref, rsem_ref = refs[2 * n], refs[2 * n + 1]
        x, y, c = lax.axis_index("x"), lax.axis_index("y"), lax.axis_index("c")
        kme = 2 * x + y
        for p in range(n):
            piece = land_in[p].at[0]
            for k in ks:
                for pc in range(1 if chip_sums else 2):
                    mine = (kme == k) if chip_sums else jnp.logical_and(kme == k, c == pc)

                    @pl.when(jnp.logical_not(mine))
                    def _():
                        _remote(piece, piece, ssem_ref.at[p], rsem_ref.at[p], (x, y, c)).wait_send()
            owner = kme == ks[0]
            for k in ks[1:]:
                owner = jnp.logical_or(owner, kme == k)

            @pl.when(owner)
            def _():
                others = land_in[p].at[pl.ds(0, land_in[p].shape[0] - 1)]
                _remote(others, others, ssem_ref.at[p], rsem_ref.at[p], (x, y, c)).wait_recv()

    arrays = list(parts) + list(lands)
    outs = pl.pallas_call(
        body, name=name, out_shape=[pltpu.HBM(a.shape, a.dtype) for a in arrays],
        in_specs=[HBM_SPEC] * (2 * n) + [SEM_SPEC, SEM_SPEC, ANY_SPEC], out_specs=[HBM_SPEC] * (2 * n),
        input_output_aliases={i: i for i in range(2 * n)},
        compiler_params=pltpu.CompilerParams(has_side_effects=EFFECT),
    )(*arrays, ssem, rsem, after)
    return list(outs[0:n]), list(outs[n:2 * n])


def _final_reduce(parts, lands, sm_in, sm_tail, sm_a, sm_b, rg_c):
    shapes = [(2 * l.shape[1], l.shape[2]) for l in lands]
    step = 128
    PR = DG // NDEV

    def body(pin, pout, ppg, ppp, l_in, l_out, l_pg, l_pp, in_h, tail_h, a_h, b_h, rg_h,
             g_in_h, g_out_h, g_pg_h, g_pp_h, v_out, rg_out, vbuf, vrecv, rgrecv,
             lb_in, lb_out, lb_pg, lb_pp, ob_in, ob_out, ob_pg, ob_pp,
             in_ref, tail_ref, a_ref, b_ref, rg_ref, g_in, g_out, g_pg, g_pp,
             lsem, ssem_l, osem, bsem_s, bsem_r, ssem, rsem):
        x, y, c = lax.axis_index("x"), lax.axis_index("y"), lax.axis_index("c")
        kme = 2 * x + y
        me = 4 * x + 2 * y + c
        sibling = (x, y, 1 - c)
        small_fetch = [pltpu.make_async_copy(src, dst, ssem_l.at[i]) for i, (src, dst) in enumerate(
            [(in_h, in_ref), (tail_h, tail_ref), (a_h, a_ref), (b_h, b_ref), (rg_h, rg_ref)])]
        for cp in small_fetch:
            cp.start()

        lands_hbm = [l_in, l_out, l_pg, l_pp]
        land_bufs = [lb_in, lb_out, lb_pg, lb_pp]
        own_bufs = [ob_in, ob_out, ob_pg, ob_pp]
        fetches = []
        for p in range(4):
            cp = pltpu.make_async_copy(lands_hbm[p], land_bufs[p], lsem.at[p])
            cp.start()
            fetches.append(cp)

        own_fetches = [pltpu.make_async_copy(pin.at[kme], ob_in, lsem.at[4])]
        own_fetches[0].start()
        for p, part in enumerate([pout, ppg, ppp]):
            hrows = own_bufs[p + 1].shape[0]
            cp = pltpu.make_async_copy(part.at[kme, pl.ds(pl.multiple_of(c * hrows, step), hrows)],
                                       own_bufs[p + 1], lsem.at[5 + p])
            cp.start()
            own_fetches.append(cp)

        def pair(ref, r0, r1):
            return jnp.concatenate([ref[r0:r0 + 1, :], ref[r1:r1 + 1, :]], axis=1)

        for cp in small_fetch:
            cp.wait()
        rows = {
            ROW_NORM_MIX: in_ref[0:1, :], ROW_FINAL_NORM: tail_ref[1:2, :], ROW_B_PG: tail_ref[2:3, :],
            ROW_PLE_NORM: tail_ref[3:4, :], ROW_CB_BA: pair(a_ref, 0, 1), ROW_BX_LAM: pair(a_ref, 2, 3),
            ROW_CW01: pair(a_ref, 4, 5), ROW_CW23: pair(a_ref, 6, 7), ROW_HG_LB: pair(b_ref, 2, 3),
            ROW_HG_NW: jnp.concatenate([b_ref[1:2, :], jnp.zeros((1, DG), F32)], axis=1),
            ROW_LOSS: tail_ref[4:5, :],
        }
        vbuf[...] = jnp.zeros_like(vbuf)
        for r, row in rows.items():
            for j in range(NDEV):
                vbuf[j, r:r + 1, :] = row[:, j * 128:(j + 1) * 128]

        def peer(mask):
            px = x ^ ((mask >> 2) & 1)
            py = y ^ ((mask >> 1) & 1)
            pc = c ^ (mask & 1)
            return (px, py, pc), 4 * px + 2 * py + pc

        def rg_rows(r):
            return pl.ds(pl.multiple_of(r * PR, PR), PR)

        first = []
        for mask in range(1, NDEV):
            dev, r = peer(mask)
            i = mask - 1
            cp = _remote(vbuf.at[r], vrecv.at[i], ssem.at[i], rsem.at[i], dev)
            cp.start()
            first.append(cp)
            cp = _remote(rg_ref.at[rg_rows(r)], rgrecv.at[i], ssem.at[7 + i], rsem.at[7 + i], dev)
            cp.start()
            first.append(cp)

        big = [(lb_in, g_in), (lb_out, g_out), (lb_pg, g_pg), (lb_pp, g_pp)]
        g_hbm = [g_in_h, g_out_h, g_pg_h, g_pp_h]
        swaps, writes = [], []
        for p, (land, gout) in enumerate(big):
            fetches[p].wait()
            own_fetches[p].wait()
            hrows = land.shape[1]
            mine0 = pl.multiple_of(c * hrows, step)
            slot = kme if p == 0 else me
            for r0 in range(0, hrows, step):
                rs = pl.ds(mine0 + r0, step)
                own = own_bufs[p][r0:r0 + step, :]
                s = jnp.zeros((step, land.shape[2]), F32)
                for j in range(land.shape[0]):
                    s = s + jnp.where(slot == j, own, land[j, r0:r0 + step, :]).astype(F32)
                gout[rs, :] = s
            mine = pl.ds(mine0, hrows)
            cp = _remote(gout.at[mine], gout.at[mine], bsem_s.at[p], bsem_r.at[p], sibling)
            cp.start()
            swaps.append(cp)
            cp = pltpu.make_async_copy(gout.at[mine], g_hbm[p].at[mine], osem.at[p])
            cp.start()
            writes.append(cp)

        sv = vbuf[me]
        sr = rg_ref[rg_rows(me), :]
        for i in range(NDEV - 1):
            first[2 * i].wait_recv()
            first[2 * i + 1].wait_recv()
            sv = sv + vrecv[i]
            sr = sr + rgrecv[i]
        v_out[me] = sv
        rg_out[rg_rows(me), :] = sr
        second = []
        for mask in range(1, NDEV):
            dev, r = peer(mask)
            i = mask - 1
            cp = _remote(v_out.at[me], v_out.at[me], ssem.at[14 + i], rsem.at[14 + i], dev)
            cp.start()
            second.append(cp)
            cp = _remote(rg_out.at[rg_rows(me)], rg_out.at[rg_rows(me)], ssem.at[21 + i], rsem.at[21 + i], dev)
            cp.start()
            second.append(cp)
        for p, (land, gout) in enumerate(big):
            hrows = land.shape[1]
            other = pl.ds(pl.multiple_of((1 - c) * hrows, step), hrows)
            _remote(gout.at[other], gout.at[other], bsem_s.at[p], bsem_r.at[p], sibling).wait_recv()
            cp = pltpu.make_async_copy(gout.at[other], g_hbm[p].at[other], osem.at[4 + p])
            cp.start()
            writes.append(cp)
        for mask in range(1, NDEV):
            dev, r = peer(mask)
            i = mask - 1
            _remote(v_out.at[r], v_out.at[r], ssem.at[14 + i], rsem.at[14 + i], dev).wait_recv()
            _remote(rg_out.at[rg_rows(r)], rg_out.at[rg_rows(r)], ssem.at[21 + i], rsem.at[21 + i], dev).wait_recv()
        for cp in writes:
            cp.wait()
        for cp in first + swaps + second:
            cp.wait_send()

    out_shape = [pltpu.HBM(s, F32) for s in shapes]
    out_shape += [jax.ShapeDtypeStruct((NDEV, VROWS, 128), F32), jax.ShapeDtypeStruct((DG, 128), F32)]
    outs = pl.pallas_call(
        body, name="final_reduce", out_shape=out_shape,
        in_specs=[HBM_SPEC] * 13, out_specs=[HBM_SPEC] * 4 + [VMEM_SPEC] * 2,
        scratch_shapes=[pltpu.VMEM((NDEV, VROWS, 128), F32), pltpu.VMEM((NDEV - 1, VROWS, 128), F32),
                        pltpu.VMEM((NDEV - 1, PR, 128), F32)]
        + [pltpu.VMEM(l.shape, BF16) for l in lands]
        + [pltpu.VMEM(l.shape[1:], BF16) for l in lands]
        + [pltpu.VMEM(a.shape, F32) for a in (sm_in, sm_tail, sm_a, sm_b, rg_c)]
        + [pltpu.VMEM(s, F32) for s in shapes]
        + [pltpu.SemaphoreType.DMA((8,)), pltpu.SemaphoreType.DMA((5,)), pltpu.SemaphoreType.DMA((8,)),
                        pltpu.SemaphoreType.DMA((4,)), pltpu.SemaphoreType.DMA((4,)),
                        pltpu.SemaphoreType.DMA((28,)), pltpu.SemaphoreType.DMA((28,))],
        compiler_params=pltpu.CompilerParams(vmem_limit_bytes=48 * MIB),
    )(*[pltpu.with_memory_space_constraint(a, pltpu.HBM)
        for a in (*parts, *lands, sm_in, sm_tail, sm_a, sm_b, rg_c)])
    return list(outs[0:4]), outs[4], outs[5]


def _adam_rows(w, g, m, v):
    m2 = ADAM_B1 * m + (1.0 - ADAM_B1) * g
    v2 = ADAM_B2 * v + (1.0 - ADAM_B2) * (g * g)
    m_hat = m2 / (1.0 - ADAM_B1 ** ADAM_STEP)
    v_hat = v2 / (1.0 - ADAM_B2 ** ADAM_STEP)
    delta = -ADAM_LR * (m_hat / (jnp.sqrt(v_hat) + ADAM_EPS) + ADAM_WD * w)
    return delta, m2, v2


def _adam_big(gs, ws, ms, vs):
    n = len(gs)
    steps = 4

    def body(*refs):
        ins, outs = refs[:4 * n], refs[4 * n:]
        for i in range(n):
            g, w, m, v = (r[...] for r in ins[4 * i:4 * i + 4])
            d, m2, v2 = _adam_rows(w, g, m, v)
            outs[3 * i][...] = d
            outs[3 * i + 1][...] = m2
            outs[3 * i + 2][...] = v2

    in_specs, out_specs, out_shape, args = [], [], [], []
    for g, w, m, v in zip(gs, ws, ms, vs):
        r, c = w.shape
        spec = lambda: pl.BlockSpec((r // steps, c), lambda i: (i, 0))
        in_specs += [spec() for _ in range(4)]
        out_specs += [spec() for _ in range(3)]
        out_shape += [jax.ShapeDtypeStruct((r, c), F32)] * 3
        args += [pltpu.with_memory_space_constraint(a, pltpu.HBM) for a in (g, w, m, v)]
    outs = pl.pallas_call(
        body, name="adam_big", grid=(steps,), in_specs=in_specs, out_specs=out_specs, out_shape=out_shape,
        compiler_params=_cparams(("parallel",), 32),
    )(*args)
    return [tuple(outs[3 * i:3 * i + 3]) for i in range(n)]


_VEC_PARAMS = [
    ("norm_mix_w", ROW_NORM_MIX, 0, D), ("final_norm_w", ROW_FINAL_NORM, 0, D),
    ("b_ple_gate", ROW_B_PG, 0, D), ("ple_norm_w", ROW_PLE_NORM, 0, D),
    ("conv_b", ROW_CB_BA, 0, DG), ("rg_ba", ROW_CB_BA, DG, DG),
    ("rg_bx", ROW_BX_LAM, 0, DG), ("rg_lambda", ROW_BX_LAM, DG, DG),
    ("hg_norm_w", ROW_HG_NW, 0, HD),
]
_SMALL_ORDER = [n for n, _, _, _ in _VEC_PARAMS] + ["hg_lb", "conv_w", "rg_wa", "rg_wx"]


def _adam_small(vred, rgred, ws, ms, vs):
    names = _SMALL_ORDER
    n = len(names)

    def body(vred_ref, rg_ref, *refs):
        w_refs = dict(zip(names, refs[0:n]))
        m_refs = dict(zip(names, refs[n:2 * n]))
        v_refs = dict(zip(names, refs[2 * n:3 * n]))
        outs = refs[3 * n:]
        o_refs = {nm: outs[4 * i:4 * i + 4] for i, nm in enumerate(names)}
        kme = 2 * lax.axis_index("x") + lax.axis_index("y")

        def update(nm, g, idx):
            d, m2, v2 = _adam_rows(w_refs[nm][idx], g, m_refs[nm][idx], v_refs[nm][idx])
            og, od, om, ov = o_refs[nm]
            og[idx] = g
            od[idx] = d
            om[idx] = m2
            ov[idx] = v2

        def packed(row, lane0, width):
            return jnp.concatenate([vred_ref[j, row:row + 1, :] for j in range(lane0 // 128, (lane0 + width) // 128)],
                                   axis=1)

        everything = (slice(None), slice(None))
        for nm, row, lane0, width in _VEC_PARAMS:
            update(nm, packed(row, lane0, width), everything)
        for r in range(2):
            update("hg_lb", packed(ROW_HG_LB, r * DG, DG), (slice(r, r + 1), slice(None)))
        for j in range(4):
            g = vred_ref[(j % 2) * 4 + kme, ROW_CW01 + j // 2:ROW_CW01 + j // 2 + 1, :]
            update("conv_w", g, (slice(j, j + 1), slice(None)))
        for r0 in range(0, DG, 128):
            rs = (slice(r0, r0 + 128), slice(None))
            both = rg_ref[r0:r0 + 128, :]
            update("rg_wa", both[:, 0:RGB], rs)
            update("rg_wx", pltpu.roll(both, RGB, 1)[:, 0:RGB], rs)

    args = [vred, rgred] + [d[nm] for d in (ws, ms, vs) for nm in names]
    out_shape = []
    for nm in names:
        out_shape += [jax.ShapeDtypeStruct(ws[nm].shape, F32)] * 4
    whole = lambda s: pl.BlockSpec(s.shape, lambda i, nd=len(s.shape): (0,) * nd)
    outs = pl.pallas_call(
        body, name="adam_small", out_shape=out_shape, grid=(1,),
        in_specs=[whole(a) for a in args], out_specs=[whole(s) for s in out_shape],
    )(*args)
    return {nm: tuple(outs[4 * i:4 * i + 4]) for i, nm in enumerate(names)}


def _conv_rows(cw_ref):
    return [jnp.concatenate([cw_ref[k, j:j + 1, :] for k in range(NSHARD)], axis=1) for j in range(4)]


def _rg_conv(xa, prev8, cw, cb, rows):
    taps = [_shift_down(xa, prev8, 3, rows), _shift_down(xa, prev8, 2, rows),
            _shift_down(xa, prev8, 1, rows), xa]
    xc = cb
    for j in range(4):
        xc = xc + taps[j] * cw[j]
    return xc, taps


def _block_mask():
    r = lax.broadcasted_iota(I32, (DG, DG), 0)
    c = lax.broadcasted_iota(I32, (DG, DG), 1)
    return (r >> 6) == (c >> 6)


def _dense_from_blocks(wc):
    j = lax.broadcasted_iota(I32, (RGB, DG), 0)
    c = lax.broadcasted_iota(I32, (RGB, DG), 1)
    spread = _mm_exact(wc, ((c & (RGB - 1)) == j).astype(F32))
    return jnp.where(_block_mask(), spread, 0.0)


def _blocks_from_dense(da, dx):
    c = lax.broadcasted_iota(I32, (DG, 128), 0)
    j = lax.broadcasted_iota(I32, (DG, 128), 1)
    hit = (c & (RGB - 1)) == (j & (RGB - 1))
    mask = _block_mask()
    return (_mm_exact(jnp.where(mask, da, 0.0), (hit & (j < RGB)).astype(F32))
            + _mm_exact(jnp.where(mask, dx, 0.0), (hit & (j >= RGB)).astype(F32)))


def _rg_gates(xc, wa, ba, wx, bx, sp, first_row):
    r = _sigmoid(_mm(xc, wa) + ba)
    i = _sigmoid(_mm(xc, wx) + bx)
    log_a = (-RG_C) * r * sp
    a = jnp.exp(log_a)
    a2 = a * a
    one_m_a2 = -jnp.tanh(log_a) * (a2 + 1.0)
    mult = jnp.where(first_row, 1.0, jnp.sqrt(one_m_a2))
    return r, i, a, a2, mult


def _softplus(z):
    return jnp.maximum(z, 0.0) + jnp.log1p(jnp.exp(-jnp.abs(z)))


def _fwd_rglru(pf, pb, cw3, conv_b, wa_c, ba, wx_c, bx, lam, dep):
    tm = 512
    ng = tm // 8

    def body(xa_ref, ga_ref, cw_ref, cb_ref, wa_ref, ba_ref, wx_ref, bx_ref, lam_ref, dep_ref,
             h_ref, ya_ref, a_s, u_s, tail_s, hc_s, wa_s, wx_s):
        i = pl.program_id(0)

        @pl.when(i == 0)
        def _():
            tail_s[...] = jnp.zeros_like(tail_s)
            hc_s[...] = jnp.zeros_like(hc_s)
            wa_s[...] = _dense_from_blocks(wa_ref[...]).astype(BF16)
            wx_s[...] = _dense_from_blocks(wx_ref[...]).astype(BF16)

        rows = lax.broadcasted_iota(I32, (tm, DG), 0)
        xa = xa_ref[...]
        xc, _ = _rg_conv(xa, tail_s[...], _conv_rows(cw_ref), cb_ref[...], rows)
        tail_s[...] = xa[tm - 8:tm, :]
        sp = _softplus(-lam_ref[...])
        rp = _mm(xc, wa_s[...]) + ba_ref[...]
        ip = _mm(xc, wx_s[...]) + bx_ref[...]
        rb = 64
        rows_b = lax.broadcasted_iota(I32, (rb, DG), 0)
        rows8 = rows_b & 7
        carry = hc_s[0:1, :]
        for b0 in range(0, tm, rb):
            sl = slice(b0, b0 + rb)
            r = _sigmoid(rp[sl])
            ig = _sigmoid(ip[sl])
            log_a = (-RG_C) * r * sp
            av = jnp.exp(log_a)
            mult = jnp.sqrt(-jnp.tanh(log_a) * (av * av + 1.0))
            if b0 == 0:
                mult = jnp.where((rows_b + i * tm) == 0, 1.0, mult)
            uv = mult * (ig * xc[sl])
            for d in (1, 2, 4):
                keep = rows8 >= d
                uv = uv + av * jnp.where(keep, _roll_in_groups(uv, d), 0.0)
                av = av * jnp.where(keep, _roll_in_groups(av, d), 1.0)
            ga = ga_ref[sl, :].astype(F32)
            gate = ga * _sigmoid(ga)
            hs = []
            for g in range(rb // 8):
                gs = slice(g * 8, (g + 1) * 8)
                hv = uv[gs] + av[gs] * carry
                carry = hv[7:8, :]
                hs.append(hv)
            hb = jnp.concatenate(hs, axis=0)
            h_ref[sl, :] = hb
            ya_ref[sl, :] = (hb * gate).astype(BF16)
        hc_s[0:1, :] = carry

    vec = lambda: pl.BlockSpec((1, DG), lambda i: (0, 0))
    blocks = lambda: pl.BlockSpec((DG, RGB), lambda i: (0, 0))
    return pl.pallas_call(
        body, name="fwd_rglru", grid=(T // tm,),
        in_specs=[pl.BlockSpec((tm, DG), lambda i: (i, 0)),
                  pl.BlockSpec((tm, DG), lambda i: (i, 0)),
                  pl.BlockSpec((NSHARD, 4, 128), lambda i: (0, 0, 0)), vec(),
                  blocks(), vec(), blocks(), vec(), vec(), ANY_SPEC],
        out_specs=[pl.BlockSpec((tm, DG), lambda i: (i, 0)),
                   pl.BlockSpec((tm, DG), lambda i: (i, 0))],
        out_shape=[jax.ShapeDtypeStruct((T, DG), F32), pltpu.HBM((T, DG), BF16)],
        scratch_shapes=[pltpu.VMEM((tm, DG), F32), pltpu.VMEM((tm, DG), F32),
                        pltpu.VMEM((8, DG), F32), pltpu.VMEM((8, DG), F32),
                        pltpu.VMEM((DG, DG), BF16), pltpu.VMEM((DG, DG), BF16)],
        compiler_params=_cparams(("arbitrary",), 48),
    )(pf, pb, cw3, conv_b, wa_c, ba, wx_c, bx, lam, dep)


def _hg_lower_bound(lb_ref):
    return _sig_pair(lb_ref[0:1, :] - lb_ref[1:2, :])


def _hg_gates(fz, lb, one_m_lb):
    sg, sn = _sig_pair(fz)
    f = lb + one_m_lb * sg
    return sg, sn, f, jnp.log(f), one_m_lb * sn


def _tri(lower):
    r = lax.broadcasted_iota(I32, (CH, CH), 0)
    c = lax.broadcasted_iota(I32, (CH, CH), 1)
    return (r >= c) if lower else (r <= c)


def _split3(v):
    hi = v.astype(BF16)
    r1 = v - hi.astype(F32)
    mid = r1.astype(BF16)
    lo = (r1 - mid.astype(F32)).astype(BF16)
    return hi, mid, lo


def _chunk_cumsum(v, rows64, reverse=False):
    del rows64
    tri = _tri(not reverse).astype(BF16)
    out = []
    for c in range(v.shape[0] // CH):
        pieces = _split3(v[c * CH:(c + 1) * CH])
        out.append(sum(jnp.dot(tri, p, preferred_element_type=F32) for p in pieces))
    return jnp.concatenate(out, axis=0)


def _chunk_rev_cumsum(v, rows64):
    return _chunk_cumsum(v, rows64, reverse=True)


def _hg_recompute(q_ref, f_ref, lb, one_m_lb, rows64, eb_s, enb_s, ekd_s, qe_s, ke_s, kd_s, dec_s):
    nc = q_ref.shape[0] // CH
    sg, sn, f, logf, k = _hg_gates(f_ref[...], lb, one_m_lb)
    q = q_ref[...].astype(F32)
    sq = _sigmoid(q)
    qs = q * sq * (HD ** -0.5)
    b = _chunk_cumsum(logf, rows64)
    for c in range(nc):
        rs = slice(c * CH, (c + 1) * CH)
        b_c = b[rs]
        bl = b_c[CH - 1:CH, :]
        eb, enb, ekd = jnp.exp(b_c), jnp.exp(-b_c), jnp.exp(bl - b_c)
        if eb_s is not None:
            eb_s[rs, :] = eb
            enb_s[rs, :] = enb
            ekd_s[rs, :] = ekd
        qe_s[rs, :] = (qs[rs] * eb).astype(BF16)
        ke_s[rs, :] = (k[rs] * enb).astype(BF16)
        kd_s[rs, :] = (k[rs] * ekd).astype(BF16)
        dec_s[c:c + 1, :] = jnp.exp(bl)
    return sg, sn, f, k, q, sq, qs


def _fwd_hgrn2(pf, pb, hg_lb, hg_nw):
    tm = 512
    nc = tm // CH

    def body(q_ref, f_ref, v_ref, g_ref, lb_ref, nw_ref, yb_ref, o_ref, sp_ref,
             st_s, qe_s, ke_s, kd_s, vb_s, dec_s, p_s, ds_s):
        i = pl.program_id(0)

        @pl.when(i == 0)
        def _():
            st_s[...] = jnp.zeros_like(st_s)

        lb, one_m_lb = _hg_lower_bound(lb_ref)
        rows64 = lax.broadcasted_iota(I32, (tm, DG), 0) & (CH - 1)
        _hg_recompute(q_ref, f_ref, lb, one_m_lb, rows64, None, None, None, qe_s, ke_s, kd_s, dec_s)
        vb_s[...] = v_ref[...]
        mask = _tri(True)
        items = [(c, hd, slice(c * CH, (c + 1) * CH), slice(hd * HD, (hd + 1) * HD))
                 for c in range(nc) for hd in range(NH)]
        for c, hd, rs, cols in items:
            p_s[c * NH + hd] = jnp.where(mask, _mm_nt(qe_s[rs, cols], ke_s[rs, cols]), 0.0).astype(BF16)
            ds_s[c * NH + hd] = _mm_tn(vb_s[rs, cols], kd_s[rs, cols])
        for c, hd, rs, cols in items:
            st = st_s[hd]
            sp_ref[hd, c] = st
            st_s[hd] = st * dec_s[c:c + 1, cols] + ds_s[c * NH + hd]
        for c, hd, rs, cols in items:
            o_ref[rs, cols] = _mm(p_s[c * NH + hd], vb_s[rs, cols]) + _mm_nt(qe_s[rs, cols], sp_ref[hd, c])
        nw = nw_ref[...]
        for hd in range(NH):
            cols = slice(hd * HD, (hd + 1) * HD)
            o = o_ref[:, cols]
            so = lax.rsqrt(jnp.mean(o * o, axis=-1, keepdims=True) + EPS)
            g = g_ref[:, cols].astype(F32)
            sg = _sigmoid(g)
            yb_ref[:, cols] = (o * so * nw * (g * sg)).astype(BF16)

    col = lambda j: pl.BlockSpec((tm, DG), lambda i: (i, j))
    return pl.pallas_call(
        body, name="fwd_hgrn2", grid=(T // tm,),
        in_specs=[col(1), col(1), col(2), col(3),
                  pl.BlockSpec((2, DG), lambda i: (0, 0)),
                  pl.BlockSpec((1, HD), lambda i: (0, 0))],
        out_specs=[pl.BlockSpec((tm, DG), lambda i: (i, 0)),
                   pl.BlockSpec((tm, DG), lambda i: (i, 0)),
                   pl.BlockSpec((NH, nc, HD, HD), lambda i: (0, i, 0, 0))],
        out_shape=[pltpu.HBM((T, DG), BF16), jax.ShapeDtypeStruct((T, DG), F32),
                   jax.ShapeDtypeStruct((NH, NCHUNK, HD, HD), F32)],
        scratch_shapes=[pltpu.VMEM((NH, HD, HD), F32),
                        pltpu.VMEM((tm, DG), BF16), pltpu.VMEM((tm, DG), BF16), pltpu.VMEM((tm, DG), BF16),
                        pltpu.VMEM((tm, DG), BF16), pltpu.VMEM((nc, DG), F32),
                        pltpu.VMEM((nc * NH, CH, CH), BF16), pltpu.VMEM((nc * NH, HD, HD), F32)],
        compiler_params=_cparams(("arbitrary",), 48),
    )(pb, pf, pb, pb, hg_lb, hg_nw)


def _tail_fwd_bwd(x, p, tgt, ya, yb, w_out_b, w_pg_b, w_pp_b, ple_nw, b_pg, fnw):
    tm = 512
    nt = T // tm
    QR = D // NSHARD

    def body(x_ref, p_ref, t_ref, ya_ref, yb_ref, wo_ref, wg_ref, wp_ref, pw_ref, b_ref, fw_ref,
             dh1_ref, dyab_ref, dwo_ref, dwg_ref, dwp_ref, sm_ref, dwo_s, dwg_s, dwp_s):
        i = pl.program_id(0)

        @pl.when(i == 0)
        def _():
            dwo_s[...] = jnp.zeros_like(dwo_s)
            dwg_s[...] = jnp.zeros_like(dwg_s)
            dwp_s[...] = jnp.zeros_like(dwp_s)
            sm_ref[...] = jnp.zeros_like(sm_ref)

        ya = ya_ref[...]
        yb = yb_ref[...]
        pv = p_ref[...].astype(BF16)
        pw = pw_ref[...]
        fw = fw_ref[...]
        h1 = x_ref[...] + _mm(ya, wo_ref[0:DG, :]) + _mm(yb, wo_ref[DG:D, :])
        s2 = lax.rsqrt(jnp.mean(h1 * h1, axis=-1, keepdims=True) + EPS)
        n2h = h1 * s2
        n2 = (n2h * pw).astype(BF16)
        z = _mm(n2, wg_ref[...]) + b_ref[...]
        gate = _sigmoid(z)
        pp = jnp.concatenate([_mm(pv, wp_ref[k]) for k in range(NSHARD)], axis=1)
        h2 = h1 + gate * pp
        s3 = lax.rsqrt(jnp.mean(h2 * h2, axis=-1, keepdims=True) + EPS)
        hn = h2 * s3
        err = hn * fw - t_ref[...]
        sm_ref[0:1, :] += _rowsum(err * err)
        dy = err * (1.0 / D)
        sm_ref[1:2, :] += _rowsum(dy * hn)
        g3 = dy * fw
        dh2 = s3 * (g3 - hn * jnp.mean(g3 * hn, axis=-1, keepdims=True))
        dpp = (dh2 * gate).astype(BF16)
        dz = dh2 * pp * gate * (1.0 - gate)
        sm_ref[2:3, :] += _rowsum(dz)
        dzb = dz.astype(BF16)
        dwg_s[...] += _mm_tn(n2, dzb)
        dn2 = _mm_nt(dzb, wg_ref[...])
        for k in range(NSHARD):
            dwp_s[k] += _mm_tn(pv, dpp[:, k * PLE:(k + 1) * PLE])
        sm_ref[3:4, :] += _rowsum(dn2 * n2h)
        g2 = dn2 * pw
        dh1 = dh2 + s2 * (g2 - n2h * jnp.mean(g2 * n2h, axis=-1, keepdims=True))
        dh1_ref[...] = dh1
        dh1b = dh1.astype(BF16)
        dyab_ref[...] = _mm_nt(dh1b, wo_ref[...])
        dwo_s[0:DG, :] += _mm_tn(ya, dh1b)
        dwo_s[DG:D, :] += _mm_tn(yb, dh1b)

        @pl.when(i == nt - 1)
        def _():
            total = jnp.sum(sm_ref[0:1, :], axis=-1, keepdims=True) * (0.5 / D)
            sm_ref[4:5, :] = jnp.broadcast_to(total, (1, D))
            for k in range(NSHARD):
                dwo_ref[k] = dwo_s[k * QR:(k + 1) * QR, :].astype(BF16)
                dwg_ref[k] = dwg_s[k * QR:(k + 1) * QR, :].astype(BF16)
                dwp_ref[k] = dwp_s[k].astype(BF16)

    row = lambda w: pl.BlockSpec((tm, w), lambda i: (i, 0))
    const2 = lambda s: pl.BlockSpec(s, lambda i: (0, 0), pipeline_mode=pl.Buffered(1))
    const3 = lambda s: pl.BlockSpec(s, lambda i: (0, 0, 0), pipeline_mode=pl.Buffered(1))
    return pl.pallas_call(
        body, name="tail_fwd_bwd", grid=(nt,),
        in_specs=[row(D), row(PLE), row(D), row(DG), row(DG),
                  const2((D, D)), const2((D, D)), const3((NSHARD, PLE, PLE)),
                  const2((1, D)), const2((1, D)), const2((1, D))],
        out_specs=[row(D), row(D), const3((NSHARD, QR, D)), const3((NSHARD, QR, D)),
                   const3((NSHARD, PLE, PLE)), const2((8, D))],
        out_shape=[jax.ShapeDtypeStruct((T, D), F32), jax.ShapeDtypeStruct((T, D), F32),
                   jax.ShapeDtypeStruct((NSHARD, QR, D), BF16), jax.ShapeDtypeStruct((NSHARD, QR, D), BF16),
                   jax.ShapeDtypeStruct((NSHARD, PLE, PLE), BF16), jax.ShapeDtypeStruct((8, D), F32)],
        scratch_shapes=[pltpu.VMEM((D, D), F32), pltpu.VMEM((D, D), F32), pltpu.VMEM((NSHARD, PLE, PLE), F32)],
        compiler_params=_cparams(("arbitrary",), 62),
    )(x, p, tgt, pltpu.with_memory_space_constraint(ya, pltpu.HBM), pltpu.with_memory_space_constraint(yb, pltpu.HBM),
      w_out_b, w_pg_b, w_pp_b, ple_nw, b_pg, fnw)


def _bwd_rglru(pf, pb, h, dyab, cw3, conv_b, wa_c, ba, wx_c, bx, lam, dep):
    tm = 512
    nt = T // tm
    ng = tm // 8

    def body(xa_ref, ga_ref, xp_ref, h_ref, hp_ref, dya_ref, cw_ref, cb_ref, wa_ref, ba_ref, wx_ref, bx_ref,
             lam_ref, dep_ref, da_ref, rg_ref, sm_ref, a_s, g_s, cg_s, nxt_s, wa_s, wx_s, dwa_s, dwx_s):
        i = pl.program_id(0)
        tile = nt - 1 - i

        @pl.when(i == 0)
        def _():
            dwa_s[...] = jnp.zeros_like(dwa_s)
            dwx_s[...] = jnp.zeros_like(dwx_s)
            sm_ref[...] = jnp.zeros_like(sm_ref)
            cg_s[...] = jnp.zeros_like(cg_s)
            nxt_s[...] = jnp.zeros_like(nxt_s)
            wa_s[...] = _dense_from_blocks(wa_ref[...]).astype(BF16)
            wx_s[...] = _dense_from_blocks(wx_ref[...]).astype(BF16)

        rows = lax.broadcasted_iota(I32, (tm, DG), 0)
        has_prev = tile > 0
        xa = xa_ref[...]
        xprev = jnp.where(has_prev, xp_ref[...], 0.0)
        cw = _conv_rows(cw_ref)
        xc, taps = _rg_conv(xa, xprev, cw, cb_ref[...], rows)
        lam_v = lam_ref[...]
        sp = _softplus(-lam_v)
        first_row = (rows + tile * tm) == 0
        r, ig, a, a2, mult = _rg_gates(xc, wa_s[...], ba_ref[...], wx_s[...], bx_ref[...], sp, first_row)
        hv = h_ref[...]
        hprev = jnp.where(has_prev, hp_ref[...], 0.0)
        h_m1 = _shift_down(hv, hprev, 1, rows)
        ga = ga_ref[...].astype(F32)
        sg = _sigmoid(ga)
        dya = dya_ref[...]
        dga = dya * hv * (sg * (1.0 + ga * (1.0 - sg)))

        av = jnp.where(rows == tm - 1, 1.0, pltpu.roll(a, tm - 1, 0))
        gv = dya * (ga * sg)
        rows8 = rows & 7
        for d in (1, 2, 4):
            keep = rows8 < 8 - d
            gv = gv + av * jnp.where(keep, _roll_in_groups(gv, 8 - d), 0.0)
            av = av * jnp.where(keep, _roll_in_groups(av, 8 - d), 1.0)
        a_s[...] = av
        g_s[...] = gv
        carry = cg_s[0:1, :]
        for g in range(ng - 1, -1, -1):
            sl = slice(g * 8, (g + 1) * 8)
            ab, gb = a_s[sl, :], g_s[sl, :]
            g_s[sl, :] = gb + ab * carry
            carry = gb[0:1, :] + ab[0:1, :] * carry
        cg_s[0:1, :] = a[0:1, :] * carry

        gt = g_s[...]
        da = gt * h_m1
        ixc = ig * xc
        di = gt * mult * xc
        dxc = gt * mult * ig
        dlog_a = da * a + jnp.where(first_row, 0.0, gt * ixc * (-a2 / mult))
        sm_ref[3:4, :] += _rowsum(dlog_a * ((-RG_C) * r))
        dpr_f = dlog_a * ((-RG_C) * sp) * r * (1.0 - r)
        dpi_f = di * ig * (1.0 - ig)
        sm_ref[1:2, :] += _rowsum(dpr_f)
        sm_ref[2:3, :] += _rowsum(dpi_f)
        dpr = dpr_f.astype(BF16)
        dpi = dpi_f.astype(BF16)
        xcb = xc.astype(BF16)
        dwa_s[...] += _mm_tn(xcb, dpr)
        dwx_s[...] += _mm_tn(xcb, dpi)
        dxc = dxc + _mm_nt(dpr, wa_s[...]) + _mm_nt(dpi, wx_s[...])
        sm_ref[0:1, :] += _rowsum(dxc)
        for j in range(4):
            sm_ref[4 + j:5 + j, :] += _rowsum(dxc * taps[j])
        nxt = nxt_s[...]
        dxa = (dxc * cw[3] + _shift_up(dxc, nxt, 1, rows) * cw[2]
               + _shift_up(dxc, nxt, 2, rows) * cw[1] + _shift_up(dxc, nxt, 3, rows) * cw[0])
        nxt_s[...] = dxc[0:8, :]
        da_ref[:, 0:DG] = dxa.astype(BF16)
        da_ref[:, DG:D] = dga.astype(BF16)

        @pl.when(i == nt - 1)
        def _():
            _, s_neg = _sig_pair(lam_v)
            sm_ref[3:4, :] = sm_ref[3:4, :] * (-s_neg)
            rg_ref[...] = _blocks_from_dense(dwa_s[...], dwx_s[...])

    vec = lambda: pl.BlockSpec((1, DG), lambda i: (0, 0))
    blocks = lambda: pl.BlockSpec((DG, RGB), lambda i: (0, 0))
    prev8 = lambda: pl.BlockSpec((8, DG), lambda i: (jnp.maximum((nt - 1 - i) * (tm // 8) - 1, 0), 0))
    return pl.pallas_call(
        body, name="bwd_rglru", grid=(nt,),
        in_specs=[pl.BlockSpec((tm, DG), lambda i: (nt - 1 - i, 0)),
                  pl.BlockSpec((tm, DG), lambda i: (nt - 1 - i, 0)),
                  prev8(),
                  pl.BlockSpec((tm, DG), lambda i: (nt - 1 - i, 0)),
                  prev8(),
                  pl.BlockSpec((tm, DG), lambda i: (nt - 1 - i, 0)),
                  pl.BlockSpec((NSHARD, 4, 128), lambda i: (0, 0, 0)), vec(),
                  blocks(), vec(), blocks(), vec(), vec(), ANY_SPEC],
        out_specs=[pl.BlockSpec((tm, D), lambda i: (nt - 1 - i, 0)),
                   pl.BlockSpec((DG, 128), lambda i: (0, 0)),
                   pl.BlockSpec((8, DG), lambda i: (0, 0))],
        out_shape=[jax.ShapeDtypeStruct((T, D), BF16), jax.ShapeDtypeStruct((DG, 128), F32),
                   jax.ShapeDtypeStruct((8, DG), F32)],
        scratch_shapes=[pltpu.VMEM((tm, DG), F32), pltpu.VMEM((tm, DG), F32),
                        pltpu.VMEM((8, DG), F32), pltpu.VMEM((8, DG), F32),
                        pltpu.VMEM((DG, DG), BF16), pltpu.VMEM((DG, DG), BF16),
                        pltpu.VMEM((DG, DG), F32), pltpu.VMEM((DG, DG), F32)],
        compiler_params=_cparams(("arbitrary",), 56),
    )(pf, pb, pf, h, h, dyab, cw3, conv_b, wa_c, ba, wx_c, bx, lam, dep)


def _bwd_hgrn2(pf, pb, o, s_prev, dyab, hg_lb, hg_nw, dep):
    tm = 512
    nt = T // tm
    nc = tm // CH

    def body(q_ref, f_ref, v_ref, g_ref, o_ref, sp_ref, dy_ref, lb_ref, nw_ref, dep_ref, db_ref, sm_ref,
             dst_s, eb_s, enb_s, ekd_s, qe_s, ke_s, kd_s, vb_s, do_s, dec_s, ddec_s, p_s, dp_s,
             g_s, dsta_s, dva_s, dqe_s, dke_s, dkd_s, dlf_s):
        i = pl.program_id(0)

        @pl.when(i == 0)
        def _():
            sm_ref[...] = jnp.zeros_like(sm_ref)
            dst_s[...] = jnp.zeros_like(dst_s)

        lb, one_m_lb = _hg_lower_bound(lb_ref)
        rows64 = lax.broadcasted_iota(I32, (tm, DG), 0) & (CH - 1)
        sg, sn, f, k, q, sq, qs = _hg_recompute(
            q_ref, f_ref, lb, one_m_lb, rows64, eb_s, enb_s, ekd_s, qe_s, ke_s, kd_s, dec_s)
        vb_s[...] = v_ref[...]

        nw = nw_ref[...]
        for hd in range(NH):
            cols = slice(hd * HD, (hd + 1) * HD)
            g = g_ref[:, cols].astype(F32)
            sgg = _sigmoid(g)
            o = o_ref[:, cols]
            so = lax.rsqrt(jnp.mean(o * o, axis=-1, keepdims=True) + EPS)
            oh = o * so
            dyb = dy_ref[:, cols]
            db_ref[:, 3 * DG + hd * HD:3 * DG + (hd + 1) * HD] = (
                dyb * (oh * nw) * (sgg * (1.0 + g * (1.0 - sgg)))).astype(BF16)
            don = dyb * (g * sgg)
            sm_ref[1:2, 0:HD] += _rowsum(don * oh)
            gw = don * nw
            do_s[:, cols] = (so * (gw - oh * jnp.mean(gw * oh, axis=-1, keepdims=True))).astype(BF16)

        mask = _tri(True)
        items = [(c, hd, slice(c * CH, (c + 1) * CH), slice(hd * HD, (hd + 1) * HD))
                 for c in range(nc) for hd in range(NH)]
        for c, hd, rs, cols in items:
            p_s[c * NH + hd] = jnp.where(mask, _mm_nt(qe_s[rs, cols], ke_s[rs, cols]), 0.0).astype(BF16)
            dp_s[c * NH + hd] = jnp.where(mask, _mm_nt(do_s[rs, cols], vb_s[rs, cols]), 0.0).astype(BF16)
        for c, hd, rs, cols in items:
            n = c * NH + hd
            dva_s[rs, cols] = _mm_tn(p_s[n], do_s[rs, cols])
            dqe_s[rs, cols] = _mm(dp_s[n], ke_s[rs, cols])
            dke_s[rs, cols] = _mm_tn(dp_s[n], qe_s[rs, cols])
            g_s[n] = _mm_tn(do_s[rs, cols], qe_s[rs, cols])
        for c, hd, rs, cols in reversed(items):
            n = c * NH + hd
            dst = dst_s[hd]
            dsta_s[n] = dst
            dst_s[hd] = dst * dec_s[c:c + 1, cols] + g_s[n]
        for c, hd, rs, cols in items:
            n = c * NH + hd
            dst = dsta_s[n]
            st_prev = sp_ref[hd, c]
            dv = dva_s[rs, cols] + _mm_nt(kd_s[rs, cols], dst)
            db_ref[rs, 2 * DG + hd * HD:2 * DG + (hd + 1) * HD] = dv.astype(BF16)
            dqe_s[rs, cols] += _mm(do_s[rs, cols], st_prev)
            dkd_s[rs, cols] = _mm(vb_s[rs, cols], dst)
            ddec_s[c:c + 1, cols] = _rowsum(dst * st_prev)

        eb, enb, ekd = eb_s[...], enb_s[...], ekd_s[...]
        dqe, dke, dkd = dqe_s[...], dke_s[...], dkd_s[...]
        t_kd = dkd * (k * ekd)
        rc = _chunk_rev_cumsum(dqe * (qs * eb) - dke * (k * enb) - t_kd, rows64)
        for c in range(nc):
            rs = slice(c * CH, (c + 1) * CH)
            dbl = _rowsum(t_kd[rs]) + ddec_s[c:c + 1, :] * dec_s[c:c + 1, :]
            dlf_s[rs, :] = rc[rs] + dbl
        t = dlf_s[...] / f - (dke * enb + dkd * ekd)
        db_ref[:, DG:2 * DG] = (one_m_lb * sg * sn * t).astype(BF16)
        sm_ref[0:1, :] += _rowsum(sn * t)
        db_ref[:, 0:DG] = (dqe * eb * (sq * (1.0 + q * (1.0 - sq))) * (HD ** -0.5)).astype(BF16)

        @pl.when(i == nt - 1)
        def _():
            dsm = sm_ref[0:1, :] * (lb * one_m_lb)
            sm_ref[2:3, :] = dsm
            sm_ref[3:4, :] = -dsm

    col = lambda j: pl.BlockSpec((tm, DG), lambda i: (nt - 1 - i, j))
    big = lambda dt: pltpu.VMEM((tm, DG), dt)
    return pl.pallas_call(
        body, name="bwd_hgrn2", grid=(nt,),
        in_specs=[col(1), col(1), col(2), col(3),
                  pl.BlockSpec((tm, DG), lambda i: (nt - 1 - i, 0)),
                  pl.BlockSpec((NH, nc, HD, HD), lambda i: (0, nt - 1 - i, 0, 0)),
                  pl.BlockSpec((tm, DG), lambda i: (nt - 1 - i, 1)),
                  pl.BlockSpec((2, DG), lambda i: (0, 0)),
                  pl.BlockSpec((1, HD), lambda i: (0, 0)), ANY_SPEC],
        out_specs=[pl.BlockSpec((tm, 4 * DG), lambda i: (nt - 1 - i, 0)),
                   pl.BlockSpec((8, DG), lambda i: (0, 0))],
        out_shape=[jax.ShapeDtypeStruct((T, 4 * DG), BF16), jax.ShapeDtypeStruct((8, DG), F32)],
        scratch_shapes=[pltpu.VMEM((NH, HD, HD), F32),
                        big(F32), big(F32), big(F32),
                        big(BF16), big(BF16), big(BF16), big(BF16), big(BF16),
                        pltpu.VMEM((nc, DG), F32), pltpu.VMEM((nc, DG), F32),
                        pltpu.VMEM((nc * NH, CH, CH), BF16), pltpu.VMEM((nc * NH, CH, CH), BF16),
                        pltpu.VMEM((nc * NH, HD, HD), F32), pltpu.VMEM((nc * NH, HD, HD), F32),
                        big(F32), big(F32), big(F32), big(F32), big(F32)],
        compiler_params=_cparams(("arbitrary",), 56),
    )(pb, pf, pb, pb, o, s_prev, dyab, hg_lb, hg_nw, dep)


def _dproj_pieces(k, da_ref, db_ref):
    if k == 0:
        return [(da_ref[:, 0:SHW], 0)]
    if k == 1:
        return [(da_ref[:, SHW:D], 0), (db_ref[:, 0:DG], D - SHW)]
    if k == 2:
        return [(db_ref[:, DG:DG + SHW], 0)]
    return [(db_ref[:, DG + SHW:4 * DG], 0)]


def _bwd_inproj_dx(x, dh1, d_a, d_b, w_in_b, nw, dep):
    tm = 512
    nt = T // tm

    def body(x_ref, dh1_ref, da_ref, db_ref, w_ref, nw_ref, dep_ref, dx_ref, sm_ref):
        i = pl.program_id(0)

        @pl.when(i == 0)
        def _():
            sm_ref[...] = jnp.zeros_like(sm_ref)

        du = None
        for k in range(NSHARD):
            for val, off in _dproj_pieces(k, da_ref, db_ref):
                t = _mm_nt(val, w_ref[k, :, off:off + val.shape[1]])
                du = t if du is None else du + t
        xv = x_ref[...]
        s = lax.rsqrt(jnp.mean(xv * xv, axis=-1, keepdims=True) + EPS)
        xh = xv * s
        sm_ref[0:1, :] += _rowsum(du * xh)
        g = du * nw_ref[...]
        dx_ref[...] = dh1_ref[...] + s * (g - xh * jnp.mean(g * xh, axis=-1, keepdims=True))

    row = lambda w: pl.BlockSpec((tm, w), lambda i: (i, 0))
    return pl.pallas_call(
        body, name="bwd_inproj_dx", grid=(nt,),
        in_specs=[row(D), row(D), row(D), row(4 * DG),
                  pl.BlockSpec((NSHARD, D, SHW), lambda i: (0, 0, 0), pipeline_mode=pl.Buffered(1)),
                  pl.BlockSpec((1, D), lambda i: (0, 0)), ANY_SPEC],
        out_specs=[row(D), pl.BlockSpec((8, D), lambda i: (0, 0))],
        out_shape=[jax.ShapeDtypeStruct((T, D), F32), jax.ShapeDtypeStruct((8, D), F32)],
        compiler_params=_cparams(("arbitrary",), 56),
    )(x, dh1, d_a, d_b, w_in_b, nw, dep)


def _bwd_inproj_dw23(u_b, d_b):
    tm = 1024
    nt = T // tm

    def body(u_ref, db_ref, dw_ref, acc):
        i = pl.program_id(0)

        @pl.when(i == 0)
        def _():
            acc[...] = jnp.zeros_like(acc)

        u = u_ref[...]
        for j, k in enumerate((2, 3)):
            for val, off in _dproj_pieces(k, None, db_ref):
                acc[j, :, off:off + val.shape[1]] += _mm_tn(u, val)

        @pl.when(i == nt - 1)
        def _():
            for j in range(2):
                for r0 in range(0, D, 256):
                    dw_ref[j, r0:r0 + 256, :] = acc[j, r0:r0 + 256, :].astype(BF16)

    row = lambda w: pl.BlockSpec((tm, w), lambda i: (i, 0))
    return pl.pallas_call(
        body, name="bwd_inproj_dw23", grid=(nt,), in_specs=[row(D), row(4 * DG)],
        out_specs=pl.BlockSpec((2, D, SHW), lambda i: (0, 0, 0)),
        out_shape=pltpu.HBM((2, D, SHW), BF16),
        scratch_shapes=[pltpu.VMEM((2, D, SHW), F32)],
        compiler_params=_cparams(("arbitrary",), 48),
    )(u_b, d_b)


def _bwd_inproj_dw01(u_b, d_a, d_b, dw23):
    tm = 1024
    nt = T // tm
    H = D // 2
    step = 128

    def body(u_ref, da_ref, db_ref, p23_ref, dw_ref, acc, send_s, recv_s, ssem, rsem):
        i = pl.program_id(0)
        x, y, c = lax.axis_index("x"), lax.axis_index("y"), lax.axis_index("c")
        sibling = (x, y, 1 - c)
        mine0 = pl.multiple_of(c * H, step)
        other0 = pl.multiple_of((1 - c) * H, step)

        def swap(k):
            return _remote(send_s.at[k], recv_s.at[k], ssem.at[k], rsem.at[k], sibling)

        @pl.when(i == 0)
        def _():
            acc[...] = jnp.zeros_like(acc)
            for k in (2, 3):
                for r0 in range(0, H, step):
                    send_s[k, r0:r0 + step, :] = p23_ref[k - 2, pl.ds(other0 + r0, step), :]
                swap(k).start()

        u = u_ref[...]
        for j, k in enumerate((0, 1)):
            for val, off in _dproj_pieces(k, da_ref, db_ref):
                acc[j, :, off:off + val.shape[1]] += _mm_tn(u, val)

        @pl.when(i == nt - 1)
        def _():
            copies = [swap(k) for k in range(NSHARD)]
            for k in (0, 1):
                for r0 in range(0, H, step):
                    send_s[k, r0:r0 + step, :] = acc[k, pl.ds(other0 + r0, step), :].astype(BF16)
                copies[k].start()
            for k in (2, 3, 0, 1):
                copies[k].wait_recv()
                for r0 in range(0, H, step):
                    rs = pl.ds(mine0 + r0, step)
                    own = acc[k, rs, :] if k < 2 else p23_ref[k - 2, rs, :].astype(F32)
                    dw_ref[k, r0:r0 + step, :] = (own + recv_s[k, r0:r0 + step, :].astype(F32)).astype(BF16)
            for cp in copies:
                cp.wait_send()

    row = lambda w: pl.BlockSpec((tm, w), lambda i: (i, 0))
    return pl.pallas_call(
        body, name="bwd_inproj_dw01", grid=(nt,),
        in_specs=[row(D), row(D), row(4 * DG),
                  pl.BlockSpec((2, D, SHW), lambda i: (0, 0, 0), pipeline_mode=pl.Buffered(1))],
        out_specs=pl.BlockSpec((NSHARD, H, SHW), lambda i: (0, 0, 0)),
        out_shape=jax.ShapeDtypeStruct((NSHARD, H, SHW), BF16),
        scratch_shapes=[pltpu.VMEM((2, D, SHW), F32), pltpu.VMEM((NSHARD, H, SHW), BF16),
                        pltpu.VMEM((NSHARD, H, SHW), BF16),
                        pltpu.SemaphoreType.DMA((NSHARD,)), pltpu.SemaphoreType.DMA((NSHARD,))],
        compiler_params=_cparams(("arbitrary",), 56),
    )(u_b, d_a, d_b, dw23)


_OUT_ORDER = ["norm_mix_w", "w_in", "conv_w", "conv_b", "rg_wa", "rg_ba", "rg_wx", "rg_bx", "rg_lambda", "hg_lb",
              "hg_norm_w", "w_out", "ple_norm_w", "w_ple_gate", "b_ple_gate", "w_ple_proj", "final_norm_w"]
_BIG = ["w_in", "w_out", "w_ple_gate", "w_ple_proj"]


def _small_view(name, a):
    if name in ("rg_wa", "rg_wx"):
        return a.reshape(DG, RGB)
    if name == "conv_w":
        return a.reshape(4, 128)
    if name == "final_norm_w":
        return a.reshape(1, D)
    return a


def _landing(rows, cols):
    return lax.empty((NDEV, rows, cols), BF16)


def kernel(x, p, norm_mix_w, w_in, conv_w, conv_b, rg_wa, rg_ba, rg_wx, rg_bx, rg_lambda, hg_lb, hg_norm_w, w_out, ple_norm_w, w_ple_gate, b_ple_gate, w_ple_proj, final_norm_w, loss_target, m_norm_mix_w, m_w_in, m_conv_w, m_conv_b, m_rg_wa, m_rg_ba, m_rg_wx, m_rg_bx, m_rg_lambda, m_hg_lb, m_hg_norm_w, m_w_out, m_ple_norm_w, m_w_ple_gate, m_b_ple_gate, m_w_ple_proj, m_final_norm_w, v_norm_mix_w, v_w_in, v_conv_w, v_conv_b, v_rg_wa, v_rg_ba, v_rg_wx, v_rg_bx, v_rg_lambda, v_hg_lb, v_hg_norm_w, v_w_out, v_ple_norm_w, v_w_ple_gate, v_b_ple_gate, v_w_ple_proj, v_final_norm_w):
    given = dict(locals())
    x2, p2, tgt = x[0], p[0, 0], loss_target[0]
    hbm = lambda a: pltpu.with_memory_space_constraint(a, pltpu.HBM)
    norm_mix_w, conv_b, rg_ba, rg_bx, rg_lambda, hg_lb, hg_norm_w, ple_norm_w, b_ple_gate = (
        hbm(a) for a in (norm_mix_w, conv_b, rg_ba, rg_bx, rg_lambda, hg_lb, hg_norm_w, ple_norm_w, b_ple_gate))
    wa_c, wx_c = hbm(_small_view("rg_wa", rg_wa)), hbm(_small_view("rg_wx", rg_wx))

    w_in_b, l_out, l_pg, l_pp, cw3, pf, pb, u_b = _gather_inproj(
        x2, norm_mix_w, w_in[0], w_out[0], w_ple_gate[0], w_ple_proj[0], conv_w[0])
    g_ssem, g_rsem, g_lands, tok = _gather_rest_start([l_out, l_pg, l_pp])

    h, ya = _fwd_rglru(pf, pb, cw3, conv_b, wa_c, rg_ba, wx_c, rg_bx, rg_lambda, tok)
    yb, o, s_prev = _fwd_hgrn2(pf, pb, hg_lb, hg_norm_w)
    w_out_b, w_pg_b, w_pp_b = _gather_rest_wait(g_ssem, g_rsem, g_lands, yb)
    dh1, dyab, dwo_b, dwg_b, dwp_b, sm_tail = _tail_fwd_bwd(
        x2, p2, tgt, ya, yb, w_out_b.reshape(D, D), w_pg_b.reshape(D, D), w_pp_b,
        ple_norm_w, b_ple_gate, final_norm_w.reshape(1, D))

    QH = D // NSHARD // 2
    r1 = _rs_start("rs_start_tail", [dwo_b, dwg_b, dwp_b], (0, 1, 2, 3),
                   [_landing(QH, D), _landing(QH, D), _landing(PLE // 2, PLE)])
    d_b, sm_b = _bwd_hgrn2(pf, pb, o, s_prev, dyab, hg_lb, hg_norm_w, r1[4])
    dw23 = _bwd_inproj_dw23(u_b, d_b)
    d_a, rg_c, sm_a = _bwd_rglru(pf, pb, h, dyab, cw3, conv_b, wa_c, rg_ba, wx_c, rg_bx, rg_lambda, dw23)
    dw_in = _bwd_inproj_dw01(u_b, d_a, d_b, dw23)
    r2 = _rs_start("rs_start_in", [dw_in], (0, 1, 2, 3), [lax.empty((NSHARD, D // 2, SHW), BF16)], chip_sums=True)
    grad_x, sm_in = _bwd_inproj_dx(x2, dh1, d_a, d_b, w_in_b, norm_mix_w, r2[4])

    parts1, lands1 = _rs_wait("rs_wait_tail", r1[0], r1[1], r1[2], (0, 1, 2, 3), r1[3], sm_in)
    parts2, lands2 = _rs_wait("rs_wait_in", r2[0], r2[1], r2[2], (0, 1, 2, 3), r2[3], sm_in, chip_sums=True)
    g_big, vred, rgred = _final_reduce(parts2 + parts1, lands2 + lands1, sm_in, sm_tail, sm_a, sm_b, rg_c)

    upd_big = _adam_big(g_big, [given[n][0] for n in _BIG], [given["m_" + n][0] for n in _BIG],
                        [given["v_" + n][0] for n in _BIG])
    small = _adam_small(vred, rgred,
                        {n: _small_view(n, given[n]) for n in _SMALL_ORDER},
                        {n: _small_view(n, given["m_" + n]) for n in _SMALL_ORDER},
                        {n: _small_view(n, given["v_" + n]) for n in _SMALL_ORDER})

    loss = vred[0, ROW_LOSS, 0]
    outs = [loss, grad_x[None]]
    for ki in range(4):
        for n in _OUT_ORDER:
            if n in _BIG:
                i = _BIG.index(n)
                a = g_big[i] if ki == 0 else upd_big[i][ki - 1]
                outs.append(a[None])
            else:
                outs.append(small[n][ki].reshape(given[n].shape))
    return tuple(outs)
```

```python
import jax
import jax.numpy as jnp
from jax import lax
from jax.experimental import pallas as pl
from jax.experimental.pallas import tpu as pltpu

F32 = jnp.float32
BF16 = jnp.bfloat16
I32 = jnp.int32
MESH = pl.DeviceIdType.MESH

T = 4096
D = 1024
DG = 512
DIN = 3072
PLE = 256
NH = 4
HD = 128
CH = 64
NCHUNK = T // CH
RGB = 64
EPS = 1e-6
RG_C = 8.0
NSHARD = 4
SHW = DIN // NSHARD
NDEV = 8

ADAM_LR = 0.001
ADAM_B1 = 0.9
ADAM_B2 = 0.999
ADAM_EPS = 1e-08
ADAM_WD = 0.01
ADAM_STEP = 10

VMEM_SPEC = pl.BlockSpec(memory_space=pltpu.VMEM)
HBM_SPEC = pl.BlockSpec(memory_space=pltpu.HBM)
SEM_SPEC = pl.BlockSpec(memory_space=pltpu.SEMAPHORE)
ANY_SPEC = pl.BlockSpec(memory_space=pl.ANY)
EFFECT = pltpu.SideEffectType.DATAFLOW_SIDE_EFFECTING
MIB = 1024 * 1024

VROWS = 16
ROW_NORM_MIX, ROW_FINAL_NORM, ROW_B_PG, ROW_PLE_NORM = 0, 1, 2, 3
ROW_CB_BA, ROW_BX_LAM, ROW_CW01, ROW_CW23, ROW_HG_LB, ROW_HG_NW, ROW_LOSS = 4, 5, 6, 7, 8, 9, 10


def _mm(a, b):
    return jnp.dot(a.astype(BF16), b.astype(BF16), preferred_element_type=F32)


def _mm_nt(a, b):
    return lax.dot_general(a.astype(BF16), b.astype(BF16), (((1,), (1,)), ((), ())),
                           preferred_element_type=F32)


def _mm_tn(a, b):
    return lax.dot_general(a.astype(BF16), b.astype(BF16), (((0,), (0,)), ((), ())),
                           preferred_element_type=F32)


def _mm_exact(a, b):
    bb = b.astype(BF16)
    hi = a.astype(BF16)
    r1 = a - hi.astype(F32)
    mid = r1.astype(BF16)
    lo = (r1 - mid.astype(F32)).astype(BF16)
    return sum(jnp.dot(p, bb, preferred_element_type=F32) for p in (hi, mid, lo))


def _sig_pair(x):
    e = jnp.exp(-jnp.abs(x))
    big = 1.0 / (1.0 + e)
    small = e * big
    pos = x >= 0
    return jnp.where(pos, big, small), jnp.where(pos, small, big)


def _sigmoid(x):
    return 1.0 / (1.0 + jnp.exp(-x))


def _rowsum(v):
    return jnp.sum(v, axis=0, keepdims=True)


def _shift_down(cur, prev8, d, rows):
    rolled = pltpu.roll(cur, d, 0)
    head = jnp.where(rows[0:8] < d, pltpu.roll(prev8, d, 0), rolled[0:8])
    return jnp.concatenate([head, rolled[8:]], axis=0)


def _shift_up(cur, next8, d, rows):
    n = cur.shape[0]
    rolled = pltpu.roll(cur, n - d, 0)
    tail = jnp.where(rows[0:8] >= 8 - d, pltpu.roll(next8, 8 - d, 0), rolled[n - 8:n])
    return jnp.concatenate([rolled[0:n - 8], tail], axis=0)


def _roll_in_groups(v, d):
    n, w = v.shape
    return pltpu.roll(v.reshape(n // 8, 8, w), d, 1).reshape(n, w)


def _cparams(sem, vmem_mib):
    return pltpu.CompilerParams(dimension_semantics=sem, vmem_limit_bytes=vmem_mib * MIB)


def _mesh_pos():
    x, y, c = lax.axis_index("x"), lax.axis_index("y"), lax.axis_index("c")
    chips = [(1 - x, y), (x, 1 - y), (1 - x, 1 - y)]
    return x, y, c, chips


def _remote(src, dst, ssem, rsem, dev):
    return pltpu.make_async_remote_copy(src_ref=src, dst_ref=dst, send_sem=ssem, recv_sem=rsem,
                                        device_id=dev, device_id_type=MESH)


def _gather_inproj(x, nw, w_in, w_out, w_pg, w_pp, conv_w):
    shapes = [w_in.shape, w_out.shape, w_pg.shape, w_pp.shape]
    tm = 1024
    nt = T // tm

    def body(x_h, nw_h, win_h, wout_h, wpg_h, wpp_h, cw_h,
             o_in_h, o_out_h, o_pg_h, o_pp_h, o_cw_h, pf_h, pb_h, u_h,
             win, wout, wpg, wpp, cw, nw_s, xbuf, ubuf, rfbuf, rbbuf, o_in, o_out, o_pg, o_pp, o_cw,
             lsem, xsem, usem, fsem, bsem, osem, ssem, rsem):
        x, y, c, chips = _mesh_pos()
        kme = 2 * x + y
        sibling = (x, y, 1 - c)
        fetch = [pltpu.make_async_copy(src, dst, lsem.at[i]) for i, (src, dst) in enumerate(
            [(win_h, win), (cw_h, cw), (nw_h, nw_s), (wout_h, wout), (wpg_h, wpg), (wpp_h, wpp)])]
        for cp in fetch:
            cp.start()

        def cast(src, dst):
            for r0 in range(0, src.shape[0], 128):
                dst[r0:r0 + 128, :] = src[r0:r0 + 128, :].astype(BF16)

        outs = []

        def write_out(src, dst, i):
            cp = pltpu.make_async_copy(src, dst, osem.at[i])
            cp.start()
            outs.append(cp)

        hrows = D // 2
        mine = pl.ds(pl.multiple_of(c * hrows, 128), hrows)
        other = pl.ds(pl.multiple_of((1 - c) * hrows, 128), hrows)

        def cast_half(rows):
            for r0 in range(0, hrows, 128):
                rs = pl.ds(pl.multiple_of(rows.start + r0, 128), 128)
                o_in[kme, rs, :] = win[rs, :].astype(BF16)

        fetch[0].wait()
        cast_half(mine)
        fetch[1].wait()
        o_cw[kme] = cw[...]

        def half(k, rows):
            return o_in.at[k, rows]

        na = (x ^ c, y ^ (1 - c), c)
        nb = (x ^ (1 - c), y ^ c, c)
        ka = 2 * na[0] + na[1]
        kb = 2 * nb[0] + nb[1]
        kd = 2 * (1 - x) + (1 - y)
        sends = [_remote(half(kme, mine), half(kme, mine), ssem.at[0], rsem.at[0], na),
                 _remote(half(kme, mine), half(kme, mine), ssem.at[1], rsem.at[1], nb)]
        for j, (px, py) in enumerate(chips):
            sends.append(_remote(o_cw.at[kme], o_cw.at[kme], ssem.at[3 + j], rsem.at[3 + j], (px, py, c)))
        for cp in sends:
            cp.start()
        cast_half(other)
        write_out(o_in.at[kme], o_in_h.at[kme], 0)
        fetch[2].wait()

        def rows_of(t):
            return pl.ds(pl.multiple_of(t * tm, tm), tm)

        def x_copy(t, slot):
            return pltpu.make_async_copy(x_h.at[rows_of(t)], xbuf.at[slot], xsem.at[slot])

        u_out = pltpu.make_async_copy(ubuf, u_h, usem.at[0])

        def f_store(t, slot, kh):
            return pltpu.make_async_copy(rfbuf.at[slot], pf_h.at[rows_of(t), pl.ds(pl.multiple_of(kh * DG, DG), DG)],
                                         fsem.at[slot])

        def b_store_even(t, slot, kh):
            return pltpu.make_async_copy(rbbuf.at[slot, :, 0:256],
                                         pb_h.at[rows_of(t), pl.ds(pl.multiple_of(kh * 2 * DG, 256), 256)],
                                         bsem.at[slot])

        def b_store_odd(t, slot, kh):
            return pltpu.make_async_copy(rbbuf.at[slot],
                                         pb_h.at[rows_of(t), pl.ds(pl.multiple_of(256 + kh * 2 * DG, 256), SHW)],
                                         bsem.at[slot])

        def run_pass(k, first, even):
            kh = k >> 1
            if first:
                x_copy(0, 0).start()

            def tile(t, carry):
                slot = t & 1
                if first:
                    @pl.when(t + 1 < nt)
                    def _():
                        x_copy(t + 1, 1 - slot).start()

                    x_copy(t, slot).wait()

                @pl.when(t >= 2)
                def _():
                    if even:
                        f_store(t - 2, slot, kh).wait()
                        b_store_even(t - 2, slot, kh).wait()
                    else:
                        b_store_odd(t - 2, slot, kh).wait()

                if first:
                    xv = xbuf[slot]
                    s = lax.rsqrt(jnp.mean(xv * xv, axis=-1, keepdims=True) + EPS)
                    u = (xv * s * nw_s[...]).astype(BF16)
                    ubuf[rows_of(t), :] = u
                else:
                    u = ubuf[rows_of(t), :]
                r = jnp.dot(u, o_in[k], preferred_element_type=F32)
                if even:
                    rfbuf[slot] = r[:, 0:DG]
                    rbbuf[slot, :, 0:256] = r[:, DG:SHW].astype(BF16)
                    f_store(t, slot, kh).start()
                    b_store_even(t, slot, kh).start()
                else:
                    rbbuf[slot] = r.astype(BF16)
                    b_store_odd(t, slot, kh).start()
                return carry

            lax.fori_loop(0, nt, tile, 0)
            for t in (nt - 2, nt - 1):
                slot = t & 1
                if even:
                    f_store(t, slot, kh).wait()
                    b_store_even(t, slot, kh).wait()
                else:
                    b_store_odd(t, slot, kh).wait()

        def project(k, first=False):
            @pl.when((k & 1) == 0)
            def _():
                run_pass(k, first, True)

            @pl.when((k & 1) == 1)
            def _():
                run_pass(k, first, False)

        def passed_on(k, sem, dev):
            cp = _remote(half(k, mine), half(k, mine), ssem.at[sem], rsem.at[sem], dev)
            cp.start()
            sends.append(cp)

        project(kme, first=True)
        u_out.start()
        for i, (src, dst, dst_h) in enumerate([(wout, o_out, o_out_h), (wpg, o_pg, o_pg_h), (wpp, o_pp, o_pp_h)]):
            fetch[3 + i].wait()
            cast(src, dst)
            write_out(dst, dst_h.at[kme], 4 + i)
        _remote(half(ka, mine), half(ka, mine), ssem.at[0], rsem.at[0], na).wait_recv()
        passed_on(ka, 2, nb)
        passed_on(ka, 6, sibling)
        _remote(half(kb, mine), half(kb, mine), ssem.at[1], rsem.at[1], nb).wait_recv()
        passed_on(kb, 7, sibling)
        _remote(half(ka, other), half(ka, other), ssem.at[7], rsem.at[7], sibling).wait_recv()
        write_out(o_in.at[ka], o_in_h.at[ka], 1)
        project(ka)
        _remote(half(kd, mine), half(kd, mine), ssem.at[2], rsem.at[2], nb).wait_recv()
        passed_on(kd, 8, sibling)
        _remote(half(kb, other), half(kb, other), ssem.at[6], rsem.at[6], sibling).wait_recv()
        write_out(o_in.at[kb], o_in_h.at[kb], 2)
        project(kb)
        _remote(half(kd, other), half(kd, other), ssem.at[8], rsem.at[8], sibling).wait_recv()
        write_out(o_in.at[kd], o_in_h.at[kd], 3)
        project(kd)
        u_out.wait()
        for j, (px, py) in enumerate(chips):
            kj = 2 * px + py
            _remote(o_cw.at[kj], o_cw.at[kj], ssem.at[3 + j], rsem.at[3 + j], (px, py, c)).wait_recv()
        write_out(o_cw, o_cw_h, 7)
        for cp in outs:
            cp.wait()
        for cp in sends:
            cp.wait_send()

    out_shape = [pltpu.HBM((NSHARD,) + s, BF16) for s in shapes]
    out_shape.append(pltpu.HBM((NSHARD,) + conv_w.shape, F32))
    out_shape += [pltpu.HBM((T, 2 * DG), F32), pltpu.HBM((T, 4 * DG), BF16), pltpu.HBM((T, D), BF16)]
    hbm_args = [pltpu.with_memory_space_constraint(a, pltpu.HBM) for a in (x, nw, w_in, w_out, w_pg, w_pp, conv_w)]
    return pl.pallas_call(
        body, name="gather_inproj", out_shape=out_shape,
        in_specs=[HBM_SPEC] * 7, out_specs=[HBM_SPEC] * 8,
        scratch_shapes=[pltpu.VMEM(a.shape, F32) for a in (w_in, w_out, w_pg, w_pp, conv_w, nw)]
        + [pltpu.VMEM((2, tm, D), F32), pltpu.VMEM((T, D), BF16),
           pltpu.VMEM((2, tm, DG), F32), pltpu.VMEM((2, tm, SHW), BF16),
           pltpu.VMEM((NSHARD,) + w_in.shape, BF16), pltpu.VMEM(w_out.shape, BF16), pltpu.VMEM(w_pg.shape, BF16),
           pltpu.VMEM(w_pp.shape, BF16), pltpu.VMEM((NSHARD,) + conv_w.shape, F32),
           pltpu.SemaphoreType.DMA((6,)), pltpu.SemaphoreType.DMA((2,)), pltpu.SemaphoreType.DMA((2,)),
           pltpu.SemaphoreType.DMA((2,)), pltpu.SemaphoreType.DMA((2,)), pltpu.SemaphoreType.DMA((8,)),
           pltpu.SemaphoreType.DMA((9,)), pltpu.SemaphoreType.DMA((9,))],
        compiler_params=pltpu.CompilerParams(vmem_limit_bytes=48 * MIB),
    )(*hbm_args)


def _gather_rest_start(lands):
    n = len(lands)

    def body(*refs):
        land_in = refs[0:n]
        ssem, rsem = refs[n], refs[n + 1]
        token = refs[2 * n + 2]
        x, y, c, chips = _mesh_pos()
        kme = 2 * x + y
        for p, land in enumerate(land_in):
            hrows = land.shape[1] // 2
            mine = pl.ds(pl.multiple_of(c * hrows, 128), hrows)
            for px, py in chips:
                for pc in range(2):
                    _remote(land.at[kme, mine], land.at[kme, mine], ssem.at[p], rsem.at[p], (px, py, pc)).start()
        token[...] = jnp.zeros_like(token)

    out_shape = ([pltpu.SemaphoreType.DMA((n,)), pltpu.SemaphoreType.DMA((n,))]
                 + [pltpu.HBM(a.shape, a.dtype) for a in lands] + [jax.ShapeDtypeStruct((8, 128), F32)])
    outs = pl.pallas_call(
        body, name="gather_rest_start", out_shape=out_shape,
        in_specs=[HBM_SPEC] * n, out_specs=[SEM_SPEC, SEM_SPEC] + [HBM_SPEC] * n + [VMEM_SPEC],
        input_output_aliases={i: 2 + i for i in range(n)},
        compiler_params=pltpu.CompilerParams(has_side_effects=EFFECT),
    )(*[pltpu.with_memory_space_constraint(a, pltpu.HBM) for a in lands])
    return outs[0], outs[1], list(outs[2:2 + n]), outs[2 + n]


def _gather_rest_wait(ssem, rsem, lands, after):
    n = len(lands)

    def body(*refs):
        land_in = refs[0:n]
        ssem_ref, rsem_ref = refs[n], refs[n + 1]
        x, y, c = lax.axis_index("x"), lax.axis_index("y"), lax.axis_index("c")
        for p, land in enumerate(land_in):
            three = land.at[pl.ds(0, 3)]
            cp = _remote(three, three, ssem_ref.at[p], rsem_ref.at[p], (x, y, c))
            cp.wait_send()
            cp.wait_recv()

    outs = pl.pallas_call(
        body, name="gather_rest_wait", out_shape=[pltpu.HBM(a.shape, a.dtype) for a in lands],
        in_specs=[HBM_SPEC] * n + [SEM_SPEC, SEM_SPEC, ANY_SPEC], out_specs=[HBM_SPEC] * n,
        input_output_aliases={i: i for i in range(n)},
        compiler_params=pltpu.CompilerParams(has_side_effects=EFFECT),
    )(*lands, ssem, rsem, after)
    return list(outs)


def _rs_start(name, parts, ks, lands, chip_sums=False):
    n = len(parts)

    def body(*refs):
        part_in, land_in = refs[0:n], refs[n:2 * n]
        ssem, rsem = refs[2 * n], refs[2 * n + 1]
        token = refs[4 * n + 2]
        x, y, c = lax.axis_index("x"), lax.axis_index("y"), lax.axis_index("c")
        kme = 2 * x + y
        me = 4 * x + 2 * y + c
        for p in range(n):
            hrows = land_in[p].shape[1]
            for i, k in enumerate(ks):
                if chip_sums:
                    @pl.when(kme != k)
                    def _():
                        _remote(part_in[p].at[i], land_in[p].at[kme], ssem.at[p], rsem.at[p], (k // 2, k % 2, c)).start()
                    continue
                for pc in range(2):
                    @pl.when(jnp.logical_or(kme != k, c != pc))
                    def _():
                        _remote(part_in[p].at[i, pl.ds(pc * hrows, hrows)], land_in[p].at[me],
                                ssem.at[p], rsem.at[p], (k // 2, k % 2, pc)).start()
        token[...] = jnp.zeros_like(token)

    arrays = list(parts) + list(lands)
    out_shape = ([pltpu.SemaphoreType.DMA((n,)), pltpu.SemaphoreType.DMA((n,))]
                 + [pltpu.HBM(a.shape, a.dtype) for a in arrays] + [jax.ShapeDtypeStruct((8, 128), F32)])
    outs = pl.pallas_call(
        body, name=name, out_shape=out_shape,
        in_specs=[HBM_SPEC] * (2 * n), out_specs=[SEM_SPEC, SEM_SPEC] + [HBM_SPEC] * (2 * n) + [VMEM_SPEC],
        input_output_aliases={i: 2 + i for i in range(2 * n)},
        compiler_params=pltpu.CompilerParams(has_side_effects=EFFECT),
    )(*[pltpu.with_memory_space_constraint(a, pltpu.HBM) for a in arrays])
    return outs[0], outs[1], list(outs[2:2 + n]), list(outs[2 + n:2 + 2 * n]), outs[2 + 2 * n]


def _rs_wait(name, ssem, rsem, parts, ks, lands, after, chip_sums=False):
    n = len(parts)

    def body(*refs):
        part_in, land_in = refs[0:n], refs[n:2 * n]
        ssem_ref, rsem_ref = refs[2 * n], refs[2 * n + 1]
        x, y, c = lax.axis_index("x"), lax.axis_index("y"), lax.axis_index("c")
        kme = 2 * x + y
        for p in range(n):
            piece = land_in[p].at[0]
            for k in ks:
                for pc in range(1 if chip_sums else 2):
                    mine = (kme == k) if chip_sums else jnp.logical_and(kme == k, c == pc)

                    @pl.when(jnp.logical_not(mine))
                    def _():
                        _remote(piece, piece, ssem_ref.at[p], rsem_ref.at[p], (x, y, c)).wait_send()
            owner = kme == ks[0]
            for k in ks[1:]:
                owner = jnp.logical_or(owner, kme == k)

            @pl.when(owner)
            def _():
                others = land_in[p].at[pl.ds(0, land_in[p].shape[0] - 1)]
                _remote(others, others, ssem_ref.at[p], rsem_ref.at[p], (x, y, c)).wait_recv()

    arrays = list(parts) + list(lands)
    outs = pl.pallas_call(
        body, name=name, out_shape=[pltpu.HBM(a.shape, a.dtype) for a in arrays],
        in_specs=[HBM_SPEC] * (2 * n) + [SEM_SPEC, SEM_SPEC, ANY_SPEC], out_specs=[HBM_SPEC] * (2 * n),
        input_output_aliases={i: i for i in range(2 * n)},
        compiler_params=pltpu.CompilerParams(has_side_effects=EFFECT),
    )(*arrays, ssem, rsem, after)
    return list(outs[0:n]), list(outs[n:2 * n])


def _final_reduce(parts, lands, sm_in, sm_tail, sm_a, sm_b, rg_c):
    shapes = [(2 * l.shape[1], l.shape[2]) for l in lands]
    step = 128
    PR = DG // NDEV

    def body(pin, pout, ppg, ppp, l_in, l_out, l_pg, l_pp, in_h, tail_h, a_h, b_h, rg_h,
             g_in_h, g_out_h, g_pg_h, g_pp_h, v_out, rg_out, vbuf, vrecv, rgrecv,
             lb_in, lb_out, lb_pg, lb_pp, ob_in, ob_out, ob_pg, ob_pp,
             in_ref, tail_ref, a_ref, b_ref, rg_ref, g_in, g_out, g_pg, g_pp,
             lsem, ssem_l, osem, bsem_s, bsem_r, ssem, rsem):
        x, y, c = lax.axis_index("x"), lax.axis_index("y"), lax.axis_index("c")
        kme = 2 * x + y
        me = 4 * x + 2 * y + c
        sibling = (x, y, 1 - c)
        small_fetch = [pltpu.make_async_copy(src, dst, ssem_l.at[i]) for i, (src, dst) in enumerate(
            [(in_h, in_ref), (tail_h, tail_ref), (a_h, a_ref), (b_h, b_ref), (rg_h, rg_ref)])]
        for cp in small_fetch:
            cp.start()

        lands_hbm = [l_in, l_out, l_pg, l_pp]
        land_bufs = [lb_in, lb_out, lb_pg, lb_pp]
        own_bufs = [ob_in, ob_out, ob_pg, ob_pp]
        fetches = []
        for p in range(4):
            cp = pltpu.make_async_copy(lands_hbm[p], land_bufs[p], lsem.at[p])
            cp.start()
            fetches.append(cp)

        own_fetches = [pltpu.make_async_copy(pin.at[kme], ob_in, lsem.at[4])]
        own_fetches[0].start()
        for p, part in enumerate([pout, ppg, ppp]):
            hrows = own_bufs[p + 1].shape[0]
            cp = pltpu.make_async_copy(part.at[kme, pl.ds(pl.multiple_of(c * hrows, step), hrows)],
                                       own_bufs[p + 1], lsem.at[5 + p])
            cp.start()
            own_fetches.append(cp)

        def pair(ref, r0, r1):
            return jnp.concatenate([ref[r0:r0 + 1, :], ref[r1:r1 + 1, :]], axis=1)

        for cp in small_fetch:
            cp.wait()
        rows = {
            ROW_NORM_MIX: in_ref[0:1, :], ROW_FINAL_NORM: tail_ref[1:2, :], ROW_B_PG: tail_ref[2:3, :],
            ROW_PLE_NORM: tail_ref[3:4, :], ROW_CB_BA: pair(a_ref, 0, 1), ROW_BX_LAM: pair(a_ref, 2, 3),
            ROW_CW01: pair(a_ref, 4, 5), ROW_CW23: pair(a_ref, 6, 7), ROW_HG_LB: pair(b_ref, 2, 3),
            ROW_HG_NW: jnp.concatenate([b_ref[1:2, :], jnp.zeros((1, DG), F32)], axis=1),
            ROW_LOSS: tail_ref[4:5, :],
        }
        vbuf[...] = jnp.zeros_like(vbuf)
        for r, row in rows.items():
            for j in range(NDEV):
                vbuf[j, r:r + 1, :] = row[:, j * 128:(j + 1) * 128]

        def peer(mask):
            px = x ^ ((mask >> 2) & 1)
            py = y ^ ((mask >> 1) & 1)
            pc = c ^ (mask & 1)
            return (px, py, pc), 4 * px + 2 * py + pc

        def rg_rows(r):
            return pl.ds(pl.multiple_of(r * PR, PR), PR)

        first = []
        for mask in range(1, NDEV):
            dev, r = peer(mask)
            i = mask - 1
            cp = _remote(vbuf.at[r], vrecv.at[i], ssem.at[i], rsem.at[i], dev)
            cp.start()
            first.append(cp)
            cp = _remote(rg_ref.at[rg_rows(r)], rgrecv.at[i], ssem.at[7 + i], rsem.at[7 + i], dev)
            cp.start()
            first.append(cp)

        big = [(lb_in, g_in), (lb_out, g_out), (lb_pg, g_pg), (lb_pp, g_pp)]
        g_hbm = [g_in_h, g_out_h, g_pg_h, g_pp_h]
        swaps, writes = [], []
        for p, (land, gout) in enumerate(big):
            fetches[p].wait()
            own_fetches[p].wait()
            hrows = land.shape[1]
            mine0 = pl.multiple_of(c * hrows, step)
            slot = kme if p == 0 else me
            for r0 in range(0, hrows, step):
                rs = pl.ds(mine0 + r0, step)
                own = own_bufs[p][r0:r0 + step, :]
                s = jnp.zeros((step, land.shape[2]), F32)
                for j in range(land.shape[0]):
                    s = s + jnp.where(slot == j, own, land[j, r0:r0 + step, :]).astype(F32)
                gout[rs, :] = s
            mine = pl.ds(mine0, hrows)
            cp = _remote(gout.at[mine], gout.at[mine], bsem_s.at[p], bsem_r.at[p], sibling)
            cp.start()
            swaps.append(cp)
            cp = pltpu.make_async_copy(gout.at[mine], g_hbm[p].at[mine], osem.at[p])
            cp.start()
            writes.append(cp)

        sv = vbuf[me]
        sr = rg_ref[rg_rows(me), :]
        for i in range(NDEV - 1):
            first[2 * i].wait_recv()
            first[2 * i + 1].wait_recv()
            sv = sv + vrecv[i]
            sr = sr + rgrecv[i]
        v_out[me] = sv
        rg_out[rg_rows(me), :] = sr
        second = []
        for mask in range(1, NDEV):
            dev, r = peer(mask)
            i = mask - 1
            cp = _remote(v_out.at[me], v_out.at[me], ssem.at[14 + i], rsem.at[14 + i], dev)
            cp.start()
            second.append(cp)
            cp = _remote(rg_out.at[rg_rows(me)], rg_out.at[rg_rows(me)], ssem.at[21 + i], rsem.at[21 + i], dev)
            cp.start()
            second.append(cp)
        for p, (land, gout) in enumerate(big):
            hrows = land.shape[1]
            other = pl.ds(pl.multiple_of((1 - c) * hrows, step), hrows)
            _remote(gout.at[other], gout.at[other], bsem_s.at[p], bsem_r.at[p], sibling).wait_recv()
            cp = pltpu.make_async_copy(gout.at[other], g_hbm[p].at[other], osem.at[4 + p])
            cp.start()
            writes.append(cp)
        for mask in range(1, NDEV):
            dev, r = peer(mask)
            i = mask - 1
            _remote(v_out.at[r], v_out.at[r], ssem.at[14 + i], rsem.at[14 + i], dev).wait_recv()
            _remote(rg_out.at[rg_rows(r)], rg_out.at[rg_rows(r)], ssem.at[21 + i], rsem.at[21 + i], dev).wait_recv()
        for cp in writes:
            cp.wait()
        for cp in first + swaps + second:
            cp.wait_send()

    out_shape = [pltpu.HBM(s, F32) for s in shapes]
    out_shape += [jax.ShapeDtypeStruct((NDEV, VROWS, 128), F32), jax.ShapeDtypeStruct((DG, 128), F32)]
    outs = pl.pallas_call(
        body, name="final_reduce", out_shape=out_shape,
        in_specs=[HBM_SPEC] * 13, out_specs=[HBM_SPEC] * 4 + [VMEM_SPEC] * 2,
        scratch_shapes=[pltpu.VMEM((NDEV, VROWS, 128), F32), pltpu.VMEM((NDEV - 1, VROWS, 128), F32),
                        pltpu.VMEM((NDEV - 1, PR, 128), F32)]
        + [pltpu.VMEM(l.shape, BF16) for l in lands]
        + [pltpu.VMEM(l.shape[1:], BF16) for l in lands]
        + [pltpu.VMEM(a.shape, F32) for a in (sm_in, sm_tail, sm_a, sm_b, rg_c)]
        + [pltpu.VMEM(s, F32) for s in shapes]
        + [pltpu.SemaphoreType.DMA((8,)), pltpu.SemaphoreType.DMA((5,)), pltpu.SemaphoreType.DMA((8,)),
                        pltpu.SemaphoreType.DMA((4,)), pltpu.SemaphoreType.DMA((4,)),
                        pltpu.SemaphoreType.DMA((28,)), pltpu.SemaphoreType.DMA((28,))],
        compiler_params=pltpu.CompilerParams(vmem_limit_bytes=48 * MIB),
    )(*[pltpu.with_memory_space_constraint(a, pltpu.HBM)
        for a in (*parts, *lands, sm_in, sm_tail, sm_a, sm_b, rg_c)])
    return list(outs[0:4]), outs[4], outs[5]


def _adam_rows(w, g, m, v):
    m2 = ADAM_B1 * m + (1.0 - ADAM_B1) * g
    v2 = ADAM_B2 * v + (1.0 - ADAM_B2) * (g * g)
    m_hat = m2 / (1.0 - ADAM_B1 ** ADAM_STEP)
    v_hat = v2 / (1.0 - ADAM_B2 ** ADAM_STEP)
    delta = -ADAM_LR * (m_hat / (jnp.sqrt(v_hat) + ADAM_EPS) + ADAM_WD * w)
    return delta, m2, v2


def _adam_big(gs, ws, ms, vs):
    n = len(gs)
    steps = 4

    def body(*refs):
        ins, outs = refs[:4 * n], refs[4 * n:]
        for i in range(n):
            g, w, m, v = (r[...] for r in ins[4 * i:4 * i + 4])
            d, m2, v2 = _adam_rows(w, g, m, v)
            outs[3 * i][...] = d
            outs[3 * i + 1][...] = m2
            outs[3 * i + 2][...] = v2

    in_specs, out_specs, out_shape, args = [], [], [], []
    for g, w, m, v in zip(gs, ws, ms, vs):
        r, c = w.shape
        spec = lambda: pl.BlockSpec((r // steps, c), lambda i: (i, 0))
        in_specs += [spec() for _ in range(4)]
        out_specs += [spec() for _ in range(3)]
        out_shape += [jax.ShapeDtypeStruct((r, c), F32)] * 3
        args += [pltpu.with_memory_space_constraint(a, pltpu.HBM) for a in (g, w, m, v)]
    outs = pl.pallas_call(
        body, name="adam_big", grid=(steps,), in_specs=in_specs, out_specs=out_specs, out_shape=out_shape,
        compiler_params=_cparams(("parallel",), 32),
    )(*args)
    return [tuple(outs[3 * i:3 * i + 3]) for i in range(n)]


_VEC_PARAMS = [
    ("norm_mix_w", ROW_NORM_MIX, 0, D), ("final_norm_w", ROW_FINAL_NORM, 0, D),
    ("b_ple_gate", ROW_B_PG, 0, D), ("ple_norm_w", ROW_PLE_NORM, 0, D),
    ("conv_b", ROW_CB_BA, 0, DG), ("rg_ba", ROW_CB_BA, DG, DG),
    ("rg_bx", ROW_BX_LAM, 0, DG), ("rg_lambda", ROW_BX_LAM, DG, DG),
    ("hg_norm_w", ROW_HG_NW, 0, HD),
]
_SMALL_ORDER = [n for n, _, _, _ in _VEC_PARAMS] + ["hg_lb", "conv_w", "rg_wa", "rg_wx"]


def _adam_small(vred, rgred, ws, ms, vs):
    names = _SMALL_ORDER
    n = len(names)

    def body(vred_ref, rg_ref, *refs):
        w_refs = dict(zip(names, refs[0:n]))
        m_refs = dict(zip(names, refs[n:2 * n]))
        v_refs = dict(zip(names, refs[2 * n:3 * n]))
        outs = refs[3 * n:]
        o_refs = {nm: outs[4 * i:4 * i + 4] for i, nm in enumerate(names)}
        kme = 2 * lax.axis_index("x") + lax.axis_index("y")

        def update(nm, g, idx):
            d, m2, v2 = _adam_rows(w_refs[nm][idx], g, m_refs[nm][idx], v_refs[nm][idx])
            og, od, om, ov = o_refs[nm]
            og[idx] = g
            od[idx] = d
            om[idx] = m2
            ov[idx] = v2

        def packed(row, lane0, width):
            return jnp.concatenate([vred_ref[j, row:row + 1, :] for j in range(lane0 // 128, (lane0 + width) // 128)],
                                   axis=1)

        everything = (slice(None), slice(None))
        for nm, row, lane0, width in _VEC_PARAMS:
            update(nm, packed(row, lane0, width), everything)
        for r in range(2):
            update("hg_lb", packed(ROW_HG_LB, r * DG, DG), (slice(r, r + 1), slice(None)))
        for j in range(4):
            g = vred_ref[(j % 2) * 4 + kme, ROW_CW01 + j // 2:ROW_CW01 + j // 2 + 1, :]
            update("conv_w", g, (slice(j, j + 1), slice(None)))
        for r0 in range(0, DG, 128):
            rs = (slice(r0, r0 + 128), slice(None))
            both = rg_ref[r0:r0 + 128, :]
            update("rg_wa", both[:, 0:RGB], rs)
            update("rg_wx", pltpu.roll(both, RGB, 1)[:, 0:RGB], rs)

    args = [vred, rgred] + [d[nm] for d in (ws, ms, vs) for nm in names]
    out_shape = []
    for nm in names:
        out_shape += [jax.ShapeDtypeStruct(ws[nm].shape, F32)] * 4
    whole = lambda s: pl.BlockSpec(s.shape, lambda i, nd=len(s.shape): (0,) * nd)
    outs = pl.pallas_call(
        body, name="adam_small", out_shape=out_shape, grid=(1,),
        in_specs=[whole(a) for a in args], out_specs=[whole(s) for s in out_shape],
    )(*args)
    return {nm: tuple(outs[4 * i:4 * i + 4]) for i, nm in enumerate(names)}


def _conv_rows(cw_ref):
    return [jnp.concatenate([cw_ref[k, j:j + 1, :] for k in range(NSHARD)], axis=1) for j in range(4)]


def _rg_conv(xa, prev8, cw, cb, rows):
    taps = [_shift_down(xa, prev8, 3, rows), _shift_down(xa, prev8, 2, rows),
            _shift_down(xa, prev8, 1, rows), xa]
    xc = cb
    for j in range(4):
        xc = xc + taps[j] * cw[j]
    return xc, taps


def _block_mask():
    r = lax.broadcasted_iota(I32, (DG, DG), 0)
    c = lax.broadcasted_iota(I32, (DG, DG), 1)
    return (r >> 6) == (c >> 6)


def _dense_from_blocks(wc):
    j = lax.broadcasted_iota(I32, (RGB, DG), 0)
    c = lax.broadcasted_iota(I32, (RGB, DG), 1)
    spread = _mm_exact(wc, ((c & (RGB - 1)) == j).astype(F32))
    return jnp.where(_block_mask(), spread, 0.0)


def _blocks_from_dense(da, dx):
    c = lax.broadcasted_iota(I32, (DG, 128), 0)
    j = lax.broadcasted_iota(I32, (DG, 128), 1)
    hit = (c & (RGB - 1)) == (j & (RGB - 1))
    mask = _block_mask()
    return (_mm_exact(jnp.where(mask, da, 0.0), (hit & (j < RGB)).astype(F32))
            + _mm_exact(jnp.where(mask, dx, 0.0), (hit & (j >= RGB)).astype(F32)))


def _rg_gates(xc, wa, ba, wx, bx, sp, first_row):
    r = _sigmoid(_mm(xc, wa) + ba)
    i = _sigmoid(_mm(xc, wx) + bx)
    log_a = (-RG_C) * r * sp
    a = jnp.exp(log_a)
    a2 = a * a
    one_m_a2 = -jnp.tanh(log_a) * (a2 + 1.0)
    mult = jnp.where(first_row, 1.0, jnp.sqrt(one_m_a2))
    return r, i, a, a2, mult


def _softplus(z):
    return jnp.maximum(z, 0.0) + jnp.log1p(jnp.exp(-jnp.abs(z)))


def _fwd_rglru(pf, pb, cw3, conv_b, wa_c, ba, wx_c, bx, lam, dep):
    tm = 512
    ng = tm // 8

    def body(xa_ref, ga_ref, cw_ref, cb_ref, wa_ref, ba_ref, wx_ref, bx_ref, lam_ref, dep_ref,
             h_ref, ya_ref, a_s, u_s, tail_s, hc_s, wa_s, wx_s):
        i = pl.program_id(0)

        @pl.when(i == 0)
        def _():
            tail_s[...] = jnp.zeros_like(tail_s)
            hc_s[...] = jnp.zeros_like(hc_s)
            wa_s[...] = _dense_from_blocks(wa_ref[...]).astype(BF16)
            wx_s[...] = _dense_from_blocks(wx_ref[...]).astype(BF16)

        rows = lax.broadcasted_iota(I32, (tm, DG), 0)
        xa = xa_ref[...]
        xc, _ = _rg_conv(xa, tail_s[...], _conv_rows(cw_ref), cb_ref[...], rows)
        tail_s[...] = xa[tm - 8:tm, :]
        sp = _softplus(-lam_ref[...])
        rp = _mm(xc, wa_s[...]) + ba_ref[...]
        ip = _mm(xc, wx_s[...]) + bx_ref[...]
        rb = 64
        rows_b = lax.broadcasted_iota(I32, (rb, DG), 0)
        rows8 = rows_b & 7
        carry = hc_s[0:1, :]
        for b0 in range(0, tm, rb):
            sl = slice(b0, b0 + rb)
            r = _sigmoid(rp[sl])
            ig = _sigmoid(ip[sl])
            log_a = (-RG_C) * r * sp
            av = jnp.exp(log_a)
            mult = jnp.sqrt(-jnp.tanh(log_a) * (av * av + 1.0))
            if b0 == 0:
                mult = jnp.where((rows_b + i * tm) == 0, 1.0, mult)
            uv = mult * (ig * xc[sl])
            for d in (1, 2, 4):
                keep = rows8 >= d
                uv = uv + av * jnp.where(keep, _roll_in_groups(uv, d), 0.0)
                av = av * jnp.where(keep, _roll_in_groups(av, d), 1.0)
            ga = ga_ref[sl, :].astype(F32)
            gate = ga * _sigmoid(ga)
            hs = []
            for g in range(rb // 8):
                gs = slice(g * 8, (g + 1) * 8)
                hv = uv[gs] + av[gs] * carry
                carry = hv[7:8, :]
                hs.append(hv)
            hb = jnp.concatenate(hs, axis=0)
            h_ref[sl, :] = hb
            ya_ref[sl, :] = (hb * gate).astype(BF16)
        hc_s[0:1, :] = carry

    vec = lambda: pl.BlockSpec((1, DG), lambda i: (0, 0))
    blocks = lambda: pl.BlockSpec((DG, RGB), lambda i: (0, 0))
    return pl.pallas_call(
        body, name="fwd_rglru", grid=(T // tm,),
        in_specs=[pl.BlockSpec((tm, DG), lambda i: (i, 0)),
                  pl.BlockSpec((tm, DG), lambda i: (i, 0)),
                  pl.BlockSpec((NSHARD, 4, 128), lambda i: (0, 0, 0)), vec(),
                  blocks(), vec(), blocks(), vec(), vec(), ANY_SPEC],
        out_specs=[pl.BlockSpec((tm, DG), lambda i: (i, 0)),
                   pl.BlockSpec((tm, DG), lambda i: (i, 0))],
        out_shape=[jax.ShapeDtypeStruct((T, DG), F32), pltpu.HBM((T, DG), BF16)],
        scratch_shapes=[pltpu.VMEM((tm, DG), F32), pltpu.VMEM((tm, DG), F32),
                        pltpu.VMEM((8, DG), F32), pltpu.VMEM((8, DG), F32),
                        pltpu.VMEM((DG, DG), BF16), pltpu.VMEM((DG, DG), BF16)],
        compiler_params=_cparams(("arbitrary",), 48),
    )(pf, pb, cw3, conv_b, wa_c, ba, wx_c, bx, lam, dep)


def _hg_lower_bound(lb_ref):
    return _sig_pair(lb_ref[0:1, :] - lb_ref[1:2, :])


def _hg_gates(fz, lb, one_m_lb):
    sg, sn = _sig_pair(fz)
    f = lb + one_m_lb * sg
    return sg, sn, f, jnp.log(f), one_m_lb * sn


def _tri(lower):
    r = lax.broadcasted_iota(I32, (CH, CH), 0)
    c = lax.broadcasted_iota(I32, (CH, CH), 1)
    return (r >= c) if lower else (r <= c)


def _split3(v):
    hi = v.astype(BF16)
    r1 = v - hi.astype(F32)
    mid = r1.astype(BF16)
    lo = (r1 - mid.astype(F32)).astype(BF16)
    return hi, mid, lo


def _chunk_cumsum(v, rows64, reverse=False):
    del rows64
    tri = _tri(not reverse).astype(BF16)
    out = []
    for c in range(v.shape[0] // CH):
        pieces = _split3(v[c * CH:(c + 1) * CH])
        out.append(sum(jnp.dot(tri, p, preferred_element_type=F32) for p in pieces))
    return jnp.concatenate(out, axis=0)


def _chunk_rev_cumsum(v, rows64):
    return _chunk_cumsum(v, rows64, reverse=True)


def _hg_recompute(q_ref, f_ref, lb, one_m_lb, rows64, eb_s, enb_s, ekd_s, qe_s, ke_s, kd_s, dec_s):
    nc = q_ref.shape[0] // CH
    sg, sn, f, logf, k = _hg_gates(f_ref[...], lb, one_m_lb)
    q = q_ref[...].astype(F32)
    sq = _sigmoid(q)
    qs = q * sq * (HD ** -0.5)
    b = _chunk_cumsum(logf, rows64)
    for c in range(nc):
        rs = slice(c * CH, (c + 1) * CH)
        b_c = b[rs]
        bl = b_c[CH - 1:CH, :]
        eb, enb, ekd = jnp.exp(b_c), jnp.exp(-b_c), jnp.exp(bl - b_c)
        if eb_s is not None:
            eb_s[rs, :] = eb
            enb_s[rs, :] = enb
            ekd_s[rs, :] = ekd
        qe_s[rs, :] = (qs[rs] * eb).astype(BF16)
        ke_s[rs, :] = (k[rs] * enb).astype(BF16)
        kd_s[rs, :] = (k[rs] * ekd).astype(BF16)
        dec_s[c:c + 1, :] = jnp.exp(bl)
    return sg, sn, f, k, q, sq, qs


def _fwd_hgrn2(pf, pb, hg_lb, hg_nw):
    tm = 512
    nc = tm // CH

    def body(q_ref, f_ref, v_ref, g_ref, lb_ref, nw_ref, yb_ref, o_ref, sp_ref,
             st_s, qe_s, ke_s, kd_s, vb_s, dec_s, p_s, ds_s):
        i = pl.program_id(0)

        @pl.when(i == 0)
        def _():
            st_s[...] = jnp.zeros_like(st_s)

        lb, one_m_lb = _hg_lower_bound(lb_ref)
        rows64 = lax.broadcasted_iota(I32, (tm, DG), 0) & (CH - 1)
        _hg_recompute(q_ref, f_ref, lb, one_m_lb, rows64, None, None, None, qe_s, ke_s, kd_s, dec_s)
        vb_s[...] = v_ref[...]
        mask = _tri(True)
        items = [(c, hd, slice(c * CH, (c + 1) * CH), slice(hd * HD, (hd + 1) * HD))
                 for c in range(nc) for hd in range(NH)]
        for c, hd, rs, cols in items:
            p_s[c * NH + hd] = jnp.where(mask, _mm_nt(qe_s[rs, cols], ke_s[rs, cols]), 0.0).astype(BF16)
            ds_s[c * NH + hd] = _mm_tn(vb_s[rs, cols], kd_s[rs, cols])
        for c, hd, rs, cols in items:
            st = st_s[hd]
            sp_ref[hd, c] = st
            st_s[hd] = st * dec_s[c:c + 1, cols] + ds_s[c * NH + hd]
        for c, hd, rs, cols in items:
            o_ref[rs, cols] = _mm(p_s[c * NH + hd], vb_s[rs, cols]) + _mm_nt(qe_s[rs, cols], sp_ref[hd, c])
        nw = nw_ref[...]
        for hd in range(NH):
            cols = slice(hd * HD, (hd + 1) * HD)
            o = o_ref[:, cols]
            so = lax.rsqrt(jnp.mean(o * o, axis=-1, keepdims=True) + EPS)
            g = g_ref[:, cols].astype(F32)
            sg = _sigmoid(g)
            yb_ref[:, cols] = (o * so * nw * (g * sg)).astype(BF16)

    col = lambda j: pl.BlockSpec((tm, DG), lambda i: (i, j))
    return pl.pallas_call(
        body, name="fwd_hgrn2", grid=(T // tm,),
        in_specs=[col(1), col(1), col(2), col(3),
                  pl.BlockSpec((2, DG), lambda i: (0, 0)),
                  pl.BlockSpec((1, HD), lambda i: (0, 0))],
        out_specs=[pl.BlockSpec((tm, DG), lambda i: (i, 0)),
                   pl.BlockSpec((tm, DG), lambda i: (i, 0)),
                   pl.BlockSpec((NH, nc, HD, HD), lambda i: (0, i, 0, 0))],
        out_shape=[pltpu.HBM((T, DG), BF16), jax.ShapeDtypeStruct((T, DG), F32),
                   jax.ShapeDtypeStruct((NH, NCHUNK, HD, HD), F32)],
        scratch_shapes=[pltpu.VMEM((NH, HD, HD), F32),
                        pltpu.VMEM((tm, DG), BF16), pltpu.VMEM((tm, DG), BF16), pltpu.VMEM((tm, DG), BF16),
                        pltpu.VMEM((tm, DG), BF16), pltpu.VMEM((nc, DG), F32),
                        pltpu.VMEM((nc * NH, CH, CH), BF16), pltpu.VMEM((nc * NH, HD, HD), F32)],
        compiler_params=_cparams(("arbitrary",), 48),
    )(pb, pf, pb, pb, hg_lb, hg_nw)


def _tail_fwd_bwd(x, p, tgt, ya, yb, w_out_b, w_pg_b, w_pp_b, ple_nw, b_pg, fnw):
    tm = 512
    nt = T // tm
    QR = D // NSHARD

    def body(x_ref, p_ref, t_ref, ya_ref, yb_ref, wo_ref, wg_ref, wp_ref, pw_ref, b_ref, fw_ref,
             dh1_ref, dyab_ref, dwo_ref, dwg_ref, dwp_ref, sm_ref, dwo_s, dwg_s, dwp_s):
        i = pl.program_id(0)

        @pl.when(i == 0)
        def _():
            dwo_s[...] = jnp.zeros_like(dwo_s)
            dwg_s[...] = jnp.zeros_like(dwg_s)
            dwp_s[...] = jnp.zeros_like(dwp_s)
            sm_ref[...] = jnp.zeros_like(sm_ref)

        ya = ya_ref[...]
        yb = yb_ref[...]
        pv = p_ref[...].astype(BF16)
        pw = pw_ref[...]
        fw = fw_ref[...]
        h1 = x_ref[...] + _mm(ya, wo_ref[0:DG, :]) + _mm(yb, wo_ref[DG:D, :])
        s2 = lax.rsqrt(jnp.mean(h1 * h1, axis=-1, keepdims=True) + EPS)
        n2h = h1 * s2
        n2 = (n2h * pw).astype(BF16)
        z = _mm(n2, wg_ref[...]) + b_ref[...]
        gate = _sigmoid(z)
        pp = jnp.concatenate([_mm(pv, wp_ref[k]) for k in range(NSHARD)], axis=1)
        h2 = h1 + gate * pp
        s3 = lax.rsqrt(jnp.mean(h2 * h2, axis=-1, keepdims=True) + EPS)
        hn = h2 * s3
        err = hn * fw - t_ref[...]
        sm_ref[0:1, :] += _rowsum(err * err)
        dy = err * (1.0 / D)
        sm_ref[1:2, :] += _rowsum(dy * hn)
        g3 = dy * fw
        dh2 = s3 * (g3 - hn * jnp.mean(g3 * hn, axis=-1, keepdims=True))
        dpp = (dh2 * gate).astype(BF16)
        dz = dh2 * pp * gate * (1.0 - gate)
        sm_ref[2:3, :] += _rowsum(dz)
        dzb = dz.astype(BF16)
        dwg_s[...] += _mm_tn(n2, dzb)
        dn2 = _mm_nt(dzb, wg_ref[...])
        for k in range(NSHARD):
            dwp_s[k] += _mm_tn(pv, dpp[:, k * PLE:(k + 1) * PLE])
        sm_ref[3:4, :] += _rowsum(dn2 * n2h)
        g2 = dn2 * pw
        dh1 = dh2 + s2 * (g2 - n2h * jnp.mean(g2 * n2h, axis=-1, keepdims=True))
        dh1_ref[...] = dh1
        dh1b = dh1.astype(BF16)
        dyab_ref[...] = _mm_nt(dh1b, wo_ref[...])
        dwo_s[0:DG, :] += _mm_tn(ya, dh1b)
        dwo_s[DG:D, :] += _mm_tn(yb, dh1b)

        @pl.when(i == nt - 1)
        def _():
            total = jnp.sum(sm_ref[0:1, :], axis=-1, keepdims=True) * (0.5 / D)
            sm_ref[4:5, :] = jnp.broadcast_to(total, (1, D))
            for k in range(NSHARD):
                dwo_ref[k] = dwo_s[k * QR:(k + 1) * QR, :].astype(BF16)
                dwg_ref[k] = dwg_s[k * QR:(k + 1) * QR, :].astype(BF16)
                dwp_ref[k] = dwp_s[k].astype(BF16)

    row = lambda w: pl.BlockSpec((tm, w), lambda i: (i, 0))
    const2 = lambda s: pl.BlockSpec(s, lambda i: (0, 0), pipeline_mode=pl.Buffered(1))
    const3 = lambda s: pl.BlockSpec(s, lambda i: (0, 0, 0), pipeline_mode=pl.Buffered(1))
    return pl.pallas_call(
        body, name="tail_fwd_bwd", grid=(nt,),
        in_specs=[row(D), row(PLE), row(D), row(DG), row(DG),
                  const2((D, D)), const2((D, D)), const3((NSHARD, PLE, PLE)),
                  const2((1, D)), const2((1, D)), const2((1, D))],
        out_specs=[row(D), row(D), const3((NSHARD, QR, D)), const3((NSHARD, QR, D)),
                   const3((NSHARD, PLE, PLE)), const2((8, D))],
        out_shape=[jax.ShapeDtypeStruct((T, D), F32), jax.ShapeDtypeStruct((T, D), F32),
                   jax.ShapeDtypeStruct((NSHARD, QR, D), BF16), jax.ShapeDtypeStruct((NSHARD, QR, D), BF16),
                   jax.ShapeDtypeStruct((NSHARD, PLE, PLE), BF16), jax.ShapeDtypeStruct((8, D), F32)],
        scratch_shapes=[pltpu.VMEM((D, D), F32), pltpu.VMEM((D, D), F32), pltpu.VMEM((NSHARD, PLE, PLE), F32)],
        compiler_params=_cparams(("arbitrary",), 62),
    )(x, p, tgt, pltpu.with_memory_space_constraint(ya, pltpu.HBM), pltpu.with_memory_space_constraint(yb, pltpu.HBM),
      w_out_b, w_pg_b, w_pp_b, ple_nw, b_pg, fnw)


def _bwd_rglru(pf, pb, h, dyab, cw3, conv_b, wa_c, ba, wx_c, bx, lam, dep):
    tm = 512
    nt = T // tm
    ng = tm // 8

    def body(xa_ref, ga_ref, xp_ref, h_ref, hp_ref, dya_ref, cw_ref, cb_ref, wa_ref, ba_ref, wx_ref, bx_ref,
             lam_ref, dep_ref, da_ref, rg_ref, sm_ref, a_s, g_s, cg_s, nxt_s, wa_s, wx_s, dwa_s, dwx_s):
        i = pl.program_id(0)
        tile = nt - 1 - i

        @pl.when(i == 0)
        def _():
            dwa_s[...] = jnp.zeros_like(dwa_s)
            dwx_s[...] = jnp.zeros_like(dwx_s)
            sm_ref[...] = jnp.zeros_like(sm_ref)
            cg_s[...] = jnp.zeros_like(cg_s)
            nxt_s[...] = jnp.zeros_like(nxt_s)
            wa_s[...] = _dense_from_blocks(wa_ref[...]).astype(BF16)
            wx_s[...] = _dense_from_blocks(wx_ref[...]).astype(BF16)

        rows = lax.broadcasted_iota(I32, (tm, DG), 0)
        has_prev = tile > 0
        xa = xa_ref[...]
        xprev = jnp.where(has_prev, xp_ref[...], 0.0)
        cw = _conv_rows(cw_ref)
        xc, taps = _rg_conv(xa, xprev, cw, cb_ref[...], rows)
        lam_v = lam_ref[...]
        sp = _softplus(-lam_v)
        first_row = (rows + tile * tm) == 0
        r, ig, a, a2, mult = _rg_gates(xc, wa_s[...], ba_ref[...], wx_s[...], bx_ref[...], sp, first_row)
        hv = h_ref[...]
        hprev = jnp.where(has_prev, hp_ref[...], 0.0)
        h_m1 = _shift_down(hv, hprev, 1, rows)
        ga = ga_ref[...].astype(F32)
        sg = _sigmoid(ga)
        dya = dya_ref[...]
        dga = dya * hv * (sg * (1.0 + ga * (1.0 - sg)))

        av = jnp.where(rows == tm - 1, 1.0, pltpu.roll(a, tm - 1, 0))
        gv = dya * (ga * sg)
        rows8 = rows & 7
        for d in (1, 2, 4):
            keep = rows8 < 8 - d
            gv = gv + av * jnp.where(keep, _roll_in_groups(gv, 8 - d), 0.0)
            av = av * jnp.where(keep, _roll_in_groups(av, 8 - d), 1.0)
        a_s[...] = av
        g_s[...] = gv
        carry = cg_s[0:1, :]
        for g in range(ng - 1, -1, -1):
            sl = slice(g * 8, (g + 1) * 8)
            ab, gb = a_s[sl, :], g_s[sl, :]
            g_s[sl, :] = gb + ab * carry
            carry = gb[0:1, :] + ab[0:1, :] * carry
        cg_s[0:1, :] = a[0:1, :] * carry

        gt = g_s[...]
        da = gt * h_m1
        ixc = ig * xc
        di = gt * mult * xc
        dxc = gt * mult * ig
        dlog_a = da * a + jnp.where(first_row, 0.0, gt * ixc * (-a2 / mult))
        sm_ref[3:4, :] += _rowsum(dlog_a * ((-RG_C) * r))
        dpr_f = dlog_a * ((-RG_C) * sp) * r * (1.0 - r)
        dpi_f = di * ig * (1.0 - ig)
        sm_ref[1:2, :] += _rowsum(dpr_f)
        sm_ref[2:3, :] += _rowsum(dpi_f)
        dpr = dpr_f.astype(BF16)
        dpi = dpi_f.astype(BF16)
        xcb = xc.astype(BF16)
        dwa_s[...] += _mm_tn(xcb, dpr)
        dwx_s[...] += _mm_tn(xcb, dpi)
        dxc = dxc + _mm_nt(dpr, wa_s[...]) + _mm_nt(dpi, wx_s[...])
        sm_ref[0:1, :] += _rowsum(dxc)
        for j in range(4):
            sm_ref[4 + j:5 + j, :] += _rowsum(dxc * taps[j])
        nxt = nxt_s[...]
        dxa = (dxc * cw[3] + _shift_up(dxc, nxt, 1, rows) * cw[2]
               + _shift_up(dxc, nxt, 2, rows) * cw[1] + _shift_up(dxc, nxt, 3, rows) * cw[0])
        nxt_s[...] = dxc[0:8, :]
        da_ref[:, 0:DG] = dxa.astype(BF16)
        da_ref[:, DG:D] = dga.astype(BF16)

        @pl.when(i == nt - 1)
        def _():
            _, s_neg = _sig_pair(lam_v)
            sm_ref[3:4, :] = sm_ref[3:4, :] * (-s_neg)
            rg_ref[...] = _blocks_from_dense(dwa_s[...], dwx_s[...])

    vec = lambda: pl.BlockSpec((1, DG), lambda i: (0, 0))
    blocks = lambda: pl.BlockSpec((DG, RGB), lambda i: (0, 0))
    prev8 = lambda: pl.BlockSpec((8, DG), lambda i: (jnp.maximum((nt - 1 - i) * (tm // 8) - 1, 0), 0))
    return pl.pallas_call(
        body, name="bwd_rglru", grid=(nt,),
        in_specs=[pl.BlockSpec((tm, DG), lambda i: (nt - 1 - i, 0)),
                  pl.BlockSpec((tm, DG), lambda i: (nt - 1 - i, 0)),
                  prev8(),
                  pl.BlockSpec((tm, DG), lambda i: (nt - 1 - i, 0)),
                  prev8(),
                  pl.BlockSpec((tm, DG), lambda i: (nt - 1 - i, 0)),
                  pl.BlockSpec((NSHARD, 4, 128), lambda i: (0, 0, 0)), vec(),
                  blocks(), vec(), blocks(), vec(), vec(), ANY_SPEC],
        out_specs=[pl.BlockSpec((tm, D), lambda i: (nt - 1 - i, 0)),
                   pl.BlockSpec((DG, 128), lambda i: (0, 0)),
                   pl.BlockSpec((8, DG), lambda i: (0, 0))],
        out_shape=[jax.ShapeDtypeStruct((T, D), BF16), jax.ShapeDtypeStruct((DG, 128), F32),
                   jax.ShapeDtypeStruct((8, DG), F32)],
        scratch_shapes=[pltpu.VMEM((tm, DG), F32), pltpu.VMEM((tm, DG), F32),
                        pltpu.VMEM((8, DG), F32), pltpu.VMEM((8, DG), F32),
                        pltpu.VMEM((DG, DG), BF16), pltpu.VMEM((DG, DG), BF16),
                        pltpu.VMEM((DG, DG), F32), pltpu.VMEM((DG, DG), F32)],
        compiler_params=_cparams(("arbitrary",), 56),
    )(pf, pb, pf, h, h, dyab, cw3, conv_b, wa_c, ba, wx_c, bx, lam, dep)


def _bwd_hgrn2(pf, pb, o, s_prev, dyab, hg_lb, hg_nw, dep):
    tm = 512
    nt = T // tm
    nc = tm // CH

    def body(q_ref, f_ref, v_ref, g_ref, o_ref, sp_ref, dy_ref, lb_ref, nw_ref, dep_ref, db_ref, sm_ref,
             dst_s, eb_s, enb_s, ekd_s, qe_s, ke_s, kd_s, vb_s, do_s, dec_s, ddec_s, p_s, dp_s,
             g_s, dsta_s, dva_s, dqe_s, dke_s, dkd_s, dlf_s):
        i = pl.program_id(0)

        @pl.when(i == 0)
        def _():
            sm_ref[...] = jnp.zeros_like(sm_ref)
            dst_s[...] = jnp.zeros_like(dst_s)

        lb, one_m_lb = _hg_lower_bound(lb_ref)
        rows64 = lax.broadcasted_iota(I32, (tm, DG), 0) & (CH - 1)
        sg, sn, f, k, q, sq, qs = _hg_recompute(
            q_ref, f_ref, lb, one_m_lb, rows64, eb_s, enb_s, ekd_s, qe_s, ke_s, kd_s, dec_s)
        vb_s[...] = v_ref[...]

        nw = nw_ref[...]
        for hd in range(NH):
            cols = slice(hd * HD, (hd + 1) * HD)
            g = g_ref[:, cols].astype(F32)
            sgg = _sigmoid(g)
            o = o_ref[:, cols]
            so = lax.rsqrt(jnp.mean(o * o, axis=-1, keepdims=True) + EPS)
            oh = o * so
            dyb = dy_ref[:, cols]
            db_ref[:, 3 * DG + hd * HD:3 * DG + (hd + 1) * HD] = (
                dyb * (oh * nw) * (sgg * (1.0 + g * (1.0 - sgg)))).astype(BF16)
            don = dyb * (g * sgg)
            sm_ref[1:2, 0:HD] += _rowsum(don * oh)
            gw = don * nw
            do_s[:, cols] = (so * (gw - oh * jnp.mean(gw * oh, axis=-1, keepdims=True))).astype(BF16)

        mask = _tri(True)
        items = [(c, hd, slice(c * CH, (c + 1) * CH), slice(hd * HD, (hd + 1) * HD))
                 for c in range(nc) for hd in range(NH)]
        for c, hd, rs, cols in items:
            p_s[c * NH + hd] = jnp.where(mask, _mm_nt(qe_s[rs, cols], ke_s[rs, cols]), 0.0).astype(BF16)
            dp_s[c * NH + hd] = jnp.where(mask, _mm_nt(do_s[rs, cols], vb_s[rs, cols]), 0.0).astype(BF16)
        for c, hd, rs, cols in items:
            n = c * NH + hd
            dva_s[rs, cols] = _mm_tn(p_s[n], do_s[rs, cols])
            dqe_s[rs, cols] = _mm(dp_s[n], ke_s[rs, cols])
            dke_s[rs, cols] = _mm_tn(dp_s[n], qe_s[rs, cols])
            g_s[n] = _mm_tn(do_s[rs, cols], qe_s[rs, cols])
        for c, hd, rs, cols in reversed(items):
            n = c * NH + hd
            dst = dst_s[hd]
            dsta_s[n] = dst
            dst_s[hd] = dst * dec_s[c:c + 1, cols] + g_s[n]
        for c, hd, rs, cols in items:
            n = c * NH + hd
            dst = dsta_s[n]
            st_prev = sp_ref[hd, c]
            dv = dva_s[rs, cols] + _mm_nt(kd_s[rs, cols], dst)
            db_ref[rs, 2 * DG + hd * HD:2 * DG + (hd + 1) * HD] = dv.astype(BF16)
            dqe_s[rs, cols] += _mm(do_s[rs, cols], st_prev)
            dkd_s[rs, cols] = _mm(vb_s[rs, cols], dst)
            ddec_s[c:c + 1, cols] = _rowsum(dst * st_prev)

        eb, enb, ekd = eb_s[...], enb_s[...], ekd_s[...]
        dqe, dke, dkd = dqe_s[...], dke_s[...], dkd_s[...]
        t_kd = dkd * (k * ekd)
        rc = _chunk_rev_cumsum(dqe * (qs * eb) - dke * (k * enb) - t_kd, rows64)
        for c in range(nc):
            rs = slice(c * CH, (c + 1) * CH)
            dbl = _rowsum(t_kd[rs]) + ddec_s[c:c + 1, :] * dec_s[c:c + 1, :]
            dlf_s[rs, :] = rc[rs] + dbl
        t = dlf_s[...] / f - (dke * enb + dkd * ekd)
        db_ref[:, DG:2 * DG] = (one_m_lb * sg * sn * t).astype(BF16)
        sm_ref[0:1, :] += _rowsum(sn * t)
        db_ref[:, 0:DG] = (dqe * eb * (sq * (1.0 + q * (1.0 - sq))) * (HD ** -0.5)).astype(BF16)

        @pl.when(i == nt - 1)
        def _():
            dsm = sm_ref[0:1, :] * (lb * one_m_lb)
            sm_ref[2:3, :] = dsm
            sm_ref[3:4, :] = -dsm

    col = lambda j: pl.BlockSpec((tm, DG), lambda i: (nt - 1 - i, j))
    big = lambda dt: pltpu.VMEM((tm, DG), dt)
    return pl.pallas_call(
        body, name="bwd_hgrn2", grid=(nt,),
        in_specs=[col(1), col(1), col(2), col(3),
                  pl.BlockSpec((tm, DG), lambda i: (nt - 1 - i, 0)),
                  pl.BlockSpec((NH, nc, HD, HD), lambda i: (0, nt - 1 - i, 0, 0)),
                  pl.BlockSpec((tm, DG), lambda i: (nt - 1 - i, 1)),
                  pl.BlockSpec((2, DG), lambda i: (0, 0)),
                  pl.BlockSpec((1, HD), lambda i: (0, 0)), ANY_SPEC],
        out_specs=[pl.BlockSpec((tm, 4 * DG), lambda i: (nt - 1 - i, 0)),
                   pl.BlockSpec((8, DG), lambda i: (0, 0))],
        out_shape=[jax.ShapeDtypeStruct((T, 4 * DG), BF16), jax.ShapeDtypeStruct((8, DG), F32)],
        scratch_shapes=[pltpu.VMEM((NH, HD, HD), F32),
                        big(F32), big(F32), big(F32),
                        big(BF16), big(BF16), big(BF16), big(BF16), big(BF16),
                        pltpu.VMEM((nc, DG), F32), pltpu.VMEM((nc, DG), F32),
                        pltpu.VMEM((nc * NH, CH, CH), BF16), pltpu.VMEM((nc * NH, CH, CH), BF16),
                        pltpu.VMEM((nc * NH, HD, HD), F32), pltpu.VMEM((nc * NH, HD, HD), F32),
                        big(F32), big(F32), big(F32), big(F32), big(F32)],
        compiler_params=_cparams(("arbitrary",), 56),
    )(pb, pf, pb, pb, o, s_prev, dyab, hg_lb, hg_nw, dep)


def _dproj_pieces(k, da_ref, db_ref):
    if k == 0:
        return [(da_ref[:, 0:SHW], 0)]
    if k == 1:
        return [(da_ref[:, SHW:D], 0), (db_ref[:, 0:DG], D - SHW)]
    if k == 2:
        return [(db_ref[:, DG:DG + SHW], 0)]
    return [(db_ref[:, DG + SHW:4 * DG], 0)]


def _bwd_inproj_dx(x, dh1, d_a, d_b, w_in_b, nw, dep):
    tm = 512
    nt = T // tm

    def body(x_ref, dh1_ref, da_ref, db_ref, w_ref, nw_ref, dep_ref, dx_ref, sm_ref):
        i = pl.program_id(0)

        @pl.when(i == 0)
        def _():
            sm_ref[...] = jnp.zeros_like(sm_ref)

        du = None
        for k in range(NSHARD):
            for val, off in _dproj_pieces(k, da_ref, db_ref):
                t = _mm_nt(val, w_ref[k, :, off:off + val.shape[1]])
                du = t if du is None else du + t
        xv = x_ref[...]
        s = lax.rsqrt(jnp.mean(xv * xv, axis=-1, keepdims=True) + EPS)
        xh = xv * s
        sm_ref[0:1, :] += _rowsum(du * xh)
        g = du * nw_ref[...]
        dx_ref[...] = dh1_ref[...] + s * (g - xh * jnp.mean(g * xh, axis=-1, keepdims=True))

    row = lambda w: pl.BlockSpec((tm, w), lambda i: (i, 0))
    return pl.pallas_call(
        body, name="bwd_inproj_dx", grid=(nt,),
        in_specs=[row(D), row(D), row(D), row(4 * DG),
                  pl.BlockSpec((NSHARD, D, SHW), lambda i: (0, 0, 0), pipeline_mode=pl.Buffered(1)),
                  pl.BlockSpec((1, D), lambda i: (0, 0)), ANY_SPEC],
        out_specs=[row(D), pl.BlockSpec((8, D), lambda i: (0, 0))],
        out_shape=[jax.ShapeDtypeStruct((T, D), F32), jax.ShapeDtypeStruct((8, D), F32)],
        compiler_params=_cparams(("arbitrary",), 56),
    )(x, dh1, d_a, d_b, w_in_b, nw, dep)


def _bwd_inproj_dw23(u_b, d_b):
    tm = 1024
    nt = T // tm

    def body(u_ref, db_ref, dw_ref, acc):
        i = pl.program_id(0)

        @pl.when(i == 0)
        def _():
            acc[...] = jnp.zeros_like(acc)

        u = u_ref[...]
        for j, k in enumerate((2, 3)):
            for val, off in _dproj_pieces(k, None, db_ref):
                acc[j, :, off:off + val.shape[1]] += _mm_tn(u, val)

        @pl.when(i == nt - 1)
        def _():
            for j in range(2):
                for r0 in range(0, D, 256):
                    dw_ref[j, r0:r0 + 256, :] = acc[j, r0:r0 + 256, :].astype(BF16)

    row = lambda w: pl.BlockSpec((tm, w), lambda i: (i, 0))
    return pl.pallas_call(
        body, name="bwd_inproj_dw23", grid=(nt,), in_specs=[row(D), row(4 * DG)],
        out_specs=pl.BlockSpec((2, D, SHW), lambda i: (0, 0, 0)),
        out_shape=pltpu.HBM((2, D, SHW), BF16),
        scratch_shapes=[pltpu.VMEM((2, D, SHW), F32)],
        compiler_params=_cparams(("arbitrary",), 48),
    )(u_b, d_b)


def _bwd_inproj_dw01(u_b, d_a, d_b, dw23):
    tm = 1024
    nt = T // tm
    H = D // 2
    step = 128

    def body(u_ref, da_ref, db_ref, p23_ref, dw_ref, acc, send_s, recv_s, ssem, rsem):
        i = pl.program_id(0)
        x, y, c = lax.axis_index("x"), lax.axis_index("y"), lax.axis_index("c")
        sibling = (x, y, 1 - c)
        mine0 = pl.multiple_of(c * H, step)
        other0 = pl.multiple_of((1 - c) * H, step)

        def swap(k):
            return _remote(send_s.at[k], recv_s.at[k], ssem.at[k], rsem.at[k], sibling)

        @pl.when(i == 0)
        def _():
            acc[...] = jnp.zeros_like(acc)
            for k in (2, 3):
                for r0 in range(0, H, step):
                    send_s[k, r0:r0 + step, :] = p23_ref[k - 2, pl.ds(other0 + r0, step), :]
                swap(k).start()

        u = u_ref[...]
        for j, k in enumerate((0, 1)):
            for val, off in _dproj_pieces(k, da_ref, db_ref):
                acc[j, :, off:off + val.shape[1]] += _mm_tn(u, val)

        @pl.when(i == nt - 1)
        def _():
            copies = [swap(k) for k in range(NSHARD)]
            for k in (0, 1):
                for r0 in range(0, H, step):
                    send_s[k, r0:r0 + step, :] = acc[k, pl.ds(other0 + r0, step), :].astype(BF16)
                copies[k].start()
            for k in (2, 3, 0, 1):
                copies[k].wait_recv()
                for r0 in range(0, H, step):
                    rs = pl.ds(mine0 + r0, step)
                    own = acc[k, rs, :] if k < 2 else p23_ref[k - 2, rs, :].astype(F32)
                    dw_ref[k, r0:r0 + step, :] = (own + recv_s[k, r0:r0 + step, :].astype(F32)).astype(BF16)
            for cp in copies:
                cp.wait_send()

    row = lambda w: pl.BlockSpec((tm, w), lambda i: (i, 0))
    return pl.pallas_call(
        body, name="bwd_inproj_dw01", grid=(nt,),
        in_specs=[row(D), row(D), row(4 * DG),
                  pl.BlockSpec((2, D, SHW), lambda i: (0, 0, 0), pipeline_mode=pl.Buffered(1))],
        out_specs=pl.BlockSpec((NSHARD, H, SHW), lambda i: (0, 0, 0)),
        out_shape=jax.ShapeDtypeStruct((NSHARD, H, SHW), BF16),
        scratch_shapes=[pltpu.VMEM((2, D, SHW), F32), pltpu.VMEM((NSHARD, H, SHW), BF16),
                        pltpu.VMEM((NSHARD, H, SHW), BF16),
                        pltpu.SemaphoreType.DMA((NSHARD,)), pltpu.SemaphoreType.DMA((NSHARD,))],
        compiler_params=_cparams(("arbitrary",), 56),
    )(u_b, d_a, d_b, dw23)


_OUT_ORDER = ["norm_mix_w", "w_in", "conv_w", "conv_b", "rg_wa", "rg_ba", "rg_wx", "rg_bx", "rg_lambda", "hg_lb",
              "hg_norm_w", "w_out", "ple_norm_w", "w_ple_gate", "b_ple_gate", "w_ple_proj", "final_norm_w"]
_BIG = ["w_in", "w_out", "w_ple_gate", "w_ple_proj"]


def _small_view(name, a):
    if name in ("rg_wa", "rg_wx"):
        return a.reshape(DG, RGB)
    if name == "conv_w":
        return a.reshape(4, 128)
    if name == "final_norm_w":
        return a.reshape(1, D)
    return a


def _landing(rows, cols):
    return lax.empty((NDEV, rows, cols), BF16)


def kernel(x, p, norm_mix_w, w_in, conv_w, conv_b, rg_wa, rg_ba, rg_wx, rg_bx, rg_lambda, hg_lb, hg_norm_w, w_out, ple_norm_w, w_ple_gate, b_ple_gate, w_ple_proj, final_norm_w, loss_target, m_norm_mix_w, m_w_in, m_conv_w, m_conv_b, m_rg_wa, m_rg_ba, m_rg_wx, m_rg_bx, m_rg_lambda, m_hg_lb, m_hg_norm_w, m_w_out, m_ple_norm_w, m_w_ple_gate, m_b_ple_gate, m_w_ple_proj, m_final_norm_w, v_norm_mix_w, v_w_in, v_conv_w, v_conv_b, v_rg_wa, v_rg_ba, v_rg_wx, v_rg_bx, v_rg_lambda, v_hg_lb, v_hg_norm_w, v_w_out, v_ple_norm_w, v_w_ple_gate, v_b_ple_gate, v_w_ple_proj, v_final_norm_w):
    given = dict(locals())
    x2, p2, tgt = x[0], p[0, 0], loss_target[0]
    hbm = lambda a: pltpu.with_memory_space_constraint(a, pltpu.HBM)
    norm_mix_w, conv_b, rg_ba, rg_bx, rg_lambda, hg_lb, hg_norm_w, ple_norm_w, b_ple_gate = (
        hbm(a) for a in (norm_mix_w, conv_b, rg_ba, rg_bx, rg_lambda, hg_lb, hg_norm_w, ple_norm_w, b_ple_gate))
    wa_c, wx_c = hbm(_small_view("rg_wa", rg_wa)), hbm(_small_view("rg_wx", rg_wx))

    w_in_b, l_out, l_pg, l_pp, cw3, pf, pb, u_b = _gather_inproj(
        x2, norm_mix_w, w_in[0], w_out[0], w_ple_gate[0], w_ple_proj[0], conv_w[0])
    g_ssem, g_rsem, g_lands, tok = _gather_rest_start([l_out, l_pg, l_pp])

    h, ya = _fwd_rglru(pf, pb, cw3, conv_b, wa_c, rg_ba, wx_c, rg_bx, rg_lambda, tok)
    yb, o, s_prev = _fwd_hgrn2(pf, pb, hg_lb, hg_norm_w)
    w_out_b, w_pg_b, w_pp_b = _gather_rest_wait(g_ssem, g_rsem, g_lands, yb)
    dh1, dyab, dwo_b, dwg_b, dwp_b, sm_tail = _tail_fwd_bwd(
        x2, p2, tgt, ya, yb, w_out_b.reshape(D, D), w_pg_b.reshape(D, D), w_pp_b,
        ple_norm_w, b_ple_gate, final_norm_w.reshape(1, D))

    QH = D // NSHARD // 2
    r1 = _rs_start("rs_start_tail", [dwo_b, dwg_b, dwp_b], (0, 1, 2, 3),
                   [_landing(QH, D), _landing(QH, D), _landing(PLE // 2, PLE)])
    d_b, sm_b = _bwd_hgrn2(pf, pb, o, s_prev, dyab, hg_lb, hg_norm_w, r1[4])
    dw23 = _bwd_inproj_dw23(u_b, d_b)
    d_a, rg_c, sm_a = _bwd_rglru(pf, pb, h, dyab, cw3, conv_b, wa_c, rg_ba, wx_c, rg_bx, rg_lambda, dw23)
    dw_in = _bwd_inproj_dw01(u_b, d_a, d_b, dw23)
    r2 = _rs_start("rs_start_in", [dw_in], (0, 1, 2, 3), [lax.empty((NSHARD, D // 2, SHW), BF16)], chip_sums=True)
    grad_x, sm_in = _bwd_inproj_dx(x2, dh1, d_a, d_b, w_in_b, norm_mix_w, r2[4])

    parts1, lands1 = _rs_wait("rs_wait_tail", r1[0], r1[1], r1[2], (0, 1, 2, 3), r1[3], sm_in)
    parts2, lands2 = _rs_wait("rs_wait_in", r2[0], r2[1], r2[2], (0, 1, 2, 3), r2[3], sm_in, chip_sums=True)
    g_big, vred, rgred = _final_reduce(parts2 + parts1, lands2 + lands1, sm_in, sm_tail, sm_a, sm_b, rg_c)

    upd_big = _adam_big(g_big, [given[n][0] for n in _BIG], [given["m_" + n][0] for n in _BIG],
                        [given["v_" + n][0] for n in _BIG])
    small = _adam_small(vred, rgred,
                        {n: _small_view(n, given[n]) for n in _SMALL_ORDER},
                        {n: _small_view(n, given["m_" + n]) for n in _SMALL_ORDER},
                        {n: _small_view(n, given["v_" + n]) for n in _SMALL_ORDER})

    loss = vred[0, ROW_LOSS, 0]
    outs = [loss, grad_x[None]]
    for ki in range(4):
        for n in _OUT_ORDER:
            if n in _BIG:
                i = _BIG.index(n)
                a = g_big[i] if ki == 0 else upd_big[i][ki - 1]
                outs.append(a[None])
            else:
                outs.append(small[n][ki].reshape(given[n].shape))
    return tuple(outs)
```

```python
import jax
import jax.numpy as jnp
from jax import lax
from jax.experimental import pallas as pl
from jax.experimental.pallas import tpu as pltpu

F32 = jnp.float32
BF16 = jnp.bfloat16
I32 = jnp.int32
MESH = pl.DeviceIdType.MESH

T = 4096
D = 1024
DG = 512
DIN = 3072
PLE = 256
NH = 4
HD = 128
CH = 64
NCHUNK = T // CH
RGB = 64
EPS = 1e-6
RG_C = 8.0
NSHARD = 4
SHW = DIN // NSHARD
NDEV = 8

ADAM_LR = 0.001
ADAM_B1 = 0.9
ADAM_B2 = 0.999
ADAM_EPS = 1e-08
ADAM_WD = 0.01
ADAM_STEP = 10

VMEM_SPEC = pl.BlockSpec(memory_space=pltpu.VMEM)
HBM_SPEC = pl.BlockSpec(memory_space=pltpu.HBM)
SEM_SPEC = pl.BlockSpec(memory_space=pltpu.SEMAPHORE)
ANY_SPEC = pl.BlockSpec(memory_space=pl.ANY)
EFFECT = pltpu.SideEffectType.DATAFLOW_SIDE_EFFECTING
MIB = 1024 * 1024

VROWS = 16
ROW_NORM_MIX, ROW_FINAL_NORM, ROW_B_PG, ROW_PLE_NORM = 0, 1, 2, 3
ROW_CB_BA, ROW_BX_LAM, ROW_CW01, ROW_CW23, ROW_HG_LB, ROW_HG_NW, ROW_LOSS = 4, 5, 6, 7, 8, 9, 10


def _mm(a, b):
    return jnp.dot(a.astype(BF16), b.astype(BF16), preferred_element_type=F32)


def _mm_nt(a, b):
    return lax.dot_general(a.astype(BF16), b.astype(BF16), (((1,), (1,)), ((), ())),
                           preferred_element_type=F32)


def _mm_tn(a, b):
    return lax.dot_general(a.astype(BF16), b.astype(BF16), (((0,), (0,)), ((), ())),
                           preferred_element_type=F32)


def _mm_exact(a, b):
    bb = b.astype(BF16)
    hi = a.astype(BF16)
    r1 = a - hi.astype(F32)
    mid = r1.astype(BF16)
    lo = (r1 - mid.astype(F32)).astype(BF16)
    return sum(jnp.dot(p, bb, preferred_element_type=F32) for p in (hi, mid, lo))


def _sig_pair(x):
    e = jnp.exp(-jnp.abs(x))
    big = 1.0 / (1.0 + e)
    small = e * big
    pos = x >= 0
    return jnp.where(pos, big, small), jnp.where(pos, small, big)


def _sigmoid(x):
    return 1.0 / (1.0 + jnp.exp(-x))


def _rowsum(v):
    return jnp.sum(v, axis=0, keepdims=True)


def _shift_down(cur, prev8, d, rows):
    rolled = pltpu.roll(cur, d, 0)
    head = jnp.where(rows[0:8] < d, pltpu.roll(prev8, d, 0), rolled[0:8])
    return jnp.concatenate([head, rolled[8:]], axis=0)


def _shift_up(cur, next8, d, rows):
    n = cur.shape[0]
    rolled = pltpu.roll(cur, n - d, 0)
    tail = jnp.where(rows[0:8] >= 8 - d, pltpu.roll(next8, 8 - d, 0), rolled[n - 8:n])
    return jnp.concatenate([rolled[0:n - 8], tail], axis=0)


def _roll_in_groups(v, d):
    n, w = v.shape
    return pltpu.roll(v.reshape(n // 8, 8, w), d, 1).reshape(n, w)


def _cparams(sem, vmem_mib):
    return pltpu.CompilerParams(dimension_semantics=sem, vmem_limit_bytes=vmem_mib * MIB)


def _mesh_pos():
    x, y, c = lax.axis_index("x"), lax.axis_index("y"), lax.axis_index("c")
    chips = [(1 - x, y), (x, 1 - y), (1 - x, 1 - y)]
    return x, y, c, chips


def _remote(src, dst, ssem, rsem, dev):
    return pltpu.make_async_remote_copy(src_ref=src, dst_ref=dst, send_sem=ssem, recv_sem=rsem,
                                        device_id=dev, device_id_type=MESH)


def _gather_inproj(x, nw, w_in, w_out, w_pg, w_pp, conv_w):
    shapes = [w_in.shape, w_out.shape, w_pg.shape, w_pp.shape]
    tm = 1024
    nt = T // tm

    def body(x_h, nw_h, win_h, wout_h, wpg_h, wpp_h, cw_h,
             o_in_h, o_out_h, o_pg_h, o_pp_h, o_cw_h, pf_h, pb_h, u_h,
             win, wout, wpg, wpp, cw, nw_s, xbuf, ubuf, rfbuf, rbbuf, o_in, o_out, o_pg, o_pp, o_cw,
             lsem, xsem, usem, fsem, bsem, osem, ssem, rsem):
        x, y, c, chips = _mesh_pos()
        kme = 2 * x + y
        sibling = (x, y, 1 - c)
        fetch = [pltpu.make_async_copy(src, dst, lsem.at[i]) for i, (src, dst) in enumerate(
            [(win_h, win), (cw_h, cw), (nw_h, nw_s), (wout_h, wout), (wpg_h, wpg), (wpp_h, wpp)])]
        for cp in fetch:
            cp.start()

        def cast(src, dst):
            for r0 in range(0, src.shape[0], 128):
                dst[r0:r0 + 128, :] = src[r0:r0 + 128, :].astype(BF16)

        outs = []

        def write_out(src, dst, i):
            cp = pltpu.make_async_copy(src, dst, osem.at[i])
            cp.start()
            outs.append(cp)

        hrows = D // 2
        mine = pl.ds(pl.multiple_of(c * hrows, 128), hrows)
        other = pl.ds(pl.multiple_of((1 - c) * hrows, 128), hrows)

        def cast_half(rows):
            for r0 in range(0, hrows, 128):
                rs = pl.ds(pl.multiple_of(rows.start + r0, 128), 128)
                o_in[kme, rs, :] = win[rs, :].astype(BF16)

        fetch[0].wait()
        cast_half(mine)
        fetch[1].wait()
        o_cw[kme] = cw[...]

        def half(k, rows):
            return o_in.at[k, rows]

        na = (x ^ c, y ^ (1 - c), c)
        nb = (x ^ (1 - c), y ^ c, c)
        ka = 2 * na[0] + na[1]
        kb = 2 * nb[0] + nb[1]
        kd = 2 * (1 - x) + (1 - y)
        sends = [_remote(half(kme, mine), half(kme, mine), ssem.at[0], rsem.at[0], na),
                 _remote(half(kme, mine), half(kme, mine), ssem.at[1], rsem.at[1], nb)]
        for j, (px, py) in enumerate(chips):
            sends.append(_remote(o_cw.at[kme], o_cw.at[kme], ssem.at[3 + j], rsem.at[3 + j], (px, py, c)))
        for cp in sends:
            cp.start()
        cast_half(other)
        write_out(o_in.at[kme], o_in_h.at[kme], 0)
        fetch[2].wait()

        def rows_of(t):
            return pl.ds(pl.multiple_of(t * tm, tm), tm)

        def x_copy(t, slot):
            return pltpu.make_async_copy(x_h.at[rows_of(t)], xbuf.at[slot], xsem.at[slot])

        u_out = pltpu.make_async_copy(ubuf, u_h, usem.at[0])

        def f_store(t, slot, kh):
            return pltpu.make_async_copy(rfbuf.at[slot], pf_h.at[rows_of(t), pl.ds(pl.multiple_of(kh * DG, DG), DG)],
                                         fsem.at[slot])

        def b_store_even(t, slot, kh):
            return pltpu.make_async_copy(rbbuf.at[slot, :, 0:256],
                                         pb_h.at[rows_of(t), pl.ds(pl.multiple_of(kh * 2 * DG, 256), 256)],
                                         bsem.at[slot])

        def b_store_odd(t, slot, kh):
            return pltpu.make_async_copy(rbbuf.at[slot],
                                         pb_h.at[rows_of(t), pl.ds(pl.multiple_of(256 + kh * 2 * DG, 256), SHW)],
                                         bsem.at[slot])

        def run_pass(k, first, even):
            kh = k >> 1
            if first:
                x_copy(0, 0).start()

            def tile(t, carry):
                slot = t & 1
                if first:
                    @pl.when(t + 1 < nt)
                    def _():
                        x_copy(t + 1, 1 - slot).start()

                    x_copy(t, slot).wait()

                @pl.when(t >= 2)
                def _():
                    if even:
                        f_store(t - 2, slot, kh).wait()
                        b_store_even(t - 2, slot, kh).wait()
                    else:
                        b_store_odd(t - 2, slot, kh).wait()

                if first:
                    xv = xbuf[slot]
                    s = lax.rsqrt(jnp.mean(xv * xv, axis=-1, keepdims=True) + EPS)
                    u = (xv * s * nw_s[...]).astype(BF16)
                    ubuf[rows_of(t), :] = u
                else:
                    u = ubuf[rows_of(t), :]
                r = jnp.dot(u, o_in[k], preferred_element_type=F32)
                if even:
                    rfbuf[slot] = r[:, 0:DG]
                    rbbuf[slot, :, 0:256] = r[:, DG:SHW].astype(BF16)
                    f_store(t, slot, kh).start()
                    b_store_even(t, slot, kh).start()
                else:
                    rbbuf[slot] = r.astype(BF16)
                    b_store_odd(t, slot, kh).start()
                return carry

            lax.fori_loop(0, nt, tile, 0)
            for t in (nt - 2, nt - 1):
                slot = t & 1
                if even:
                    f_store(t, slot, kh).wait()
                    b_store_even(t, slot, kh).wait()
                else:
                    b_store_odd(t, slot, kh).wait()

        def project(k, first=False):
            @pl.when((k & 1) == 0)
            def _():
                run_pass(k, first, True)

            @pl.when((k & 1) == 1)
            def _():
                run_pass(k, first, False)

        def passed_on(k, sem, dev):
            cp = _remote(half(k, mine), half(k, mine), ssem.at[sem], rsem.at[sem], dev)
            cp.start()
            sends.append(cp)

        project(kme, first=True)
        u_out.start()
        for i, (src, dst, dst_h) in enumerate([(wout, o_out, o_out_h), (wpg, o_pg, o_pg_h), (wpp, o_pp, o_pp_h)]):
            fetch[3 + i].wait()
            cast(src, dst)
            write_out(dst, dst_h.at[kme], 4 + i)
        _remote(half(ka, mine), half(ka, mine), ssem.at[0], rsem.at[0], na).wait_recv()
        passed_on(ka, 2, nb)
        passed_on(ka, 6, sibling)
        _remote(half(kb, mine), half(kb, mine), ssem.at[1], rsem.at[1], nb).wait_recv()
        passed_on(kb, 7, sibling)
        _remote(half(ka, other), half(ka, other), ssem.at[7], rsem.at[7], sibling).wait_recv()
        write_out(o_in.at[ka], o_in_h.at[ka], 1)
        project(ka)
        _remote(half(kd, mine), half(kd, mine), ssem.at[2], rsem.at[2], nb).wait_recv()
        passed_on(kd, 8, sibling)
        _remote(half(kb, other), half(kb, other), ssem.at[6], rsem.at[6], sibling).wait_recv()
        write_out(o_in.at[kb], o_in_h.at[kb], 2)
        project(kb)
        _remote(half(kd, other), half(kd, other), ssem.at[8], rsem.at[8], sibling).wait_recv()
        write_out(o_in.at[kd], o_in_h.at[kd], 3)
        project(kd)
        u_out.wait()
        for j, (px, py) in enumerate(chips):
            kj = 2 * px + py
            _remote(o_cw.at[kj], o_cw.at[kj], ssem.at[3 + j], rsem.at[3 + j], (px, py, c)).wait_recv()
        write_out(o_cw, o_cw_h, 7)
        for cp in outs:
            cp.wait()
        for cp in sends:
            cp.wait_send()

    out_shape = [pltpu.HBM((NSHARD,) + s, BF16) for s in shapes]
    out_shape.append(pltpu.HBM((NSHARD,) + conv_w.shape, F32))
    out_shape += [pltpu.HBM((T, 2 * DG), F32), pltpu.HBM((T, 4 * DG), BF16), pltpu.HBM((T, D), BF16)]
    hbm_args = [pltpu.with_memory_space_constraint(a, pltpu.HBM) for a in (x, nw, w_in, w_out, w_pg, w_pp, conv_w)]
    return pl.pallas_call(
        body, name="gather_inproj", out_shape=out_shape,
        in_specs=[HBM_SPEC] * 7, out_specs=[HBM_SPEC] * 8,
        scratch_shapes=[pltpu.VMEM(a.shape, F32) for a in (w_in, w_out, w_pg, w_pp, conv_w, nw)]
        + [pltpu.VMEM((2, tm, D), F32), pltpu.VMEM((T, D), BF16),
           pltpu.VMEM((2, tm, DG), F32), pltpu.VMEM((2, tm, SHW), BF16),
           pltpu.VMEM((NSHARD,) + w_in.shape, BF16), pltpu.VMEM(w_out.shape, BF16), pltpu.VMEM(w_pg.shape, BF16),
           pltpu.VMEM(w_pp.shape, BF16), pltpu.VMEM((NSHARD,) + conv_w.shape, F32),
           pltpu.SemaphoreType.DMA((6,)), pltpu.SemaphoreType.DMA((2,)), pltpu.SemaphoreType.DMA((2,)),
           pltpu.SemaphoreType.DMA((2,)), pltpu.SemaphoreType.DMA((2,)), pltpu.SemaphoreType.DMA((8,)),
           pltpu.SemaphoreType.DMA((9,)), pltpu.SemaphoreType.DMA((9,))],
        compiler_params=pltpu.CompilerParams(vmem_limit_bytes=48 * MIB),
    )(*hbm_args)


def _gather_rest_start(lands):
    n = len(lands)

    def body(*refs):
        land_in = refs[0:n]
        ssem, rsem = refs[n], refs[n + 1]
        token = refs[2 * n + 2]
        x, y, c, chips = _mesh_pos()
        kme = 2 * x + y
        for p, land in enumerate(land_in):
            hrows = land.shape[1] // 2
            mine = pl.ds(pl.multiple_of(c * hrows, 128), hrows)
            for px, py in chips:
                for pc in range(2):
                    _remote(land.at[kme, mine], land.at[kme, mine], ssem.at[p], rsem.at[p], (px, py, pc)).start()
        token[...] = jnp.zeros_like(token)

    out_shape = ([pltpu.SemaphoreType.DMA((n,)), pltpu.SemaphoreType.DMA((n,))]
                 + [pltpu.HBM(a.shape, a.dtype) for a in lands] + [jax.ShapeDtypeStruct((8, 128), F32)])
    outs = pl.pallas_call(
        body, name="gather_rest_start", out_shape=out_shape,
        in_specs=[HBM_SPEC] * n, out_specs=[SEM_SPEC, SEM_SPEC] + [HBM_SPEC] * n + [VMEM_SPEC],
        input_output_aliases={i: 2 + i for i in range(n)},
        compiler_params=pltpu.CompilerParams(has_side_effects=EFFECT),
    )(*[pltpu.with_memory_space_constraint(a, pltpu.HBM) for a in lands])
    return outs[0], outs[1], list(outs[2:2 + n]), outs[2 + n]


def _gather_rest_wait(ssem, rsem, lands, after):
    n = len(lands)

    def body(*refs):
        land_in = refs[0:n]
        ssem_ref, rsem_ref = refs[n], refs[n + 1]
        x, y, c = lax.axis_index("x"), lax.axis_index("y"), lax.axis_index("c")
        for p, land in enumerate(land_in):
            three = land.at[pl.ds(0, 3)]
            cp = _remote(three, three, ssem_ref.at[p], rsem_ref.at[p], (x, y, c))
            cp.wait_send()
            cp.wait_recv()

    outs = pl.pallas_call(
        body, name="gather_rest_wait", out_shape=[pltpu.HBM(a.shape, a.dtype) for a in lands],
        in_specs=[HBM_SPEC] * n + [SEM_SPEC, SEM_SPEC, ANY_SPEC], out_specs=[HBM_SPEC] * n,
        input_output_aliases={i: i for i in range(n)},
        compiler_params=pltpu.CompilerParams(has_side_effects=EFFECT),
    )(*lands, ssem, rsem, after)
    return list(outs)


def _rs_start(name, parts, ks, lands, chip_sums=False):
    n = len(parts)

    def body(*refs):
        part_in, land_in = refs[0:n], refs[n:2 * n]
        ssem, rsem = refs[2 * n], refs[2 * n + 1]
        token = refs[4 * n + 2]
        x, y, c = lax.axis_index("x"), lax.axis_index("y"), lax.axis_index("c")
        kme = 2 * x + y
        me = 4 * x + 2 * y + c
        for p in range(n):
            hrows = land_in[p].shape[1]
            for i, k in enumerate(ks):
                if chip_sums:
                    @pl.when(kme != k)
                    def _():
                        _remote(part_in[p].at[i], land_in[p].at[kme], ssem.at[p], rsem.at[p], (k // 2, k % 2, c)).start()
                    continue
                for pc in range(2):
                    @pl.when(jnp.logical_or(kme != k, c != pc))
                    def _():
                        _remote(part_in[p].at[i, pl.ds(pc * hrows, hrows)], land_in[p].at[me],
                                ssem.at[p], rsem.at[p], (k // 2, k % 2, pc)).start()
        token[...] = jnp.zeros_like(token)

    arrays = list(parts) + list(lands)
    out_shape = ([pltpu.SemaphoreType.DMA((n,)), pltpu.SemaphoreType.DMA((n,))]
                 + [pltpu.HBM(a.shape, a.dtype) for a in arrays] + [jax.ShapeDtypeStruct((8, 128), F32)])
    outs = pl.pallas_call(
        body, name=name, out_shape=out_shape,
        in_specs=[HBM_SPEC] * (2 * n), out_specs=[SEM_SPEC, SEM_SPEC] + [HBM_SPEC] * (2 * n) + [VMEM_SPEC],
        input_output_aliases={i: 2 + i for i in range(2 * n)},
        compiler_params=pltpu.CompilerParams(has_side_effects=EFFECT),
    )(*[pltpu.with_memory_space_constraint(a, pltpu.HBM) for a in arrays])
    return outs[0], outs[1], list(outs[2:2 + n]), list(outs[2 + n:2 + 2 * n]), outs[2 + 2 * n]


def _rs_wait(name, ssem, rsem, parts, ks, lands, after, chip_sums=False):
    n = len(parts)

    def body(*refs):
        part_in, land_in = refs[0:n], refs[n:2 * n]
        ssem_ref, rsem_ref = refs[2 * n], refs[2 * n + 1]
        x, y, c = lax.axis_index("x"), lax.axis_index("y"), lax.axis_index("c")
        kme = 2 * x + y
        for p in range(n):
            piece = land_in[p].at[0]
            for k in ks:
                for pc in range(1 if chip_sums else 2):
                    mine = (kme == k) if chip_sums else jnp.logical_and(kme == k, c == pc)

                    @pl.when(jnp.logical_not(mine))
                    def _():
                        _remote(piece, piece, ssem_ref.at[p], rsem_ref.at[p], (x, y, c)).wait_send()
            owner = kme == ks[0]
            for k in ks[1:]:
                owner = jnp.logical_or(owner, kme == k)

            @pl.when(owner)
            def _():
                others = land_in[p].at[pl.ds(0, land_in[p].shape[0] - 1)]
                _remote(others, others, ssem_ref.at[p], rsem_ref.at[p], (x, y, c)).wait_recv()

    arrays = list(parts) + list(lands)
    outs = pl.pallas_call(
        body, name=name, out_shape=[pltpu.HBM(a.shape, a.dtype) for a in arrays],
        in_specs=[HBM_SPEC] * (2 * n) + [SEM_SPEC, SEM_SPEC, ANY_SPEC], out_specs=[HBM_SPEC] * (2 * n),
        input_output_aliases={i: i for i in range(2 * n)},
        compiler_params=pltpu.CompilerParams(has_side_effects=EFFECT),
    )(*arrays, ssem, rsem, after)
    return list(outs[0:n]), list(outs[n:2 * n])


def _final_reduce(parts, lands, sm_in, sm_tail, sm_a, sm_b, rg_c):
    shapes = [(2 * l.shape[1], l.shape[2]) for l in lands]
    step = 128
    PR = DG // NDEV

    def body(pin, pout, ppg, ppp, l_in, l_out, l_pg, l_pp, in_h, tail_h, a_h, b_h, rg_h,
             g_in_h, g_out_h, g_pg_h, g_pp_h, v_out, rg_out, vbuf, vrecv, rgrecv,
             lb_in, lb_out, lb_pg, lb_pp, ob_in, ob_out, ob_pg, ob_pp,
             in_ref, tail_ref, a_ref, b_ref, rg_ref, g_in, g_out, g_pg, g_pp,
             lsem, ssem_l, osem, bsem_s, bsem_r, ssem, rsem):
        x, y, c = lax.axis_index("x"), lax.axis_index("y"), lax.axis_index("c")
        kme = 2 * x + y
        me = 4 * x + 2 * y + c
        sibling = (x, y, 1 - c)
        small_fetch = [pltpu.make_async_copy(src, dst, ssem_l.at[i]) for i, (src, dst) in enumerate(
            [(in_h, in_ref), (tail_h, tail_ref), (a_h, a_ref), (b_h, b_ref), (rg_h, rg_ref)])]
        for cp in small_fetch:
            cp.start()

        lands_hbm = [l_in, l_out, l_pg, l_pp]
        land_bufs = [lb_in, lb_out, lb_pg, lb_pp]
        own_bufs = [ob_in, ob_out, ob_pg, ob_pp]
        fetches = []
        for p in range(4):
            cp = pltpu.make_async_copy(lands_hbm[p], land_bufs[p], lsem.at[p])
            cp.start()
            fetches.append(cp)

        own_fetches = [pltpu.make_async_copy(pin.at[kme], ob_in, lsem.at[4])]
        own_fetches[0].start()
        for p, part in enumerate([pout, ppg, ppp]):
            hrows = own_bufs[p + 1].shape[0]
            cp = pltpu.make_async_copy(part.at[kme, pl.ds(pl.multiple_of(c * hrows, step), hrows)],
                                       own_bufs[p + 1], lsem.at[5 + p])
            cp.start()
            own_fetches.append(cp)

        def pair(ref, r0, r1):
            return jnp.concatenate([ref[r0:r0 + 1, :], ref[r1:r1 + 1, :]], axis=1)

        for cp in small_fetch:
            cp.wait()
        rows = {
            ROW_NORM_MIX: in_ref[0:1, :], ROW_FINAL_NORM: tail_ref[1:2, :], ROW_B_PG: tail_ref[2:3, :],
            ROW_PLE_NORM: tail_ref[3:4, :], ROW_CB_BA: pair(a_ref, 0, 1), ROW_BX_LAM: pair(a_ref, 2, 3),
            ROW_CW01: pair(a_ref, 4, 5), ROW_CW23: pair(a_ref, 6, 7), ROW_HG_LB: pair(b_ref, 2, 3),
            ROW_HG_NW: jnp.concatenate([b_ref[1:2, :], jnp.zeros((1, DG), F32)], axis=1),
            ROW_LOSS: tail_ref[4:5, :],
        }
        vbuf[...] = jnp.zeros_like(vbuf)
        for r, row in rows.items():
            for j in range(NDEV):
                vbuf[j, r:r + 1, :] = row[:, j * 128:(j + 1) * 128]

        def peer(mask):
            px = x ^ ((mask >> 2) & 1)
            py = y ^ ((mask >> 1) & 1)
            pc = c ^ (mask & 1)
            return (px, py, pc), 4 * px + 2 * py + pc

        def rg_rows(r):
            return pl.ds(pl.multiple_of(r * PR, PR), PR)

        first = []
        for mask in range(1, NDEV):
            dev, r = peer(mask)
            i = mask - 1
            cp = _remote(vbuf.at[r], vrecv.at[i], ssem.at[i], rsem.at[i], dev)
            cp.start()
            first.append(cp)
            cp = _remote(rg_ref.at[rg_rows(r)], rgrecv.at[i], ssem.at[7 + i], rsem.at[7 + i], dev)
            cp.start()
            first.append(cp)

        big = [(lb_in, g_in), (lb_out, g_out), (lb_pg, g_pg), (lb_pp, g_pp)]
        g_hbm = [g_in_h, g_out_h, g_pg_h, g_pp_h]
        swaps, writes = [], []
        for p, (land, gout) in enumerate(big):
            fetches[p].wait()
            own_fetches[p].wait()
            hrows = land.shape[1]
            mine0 = pl.multiple_of(c * hrows, step)
            slot = kme if p == 0 else me
            for r0 in range(0, hrows, step):
                rs = pl.ds(mine0 + r0, step)
                own = own_bufs[p][r0:r0 + step, :]
                s = jnp.zeros((step, land.shape[2]), F32)
                for j in range(land.shape[0]):
                    s = s + jnp.where(slot == j, own, land[j, r0:r0 + step, :]).astype(F32)
                gout[rs, :] = s
            mine = pl.ds(mine0, hrows)
            cp = _remote(gout.at[mine], gout.at[mine], bsem_s.at[p], bsem_r.at[p], sibling)
            cp.start()
            swaps.append(cp)
            cp = pltpu.make_async_copy(gout.at[mine], g_hbm[p].at[mine], osem.at[p])
            cp.start()
            writes.append(cp)

        sv = vbuf[me]
        sr = rg_ref[rg_rows(me), :]
        for i in range(NDEV - 1):
            first[2 * i].wait_recv()
            first[2 * i + 1].wait_recv()
            sv = sv + vrecv[i]
            sr = sr + rgrecv[i]
        v_out[me] = sv
        rg_out[rg_rows(me), :] = sr
        second = []
        for mask in range(1, NDEV):
            dev, r = peer(mask)
            i = mask - 1
            cp = _remote(v_out.at[me], v_out.at[me], ssem.at[14 + i], rsem.at[14 + i], dev)
            cp.start()
            second.append(cp)
            cp = _remote(rg_out.at[rg_rows(me)], rg_out.at[rg_rows(me)], ssem.at[21 + i], rsem.at[21 + i], dev)
            cp.start()
            second.append(cp)
        for p, (land, gout) in enumerate(big):
            hrows = land.shape[1]
            other = pl.ds(pl.multiple_of((1 - c) * hrows, step), hrows)
            _remote(gout.at[other], gout.at[other], bsem_s.at[p], bsem_r.at[p], sibling).wait_recv()
            cp = pltpu.make_async_copy(gout.at[other], g_hbm[p].at[other], osem.at[4 + p])
            cp.start()
            writes.append(cp)
        for mask in range(1, NDEV):
            dev, r = peer(mask)
            i = mask - 1
            _remote(v_out.at[r], v_out.at[r], ssem.at[14 + i], rsem.at[14 + i], dev).wait_recv()
            _remote(rg_out.at[rg_rows(r)], rg_out.at[rg_rows(r)], ssem.at[21 + i], rsem.at[21 + i], dev).wait_recv()
        for cp in writes:
            cp.wait()
        for cp in first + swaps + second:
            cp.wait_send()

    out_shape = [pltpu.HBM(s, F32) for s in shapes]
    out_shape += [jax.ShapeDtypeStruct((NDEV, VROWS, 128), F32), jax.ShapeDtypeStruct((DG, 128), F32)]
    outs = pl.pallas_call(
        body, name="final_reduce", out_shape=out_shape,
        in_specs=[HBM_SPEC] * 13, out_specs=[HBM_SPEC] * 4 + [VMEM_SPEC] * 2,
        scratch_shapes=[pltpu.VMEM((NDEV, VROWS, 128), F32), pltpu.VMEM((NDEV - 1, VROWS, 128), F32),
                        pltpu.VMEM((NDEV - 1, PR, 128), F32)]
        + [pltpu.VMEM(l.shape, BF16) for l in lands]
        + [pltpu.VMEM(l.shape[1:], BF16) for l in lands]
        + [pltpu.VMEM(a.shape, F32) for a in (sm_in, sm_tail, sm_a, sm_b, rg_c)]
        + [pltpu.VMEM(s, F32) for s in shapes]
        + [pltpu.SemaphoreType.DMA((8,)), pltpu.SemaphoreType.DMA((5,)), pltpu.SemaphoreType.DMA((8,)),
                        pltpu.SemaphoreType.DMA((4,)), pltpu.SemaphoreType.DMA((4,)),
                        pltpu.SemaphoreType.DMA((28,)), pltpu.SemaphoreType.DMA((28,))],
        compiler_params=pltpu.CompilerParams(vmem_limit_bytes=48 * MIB),
    )(*[pltpu.with_memory_space_constraint(a, pltpu.HBM)
        for a in (*parts, *lands, sm_in, sm_tail, sm_a, sm_b, rg_c)])
    return list(outs[0:4]), outs[4], outs[5]


def _adam_rows(w, g, m, v):
    m2 = ADAM_B1 * m + (1.0 - ADAM_B1) * g
    v2 = ADAM_B2 * v + (1.0 - ADAM_B2) * (g * g)
    m_hat = m2 / (1.0 - ADAM_B1 ** ADAM_STEP)
    v_hat = v2 / (1.0 - ADAM_B2 ** ADAM_STEP)
    delta = -ADAM_LR * (m_hat / (jnp.sqrt(v_hat) + ADAM_EPS) + ADAM_WD * w)
    return delta, m2, v2


def _adam_big(gs, ws, ms, vs):
    n = len(gs)
    steps = 4

    def body(*refs):
        ins, outs = refs[:4 * n], refs[4 * n:]
        for i in range(n):
            g, w, m, v = (r[...] for r in ins[4 * i:4 * i + 4])
            d, m2, v2 = _adam_rows(w, g, m, v)
            outs[3 * i][...] = d
            outs[3 * i + 1][...] = m2
            outs[3 * i + 2][...] = v2

    in_specs, out_specs, out_shape, args = [], [], [], []
    for g, w, m, v in zip(gs, ws, ms, vs):
        r, c = w.shape
        spec = lambda: pl.BlockSpec((r // steps, c), lambda i: (i, 0))
        in_specs += [spec() for _ in range(4)]
        out_specs += [spec() for _ in range(3)]
        out_shape += [jax.ShapeDtypeStruct((r, c), F32)] * 3
        args += [pltpu.with_memory_space_constraint(a, pltpu.HBM) for a in (g, w, m, v)]
    outs = pl.pallas_call(
        body, name="adam_big", grid=(steps,), in_specs=in_specs, out_specs=out_specs, out_shape=out_shape,
        compiler_params=_cparams(("parallel",), 32),
    )(*args)
    return [tuple(outs[3 * i:3 * i + 3]) for i in range(n)]


_VEC_PARAMS = [
    ("norm_mix_w", ROW_NORM_MIX, 0, D), ("final_norm_w", ROW_FINAL_NORM, 0, D),
    ("b_ple_gate", ROW_B_PG, 0, D), ("ple_norm_w", ROW_PLE_NORM, 0, D),
    ("conv_b", ROW_CB_BA, 0, DG), ("rg_ba", ROW_CB_BA, DG, DG),
    ("rg_bx", ROW_BX_LAM, 0, DG), ("rg_lambda", ROW_BX_LAM, DG, DG),
    ("hg_norm_w", ROW_HG_NW, 0, HD),
]
_SMALL_ORDER = [n for n, _, _, _ in _VEC_PARAMS] + ["hg_lb", "conv_w", "rg_wa", "rg_wx"]


def _adam_small(vred, rgred, ws, ms, vs):
    names = _SMALL_ORDER
    n = len(names)

    def body(vred_ref, rg_ref, *refs):
        w_refs = dict(zip(names, refs[0:n]))
        m_refs = dict(zip(names, refs[n:2 * n]))
        v_refs = dict(zip(names, refs[2 * n:3 * n]))
        outs = refs[3 * n:]
        o_refs = {nm: outs[4 * i:4 * i + 4] for i, nm in enumerate(names)}
        kme = 2 * lax.axis_index("x") + lax.axis_index("y")

        def update(nm, g, idx):
            d, m2, v2 = _adam_rows(w_refs[nm][idx], g, m_refs[nm][idx], v_refs[nm][idx])
            og, od, om, ov = o_refs[nm]
            og[idx] = g
            od[idx] = d
            om[idx] = m2
            ov[idx] = v2

        def packed(row, lane0, width):
            return jnp.concatenate([vred_ref[j, row:row + 1, :] for j in range(lane0 // 128, (lane0 + width) // 128)],
                                   axis=1)

        everything = (slice(None), slice(None))
        for nm, row, lane0, width in _VEC_PARAMS:
            update(nm, packed(row, lane0, width), everything)
        for r in range(2):
            update("hg_lb", packed(ROW_HG_LB, r * DG, DG), (slice(r, r + 1), slice(None)))
        for j in range(4):
            g = vred_ref[(j % 2) * 4 + kme, ROW_CW01 + j // 2:ROW_CW01 + j // 2 + 1, :]
            update("conv_w", g, (slice(j, j + 1), slice(None)))
        for r0 in range(0, DG, 128):
            rs = (slice(r0, r0 + 128), slice(None))
            both = rg_ref[r0:r0 + 128, :]
            update("rg_wa", both[:, 0:RGB], rs)
            update("rg_wx", pltpu.roll(both, RGB, 1)[:, 0:RGB], rs)

    args = [vred, rgred] + [d[nm] for d in (ws, ms, vs) for nm in names]
    out_shape = []
    for nm in names:
        out_shape += [jax.ShapeDtypeStruct(ws[nm].shape, F32)] * 4
    whole = lambda s: pl.BlockSpec(s.shape, lambda i, nd=len(s.shape): (0,) * nd)
    outs = pl.pallas_call(
        body, name="adam_small", out_shape=out_shape, grid=(1,),
        in_specs=[whole(a) for a in args], out_specs=[whole(s) for s in out_shape],
    )(*args)
    return {nm: tuple(outs[4 * i:4 * i + 4]) for i, nm in enumerate(names)}


def _conv_rows(cw_ref):
    return [jnp.concatenate([cw_ref[k, j:j + 1, :] for k in range(NSHARD)], axis=1) for j in range(4)]


def _rg_conv(xa, prev8, cw, cb, rows):
    taps = [_shift_down(xa, prev8, 3, rows), _shift_down(xa, prev8, 2, rows),
            _shift_down(xa, prev8, 1, rows), xa]
    xc = cb
    for j in range(4):
        xc = xc + taps[j] * cw[j]
    return xc, taps


def _block_mask():
    r = lax.broadcasted_iota(I32, (DG, DG), 0)
    c = lax.broadcasted_iota(I32, (DG, DG), 1)
    return (r >> 6) == (c >> 6)


def _dense_from_blocks(wc):
    j = lax.broadcasted_iota(I32, (RGB, DG), 0)
    c = lax.broadcasted_iota(I32, (RGB, DG), 1)
    spread = _mm_exact(wc, ((c & (RGB - 1)) == j).astype(F32))
    return jnp.where(_block_mask(), spread, 0.0)


def _blocks_from_dense(da, dx):
    c = lax.broadcasted_iota(I32, (DG, 128), 0)
    j = lax.broadcasted_iota(I32, (DG, 128), 1)
    hit = (c & (RGB - 1)) == (j & (RGB - 1))
    mask = _block_mask()
    return (_mm_exact(jnp.where(mask, da, 0.0), (hit & (j < RGB)).astype(F32))
            + _mm_exact(jnp.where(mask, dx, 0.0), (hit & (j >= RGB)).astype(F32)))


def _rg_gates(xc, wa, ba, wx, bx, sp, first_row):
    r = _sigmoid(_mm(xc, wa) + ba)
    i = _sigmoid(_mm(xc, wx) + bx)
    log_a = (-RG_C) * r * sp
    a = jnp.exp(log_a)
    a2 = a * a
    one_m_a2 = -jnp.tanh(log_a) * (a2 + 1.0)
    mult = jnp.where(first_row, 1.0, jnp.sqrt(one_m_a2))
    return r, i, a, a2, mult


def _softplus(z):
    return jnp.maximum(z, 0.0) + jnp.log1p(jnp.exp(-jnp.abs(z)))


def _fwd_rglru(pf, pb, cw3, conv_b, wa_c, ba, wx_c, bx, lam, dep):
    tm = 512
    ng = tm // 8

    def body(xa_ref, ga_ref, cw_ref, cb_ref, wa_ref, ba_ref, wx_ref, bx_ref, lam_ref, dep_ref,
             h_ref, ya_ref, a_s, u_s, tail_s, hc_s, wa_s, wx_s):
        i = pl.program_id(0)

        @pl.when(i == 0)
        def _():
            tail_s[...] = jnp.zeros_like(tail_s)
            hc_s[...] = jnp.zeros_like(hc_s)
            wa_s[...] = _dense_from_blocks(wa_ref[...]).astype(BF16)
            wx_s[...] = _dense_from_blocks(wx_ref[...]).astype(BF16)

        rows = lax.broadcasted_iota(I32, (tm, DG), 0)
        xa = xa_ref[...]
        xc, _ = _rg_conv(xa, tail_s[...], _conv_rows(cw_ref), cb_ref[...], rows)
        tail_s[...] = xa[tm - 8:tm, :]
        sp = _softplus(-lam_ref[...])
        rp = _mm(xc, wa_s[...]) + ba_ref[...]
        ip = _mm(xc, wx_s[...]) + bx_ref[...]
        rb = 64
        rows_b = lax.broadcasted_iota(I32, (rb, DG), 0)
        rows8 = rows_b & 7
        carry = hc_s[0:1, :]
        for b0 in range(0, tm, rb):
            sl = slice(b0, b0 + rb)
            r = _sigmoid(rp[sl])
            ig = _sigmoid(ip[sl])
            log_a = (-RG_C) * r * sp
            av = jnp.exp(log_a)
            mult = jnp.sqrt(-jnp.tanh(log_a) * (av * av + 1.0))
            if b0 == 0:
                mult = jnp.where((rows_b + i * tm) == 0, 1.0, mult)
            uv = mult * (ig * xc[sl])
            for d in (1, 2, 4):
                keep = rows8 >= d
                uv = uv + av * jnp.where(keep, _roll_in_groups(uv, d), 0.0)
                av = av * jnp.where(keep, _roll_in_groups(av, d), 1.0)
            ga = ga_ref[sl, :].astype(F32)
            gate = ga * _sigmoid(ga)
            hs = []
            for g in range(rb // 8):
                gs = slice(g * 8, (g + 1) * 8)
                hv = uv[gs] + av[gs] * carry
                carry = hv[7:8, :]
                hs.append(hv)
            hb = jnp.concatenate(hs, axis=0)
            h_ref[sl, :] = hb
            ya_ref[sl, :] = (hb * gate).astype(BF16)
        hc_s[0:1, :] = carry

    vec = lambda: pl.BlockSpec((1, DG), lambda i: (0, 0))
    blocks = lambda: pl.BlockSpec((DG, RGB), lambda i: (0, 0))
    return pl.pallas_call(
        body, name="fwd_rglru", grid=(T // tm,),
        in_specs=[pl.BlockSpec((tm, DG), lambda i: (i, 0)),
                  pl.BlockSpec((tm, DG), lambda i: (i, 0)),
                  pl.BlockSpec((NSHARD, 4, 128), lambda i: (0, 0, 0)), vec(),
                  blocks(), vec(), blocks(), vec(), vec(), ANY_SPEC],
        out_specs=[pl.BlockSpec((tm, DG), lambda i: (i, 0)),
                   pl.BlockSpec((tm, DG), lambda i: (i, 0))],
        out_shape=[jax.ShapeDtypeStruct((T, DG), F32), pltpu.HBM((T, DG), BF16)],
        scratch_shapes=[pltpu.VMEM((tm, DG), F32), pltpu.VMEM((tm, DG), F32),
                        pltpu.VMEM((8, DG), F32), pltpu.VMEM((8, DG), F32),
                        pltpu.VMEM((DG, DG), BF16), pltpu.VMEM((DG, DG), BF16)],
        compiler_params=_cparams(("arbitrary",), 48),
    )(pf, pb, cw3, conv_b, wa_c, ba, wx_c, bx, lam, dep)


def _hg_lower_bound(lb_ref):
    return _sig_pair(lb_ref[0:1, :] - lb_ref[1:2, :])


def _hg_gates(fz, lb, one_m_lb):
    sg, sn = _sig_pair(fz)
    f = lb + one_m_lb * sg
    return sg, sn, f, jnp.log(f), one_m_lb * sn


def _tri(lower):
    r = lax.broadcasted_iota(I32, (CH, CH), 0)
    c = lax.broadcasted_iota(I32, (CH, CH), 1)
    return (r >= c) if lower else (r <= c)


def _split3(v):
    hi = v.astype(BF16)
    r1 = v - hi.astype(F32)
    mid = r1.astype(BF16)
    lo = (r1 - mid.astype(F32)).astype(BF16)
    return hi, mid, lo


def _chunk_cumsum(v, rows64, reverse=False):
    del rows64
    tri = _tri(not reverse).astype(BF16)
    out = []
    for c in range(v.shape[0] // CH):
        pieces = _split3(v[c * CH:(c + 1) * CH])
        out.append(sum(jnp.dot(tri, p, preferred_element_type=F32) for p in pieces))
    return jnp.concatenate(out, axis=0)


def _chunk_rev_cumsum(v, rows64):
    return _chunk_cumsum(v, rows64, reverse=True)


def _hg_recompute(q_ref, f_ref, lb, one_m_lb, rows64, eb_s, enb_s, ekd_s, qe_s, ke_s, kd_s, dec_s):
    nc = q_ref.shape[0] // CH
    sg, sn, f, logf, k = _hg_gates(f_ref[...], lb, one_m_lb)
    q = q_ref[...].astype(F32)
    sq = _sigmoid(q)
    qs = q * sq * (HD ** -0.5)
    b = _chunk_cumsum(logf, rows64)
    for c in range(nc):
        rs = slice(c * CH, (c + 1) * CH)
        b_c = b[rs]
        bl = b_c[CH - 1:CH, :]
        eb, enb, ekd = jnp.exp(b_c), jnp.exp(-b_c), jnp.exp(bl - b_c)
        if eb_s is not None:
            eb_s[rs, :] = eb
            enb_s[rs, :] = enb
            ekd_s[rs, :] = ekd
        qe_s[rs, :] = (qs[rs] * eb).astype(BF16)
        ke_s[rs, :] = (k[rs] * enb).astype(BF16)
        kd_s[rs, :] = (k[rs] * ekd).astype(BF16)
        dec_s[c:c + 1, :] = jnp.exp(bl)
    return sg, sn, f, k, q, sq, qs


def _fwd_hgrn2(pf, pb, hg_lb, hg_nw):
    tm = 1024
    nc = tm // CH

    def body(q_ref, f_ref, v_ref, g_ref, lb_ref, nw_ref, yb_ref, o_ref, sp_ref,
             st_s, qe_s, ke_s, kd_s, vb_s, dec_s, p_s, ds_s):
        i = pl.program_id(0)

        @pl.when(i == 0)
        def _():
            st_s[...] = jnp.zeros_like(st_s)

        lb, one_m_lb = _hg_lower_bound(lb_ref)
        rows64 = lax.broadcasted_iota(I32, (tm, DG), 0) & (CH - 1)
        _hg_recompute(q_ref, f_ref, lb, one_m_lb, rows64, None, None, None, qe_s, ke_s, kd_s, dec_s)
        vb_s[...] = v_ref[...]
        mask = _tri(True)
        items = [(c, hd, slice(c * CH, (c + 1) * CH), slice(hd * HD, (hd + 1) * HD))
                 for c in range(nc) for hd in range(NH)]
        for c, hd, rs, cols in items:
            p_s[c * NH + hd] = jnp.where(mask, _mm_nt(qe_s[rs, cols], ke_s[rs, cols]), 0.0).astype(BF16)
            ds_s[c * NH + hd] = _mm_tn(vb_s[rs, cols], kd_s[rs, cols])
        for c, hd, rs, cols in items:
            st = st_s[hd]
            sp_ref[hd, c] = st
            st_s[hd] = st * dec_s[c:c + 1, cols] + ds_s[c * NH + hd]
        for c, hd, rs, cols in items:
            o_ref[rs, cols] = _mm(p_s[c * NH + hd], vb_s[rs, cols]) + _mm_nt(qe_s[rs, cols], sp_ref[hd, c])
        nw = nw_ref[...]
        for hd in range(NH):
            cols = slice(hd * HD, (hd + 1) * HD)
            o = o_ref[:, cols]
            so = lax.rsqrt(jnp.mean(o * o, axis=-1, keepdims=True) + EPS)
            g = g_ref[:, cols].astype(F32)
            sg = _sigmoid(g)
            yb_ref[:, cols] = (o * so * nw * (g * sg)).astype(BF16)

    col = lambda j: pl.BlockSpec((tm, DG), lambda i: (i, j))
    return pl.pallas_call(
        body, name="fwd_hgrn2", grid=(T // tm,),
        in_specs=[col(1), col(1), col(2), col(3),
                  pl.BlockSpec((2, DG), lambda i: (0, 0)),
                  pl.BlockSpec((1, HD), lambda i: (0, 0))],
        out_specs=[pl.BlockSpec((tm, DG), lambda i: (i, 0)),
                   pl.BlockSpec((tm, DG), lambda i: (i, 0)),
                   pl.BlockSpec((NH, nc, HD, HD), lambda i: (0, i, 0, 0))],
        out_shape=[pltpu.HBM((T, DG), BF16), jax.ShapeDtypeStruct((T, DG), F32),
                   jax.ShapeDtypeStruct((NH, NCHUNK, HD, HD), F32)],
        scratch_shapes=[pltpu.VMEM((NH, HD, HD), F32),
                        pltpu.VMEM((tm, DG), BF16), pltpu.VMEM((tm, DG), BF16), pltpu.VMEM((tm, DG), BF16),
                        pltpu.VMEM((tm, DG), BF16), pltpu.VMEM((nc, DG), F32),
                        pltpu.VMEM((nc * NH, CH, CH), BF16), pltpu.VMEM((nc * NH, HD, HD), F32)],
        compiler_params=_cparams(("arbitrary",), 48),
    )(pb, pf, pb, pb, hg_lb, hg_nw)


def _tail_fwd_bwd(x, p, tgt, ya, yb, w_out_b, w_pg_b, w_pp_b, ple_nw, b_pg, fnw):
    tm = 512
    nt = T // tm
    QR = D // NSHARD

    def body(x_ref, p_ref, t_ref, ya_ref, yb_ref, wo_ref, wg_ref, wp_ref, pw_ref, b_ref, fw_ref,
             dh1_ref, dyab_ref, dwo_ref, dwg_ref, dwp_ref, sm_ref, dwo_s, dwg_s, dwp_s):
        i = pl.program_id(0)

        @pl.when(i == 0)
        def _():
            dwo_s[...] = jnp.zeros_like(dwo_s)
            dwg_s[...] = jnp.zeros_like(dwg_s)
            dwp_s[...] = jnp.zeros_like(dwp_s)
            sm_ref[...] = jnp.zeros_like(sm_ref)

        ya = ya_ref[...]
        yb = yb_ref[...]
        pv = p_ref[...].astype(BF16)
        pw = pw_ref[...]
        fw = fw_ref[...]
        h1 = x_ref[...] + _mm(ya, wo_ref[0:DG, :]) + _mm(yb, wo_ref[DG:D, :])
        s2 = lax.rsqrt(jnp.mean(h1 * h1, axis=-1, keepdims=True) + EPS)
        n2h = h1 * s2
        n2 = (n2h * pw).astype(BF16)
        z = _mm(n2, wg_ref[...]) + b_ref[...]
        gate = _sigmoid(z)
        pp = jnp.concatenate([_mm(pv, wp_ref[k]) for k in range(NSHARD)], axis=1)
        h2 = h1 + gate * pp
        s3 = lax.rsqrt(jnp.mean(h2 * h2, axis=-1, keepdims=True) + EPS)
        hn = h2 * s3
        err = hn * fw - t_ref[...]
        sm_ref[0:1, :] += _rowsum(err * err)
        dy = err * (1.0 / D)
        sm_ref[1:2, :] += _rowsum(dy * hn)
        g3 = dy * fw
        dh2 = s3 * (g3 - hn * jnp.mean(g3 * hn, axis=-1, keepdims=True))
        dpp = (dh2 * gate).astype(BF16)
        dz = dh2 * pp * gate * (1.0 - gate)
        sm_ref[2:3, :] += _rowsum(dz)
        dzb = dz.astype(BF16)
        dwg_s[...] += _mm_tn(n2, dzb)
        dn2 = _mm_nt(dzb, wg_ref[...])
        for k in range(NSHARD):
            dwp_s[k] += _mm_tn(pv, dpp[:, k * PLE:(k + 1) * PLE])
        sm_ref[3:4, :] += _rowsum(dn2 * n2h)
        g2 = dn2 * pw
        dh1 = dh2 + s2 * (g2 - n2h * jnp.mean(g2 * n2h, axis=-1, keepdims=True))
        dh1_ref[...] = dh1
        dh1b = dh1.astype(BF16)
        dyab_ref[...] = _mm_nt(dh1b, wo_ref[...])
        dwo_s[0:DG, :] += _mm_tn(ya, dh1b)
        dwo_s[DG:D, :] += _mm_tn(yb, dh1b)

        @pl.when(i == nt - 1)
        def _():
            total = jnp.sum(sm_ref[0:1, :], axis=-1, keepdims=True) * (0.5 / D)
            sm_ref[4:5, :] = jnp.broadcast_to(total, (1, D))
            for k in range(NSHARD):
                dwo_ref[k] = dwo_s[k * QR:(k + 1) * QR, :].astype(BF16)
                dwg_ref[k] = dwg_s[k * QR:(k + 1) * QR, :].astype(BF16)
                dwp_ref[k] = dwp_s[k].astype(BF16)

    row = lambda w: pl.BlockSpec((tm, w), lambda i: (i, 0))
    const2 = lambda s: pl.BlockSpec(s, lambda i: (0, 0), pipeline_mode=pl.Buffered(1))
    const3 = lambda s: pl.BlockSpec(s, lambda i: (0, 0, 0), pipeline_mode=pl.Buffered(1))
    return pl.pallas_call(
        body, name="tail_fwd_bwd", grid=(nt,),
        in_specs=[row(D), row(PLE), row(D), row(DG), row(DG),
                  const2((D, D)), const2((D, D)), const3((NSHARD, PLE, PLE)),
                  const2((1, D)), const2((1, D)), const2((1, D))],
        out_specs=[row(D), row(D), const3((NSHARD, QR, D)), const3((NSHARD, QR, D)),
                   const3((NSHARD, PLE, PLE)), const2((8, D))],
        out_shape=[jax.ShapeDtypeStruct((T, D), F32), jax.ShapeDtypeStruct((T, D), F32),
                   jax.ShapeDtypeStruct((NSHARD, QR, D), BF16), jax.ShapeDtypeStruct((NSHARD, QR, D), BF16),
                   jax.ShapeDtypeStruct((NSHARD, PLE, PLE), BF16), jax.ShapeDtypeStruct((8, D), F32)],
        scratch_shapes=[pltpu.VMEM((D, D), F32), pltpu.VMEM((D, D), F32), pltpu.VMEM((NSHARD, PLE, PLE), F32)],
        compiler_params=_cparams(("arbitrary",), 62),
    )(x, p, tgt, pltpu.with_memory_space_constraint(ya, pltpu.HBM), pltpu.with_memory_space_constraint(yb, pltpu.HBM),
      w_out_b, w_pg_b, w_pp_b, ple_nw, b_pg, fnw)


def _bwd_rglru(pf, pb, h, dyab, cw3, conv_b, wa_c, ba, wx_c, bx, lam, dep):
    tm = 512
    nt = T // tm
    ng = tm // 8

    def body(xa_ref, ga_ref, xp_ref, h_ref, hp_ref, dya_ref, cw_ref, cb_ref, wa_ref, ba_ref, wx_ref, bx_ref,
             lam_ref, dep_ref, da_ref, rg_ref, sm_ref, a_s, g_s, cg_s, nxt_s, wa_s, wx_s, dwa_s, dwx_s):
        i = pl.program_id(0)
        tile = nt - 1 - i

        @pl.when(i == 0)
        def _():
            dwa_s[...] = jnp.zeros_like(dwa_s)
            dwx_s[...] = jnp.zeros_like(dwx_s)
            sm_ref[...] = jnp.zeros_like(sm_ref)
            cg_s[...] = jnp.zeros_like(cg_s)
            nxt_s[...] = jnp.zeros_like(nxt_s)
            wa_s[...] = _dense_from_blocks(wa_ref[...]).astype(BF16)
            wx_s[...] = _dense_from_blocks(wx_ref[...]).astype(BF16)

        rows = lax.broadcasted_iota(I32, (tm, DG), 0)
        has_prev = tile > 0
        xa = xa_ref[...]
        xprev = jnp.where(has_prev, xp_ref[...], 0.0)
        cw = _conv_rows(cw_ref)
        xc, taps = _rg_conv(xa, xprev, cw, cb_ref[...], rows)
        lam_v = lam_ref[...]
        sp = _softplus(-lam_v)
        first_row = (rows + tile * tm) == 0
        r, ig, a, a2, mult = _rg_gates(xc, wa_s[...], ba_ref[...], wx_s[...], bx_ref[...], sp, first_row)
        hv = h_ref[...]
        hprev = jnp.where(has_prev, hp_ref[...], 0.0)
        h_m1 = _shift_down(hv, hprev, 1, rows)
        ga = ga_ref[...].astype(F32)
        sg = _sigmoid(ga)
        dya = dya_ref[...]
        dga = dya * hv * (sg * (1.0 + ga * (1.0 - sg)))

        av = jnp.where(rows == tm - 1, 1.0, pltpu.roll(a, tm - 1, 0))
        gv = dya * (ga * sg)
        rows8 = rows & 7
        for d in (1, 2, 4):
            keep = rows8 < 8 - d
            gv = gv + av * jnp.where(keep, _roll_in_groups(gv, 8 - d), 0.0)
            av = av * jnp.where(keep, _roll_in_groups(av, 8 - d), 1.0)
        a_s[...] = av
        g_s[...] = gv
        carry = cg_s[0:1, :]
        for g in range(ng - 1, -1, -1):
            sl = slice(g * 8, (g + 1) * 8)
            ab, gb = a_s[sl, :], g_s[sl, :]
            g_s[sl, :] = gb + ab * carry
            carry = gb[0:1, :] + ab[0:1, :] * carry
        cg_s[0:1, :] = a[0:1, :] * carry

        gt = g_s[...]
        da = gt * h_m1
        ixc = ig * xc
        di = gt * mult * xc
        dxc = gt * mult * ig
        dlog_a = da * a + jnp.where(first_row, 0.0, gt * ixc * (-a2 / mult))
        sm_ref[3:4, :] += _rowsum(dlog_a * ((-RG_C) * r))
        dpr_f = dlog_a * ((-RG_C) * sp) * r * (1.0 - r)
        dpi_f = di * ig * (1.0 - ig)
        sm_ref[1:2, :] += _rowsum(dpr_f)
        sm_ref[2:3, :] += _rowsum(dpi_f)
        dpr = dpr_f.astype(BF16)
        dpi = dpi_f.astype(BF16)
        xcb = xc.astype(BF16)
        dwa_s[...] += _mm_tn(xcb, dpr)
        dwx_s[...] += _mm_tn(xcb, dpi)
        dxc = dxc + _mm_nt(dpr, wa_s[...]) + _mm_nt(dpi, wx_s[...])
        sm_ref[0:1, :] += _rowsum(dxc)
        for j in range(4):
            sm_ref[4 + j:5 + j, :] += _rowsum(dxc * taps[j])
        nxt = nxt_s[...]
        dxa = (dxc * cw[3] + _shift_up(dxc, nxt, 1, rows) * cw[2]
               + _shift_up(dxc, nxt, 2, rows) * cw[1] + _shift_up(dxc, nxt, 3, rows) * cw[0])
        nxt_s[...] = dxc[0:8, :]
        da_ref[:, 0:DG] = dxa.astype(BF16)
        da_ref[:, DG:D] = dga.astype(BF16)

        @pl.when(i == nt - 1)
        def _():
            _, s_neg = _sig_pair(lam_v)
            sm_ref[3:4, :] = sm_ref[3:4, :] * (-s_neg)
            rg_ref[...] = _blocks_from_dense(dwa_s[...], dwx_s[...])

    vec = lambda: pl.BlockSpec((1, DG), lambda i: (0, 0))
    blocks = lambda: pl.BlockSpec((DG, RGB), lambda i: (0, 0))
    prev8 = lambda: pl.BlockSpec((8, DG), lambda i: (jnp.maximum((nt - 1 - i) * (tm // 8) - 1, 0), 0))
    return pl.pallas_call(
        body, name="bwd_rglru", grid=(nt,),
        in_specs=[pl.BlockSpec((tm, DG), lambda i: (nt - 1 - i, 0)),
                  pl.BlockSpec((tm, DG), lambda i: (nt - 1 - i, 0)),
                  prev8(),
                  pl.BlockSpec((tm, DG), lambda i: (nt - 1 - i, 0)),
                  prev8(),
                  pl.BlockSpec((tm, DG), lambda i: (nt - 1 - i, 0)),
                  pl.BlockSpec((NSHARD, 4, 128), lambda i: (0, 0, 0)), vec(),
                  blocks(), vec(), blocks(), vec(), vec(), ANY_SPEC],
        out_specs=[pl.BlockSpec((tm, D), lambda i: (nt - 1 - i, 0)),
                   pl.BlockSpec((DG, 128), lambda i: (0, 0)),
                   pl.BlockSpec((8, DG), lambda i: (0, 0))],
        out_shape=[jax.ShapeDtypeStruct((T, D), BF16), jax.ShapeDtypeStruct((DG, 128), F32),
                   jax.ShapeDtypeStruct((8, DG), F32)],
        scratch_shapes=[pltpu.VMEM((tm, DG), F32), pltpu.VMEM((tm, DG), F32),
                        pltpu.VMEM((8, DG), F32), pltpu.VMEM((8, DG), F32),
                        pltpu.VMEM((DG, DG), BF16), pltpu.VMEM((DG, DG), BF16),
                        pltpu.VMEM((DG, DG), F32), pltpu.VMEM((DG, DG), F32)],
        compiler_params=_cparams(("arbitrary",), 56),
    )(pf, pb, pf, h, h, dyab, cw3, conv_b, wa_c, ba, wx_c, bx, lam, dep)


def _bwd_hgrn2(pf, pb, o, s_prev, dyab, hg_lb, hg_nw, dep):
    tm = 512
    nt = T // tm
    nc = tm // CH

    def body(q_ref, f_ref, v_ref, g_ref, o_ref, sp_ref, dy_ref, lb_ref, nw_ref, dep_ref, db_ref, sm_ref,
             dst_s, eb_s, enb_s, ekd_s, qe_s, ke_s, kd_s, vb_s, do_s, dec_s, ddec_s, p_s, dp_s,
             g_s, dsta_s, dva_s, dqe_s, dke_s, dkd_s, dlf_s):
        i = pl.program_id(0)

        @pl.when(i == 0)
        def _():
            sm_ref[...] = jnp.zeros_like(sm_ref)
            dst_s[...] = jnp.zeros_like(dst_s)

        lb, one_m_lb = _hg_lower_bound(lb_ref)
        rows64 = lax.broadcasted_iota(I32, (tm, DG), 0) & (CH - 1)
        sg, sn, f, k, q, sq, qs = _hg_recompute(
            q_ref, f_ref, lb, one_m_lb, rows64, eb_s, enb_s, ekd_s, qe_s, ke_s, kd_s, dec_s)
        vb_s[...] = v_ref[...]

        nw = nw_ref[...]
        for hd in range(NH):
            cols = slice(hd * HD, (hd + 1) * HD)
            g = g_ref[:, cols].astype(F32)
            sgg = _sigmoid(g)
            o = o_ref[:, cols]
            so = lax.rsqrt(jnp.mean(o * o, axis=-1, keepdims=True) + EPS)
            oh = o * so
            dyb = dy_ref[:, cols]
            db_ref[:, 3 * DG + hd * HD:3 * DG + (hd + 1) * HD] = (
                dyb * (oh * nw) * (sgg * (1.0 + g * (1.0 - sgg)))).astype(BF16)
            don = dyb * (g * sgg)
            sm_ref[1:2, 0:HD] += _rowsum(don * oh)
            gw = don * nw
            do_s[:, cols] = (so * (gw - oh * jnp.mean(gw * oh, axis=-1, keepdims=True))).astype(BF16)

        mask = _tri(True)
        items = [(c, hd, slice(c * CH, (c + 1) * CH), slice(hd * HD, (hd + 1) * HD))
                 for c in range(nc) for hd in range(NH)]
        for c, hd, rs, cols in items:
            p_s[c * NH + hd] = jnp.where(mask, _mm_nt(qe_s[rs, cols], ke_s[rs, cols]), 0.0).astype(BF16)
            dp_s[c * NH + hd] = jnp.where(mask, _mm_nt(do_s[rs, cols], vb_s[rs, cols]), 0.0).astype(BF16)
        for c, hd, rs, cols in items:
            n = c * NH + hd
            dva_s[rs, cols] = _mm_tn(p_s[n], do_s[rs, cols])
            dqe_s[rs, cols] = _mm(dp_s[n], ke_s[rs, cols])
            dke_s[rs, cols] = _mm_tn(dp_s[n], qe_s[rs, cols])
            g_s[n] = _mm_tn(do_s[rs, cols], qe_s[rs, cols])
        for c, hd, rs, cols in reversed(items):
            n = c * NH + hd
            dst = dst_s[hd]
            dsta_s[n] = dst
            dst_s[hd] = dst * dec_s[c:c + 1, cols] + g_s[n]
        for c, hd, rs, cols in items:
            n = c * NH + hd
            dst = dsta_s[n]
            st_prev = sp_ref[hd, c]
            dv = dva_s[rs, cols] + _mm_nt(kd_s[rs, cols], dst)
            db_ref[rs, 2 * DG + hd * HD:2 * DG + (hd + 1) * HD] = dv.astype(BF16)
            dqe_s[rs, cols] += _mm(do_s[rs, cols], st_prev)
            dkd_s[rs, cols] = _mm(vb_s[rs, cols], dst)
            ddec_s[c:c + 1, cols] = _rowsum(dst * st_prev)

        eb, enb, ekd = eb_s[...], enb_s[...], ekd_s[...]
        dqe, dke, dkd = dqe_s[...], dke_s[...], dkd_s[...]
        t_kd = dkd * (k * ekd)
        rc = _chunk_rev_cumsum(dqe * (qs * eb) - dke * (k * enb) - t_kd, rows64)
        for c in range(nc):
            rs = slice(c * CH, (c + 1) * CH)
            dbl = _rowsum(t_kd[rs]) + ddec_s[c:c + 1, :] * dec_s[c:c + 1, :]
            dlf_s[rs, :] = rc[rs] + dbl
        t = dlf_s[...] / f - (dke * enb + dkd * ekd)
        db_ref[:, DG:2 * DG] = (one_m_lb * sg * sn * t).astype(BF16)
        sm_ref[0:1, :] += _rowsum(sn * t)
        db_ref[:, 0:DG] = (dqe * eb * (sq * (1.0 + q * (1.0 - sq))) * (HD ** -0.5)).astype(BF16)

        @pl.when(i == nt - 1)
        def _():
            dsm = sm_ref[0:1, :] * (lb * one_m_lb)
            sm_ref[2:3, :] = dsm
            sm_ref[3:4, :] = -dsm

    col = lambda j: pl.BlockSpec((tm, DG), lambda i: (nt - 1 - i, j))
    big = lambda dt: pltpu.VMEM((tm, DG), dt)
    return pl.pallas_call(
        body, name="bwd_hgrn2", grid=(nt,),
        in_specs=[col(1), col(1), col(2), col(3),
                  pl.BlockSpec((tm, DG), lambda i: (nt - 1 - i, 0)),
                  pl.BlockSpec((NH, nc, HD, HD), lambda i: (0, nt - 1 - i, 0, 0)),
                  pl.BlockSpec((tm, DG), lambda i: (nt - 1 - i, 1)),
                  pl.BlockSpec((2, DG), lambda i: (0, 0)),
                  pl.BlockSpec((1, HD), lambda i: (0, 0)), ANY_SPEC],
        out_specs=[pl.BlockSpec((tm, 4 * DG), lambda i: (nt - 1 - i, 0)),
                   pl.BlockSpec((8, DG), lambda i: (0, 0))],
        out_shape=[jax.ShapeDtypeStruct((T, 4 * DG), BF16), jax.ShapeDtypeStruct((8, DG), F32)],
        scratch_shapes=[pltpu.VMEM((NH, HD, HD), F32),
                        big(F32), big(F32), big(F32),
                        big(BF16), big(BF16), big(BF16), big(BF16), big(BF16),
                        pltpu.VMEM((nc, DG), F32), pltpu.VMEM((nc, DG), F32),
                        pltpu.VMEM((nc * NH, CH, CH), BF16), pltpu.VMEM((nc * NH, CH, CH), BF16),
                        pltpu.VMEM((nc * NH, HD, HD), F32), pltpu.VMEM((nc * NH, HD, HD), F32),
                        big(F32), big(F32), big(F32), big(F32), big(F32)],
        compiler_params=_cparams(("arbitrary",), 56),
    )(pb, pf, pb, pb, o, s_prev, dyab, hg_lb, hg_nw, dep)


def _dproj_pieces(k, da_ref, db_ref):
    if k == 0:
        return [(da_ref[:, 0:SHW], 0)]
    if k == 1:
        return [(da_ref[:, SHW:D], 0), (db_ref[:, 0:DG], D - SHW)]
    if k == 2:
        return [(db_ref[:, DG:DG + SHW], 0)]
    return [(db_ref[:, DG + SHW:4 * DG], 0)]


def _bwd_inproj_dx(x, dh1, d_a, d_b, w_in_b, nw, dep):
    tm = 512
    nt = T // tm

    def body(x_ref, dh1_ref, da_ref, db_ref, w_ref, nw_ref, dep_ref, dx_ref, sm_ref):
        i = pl.program_id(0)

        @pl.when(i == 0)
        def _():
            sm_ref[...] = jnp.zeros_like(sm_ref)

        du = None
        for k in range(NSHARD):
            for val, off in _dproj_pieces(k, da_ref, db_ref):
                t = _mm_nt(val, w_ref[k, :, off:off + val.shape[1]])
                du = t if du is None else du + t
        xv = x_ref[...]
        s = lax.rsqrt(jnp.mean(xv * xv, axis=-1, keepdims=True) + EPS)
        xh = xv * s
        sm_ref[0:1, :] += _rowsum(du * xh)
        g = du * nw_ref[...]
        dx_ref[...] = dh1_ref[...] + s * (g - xh * jnp.mean(g * xh, axis=-1, keepdims=True))

    row = lambda w: pl.BlockSpec((tm, w), lambda i: (i, 0))
    return pl.pallas_call(
        body, name="bwd_inproj_dx", grid=(nt,),
        in_specs=[row(D), row(D), row(D), row(4 * DG),
                  pl.BlockSpec((NSHARD, D, SHW), lambda i: (0, 0, 0), pipeline_mode=pl.Buffered(1)),
                  pl.BlockSpec((1, D), lambda i: (0, 0)), ANY_SPEC],
        out_specs=[row(D), pl.BlockSpec((8, D), lambda i: (0, 0))],
        out_shape=[jax.ShapeDtypeStruct((T, D), F32), jax.ShapeDtypeStruct((8, D), F32)],
        compiler_params=_cparams(("arbitrary",), 56),
    )(x, dh1, d_a, d_b, w_in_b, nw, dep)


def _bwd_inproj_dw23(u_b, d_b):
    tm = 1024
    nt = T // tm

    def body(u_ref, db_ref, dw_ref, acc):
        i = pl.program_id(0)

        @pl.when(i == 0)
        def _():
            acc[...] = jnp.zeros_like(acc)

        u = u_ref[...]
        for j, k in enumerate((2, 3)):
            for val, off in _dproj_pieces(k, None, db_ref):
                acc[j, :, off:off + val.shape[1]] += _mm_tn(u, val)

        @pl.when(i == nt - 1)
        def _():
            for j in range(2):
                for r0 in range(0, D, 256):
                    dw_ref[j, r0:r0 + 256, :] = acc[j, r0:r0 + 256, :].astype(BF16)

    row = lambda w: pl.BlockSpec((tm, w), lambda i: (i, 0))
    return pl.pallas_call(
        body, name="bwd_inproj_dw23", grid=(nt,), in_specs=[row(D), row(4 * DG)],
        out_specs=pl.BlockSpec((2, D, SHW), lambda i: (0, 0, 0)),
        out_shape=pltpu.HBM((2, D, SHW), BF16),
        scratch_shapes=[pltpu.VMEM((2, D, SHW), F32)],
        compiler_params=_cparams(("arbitrary",), 48),
    )(u_b, d_b)


def _bwd_inproj_dw01(u_b, d_a, d_b, dw23):
    tm = 1024
    nt = T // tm
    H = D // 2
    step = 128

    def body(u_ref, da_ref, db_ref, p23_ref, dw_ref, acc, send_s, recv_s, ssem, rsem):
        i = pl.program_id(0)
        x, y, c = lax.axis_index("x"), lax.axis_index("y"), lax.axis_index("c")
        sibling = (x, y, 1 - c)
        mine0 = pl.multiple_of(c * H, step)
        other0 = pl.multiple_of((1 - c) * H, step)

        def swap(k):
            return _remote(send_s.at[k], recv_s.at[k], ssem.at[k], rsem.at[k], sibling)

        @pl.when(i == 0)
        def _():
            acc[...] = jnp.zeros_like(acc)
            for k in (2, 3):
                for r0 in range(0, H, step):
                    send_s[k, r0:r0 + step, :] = p23_ref[k - 2, pl.ds(other0 + r0, step), :]
                swap(k).start()

        u = u_ref[...]
        for j, k in enumerate((0, 1)):
            for val, off in _dproj_pieces(k, da_ref, db_ref):
                acc[j, :, off:off + val.shape[1]] += _mm_tn(u, val)

        @pl.when(i == nt - 1)
        def _():
            copies = [swap(k) for k in range(NSHARD)]
            for k in (0, 1):
                for r0 in range(0, H, step):
                    send_s[k, r0:r0 + step, :] = acc[k, pl.ds(other0 + r0, step), :].astype(BF16)
                copies[k].start()
            for k in (2, 3, 0, 1):
                copies[k].wait_recv()
                for r0 in range(0, H, step):
                    rs = pl.ds(mine0 + r0, step)
                    own = acc[k, rs, :] if k < 2 else p23_ref[k - 2, rs, :].astype(F32)
                    dw_ref[k, r0:r0 + step, :] = (own + recv_s[k, r0:r0 + step, :].astype(F32)).astype(BF16)
            for cp in copies:
                cp.wait_send()

    row = lambda w: pl.BlockSpec((tm, w), lambda i: (i, 0))
    return pl.pallas_call(
        body, name="bwd_inproj_dw01", grid=(nt,),
        in_specs=[row(D), row(D), row(4 * DG),
                  pl.BlockSpec((2, D, SHW), lambda i: (0, 0, 0), pipeline_mode=pl.Buffered(1))],
        out_specs=pl.BlockSpec((NSHARD, H, SHW), lambda i: (0, 0, 0)),
        out_shape=jax.ShapeDtypeStruct((NSHARD, H, SHW), BF16),
        scratch_shapes=[pltpu.VMEM((2, D, SHW), F32), pltpu.VMEM((NSHARD, H, SHW), BF16),
                        pltpu.VMEM((NSHARD, H, SHW), BF16),
                        pltpu.SemaphoreType.DMA((NSHARD,)), pltpu.SemaphoreType.DMA((NSHARD,))],
        compiler_params=_cparams(("arbitrary",), 56),
    )(u_b, d_a, d_b, dw23)


_OUT_ORDER = ["norm_mix_w", "w_in", "conv_w", "conv_b", "rg_wa", "rg_ba", "rg_wx", "rg_bx", "rg_lambda", "hg_lb",
              "hg_norm_w", "w_out", "ple_norm_w", "w_ple_gate", "b_ple_gate", "w_ple_proj", "final_norm_w"]
_BIG = ["w_in", "w_out", "w_ple_gate", "w_ple_proj"]


def _small_view(name, a):
    if name in ("rg_wa", "rg_wx"):
        return a.reshape(DG, RGB)
    if name == "conv_w":
        return a.reshape(4, 128)
    if name == "final_norm_w":
        return a.reshape(1, D)
    return a


def _landing(rows, cols):
    return lax.empty((NDEV, rows, cols), BF16)


def kernel(x, p, norm_mix_w, w_in, conv_w, conv_b, rg_wa, rg_ba, rg_wx, rg_bx, rg_lambda, hg_lb, hg_norm_w, w_out, ple_norm_w, w_ple_gate, b_ple_gate, w_ple_proj, final_norm_w, loss_target, m_norm_mix_w, m_w_in, m_conv_w, m_conv_b, m_rg_wa, m_rg_ba, m_rg_wx, m_rg_bx, m_rg_lambda, m_hg_lb, m_hg_norm_w, m_w_out, m_ple_norm_w, m_w_ple_gate, m_b_ple_gate, m_w_ple_proj, m_final_norm_w, v_norm_mix_w, v_w_in, v_conv_w, v_conv_b, v_rg_wa, v_rg_ba, v_rg_wx, v_rg_bx, v_rg_lambda, v_hg_lb, v_hg_norm_w, v_w_out, v_ple_norm_w, v_w_ple_gate, v_b_ple_gate, v_w_ple_proj, v_final_norm_w):
    given = dict(locals())
    x2, p2, tgt = x[0], p[0, 0], loss_target[0]
    hbm = lambda a: pltpu.with_memory_space_constraint(a, pltpu.HBM)
    norm_mix_w, conv_b, rg_ba, rg_bx, rg_lambda, hg_lb, hg_norm_w, ple_norm_w, b_ple_gate = (
        hbm(a) for a in (norm_mix_w, conv_b, rg_ba, rg_bx, rg_lambda, hg_lb, hg_norm_w, ple_norm_w, b_ple_gate))
    wa_c, wx_c = hbm(_small_view("rg_wa", rg_wa)), hbm(_small_view("rg_wx", rg_wx))

    w_in_b, l_out, l_pg, l_pp, cw3, pf, pb, u_b = _gather_inproj(
        x2, norm_mix_w, w_in[0], w_out[0], w_ple_gate[0], w_ple_proj[0], conv_w[0])
    g_ssem, g_rsem, g_lands, tok = _gather_rest_start([l_out, l_pg, l_pp])

    h, ya = _fwd_rglru(pf, pb, cw3, conv_b, wa_c, rg_ba, wx_c, rg_bx, rg_lambda, tok)
    yb, o, s_prev = _fwd_hgrn2(pf, pb, hg_lb, hg_norm_w)
    w_out_b, w_pg_b, w_pp_b = _gather_rest_wait(g_ssem, g_rsem, g_lands, yb)
    dh1, dyab, dwo_b, dwg_b, dwp_b, sm_tail = _tail_fwd_bwd(
        x2, p2, tgt, ya, yb, w_out_b.reshape(D, D), w_pg_b.reshape(D, D), w_pp_b,
        ple_norm_w, b_ple_gate, final_norm_w.reshape(1, D))

    QH = D // NSHARD // 2
    r1 = _rs_start("rs_start_tail", [dwo_b, dwg_b, dwp_b], (0, 1, 2, 3),
                   [_landing(QH, D), _landing(QH, D), _landing(PLE // 2, PLE)])
    d_b, sm_b = _bwd_hgrn2(pf, pb, o, s_prev, dyab, hg_lb, hg_norm_w, r1[4])
    dw23 = _bwd_inproj_dw23(u_b, d_b)
    d_a, rg_c, sm_a = _bwd_rglru(pf, pb, h, dyab, cw3, conv_b, wa_c, rg_ba, wx_c, rg_bx, rg_lambda, dw23)
    dw_in = _bwd_inproj_dw01(u_b, d_a, d_b, dw23)
    r2 = _rs_start("rs_start_in", [dw_in], (0, 1, 2, 3), [lax.empty((NSHARD, D // 2, SHW), BF16)], chip_sums=True)
    grad_x, sm_in = _bwd_inproj_dx(x2, dh1, d_a, d_b, w_in_b, norm_mix_w, r2[4])

    parts1, lands1 = _rs_wait("rs_wait_tail", r1[0], r1[1], r1[2], (0, 1, 2, 3), r1[3], sm_in)
    parts2, lands2 = _rs_wait("rs_wait_in", r2[0], r2[1], r2[2], (0, 1, 2, 3), r2[3], sm_in, chip_sums=True)
    g_big, vred, rgred = _final_reduce(parts2 + parts1, lands2 + lands1, sm_in, sm_tail, sm_a, sm_b, rg_c)

    upd_big = _adam_big(g_big, [given[n][0] for n in _BIG], [given["m_" + n][0] for n in _BIG],
                        [given["v_" + n][0] for n in _BIG])
    small = _adam_small(vred, rgred,
                        {n: _small_view(n, given[n]) for n in _SMALL_ORDER},
                        {n: _small_view(n, given["m_" + n]) for n in _SMALL_ORDER},
                        {n: _small_view(n, given["v_" + n]) for n in _SMALL_ORDER})

    loss = vred[0, ROW_LOSS, 0]
    outs = [loss, grad_x[None]]
    for ki in range(4):
        for n in _OUT_ORDER:
            if n in _BIG:
                i = _BIG.index(n)
                a = g_big[i] if ki == 0 else upd_big[i][ki - 1]
                outs.append(a[None])
            else:
                outs.append(small[n][ki].reshape(given[n].shape))
    return tuple(outs)
```

```python
import jax
import jax.numpy as jnp
from jax import lax
from jax.experimental import pallas as pl
from jax.experimental.pallas import tpu as pltpu

F32 = jnp.float32
BF16 = jnp.bfloat16
I32 = jnp.int32
MESH = pl.DeviceIdType.MESH

T = 4096
D = 1024
DG = 512
DIN = 3072
PLE = 256
NH = 4
HD = 128
CH = 64
NCHUNK = T // CH
RGB = 64
EPS = 1e-6
RG_C = 8.0
NSHARD = 4
SHW = DIN // NSHARD
NDEV = 8

ADAM_LR = 0.001
ADAM_B1 = 0.9
ADAM_B2 = 0.999
ADAM_EPS = 1e-08
ADAM_WD = 0.01
ADAM_STEP = 10

VMEM_SPEC = pl.BlockSpec(memory_space=pltpu.VMEM)
HBM_SPEC = pl.BlockSpec(memory_space=pltpu.HBM)
SEM_SPEC = pl.BlockSpec(memory_space=pltpu.SEMAPHORE)
ANY_SPEC = pl.BlockSpec(memory_space=pl.ANY)
EFFECT = pltpu.SideEffectType.DATAFLOW_SIDE_EFFECTING
MIB = 1024 * 1024

VROWS = 16
ROW_NORM_MIX, ROW_FINAL_NORM, ROW_B_PG, ROW_PLE_NORM = 0, 1, 2, 3
ROW_CB_BA, ROW_BX_LAM, ROW_CW01, ROW_CW23, ROW_HG_LB, ROW_HG_NW, ROW_LOSS = 4, 5, 6, 7, 8, 9, 10


def _mm(a, b):
    return jnp.dot(a.astype(BF16), b.astype(BF16), preferred_element_type=F32)


def _mm_nt(a, b):
    return lax.dot_general(a.astype(BF16), b.astype(BF16), (((1,), (1,)), ((), ())),
                           preferred_element_type=F32)


def _mm_tn(a, b):
    return lax.dot_general(a.astype(BF16), b.astype(BF16), (((0,), (0,)), ((), ())),
                           preferred_element_type=F32)


def _mm_exact(a, b):
    bb = b.astype(BF16)
    hi = a.astype(BF16)
    r1 = a - hi.astype(F32)
    mid = r1.astype(BF16)
    lo = (r1 - mid.astype(F32)).astype(BF16)
    return sum(jnp.dot(p, bb, preferred_element_type=F32) for p in (hi, mid, lo))


def _sig_pair(x):
    e = jnp.exp(-jnp.abs(x))
    big = 1.0 / (1.0 + e)
    small = e * big
    pos = x >= 0
    return jnp.where(pos, big, small), jnp.where(pos, small, big)


def _sigmoid(x):
    return 1.0 / (1.0 + jnp.exp(-x))


def _rowsum(v):
    return jnp.sum(v, axis=0, keepdims=True)


def _shift_down(cur, prev8, d, rows):
    rolled = pltpu.roll(cur, d, 0)
    head = jnp.where(rows[0:8] < d, pltpu.roll(prev8, d, 0), rolled[0:8])
    return jnp.concatenate([head, rolled[8:]], axis=0)


def _shift_up(cur, next8, d, rows):
    n = cur.shape[0]
    rolled = pltpu.roll(cur, n - d, 0)
    tail = jnp.where(rows[0:8] >= 8 - d, pltpu.roll(next8, 8 - d, 0), rolled[n - 8:n])
    return jnp.concatenate([rolled[0:n - 8], tail], axis=0)


def _roll_in_groups(v, d):
    n, w = v.shape
    return pltpu.roll(v.reshape(n // 8, 8, w), d, 1).reshape(n, w)


def _cparams(sem, vmem_mib):
    return pltpu.CompilerParams(dimension_semantics=sem, vmem_limit_bytes=vmem_mib * MIB)


def _mesh_pos():
    x, y, c = lax.axis_index("x"), lax.axis_index("y"), lax.axis_index("c")
    chips = [(1 - x, y), (x, 1 - y), (1 - x, 1 - y)]
    return x, y, c, chips


def _remote(src, dst, ssem, rsem, dev):
    return pltpu.make_async_remote_copy(src_ref=src, dst_ref=dst, send_sem=ssem, recv_sem=rsem,
                                        device_id=dev, device_id_type=MESH)


def _gather_inproj(x, nw, w_in, w_out, w_pg, w_pp, conv_w):
    shapes = [w_in.shape, w_out.shape, w_pg.shape, w_pp.shape]
    tm = 1024
    nt = T // tm

    def body(x_h, nw_h, win_h, wout_h, wpg_h, wpp_h, cw_h,
             o_in_h, o_out_h, o_pg_h, o_pp_h, o_cw_h, pf_h, pb_h, u_h,
             win, wout, wpg, wpp, cw, nw_s, xbuf, ubuf, rfbuf, rbbuf, o_in, o_out, o_pg, o_pp, o_cw,
             lsem, xsem, usem, fsem, bsem, osem, ssem, rsem):
        x, y, c, chips = _mesh_pos()
        kme = 2 * x + y
        sibling = (x, y, 1 - c)
        fetch = [pltpu.make_async_copy(src, dst, lsem.at[i]) for i, (src, dst) in enumerate(
            [(win_h, win), (cw_h, cw), (nw_h, nw_s), (wout_h, wout), (wpg_h, wpg), (wpp_h, wpp)])]
        for cp in fetch:
            cp.start()

        def cast(src, dst):
            for r0 in range(0, src.shape[0], 128):
                dst[r0:r0 + 128, :] = src[r0:r0 + 128, :].astype(BF16)

        outs = []

        def write_out(src, dst, i):
            cp = pltpu.make_async_copy(src, dst, osem.at[i])
            cp.start()
            outs.append(cp)

        hrows = D // 2
        mine = pl.ds(pl.multiple_of(c * hrows, 128), hrows)
        other = pl.ds(pl.multiple_of((1 - c) * hrows, 128), hrows)

        def cast_half(rows):
            for r0 in range(0, hrows, 128):
                rs = pl.ds(pl.multiple_of(rows.start + r0, 128), 128)
                o_in[kme, rs, :] = win[rs, :].astype(BF16)

        fetch[0].wait()
        cast_half(mine)
        fetch[1].wait()
        o_cw[kme] = cw[...]

        def half(k, rows):
            return o_in.at[k, rows]

        na = (x ^ c, y ^ (1 - c), c)
        nb = (x ^ (1 - c), y ^ c, c)
        ka = 2 * na[0] + na[1]
        kb = 2 * nb[0] + nb[1]
        kd = 2 * (1 - x) + (1 - y)
        sends = [_remote(half(kme, mine), half(kme, mine), ssem.at[0], rsem.at[0], na),
                 _remote(half(kme, mine), half(kme, mine), ssem.at[1], rsem.at[1], nb)]
        for j, (px, py) in enumerate(chips):
            sends.append(_remote(o_cw.at[kme], o_cw.at[kme], ssem.at[3 + j], rsem.at[3 + j], (px, py, c)))
        for cp in sends:
            cp.start()
        cast_half(other)
        write_out(o_in.at[kme], o_in_h.at[kme], 0)
        fetch[2].wait()

        def rows_of(t):
            return pl.ds(pl.multiple_of(t * tm, tm), tm)

        def x_copy(t, slot):
            return pltpu.make_async_copy(x_h.at[rows_of(t)], xbuf.at[slot], xsem.at[slot])

        u_out = pltpu.make_async_copy(ubuf, u_h, usem.at[0])

        def f_store(t, slot, kh):
            return pltpu.make_async_copy(rfbuf.at[slot], pf_h.at[rows_of(t), pl.ds(pl.multiple_of(kh * DG, DG), DG)],
                                         fsem.at[slot])

        def b_store_even(t, slot, kh):
            return pltpu.make_async_copy(rbbuf.at[slot, :, 0:256],
                                         pb_h.at[rows_of(t), pl.ds(pl.multiple_of(kh * 2 * DG, 256), 256)],
                                         bsem.at[slot])

        def b_store_odd(t, slot, kh):
            return pltpu.make_async_copy(rbbuf.at[slot],
                                         pb_h.at[rows_of(t), pl.ds(pl.multiple_of(256 + kh * 2 * DG, 256), SHW)],
                                         bsem.at[slot])

        def run_pass(k, first, even):
            kh = k >> 1
            if first:
                x_copy(0, 0).start()

            def tile(t, carry):
                slot = t & 1
                if first:
                    @pl.when(t + 1 < nt)
                    def _():
                        x_copy(t + 1, 1 - slot).start()

                    x_copy(t, slot).wait()

                @pl.when(t >= 2)
                def _():
                    if even:
                        f_store(t - 2, slot, kh).wait()
                        b_store_even(t - 2, slot, kh).wait()
                    else:
                        b_store_odd(t - 2, slot, kh).wait()

                if first:
                    xv = xbuf[slot]
                    s = lax.rsqrt(jnp.mean(xv * xv, axis=-1, keepdims=True) + EPS)
                    u = (xv * s * nw_s[...]).astype(BF16)
                    ubuf[rows_of(t), :] = u
                else:
                    u = ubuf[rows_of(t), :]
                r = jnp.dot(u, o_in[k], preferred_element_type=F32)
                if even:
                    rfbuf[slot] = r[:, 0:DG]
                    rbbuf[slot, :, 0:256] = r[:, DG:SHW].astype(BF16)
                    f_store(t, slot, kh).start()
                    b_store_even(t, slot, kh).start()
                else:
                    rbbuf[slot] = r.astype(BF16)
                    b_store_odd(t, slot, kh).start()
                return carry

            lax.fori_loop(0, nt, tile, 0)
            for t in (nt - 2, nt - 1):
                slot = t & 1
                if even:
                    f_store(t, slot, kh).wait()
                    b_store_even(t, slot, kh).wait()
                else:
                    b_store_odd(t, slot, kh).wait()

        def project(k, first=False):
            @pl.when((k & 1) == 0)
            def _():
                run_pass(k, first, True)

            @pl.when((k & 1) == 1)
            def _():
                run_pass(k, first, False)

        def passed_on(k, sem, dev):
            cp = _remote(half(k, mine), half(k, mine), ssem.at[sem], rsem.at[sem], dev)
            cp.start()
            sends.append(cp)

        project(kme, first=True)
        u_out.start()
        for i, (src, dst, dst_h) in enumerate([(wout, o_out, o_out_h), (wpg, o_pg, o_pg_h), (wpp, o_pp, o_pp_h)]):
            fetch[3 + i].wait()
            cast(src, dst)
            write_out(dst, dst_h.at[kme], 4 + i)
        _remote(half(ka, mine), half(ka, mine), ssem.at[0], rsem.at[0], na).wait_recv()
        passed_on(ka, 2, nb)
        passed_on(ka, 6, sibling)
        _remote(half(kb, mine), half(kb, mine), ssem.at[1], rsem.at[1], nb).wait_recv()
        passed_on(kb, 7, sibling)
        _remote(half(ka, other), half(ka, other), ssem.at[7], rsem.at[7], sibling).wait_recv()
        write_out(o_in.at[ka], o_in_h.at[ka], 1)
        project(ka)
        _remote(half(kd, mine), half(kd, mine), ssem.at[2], rsem.at[2], nb).wait_recv()
        passed_on(kd, 8, sibling)
        _remote(half(kb, other), half(kb, other), ssem.at[6], rsem.at[6], sibling).wait_recv()
        write_out(o_in.at[kb], o_in_h.at[kb], 2)
        project(kb)
        _remote(half(kd, other), half(kd, other), ssem.at[8], rsem.at[8], sibling).wait_recv()
        write_out(o_in.at[kd], o_in_h.at[kd], 3)
        project(kd)
        u_out.wait()
        for j, (px, py) in enumerate(chips):
            kj = 2 * px + py
            _remote(o_cw.at[kj], o_cw.at[kj], ssem.at[3 + j], rsem.at[3 + j], (px, py, c)).wait_recv()
        write_out(o_cw, o_cw_h, 7)
        for cp in outs:
            cp.wait()
        for cp in sends:
            cp.wait_send()

    out_shape = [pltpu.HBM((NSHARD,) + s, BF16) for s in shapes]
    out_shape.append(pltpu.HBM((NSHARD,) + conv_w.shape, F32))
    out_shape += [pltpu.HBM((T, 2 * DG), F32), pltpu.HBM((T, 4 * DG), BF16), pltpu.HBM((T, D), BF16)]
    hbm_args = [pltpu.with_memory_space_constraint(a, pltpu.HBM) for a in (x, nw, w_in, w_out, w_pg, w_pp, conv_w)]
    return pl.pallas_call(
        body, name="gather_inproj", out_shape=out_shape,
        in_specs=[HBM_SPEC] * 7, out_specs=[HBM_SPEC] * 8,
        scratch_shapes=[pltpu.VMEM(a.shape, F32) for a in (w_in, w_out, w_pg, w_pp, conv_w, nw)]
        + [pltpu.VMEM((2, tm, D), F32), pltpu.VMEM((T, D), BF16),
           pltpu.VMEM((2, tm, DG), F32), pltpu.VMEM((2, tm, SHW), BF16),
           pltpu.VMEM((NSHARD,) + w_in.shape, BF16), pltpu.VMEM(w_out.shape, BF16), pltpu.VMEM(w_pg.shape, BF16),
           pltpu.VMEM(w_pp.shape, BF16), pltpu.VMEM((NSHARD,) + conv_w.shape, F32),
           pltpu.SemaphoreType.DMA((6,)), pltpu.SemaphoreType.DMA((2,)), pltpu.SemaphoreType.DMA((2,)),
           pltpu.SemaphoreType.DMA((2,)), pltpu.SemaphoreType.DMA((2,)), pltpu.SemaphoreType.DMA((8,)),
           pltpu.SemaphoreType.DMA((9,)), pltpu.SemaphoreType.DMA((9,))],
        compiler_params=pltpu.CompilerParams(vmem_limit_bytes=48 * MIB),
    )(*hbm_args)


def _gather_rest_start(lands):
    n = len(lands)

    def body(*refs):
        land_in = refs[0:n]
        ssem, rsem = refs[n], refs[n + 1]
        token = refs[2 * n + 2]
        x, y, c, chips = _mesh_pos()
        kme = 2 * x + y
        for p, land in enumerate(land_in):
            hrows = land.shape[1] // 2
            mine = pl.ds(pl.multiple_of(c * hrows, 128), hrows)
            for px, py in chips:
                for pc in range(2):
                    _remote(land.at[kme, mine], land.at[kme, mine], ssem.at[p], rsem.at[p], (px, py, pc)).start()
        token[...] = jnp.zeros_like(token)

    out_shape = ([pltpu.SemaphoreType.DMA((n,)), pltpu.SemaphoreType.DMA((n,))]
                 + [pltpu.HBM(a.shape, a.dtype) for a in lands] + [jax.ShapeDtypeStruct((8, 128), F32)])
    outs = pl.pallas_call(
        body, name="gather_rest_start", out_shape=out_shape,
        in_specs=[HBM_SPEC] * n, out_specs=[SEM_SPEC, SEM_SPEC] + [HBM_SPEC] * n + [VMEM_SPEC],
        input_output_aliases={i: 2 + i for i in range(n)},
        compiler_params=pltpu.CompilerParams(has_side_effects=EFFECT),
    )(*[pltpu.with_memory_space_constraint(a, pltpu.HBM) for a in lands])
    return outs[0], outs[1], list(outs[2:2 + n]), outs[2 + n]


def _gather_rest_wait(ssem, rsem, lands, after):
    n = len(lands)

    def body(*refs):
        land_in = refs[0:n]
        ssem_ref, rsem_ref = refs[n], refs[n + 1]
        x, y, c = lax.axis_index("x"), lax.axis_index("y"), lax.axis_index("c")
        for p, land in enumerate(land_in):
            three = land.at[pl.ds(0, 3)]
            cp = _remote(three, three, ssem_ref.at[p], rsem_ref.at[p], (x, y, c))
            cp.wait_send()
            cp.wait_recv()

    outs = pl.pallas_call(
        body, name="gather_rest_wait", out_shape=[pltpu.HBM(a.shape, a.dtype) for a in lands],
        in_specs=[HBM_SPEC] * n + [SEM_SPEC, SEM_SPEC, ANY_SPEC], out_specs=[HBM_SPEC] * n,
        input_output_aliases={i: i for i in range(n)},
        compiler_params=pltpu.CompilerParams(has_side_effects=EFFECT),
    )(*lands, ssem, rsem, after)
    return list(outs)


def _rs_start(name, parts, ks, lands, chip_sums=False):
    n = len(parts)

    def body(*refs):
        part_in, land_in = refs[0:n], refs[n:2 * n]
        ssem, rsem = refs[2 * n], refs[2 * n + 1]
        token = refs[4 * n + 2]
        x, y, c = lax.axis_index("x"), lax.axis_index("y"), lax.axis_index("c")
        kme = 2 * x + y
        me = 4 * x + 2 * y + c
        for p in range(n):
            hrows = land_in[p].shape[1]
            for i, k in enumerate(ks):
                if chip_sums:
                    @pl.when(kme != k)
                    def _():
                        _remote(part_in[p].at[i], land_in[p].at[kme], ssem.at[p], rsem.at[p], (k // 2, k % 2, c)).start()
                    continue
                for pc in range(2):
                    @pl.when(jnp.logical_or(kme != k, c != pc))
                    def _():
                        _remote(part_in[p].at[i, pl.ds(pc * hrows, hrows)], land_in[p].at[me],
                                ssem.at[p], rsem.at[p], (k // 2, k % 2, pc)).start()
        token[...] = jnp.zeros_like(token)

    arrays = list(parts) + list(lands)
    out_shape = ([pltpu.SemaphoreType.DMA((n,)), pltpu.SemaphoreType.DMA((n,))]
                 + [pltpu.HBM(a.shape, a.dtype) for a in arrays] + [jax.ShapeDtypeStruct((8, 128), F32)])
    outs = pl.pallas_call(
        body, name=name, out_shape=out_shape,
        in_specs=[HBM_SPEC] * (2 * n), out_specs=[SEM_SPEC, SEM_SPEC] + [HBM_SPEC] * (2 * n) + [VMEM_SPEC],
        input_output_aliases={i: 2 + i for i in range(2 * n)},
        compiler_params=pltpu.CompilerParams(has_side_effects=EFFECT),
    )(*[pltpu.with_memory_space_constraint(a, pltpu.HBM) for a in arrays])
    return outs[0], outs[1], list(outs[2:2 + n]), list(outs[2 + n:2 + 2 * n]), outs[2 + 2 * n]


def _rs_wait(name, ssem, rsem, parts, ks, lands, after, chip_sums=False):
    n = len(parts)

    def body(*refs):
        part_in, land_in = refs[0:n], refs[n:2 * n]
        ssem_ref, rsem_ref = refs[2 * n], refs[2 * n + 1]
        x, y, c = lax.axis_index("x"), lax.axis_index("y"), lax.axis_index("c")
        kme = 2 * x + y
        for p in range(n):
            piece = land_in[p].at[0]
            for k in ks:
                for pc in range(1 if chip_sums else 2):
                    mine = (kme == k) if chip_sums else jnp.logical_and(kme == k, c == pc)

                    @pl.when(jnp.logical_not(mine))
                    def _():
                        _remote(piece, piece, ssem_ref.at[p], rsem_ref.at[p], (x, y, c)).wait_send()
            owner = kme == ks[0]
            for k in ks[1:]:
                owner = jnp.logical_or(owner, kme == k)

            @pl.when(owner)
            def _():
                others = land_in[p].at[pl.ds(0, land_in[p].shape[0] - 1)]
                _remote(others, others, ssem_ref.at[p], rsem_ref.at[p], (x, y, c)).wait_recv()

    arrays = list(parts) + list(lands)
    outs = pl.pallas_call(
        body, name=name, out_shape=[pltpu.HBM(a.shape, a.dtype) for a in arrays],
        in_specs=[HBM_SPEC] * (2 * n) + [SEM_SPEC, SEM_SPEC, ANY_SPEC], out_specs=[HBM_SPEC] * (2 * n),
        input_output_aliases={i: i for i in range(2 * n)},
        compiler_params=pltpu.CompilerParams(has_side_effects=EFFECT),
    )(*arrays, ssem, rsem, after)
    return list(outs[0:n]), list(outs[n:2 * n])


def _final_reduce(parts, lands, sm_in, sm_tail, sm_a, sm_b, rg_c):
    shapes = [(2 * l.shape[1], l.shape[2]) for l in lands]
    step = 128
    PR = DG // NDEV

    def body(pin, pout, ppg, ppp, l_in, l_out, l_pg, l_pp, in_h, tail_h, a_h, b_h, rg_h,
             g_in_h, g_out_h, g_pg_h, g_pp_h, v_out, rg_out, vbuf, vrecv, rgrecv,
             lb_in, lb_out, lb_pg, lb_pp, ob_in, ob_out, ob_pg, ob_pp,
             in_ref, tail_ref, a_ref, b_ref, rg_ref, g_in, g_out, g_pg, g_pp,
             lsem, ssem_l, osem, bsem_s, bsem_r, ssem, rsem):
        x, y, c = lax.axis_index("x"), lax.axis_index("y"), lax.axis_index("c")
        kme = 2 * x + y
        me = 4 * x + 2 * y + c
        sibling = (x, y, 1 - c)
        small_fetch = [pltpu.make_async_copy(src, dst, ssem_l.at[i]) for i, (src, dst) in enumerate(
            [(in_h, in_ref), (tail_h, tail_ref), (a_h, a_ref), (b_h, b_ref), (rg_h, rg_ref)])]
        for cp in small_fetch:
            cp.start()

        lands_hbm = [l_in, l_out, l_pg, l_pp]
        land_bufs = [lb_in, lb_out, lb_pg, lb_pp]
        own_bufs = [ob_in, ob_out, ob_pg, ob_pp]
        fetches = []
        for p in range(4):
            cp = pltpu.make_async_copy(lands_hbm[p], land_bufs[p], lsem.at[p])
            cp.start()
            fetches.append(cp)

        own_fetches = [pltpu.make_async_copy(pin.at[kme], ob_in, lsem.at[4])]
        own_fetches[0].start()
        for p, part in enumerate([pout, ppg, ppp]):
            hrows = own_bufs[p + 1].shape[0]
            cp = pltpu.make_async_copy(part.at[kme, pl.ds(pl.multiple_of(c * hrows, step), hrows)],
                                       own_bufs[p + 1], lsem.at[5 + p])
            cp.start()
            own_fetches.append(cp)

        def pair(ref, r0, r1):
            return jnp.concatenate([ref[r0:r0 + 1, :], ref[r1:r1 + 1, :]], axis=1)

        for cp in small_fetch:
            cp.wait()
        rows = {
            ROW_NORM_MIX: in_ref[0:1, :], ROW_FINAL_NORM: tail_ref[1:2, :], ROW_B_PG: tail_ref[2:3, :],
            ROW_PLE_NORM: tail_ref[3:4, :], ROW_CB_BA: pair(a_ref, 0, 1), ROW_BX_LAM: pair(a_ref, 2, 3),
            ROW_CW01: pair(a_ref, 4, 5), ROW_CW23: pair(a_ref, 6, 7), ROW_HG_LB: pair(b_ref, 2, 3),
            ROW_HG_NW: jnp.concatenate([b_ref[1:2, :], jnp.zeros((1, DG), F32)], axis=1),
            ROW_LOSS: tail_ref[4:5, :],
        }
        vbuf[...] = jnp.zeros_like(vbuf)
        for r, row in rows.items():
            for j in range(NDEV):
                vbuf[j, r:r + 1, :] = row[:, j * 128:(j + 1) * 128]

        def peer(mask):
            px = x ^ ((mask >> 2) & 1)
            py = y ^ ((mask >> 1) & 1)
            pc = c ^ (mask & 1)
            return (px, py, pc), 4 * px + 2 * py + pc

        def rg_rows(r):
            return pl.ds(pl.multiple_of(r * PR, PR), PR)

        first = []
        for mask in range(1, NDEV):
            dev, r = peer(mask)
            i = mask - 1
            cp = _remote(vbuf.at[r], vrecv.at[i], ssem.at[i], rsem.at[i], dev)
            cp.start()
            first.append(cp)
            cp = _remote(rg_ref.at[rg_rows(r)], rgrecv.at[i], ssem.at[7 + i], rsem.at[7 + i], dev)
            cp.start()
            first.append(cp)

        big = [(lb_in, g_in), (lb_out, g_out), (lb_pg, g_pg), (lb_pp, g_pp)]
        g_hbm = [g_in_h, g_out_h, g_pg_h, g_pp_h]
        swaps, writes = [], []
        for p, (land, gout) in enumerate(big):
            fetches[p].wait()
            own_fetches[p].wait()
            hrows = land.shape[1]
            mine0 = pl.multiple_of(c * hrows, step)
            slot = kme if p == 0 else me
            for r0 in range(0, hrows, step):
                rs = pl.ds(mine0 + r0, step)
                own = own_bufs[p][r0:r0 + step, :]
                s = jnp.zeros((step, land.shape[2]), F32)
                for j in range(land.shape[0]):
                    s = s + jnp.where(slot == j, own, land[j, r0:r0 + step, :]).astype(F32)
                gout[rs, :] = s
            mine = pl.ds(mine0, hrows)
            cp = _remote(gout.at[mine], gout.at[mine], bsem_s.at[p], bsem_r.at[p], sibling)
            cp.start()
            swaps.append(cp)
            cp = pltpu.make_async_copy(gout.at[mine], g_hbm[p].at[mine], osem.at[p])
            cp.start()
            writes.append(cp)

        sv = vbuf[me]
        sr = rg_ref[rg_rows(me), :]
        for i in range(NDEV - 1):
            first[2 * i].wait_recv()
            first[2 * i + 1].wait_recv()
            sv = sv + vrecv[i]
            sr = sr + rgrecv[i]
        v_out[me] = sv
        rg_out[rg_rows(me), :] = sr
        second = []
        for mask in range(1, NDEV):
            dev, r = peer(mask)
            i = mask - 1
            cp = _remote(v_out.at[me], v_out.at[me], ssem.at[14 + i], rsem.at[14 + i], dev)
            cp.start()
            second.append(cp)
            cp = _remote(rg_out.at[rg_rows(me)], rg_out.at[rg_rows(me)], ssem.at[21 + i], rsem.at[21 + i], dev)
            cp.start()
            second.append(cp)
        for p, (land, gout) in enumerate(big):
            hrows = land.shape[1]
            other = pl.ds(pl.multiple_of((1 - c) * hrows, step), hrows)
            _remote(gout.at[other], gout.at[other], bsem_s.at[p], bsem_r.at[p], sibling).wait_recv()
            cp = pltpu.make_async_copy(gout.at[other], g_hbm[p].at[other], osem.at[4 + p])
            cp.start()
            writes.append(cp)
        for mask in range(1, NDEV):
            dev, r = peer(mask)
            i = mask - 1
            _remote(v_out.at[r], v_out.at[r], ssem.at[14 + i], rsem.at[14 + i], dev).wait_recv()
            _remote(rg_out.at[rg_rows(r)], rg_out.at[rg_rows(r)], ssem.at[21 + i], rsem.at[21 + i], dev).wait_recv()
        for cp in writes:
            cp.wait()
        for cp in first + swaps + second:
            cp.wait_send()

    out_shape = [pltpu.HBM(s, F32) for s in shapes]
    out_shape += [jax.ShapeDtypeStruct((NDEV, VROWS, 128), F32), jax.ShapeDtypeStruct((DG, 128), F32)]
    outs = pl.pallas_call(
        body, name="final_reduce", out_shape=out_shape,
        in_specs=[HBM_SPEC] * 13, out_specs=[HBM_SPEC] * 4 + [VMEM_SPEC] * 2,
        scratch_shapes=[pltpu.VMEM((NDEV, VROWS, 128), F32), pltpu.VMEM((NDEV - 1, VROWS, 128), F32),
                        pltpu.VMEM((NDEV - 1, PR, 128), F32)]
        + [pltpu.VMEM(l.shape, BF16) for l in lands]
        + [pltpu.VMEM(l.shape[1:], BF16) for l in lands]
        + [pltpu.VMEM(a.shape, F32) for a in (sm_in, sm_tail, sm_a, sm_b, rg_c)]
        + [pltpu.VMEM(s, F32) for s in shapes]
        + [pltpu.SemaphoreType.DMA((8,)), pltpu.SemaphoreType.DMA((5,)), pltpu.SemaphoreType.DMA((8,)),
                        pltpu.SemaphoreType.DMA((4,)), pltpu.SemaphoreType.DMA((4,)),
                        pltpu.SemaphoreType.DMA((28,)), pltpu.SemaphoreType.DMA((28,))],
        compiler_params=pltpu.CompilerParams(vmem_limit_bytes=48 * MIB),
    )(*[pltpu.with_memory_space_constraint(a, pltpu.HBM)
        for a in (*parts, *lands, sm_in, sm_tail, sm_a, sm_b, rg_c)])
    return list(outs[0:4]), outs[4], outs[5]


def _adam_rows(w, g, m, v):
    m2 = ADAM_B1 * m + (1.0 - ADAM_B1) * g
    v2 = ADAM_B2 * v + (1.0 - ADAM_B2) * (g * g)
    m_hat = m2 / (1.0 - ADAM_B1 ** ADAM_STEP)
    v_hat = v2 / (1.0 - ADAM_B2 ** ADAM_STEP)
    delta = -ADAM_LR * (m_hat / (jnp.sqrt(v_hat) + ADAM_EPS) + ADAM_WD * w)
    return delta, m2, v2


def _adam_big(gs, ws, ms, vs):
    n = len(gs)
    steps = 4

    def body(*refs):
        ins, outs = refs[:4 * n], refs[4 * n:]
        for i in range(n):
            g, w, m, v = (r[...] for r in ins[4 * i:4 * i + 4])
            d, m2, v2 = _adam_rows(w, g, m, v)
            outs[3 * i][...] = d
            outs[3 * i + 1][...] = m2
            outs[3 * i + 2][...] = v2

    in_specs, out_specs, out_shape, args = [], [], [], []
    for g, w, m, v in zip(gs, ws, ms, vs):
        r, c = w.shape
        spec = lambda: pl.BlockSpec((r // steps, c), lambda i: (i, 0))
        in_specs += [spec() for _ in range(4)]
        out_specs += [spec() for _ in range(3)]
        out_shape += [jax.ShapeDtypeStruct((r, c), F32)] * 3
        args += [pltpu.with_memory_space_constraint(a, pltpu.HBM) for a in (g, w, m, v)]
    outs = pl.pallas_call(
        body, name="adam_big", grid=(steps,), in_specs=in_specs, out_specs=out_specs, out_shape=out_shape,
        compiler_params=_cparams(("parallel",), 32),
    )(*args)
    return [tuple(outs[3 * i:3 * i + 3]) for i in range(n)]


_VEC_PARAMS = [
    ("norm_mix_w", ROW_NORM_MIX, 0, D), ("final_norm_w", ROW_FINAL_NORM, 0, D),
    ("b_ple_gate", ROW_B_PG, 0, D), ("ple_norm_w", ROW_PLE_NORM, 0, D),
    ("conv_b", ROW_CB_BA, 0, DG), ("rg_ba", ROW_CB_BA, DG, DG),
    ("rg_bx", ROW_BX_LAM, 0, DG), ("rg_lambda", ROW_BX_LAM, DG, DG),
    ("hg_norm_w", ROW_HG_NW, 0, HD),
]
_SMALL_ORDER = [n for n, _, _, _ in _VEC_PARAMS] + ["hg_lb", "conv_w", "rg_wa", "rg_wx"]


def _adam_small(vred, rgred, ws, ms, vs):
    names = _SMALL_ORDER
    n = len(names)

    def body(vred_ref, rg_ref, *refs):
        w_refs = dict(zip(names, refs[0:n]))
        m_refs = dict(zip(names, refs[n:2 * n]))
        v_refs = dict(zip(names, refs[2 * n:3 * n]))
        outs = refs[3 * n:]
        o_refs = {nm: outs[4 * i:4 * i + 4] for i, nm in enumerate(names)}
        kme = 2 * lax.axis_index("x") + lax.axis_index("y")

        def update(nm, g, idx):
            d, m2, v2 = _adam_rows(w_refs[nm][idx], g, m_refs[nm][idx], v_refs[nm][idx])
            og, od, om, ov = o_refs[nm]
            og[idx] = g
            od[idx] = d
            om[idx] = m2
            ov[idx] = v2

        def packed(row, lane0, width):
            return jnp.concatenate([vred_ref[j, row:row + 1, :] for j in range(lane0 // 128, (lane0 + width) // 128)],
                                   axis=1)

        everything = (slice(None), slice(None))
        for nm, row, lane0, width in _VEC_PARAMS:
            update(nm, packed(row, lane0, width), everything)
        for r in range(2):
            update("hg_lb", packed(ROW_HG_LB, r * DG, DG), (slice(r, r + 1), slice(None)))
        for j in range(4):
            g = vred_ref[(j % 2) * 4 + kme, ROW_CW01 + j // 2:ROW_CW01 + j // 2 + 1, :]
            update("conv_w", g, (slice(j, j + 1), slice(None)))
        for r0 in range(0, DG, 128):
            rs = (slice(r0, r0 + 128), slice(None))
            both = rg_ref[r0:r0 + 128, :]
            update("rg_wa", both[:, 0:RGB], rs)
            update("rg_wx", pltpu.roll(both, RGB, 1)[:, 0:RGB], rs)

    args = [vred, rgred] + [d[nm] for d in (ws, ms, vs) for nm in names]
    out_shape = []
    for nm in names:
        out_shape += [jax.ShapeDtypeStruct(ws[nm].shape, F32)] * 4
    whole = lambda s: pl.BlockSpec(s.shape, lambda i, nd=len(s.shape): (0,) * nd)
    outs = pl.pallas_call(
        body, name="adam_small", out_shape=out_shape, grid=(1,),
        in_specs=[whole(a) for a in args], out_specs=[whole(s) for s in out_shape],
    )(*args)
    return {nm: tuple(outs[4 * i:4 * i + 4]) for i, nm in enumerate(names)}


def _conv_rows(cw_ref):
    return [jnp.concatenate([cw_ref[k, j:j + 1, :] for k in range(NSHARD)], axis=1) for j in range(4)]


def _rg_conv(xa, prev8, cw, cb, rows):
    taps = [_shift_down(xa, prev8, 3, rows), _shift_down(xa, prev8, 2, rows),
            _shift_down(xa, prev8, 1, rows), xa]
    xc = cb
    for j in range(4):
        xc = xc + taps[j] * cw[j]
    return xc, taps


def _block_mask():
    r = lax.broadcasted_iota(I32, (DG, DG), 0)
    c = lax.broadcasted_iota(I32, (DG, DG), 1)
    return (r >> 6) == (c >> 6)


def _dense_from_blocks(wc):
    j = lax.broadcasted_iota(I32, (RGB, DG), 0)
    c = lax.broadcasted_iota(I32, (RGB, DG), 1)
    spread = _mm_exact(wc, ((c & (RGB - 1)) == j).astype(F32))
    return jnp.where(_block_mask(), spread, 0.0)


def _blocks_from_dense(da, dx):
    c = lax.broadcasted_iota(I32, (DG, 128), 0)
    j = lax.broadcasted_iota(I32, (DG, 128), 1)
    hit = (c & (RGB - 1)) == (j & (RGB - 1))
    mask = _block_mask()
    return (_mm_exact(jnp.where(mask, da, 0.0), (hit & (j < RGB)).astype(F32))
            + _mm_exact(jnp.where(mask, dx, 0.0), (hit & (j >= RGB)).astype(F32)))


def _rg_gates(xc, wa, ba, wx, bx, sp, first_row):
    r = _sigmoid(_mm(xc, wa) + ba)
    i = _sigmoid(_mm(xc, wx) + bx)
    log_a = (-RG_C) * r * sp
    a = jnp.exp(log_a)
    a2 = a * a
    one_m_a2 = -jnp.tanh(log_a) * (a2 + 1.0)
    mult = jnp.where(first_row, 1.0, jnp.sqrt(one_m_a2))
    return r, i, a, a2, mult


def _softplus(z):
    return jnp.maximum(z, 0.0) + jnp.log1p(jnp.exp(-jnp.abs(z)))


def _fwd_rglru(pf, pb, cw3, conv_b, wa_c, ba, wx_c, bx, lam, dep):
    tm = 512
    ng = tm // 8

    def body(xa_ref, ga_ref, cw_ref, cb_ref, wa_ref, ba_ref, wx_ref, bx_ref, lam_ref, dep_ref,
             h_ref, ya_ref, gates_ref, a_s, u_s, tail_s, hc_s, wa_s, wx_s):
        i = pl.program_id(0)

        @pl.when(i == 0)
        def _():
            tail_s[...] = jnp.zeros_like(tail_s)
            hc_s[...] = jnp.zeros_like(hc_s)
            wa_s[...] = _dense_from_blocks(wa_ref[...]).astype(BF16)
            wx_s[...] = _dense_from_blocks(wx_ref[...]).astype(BF16)

        rows = lax.broadcasted_iota(I32, (tm, DG), 0)
        xa = xa_ref[...]
        xc, _ = _rg_conv(xa, tail_s[...], _conv_rows(cw_ref), cb_ref[...], rows)
        tail_s[...] = xa[tm - 8:tm, :]
        sp = _softplus(-lam_ref[...])
        rp = _mm(xc, wa_s[...]) + ba_ref[...]
        ip = _mm(xc, wx_s[...]) + bx_ref[...]
        rb = 64
        rows_b = lax.broadcasted_iota(I32, (rb, DG), 0)
        rows8 = rows_b & 7
        carry = hc_s[0:1, :]
        for b0 in range(0, tm, rb):
            sl = slice(b0, b0 + rb)
            r = _sigmoid(rp[sl])
            ig = _sigmoid(ip[sl])
            log_a = (-RG_C) * r * sp
            av = jnp.exp(log_a)
            mult = jnp.sqrt(-jnp.tanh(log_a) * (av * av + 1.0))
            if b0 == 0:
                mult = jnp.where((rows_b + i * tm) == 0, 1.0, mult)
            gates_ref[0, sl, :] = r
            gates_ref[1, sl, :] = ig
            gates_ref[2, sl, :] = av
            gates_ref[3, sl, :] = mult
            uv = mult * (ig * xc[sl])
            for d in (1, 2, 4):
                keep = rows8 >= d
                uv = uv + av * jnp.where(keep, _roll_in_groups(uv, d), 0.0)
                av = av * jnp.where(keep, _roll_in_groups(av, d), 1.0)
            ga = ga_ref[sl, :].astype(F32)
            gate = ga * _sigmoid(ga)
            hs = []
            for g in range(rb // 8):
                gs = slice(g * 8, (g + 1) * 8)
                hv = uv[gs] + av[gs] * carry
                carry = hv[7:8, :]
                hs.append(hv)
            hb = jnp.concatenate(hs, axis=0)
            h_ref[sl, :] = hb
            ya_ref[sl, :] = (hb * gate).astype(BF16)
        hc_s[0:1, :] = carry

    vec = lambda: pl.BlockSpec((1, DG), lambda i: (0, 0))
    blocks = lambda: pl.BlockSpec((DG, RGB), lambda i: (0, 0))
    return pl.pallas_call(
        body, name="fwd_rglru", grid=(T // tm,),
        in_specs=[pl.BlockSpec((tm, DG), lambda i: (i, 0)),
                  pl.BlockSpec((tm, DG), lambda i: (i, 0)),
                  pl.BlockSpec((NSHARD, 4, 128), lambda i: (0, 0, 0)), vec(),
                  blocks(), vec(), blocks(), vec(), vec(), ANY_SPEC],
        out_specs=[pl.BlockSpec((tm, DG), lambda i: (i, 0)),
                   pl.BlockSpec((tm, DG), lambda i: (i, 0)),
                   pl.BlockSpec((4, tm, DG), lambda i: (0, i, 0))],
        out_shape=[jax.ShapeDtypeStruct((T, DG), F32), pltpu.HBM((T, DG), BF16),
                   jax.ShapeDtypeStruct((4, T, DG), F32)],
        scratch_shapes=[pltpu.VMEM((tm, DG), F32), pltpu.VMEM((tm, DG), F32),
                        pltpu.VMEM((8, DG), F32), pltpu.VMEM((8, DG), F32),
                        pltpu.VMEM((DG, DG), BF16), pltpu.VMEM((DG, DG), BF16)],
        compiler_params=_cparams(("arbitrary",), 48),
    )(pf, pb, cw3, conv_b, wa_c, ba, wx_c, bx, lam, dep)


def _hg_lower_bound(lb_ref):
    return _sig_pair(lb_ref[0:1, :] - lb_ref[1:2, :])


def _hg_gates(fz, lb, one_m_lb):
    sg, sn = _sig_pair(fz)
    f = lb + one_m_lb * sg
    return sg, sn, f, jnp.log(f), one_m_lb * sn


def _tri(lower):
    r = lax.broadcasted_iota(I32, (CH, CH), 0)
    c = lax.broadcasted_iota(I32, (CH, CH), 1)
    return (r >= c) if lower else (r <= c)


def _split3(v):
    hi = v.astype(BF16)
    r1 = v - hi.astype(F32)
    mid = r1.astype(BF16)
    lo = (r1 - mid.astype(F32)).astype(BF16)
    return hi, mid, lo


def _chunk_cumsum(v, rows64, reverse=False):
    del rows64
    tri = _tri(not reverse).astype(BF16)
    out = []
    for c in range(v.shape[0] // CH):
        pieces = _split3(v[c * CH:(c + 1) * CH])
        out.append(sum(jnp.dot(tri, p, preferred_element_type=F32) for p in pieces))
    return jnp.concatenate(out, axis=0)


def _chunk_rev_cumsum(v, rows64):
    return _chunk_cumsum(v, rows64, reverse=True)


def _hg_recompute(q_ref, f_ref, lb, one_m_lb, rows64, eb_s, enb_s, ekd_s, qe_s, ke_s, kd_s, dec_s):
    nc = q_ref.shape[0] // CH
    sg, sn, f, logf, k = _hg_gates(f_ref[...], lb, one_m_lb)
    q = q_ref[...].astype(F32)
    sq = _sigmoid(q)
    qs = q * sq * (HD ** -0.5)
    b = _chunk_cumsum(logf, rows64)
    for c in range(nc):
        rs = slice(c * CH, (c + 1) * CH)
        b_c = b[rs]
        bl = b_c[CH - 1:CH, :]
        eb, enb, ekd = jnp.exp(b_c), jnp.exp(-b_c), jnp.exp(bl - b_c)
        if eb_s is not None:
            eb_s[rs, :] = eb
            enb_s[rs, :] = enb
            ekd_s[rs, :] = ekd
        qe_s[rs, :] = (qs[rs] * eb).astype(BF16)
        ke_s[rs, :] = (k[rs] * enb).astype(BF16)
        kd_s[rs, :] = (k[rs] * ekd).astype(BF16)
        dec_s[c:c + 1, :] = jnp.exp(bl)
    return sg, sn, f, k, q, sq, qs


def _fwd_hgrn2(pf, pb, hg_lb, hg_nw):
    tm = 512
    nc = tm // CH

    def body(q_ref, f_ref, v_ref, g_ref, lb_ref, nw_ref, yb_ref, o_ref, sp_ref,
             st_s, qe_s, ke_s, kd_s, vb_s, dec_s, p_s, ds_s):
        i = pl.program_id(0)

        @pl.when(i == 0)
        def _():
            st_s[...] = jnp.zeros_like(st_s)

        lb, one_m_lb = _hg_lower_bound(lb_ref)
        rows64 = lax.broadcasted_iota(I32, (tm, DG), 0) & (CH - 1)
        _hg_recompute(q_ref, f_ref, lb, one_m_lb, rows64, None, None, None, qe_s, ke_s, kd_s, dec_s)
        vb_s[...] = v_ref[...]
        mask = _tri(True)
        items = [(c, hd, slice(c * CH, (c + 1) * CH), slice(hd * HD, (hd + 1) * HD))
                 for c in range(nc) for hd in range(NH)]
        for c, hd, rs, cols in items:
            p_s[c * NH + hd] = jnp.where(mask, _mm_nt(qe_s[rs, cols], ke_s[rs, cols]), 0.0).astype(BF16)
            ds_s[c * NH + hd] = _mm_tn(vb_s[rs, cols], kd_s[rs, cols])
        for c, hd, rs, cols in items:
            st = st_s[hd]
            sp_ref[hd, c] = st
            st_s[hd] = st * dec_s[c:c + 1, cols] + ds_s[c * NH + hd]
        for c, hd, rs, cols in items:
            o_ref[rs, cols] = _mm(p_s[c * NH + hd], vb_s[rs, cols]) + _mm_nt(qe_s[rs, cols], sp_ref[hd, c])
        nw = nw_ref[...]
        for hd in range(NH):
            cols = slice(hd * HD, (hd + 1) * HD)
            o = o_ref[:, cols]
            so = lax.rsqrt(jnp.mean(o * o, axis=-1, keepdims=True) + EPS)
            g = g_ref[:, cols].astype(F32)
            sg = _sigmoid(g)
            yb_ref[:, cols] = (o * so * nw * (g * sg)).astype(BF16)

    col = lambda j: pl.BlockSpec((tm, DG), lambda i: (i, j))
    return pl.pallas_call(
        body, name="fwd_hgrn2", grid=(T // tm,),
        in_specs=[col(1), col(1), col(2), col(3),
                  pl.BlockSpec((2, DG), lambda i: (0, 0)),
                  pl.BlockSpec((1, HD), lambda i: (0, 0))],
        out_specs=[pl.BlockSpec((tm, DG), lambda i: (i, 0)),
                   pl.BlockSpec((tm, DG), lambda i: (i, 0)),
                   pl.BlockSpec((NH, nc, HD, HD), lambda i: (0, i, 0, 0))],
        out_shape=[pltpu.HBM((T, DG), BF16), jax.ShapeDtypeStruct((T, DG), F32),
                   jax.ShapeDtypeStruct((NH, NCHUNK, HD, HD), F32)],
        scratch_shapes=[pltpu.VMEM((NH, HD, HD), F32),
                        pltpu.VMEM((tm, DG), BF16), pltpu.VMEM((tm, DG), BF16), pltpu.VMEM((tm, DG), BF16),
                        pltpu.VMEM((tm, DG), BF16), pltpu.VMEM((nc, DG), F32),
                        pltpu.VMEM((nc * NH, CH, CH), BF16), pltpu.VMEM((nc * NH, HD, HD), F32)],
        compiler_params=_cparams(("arbitrary",), 48),
    )(pb, pf, pb, pb, hg_lb, hg_nw)


def _tail_fwd_bwd(x, p, tgt, ya, yb, w_out_b, w_pg_b, w_pp_b, ple_nw, b_pg, fnw):
    tm = 512
    nt = T // tm
    QR = D // NSHARD

    def body(x_ref, p_ref, t_ref, ya_ref, yb_ref, wo_ref, wg_ref, wp_ref, pw_ref, b_ref, fw_ref,
             dh1_ref, dyab_ref, dwo_ref, dwg_ref, dwp_ref, sm_ref, dwo_s, dwg_s, dwp_s):
        i = pl.program_id(0)

        @pl.when(i == 0)
        def _():
            dwo_s[...] = jnp.zeros_like(dwo_s)
            dwg_s[...] = jnp.zeros_like(dwg_s)
            dwp_s[...] = jnp.zeros_like(dwp_s)
            sm_ref[...] = jnp.zeros_like(sm_ref)

        ya = ya_ref[...]
        yb = yb_ref[...]
        pv = p_ref[...].astype(BF16)
        pw = pw_ref[...]
        fw = fw_ref[...]
        h1 = x_ref[...] + _mm(ya, wo_ref[0:DG, :]) + _mm(yb, wo_ref[DG:D, :])
        s2 = lax.rsqrt(jnp.mean(h1 * h1, axis=-1, keepdims=True) + EPS)
        n2h = h1 * s2
        n2 = (n2h * pw).astype(BF16)
        z = _mm(n2, wg_ref[...]) + b_ref[...]
        gate = _sigmoid(z)
        pp = jnp.concatenate([_mm(pv, wp_ref[k]) for k in range(NSHARD)], axis=1)
        h2 = h1 + gate * pp
        s3 = lax.rsqrt(jnp.mean(h2 * h2, axis=-1, keepdims=True) + EPS)
        hn = h2 * s3
        err = hn * fw - t_ref[...]
        sm_ref[0:1, :] += _rowsum(err * err)
        dy = err * (1.0 / D)
        sm_ref[1:2, :] += _rowsum(dy * hn)
        g3 = dy * fw
        dh2 = s3 * (g3 - hn * jnp.mean(g3 * hn, axis=-1, keepdims=True))
        dpp = (dh2 * gate).astype(BF16)
        dz = dh2 * pp * gate * (1.0 - gate)
        sm_ref[2:3, :] += _rowsum(dz)
        dzb = dz.astype(BF16)
        dwg_s[...] += _mm_tn(n2, dzb)
        dn2 = _mm_nt(dzb, wg_ref[...])
        for k in range(NSHARD):
            dwp_s[k] += _mm_tn(pv, dpp[:, k * PLE:(k + 1) * PLE])
        sm_ref[3:4, :] += _rowsum(dn2 * n2h)
        g2 = dn2 * pw
        dh1 = dh2 + s2 * (g2 - n2h * jnp.mean(g2 * n2h, axis=-1, keepdims=True))
        dh1_ref[...] = dh1
        dh1b = dh1.astype(BF16)
        dyab_ref[...] = _mm_nt(dh1b, wo_ref[...])
        dwo_s[0:DG, :] += _mm_tn(ya, dh1b)
        dwo_s[DG:D, :] += _mm_tn(yb, dh1b)

        @pl.when(i == nt - 1)
        def _():
            total = jnp.sum(sm_ref[0:1, :], axis=-1, keepdims=True) * (0.5 / D)
            sm_ref[4:5, :] = jnp.broadcast_to(total, (1, D))
            for k in range(NSHARD):
                dwo_ref[k] = dwo_s[k * QR:(k + 1) * QR, :].astype(BF16)
                dwg_ref[k] = dwg_s[k * QR:(k + 1) * QR, :].astype(BF16)
                dwp_ref[k] = dwp_s[k].astype(BF16)

    row = lambda w: pl.BlockSpec((tm, w), lambda i: (i, 0))
    const2 = lambda s: pl.BlockSpec(s, lambda i: (0, 0), pipeline_mode=pl.Buffered(1))
    const3 = lambda s: pl.BlockSpec(s, lambda i: (0, 0, 0), pipeline_mode=pl.Buffered(1))
    return pl.pallas_call(
        body, name="tail_fwd_bwd", grid=(nt,),
        in_specs=[row(D), row(PLE), row(D), row(DG), row(DG),
                  const2((D, D)), const2((D, D)), const3((NSHARD, PLE, PLE)),
                  const2((1, D)), const2((1, D)), const2((1, D))],
        out_specs=[row(D), row(D), const3((NSHARD, QR, D)), const3((NSHARD, QR, D)),
                   const3((NSHARD, PLE, PLE)), const2((8, D))],
        out_shape=[jax.ShapeDtypeStruct((T, D), F32), jax.ShapeDtypeStruct((T, D), F32),
                   jax.ShapeDtypeStruct((NSHARD, QR, D), BF16), jax.ShapeDtypeStruct((NSHARD, QR, D), BF16),
                   jax.ShapeDtypeStruct((NSHARD, PLE, PLE), BF16), jax.ShapeDtypeStruct((8, D), F32)],
        scratch_shapes=[pltpu.VMEM((D, D), F32), pltpu.VMEM((D, D), F32), pltpu.VMEM((NSHARD, PLE, PLE), F32)],
        compiler_params=_cparams(("arbitrary",), 62),
    )(x, p, tgt, pltpu.with_memory_space_constraint(ya, pltpu.HBM), pltpu.with_memory_space_constraint(yb, pltpu.HBM),
      w_out_b, w_pg_b, w_pp_b, ple_nw, b_pg, fnw)


def _bwd_rglru(pf, pb, h, gates, dyab, cw3, conv_b, wa_c, ba, wx_c, bx, lam, dep):
    tm = 512
    nt = T // tm
    ng = tm // 8

    def body(xa_ref, ga_ref, xp_ref, h_ref, hp_ref, gates_ref, dya_ref, cw_ref, cb_ref, wa_ref, ba_ref, wx_ref, bx_ref,
             lam_ref, dep_ref, da_ref, rg_ref, sm_ref, a_s, g_s, cg_s, nxt_s, wa_s, wx_s, dwa_s, dwx_s):
        i = pl.program_id(0)
        tile = nt - 1 - i

        @pl.when(i == 0)
        def _():
            dwa_s[...] = jnp.zeros_like(dwa_s)
            dwx_s[...] = jnp.zeros_like(dwx_s)
            sm_ref[...] = jnp.zeros_like(sm_ref)
            cg_s[...] = jnp.zeros_like(cg_s)
            nxt_s[...] = jnp.zeros_like(nxt_s)
            wa_s[...] = _dense_from_blocks(wa_ref[...]).astype(BF16)
            wx_s[...] = _dense_from_blocks(wx_ref[...]).astype(BF16)

        rows = lax.broadcasted_iota(I32, (tm, DG), 0)
        has_prev = tile > 0
        xa = xa_ref[...]
        xprev = jnp.where(has_prev, xp_ref[...], 0.0)
        cw = _conv_rows(cw_ref)
        xc, taps = _rg_conv(xa, xprev, cw, cb_ref[...], rows)
        lam_v = lam_ref[...]
        sp = _softplus(-lam_v)
        first_row = (rows + tile * tm) == 0
        r, ig, a, mult = gates_ref[0], gates_ref[1], gates_ref[2], gates_ref[3]
        a2 = a * a
        hv = h_ref[...]
        hprev = jnp.where(has_prev, hp_ref[...], 0.0)
        h_m1 = _shift_down(hv, hprev, 1, rows)
        ga = ga_ref[...].astype(F32)
        sg = _sigmoid(ga)
        dya = dya_ref[...]
        dga = dya * hv * (sg * (1.0 + ga * (1.0 - sg)))

        av = jnp.where(rows == tm - 1, 1.0, pltpu.roll(a, tm - 1, 0))
        gv = dya * (ga * sg)
        rows8 = rows & 7
        for d in (1, 2, 4):
            keep = rows8 < 8 - d
            gv = gv + av * jnp.where(keep, _roll_in_groups(gv, 8 - d), 0.0)
            av = av * jnp.where(keep, _roll_in_groups(av, 8 - d), 1.0)
        a_s[...] = av
        g_s[...] = gv
        carry = cg_s[0:1, :]
        for g in range(ng - 1, -1, -1):
            sl = slice(g * 8, (g + 1) * 8)
            ab, gb = a_s[sl, :], g_s[sl, :]
            g_s[sl, :] = gb + ab * carry
            carry = gb[0:1, :] + ab[0:1, :] * carry
        cg_s[0:1, :] = a[0:1, :] * carry

        gt = g_s[...]
        da = gt * h_m1
        ixc = ig * xc
        di = gt * mult * xc
        dxc = gt * mult * ig
        dlog_a = da * a + jnp.where(first_row, 0.0, gt * ixc * (-a2 / mult))
        sm_ref[3:4, :] += _rowsum(dlog_a * ((-RG_C) * r))
        dpr_f = dlog_a * ((-RG_C) * sp) * r * (1.0 - r)
        dpi_f = di * ig * (1.0 - ig)
        sm_ref[1:2, :] += _rowsum(dpr_f)
        sm_ref[2:3, :] += _rowsum(dpi_f)
        dpr = dpr_f.astype(BF16)
        dpi = dpi_f.astype(BF16)
        xcb = xc.astype(BF16)
        dwa_s[...] += _mm_tn(xcb, dpr)
        dwx_s[...] += _mm_tn(xcb, dpi)
        dxc = dxc + _mm_nt(dpr, wa_s[...]) + _mm_nt(dpi, wx_s[...])
        sm_ref[0:1, :] += _rowsum(dxc)
        for j in range(4):
            sm_ref[4 + j:5 + j, :] += _rowsum(dxc * taps[j])
        nxt = nxt_s[...]
        dxa = (dxc * cw[3] + _shift_up(dxc, nxt, 1, rows) * cw[2]
               + _shift_up(dxc, nxt, 2, rows) * cw[1] + _shift_up(dxc, nxt, 3, rows) * cw[0])
        nxt_s[...] = dxc[0:8, :]
        da_ref[:, 0:DG] = dxa.astype(BF16)
        da_ref[:, DG:D] = dga.astype(BF16)

        @pl.when(i == nt - 1)
        def _():
            _, s_neg = _sig_pair(lam_v)
            sm_ref[3:4, :] = sm_ref[3:4, :] * (-s_neg)
            rg_ref[...] = _blocks_from_dense(dwa_s[...], dwx_s[...])

    vec = lambda: pl.BlockSpec((1, DG), lambda i: (0, 0))
    blocks = lambda: pl.BlockSpec((DG, RGB), lambda i: (0, 0))
    prev8 = lambda: pl.BlockSpec((8, DG), lambda i: (jnp.maximum((nt - 1 - i) * (tm // 8) - 1, 0), 0))
    return pl.pallas_call(
        body, name="bwd_rglru", grid=(nt,),
        in_specs=[pl.BlockSpec((tm, DG), lambda i: (nt - 1 - i, 0)),
                  pl.BlockSpec((tm, DG), lambda i: (nt - 1 - i, 0)),
                  prev8(),
                  pl.BlockSpec((tm, DG), lambda i: (nt - 1 - i, 0)),
                  prev8(),
                  pl.BlockSpec((4, tm, DG), lambda i: (0, nt - 1 - i, 0)),
                  pl.BlockSpec((tm, DG), lambda i: (nt - 1 - i, 0)),
                  pl.BlockSpec((NSHARD, 4, 128), lambda i: (0, 0, 0)), vec(),
                  blocks(), vec(), blocks(), vec(), vec(), ANY_SPEC],
        out_specs=[pl.BlockSpec((tm, D), lambda i: (nt - 1 - i, 0)),
                   pl.BlockSpec((DG, 128), lambda i: (0, 0)),
                   pl.BlockSpec((8, DG), lambda i: (0, 0))],
        out_shape=[jax.ShapeDtypeStruct((T, D), BF16), jax.ShapeDtypeStruct((DG, 128), F32),
                   jax.ShapeDtypeStruct((8, DG), F32)],
        scratch_shapes=[pltpu.VMEM((tm, DG), F32), pltpu.VMEM((tm, DG), F32),
                        pltpu.VMEM((8, DG), F32), pltpu.VMEM((8, DG), F32),
                        pltpu.VMEM((DG, DG), BF16), pltpu.VMEM((DG, DG), BF16),
                        pltpu.VMEM((DG, DG), F32), pltpu.VMEM((DG, DG), F32)],
        compiler_params=_cparams(("arbitrary",), 56),
    )(pf, pb, pf, h, h, gates, dyab, cw3, conv_b, wa_c, ba, wx_c, bx, lam, dep)


def _bwd_hgrn2(pf, pb, o, s_prev, dyab, hg_lb, hg_nw, dep):
    tm = 512
    nt = T // tm
    nc = tm // CH

    def body(q_ref, f_ref, v_ref, g_ref, o_ref, sp_ref, dy_ref, lb_ref, nw_ref, dep_ref, db_ref, sm_ref,
             dst_s, eb_s, enb_s, ekd_s, qe_s, ke_s, kd_s, vb_s, do_s, dec_s, ddec_s, p_s, dp_s,
             g_s, dsta_s, dva_s, dqe_s, dke_s, dkd_s, dlf_s):
        i = pl.program_id(0)

        @pl.when(i == 0)
        def _():
            sm_ref[...] = jnp.zeros_like(sm_ref)
            dst_s[...] = jnp.zeros_like(dst_s)

        lb, one_m_lb = _hg_lower_bound(lb_ref)
        rows64 = lax.broadcasted_iota(I32, (tm, DG), 0) & (CH - 1)
        sg, sn, f, k, q, sq, qs = _hg_recompute(
            q_ref, f_ref, lb, one_m_lb, rows64, eb_s, enb_s, ekd_s, qe_s, ke_s, kd_s, dec_s)
        vb_s[...] = v_ref[...]

        nw = nw_ref[...]
        for hd in range(NH):
            cols = slice(hd * HD, (hd + 1) * HD)
            g = g_ref[:, cols].astype(F32)
            sgg = _sigmoid(g)
            o = o_ref[:, cols]
            so = lax.rsqrt(jnp.mean(o * o, axis=-1, keepdims=True) + EPS)
            oh = o * so
            dyb = dy_ref[:, cols]
            db_ref[:, 3 * DG + hd * HD:3 * DG + (hd + 1) * HD] = (
                dyb * (oh * nw) * (sgg * (1.0 + g * (1.0 - sgg)))).astype(BF16)
            don = dyb * (g * sgg)
            sm_ref[1:2, 0:HD] += _rowsum(don * oh)
            gw = don * nw
            do_s[:, cols] = (so * (gw - oh * jnp.mean(gw * oh, axis=-1, keepdims=True))).astype(BF16)

        mask = _tri(True)
        items = [(c, hd, slice(c * CH, (c + 1) * CH), slice(hd * HD, (hd + 1) * HD))
                 for c in range(nc) for hd in range(NH)]
        for c, hd, rs, cols in items:
            p_s[c * NH + hd] = jnp.where(mask, _mm_nt(qe_s[rs, cols], ke_s[rs, cols]), 0.0).astype(BF16)
            dp_s[c * NH + hd] = jnp.where(mask, _mm_nt(do_s[rs, cols], vb_s[rs, cols]), 0.0).astype(BF16)
        for c, hd, rs, cols in items:
            n = c * NH + hd
            dva_s[rs, cols] = _mm_tn(p_s[n], do_s[rs, cols])
            dqe_s[rs, cols] = _mm(dp_s[n], ke_s[rs, cols])
            dke_s[rs, cols] = _mm_tn(dp_s[n], qe_s[rs, cols])
            g_s[n] = _mm_tn(do_s[rs, cols], qe_s[rs, cols])
        for c, hd, rs, cols in reversed(items):
            n = c * NH + hd
            dst = dst_s[hd]
            dsta_s[n] = dst
            dst_s[hd] = dst * dec_s[c:c + 1, cols] + g_s[n]
        for c, hd, rs, cols in items:
            n = c * NH + hd
            dst = dsta_s[n]
            st_prev = sp_ref[hd, c]
            dv = dva_s[rs, cols] + _mm_nt(kd_s[rs, cols], dst)
            db_ref[rs, 2 * DG + hd * HD:2 * DG + (hd + 1) * HD] = dv.astype(BF16)
            dqe_s[rs, cols] += _mm(do_s[rs, cols], st_prev)
            dkd_s[rs, cols] = _mm(vb_s[rs, cols], dst)
            ddec_s[c:c + 1, cols] = _rowsum(dst * st_prev)

        eb, enb, ekd = eb_s[...], enb_s[...], ekd_s[...]
        dqe, dke, dkd = dqe_s[...], dke_s[...], dkd_s[...]
        t_kd = dkd * (k * ekd)
        rc = _chunk_rev_cumsum(dqe * (qs * eb) - dke * (k * enb) - t_kd, rows64)
        for c in range(nc):
            rs = slice(c * CH, (c + 1) * CH)
            dbl = _rowsum(t_kd[rs]) + ddec_s[c:c + 1, :] * dec_s[c:c + 1, :]
            dlf_s[rs, :] = rc[rs] + dbl
        t = dlf_s[...] / f - (dke * enb + dkd * ekd)
        db_ref[:, DG:2 * DG] = (one_m_lb * sg * sn * t).astype(BF16)
        sm_ref[0:1, :] += _rowsum(sn * t)
        db_ref[:, 0:DG] = (dqe * eb * (sq * (1.0 + q * (1.0 - sq))) * (HD ** -0.5)).astype(BF16)

        @pl.when(i == nt - 1)
        def _():
            dsm = sm_ref[0:1, :] * (lb * one_m_lb)
            sm_ref[2:3, :] = dsm
            sm_ref[3:4, :] = -dsm

    col = lambda j: pl.BlockSpec((tm, DG), lambda i: (nt - 1 - i, j))
    big = lambda dt: pltpu.VMEM((tm, DG), dt)
    return pl.pallas_call(
        body, name="bwd_hgrn2", grid=(nt,),
        in_specs=[col(1), col(1), col(2), col(3),
                  pl.BlockSpec((tm, DG), lambda i: (nt - 1 - i, 0)),
                  pl.BlockSpec((NH, nc, HD, HD), lambda i: (0, nt - 1 - i, 0, 0)),
                  pl.BlockSpec((tm, DG), lambda i: (nt - 1 - i, 1)),
                  pl.BlockSpec((2, DG), lambda i: (0, 0)),
                  pl.BlockSpec((1, HD), lambda i: (0, 0)), ANY_SPEC],
        out_specs=[pl.BlockSpec((tm, 4 * DG), lambda i: (nt - 1 - i, 0)),
                   pl.BlockSpec((8, DG), lambda i: (0, 0))],
        out_shape=[jax.ShapeDtypeStruct((T, 4 * DG), BF16), jax.ShapeDtypeStruct((8, DG), F32)],
        scratch_shapes=[pltpu.VMEM((NH, HD, HD), F32),
                        big(F32), big(F32), big(F32),
                        big(BF16), big(BF16), big(BF16), big(BF16), big(BF16),
                        pltpu.VMEM((nc, DG), F32), pltpu.VMEM((nc, DG), F32),
                        pltpu.VMEM((nc * NH, CH, CH), BF16), pltpu.VMEM((nc * NH, CH, CH), BF16),
                        pltpu.VMEM((nc * NH, HD, HD), F32), pltpu.VMEM((nc * NH, HD, HD), F32),
                        big(F32), big(F32), big(F32), big(F32), big(F32)],
        compiler_params=_cparams(("arbitrary",), 56),
    )(pb, pf, pb, pb, o, s_prev, dyab, hg_lb, hg_nw, dep)


def _dproj_pieces(k, da_ref, db_ref):
    if k == 0:
        return [(da_ref[:, 0:SHW], 0)]
    if k == 1:
        return [(da_ref[:, SHW:D], 0), (db_ref[:, 0:DG], D - SHW)]
    if k == 2:
        return [(db_ref[:, DG:DG + SHW], 0)]
    return [(db_ref[:, DG + SHW:4 * DG], 0)]


def _bwd_inproj_dx(x, dh1, d_a, d_b, w_in_b, nw, dep):
    tm = 512
    nt = T // tm

    def body(x_ref, dh1_ref, da_ref, db_ref, w_ref, nw_ref, dep_ref, dx_ref, sm_ref):
        i = pl.program_id(0)

        @pl.when(i == 0)
        def _():
            sm_ref[...] = jnp.zeros_like(sm_ref)

        du = None
        for k in range(NSHARD):
            for val, off in _dproj_pieces(k, da_ref, db_ref):
                t = _mm_nt(val, w_ref[k, :, off:off + val.shape[1]])
                du = t if du is None else du + t
        xv = x_ref[...]
        s = lax.rsqrt(jnp.mean(xv * xv, axis=-1, keepdims=True) + EPS)
        xh = xv * s
        sm_ref[0:1, :] += _rowsum(du * xh)
        g = du * nw_ref[...]
        dx_ref[...] = dh1_ref[...] + s * (g - xh * jnp.mean(g * xh, axis=-1, keepdims=True))

    row = lambda w: pl.BlockSpec((tm, w), lambda i: (i, 0))
    return pl.pallas_call(
        body, name="bwd_inproj_dx", grid=(nt,),
        in_specs=[row(D), row(D), row(D), row(4 * DG),
                  pl.BlockSpec((NSHARD, D, SHW), lambda i: (0, 0, 0), pipeline_mode=pl.Buffered(1)),
                  pl.BlockSpec((1, D), lambda i: (0, 0)), ANY_SPEC],
        out_specs=[row(D), pl.BlockSpec((8, D), lambda i: (0, 0))],
        out_shape=[jax.ShapeDtypeStruct((T, D), F32), jax.ShapeDtypeStruct((8, D), F32)],
        compiler_params=_cparams(("arbitrary",), 56),
    )(x, dh1, d_a, d_b, w_in_b, nw, dep)


def _bwd_inproj_dw23(u_b, d_b):
    tm = 1024
    nt = T // tm

    def body(u_ref, db_ref, dw_ref, acc):
        i = pl.program_id(0)

        @pl.when(i == 0)
        def _():
            acc[...] = jnp.zeros_like(acc)

        u = u_ref[...]
        for j, k in enumerate((2, 3)):
            for val, off in _dproj_pieces(k, None, db_ref):
                acc[j, :, off:off + val.shape[1]] += _mm_tn(u, val)

        @pl.when(i == nt - 1)
        def _():
            for j in range(2):
                for r0 in range(0, D, 256):
                    dw_ref[j, r0:r0 + 256, :] = acc[j, r0:r0 + 256, :].astype(BF16)

    row = lambda w: pl.BlockSpec((tm, w), lambda i: (i, 0))
    return pl.pallas_call(
        body, name="bwd_inproj_dw23", grid=(nt,), in_specs=[row(D), row(4 * DG)],
        out_specs=pl.BlockSpec((2, D, SHW), lambda i: (0, 0, 0)),
        out_shape=pltpu.HBM((2, D, SHW), BF16),
        scratch_shapes=[pltpu.VMEM((2, D, SHW), F32)],
        compiler_params=_cparams(("arbitrary",), 48),
    )(u_b, d_b)


def _bwd_inproj_dw01(u_b, d_a, d_b, dw23):
    tm = 1024
    nt = T // tm
    H = D // 2
    step = 128

    def body(u_ref, da_ref, db_ref, p23_ref, dw_ref, acc, send_s, recv_s, ssem, rsem):
        i = pl.program_id(0)
        x, y, c = lax.axis_index("x"), lax.axis_index("y"), lax.axis_index("c")
        sibling = (x, y, 1 - c)
        mine0 = pl.multiple_of(c * H, step)
        other0 = pl.multiple_of((1 - c) * H, step)

        def swap(k):
            return _remote(send_s.at[k], recv_s.at[k], ssem.at[k], rsem.at[k], sibling)

        @pl.when(i == 0)
        def _():
            acc[...] = jnp.zeros_like(acc)
            for k in (2, 3):
                for r0 in range(0, H, step):
                    send_s[k, r0:r0 + step, :] = p23_ref[k - 2, pl.ds(other0 + r0, step), :]
                swap(k).start()

        u = u_ref[...]
        for j, k in enumerate((0, 1)):
            for val, off in _dproj_pieces(k, da_ref, db_ref):
                acc[j, :, off:off + val.shape[1]] += _mm_tn(u, val)

        @pl.when(i == nt - 1)
        def _():
            copies = [swap(k) for k in range(NSHARD)]
            for k in (0, 1):
                for r0 in range(0, H, step):
                    send_s[k, r0:r0 + step, :] = acc[k, pl.ds(other0 + r0, step), :].astype(BF16)
                copies[k].start()
            for k in (2, 3, 0, 1):
                copies[k].wait_recv()
                for r0 in range(0, H, step):
                    rs = pl.ds(mine0 + r0, step)
                    own = acc[k, rs, :] if k < 2 else p23_ref[k - 2, rs, :].astype(F32)
                    dw_ref[k, r0:r0 + step, :] = (own + recv_s[k, r0:r0 + step, :].astype(F32)).astype(BF16)
            for cp in copies:
                cp.wait_send()

    row = lambda w: pl.BlockSpec((tm, w), lambda i: (i, 0))
    return pl.pallas_call(
        body, name="bwd_inproj_dw01", grid=(nt,),
        in_specs=[row(D), row(D), row(4 * DG),
                  pl.BlockSpec((2, D, SHW), lambda i: (0, 0, 0), pipeline_mode=pl.Buffered(1))],
        out_specs=pl.BlockSpec((NSHARD, H, SHW), lambda i: (0, 0, 0)),
        out_shape=jax.ShapeDtypeStruct((NSHARD, H, SHW), BF16),
        scratch_shapes=[pltpu.VMEM((2, D, SHW), F32), pltpu.VMEM((NSHARD, H, SHW), BF16),
                        pltpu.VMEM((NSHARD, H, SHW), BF16),
                        pltpu.SemaphoreType.DMA((NSHARD,)), pltpu.SemaphoreType.DMA((NSHARD,))],
        compiler_params=_cparams(("arbitrary",), 56),
    )(u_b, d_a, d_b, dw23)


_OUT_ORDER = ["norm_mix_w", "w_in", "conv_w", "conv_b", "rg_wa", "rg_ba", "rg_wx", "rg_bx", "rg_lambda", "hg_lb",
              "hg_norm_w", "w_out", "ple_norm_w", "w_ple_gate", "b_ple_gate", "w_ple_proj", "final_norm_w"]
_BIG = ["w_in", "w_out", "w_ple_gate", "w_ple_proj"]


def _small_view(name, a):
    if name in ("rg_wa", "rg_wx"):
        return a.reshape(DG, RGB)
    if name == "conv_w":
        return a.reshape(4, 128)
    if name == "final_norm_w":
        return a.reshape(1, D)
    return a


def _landing(rows, cols):
    return lax.empty((NDEV, rows, cols), BF16)


def kernel(x, p, norm_mix_w, w_in, conv_w, conv_b, rg_wa, rg_ba, rg_wx, rg_bx, rg_lambda, hg_lb, hg_norm_w, w_out, ple_norm_w, w_ple_gate, b_ple_gate, w_ple_proj, final_norm_w, loss_target, m_norm_mix_w, m_w_in, m_conv_w, m_conv_b, m_rg_wa, m_rg_ba, m_rg_wx, m_rg_bx, m_rg_lambda, m_hg_lb, m_hg_norm_w, m_w_out, m_ple_norm_w, m_w_ple_gate, m_b_ple_gate, m_w_ple_proj, m_final_norm_w, v_norm_mix_w, v_w_in, v_conv_w, v_conv_b, v_rg_wa, v_rg_ba, v_rg_wx, v_rg_bx, v_rg_lambda, v_hg_lb, v_hg_norm_w, v_w_out, v_ple_norm_w, v_w_ple_gate, v_b_ple_gate, v_w_ple_proj, v_final_norm_w):
    given = dict(locals())
    x2, p2, tgt = x[0], p[0, 0], loss_target[0]
    hbm = lambda a: pltpu.with_memory_space_constraint(a, pltpu.HBM)
    norm_mix_w, conv_b, rg_ba, rg_bx, rg_lambda, hg_lb, hg_norm_w, ple_norm_w, b_ple_gate = (
        hbm(a) for a in (norm_mix_w, conv_b, rg_ba, rg_bx, rg_lambda, hg_lb, hg_norm_w, ple_norm_w, b_ple_gate))
    wa_c, wx_c = hbm(_small_view("rg_wa", rg_wa)), hbm(_small_view("rg_wx", rg_wx))

    w_in_b, l_out, l_pg, l_pp, cw3, pf, pb, u_b = _gather_inproj(
        x2, norm_mix_w, w_in[0], w_out[0], w_ple_gate[0], w_ple_proj[0], conv_w[0])
    g_ssem, g_rsem, g_lands, tok = _gather_rest_start([l_out, l_pg, l_pp])

    h, ya, gates = _fwd_rglru(pf, pb, cw3, conv_b, wa_c, rg_ba, wx_c, rg_bx, rg_lambda, tok)
    yb, o, s_prev = _fwd_hgrn2(pf, pb, hg_lb, hg_norm_w)
    w_out_b, w_pg_b, w_pp_b = _gather_rest_wait(g_ssem, g_rsem, g_lands, yb)
    dh1, dyab, dwo_b, dwg_b, dwp_b, sm_tail = _tail_fwd_bwd(
        x2, p2, tgt, ya, yb, w_out_b.reshape(D, D), w_pg_b.reshape(D, D), w_pp_b,
        ple_norm_w, b_ple_gate, final_norm_w.reshape(1, D))

    QH = D // NSHARD // 2
    r1 = _rs_start("rs_start_tail", [dwo_b, dwg_b, dwp_b], (0, 1, 2, 3),
                   [_landing(QH, D), _landing(QH, D), _landing(PLE // 2, PLE)])
    d_b, sm_b = _bwd_hgrn2(pf, pb, o, s_prev, dyab, hg_lb, hg_norm_w, r1[4])
    dw23 = _bwd_inproj_dw23(u_b, d_b)
    d_a, rg_c, sm_a = _bwd_rglru(pf, pb, h, gates, dyab, cw3, conv_b, wa_c, rg_ba, wx_c, rg_bx, rg_lambda, dw23)
    dw_in = _bwd_inproj_dw01(u_b, d_a, d_b, dw23)
    r2 = _rs_start("rs_start_in", [dw_in], (0, 1, 2, 3), [lax.empty((NSHARD, D // 2, SHW), BF16)], chip_sums=True)
    grad_x, sm_in = _bwd_inproj_dx(x2, dh1, d_a, d_b, w_in_b, norm_mix_w, r2[4])

    parts1, lands1 = _rs_wait("rs_wait_tail", r1[0], r1[1], r1[2], (0, 1, 2, 3), r1[3], sm_in)
    parts2, lands2 = _rs_wait("rs_wait_in", r2[0], r2[1], r2[2], (0, 1, 2, 3), r2[3], sm_in, chip_sums=True)
    g_big, vred, rgred = _final_reduce(parts2 + parts1, lands2 + lands1, sm_in, sm_tail, sm_a, sm_b, rg_c)

    upd_big = _adam_big(g_big, [given[n][0] for n in _BIG], [given["m_" + n][0] for n in _BIG],
                        [given["v_" + n][0] for n in _BIG])
    small = _adam_small(vred, rgred,
                        {n: _small_view(n, given[n]) for n in _SMALL_ORDER},
                        {n: _small_view(n, given["m_" + n]) for n in _SMALL_ORDER},
                        {n: _small_view(n, given["v_" + n]) for n in _SMALL_ORDER})

    loss = vred[0, ROW_LOSS, 0]
    outs = [loss, grad_x[None]]
    for ki in range(4):
        for n in _OUT_ORDER:
            if n in _BIG:
                i = _BIG.index(n)
                a = g_big[i] if ki == 0 else upd_big[i][ki - 1]
                outs.append(a[None])
            else:
                outs.append(small[n][ki].reshape(given[n].shape))
    return tuple(outs)
```
